```python
import math
import jax, jax.numpy as jnp
from jax import lax
import numpy as np

D_MODEL = 1024
BATCH = 16
SEQ = 4096
DEPTH = 1

HEAD_DIM = 64
N_HEADS_FOX = 8
N_HEADS_DIL = 8
WIDTH_FOX = N_HEADS_FOX * HEAD_DIM
WIDTH_DIL = N_HEADS_DIL * HEAD_DIM
D_MIX = WIDTH_FOX + WIDTH_DIL
D_IN = 3 * WIDTH_FOX + N_HEADS_FOX + 3 * WIDTH_DIL
BLOCK = 128
DILATED_PATTERNS = ((128, 1), (512, 4), (2048, 16))
ROPE_THETA = 500000.0
ROPE_DIMS = HEAD_DIM // 4
D_FF = 2816
CONV_WIDTH = 3
DEEPNORM_ALPHA = (2.0 * DEPTH) ** 0.25
DEEPNORM_BETA = (8.0 * DEPTH) ** -0.25
LN_EPS = 1e-5
RMS_EPS = 1e-6

kernel_name = "fox_dilated_hybrid_deepnorm_block"


def _layer_norm(x, g, b):
    xf = x.astype(jnp.float32)
    mu = jnp.mean(xf, axis=-1, keepdims=True)
    var = jnp.mean(jnp.square(xf - mu), axis=-1, keepdims=True)
    y = (xf - mu) * lax.rsqrt(var + LN_EPS)
    return (y * g.astype(jnp.float32) + b.astype(jnp.float32)).astype(x.dtype)


def _head_rms_norm(o, gain):
    of = o.astype(jnp.float32)
    of = of * lax.rsqrt(jnp.mean(jnp.square(of), axis=-1, keepdims=True) + RMS_EPS)
    B, S, H, Dh = o.shape
    return (of.reshape(B, S, H * Dh) * gain.astype(jnp.float32)).astype(o.dtype)


def _partial_rotary(t, positions):
    half = ROPE_DIMS // 2
    freqs = ROPE_THETA ** (-jnp.arange(0, ROPE_DIMS, 2, dtype=jnp.float32) / ROPE_DIMS)
    ang = positions.astype(jnp.float32)[:, :, None] * freqs
    cos = jnp.cos(ang)[:, :, None, :]
    sin = jnp.sin(ang)[:, :, None, :]
    tf = t.astype(jnp.float32)
    t1, t2, rest = tf[..., :half], tf[..., half:ROPE_DIMS], tf[..., ROPE_DIMS:]
    rot = jnp.concatenate([t1 * cos - t2 * sin, t2 * cos + t1 * sin, rest], axis=-1)
    return rot.astype(t.dtype)


def _forgetting_attention(q, k, v, log_f):
    B, S, H, Dh = q.shape
    nb = S // BLOCK
    scale = Dh ** -0.5
    F = jnp.cumsum(log_f, axis=1).transpose(0, 2, 1)
    qh, kh, vh = (t.transpose(0, 2, 1, 3) for t in (q, k, v))
    q_blocks = jnp.moveaxis(qh.reshape(B, H, nb, BLOCK, Dh), 2, 0)
    F_blocks = jnp.moveaxis(F.reshape(B, H, nb, BLOCK), 2, 0)
    k_pos = jnp.arange(S)

    def one_block(args):
        qb, Fq, n = args
        s = jnp.einsum('bhqd,bhkd->bhqk', qb, kh).astype(jnp.float32) * scale
        s = s + Fq[..., :, None] - F[:, :, None, :]
        q_pos = n * BLOCK + jnp.arange(BLOCK)
        s = jnp.where(k_pos[None, :] <= q_pos[:, None], s, -jnp.inf)
        p = jax.nn.softmax(s, axis=-1)
        return jnp.einsum('bhqk,bhkd->bhqd', p.astype(vh.dtype), vh)

    o = lax.map(one_block, (q_blocks, F_blocks, jnp.arange(nb)))
    o = jnp.moveaxis(o, 0, 2).reshape(B, H, S, Dh)
    return o.transpose(0, 2, 1, 3)


def _dilated_pattern(q, k, v, dilation, steps):
    B, S, H, Dh = q.shape
    L = S // dilation
    BB = B * dilation
    scale = Dh ** -0.5

    def to_sub(t):
        return t.reshape(B, L, dilation, H, Dh).transpose(0, 2, 3, 1, 4).reshape(BB, H, L, Dh)

    nb = -(-L // BLOCK)
    Lp = nb * BLOCK
    pad = ((0, 0), (0, 0), (0, Lp - L), (0, 0))
    qs, ks, vs = (jnp.pad(to_sub(t), pad).reshape(BB, H, nb, BLOCK, Dh) for t in (q, k, v))
    blk_pad = ((0, 0), (0, 0), (1, 0), (0, 0), (0, 0))
    k_cat = jnp.concatenate([jnp.pad(ks, blk_pad)[:, :, :-1], ks], axis=3)
    v_cat = jnp.concatenate([jnp.pad(vs, blk_pad)[:, :, :-1], vs], axis=3)

    s = jnp.einsum('bhnqd,bhnkd->bhnqk', qs, k_cat).astype(jnp.float32) * scale
    qi = jnp.arange(BLOCK)[:, None]
    kj = jnp.arange(2 * BLOCK)[None, :]
    dist = qi + BLOCK - kj
    k_pos = jnp.arange(nb)[:, None, None] * BLOCK + kj - BLOCK
    mask = (dist >= 0) & (dist <= steps) & (k_pos >= 0)
    s = jnp.where(mask, s, -jnp.inf)
    m = jnp.max(s, axis=-1, keepdims=True)
    p = jnp.exp(s - m)
    l = jnp.sum(p, axis=-1, keepdims=True)
    o = jnp.einsum('bhnqk,bhnkd->bhnqd', (p / l).astype(v.dtype), v_cat)
    lse = (m + jnp.log(l))[..., 0]

    o = o.reshape(BB, H, Lp, Dh)[:, :, :L]
    o = o.reshape(B, dilation, H, L, Dh).transpose(0, 3, 1, 2, 4).reshape(B, S, H, Dh)
    lse = lse.reshape(BB, H, Lp)[:, :, :L]
    lse = lse.reshape(B, dilation, H, L).transpose(0, 3, 1, 2).reshape(B, S, H)
    return o, lse


def _dilated_attention(q, k, v):
    outs, lses = [], []
    for window, dilation in DILATED_PATTERNS:
        o, lse = _dilated_pattern(q, k, v, dilation, window // dilation)
        outs.append(o)
        lses.append(lse)
    w = jax.nn.softmax(jnp.stack(lses, axis=0), axis=0)
    o = jnp.sum(w[..., None] * jnp.stack(outs, axis=0).astype(jnp.float32), axis=0)
    return o.astype(q.dtype)


def _token_mixer(h, positions, w_in, b_fgate, gn_a, gn_b, w_out):
    B, S, _ = h.shape
    z = h @ w_in
    o0 = 0
    qa = z[..., o0:o0 + WIDTH_FOX]; o0 += WIDTH_FOX
    ka = z[..., o0:o0 + WIDTH_FOX]; o0 += WIDTH_FOX
    va = z[..., o0:o0 + WIDTH_FOX]; o0 += WIDTH_FOX
    fa = z[..., o0:o0 + N_HEADS_FOX]; o0 += N_HEADS_FOX
    qb = z[..., o0:o0 + WIDTH_DIL]; o0 += WIDTH_DIL
    kb = z[..., o0:o0 + WIDTH_DIL]; o0 += WIDTH_DIL
    vb = z[..., o0:o0 + WIDTH_DIL]

    heads = lambda t, H: t.reshape(B, S, H, HEAD_DIM)
    log_f = jax.nn.log_sigmoid((fa + b_fgate).astype(jnp.float32))
    oa = _forgetting_attention(heads(qa, N_HEADS_FOX), heads(ka, N_HEADS_FOX),
                               heads(va, N_HEADS_FOX), log_f)
    qb = _partial_rotary(heads(qb, N_HEADS_DIL), positions)
    kb = _partial_rotary(heads(kb, N_HEADS_DIL), positions)
    ob = _dilated_attention(qb, kb, heads(vb, N_HEADS_DIL))

    merged = jnp.concatenate([_head_rms_norm(oa, gn_a), _head_rms_norm(ob, gn_b)], axis=-1)
    return merged @ w_out


def _conv_ffn(h, w_up, conv_w, conv_b, w_down):
    u = h @ w_up
    up = jnp.pad(u, ((0, 0), (CONV_WIDTH - 1, 0), (0, 0)))
    S = h.shape[1]
    y = conv_b + sum(up[:, i:i + S] * conv_w[i] for i in range(CONV_WIDTH))
    a, g = jnp.split(y, 2, axis=-1)
    return (jax.nn.silu(g) * a) @ w_down


def _fwd_setup_inputs(seed: int = 0) -> dict:
    key = jax.random.key(seed)
    ks = jax.random.split(key, 20)
    n = jax.random.normal
    f32 = jnp.float32
    x = n(ks[0], (BATCH, SEQ, D_MODEL), f32)
    c = n(ks[1], (BATCH, D_MODEL), f32)
    offset = jax.random.randint(ks[2], (BATCH, 1), 0, 1024, dtype=jnp.int32)
    positions = (offset + jnp.arange(SEQ, dtype=jnp.int32)[None, :]).astype(jnp.int32)
    w_ada = n(ks[3], (DEPTH, D_MODEL, 6 * D_MODEL), f32) * D_MODEL ** -0.5
    b_ada = 0.02 * n(ks[4], (DEPTH, 6 * D_MODEL), f32)
    w_in = n(ks[5], (DEPTH, D_MODEL, D_IN), f32) * D_MODEL ** -0.5
    b_fgate = jnp.linspace(1.0, 6.0, N_HEADS_FOX, dtype=f32)[None, :] + 0.1 * n(ks[6], (DEPTH, N_HEADS_FOX), f32)
    gn_a = 1.0 + 0.02 * n(ks[7], (DEPTH, WIDTH_FOX), f32)
    gn_b = 1.0 + 0.02 * n(ks[8], (DEPTH, WIDTH_DIL), f32)
    w_out = n(ks[9], (DEPTH, D_MIX, D_MODEL), f32) * D_MIX ** -0.5 * DEEPNORM_BETA
    ln1_g = 1.0 + 0.02 * n(ks[10], (DEPTH, D_MODEL), f32)
    ln1_b = 0.02 * n(ks[11], (DEPTH, D_MODEL), f32)
    w_up = n(ks[12], (DEPTH, D_MODEL, 2 * D_FF), f32) * D_MODEL ** -0.5
    conv_w = n(ks[13], (DEPTH, CONV_WIDTH, 2 * D_FF), f32) * CONV_WIDTH ** -0.5
    conv_b = 0.02 * n(ks[14], (DEPTH, 2 * D_FF), f32)
    w_down = n(ks[15], (DEPTH, D_FF, D_MODEL), f32) * D_FF ** -0.5 * DEEPNORM_BETA
    ln2_g = 1.0 + 0.02 * n(ks[16], (DEPTH, D_MODEL), f32)
    ln2_b = 0.02 * n(ks[17], (DEPTH, D_MODEL), f32)
    return {"x": x, "c": c, "positions": positions, "w_ada": w_ada, "b_ada": b_ada,
            "w_in": w_in, "b_fgate": b_fgate, "gn_a": gn_a, "gn_b": gn_b, "w_out": w_out,
            "ln1_g": ln1_g, "ln1_b": ln1_b, "w_up": w_up, "conv_w": conv_w, "conv_b": conv_b,
            "w_down": w_down, "ln2_g": ln2_g, "ln2_b": ln2_b}


def _fwd_reference(x, c, positions, w_ada, b_ada, w_in, b_fgate, gn_a, gn_b, w_out,
              ln1_g, ln1_b, w_up, conv_w, conv_b, w_down, ln2_g, ln2_b):
    for l in range(DEPTH):
        ada = jax.nn.silu(c) @ w_ada[l] + b_ada[l]
        sh_a, sc_a, g_a, sh_f, sc_f, g_f = (t[:, None, :] for t in jnp.split(ada, 6, axis=-1))
        h = x * (1.0 + sc_a) + sh_a
        mix = _token_mixer(h, positions, w_in[l], b_fgate[l], gn_a[l], gn_b[l], w_out[l])
        x = _layer_norm(DEEPNORM_ALPHA * x + g_a * mix, ln1_g[l], ln1_b[l])
        h = x * (1.0 + sc_f) + sh_f
        ffn = _conv_ffn(h, w_up[l], conv_w[l], conv_b[l], w_down[l])
        x = _layer_norm(DEEPNORM_ALPHA * x + g_f * ffn, ln2_g[l], ln2_b[l])
    return x


import jax as _jax
import jax.numpy as _jnp

TWIN_FORMAT = 'train_step'
FWD_PARAMS = ['x', 'c', 'positions', 'w_ada', 'b_ada', 'w_in', 'b_fgate', 'gn_a', 'gn_b', 'w_out', 'ln1_g', 'ln1_b', 'w_up', 'conv_w', 'conv_b', 'w_down', 'ln2_g', 'ln2_b']
TWIN_WEIGHTS = ['w_ada', 'b_ada', 'w_in', 'b_fgate', 'gn_a', 'gn_b', 'w_out', 'ln1_g', 'ln1_b', 'w_up', 'conv_w', 'conv_b', 'w_down', 'ln2_g', 'ln2_b']
TWIN_DIFF_INPUT = 'x'
TWIN_INPUTS = ['x', 'c', 'positions', 'w_ada', 'b_ada', 'w_in', 'b_fgate', 'gn_a', 'gn_b', 'w_out', 'ln1_g', 'ln1_b', 'w_up', 'conv_w', 'conv_b', 'w_down', 'ln2_g', 'ln2_b', 'loss_target', 'm_w_ada', 'm_b_ada', 'm_w_in', 'm_b_fgate', 'm_gn_a', 'm_gn_b', 'm_w_out', 'm_ln1_g', 'm_ln1_b', 'm_w_up', 'm_conv_w', 'm_conv_b', 'm_w_down', 'm_ln2_g', 'm_ln2_b', 'v_w_ada', 'v_b_ada', 'v_w_in', 'v_b_fgate', 'v_gn_a', 'v_gn_b', 'v_w_out', 'v_ln1_g', 'v_ln1_b', 'v_w_up', 'v_conv_w', 'v_conv_b', 'v_w_down', 'v_ln2_g', 'v_ln2_b']
TWIN_OUTPUTS = ['loss', 'grad_x', 'grad_w_ada', 'grad_b_ada', 'grad_w_in', 'grad_b_fgate', 'grad_gn_a', 'grad_gn_b', 'grad_w_out', 'grad_ln1_g', 'grad_ln1_b', 'grad_w_up', 'grad_conv_w', 'grad_conv_b', 'grad_w_down', 'grad_ln2_g', 'grad_ln2_b', 'delta_w_ada', 'delta_b_ada', 'delta_w_in', 'delta_b_fgate', 'delta_gn_a', 'delta_gn_b', 'delta_w_out', 'delta_ln1_g', 'delta_ln1_b', 'delta_w_up', 'delta_conv_w', 'delta_conv_b', 'delta_w_down', 'delta_ln2_g', 'delta_ln2_b', 'new_m_w_ada', 'new_m_b_ada', 'new_m_w_in', 'new_m_b_fgate', 'new_m_gn_a', 'new_m_gn_b', 'new_m_w_out', 'new_m_ln1_g', 'new_m_ln1_b', 'new_m_w_up', 'new_m_conv_w', 'new_m_conv_b', 'new_m_w_down', 'new_m_ln2_g', 'new_m_ln2_b', 'new_v_w_ada', 'new_v_b_ada', 'new_v_w_in', 'new_v_b_fgate', 'new_v_gn_a', 'new_v_gn_b', 'new_v_w_out', 'new_v_ln1_g', 'new_v_ln1_b', 'new_v_w_up', 'new_v_conv_w', 'new_v_conv_b', 'new_v_w_down', 'new_v_ln2_g', 'new_v_ln2_b']
TWIN_LEAF_KINDS = {'loss': 'loss', 'grad_x': 'grad_x', 'grad_w_ada': 'grad_w', 'grad_b_ada': 'grad_w', 'grad_w_in': 'grad_w', 'grad_b_fgate': 'grad_w', 'grad_gn_a': 'grad_w', 'grad_gn_b': 'grad_w', 'grad_w_out': 'grad_w', 'grad_ln1_g': 'grad_w', 'grad_ln1_b': 'grad_w', 'grad_w_up': 'grad_w', 'grad_conv_w': 'grad_w', 'grad_conv_b': 'grad_w', 'grad_w_down': 'grad_w', 'grad_ln2_g': 'grad_w', 'grad_ln2_b': 'grad_w', 'delta_w_ada': 'delta_w', 'delta_b_ada': 'delta_w', 'delta_w_in': 'delta_w', 'delta_b_fgate': 'delta_w', 'delta_gn_a': 'delta_w', 'delta_gn_b': 'delta_w', 'delta_w_out': 'delta_w', 'delta_ln1_g': 'delta_w', 'delta_ln1_b': 'delta_w', 'delta_w_up': 'delta_w', 'delta_conv_w': 'delta_w', 'delta_conv_b': 'delta_w', 'delta_w_down': 'delta_w', 'delta_ln2_g': 'delta_w', 'delta_ln2_b': 'delta_w', 'new_m_w_ada': 'new_m', 'new_m_b_ada': 'new_m', 'new_m_w_in': 'new_m', 'new_m_b_fgate': 'new_m', 'new_m_gn_a': 'new_m', 'new_m_gn_b': 'new_m', 'new_m_w_out': 'new_m', 'new_m_ln1_g': 'new_m', 'new_m_ln1_b': 'new_m', 'new_m_w_up': 'new_m', 'new_m_conv_w': 'new_m', 'new_m_conv_b': 'new_m', 'new_m_w_down': 'new_m', 'new_m_ln2_g': 'new_m', 'new_m_ln2_b': 'new_m', 'new_v_w_ada': 'new_v', 'new_v_b_ada': 'new_v', 'new_v_w_in': 'new_v', 'new_v_b_fgate': 'new_v', 'new_v_gn_a': 'new_v', 'new_v_gn_b': 'new_v', 'new_v_w_out': 'new_v', 'new_v_ln1_g': 'new_v', 'new_v_ln1_b': 'new_v', 'new_v_w_up': 'new_v', 'new_v_conv_w': 'new_v', 'new_v_conv_b': 'new_v', 'new_v_w_down': 'new_v', 'new_v_ln2_g': 'new_v', 'new_v_ln2_b': 'new_v'}


def _forward(args):
    return _fwd_reference(*[args[k] for k in FWD_PARAMS])


def _output_shape():
    out = _jax.eval_shape(lambda: _forward(_fwd_setup_inputs(0)))
    return out.shape, out.dtype

N_MICROBATCH = 1
ADAM_LR = 0.001
ADAM_B1 = 0.9
ADAM_B2 = 0.999
ADAM_EPS = 1e-08
ADAM_WD = 0.01
ADAM_STEP = 10
PER_EXAMPLE_BATCH_AXIS = {'x': 0, 'c': 0, 'positions': 0, 'loss_target': 0}
SHARED_INPUTS = []
_WEIGHT_DTYPES = {'w_ada': _jnp.float32, 'b_ada': _jnp.float32, 'w_in': _jnp.float32, 'b_fgate': _jnp.float32, 'gn_a': _jnp.float32, 'gn_b': _jnp.float32, 'w_out': _jnp.float32, 'ln1_g': _jnp.float32, 'ln1_b': _jnp.float32, 'w_up': _jnp.float32, 'conv_w': _jnp.float32, 'conv_b': _jnp.float32, 'w_down': _jnp.float32, 'ln2_g': _jnp.float32, 'ln2_b': _jnp.float32}
MOMENT_SCALE = {'w_ada': 8.947464e-02, 'b_ada': 1.500704e-01, 'w_in': 6.960486e-02, 'b_fgate': 1.240065e-01, 'gn_a': 1.014232e-01, 'gn_b': 1.043131e-01, 'w_out': 1.833606e-01, 'ln1_g': 1.673572e+00, 'ln1_b': 6.610439e-01, 'w_up': 5.575236e-02, 'conv_w': 5.647929e-02, 'conv_b': 4.467164e-02, 'w_down': 1.534476e-01, 'ln2_g': 6.445859e+01, 'ln2_b': 8.853185e+00}


def _to_microbatches(a, axis):
    t = _jnp.moveaxis(a, axis, 0)
    t = t.reshape((N_MICROBATCH, t.shape[0] // N_MICROBATCH) + t.shape[1:])
    return _jnp.moveaxis(t, 1, axis + 1)


def setup_inputs(seed: int = 0) -> dict:
    inp = _fwd_setup_inputs(seed)
    key = _jax.random.fold_in(_jax.random.key(seed), 7919)
    shape, _ = _output_shape()
    out = dict(inp)
    out["loss_target"] = _jax.random.normal(_jax.random.fold_in(key, 0), shape, _jnp.float32)
    for i, name in enumerate(TWIN_WEIGHTS):
        w = inp[name].astype(_jnp.float32)
        if MOMENT_SCALE is None:
            s = _jnp.sqrt(_jnp.mean(_jnp.square(w)) + 1e-30)
        else:
            s = MOMENT_SCALE[name]
        km, kv = _jax.random.split(_jax.random.fold_in(key, i + 1))
        out[name] = w
        out["m_" + name] = s * _jax.random.normal(km, w.shape, _jnp.float32)
        out["v_" + name] = (s * s) * _jax.random.uniform(kv, w.shape, _jnp.float32, 0.5, 1.5)
    if N_MICROBATCH > 1:
        for name, axis in PER_EXAMPLE_BATCH_AXIS.items():
            out[name] = _to_microbatches(out[name], axis)
    return {'x': out['x'], 'c': out['c'], 'positions': out['positions'], 'w_ada': out['w_ada'], 'b_ada': out['b_ada'], 'w_in': out['w_in'], 'b_fgate': out['b_fgate'], 'gn_a': out['gn_a'], 'gn_b': out['gn_b'], 'w_out': out['w_out'], 'ln1_g': out['ln1_g'], 'ln1_b': out['ln1_b'], 'w_up': out['w_up'], 'conv_w': out['conv_w'], 'conv_b': out['conv_b'], 'w_down': out['w_down'], 'ln2_g': out['ln2_g'], 'ln2_b': out['ln2_b'], 'loss_target': out['loss_target'], 'm_w_ada': out['m_w_ada'], 'm_b_ada': out['m_b_ada'], 'm_w_in': out['m_w_in'], 'm_b_fgate': out['m_b_fgate'], 'm_gn_a': out['m_gn_a'], 'm_gn_b': out['m_gn_b'], 'm_w_out': out['m_w_out'], 'm_ln1_g': out['m_ln1_g'], 'm_ln1_b': out['m_ln1_b'], 'm_w_up': out['m_w_up'], 'm_conv_w': out['m_conv_w'], 'm_conv_b': out['m_conv_b'], 'm_w_down': out['m_w_down'], 'm_ln2_g': out['m_ln2_g'], 'm_ln2_b': out['m_ln2_b'], 'v_w_ada': out['v_w_ada'], 'v_b_ada': out['v_b_ada'], 'v_w_in': out['v_w_in'], 'v_b_fgate': out['v_b_fgate'], 'v_gn_a': out['v_gn_a'], 'v_gn_b': out['v_gn_b'], 'v_w_out': out['v_w_out'], 'v_ln1_g': out['v_ln1_g'], 'v_ln1_b': out['v_ln1_b'], 'v_w_up': out['v_w_up'], 'v_conv_w': out['v_conv_w'], 'v_conv_b': out['v_conv_b'], 'v_w_down': out['v_w_down'], 'v_ln2_g': out['v_ln2_g'], 'v_ln2_b': out['v_ln2_b']}


def _loss(weights, diff, rest, loss_target):
    with _jax.named_scope("forward"):
        args = {**rest, TWIN_DIFF_INPUT: diff, **{k: w.astype(_WEIGHT_DTYPES[k]) for k, w in weights.items()}}
        y = _forward(args)
    with _jax.named_scope("loss_head"):
        err = _jnp.square(y.astype(_jnp.float32) - loss_target)
        return 0.5 * _jnp.sum(_jnp.mean(err, axis=-1)) if err.ndim else 0.5 * err


def _adamw(w, g, m, v):
    m = ADAM_B1 * m + (1.0 - ADAM_B1) * g
    v = ADAM_B2 * v + (1.0 - ADAM_B2) * _jnp.square(g)
    m_hat = m / (1.0 - ADAM_B1 ** ADAM_STEP)
    v_hat = v / (1.0 - ADAM_B2 ** ADAM_STEP)
    delta = -ADAM_LR * (m_hat / (_jnp.sqrt(v_hat) + ADAM_EPS) + ADAM_WD * w)
    return delta, m, v


def reference(x, c, positions, w_ada, b_ada, w_in, b_fgate, gn_a, gn_b, w_out, ln1_g, ln1_b, w_up, conv_w, conv_b, w_down, ln2_g, ln2_b, loss_target, m_w_ada, m_b_ada, m_w_in, m_b_fgate, m_gn_a, m_gn_b, m_w_out, m_ln1_g, m_ln1_b, m_w_up, m_conv_w, m_conv_b, m_w_down, m_ln2_g, m_ln2_b, v_w_ada, v_b_ada, v_w_in, v_b_fgate, v_gn_a, v_gn_b, v_w_out, v_ln1_g, v_ln1_b, v_w_up, v_conv_w, v_conv_b, v_w_down, v_ln2_g, v_ln2_b):
    given = dict(x=x, c=c, positions=positions, w_ada=w_ada, b_ada=b_ada, w_in=w_in, b_fgate=b_fgate, gn_a=gn_a, gn_b=gn_b, w_out=w_out, ln1_g=ln1_g, ln1_b=ln1_b, w_up=w_up, conv_w=conv_w, conv_b=conv_b, w_down=w_down, ln2_g=ln2_g, ln2_b=ln2_b, loss_target=loss_target, m_w_ada=m_w_ada, m_b_ada=m_b_ada, m_w_in=m_w_in, m_b_fgate=m_b_fgate, m_gn_a=m_gn_a, m_gn_b=m_gn_b, m_w_out=m_w_out, m_ln1_g=m_ln1_g, m_ln1_b=m_ln1_b, m_w_up=m_w_up, m_conv_w=m_conv_w, m_conv_b=m_conv_b, m_w_down=m_w_down, m_ln2_g=m_ln2_g, m_ln2_b=m_ln2_b, v_w_ada=v_w_ada, v_b_ada=v_b_ada, v_w_in=v_w_in, v_b_fgate=v_b_fgate, v_gn_a=v_gn_a, v_gn_b=v_gn_b, v_w_out=v_w_out, v_ln1_g=v_ln1_g, v_ln1_b=v_ln1_b, v_w_up=v_w_up, v_conv_w=v_conv_w, v_conv_b=v_conv_b, v_w_down=v_w_down, v_ln2_g=v_ln2_g, v_ln2_b=v_ln2_b)
    weights = {n: given[n] for n in TWIN_WEIGHTS}
    shared = {n: given[n] for n in SHARED_INPUTS}
    per_example = {n: given[n] for n in ['x', 'c', 'positions']}
    grad_fn = _jax.value_and_grad(_loss, argnums=(0, 1))

    def one_microbatch(ex, loss_target):
        ex = dict(ex)
        diff = ex.pop(TWIN_DIFF_INPUT)
        return grad_fn(weights, diff, {**shared, **ex}, loss_target)

    if N_MICROBATCH == 1:
        loss, (grad_w, grad_x) = one_microbatch(per_example, given["loss_target"])
    else:
        def body(carry, xs):
            loss_sum, grad_sum = carry
            l_k, (gw_k, gx_k) = one_microbatch(xs[0], xs[1])
            with _jax.named_scope("update"):
                return (loss_sum + l_k, _jax.tree.map(_jnp.add, grad_sum, gw_k)), gx_k

        init = (_jnp.zeros((), _jnp.float32), _jax.tree.map(_jnp.zeros_like, weights))
        (loss, grad_w), grad_x = _jax.lax.scan(body, init, (per_example, given["loss_target"]))
    with _jax.named_scope("update"):
        delta_w, new_m, new_v = {}, {}, {}
        for n in TWIN_WEIGHTS:
            delta_w[n], new_m[n], new_v[n] = _adamw(weights[n], grad_w[n], given["m_" + n], given["v_" + n])
    return (loss, grad_x, *[grad_w[n] for n in TWIN_WEIGHTS], *[delta_w[n] for n in TWIN_WEIGHTS],
            *[new_m[n] for n in TWIN_WEIGHTS], *[new_v[n] for n in TWIN_WEIGHTS])
```

```python
import functools

import numpy as np
import jax
import jax.numpy as jnp
from jax import lax
from jax.experimental import pallas as pl
from jax.experimental.pallas import tpu as pltpu

F32, BF16 = jnp.float32, jnp.bfloat16
HIGHEST = lax.Precision.HIGHEST
MESH = pl.DeviceIdType.MESH
ANY = pl.BlockSpec(memory_space=pl.ANY)

D_MODEL = 1024
N_HEADS = 8
HEAD_DIM = 64
WIDTH = 512
D_FF = 2816
N_DEV = 8
ROPE_DIMS = 16
ROPE_THETA = 500000.0
ALPHA = 2.0 ** 0.25
LN_EPS = 1e-5
RMS_EPS = 1e-6
NEG = -1e30
Q_SCALE = 0.125
BLK = 128
LANES = 128
VMEM_LIMIT_BYTES = 56 * 1024 * 1024

ADAM_LR, ADAM_B1, ADAM_B2, ADAM_EPS, ADAM_WD, ADAM_STEP = 0.001, 0.9, 0.999, 1e-08, 0.01, 10


def _params(vmem=VMEM_LIMIT_BYTES):
    return pltpu.CompilerParams(vmem_limit_bytes=vmem)


def _nn(a, b):
    return jnp.dot(a, b, preferred_element_type=F32)


def _nt(a, b):
    return lax.dot_general(a, b, (((1,), (1,)), ((), ())), preferred_element_type=F32)


def _tn(a, b):
    return lax.dot_general(a, b, (((0,), (0,)), ((), ())), preferred_element_type=F32)


def _head_mats():
    r = lax.broadcasted_iota(jnp.int32, (LANES, WIDTH), 0)
    c = lax.broadcasted_iota(jnp.int32, (LANES, WIDTH), 1)
    e = ((c >> 6) == r).astype(F32)
    r2 = lax.broadcasted_iota(jnp.int32, (WIDTH, LANES), 0)
    c2 = lax.broadcasted_iota(jnp.int32, (WIDTH, LANES), 1)
    et = ((r2 >> 6) == c2).astype(F32)
    return e, et


def _hexp(w, e):
    return jnp.dot(w, e, precision=HIGHEST, preferred_element_type=F32)


def _hsum(x, et):
    return jnp.dot(x, et, precision=HIGHEST, preferred_element_type=F32)


def _rope_tabs(pos_ref, fr_ref, sign):
    ang = pos_ref[...].astype(F32) * fr_ref[...]
    lane = lax.broadcasted_iota(jnp.int32, ang.shape, 1) & (HEAD_DIM - 1)
    m1 = lane < ROPE_DIMS // 2
    m2 = (lane >= ROPE_DIMS // 2) & (lane < ROPE_DIMS)
    cos = jnp.cos(ang)
    sin = jnp.sin(ang) * sign
    return (jnp.where(m1 | m2, cos, 1.0), jnp.where(m1, -sin, 0.0), jnp.where(m2, sin, 0.0))


def _rope(z, tabs):
    c, s1, s2 = tabs
    parts = []
    for p in range(z.shape[1] // LANES):
        zp = z[:, LANES * p:LANES * (p + 1)]
        parts.append(zp * c + pltpu.roll(zp, LANES - 8, 1) * s1 + pltpu.roll(zp, 8, 1) * s2)
    return jnp.concatenate(parts, axis=1)


def _half_masks(rows):
    lane = lax.broadcasted_iota(jnp.int32, (rows, LANES), 1)
    lo = lane < HEAD_DIM
    return lo, jnp.logical_not(lo)


def _layer_norm_bwd(dxh, xh, rstd):
    m1 = jnp.mean(dxh, axis=1, keepdims=True)
    m2 = jnp.mean(dxh * xh, axis=1, keepdims=True)
    return rstd * (dxh - m1 - xh * m2)


def _coords():
    return lax.axis_index("x"), lax.axis_index("y"), lax.axis_index("c")


def _peer(x, y, c, k):
    return (1 - x if k & 4 else x, 1 - y if k & 2 else y, 1 - c if k & 1 else c)


def _weight_gather(shards):
    n = len(shards)

    def body(*refs):
        ins, outs = refs[:n], refs[n:2 * n]
        send_sems, recv_sems, local_sems = refs[2 * n:]
        x, y, c = _coords()
        me = 4 * x + 2 * y + c
        local = [pltpu.make_async_copy(ins[t], outs[t].at[me], local_sems.at[t]) for t in range(n)]
        for cp in local:
            cp.start()
        copies = []
        for k in range(1, N_DEV):
            for t in range(n):
                cp = pltpu.make_async_remote_copy(
                    src_ref=ins[t], dst_ref=outs[t].at[me], send_sem=send_sems.at[k - 1, t],
                    recv_sem=recv_sems.at[k - 1, t], device_id=_peer(x, y, c, k), device_id_type=MESH)
                cp.start()
                copies.append(cp)
        for cp in copies:
            cp.wait()
        for cp in local:
            cp.wait()

    return pl.pallas_call(
        body, name="weight_gather",
        out_shape=[jax.ShapeDtypeStruct((N_DEV,) + s.shape, s.dtype) for s in shards],
        in_specs=[ANY] * n, out_specs=[ANY] * n,
        scratch_shapes=[pltpu.SemaphoreType.DMA((N_DEV - 1, n)), pltpu.SemaphoreType.DMA((N_DEV - 1, n)),
                        pltpu.SemaphoreType.DMA((n,))],
    )(*shards)


def _grad_exchange(grads, small):
    n = len(grads)

    def body(*refs):
        ins, small_ref = refs[:n], refs[n]
        outs, small_out = refs[n + 1:2 * n + 1], refs[2 * n + 1]
        send_sems, recv_sems, local_sems = refs[2 * n + 2:]
        x, y, c = _coords()
        me = 4 * x + 2 * y + c
        local = [pltpu.make_async_copy(ins[t].at[me], outs[t].at[me], local_sems.at[t]) for t in range(n)]
        local.append(pltpu.make_async_copy(small_ref, small_out.at[me], local_sems.at[n]))
        for cp in local:
            cp.start()
        copies = []
        for k in range(1, N_DEV):
            px, py, pc = _peer(x, y, c, k)
            dest = 4 * px + 2 * py + pc
            for t in range(n + 1):
                src = small_ref if t == n else ins[t].at[dest]
                dst = small_out.at[me] if t == n else outs[t].at[me]
                cp = pltpu.make_async_remote_copy(
                    src_ref=src, dst_ref=dst, send_sem=send_sems.at[k - 1, t], recv_sem=recv_sems.at[k - 1, t],
                    device_id=(px, py, pc), device_id_type=MESH)
                cp.start()
                copies.append(cp)
        for cp in copies:
            cp.wait()
        for cp in local:
            cp.wait()

    shapes = [jax.ShapeDtypeStruct(g.shape, g.dtype) for g in grads]
    shapes.append(jax.ShapeDtypeStruct((N_DEV,) + small.shape, small.dtype))
    return pl.pallas_call(
        body, name="grad_exchange", out_shape=shapes,
        in_specs=[ANY] * (n + 1), out_specs=[ANY] * (n + 1),
        scratch_shapes=[pltpu.SemaphoreType.DMA((N_DEV - 1, n + 1)), pltpu.SemaphoreType.DMA((N_DEV - 1, n + 1)),
                        pltpu.SemaphoreType.DMA((n + 1,))],
    )(*grads, small)


def _adamw(parts, w, m, v, rows, name):
    _, r_all, cols = parts.shape
    c1 = 1.0 - ADAM_B1 ** ADAM_STEP
    c2 = 1.0 - ADAM_B2 ** ADAM_STEP

    def body(p_ref, w_ref, m_ref, v_ref, g_ref, d_ref, mo_ref, vo_ref):
        g = p_ref[0]
        for s in range(1, N_DEV):
            g = g + p_ref[s]
        mn = ADAM_B1 * m_ref[...] + (1.0 - ADAM_B1) * g
        vn = ADAM_B2 * v_ref[...] + (1.0 - ADAM_B2) * (g * g)
        m_hat = mn / c1
        v_hat = vn / c2
        g_ref[...] = g
        d_ref[...] = -ADAM_LR * (m_hat / (jnp.sqrt(v_hat) + ADAM_EPS) + ADAM_WD * w_ref[...])
        mo_ref[...] = mn
        vo_ref[...] = vn

    spec = pl.BlockSpec((rows, cols), lambda i: (i, 0))
    return pl.pallas_call(
        body, name=name, grid=(r_all // rows,),
        in_specs=[pl.BlockSpec((N_DEV, rows, cols), lambda i: (0, i, 0)), spec, spec, spec],
        out_specs=[spec] * 4, out_shape=[jax.ShapeDtypeStruct((r_all, cols), F32)] * 4,
        compiler_params=_params(),
    )(parts, w, m, v)


def _matmul(a, w, transposed_w, out_dtype, tm, chunk, name):
    t_all, k = a.shape
    n = w.shape[0] if transposed_w else w.shape[1]

    def body(a_ref, w_ref, o_ref):
        av = a_ref[...]
        for j in range(n // chunk):
            cs = slice(j * chunk, (j + 1) * chunk)
            r = _nt(av, w_ref[cs, :]) if transposed_w else _nn(av, w_ref[:, cs])
            o_ref[:, cs] = r.astype(out_dtype)

    return pl.pallas_call(
        body, name=name, grid=(t_all // tm,),
        in_specs=[pl.BlockSpec((tm, k), lambda i: (i, 0)), pl.BlockSpec(w.shape, lambda i: (0, 0))],
        out_specs=pl.BlockSpec((tm, n), lambda i: (i, 0)),
        out_shape=jax.ShapeDtypeStruct((t_all, n), out_dtype), compiler_params=_params(),
    )(a, w)


def _matmul_tn(a, b, tn, tk, name):
    t_all, k1 = a.shape
    n = b.shape[1]

    def body(a_ref, b_ref, o_ref):
        @pl.when(pl.program_id(1) == 0)
        def _():
            o_ref[...] = jnp.zeros_like(o_ref)
        o_ref[...] += _tn(a_ref[...], b_ref[...])

    return pl.pallas_call(
        body, name=name, grid=(n // tn, t_all // tk),
        in_specs=[pl.BlockSpec((tk, k1), lambda j, t: (t, 0)), pl.BlockSpec((tk, tn), lambda j, t: (t, j))],
        out_specs=pl.BlockSpec((k1, tn), lambda j, t: (0, j)),
        out_shape=jax.ShapeDtypeStruct((k1, n), F32), compiler_params=_params(),
    )(a, b)


def _matmul_rows(a, b, tk, name):
    r, t_all = a.shape
    n = b.shape[1]

    def body(a_ref, b_ref, o_ref):
        @pl.when(pl.program_id(0) == 0)
        def _():
            o_ref[...] = jnp.zeros_like(o_ref)
        o_ref[...] += _nn(a_ref[...], b_ref[...])

    return pl.pallas_call(
        body, name=name, grid=(t_all // tk,),
        in_specs=[pl.BlockSpec((r, tk), lambda t: (0, t)), pl.BlockSpec((tk, n), lambda t: (t, 0))],
        out_specs=pl.BlockSpec((r, n), lambda t: (0, 0)),
        out_shape=jax.ShapeDtypeStruct((r, n), F32), compiler_params=_params(),
    )(a, b)


def _ada_fwd(c16, w_ada, b_ada):
    n = w_ada.shape[1]
    tn = n // N_DEV

    def body(c_ref, w_ref, b_ref, o_ref):
        cv = c_ref[...]
        s = (cv * jax.nn.sigmoid(cv)).astype(BF16)
        o_ref[...] = _nn(s, w_ref[...]) + b_ref[...]

    return pl.pallas_call(
        body, name="ada_fwd", grid=(N_DEV,),
        in_specs=[pl.BlockSpec(c16.shape, lambda j: (0, 0)), pl.BlockSpec((D_MODEL, tn), lambda j: (0, j)),
                  pl.BlockSpec((1, tn), lambda j: (0, j))],
        out_specs=pl.BlockSpec((c16.shape[0], tn), lambda j: (0, j)),
        out_shape=jax.ShapeDtypeStruct((c16.shape[0], n), F32), compiler_params=_params(),
    )(c16, w_ada, b_ada)


def _ada_bwd(c16, dada16):
    n = dada16.shape[1]
    tn = n // N_DEV

    def body(c_ref, d_ref, o_ref):
        cv = c_ref[...]
        s = (cv * jax.nn.sigmoid(cv)).astype(BF16)
        o_ref[...] = _tn(s, d_ref[...].astype(BF16))

    return pl.pallas_call(
        body, name="ada_bwd", grid=(N_DEV,),
        in_specs=[pl.BlockSpec(c16.shape, lambda j: (0, 0)), pl.BlockSpec((c16.shape[0], tn), lambda j: (0, j))],
        out_specs=pl.BlockSpec((D_MODEL, tn), lambda j: (0, j)),
        out_shape=jax.ShapeDtypeStruct((D_MODEL, n), F32), compiler_params=_params(),
    )(c16, dada16)


def _inproj(x, ada3, pos, wqkv, wf16, freq, seq):
    t_all = x.shape[0]
    tm = 256
    nts = seq // tm

    def body(x_ref, ada_ref, pos_ref, w_ref, wf_ref, fr_ref, h1_ref, za_ref, zb_ref, fa_ref):
        h1 = (x_ref[...] * (1.0 + ada_ref[0, 1:2, :]) + ada_ref[0, 0:1, :]).astype(BF16)
        h1_ref[...] = h1
        tabs = _rope_tabs(pos_ref, fr_ref, 1.0)
        for n in range(6):
            z = _nn(h1, w_ref[:, n * WIDTH:(n + 1) * WIDTH])
            if n in (3, 4):
                z = _rope(z, tabs)
            if n in (0, 3):
                z = z * Q_SCALE
            dst = za_ref if n < 3 else zb_ref
            dst[:, (n % 3) * WIDTH:(n % 3 + 1) * WIDTH] = z.astype(BF16)
        fa_ref[...] = _nt(wf_ref[...], h1)[:N_HEADS]

    tok = lambda w: pl.BlockSpec((tm, w), lambda i: (i, 0))
    return pl.pallas_call(
        body, name="inproj", grid=(t_all // tm,),
        in_specs=[tok(D_MODEL), pl.BlockSpec((1, 6, D_MODEL), lambda i: (i // nts, 0, 0)), tok(1),
                  pl.BlockSpec(wqkv.shape, lambda i: (0, 0)), pl.BlockSpec(wf16.shape, lambda i: (0, 0)),
                  pl.BlockSpec((1, LANES), lambda i: (0, 0))],
        out_specs=[tok(D_MODEL), tok(3 * WIDTH), tok(3 * WIDTH), pl.BlockSpec((N_HEADS, tm), lambda i: (0, i))],
        out_shape=[jax.ShapeDtypeStruct((t_all, D_MODEL), BF16), jax.ShapeDtypeStruct((t_all, 3 * WIDTH), BF16),
                   jax.ShapeDtypeStruct((t_all, 3 * WIDTH), BF16), jax.ShapeDtypeStruct((N_HEADS, t_all), F32)],
        compiler_params=_params(),
    )(x, ada3, pos, wqkv, wf16, freq)


def _fgate_fwd(fa_t, bf, seq):
    t_all = fa_t.shape[1]

    def body(fa_ref, b_ref, f_ref):
        lane = lax.broadcasted_iota(jnp.int32, (N_HEADS, LANES), 1)

        def chunk(j, carry):
            sl = pl.ds(pl.multiple_of(j * LANES, LANES), LANES)
            xv = fa_ref[:, sl] + b_ref[...]
            lf = jnp.minimum(xv, 0.0) - jnp.log(1.0 + jnp.exp(-jnp.abs(xv)))
            for s in (1, 2, 4, 8, 16, 32, 64):
                lf = lf + jnp.where(lane >= s, pltpu.roll(lf, s, 1), 0.0)
            lf = lf + carry
            f_ref[:, sl] = lf
            return lf[:, LANES - 1:LANES]

        lax.fori_loop(0, seq // LANES, chunk, jnp.zeros((N_HEADS, 1), F32))

    return pl.pallas_call(
        body, name="fgate_fwd", grid=(t_all // seq,),
        in_specs=[pl.BlockSpec((N_HEADS, seq), lambda b: (0, b)), pl.BlockSpec((N_HEADS, 1), lambda b: (0, 0))],
        out_specs=pl.BlockSpec((N_HEADS, seq), lambda b: (0, b)),
        out_shape=jax.ShapeDtypeStruct((N_HEADS, t_all), F32), compiler_params=_params(),
    )(fa_t, bf)


def _fgate_bwd(df_t, fa_t, bf, seq):
    t_all = fa_t.shape[1]

    def body(df_ref, fa_ref, b_ref, o_ref, s_ref):
        lane = lax.broadcasted_iota(jnp.int32, (N_HEADS, LANES), 1)

        @pl.when(pl.program_id(0) == 0)
        def _():
            s_ref[...] = jnp.zeros_like(s_ref)

        def chunk(jj, carry):
            car, tot = carry
            j = seq // LANES - 1 - jj
            sl = pl.ds(pl.multiple_of(j * LANES, LANES), LANES)
            d = df_ref[:, sl]
            for s in (1, 2, 4, 8, 16, 32, 64):
                d = d + jnp.where(lane < LANES - s, pltpu.roll(d, LANES - s, 1), 0.0)
            d = d + car
            dfa = d * jax.nn.sigmoid(-(fa_ref[:, sl] + b_ref[...]))
            o_ref[:, sl] = dfa
            return d[:, 0:1], tot + jnp.sum(dfa, axis=1, keepdims=True)

        z = jnp.zeros((N_HEADS, 1), F32)
        _, tot = lax.fori_loop(0, seq // LANES, chunk, (z, z))
        s_ref[...] += jnp.broadcast_to(tot, (N_HEADS, LANES))

    row = pl.BlockSpec((N_HEADS, seq), lambda b: (0, b))
    return pl.pallas_call(
        body, name="fgate_bwd", grid=(t_all // seq,),
        in_specs=[row, row, pl.BlockSpec((N_HEADS, 1), lambda b: (0, 0))],
        out_specs=[row, pl.BlockSpec((N_HEADS, LANES), lambda b: (0, 0))],
        out_shape=[jax.ShapeDtypeStruct((N_HEADS, t_all), F32), jax.ShapeDtypeStruct((N_HEADS, LANES), F32)],
        compiler_params=_params(),
    )(df_t, fa_t, bf)


FOX_T = 256


def _fox_fwd(za, f_col, f_row, seq):
    t_all = za.shape[0]
    tq = FOX_T
    nq = seq // tq

    def body(q_ref, k_ref, v_ref, fc_ref, fr_ref, o_ref, lse_ref):
        i = pl.program_id(1)
        lo, hi = _half_masks(tq)
        r = lax.broadcasted_iota(jnp.int32, (tq, tq), 0)
        c = lax.broadcasted_iota(jnp.int32, (tq, tq), 1)
        tri = c <= r
        lse_ref[...] = jnp.zeros_like(lse_ref)
        for p in range(4):
            cs = slice(LANES * p, LANES * (p + 1))
            qp = q_ref[:, cs]
            res = []
            for e in (0, 1):
                h = 2 * p + e
                qe = jnp.where(lo if e == 0 else hi, qp, jnp.zeros_like(qp))
                fq = fc_ref[:, h:h + 1]

                def scores(j, qe=qe, fq=fq, h=h, cs=cs):
                    sl = pl.ds(pl.multiple_of(j * tq, tq), tq)
                    return _nt(qe, k_ref[sl, cs]) + fq - fr_ref[h:h + 1, sl], sl

                def update(s, sl, m, l, acc, cs=cs):
                    mn = jnp.maximum(m, jnp.max(s, axis=1, keepdims=True))
                    a = jnp.exp(m - mn)
                    pe = jnp.exp(s - mn)
                    return (mn, a * l + jnp.sum(pe, axis=1, keepdims=True),
                            a * acc + _nn(pe.astype(BF16), v_ref[sl, cs]))

                def step(j, car, scores=scores, update=update):
                    s, sl = scores(j)
                    return update(s, sl, *car)

                init = (jnp.full((tq, 1), NEG, F32), jnp.zeros((tq, 1), F32), jnp.zeros((tq, LANES), F32))
                car = lax.fori_loop(0, i, step, init)
                s, sl = scores(i)
                m, l, acc = update(jnp.where(tri, s, NEG), sl, *car)
                res.append(acc / l)
                lse_ref[:, h:h + 1] = m + jnp.log(l)
            o_ref[:, cs] = jnp.where(lo, res[0], res[1])

    return pl.pallas_call(
        body, name="fox_fwd", grid=(t_all // seq, nq),
        in_specs=[pl.BlockSpec((tq, WIDTH), lambda b, i: (b * nq + i, 0)),
                  pl.BlockSpec((seq, WIDTH), lambda b, i: (b, 1)), pl.BlockSpec((seq, WIDTH), lambda b, i: (b, 2)),
                  pl.BlockSpec((tq, LANES), lambda b, i: (b * nq + i, 0)),
                  pl.BlockSpec((N_HEADS, seq), lambda b, i: (0, b))],
        out_specs=[pl.BlockSpec((tq, WIDTH), lambda b, i: (b * nq + i, 0)),
                   pl.BlockSpec((tq, LANES), lambda b, i: (b * nq + i, 0))],
        out_shape=[jax.ShapeDtypeStruct((t_all, WIDTH), F32), jax.ShapeDtypeStruct((t_all, LANES), F32)],
        compiler_params=_params(),
    )(za, za, za, f_col, f_row)


def _fox_bwd_dq(za, do, f_col, f_row, lse, dl, seq):
    t_all = za.shape[0]
    tq = FOX_T
    nq = seq // tq

    def body(q_ref, k_ref, v_ref, do_ref, fc_ref, fr_ref, lse_ref, dl_ref, dq_ref, df_ref):
        i = pl.program_id(1)
        lo, hi = _half_masks(tq)
        r = lax.broadcasted_iota(jnp.int32, (tq, tq), 0)
        c = lax.broadcasted_iota(jnp.int32, (tq, tq), 1)
        tri = c <= r
        df_ref[...] = jnp.zeros_like(df_ref)
        for p in range(4):
            cs = slice(LANES * p, LANES * (p + 1))
            qp = q_ref[:, cs]
            dop = do_ref[:, cs]
            res = []
            for e in (0, 1):
                h = 2 * p + e
                sel = lo if e == 0 else hi
                qe = jnp.where(sel, qp, jnp.zeros_like(qp))
                doe = jnp.where(sel, dop, jnp.zeros_like(dop))
                fq = fc_ref[:, h:h + 1] - lse_ref[:, h:h + 1]
                dlh = dl_ref[:, h:h + 1]

                def contrib(j, masked, qe=qe, doe=doe, fq=fq, dlh=dlh, h=h, cs=cs):
                    sl = pl.ds(pl.multiple_of(j * tq, tq), tq)
                    kj = k_ref[sl, cs]
                    s = _nt(qe, kj) + fq - fr_ref[h:h + 1, sl]
                    if masked:
                        s = jnp.where(tri, s, NEG)
                    pr = jnp.exp(s)
                    ds = pr * (_nt(doe, v_ref[sl, cs]) - dlh)
                    return _nn(ds.astype(BF16), kj), jnp.sum(ds, axis=1, keepdims=True)

                def step(j, car, contrib=contrib):
                    a, d = contrib(j, False)
                    return car[0] + a, car[1] + d

                acc, dfh = lax.fori_loop(0, i, step, (jnp.zeros((tq, LANES), F32), jnp.zeros((tq, 1), F32)))
                a, d = contrib(i, True)
                res.append(acc + a)
                df_ref[:, h:h + 1] = dfh + d
            dq_ref[:, cs] = (jnp.where(lo, res[0], res[1]) * Q_SCALE).astype(BF16)

    tile = lambda w: pl.BlockSpec((tq, w), lambda b, i: (b * nq + i, 0))
    return pl.pallas_call(
        body, name="fox_bwd_dq", grid=(t_all // seq, nq),
        in_specs=[tile(WIDTH), pl.BlockSpec((seq, WIDTH), lambda b, i: (b, 1)),
                  pl.BlockSpec((seq, WIDTH), lambda b, i: (b, 2)), tile(WIDTH), tile(LANES),
                  pl.BlockSpec((N_HEADS, seq), lambda b, i: (0, b)), tile(LANES), tile(LANES)],
        out_specs=[tile(WIDTH), tile(LANES)],
        out_shape=[jax.ShapeDtypeStruct((t_all, WIDTH), BF16), jax.ShapeDtypeStruct((t_all, LANES), F32)],
        compiler_params=_params(),
    )(za, za, za, do, f_col, f_row, lse, dl)


def _fox_bwd_dkv(za, do, f_col, f_row, lse_row, dl_row, seq):
    t_all = za.shape[0]
    tk = FOX_T
    nk = seq // tk

    def body(k_ref, v_ref, q_ref, do_ref, fc_ref, fr_ref, lr_ref, dr_ref, dk_ref, dv_ref, df_ref):
        j = pl.program_id(1)
        lo, hi = _half_masks(tk)
        r = lax.broadcasted_iota(jnp.int32, (tk, tk), 0)
        c = lax.broadcasted_iota(jnp.int32, (tk, tk), 1)
        tri = c >= r
        df_ref[...] = jnp.zeros_like(df_ref)
        for p in range(4):
            cs = slice(LANES * p, LANES * (p + 1))
            kp = k_ref[:, cs]
            vp = v_ref[:, cs]
            rk, rv = [], []
            for e in (0, 1):
                h = 2 * p + e
                sel = lo if e == 0 else hi
                ke = jnp.where(sel, kp, jnp.zeros_like(kp))
                ve = jnp.where(sel, vp, jnp.zeros_like(vp))
                fk = fc_ref[:, h:h + 1]

                def contrib(i, masked, ke=ke, ve=ve, fk=fk, h=h, cs=cs):
                    sl = pl.ds(pl.multiple_of(i * tk, tk), tk)
                    qi = q_ref[sl, cs]
                    doi = do_ref[sl, cs]
                    st = _nt(ke, qi) + (fr_ref[h:h + 1, sl] - lr_ref[h:h + 1, sl]) - fk
                    if masked:
                        st = jnp.where(tri, st, NEG)
                    pt = jnp.exp(st)
                    dst = pt * (_nt(ve, doi) - dr_ref[h:h + 1, sl])
                    return (_nn(dst.astype(BF16), qi), _nn(pt.astype(BF16), doi),
                            jnp.sum(dst, axis=1, keepdims=True))

                def step(i, car, contrib=contrib):
                    a, b, d = contrib(i, False)
                    return car[0] + a, car[1] + b, car[2] + d

                first = contrib(j, True)
                dk, dv, dfh = lax.fori_loop(j + 1, nk, step, first)
                rk.append(dk)
                rv.append(dv)
                df_ref[:, h:h + 1] = -dfh
            dk_ref[:, cs] = jnp.where(lo, rk[0], rk[1]).astype(BF16)
            dv_ref[:, cs] = jnp.where(lo, rv[0], rv[1]).astype(BF16)

    tile = lambda w, col: pl.BlockSpec((tk, w), lambda b, j: (b * nk + j, col))
    full = lambda col: pl.BlockSpec((seq, WIDTH), lambda b, j: (b, col))
    row = pl.BlockSpec((N_HEADS, seq), lambda b, j: (0, b))
    return pl.pallas_call(
        body, name="fox_bwd_dkv", grid=(t_all // seq, nk),
        in_specs=[tile(WIDTH, 1), tile(WIDTH, 2), full(0), full(0), tile(LANES, 0), row, row, row],
        out_specs=[tile(WIDTH, 0), tile(WIDTH, 0), tile(LANES, 0)],
        out_shape=[jax.ShapeDtypeStruct((t_all, WIDTH), BF16), jax.ShapeDtypeStruct((t_all, WIDTH), BF16),
                   jax.ShapeDtypeStruct((t_all, LANES), F32)],
        compiler_params=_params(),
    )(za, za, za, do, f_col, f_row, lse_row, dl_row)


def _dil_mask(has_prev):
    qi = lax.broadcasted_iota(jnp.int32, (BLK, 2 * BLK), 0)
    kj = lax.broadcasted_iota(jnp.int32, (BLK, 2 * BLK), 1)
    dist = qi + BLK - kj
    return (dist >= 0) & (dist <= BLK) & ((kj >= BLK) | has_prev)


def _dil_specs(t_all):
    nb = t_all // BLK
    cur = pl.BlockSpec((1, BLK, WIDTH), lambda p, n: (p, n, 0))
    prev = pl.BlockSpec((1, BLK, WIDTH), lambda p, n: (p, jnp.maximum(n - 1, 0), 0))
    nxt = pl.BlockSpec((1, BLK, WIDTH), lambda p, n: (p, jnp.minimum(n + 1, nb - 1), 0))
    stat = pl.BlockSpec((1, BLK, LANES), lambda p, n: (p, n, 0))
    return nb, cur, prev, nxt, stat


def _dil_fwd(qs, ks, vs, seq):
    t_all = qs.shape[1]
    nb, cur, prev, _, stat = _dil_specs(t_all)

    def body(q_ref, kp_ref, kc_ref, vp_ref, vc_ref, o_ref, lse_ref):
        nbs = (seq // BLK) >> (2 * pl.program_id(0))
        mask = _dil_mask((pl.program_id(1) & (nbs - 1)) != 0)
        lo, hi = _half_masks(BLK)
        lse_ref[...] = jnp.zeros_like(lse_ref)
        for p in range(4):
            cs = slice(LANES * p, LANES * (p + 1))
            qp = q_ref[0, :, cs]
            kcat = jnp.concatenate([kp_ref[0, :, cs], kc_ref[0, :, cs]], axis=0)
            vcat = jnp.concatenate([vp_ref[0, :, cs], vc_ref[0, :, cs]], axis=0)
            res = []
            for e in (0, 1):
                h = 2 * p + e
                qe = jnp.where(lo if e == 0 else hi, qp, jnp.zeros_like(qp))
                s = jnp.where(mask, _nt(qe, kcat), NEG)
                m = jnp.max(s, axis=1, keepdims=True)
                pe = jnp.exp(s - m)
                l = jnp.sum(pe, axis=1, keepdims=True)
                res.append(_nn(pe.astype(BF16), vcat) / l)
                lse_ref[0, :, h:h + 1] = m + jnp.log(l)
            o_ref[0, :, cs] = jnp.where(lo, res[0], res[1])

    return pl.pallas_call(
        body, name="dil_fwd", grid=(3, nb), in_specs=[cur, prev, cur, prev, cur], out_specs=[cur, stat],
        out_shape=[jax.ShapeDtypeStruct((3, t_all, WIDTH), F32), jax.ShapeDtypeStruct((3, t_all, LANES), F32)],
        compiler_params=_params(),
    )(qs, ks, ks, vs, vs)


def _dil_bwd_dq(qs, ks, vs, dos, lses, dls, seq):
    t_all = qs.shape[1]
    nb, cur, prev, _, stat = _dil_specs(t_all)

    def body(q_ref, kp_ref, kc_ref, vp_ref, vc_ref, do_ref, lse_ref, dl_ref, dq_ref):
        nbs = (seq // BLK) >> (2 * pl.program_id(0))
        mask = _dil_mask((pl.program_id(1) & (nbs - 1)) != 0)
        lo, hi = _half_masks(BLK)
        for p in range(4):
            cs = slice(LANES * p, LANES * (p + 1))
            qp = q_ref[0, :, cs]
            dop = do_ref[0, :, cs]
            kcat = jnp.concatenate([kp_ref[0, :, cs], kc_ref[0, :, cs]], axis=0)
            vcat = jnp.concatenate([vp_ref[0, :, cs], vc_ref[0, :, cs]], axis=0)
            res = []
            for e in (0, 1):
                h = 2 * p + e
                sel = lo if e == 0 else hi
                qe = jnp.where(sel, qp, jnp.zeros_like(qp))
                doe = jnp.where(sel, dop, jnp.zeros_like(dop))
                s = jnp.where(mask, _nt(qe, kcat) - lse_ref[0, :, h:h + 1], NEG)
                ds = jnp.exp(s) * (_nt(doe, vcat) - dl_ref[0, :, h:h + 1])
                res.append(_nn(ds.astype(BF16), kcat))
            dq_ref[0, :, cs] = jnp.where(lo, res[0], res[1]) * Q_SCALE

    return pl.pallas_call(
        body, name="dil_bwd_dq", grid=(3, nb), in_specs=[cur, prev, cur, prev, cur, cur, stat, stat], out_specs=cur,
        out_shape=jax.ShapeDtypeStruct((3, t_all, WIDTH), F32), compiler_params=_params(),
    )(qs, ks, ks, vs, vs, dos, lses, dls)


def _dil_bwd_dkv(qs, ks, vs, dos, lse_rows, dl_rows, seq):
    t_all = qs.shape[1]
    nb, cur, _, nxt, _ = _dil_specs(t_all)
    rcur = pl.BlockSpec((1, N_HEADS, BLK), lambda p, n: (p, 0, n))
    rnxt = pl.BlockSpec((1, N_HEADS, BLK), lambda p, n: (p, 0, jnp.minimum(n + 1, nb - 1)))

    def body(k_ref, v_ref, qc_ref, qn_ref, dc_ref, dn_ref, lc_ref, ln_ref, ec_ref, en_ref, dk_ref, dv_ref):
        nbs = (seq // BLK) >> (2 * pl.program_id(0))
        has_next = ((pl.program_id(1) + 1) & (nbs - 1)) != 0
        r = lax.broadcasted_iota(jnp.int32, (BLK, 2 * BLK), 0)
        c = lax.broadcasted_iota(jnp.int32, (BLK, 2 * BLK), 1)
        mask = ((c < BLK) & (c >= r)) | ((c >= BLK) & (c - BLK <= r) & has_next)
        lo, hi = _half_masks(BLK)
        for p in range(4):
            cs = slice(LANES * p, LANES * (p + 1))
            kp = k_ref[0, :, cs]
            vp = v_ref[0, :, cs]
            qcat = jnp.concatenate([qc_ref[0, :, cs], qn_ref[0, :, cs]], axis=0)
            dcat = jnp.concatenate([dc_ref[0, :, cs], dn_ref[0, :, cs]], axis=0)
            rk, rv = [], []
            for e in (0, 1):
                h = 2 * p + e
                sel = lo if e == 0 else hi
                ke = jnp.where(sel, kp, jnp.zeros_like(kp))
                ve = jnp.where(sel, vp, jnp.zeros_like(vp))
                lrow = jnp.concatenate([lc_ref[0, h:h + 1, :], ln_ref[0, h:h + 1, :]], axis=1)
                erow = jnp.concatenate([ec_ref[0, h:h + 1, :], en_ref[0, h:h + 1, :]], axis=1)
                st = jnp.where(mask, _nt(ke, qcat) - lrow, NEG)
                pt = jnp.exp(st)
                dst = pt * (_nt(ve, dcat) - erow)
                rk.append(_nn(dst.astype(BF16), qcat))
                rv.append(_nn(pt.astype(BF16), dcat))
            dk_ref[0, :, cs] = jnp.where(lo, rk[0], rk[1])
            dv_ref[0, :, cs] = jnp.where(lo, rv[0], rv[1])

    return pl.pallas_call(
        body, name="dil_bwd_dkv", grid=(3, nb),
        in_specs=[cur, cur, cur, nxt, cur, nxt, rcur, rnxt, rcur, rnxt], out_specs=[cur, cur],
        out_shape=[jax.ShapeDtypeStruct((3, t_all, WIDTH), F32)] * 2, compiler_params=_params(),
    )(ks, vs, qs, qs, dos, dos, lse_rows, lse_rows, dl_rows, dl_rows)


def _mix_out(oa, o3, l3, gn_a, gn_b, w_out, x, ada3, ln_g, ln_b, seq):
    t_all = x.shape[0]
    tm = 256
    nts = seq // tm

    def body(oa_ref, o1_ref, o2_ref, o3_ref, l1_ref, l2_ref, l3_ref, ga_ref, gb_ref, w_ref, x_ref, ada_ref, g_ref,
             b_ref, ob_ref, lse_ref, mg_ref, mix_ref, xh_ref, rs_ref, h2_ref):
        e, et = _head_mats()
        la, lb, lc = l1_ref[...], l2_ref[...], l3_ref[...]
        mx = jnp.maximum(jnp.maximum(la, lb), lc)
        ea, eb, ec = jnp.exp(la - mx), jnp.exp(lb - mx), jnp.exp(lc - mx)
        tot = ea + eb + ec
        lse_ref[...] = mx + jnp.log(tot)
        ob = (o1_ref[...] * _hexp(ea / tot, e) + o2_ref[...] * _hexp(eb / tot, e) + o3_ref[...] * _hexp(ec / tot, e))
        ob_ref[...] = ob

        def rms(o, gain):
            rr = lax.rsqrt(_hsum(o * o, et) * (1.0 / HEAD_DIM) + RMS_EPS)
            return o * _hexp(rr, e) * gain

        merged = jnp.concatenate([rms(oa_ref[...], ga_ref[...]), rms(ob, gb_ref[...])], axis=1).astype(BF16)
        mg_ref[...] = merged
        mix = _nn(merged, w_ref[...])
        mix_ref[...] = mix.astype(BF16)
        r1 = ALPHA * x_ref[...] + ada_ref[0, 2:3, :] * mix
        d = r1 - jnp.mean(r1, axis=1, keepdims=True)
        rstd = lax.rsqrt(jnp.mean(d * d, axis=1, keepdims=True) + LN_EPS)
        xh = d * rstd
        xh_ref[...] = xh
        rs_ref[...] = jnp.broadcast_to(rstd, (tm, LANES))
        x1 = xh * g_ref[...] + b_ref[...]
        h2_ref[...] = (x1 * (1.0 + ada_ref[0, 4:5, :]) + ada_ref[0, 3:4, :]).astype(BF16)

    tok = lambda w: pl.BlockSpec((tm, w), lambda i: (i, 0))
    vec = lambda w: pl.BlockSpec((1, w), lambda i: (0, 0))
    return pl.pallas_call(
        body, name="mix_out", grid=(t_all // tm,),
        in_specs=[tok(WIDTH)] * 4 + [tok(LANES)] * 3 + [vec(WIDTH), vec(WIDTH),
                  pl.BlockSpec(w_out.shape, lambda i: (0, 0)), tok(D_MODEL),
                  pl.BlockSpec((1, 6, D_MODEL), lambda i: (i // nts, 0, 0)), vec(D_MODEL), vec(D_MODEL)],
        out_specs=[tok(WIDTH), tok(LANES), tok(D_MODEL), tok(D_MODEL), tok(D_MODEL), tok(LANES), tok(D_MODEL)],
        out_shape=[jax.ShapeDtypeStruct((t_all, WIDTH), F32), jax.ShapeDtypeStruct((t_all, LANES), F32),
                   jax.ShapeDtypeStruct((t_all, D_MODEL), BF16), jax.ShapeDtypeStruct((t_all, D_MODEL), BF16),
                   jax.ShapeDtypeStruct((t_all, D_MODEL), F32), jax.ShapeDtypeStruct((t_all, LANES), F32),
                   jax.ShapeDtypeStruct((t_all, D_MODEL), BF16)],
        compiler_params=_params(),
    )(oa, o3[0], o3[1], o3[2], l3[0], l3[1], l3[2], gn_a, gn_b, w_out, x, ada3, ln_g, ln_b)


def _mix_out_bwd(dmix, w_out, oa, ob, gn_a, gn_b):
    t_all = dmix.shape[0]
    tm = 256

    def body(dm_ref, w_ref, oa_ref, ob_ref, ga_ref, gb_ref, doa_ref, dob_ref, dla_ref, dlb_ref, acc_ref):
        @pl.when(pl.program_id(0) == 0)
        def _():
            acc_ref[...] = jnp.zeros_like(acc_ref)
        e, et = _head_mats()
        dmg = _nt(dm_ref[...], w_ref[...])

        def group(o, dn, gain):
            rr = lax.rsqrt(_hsum(o * o, et) * (1.0 / HEAD_DIM) + RMS_EPS)
            re = _hexp(rr, e)
            dgain = jnp.sum(dn * o * re, axis=0, keepdims=True)
            dxn = dn * gain
            tt = _hsum(dxn * o, et) * (rr * rr * rr) * (1.0 / HEAD_DIM)
            do = re * dxn - o * _hexp(tt, e)
            return do, _hsum(do * o, et), dgain

        doa, dla, dga = group(oa_ref[...], dmg[:, :WIDTH], ga_ref[...])
        dob, dlb, dgb = group(ob_ref[...], dmg[:, WIDTH:], gb_ref[...])
        doa_ref[...] = doa.astype(BF16)
        dob_ref[...] = dob.astype(BF16)
        dla_ref[...] = dla
        dlb_ref[...] = dlb
        acc_ref[0:1, :] += jnp.concatenate([dga, dgb], axis=1)

    tok = lambda w: pl.BlockSpec((tm, w), lambda i: (i, 0))
    vec = lambda w: pl.BlockSpec((1, w), lambda i: (0, 0))
    return pl.pallas_call(
        body, name="mix_out_bwd", grid=(t_all // tm,),
        in_specs=[tok(D_MODEL), pl.BlockSpec(w_out.shape, lambda i: (0, 0)), tok(WIDTH), tok(WIDTH), vec(WIDTH),
                  vec(WIDTH)],
        out_specs=[tok(WIDTH), tok(WIDTH), tok(LANES), tok(LANES), pl.BlockSpec((8, D_MODEL), lambda i: (0, 0))],
        out_shape=[jax.ShapeDtypeStruct((t_all, WIDTH), BF16), jax.ShapeDtypeStruct((t_all, WIDTH), BF16),
                   jax.ShapeDtypeStruct((t_all, LANES), F32), jax.ShapeDtypeStruct((t_all, LANES), F32),
                   jax.ShapeDtypeStruct((8, D_MODEL), F32)],
        compiler_params=_params(),
    )(dmix, w_out, oa, ob, gn_a, gn_b)


def _inproj_bwd(dza, dqb, dkb, dvb, dfa16, pos, wqkv, wf16, freq, dr1, x, ada3, seq):
    t_all = x.shape[0]
    tm = 256
    nts = seq // tm
    nbat = t_all // seq

    def body(dza_ref, dqb_ref, dkb_ref, dvb_ref, dfa_ref, pos_ref, w_ref, wf_ref, fr_ref, dr1_ref, x_ref, ada_ref,
             gx_ref, dz_ref, acc_ref):
        i = pl.program_id(0)

        @pl.when(i == 0)
        def _():
            acc_ref[...] = jnp.zeros_like(acc_ref)
        tabs = _rope_tabs(pos_ref, fr_ref, -1.0)
        dz_ref[:, :3 * WIDTH] = dza_ref[...]
        dz_ref[:, 3 * WIDTH:4 * WIDTH] = _rope(dqb_ref[...], tabs).astype(BF16)
        dz_ref[:, 4 * WIDTH:5 * WIDTH] = _rope(dkb_ref[...], tabs).astype(BF16)
        dz_ref[:, 5 * WIDTH:] = dvb_ref[...].astype(BF16)
        dh1 = _tn(dfa_ref[...], wf_ref[...])
        for n in range(6):
            cs = slice(n * WIDTH, (n + 1) * WIDTH)
            dh1 = dh1 + _nt(dz_ref[:, cs], w_ref[:, cs])
        xv = x_ref[...]
        gx_ref[...] = ALPHA * dr1_ref[...] + dh1 * (1.0 + ada_ref[0, 1:2, :])
        b = i // nts
        acc_ref[pl.ds(b, 1), :] += jnp.sum(dh1 * xv, axis=0, keepdims=True)
        acc_ref[pl.ds(8 + b, 1), :] += jnp.sum(dh1, axis=0, keepdims=True)

    tok = lambda w: pl.BlockSpec((tm, w), lambda i: (i, 0))
    return pl.pallas_call(
        body, name="inproj_bwd", grid=(t_all // tm,),
        in_specs=[tok(3 * WIDTH), tok(WIDTH), tok(WIDTH), tok(WIDTH), pl.BlockSpec((16, tm), lambda i: (0, i)),
                  tok(1), pl.BlockSpec(wqkv.shape, lambda i: (0, 0)), pl.BlockSpec(wf16.shape, lambda i: (0, 0)),
                  pl.BlockSpec((1, LANES), lambda i: (0, 0)), tok(D_MODEL), tok(D_MODEL),
                  pl.BlockSpec((1, 6, D_MODEL), lambda i: (i // nts, 0, 0))],
        out_specs=[tok(D_MODEL), tok(6 * WIDTH), pl.BlockSpec((16, D_MODEL), lambda i: (0, 0))],
        out_shape=[jax.ShapeDtypeStruct((t_all, D_MODEL), F32), jax.ShapeDtypeStruct((t_all, 6 * WIDTH), BF16),
                   jax.ShapeDtypeStruct((16, D_MODEL), F32)],
        compiler_params=_params(),
    )(dza, dqb, dkb, dvb, dfa16, pos, wqkv, wf16, freq, dr1, x, ada3)


FFN_TM = 512
FFN_TN = 256
HALO = 8


def _conv(cat_ref, w_ref, b_ref, rows):
    return (b_ref[...] + w_ref[0:1, :] * cat_ref[pl.ds(HALO - 2, rows), :] + w_ref[1:2, :] * cat_ref[pl.ds(HALO - 1, rows), :]
            + w_ref[2:3, :] * cat_ref[pl.ds(HALO, rows), :])


def _ffn_gate(u, conv_w, conv_b, seq):
    t_all = u.shape[0]
    tm, tn = FFN_TM, FFN_TN
    nc = D_FF // tn
    nts = seq // tm

    def body(ua_ref, uap_ref, ug_ref, ugp_ref, wa_ref, wg_ref, ba_ref, bg_ref, o_ref, ca_ref, cg_ref):
        first = (pl.program_id(0) % nts) == 0
        zero = jnp.zeros((HALO, tn), F32)
        ca_ref[0:HALO, :] = jnp.where(first, zero, uap_ref[...])
        cg_ref[0:HALO, :] = jnp.where(first, zero, ugp_ref[...])
        ca_ref[HALO:, :] = ua_ref[...]
        cg_ref[HALO:, :] = ug_ref[...]
        ya = _conv(ca_ref, wa_ref, ba_ref, tm)
        yg = _conv(cg_ref, wg_ref, bg_ref, tm)
        o_ref[...] = (yg * jax.nn.sigmoid(yg) * ya).astype(BF16)

    cur = lambda off: pl.BlockSpec((tm, tn), lambda t, n: (t, n + off))
    prev = lambda off: pl.BlockSpec((HALO, tn), lambda t, n: (jnp.maximum(t * (tm // HALO) - 1, 0), n + off))
    vec = lambda r, off: pl.BlockSpec((r, tn), lambda t, n: (0, n + off))
    return pl.pallas_call(
        body, name="ffn_gate", grid=(t_all // tm, nc),
        in_specs=[cur(0), prev(0), cur(nc), prev(nc), vec(3, 0), vec(3, nc), vec(1, 0), vec(1, nc)],
        out_specs=pl.BlockSpec((tm, tn), lambda t, n: (t, n)),
        out_shape=jax.ShapeDtypeStruct((t_all, D_FF), BF16),
        scratch_shapes=[pltpu.VMEM((tm + HALO, tn), F32)] * 2, compiler_params=_params(),
    )(u, u, u, u, conv_w, conv_w, conv_b, conv_b)


def _ffn_gate_bwd(u, dfi, conv_w, conv_b, seq):
    t_all = u.shape[0]
    tm, tn = FFN_TM, FFN_TN
    nc = D_FF // tn
    nts = seq // tm
    ext = tm + HALO

    def body(ua_ref, uap_ref, uan_ref, ug_ref, ugp_ref, ugn_ref, df_ref, dfn_ref, wa_ref, wg_ref, ba_ref, bg_ref,
             dua_ref, dug_ref, acca_ref, accg_ref, ca_ref, cg_ref, ya_ref, yg_ref):
        t = pl.program_id(1)
        first = (t % nts) == 0
        last = (t % nts) == nts - 1

        @pl.when(t == 0)
        def _():
            acca_ref[...] = jnp.zeros_like(acca_ref)
            accg_ref[...] = jnp.zeros_like(accg_ref)
        zero = jnp.zeros((HALO, tn), F32)
        for cat, cur, prv, nxt in ((ca_ref, ua_ref, uap_ref, uan_ref), (cg_ref, ug_ref, ugp_ref, ugn_ref)):
            cat[0:HALO, :] = jnp.where(first, zero, prv[...])
            cat[HALO:HALO + tm, :] = cur[...]
            cat[HALO + tm:, :] = nxt[...]
        ya = _conv(ca_ref, wa_ref, ba_ref, ext)
        yg = _conv(cg_ref, wg_ref, bg_ref, ext)
        dfe = jnp.concatenate([df_ref[...].astype(F32), dfn_ref[0:HALO, :].astype(F32)], axis=0)
        row = lax.broadcasted_iota(jnp.int32, (ext, tn), 0)
        dfe = jnp.where((row < tm) | jnp.logical_not(last), dfe, 0.0)
        sg = jax.nn.sigmoid(yg)
        ya_ref[...] = dfe * (yg * sg)
        yg_ref[...] = dfe * ya * (sg * (1.0 + yg * (1.0 - sg)))
        for dy, cat, w_ref, du_ref, acc in ((ya_ref, ca_ref, wa_ref, dua_ref, acca_ref),
                                             (yg_ref, cg_ref, wg_ref, dug_ref, accg_ref)):
            d0 = dy[0:tm, :]
            du = w_ref[2:3, :] * d0 + w_ref[1:2, :] * dy[pl.ds(1, tm), :] + w_ref[0:1, :] * dy[pl.ds(2, tm), :]
            du_ref[...] = du.astype(BF16)
            for k in range(3):
                acc[k:k + 1, :] += jnp.sum(d0 * cat[pl.ds(HALO - 2 + k, tm), :], axis=0, keepdims=True)
            acc[3:4, :] += jnp.sum(d0, axis=0, keepdims=True)

    nrow = t_all // HALO
    cur = lambda off: pl.BlockSpec((tm, tn), lambda n, t: (t, n + off))
    prev = lambda off: pl.BlockSpec((HALO, tn), lambda n, t: (jnp.maximum(t * (tm // HALO) - 1, 0), n + off))
    nxt = lambda off: pl.BlockSpec((HALO, tn), lambda n, t: (jnp.minimum((t + 1) * (tm // HALO), nrow - 1), n + off))
    vec = lambda r, off: pl.BlockSpec((r, tn), lambda n, t: (0, n + off))
    dcur = pl.BlockSpec((tm, tn), lambda n, t: (t, n))
    dnxt = pl.BlockSpec((16, tn), lambda n, t: (jnp.minimum((t + 1) * (tm // 16), t_all // 16 - 1), n))
    acc = pl.BlockSpec((8, tn), lambda n, t: (0, n))
    return pl.pallas_call(
        body, name="ffn_gate_bwd", grid=(nc, t_all // tm),
        in_specs=[cur(0), prev(0), nxt(0), cur(nc), prev(nc), nxt(nc), dcur, dnxt, vec(3, 0), vec(3, nc), vec(1, 0),
                  vec(1, nc)],
        out_specs=[dcur, dcur, acc, acc],
        out_shape=[jax.ShapeDtypeStruct((t_all, D_FF), BF16), jax.ShapeDtypeStruct((t_all, D_FF), BF16),
                   jax.ShapeDtypeStruct((8, D_FF), F32), jax.ShapeDtypeStruct((8, D_FF), F32)],
        scratch_shapes=[pltpu.VMEM((tm + 2 * HALO, tn), F32)] * 2 + [pltpu.VMEM((ext, tn), F32)] * 2,
        compiler_params=_params(),
    )(u, u, u, u, u, u, dfi, dfi, conv_w, conv_w, conv_b, conv_b)


def _ffn_down(ffn_in, w_down, xh1, ln1_g, ln1_b, ada3, ln2_g, ln2_b, target, seq):
    t_all = xh1.shape[0]
    tm = 256
    nts = seq // tm

    def body(f_ref, w_ref, xh_ref, g1_ref, b1_ref, ada_ref, g2_ref, b2_ref, tg_ref, dr2_ref, acc_ref):
        i = pl.program_id(0)

        @pl.when(i == 0)
        def _():
            acc_ref[...] = jnp.zeros_like(acc_ref)
        ffn = _nn(f_ref[...], w_ref[...])
        x1 = xh_ref[...] * g1_ref[...] + b1_ref[...]
        r2 = ALPHA * x1 + ada_ref[0, 5:6, :] * ffn
        d = r2 - jnp.mean(r2, axis=1, keepdims=True)
        rstd = lax.rsqrt(jnp.mean(d * d, axis=1, keepdims=True) + LN_EPS)
        xh2 = d * rstd
        diff = xh2 * g2_ref[...] + b2_ref[...] - tg_ref[...]
        dy = diff * (1.0 / D_MODEL)
        dr2 = _layer_norm_bwd(dy * g2_ref[...], xh2, rstd)
        dr2_ref[...] = dr2
        acc_ref[0:1, :] += jnp.sum(dy * xh2, axis=0, keepdims=True)
        acc_ref[1:2, :] += jnp.sum(dy, axis=0, keepdims=True)
        acc_ref[2:3, :] += jnp.sum(diff * diff, axis=0, keepdims=True) * (0.5 / D_MODEL)
        acc_ref[pl.ds(8 + i // nts, 1), :] += jnp.sum(dr2 * ffn, axis=0, keepdims=True)

    tok = lambda w: pl.BlockSpec((tm, w), lambda i: (i, 0))
    vec = pl.BlockSpec((1, D_MODEL), lambda i: (0, 0))
    return pl.pallas_call(
        body, name="ffn_down", grid=(t_all // tm,),
        in_specs=[tok(D_FF), pl.BlockSpec(w_down.shape, lambda i: (0, 0)), tok(D_MODEL), vec, vec,
                  pl.BlockSpec((1, 6, D_MODEL), lambda i: (i // nts, 0, 0)), vec, vec, tok(D_MODEL)],
        out_specs=[tok(D_MODEL), pl.BlockSpec((16, D_MODEL), lambda i: (0, 0))],
        out_shape=[jax.ShapeDtypeStruct((t_all, D_MODEL), F32), jax.ShapeDtypeStruct((16, D_MODEL), F32)],
        compiler_params=_params(),
    )(ffn_in, w_down, xh1, ln1_g, ln1_b, ada3, ln2_g, ln2_b, target)


def _ffn_down_bwd(dr2, ada3, w_down, seq):
    t_all = dr2.shape[0]
    tm = 256
    nts = seq // tm

    def body(d_ref, ada_ref, w_ref, dffn_ref, dfi_ref):
        dffn = (d_ref[...] * ada_ref[0, 5:6, :]).astype(BF16)
        dffn_ref[...] = dffn
        dfi_ref[...] = _nt(dffn, w_ref[...]).astype(BF16)

    tok = lambda w: pl.BlockSpec((tm, w), lambda i: (i, 0))
    return pl.pallas_call(
        body, name="ffn_down_bwd", grid=(t_all // tm,),
        in_specs=[tok(D_MODEL), pl.BlockSpec((1, 6, D_MODEL), lambda i: (i // nts, 0, 0)),
                  pl.BlockSpec(w_down.shape, lambda i: (0, 0))],
        out_specs=[tok(D_MODEL), tok(D_FF)],
        out_shape=[jax.ShapeDtypeStruct((t_all, D_MODEL), BF16), jax.ShapeDtypeStruct((t_all, D_FF), BF16)],
        compiler_params=_params(),
    )(dr2, ada3, w_down)


def _ffn_up_bwd(du_a, du_g, w_up, dr2, xh1, rs1, mix, ada3, ln1_g, ln1_b, seq):
    t_all = dr2.shape[0]
    tm = 256
    nts = seq // tm

    def body(da_ref, dg_ref, w_ref, dr2_ref, xh_ref, rs_ref, mix_ref, ada_ref, g_ref, b_ref, dr1_ref, dmix_ref,
             acc_ref):
        i = pl.program_id(0)

        @pl.when(i == 0)
        def _():
            acc_ref[...] = jnp.zeros_like(acc_ref)
        dh2 = _nt(da_ref[...], w_ref[:, :D_FF]) + _nt(dg_ref[...], w_ref[:, D_FF:])
        xh = xh_ref[...]
        x1 = xh * g_ref[...] + b_ref[...]
        dx1 = ALPHA * dr2_ref[...] + dh2 * (1.0 + ada_ref[0, 4:5, :])
        dr1 = _layer_norm_bwd(dx1 * g_ref[...], xh, rs_ref[:, 0:1])
        dr1_ref[...] = dr1
        dmix_ref[...] = (dr1 * ada_ref[0, 2:3, :]).astype(BF16)
        b = i // nts
        acc_ref[0:1, :] += jnp.sum(dx1 * xh, axis=0, keepdims=True)
        acc_ref[1:2, :] += jnp.sum(dx1, axis=0, keepdims=True)
        acc_ref[pl.ds(8 + b, 1), :] += jnp.sum(dh2 * x1, axis=0, keepdims=True)
        acc_ref[pl.ds(16 + b, 1), :] += jnp.sum(dh2, axis=0, keepdims=True)
        acc_ref[pl.ds(24 + b, 1), :] += jnp.sum(dr1 * mix_ref[...].astype(F32), axis=0, keepdims=True)

    tok = lambda w: pl.BlockSpec((tm, w), lambda i: (i, 0))
    vec = pl.BlockSpec((1, D_MODEL), lambda i: (0, 0))
    return pl.pallas_call(
        body, name="ffn_up_bwd", grid=(t_all // tm,),
        in_specs=[tok(D_FF), tok(D_FF), pl.BlockSpec(w_up.shape, lambda i: (0, 0)), tok(D_MODEL), tok(D_MODEL),
                  tok(LANES), tok(D_MODEL), pl.BlockSpec((1, 6, D_MODEL), lambda i: (i // nts, 0, 0)), vec, vec],
        out_specs=[tok(D_MODEL), tok(D_MODEL), pl.BlockSpec((32, D_MODEL), lambda i: (0, 0))],
        out_shape=[jax.ShapeDtypeStruct((t_all, D_MODEL), F32), jax.ShapeDtypeStruct((t_all, D_MODEL), BF16),
                   jax.ShapeDtypeStruct((32, D_MODEL), F32)],
        compiler_params=_params(),
    )(du_a, du_g, w_up, dr2, xh1, rs1, mix, ada3, ln1_g, ln1_b)


def _perm(a, d, seq):
    if d == 1:
        return a
    t_all, w = a.shape
    return a.reshape(t_all // seq, seq // d, d, w).transpose(0, 2, 1, 3).reshape(t_all, w)


def _unperm(a, d, seq):
    if d == 1:
        return a
    t_all, w = a.shape
    return a.reshape(t_all // seq, d, seq // d, w).transpose(0, 2, 1, 3).reshape(t_all, w)


DILATIONS = (1, 4, 16)


def _stack_perm(a, seq):
    return jnp.stack([_perm(a, d, seq) for d in DILATIONS])


def _rows(a):
    return a[:, :N_HEADS].T


def _rope_freq():
    f = np.float32(ROPE_THETA) ** (-np.arange(0, ROPE_DIMS, 2, dtype=np.float32) / np.float32(ROPE_DIMS))
    return jnp.asarray(np.tile(f.astype(np.float32), LANES // (ROPE_DIMS // 2))[None, :])


def _local_step(x, c, positions, target, w_ada, b_ada, w_in, b_fgate, gn_a, gn_b, w_out, ln1_g, ln1_b, w_up,
                conv_w, conv_b, w_down, ln2_g, ln2_b):
    nbat, seq, _ = x.shape
    t_all = nbat * seq
    xf = x.reshape(t_all, D_MODEL)
    tg = target.reshape(t_all, D_MODEL)
    pos = positions.reshape(t_all, 1)
    freq = _rope_freq()
    c16 = jnp.zeros((16, D_MODEL), F32).at[:nbat].set(c)

    wqkv = jnp.concatenate([w_in[:, :3 * WIDTH], w_in[:, 3 * WIDTH + N_HEADS:]], axis=1)
    wf16 = jnp.zeros((16, D_MODEL), BF16).at[:N_HEADS].set(w_in[:, 3 * WIDTH:3 * WIDTH + N_HEADS].T)
    bf = b_fgate.reshape(N_HEADS, 1)

    ada = _ada_fwd(c16, w_ada, b_ada)
    ada3 = ada[:nbat].reshape(nbat, 6, D_MODEL)
    h1, za, zb, fa_t = _inproj(xf, ada3, pos, wqkv, wf16, freq, seq)
    f_row = _fgate_fwd(fa_t, bf, seq)
    f_col = jnp.zeros((t_all, LANES), F32).at[:, :N_HEADS].set(f_row.T)
    oa, lse_a = _fox_fwd(za, f_col, f_row, seq)
    qs = _stack_perm(zb[:, :WIDTH], seq)
    ks = _stack_perm(zb[:, WIDTH:2 * WIDTH], seq)
    vs = _stack_perm(zb[:, 2 * WIDTH:], seq)
    o3p, l3p = _dil_fwd(qs, ks, vs, seq)
    o3 = [_unperm(o3p[p], d, seq) for p, d in enumerate(DILATIONS)]
    l3 = [_unperm(l3p[p], d, seq) for p, d in enumerate(DILATIONS)]
    ob, lse_b, merged, mix, xh1, rs1, h2 = _mix_out(oa, o3, l3, gn_a, gn_b, w_out, xf, ada3, ln1_g, ln1_b, seq)
    u = _matmul(h2, w_up, False, F32, 256, 512, "ffn_up")
    ffn_in = _ffn_gate(u, conv_w, conv_b, seq)
    dr2, acc2 = _ffn_down(ffn_in, w_down, xh1, ln1_g, ln1_b, ada3, ln2_g, ln2_b, tg, seq)

    dffn, dfi = _ffn_down_bwd(dr2, ada3, w_down, seq)
    d_w_down = _matmul_tn(ffn_in, dffn, 512, 512, "dw_down")
    du_a, du_g, acc_ca, acc_cg = _ffn_gate_bwd(u, dfi, conv_w, conv_b, seq)
    dr1, dmix, acc1 = _ffn_up_bwd(du_a, du_g, w_up, dr2, xh1, rs1, mix, ada3, ln1_g, ln1_b, seq)
    d_w_up = jnp.concatenate([_matmul_tn(h2, du_a, 256, 512, "dw_up_a"), _matmul_tn(h2, du_g, 256, 512, "dw_up_g")],
                             axis=1)

    doa, dob, dl_a, dl_b, acc_gn = _mix_out_bwd(dmix, w_out, oa, ob, gn_a, gn_b)
    d_w_out = _matmul_tn(merged, dmix, 512, 512, "dw_out")
    dqa, df_q = _fox_bwd_dq(za, doa, f_col, f_row, lse_a, dl_a, seq)
    dka, dva, df_k = _fox_bwd_dkv(za, doa, f_col, f_row, _rows(lse_a), _rows(dl_a), seq)
    dfa_t, dbf = _fgate_bwd(_rows(df_q + df_k), fa_t, bf, seq)
    dos = _stack_perm(dob, seq)
    lses = _stack_perm(lse_b, seq)
    dls = _stack_perm(dl_b, seq)
    dq3 = _dil_bwd_dq(qs, ks, vs, dos, lses, dls, seq)
    lse_rows = lses[:, :, :N_HEADS].transpose(0, 2, 1)
    dl_rows = dls[:, :, :N_HEADS].transpose(0, 2, 1)
    dk3, dv3 = _dil_bwd_dkv(qs, ks, vs, dos, lse_rows, dl_rows, seq)
    unsum = lambda a3: sum(_unperm(a3[p], d, seq) for p, d in enumerate(DILATIONS))
    dza = jnp.concatenate([dqa, dka, dva], axis=1)
    dfa16 = jnp.zeros((16, t_all), BF16).at[:N_HEADS].set(dfa_t.astype(BF16))
    grad_x, dz, acc0 = _inproj_bwd(dza, unsum(dq3), unsum(dk3), unsum(dv3), dfa16, pos, wqkv, wf16, freq, dr1, xf,
                                   ada3, seq)
    d_wqkv = _matmul_tn(h1, dz, 512, 512, "dw_in")
    d_wf = _matmul_rows(dfa16, h1, 512, "dw_fgate")[:N_HEADS].T
    d_w_in = jnp.concatenate([d_wqkv[:, :3 * WIDTH], d_wf, d_wqkv[:, 3 * WIDTH:]], axis=1)

    dada = jnp.concatenate([acc0[8:8 + nbat], acc0[:nbat], acc1[24:24 + nbat], acc1[16:16 + nbat], acc1[8:8 + nbat],
                            acc2[8:8 + nbat]], axis=1)
    dada16 = jnp.zeros((16, 6 * D_MODEL), F32).at[:nbat].set(dada)
    d_w_ada = _ada_bwd(c16, dada16)

    grads = dict(
        w_ada=d_w_ada, b_ada=jnp.sum(dada, axis=0, keepdims=True), w_in=d_w_in, b_fgate=dbf[:, 0][None, :],
        gn_a=acc_gn[0:1, :WIDTH], gn_b=acc_gn[0:1, WIDTH:], w_out=d_w_out, ln1_g=acc1[0:1], ln1_b=acc1[1:2],
        w_up=d_w_up, conv_w=jnp.concatenate([acc_ca[0:3], acc_cg[0:3]], axis=1),
        conv_b=jnp.concatenate([acc_ca[3:4], acc_cg[3:4]], axis=1), w_down=d_w_down, ln2_g=acc2[0:1],
        ln2_b=acc2[1:2])
    return acc2[2:3], grad_x.reshape(x.shape), grads


BIG = ("w_ada", "w_in", "w_out", "w_up", "conv_w", "w_down")
COLUMN_SHARDED = ("w_ada", "w_in", "w_up", "conv_w")
SMALL = ("b_ada", "b_fgate", "gn_a", "gn_b", "ln1_g", "ln1_b", "conv_b", "ln2_g", "ln2_b")
ADAM_ROWS = dict(w_ada=256, w_in=256, w_out=128, w_up=256, conv_w=3, w_down=176)
SMALL_ROWS = 24


def _full_from_gathered(name, g):
    if name in COLUMN_SHARDED:
        return g.transpose(1, 0, 2).reshape(g.shape[1], N_DEV * g.shape[2])
    return g.reshape(N_DEV * g.shape[1], g.shape[2])


def _dest_major(name, full):
    if name in COLUMN_SHARDED:
        r, cfull = full.shape
        return full.reshape(r, N_DEV, cfull // N_DEV).transpose(1, 0, 2)
    return full.reshape(N_DEV, full.shape[0] // N_DEV, full.shape[1])


def _pack_small(vals, extra=None):
    parts = [vals[n].reshape(-1) for n in SMALL]
    if extra is not None:
        parts.append(extra.reshape(-1))
    flat = jnp.concatenate(parts)
    return jnp.pad(flat, (0, SMALL_ROWS * D_MODEL - flat.shape[0])).reshape(SMALL_ROWS, D_MODEL)


def _unpack_small(packed, like):
    flat = packed.reshape(-1)
    out, off = {}, 0
    for n in SMALL:
        size = like[n].size
        out[n] = flat[off:off + size].reshape(like[n].shape)
        off += size
    return out, flat[off:off + D_MODEL]


def kernel(x, c, positions, w_ada, b_ada, w_in, b_fgate, gn_a, gn_b, w_out, ln1_g, ln1_b, w_up, conv_w, conv_b, w_down, ln2_g, ln2_b, loss_target, m_w_ada, m_b_ada, m_w_in, m_b_fgate, m_gn_a, m_gn_b, m_w_out, m_ln1_g, m_ln1_b, m_w_up, m_conv_w, m_conv_b, m_w_down, m_ln2_g, m_ln2_b, v_w_ada, v_b_ada, v_w_in, v_b_fgate, v_gn_a, v_gn_b, v_w_out, v_ln1_g, v_ln1_b, v_w_up, v_conv_w, v_conv_b, v_w_down, v_ln2_g, v_ln2_b):
    w = dict(w_ada=w_ada[0], b_ada=b_ada, w_in=w_in[0], b_fgate=b_fgate, gn_a=gn_a, gn_b=gn_b, w_out=w_out[0],
             ln1_g=ln1_g, ln1_b=ln1_b, w_up=w_up[0], conv_w=conv_w[0], conv_b=conv_b, w_down=w_down[0], ln2_g=ln2_g,
             ln2_b=ln2_b)
    m = dict(w_ada=m_w_ada[0], b_ada=m_b_ada, w_in=m_w_in[0], b_fgate=m_b_fgate, gn_a=m_gn_a, gn_b=m_gn_b,
             w_out=m_w_out[0], ln1_g=m_ln1_g, ln1_b=m_ln1_b, w_up=m_w_up[0], conv_w=m_conv_w[0], conv_b=m_conv_b,
             w_down=m_w_down[0], ln2_g=m_ln2_g, ln2_b=m_ln2_b)
    v = dict(w_ada=v_w_ada[0], b_ada=v_b_ada, w_in=v_w_in[0], b_fgate=v_b_fgate, gn_a=v_gn_a, gn_b=v_gn_b,
             w_out=v_w_out[0], ln1_g=v_ln1_g, ln1_b=v_ln1_b, w_up=v_w_up[0], conv_w=v_conv_w[0], conv_b=v_conv_b,
             w_down=v_w_down[0], ln2_g=v_ln2_g, ln2_b=v_ln2_b)

    shards = [w[n] if n == "conv_w" else w[n].astype(BF16) for n in BIG]
    gathered = _weight_gather(shards)
    full = {n: _full_from_gathered(n, g) for n, g in zip(BIG, gathered)}

    loss_lanes, grad_x, g_local = _local_step(
        x, c, positions, loss_target, full["w_ada"], b_ada, full["w_in"], b_fgate, gn_a, gn_b, full["w_out"], ln1_g,
        ln1_b, full["w_up"], full["conv_w"], conv_b, full["w_down"], ln2_g, ln2_b)

    small_local = _pack_small(g_local, loss_lanes)
    exchanged = _grad_exchange([_dest_major(n, g_local[n]) for n in BIG], small_local)

    grad, delta, new_m, new_v = {}, {}, {}, {}
    for n, parts in zip(BIG, exchanged[:-1]):
        grad[n], delta[n], new_m[n], new_v[n] = (
            a[None] for a in _adamw(parts, w[n], m[n], v[n], ADAM_ROWS[n], "adamw_" + n))
    packed = _adamw(exchanged[-1], _pack_small(w), _pack_small(m), _pack_small(v), SMALL_ROWS, "adamw_small")
    for dst, pk in zip((grad, delta, new_m, new_v), packed):
        vals, lanes = _unpack_small(pk, w)
        dst.update(vals)
        if dst is grad:
            loss = jnp.sum(lanes)

    order = ("w_ada", "b_ada", "w_in", "b_fgate", "gn_a", "gn_b", "w_out", "ln1_g", "ln1_b", "w_up", "conv_w", "conv_b",
             "w_down", "ln2_g", "ln2_b")
    return (loss, grad_x, *[grad[n] for n in order], *[delta[n] for n in order], *[new_m[n] for n in order],
            *[new_v[n] for n in order])
```

```python
import functools

import numpy as np
import jax
import jax.numpy as jnp
from jax import lax
from jax.experimental import pallas as pl
from jax.experimental.pallas import tpu as pltpu

F32, BF16 = jnp.float32, jnp.bfloat16
HIGHEST = lax.Precision.HIGHEST
MESH = pl.DeviceIdType.MESH
ANY = pl.BlockSpec(memory_space=pl.ANY)

D_MODEL = 1024
N_HEADS = 8
HEAD_DIM = 64
WIDTH = 512
D_FF = 2816
N_DEV = 8
ROPE_DIMS = 16
ROPE_THETA = 500000.0
ALPHA = 2.0 ** 0.25
LN_EPS = 1e-5
RMS_EPS = 1e-6
NEG = -1e30
Q_SCALE = 0.125
BLK = 128
LANES = 128
VMEM_LIMIT_BYTES = 56 * 1024 * 1024

ADAM_LR, ADAM_B1, ADAM_B2, ADAM_EPS, ADAM_WD, ADAM_STEP = 0.001, 0.9, 0.999, 1e-08, 0.01, 10


def _params(vmem=VMEM_LIMIT_BYTES):
    return pltpu.CompilerParams(vmem_limit_bytes=vmem)


def _nn(a, b):
    return jnp.dot(a, b, preferred_element_type=F32)


def _nt(a, b):
    return lax.dot_general(a, b, (((1,), (1,)), ((), ())), preferred_element_type=F32)


def _tn(a, b):
    return lax.dot_general(a, b, (((0,), (0,)), ((), ())), preferred_element_type=F32)


def _head_mats():
    r = lax.broadcasted_iota(jnp.int32, (LANES, WIDTH), 0)
    c = lax.broadcasted_iota(jnp.int32, (LANES, WIDTH), 1)
    e = ((c >> 6) == r).astype(F32)
    r2 = lax.broadcasted_iota(jnp.int32, (WIDTH, LANES), 0)
    c2 = lax.broadcasted_iota(jnp.int32, (WIDTH, LANES), 1)
    et = ((r2 >> 6) == c2).astype(F32)
    return e, et


def _hexp(w, e):
    return jnp.dot(w, e, precision=HIGHEST, preferred_element_type=F32)


def _hsum(x, et):
    return jnp.dot(x, et, precision=HIGHEST, preferred_element_type=F32)


def _rope_tabs(pos_ref, fr_ref, sign):
    ang = pos_ref[...].astype(F32) * fr_ref[...]
    lane = lax.broadcasted_iota(jnp.int32, ang.shape, 1) & (HEAD_DIM - 1)
    m1 = lane < ROPE_DIMS // 2
    m2 = (lane >= ROPE_DIMS // 2) & (lane < ROPE_DIMS)
    cos = jnp.cos(ang)
    sin = jnp.sin(ang) * sign
    return (jnp.where(m1 | m2, cos, 1.0), jnp.where(m1, -sin, 0.0), jnp.where(m2, sin, 0.0))


def _rope(z, tabs):
    c, s1, s2 = tabs
    parts = []
    for p in range(z.shape[1] // LANES):
        zp = z[:, LANES * p:LANES * (p + 1)]
        parts.append(zp * c + pltpu.roll(zp, LANES - 8, 1) * s1 + pltpu.roll(zp, 8, 1) * s2)
    return jnp.concatenate(parts, axis=1)


def _half_masks(rows):
    lane = lax.broadcasted_iota(jnp.int32, (rows, LANES), 1)
    lo = lane < HEAD_DIM
    return lo, jnp.logical_not(lo)


def _layer_norm_bwd(dxh, xh, rstd):
    m1 = jnp.mean(dxh, axis=1, keepdims=True)
    m2 = jnp.mean(dxh * xh, axis=1, keepdims=True)
    return rstd * (dxh - m1 - xh * m2)


def _coords():
    return lax.axis_index("x"), lax.axis_index("y"), lax.axis_index("c")


def _peer(x, y, c, k):
    return (1 - x if k & 4 else x, 1 - y if k & 2 else y, 1 - c if k & 1 else c)


def _weight_gather(shards):
    n = len(shards)

    def body(*refs):
        ins, outs = refs[:n], refs[n:2 * n]
        send_sems, recv_sems, local_sems = refs[2 * n:]
        x, y, c = _coords()
        me = 4 * x + 2 * y + c
        local = [pltpu.make_async_copy(ins[t], outs[t].at[me], local_sems.at[t]) for t in range(n)]
        for cp in local:
            cp.start()
        copies = []
        for k in range(1, N_DEV):
            for t in range(n):
                cp = pltpu.make_async_remote_copy(
                    src_ref=ins[t], dst_ref=outs[t].at[me], send_sem=send_sems.at[k - 1, t],
                    recv_sem=recv_sems.at[k - 1, t], device_id=_peer(x, y, c, k), device_id_type=MESH)
                cp.start()
                copies.append(cp)
        for cp in copies:
            cp.wait()
        for cp in local:
            cp.wait()

    return pl.pallas_call(
        body, name="weight_gather",
        out_shape=[jax.ShapeDtypeStruct((N_DEV,) + s.shape, s.dtype) for s in shards],
        in_specs=[ANY] * n, out_specs=[ANY] * n,
        scratch_shapes=[pltpu.SemaphoreType.DMA((N_DEV - 1, n)), pltpu.SemaphoreType.DMA((N_DEV - 1, n)),
                        pltpu.SemaphoreType.DMA((n,))],
    )(*shards)


def _grad_exchange(grads, small):
    n = len(grads)

    def body(*refs):
        ins, small_ref = refs[:n], refs[n]
        outs, small_out = refs[n + 1:2 * n + 1], refs[2 * n + 1]
        send_sems, recv_sems, local_sems = refs[2 * n + 2:]
        x, y, c = _coords()
        me = 4 * x + 2 * y + c
        local = [pltpu.make_async_copy(ins[t].at[me], outs[t].at[me], local_sems.at[t]) for t in range(n)]
        local.append(pltpu.make_async_copy(small_ref, small_out.at[me], local_sems.at[n]))
        for cp in local:
            cp.start()
        copies = []
        for k in range(1, N_DEV):
            px, py, pc = _peer(x, y, c, k)
            dest = 4 * px + 2 * py + pc
            for t in range(n + 1):
                src = small_ref if t == n else ins[t].at[dest]
                dst = small_out.at[me] if t == n else outs[t].at[me]
                cp = pltpu.make_async_remote_copy(
                    src_ref=src, dst_ref=dst, send_sem=send_sems.at[k - 1, t], recv_sem=recv_sems.at[k - 1, t],
                    device_id=(px, py, pc), device_id_type=MESH)
                cp.start()
                copies.append(cp)
        for cp in copies:
            cp.wait()
        for cp in local:
            cp.wait()

    shapes = [jax.ShapeDtypeStruct(g.shape, g.dtype) for g in grads]
    shapes.append(jax.ShapeDtypeStruct((N_DEV,) + small.shape, small.dtype))
    return pl.pallas_call(
        body, name="grad_exchange", out_shape=shapes,
        in_specs=[ANY] * (n + 1), out_specs=[ANY] * (n + 1),
        scratch_shapes=[pltpu.SemaphoreType.DMA((N_DEV - 1, n + 1)), pltpu.SemaphoreType.DMA((N_DEV - 1, n + 1)),
                        pltpu.SemaphoreType.DMA((n + 1,))],
    )(*grads, small)


def _adamw(parts, w, m, v, rows, name):
    _, r_all, cols = parts.shape
    c1 = 1.0 - ADAM_B1 ** ADAM_STEP
    c2 = 1.0 - ADAM_B2 ** ADAM_STEP

    def body(p_ref, w_ref, m_ref, v_ref, g_ref, d_ref, mo_ref, vo_ref):
        g = p_ref[0]
        for s in range(1, N_DEV):
            g = g + p_ref[s]
        mn = ADAM_B1 * m_ref[...] + (1.0 - ADAM_B1) * g
        vn = ADAM_B2 * v_ref[...] + (1.0 - ADAM_B2) * (g * g)
        m_hat = mn / c1
        v_hat = vn / c2
        g_ref[...] = g
        d_ref[...] = -ADAM_LR * (m_hat / (jnp.sqrt(v_hat) + ADAM_EPS) + ADAM_WD * w_ref[...])
        mo_ref[...] = mn
        vo_ref[...] = vn

    spec = pl.BlockSpec((rows, cols), lambda i: (i, 0))
    return pl.pallas_call(
        body, name=name, grid=(r_all // rows,),
        in_specs=[pl.BlockSpec((N_DEV, rows, cols), lambda i: (0, i, 0)), spec, spec, spec],
        out_specs=[spec] * 4, out_shape=[jax.ShapeDtypeStruct((r_all, cols), F32)] * 4,
        compiler_params=_params(),
    )(parts, w, m, v)


def _matmul(a, w, transposed_w, out_dtype, tm, chunk, name):
    t_all, k = a.shape
    n = w.shape[0] if transposed_w else w.shape[1]

    def body(a_ref, w_ref, o_ref):
        av = a_ref[...]
        for j in range(n // chunk):
            cs = slice(j * chunk, (j + 1) * chunk)
            r = _nt(av, w_ref[cs, :]) if transposed_w else _nn(av, w_ref[:, cs])
            o_ref[:, cs] = r.astype(out_dtype)

    return pl.pallas_call(
        body, name=name, grid=(t_all // tm,),
        in_specs=[pl.BlockSpec((tm, k), lambda i: (i, 0)), pl.BlockSpec(w.shape, lambda i: (0, 0))],
        out_specs=pl.BlockSpec((tm, n), lambda i: (i, 0)),
        out_shape=jax.ShapeDtypeStruct((t_all, n), out_dtype), compiler_params=_params(),
    )(a, w)


def _matmul_tn(a, b, tn, tk, name):
    t_all, k1 = a.shape
    n = b.shape[1]

    def body(a_ref, b_ref, o_ref):
        @pl.when(pl.program_id(1) == 0)
        def _():
            o_ref[...] = jnp.zeros_like(o_ref)
        o_ref[...] += _tn(a_ref[...], b_ref[...])

    return pl.pallas_call(
        body, name=name, grid=(n // tn, t_all // tk),
        in_specs=[pl.BlockSpec((tk, k1), lambda j, t: (t, 0)), pl.BlockSpec((tk, tn), lambda j, t: (t, j))],
        out_specs=pl.BlockSpec((k1, tn), lambda j, t: (0, j)),
        out_shape=jax.ShapeDtypeStruct((k1, n), F32), compiler_params=_params(),
    )(a, b)


def _matmul_rows(a, b, tk, name):
    r, t_all = a.shape
    n = b.shape[1]

    def body(a_ref, b_ref, o_ref):
        @pl.when(pl.program_id(0) == 0)
        def _():
            o_ref[...] = jnp.zeros_like(o_ref)
        o_ref[...] += _nn(a_ref[...], b_ref[...])

    return pl.pallas_call(
        body, name=name, grid=(t_all // tk,),
        in_specs=[pl.BlockSpec((r, tk), lambda t: (0, t)), pl.BlockSpec((tk, n), lambda t: (t, 0))],
        out_specs=pl.BlockSpec((r, n), lambda t: (0, 0)),
        out_shape=jax.ShapeDtypeStruct((r, n), F32), compiler_params=_params(),
    )(a, b)


def _ada_fwd(c16, w_ada, b_ada):
    n = w_ada.shape[1]
    tn = n // N_DEV

    def body(c_ref, w_ref, b_ref, o_ref):
        cv = c_ref[...]
        s = (cv * jax.nn.sigmoid(cv)).astype(BF16)
        o_ref[...] = _nn(s, w_ref[...]) + b_ref[...]

    return pl.pallas_call(
        body, name="ada_fwd", grid=(N_DEV,),
        in_specs=[pl.BlockSpec(c16.shape, lambda j: (0, 0)), pl.BlockSpec((D_MODEL, tn), lambda j: (0, j)),
                  pl.BlockSpec((1, tn), lambda j: (0, j))],
        out_specs=pl.BlockSpec((c16.shape[0], tn), lambda j: (0, j)),
        out_shape=jax.ShapeDtypeStruct((c16.shape[0], n), F32), compiler_params=_params(),
    )(c16, w_ada, b_ada)


def _ada_bwd(c16, dada16):
    n = dada16.shape[1]
    tn = n // N_DEV

    def body(c_ref, d_ref, o_ref):
        cv = c_ref[...]
        s = (cv * jax.nn.sigmoid(cv)).astype(BF16)
        o_ref[...] = _tn(s, d_ref[...].astype(BF16))

    return pl.pallas_call(
        body, name="ada_bwd", grid=(N_DEV,),
        in_specs=[pl.BlockSpec(c16.shape, lambda j: (0, 0)), pl.BlockSpec((c16.shape[0], tn), lambda j: (0, j))],
        out_specs=pl.BlockSpec((D_MODEL, tn), lambda j: (0, j)),
        out_shape=jax.ShapeDtypeStruct((D_MODEL, n), F32), compiler_params=_params(),
    )(c16, dada16)


def _inproj(x, ada3, pos, wqkv, wf16, freq, seq):
    t_all = x.shape[0]
    tm = 256
    nts = seq // tm

    def body(x_ref, ada_ref, pos_ref, w_ref, wf_ref, fr_ref, h1_ref, za_ref, zb_ref, fa_ref):
        h1 = (x_ref[...] * (1.0 + ada_ref[0, 1:2, :]) + ada_ref[0, 0:1, :]).astype(BF16)
        h1_ref[...] = h1
        tabs = _rope_tabs(pos_ref, fr_ref, 1.0)
        for n in range(6):
            z = _nn(h1, w_ref[:, n * WIDTH:(n + 1) * WIDTH])
            if n in (3, 4):
                z = _rope(z, tabs)
            if n in (0, 3):
                z = z * Q_SCALE
            dst = za_ref if n < 3 else zb_ref
            dst[:, (n % 3) * WIDTH:(n % 3 + 1) * WIDTH] = z.astype(BF16)
        fa_ref[...] = _nt(wf_ref[...], h1)[:N_HEADS]

    tok = lambda w: pl.BlockSpec((tm, w), lambda i: (i, 0))
    return pl.pallas_call(
        body, name="inproj", grid=(t_all // tm,),
        in_specs=[tok(D_MODEL), pl.BlockSpec((1, 6, D_MODEL), lambda i: (i // nts, 0, 0)), tok(1),
                  pl.BlockSpec(wqkv.shape, lambda i: (0, 0)), pl.BlockSpec(wf16.shape, lambda i: (0, 0)),
                  pl.BlockSpec((1, LANES), lambda i: (0, 0))],
        out_specs=[tok(D_MODEL), tok(3 * WIDTH), tok(3 * WIDTH), pl.BlockSpec((N_HEADS, tm), lambda i: (0, i))],
        out_shape=[jax.ShapeDtypeStruct((t_all, D_MODEL), BF16), jax.ShapeDtypeStruct((t_all, 3 * WIDTH), BF16),
                   jax.ShapeDtypeStruct((t_all, 3 * WIDTH), BF16), jax.ShapeDtypeStruct((N_HEADS, t_all), F32)],
        compiler_params=_params(),
    )(x, ada3, pos, wqkv, wf16, freq)


def _fgate_fwd(fa_t, bf, seq):
    t_all = fa_t.shape[1]

    def body(fa_ref, b_ref, f_ref):
        lane = lax.broadcasted_iota(jnp.int32, (N_HEADS, LANES), 1)

        def chunk(j, carry):
            sl = pl.ds(pl.multiple_of(j * LANES, LANES), LANES)
            xv = fa_ref[:, sl] + b_ref[...]
            lf = jnp.minimum(xv, 0.0) - jnp.log(1.0 + jnp.exp(-jnp.abs(xv)))
            for s in (1, 2, 4, 8, 16, 32, 64):
                lf = lf + jnp.where(lane >= s, pltpu.roll(lf, s, 1), 0.0)
            lf = lf + carry
            f_ref[:, sl] = lf
            return lf[:, LANES - 1:LANES]

        lax.fori_loop(0, seq // LANES, chunk, jnp.zeros((N_HEADS, 1), F32))

    return pl.pallas_call(
        body, name="fgate_fwd", grid=(t_all // seq,),
        in_specs=[pl.BlockSpec((N_HEADS, seq), lambda b: (0, b)), pl.BlockSpec((N_HEADS, 1), lambda b: (0, 0))],
        out_specs=pl.BlockSpec((N_HEADS, seq), lambda b: (0, b)),
        out_shape=jax.ShapeDtypeStruct((N_HEADS, t_all), F32), compiler_params=_params(),
    )(fa_t, bf)


def _fgate_bwd(df_t, fa_t, bf, seq):
    t_all = fa_t.shape[1]

    def body(df_ref, fa_ref, b_ref, o_ref, s_ref):
        lane = lax.broadcasted_iota(jnp.int32, (N_HEADS, LANES), 1)

        @pl.when(pl.program_id(0) == 0)
        def _():
            s_ref[...] = jnp.zeros_like(s_ref)

        def chunk(jj, carry):
            car, tot = carry
            j = seq // LANES - 1 - jj
            sl = pl.ds(pl.multiple_of(j * LANES, LANES), LANES)
            d = df_ref[:, sl]
            for s in (1, 2, 4, 8, 16, 32, 64):
                d = d + jnp.where(lane < LANES - s, pltpu.roll(d, LANES - s, 1), 0.0)
            d = d + car
            dfa = d * jax.nn.sigmoid(-(fa_ref[:, sl] + b_ref[...]))
            o_ref[:, sl] = dfa
            return d[:, 0:1], tot + jnp.sum(dfa, axis=1, keepdims=True)

        z = jnp.zeros((N_HEADS, 1), F32)
        _, tot = lax.fori_loop(0, seq // LANES, chunk, (z, z))
        s_ref[...] += jnp.broadcast_to(tot, (N_HEADS, LANES))

    row = pl.BlockSpec((N_HEADS, seq), lambda b: (0, b))
    return pl.pallas_call(
        body, name="fgate_bwd", grid=(t_all // seq,),
        in_specs=[row, row, pl.BlockSpec((N_HEADS, 1), lambda b: (0, 0))],
        out_specs=[row, pl.BlockSpec((N_HEADS, LANES), lambda b: (0, 0))],
        out_shape=[jax.ShapeDtypeStruct((N_HEADS, t_all), F32), jax.ShapeDtypeStruct((N_HEADS, LANES), F32)],
        compiler_params=_params(),
    )(df_t, fa_t, bf)


FOX_T = 256


def _fox_prep(dst, src_ref, lo, hi):
    for p in range(4):
        v = src_ref[:, LANES * p:LANES * (p + 1)]
        dst[2 * p] = jnp.where(lo, v, jnp.zeros_like(v))
        dst[2 * p + 1] = jnp.where(hi, v, jnp.zeros_like(v))


def _fox_fwd(za, vt, f_col, seq):
    t_all = za.shape[0]
    tq = FOX_T
    nq = seq // tq

    def body(q_ref, k_ref, vt_ref, fc_ref, o_ref, lse_ref, qm_sc, m_sc, l_sc, acc_sc):
        i = pl.program_id(1)
        lo, hi = _half_masks(tq)
        r = lax.broadcasted_iota(jnp.int32, (tq, tq), 0)
        c = lax.broadcasted_iota(jnp.int32, (tq, tq), 1)
        tri = c >= r
        _fox_prep(qm_sc, q_ref, lo, hi)
        m_sc[...] = jnp.full(m_sc.shape, NEG, F32)
        l_sc[...] = jnp.zeros_like(l_sc)
        acc_sc[...] = jnp.zeros_like(acc_sc)

        def block(j, masked):
            sl = pl.ds(pl.multiple_of(j * tq, tq), tq)
            for p in range(4):
                kj = k_ref[sl, LANES * p:LANES * (p + 1)]
                for h in (2 * p, 2 * p + 1):
                    st = _nt(kj, qm_sc[h]) - fc_ref[sl, h:h + 1]
                    if masked:
                        st = jnp.where(tri, st, NEG)
                    m = m_sc[h:h + 1, :]
                    mn = jnp.maximum(m, jnp.max(st, axis=0, keepdims=True))
                    a = jnp.exp(m - mn)
                    pe = jnp.exp(st - mn)
                    m_sc[h:h + 1, :] = mn
                    l_sc[h:h + 1, :] = a * l_sc[h:h + 1, :] + jnp.sum(pe, axis=0, keepdims=True)
                    acc_sc[h] = a * acc_sc[h] + _nn(vt_ref[HEAD_DIM * h:HEAD_DIM * (h + 1), sl], pe.astype(BF16))

        def step(j, carry):
            block(j, False)
            return carry

        lax.fori_loop(0, i, step, 0)
        block(i, True)
        lse_ref[...] = m_sc[...] + jnp.log(l_sc[...])
        for p in range(4):
            ot = jnp.concatenate([acc_sc[h] / l_sc[h:h + 1, :] for h in (2 * p, 2 * p + 1)], axis=0)
            o_ref[:, LANES * p:LANES * (p + 1)] = ot.T

    return pl.pallas_call(
        body, name="fox_fwd", grid=(t_all // seq, nq),
        in_specs=[pl.BlockSpec((tq, WIDTH), lambda b, i: (b * nq + i, 0)),
                  pl.BlockSpec((seq, WIDTH), lambda b, i: (b, 1)), pl.BlockSpec((WIDTH, seq), lambda b, i: (b, 0)),
                  pl.BlockSpec((seq, LANES), lambda b, i: (b, 0))],
        out_specs=[pl.BlockSpec((tq, WIDTH), lambda b, i: (b * nq + i, 0)),
                   pl.BlockSpec((N_HEADS, tq), lambda b, i: (0, b * nq + i))],
        out_shape=[jax.ShapeDtypeStruct((t_all, WIDTH), F32), jax.ShapeDtypeStruct((N_HEADS, t_all), F32)],
        scratch_shapes=[pltpu.VMEM((N_HEADS, tq, LANES), BF16), pltpu.VMEM((N_HEADS, tq), F32),
                        pltpu.VMEM((N_HEADS, tq), F32), pltpu.VMEM((N_HEADS, HEAD_DIM, tq), F32)],
        compiler_params=_params(),
    )(za, za, vt, f_col)


def _fox_bwd_dq(za, do, f_row, lse, dl, seq):
    t_all = za.shape[0]
    tq = FOX_T
    nq = seq // tq

    def body(q_ref, k_ref, v_ref, do_ref, fr_ref, lse_ref, dl_ref, dq_ref, df_ref, qm_sc, dm_sc, nl_sc, dd_sc,
             acc_sc, rs_sc):
        i = pl.program_id(1)
        lo, hi = _half_masks(tq)
        r = lax.broadcasted_iota(jnp.int32, (tq, tq), 0)
        c = lax.broadcasted_iota(jnp.int32, (tq, tq), 1)
        tri = c <= r
        _fox_prep(qm_sc, q_ref, lo, hi)
        _fox_prep(dm_sc, do_ref, lo, hi)
        for h in range(N_HEADS):
            nl_sc[h] = jnp.broadcast_to(lse_ref[:, h:h + 1], (tq, tq))
            dd_sc[h] = jnp.broadcast_to(dl_ref[:, h:h + 1], (tq, tq))
        acc_sc[...] = jnp.zeros_like(acc_sc)
        rs_sc[...] = jnp.zeros_like(rs_sc)

        def block(j, masked):
            sl = pl.ds(pl.multiple_of(j * tq, tq), tq)
            for p in range(4):
                cs = slice(LANES * p, LANES * (p + 1))
                kj = k_ref[sl, cs]
                vj = v_ref[sl, cs]
                for h in (2 * p, 2 * p + 1):
                    s = _nt(qm_sc[h], kj) - fr_ref[h:h + 1, sl] - nl_sc[h]
                    if masked:
                        s = jnp.where(tri, s, NEG)
                    ds = jnp.exp(s) * (_nt(dm_sc[h], vj) - dd_sc[h])
                    acc_sc[h] += _nn(ds.astype(BF16), kj)
                    rs_sc[h] += ds[:, :LANES] + ds[:, LANES:]

        def step(j, carry):
            block(j, False)
            return carry

        lax.fori_loop(0, i, step, 0)
        block(i, True)
        df_ref[...] = jnp.zeros_like(df_ref)
        for p in range(4):
            dq_ref[:, LANES * p:LANES * (p + 1)] = (
                jnp.where(lo, acc_sc[2 * p], acc_sc[2 * p + 1]) * Q_SCALE).astype(BF16)
            for h in (2 * p, 2 * p + 1):
                df_ref[:, h:h + 1] = jnp.sum(rs_sc[h], axis=1, keepdims=True)

    tile = lambda w: pl.BlockSpec((tq, w), lambda b, i: (b * nq + i, 0))
    return pl.pallas_call(
        body, name="fox_bwd_dq", grid=(t_all // seq, nq),
        in_specs=[tile(WIDTH), pl.BlockSpec((seq, WIDTH), lambda b, i: (b, 1)),
                  pl.BlockSpec((seq, WIDTH), lambda b, i: (b, 2)), tile(WIDTH),
                  pl.BlockSpec((N_HEADS, seq), lambda b, i: (0, b)), tile(LANES), tile(LANES)],
        out_specs=[tile(WIDTH), tile(LANES)],
        out_shape=[jax.ShapeDtypeStruct((t_all, WIDTH), BF16), jax.ShapeDtypeStruct((t_all, LANES), F32)],
        scratch_shapes=[pltpu.VMEM((N_HEADS, tq, LANES), BF16), pltpu.VMEM((N_HEADS, tq, LANES), BF16),
                        pltpu.VMEM((N_HEADS, tq, tq), F32), pltpu.VMEM((N_HEADS, tq, tq), F32),
                        pltpu.VMEM((N_HEADS, tq, LANES), F32), pltpu.VMEM((N_HEADS, tq, LANES), F32)],
        compiler_params=_params(),
    )(za, za, za, do, f_row, lse, dl)


def _fox_bwd_dkv(za, do, f_col, lse_row, dl_row, seq):
    t_all = za.shape[0]
    tk = FOX_T
    nk = seq // tk

    def body(k_ref, v_ref, q_ref, do_ref, fc_ref, lr_ref, dr_ref, dk_ref, dv_ref, df_ref, km_sc, vm_sc, fk_sc,
             dk_sc, dv_sc, cs_sc):
        j = pl.program_id(1)
        lo, hi = _half_masks(tk)
        r = lax.broadcasted_iota(jnp.int32, (tk, tk), 0)
        c = lax.broadcasted_iota(jnp.int32, (tk, tk), 1)
        tri = c >= r
        _fox_prep(km_sc, k_ref, lo, hi)
        _fox_prep(vm_sc, v_ref, lo, hi)
        for h in range(N_HEADS):
            fk_sc[h] = jnp.broadcast_to(fc_ref[:, h:h + 1], (tk, tk))
        dk_sc[...] = jnp.zeros_like(dk_sc)
        dv_sc[...] = jnp.zeros_like(dv_sc)
        cs_sc[...] = jnp.zeros_like(cs_sc)

        def block(i, masked):
            sl = pl.ds(pl.multiple_of(i * tk, tk), tk)
            for p in range(4):
                cs = slice(LANES * p, LANES * (p + 1))
                qi = q_ref[sl, cs]
                doi = do_ref[sl, cs]
                for h in (2 * p, 2 * p + 1):
                    st = _nt(km_sc[h], qi) - fk_sc[h] - lr_ref[h:h + 1, sl]
                    if masked:
                        st = jnp.where(tri, st, NEG)
                    pt = jnp.exp(st)
                    dst = pt * (_nt(vm_sc[h], doi) - dr_ref[h:h + 1, sl])
                    dv_sc[h] += _nn(pt.astype(BF16), doi)
                    dk_sc[h] += _nn(dst.astype(BF16), qi)
                    cs_sc[h] += dst[:, :LANES] + dst[:, LANES:]

        def step(i, carry):
            block(i, False)
            return carry

        block(j, True)
        lax.fori_loop(j + 1, nk, step, 0)
        df_ref[...] = jnp.zeros_like(df_ref)
        for p in range(4):
            cs = slice(LANES * p, LANES * (p + 1))
            dk_ref[:, cs] = jnp.where(lo, dk_sc[2 * p], dk_sc[2 * p + 1]).astype(BF16)
            dv_ref[:, cs] = jnp.where(lo, dv_sc[2 * p], dv_sc[2 * p + 1]).astype(BF16)
            for h in (2 * p, 2 * p + 1):
                df_ref[:, h:h + 1] = -jnp.sum(cs_sc[h], axis=1, keepdims=True)

    tile = lambda w, col: pl.BlockSpec((tk, w), lambda b, j: (b * nk + j, col))
    full = lambda col: pl.BlockSpec((seq, WIDTH), lambda b, j: (b, col))
    row = pl.BlockSpec((N_HEADS, seq), lambda b, j: (0, b))
    acc = pltpu.VMEM((N_HEADS, tk, LANES), F32)
    return pl.pallas_call(
        body, name="fox_bwd_dkv", grid=(t_all // seq, nk),
        in_specs=[tile(WIDTH, 1), tile(WIDTH, 2), full(0), full(0), tile(LANES, 0), row, row],
        out_specs=[tile(WIDTH, 0), tile(WIDTH, 0), tile(LANES, 0)],
        out_shape=[jax.ShapeDtypeStruct((t_all, WIDTH), BF16), jax.ShapeDtypeStruct((t_all, WIDTH), BF16),
                   jax.ShapeDtypeStruct((t_all, LANES), F32)],
        scratch_shapes=[pltpu.VMEM((N_HEADS, tk, LANES), BF16), pltpu.VMEM((N_HEADS, tk, LANES), BF16),
                        pltpu.VMEM((N_HEADS, tk, tk), F32), acc, acc, acc],
        compiler_params=_params(),
    )(za, za, za, do, f_col, lse_row, dl_row)


def _dil_mask(has_prev):
    qi = lax.broadcasted_iota(jnp.int32, (BLK, 2 * BLK), 0)
    kj = lax.broadcasted_iota(jnp.int32, (BLK, 2 * BLK), 1)
    dist = qi + BLK - kj
    return (dist >= 0) & (dist <= BLK) & ((kj >= BLK) | has_prev)


def _dil_specs(t_all):
    nb = t_all // BLK
    cur = pl.BlockSpec((1, BLK, WIDTH), lambda p, n: (p, n, 0))
    prev = pl.BlockSpec((1, BLK, WIDTH), lambda p, n: (p, jnp.maximum(n - 1, 0), 0))
    nxt = pl.BlockSpec((1, BLK, WIDTH), lambda p, n: (p, jnp.minimum(n + 1, nb - 1), 0))
    stat = pl.BlockSpec((1, BLK, LANES), lambda p, n: (p, n, 0))
    return nb, cur, prev, nxt, stat


def _dil_fwd(qs, ks, vs, seq):
    t_all = qs.shape[1]
    nb, cur, prev, _, stat = _dil_specs(t_all)

    def body(q_ref, kp_ref, kc_ref, vp_ref, vc_ref, o_ref, lse_ref):
        nbs = (seq // BLK) >> (2 * pl.program_id(0))
        mask = _dil_mask((pl.program_id(1) & (nbs - 1)) != 0)
        lo, hi = _half_masks(BLK)
        lse_ref[...] = jnp.zeros_like(lse_ref)
        for p in range(4):
            cs = slice(LANES * p, LANES * (p + 1))
            qp = q_ref[0, :, cs]
            kcat = jnp.concatenate([kp_ref[0, :, cs], kc_ref[0, :, cs]], axis=0)
            vcat = jnp.concatenate([vp_ref[0, :, cs], vc_ref[0, :, cs]], axis=0)
            res = []
            for e in (0, 1):
                h = 2 * p + e
                qe = jnp.where(lo if e == 0 else hi, qp, jnp.zeros_like(qp))
                s = jnp.where(mask, _nt(qe, kcat), NEG)
                m = jnp.max(s, axis=1, keepdims=True)
                pe = jnp.exp(s - m)
                l = jnp.sum(pe, axis=1, keepdims=True)
                res.append(_nn(pe.astype(BF16), vcat) / l)
                lse_ref[0, :, h:h + 1] = m + jnp.log(l)
            o_ref[0, :, cs] = jnp.where(lo, res[0], res[1])

    return pl.pallas_call(
        body, name="dil_fwd", grid=(3, nb), in_specs=[cur, prev, cur, prev, cur], out_specs=[cur, stat],
        out_shape=[jax.ShapeDtypeStruct((3, t_all, WIDTH), F32), jax.ShapeDtypeStruct((3, t_all, LANES), F32)],
        compiler_params=_params(),
    )(qs, ks, ks, vs, vs)


def _dil_bwd_dq(qs, ks, vs, dos, lses, dls, seq):
    t_all = qs.shape[1]
    nb, cur, prev, _, stat = _dil_specs(t_all)

    def body(q_ref, kp_ref, kc_ref, vp_ref, vc_ref, do_ref, lse_ref, dl_ref, dq_ref):
        nbs = (seq // BLK) >> (2 * pl.program_id(0))
        mask = _dil_mask((pl.program_id(1) & (nbs - 1)) != 0)
        lo, hi = _half_masks(BLK)
        for p in range(4):
            cs = slice(LANES * p, LANES * (p + 1))
            qp = q_ref[0, :, cs]
            dop = do_ref[0, :, cs]
            kcat = jnp.concatenate([kp_ref[0, :, cs], kc_ref[0, :, cs]], axis=0)
            vcat = jnp.concatenate([vp_ref[0, :, cs], vc_ref[0, :, cs]], axis=0)
            res = []
            for e in (0, 1):
                h = 2 * p + e
                sel = lo if e == 0 else hi
                qe = jnp.where(sel, qp, jnp.zeros_like(qp))
                doe = jnp.where(sel, dop, jnp.zeros_like(dop))
                s = jnp.where(mask, _nt(qe, kcat) - lse_ref[0, :, h:h + 1], NEG)
                ds = jnp.exp(s) * (_nt(doe, vcat) - dl_ref[0, :, h:h + 1])
                res.append(_nn(ds.astype(BF16), kcat))
            dq_ref[0, :, cs] = jnp.where(lo, res[0], res[1]) * Q_SCALE

    return pl.pallas_call(
        body, name="dil_bwd_dq", grid=(3, nb), in_specs=[cur, prev, cur, prev, cur, cur, stat, stat], out_specs=cur,
        out_shape=jax.ShapeDtypeStruct((3, t_all, WIDTH), F32), compiler_params=_params(),
    )(qs, ks, ks, vs, vs, dos, lses, dls)


def _dil_bwd_dkv(qs, ks, vs, dos, lse_rows, dl_rows, seq):
    t_all = qs.shape[1]
    nb, cur, _, nxt, _ = _dil_specs(t_all)
    rcur = pl.BlockSpec((1, N_HEADS, BLK), lambda p, n: (p, 0, n))
    rnxt = pl.BlockSpec((1, N_HEADS, BLK), lambda p, n: (p, 0, jnp.minimum(n + 1, nb - 1)))

    def body(k_ref, v_ref, qc_ref, qn_ref, dc_ref, dn_ref, lc_ref, ln_ref, ec_ref, en_ref, dk_ref, dv_ref):
        nbs = (seq // BLK) >> (2 * pl.program_id(0))
        has_next = ((pl.program_id(1) + 1) & (nbs - 1)) != 0
        r = lax.broadcasted_iota(jnp.int32, (BLK, 2 * BLK), 0)
        c = lax.broadcasted_iota(jnp.int32, (BLK, 2 * BLK), 1)
        mask = ((c < BLK) & (c >= r)) | ((c >= BLK) & (c - BLK <= r) & has_next)
        lo, hi = _half_masks(BLK)
        for p in range(4):
            cs = slice(LANES * p, LANES * (p + 1))
            kp = k_ref[0, :, cs]
            vp = v_ref[0, :, cs]
            qcat = jnp.concatenate([qc_ref[0, :, cs], qn_ref[0, :, cs]], axis=0)
            dcat = jnp.concatenate([dc_ref[0, :, cs], dn_ref[0, :, cs]], axis=0)
            rk, rv = [], []
            for e in (0, 1):
                h = 2 * p + e
                sel = lo if e == 0 else hi
                ke = jnp.where(sel, kp, jnp.zeros_like(kp))
                ve = jnp.where(sel, vp, jnp.zeros_like(vp))
                lrow = jnp.concatenate([lc_ref[0, h:h + 1, :], ln_ref[0, h:h + 1, :]], axis=1)
                erow = jnp.concatenate([ec_ref[0, h:h + 1, :], en_ref[0, h:h + 1, :]], axis=1)
                st = jnp.where(mask, _nt(ke, qcat) - lrow, NEG)
                pt = jnp.exp(st)
                dst = pt * (_nt(ve, dcat) - erow)
                rk.append(_nn(dst.astype(BF16), qcat))
                rv.append(_nn(pt.astype(BF16), dcat))
            dk_ref[0, :, cs] = jnp.where(lo, rk[0], rk[1])
            dv_ref[0, :, cs] = jnp.where(lo, rv[0], rv[1])

    return pl.pallas_call(
        body, name="dil_bwd_dkv", grid=(3, nb),
        in_specs=[cur, cur, cur, nxt, cur, nxt, rcur, rnxt, rcur, rnxt], out_specs=[cur, cur],
        out_shape=[jax.ShapeDtypeStruct((3, t_all, WIDTH), F32)] * 2, compiler_params=_params(),
    )(ks, vs, qs, qs, dos, dos, lse_rows, lse_rows, dl_rows, dl_rows)


def _mix_out(oa, o3, l3, gn_a, gn_b, w_out, x, ada3, ln_g, ln_b, seq):
    t_all = x.shape[0]
    tm = 256
    nts = seq // tm

    def body(oa_ref, o1_ref, o2_ref, o3_ref, l1_ref, l2_ref, l3_ref, ga_ref, gb_ref, w_ref, x_ref, ada_ref, g_ref,
             b_ref, ob_ref, lse_ref, mg_ref, mix_ref, xh_ref, rs_ref, h2_ref):
        e, et = _head_mats()
        la, lb, lc = l1_ref[...], l2_ref[...], l3_ref[...]
        mx = jnp.maximum(jnp.maximum(la, lb), lc)
        ea, eb, ec = jnp.exp(la - mx), jnp.exp(lb - mx), jnp.exp(lc - mx)
        tot = ea + eb + ec
        lse_ref[...] = mx + jnp.log(tot)
        ob = (o1_ref[...] * _hexp(ea / tot, e) + o2_ref[...] * _hexp(eb / tot, e) + o3_ref[...] * _hexp(ec / tot, e))
        ob_ref[...] = ob

        def rms(o, gain):
            rr = lax.rsqrt(_hsum(o * o, et) * (1.0 / HEAD_DIM) + RMS_EPS)
            return o * _hexp(rr, e) * gain

        merged = jnp.concatenate([rms(oa_ref[...], ga_ref[...]), rms(ob, gb_ref[...])], axis=1).astype(BF16)
        mg_ref[...] = merged
        mix = _nn(merged, w_ref[...])
        mix_ref[...] = mix.astype(BF16)
        r1 = ALPHA * x_ref[...] + ada_ref[0, 2:3, :] * mix
        d = r1 - jnp.mean(r1, axis=1, keepdims=True)
        rstd = lax.rsqrt(jnp.mean(d * d, axis=1, keepdims=True) + LN_EPS)
        xh = d * rstd
        xh_ref[...] = xh
        rs_ref[...] = jnp.broadcast_to(rstd, (tm, LANES))
        x1 = xh * g_ref[...] + b_ref[...]
        h2_ref[...] = (x1 * (1.0 + ada_ref[0, 4:5, :]) + ada_ref[0, 3:4, :]).astype(BF16)

    tok = lambda w: pl.BlockSpec((tm, w), lambda i: (i, 0))
    vec = lambda w: pl.BlockSpec((1, w), lambda i: (0, 0))
    return pl.pallas_call(
        body, name="mix_out", grid=(t_all // tm,),
        in_specs=[tok(WIDTH)] * 4 + [tok(LANES)] * 3 + [vec(WIDTH), vec(WIDTH),
                  pl.BlockSpec(w_out.shape, lambda i: (0, 0)), tok(D_MODEL),
                  pl.BlockSpec((1, 6, D_MODEL), lambda i: (i // nts, 0, 0)), vec(D_MODEL), vec(D_MODEL)],
        out_specs=[tok(WIDTH), tok(LANES), tok(D_MODEL), tok(D_MODEL), tok(D_MODEL), tok(LANES), tok(D_MODEL)],
        out_shape=[jax.ShapeDtypeStruct((t_all, WIDTH), F32), jax.ShapeDtypeStruct((t_all, LANES), F32),
                   jax.ShapeDtypeStruct((t_all, D_MODEL), BF16), jax.ShapeDtypeStruct((t_all, D_MODEL), BF16),
                   jax.ShapeDtypeStruct((t_all, D_MODEL), F32), jax.ShapeDtypeStruct((t_all, LANES), F32),
                   jax.ShapeDtypeStruct((t_all, D_MODEL), BF16)],
        compiler_params=_params(),
    )(oa, o3[0], o3[1], o3[2], l3[0], l3[1], l3[2], gn_a, gn_b, w_out, x, ada3, ln_g, ln_b)


def _mix_out_bwd(dmix, w_out, oa, ob, gn_a, gn_b):
    t_all = dmix.shape[0]
    tm = 256

    def body(dm_ref, w_ref, oa_ref, ob_ref, ga_ref, gb_ref, doa_ref, dob_ref, dla_ref, dlb_ref, acc_ref):
        @pl.when(pl.program_id(0) == 0)
        def _():
            acc_ref[...] = jnp.zeros_like(acc_ref)
        e, et = _head_mats()
        dmg = _nt(dm_ref[...], w_ref[...])

        def group(o, dn, gain):
            rr = lax.rsqrt(_hsum(o * o, et) * (1.0 / HEAD_DIM) + RMS_EPS)
            re = _hexp(rr, e)
            dgain = jnp.sum(dn * o * re, axis=0, keepdims=True)
            dxn = dn * gain
            tt = _hsum(dxn * o, et) * (rr * rr * rr) * (1.0 / HEAD_DIM)
            do = re * dxn - o * _hexp(tt, e)
            return do, _hsum(do * o, et), dgain

        doa, dla, dga = group(oa_ref[...], dmg[:, :WIDTH], ga_ref[...])
        dob, dlb, dgb = group(ob_ref[...], dmg[:, WIDTH:], gb_ref[...])
        doa_ref[...] = doa.astype(BF16)
        dob_ref[...] = dob.astype(BF16)
        dla_ref[...] = dla
        dlb_ref[...] = dlb
        acc_ref[0:1, :] += jnp.concatenate([dga, dgb], axis=1)

    tok = lambda w: pl.BlockSpec((tm, w), lambda i: (i, 0))
    vec = lambda w: pl.BlockSpec((1, w), lambda i: (0, 0))
    return pl.pallas_call(
        body, name="mix_out_bwd", grid=(t_all // tm,),
        in_specs=[tok(D_MODEL), pl.BlockSpec(w_out.shape, lambda i: (0, 0)), tok(WIDTH), tok(WIDTH), vec(WIDTH),
                  vec(WIDTH)],
        out_specs=[tok(WIDTH), tok(WIDTH), tok(LANES), tok(LANES), pl.BlockSpec((8, D_MODEL), lambda i: (0, 0))],
        out_shape=[jax.ShapeDtypeStruct((t_all, WIDTH), BF16), jax.ShapeDtypeStruct((t_all, WIDTH), BF16),
                   jax.ShapeDtypeStruct((t_all, LANES), F32), jax.ShapeDtypeStruct((t_all, LANES), F32),
                   jax.ShapeDtypeStruct((8, D_MODEL), F32)],
        compiler_params=_params(),
    )(dmix, w_out, oa, ob, gn_a, gn_b)


def _inproj_bwd(dza, dqb, dkb, dvb, dfa16, pos, wqkv, wf16, freq, dr1, x, ada3, seq):
    t_all = x.shape[0]
    tm = 256
    nts = seq // tm
    nbat = t_all // seq

    def body(dza_ref, dqb_ref, dkb_ref, dvb_ref, dfa_ref, pos_ref, w_ref, wf_ref, fr_ref, dr1_ref, x_ref, ada_ref,
             gx_ref, dz_ref, acc_ref):
        i = pl.program_id(0)

        @pl.when(i == 0)
        def _():
            acc_ref[...] = jnp.zeros_like(acc_ref)
        tabs = _rope_tabs(pos_ref, fr_ref, -1.0)
        dz_ref[:, :3 * WIDTH] = dza_ref[...]
        dz_ref[:, 3 * WIDTH:4 * WIDTH] = _rope(dqb_ref[...], tabs).astype(BF16)
        dz_ref[:, 4 * WIDTH:5 * WIDTH] = _rope(dkb_ref[...], tabs).astype(BF16)
        dz_ref[:, 5 * WIDTH:] = dvb_ref[...].astype(BF16)
        dh1 = _tn(dfa_ref[...], wf_ref[...])
        for n in range(6):
            cs = slice(n * WIDTH, (n + 1) * WIDTH)
            dh1 = dh1 + _nt(dz_ref[:, cs], w_ref[:, cs])
        xv = x_ref[...]
        gx_ref[...] = ALPHA * dr1_ref[...] + dh1 * (1.0 + ada_ref[0, 1:2, :])
        b = i // nts
        acc_ref[pl.ds(b, 1), :] += jnp.sum(dh1 * xv, axis=0, keepdims=True)
        acc_ref[pl.ds(8 + b, 1), :] += jnp.sum(dh1, axis=0, keepdims=True)

    tok = lambda w: pl.BlockSpec((tm, w), lambda i: (i, 0))
    return pl.pallas_call(
        body, name="inproj_bwd", grid=(t_all // tm,),
        in_specs=[tok(3 * WIDTH), tok(WIDTH), tok(WIDTH), tok(WIDTH), pl.BlockSpec((16, tm), lambda i: (0, i)),
                  tok(1), pl.BlockSpec(wqkv.shape, lambda i: (0, 0)), pl.BlockSpec(wf16.shape, lambda i: (0, 0)),
                  pl.BlockSpec((1, LANES), lambda i: (0, 0)), tok(D_MODEL), tok(D_MODEL),
                  pl.BlockSpec((1, 6, D_MODEL), lambda i: (i // nts, 0, 0))],
        out_specs=[tok(D_MODEL), tok(6 * WIDTH), pl.BlockSpec((16, D_MODEL), lambda i: (0, 0))],
        out_shape=[jax.ShapeDtypeStruct((t_all, D_MODEL), F32), jax.ShapeDtypeStruct((t_all, 6 * WIDTH), BF16),
                   jax.ShapeDtypeStruct((16, D_MODEL), F32)],
        compiler_params=_params(),
    )(dza, dqb, dkb, dvb, dfa16, pos, wqkv, wf16, freq, dr1, x, ada3)


FFN_TM = 512
FFN_TN = 256
HALO = 8


def _conv(cat_ref, w_ref, b_ref, rows):
    return (b_ref[...] + w_ref[0:1, :] * cat_ref[pl.ds(HALO - 2, rows), :] + w_ref[1:2, :] * cat_ref[pl.ds(HALO - 1, rows), :]
            + w_ref[2:3, :] * cat_ref[pl.ds(HALO, rows), :])


def _ffn_gate(u, conv_w, conv_b, seq):
    t_all = u.shape[0]
    tm, tn = FFN_TM, FFN_TN
    nc = D_FF // tn
    nts = seq // tm

    def body(ua_ref, uap_ref, ug_ref, ugp_ref, wa_ref, wg_ref, ba_ref, bg_ref, o_ref, ca_ref, cg_ref):
        first = (pl.program_id(0) % nts) == 0
        zero = jnp.zeros((HALO, tn), F32)
        ca_ref[0:HALO, :] = jnp.where(first, zero, uap_ref[...])
        cg_ref[0:HALO, :] = jnp.where(first, zero, ugp_ref[...])
        ca_ref[HALO:, :] = ua_ref[...]
        cg_ref[HALO:, :] = ug_ref[...]
        ya = _conv(ca_ref, wa_ref, ba_ref, tm)
        yg = _conv(cg_ref, wg_ref, bg_ref, tm)
        o_ref[...] = (yg * jax.nn.sigmoid(yg) * ya).astype(BF16)

    cur = lambda off: pl.BlockSpec((tm, tn), lambda t, n: (t, n + off))
    prev = lambda off: pl.BlockSpec((HALO, tn), lambda t, n: (jnp.maximum(t * (tm // HALO) - 1, 0), n + off))
    vec = lambda r, off: pl.BlockSpec((r, tn), lambda t, n: (0, n + off))
    return pl.pallas_call(
        body, name="ffn_gate", grid=(t_all // tm, nc),
        in_specs=[cur(0), prev(0), cur(nc), prev(nc), vec(3, 0), vec(3, nc), vec(1, 0), vec(1, nc)],
        out_specs=pl.BlockSpec((tm, tn), lambda t, n: (t, n)),
        out_shape=jax.ShapeDtypeStruct((t_all, D_FF), BF16),
        scratch_shapes=[pltpu.VMEM((tm + HALO, tn), F32)] * 2, compiler_params=_params(),
    )(u, u, u, u, conv_w, conv_w, conv_b, conv_b)


def _ffn_gate_bwd(u, dfi, conv_w, conv_b, seq):
    t_all = u.shape[0]
    tm, tn = FFN_TM, FFN_TN
    nc = D_FF // tn
    nts = seq // tm
    ext = tm + HALO

    def body(ua_ref, uap_ref, uan_ref, ug_ref, ugp_ref, ugn_ref, df_ref, dfn_ref, wa_ref, wg_ref, ba_ref, bg_ref,
             dua_ref, dug_ref, acca_ref, accg_ref, ca_ref, cg_ref, ya_ref, yg_ref):
        t = pl.program_id(1)
        first = (t % nts) == 0
        last = (t % nts) == nts - 1

        @pl.when(t == 0)
        def _():
            acca_ref[...] = jnp.zeros_like(acca_ref)
            accg_ref[...] = jnp.zeros_like(accg_ref)
        zero = jnp.zeros((HALO, tn), F32)
        for cat, cur, prv, nxt in ((ca_ref, ua_ref, uap_ref, uan_ref), (cg_ref, ug_ref, ugp_ref, ugn_ref)):
            cat[0:HALO, :] = jnp.where(first, zero, prv[...])
            cat[HALO:HALO + tm, :] = cur[...]
            cat[HALO + tm:, :] = nxt[...]
        ya = _conv(ca_ref, wa_ref, ba_ref, ext)
        yg = _conv(cg_ref, wg_ref, bg_ref, ext)
        dfe = jnp.concatenate([df_ref[...].astype(F32), dfn_ref[0:HALO, :].astype(F32)], axis=0)
        row = lax.broadcasted_iota(jnp.int32, (ext, tn), 0)
        dfe = jnp.where((row < tm) | jnp.logical_not(last), dfe, 0.0)
        sg = jax.nn.sigmoid(yg)
        ya_ref[...] = dfe * (yg * sg)
        yg_ref[...] = dfe * ya * (sg * (1.0 + yg * (1.0 - sg)))
        for dy, cat, w_ref, du_ref, acc in ((ya_ref, ca_ref, wa_ref, dua_ref, acca_ref),
                                             (yg_ref, cg_ref, wg_ref, dug_ref, accg_ref)):
            d0 = dy[0:tm, :]
            du = w_ref[2:3, :] * d0 + w_ref[1:2, :] * dy[pl.ds(1, tm), :] + w_ref[0:1, :] * dy[pl.ds(2, tm), :]
            du_ref[...] = du.astype(BF16)
            for k in range(3):
                acc[k:k + 1, :] += jnp.sum(d0 * cat[pl.ds(HALO - 2 + k, tm), :], axis=0, keepdims=True)
            acc[3:4, :] += jnp.sum(d0, axis=0, keepdims=True)

    nrow = t_all // HALO
    cur = lambda off: pl.BlockSpec((tm, tn), lambda n, t: (t, n + off))
    prev = lambda off: pl.BlockSpec((HALO, tn), lambda n, t: (jnp.maximum(t * (tm // HALO) - 1, 0), n + off))
    nxt = lambda off: pl.BlockSpec((HALO, tn), lambda n, t: (jnp.minimum((t + 1) * (tm // HALO), nrow - 1), n + off))
    vec = lambda r, off: pl.BlockSpec((r, tn), lambda n, t: (0, n + off))
    dcur = pl.BlockSpec((tm, tn), lambda n, t: (t, n))
    dnxt = pl.BlockSpec((16, tn), lambda n, t: (jnp.minimum((t + 1) * (tm // 16), t_all // 16 - 1), n))
    acc = pl.BlockSpec((8, tn), lambda n, t: (0, n))
    return pl.pallas_call(
        body, name="ffn_gate_bwd", grid=(nc, t_all // tm),
        in_specs=[cur(0), prev(0), nxt(0), cur(nc), prev(nc), nxt(nc), dcur, dnxt, vec(3, 0), vec(3, nc), vec(1, 0),
                  vec(1, nc)],
        out_specs=[dcur, dcur, acc, acc],
        out_shape=[jax.ShapeDtypeStruct((t_all, D_FF), BF16), jax.ShapeDtypeStruct((t_all, D_FF), BF16),
                   jax.ShapeDtypeStruct((8, D_FF), F32), jax.ShapeDtypeStruct((8, D_FF), F32)],
        scratch_shapes=[pltpu.VMEM((tm + 2 * HALO, tn), F32)] * 2 + [pltpu.VMEM((ext, tn), F32)] * 2,
        compiler_params=_params(),
    )(u, u, u, u, u, u, dfi, dfi, conv_w, conv_w, conv_b, conv_b)


def _ffn_down(ffn_in, w_down, xh1, ln1_g, ln1_b, ada3, ln2_g, ln2_b, target, seq):
    t_all = xh1.shape[0]
    tm = 256
    nts = seq // tm

    def body(f_ref, w_ref, xh_ref, g1_ref, b1_ref, ada_ref, g2_ref, b2_ref, tg_ref, dr2_ref, acc_ref):
        i = pl.program_id(0)

        @pl.when(i == 0)
        def _():
            acc_ref[...] = jnp.zeros_like(acc_ref)
        ffn = _nn(f_ref[...], w_ref[...])
        x1 = xh_ref[...] * g1_ref[...] + b1_ref[...]
        r2 = ALPHA * x1 + ada_ref[0, 5:6, :] * ffn
        d = r2 - jnp.mean(r2, axis=1, keepdims=True)
        rstd = lax.rsqrt(jnp.mean(d * d, axis=1, keepdims=True) + LN_EPS)
        xh2 = d * rstd
        diff = xh2 * g2_ref[...] + b2_ref[...] - tg_ref[...]
        dy = diff * (1.0 / D_MODEL)
        dr2 = _layer_norm_bwd(dy * g2_ref[...], xh2, rstd)
        dr2_ref[...] = dr2
        acc_ref[0:1, :] += jnp.sum(dy * xh2, axis=0, keepdims=True)
        acc_ref[1:2, :] += jnp.sum(dy, axis=0, keepdims=True)
        acc_ref[2:3, :] += jnp.sum(diff * diff, axis=0, keepdims=True) * (0.5 / D_MODEL)
        acc_ref[pl.ds(8 + i // nts, 1), :] += jnp.sum(dr2 * ffn, axis=0, keepdims=True)

    tok = lambda w: pl.BlockSpec((tm, w), lambda i: (i, 0))
    vec = pl.BlockSpec((1, D_MODEL), lambda i: (0, 0))
    return pl.pallas_call(
        body, name="ffn_down", grid=(t_all // tm,),
        in_specs=[tok(D_FF), pl.BlockSpec(w_down.shape, lambda i: (0, 0)), tok(D_MODEL), vec, vec,
                  pl.BlockSpec((1, 6, D_MODEL), lambda i: (i // nts, 0, 0)), vec, vec, tok(D_MODEL)],
        out_specs=[tok(D_MODEL), pl.BlockSpec((16, D_MODEL), lambda i: (0, 0))],
        out_shape=[jax.ShapeDtypeStruct((t_all, D_MODEL), F32), jax.ShapeDtypeStruct((16, D_MODEL), F32)],
        compiler_params=_params(),
    )(ffn_in, w_down, xh1, ln1_g, ln1_b, ada3, ln2_g, ln2_b, target)


def _ffn_down_bwd(dr2, ada3, w_down, seq):
    t_all = dr2.shape[0]
    tm = 256
    nts = seq // tm

    def body(d_ref, ada_ref, w_ref, dffn_ref, dfi_ref):
        dffn = (d_ref[...] * ada_ref[0, 5:6, :]).astype(BF16)
        dffn_ref[...] = dffn
        dfi_ref[...] = _nt(dffn, w_ref[...]).astype(BF16)

    tok = lambda w: pl.BlockSpec((tm, w), lambda i: (i, 0))
    return pl.pallas_call(
        body, name="ffn_down_bwd", grid=(t_all // tm,),
        in_specs=[tok(D_MODEL), pl.BlockSpec((1, 6, D_MODEL), lambda i: (i // nts, 0, 0)),
                  pl.BlockSpec(w_down.shape, lambda i: (0, 0))],
        out_specs=[tok(D_MODEL), tok(D_FF)],
        out_shape=[jax.ShapeDtypeStruct((t_all, D_MODEL), BF16), jax.ShapeDtypeStruct((t_all, D_FF), BF16)],
        compiler_params=_params(),
    )(dr2, ada3, w_down)


def _ffn_up_bwd(du_a, du_g, w_up, dr2, xh1, rs1, mix, ada3, ln1_g, ln1_b, seq):
    t_all = dr2.shape[0]
    tm = 256
    nts = seq // tm

    def body(da_ref, dg_ref, w_ref, dr2_ref, xh_ref, rs_ref, mix_ref, ada_ref, g_ref, b_ref, dr1_ref, dmix_ref,
             acc_ref):
        i = pl.program_id(0)

        @pl.when(i == 0)
        def _():
            acc_ref[...] = jnp.zeros_like(acc_ref)
        dh2 = _nt(da_ref[...], w_ref[:, :D_FF]) + _nt(dg_ref[...], w_ref[:, D_FF:])
        xh = xh_ref[...]
        x1 = xh * g_ref[...] + b_ref[...]
        dx1 = ALPHA * dr2_ref[...] + dh2 * (1.0 + ada_ref[0, 4:5, :])
        dr1 = _layer_norm_bwd(dx1 * g_ref[...], xh, rs_ref[:, 0:1])
        dr1_ref[...] = dr1
        dmix_ref[...] = (dr1 * ada_ref[0, 2:3, :]).astype(BF16)
        b = i // nts
        acc_ref[0:1, :] += jnp.sum(dx1 * xh, axis=0, keepdims=True)
        acc_ref[1:2, :] += jnp.sum(dx1, axis=0, keepdims=True)
        acc_ref[pl.ds(8 + b, 1), :] += jnp.sum(dh2 * x1, axis=0, keepdims=True)
        acc_ref[pl.ds(16 + b, 1), :] += jnp.sum(dh2, axis=0, keepdims=True)
        acc_ref[pl.ds(24 + b, 1), :] += jnp.sum(dr1 * mix_ref[...].astype(F32), axis=0, keepdims=True)

    tok = lambda w: pl.BlockSpec((tm, w), lambda i: (i, 0))
    vec = pl.BlockSpec((1, D_MODEL), lambda i: (0, 0))
    return pl.pallas_call(
        body, name="ffn_up_bwd", grid=(t_all // tm,),
        in_specs=[tok(D_FF), tok(D_FF), pl.BlockSpec(w_up.shape, lambda i: (0, 0)), tok(D_MODEL), tok(D_MODEL),
                  tok(LANES), tok(D_MODEL), pl.BlockSpec((1, 6, D_MODEL), lambda i: (i // nts, 0, 0)), vec, vec],
        out_specs=[tok(D_MODEL), tok(D_MODEL), pl.BlockSpec((32, D_MODEL), lambda i: (0, 0))],
        out_shape=[jax.ShapeDtypeStruct((t_all, D_MODEL), F32), jax.ShapeDtypeStruct((t_all, D_MODEL), BF16),
                   jax.ShapeDtypeStruct((32, D_MODEL), F32)],
        compiler_params=_params(),
    )(du_a, du_g, w_up, dr2, xh1, rs1, mix, ada3, ln1_g, ln1_b)


def _perm(a, d, seq):
    if d == 1:
        return a
    t_all, w = a.shape
    return a.reshape(t_all // seq, seq // d, d, w).transpose(0, 2, 1, 3).reshape(t_all, w)


def _unperm(a, d, seq):
    if d == 1:
        return a
    t_all, w = a.shape
    return a.reshape(t_all // seq, d, seq // d, w).transpose(0, 2, 1, 3).reshape(t_all, w)


DILATIONS = (1, 4, 16)


def _stack_perm(a, seq):
    return jnp.stack([_perm(a, d, seq) for d in DILATIONS])


def _rows(a):
    return a[:, :N_HEADS].T


def _rope_freq():
    f = np.float32(ROPE_THETA) ** (-np.arange(0, ROPE_DIMS, 2, dtype=np.float32) / np.float32(ROPE_DIMS))
    return jnp.asarray(np.tile(f.astype(np.float32), LANES // (ROPE_DIMS // 2))[None, :])


def _local_step(x, c, positions, target, w_ada, b_ada, w_in, b_fgate, gn_a, gn_b, w_out, ln1_g, ln1_b, w_up,
                conv_w, conv_b, w_down, ln2_g, ln2_b):
    nbat, seq, _ = x.shape
    t_all = nbat * seq
    xf = x.reshape(t_all, D_MODEL)
    tg = target.reshape(t_all, D_MODEL)
    pos = positions.reshape(t_all, 1)
    freq = _rope_freq()
    c16 = jnp.zeros((16, D_MODEL), F32).at[:nbat].set(c)

    wqkv = jnp.concatenate([w_in[:, :3 * WIDTH], w_in[:, 3 * WIDTH + N_HEADS:]], axis=1)
    wf16 = jnp.zeros((16, D_MODEL), BF16).at[:N_HEADS].set(w_in[:, 3 * WIDTH:3 * WIDTH + N_HEADS].T)
    bf = b_fgate.reshape(N_HEADS, 1)

    ada = _ada_fwd(c16, w_ada, b_ada)
    ada3 = ada[:nbat].reshape(nbat, 6, D_MODEL)
    h1, za, zb, fa_t = _inproj(xf, ada3, pos, wqkv, wf16, freq, seq)
    f_row = _fgate_fwd(fa_t, bf, seq)
    f_col = jnp.zeros((t_all, LANES), F32).at[:, :N_HEADS].set(f_row.T)
    vt = za[:, 2 * WIDTH:].reshape(nbat, seq, WIDTH).transpose(0, 2, 1).reshape(nbat * WIDTH, seq)
    oa, lse_row_a = _fox_fwd(za, vt, f_col, seq)
    lse_a = jnp.zeros((t_all, LANES), F32).at[:, :N_HEADS].set(lse_row_a.T)
    qs = _stack_perm(zb[:, :WIDTH], seq)
    ks = _stack_perm(zb[:, WIDTH:2 * WIDTH], seq)
    vs = _stack_perm(zb[:, 2 * WIDTH:], seq)
    o3p, l3p = _dil_fwd(qs, ks, vs, seq)
    o3 = [_unperm(o3p[p], d, seq) for p, d in enumerate(DILATIONS)]
    l3 = [_unperm(l3p[p], d, seq) for p, d in enumerate(DILATIONS)]
    ob, lse_b, merged, mix, xh1, rs1, h2 = _mix_out(oa, o3, l3, gn_a, gn_b, w_out, xf, ada3, ln1_g, ln1_b, seq)
    u = _matmul(h2, w_up, False, F32, 256, 512, "ffn_up")
    ffn_in = _ffn_gate(u, conv_w, conv_b, seq)
    dr2, acc2 = _ffn_down(ffn_in, w_down, xh1, ln1_g, ln1_b, ada3, ln2_g, ln2_b, tg, seq)

    dffn, dfi = _ffn_down_bwd(dr2, ada3, w_down, seq)
    d_w_down = _matmul_tn(ffn_in, dffn, 512, 512, "dw_down")
    du_a, du_g, acc_ca, acc_cg = _ffn_gate_bwd(u, dfi, conv_w, conv_b, seq)
    dr1, dmix, acc1 = _ffn_up_bwd(du_a, du_g, w_up, dr2, xh1, rs1, mix, ada3, ln1_g, ln1_b, seq)
    d_w_up = jnp.concatenate([_matmul_tn(h2, du_a, 256, 512, "dw_up_a"), _matmul_tn(h2, du_g, 256, 512, "dw_up_g")],
                             axis=1)

    doa, dob, dl_a, dl_b, acc_gn = _mix_out_bwd(dmix, w_out, oa, ob, gn_a, gn_b)
    d_w_out = _matmul_tn(merged, dmix, 512, 512, "dw_out")
    dqa, df_q = _fox_bwd_dq(za, doa, f_row, lse_a, dl_a, seq)
    dka, dva, df_k = _fox_bwd_dkv(za, doa, f_col, lse_row_a, _rows(dl_a), seq)
    dfa_t, dbf = _fgate_bwd(_rows(df_q + df_k), fa_t, bf, seq)
    dos = _stack_perm(dob, seq)
    lses = _stack_perm(lse_b, seq)
    dls = _stack_perm(dl_b, seq)
    dq3 = _dil_bwd_dq(qs, ks, vs, dos, lses, dls, seq)
    lse_rows = lses[:, :, :N_HEADS].transpose(0, 2, 1)
    dl_rows = dls[:, :, :N_HEADS].transpose(0, 2, 1)
    dk3, dv3 = _dil_bwd_dkv(qs, ks, vs, dos, lse_rows, dl_rows, seq)
    unsum = lambda a3: sum(_unperm(a3[p], d, seq) for p, d in enumerate(DILATIONS))
    dza = jnp.concatenate([dqa, dka, dva], axis=1)
    dfa16 = jnp.zeros((16, t_all), BF16).at[:N_HEADS].set(dfa_t.astype(BF16))
    grad_x, dz, acc0 = _inproj_bwd(dza, unsum(dq3), unsum(dk3), unsum(dv3), dfa16, pos, wqkv, wf16, freq, dr1, xf,
                                   ada3, seq)
    d_wqkv = _matmul_tn(h1, dz, 512, 512, "dw_in")
    d_wf = _matmul_rows(dfa16, h1, 512, "dw_fgate")[:N_HEADS].T
    d_w_in = jnp.concatenate([d_wqkv[:, :3 * WIDTH], d_wf, d_wqkv[:, 3 * WIDTH:]], axis=1)

    dada = jnp.concatenate([acc0[8:8 + nbat], acc0[:nbat], acc1[24:24 + nbat], acc1[16:16 + nbat], acc1[8:8 + nbat],
                            acc2[8:8 + nbat]], axis=1)
    dada16 = jnp.zeros((16, 6 * D_MODEL), F32).at[:nbat].set(dada)
    d_w_ada = _ada_bwd(c16, dada16)

    grads = dict(
        w_ada=d_w_ada, b_ada=jnp.sum(dada, axis=0, keepdims=True), w_in=d_w_in, b_fgate=dbf[:, 0][None, :],
        gn_a=acc_gn[0:1, :WIDTH], gn_b=acc_gn[0:1, WIDTH:], w_out=d_w_out, ln1_g=acc1[0:1], ln1_b=acc1[1:2],
        w_up=d_w_up, conv_w=jnp.concatenate([acc_ca[0:3], acc_cg[0:3]], axis=1),
        conv_b=jnp.concatenate([acc_ca[3:4], acc_cg[3:4]], axis=1), w_down=d_w_down, ln2_g=acc2[0:1],
        ln2_b=acc2[1:2])
    return acc2[2:3], grad_x.reshape(x.shape), grads


BIG = ("w_ada", "w_in", "w_out", "w_up", "conv_w", "w_down")
COLUMN_SHARDED = ("w_ada", "w_in", "w_up", "conv_w")
SMALL = ("b_ada", "b_fgate", "gn_a", "gn_b", "ln1_g", "ln1_b", "conv_b", "ln2_g", "ln2_b")
ADAM_ROWS = dict(w_ada=256, w_in=256, w_out=128, w_up=256, conv_w=3, w_down=176)
SMALL_ROWS = 24


def _full_from_gathered(name, g):
    if name in COLUMN_SHARDED:
        return g.transpose(1, 0, 2).reshape(g.shape[1], N_DEV * g.shape[2])
    return g.reshape(N_DEV * g.shape[1], g.shape[2])


def _dest_major(name, full):
    if name in COLUMN_SHARDED:
        r, cfull = full.shape
        return full.reshape(r, N_DEV, cfull // N_DEV).transpose(1, 0, 2)
    return full.reshape(N_DEV, full.shape[0] // N_DEV, full.shape[1])


def _pack_small(vals, extra=None):
    parts = [vals[n].reshape(-1) for n in SMALL]
    if extra is not None:
        parts.append(extra.reshape(-1))
    flat = jnp.concatenate(parts)
    return jnp.pad(flat, (0, SMALL_ROWS * D_MODEL - flat.shape[0])).reshape(SMALL_ROWS, D_MODEL)


def _unpack_small(packed, like):
    flat = packed.reshape(-1)
    out, off = {}, 0
    for n in SMALL:
        size = like[n].size
        out[n] = flat[off:off + size].reshape(like[n].shape)
        off += size
    return out, flat[off:off + D_MODEL]


def kernel(x, c, positions, w_ada, b_ada, w_in, b_fgate, gn_a, gn_b, w_out, ln1_g, ln1_b, w_up, conv_w, conv_b, w_down, ln2_g, ln2_b, loss_target, m_w_ada, m_b_ada, m_w_in, m_b_fgate, m_gn_a, m_gn_b, m_w_out, m_ln1_g, m_ln1_b, m_w_up, m_conv_w, m_conv_b, m_w_down, m_ln2_g, m_ln2_b, v_w_ada, v_b_ada, v_w_in, v_b_fgate, v_gn_a, v_gn_b, v_w_out, v_ln1_g, v_ln1_b, v_w_up, v_conv_w, v_conv_b, v_w_down, v_ln2_g, v_ln2_b):
    w = dict(w_ada=w_ada[0], b_ada=b_ada, w_in=w_in[0], b_fgate=b_fgate, gn_a=gn_a, gn_b=gn_b, w_out=w_out[0],
             ln1_g=ln1_g, ln1_b=ln1_b, w_up=w_up[0], conv_w=conv_w[0], conv_b=conv_b, w_down=w_down[0], ln2_g=ln2_g,
             ln2_b=ln2_b)
    m = dict(w_ada=m_w_ada[0], b_ada=m_b_ada, w_in=m_w_in[0], b_fgate=m_b_fgate, gn_a=m_gn_a, gn_b=m_gn_b,
             w_out=m_w_out[0], ln1_g=m_ln1_g, ln1_b=m_ln1_b, w_up=m_w_up[0], conv_w=m_conv_w[0], conv_b=m_conv_b,
             w_down=m_w_down[0], ln2_g=m_ln2_g, ln2_b=m_ln2_b)
    v = dict(w_ada=v_w_ada[0], b_ada=v_b_ada, w_in=v_w_in[0], b_fgate=v_b_fgate, gn_a=v_gn_a, gn_b=v_gn_b,
             w_out=v_w_out[0], ln1_g=v_ln1_g, ln1_b=v_ln1_b, w_up=v_w_up[0], conv_w=v_conv_w[0], conv_b=v_conv_b,
             w_down=v_w_down[0], ln2_g=v_ln2_g, ln2_b=v_ln2_b)

    shards = [w[n] if n == "conv_w" else w[n].astype(BF16) for n in BIG]
    gathered = _weight_gather(shards)
    full = {n: _full_from_gathered(n, g) for n, g in zip(BIG, gathered)}

    loss_lanes, grad_x, g_local = _local_step(
        x, c, positions, loss_target, full["w_ada"], b_ada, full["w_in"], b_fgate, gn_a, gn_b, full["w_out"], ln1_g,
        ln1_b, full["w_up"], full["conv_w"], conv_b, full["w_down"], ln2_g, ln2_b)

    small_local = _pack_small(g_local, loss_lanes)
    exchanged = _grad_exchange([_dest_major(n, g_local[n]) for n in BIG], small_local)

    grad, delta, new_m, new_v = {}, {}, {}, {}
    for n, parts in zip(BIG, exchanged[:-1]):
        grad[n], delta[n], new_m[n], new_v[n] = (
            a[None] for a in _adamw(parts, w[n], m[n], v[n], ADAM_ROWS[n], "adamw_" + n))
    packed = _adamw(exchanged[-1], _pack_small(w), _pack_small(m), _pack_small(v), SMALL_ROWS, "adamw_small")
    for dst, pk in zip((grad, delta, new_m, new_v), packed):
        vals, lanes = _unpack_small(pk, w)
        dst.update(vals)
        if dst is grad:
            loss = jnp.sum(lanes)

    order = ("w_ada", "b_ada", "w_in", "b_fgate", "gn_a", "gn_b", "w_out", "ln1_g", "ln1_b", "w_up", "conv_w", "conv_b",
             "w_down", "ln2_g", "ln2_b")
    return (loss, grad_x, *[grad[n] for n in order], *[delta[n] for n in order], *[new_m[n] for n in order],
            *[new_v[n] for n in order])
```

```python
import functools

import numpy as np
import jax
import jax.numpy as jnp
from jax import lax
from jax.experimental import pallas as pl
from jax.experimental.pallas import tpu as pltpu

F32, BF16 = jnp.float32, jnp.bfloat16
HIGHEST = lax.Precision.HIGHEST
MESH = pl.DeviceIdType.MESH
ANY = pl.BlockSpec(memory_space=pl.ANY)

D_MODEL = 1024
N_HEADS = 8
HEAD_DIM = 64
WIDTH = 512
D_FF = 2816
N_DEV = 8
ROPE_DIMS = 16
ROPE_THETA = 500000.0
ALPHA = 2.0 ** 0.25
LN_EPS = 1e-5
RMS_EPS = 1e-6
NEG = -1e30
Q_SCALE = 0.125
BLK = 128
LANES = 128
VMEM_LIMIT_BYTES = 56 * 1024 * 1024

ADAM_LR, ADAM_B1, ADAM_B2, ADAM_EPS, ADAM_WD, ADAM_STEP = 0.001, 0.9, 0.999, 1e-08, 0.01, 10


def _params(vmem=VMEM_LIMIT_BYTES):
    return pltpu.CompilerParams(vmem_limit_bytes=vmem)


def _nn(a, b):
    return jnp.dot(a, b, preferred_element_type=F32)


def _nt(a, b):
    return lax.dot_general(a, b, (((1,), (1,)), ((), ())), preferred_element_type=F32)


def _tn(a, b):
    return lax.dot_general(a, b, (((0,), (0,)), ((), ())), preferred_element_type=F32)


def _head_mats():
    r = lax.broadcasted_iota(jnp.int32, (LANES, WIDTH), 0)
    c = lax.broadcasted_iota(jnp.int32, (LANES, WIDTH), 1)
    e = ((c >> 6) == r).astype(F32)
    r2 = lax.broadcasted_iota(jnp.int32, (WIDTH, LANES), 0)
    c2 = lax.broadcasted_iota(jnp.int32, (WIDTH, LANES), 1)
    et = ((r2 >> 6) == c2).astype(F32)
    return e, et


def _hexp(w, e):
    return jnp.dot(w, e, precision=HIGHEST, preferred_element_type=F32)


def _hsum(x, et):
    return jnp.dot(x, et, precision=HIGHEST, preferred_element_type=F32)


def _rope_tabs(pos_ref, fr_ref, sign):
    ang = pos_ref[...].astype(F32) * fr_ref[...]
    lane = lax.broadcasted_iota(jnp.int32, ang.shape, 1) & (HEAD_DIM - 1)
    m1 = lane < ROPE_DIMS // 2
    m2 = (lane >= ROPE_DIMS // 2) & (lane < ROPE_DIMS)
    cos = jnp.cos(ang)
    sin = jnp.sin(ang) * sign
    return (jnp.where(m1 | m2, cos, 1.0), jnp.where(m1, -sin, 0.0), jnp.where(m2, sin, 0.0))


def _rope(z, tabs):
    c, s1, s2 = tabs
    parts = []
    for p in range(z.shape[1] // LANES):
        zp = z[:, LANES * p:LANES * (p + 1)]
        parts.append(zp * c + pltpu.roll(zp, LANES - 8, 1) * s1 + pltpu.roll(zp, 8, 1) * s2)
    return jnp.concatenate(parts, axis=1)


def _half_masks(rows):
    lane = lax.broadcasted_iota(jnp.int32, (rows, LANES), 1)
    lo = lane < HEAD_DIM
    return lo, jnp.logical_not(lo)


def _layer_norm_bwd(dxh, xh, rstd):
    m1 = jnp.mean(dxh, axis=1, keepdims=True)
    m2 = jnp.mean(dxh * xh, axis=1, keepdims=True)
    return rstd * (dxh - m1 - xh * m2)


def _coords():
    return lax.axis_index("x"), lax.axis_index("y"), lax.axis_index("c")


def _peer(x, y, c, k):
    return (1 - x if k & 4 else x, 1 - y if k & 2 else y, 1 - c if k & 1 else c)


def _comm_sems(n):
    return [pltpu.SemaphoreType.DMA((N_DEV - 1, n)), pltpu.SemaphoreType.DMA((N_DEV - 1, n)),
            pltpu.SemaphoreType.DMA((n,))]


def _comm_copies(ins, outs, to_all, sems):
    send_sems, recv_sems, local_sems = sems
    x, y, c = _coords()
    me = 4 * x + 2 * y + c
    copies = [pltpu.make_async_copy(ins[t] if to_all[t] else ins[t].at[me], outs[t].at[me], local_sems.at[t])
              for t in range(len(ins))]
    for k in range(1, N_DEV):
        px, py, pc = _peer(x, y, c, k)
        dest = 4 * px + 2 * py + pc
        for t in range(len(ins)):
            copies.append(pltpu.make_async_remote_copy(
                src_ref=ins[t] if to_all[t] else ins[t].at[dest], dst_ref=outs[t].at[me],
                send_sem=send_sems.at[k - 1, t], recv_sem=recv_sems.at[k - 1, t],
                device_id=(px, py, pc), device_id_type=MESH))
    return copies


def _comm_out_shapes(ins, to_all):
    return [jax.ShapeDtypeStruct(((N_DEV,) + a.shape) if ta else a.shape, a.dtype) for a, ta in zip(ins, to_all)]


def _exchange(ins, to_all, name):
    n = len(ins)

    def body(*refs):
        copies = _comm_copies(refs[:n], refs[n:2 * n], to_all, refs[2 * n:])
        for cp in copies:
            cp.start()
        for cp in copies:
            cp.wait()

    return pl.pallas_call(
        body, name=name, out_shape=_comm_out_shapes(ins, to_all), in_specs=[ANY] * n, out_specs=[ANY] * n,
        scratch_shapes=_comm_sems(n),
    )(*ins)


def _adamw(parts, w, m, v, rows, name):
    _, r_all, cols = parts.shape
    c1 = 1.0 - ADAM_B1 ** ADAM_STEP
    c2 = 1.0 - ADAM_B2 ** ADAM_STEP

    def body(p_ref, w_ref, m_ref, v_ref, g_ref, d_ref, mo_ref, vo_ref):
        g = p_ref[0].astype(F32)
        for s in range(1, N_DEV):
            g = g + p_ref[s].astype(F32)
        mn = ADAM_B1 * m_ref[...] + (1.0 - ADAM_B1) * g
        vn = ADAM_B2 * v_ref[...] + (1.0 - ADAM_B2) * (g * g)
        m_hat = mn / c1
        v_hat = vn / c2
        g_ref[...] = g
        d_ref[...] = -ADAM_LR * (m_hat / (jnp.sqrt(v_hat) + ADAM_EPS) + ADAM_WD * w_ref[...])
        mo_ref[...] = mn
        vo_ref[...] = vn

    spec = pl.BlockSpec((rows, cols), lambda i: (i, 0))
    return pl.pallas_call(
        body, name=name, grid=(r_all // rows,),
        in_specs=[pl.BlockSpec((N_DEV, rows, cols), lambda i: (0, i, 0)), spec, spec, spec],
        out_specs=[spec] * 4, out_shape=[jax.ShapeDtypeStruct((r_all, cols), F32)] * 4,
        compiler_params=_params(),
    )(parts, w, m, v)


def _matmul(a, w, transposed_w, out_dtype, tm, chunk, name):
    t_all, k = a.shape
    n = w.shape[0] if transposed_w else w.shape[1]

    def body(a_ref, w_ref, o_ref):
        av = a_ref[...]
        for j in range(n // chunk):
            cs = slice(j * chunk, (j + 1) * chunk)
            r = _nt(av, w_ref[cs, :]) if transposed_w else _nn(av, w_ref[:, cs])
            o_ref[:, cs] = r.astype(out_dtype)

    return pl.pallas_call(
        body, name=name, grid=(t_all // tm,),
        in_specs=[pl.BlockSpec((tm, k), lambda i: (i, 0)), pl.BlockSpec(w.shape, lambda i: (0, 0))],
        out_specs=pl.BlockSpec((tm, n), lambda i: (i, 0)),
        out_shape=jax.ShapeDtypeStruct((t_all, n), out_dtype), compiler_params=_params(),
    )(a, w)


def _matmul_tn(a, b, tn, tk, name):
    t_all, k1 = a.shape
    n = b.shape[1]

    def body(a_ref, b_ref, o_ref):
        @pl.when(pl.program_id(1) == 0)
        def _():
            o_ref[...] = jnp.zeros_like(o_ref)
        o_ref[...] += _tn(a_ref[...], b_ref[...])

    return pl.pallas_call(
        body, name=name, grid=(n // tn, t_all // tk),
        in_specs=[pl.BlockSpec((tk, k1), lambda j, t: (t, 0)), pl.BlockSpec((tk, tn), lambda j, t: (t, j))],
        out_specs=pl.BlockSpec((k1, tn), lambda j, t: (0, j)),
        out_shape=jax.ShapeDtypeStruct((k1, n), F32), compiler_params=_params(),
    )(a, b)


def _matmul_rows(a, b, tk, name):
    r, t_all = a.shape
    n = b.shape[1]

    def body(a_ref, b_ref, o_ref):
        @pl.when(pl.program_id(0) == 0)
        def _():
            o_ref[...] = jnp.zeros_like(o_ref)
        o_ref[...] += _nn(a_ref[...], b_ref[...])

    return pl.pallas_call(
        body, name=name, grid=(t_all // tk,),
        in_specs=[pl.BlockSpec((r, tk), lambda t: (0, t)), pl.BlockSpec((tk, n), lambda t: (t, 0))],
        out_specs=pl.BlockSpec((r, n), lambda t: (0, 0)),
        out_shape=jax.ShapeDtypeStruct((r, n), F32), compiler_params=_params(),
    )(a, b)


def _ada_fwd(c16, w_ada, b_ada):
    n = w_ada.shape[1]
    tn = n // N_DEV

    def body(c_ref, w_ref, b_ref, o_ref):
        cv = c_ref[...]
        s = (cv * jax.nn.sigmoid(cv)).astype(BF16)
        o_ref[...] = _nn(s, w_ref[...]) + b_ref[...]

    return pl.pallas_call(
        body, name="ada_fwd", grid=(N_DEV,),
        in_specs=[pl.BlockSpec(c16.shape, lambda j: (0, 0)), pl.BlockSpec((D_MODEL, tn), lambda j: (0, j)),
                  pl.BlockSpec((1, tn), lambda j: (0, j))],
        out_specs=pl.BlockSpec((c16.shape[0], tn), lambda j: (0, j)),
        out_shape=jax.ShapeDtypeStruct((c16.shape[0], n), F32), compiler_params=_params(),
    )(c16, w_ada, b_ada)


def _ada_bwd(c16, dada16):
    n = dada16.shape[1]
    tn = n // N_DEV

    def body(c_ref, d_ref, o_ref):
        cv = c_ref[...]
        s = (cv * jax.nn.sigmoid(cv)).astype(BF16)
        o_ref[...] = _tn(s, d_ref[...].astype(BF16))

    return pl.pallas_call(
        body, name="ada_bwd", grid=(N_DEV,),
        in_specs=[pl.BlockSpec(c16.shape, lambda j: (0, 0)), pl.BlockSpec((c16.shape[0], tn), lambda j: (0, j))],
        out_specs=pl.BlockSpec((D_MODEL, tn), lambda j: (0, j)),
        out_shape=jax.ShapeDtypeStruct((D_MODEL, n), F32), compiler_params=_params(),
    )(c16, dada16)


def _inproj(x, ada3, pos, wqkv, wf16, freq, seq):
    t_all = x.shape[0]
    tm = 256
    nts = seq // tm

    def body(x_ref, ada_ref, pos_ref, w_ref, wf_ref, fr_ref, h1_ref, za_ref, zb_ref, fa_ref):
        h1 = (x_ref[...] * (1.0 + ada_ref[0, 1:2, :]) + ada_ref[0, 0:1, :]).astype(BF16)
        h1_ref[...] = h1
        tabs = _rope_tabs(pos_ref, fr_ref, 1.0)
        for n in range(6):
            z = _nn(h1, w_ref[:, n * WIDTH:(n + 1) * WIDTH])
            if n in (3, 4):
                z = _rope(z, tabs)
            if n in (0, 3):
                z = z * Q_SCALE
            dst = za_ref if n < 3 else zb_ref
            dst[:, (n % 3) * WIDTH:(n % 3 + 1) * WIDTH] = z.astype(BF16)
        fa_ref[...] = _nt(wf_ref[...], h1)[:N_HEADS]

    tok = lambda w: pl.BlockSpec((tm, w), lambda i: (i, 0))
    return pl.pallas_call(
        body, name="inproj", grid=(t_all // tm,),
        in_specs=[tok(D_MODEL), pl.BlockSpec((1, 6, D_MODEL), lambda i: (i // nts, 0, 0)), tok(1),
                  pl.BlockSpec(wqkv.shape, lambda i: (0, 0)), pl.BlockSpec(wf16.shape, lambda i: (0, 0)),
                  pl.BlockSpec((1, LANES), lambda i: (0, 0))],
        out_specs=[tok(D_MODEL), tok(3 * WIDTH), tok(3 * WIDTH), pl.BlockSpec((N_HEADS, tm), lambda i: (0, i))],
        out_shape=[jax.ShapeDtypeStruct((t_all, D_MODEL), BF16), jax.ShapeDtypeStruct((t_all, 3 * WIDTH), BF16),
                   jax.ShapeDtypeStruct((t_all, 3 * WIDTH), BF16), jax.ShapeDtypeStruct((N_HEADS, t_all), F32)],
        compiler_params=_params(),
    )(x, ada3, pos, wqkv, wf16, freq)


def _fgate_fwd(fa_t, bf, seq):
    t_all = fa_t.shape[1]

    def body(fa_ref, b_ref, f_ref):
        lane = lax.broadcasted_iota(jnp.int32, (N_HEADS, LANES), 1)

        def chunk(j, carry):
            sl = pl.ds(pl.multiple_of(j * LANES, LANES), LANES)
            xv = fa_ref[:, sl] + b_ref[...]
            lf = jnp.minimum(xv, 0.0) - jnp.log(1.0 + jnp.exp(-jnp.abs(xv)))
            for s in (1, 2, 4, 8, 16, 32, 64):
                lf = lf + jnp.where(lane >= s, pltpu.roll(lf, s, 1), 0.0)
            lf = lf + carry
            f_ref[:, sl] = lf
            return lf[:, LANES - 1:LANES]

        lax.fori_loop(0, seq // LANES, chunk, jnp.zeros((N_HEADS, 1), F32))

    return pl.pallas_call(
        body, name="fgate_fwd", grid=(t_all // seq,),
        in_specs=[pl.BlockSpec((N_HEADS, seq), lambda b: (0, b)), pl.BlockSpec((N_HEADS, 1), lambda b: (0, 0))],
        out_specs=pl.BlockSpec((N_HEADS, seq), lambda b: (0, b)),
        out_shape=jax.ShapeDtypeStruct((N_HEADS, t_all), F32), compiler_params=_params(),
    )(fa_t, bf)


def _fgate_bwd(df_t, fa_t, bf, seq):
    t_all = fa_t.shape[1]

    def body(df_ref, fa_ref, b_ref, o_ref, s_ref):
        lane = lax.broadcasted_iota(jnp.int32, (N_HEADS, LANES), 1)

        @pl.when(pl.program_id(0) == 0)
        def _():
            s_ref[...] = jnp.zeros_like(s_ref)

        def chunk(jj, carry):
            car, tot = carry
            j = seq // LANES - 1 - jj
            sl = pl.ds(pl.multiple_of(j * LANES, LANES), LANES)
            d = df_ref[:, sl]
            for s in (1, 2, 4, 8, 16, 32, 64):
                d = d + jnp.where(lane < LANES - s, pltpu.roll(d, LANES - s, 1), 0.0)
            d = d + car
            dfa = d * jax.nn.sigmoid(-(fa_ref[:, sl] + b_ref[...]))
            o_ref[:, sl] = dfa
            return d[:, 0:1], tot + jnp.sum(dfa, axis=1, keepdims=True)

        z = jnp.zeros((N_HEADS, 1), F32)
        _, tot = lax.fori_loop(0, seq // LANES, chunk, (z, z))
        s_ref[...] += jnp.broadcast_to(tot, (N_HEADS, LANES))

    row = pl.BlockSpec((N_HEADS, seq), lambda b: (0, b))
    return pl.pallas_call(
        body, name="fgate_bwd", grid=(t_all // seq,),
        in_specs=[row, row, pl.BlockSpec((N_HEADS, 1), lambda b: (0, 0))],
        out_specs=[row, pl.BlockSpec((N_HEADS, LANES), lambda b: (0, 0))],
        out_shape=[jax.ShapeDtypeStruct((N_HEADS, t_all), F32), jax.ShapeDtypeStruct((N_HEADS, LANES), F32)],
        compiler_params=_params(),
    )(df_t, fa_t, bf)


FOX_T = 256


def _fox_prep(dst, src_ref, lo, hi):
    for p in range(4):
        v = src_ref[:, LANES * p:LANES * (p + 1)]
        dst[2 * p] = jnp.where(lo, v, jnp.zeros_like(v))
        dst[2 * p + 1] = jnp.where(hi, v, jnp.zeros_like(v))


def _fox_fwd(za, vt, f_col, seq, shards):
    t_all = za.shape[0]
    tq = FOX_T
    nq = seq // tq
    nbat = t_all // seq
    n = len(shards)
    to_all = [True] * n

    def body(*refs):
        q_ref, k_ref, vt_ref, fc_ref = refs[:4]
        o_ref, lse_ref = refs[4 + n:6 + n]
        qm_sc, m_sc, l_sc, acc_sc = refs[6 + 2 * n:10 + 2 * n]
        comm = (refs[4:4 + n], refs[6 + n:6 + 2 * n], to_all, refs[10 + 2 * n:])
        i = pl.program_id(1)

        @pl.when((pl.program_id(0) == 0) & (i == 0))
        def _():
            for cp in _comm_copies(*comm):
                cp.start()
        lo, hi = _half_masks(tq)
        r = lax.broadcasted_iota(jnp.int32, (tq, tq), 0)
        c = lax.broadcasted_iota(jnp.int32, (tq, tq), 1)
        tri = c >= r
        _fox_prep(qm_sc, q_ref, lo, hi)
        m_sc[...] = jnp.full(m_sc.shape, NEG, F32)
        l_sc[...] = jnp.zeros_like(l_sc)
        acc_sc[...] = jnp.zeros_like(acc_sc)

        def block(j, masked):
            sl = pl.ds(pl.multiple_of(j * tq, tq), tq)
            for p in range(4):
                kj = k_ref[sl, LANES * p:LANES * (p + 1)]
                for h in (2 * p, 2 * p + 1):
                    st = _nt(kj, qm_sc[h]) - fc_ref[sl, h:h + 1]
                    if masked:
                        st = jnp.where(tri, st, NEG)
                    m = m_sc[h:h + 1, :]
                    mn = jnp.maximum(m, jnp.max(st, axis=0, keepdims=True))
                    a = jnp.exp(m - mn)
                    pe = jnp.exp(st - mn)
                    m_sc[h:h + 1, :] = mn
                    l_sc[h:h + 1, :] = a * l_sc[h:h + 1, :] + jnp.sum(pe, axis=0, keepdims=True)
                    acc_sc[h] = a * acc_sc[h] + _nn(vt_ref[HEAD_DIM * h:HEAD_DIM * (h + 1), sl], pe.astype(BF16))

        def step(j, carry):
            block(j, False)
            return carry

        lax.fori_loop(0, i, step, 0)
        block(i, True)
        lse_ref[...] = m_sc[...] + jnp.log(l_sc[...])
        for p in range(4):
            ot = jnp.concatenate([acc_sc[h] / l_sc[h:h + 1, :] for h in (2 * p, 2 * p + 1)], axis=0)
            o_ref[:, LANES * p:LANES * (p + 1)] = ot.T

        @pl.when((pl.program_id(0) == nbat - 1) & (i == nq - 1))
        def _():
            for cp in _comm_copies(*comm):
                cp.wait()

    res = pl.pallas_call(
        body, name="fox_fwd", grid=(nbat, nq),
        in_specs=[pl.BlockSpec((tq, WIDTH), lambda b, i: (b * nq + i, 0)),
                  pl.BlockSpec((seq, WIDTH), lambda b, i: (b, 1)), pl.BlockSpec((WIDTH, seq), lambda b, i: (b, 0)),
                  pl.BlockSpec((seq, LANES), lambda b, i: (b, 0))] + [ANY] * n,
        out_specs=[pl.BlockSpec((tq, WIDTH), lambda b, i: (b * nq + i, 0)),
                   pl.BlockSpec((N_HEADS, tq), lambda b, i: (0, b * nq + i))] + [ANY] * n,
        out_shape=[jax.ShapeDtypeStruct((t_all, WIDTH), F32), jax.ShapeDtypeStruct((N_HEADS, t_all), F32)]
        + _comm_out_shapes(shards, to_all),
        scratch_shapes=[pltpu.VMEM((N_HEADS, tq, LANES), BF16), pltpu.VMEM((N_HEADS, tq), F32),
                        pltpu.VMEM((N_HEADS, tq), F32), pltpu.VMEM((N_HEADS, HEAD_DIM, tq), F32)] + _comm_sems(n),
        compiler_params=_params(),
    )(za, za, vt, f_col, *shards)
    return res[0], res[1], res[2:]


def _fox_bwd_dq(za, do, f_row, lse, dl, seq):
    t_all = za.shape[0]
    tq = FOX_T
    nq = seq // tq

    def body(q_ref, k_ref, v_ref, do_ref, fr_ref, lse_ref, dl_ref, dq_ref, df_ref, qm_sc, dm_sc, nl_sc, dd_sc,
             acc_sc, rs_sc):
        i = pl.program_id(1)
        lo, hi = _half_masks(tq)
        r = lax.broadcasted_iota(jnp.int32, (tq, tq), 0)
        c = lax.broadcasted_iota(jnp.int32, (tq, tq), 1)
        tri = c <= r
        _fox_prep(qm_sc, q_ref, lo, hi)
        _fox_prep(dm_sc, do_ref, lo, hi)
        for h in range(N_HEADS):
            nl_sc[h] = jnp.broadcast_to(lse_ref[:, h:h + 1], (tq, tq))
            dd_sc[h] = jnp.broadcast_to(dl_ref[:, h:h + 1], (tq, tq))
        acc_sc[...] = jnp.zeros_like(acc_sc)
        rs_sc[...] = jnp.zeros_like(rs_sc)

        def block(j, masked):
            sl = pl.ds(pl.multiple_of(j * tq, tq), tq)
            for p in range(4):
                cs = slice(LANES * p, LANES * (p + 1))
                kj = k_ref[sl, cs]
                vj = v_ref[sl, cs]
                for h in (2 * p, 2 * p + 1):
                    s = _nt(qm_sc[h], kj) - fr_ref[h:h + 1, sl] - nl_sc[h]
                    if masked:
                        s = jnp.where(tri, s, NEG)
                    ds = jnp.exp(s) * (_nt(dm_sc[h], vj) - dd_sc[h])
                    acc_sc[h] += _nn(ds.astype(BF16), kj)
                    rs_sc[h] += ds[:, :LANES] + ds[:, LANES:]

        def step(j, carry):
            block(j, False)
            return carry

        lax.fori_loop(0, i, step, 0)
        block(i, True)
        df_ref[...] = jnp.zeros_like(df_ref)
        for p in range(4):
            dq_ref[:, LANES * p:LANES * (p + 1)] = (
                jnp.where(lo, acc_sc[2 * p], acc_sc[2 * p + 1]) * Q_SCALE).astype(BF16)
            for h in (2 * p, 2 * p + 1):
                df_ref[:, h:h + 1] = jnp.sum(rs_sc[h], axis=1, keepdims=True)

    tile = lambda w: pl.BlockSpec((tq, w), lambda b, i: (b * nq + i, 0))
    return pl.pallas_call(
        body, name="fox_bwd_dq", grid=(t_all // seq, nq),
        in_specs=[tile(WIDTH), pl.BlockSpec((seq, WIDTH), lambda b, i: (b, 1)),
                  pl.BlockSpec((seq, WIDTH), lambda b, i: (b, 2)), tile(WIDTH),
                  pl.BlockSpec((N_HEADS, seq), lambda b, i: (0, b)), tile(LANES), tile(LANES)],
        out_specs=[tile(WIDTH), tile(LANES)],
        out_shape=[jax.ShapeDtypeStruct((t_all, WIDTH), BF16), jax.ShapeDtypeStruct((t_all, LANES), F32)],
        scratch_shapes=[pltpu.VMEM((N_HEADS, tq, LANES), BF16), pltpu.VMEM((N_HEADS, tq, LANES), BF16),
                        pltpu.VMEM((N_HEADS, tq, tq), F32), pltpu.VMEM((N_HEADS, tq, tq), F32),
                        pltpu.VMEM((N_HEADS, tq, LANES), F32), pltpu.VMEM((N_HEADS, tq, LANES), F32)],
        compiler_params=_params(),
    )(za, za, za, do, f_row, lse, dl)


def _fox_bwd_dkv(za, do, f_col, lse_row, dl_row, seq, grads):
    t_all = za.shape[0]
    tk = FOX_T
    nk = seq // tk
    nbat = t_all // seq
    n = len(grads)
    to_all = [False] * n

    def body(*refs):
        k_ref, v_ref, q_ref, do_ref, fc_ref, lr_ref, dr_ref = refs[:7]
        dk_ref, dv_ref, df_ref = refs[7 + n:10 + n]
        km_sc, vm_sc, fk_sc, dk_sc, dv_sc, cs_sc = refs[10 + 2 * n:16 + 2 * n]
        comm = (refs[7:7 + n], refs[10 + n:10 + 2 * n], to_all, refs[16 + 2 * n:])
        j = pl.program_id(1)

        @pl.when((pl.program_id(0) == 0) & (j == 0))
        def _():
            for cp in _comm_copies(*comm):
                cp.start()
        lo, hi = _half_masks(tk)
        r = lax.broadcasted_iota(jnp.int32, (tk, tk), 0)
        c = lax.broadcasted_iota(jnp.int32, (tk, tk), 1)
        tri = c >= r
        _fox_prep(km_sc, k_ref, lo, hi)
        _fox_prep(vm_sc, v_ref, lo, hi)
        for h in range(N_HEADS):
            fk_sc[h] = jnp.broadcast_to(fc_ref[:, h:h + 1], (tk, tk))
        dk_sc[...] = jnp.zeros_like(dk_sc)
        dv_sc[...] = jnp.zeros_like(dv_sc)
        cs_sc[...] = jnp.zeros_like(cs_sc)

        def block(i, masked):
            sl = pl.ds(pl.multiple_of(i * tk, tk), tk)
            for p in range(4):
                cs = slice(LANES * p, LANES * (p + 1))
                qi = q_ref[sl, cs]
                doi = do_ref[sl, cs]
                for h in (2 * p, 2 * p + 1):
                    st = _nt(km_sc[h], qi) - fk_sc[h] - lr_ref[h:h + 1, sl]
                    if masked:
                        st = jnp.where(tri, st, NEG)
                    pt = jnp.exp(st)
                    dst = pt * (_nt(vm_sc[h], doi) - dr_ref[h:h + 1, sl])
                    dv_sc[h] += _nn(pt.astype(BF16), doi)
                    dk_sc[h] += _nn(dst.astype(BF16), qi)
                    cs_sc[h] += dst[:, :LANES] + dst[:, LANES:]

        def step(i, carry):
            block(i, False)
            return carry

        block(j, True)
        lax.fori_loop(j + 1, nk, step, 0)
        df_ref[...] = jnp.zeros_like(df_ref)
        for p in range(4):
            cs = slice(LANES * p, LANES * (p + 1))
            dk_ref[:, cs] = jnp.where(lo, dk_sc[2 * p], dk_sc[2 * p + 1]).astype(BF16)
            dv_ref[:, cs] = jnp.where(lo, dv_sc[2 * p], dv_sc[2 * p + 1]).astype(BF16)
            for h in (2 * p, 2 * p + 1):
                df_ref[:, h:h + 1] = -jnp.sum(cs_sc[h], axis=1, keepdims=True)

        @pl.when((pl.program_id(0) == nbat - 1) & (j == nk - 1))
        def _():
            for cp in _comm_copies(*comm):
                cp.wait()

    tile = lambda w, col: pl.BlockSpec((tk, w), lambda b, j: (b * nk + j, col))
    full = lambda col: pl.BlockSpec((seq, WIDTH), lambda b, j: (b, col))
    row = pl.BlockSpec((N_HEADS, seq), lambda b, j: (0, b))
    acc = pltpu.VMEM((N_HEADS, tk, LANES), F32)
    res = pl.pallas_call(
        body, name="fox_bwd_dkv", grid=(nbat, nk),
        in_specs=[tile(WIDTH, 1), tile(WIDTH, 2), full(0), full(0), tile(LANES, 0), row, row] + [ANY] * n,
        out_specs=[tile(WIDTH, 0), tile(WIDTH, 0), tile(LANES, 0)] + [ANY] * n,
        out_shape=[jax.ShapeDtypeStruct((t_all, WIDTH), BF16), jax.ShapeDtypeStruct((t_all, WIDTH), BF16),
                   jax.ShapeDtypeStruct((t_all, LANES), F32)] + _comm_out_shapes(grads, to_all),
        scratch_shapes=[pltpu.VMEM((N_HEADS, tk, LANES), BF16), pltpu.VMEM((N_HEADS, tk, LANES), BF16),
                        pltpu.VMEM((N_HEADS, tk, tk), F32), acc, acc, acc] + _comm_sems(n),
        compiler_params=_params(),
    )(za, za, za, do, f_col, lse_row, dl_row, *grads)
    return res[0], res[1], res[2], res[3:]


def _dil_mask(has_prev):
    qi = lax.broadcasted_iota(jnp.int32, (BLK, 2 * BLK), 0)
    kj = lax.broadcasted_iota(jnp.int32, (BLK, 2 * BLK), 1)
    dist = qi + BLK - kj
    return (dist >= 0) & (dist <= BLK) & ((kj >= BLK) | has_prev)


def _dil_specs(t_all):
    nb = t_all // BLK
    cur = pl.BlockSpec((1, BLK, WIDTH), lambda p, n: (p, n, 0))
    prev = pl.BlockSpec((1, BLK, WIDTH), lambda p, n: (p, jnp.maximum(n - 1, 0), 0))
    nxt = pl.BlockSpec((1, BLK, WIDTH), lambda p, n: (p, jnp.minimum(n + 1, nb - 1), 0))
    stat = pl.BlockSpec((1, BLK, LANES), lambda p, n: (p, n, 0))
    return nb, cur, prev, nxt, stat


def _dil_fwd(qs, ks, vs, seq):
    t_all = qs.shape[1]
    nb, cur, prev, _, stat = _dil_specs(t_all)

    def body(q_ref, kp_ref, kc_ref, vp_ref, vc_ref, o_ref, lse_ref):
        nbs = (seq // BLK) >> (2 * pl.program_id(0))
        mask = _dil_mask((pl.program_id(1) & (nbs - 1)) != 0)
        lo, hi = _half_masks(BLK)
        lse_ref[...] = jnp.zeros_like(lse_ref)
        for p in range(4):
            cs = slice(LANES * p, LANES * (p + 1))
            qp = q_ref[0, :, cs]
            kcat = jnp.concatenate([kp_ref[0, :, cs], kc_ref[0, :, cs]], axis=0)
            vcat = jnp.concatenate([vp_ref[0, :, cs], vc_ref[0, :, cs]], axis=0)
            res = []
            for e in (0, 1):
                h = 2 * p + e
                qe = jnp.where(lo if e == 0 else hi, qp, jnp.zeros_like(qp))
                s = jnp.where(mask, _nt(qe, kcat), NEG)
                m = jnp.max(s, axis=1, keepdims=True)
                pe = jnp.exp(s - m)
                l = jnp.sum(pe, axis=1, keepdims=True)
                res.append(_nn(pe.astype(BF16), vcat) / l)
                lse_ref[0, :, h:h + 1] = m + jnp.log(l)
            o_ref[0, :, cs] = jnp.where(lo, res[0], res[1])

    return pl.pallas_call(
        body, name="dil_fwd", grid=(3, nb), in_specs=[cur, prev, cur, prev, cur], out_specs=[cur, stat],
        out_shape=[jax.ShapeDtypeStruct((3, t_all, WIDTH), F32), jax.ShapeDtypeStruct((3, t_all, LANES), F32)],
        compiler_params=_params(),
    )(qs, ks, ks, vs, vs)


def _dil_bwd_dq(qs, ks, vs, dos, lses, dls, seq):
    t_all = qs.shape[1]
    nb, cur, prev, _, stat = _dil_specs(t_all)

    def body(q_ref, kp_ref, kc_ref, vp_ref, vc_ref, do_ref, lse_ref, dl_ref, dq_ref):
        nbs = (seq // BLK) >> (2 * pl.program_id(0))
        mask = _dil_mask((pl.program_id(1) & (nbs - 1)) != 0)
        lo, hi = _half_masks(BLK)
        for p in range(4):
            cs = slice(LANES * p, LANES * (p + 1))
            qp = q_ref[0, :, cs]
            dop = do_ref[0, :, cs]
            kcat = jnp.concatenate([kp_ref[0, :, cs], kc_ref[0, :, cs]], axis=0)
            vcat = jnp.concatenate([vp_ref[0, :, cs], vc_ref[0, :, cs]], axis=0)
            res = []
            for e in (0, 1):
                h = 2 * p + e
                sel = lo if e == 0 else hi
                qe = jnp.where(sel, qp, jnp.zeros_like(qp))
                doe = jnp.where(sel, dop, jnp.zeros_like(dop))
                s = jnp.where(mask, _nt(qe, kcat) - lse_ref[0, :, h:h + 1], NEG)
                ds = jnp.exp(s) * (_nt(doe, vcat) - dl_ref[0, :, h:h + 1])
                res.append(_nn(ds.astype(BF16), kcat))
            dq_ref[0, :, cs] = jnp.where(lo, res[0], res[1]) * Q_SCALE

    return pl.pallas_call(
        body, name="dil_bwd_dq", grid=(3, nb), in_specs=[cur, prev, cur, prev, cur, cur, stat, stat], out_specs=cur,
        out_shape=jax.ShapeDtypeStruct((3, t_all, WIDTH), F32), compiler_params=_params(),
    )(qs, ks, ks, vs, vs, dos, lses, dls)


def _dil_bwd_dkv(qs, ks, vs, dos, lse_rows, dl_rows, seq):
    t_all = qs.shape[1]
    nb, cur, _, nxt, _ = _dil_specs(t_all)
    rcur = pl.BlockSpec((1, N_HEADS, BLK), lambda p, n: (p, 0, n))
    rnxt = pl.BlockSpec((1, N_HEADS, BLK), lambda p, n: (p, 0, jnp.minimum(n + 1, nb - 1)))

    def body(k_ref, v_ref, qc_ref, qn_ref, dc_ref, dn_ref, lc_ref, ln_ref, ec_ref, en_ref, dk_ref, dv_ref):
        nbs = (seq // BLK) >> (2 * pl.program_id(0))
        has_next = ((pl.program_id(1) + 1) & (nbs - 1)) != 0
        r = lax.broadcasted_iota(jnp.int32, (BLK, 2 * BLK), 0)
        c = lax.broadcasted_iota(jnp.int32, (BLK, 2 * BLK), 1)
        mask = ((c < BLK) & (c >= r)) | ((c >= BLK) & (c - BLK <= r) & has_next)
        lo, hi = _half_masks(BLK)
        for p in range(4):
            cs = slice(LANES * p, LANES * (p + 1))
            kp = k_ref[0, :, cs]
            vp = v_ref[0, :, cs]
            qcat = jnp.concatenate([qc_ref[0, :, cs], qn_ref[0, :, cs]], axis=0)
            dcat = jnp.concatenate([dc_ref[0, :, cs], dn_ref[0, :, cs]], axis=0)
            rk, rv = [], []
            for e in (0, 1):
                h = 2 * p + e
                sel = lo if e == 0 else hi
                ke = jnp.where(sel, kp, jnp.zeros_like(kp))
                ve = jnp.where(sel, vp, jnp.zeros_like(vp))
                lrow = jnp.concatenate([lc_ref[0, h:h + 1, :], ln_ref[0, h:h + 1, :]], axis=1)
                erow = jnp.concatenate([ec_ref[0, h:h + 1, :], en_ref[0, h:h + 1, :]], axis=1)
                st = jnp.where(mask, _nt(ke, qcat) - lrow, NEG)
                pt = jnp.exp(st)
                dst = pt * (_nt(ve, dcat) - erow)
                rk.append(_nn(dst.astype(BF16), qcat))
                rv.append(_nn(pt.astype(BF16), dcat))
            dk_ref[0, :, cs] = jnp.where(lo, rk[0], rk[1])
            dv_ref[0, :, cs] = jnp.where(lo, rv[0], rv[1])

    return pl.pallas_call(
        body, name="dil_bwd_dkv", grid=(3, nb),
        in_specs=[cur, cur, cur, nxt, cur, nxt, rcur, rnxt, rcur, rnxt], out_specs=[cur, cur],
        out_shape=[jax.ShapeDtypeStruct((3, t_all, WIDTH), F32)] * 2, compiler_params=_params(),
    )(ks, vs, qs, qs, dos, dos, lse_rows, lse_rows, dl_rows, dl_rows)


def _mix_out(oa, o3, l3, gn_a, gn_b, w_out, x, ada3, ln_g, ln_b, seq):
    t_all = x.shape[0]
    tm = 256
    nts = seq // tm

    def body(oa_ref, o1_ref, o2_ref, o3_ref, l1_ref, l2_ref, l3_ref, ga_ref, gb_ref, w_ref, x_ref, ada_ref, g_ref,
             b_ref, ob_ref, lse_ref, mg_ref, mix_ref, xh_ref, rs_ref, h2_ref):
        e, et = _head_mats()
        la, lb, lc = l1_ref[...], l2_ref[...], l3_ref[...]
        mx = jnp.maximum(jnp.maximum(la, lb), lc)
        ea, eb, ec = jnp.exp(la - mx), jnp.exp(lb - mx), jnp.exp(lc - mx)
        tot = ea + eb + ec
        lse_ref[...] = mx + jnp.log(tot)
        ob = (o1_ref[...] * _hexp(ea / tot, e) + o2_ref[...] * _hexp(eb / tot, e) + o3_ref[...] * _hexp(ec / tot, e))
        ob_ref[...] = ob

        def rms(o, gain):
            rr = lax.rsqrt(_hsum(o * o, et) * (1.0 / HEAD_DIM) + RMS_EPS)
            return o * _hexp(rr, e) * gain

        merged = jnp.concatenate([rms(oa_ref[...], ga_ref[...]), rms(ob, gb_ref[...])], axis=1).astype(BF16)
        mg_ref[...] = merged
        mix = _nn(merged, w_ref[...])
        mix_ref[...] = mix.astype(BF16)
        r1 = ALPHA * x_ref[...] + ada_ref[0, 2:3, :] * mix
        d = r1 - jnp.mean(r1, axis=1, keepdims=True)
        rstd = lax.rsqrt(jnp.mean(d * d, axis=1, keepdims=True) + LN_EPS)
        xh = d * rstd
        xh_ref[...] = xh
        rs_ref[...] = jnp.broadcast_to(rstd, (tm, LANES))
        x1 = xh * g_ref[...] + b_ref[...]
        h2_ref[...] = (x1 * (1.0 + ada_ref[0, 4:5, :]) + ada_ref[0, 3:4, :]).astype(BF16)

    tok = lambda w: pl.BlockSpec((tm, w), lambda i: (i, 0))
    vec = lambda w: pl.BlockSpec((1, w), lambda i: (0, 0))
    return pl.pallas_call(
        body, name="mix_out", grid=(t_all // tm,),
        in_specs=[tok(WIDTH)] * 4 + [tok(LANES)] * 3 + [vec(WIDTH), vec(WIDTH),
                  pl.BlockSpec(w_out.shape, lambda i: (0, 0)), tok(D_MODEL),
                  pl.BlockSpec((1, 6, D_MODEL), lambda i: (i // nts, 0, 0)), vec(D_MODEL), vec(D_MODEL)],
        out_specs=[tok(WIDTH), tok(LANES), tok(D_MODEL), tok(D_MODEL), tok(D_MODEL), tok(LANES), tok(D_MODEL)],
        out_shape=[jax.ShapeDtypeStruct((t_all, WIDTH), F32), jax.ShapeDtypeStruct((t_all, LANES), F32),
                   jax.ShapeDtypeStruct((t_all, D_MODEL), BF16), jax.ShapeDtypeStruct((t_all, D_MODEL), BF16),
                   jax.ShapeDtypeStruct((t_all, D_MODEL), F32), jax.ShapeDtypeStruct((t_all, LANES), F32),
                   jax.ShapeDtypeStruct((t_all, D_MODEL), BF16)],
        compiler_params=_params(),
    )(oa, o3[0], o3[1], o3[2], l3[0], l3[1], l3[2], gn_a, gn_b, w_out, x, ada3, ln_g, ln_b)


def _mix_out_bwd(dmix, w_out, oa, ob, gn_a, gn_b):
    t_all = dmix.shape[0]
    tm = 256

    def body(dm_ref, w_ref, oa_ref, ob_ref, ga_ref, gb_ref, doa_ref, dob_ref, dla_ref, dlb_ref, acc_ref):
        @pl.when(pl.program_id(0) == 0)
        def _():
            acc_ref[...] = jnp.zeros_like(acc_ref)
        e, et = _head_mats()
        dmg = _nt(dm_ref[...], w_ref[...])

        def group(o, dn, gain):
            rr = lax.rsqrt(_hsum(o * o, et) * (1.0 / HEAD_DIM) + RMS_EPS)
            re = _hexp(rr, e)
            dgain = jnp.sum(dn * o * re, axis=0, keepdims=True)
            dxn = dn * gain
            tt = _hsum(dxn * o, et) * (rr * rr * rr) * (1.0 / HEAD_DIM)
            do = re * dxn - o * _hexp(tt, e)
            return do, _hsum(do * o, et), dgain

        doa, dla, dga = group(oa_ref[...], dmg[:, :WIDTH], ga_ref[...])
        dob, dlb, dgb = group(ob_ref[...], dmg[:, WIDTH:], gb_ref[...])
        doa_ref[...] = doa.astype(BF16)
        dob_ref[...] = dob.astype(BF16)
        dla_ref[...] = dla
        dlb_ref[...] = dlb
        acc_ref[0:1, :] += jnp.concatenate([dga, dgb], axis=1)

    tok = lambda w: pl.BlockSpec((tm, w), lambda i: (i, 0))
    vec = lambda w: pl.BlockSpec((1, w), lambda i: (0, 0))
    return pl.pallas_call(
        body, name="mix_out_bwd", grid=(t_all // tm,),
        in_specs=[tok(D_MODEL), pl.BlockSpec(w_out.shape, lambda i: (0, 0)), tok(WIDTH), tok(WIDTH), vec(WIDTH),
                  vec(WIDTH)],
        out_specs=[tok(WIDTH), tok(WIDTH), tok(LANES), tok(LANES), pl.BlockSpec((8, D_MODEL), lambda i: (0, 0))],
        out_shape=[jax.ShapeDtypeStruct((t_all, WIDTH), BF16), jax.ShapeDtypeStruct((t_all, WIDTH), BF16),
                   jax.ShapeDtypeStruct((t_all, LANES), F32), jax.ShapeDtypeStruct((t_all, LANES), F32),
                   jax.ShapeDtypeStruct((8, D_MODEL), F32)],
        compiler_params=_params(),
    )(dmix, w_out, oa, ob, gn_a, gn_b)


def _inproj_bwd(dza, dqb, dkb, dvb, dfa16, pos, wqkv, wf16, freq, dr1, x, ada3, seq):
    t_all = x.shape[0]
    tm = 256
    nts = seq // tm
    nbat = t_all // seq

    def body(dza_ref, dqb_ref, dkb_ref, dvb_ref, dfa_ref, pos_ref, w_ref, wf_ref, fr_ref, dr1_ref, x_ref, ada_ref,
             gx_ref, dz_ref, acc_ref):
        i = pl.program_id(0)

        @pl.when(i == 0)
        def _():
            acc_ref[...] = jnp.zeros_like(acc_ref)
        tabs = _rope_tabs(pos_ref, fr_ref, -1.0)
        dz_ref[:, :3 * WIDTH] = dza_ref[...]
        dz_ref[:, 3 * WIDTH:4 * WIDTH] = _rope(dqb_ref[...], tabs).astype(BF16)
        dz_ref[:, 4 * WIDTH:5 * WIDTH] = _rope(dkb_ref[...], tabs).astype(BF16)
        dz_ref[:, 5 * WIDTH:] = dvb_ref[...].astype(BF16)
        dh1 = _tn(dfa_ref[...], wf_ref[...])
        for n in range(6):
            cs = slice(n * WIDTH, (n + 1) * WIDTH)
            dh1 = dh1 + _nt(dz_ref[:, cs], w_ref[:, cs])
        xv = x_ref[...]
        gx_ref[...] = ALPHA * dr1_ref[...] + dh1 * (1.0 + ada_ref[0, 1:2, :])
        b = i // nts
        acc_ref[pl.ds(b, 1), :] += jnp.sum(dh1 * xv, axis=0, keepdims=True)
        acc_ref[pl.ds(8 + b, 1), :] += jnp.sum(dh1, axis=0, keepdims=True)

    tok = lambda w: pl.BlockSpec((tm, w), lambda i: (i, 0))
    return pl.pallas_call(
        body, name="inproj_bwd", grid=(t_all // tm,),
        in_specs=[tok(3 * WIDTH), tok(WIDTH), tok(WIDTH), tok(WIDTH), pl.BlockSpec((16, tm), lambda i: (0, i)),
                  tok(1), pl.BlockSpec(wqkv.shape, lambda i: (0, 0)), pl.BlockSpec(wf16.shape, lambda i: (0, 0)),
                  pl.BlockSpec((1, LANES), lambda i: (0, 0)), tok(D_MODEL), tok(D_MODEL),
                  pl.BlockSpec((1, 6, D_MODEL), lambda i: (i // nts, 0, 0))],
        out_specs=[tok(D_MODEL), tok(6 * WIDTH), pl.BlockSpec((16, D_MODEL), lambda i: (0, 0))],
        out_shape=[jax.ShapeDtypeStruct((t_all, D_MODEL), F32), jax.ShapeDtypeStruct((t_all, 6 * WIDTH), BF16),
                   jax.ShapeDtypeStruct((16, D_MODEL), F32)],
        compiler_params=_params(),
    )(dza, dqb, dkb, dvb, dfa16, pos, wqkv, wf16, freq, dr1, x, ada3)


FFN_TM = 512
FFN_TN = 256
HALO = 8


def _conv(cat_ref, w_ref, b_ref, rows):
    return (b_ref[...] + w_ref[0:1, :] * cat_ref[pl.ds(HALO - 2, rows), :] + w_ref[1:2, :] * cat_ref[pl.ds(HALO - 1, rows), :]
            + w_ref[2:3, :] * cat_ref[pl.ds(HALO, rows), :])


def _ffn_gate(u, conv_w, conv_b, seq):
    t_all = u.shape[0]
    tm, tn = FFN_TM, FFN_TN
    nc = D_FF // tn
    nts = seq // tm

    def body(ua_ref, uap_ref, ug_ref, ugp_ref, wa_ref, wg_ref, ba_ref, bg_ref, o_ref, ca_ref, cg_ref):
        first = (pl.program_id(0) % nts) == 0
        zero = jnp.zeros((HALO, tn), F32)
        ca_ref[0:HALO, :] = jnp.where(first, zero, uap_ref[...])
        cg_ref[0:HALO, :] = jnp.where(first, zero, ugp_ref[...])
        ca_ref[HALO:, :] = ua_ref[...]
        cg_ref[HALO:, :] = ug_ref[...]
        ya = _conv(ca_ref, wa_ref, ba_ref, tm)
        yg = _conv(cg_ref, wg_ref, bg_ref, tm)
        o_ref[...] = (yg * jax.nn.sigmoid(yg) * ya).astype(BF16)

    cur = lambda off: pl.BlockSpec((tm, tn), lambda t, n: (t, n + off))
    prev = lambda off: pl.BlockSpec((HALO, tn), lambda t, n: (jnp.maximum(t * (tm // HALO) - 1, 0), n + off))
    vec = lambda r, off: pl.BlockSpec((r, tn), lambda t, n: (0, n + off))
    return pl.pallas_call(
        body, name="ffn_gate", grid=(t_all // tm, nc),
        in_specs=[cur(0), prev(0), cur(nc), prev(nc), vec(3, 0), vec(3, nc), vec(1, 0), vec(1, nc)],
        out_specs=pl.BlockSpec((tm, tn), lambda t, n: (t, n)),
        out_shape=jax.ShapeDtypeStruct((t_all, D_FF), BF16),
        scratch_shapes=[pltpu.VMEM((tm + HALO, tn), F32)] * 2, compiler_params=_params(),
    )(u, u, u, u, conv_w, conv_w, conv_b, conv_b)


def _ffn_gate_bwd(u, dfi, conv_w, conv_b, seq):
    t_all = u.shape[0]
    tm, tn = FFN_TM, FFN_TN
    nc = D_FF // tn
    nts = seq // tm
    ext = tm + HALO

    def body(ua_ref, uap_ref, uan_ref, ug_ref, ugp_ref, ugn_ref, df_ref, dfn_ref, wa_ref, wg_ref, ba_ref, bg_ref,
             dua_ref, dug_ref, acca_ref, accg_ref, ca_ref, cg_ref, ya_ref, yg_ref):
        t = pl.program_id(1)
        first = (t % nts) == 0
        last = (t % nts) == nts - 1

        @pl.when(t == 0)
        def _():
            acca_ref[...] = jnp.zeros_like(acca_ref)
            accg_ref[...] = jnp.zeros_like(accg_ref)
        zero = jnp.zeros((HALO, tn), F32)
        for cat, cur, prv, nxt in ((ca_ref, ua_ref, uap_ref, uan_ref), (cg_ref, ug_ref, ugp_ref, ugn_ref)):
            cat[0:HALO, :] = jnp.where(first, zero, prv[...])
            cat[HALO:HALO + tm, :] = cur[...]
            cat[HALO + tm:, :] = nxt[...]
        ya = _conv(ca_ref, wa_ref, ba_ref, ext)
        yg = _conv(cg_ref, wg_ref, bg_ref, ext)
        dfe = jnp.concatenate([df_ref[...].astype(F32), dfn_ref[0:HALO, :].astype(F32)], axis=0)
        row = lax.broadcasted_iota(jnp.int32, (ext, tn), 0)
        dfe = jnp.where((row < tm) | jnp.logical_not(last), dfe, 0.0)
        sg = jax.nn.sigmoid(yg)
        ya_ref[...] = dfe * (yg * sg)
        yg_ref[...] = dfe * ya * (sg * (1.0 + yg * (1.0 - sg)))
        for dy, cat, w_ref, du_ref, acc in ((ya_ref, ca_ref, wa_ref, dua_ref, acca_ref),
                                             (yg_ref, cg_ref, wg_ref, dug_ref, accg_ref)):
            d0 = dy[0:tm, :]
            du = w_ref[2:3, :] * d0 + w_ref[1:2, :] * dy[pl.ds(1, tm), :] + w_ref[0:1, :] * dy[pl.ds(2, tm), :]
            du_ref[...] = du.astype(BF16)
            for k in range(3):
                acc[k:k + 1, :] += jnp.sum(d0 * cat[pl.ds(HALO - 2 + k, tm), :], axis=0, keepdims=True)
            acc[3:4, :] += jnp.sum(d0, axis=0, keepdims=True)

    nrow = t_all // HALO
    cur = lambda off: pl.BlockSpec((tm, tn), lambda n, t: (t, n + off))
    prev = lambda off: pl.BlockSpec((HALO, tn), lambda n, t: (jnp.maximum(t * (tm // HALO) - 1, 0), n + off))
    nxt = lambda off: pl.BlockSpec((HALO, tn), lambda n, t: (jnp.minimum((t + 1) * (tm // HALO), nrow - 1), n + off))
    vec = lambda r, off: pl.BlockSpec((r, tn), lambda n, t: (0, n + off))
    dcur = pl.BlockSpec((tm, tn), lambda n, t: (t, n))
    dnxt = pl.BlockSpec((16, tn), lambda n, t: (jnp.minimum((t + 1) * (tm // 16), t_all // 16 - 1), n))
    acc = pl.BlockSpec((8, tn), lambda n, t: (0, n))
    return pl.pallas_call(
        body, name="ffn_gate_bwd", grid=(nc, t_all // tm),
        in_specs=[cur(0), prev(0), nxt(0), cur(nc), prev(nc), nxt(nc), dcur, dnxt, vec(3, 0), vec(3, nc), vec(1, 0),
                  vec(1, nc)],
        out_specs=[dcur, dcur, acc, acc],
        out_shape=[jax.ShapeDtypeStruct((t_all, D_FF), BF16), jax.ShapeDtypeStruct((t_all, D_FF), BF16),
                   jax.ShapeDtypeStruct((8, D_FF), F32), jax.ShapeDtypeStruct((8, D_FF), F32)],
        scratch_shapes=[pltpu.VMEM((tm + 2 * HALO, tn), F32)] * 2 + [pltpu.VMEM((ext, tn), F32)] * 2,
        compiler_params=_params(),
    )(u, u, u, u, u, u, dfi, dfi, conv_w, conv_w, conv_b, conv_b)


def _ffn_down(ffn_in, w_down, xh1, ln1_g, ln1_b, ada3, ln2_g, ln2_b, target, seq):
    t_all = xh1.shape[0]
    tm = 256
    nts = seq // tm

    def body(f_ref, w_ref, xh_ref, g1_ref, b1_ref, ada_ref, g2_ref, b2_ref, tg_ref, dr2_ref, acc_ref):
        i = pl.program_id(0)

        @pl.when(i == 0)
        def _():
            acc_ref[...] = jnp.zeros_like(acc_ref)
        ffn = _nn(f_ref[...], w_ref[...])
        x1 = xh_ref[...] * g1_ref[...] + b1_ref[...]
        r2 = ALPHA * x1 + ada_ref[0, 5:6, :] * ffn
        d = r2 - jnp.mean(r2, axis=1, keepdims=True)
        rstd = lax.rsqrt(jnp.mean(d * d, axis=1, keepdims=True) + LN_EPS)
        xh2 = d * rstd
        diff = xh2 * g2_ref[...] + b2_ref[...] - tg_ref[...]
        dy = diff * (1.0 / D_MODEL)
        dr2 = _layer_norm_bwd(dy * g2_ref[...], xh2, rstd)
        dr2_ref[...] = dr2
        acc_ref[0:1, :] += jnp.sum(dy * xh2, axis=0, keepdims=True)
        acc_ref[1:2, :] += jnp.sum(dy, axis=0, keepdims=True)
        acc_ref[2:3, :] += jnp.sum(diff * diff, axis=0, keepdims=True) * (0.5 / D_MODEL)
        acc_ref[pl.ds(8 + i // nts, 1), :] += jnp.sum(dr2 * ffn, axis=0, keepdims=True)

    tok = lambda w: pl.BlockSpec((tm, w), lambda i: (i, 0))
    vec = pl.BlockSpec((1, D_MODEL), lambda i: (0, 0))
    return pl.pallas_call(
        body, name="ffn_down", grid=(t_all // tm,),
        in_specs=[tok(D_FF), pl.BlockSpec(w_down.shape, lambda i: (0, 0)), tok(D_MODEL), vec, vec,
                  pl.BlockSpec((1, 6, D_MODEL), lambda i: (i // nts, 0, 0)), vec, vec, tok(D_MODEL)],
        out_specs=[tok(D_MODEL), pl.BlockSpec((16, D_MODEL), lambda i: (0, 0))],
        out_shape=[jax.ShapeDtypeStruct((t_all, D_MODEL), F32), jax.ShapeDtypeStruct((16, D_MODEL), F32)],
        compiler_params=_params(),
    )(ffn_in, w_down, xh1, ln1_g, ln1_b, ada3, ln2_g, ln2_b, target)


def _ffn_down_bwd(dr2, ada3, w_down, seq):
    t_all = dr2.shape[0]
    tm = 256
    nts = seq // tm

    def body(d_ref, ada_ref, w_ref, dffn_ref, dfi_ref):
        dffn = (d_ref[...] * ada_ref[0, 5:6, :]).astype(BF16)
        dffn_ref[...] = dffn
        dfi_ref[...] = _nt(dffn, w_ref[...]).astype(BF16)

    tok = lambda w: pl.BlockSpec((tm, w), lambda i: (i, 0))
    return pl.pallas_call(
        body, name="ffn_down_bwd", grid=(t_all // tm,),
        in_specs=[tok(D_MODEL), pl.BlockSpec((1, 6, D_MODEL), lambda i: (i // nts, 0, 0)),
                  pl.BlockSpec(w_down.shape, lambda i: (0, 0))],
        out_specs=[tok(D_MODEL), tok(D_FF)],
        out_shape=[jax.ShapeDtypeStruct((t_all, D_MODEL), BF16), jax.ShapeDtypeStruct((t_all, D_FF), BF16)],
        compiler_params=_params(),
    )(dr2, ada3, w_down)


def _ffn_up_bwd(du_a, du_g, w_up, dr2, xh1, rs1, mix, ada3, ln1_g, ln1_b, seq):
    t_all = dr2.shape[0]
    tm = 256
    nts = seq // tm

    def body(da_ref, dg_ref, w_ref, dr2_ref, xh_ref, rs_ref, mix_ref, ada_ref, g_ref, b_ref, dr1_ref, dmix_ref,
             acc_ref):
        i = pl.program_id(0)

        @pl.when(i == 0)
        def _():
            acc_ref[...] = jnp.zeros_like(acc_ref)
        dh2 = _nt(da_ref[...], w_ref[:, :D_FF]) + _nt(dg_ref[...], w_ref[:, D_FF:])
        xh = xh_ref[...]
        x1 = xh * g_ref[...] + b_ref[...]
        dx1 = ALPHA * dr2_ref[...] + dh2 * (1.0 + ada_ref[0, 4:5, :])
        dr1 = _layer_norm_bwd(dx1 * g_ref[...], xh, rs_ref[:, 0:1])
        dr1_ref[...] = dr1
        dmix_ref[...] = (dr1 * ada_ref[0, 2:3, :]).astype(BF16)
        b = i // nts
        acc_ref[0:1, :] += jnp.sum(dx1 * xh, axis=0, keepdims=True)
        acc_ref[1:2, :] += jnp.sum(dx1, axis=0, keepdims=True)
        acc_ref[pl.ds(8 + b, 1), :] += jnp.sum(dh2 * x1, axis=0, keepdims=True)
        acc_ref[pl.ds(16 + b, 1), :] += jnp.sum(dh2, axis=0, keepdims=True)
        acc_ref[pl.ds(24 + b, 1), :] += jnp.sum(dr1 * mix_ref[...].astype(F32), axis=0, keepdims=True)

    tok = lambda w: pl.BlockSpec((tm, w), lambda i: (i, 0))
    vec = pl.BlockSpec((1, D_MODEL), lambda i: (0, 0))
    return pl.pallas_call(
        body, name="ffn_up_bwd", grid=(t_all // tm,),
        in_specs=[tok(D_FF), tok(D_FF), pl.BlockSpec(w_up.shape, lambda i: (0, 0)), tok(D_MODEL), tok(D_MODEL),
                  tok(LANES), tok(D_MODEL), pl.BlockSpec((1, 6, D_MODEL), lambda i: (i // nts, 0, 0)), vec, vec],
        out_specs=[tok(D_MODEL), tok(D_MODEL), pl.BlockSpec((32, D_MODEL), lambda i: (0, 0))],
        out_shape=[jax.ShapeDtypeStruct((t_all, D_MODEL), F32), jax.ShapeDtypeStruct((t_all, D_MODEL), BF16),
                   jax.ShapeDtypeStruct((32, D_MODEL), F32)],
        compiler_params=_params(),
    )(du_a, du_g, w_up, dr2, xh1, rs1, mix, ada3, ln1_g, ln1_b)


def _perm(a, d, seq):
    if d == 1:
        return a
    t_all, w = a.shape
    return a.reshape(t_all // seq, seq // d, d, w).transpose(0, 2, 1, 3).reshape(t_all, w)


def _unperm(a, d, seq):
    if d == 1:
        return a
    t_all, w = a.shape
    return a.reshape(t_all // seq, d, seq // d, w).transpose(0, 2, 1, 3).reshape(t_all, w)


DILATIONS = (1, 4, 16)


def _stack_perm(a, seq):
    return jnp.stack([_perm(a, d, seq) for d in DILATIONS])


def _rows(a):
    return a[:, :N_HEADS].T


def _rope_freq():
    f = np.float32(ROPE_THETA) ** (-np.arange(0, ROPE_DIMS, 2, dtype=np.float32) / np.float32(ROPE_DIMS))
    return jnp.asarray(np.tile(f.astype(np.float32), LANES // (ROPE_DIMS // 2))[None, :])


def _local_step(x, c, positions, target, w_ada, b_ada, w_in, b_fgate, gn_a, gn_b, ln1_g, ln1_b, conv_b, ln2_g, ln2_b,
                late_shards):
    nbat, seq, _ = x.shape
    t_all = nbat * seq
    xf = x.reshape(t_all, D_MODEL)
    tg = target.reshape(t_all, D_MODEL)
    pos = positions.reshape(t_all, 1)
    freq = _rope_freq()
    c16 = jnp.zeros((16, D_MODEL), F32).at[:nbat].set(c)

    wqkv = jnp.concatenate([w_in[:, :3 * WIDTH], w_in[:, 3 * WIDTH + N_HEADS:]], axis=1)
    wf16 = jnp.zeros((16, D_MODEL), BF16).at[:N_HEADS].set(w_in[:, 3 * WIDTH:3 * WIDTH + N_HEADS].T)
    bf = b_fgate.reshape(N_HEADS, 1)

    ada = _ada_fwd(c16, w_ada, b_ada)
    ada3 = ada[:nbat].reshape(nbat, 6, D_MODEL)
    h1, za, zb, fa_t = _inproj(xf, ada3, pos, wqkv, wf16, freq, seq)
    f_row = _fgate_fwd(fa_t, bf, seq)
    f_col = jnp.zeros((t_all, LANES), F32).at[:, :N_HEADS].set(f_row.T)
    vt = za[:, 2 * WIDTH:].reshape(nbat, seq, WIDTH).transpose(0, 2, 1).reshape(nbat * WIDTH, seq)
    oa, lse_row_a, gathered = _fox_fwd(za, vt, f_col, seq, [late_shards[n] for n in LATE])
    w_out, w_up, conv_w, w_down = (_full_from_gathered(n, g) for n, g in zip(LATE, gathered))
    lse_a = jnp.zeros((t_all, LANES), F32).at[:, :N_HEADS].set(lse_row_a.T)
    qs = _stack_perm(zb[:, :WIDTH], seq)
    ks = _stack_perm(zb[:, WIDTH:2 * WIDTH], seq)
    vs = _stack_perm(zb[:, 2 * WIDTH:], seq)
    o3p, l3p = _dil_fwd(qs, ks, vs, seq)
    o3 = [_unperm(o3p[p], d, seq) for p, d in enumerate(DILATIONS)]
    l3 = [_unperm(l3p[p], d, seq) for p, d in enumerate(DILATIONS)]
    ob, lse_b, merged, mix, xh1, rs1, h2 = _mix_out(oa, o3, l3, gn_a, gn_b, w_out, xf, ada3, ln1_g, ln1_b, seq)
    u = _matmul(h2, w_up, False, F32, 256, 512, "ffn_up")
    ffn_in = _ffn_gate(u, conv_w, conv_b, seq)
    dr2, acc2 = _ffn_down(ffn_in, w_down, xh1, ln1_g, ln1_b, ada3, ln2_g, ln2_b, tg, seq)

    dffn, dfi = _ffn_down_bwd(dr2, ada3, w_down, seq)
    d_w_down = _matmul_tn(ffn_in, dffn, 512, 512, "dw_down")
    du_a, du_g, acc_ca, acc_cg = _ffn_gate_bwd(u, dfi, conv_w, conv_b, seq)
    dr1, dmix, acc1 = _ffn_up_bwd(du_a, du_g, w_up, dr2, xh1, rs1, mix, ada3, ln1_g, ln1_b, seq)
    d_w_up = jnp.concatenate([_matmul_tn(h2, du_a, 256, 512, "dw_up_a"), _matmul_tn(h2, du_g, 256, 512, "dw_up_g")],
                             axis=1)

    doa, dob, dl_a, dl_b, acc_gn = _mix_out_bwd(dmix, w_out, oa, ob, gn_a, gn_b)
    d_w_out = _matmul_tn(merged, dmix, 512, 512, "dw_out")
    dqa, df_q = _fox_bwd_dq(za, doa, f_row, lse_a, dl_a, seq)
    late_grads = dict(w_out=d_w_out, w_up=d_w_up, conv_w=jnp.concatenate([acc_ca[0:3], acc_cg[0:3]], axis=1),
                      w_down=d_w_down)
    dka, dva, df_k, late_parts = _fox_bwd_dkv(za, doa, f_col, lse_row_a, _rows(dl_a), seq,
                                              [_payload(n, _dest_major(n, late_grads[n])) for n in LATE])
    dfa_t, dbf = _fgate_bwd(_rows(df_q + df_k), fa_t, bf, seq)
    dos = _stack_perm(dob, seq)
    lses = _stack_perm(lse_b, seq)
    dls = _stack_perm(dl_b, seq)
    dq3 = _dil_bwd_dq(qs, ks, vs, dos, lses, dls, seq)
    lse_rows = lses[:, :, :N_HEADS].transpose(0, 2, 1)
    dl_rows = dls[:, :, :N_HEADS].transpose(0, 2, 1)
    dk3, dv3 = _dil_bwd_dkv(qs, ks, vs, dos, lse_rows, dl_rows, seq)
    unsum = lambda a3: sum(_unperm(a3[p], d, seq) for p, d in enumerate(DILATIONS))
    dza = jnp.concatenate([dqa, dka, dva], axis=1)
    dfa16 = jnp.zeros((16, t_all), BF16).at[:N_HEADS].set(dfa_t.astype(BF16))
    grad_x, dz, acc0 = _inproj_bwd(dza, unsum(dq3), unsum(dk3), unsum(dv3), dfa16, pos, wqkv, wf16, freq, dr1, xf,
                                   ada3, seq)
    d_wqkv = _matmul_tn(h1, dz, 512, 512, "dw_in")
    d_wf = _matmul_rows(dfa16, h1, 512, "dw_fgate")[:N_HEADS].T
    d_w_in = jnp.concatenate([d_wqkv[:, :3 * WIDTH], d_wf, d_wqkv[:, 3 * WIDTH:]], axis=1)

    dada = jnp.concatenate([acc0[8:8 + nbat], acc0[:nbat], acc1[24:24 + nbat], acc1[16:16 + nbat], acc1[8:8 + nbat],
                            acc2[8:8 + nbat]], axis=1)
    dada16 = jnp.zeros((16, 6 * D_MODEL), F32).at[:nbat].set(dada)
    d_w_ada = _ada_bwd(c16, dada16)

    grads = dict(
        w_ada=d_w_ada, b_ada=jnp.sum(dada, axis=0, keepdims=True), w_in=d_w_in, b_fgate=dbf[:, 0][None, :],
        gn_a=acc_gn[0:1, :WIDTH], gn_b=acc_gn[0:1, WIDTH:], ln1_g=acc1[0:1], ln1_b=acc1[1:2],
        conv_b=jnp.concatenate([acc_ca[3:4], acc_cg[3:4]], axis=1), ln2_g=acc2[0:1], ln2_b=acc2[1:2])
    return acc2[2:3], grad_x.reshape(x.shape), grads, dict(zip(LATE, late_parts))


EARLY = ("w_ada", "w_in")
LATE = ("w_out", "w_up", "conv_w", "w_down")
BIG = EARLY + LATE
COLUMN_SHARDED = ("w_ada", "w_in", "w_up", "conv_w")


def _payload(name, a):
    return a if name == "conv_w" else a.astype(BF16)
SMALL = ("b_ada", "b_fgate", "gn_a", "gn_b", "ln1_g", "ln1_b", "conv_b", "ln2_g", "ln2_b")
ADAM_ROWS = dict(w_ada=256, w_in=256, w_out=128, w_up=256, conv_w=3, w_down=176)
SMALL_ROWS = 24


def _full_from_gathered(name, g):
    if name in COLUMN_SHARDED:
        return g.transpose(1, 0, 2).reshape(g.shape[1], N_DEV * g.shape[2])
    return g.reshape(N_DEV * g.shape[1], g.shape[2])


def _dest_major(name, full):
    if name in COLUMN_SHARDED:
        r, cfull = full.shape
        return full.reshape(r, N_DEV, cfull // N_DEV).transpose(1, 0, 2)
    return full.reshape(N_DEV, full.shape[0] // N_DEV, full.shape[1])


def _pack_small(vals, extra=None):
    parts = [vals[n].reshape(-1) for n in SMALL]
    if extra is not None:
        parts.append(extra.reshape(-1))
    flat = jnp.concatenate(parts)
    return jnp.pad(flat, (0, SMALL_ROWS * D_MODEL - flat.shape[0])).reshape(SMALL_ROWS, D_MODEL)


def _unpack_small(packed, like):
    flat = packed.reshape(-1)
    out, off = {}, 0
    for n in SMALL:
        size = like[n].size
        out[n] = flat[off:off + size].reshape(like[n].shape)
        off += size
    return out, flat[off:off + D_MODEL]


def kernel(x, c, positions, w_ada, b_ada, w_in, b_fgate, gn_a, gn_b, w_out, ln1_g, ln1_b, w_up, conv_w, conv_b, w_down, ln2_g, ln2_b, loss_target, m_w_ada, m_b_ada, m_w_in, m_b_fgate, m_gn_a, m_gn_b, m_w_out, m_ln1_g, m_ln1_b, m_w_up, m_conv_w, m_conv_b, m_w_down, m_ln2_g, m_ln2_b, v_w_ada, v_b_ada, v_w_in, v_b_fgate, v_gn_a, v_gn_b, v_w_out, v_ln1_g, v_ln1_b, v_w_up, v_conv_w, v_conv_b, v_w_down, v_ln2_g, v_ln2_b):
    w = dict(w_ada=w_ada[0], b_ada=b_ada, w_in=w_in[0], b_fgate=b_fgate, gn_a=gn_a, gn_b=gn_b, w_out=w_out[0],
             ln1_g=ln1_g, ln1_b=ln1_b, w_up=w_up[0], conv_w=conv_w[0], conv_b=conv_b, w_down=w_down[0], ln2_g=ln2_g,
             ln2_b=ln2_b)
    m = dict(w_ada=m_w_ada[0], b_ada=m_b_ada, w_in=m_w_in[0], b_fgate=m_b_fgate, gn_a=m_gn_a, gn_b=m_gn_b,
             w_out=m_w_out[0], ln1_g=m_ln1_g, ln1_b=m_ln1_b, w_up=m_w_up[0], conv_w=m_conv_w[0], conv_b=m_conv_b,
             w_down=m_w_down[0], ln2_g=m_ln2_g, ln2_b=m_ln2_b)
    v = dict(w_ada=v_w_ada[0], b_ada=v_b_ada, w_in=v_w_in[0], b_fgate=v_b_fgate, gn_a=v_gn_a, gn_b=v_gn_b,
             w_out=v_w_out[0], ln1_g=v_ln1_g, ln1_b=v_ln1_b, w_up=v_w_up[0], conv_w=v_conv_w[0], conv_b=v_conv_b,
             w_down=v_w_down[0], ln2_g=v_ln2_g, ln2_b=v_ln2_b)

    gathered = _exchange([_payload(n, w[n]) for n in EARLY], [True] * len(EARLY), "weight_gather")
    full = {n: _full_from_gathered(n, g) for n, g in zip(EARLY, gathered)}

    loss_lanes, grad_x, g_local, parts = _local_step(
        x, c, positions, loss_target, full["w_ada"], b_ada, full["w_in"], b_fgate, gn_a, gn_b, ln1_g, ln1_b, conv_b,
        ln2_g, ln2_b, {n: _payload(n, w[n]) for n in LATE})

    exchanged = _exchange([_payload(n, _dest_major(n, g_local[n])) for n in EARLY] + [_pack_small(g_local, loss_lanes)],
                          [False] * len(EARLY) + [True], "grad_exchange")
    parts.update(zip(EARLY, exchanged))

    grad, delta, new_m, new_v = {}, {}, {}, {}
    for n in BIG:
        grad[n], delta[n], new_m[n], new_v[n] = (
            a[None] for a in _adamw(parts[n], w[n], m[n], v[n], ADAM_ROWS[n], "adamw_" + n))
    packed = _adamw(exchanged[-1], _pack_small(w), _pack_small(m), _pack_small(v), SMALL_ROWS, "adamw_small")
    for dst, pk in zip((grad, delta, new_m, new_v), packed):
        vals, lanes = _unpack_small(pk, w)
        dst.update(vals)
        if dst is grad:
            loss = jnp.sum(lanes)

    order = ("w_ada", "b_ada", "w_in", "b_fgate", "gn_a", "gn_b", "w_out", "ln1_g", "ln1_b", "w_up", "conv_w", "conv_b",
             "w_down", "ln2_g", "ln2_b")
    return (loss, grad_x, *[grad[n] for n in order], *[delta[n] for n in order], *[new_m[n] for n in order],
            *[new_v[n] for n in order])
```

```python
import functools

import numpy as np
import jax
import jax.numpy as jnp
from jax import lax
from jax.experimental import pallas as pl
from jax.experimental.pallas import tpu as pltpu

F32, BF16 = jnp.float32, jnp.bfloat16
HIGHEST = lax.Precision.HIGHEST
MESH = pl.DeviceIdType.MESH
ANY = pl.BlockSpec(memory_space=pl.ANY)

D_MODEL = 1024
N_HEADS = 8
HEAD_DIM = 64
WIDTH = 512
D_FF = 2816
N_DEV = 8
ROPE_DIMS = 16
ROPE_THETA = 500000.0
ALPHA = 2.0 ** 0.25
LN_EPS = 1e-5
RMS_EPS = 1e-6
NEG = -1e30
Q_SCALE = 0.125
BLK = 128
LANES = 128
VMEM_LIMIT_BYTES = 56 * 1024 * 1024

ADAM_LR, ADAM_B1, ADAM_B2, ADAM_EPS, ADAM_WD, ADAM_STEP = 0.001, 0.9, 0.999, 1e-08, 0.01, 10


def _params(vmem=VMEM_LIMIT_BYTES):
    return pltpu.CompilerParams(vmem_limit_bytes=vmem)


def _nn(a, b):
    return jnp.dot(a, b, preferred_element_type=F32)


def _nt(a, b):
    return lax.dot_general(a, b, (((1,), (1,)), ((), ())), preferred_element_type=F32)


def _tn(a, b):
    return lax.dot_general(a, b, (((0,), (0,)), ((), ())), preferred_element_type=F32)


def _head_mats():
    r = lax.broadcasted_iota(jnp.int32, (LANES, WIDTH), 0)
    c = lax.broadcasted_iota(jnp.int32, (LANES, WIDTH), 1)
    e = ((c >> 6) == r).astype(F32)
    r2 = lax.broadcasted_iota(jnp.int32, (WIDTH, LANES), 0)
    c2 = lax.broadcasted_iota(jnp.int32, (WIDTH, LANES), 1)
    et = ((r2 >> 6) == c2).astype(F32)
    return e, et


def _hexp(w, e):
    return jnp.dot(w, e, precision=HIGHEST, preferred_element_type=F32)


def _hsum(x, et):
    return jnp.dot(x, et, precision=HIGHEST, preferred_element_type=F32)


def _rope_tabs(pos_ref, fr_ref, sign):
    ang = pos_ref[...].astype(F32) * fr_ref[...]
    lane = lax.broadcasted_iota(jnp.int32, ang.shape, 1) & (HEAD_DIM - 1)
    m1 = lane < ROPE_DIMS // 2
    m2 = (lane >= ROPE_DIMS // 2) & (lane < ROPE_DIMS)
    cos = jnp.cos(ang)
    sin = jnp.sin(ang) * sign
    return (jnp.where(m1 | m2, cos, 1.0), jnp.where(m1, -sin, 0.0), jnp.where(m2, sin, 0.0))


def _rope(z, tabs):
    c, s1, s2 = tabs
    parts = []
    for p in range(z.shape[1] // LANES):
        zp = z[:, LANES * p:LANES * (p + 1)]
        parts.append(zp * c + pltpu.roll(zp, LANES - 8, 1) * s1 + pltpu.roll(zp, 8, 1) * s2)
    return jnp.concatenate(parts, axis=1)


def _half_masks(rows):
    lane = lax.broadcasted_iota(jnp.int32, (rows, LANES), 1)
    lo = lane < HEAD_DIM
    return lo, jnp.logical_not(lo)


def _layer_norm_bwd(dxh, xh, rstd):
    m1 = jnp.mean(dxh, axis=1, keepdims=True)
    m2 = jnp.mean(dxh * xh, axis=1, keepdims=True)
    return rstd * (dxh - m1 - xh * m2)


def _coords():
    return lax.axis_index("x"), lax.axis_index("y"), lax.axis_index("c")


def _peer(x, y, c, k):
    return (1 - x if k & 4 else x, 1 - y if k & 2 else y, 1 - c if k & 1 else c)


def _comm_sems(n):
    return [pltpu.SemaphoreType.DMA((N_DEV - 1, n)), pltpu.SemaphoreType.DMA((N_DEV - 1, n)),
            pltpu.SemaphoreType.DMA((n,))]


def _comm_copies(ins, outs, to_all, sems):
    send_sems, recv_sems, local_sems = sems
    x, y, c = _coords()
    me = 4 * x + 2 * y + c
    copies = [pltpu.make_async_copy(ins[t] if to_all[t] else ins[t].at[me], outs[t].at[me], local_sems.at[t])
              for t in range(len(ins))]
    for k in range(1, N_DEV):
        px, py, pc = _peer(x, y, c, k)
        dest = 4 * px + 2 * py + pc
        for t in range(len(ins)):
            copies.append(pltpu.make_async_remote_copy(
                src_ref=ins[t] if to_all[t] else ins[t].at[dest], dst_ref=outs[t].at[me],
                send_sem=send_sems.at[k - 1, t], recv_sem=recv_sems.at[k - 1, t],
                device_id=(px, py, pc), device_id_type=MESH))
    return copies


def _comm_out_shapes(ins, to_all):
    return [jax.ShapeDtypeStruct(((N_DEV,) + a.shape) if ta else a.shape, a.dtype) for a, ta in zip(ins, to_all)]


def _exchange(ins, to_all, name):
    n = len(ins)

    def body(*refs):
        copies = _comm_copies(refs[:n], refs[n:2 * n], to_all, refs[2 * n:])
        for cp in copies:
            cp.start()
        for cp in copies:
            cp.wait()

    return pl.pallas_call(
        body, name=name, out_shape=_comm_out_shapes(ins, to_all), in_specs=[ANY] * n, out_specs=[ANY] * n,
        scratch_shapes=_comm_sems(n),
    )(*ins)


def _adamw(parts, w, m, v, rows, name):
    n_parts, r_all, cols = parts.shape
    c1 = 1.0 - ADAM_B1 ** ADAM_STEP
    c2 = 1.0 - ADAM_B2 ** ADAM_STEP

    def body(p_ref, w_ref, m_ref, v_ref, g_ref, d_ref, mo_ref, vo_ref):
        g = p_ref[0].astype(F32)
        for s in range(1, n_parts):
            g = g + p_ref[s].astype(F32)
        mn = ADAM_B1 * m_ref[...] + (1.0 - ADAM_B1) * g
        vn = ADAM_B2 * v_ref[...] + (1.0 - ADAM_B2) * (g * g)
        m_hat = mn / c1
        v_hat = vn / c2
        g_ref[...] = g
        d_ref[...] = -ADAM_LR * (m_hat / (jnp.sqrt(v_hat) + ADAM_EPS) + ADAM_WD * w_ref[...])
        mo_ref[...] = mn
        vo_ref[...] = vn

    spec = pl.BlockSpec((rows, cols), lambda i: (i, 0))
    return pl.pallas_call(
        body, name=name, grid=(r_all // rows,),
        in_specs=[pl.BlockSpec((n_parts, rows, cols), lambda i: (0, i, 0)), spec, spec, spec],
        out_specs=[spec] * 4, out_shape=[jax.ShapeDtypeStruct((r_all, cols), F32)] * 4,
        compiler_params=_params(),
    )(parts, w, m, v)


def _matmul(a, w, transposed_w, out_dtype, tm, chunk, name):
    t_all, k = a.shape
    n = w.shape[0] if transposed_w else w.shape[1]

    def body(a_ref, w_ref, o_ref):
        av = a_ref[...]
        for j in range(n // chunk):
            cs = slice(j * chunk, (j + 1) * chunk)
            r = _nt(av, w_ref[cs, :]) if transposed_w else _nn(av, w_ref[:, cs])
            o_ref[:, cs] = r.astype(out_dtype)

    return pl.pallas_call(
        body, name=name, grid=(t_all // tm,),
        in_specs=[pl.BlockSpec((tm, k), lambda i: (i, 0)), pl.BlockSpec(w.shape, lambda i: (0, 0))],
        out_specs=pl.BlockSpec((tm, n), lambda i: (i, 0)),
        out_shape=jax.ShapeDtypeStruct((t_all, n), out_dtype), compiler_params=_params(),
    )(a, w)


def _matmul_tn(a, b, tn, tk, name):
    t_all, k1 = a.shape
    n = b.shape[1]

    def body(a_ref, b_ref, o_ref):
        @pl.when(pl.program_id(1) == 0)
        def _():
            o_ref[...] = jnp.zeros_like(o_ref)
        o_ref[...] += _tn(a_ref[...], b_ref[...])

    return pl.pallas_call(
        body, name=name, grid=(n // tn, t_all // tk),
        in_specs=[pl.BlockSpec((tk, k1), lambda j, t: (t, 0)), pl.BlockSpec((tk, tn), lambda j, t: (t, j))],
        out_specs=pl.BlockSpec((k1, tn), lambda j, t: (0, j)),
        out_shape=jax.ShapeDtypeStruct((k1, n), F32), compiler_params=_params(),
    )(a, b)


def _matmul_rows(a, b, tk, name):
    r, t_all = a.shape
    n = b.shape[1]

    def body(a_ref, b_ref, o_ref):
        @pl.when(pl.program_id(0) == 0)
        def _():
            o_ref[...] = jnp.zeros_like(o_ref)
        o_ref[...] += _nn(a_ref[...], b_ref[...])

    return pl.pallas_call(
        body, name=name, grid=(t_all // tk,),
        in_specs=[pl.BlockSpec((r, tk), lambda t: (0, t)), pl.BlockSpec((tk, n), lambda t: (t, 0))],
        out_specs=pl.BlockSpec((r, n), lambda t: (0, 0)),
        out_shape=jax.ShapeDtypeStruct((r, n), F32), compiler_params=_params(),
    )(a, b)


def _ada_fwd(c_all, w_ada, b_ada):
    whole = lambda a: pl.BlockSpec(a.shape, lambda j: (0, 0))

    def body(c_ref, w_ref, b_ref, o_ref):
        cv = c_ref[...]
        s = (cv * jax.nn.sigmoid(cv)).astype(BF16)
        o_ref[...] = _nn(s, w_ref[...].astype(BF16)) + b_ref[...]

    out = jax.ShapeDtypeStruct((c_all.shape[0], w_ada.shape[1]), F32)
    return pl.pallas_call(
        body, name="ada_fwd", grid=(1,), in_specs=[whole(c_all), whole(w_ada), whole(b_ada)], out_specs=whole(out),
        out_shape=out, compiler_params=_params(),
    )(c_all, w_ada, b_ada)


def _ada_bwd(c_all, dada):
    whole = lambda a: pl.BlockSpec(a.shape, lambda j: (0, 0))

    def body(c_ref, d_ref, o_ref):
        cv = c_ref[...]
        s = (cv * jax.nn.sigmoid(cv)).astype(BF16)
        o_ref[...] = _tn(s, d_ref[...].astype(BF16))

    out = jax.ShapeDtypeStruct((D_MODEL, dada.shape[1]), F32)
    return pl.pallas_call(
        body, name="ada_bwd", grid=(1,), in_specs=[whole(c_all), whole(dada)], out_specs=whole(out), out_shape=out,
        compiler_params=_params(),
    )(c_all, dada)


def _inproj(x, ada3, pos, wqkv, wf16, freq, seq):
    t_all = x.shape[0]
    tm = 256
    nts = seq // tm

    def body(x_ref, ada_ref, pos_ref, w_ref, wf_ref, fr_ref, h1_ref, za_ref, zb_ref, fa_ref):
        h1 = (x_ref[...] * (1.0 + ada_ref[0, 1:2, :]) + ada_ref[0, 0:1, :]).astype(BF16)
        h1_ref[...] = h1
        tabs = _rope_tabs(pos_ref, fr_ref, 1.0)
        for n in range(6):
            z = _nn(h1, w_ref[:, n * WIDTH:(n + 1) * WIDTH])
            if n in (3, 4):
                z = _rope(z, tabs)
            if n in (0, 3):
                z = z * Q_SCALE
            dst = za_ref if n < 3 else zb_ref
            dst[:, (n % 3) * WIDTH:(n % 3 + 1) * WIDTH] = z.astype(BF16)
        fa_ref[...] = _nt(wf_ref[...], h1)[:N_HEADS]

    tok = lambda w: pl.BlockSpec((tm, w), lambda i: (i, 0))
    return pl.pallas_call(
        body, name="inproj", grid=(t_all // tm,),
        in_specs=[tok(D_MODEL), pl.BlockSpec((1, 6, D_MODEL), lambda i: (i // nts, 0, 0)), tok(1),
                  pl.BlockSpec(wqkv.shape, lambda i: (0, 0)), pl.BlockSpec(wf16.shape, lambda i: (0, 0)),
                  pl.BlockSpec((1, LANES), lambda i: (0, 0))],
        out_specs=[tok(D_MODEL), tok(3 * WIDTH), tok(3 * WIDTH), pl.BlockSpec((N_HEADS, tm), lambda i: (0, i))],
        out_shape=[jax.ShapeDtypeStruct((t_all, D_MODEL), BF16), jax.ShapeDtypeStruct((t_all, 3 * WIDTH), BF16),
                   jax.ShapeDtypeStruct((t_all, 3 * WIDTH), BF16), jax.ShapeDtypeStruct((N_HEADS, t_all), F32)],
        compiler_params=_params(),
    )(x, ada3, pos, wqkv, wf16, freq)


def _fgate_fwd(fa_t, bf, seq):
    t_all = fa_t.shape[1]

    def body(fa_ref, b_ref, f_ref):
        lane = lax.broadcasted_iota(jnp.int32, (N_HEADS, LANES), 1)

        def chunk(j, carry):
            sl = pl.ds(pl.multiple_of(j * LANES, LANES), LANES)
            xv = fa_ref[:, sl] + b_ref[...]
            lf = jnp.minimum(xv, 0.0) - jnp.log(1.0 + jnp.exp(-jnp.abs(xv)))
            for s in (1, 2, 4, 8, 16, 32, 64):
                lf = lf + jnp.where(lane >= s, pltpu.roll(lf, s, 1), 0.0)
            lf = lf + carry
            f_ref[:, sl] = lf
            return lf[:, LANES - 1:LANES]

        lax.fori_loop(0, seq // LANES, chunk, jnp.zeros((N_HEADS, 1), F32))

    return pl.pallas_call(
        body, name="fgate_fwd", grid=(t_all // seq,),
        in_specs=[pl.BlockSpec((N_HEADS, seq), lambda b: (0, b)), pl.BlockSpec((N_HEADS, 1), lambda b: (0, 0))],
        out_specs=pl.BlockSpec((N_HEADS, seq), lambda b: (0, b)),
        out_shape=jax.ShapeDtypeStruct((N_HEADS, t_all), F32), compiler_params=_params(),
    )(fa_t, bf)


def _fgate_bwd(df_t, fa_t, bf, seq):
    t_all = fa_t.shape[1]

    def body(df_ref, fa_ref, b_ref, o_ref, s_ref):
        lane = lax.broadcasted_iota(jnp.int32, (N_HEADS, LANES), 1)

        @pl.when(pl.program_id(0) == 0)
        def _():
            s_ref[...] = jnp.zeros_like(s_ref)

        def chunk(jj, carry):
            car, tot = carry
            j = seq // LANES - 1 - jj
            sl = pl.ds(pl.multiple_of(j * LANES, LANES), LANES)
            d = df_ref[:, sl]
            for s in (1, 2, 4, 8, 16, 32, 64):
                d = d + jnp.where(lane < LANES - s, pltpu.roll(d, LANES - s, 1), 0.0)
            d = d + car
            dfa = d * jax.nn.sigmoid(-(fa_ref[:, sl] + b_ref[...]))
            o_ref[:, sl] = dfa
            return d[:, 0:1], tot + jnp.sum(dfa, axis=1, keepdims=True)

        z = jnp.zeros((N_HEADS, 1), F32)
        _, tot = lax.fori_loop(0, seq // LANES, chunk, (z, z))
        s_ref[...] += jnp.broadcast_to(tot, (N_HEADS, LANES))

    row = pl.BlockSpec((N_HEADS, seq), lambda b: (0, b))
    return pl.pallas_call(
        body, name="fgate_bwd", grid=(t_all // seq,),
        in_specs=[row, row, pl.BlockSpec((N_HEADS, 1), lambda b: (0, 0))],
        out_specs=[row, pl.BlockSpec((N_HEADS, LANES), lambda b: (0, 0))],
        out_shape=[jax.ShapeDtypeStruct((N_HEADS, t_all), F32), jax.ShapeDtypeStruct((N_HEADS, LANES), F32)],
        compiler_params=_params(),
    )(df_t, fa_t, bf)


FOX_T = 256


def _fox_prep(dst, src_ref, lo, hi):
    for p in range(4):
        v = src_ref[:, LANES * p:LANES * (p + 1)]
        dst[2 * p] = jnp.where(lo, v, jnp.zeros_like(v))
        dst[2 * p + 1] = jnp.where(hi, v, jnp.zeros_like(v))


def _fox_fwd(za, vt, f_col, seq, shards):
    t_all = za.shape[0]
    tq = FOX_T
    nq = seq // tq
    nbat = t_all // seq
    n = len(shards)
    to_all = [True] * n

    def body(*refs):
        q_ref, k_ref, vt_ref, fc_ref = refs[:4]
        o_ref, lse_ref = refs[4 + n:6 + n]
        qm_sc, m_sc, l_sc, acc_sc = refs[6 + 2 * n:10 + 2 * n]
        comm = (refs[4:4 + n], refs[6 + n:6 + 2 * n], to_all, refs[10 + 2 * n:])
        i = pl.program_id(1)

        @pl.when((pl.program_id(0) == 0) & (i == 0))
        def _():
            for cp in _comm_copies(*comm):
                cp.start()
        lo, hi = _half_masks(tq)
        r = lax.broadcasted_iota(jnp.int32, (tq, tq), 0)
        c = lax.broadcasted_iota(jnp.int32, (tq, tq), 1)
        tri = c >= r
        _fox_prep(qm_sc, q_ref, lo, hi)
        m_sc[...] = jnp.full(m_sc.shape, NEG, F32)
        l_sc[...] = jnp.zeros_like(l_sc)
        acc_sc[...] = jnp.zeros_like(acc_sc)

        def block(j, masked):
            sl = pl.ds(pl.multiple_of(j * tq, tq), tq)
            for p in range(4):
                kj = k_ref[sl, LANES * p:LANES * (p + 1)]
                for h in (2 * p, 2 * p + 1):
                    st = _nt(kj, qm_sc[h]) - fc_ref[sl, h:h + 1]
                    if masked:
                        st = jnp.where(tri, st, NEG)
                    m = m_sc[h:h + 1, :]
                    mn = jnp.maximum(m, jnp.max(st, axis=0, keepdims=True))
                    a = jnp.exp(m - mn)
                    pe = jnp.exp(st - mn)
                    m_sc[h:h + 1, :] = mn
                    l_sc[h:h + 1, :] = a * l_sc[h:h + 1, :] + jnp.sum(pe, axis=0, keepdims=True)
                    acc_sc[h] = a * acc_sc[h] + _nn(vt_ref[HEAD_DIM * h:HEAD_DIM * (h + 1), sl], pe.astype(BF16))

        def step(j, carry):
            block(j, False)
            return carry

        lax.fori_loop(0, i, step, 0)
        block(i, True)
        lse_ref[...] = m_sc[...] + jnp.log(l_sc[...])
        for p in range(4):
            ot = jnp.concatenate([acc_sc[h] / l_sc[h:h + 1, :] for h in (2 * p, 2 * p + 1)], axis=0)
            o_ref[:, LANES * p:LANES * (p + 1)] = ot.T

        @pl.when((pl.program_id(0) == nbat - 1) & (i == nq - 1))
        def _():
            for cp in _comm_copies(*comm):
                cp.wait()

    res = pl.pallas_call(
        body, name="fox_fwd", grid=(nbat, nq),
        in_specs=[pl.BlockSpec((tq, WIDTH), lambda b, i: (b * nq + i, 0)),
                  pl.BlockSpec((seq, WIDTH), lambda b, i: (b, 1)), pl.BlockSpec((WIDTH, seq), lambda b, i: (b, 0)),
                  pl.BlockSpec((seq, LANES), lambda b, i: (b, 0))] + [ANY] * n,
        out_specs=[pl.BlockSpec((tq, WIDTH), lambda b, i: (b * nq + i, 0)),
                   pl.BlockSpec((N_HEADS, tq), lambda b, i: (0, b * nq + i))] + [ANY] * n,
        out_shape=[jax.ShapeDtypeStruct((t_all, WIDTH), F32), jax.ShapeDtypeStruct((N_HEADS, t_all), F32)]
        + _comm_out_shapes(shards, to_all),
        scratch_shapes=[pltpu.VMEM((N_HEADS, tq, LANES), BF16), pltpu.VMEM((N_HEADS, tq), F32),
                        pltpu.VMEM((N_HEADS, tq), F32), pltpu.VMEM((N_HEADS, HEAD_DIM, tq), F32)] + _comm_sems(n),
        compiler_params=_params(),
    )(za, za, vt, f_col, *shards)
    return res[0], res[1], res[2:]


def _fox_bwd_dq(za, do, f_row, lse, dl, seq):
    t_all = za.shape[0]
    tq = FOX_T
    nq = seq // tq

    def body(q_ref, k_ref, v_ref, do_ref, fr_ref, lse_ref, dl_ref, dq_ref, df_ref, qm_sc, dm_sc, nl_sc, dd_sc,
             acc_sc, rs_sc):
        i = pl.program_id(1)
        lo, hi = _half_masks(tq)
        r = lax.broadcasted_iota(jnp.int32, (tq, tq), 0)
        c = lax.broadcasted_iota(jnp.int32, (tq, tq), 1)
        tri = c <= r
        _fox_prep(qm_sc, q_ref, lo, hi)
        _fox_prep(dm_sc, do_ref, lo, hi)
        for h in range(N_HEADS):
            nl_sc[h] = jnp.broadcast_to(lse_ref[:, h:h + 1], (tq, tq))
            dd_sc[h] = jnp.broadcast_to(dl_ref[:, h:h + 1], (tq, tq))
        acc_sc[...] = jnp.zeros_like(acc_sc)
        rs_sc[...] = jnp.zeros_like(rs_sc)

        def block(j, masked):
            sl = pl.ds(pl.multiple_of(j * tq, tq), tq)
            for p in range(4):
                cs = slice(LANES * p, LANES * (p + 1))
                kj = k_ref[sl, cs]
                vj = v_ref[sl, cs]
                for h in (2 * p, 2 * p + 1):
                    s = _nt(qm_sc[h], kj) - fr_ref[h:h + 1, sl] - nl_sc[h]
                    if masked:
                        s = jnp.where(tri, s, NEG)
                    ds = jnp.exp(s) * (_nt(dm_sc[h], vj) - dd_sc[h])
                    acc_sc[h] += _nn(ds.astype(BF16), kj)
                    rs_sc[h] += ds[:, :LANES] + ds[:, LANES:]

        def step(j, carry):
            block(j, False)
            return carry

        lax.fori_loop(0, i, step, 0)
        block(i, True)
        df_ref[...] = jnp.zeros_like(df_ref)
        for p in range(4):
            dq_ref[:, LANES * p:LANES * (p + 1)] = (
                jnp.where(lo, acc_sc[2 * p], acc_sc[2 * p + 1]) * Q_SCALE).astype(BF16)
            for h in (2 * p, 2 * p + 1):
                df_ref[:, h:h + 1] = jnp.sum(rs_sc[h], axis=1, keepdims=True)

    tile = lambda w: pl.BlockSpec((tq, w), lambda b, i: (b * nq + i, 0))
    return pl.pallas_call(
        body, name="fox_bwd_dq", grid=(t_all // seq, nq),
        in_specs=[tile(WIDTH), pl.BlockSpec((seq, WIDTH), lambda b, i: (b, 1)),
                  pl.BlockSpec((seq, WIDTH), lambda b, i: (b, 2)), tile(WIDTH),
                  pl.BlockSpec((N_HEADS, seq), lambda b, i: (0, b)), tile(LANES), tile(LANES)],
        out_specs=[tile(WIDTH), tile(LANES)],
        out_shape=[jax.ShapeDtypeStruct((t_all, WIDTH), BF16), jax.ShapeDtypeStruct((t_all, LANES), F32)],
        scratch_shapes=[pltpu.VMEM((N_HEADS, tq, LANES), BF16), pltpu.VMEM((N_HEADS, tq, LANES), BF16),
                        pltpu.VMEM((N_HEADS, tq, tq), F32), pltpu.VMEM((N_HEADS, tq, tq), F32),
                        pltpu.VMEM((N_HEADS, tq, LANES), F32), pltpu.VMEM((N_HEADS, tq, LANES), F32)],
        compiler_params=_params(),
    )(za, za, za, do, f_row, lse, dl)


def _fox_bwd_dkv(za, do, f_col, lse_row, dl_row, seq, grads):
    t_all = za.shape[0]
    tk = FOX_T
    nk = seq // tk
    nbat = t_all // seq
    n = len(grads)
    to_all = [False] * n

    def body(*refs):
        k_ref, v_ref, q_ref, do_ref, fc_ref, lr_ref, dr_ref = refs[:7]
        dk_ref, dv_ref, df_ref = refs[7 + n:10 + n]
        km_sc, vm_sc, fk_sc, dk_sc, dv_sc, cs_sc = refs[10 + 2 * n:16 + 2 * n]
        comm = (refs[7:7 + n], refs[10 + n:10 + 2 * n], to_all, refs[16 + 2 * n:])
        j = pl.program_id(1)

        @pl.when((pl.program_id(0) == 0) & (j == 0))
        def _():
            for cp in _comm_copies(*comm):
                cp.start()
        lo, hi = _half_masks(tk)
        r = lax.broadcasted_iota(jnp.int32, (tk, tk), 0)
        c = lax.broadcasted_iota(jnp.int32, (tk, tk), 1)
        tri = c >= r
        _fox_prep(km_sc, k_ref, lo, hi)
        _fox_prep(vm_sc, v_ref, lo, hi)
        for h in range(N_HEADS):
            fk_sc[h] = jnp.broadcast_to(fc_ref[:, h:h + 1], (tk, tk))
        dk_sc[...] = jnp.zeros_like(dk_sc)
        dv_sc[...] = jnp.zeros_like(dv_sc)
        cs_sc[...] = jnp.zeros_like(cs_sc)

        def block(i, masked):
            sl = pl.ds(pl.multiple_of(i * tk, tk), tk)
            for p in range(4):
                cs = slice(LANES * p, LANES * (p + 1))
                qi = q_ref[sl, cs]
                doi = do_ref[sl, cs]
                for h in (2 * p, 2 * p + 1):
                    st = _nt(km_sc[h], qi) - fk_sc[h] - lr_ref[h:h + 1, sl]
                    if masked:
                        st = jnp.where(tri, st, NEG)
                    pt = jnp.exp(st)
                    dst = pt * (_nt(vm_sc[h], doi) - dr_ref[h:h + 1, sl])
                    dv_sc[h] += _nn(pt.astype(BF16), doi)
                    dk_sc[h] += _nn(dst.astype(BF16), qi)
                    cs_sc[h] += dst[:, :LANES] + dst[:, LANES:]

        def step(i, carry):
            block(i, False)
            return carry

        block(j, True)
        lax.fori_loop(j + 1, nk, step, 0)
        df_ref[...] = jnp.zeros_like(df_ref)
        for p in range(4):
            cs = slice(LANES * p, LANES * (p + 1))
            dk_ref[:, cs] = jnp.where(lo, dk_sc[2 * p], dk_sc[2 * p + 1]).astype(BF16)
            dv_ref[:, cs] = jnp.where(lo, dv_sc[2 * p], dv_sc[2 * p + 1]).astype(BF16)
            for h in (2 * p, 2 * p + 1):
                df_ref[:, h:h + 1] = -jnp.sum(cs_sc[h], axis=1, keepdims=True)

        @pl.when((pl.program_id(0) == nbat - 1) & (j == nk - 1))
        def _():
            for cp in _comm_copies(*comm):
                cp.wait()

    tile = lambda w, col: pl.BlockSpec((tk, w), lambda b, j: (b * nk + j, col))
    full = lambda col: pl.BlockSpec((seq, WIDTH), lambda b, j: (b, col))
    row = pl.BlockSpec((N_HEADS, seq), lambda b, j: (0, b))
    acc = pltpu.VMEM((N_HEADS, tk, LANES), F32)
    res = pl.pallas_call(
        body, name="fox_bwd_dkv", grid=(nbat, nk),
        in_specs=[tile(WIDTH, 1), tile(WIDTH, 2), full(0), full(0), tile(LANES, 0), row, row] + [ANY] * n,
        out_specs=[tile(WIDTH, 0), tile(WIDTH, 0), tile(LANES, 0)] + [ANY] * n,
        out_shape=[jax.ShapeDtypeStruct((t_all, WIDTH), BF16), jax.ShapeDtypeStruct((t_all, WIDTH), BF16),
                   jax.ShapeDtypeStruct((t_all, LANES), F32)] + _comm_out_shapes(grads, to_all),
        scratch_shapes=[pltpu.VMEM((N_HEADS, tk, LANES), BF16), pltpu.VMEM((N_HEADS, tk, LANES), BF16),
                        pltpu.VMEM((N_HEADS, tk, tk), F32), acc, acc, acc] + _comm_sems(n),
        compiler_params=_params(),
    )(za, za, za, do, f_col, lse_row, dl_row, *grads)
    return res[0], res[1], res[2], res[3:]


def _dil_mask(has_prev):
    qi = lax.broadcasted_iota(jnp.int32, (BLK, 2 * BLK), 0)
    kj = lax.broadcasted_iota(jnp.int32, (BLK, 2 * BLK), 1)
    dist = qi + BLK - kj
    return (dist >= 0) & (dist <= BLK) & ((kj >= BLK) | has_prev)


def _dil_specs(t_all):
    nb = t_all // BLK
    cur = pl.BlockSpec((1, BLK, WIDTH), lambda p, n: (p, n, 0))
    prev = pl.BlockSpec((1, BLK, WIDTH), lambda p, n: (p, jnp.maximum(n - 1, 0), 0))
    nxt = pl.BlockSpec((1, BLK, WIDTH), lambda p, n: (p, jnp.minimum(n + 1, nb - 1), 0))
    stat = pl.BlockSpec((1, BLK, LANES), lambda p, n: (p, n, 0))
    return nb, cur, prev, nxt, stat


def _dil_fwd(qs, ks, vs, seq):
    t_all = qs.shape[1]
    nb, cur, prev, _, stat = _dil_specs(t_all)

    def body(q_ref, kp_ref, kc_ref, vp_ref, vc_ref, o_ref, lse_ref):
        nbs = (seq // BLK) >> (2 * pl.program_id(0))
        mask = _dil_mask((pl.program_id(1) & (nbs - 1)) != 0)
        lo, hi = _half_masks(BLK)
        lse_ref[...] = jnp.zeros_like(lse_ref)
        for p in range(4):
            cs = slice(LANES * p, LANES * (p + 1))
            qp = q_ref[0, :, cs]
            kcat = jnp.concatenate([kp_ref[0, :, cs], kc_ref[0, :, cs]], axis=0)
            vcat = jnp.concatenate([vp_ref[0, :, cs], vc_ref[0, :, cs]], axis=0)
            res = []
            for e in (0, 1):
                h = 2 * p + e
                qe = jnp.where(lo if e == 0 else hi, qp, jnp.zeros_like(qp))
                s = jnp.where(mask, _nt(qe, kcat), NEG)
                m = jnp.max(s, axis=1, keepdims=True)
                pe = jnp.exp(s - m)
                l = jnp.sum(pe, axis=1, keepdims=True)
                res.append(_nn(pe.astype(BF16), vcat) / l)
                lse_ref[0, :, h:h + 1] = m + jnp.log(l)
            o_ref[0, :, cs] = jnp.where(lo, res[0], res[1])

    return pl.pallas_call(
        body, name="dil_fwd", grid=(3, nb), in_specs=[cur, prev, cur, prev, cur], out_specs=[cur, stat],
        out_shape=[jax.ShapeDtypeStruct((3, t_all, WIDTH), F32), jax.ShapeDtypeStruct((3, t_all, LANES), F32)],
        compiler_params=_params(),
    )(qs, ks, ks, vs, vs)


def _dil_bwd_dq(qs, ks, vs, dos, lses, dls, seq):
    t_all = qs.shape[1]
    nb, cur, prev, _, stat = _dil_specs(t_all)

    def body(q_ref, kp_ref, kc_ref, vp_ref, vc_ref, do_ref, lse_ref, dl_ref, dq_ref):
        nbs = (seq // BLK) >> (2 * pl.program_id(0))
        mask = _dil_mask((pl.program_id(1) & (nbs - 1)) != 0)
        lo, hi = _half_masks(BLK)
        for p in range(4):
            cs = slice(LANES * p, LANES * (p + 1))
            qp = q_ref[0, :, cs]
            dop = do_ref[0, :, cs]
            kcat = jnp.concatenate([kp_ref[0, :, cs], kc_ref[0, :, cs]], axis=0)
            vcat = jnp.concatenate([vp_ref[0, :, cs], vc_ref[0, :, cs]], axis=0)
            res = []
            for e in (0, 1):
                h = 2 * p + e
                sel = lo if e == 0 else hi
                qe = jnp.where(sel, qp, jnp.zeros_like(qp))
                doe = jnp.where(sel, dop, jnp.zeros_like(dop))
                s = jnp.where(mask, _nt(qe, kcat) - lse_ref[0, :, h:h + 1], NEG)
                ds = jnp.exp(s) * (_nt(doe, vcat) - dl_ref[0, :, h:h + 1])
                res.append(_nn(ds.astype(BF16), kcat))
            dq_ref[0, :, cs] = jnp.where(lo, res[0], res[1]) * Q_SCALE

    return pl.pallas_call(
        body, name="dil_bwd_dq", grid=(3, nb), in_specs=[cur, prev, cur, prev, cur, cur, stat, stat], out_specs=cur,
        out_shape=jax.ShapeDtypeStruct((3, t_all, WIDTH), F32), compiler_params=_params(),
    )(qs, ks, ks, vs, vs, dos, lses, dls)


def _dil_bwd_dkv(qs, ks, vs, dos, lse_rows, dl_rows, seq):
    t_all = qs.shape[1]
    nb, cur, _, nxt, _ = _dil_specs(t_all)
    rcur = pl.BlockSpec((1, N_HEADS, BLK), lambda p, n: (p, 0, n))
    rnxt = pl.BlockSpec((1, N_HEADS, BLK), lambda p, n: (p, 0, jnp.minimum(n + 1, nb - 1)))

    def body(k_ref, v_ref, qc_ref, qn_ref, dc_ref, dn_ref, lc_ref, ln_ref, ec_ref, en_ref, dk_ref, dv_ref):
        nbs = (seq // BLK) >> (2 * pl.program_id(0))
        has_next = ((pl.program_id(1) + 1) & (nbs - 1)) != 0
        r = lax.broadcasted_iota(jnp.int32, (BLK, 2 * BLK), 0)
        c = lax.broadcasted_iota(jnp.int32, (BLK, 2 * BLK), 1)
        mask = ((c < BLK) & (c >= r)) | ((c >= BLK) & (c - BLK <= r) & has_next)
        lo, hi = _half_masks(BLK)
        for p in range(4):
            cs = slice(LANES * p, LANES * (p + 1))
            kp = k_ref[0, :, cs]
            vp = v_ref[0, :, cs]
            qcat = jnp.concatenate([qc_ref[0, :, cs], qn_ref[0, :, cs]], axis=0)
            dcat = jnp.concatenate([dc_ref[0, :, cs], dn_ref[0, :, cs]], axis=0)
            rk, rv = [], []
            for e in (0, 1):
                h = 2 * p + e
                sel = lo if e == 0 else hi
                ke = jnp.where(sel, kp, jnp.zeros_like(kp))
                ve = jnp.where(sel, vp, jnp.zeros_like(vp))
                lrow = jnp.concatenate([lc_ref[0, h:h + 1, :], ln_ref[0, h:h + 1, :]], axis=1)
                erow = jnp.concatenate([ec_ref[0, h:h + 1, :], en_ref[0, h:h + 1, :]], axis=1)
                st = jnp.where(mask, _nt(ke, qcat) - lrow, NEG)
                pt = jnp.exp(st)
                dst = pt * (_nt(ve, dcat) - erow)
                rk.append(_nn(dst.astype(BF16), qcat))
                rv.append(_nn(pt.astype(BF16), dcat))
            dk_ref[0, :, cs] = jnp.where(lo, rk[0], rk[1])
            dv_ref[0, :, cs] = jnp.where(lo, rv[0], rv[1])

    return pl.pallas_call(
        body, name="dil_bwd_dkv", grid=(3, nb),
        in_specs=[cur, cur, cur, nxt, cur, nxt, rcur, rnxt, rcur, rnxt], out_specs=[cur, cur],
        out_shape=[jax.ShapeDtypeStruct((3, t_all, WIDTH), F32)] * 2, compiler_params=_params(),
    )(ks, vs, qs, qs, dos, dos, lse_rows, lse_rows, dl_rows, dl_rows)


def _mix_out(oa, o3, l3, gn_a, gn_b, w_out, x, ada3, ln_g, ln_b, seq):
    t_all = x.shape[0]
    tm = 256
    nts = seq // tm

    def body(oa_ref, o1_ref, o2_ref, o3_ref, l1_ref, l2_ref, l3_ref, ga_ref, gb_ref, w_ref, x_ref, ada_ref, g_ref,
             b_ref, ob_ref, lse_ref, mg_ref, mix_ref, xh_ref, rs_ref, h2_ref):
        e, et = _head_mats()
        la, lb, lc = l1_ref[...], l2_ref[...], l3_ref[...]
        mx = jnp.maximum(jnp.maximum(la, lb), lc)
        ea, eb, ec = jnp.exp(la - mx), jnp.exp(lb - mx), jnp.exp(lc - mx)
        tot = ea + eb + ec
        lse_ref[...] = mx + jnp.log(tot)
        ob = (o1_ref[...] * _hexp(ea / tot, e) + o2_ref[...] * _hexp(eb / tot, e) + o3_ref[...] * _hexp(ec / tot, e))
        ob_ref[...] = ob

        def rms(o, gain):
            rr = lax.rsqrt(_hsum(o * o, et) * (1.0 / HEAD_DIM) + RMS_EPS)
            return o * _hexp(rr, e) * gain

        merged = jnp.concatenate([rms(oa_ref[...], ga_ref[...]), rms(ob, gb_ref[...])], axis=1).astype(BF16)
        mg_ref[...] = merged
        mix = _nn(merged, w_ref[...])
        mix_ref[...] = mix.astype(BF16)
        r1 = ALPHA * x_ref[...] + ada_ref[0, 2:3, :] * mix
        d = r1 - jnp.mean(r1, axis=1, keepdims=True)
        rstd = lax.rsqrt(jnp.mean(d * d, axis=1, keepdims=True) + LN_EPS)
        xh = d * rstd
        xh_ref[...] = xh
        rs_ref[...] = jnp.broadcast_to(rstd, (tm, LANES))
        x1 = xh * g_ref[...] + b_ref[...]
        h2_ref[...] = (x1 * (1.0 + ada_ref[0, 4:5, :]) + ada_ref[0, 3:4, :]).astype(BF16)

    tok = lambda w: pl.BlockSpec((tm, w), lambda i: (i, 0))
    vec = lambda w: pl.BlockSpec((1, w), lambda i: (0, 0))
    return pl.pallas_call(
        body, name="mix_out", grid=(t_all // tm,),
        in_specs=[tok(WIDTH)] * 4 + [tok(LANES)] * 3 + [vec(WIDTH), vec(WIDTH),
                  pl.BlockSpec(w_out.shape, lambda i: (0, 0)), tok(D_MODEL),
                  pl.BlockSpec((1, 6, D_MODEL), lambda i: (i // nts, 0, 0)), vec(D_MODEL), vec(D_MODEL)],
        out_specs=[tok(WIDTH), tok(LANES), tok(D_MODEL), tok(D_MODEL), tok(D_MODEL), tok(LANES), tok(D_MODEL)],
        out_shape=[jax.ShapeDtypeStruct((t_all, WIDTH), F32), jax.ShapeDtypeStruct((t_all, LANES), F32),
                   jax.ShapeDtypeStruct((t_all, D_MODEL), BF16), jax.ShapeDtypeStruct((t_all, D_MODEL), BF16),
                   jax.ShapeDtypeStruct((t_all, D_MODEL), F32), jax.ShapeDtypeStruct((t_all, LANES), F32),
                   jax.ShapeDtypeStruct((t_all, D_MODEL), BF16)],
        compiler_params=_params(),
    )(oa, o3[0], o3[1], o3[2], l3[0], l3[1], l3[2], gn_a, gn_b, w_out, x, ada3, ln_g, ln_b)


def _mix_out_bwd(dmix, w_out, oa, ob, gn_a, gn_b):
    t_all = dmix.shape[0]
    tm = 256

    def body(dm_ref, w_ref, oa_ref, ob_ref, ga_ref, gb_ref, doa_ref, dob_ref, dla_ref, dlb_ref, acc_ref):
        @pl.when(pl.program_id(0) == 0)
        def _():
            acc_ref[...] = jnp.zeros_like(acc_ref)
        e, et = _head_mats()
        dmg = _nt(dm_ref[...], w_ref[...])

        def group(o, dn, gain):
            rr = lax.rsqrt(_hsum(o * o, et) * (1.0 / HEAD_DIM) + RMS_EPS)
            re = _hexp(rr, e)
            dgain = jnp.sum(dn * o * re, axis=0, keepdims=True)
            dxn = dn * gain
            tt = _hsum(dxn * o, et) * (rr * rr * rr) * (1.0 / HEAD_DIM)
            do = re * dxn - o * _hexp(tt, e)
            return do, _hsum(do * o, et), dgain

        doa, dla, dga = group(oa_ref[...], dmg[:, :WIDTH], ga_ref[...])
        dob, dlb, dgb = group(ob_ref[...], dmg[:, WIDTH:], gb_ref[...])
        doa_ref[...] = doa.astype(BF16)
        dob_ref[...] = dob.astype(BF16)
        dla_ref[...] = dla
        dlb_ref[...] = dlb
        acc_ref[0:1, :] += jnp.concatenate([dga, dgb], axis=1)

    tok = lambda w: pl.BlockSpec((tm, w), lambda i: (i, 0))
    vec = lambda w: pl.BlockSpec((1, w), lambda i: (0, 0))
    return pl.pallas_call(
        body, name="mix_out_bwd", grid=(t_all // tm,),
        in_specs=[tok(D_MODEL), pl.BlockSpec(w_out.shape, lambda i: (0, 0)), tok(WIDTH), tok(WIDTH), vec(WIDTH),
                  vec(WIDTH)],
        out_specs=[tok(WIDTH), tok(WIDTH), tok(LANES), tok(LANES), pl.BlockSpec((8, D_MODEL), lambda i: (0, 0))],
        out_shape=[jax.ShapeDtypeStruct((t_all, WIDTH), BF16), jax.ShapeDtypeStruct((t_all, WIDTH), BF16),
                   jax.ShapeDtypeStruct((t_all, LANES), F32), jax.ShapeDtypeStruct((t_all, LANES), F32),
                   jax.ShapeDtypeStruct((8, D_MODEL), F32)],
        compiler_params=_params(),
    )(dmix, w_out, oa, ob, gn_a, gn_b)


def _inproj_bwd(dza, dqb, dkb, dvb, dfa16, pos, wqkv, wf16, freq, dr1, x, ada3, seq):
    t_all = x.shape[0]
    tm = 256
    nts = seq // tm
    nbat = t_all // seq

    def body(dza_ref, dqb_ref, dkb_ref, dvb_ref, dfa_ref, pos_ref, w_ref, wf_ref, fr_ref, dr1_ref, x_ref, ada_ref,
             gx_ref, dz_ref, acc_ref):
        i = pl.program_id(0)

        @pl.when(i == 0)
        def _():
            acc_ref[...] = jnp.zeros_like(acc_ref)
        tabs = _rope_tabs(pos_ref, fr_ref, -1.0)
        dz_ref[:, :3 * WIDTH] = dza_ref[...]
        dz_ref[:, 3 * WIDTH:4 * WIDTH] = _rope(dqb_ref[...], tabs).astype(BF16)
        dz_ref[:, 4 * WIDTH:5 * WIDTH] = _rope(dkb_ref[...], tabs).astype(BF16)
        dz_ref[:, 5 * WIDTH:] = dvb_ref[...].astype(BF16)
        dh1 = _tn(dfa_ref[...], wf_ref[...])
        for n in range(6):
            cs = slice(n * WIDTH, (n + 1) * WIDTH)
            dh1 = dh1 + _nt(dz_ref[:, cs], w_ref[:, cs])
        xv = x_ref[...]
        gx_ref[...] = ALPHA * dr1_ref[...] + dh1 * (1.0 + ada_ref[0, 1:2, :])
        b = i // nts
        acc_ref[pl.ds(b, 1), :] += jnp.sum(dh1 * xv, axis=0, keepdims=True)
        acc_ref[pl.ds(8 + b, 1), :] += jnp.sum(dh1, axis=0, keepdims=True)

    tok = lambda w: pl.BlockSpec((tm, w), lambda i: (i, 0))
    return pl.pallas_call(
        body, name="inproj_bwd", grid=(t_all // tm,),
        in_specs=[tok(3 * WIDTH), tok(WIDTH), tok(WIDTH), tok(WIDTH), pl.BlockSpec((16, tm), lambda i: (0, i)),
                  tok(1), pl.BlockSpec(wqkv.shape, lambda i: (0, 0)), pl.BlockSpec(wf16.shape, lambda i: (0, 0)),
                  pl.BlockSpec((1, LANES), lambda i: (0, 0)), tok(D_MODEL), tok(D_MODEL),
                  pl.BlockSpec((1, 6, D_MODEL), lambda i: (i // nts, 0, 0))],
        out_specs=[tok(D_MODEL), tok(6 * WIDTH), pl.BlockSpec((16, D_MODEL), lambda i: (0, 0))],
        out_shape=[jax.ShapeDtypeStruct((t_all, D_MODEL), F32), jax.ShapeDtypeStruct((t_all, 6 * WIDTH), BF16),
                   jax.ShapeDtypeStruct((16, D_MODEL), F32)],
        compiler_params=_params(),
    )(dza, dqb, dkb, dvb, dfa16, pos, wqkv, wf16, freq, dr1, x, ada3)


FFN_TM = 512
FFN_TN = 256
HALO = 8


FFN_CHUNK = 64


def _conv(cat_ref, w_ref, b_ref, start, rows):
    return (b_ref[...] + w_ref[0:1, :] * cat_ref[pl.ds(start + HALO - 2, rows), :]
            + w_ref[1:2, :] * cat_ref[pl.ds(start + HALO - 1, rows), :]
            + w_ref[2:3, :] * cat_ref[pl.ds(start + HALO, rows), :])


def _ffn_gate(u, conv_w, conv_b, seq):
    t_all = u.shape[0]
    tm, tn = FFN_TM, FFN_TN
    nc = D_FF // tn
    nts = seq // tm

    def body(ua_ref, uap_ref, ug_ref, ugp_ref, wa_ref, wg_ref, ba_ref, bg_ref, o_ref, ca_ref, cg_ref):
        first = (pl.program_id(0) % nts) == 0
        zero = jnp.zeros((HALO, tn), F32)
        ca_ref[0:HALO, :] = jnp.where(first, zero, uap_ref[...])
        cg_ref[0:HALO, :] = jnp.where(first, zero, ugp_ref[...])
        ca_ref[HALO:, :] = ua_ref[...]
        cg_ref[HALO:, :] = ug_ref[...]
        for c0 in range(0, tm, FFN_CHUNK):
            ya = _conv(ca_ref, wa_ref, ba_ref, c0, FFN_CHUNK)
            yg = _conv(cg_ref, wg_ref, bg_ref, c0, FFN_CHUNK)
            o_ref[c0:c0 + FFN_CHUNK, :] = (yg * jax.nn.sigmoid(yg) * ya).astype(BF16)

    cur = lambda off: pl.BlockSpec((tm, tn), lambda t, n: (t, n + off))
    prev = lambda off: pl.BlockSpec((HALO, tn), lambda t, n: (jnp.maximum(t * (tm // HALO) - 1, 0), n + off))
    vec = lambda r, off: pl.BlockSpec((r, tn), lambda t, n: (0, n + off))
    return pl.pallas_call(
        body, name="ffn_gate", grid=(t_all // tm, nc),
        in_specs=[cur(0), prev(0), cur(nc), prev(nc), vec(3, 0), vec(3, nc), vec(1, 0), vec(1, nc)],
        out_specs=pl.BlockSpec((tm, tn), lambda t, n: (t, n)),
        out_shape=jax.ShapeDtypeStruct((t_all, D_FF), BF16),
        scratch_shapes=[pltpu.VMEM((tm + HALO, tn), F32)] * 2, compiler_params=_params(),
    )(u, u, u, u, conv_w, conv_w, conv_b, conv_b)


def _ffn_gate_bwd(u, dfi, conv_w, conv_b, seq):
    t_all = u.shape[0]
    tm, tn = FFN_TM, FFN_TN
    nc = D_FF // tn
    nts = seq // tm

    def body(ua_ref, uap_ref, uan_ref, ug_ref, ugp_ref, ugn_ref, df_ref, dfn_ref, wa_ref, wg_ref, ba_ref, bg_ref,
             dua_ref, dug_ref, acca_ref, accg_ref, ca_ref, cg_ref, ya_ref, yg_ref):
        t = pl.program_id(1)
        first = (t % nts) == 0
        last = (t % nts) == nts - 1

        @pl.when(t == 0)
        def _():
            acca_ref[...] = jnp.zeros_like(acca_ref)
            accg_ref[...] = jnp.zeros_like(accg_ref)
        zero = jnp.zeros((HALO, tn), F32)
        for cat, cur, prv, nxt in ((ca_ref, ua_ref, uap_ref, uan_ref), (cg_ref, ug_ref, ugp_ref, ugn_ref)):
            cat[0:HALO, :] = jnp.where(first, zero, prv[...])
            cat[HALO:HALO + tm, :] = cur[...]
            cat[HALO + tm:, :] = nxt[...]
        ch = FFN_CHUNK
        sums = [[jnp.zeros((1, tn), F32) for _ in range(4)] for _ in range(2)]
        for ci, c0 in enumerate(range(0, tm, ch)):
            ya = _conv(ca_ref, wa_ref, ba_ref, c0, ch + HALO)
            yg = _conv(cg_ref, wg_ref, bg_ref, c0, ch + HALO)
            if c0 + ch < tm:
                beyond = df_ref[c0 + ch:c0 + ch + 16, :].astype(F32)[:HALO]
            else:
                beyond = jnp.where(last, 0.0, dfn_ref[...].astype(F32)[:HALO])
            dfe = jnp.concatenate([df_ref[c0:c0 + ch, :].astype(F32), beyond], axis=0)
            sg = jax.nn.sigmoid(yg)
            ya_ref[ci] = dfe * (yg * sg)
            yg_ref[ci] = dfe * ya * (sg * (1.0 + yg * (1.0 - sg)))
            for half, (dy, cat, w_ref, du_ref) in enumerate(((ya_ref, ca_ref, wa_ref, dua_ref),
                                                             (yg_ref, cg_ref, wg_ref, dug_ref))):
                d0 = dy[ci, 0:ch, :]
                du = (w_ref[2:3, :] * d0 + w_ref[1:2, :] * dy[ci, pl.ds(1, ch), :]
                      + w_ref[0:1, :] * dy[ci, pl.ds(2, ch), :])
                du_ref[c0:c0 + ch, :] = du.astype(BF16)
                for k in range(3):
                    sums[half][k] += jnp.sum(d0 * cat[pl.ds(c0 + HALO - 2 + k, ch), :], axis=0, keepdims=True)
                sums[half][3] += jnp.sum(d0, axis=0, keepdims=True)
        for half, acc in enumerate((acca_ref, accg_ref)):
            for k in range(4):
                acc[k:k + 1, :] += sums[half][k]

    nrow = t_all // HALO
    cur = lambda off: pl.BlockSpec((tm, tn), lambda n, t: (t, n + off))
    prev = lambda off: pl.BlockSpec((HALO, tn), lambda n, t: (jnp.maximum(t * (tm // HALO) - 1, 0), n + off))
    nxt = lambda off: pl.BlockSpec((HALO, tn), lambda n, t: (jnp.minimum((t + 1) * (tm // HALO), nrow - 1), n + off))
    vec = lambda r, off: pl.BlockSpec((r, tn), lambda n, t: (0, n + off))
    dcur = pl.BlockSpec((tm, tn), lambda n, t: (t, n))
    dnxt = pl.BlockSpec((16, tn), lambda n, t: (jnp.minimum((t + 1) * (tm // 16), t_all // 16 - 1), n))
    acc = pl.BlockSpec((8, tn), lambda n, t: (0, n))
    return pl.pallas_call(
        body, name="ffn_gate_bwd", grid=(nc, t_all // tm),
        in_specs=[cur(0), prev(0), nxt(0), cur(nc), prev(nc), nxt(nc), dcur, dnxt, vec(3, 0), vec(3, nc), vec(1, 0),
                  vec(1, nc)],
        out_specs=[dcur, dcur, acc, acc],
        out_shape=[jax.ShapeDtypeStruct((t_all, D_FF), BF16), jax.ShapeDtypeStruct((t_all, D_FF), BF16),
                   jax.ShapeDtypeStruct((8, D_FF), F32), jax.ShapeDtypeStruct((8, D_FF), F32)],
        scratch_shapes=[pltpu.VMEM((tm + 2 * HALO, tn), F32)] * 2
        + [pltpu.VMEM((tm // FFN_CHUNK, FFN_CHUNK + HALO, tn), F32)] * 2,
        compiler_params=_params(),
    )(u, u, u, u, u, u, dfi, dfi, conv_w, conv_w, conv_b, conv_b)


def _ffn_down(ffn_in, w_down, xh1, ln1_g, ln1_b, ada3, ln2_g, ln2_b, target, seq):
    t_all = xh1.shape[0]
    tm = 256
    nts = seq // tm

    def body(f_ref, w_ref, xh_ref, g1_ref, b1_ref, ada_ref, g2_ref, b2_ref, tg_ref, dr2_ref, acc_ref):
        i = pl.program_id(0)

        @pl.when(i == 0)
        def _():
            acc_ref[...] = jnp.zeros_like(acc_ref)
        ffn = _nn(f_ref[...], w_ref[...])
        x1 = xh_ref[...] * g1_ref[...] + b1_ref[...]
        r2 = ALPHA * x1 + ada_ref[0, 5:6, :] * ffn
        d = r2 - jnp.mean(r2, axis=1, keepdims=True)
        rstd = lax.rsqrt(jnp.mean(d * d, axis=1, keepdims=True) + LN_EPS)
        xh2 = d * rstd
        diff = xh2 * g2_ref[...] + b2_ref[...] - tg_ref[...]
        dy = diff * (1.0 / D_MODEL)
        dr2 = _layer_norm_bwd(dy * g2_ref[...], xh2, rstd)
        dr2_ref[...] = dr2
        acc_ref[0:1, :] += jnp.sum(dy * xh2, axis=0, keepdims=True)
        acc_ref[1:2, :] += jnp.sum(dy, axis=0, keepdims=True)
        acc_ref[2:3, :] += jnp.sum(diff * diff, axis=0, keepdims=True) * (0.5 / D_MODEL)
        acc_ref[pl.ds(8 + i // nts, 1), :] += jnp.sum(dr2 * ffn, axis=0, keepdims=True)

    tok = lambda w: pl.BlockSpec((tm, w), lambda i: (i, 0))
    vec = pl.BlockSpec((1, D_MODEL), lambda i: (0, 0))
    return pl.pallas_call(
        body, name="ffn_down", grid=(t_all // tm,),
        in_specs=[tok(D_FF), pl.BlockSpec(w_down.shape, lambda i: (0, 0)), tok(D_MODEL), vec, vec,
                  pl.BlockSpec((1, 6, D_MODEL), lambda i: (i // nts, 0, 0)), vec, vec, tok(D_MODEL)],
        out_specs=[tok(D_MODEL), pl.BlockSpec((16, D_MODEL), lambda i: (0, 0))],
        out_shape=[jax.ShapeDtypeStruct((t_all, D_MODEL), F32), jax.ShapeDtypeStruct((16, D_MODEL), F32)],
        compiler_params=_params(),
    )(ffn_in, w_down, xh1, ln1_g, ln1_b, ada3, ln2_g, ln2_b, target)


def _ffn_down_bwd(dr2, ada3, w_down, seq):
    t_all = dr2.shape[0]
    tm = 256
    nts = seq // tm

    def body(d_ref, ada_ref, w_ref, dffn_ref, dfi_ref):
        dffn = (d_ref[...] * ada_ref[0, 5:6, :]).astype(BF16)
        dffn_ref[...] = dffn
        dfi_ref[...] = _nt(dffn, w_ref[...]).astype(BF16)

    tok = lambda w: pl.BlockSpec((tm, w), lambda i: (i, 0))
    return pl.pallas_call(
        body, name="ffn_down_bwd", grid=(t_all // tm,),
        in_specs=[tok(D_MODEL), pl.BlockSpec((1, 6, D_MODEL), lambda i: (i // nts, 0, 0)),
                  pl.BlockSpec(w_down.shape, lambda i: (0, 0))],
        out_specs=[tok(D_MODEL), tok(D_FF)],
        out_shape=[jax.ShapeDtypeStruct((t_all, D_MODEL), BF16), jax.ShapeDtypeStruct((t_all, D_FF), BF16)],
        compiler_params=_params(),
    )(dr2, ada3, w_down)


def _ffn_up_bwd(du_a, du_g, w_up, dr2, xh1, rs1, mix, ada3, ln1_g, ln1_b, seq):
    t_all = dr2.shape[0]
    tm = 256
    nts = seq // tm

    def body(da_ref, dg_ref, w_ref, dr2_ref, xh_ref, rs_ref, mix_ref, ada_ref, g_ref, b_ref, dr1_ref, dmix_ref,
             acc_ref):
        i = pl.program_id(0)

        @pl.when(i == 0)
        def _():
            acc_ref[...] = jnp.zeros_like(acc_ref)
        dh2 = _nt(da_ref[...], w_ref[:, :D_FF]) + _nt(dg_ref[...], w_ref[:, D_FF:])
        xh = xh_ref[...]
        x1 = xh * g_ref[...] + b_ref[...]
        dx1 = ALPHA * dr2_ref[...] + dh2 * (1.0 + ada_ref[0, 4:5, :])
        dr1 = _layer_norm_bwd(dx1 * g_ref[...], xh, rs_ref[:, 0:1])
        dr1_ref[...] = dr1
        dmix_ref[...] = (dr1 * ada_ref[0, 2:3, :]).astype(BF16)
        b = i // nts
        acc_ref[0:1, :] += jnp.sum(dx1 * xh, axis=0, keepdims=True)
        acc_ref[1:2, :] += jnp.sum(dx1, axis=0, keepdims=True)
        acc_ref[pl.ds(8 + b, 1), :] += jnp.sum(dh2 * x1, axis=0, keepdims=True)
        acc_ref[pl.ds(16 + b, 1), :] += jnp.sum(dh2, axis=0, keepdims=True)
        acc_ref[pl.ds(24 + b, 1), :] += jnp.sum(dr1 * mix_ref[...].astype(F32), axis=0, keepdims=True)

    tok = lambda w: pl.BlockSpec((tm, w), lambda i: (i, 0))
    vec = pl.BlockSpec((1, D_MODEL), lambda i: (0, 0))
    return pl.pallas_call(
        body, name="ffn_up_bwd", grid=(t_all // tm,),
        in_specs=[tok(D_FF), tok(D_FF), pl.BlockSpec(w_up.shape, lambda i: (0, 0)), tok(D_MODEL), tok(D_MODEL),
                  tok(LANES), tok(D_MODEL), pl.BlockSpec((1, 6, D_MODEL), lambda i: (i // nts, 0, 0)), vec, vec],
        out_specs=[tok(D_MODEL), tok(D_MODEL), pl.BlockSpec((32, D_MODEL), lambda i: (0, 0))],
        out_shape=[jax.ShapeDtypeStruct((t_all, D_MODEL), F32), jax.ShapeDtypeStruct((t_all, D_MODEL), BF16),
                   jax.ShapeDtypeStruct((32, D_MODEL), F32)],
        compiler_params=_params(),
    )(du_a, du_g, w_up, dr2, xh1, rs1, mix, ada3, ln1_g, ln1_b)


def _perm(a, d, seq):
    if d == 1:
        return a
    t_all, w = a.shape
    return a.reshape(t_all // seq, seq // d, d, w).transpose(0, 2, 1, 3).reshape(t_all, w)


def _unperm(a, d, seq):
    if d == 1:
        return a
    t_all, w = a.shape
    return a.reshape(t_all // seq, d, seq // d, w).transpose(0, 2, 1, 3).reshape(t_all, w)


DILATIONS = (1, 4, 16)


def _stack_perm(a, seq):
    return jnp.stack([_perm(a, d, seq) for d in DILATIONS])


def _rows(a):
    return a[:, :N_HEADS].T


def _rope_freq():
    f = np.float32(ROPE_THETA) ** (-np.arange(0, ROPE_DIMS, 2, dtype=np.float32) / np.float32(ROPE_DIMS))
    return jnp.asarray(np.tile(f.astype(np.float32), LANES // (ROPE_DIMS // 2))[None, :])


def _local_step(x, positions, target, ada3, w_in, b_fgate, gn_a, gn_b, ln1_g, ln1_b, conv_b, ln2_g, ln2_b,
                late_shards):
    nbat, seq, _ = x.shape
    t_all = nbat * seq
    xf = x.reshape(t_all, D_MODEL)
    tg = target.reshape(t_all, D_MODEL)
    pos = positions.reshape(t_all, 1)
    freq = _rope_freq()

    wqkv = jnp.concatenate([w_in[:, :3 * WIDTH], w_in[:, 3 * WIDTH + N_HEADS:]], axis=1)
    wf16 = jnp.zeros((16, D_MODEL), BF16).at[:N_HEADS].set(w_in[:, 3 * WIDTH:3 * WIDTH + N_HEADS].T)
    bf = b_fgate.reshape(N_HEADS, 1)

    h1, za, zb, fa_t = _inproj(xf, ada3, pos, wqkv, wf16, freq, seq)
    f_row = _fgate_fwd(fa_t, bf, seq)
    f_col = jnp.zeros((t_all, LANES), F32).at[:, :N_HEADS].set(f_row.T)
    vt = za[:, 2 * WIDTH:].reshape(nbat, seq, WIDTH).transpose(0, 2, 1).reshape(nbat * WIDTH, seq)
    oa, lse_row_a, gathered = _fox_fwd(za, vt, f_col, seq, [late_shards[n] for n in LATE])
    w_out, w_up, conv_w, w_down = (_full_from_gathered(n, g) for n, g in zip(LATE, gathered))
    lse_a = jnp.zeros((t_all, LANES), F32).at[:, :N_HEADS].set(lse_row_a.T)
    qs = _stack_perm(zb[:, :WIDTH], seq)
    ks = _stack_perm(zb[:, WIDTH:2 * WIDTH], seq)
    vs = _stack_perm(zb[:, 2 * WIDTH:], seq)
    o3p, l3p = _dil_fwd(qs, ks, vs, seq)
    o3 = [_unperm(o3p[p], d, seq) for p, d in enumerate(DILATIONS)]
    l3 = [_unperm(l3p[p], d, seq) for p, d in enumerate(DILATIONS)]
    ob, lse_b, merged, mix, xh1, rs1, h2 = _mix_out(oa, o3, l3, gn_a, gn_b, w_out, xf, ada3, ln1_g, ln1_b, seq)
    u = _matmul(h2, w_up, False, F32, 256, 512, "ffn_up")
    ffn_in = _ffn_gate(u, conv_w, conv_b, seq)
    dr2, acc2 = _ffn_down(ffn_in, w_down, xh1, ln1_g, ln1_b, ada3, ln2_g, ln2_b, tg, seq)

    dffn, dfi = _ffn_down_bwd(dr2, ada3, w_down, seq)
    d_w_down = _matmul_tn(ffn_in, dffn, 512, 512, "dw_down")
    du_a, du_g, acc_ca, acc_cg = _ffn_gate_bwd(u, dfi, conv_w, conv_b, seq)
    dr1, dmix, acc1 = _ffn_up_bwd(du_a, du_g, w_up, dr2, xh1, rs1, mix, ada3, ln1_g, ln1_b, seq)
    d_w_up = jnp.concatenate([_matmul_tn(h2, du_a, 256, 512, "dw_up_a"), _matmul_tn(h2, du_g, 256, 512, "dw_up_g")],
                             axis=1)

    doa, dob, dl_a, dl_b, acc_gn = _mix_out_bwd(dmix, w_out, oa, ob, gn_a, gn_b)
    d_w_out = _matmul_tn(merged, dmix, 512, 512, "dw_out")
    dqa, df_q = _fox_bwd_dq(za, doa, f_row, lse_a, dl_a, seq)
    late_grads = dict(w_out=d_w_out, w_up=d_w_up, conv_w=jnp.concatenate([acc_ca[0:3], acc_cg[0:3]], axis=1),
                      w_down=d_w_down)
    dka, dva, df_k, late_parts = _fox_bwd_dkv(za, doa, f_col, lse_row_a, _rows(dl_a), seq,
                                              [_payload(n, _dest_major(n, late_grads[n])) for n in LATE])
    dfa_t, dbf = _fgate_bwd(_rows(df_q + df_k), fa_t, bf, seq)
    dos = _stack_perm(dob, seq)
    lses = _stack_perm(lse_b, seq)
    dls = _stack_perm(dl_b, seq)
    dq3 = _dil_bwd_dq(qs, ks, vs, dos, lses, dls, seq)
    lse_rows = lses[:, :, :N_HEADS].transpose(0, 2, 1)
    dl_rows = dls[:, :, :N_HEADS].transpose(0, 2, 1)
    dk3, dv3 = _dil_bwd_dkv(qs, ks, vs, dos, lse_rows, dl_rows, seq)
    unsum = lambda a3: sum(_unperm(a3[p], d, seq) for p, d in enumerate(DILATIONS))
    dza = jnp.concatenate([dqa, dka, dva], axis=1)
    dfa16 = jnp.zeros((16, t_all), BF16).at[:N_HEADS].set(dfa_t.astype(BF16))
    grad_x, dz, acc0 = _inproj_bwd(dza, unsum(dq3), unsum(dk3), unsum(dv3), dfa16, pos, wqkv, wf16, freq, dr1, xf,
                                   ada3, seq)
    d_wqkv = _matmul_tn(h1, dz, 512, 512, "dw_in")
    d_wf = _matmul_rows(dfa16, h1, 512, "dw_fgate")[:N_HEADS].T
    d_w_in = jnp.concatenate([d_wqkv[:, :3 * WIDTH], d_wf, d_wqkv[:, 3 * WIDTH:]], axis=1)

    dada = jnp.concatenate([acc0[8:8 + nbat], acc0[:nbat], acc1[24:24 + nbat], acc1[16:16 + nbat], acc1[8:8 + nbat],
                            acc2[8:8 + nbat]], axis=1)

    grads = dict(
        dada=dada, b_ada=jnp.sum(dada, axis=0, keepdims=True), w_in=d_w_in, b_fgate=dbf[:, 0][None, :],
        gn_a=acc_gn[0:1, :WIDTH], gn_b=acc_gn[0:1, WIDTH:], ln1_g=acc1[0:1], ln1_b=acc1[1:2],
        conv_b=jnp.concatenate([acc_ca[3:4], acc_cg[3:4]], axis=1), ln2_g=acc2[0:1], ln2_b=acc2[1:2])
    return acc2[2:3], grad_x.reshape(x.shape), grads, dict(zip(LATE, late_parts))


LATE = ("w_out", "w_up", "conv_w", "w_down")
BIG = ("w_ada", "w_in") + LATE
COLUMN_SHARDED = ("w_ada", "w_in", "w_up", "conv_w")


def _payload(name, a):
    return a if name == "conv_w" else a.astype(BF16)
SMALL = ("b_ada", "b_fgate", "gn_a", "gn_b", "ln1_g", "ln1_b", "conv_b", "ln2_g", "ln2_b")
ADAM_ROWS = dict(w_ada=256, w_in=256, w_out=128, w_up=256, conv_w=3, w_down=176)
SMALL_ROWS = 24


def _full_from_gathered(name, g):
    if name in COLUMN_SHARDED:
        return g.transpose(1, 0, 2).reshape(g.shape[1], N_DEV * g.shape[2])
    return g.reshape(N_DEV * g.shape[1], g.shape[2])


def _dest_major(name, full):
    if name in COLUMN_SHARDED:
        r, cfull = full.shape
        return full.reshape(r, N_DEV, cfull // N_DEV).transpose(1, 0, 2)
    return full.reshape(N_DEV, full.shape[0] // N_DEV, full.shape[1])


def _pack_small(vals, extra=None):
    parts = [vals[n].reshape(-1) for n in SMALL]
    if extra is not None:
        parts.append(extra.reshape(-1))
    flat = jnp.concatenate(parts)
    return jnp.pad(flat, (0, SMALL_ROWS * D_MODEL - flat.shape[0])).reshape(SMALL_ROWS, D_MODEL)


def _unpack_small(packed, like):
    flat = packed.reshape(-1)
    out, off = {}, 0
    for n in SMALL:
        size = like[n].size
        out[n] = flat[off:off + size].reshape(like[n].shape)
        off += size
    return out, flat[off:off + D_MODEL]


def kernel(x, c, positions, w_ada, b_ada, w_in, b_fgate, gn_a, gn_b, w_out, ln1_g, ln1_b, w_up, conv_w, conv_b, w_down, ln2_g, ln2_b, loss_target, m_w_ada, m_b_ada, m_w_in, m_b_fgate, m_gn_a, m_gn_b, m_w_out, m_ln1_g, m_ln1_b, m_w_up, m_conv_w, m_conv_b, m_w_down, m_ln2_g, m_ln2_b, v_w_ada, v_b_ada, v_w_in, v_b_fgate, v_gn_a, v_gn_b, v_w_out, v_ln1_g, v_ln1_b, v_w_up, v_conv_w, v_conv_b, v_w_down, v_ln2_g, v_ln2_b):
    w = dict(w_ada=w_ada[0], b_ada=b_ada, w_in=w_in[0], b_fgate=b_fgate, gn_a=gn_a, gn_b=gn_b, w_out=w_out[0],
             ln1_g=ln1_g, ln1_b=ln1_b, w_up=w_up[0], conv_w=conv_w[0], conv_b=conv_b, w_down=w_down[0], ln2_g=ln2_g,
             ln2_b=ln2_b)
    m = dict(w_ada=m_w_ada[0], b_ada=m_b_ada, w_in=m_w_in[0], b_fgate=m_b_fgate, gn_a=m_gn_a, gn_b=m_gn_b,
             w_out=m_w_out[0], ln1_g=m_ln1_g, ln1_b=m_ln1_b, w_up=m_w_up[0], conv_w=m_conv_w[0], conv_b=m_conv_b,
             w_down=m_w_down[0], ln2_g=m_ln2_g, ln2_b=m_ln2_b)
    v = dict(w_ada=v_w_ada[0], b_ada=v_b_ada, w_in=v_w_in[0], b_fgate=v_b_fgate, gn_a=v_gn_a, gn_b=v_gn_b,
             w_out=v_w_out[0], ln1_g=v_ln1_g, ln1_b=v_ln1_b, w_up=v_w_up[0], conv_w=v_conv_w[0], conv_b=v_conv_b,
             w_down=v_w_down[0], ln2_g=v_ln2_g, ln2_b=v_ln2_b)

    nbat = x.shape[0]
    me = 4 * lax.axis_index("x") + 2 * lax.axis_index("y") + lax.axis_index("c")
    ada_cols = w["w_ada"].shape[1]

    c_all, w_in_all = _exchange([c, _payload("w_in", w["w_in"])], [True, True], "weight_gather")
    c_all = c_all.reshape(N_DEV * nbat, D_MODEL)
    ada_mine = _ada_fwd(c_all, w["w_ada"], lax.dynamic_slice(b_ada, (0, me * ada_cols), (1, ada_cols)))
    (ada_parts,) = _exchange([ada_mine.reshape(N_DEV, nbat, ada_cols)], [False], "ada_exchange")
    ada3 = ada_parts.transpose(1, 0, 2).reshape(nbat, 6, D_MODEL)

    loss_lanes, grad_x, g_local, parts = _local_step(
        x, positions, loss_target, ada3, _full_from_gathered("w_in", w_in_all), b_fgate, gn_a, gn_b, ln1_g, ln1_b,
        conv_b, ln2_g, ln2_b, {n: _payload(n, w[n]) for n in LATE})

    parts["w_in"], dada_all, small_all = _exchange(
        [_payload("w_in", _dest_major("w_in", g_local["w_in"])), g_local["dada"], _pack_small(g_local, loss_lanes)],
        [False, True, True], "grad_exchange")
    dada_cols = lax.dynamic_slice(dada_all.reshape(N_DEV * nbat, 6 * D_MODEL), (0, me * ada_cols),
                                  (N_DEV * nbat, ada_cols))
    parts["w_ada"] = _ada_bwd(c_all, dada_cols)[None]

    grad, delta, new_m, new_v = {}, {}, {}, {}
    for n in BIG:
        grad[n], delta[n], new_m[n], new_v[n] = (
            a[None] for a in _adamw(parts[n], w[n], m[n], v[n], ADAM_ROWS[n], "adamw_" + n))
    packed = _adamw(small_all, _pack_small(w), _pack_small(m), _pack_small(v), SMALL_ROWS, "adamw_small")
    for dst, pk in zip((grad, delta, new_m, new_v), packed):
        vals, lanes = _unpack_small(pk, w)
        dst.update(vals)
        if dst is grad:
            loss = jnp.sum(lanes)

    order = ("w_ada", "b_ada", "w_in", "b_fgate", "gn_a", "gn_b", "w_out", "ln1_g", "ln1_b", "w_up", "conv_w", "conv_b",
             "w_down", "ln2_g", "ln2_b")
    return (loss, grad_x, *[grad[n] for n in order], *[delta[n] for n in order], *[new_m[n] for n in order],
            *[new_v[n] for n in order])
```

```python
import functools

import numpy as np
import jax
import jax.numpy as jnp
from jax import lax
from jax.experimental import pallas as pl
from jax.experimental.pallas import tpu as pltpu

F32, BF16 = jnp.float32, jnp.bfloat16
HIGHEST = lax.Precision.HIGHEST
MESH = pl.DeviceIdType.MESH
ANY = pl.BlockSpec(memory_space=pl.ANY)

D_MODEL = 1024
N_HEADS = 8
HEAD_DIM = 64
WIDTH = 512
D_FF = 2816
N_DEV = 8
ROPE_DIMS = 16
ROPE_THETA = 500000.0
ALPHA = 2.0 ** 0.25
LN_EPS = 1e-5
RMS_EPS = 1e-6
NEG = -1e30
Q_SCALE = 0.125
BLK = 128
LANES = 128
VMEM_LIMIT_BYTES = 56 * 1024 * 1024

ADAM_LR, ADAM_B1, ADAM_B2, ADAM_EPS, ADAM_WD, ADAM_STEP = 0.001, 0.9, 0.999, 1e-08, 0.01, 10


def _params(vmem=VMEM_LIMIT_BYTES):
    return pltpu.CompilerParams(vmem_limit_bytes=vmem)


def _nn(a, b):
    return jnp.dot(a, b, preferred_element_type=F32)


def _nt(a, b):
    return lax.dot_general(a, b, (((1,), (1,)), ((), ())), preferred_element_type=F32)


def _tn(a, b):
    return lax.dot_general(a, b, (((0,), (0,)), ((), ())), preferred_element_type=F32)


def _head_mats():
    r = lax.broadcasted_iota(jnp.int32, (LANES, WIDTH), 0)
    c = lax.broadcasted_iota(jnp.int32, (LANES, WIDTH), 1)
    e = ((c >> 6) == r).astype(F32)
    r2 = lax.broadcasted_iota(jnp.int32, (WIDTH, LANES), 0)
    c2 = lax.broadcasted_iota(jnp.int32, (WIDTH, LANES), 1)
    et = ((r2 >> 6) == c2).astype(F32)
    return e, et


def _hexp(w, e):
    return jnp.dot(w, e, precision=HIGHEST, preferred_element_type=F32)


def _hsum(x, et):
    return jnp.dot(x, et, precision=HIGHEST, preferred_element_type=F32)


def _rope_tabs(pos_ref, fr_ref, sign):
    ang = pos_ref[...].astype(F32) * fr_ref[...]
    lane = lax.broadcasted_iota(jnp.int32, ang.shape, 1) & (HEAD_DIM - 1)
    m1 = lane < ROPE_DIMS // 2
    m2 = (lane >= ROPE_DIMS // 2) & (lane < ROPE_DIMS)
    cos = jnp.cos(ang)
    sin = jnp.sin(ang) * sign
    return (jnp.where(m1 | m2, cos, 1.0), jnp.where(m1, -sin, 0.0), jnp.where(m2, sin, 0.0))


def _rope(z, tabs):
    c, s1, s2 = tabs
    parts = []
    for p in range(z.shape[1] // LANES):
        zp = z[:, LANES * p:LANES * (p + 1)]
        parts.append(zp * c + pltpu.roll(zp, LANES - 8, 1) * s1 + pltpu.roll(zp, 8, 1) * s2)
    return jnp.concatenate(parts, axis=1)


def _half_masks(rows):
    lane = lax.broadcasted_iota(jnp.int32, (rows, LANES), 1)
    lo = lane < HEAD_DIM
    return lo, jnp.logical_not(lo)


def _layer_norm_bwd(dxh, xh, rstd):
    m1 = jnp.mean(dxh, axis=1, keepdims=True)
    m2 = jnp.mean(dxh * xh, axis=1, keepdims=True)
    return rstd * (dxh - m1 - xh * m2)


def _coords():
    return lax.axis_index("x"), lax.axis_index("y"), lax.axis_index("c")


def _peer(x, y, c, k):
    return (1 - x if k & 4 else x, 1 - y if k & 2 else y, 1 - c if k & 1 else c)


def _comm_sems(n):
    return [pltpu.SemaphoreType.DMA((N_DEV - 1, n)), pltpu.SemaphoreType.DMA((N_DEV - 1, n)),
            pltpu.SemaphoreType.DMA((n,))]


def _comm_copies(ins, outs, to_all, sems):
    send_sems, recv_sems, local_sems = sems
    x, y, c = _coords()
    me = 4 * x + 2 * y + c
    copies = [pltpu.make_async_copy(ins[t] if to_all[t] else ins[t].at[me], outs[t].at[me], local_sems.at[t])
              for t in range(len(ins))]
    for k in range(1, N_DEV):
        px, py, pc = _peer(x, y, c, k)
        dest = 4 * px + 2 * py + pc
        for t in range(len(ins)):
            copies.append(pltpu.make_async_remote_copy(
                src_ref=ins[t] if to_all[t] else ins[t].at[dest], dst_ref=outs[t].at[me],
                send_sem=send_sems.at[k - 1, t], recv_sem=recv_sems.at[k - 1, t],
                device_id=(px, py, pc), device_id_type=MESH))
    return copies


def _comm_out_shapes(ins, to_all):
    return [jax.ShapeDtypeStruct(((N_DEV,) + a.shape) if ta else a.shape, a.dtype) for a, ta in zip(ins, to_all)]


def _exchange(ins, to_all, name):
    n = len(ins)

    def body(*refs):
        copies = _comm_copies(refs[:n], refs[n:2 * n], to_all, refs[2 * n:])
        for cp in copies:
            cp.start()
        for cp in copies:
            cp.wait()

    return pl.pallas_call(
        body, name=name, out_shape=_comm_out_shapes(ins, to_all), in_specs=[ANY] * n, out_specs=[ANY] * n,
        scratch_shapes=_comm_sems(n),
    )(*ins)


def _adamw(parts, w, m, v, rows, name):
    n_parts, r_all, cols = parts.shape
    c1 = 1.0 - ADAM_B1 ** ADAM_STEP
    c2 = 1.0 - ADAM_B2 ** ADAM_STEP

    def body(p_ref, w_ref, m_ref, v_ref, g_ref, d_ref, mo_ref, vo_ref):
        g = p_ref[0].astype(F32)
        for s in range(1, n_parts):
            g = g + p_ref[s].astype(F32)
        mn = ADAM_B1 * m_ref[...] + (1.0 - ADAM_B1) * g
        vn = ADAM_B2 * v_ref[...] + (1.0 - ADAM_B2) * (g * g)
        m_hat = mn / c1
        v_hat = vn / c2
        g_ref[...] = g
        d_ref[...] = -ADAM_LR * (m_hat / (jnp.sqrt(v_hat) + ADAM_EPS) + ADAM_WD * w_ref[...])
        mo_ref[...] = mn
        vo_ref[...] = vn

    spec = pl.BlockSpec((rows, cols), lambda i: (i, 0))
    return pl.pallas_call(
        body, name=name, grid=(r_all // rows,),
        in_specs=[pl.BlockSpec((n_parts, rows, cols), lambda i: (0, i, 0)), spec, spec, spec],
        out_specs=[spec] * 4, out_shape=[jax.ShapeDtypeStruct((r_all, cols), F32)] * 4,
        compiler_params=_params(),
    )(parts, w, m, v)


def _matmul(a, w, transposed_w, out_dtype, tm, chunk, name):
    t_all, k = a.shape
    n = w.shape[0] if transposed_w else w.shape[1]

    def body(a_ref, w_ref, o_ref):
        av = a_ref[...]
        for j in range(n // chunk):
            cs = slice(j * chunk, (j + 1) * chunk)
            r = _nt(av, w_ref[cs, :]) if transposed_w else _nn(av, w_ref[:, cs])
            o_ref[:, cs] = r.astype(out_dtype)

    return pl.pallas_call(
        body, name=name, grid=(t_all // tm,),
        in_specs=[pl.BlockSpec((tm, k), lambda i: (i, 0)), pl.BlockSpec(w.shape, lambda i: (0, 0))],
        out_specs=pl.BlockSpec((tm, n), lambda i: (i, 0)),
        out_shape=jax.ShapeDtypeStruct((t_all, n), out_dtype), compiler_params=_params(),
    )(a, w)


def _matmul_tn(a, b, tn, tk, name):
    t_all, k1 = a.shape
    n = b.shape[1]

    def body(a_ref, b_ref, o_ref):
        @pl.when(pl.program_id(1) == 0)
        def _():
            o_ref[...] = jnp.zeros_like(o_ref)
        o_ref[...] += _tn(a_ref[...], b_ref[...])

    return pl.pallas_call(
        body, name=name, grid=(n // tn, t_all // tk),
        in_specs=[pl.BlockSpec((tk, k1), lambda j, t: (t, 0)), pl.BlockSpec((tk, tn), lambda j, t: (t, j))],
        out_specs=pl.BlockSpec((k1, tn), lambda j, t: (0, j)),
        out_shape=jax.ShapeDtypeStruct((k1, n), F32), compiler_params=_params(),
    )(a, b)


def _matmul_rows(a, b, tk, name):
    r, t_all = a.shape
    n = b.shape[1]

    def body(a_ref, b_ref, o_ref):
        @pl.when(pl.program_id(0) == 0)
        def _():
            o_ref[...] = jnp.zeros_like(o_ref)
        o_ref[...] += _nn(a_ref[...], b_ref[...])

    return pl.pallas_call(
        body, name=name, grid=(t_all // tk,),
        in_specs=[pl.BlockSpec((r, tk), lambda t: (0, t)), pl.BlockSpec((tk, n), lambda t: (t, 0))],
        out_specs=pl.BlockSpec((r, n), lambda t: (0, 0)),
        out_shape=jax.ShapeDtypeStruct((r, n), F32), compiler_params=_params(),
    )(a, b)


def _ada_fwd(c_all, w_ada, b_ada):
    whole = lambda a: pl.BlockSpec(a.shape, lambda j: (0, 0))

    def body(c_ref, w_ref, b_ref, o_ref):
        cv = c_ref[...]
        s = (cv * jax.nn.sigmoid(cv)).astype(BF16)
        o_ref[...] = _nn(s, w_ref[...].astype(BF16)) + b_ref[...]

    out = jax.ShapeDtypeStruct((c_all.shape[0], w_ada.shape[1]), F32)
    return pl.pallas_call(
        body, name="ada_fwd", grid=(1,), in_specs=[whole(c_all), whole(w_ada), whole(b_ada)], out_specs=whole(out),
        out_shape=out, compiler_params=_params(),
    )(c_all, w_ada, b_ada)


def _ada_bwd(c_all, dada):
    whole = lambda a: pl.BlockSpec(a.shape, lambda j: (0, 0))

    def body(c_ref, d_ref, o_ref):
        cv = c_ref[...]
        s = (cv * jax.nn.sigmoid(cv)).astype(BF16)
        o_ref[...] = _tn(s, d_ref[...].astype(BF16))

    out = jax.ShapeDtypeStruct((D_MODEL, dada.shape[1]), F32)
    return pl.pallas_call(
        body, name="ada_bwd", grid=(1,), in_specs=[whole(c_all), whole(dada)], out_specs=whole(out), out_shape=out,
        compiler_params=_params(),
    )(c_all, dada)


def _inproj(x, ada3, pos, wqkv, wf16, freq, seq):
    t_all = x.shape[0]
    tm = 256
    nts = seq // tm

    def body(x_ref, ada_ref, pos_ref, w_ref, wf_ref, fr_ref, h1_ref, za_ref, zb_ref, fa_ref):
        h1 = (x_ref[...] * (1.0 + ada_ref[0, 1:2, :]) + ada_ref[0, 0:1, :]).astype(BF16)
        h1_ref[...] = h1
        tabs = _rope_tabs(pos_ref, fr_ref, 1.0)
        for n in range(6):
            z = _nn(h1, w_ref[:, n * WIDTH:(n + 1) * WIDTH])
            if n in (3, 4):
                z = _rope(z, tabs)
            if n in (0, 3):
                z = z * Q_SCALE
            dst = za_ref if n < 3 else zb_ref
            dst[:, (n % 3) * WIDTH:(n % 3 + 1) * WIDTH] = z.astype(BF16)
        fa_ref[...] = _nt(wf_ref[...], h1)[:N_HEADS]

    tok = lambda w: pl.BlockSpec((tm, w), lambda i: (i, 0))
    return pl.pallas_call(
        body, name="inproj", grid=(t_all // tm,),
        in_specs=[tok(D_MODEL), pl.BlockSpec((1, 6, D_MODEL), lambda i: (i // nts, 0, 0)), tok(1),
                  pl.BlockSpec(wqkv.shape, lambda i: (0, 0)), pl.BlockSpec(wf16.shape, lambda i: (0, 0)),
                  pl.BlockSpec((1, LANES), lambda i: (0, 0))],
        out_specs=[tok(D_MODEL), tok(3 * WIDTH), tok(3 * WIDTH), pl.BlockSpec((N_HEADS, tm), lambda i: (0, i))],
        out_shape=[jax.ShapeDtypeStruct((t_all, D_MODEL), BF16), jax.ShapeDtypeStruct((t_all, 3 * WIDTH), BF16),
                   jax.ShapeDtypeStruct((t_all, 3 * WIDTH), BF16), jax.ShapeDtypeStruct((N_HEADS, t_all), F32)],
        compiler_params=_params(),
    )(x, ada3, pos, wqkv, wf16, freq)


def _fgate_fwd(fa_t, bf, seq):
    t_all = fa_t.shape[1]

    def body(fa_ref, b_ref, f_ref):
        lane = lax.broadcasted_iota(jnp.int32, (N_HEADS, LANES), 1)

        def chunk(j, carry):
            sl = pl.ds(pl.multiple_of(j * LANES, LANES), LANES)
            xv = fa_ref[:, sl] + b_ref[...]
            lf = jnp.minimum(xv, 0.0) - jnp.log(1.0 + jnp.exp(-jnp.abs(xv)))
            for s in (1, 2, 4, 8, 16, 32, 64):
                lf = lf + jnp.where(lane >= s, pltpu.roll(lf, s, 1), 0.0)
            lf = lf + carry
            f_ref[:, sl] = lf
            return lf[:, LANES - 1:LANES]

        lax.fori_loop(0, seq // LANES, chunk, jnp.zeros((N_HEADS, 1), F32))

    return pl.pallas_call(
        body, name="fgate_fwd", grid=(t_all // seq,),
        in_specs=[pl.BlockSpec((N_HEADS, seq), lambda b: (0, b)), pl.BlockSpec((N_HEADS, 1), lambda b: (0, 0))],
        out_specs=pl.BlockSpec((N_HEADS, seq), lambda b: (0, b)),
        out_shape=jax.ShapeDtypeStruct((N_HEADS, t_all), F32), compiler_params=_params(),
    )(fa_t, bf)


def _fgate_bwd(df_t, fa_t, bf, seq):
    t_all = fa_t.shape[1]

    def body(df_ref, fa_ref, b_ref, o_ref, s_ref):
        lane = lax.broadcasted_iota(jnp.int32, (N_HEADS, LANES), 1)

        @pl.when(pl.program_id(0) == 0)
        def _():
            s_ref[...] = jnp.zeros_like(s_ref)

        def chunk(jj, carry):
            car, tot = carry
            j = seq // LANES - 1 - jj
            sl = pl.ds(pl.multiple_of(j * LANES, LANES), LANES)
            d = df_ref[:, sl]
            for s in (1, 2, 4, 8, 16, 32, 64):
                d = d + jnp.where(lane < LANES - s, pltpu.roll(d, LANES - s, 1), 0.0)
            d = d + car
            dfa = d * jax.nn.sigmoid(-(fa_ref[:, sl] + b_ref[...]))
            o_ref[:, sl] = dfa
            return d[:, 0:1], tot + jnp.sum(dfa, axis=1, keepdims=True)

        z = jnp.zeros((N_HEADS, 1), F32)
        _, tot = lax.fori_loop(0, seq // LANES, chunk, (z, z))
        s_ref[...] += jnp.broadcast_to(tot, (N_HEADS, LANES))

    row = pl.BlockSpec((N_HEADS, seq), lambda b: (0, b))
    return pl.pallas_call(
        body, name="fgate_bwd", grid=(t_all // seq,),
        in_specs=[row, row, pl.BlockSpec((N_HEADS, 1), lambda b: (0, 0))],
        out_specs=[row, pl.BlockSpec((N_HEADS, LANES), lambda b: (0, 0))],
        out_shape=[jax.ShapeDtypeStruct((N_HEADS, t_all), F32), jax.ShapeDtypeStruct((N_HEADS, LANES), F32)],
        compiler_params=_params(),
    )(df_t, fa_t, bf)


FOX_T = 256


def _fox_prep(dst, src_ref, lo, hi):
    for p in range(4):
        v = src_ref[:, LANES * p:LANES * (p + 1)]
        dst[2 * p] = jnp.where(lo, v, jnp.zeros_like(v))
        dst[2 * p + 1] = jnp.where(hi, v, jnp.zeros_like(v))


def _fox_fwd(za, vt, f_col, seq, shards):
    t_all = za.shape[0]
    tq = FOX_T
    nq = seq // tq
    nbat = t_all // seq
    n = len(shards)
    to_all = [True] * n

    def body(*refs):
        q_ref, k_ref, vt_ref, fc_ref = refs[:4]
        o_ref, lse_ref = refs[4 + n:6 + n]
        qm_sc, m_sc, l_sc, acc_sc = refs[6 + 2 * n:10 + 2 * n]
        comm = (refs[4:4 + n], refs[6 + n:6 + 2 * n], to_all, refs[10 + 2 * n:])
        i = pl.program_id(1)

        @pl.when((pl.program_id(0) == 0) & (i == 0))
        def _():
            for cp in _comm_copies(*comm):
                cp.start()
        lo, hi = _half_masks(tq)
        r = lax.broadcasted_iota(jnp.int32, (tq, tq), 0)
        c = lax.broadcasted_iota(jnp.int32, (tq, tq), 1)
        tri = c >= r
        _fox_prep(qm_sc, q_ref, lo, hi)
        m_sc[...] = jnp.full(m_sc.shape, NEG, F32)
        l_sc[...] = jnp.zeros_like(l_sc)
        acc_sc[...] = jnp.zeros_like(acc_sc)

        def block(j, masked):
            sl = pl.ds(pl.multiple_of(j * tq, tq), tq)
            for p in range(4):
                kj = k_ref[sl, LANES * p:LANES * (p + 1)]
                for h in (2 * p, 2 * p + 1):
                    st = _nt(kj, qm_sc[h]) - fc_ref[sl, h:h + 1]
                    if masked:
                        st = jnp.where(tri, st, NEG)
                    m = m_sc[h:h + 1, :]
                    mn = jnp.maximum(m, jnp.max(st, axis=0, keepdims=True))
                    a = jnp.exp(m - mn)
                    pe = jnp.exp(st - mn)
                    m_sc[h:h + 1, :] = mn
                    l_sc[h:h + 1, :] = a * l_sc[h:h + 1, :] + jnp.sum(pe, axis=0, keepdims=True)
                    acc_sc[h] = a * acc_sc[h] + _nn(vt_ref[HEAD_DIM * h:HEAD_DIM * (h + 1), sl], pe.astype(BF16))

        def step(j, carry):
            block(j, False)
            return carry

        lax.fori_loop(0, i, step, 0)
        block(i, True)
        lse_ref[...] = m_sc[...] + jnp.log(l_sc[...])
        for p in range(4):
            ot = jnp.concatenate([acc_sc[h] / l_sc[h:h + 1, :] for h in (2 * p, 2 * p + 1)], axis=0)
            o_ref[:, LANES * p:LANES * (p + 1)] = ot.T

        @pl.when((pl.program_id(0) == nbat - 1) & (i == nq - 1))
        def _():
            for cp in _comm_copies(*comm):
                cp.wait()

    res = pl.pallas_call(
        body, name="fox_fwd", grid=(nbat, nq),
        in_specs=[pl.BlockSpec((tq, WIDTH), lambda b, i: (b * nq + i, 0)),
                  pl.BlockSpec((seq, WIDTH), lambda b, i: (b, 1)), pl.BlockSpec((WIDTH, seq), lambda b, i: (b, 0)),
                  pl.BlockSpec((seq, LANES), lambda b, i: (b, 0))] + [ANY] * n,
        out_specs=[pl.BlockSpec((tq, WIDTH), lambda b, i: (b * nq + i, 0)),
                   pl.BlockSpec((N_HEADS, tq), lambda b, i: (0, b * nq + i))] + [ANY] * n,
        out_shape=[jax.ShapeDtypeStruct((t_all, WIDTH), F32), jax.ShapeDtypeStruct((N_HEADS, t_all), F32)]
        + _comm_out_shapes(shards, to_all),
        scratch_shapes=[pltpu.VMEM((N_HEADS, tq, LANES), BF16), pltpu.VMEM((N_HEADS, tq), F32),
                        pltpu.VMEM((N_HEADS, tq), F32), pltpu.VMEM((N_HEADS, HEAD_DIM, tq), F32)] + _comm_sems(n),
        compiler_params=_params(),
    )(za, za, vt, f_col, *shards)
    return res[0], res[1], res[2:]


def _fox_bwd(za, do, f_col, lse_row, dl_row, seq, grads):
    t_all = za.shape[0]
    tk = FOX_T
    nk = seq // tk
    nbat = t_all // seq
    n = len(grads)
    to_all = [False] * n

    def body(*refs):
        k_ref, v_ref, q_ref, do_ref, fc_ref, lr_ref, dr_ref = refs[:7]
        dk_ref, dv_ref, df_ref, dqt_ref, dfq_ref = refs[7 + n:12 + n]
        km_sc, vm_sc, fk_sc, dk_sc, dv_sc, cs_sc, kt_sc = refs[12 + 2 * n:19 + 2 * n]
        comm = (refs[7:7 + n], refs[12 + n:12 + 2 * n], to_all, refs[19 + 2 * n:])
        j = pl.program_id(1)

        @pl.when(j == 0)
        def _():
            dqt_ref[...] = jnp.zeros_like(dqt_ref)
            dfq_ref[...] = jnp.zeros_like(dfq_ref)

        @pl.when((pl.program_id(0) == 0) & (j == 0))
        def _():
            for cp in _comm_copies(*comm):
                cp.start()
        lo, hi = _half_masks(tk)
        r = lax.broadcasted_iota(jnp.int32, (tk, tk), 0)
        c = lax.broadcasted_iota(jnp.int32, (tk, tk), 1)
        tri = c >= r
        _fox_prep(km_sc, k_ref, lo, hi)
        _fox_prep(vm_sc, v_ref, lo, hi)
        for h in range(N_HEADS):
            fk_sc[h] = jnp.broadcast_to(fc_ref[:, h:h + 1], (tk, tk))
        for p in range(4):
            kt_sc[p] = k_ref[:, LANES * p:LANES * (p + 1)].astype(F32).T.astype(BF16)
        dk_sc[...] = jnp.zeros_like(dk_sc)
        dv_sc[...] = jnp.zeros_like(dv_sc)
        cs_sc[...] = jnp.zeros_like(cs_sc)

        def block(i, masked):
            sl = pl.ds(pl.multiple_of(i * tk, tk), tk)
            for p in range(4):
                cs = slice(LANES * p, LANES * (p + 1))
                qi = q_ref[sl, cs]
                doi = do_ref[sl, cs]
                for h in (2 * p, 2 * p + 1):
                    st = _nt(km_sc[h], qi) - fk_sc[h] - lr_ref[h:h + 1, sl]
                    if masked:
                        st = jnp.where(tri, st, NEG)
                    pt = jnp.exp(st)
                    dst = pt * (_nt(vm_sc[h], doi) - dr_ref[h:h + 1, sl])
                    dsb = dst.astype(BF16)
                    dv_sc[h] += _nn(pt.astype(BF16), doi)
                    dk_sc[h] += _nn(dsb, qi)
                    cs_sc[h] += dst[:, :LANES] + dst[:, LANES:]
                    kt = kt_sc[p, HEAD_DIM * (h % 2):HEAD_DIM * (h % 2 + 1), :]
                    dqt_ref[HEAD_DIM * h:HEAD_DIM * (h + 1), sl] += _nn(kt, dsb)
                    dfq_ref[h:h + 1, sl] += jnp.sum(dst, axis=0, keepdims=True)

        def step(i, carry):
            block(i, False)
            return carry

        block(j, True)
        lax.fori_loop(j + 1, nk, step, 0)
        df_ref[...] = jnp.zeros_like(df_ref)
        for p in range(4):
            cs = slice(LANES * p, LANES * (p + 1))
            dk_ref[:, cs] = jnp.where(lo, dk_sc[2 * p], dk_sc[2 * p + 1]).astype(BF16)
            dv_ref[:, cs] = jnp.where(lo, dv_sc[2 * p], dv_sc[2 * p + 1]).astype(BF16)
            for h in (2 * p, 2 * p + 1):
                df_ref[:, h:h + 1] = -jnp.sum(cs_sc[h], axis=1, keepdims=True)

        @pl.when(j == nk - 1)
        def _():
            dqt_ref[...] = dqt_ref[...] * Q_SCALE

        @pl.when((pl.program_id(0) == nbat - 1) & (j == nk - 1))
        def _():
            for cp in _comm_copies(*comm):
                cp.wait()

    tile = lambda w, col: pl.BlockSpec((tk, w), lambda b, j: (b * nk + j, col))
    full = lambda col: pl.BlockSpec((seq, WIDTH), lambda b, j: (b, col))
    row = pl.BlockSpec((N_HEADS, seq), lambda b, j: (0, b))
    acc = pltpu.VMEM((N_HEADS, tk, LANES), F32)
    res = pl.pallas_call(
        body, name="fox_bwd", grid=(nbat, nk),
        in_specs=[tile(WIDTH, 1), tile(WIDTH, 2), full(0), full(0), tile(LANES, 0), row, row] + [ANY] * n,
        out_specs=[tile(WIDTH, 0), tile(WIDTH, 0), tile(LANES, 0), pl.BlockSpec((WIDTH, seq), lambda b, j: (b, 0)),
                   row] + [ANY] * n,
        out_shape=[jax.ShapeDtypeStruct((t_all, WIDTH), BF16), jax.ShapeDtypeStruct((t_all, WIDTH), BF16),
                   jax.ShapeDtypeStruct((t_all, LANES), F32), jax.ShapeDtypeStruct((nbat * WIDTH, seq), F32),
                   jax.ShapeDtypeStruct((N_HEADS, t_all), F32)] + _comm_out_shapes(grads, to_all),
        scratch_shapes=[pltpu.VMEM((N_HEADS, tk, LANES), BF16), pltpu.VMEM((N_HEADS, tk, LANES), BF16),
                        pltpu.VMEM((N_HEADS, tk, tk), F32), acc, acc, acc, pltpu.VMEM((4, LANES, tk), BF16)]
        + _comm_sems(n),
        compiler_params=_params(),
    )(za, za, za, do, f_col, lse_row, dl_row, *grads)
    return res[0], res[1], res[2], res[3], res[4], res[5:]


def _dil_mask(has_prev):
    qi = lax.broadcasted_iota(jnp.int32, (BLK, 2 * BLK), 0)
    kj = lax.broadcasted_iota(jnp.int32, (BLK, 2 * BLK), 1)
    dist = qi + BLK - kj
    return (dist >= 0) & (dist <= BLK) & ((kj >= BLK) | has_prev)


def _dil_specs(t_all):
    nb = t_all // BLK
    cur = pl.BlockSpec((1, BLK, WIDTH), lambda p, n: (p, n, 0))
    prev = pl.BlockSpec((1, BLK, WIDTH), lambda p, n: (p, jnp.maximum(n - 1, 0), 0))
    nxt = pl.BlockSpec((1, BLK, WIDTH), lambda p, n: (p, jnp.minimum(n + 1, nb - 1), 0))
    stat = pl.BlockSpec((1, BLK, LANES), lambda p, n: (p, n, 0))
    return nb, cur, prev, nxt, stat


def _dil_fwd(qs, ks, vs, seq):
    t_all = qs.shape[1]
    nb, cur, prev, _, stat = _dil_specs(t_all)

    def body(q_ref, kp_ref, kc_ref, vp_ref, vc_ref, o_ref, lse_ref):
        nbs = (seq // BLK) >> (2 * pl.program_id(0))
        mask = _dil_mask((pl.program_id(1) & (nbs - 1)) != 0)
        lo, hi = _half_masks(BLK)
        lse_ref[...] = jnp.zeros_like(lse_ref)
        for p in range(4):
            cs = slice(LANES * p, LANES * (p + 1))
            qp = q_ref[0, :, cs]
            kcat = jnp.concatenate([kp_ref[0, :, cs], kc_ref[0, :, cs]], axis=0)
            vcat = jnp.concatenate([vp_ref[0, :, cs], vc_ref[0, :, cs]], axis=0)
            res = []
            for e in (0, 1):
                h = 2 * p + e
                qe = jnp.where(lo if e == 0 else hi, qp, jnp.zeros_like(qp))
                s = jnp.where(mask, _nt(qe, kcat), NEG)
                m = jnp.max(s, axis=1, keepdims=True)
                pe = jnp.exp(s - m)
                l = jnp.sum(pe, axis=1, keepdims=True)
                res.append(_nn(pe.astype(BF16), vcat) / l)
                lse_ref[0, :, h:h + 1] = m + jnp.log(l)
            o_ref[0, :, cs] = jnp.where(lo, res[0], res[1])

    return pl.pallas_call(
        body, name="dil_fwd", grid=(3, nb), in_specs=[cur, prev, cur, prev, cur], out_specs=[cur, stat],
        out_shape=[jax.ShapeDtypeStruct((3, t_all, WIDTH), F32), jax.ShapeDtypeStruct((3, t_all, LANES), F32)],
        compiler_params=_params(),
    )(qs, ks, ks, vs, vs)


def _dil_bwd_dq(qs, ks, vs, dos, lses, dls, seq):
    t_all = qs.shape[1]
    nb, cur, prev, _, stat = _dil_specs(t_all)

    def body(q_ref, kp_ref, kc_ref, vp_ref, vc_ref, do_ref, lse_ref, dl_ref, dq_ref):
        nbs = (seq // BLK) >> (2 * pl.program_id(0))
        mask = _dil_mask((pl.program_id(1) & (nbs - 1)) != 0)
        lo, hi = _half_masks(BLK)
        for p in range(4):
            cs = slice(LANES * p, LANES * (p + 1))
            qp = q_ref[0, :, cs]
            dop = do_ref[0, :, cs]
            kcat = jnp.concatenate([kp_ref[0, :, cs], kc_ref[0, :, cs]], axis=0)
            vcat = jnp.concatenate([vp_ref[0, :, cs], vc_ref[0, :, cs]], axis=0)
            res = []
            for e in (0, 1):
                h = 2 * p + e
                sel = lo if e == 0 else hi
                qe = jnp.where(sel, qp, jnp.zeros_like(qp))
                doe = jnp.where(sel, dop, jnp.zeros_like(dop))
                s = jnp.where(mask, _nt(qe, kcat) - lse_ref[0, :, h:h + 1], NEG)
                ds = jnp.exp(s) * (_nt(doe, vcat) - dl_ref[0, :, h:h + 1])
                res.append(_nn(ds.astype(BF16), kcat))
            dq_ref[0, :, cs] = jnp.where(lo, res[0], res[1]) * Q_SCALE

    return pl.pallas_call(
        body, name="dil_bwd_dq", grid=(3, nb), in_specs=[cur, prev, cur, prev, cur, cur, stat, stat], out_specs=cur,
        out_shape=jax.ShapeDtypeStruct((3, t_all, WIDTH), F32), compiler_params=_params(),
    )(qs, ks, ks, vs, vs, dos, lses, dls)


def _dil_bwd_dkv(qs, ks, vs, dos, lse_rows, dl_rows, seq):
    t_all = qs.shape[1]
    nb, cur, _, nxt, _ = _dil_specs(t_all)
    rcur = pl.BlockSpec((1, N_HEADS, BLK), lambda p, n: (p, 0, n))
    rnxt = pl.BlockSpec((1, N_HEADS, BLK), lambda p, n: (p, 0, jnp.minimum(n + 1, nb - 1)))

    def body(k_ref, v_ref, qc_ref, qn_ref, dc_ref, dn_ref, lc_ref, ln_ref, ec_ref, en_ref, dk_ref, dv_ref):
        nbs = (seq // BLK) >> (2 * pl.program_id(0))
        has_next = ((pl.program_id(1) + 1) & (nbs - 1)) != 0
        r = lax.broadcasted_iota(jnp.int32, (BLK, 2 * BLK), 0)
        c = lax.broadcasted_iota(jnp.int32, (BLK, 2 * BLK), 1)
        mask = ((c < BLK) & (c >= r)) | ((c >= BLK) & (c - BLK <= r) & has_next)
        lo, hi = _half_masks(BLK)
        for p in range(4):
            cs = slice(LANES * p, LANES * (p + 1))
            kp = k_ref[0, :, cs]
            vp = v_ref[0, :, cs]
            qcat = jnp.concatenate([qc_ref[0, :, cs], qn_ref[0, :, cs]], axis=0)
            dcat = jnp.concatenate([dc_ref[0, :, cs], dn_ref[0, :, cs]], axis=0)
            rk, rv = [], []
            for e in (0, 1):
                h = 2 * p + e
                sel = lo if e == 0 else hi
                ke = jnp.where(sel, kp, jnp.zeros_like(kp))
                ve = jnp.where(sel, vp, jnp.zeros_like(vp))
                lrow = jnp.concatenate([lc_ref[0, h:h + 1, :], ln_ref[0, h:h + 1, :]], axis=1)
                erow = jnp.concatenate([ec_ref[0, h:h + 1, :], en_ref[0, h:h + 1, :]], axis=1)
                st = jnp.where(mask, _nt(ke, qcat) - lrow, NEG)
                pt = jnp.exp(st)
                dst = pt * (_nt(ve, dcat) - erow)
                rk.append(_nn(dst.astype(BF16), qcat))
                rv.append(_nn(pt.astype(BF16), dcat))
            dk_ref[0, :, cs] = jnp.where(lo, rk[0], rk[1])
            dv_ref[0, :, cs] = jnp.where(lo, rv[0], rv[1])

    return pl.pallas_call(
        body, name="dil_bwd_dkv", grid=(3, nb),
        in_specs=[cur, cur, cur, nxt, cur, nxt, rcur, rnxt, rcur, rnxt], out_specs=[cur, cur],
        out_shape=[jax.ShapeDtypeStruct((3, t_all, WIDTH), F32)] * 2, compiler_params=_params(),
    )(ks, vs, qs, qs, dos, dos, lse_rows, lse_rows, dl_rows, dl_rows)


def _mix_out(oa, o3, l3, gn_a, gn_b, w_out, x, ada3, ln_g, ln_b, seq):
    t_all = x.shape[0]
    tm = 256
    nts = seq // tm

    def body(oa_ref, o1_ref, o2_ref, o3_ref, l1_ref, l2_ref, l3_ref, ga_ref, gb_ref, w_ref, x_ref, ada_ref, g_ref,
             b_ref, ob_ref, lse_ref, mg_ref, mix_ref, xh_ref, rs_ref, h2_ref):
        e, et = _head_mats()
        la, lb, lc = l1_ref[...], l2_ref[...], l3_ref[...]
        mx = jnp.maximum(jnp.maximum(la, lb), lc)
        ea, eb, ec = jnp.exp(la - mx), jnp.exp(lb - mx), jnp.exp(lc - mx)
        tot = ea + eb + ec
        lse_ref[...] = mx + jnp.log(tot)
        ob = (o1_ref[...] * _hexp(ea / tot, e) + o2_ref[...] * _hexp(eb / tot, e) + o3_ref[...] * _hexp(ec / tot, e))
        ob_ref[...] = ob

        def rms(o, gain):
            rr = lax.rsqrt(_hsum(o * o, et) * (1.0 / HEAD_DIM) + RMS_EPS)
            return o * _hexp(rr, e) * gain

        merged = jnp.concatenate([rms(oa_ref[...], ga_ref[...]), rms(ob, gb_ref[...])], axis=1).astype(BF16)
        mg_ref[...] = merged
        mix = _nn(merged, w_ref[...])
        mix_ref[...] = mix.astype(BF16)
        r1 = ALPHA * x_ref[...] + ada_ref[0, 2:3, :] * mix
        d = r1 - jnp.mean(r1, axis=1, keepdims=True)
        rstd = lax.rsqrt(jnp.mean(d * d, axis=1, keepdims=True) + LN_EPS)
        xh = d * rstd
        xh_ref[...] = xh
        rs_ref[...] = jnp.broadcast_to(rstd, (tm, LANES))
        x1 = xh * g_ref[...] + b_ref[...]
        h2_ref[...] = (x1 * (1.0 + ada_ref[0, 4:5, :]) + ada_ref[0, 3:4, :]).astype(BF16)

    tok = lambda w: pl.BlockSpec((tm, w), lambda i: (i, 0))
    vec = lambda w: pl.BlockSpec((1, w), lambda i: (0, 0))
    return pl.pallas_call(
        body, name="mix_out", grid=(t_all // tm,),
        in_specs=[tok(WIDTH)] * 4 + [tok(LANES)] * 3 + [vec(WIDTH), vec(WIDTH),
                  pl.BlockSpec(w_out.shape, lambda i: (0, 0)), tok(D_MODEL),
                  pl.BlockSpec((1, 6, D_MODEL), lambda i: (i // nts, 0, 0)), vec(D_MODEL), vec(D_MODEL)],
        out_specs=[tok(WIDTH), tok(LANES), tok(D_MODEL), tok(D_MODEL), tok(D_MODEL), tok(LANES), tok(D_MODEL)],
        out_shape=[jax.ShapeDtypeStruct((t_all, WIDTH), F32), jax.ShapeDtypeStruct((t_all, LANES), F32),
                   jax.ShapeDtypeStruct((t_all, D_MODEL), BF16), jax.ShapeDtypeStruct((t_all, D_MODEL), BF16),
                   jax.ShapeDtypeStruct((t_all, D_MODEL), F32), jax.ShapeDtypeStruct((t_all, LANES), F32),
                   jax.ShapeDtypeStruct((t_all, D_MODEL), BF16)],
        compiler_params=_params(),
    )(oa, o3[0], o3[1], o3[2], l3[0], l3[1], l3[2], gn_a, gn_b, w_out, x, ada3, ln_g, ln_b)


def _mix_out_bwd(dmix, w_out, oa, ob, gn_a, gn_b):
    t_all = dmix.shape[0]
    tm = 256

    def body(dm_ref, w_ref, oa_ref, ob_ref, ga_ref, gb_ref, doa_ref, dob_ref, dla_ref, dlb_ref, acc_ref):
        @pl.when(pl.program_id(0) == 0)
        def _():
            acc_ref[...] = jnp.zeros_like(acc_ref)
        e, et = _head_mats()
        dmg = _nt(dm_ref[...], w_ref[...])

        def group(o, dn, gain):
            rr = lax.rsqrt(_hsum(o * o, et) * (1.0 / HEAD_DIM) + RMS_EPS)
            re = _hexp(rr, e)
            dgain = jnp.sum(dn * o * re, axis=0, keepdims=True)
            dxn = dn * gain
            tt = _hsum(dxn * o, et) * (rr * rr * rr) * (1.0 / HEAD_DIM)
            do = re * dxn - o * _hexp(tt, e)
            return do, _hsum(do * o, et), dgain

        doa, dla, dga = group(oa_ref[...], dmg[:, :WIDTH], ga_ref[...])
        dob, dlb, dgb = group(ob_ref[...], dmg[:, WIDTH:], gb_ref[...])
        doa_ref[...] = doa.astype(BF16)
        dob_ref[...] = dob.astype(BF16)
        dla_ref[...] = dla
        dlb_ref[...] = dlb
        acc_ref[0:1, :] += jnp.concatenate([dga, dgb], axis=1)

    tok = lambda w: pl.BlockSpec((tm, w), lambda i: (i, 0))
    vec = lambda w: pl.BlockSpec((1, w), lambda i: (0, 0))
    return pl.pallas_call(
        body, name="mix_out_bwd", grid=(t_all // tm,),
        in_specs=[tok(D_MODEL), pl.BlockSpec(w_out.shape, lambda i: (0, 0)), tok(WIDTH), tok(WIDTH), vec(WIDTH),
                  vec(WIDTH)],
        out_specs=[tok(WIDTH), tok(WIDTH), tok(LANES), tok(LANES), pl.BlockSpec((8, D_MODEL), lambda i: (0, 0))],
        out_shape=[jax.ShapeDtypeStruct((t_all, WIDTH), BF16), jax.ShapeDtypeStruct((t_all, WIDTH), BF16),
                   jax.ShapeDtypeStruct((t_all, LANES), F32), jax.ShapeDtypeStruct((t_all, LANES), F32),
                   jax.ShapeDtypeStruct((8, D_MODEL), F32)],
        compiler_params=_params(),
    )(dmix, w_out, oa, ob, gn_a, gn_b)


def _inproj_bwd(dza, dqb, dkb, dvb, dfa16, pos, wqkv, wf16, freq, dr1, x, ada3, seq):
    t_all = x.shape[0]
    tm = 256
    nts = seq // tm
    nbat = t_all // seq

    def body(dza_ref, dqb_ref, dkb_ref, dvb_ref, dfa_ref, pos_ref, w_ref, wf_ref, fr_ref, dr1_ref, x_ref, ada_ref,
             gx_ref, dz_ref, acc_ref):
        i = pl.program_id(0)

        @pl.when(i == 0)
        def _():
            acc_ref[...] = jnp.zeros_like(acc_ref)
        tabs = _rope_tabs(pos_ref, fr_ref, -1.0)
        dz_ref[:, :3 * WIDTH] = dza_ref[...]
        dz_ref[:, 3 * WIDTH:4 * WIDTH] = _rope(dqb_ref[...], tabs).astype(BF16)
        dz_ref[:, 4 * WIDTH:5 * WIDTH] = _rope(dkb_ref[...], tabs).astype(BF16)
        dz_ref[:, 5 * WIDTH:] = dvb_ref[...].astype(BF16)
        dh1 = _tn(dfa_ref[...], wf_ref[...])
        for n in range(6):
            cs = slice(n * WIDTH, (n + 1) * WIDTH)
            dh1 = dh1 + _nt(dz_ref[:, cs], w_ref[:, cs])
        xv = x_ref[...]
        gx_ref[...] = ALPHA * dr1_ref[...] + dh1 * (1.0 + ada_ref[0, 1:2, :])
        b = i // nts
        acc_ref[pl.ds(b, 1), :] += jnp.sum(dh1 * xv, axis=0, keepdims=True)
        acc_ref[pl.ds(8 + b, 1), :] += jnp.sum(dh1, axis=0, keepdims=True)

    tok = lambda w: pl.BlockSpec((tm, w), lambda i: (i, 0))
    return pl.pallas_call(
        body, name="inproj_bwd", grid=(t_all // tm,),
        in_specs=[tok(3 * WIDTH), tok(WIDTH), tok(WIDTH), tok(WIDTH), pl.BlockSpec((16, tm), lambda i: (0, i)),
                  tok(1), pl.BlockSpec(wqkv.shape, lambda i: (0, 0)), pl.BlockSpec(wf16.shape, lambda i: (0, 0)),
                  pl.BlockSpec((1, LANES), lambda i: (0, 0)), tok(D_MODEL), tok(D_MODEL),
                  pl.BlockSpec((1, 6, D_MODEL), lambda i: (i // nts, 0, 0))],
        out_specs=[tok(D_MODEL), tok(6 * WIDTH), pl.BlockSpec((16, D_MODEL), lambda i: (0, 0))],
        out_shape=[jax.ShapeDtypeStruct((t_all, D_MODEL), F32), jax.ShapeDtypeStruct((t_all, 6 * WIDTH), BF16),
                   jax.ShapeDtypeStruct((16, D_MODEL), F32)],
        compiler_params=_params(),
    )(dza, dqb, dkb, dvb, dfa16, pos, wqkv, wf16, freq, dr1, x, ada3)


FFN_TM = 512
FFN_TN = 256
HALO = 8


FFN_CHUNK = 64


def _conv(cat_ref, w_ref, b_ref, start, rows):
    return (b_ref[...] + w_ref[0:1, :] * cat_ref[pl.ds(start + HALO - 2, rows), :]
            + w_ref[1:2, :] * cat_ref[pl.ds(start + HALO - 1, rows), :]
            + w_ref[2:3, :] * cat_ref[pl.ds(start + HALO, rows), :])


def _ffn_gate(u, conv_w, conv_b, seq):
    t_all = u.shape[0]
    tm, tn = FFN_TM, FFN_TN
    nc = D_FF // tn
    nts = seq // tm

    def body(ua_ref, uap_ref, ug_ref, ugp_ref, wa_ref, wg_ref, ba_ref, bg_ref, o_ref, ca_ref, cg_ref):
        first = (pl.program_id(0) % nts) == 0
        zero = jnp.zeros((HALO, tn), F32)
        ca_ref[0:HALO, :] = jnp.where(first, zero, uap_ref[...])
        cg_ref[0:HALO, :] = jnp.where(first, zero, ugp_ref[...])
        ca_ref[HALO:, :] = ua_ref[...]
        cg_ref[HALO:, :] = ug_ref[...]
        for c0 in range(0, tm, FFN_CHUNK):
            ya = _conv(ca_ref, wa_ref, ba_ref, c0, FFN_CHUNK)
            yg = _conv(cg_ref, wg_ref, bg_ref, c0, FFN_CHUNK)
            o_ref[c0:c0 + FFN_CHUNK, :] = (yg * jax.nn.sigmoid(yg) * ya).astype(BF16)

    cur = lambda off: pl.BlockSpec((tm, tn), lambda t, n: (t, n + off))
    prev = lambda off: pl.BlockSpec((HALO, tn), lambda t, n: (jnp.maximum(t * (tm // HALO) - 1, 0), n + off))
    vec = lambda r, off: pl.BlockSpec((r, tn), lambda t, n: (0, n + off))
    return pl.pallas_call(
        body, name="ffn_gate", grid=(t_all // tm, nc),
        in_specs=[cur(0), prev(0), cur(nc), prev(nc), vec(3, 0), vec(3, nc), vec(1, 0), vec(1, nc)],
        out_specs=pl.BlockSpec((tm, tn), lambda t, n: (t, n)),
        out_shape=jax.ShapeDtypeStruct((t_all, D_FF), BF16),
        scratch_shapes=[pltpu.VMEM((tm + HALO, tn), F32)] * 2, compiler_params=_params(),
    )(u, u, u, u, conv_w, conv_w, conv_b, conv_b)


def _ffn_gate_bwd(u, dfi, conv_w, conv_b, seq):
    t_all = u.shape[0]
    tm, tn = FFN_TM, FFN_TN
    nc = D_FF // tn
    nts = seq // tm

    def body(ua_ref, uap_ref, uan_ref, ug_ref, ugp_ref, ugn_ref, df_ref, dfn_ref, wa_ref, wg_ref, ba_ref, bg_ref,
             dua_ref, dug_ref, acca_ref, accg_ref, ca_ref, cg_ref, ya_ref, yg_ref):
        t = pl.program_id(1)
        first = (t % nts) == 0
        last = (t % nts) == nts - 1

        @pl.when(t == 0)
        def _():
            acca_ref[...] = jnp.zeros_like(acca_ref)
            accg_ref[...] = jnp.zeros_like(accg_ref)
        zero = jnp.zeros((HALO, tn), F32)
        for cat, cur, prv, nxt in ((ca_ref, ua_ref, uap_ref, uan_ref), (cg_ref, ug_ref, ugp_ref, ugn_ref)):
            cat[0:HALO, :] = jnp.where(first, zero, prv[...])
            cat[HALO:HALO + tm, :] = cur[...]
            cat[HALO + tm:, :] = nxt[...]
        ch = FFN_CHUNK
        sums = [[jnp.zeros((1, tn), F32) for _ in range(4)] for _ in range(2)]
        for ci, c0 in enumerate(range(0, tm, ch)):
            ya = _conv(ca_ref, wa_ref, ba_ref, c0, ch + HALO)
            yg = _conv(cg_ref, wg_ref, bg_ref, c0, ch + HALO)
            if c0 + ch < tm:
                beyond = df_ref[c0 + ch:c0 + ch + 16, :].astype(F32)[:HALO]
            else:
                beyond = jnp.where(last, 0.0, dfn_ref[...].astype(F32)[:HALO])
            dfe = jnp.concatenate([df_ref[c0:c0 + ch, :].astype(F32), beyond], axis=0)
            sg = jax.nn.sigmoid(yg)
            ya_ref[ci] = dfe * (yg * sg)
            yg_ref[ci] = dfe * ya * (sg * (1.0 + yg * (1.0 - sg)))
            for half, (dy, cat, w_ref, du_ref) in enumerate(((ya_ref, ca_ref, wa_ref, dua_ref),
                                                             (yg_ref, cg_ref, wg_ref, dug_ref))):
                d0 = dy[ci, 0:ch, :]
                du = (w_ref[2:3, :] * d0 + w_ref[1:2, :] * dy[ci, pl.ds(1, ch), :]
                      + w_ref[0:1, :] * dy[ci, pl.ds(2, ch), :])
                du_ref[c0:c0 + ch, :] = du.astype(BF16)
                for k in range(3):
                    sums[half][k] += jnp.sum(d0 * cat[pl.ds(c0 + HALO - 2 + k, ch), :], axis=0, keepdims=True)
                sums[half][3] += jnp.sum(d0, axis=0, keepdims=True)
        for half, acc in enumerate((acca_ref, accg_ref)):
            for k in range(4):
                acc[k:k + 1, :] += sums[half][k]

    nrow = t_all // HALO
    cur = lambda off: pl.BlockSpec((tm, tn), lambda n, t: (t, n + off))
    prev = lambda off: pl.BlockSpec((HALO, tn), lambda n, t: (jnp.maximum(t * (tm // HALO) - 1, 0), n + off))
    nxt = lambda off: pl.BlockSpec((HALO, tn), lambda n, t: (jnp.minimum((t + 1) * (tm // HALO), nrow - 1), n + off))
    vec = lambda r, off: pl.BlockSpec((r, tn), lambda n, t: (0, n + off))
    dcur = pl.BlockSpec((tm, tn), lambda n, t: (t, n))
    dnxt = pl.BlockSpec((16, tn), lambda n, t: (jnp.minimum((t + 1) * (tm // 16), t_all // 16 - 1), n))
    acc = pl.BlockSpec((8, tn), lambda n, t: (0, n))
    return pl.pallas_call(
        body, name="ffn_gate_bwd", grid=(nc, t_all // tm),
        in_specs=[cur(0), prev(0), nxt(0), cur(nc), prev(nc), nxt(nc), dcur, dnxt, vec(3, 0), vec(3, nc), vec(1, 0),
                  vec(1, nc)],
        out_specs=[dcur, dcur, acc, acc],
        out_shape=[jax.ShapeDtypeStruct((t_all, D_FF), BF16), jax.ShapeDtypeStruct((t_all, D_FF), BF16),
                   jax.ShapeDtypeStruct((8, D_FF), F32), jax.ShapeDtypeStruct((8, D_FF), F32)],
        scratch_shapes=[pltpu.VMEM((tm + 2 * HALO, tn), F32)] * 2
        + [pltpu.VMEM((tm // FFN_CHUNK, FFN_CHUNK + HALO, tn), F32)] * 2,
        compiler_params=_params(),
    )(u, u, u, u, u, u, dfi, dfi, conv_w, conv_w, conv_b, conv_b)


def _ffn_down(ffn_in, w_down, xh1, ln1_g, ln1_b, ada3, ln2_g, ln2_b, target, seq):
    t_all = xh1.shape[0]
    tm = 256
    nts = seq // tm

    def body(f_ref, w_ref, xh_ref, g1_ref, b1_ref, ada_ref, g2_ref, b2_ref, tg_ref, dr2_ref, acc_ref):
        i = pl.program_id(0)

        @pl.when(i == 0)
        def _():
            acc_ref[...] = jnp.zeros_like(acc_ref)
        ffn = _nn(f_ref[...], w_ref[...])
        x1 = xh_ref[...] * g1_ref[...] + b1_ref[...]
        r2 = ALPHA * x1 + ada_ref[0, 5:6, :] * ffn
        d = r2 - jnp.mean(r2, axis=1, keepdims=True)
        rstd = lax.rsqrt(jnp.mean(d * d, axis=1, keepdims=True) + LN_EPS)
        xh2 = d * rstd
        diff = xh2 * g2_ref[...] + b2_ref[...] - tg_ref[...]
        dy = diff * (1.0 / D_MODEL)
        dr2 = _layer_norm_bwd(dy * g2_ref[...], xh2, rstd)
        dr2_ref[...] = dr2
        acc_ref[0:1, :] += jnp.sum(dy * xh2, axis=0, keepdims=True)
        acc_ref[1:2, :] += jnp.sum(dy, axis=0, keepdims=True)
        acc_ref[2:3, :] += jnp.sum(diff * diff, axis=0, keepdims=True) * (0.5 / D_MODEL)
        acc_ref[pl.ds(8 + i // nts, 1), :] += jnp.sum(dr2 * ffn, axis=0, keepdims=True)

    tok = lambda w: pl.BlockSpec((tm, w), lambda i: (i, 0))
    vec = pl.BlockSpec((1, D_MODEL), lambda i: (0, 0))
    return pl.pallas_call(
        body, name="ffn_down", grid=(t_all // tm,),
        in_specs=[tok(D_FF), pl.BlockSpec(w_down.shape, lambda i: (0, 0)), tok(D_MODEL), vec, vec,
                  pl.BlockSpec((1, 6, D_MODEL), lambda i: (i // nts, 0, 0)), vec, vec, tok(D_MODEL)],
        out_specs=[tok(D_MODEL), pl.BlockSpec((16, D_MODEL), lambda i: (0, 0))],
        out_shape=[jax.ShapeDtypeStruct((t_all, D_MODEL), F32), jax.ShapeDtypeStruct((16, D_MODEL), F32)],
        compiler_params=_params(),
    )(ffn_in, w_down, xh1, ln1_g, ln1_b, ada3, ln2_g, ln2_b, target)


def _ffn_down_bwd(dr2, ada3, w_down, seq):
    t_all = dr2.shape[0]
    tm = 256
    nts = seq // tm

    def body(d_ref, ada_ref, w_ref, dffn_ref, dfi_ref):
        dffn = (d_ref[...] * ada_ref[0, 5:6, :]).astype(BF16)
        dffn_ref[...] = dffn
        dfi_ref[...] = _nt(dffn, w_ref[...]).astype(BF16)

    tok = lambda w: pl.BlockSpec((tm, w), lambda i: (i, 0))
    return pl.pallas_call(
        body, name="ffn_down_bwd", grid=(t_all // tm,),
        in_specs=[tok(D_MODEL), pl.BlockSpec((1, 6, D_MODEL), lambda i: (i // nts, 0, 0)),
                  pl.BlockSpec(w_down.shape, lambda i: (0, 0))],
        out_specs=[tok(D_MODEL), tok(D_FF)],
        out_shape=[jax.ShapeDtypeStruct((t_all, D_MODEL), BF16), jax.ShapeDtypeStruct((t_all, D_FF), BF16)],
        compiler_params=_params(),
    )(dr2, ada3, w_down)


def _ffn_up_bwd(du_a, du_g, w_up, dr2, xh1, rs1, mix, ada3, ln1_g, ln1_b, seq):
    t_all = dr2.shape[0]
    tm = 256
    nts = seq // tm

    def body(da_ref, dg_ref, w_ref, dr2_ref, xh_ref, rs_ref, mix_ref, ada_ref, g_ref, b_ref, dr1_ref, dmix_ref,
             acc_ref):
        i = pl.program_id(0)

        @pl.when(i == 0)
        def _():
            acc_ref[...] = jnp.zeros_like(acc_ref)
        dh2 = _nt(da_ref[...], w_ref[:, :D_FF]) + _nt(dg_ref[...], w_ref[:, D_FF:])
        xh = xh_ref[...]
        x1 = xh * g_ref[...] + b_ref[...]
        dx1 = ALPHA * dr2_ref[...] + dh2 * (1.0 + ada_ref[0, 4:5, :])
        dr1 = _layer_norm_bwd(dx1 * g_ref[...], xh, rs_ref[:, 0:1])
        dr1_ref[...] = dr1
        dmix_ref[...] = (dr1 * ada_ref[0, 2:3, :]).astype(BF16)
        b = i // nts
        acc_ref[0:1, :] += jnp.sum(dx1 * xh, axis=0, keepdims=True)
        acc_ref[1:2, :] += jnp.sum(dx1, axis=0, keepdims=True)
        acc_ref[pl.ds(8 + b, 1), :] += jnp.sum(dh2 * x1, axis=0, keepdims=True)
        acc_ref[pl.ds(16 + b, 1), :] += jnp.sum(dh2, axis=0, keepdims=True)
        acc_ref[pl.ds(24 + b, 1), :] += jnp.sum(dr1 * mix_ref[...].astype(F32), axis=0, keepdims=True)

    tok = lambda w: pl.BlockSpec((tm, w), lambda i: (i, 0))
    vec = pl.BlockSpec((1, D_MODEL), lambda i: (0, 0))
    return pl.pallas_call(
        body, name="ffn_up_bwd", grid=(t_all // tm,),
        in_specs=[tok(D_FF), tok(D_FF), pl.BlockSpec(w_up.shape, lambda i: (0, 0)), tok(D_MODEL), tok(D_MODEL),
                  tok(LANES), tok(D_MODEL), pl.BlockSpec((1, 6, D_MODEL), lambda i: (i // nts, 0, 0)), vec, vec],
        out_specs=[tok(D_MODEL), tok(D_MODEL), pl.BlockSpec((32, D_MODEL), lambda i: (0, 0))],
        out_shape=[jax.ShapeDtypeStruct((t_all, D_MODEL), F32), jax.ShapeDtypeStruct((t_all, D_MODEL), BF16),
                   jax.ShapeDtypeStruct((32, D_MODEL), F32)],
        compiler_params=_params(),
    )(du_a, du_g, w_up, dr2, xh1, rs1, mix, ada3, ln1_g, ln1_b)


def _perm(a, d, seq):
    if d == 1:
        return a
    t_all, w = a.shape
    return a.reshape(t_all // seq, seq // d, d, w).transpose(0, 2, 1, 3).reshape(t_all, w)


def _unperm(a, d, seq):
    if d == 1:
        return a
    t_all, w = a.shape
    return a.reshape(t_all // seq, d, seq // d, w).transpose(0, 2, 1, 3).reshape(t_all, w)


DILATIONS = (1, 4, 16)


def _stack_perm(a, seq):
    return jnp.stack([_perm(a, d, seq) for d in DILATIONS])


def _rows(a):
    return a[:, :N_HEADS].T


def _rope_freq():
    f = np.float32(ROPE_THETA) ** (-np.arange(0, ROPE_DIMS, 2, dtype=np.float32) / np.float32(ROPE_DIMS))
    return jnp.asarray(np.tile(f.astype(np.float32), LANES // (ROPE_DIMS // 2))[None, :])


def _local_step(x, positions, target, ada3, w_in, b_fgate, gn_a, gn_b, ln1_g, ln1_b, conv_b, ln2_g, ln2_b,
                late_shards):
    nbat, seq, _ = x.shape
    t_all = nbat * seq
    xf = x.reshape(t_all, D_MODEL)
    tg = target.reshape(t_all, D_MODEL)
    pos = positions.reshape(t_all, 1)
    freq = _rope_freq()

    wqkv = jnp.concatenate([w_in[:, :3 * WIDTH], w_in[:, 3 * WIDTH + N_HEADS:]], axis=1)
    wf16 = jnp.zeros((16, D_MODEL), BF16).at[:N_HEADS].set(w_in[:, 3 * WIDTH:3 * WIDTH + N_HEADS].T)
    bf = b_fgate.reshape(N_HEADS, 1)

    h1, za, zb, fa_t = _inproj(xf, ada3, pos, wqkv, wf16, freq, seq)
    f_row = _fgate_fwd(fa_t, bf, seq)
    f_col = jnp.zeros((t_all, LANES), F32).at[:, :N_HEADS].set(f_row.T)
    vt = za[:, 2 * WIDTH:].reshape(nbat, seq, WIDTH).transpose(0, 2, 1).reshape(nbat * WIDTH, seq)
    oa, lse_row_a, gathered = _fox_fwd(za, vt, f_col, seq, [late_shards[n] for n in LATE])
    w_out, w_up, conv_w, w_down = (_full_from_gathered(n, g) for n, g in zip(LATE, gathered))
    qs = _stack_perm(zb[:, :WIDTH], seq)
    ks = _stack_perm(zb[:, WIDTH:2 * WIDTH], seq)
    vs = _stack_perm(zb[:, 2 * WIDTH:], seq)
    o3p, l3p = _dil_fwd(qs, ks, vs, seq)
    o3 = [_unperm(o3p[p], d, seq) for p, d in enumerate(DILATIONS)]
    l3 = [_unperm(l3p[p], d, seq) for p, d in enumerate(DILATIONS)]
    ob, lse_b, merged, mix, xh1, rs1, h2 = _mix_out(oa, o3, l3, gn_a, gn_b, w_out, xf, ada3, ln1_g, ln1_b, seq)
    u = _matmul(h2, w_up, False, F32, 256, 512, "ffn_up")
    ffn_in = _ffn_gate(u, conv_w, conv_b, seq)
    dr2, acc2 = _ffn_down(ffn_in, w_down, xh1, ln1_g, ln1_b, ada3, ln2_g, ln2_b, tg, seq)

    dffn, dfi = _ffn_down_bwd(dr2, ada3, w_down, seq)
    d_w_down = _matmul_tn(ffn_in, dffn, 512, 512, "dw_down")
    du_a, du_g, acc_ca, acc_cg = _ffn_gate_bwd(u, dfi, conv_w, conv_b, seq)
    dr1, dmix, acc1 = _ffn_up_bwd(du_a, du_g, w_up, dr2, xh1, rs1, mix, ada3, ln1_g, ln1_b, seq)
    d_w_up = jnp.concatenate([_matmul_tn(h2, du_a, 256, 512, "dw_up_a"), _matmul_tn(h2, du_g, 256, 512, "dw_up_g")],
                             axis=1)

    doa, dob, dl_a, dl_b, acc_gn = _mix_out_bwd(dmix, w_out, oa, ob, gn_a, gn_b)
    d_w_out = _matmul_tn(merged, dmix, 512, 512, "dw_out")
    late_grads = dict(w_out=d_w_out, w_up=d_w_up, conv_w=jnp.concatenate([acc_ca[0:3], acc_cg[0:3]], axis=1),
                      w_down=d_w_down)
    dka, dva, df_k, dqt, df_q, late_parts = _fox_bwd(za, doa, f_col, lse_row_a, _rows(dl_a), seq,
                                                     [_payload(n, _dest_major(n, late_grads[n])) for n in LATE])
    dqa = dqt.reshape(nbat, WIDTH, seq).transpose(0, 2, 1).reshape(t_all, WIDTH).astype(BF16)
    dfa_t, dbf = _fgate_bwd(_rows(df_k) + df_q, fa_t, bf, seq)
    dos = _stack_perm(dob, seq)
    lses = _stack_perm(lse_b, seq)
    dls = _stack_perm(dl_b, seq)
    dq3 = _dil_bwd_dq(qs, ks, vs, dos, lses, dls, seq)
    lse_rows = lses[:, :, :N_HEADS].transpose(0, 2, 1)
    dl_rows = dls[:, :, :N_HEADS].transpose(0, 2, 1)
    dk3, dv3 = _dil_bwd_dkv(qs, ks, vs, dos, lse_rows, dl_rows, seq)
    unsum = lambda a3: sum(_unperm(a3[p], d, seq) for p, d in enumerate(DILATIONS))
    dza = jnp.concatenate([dqa, dka, dva], axis=1)
    dfa16 = jnp.zeros((16, t_all), BF16).at[:N_HEADS].set(dfa_t.astype(BF16))
    grad_x, dz, acc0 = _inproj_bwd(dza, unsum(dq3), unsum(dk3), unsum(dv3), dfa16, pos, wqkv, wf16, freq, dr1, xf,
                                   ada3, seq)
    d_wqkv = _matmul_tn(h1, dz, 512, 512, "dw_in")
    d_wf = _matmul_rows(dfa16, h1, 512, "dw_fgate")[:N_HEADS].T
    d_w_in = jnp.concatenate([d_wqkv[:, :3 * WIDTH], d_wf, d_wqkv[:, 3 * WIDTH:]], axis=1)

    dada = jnp.concatenate([acc0[8:8 + nbat], acc0[:nbat], acc1[24:24 + nbat], acc1[16:16 + nbat], acc1[8:8 + nbat],
                            acc2[8:8 + nbat]], axis=1)

    grads = dict(
        dada=dada, b_ada=jnp.sum(dada, axis=0, keepdims=True), w_in=d_w_in, b_fgate=dbf[:, 0][None, :],
        gn_a=acc_gn[0:1, :WIDTH], gn_b=acc_gn[0:1, WIDTH:], ln1_g=acc1[0:1], ln1_b=acc1[1:2],
        conv_b=jnp.concatenate([acc_ca[3:4], acc_cg[3:4]], axis=1), ln2_g=acc2[0:1], ln2_b=acc2[1:2])
    return acc2[2:3], grad_x.reshape(x.shape), grads, dict(zip(LATE, late_parts))


LATE = ("w_out", "w_up", "conv_w", "w_down")
BIG = ("w_ada", "w_in") + LATE
COLUMN_SHARDED = ("w_ada", "w_in", "w_up", "conv_w")


def _payload(name, a):
    return a if name == "conv_w" else a.astype(BF16)
SMALL = ("b_ada", "b_fgate", "gn_a", "gn_b", "ln1_g", "ln1_b", "conv_b", "ln2_g", "ln2_b")
ADAM_ROWS = dict(w_ada=256, w_in=256, w_out=128, w_up=256, conv_w=3, w_down=176)
SMALL_ROWS = 24


def _full_from_gathered(name, g):
    if name in COLUMN_SHARDED:
        return g.transpose(1, 0, 2).reshape(g.shape[1], N_DEV * g.shape[2])
    return g.reshape(N_DEV * g.shape[1], g.shape[2])


def _dest_major(name, full):
    if name in COLUMN_SHARDED:
        r, cfull = full.shape
        return full.reshape(r, N_DEV, cfull // N_DEV).transpose(1, 0, 2)
    return full.reshape(N_DEV, full.shape[0] // N_DEV, full.shape[1])


def _pack_small(vals, extra=None):
    parts = [vals[n].reshape(-1) for n in SMALL]
    if extra is not None:
        parts.append(extra.reshape(-1))
    flat = jnp.concatenate(parts)
    return jnp.pad(flat, (0, SMALL_ROWS * D_MODEL - flat.shape[0])).reshape(SMALL_ROWS, D_MODEL)


def _unpack_small(packed, like):
    flat = packed.reshape(-1)
    out, off = {}, 0
    for n in SMALL:
        size = like[n].size
        out[n] = flat[off:off + size].reshape(like[n].shape)
        off += size
    return out, flat[off:off + D_MODEL]


def kernel(x, c, positions, w_ada, b_ada, w_in, b_fgate, gn_a, gn_b, w_out, ln1_g, ln1_b, w_up, conv_w, conv_b, w_down, ln2_g, ln2_b, loss_target, m_w_ada, m_b_ada, m_w_in, m_b_fgate, m_gn_a, m_gn_b, m_w_out, m_ln1_g, m_ln1_b, m_w_up, m_conv_w, m_conv_b, m_w_down, m_ln2_g, m_ln2_b, v_w_ada, v_b_ada, v_w_in, v_b_fgate, v_gn_a, v_gn_b, v_w_out, v_ln1_g, v_ln1_b, v_w_up, v_conv_w, v_conv_b, v_w_down, v_ln2_g, v_ln2_b):
    w = dict(w_ada=w_ada[0], b_ada=b_ada, w_in=w_in[0], b_fgate=b_fgate, gn_a=gn_a, gn_b=gn_b, w_out=w_out[0],
             ln1_g=ln1_g, ln1_b=ln1_b, w_up=w_up[0], conv_w=conv_w[0], conv_b=conv_b, w_down=w_down[0], ln2_g=ln2_g,
             ln2_b=ln2_b)
    m = dict(w_ada=m_w_ada[0], b_ada=m_b_ada, w_in=m_w_in[0], b_fgate=m_b_fgate, gn_a=m_gn_a, gn_b=m_gn_b,
             w_out=m_w_out[0], ln1_g=m_ln1_g, ln1_b=m_ln1_b, w_up=m_w_up[0], conv_w=m_conv_w[0], conv_b=m_conv_b,
             w_down=m_w_down[0], ln2_g=m_ln2_g, ln2_b=m_ln2_b)
    v = dict(w_ada=v_w_ada[0], b_ada=v_b_ada, w_in=v_w_in[0], b_fgate=v_b_fgate, gn_a=v_gn_a, gn_b=v_gn_b,
             w_out=v_w_out[0], ln1_g=v_ln1_g, ln1_b=v_ln1_b, w_up=v_w_up[0], conv_w=v_conv_w[0], conv_b=v_conv_b,
             w_down=v_w_down[0], ln2_g=v_ln2_g, ln2_b=v_ln2_b)

    nbat = x.shape[0]
    me = 4 * lax.axis_index("x") + 2 * lax.axis_index("y") + lax.axis_index("c")
    ada_cols = w["w_ada"].shape[1]

    c_all, w_in_all = _exchange([c, _payload("w_in", w["w_in"])], [True, True], "weight_gather")
    c_all = c_all.reshape(N_DEV * nbat, D_MODEL)
    ada_mine = _ada_fwd(c_all, w["w_ada"], lax.dynamic_slice(b_ada, (0, me * ada_cols), (1, ada_cols)))
    (ada_parts,) = _exchange([ada_mine.reshape(N_DEV, nbat, ada_cols)], [False], "ada_exchange")
    ada3 = ada_parts.transpose(1, 0, 2).reshape(nbat, 6, D_MODEL)

    loss_lanes, grad_x, g_local, parts = _local_step(
        x, positions, loss_target, ada3, _full_from_gathered("w_in", w_in_all), b_fgate, gn_a, gn_b, ln1_g, ln1_b,
        conv_b, ln2_g, ln2_b, {n: _payload(n, w[n]) for n in LATE})

    parts["w_in"], dada_all, small_all = _exchange(
        [_payload("w_in", _dest_major("w_in", g_local["w_in"])), g_local["dada"], _pack_small(g_local, loss_lanes)],
        [False, True, True], "grad_exchange")
    dada_cols = lax.dynamic_slice(dada_all.reshape(N_DEV * nbat, 6 * D_MODEL), (0, me * ada_cols),
                                  (N_DEV * nbat, ada_cols))
    parts["w_ada"] = _ada_bwd(c_all, dada_cols)[None]

    grad, delta, new_m, new_v = {}, {}, {}, {}
    for n in BIG:
        grad[n], delta[n], new_m[n], new_v[n] = (
            a[None] for a in _adamw(parts[n], w[n], m[n], v[n], ADAM_ROWS[n], "adamw_" + n))
    packed = _adamw(small_all, _pack_small(w), _pack_small(m), _pack_small(v), SMALL_ROWS, "adamw_small")
    for dst, pk in zip((grad, delta, new_m, new_v), packed):
        vals, lanes = _unpack_small(pk, w)
        dst.update(vals)
        if dst is grad:
            loss = jnp.sum(lanes)

    order = ("w_ada", "b_ada", "w_in", "b_fgate", "gn_a", "gn_b", "w_out", "ln1_g", "ln1_b", "w_up", "conv_w", "conv_b",
             "w_down", "ln2_g", "ln2_b")
    return (loss, grad_x, *[grad[n] for n in order], *[delta[n] for n in order], *[new_m[n] for n in order],
            *[new_v[n] for n in order])
```

```python
import functools

import numpy as np
import jax
import jax.numpy as jnp
from jax import lax
from jax.experimental import pallas as pl
from jax.experimental.pallas import tpu as pltpu

F32, BF16 = jnp.float32, jnp.bfloat16
HIGHEST = lax.Precision.HIGHEST
MESH = pl.DeviceIdType.MESH
ANY = pl.BlockSpec(memory_space=pl.ANY)

D_MODEL = 1024
N_HEADS = 8
HEAD_DIM = 64
WIDTH = 512
D_FF = 2816
N_DEV = 8
ROPE_DIMS = 16
ROPE_THETA = 500000.0
ALPHA = 2.0 ** 0.25
LN_EPS = 1e-5
RMS_EPS = 1e-6
NEG = -1e30
Q_SCALE = 0.125
BLK = 128
LANES = 128
VMEM_LIMIT_BYTES = 56 * 1024 * 1024

ADAM_LR, ADAM_B1, ADAM_B2, ADAM_EPS, ADAM_WD, ADAM_STEP = 0.001, 0.9, 0.999, 1e-08, 0.01, 10


def _params(vmem=VMEM_LIMIT_BYTES):
    return pltpu.CompilerParams(vmem_limit_bytes=vmem)


def _nn(a, b):
    return jnp.dot(a, b, preferred_element_type=F32)


def _nt(a, b):
    return lax.dot_general(a, b, (((1,), (1,)), ((), ())), preferred_element_type=F32)


def _tn(a, b):
    return lax.dot_general(a, b, (((0,), (0,)), ((), ())), preferred_element_type=F32)


def _head_mats():
    r = lax.broadcasted_iota(jnp.int32, (LANES, WIDTH), 0)
    c = lax.broadcasted_iota(jnp.int32, (LANES, WIDTH), 1)
    e = ((c >> 6) == r).astype(F32)
    r2 = lax.broadcasted_iota(jnp.int32, (WIDTH, LANES), 0)
    c2 = lax.broadcasted_iota(jnp.int32, (WIDTH, LANES), 1)
    et = ((r2 >> 6) == c2).astype(F32)
    return e, et


def _hexp(w, e):
    return jnp.dot(w, e, precision=HIGHEST, preferred_element_type=F32)


def _hsum(x, et):
    return jnp.dot(x, et, precision=HIGHEST, preferred_element_type=F32)


def _rope_tabs(pos_ref, fr_ref, sign):
    ang = pos_ref[...].astype(F32) * fr_ref[...]
    lane = lax.broadcasted_iota(jnp.int32, ang.shape, 1) & (HEAD_DIM - 1)
    m1 = lane < ROPE_DIMS // 2
    m2 = (lane >= ROPE_DIMS // 2) & (lane < ROPE_DIMS)
    cos = jnp.cos(ang)
    sin = jnp.sin(ang) * sign
    return (jnp.where(m1 | m2, cos, 1.0), jnp.where(m1, -sin, 0.0), jnp.where(m2, sin, 0.0))


def _rope(z, tabs):
    c, s1, s2 = tabs
    parts = []
    for p in range(z.shape[1] // LANES):
        zp = z[:, LANES * p:LANES * (p + 1)]
        parts.append(zp * c + pltpu.roll(zp, LANES - 8, 1) * s1 + pltpu.roll(zp, 8, 1) * s2)
    return jnp.concatenate(parts, axis=1)


def _half_masks(rows):
    lane = lax.broadcasted_iota(jnp.int32, (rows, LANES), 1)
    lo = lane < HEAD_DIM
    return lo, jnp.logical_not(lo)


def _layer_norm_bwd(dxh, xh, rstd):
    m1 = jnp.mean(dxh, axis=1, keepdims=True)
    m2 = jnp.mean(dxh * xh, axis=1, keepdims=True)
    return rstd * (dxh - m1 - xh * m2)


def _coords():
    return lax.axis_index("x"), lax.axis_index("y"), lax.axis_index("c")


def _peer(x, y, c, k):
    return (1 - x if k & 4 else x, 1 - y if k & 2 else y, 1 - c if k & 1 else c)


def _comm_sems(n):
    return [pltpu.SemaphoreType.DMA((N_DEV - 1, n)), pltpu.SemaphoreType.DMA((N_DEV - 1, n)),
            pltpu.SemaphoreType.DMA((n,))]


def _comm_copies(ins, outs, to_all, sems):
    send_sems, recv_sems, local_sems = sems
    x, y, c = _coords()
    me = 4 * x + 2 * y + c
    copies = [pltpu.make_async_copy(ins[t] if to_all[t] else ins[t].at[me], outs[t].at[me], local_sems.at[t])
              for t in range(len(ins))]
    for k in range(1, N_DEV):
        px, py, pc = _peer(x, y, c, k)
        dest = 4 * px + 2 * py + pc
        for t in range(len(ins)):
            copies.append(pltpu.make_async_remote_copy(
                src_ref=ins[t] if to_all[t] else ins[t].at[dest], dst_ref=outs[t].at[me],
                send_sem=send_sems.at[k - 1, t], recv_sem=recv_sems.at[k - 1, t],
                device_id=(px, py, pc), device_id_type=MESH))
    return copies


def _comm_out_shapes(ins, to_all):
    return [jax.ShapeDtypeStruct(((N_DEV,) + a.shape) if ta else a.shape, a.dtype) for a, ta in zip(ins, to_all)]


def _exchange(ins, to_all, name):
    n = len(ins)

    def body(*refs):
        copies = _comm_copies(refs[:n], refs[n:2 * n], to_all, refs[2 * n:])
        for cp in copies:
            cp.start()
        for cp in copies:
            cp.wait()

    return pl.pallas_call(
        body, name=name, out_shape=_comm_out_shapes(ins, to_all), in_specs=[ANY] * n, out_specs=[ANY] * n,
        scratch_shapes=_comm_sems(n),
    )(*ins)


def _adamw(parts, w, m, v, rows, name):
    n_parts, r_all, cols = parts.shape
    c1 = 1.0 - ADAM_B1 ** ADAM_STEP
    c2 = 1.0 - ADAM_B2 ** ADAM_STEP

    def body(p_ref, w_ref, m_ref, v_ref, g_ref, d_ref, mo_ref, vo_ref):
        g = p_ref[0].astype(F32)
        for s in range(1, n_parts):
            g = g + p_ref[s].astype(F32)
        mn = ADAM_B1 * m_ref[...] + (1.0 - ADAM_B1) * g
        vn = ADAM_B2 * v_ref[...] + (1.0 - ADAM_B2) * (g * g)
        m_hat = mn / c1
        v_hat = vn / c2
        g_ref[...] = g
        d_ref[...] = -ADAM_LR * (m_hat / (jnp.sqrt(v_hat) + ADAM_EPS) + ADAM_WD * w_ref[...])
        mo_ref[...] = mn
        vo_ref[...] = vn

    spec = pl.BlockSpec((rows, cols), lambda i: (i, 0))
    return pl.pallas_call(
        body, name=name, grid=(r_all // rows,),
        in_specs=[pl.BlockSpec((n_parts, rows, cols), lambda i: (0, i, 0)), spec, spec, spec],
        out_specs=[spec] * 4, out_shape=[jax.ShapeDtypeStruct((r_all, cols), F32)] * 4,
        compiler_params=_params(),
    )(parts, w, m, v)


def _matmul(a, w, transposed_w, out_dtype, tm, chunk, name):
    t_all, k = a.shape
    n = w.shape[0] if transposed_w else w.shape[1]

    def body(a_ref, w_ref, o_ref):
        av = a_ref[...]
        for j in range(n // chunk):
            cs = slice(j * chunk, (j + 1) * chunk)
            r = _nt(av, w_ref[cs, :]) if transposed_w else _nn(av, w_ref[:, cs])
            o_ref[:, cs] = r.astype(out_dtype)

    return pl.pallas_call(
        body, name=name, grid=(t_all // tm,),
        in_specs=[pl.BlockSpec((tm, k), lambda i: (i, 0)), pl.BlockSpec(w.shape, lambda i: (0, 0))],
        out_specs=pl.BlockSpec((tm, n), lambda i: (i, 0)),
        out_shape=jax.ShapeDtypeStruct((t_all, n), out_dtype), compiler_params=_params(),
    )(a, w)


def _matmul_tn(a, b, tn, tk, name):
    t_all, k1 = a.shape
    n = b.shape[1]

    def body(a_ref, b_ref, o_ref):
        @pl.when(pl.program_id(1) == 0)
        def _():
            o_ref[...] = jnp.zeros_like(o_ref)
        o_ref[...] += _tn(a_ref[...], b_ref[...])

    return pl.pallas_call(
        body, name=name, grid=(n // tn, t_all // tk),
        in_specs=[pl.BlockSpec((tk, k1), lambda j, t: (t, 0)), pl.BlockSpec((tk, tn), lambda j, t: (t, j))],
        out_specs=pl.BlockSpec((k1, tn), lambda j, t: (0, j)),
        out_shape=jax.ShapeDtypeStruct((k1, n), F32), compiler_params=_params(),
    )(a, b)


def _matmul_rows(a, b, tk, name):
    r, t_all = a.shape
    n = b.shape[1]

    def body(a_ref, b_ref, o_ref):
        @pl.when(pl.program_id(0) == 0)
        def _():
            o_ref[...] = jnp.zeros_like(o_ref)
        o_ref[...] += _nn(a_ref[...], b_ref[...])

    return pl.pallas_call(
        body, name=name, grid=(t_all // tk,),
        in_specs=[pl.BlockSpec((r, tk), lambda t: (0, t)), pl.BlockSpec((tk, n), lambda t: (t, 0))],
        out_specs=pl.BlockSpec((r, n), lambda t: (0, 0)),
        out_shape=jax.ShapeDtypeStruct((r, n), F32), compiler_params=_params(),
    )(a, b)


def _ada_fwd(c_all, w_ada, b_ada):
    whole = lambda a: pl.BlockSpec(a.shape, lambda j: (0, 0))

    def body(c_ref, w_ref, b_ref, o_ref):
        cv = c_ref[...]
        s = (cv * jax.nn.sigmoid(cv)).astype(BF16)
        o_ref[...] = _nn(s, w_ref[...].astype(BF16)) + b_ref[...]

    out = jax.ShapeDtypeStruct((c_all.shape[0], w_ada.shape[1]), F32)
    return pl.pallas_call(
        body, name="ada_fwd", grid=(1,), in_specs=[whole(c_all), whole(w_ada), whole(b_ada)], out_specs=whole(out),
        out_shape=out, compiler_params=_params(),
    )(c_all, w_ada, b_ada)


def _ada_bwd(c_all, dada):
    whole = lambda a: pl.BlockSpec(a.shape, lambda j: (0, 0))

    def body(c_ref, d_ref, o_ref):
        cv = c_ref[...]
        s = (cv * jax.nn.sigmoid(cv)).astype(BF16)
        o_ref[...] = _tn(s, d_ref[...].astype(BF16))

    out = jax.ShapeDtypeStruct((D_MODEL, dada.shape[1]), F32)
    return pl.pallas_call(
        body, name="ada_bwd", grid=(1,), in_specs=[whole(c_all), whole(dada)], out_specs=whole(out), out_shape=out,
        compiler_params=_params(),
    )(c_all, dada)


def _inproj(x, ada3, pos, wqkv, wf16, freq, seq):
    t_all = x.shape[0]
    tm = 256
    nts = seq // tm

    def body(x_ref, ada_ref, pos_ref, w_ref, wf_ref, fr_ref, h1_ref, za_ref, zb_ref, fa_ref):
        h1 = (x_ref[...] * (1.0 + ada_ref[0, 1:2, :]) + ada_ref[0, 0:1, :]).astype(BF16)
        h1_ref[...] = h1
        tabs = _rope_tabs(pos_ref, fr_ref, 1.0)
        for n in range(6):
            z = _nn(h1, w_ref[:, n * WIDTH:(n + 1) * WIDTH])
            if n in (3, 4):
                z = _rope(z, tabs)
            if n in (0, 3):
                z = z * Q_SCALE
            dst = za_ref if n < 3 else zb_ref
            dst[:, (n % 3) * WIDTH:(n % 3 + 1) * WIDTH] = z.astype(BF16)
        fa_ref[...] = _nt(wf_ref[...], h1)[:N_HEADS]

    tok = lambda w: pl.BlockSpec((tm, w), lambda i: (i, 0))
    return pl.pallas_call(
        body, name="inproj", grid=(t_all // tm,),
        in_specs=[tok(D_MODEL), pl.BlockSpec((1, 6, D_MODEL), lambda i: (i // nts, 0, 0)), tok(1),
                  pl.BlockSpec(wqkv.shape, lambda i: (0, 0)), pl.BlockSpec(wf16.shape, lambda i: (0, 0)),
                  pl.BlockSpec((1, LANES), lambda i: (0, 0))],
        out_specs=[tok(D_MODEL), tok(3 * WIDTH), tok(3 * WIDTH), pl.BlockSpec((N_HEADS, tm), lambda i: (0, i))],
        out_shape=[jax.ShapeDtypeStruct((t_all, D_MODEL), BF16), jax.ShapeDtypeStruct((t_all, 3 * WIDTH), BF16),
                   jax.ShapeDtypeStruct((t_all, 3 * WIDTH), BF16), jax.ShapeDtypeStruct((N_HEADS, t_all), F32)],
        compiler_params=_params(),
    )(x, ada3, pos, wqkv, wf16, freq)


def _fgate_fwd(fa_t, bf, seq):
    t_all = fa_t.shape[1]

    def body(fa_ref, b_ref, f_ref):
        lane = lax.broadcasted_iota(jnp.int32, (N_HEADS, LANES), 1)

        def chunk(j, carry):
            sl = pl.ds(pl.multiple_of(j * LANES, LANES), LANES)
            xv = fa_ref[:, sl] + b_ref[...]
            lf = jnp.minimum(xv, 0.0) - jnp.log(1.0 + jnp.exp(-jnp.abs(xv)))
            for s in (1, 2, 4, 8, 16, 32, 64):
                lf = lf + jnp.where(lane >= s, pltpu.roll(lf, s, 1), 0.0)
            lf = lf + carry
            f_ref[:, sl] = lf
            return lf[:, LANES - 1:LANES]

        lax.fori_loop(0, seq // LANES, chunk, jnp.zeros((N_HEADS, 1), F32))

    return pl.pallas_call(
        body, name="fgate_fwd", grid=(t_all // seq,),
        in_specs=[pl.BlockSpec((N_HEADS, seq), lambda b: (0, b)), pl.BlockSpec((N_HEADS, 1), lambda b: (0, 0))],
        out_specs=pl.BlockSpec((N_HEADS, seq), lambda b: (0, b)),
        out_shape=jax.ShapeDtypeStruct((N_HEADS, t_all), F32), compiler_params=_params(),
    )(fa_t, bf)


def _fgate_bwd(df_t, fa_t, bf, seq):
    t_all = fa_t.shape[1]

    def body(df_ref, fa_ref, b_ref, o_ref, s_ref):
        lane = lax.broadcasted_iota(jnp.int32, (N_HEADS, LANES), 1)

        @pl.when(pl.program_id(0) == 0)
        def _():
            s_ref[...] = jnp.zeros_like(s_ref)

        def chunk(jj, carry):
            car, tot = carry
            j = seq // LANES - 1 - jj
            sl = pl.ds(pl.multiple_of(j * LANES, LANES), LANES)
            d = df_ref[:, sl]
            for s in (1, 2, 4, 8, 16, 32, 64):
                d = d + jnp.where(lane < LANES - s, pltpu.roll(d, LANES - s, 1), 0.0)
            d = d + car
            dfa = d * jax.nn.sigmoid(-(fa_ref[:, sl] + b_ref[...]))
            o_ref[:, sl] = dfa
            return d[:, 0:1], tot + jnp.sum(dfa, axis=1, keepdims=True)

        z = jnp.zeros((N_HEADS, 1), F32)
        _, tot = lax.fori_loop(0, seq // LANES, chunk, (z, z))
        s_ref[...] += jnp.broadcast_to(tot, (N_HEADS, LANES))

    row = pl.BlockSpec((N_HEADS, seq), lambda b: (0, b))
    return pl.pallas_call(
        body, name="fgate_bwd", grid=(t_all // seq,),
        in_specs=[row, row, pl.BlockSpec((N_HEADS, 1), lambda b: (0, 0))],
        out_specs=[row, pl.BlockSpec((N_HEADS, LANES), lambda b: (0, 0))],
        out_shape=[jax.ShapeDtypeStruct((N_HEADS, t_all), F32), jax.ShapeDtypeStruct((N_HEADS, LANES), F32)],
        compiler_params=_params(),
    )(df_t, fa_t, bf)


FOX_T = 256


def _fox_prep(dst, src_ref, lo, hi):
    for p in range(4):
        v = src_ref[:, LANES * p:LANES * (p + 1)]
        dst[2 * p] = jnp.where(lo, v, jnp.zeros_like(v))
        dst[2 * p + 1] = jnp.where(hi, v, jnp.zeros_like(v))


def _fox_fwd(za, vt, f_col, seq, shards):
    t_all = za.shape[0]
    tq = FOX_T
    nq = seq // tq
    nbat = t_all // seq
    n = len(shards)
    to_all = [True] * n

    def body(*refs):
        q_ref, k_ref, vt_ref, fc_ref = refs[:4]
        o_ref, lse_ref = refs[4 + n:6 + n]
        qm_sc, m_sc, l_sc, acc_sc, a_sc, st_sc, pe_sc = refs[6 + 2 * n:13 + 2 * n]
        comm = (refs[4:4 + n], refs[6 + n:6 + 2 * n], to_all, refs[13 + 2 * n:])
        i = pl.program_id(1)

        @pl.when((pl.program_id(0) == 0) & (i == 0))
        def _():
            for cp in _comm_copies(*comm):
                cp.start()
        lo, hi = _half_masks(tq)
        r = lax.broadcasted_iota(jnp.int32, (tq, tq), 0)
        c = lax.broadcasted_iota(jnp.int32, (tq, tq), 1)
        tri = c >= r
        _fox_prep(qm_sc, q_ref, lo, hi)
        m_sc[...] = jnp.full(m_sc.shape, NEG, F32)
        l_sc[...] = jnp.zeros_like(l_sc)
        acc_sc[...] = jnp.zeros_like(acc_sc)

        def block(j, masked):
            sl = pl.ds(pl.multiple_of(j * tq, tq), tq)
            for p in range(4):
                kj = k_ref[sl, LANES * p:LANES * (p + 1)]
                for h in (2 * p, 2 * p + 1):
                    st = _nt(kj, qm_sc[h]) - fc_ref[sl, h:h + 1]
                    st_sc[h] = jnp.where(tri, st, NEG) if masked else st
            for h in range(N_HEADS):
                st = st_sc[h]
                m = m_sc[h:h + 1, :]
                mn = jnp.maximum(m, jnp.max(st, axis=0, keepdims=True))
                a = jnp.exp(m - mn)
                pe = jnp.exp(st - mn)
                m_sc[h:h + 1, :] = mn
                a_sc[h:h + 1, :] = a
                l_sc[h:h + 1, :] = a * l_sc[h:h + 1, :] + jnp.sum(pe, axis=0, keepdims=True)
                pe_sc[h] = pe.astype(BF16)
            for h in range(N_HEADS):
                acc_sc[h] = a_sc[h:h + 1, :] * acc_sc[h] + _nn(vt_ref[HEAD_DIM * h:HEAD_DIM * (h + 1), sl], pe_sc[h])

        def step(j, carry):
            block(j, False)
            return carry

        lax.fori_loop(0, i, step, 0)
        block(i, True)
        lse_ref[...] = m_sc[...] + jnp.log(l_sc[...])
        for p in range(4):
            ot = jnp.concatenate([acc_sc[h] / l_sc[h:h + 1, :] for h in (2 * p, 2 * p + 1)], axis=0)
            o_ref[:, LANES * p:LANES * (p + 1)] = ot.T

        @pl.when((pl.program_id(0) == nbat - 1) & (i == nq - 1))
        def _():
            for cp in _comm_copies(*comm):
                cp.wait()

    res = pl.pallas_call(
        body, name="fox_fwd", grid=(nbat, nq),
        in_specs=[pl.BlockSpec((tq, WIDTH), lambda b, i: (b * nq + i, 0)),
                  pl.BlockSpec((seq, WIDTH), lambda b, i: (b, 1)), pl.BlockSpec((WIDTH, seq), lambda b, i: (b, 0)),
                  pl.BlockSpec((seq, LANES), lambda b, i: (b, 0))] + [ANY] * n,
        out_specs=[pl.BlockSpec((tq, WIDTH), lambda b, i: (b * nq + i, 0)),
                   pl.BlockSpec((N_HEADS, tq), lambda b, i: (0, b * nq + i))] + [ANY] * n,
        out_shape=[jax.ShapeDtypeStruct((t_all, WIDTH), F32), jax.ShapeDtypeStruct((N_HEADS, t_all), F32)]
        + _comm_out_shapes(shards, to_all),
        scratch_shapes=[pltpu.VMEM((N_HEADS, tq, LANES), BF16), pltpu.VMEM((N_HEADS, tq), F32),
                        pltpu.VMEM((N_HEADS, tq), F32), pltpu.VMEM((N_HEADS, HEAD_DIM, tq), F32),
                        pltpu.VMEM((N_HEADS, tq), F32), pltpu.VMEM((N_HEADS, tq, tq), F32),
                        pltpu.VMEM((N_HEADS, tq, tq), BF16)] + _comm_sems(n),
        compiler_params=_params(),
    )(za, za, vt, f_col, *shards)
    return res[0], res[1], res[2:]


def _fox_bwd(za, do, f_col, lse_row, dl_row, seq, grads):
    t_all = za.shape[0]
    tk = FOX_T
    nk = seq // tk
    nbat = t_all // seq
    n = len(grads)
    to_all = [False] * n

    def body(*refs):
        k_ref, v_ref, q_ref, do_ref, fc_ref, lr_ref, dr_ref = refs[:7]
        dk_ref, dv_ref, df_ref, dqt_ref, dfq_ref = refs[7 + n:12 + n]
        km_sc, vm_sc, fk_sc, dk_sc, dv_sc, cs_sc, kt_sc, st_sc, dp_sc, pt_sc, ds_sc = refs[12 + 2 * n:23 + 2 * n]
        comm = (refs[7:7 + n], refs[12 + n:12 + 2 * n], to_all, refs[23 + 2 * n:])
        j = pl.program_id(1)

        @pl.when(j == 0)
        def _():
            dqt_ref[...] = jnp.zeros_like(dqt_ref)
            dfq_ref[...] = jnp.zeros_like(dfq_ref)

        @pl.when((pl.program_id(0) == 0) & (j == 0))
        def _():
            for cp in _comm_copies(*comm):
                cp.start()
        lo, hi = _half_masks(tk)
        r = lax.broadcasted_iota(jnp.int32, (tk, tk), 0)
        c = lax.broadcasted_iota(jnp.int32, (tk, tk), 1)
        tri = c >= r
        _fox_prep(km_sc, k_ref, lo, hi)
        _fox_prep(vm_sc, v_ref, lo, hi)
        for h in range(N_HEADS):
            fk_sc[h] = jnp.broadcast_to(fc_ref[:, h:h + 1], (tk, tk))
        for p in range(4):
            kt_sc[p] = k_ref[:, LANES * p:LANES * (p + 1)].astype(F32).T.astype(BF16)
        dk_sc[...] = jnp.zeros_like(dk_sc)
        dv_sc[...] = jnp.zeros_like(dv_sc)
        cs_sc[...] = jnp.zeros_like(cs_sc)

        def block(i, masked):
            sl = pl.ds(pl.multiple_of(i * tk, tk), tk)
            for p in range(4):
                cs = slice(LANES * p, LANES * (p + 1))
                qi = q_ref[sl, cs]
                doi = do_ref[sl, cs]
                for h in (2 * p, 2 * p + 1):
                    st = _nt(km_sc[h], qi) - fk_sc[h] - lr_ref[h:h + 1, sl]
                    st_sc[h] = jnp.where(tri, st, NEG) if masked else st
                    dp_sc[h] = _nt(vm_sc[h], doi) - dr_ref[h:h + 1, sl]
            for h in range(N_HEADS):
                pt = jnp.exp(st_sc[h])
                dst = pt * dp_sc[h]
                pt_sc[h] = pt.astype(BF16)
                ds_sc[h] = dst.astype(BF16)
                cs_sc[h] += dst[:, :LANES] + dst[:, LANES:]
                dfq_ref[h:h + 1, sl] += jnp.sum(dst, axis=0, keepdims=True)
            for p in range(4):
                cs = slice(LANES * p, LANES * (p + 1))
                qi = q_ref[sl, cs]
                doi = do_ref[sl, cs]
                for h in (2 * p, 2 * p + 1):
                    dv_sc[h] += _nn(pt_sc[h], doi)
                    dk_sc[h] += _nn(ds_sc[h], qi)
                    kt = kt_sc[p, HEAD_DIM * (h % 2):HEAD_DIM * (h % 2 + 1), :]
                    dqt_ref[HEAD_DIM * h:HEAD_DIM * (h + 1), sl] += _nn(kt, ds_sc[h])

        def step(i, carry):
            block(i, False)
            return carry

        block(j, True)
        lax.fori_loop(j + 1, nk, step, 0)
        df_ref[...] = jnp.zeros_like(df_ref)
        for p in range(4):
            cs = slice(LANES * p, LANES * (p + 1))
            dk_ref[:, cs] = jnp.where(lo, dk_sc[2 * p], dk_sc[2 * p + 1]).astype(BF16)
            dv_ref[:, cs] = jnp.where(lo, dv_sc[2 * p], dv_sc[2 * p + 1]).astype(BF16)
            for h in (2 * p, 2 * p + 1):
                df_ref[:, h:h + 1] = -jnp.sum(cs_sc[h], axis=1, keepdims=True)

        @pl.when(j == nk - 1)
        def _():
            dqt_ref[...] = dqt_ref[...] * Q_SCALE

        @pl.when((pl.program_id(0) == nbat - 1) & (j == nk - 1))
        def _():
            for cp in _comm_copies(*comm):
                cp.wait()

    tile = lambda w, col: pl.BlockSpec((tk, w), lambda b, j: (b * nk + j, col))
    full = lambda col: pl.BlockSpec((seq, WIDTH), lambda b, j: (b, col))
    row = pl.BlockSpec((N_HEADS, seq), lambda b, j: (0, b))
    acc = pltpu.VMEM((N_HEADS, tk, LANES), F32)
    res = pl.pallas_call(
        body, name="fox_bwd", grid=(nbat, nk),
        in_specs=[tile(WIDTH, 1), tile(WIDTH, 2), full(0), full(0), tile(LANES, 0), row, row] + [ANY] * n,
        out_specs=[tile(WIDTH, 0), tile(WIDTH, 0), tile(LANES, 0), pl.BlockSpec((WIDTH, seq), lambda b, j: (b, 0)),
                   row] + [ANY] * n,
        out_shape=[jax.ShapeDtypeStruct((t_all, WIDTH), BF16), jax.ShapeDtypeStruct((t_all, WIDTH), BF16),
                   jax.ShapeDtypeStruct((t_all, LANES), F32), jax.ShapeDtypeStruct((nbat * WIDTH, seq), F32),
                   jax.ShapeDtypeStruct((N_HEADS, t_all), F32)] + _comm_out_shapes(grads, to_all),
        scratch_shapes=[pltpu.VMEM((N_HEADS, tk, LANES), BF16), pltpu.VMEM((N_HEADS, tk, LANES), BF16),
                        pltpu.VMEM((N_HEADS, tk, tk), F32), acc, acc, acc, pltpu.VMEM((4, LANES, tk), BF16),
                        pltpu.VMEM((N_HEADS, tk, tk), F32), pltpu.VMEM((N_HEADS, tk, tk), F32),
                        pltpu.VMEM((N_HEADS, tk, tk), BF16), pltpu.VMEM((N_HEADS, tk, tk), BF16)]
        + _comm_sems(n),
        compiler_params=_params(),
    )(za, za, za, do, f_col, lse_row, dl_row, *grads)
    return res[0], res[1], res[2], res[3], res[4], res[5:]


def _dil_mask(has_prev):
    qi = lax.broadcasted_iota(jnp.int32, (BLK, 2 * BLK), 0)
    kj = lax.broadcasted_iota(jnp.int32, (BLK, 2 * BLK), 1)
    dist = qi + BLK - kj
    return (dist >= 0) & (dist <= BLK) & ((kj >= BLK) | has_prev)


def _dil_specs(t_all):
    nb = t_all // BLK
    cur = pl.BlockSpec((1, BLK, WIDTH), lambda p, n: (p, n, 0))
    prev = pl.BlockSpec((1, BLK, WIDTH), lambda p, n: (p, jnp.maximum(n - 1, 0), 0))
    nxt = pl.BlockSpec((1, BLK, WIDTH), lambda p, n: (p, jnp.minimum(n + 1, nb - 1), 0))
    stat = pl.BlockSpec((1, BLK, LANES), lambda p, n: (p, n, 0))
    return nb, cur, prev, nxt, stat


def _dil_fwd(qs, ks, vs, seq):
    t_all = qs.shape[1]
    nb, cur, prev, _, stat = _dil_specs(t_all)

    def body(q_ref, kp_ref, kc_ref, vp_ref, vc_ref, o_ref, lse_ref, s_sc, p_sc):
        nbs = (seq // BLK) >> (2 * pl.program_id(0))
        mask = _dil_mask((pl.program_id(1) & (nbs - 1)) != 0)
        lo, hi = _half_masks(BLK)
        lse_ref[...] = jnp.zeros_like(lse_ref)
        for p in range(4):
            cs = slice(LANES * p, LANES * (p + 1))
            qp = q_ref[0, :, cs]
            kcat = jnp.concatenate([kp_ref[0, :, cs], kc_ref[0, :, cs]], axis=0)
            for e in (0, 1):
                qe = jnp.where(lo if e == 0 else hi, qp, jnp.zeros_like(qp))
                s_sc[2 * p + e] = jnp.where(mask, _nt(qe, kcat), NEG)
        inv = []
        for h in range(N_HEADS):
            s = s_sc[h]
            m = jnp.max(s, axis=1, keepdims=True)
            pe = jnp.exp(s - m)
            l = jnp.sum(pe, axis=1, keepdims=True)
            p_sc[h] = pe.astype(BF16)
            inv.append(1.0 / l)
            lse_ref[0, :, h:h + 1] = m + jnp.log(l)
        for p in range(4):
            cs = slice(LANES * p, LANES * (p + 1))
            vcat = jnp.concatenate([vp_ref[0, :, cs], vc_ref[0, :, cs]], axis=0)
            res = [_nn(p_sc[h], vcat) * inv[h] for h in (2 * p, 2 * p + 1)]
            o_ref[0, :, cs] = jnp.where(lo, res[0], res[1])

    return pl.pallas_call(
        body, name="dil_fwd", grid=(3, nb), in_specs=[cur, prev, cur, prev, cur], out_specs=[cur, stat],
        out_shape=[jax.ShapeDtypeStruct((3, t_all, WIDTH), F32), jax.ShapeDtypeStruct((3, t_all, LANES), F32)],
        scratch_shapes=[pltpu.VMEM((N_HEADS, BLK, 2 * BLK), F32), pltpu.VMEM((N_HEADS, BLK, 2 * BLK), BF16)],
        compiler_params=_params(),
    )(qs, ks, ks, vs, vs)


def _dil_bwd_dq(qs, ks, vs, dos, lses, dls, seq):
    t_all = qs.shape[1]
    nb, cur, prev, _, stat = _dil_specs(t_all)

    def body(q_ref, kp_ref, kc_ref, vp_ref, vc_ref, do_ref, lse_ref, dl_ref, dq_ref, s_sc, dp_sc, ds_sc):
        nbs = (seq // BLK) >> (2 * pl.program_id(0))
        mask = _dil_mask((pl.program_id(1) & (nbs - 1)) != 0)
        lo, hi = _half_masks(BLK)
        for p in range(4):
            cs = slice(LANES * p, LANES * (p + 1))
            qp = q_ref[0, :, cs]
            dop = do_ref[0, :, cs]
            kcat = jnp.concatenate([kp_ref[0, :, cs], kc_ref[0, :, cs]], axis=0)
            vcat = jnp.concatenate([vp_ref[0, :, cs], vc_ref[0, :, cs]], axis=0)
            for e in (0, 1):
                h = 2 * p + e
                sel = lo if e == 0 else hi
                qe = jnp.where(sel, qp, jnp.zeros_like(qp))
                doe = jnp.where(sel, dop, jnp.zeros_like(dop))
                s_sc[h] = jnp.where(mask, _nt(qe, kcat) - lse_ref[0, :, h:h + 1], NEG)
                dp_sc[h] = _nt(doe, vcat) - dl_ref[0, :, h:h + 1]
        for h in range(N_HEADS):
            ds_sc[h] = (jnp.exp(s_sc[h]) * dp_sc[h]).astype(BF16)
        for p in range(4):
            cs = slice(LANES * p, LANES * (p + 1))
            kcat = jnp.concatenate([kp_ref[0, :, cs], kc_ref[0, :, cs]], axis=0)
            dq_ref[0, :, cs] = jnp.where(lo, _nn(ds_sc[2 * p], kcat), _nn(ds_sc[2 * p + 1], kcat)) * Q_SCALE

    wide = pltpu.VMEM((N_HEADS, BLK, 2 * BLK), F32)
    return pl.pallas_call(
        body, name="dil_bwd_dq", grid=(3, nb), in_specs=[cur, prev, cur, prev, cur, cur, stat, stat], out_specs=cur,
        out_shape=jax.ShapeDtypeStruct((3, t_all, WIDTH), F32),
        scratch_shapes=[wide, wide, pltpu.VMEM((N_HEADS, BLK, 2 * BLK), BF16)], compiler_params=_params(),
    )(qs, ks, ks, vs, vs, dos, lses, dls)


def _dil_bwd_dkv(qs, ks, vs, dos, lse_rows, dl_rows, seq):
    t_all = qs.shape[1]
    nb, cur, _, nxt, _ = _dil_specs(t_all)
    rcur = pl.BlockSpec((1, N_HEADS, BLK), lambda p, n: (p, 0, n))
    rnxt = pl.BlockSpec((1, N_HEADS, BLK), lambda p, n: (p, 0, jnp.minimum(n + 1, nb - 1)))

    def body(k_ref, v_ref, qc_ref, qn_ref, dc_ref, dn_ref, lc_ref, ln_ref, ec_ref, en_ref, dk_ref, dv_ref, s_sc, dp_sc,
             pt_sc, ds_sc):
        nbs = (seq // BLK) >> (2 * pl.program_id(0))
        has_next = ((pl.program_id(1) + 1) & (nbs - 1)) != 0
        r = lax.broadcasted_iota(jnp.int32, (BLK, 2 * BLK), 0)
        c = lax.broadcasted_iota(jnp.int32, (BLK, 2 * BLK), 1)
        mask = ((c < BLK) & (c >= r)) | ((c >= BLK) & (c - BLK <= r) & has_next)
        lo, hi = _half_masks(BLK)
        for p in range(4):
            cs = slice(LANES * p, LANES * (p + 1))
            kp = k_ref[0, :, cs]
            vp = v_ref[0, :, cs]
            qcat = jnp.concatenate([qc_ref[0, :, cs], qn_ref[0, :, cs]], axis=0)
            dcat = jnp.concatenate([dc_ref[0, :, cs], dn_ref[0, :, cs]], axis=0)
            for e in (0, 1):
                h = 2 * p + e
                sel = lo if e == 0 else hi
                ke = jnp.where(sel, kp, jnp.zeros_like(kp))
                ve = jnp.where(sel, vp, jnp.zeros_like(vp))
                lrow = jnp.concatenate([lc_ref[0, h:h + 1, :], ln_ref[0, h:h + 1, :]], axis=1)
                erow = jnp.concatenate([ec_ref[0, h:h + 1, :], en_ref[0, h:h + 1, :]], axis=1)
                s_sc[h] = jnp.where(mask, _nt(ke, qcat) - lrow, NEG)
                dp_sc[h] = _nt(ve, dcat) - erow
        for h in range(N_HEADS):
            pt = jnp.exp(s_sc[h])
            pt_sc[h] = pt.astype(BF16)
            ds_sc[h] = (pt * dp_sc[h]).astype(BF16)
        for p in range(4):
            cs = slice(LANES * p, LANES * (p + 1))
            qcat = jnp.concatenate([qc_ref[0, :, cs], qn_ref[0, :, cs]], axis=0)
            dcat = jnp.concatenate([dc_ref[0, :, cs], dn_ref[0, :, cs]], axis=0)
            dk_ref[0, :, cs] = jnp.where(lo, _nn(ds_sc[2 * p], qcat), _nn(ds_sc[2 * p + 1], qcat))
            dv_ref[0, :, cs] = jnp.where(lo, _nn(pt_sc[2 * p], dcat), _nn(pt_sc[2 * p + 1], dcat))

    wide = pltpu.VMEM((N_HEADS, BLK, 2 * BLK), F32)
    half = pltpu.VMEM((N_HEADS, BLK, 2 * BLK), BF16)
    return pl.pallas_call(
        body, name="dil_bwd_dkv", grid=(3, nb),
        in_specs=[cur, cur, cur, nxt, cur, nxt, rcur, rnxt, rcur, rnxt], out_specs=[cur, cur],
        out_shape=[jax.ShapeDtypeStruct((3, t_all, WIDTH), F32)] * 2, scratch_shapes=[wide, wide, half, half],
        compiler_params=_params(),
    )(ks, vs, qs, qs, dos, dos, lse_rows, lse_rows, dl_rows, dl_rows)


def _mix_out(oa, o3, l3, gn_a, gn_b, w_out, x, ada3, ln_g, ln_b, seq):
    t_all = x.shape[0]
    tm = 256
    nts = seq // tm

    def body(oa_ref, o1_ref, o2_ref, o3_ref, l1_ref, l2_ref, l3_ref, ga_ref, gb_ref, w_ref, x_ref, ada_ref, g_ref,
             b_ref, ob_ref, lse_ref, mg_ref, mix_ref, xh_ref, rs_ref, h2_ref):
        e, et = _head_mats()
        la, lb, lc = l1_ref[...], l2_ref[...], l3_ref[...]
        mx = jnp.maximum(jnp.maximum(la, lb), lc)
        ea, eb, ec = jnp.exp(la - mx), jnp.exp(lb - mx), jnp.exp(lc - mx)
        tot = ea + eb + ec
        lse_ref[...] = mx + jnp.log(tot)
        ob = (o1_ref[...] * _hexp(ea / tot, e) + o2_ref[...] * _hexp(eb / tot, e) + o3_ref[...] * _hexp(ec / tot, e))
        ob_ref[...] = ob

        def rms(o, gain):
            rr = lax.rsqrt(_hsum(o * o, et) * (1.0 / HEAD_DIM) + RMS_EPS)
            return o * _hexp(rr, e) * gain

        merged = jnp.concatenate([rms(oa_ref[...], ga_ref[...]), rms(ob, gb_ref[...])], axis=1).astype(BF16)
        mg_ref[...] = merged
        mix = _nn(merged, w_ref[...])
        mix_ref[...] = mix.astype(BF16)
        r1 = ALPHA * x_ref[...] + ada_ref[0, 2:3, :] * mix
        d = r1 - jnp.mean(r1, axis=1, keepdims=True)
        rstd = lax.rsqrt(jnp.mean(d * d, axis=1, keepdims=True) + LN_EPS)
        xh = d * rstd
        xh_ref[...] = xh
        rs_ref[...] = jnp.broadcast_to(rstd, (tm, LANES))
        x1 = xh * g_ref[...] + b_ref[...]
        h2_ref[...] = (x1 * (1.0 + ada_ref[0, 4:5, :]) + ada_ref[0, 3:4, :]).astype(BF16)

    tok = lambda w: pl.BlockSpec((tm, w), lambda i: (i, 0))
    vec = lambda w: pl.BlockSpec((1, w), lambda i: (0, 0))
    return pl.pallas_call(
        body, name="mix_out", grid=(t_all // tm,),
        in_specs=[tok(WIDTH)] * 4 + [tok(LANES)] * 3 + [vec(WIDTH), vec(WIDTH),
                  pl.BlockSpec(w_out.shape, lambda i: (0, 0)), tok(D_MODEL),
                  pl.BlockSpec((1, 6, D_MODEL), lambda i: (i // nts, 0, 0)), vec(D_MODEL), vec(D_MODEL)],
        out_specs=[tok(WIDTH), tok(LANES), tok(D_MODEL), tok(D_MODEL), tok(D_MODEL), tok(LANES), tok(D_MODEL)],
        out_shape=[jax.ShapeDtypeStruct((t_all, WIDTH), F32), jax.ShapeDtypeStruct((t_all, LANES), F32),
                   jax.ShapeDtypeStruct((t_all, D_MODEL), BF16), jax.ShapeDtypeStruct((t_all, D_MODEL), BF16),
                   jax.ShapeDtypeStruct((t_all, D_MODEL), F32), jax.ShapeDtypeStruct((t_all, LANES), F32),
                   jax.ShapeDtypeStruct((t_all, D_MODEL), BF16)],
        compiler_params=_params(),
    )(oa, o3[0], o3[1], o3[2], l3[0], l3[1], l3[2], gn_a, gn_b, w_out, x, ada3, ln_g, ln_b)


def _mix_out_bwd(dmix, w_out, oa, ob, gn_a, gn_b):
    t_all = dmix.shape[0]
    tm = 256

    def body(dm_ref, w_ref, oa_ref, ob_ref, ga_ref, gb_ref, doa_ref, dob_ref, dla_ref, dlb_ref, acc_ref):
        @pl.when(pl.program_id(0) == 0)
        def _():
            acc_ref[...] = jnp.zeros_like(acc_ref)
        e, et = _head_mats()
        dmg = _nt(dm_ref[...], w_ref[...])

        def group(o, dn, gain):
            rr = lax.rsqrt(_hsum(o * o, et) * (1.0 / HEAD_DIM) + RMS_EPS)
            re = _hexp(rr, e)
            dgain = jnp.sum(dn * o * re, axis=0, keepdims=True)
            dxn = dn * gain
            tt = _hsum(dxn * o, et) * (rr * rr * rr) * (1.0 / HEAD_DIM)
            do = re * dxn - o * _hexp(tt, e)
            return do, _hsum(do * o, et), dgain

        doa, dla, dga = group(oa_ref[...], dmg[:, :WIDTH], ga_ref[...])
        dob, dlb, dgb = group(ob_ref[...], dmg[:, WIDTH:], gb_ref[...])
        doa_ref[...] = doa.astype(BF16)
        dob_ref[...] = dob.astype(BF16)
        dla_ref[...] = dla
        dlb_ref[...] = dlb
        acc_ref[0:1, :] += jnp.concatenate([dga, dgb], axis=1)

    tok = lambda w: pl.BlockSpec((tm, w), lambda i: (i, 0))
    vec = lambda w: pl.BlockSpec((1, w), lambda i: (0, 0))
    return pl.pallas_call(
        body, name="mix_out_bwd", grid=(t_all // tm,),
        in_specs=[tok(D_MODEL), pl.BlockSpec(w_out.shape, lambda i: (0, 0)), tok(WIDTH), tok(WIDTH), vec(WIDTH),
                  vec(WIDTH)],
        out_specs=[tok(WIDTH), tok(WIDTH), tok(LANES), tok(LANES), pl.BlockSpec((8, D_MODEL), lambda i: (0, 0))],
        out_shape=[jax.ShapeDtypeStruct((t_all, WIDTH), BF16), jax.ShapeDtypeStruct((t_all, WIDTH), BF16),
                   jax.ShapeDtypeStruct((t_all, LANES), F32), jax.ShapeDtypeStruct((t_all, LANES), F32),
                   jax.ShapeDtypeStruct((8, D_MODEL), F32)],
        compiler_params=_params(),
    )(dmix, w_out, oa, ob, gn_a, gn_b)


def _inproj_bwd(dza, dqb, dkb, dvb, dfa16, pos, wqkv, wf16, freq, dr1, x, ada3, seq):
    t_all = x.shape[0]
    tm = 256
    nts = seq // tm
    nbat = t_all // seq

    def body(dza_ref, dqb_ref, dkb_ref, dvb_ref, dfa_ref, pos_ref, w_ref, wf_ref, fr_ref, dr1_ref, x_ref, ada_ref,
             gx_ref, dz_ref, acc_ref):
        i = pl.program_id(0)

        @pl.when(i == 0)
        def _():
            acc_ref[...] = jnp.zeros_like(acc_ref)
        tabs = _rope_tabs(pos_ref, fr_ref, -1.0)
        dz_ref[:, :3 * WIDTH] = dza_ref[...]
        dz_ref[:, 3 * WIDTH:4 * WIDTH] = _rope(dqb_ref[...], tabs).astype(BF16)
        dz_ref[:, 4 * WIDTH:5 * WIDTH] = _rope(dkb_ref[...], tabs).astype(BF16)
        dz_ref[:, 5 * WIDTH:] = dvb_ref[...].astype(BF16)
        dh1 = _tn(dfa_ref[...], wf_ref[...])
        for n in range(6):
            cs = slice(n * WIDTH, (n + 1) * WIDTH)
            dh1 = dh1 + _nt(dz_ref[:, cs], w_ref[:, cs])
        xv = x_ref[...]
        gx_ref[...] = ALPHA * dr1_ref[...] + dh1 * (1.0 + ada_ref[0, 1:2, :])
        b = i // nts
        acc_ref[pl.ds(b, 1), :] += jnp.sum(dh1 * xv, axis=0, keepdims=True)
        acc_ref[pl.ds(8 + b, 1), :] += jnp.sum(dh1, axis=0, keepdims=True)

    tok = lambda w: pl.BlockSpec((tm, w), lambda i: (i, 0))
    return pl.pallas_call(
        body, name="inproj_bwd", grid=(t_all // tm,),
        in_specs=[tok(3 * WIDTH), tok(WIDTH), tok(WIDTH), tok(WIDTH), pl.BlockSpec((16, tm), lambda i: (0, i)),
                  tok(1), pl.BlockSpec(wqkv.shape, lambda i: (0, 0)), pl.BlockSpec(wf16.shape, lambda i: (0, 0)),
                  pl.BlockSpec((1, LANES), lambda i: (0, 0)), tok(D_MODEL), tok(D_MODEL),
                  pl.BlockSpec((1, 6, D_MODEL), lambda i: (i // nts, 0, 0))],
        out_specs=[tok(D_MODEL), tok(6 * WIDTH), pl.BlockSpec((16, D_MODEL), lambda i: (0, 0))],
        out_shape=[jax.ShapeDtypeStruct((t_all, D_MODEL), F32), jax.ShapeDtypeStruct((t_all, 6 * WIDTH), BF16),
                   jax.ShapeDtypeStruct((16, D_MODEL), F32)],
        compiler_params=_params(),
    )(dza, dqb, dkb, dvb, dfa16, pos, wqkv, wf16, freq, dr1, x, ada3)


FFN_TM = 512
FFN_TN = 256
HALO = 8


FFN_CHUNK = 64


def _conv(cat_ref, w_ref, b_ref, start, rows):
    return (b_ref[...] + w_ref[0:1, :] * cat_ref[pl.ds(start + HALO - 2, rows), :]
            + w_ref[1:2, :] * cat_ref[pl.ds(start + HALO - 1, rows), :]
            + w_ref[2:3, :] * cat_ref[pl.ds(start + HALO, rows), :])


def _ffn_gate(u, conv_w, conv_b, seq):
    t_all = u.shape[0]
    tm, tn = FFN_TM, FFN_TN
    nc = D_FF // tn
    nts = seq // tm

    def body(ua_ref, uap_ref, ug_ref, ugp_ref, wa_ref, wg_ref, ba_ref, bg_ref, o_ref, ca_ref, cg_ref):
        first = (pl.program_id(0) % nts) == 0
        zero = jnp.zeros((HALO, tn), F32)
        ca_ref[0:HALO, :] = jnp.where(first, zero, uap_ref[...])
        cg_ref[0:HALO, :] = jnp.where(first, zero, ugp_ref[...])
        ca_ref[HALO:, :] = ua_ref[...]
        cg_ref[HALO:, :] = ug_ref[...]
        for c0 in range(0, tm, FFN_CHUNK):
            ya = _conv(ca_ref, wa_ref, ba_ref, c0, FFN_CHUNK)
            yg = _conv(cg_ref, wg_ref, bg_ref, c0, FFN_CHUNK)
            o_ref[c0:c0 + FFN_CHUNK, :] = (yg * jax.nn.sigmoid(yg) * ya).astype(BF16)

    cur = lambda off: pl.BlockSpec((tm, tn), lambda t, n: (t, n + off))
    prev = lambda off: pl.BlockSpec((HALO, tn), lambda t, n: (jnp.maximum(t * (tm // HALO) - 1, 0), n + off))
    vec = lambda r, off: pl.BlockSpec((r, tn), lambda t, n: (0, n + off))
    return pl.pallas_call(
        body, name="ffn_gate", grid=(t_all // tm, nc),
        in_specs=[cur(0), prev(0), cur(nc), prev(nc), vec(3, 0), vec(3, nc), vec(1, 0), vec(1, nc)],
        out_specs=pl.BlockSpec((tm, tn), lambda t, n: (t, n)),
        out_shape=jax.ShapeDtypeStruct((t_all, D_FF), BF16),
        scratch_shapes=[pltpu.VMEM((tm + HALO, tn), F32)] * 2, compiler_params=_params(),
    )(u, u, u, u, conv_w, conv_w, conv_b, conv_b)


def _ffn_gate_bwd(u, dfi, conv_w, conv_b, seq):
    t_all = u.shape[0]
    tm, tn = FFN_TM, FFN_TN
    nc = D_FF // tn
    nts = seq // tm

    def body(ua_ref, uap_ref, uan_ref, ug_ref, ugp_ref, ugn_ref, df_ref, dfn_ref, wa_ref, wg_ref, ba_ref, bg_ref,
             dua_ref, dug_ref, acca_ref, accg_ref, ca_ref, cg_ref, ya_ref, yg_ref):
        t = pl.program_id(1)
        first = (t % nts) == 0
        last = (t % nts) == nts - 1

        @pl.when(t == 0)
        def _():
            acca_ref[...] = jnp.zeros_like(acca_ref)
            accg_ref[...] = jnp.zeros_like(accg_ref)
        zero = jnp.zeros((HALO, tn), F32)
        for cat, cur, prv, nxt in ((ca_ref, ua_ref, uap_ref, uan_ref), (cg_ref, ug_ref, ugp_ref, ugn_ref)):
            cat[0:HALO, :] = jnp.where(first, zero, prv[...])
            cat[HALO:HALO + tm, :] = cur[...]
            cat[HALO + tm:, :] = nxt[...]
        ch = FFN_CHUNK
        sums = [[jnp.zeros((1, tn), F32) for _ in range(4)] for _ in range(2)]
        for ci, c0 in enumerate(range(0, tm, ch)):
            ya = _conv(ca_ref, wa_ref, ba_ref, c0, ch + HALO)
            yg = _conv(cg_ref, wg_ref, bg_ref, c0, ch + HALO)
            if c0 + ch < tm:
                beyond = df_ref[c0 + ch:c0 + ch + 16, :].astype(F32)[:HALO]
            else:
                beyond = jnp.where(last, 0.0, dfn_ref[...].astype(F32)[:HALO])
            dfe = jnp.concatenate([df_ref[c0:c0 + ch, :].astype(F32), beyond], axis=0)
            sg = jax.nn.sigmoid(yg)
            ya_ref[ci] = dfe * (yg * sg)
            yg_ref[ci] = dfe * ya * (sg * (1.0 + yg * (1.0 - sg)))
            for half, (dy, cat, w_ref, du_ref) in enumerate(((ya_ref, ca_ref, wa_ref, dua_ref),
                                                             (yg_ref, cg_ref, wg_ref, dug_ref))):
                d0 = dy[ci, 0:ch, :]
                du = (w_ref[2:3, :] * d0 + w_ref[1:2, :] * dy[ci, pl.ds(1, ch), :]
                      + w_ref[0:1, :] * dy[ci, pl.ds(2, ch), :])
                du_ref[c0:c0 + ch, :] = du.astype(BF16)
                for k in range(3):
                    sums[half][k] += jnp.sum(d0 * cat[pl.ds(c0 + HALO - 2 + k, ch), :], axis=0, keepdims=True)
                sums[half][3] += jnp.sum(d0, axis=0, keepdims=True)
        for half, acc in enumerate((acca_ref, accg_ref)):
            for k in range(4):
                acc[k:k + 1, :] += sums[half][k]

    nrow = t_all // HALO
    cur = lambda off: pl.BlockSpec((tm, tn), lambda n, t: (t, n + off))
    prev = lambda off: pl.BlockSpec((HALO, tn), lambda n, t: (jnp.maximum(t * (tm // HALO) - 1, 0), n + off))
    nxt = lambda off: pl.BlockSpec((HALO, tn), lambda n, t: (jnp.minimum((t + 1) * (tm // HALO), nrow - 1), n + off))
    vec = lambda r, off: pl.BlockSpec((r, tn), lambda n, t: (0, n + off))
    dcur = pl.BlockSpec((tm, tn), lambda n, t: (t, n))
    dnxt = pl.BlockSpec((16, tn), lambda n, t: (jnp.minimum((t + 1) * (tm // 16), t_all // 16 - 1), n))
    acc = pl.BlockSpec((8, tn), lambda n, t: (0, n))
    return pl.pallas_call(
        body, name="ffn_gate_bwd", grid=(nc, t_all // tm),
        in_specs=[cur(0), prev(0), nxt(0), cur(nc), prev(nc), nxt(nc), dcur, dnxt, vec(3, 0), vec(3, nc), vec(1, 0),
                  vec(1, nc)],
        out_specs=[dcur, dcur, acc, acc],
        out_shape=[jax.ShapeDtypeStruct((t_all, D_FF), BF16), jax.ShapeDtypeStruct((t_all, D_FF), BF16),
                   jax.ShapeDtypeStruct((8, D_FF), F32), jax.ShapeDtypeStruct((8, D_FF), F32)],
        scratch_shapes=[pltpu.VMEM((tm + 2 * HALO, tn), F32)] * 2
        + [pltpu.VMEM((tm // FFN_CHUNK, FFN_CHUNK + HALO, tn), F32)] * 2,
        compiler_params=_params(),
    )(u, u, u, u, u, u, dfi, dfi, conv_w, conv_w, conv_b, conv_b)


def _ffn_down(ffn_in, w_down, xh1, ln1_g, ln1_b, ada3, ln2_g, ln2_b, target, seq):
    t_all = xh1.shape[0]
    tm = 256
    nts = seq // tm

    def body(f_ref, w_ref, xh_ref, g1_ref, b1_ref, ada_ref, g2_ref, b2_ref, tg_ref, dr2_ref, acc_ref):
        i = pl.program_id(0)

        @pl.when(i == 0)
        def _():
            acc_ref[...] = jnp.zeros_like(acc_ref)
        ffn = _nn(f_ref[...], w_ref[...])
        x1 = xh_ref[...] * g1_ref[...] + b1_ref[...]
        r2 = ALPHA * x1 + ada_ref[0, 5:6, :] * ffn
        d = r2 - jnp.mean(r2, axis=1, keepdims=True)
        rstd = lax.rsqrt(jnp.mean(d * d, axis=1, keepdims=True) + LN_EPS)
        xh2 = d * rstd
        diff = xh2 * g2_ref[...] + b2_ref[...] - tg_ref[...]
        dy = diff * (1.0 / D_MODEL)
        dr2 = _layer_norm_bwd(dy * g2_ref[...], xh2, rstd)
        dr2_ref[...] = dr2
        acc_ref[0:1, :] += jnp.sum(dy * xh2, axis=0, keepdims=True)
        acc_ref[1:2, :] += jnp.sum(dy, axis=0, keepdims=True)
        acc_ref[2:3, :] += jnp.sum(diff * diff, axis=0, keepdims=True) * (0.5 / D_MODEL)
        acc_ref[pl.ds(8 + i // nts, 1), :] += jnp.sum(dr2 * ffn, axis=0, keepdims=True)

    tok = lambda w: pl.BlockSpec((tm, w), lambda i: (i, 0))
    vec = pl.BlockSpec((1, D_MODEL), lambda i: (0, 0))
    return pl.pallas_call(
        body, name="ffn_down", grid=(t_all // tm,),
        in_specs=[tok(D_FF), pl.BlockSpec(w_down.shape, lambda i: (0, 0)), tok(D_MODEL), vec, vec,
                  pl.BlockSpec((1, 6, D_MODEL), lambda i: (i // nts, 0, 0)), vec, vec, tok(D_MODEL)],
        out_specs=[tok(D_MODEL), pl.BlockSpec((16, D_MODEL), lambda i: (0, 0))],
        out_shape=[jax.ShapeDtypeStruct((t_all, D_MODEL), F32), jax.ShapeDtypeStruct((16, D_MODEL), F32)],
        compiler_params=_params(),
    )(ffn_in, w_down, xh1, ln1_g, ln1_b, ada3, ln2_g, ln2_b, target)


def _ffn_down_bwd(dr2, ada3, w_down, seq):
    t_all = dr2.shape[0]
    tm = 256
    nts = seq // tm

    def body(d_ref, ada_ref, w_ref, dffn_ref, dfi_ref):
        dffn = (d_ref[...] * ada_ref[0, 5:6, :]).astype(BF16)
        dffn_ref[...] = dffn
        dfi_ref[...] = _nt(dffn, w_ref[...]).astype(BF16)

    tok = lambda w: pl.BlockSpec((tm, w), lambda i: (i, 0))
    return pl.pallas_call(
        body, name="ffn_down_bwd", grid=(t_all // tm,),
        in_specs=[tok(D_MODEL), pl.BlockSpec((1, 6, D_MODEL), lambda i: (i // nts, 0, 0)),
                  pl.BlockSpec(w_down.shape, lambda i: (0, 0))],
        out_specs=[tok(D_MODEL), tok(D_FF)],
        out_shape=[jax.ShapeDtypeStruct((t_all, D_MODEL), BF16), jax.ShapeDtypeStruct((t_all, D_FF), BF16)],
        compiler_params=_params(),
    )(dr2, ada3, w_down)


def _ffn_up_bwd(du_a, du_g, w_up, dr2, xh1, rs1, mix, ada3, ln1_g, ln1_b, seq):
    t_all = dr2.shape[0]
    tm = 256
    nts = seq // tm

    def body(da_ref, dg_ref, w_ref, dr2_ref, xh_ref, rs_ref, mix_ref, ada_ref, g_ref, b_ref, dr1_ref, dmix_ref,
             acc_ref):
        i = pl.program_id(0)

        @pl.when(i == 0)
        def _():
            acc_ref[...] = jnp.zeros_like(acc_ref)
        dh2 = _nt(da_ref[...], w_ref[:, :D_FF]) + _nt(dg_ref[...], w_ref[:, D_FF:])
        xh = xh_ref[...]
        x1 = xh * g_ref[...] + b_ref[...]
        dx1 = ALPHA * dr2_ref[...] + dh2 * (1.0 + ada_ref[0, 4:5, :])
        dr1 = _layer_norm_bwd(dx1 * g_ref[...], xh, rs_ref[:, 0:1])
        dr1_ref[...] = dr1
        dmix_ref[...] = (dr1 * ada_ref[0, 2:3, :]).astype(BF16)
        b = i // nts
        acc_ref[0:1, :] += jnp.sum(dx1 * xh, axis=0, keepdims=True)
        acc_ref[1:2, :] += jnp.sum(dx1, axis=0, keepdims=True)
        acc_ref[pl.ds(8 + b, 1), :] += jnp.sum(dh2 * x1, axis=0, keepdims=True)
        acc_ref[pl.ds(16 + b, 1), :] += jnp.sum(dh2, axis=0, keepdims=True)
        acc_ref[pl.ds(24 + b, 1), :] += jnp.sum(dr1 * mix_ref[...].astype(F32), axis=0, keepdims=True)

    tok = lambda w: pl.BlockSpec((tm, w), lambda i: (i, 0))
    vec = pl.BlockSpec((1, D_MODEL), lambda i: (0, 0))
    return pl.pallas_call(
        body, name="ffn_up_bwd", grid=(t_all // tm,),
        in_specs=[tok(D_FF), tok(D_FF), pl.BlockSpec(w_up.shape, lambda i: (0, 0)), tok(D_MODEL), tok(D_MODEL),
                  tok(LANES), tok(D_MODEL), pl.BlockSpec((1, 6, D_MODEL), lambda i: (i // nts, 0, 0)), vec, vec],
        out_specs=[tok(D_MODEL), tok(D_MODEL), pl.BlockSpec((32, D_MODEL), lambda i: (0, 0))],
        out_shape=[jax.ShapeDtypeStruct((t_all, D_MODEL), F32), jax.ShapeDtypeStruct((t_all, D_MODEL), BF16),
                   jax.ShapeDtypeStruct((32, D_MODEL), F32)],
        compiler_params=_params(),
    )(du_a, du_g, w_up, dr2, xh1, rs1, mix, ada3, ln1_g, ln1_b)


def _perm(a, d, seq):
    if d == 1:
        return a
    t_all, w = a.shape
    return a.reshape(t_all // seq, seq // d, d, w).transpose(0, 2, 1, 3).reshape(t_all, w)


def _unperm(a, d, seq):
    if d == 1:
        return a
    t_all, w = a.shape
    return a.reshape(t_all // seq, d, seq // d, w).transpose(0, 2, 1, 3).reshape(t_all, w)


DILATIONS = (1, 4, 16)


def _stack_perm(a, seq):
    return jnp.stack([_perm(a, d, seq) for d in DILATIONS])


def _rows(a):
    return a[:, :N_HEADS].T


def _rope_freq():
    f = np.float32(ROPE_THETA) ** (-np.arange(0, ROPE_DIMS, 2, dtype=np.float32) / np.float32(ROPE_DIMS))
    return jnp.asarray(np.tile(f.astype(np.float32), LANES // (ROPE_DIMS // 2))[None, :])


def _local_step(x, positions, target, ada3, w_in, b_fgate, gn_a, gn_b, ln1_g, ln1_b, conv_b, ln2_g, ln2_b,
                late_shards):
    nbat, seq, _ = x.shape
    t_all = nbat * seq
    xf = x.reshape(t_all, D_MODEL)
    tg = target.reshape(t_all, D_MODEL)
    pos = positions.reshape(t_all, 1)
    freq = _rope_freq()

    wqkv = jnp.concatenate([w_in[:, :3 * WIDTH], w_in[:, 3 * WIDTH + N_HEADS:]], axis=1)
    wf16 = jnp.zeros((16, D_MODEL), BF16).at[:N_HEADS].set(w_in[:, 3 * WIDTH:3 * WIDTH + N_HEADS].T)
    bf = b_fgate.reshape(N_HEADS, 1)

    h1, za, zb, fa_t = _inproj(xf, ada3, pos, wqkv, wf16, freq, seq)
    f_row = _fgate_fwd(fa_t, bf, seq)
    f_col = jnp.zeros((t_all, LANES), F32).at[:, :N_HEADS].set(f_row.T)
    vt = za[:, 2 * WIDTH:].reshape(nbat, seq, WIDTH).transpose(0, 2, 1).reshape(nbat * WIDTH, seq)
    oa, lse_row_a, gathered = _fox_fwd(za, vt, f_col, seq, [late_shards[n] for n in LATE])
    w_out, w_up, conv_w, w_down = (_full_from_gathered(n, g) for n, g in zip(LATE, gathered))
    qs = _stack_perm(zb[:, :WIDTH], seq)
    ks = _stack_perm(zb[:, WIDTH:2 * WIDTH], seq)
    vs = _stack_perm(zb[:, 2 * WIDTH:], seq)
    o3p, l3p = _dil_fwd(qs, ks, vs, seq)
    o3 = [_unperm(o3p[p], d, seq) for p, d in enumerate(DILATIONS)]
    l3 = [_unperm(l3p[p], d, seq) for p, d in enumerate(DILATIONS)]
    ob, lse_b, merged, mix, xh1, rs1, h2 = _mix_out(oa, o3, l3, gn_a, gn_b, w_out, xf, ada3, ln1_g, ln1_b, seq)
    u = _matmul(h2, w_up, False, F32, 256, 512, "ffn_up")
    ffn_in = _ffn_gate(u, conv_w, conv_b, seq)
    dr2, acc2 = _ffn_down(ffn_in, w_down, xh1, ln1_g, ln1_b, ada3, ln2_g, ln2_b, tg, seq)

    dffn, dfi = _ffn_down_bwd(dr2, ada3, w_down, seq)
    d_w_down = _matmul_tn(ffn_in, dffn, 512, 512, "dw_down")
    du_a, du_g, acc_ca, acc_cg = _ffn_gate_bwd(u, dfi, conv_w, conv_b, seq)
    dr1, dmix, acc1 = _ffn_up_bwd(du_a, du_g, w_up, dr2, xh1, rs1, mix, ada3, ln1_g, ln1_b, seq)
    d_w_up = jnp.concatenate([_matmul_tn(h2, du_a, 256, 512, "dw_up_a"), _matmul_tn(h2, du_g, 256, 512, "dw_up_g")],
                             axis=1)

    doa, dob, dl_a, dl_b, acc_gn = _mix_out_bwd(dmix, w_out, oa, ob, gn_a, gn_b)
    d_w_out = _matmul_tn(merged, dmix, 512, 512, "dw_out")
    late_grads = dict(w_out=d_w_out, w_up=d_w_up, conv_w=jnp.concatenate([acc_ca[0:3], acc_cg[0:3]], axis=1),
                      w_down=d_w_down)
    dka, dva, df_k, dqt, df_q, late_parts = _fox_bwd(za, doa, f_col, lse_row_a, _rows(dl_a), seq,
                                                     [_payload(n, _dest_major(n, late_grads[n])) for n in LATE])
    dqa = dqt.reshape(nbat, WIDTH, seq).transpose(0, 2, 1).reshape(t_all, WIDTH).astype(BF16)
    dfa_t, dbf = _fgate_bwd(_rows(df_k) + df_q, fa_t, bf, seq)
    dos = _stack_perm(dob, seq)
    lses = _stack_perm(lse_b, seq)
    dls = _stack_perm(dl_b, seq)
    dq3 = _dil_bwd_dq(qs, ks, vs, dos, lses, dls, seq)
    lse_rows = lses[:, :, :N_HEADS].transpose(0, 2, 1)
    dl_rows = dls[:, :, :N_HEADS].transpose(0, 2, 1)
    dk3, dv3 = _dil_bwd_dkv(qs, ks, vs, dos, lse_rows, dl_rows, seq)
    unsum = lambda a3: sum(_unperm(a3[p], d, seq) for p, d in enumerate(DILATIONS))
    dza = jnp.concatenate([dqa, dka, dva], axis=1)
    dfa16 = jnp.zeros((16, t_all), BF16).at[:N_HEADS].set(dfa_t.astype(BF16))
    grad_x, dz, acc0 = _inproj_bwd(dza, unsum(dq3), unsum(dk3), unsum(dv3), dfa16, pos, wqkv, wf16, freq, dr1, xf,
                                   ada3, seq)
    d_wqkv = _matmul_tn(h1, dz, 512, 512, "dw_in")
    d_wf = _matmul_rows(dfa16, h1, 512, "dw_fgate")[:N_HEADS].T
    d_w_in = jnp.concatenate([d_wqkv[:, :3 * WIDTH], d_wf, d_wqkv[:, 3 * WIDTH:]], axis=1)

    dada = jnp.concatenate([acc0[8:8 + nbat], acc0[:nbat], acc1[24:24 + nbat], acc1[16:16 + nbat], acc1[8:8 + nbat],
                            acc2[8:8 + nbat]], axis=1)

    grads = dict(
        dada=dada, b_ada=jnp.sum(dada, axis=0, keepdims=True), w_in=d_w_in, b_fgate=dbf[:, 0][None, :],
        gn_a=acc_gn[0:1, :WIDTH], gn_b=acc_gn[0:1, WIDTH:], ln1_g=acc1[0:1], ln1_b=acc1[1:2],
        conv_b=jnp.concatenate([acc_ca[3:4], acc_cg[3:4]], axis=1), ln2_g=acc2[0:1], ln2_b=acc2[1:2])
    return acc2[2:3], grad_x.reshape(x.shape), grads, dict(zip(LATE, late_parts))


LATE = ("w_out", "w_up", "conv_w", "w_down")
BIG = ("w_ada", "w_in") + LATE
COLUMN_SHARDED = ("w_ada", "w_in", "w_up", "conv_w")


def _payload(name, a):
    return a if name == "conv_w" else a.astype(BF16)
SMALL = ("b_ada", "b_fgate", "gn_a", "gn_b", "ln1_g", "ln1_b", "conv_b", "ln2_g", "ln2_b")
ADAM_ROWS = dict(w_ada=256, w_in=256, w_out=128, w_up=256, conv_w=3, w_down=176)
SMALL_ROWS = 24


def _full_from_gathered(name, g):
    if name in COLUMN_SHARDED:
        return g.transpose(1, 0, 2).reshape(g.shape[1], N_DEV * g.shape[2])
    return g.reshape(N_DEV * g.shape[1], g.shape[2])


def _dest_major(name, full):
    if name in COLUMN_SHARDED:
        r, cfull = full.shape
        return full.reshape(r, N_DEV, cfull // N_DEV).transpose(1, 0, 2)
    return full.reshape(N_DEV, full.shape[0] // N_DEV, full.shape[1])


def _pack_small(vals, extra=None):
    parts = [vals[n].reshape(-1) for n in SMALL]
    if extra is not None:
        parts.append(extra.reshape(-1))
    flat = jnp.concatenate(parts)
    return jnp.pad(flat, (0, SMALL_ROWS * D_MODEL - flat.shape[0])).reshape(SMALL_ROWS, D_MODEL)


def _unpack_small(packed, like):
    flat = packed.reshape(-1)
    out, off = {}, 0
    for n in SMALL:
        size = like[n].size
        out[n] = flat[off:off + size].reshape(like[n].shape)
        off += size
    return out, flat[off:off + D_MODEL]


def kernel(x, c, positions, w_ada, b_ada, w_in, b_fgate, gn_a, gn_b, w_out, ln1_g, ln1_b, w_up, conv_w, conv_b, w_down, ln2_g, ln2_b, loss_target, m_w_ada, m_b_ada, m_w_in, m_b_fgate, m_gn_a, m_gn_b, m_w_out, m_ln1_g, m_ln1_b, m_w_up, m_conv_w, m_conv_b, m_w_down, m_ln2_g, m_ln2_b, v_w_ada, v_b_ada, v_w_in, v_b_fgate, v_gn_a, v_gn_b, v_w_out, v_ln1_g, v_ln1_b, v_w_up, v_conv_w, v_conv_b, v_w_down, v_ln2_g, v_ln2_b):
    w = dict(w_ada=w_ada[0], b_ada=b_ada, w_in=w_in[0], b_fgate=b_fgate, gn_a=gn_a, gn_b=gn_b, w_out=w_out[0],
             ln1_g=ln1_g, ln1_b=ln1_b, w_up=w_up[0], conv_w=conv_w[0], conv_b=conv_b, w_down=w_down[0], ln2_g=ln2_g,
             ln2_b=ln2_b)
    m = dict(w_ada=m_w_ada[0], b_ada=m_b_ada, w_in=m_w_in[0], b_fgate=m_b_fgate, gn_a=m_gn_a, gn_b=m_gn_b,
             w_out=m_w_out[0], ln1_g=m_ln1_g, ln1_b=m_ln1_b, w_up=m_w_up[0], conv_w=m_conv_w[0], conv_b=m_conv_b,
             w_down=m_w_down[0], ln2_g=m_ln2_g, ln2_b=m_ln2_b)
    v = dict(w_ada=v_w_ada[0], b_ada=v_b_ada, w_in=v_w_in[0], b_fgate=v_b_fgate, gn_a=v_gn_a, gn_b=v_gn_b,
             w_out=v_w_out[0], ln1_g=v_ln1_g, ln1_b=v_ln1_b, w_up=v_w_up[0], conv_w=v_conv_w[0], conv_b=v_conv_b,
             w_down=v_w_down[0], ln2_g=v_ln2_g, ln2_b=v_ln2_b)

    nbat = x.shape[0]
    me = 4 * lax.axis_index("x") + 2 * lax.axis_index("y") + lax.axis_index("c")
    ada_cols = w["w_ada"].shape[1]

    c_all, w_in_all = _exchange([c, _payload("w_in", w["w_in"])], [True, True], "weight_gather")
    c_all = c_all.reshape(N_DEV * nbat, D_MODEL)
    ada_mine = _ada_fwd(c_all, w["w_ada"], lax.dynamic_slice(b_ada, (0, me * ada_cols), (1, ada_cols)))
    (ada_parts,) = _exchange([ada_mine.reshape(N_DEV, nbat, ada_cols)], [False], "ada_exchange")
    ada3 = ada_parts.transpose(1, 0, 2).reshape(nbat, 6, D_MODEL)

    loss_lanes, grad_x, g_local, parts = _local_step(
        x, positions, loss_target, ada3, _full_from_gathered("w_in", w_in_all), b_fgate, gn_a, gn_b, ln1_g, ln1_b,
        conv_b, ln2_g, ln2_b, {n: _payload(n, w[n]) for n in LATE})

    parts["w_in"], dada_all, small_all = _exchange(
        [_payload("w_in", _dest_major("w_in", g_local["w_in"])), g_local["dada"], _pack_small(g_local, loss_lanes)],
        [False, True, True], "grad_exchange")
    dada_cols = lax.dynamic_slice(dada_all.reshape(N_DEV * nbat, 6 * D_MODEL), (0, me * ada_cols),
                                  (N_DEV * nbat, ada_cols))
    parts["w_ada"] = _ada_bwd(c_all, dada_cols)[None]

    grad, delta, new_m, new_v = {}, {}, {}, {}
    for n in BIG:
        grad[n], delta[n], new_m[n], new_v[n] = (
            a[None] for a in _adamw(parts[n], w[n], m[n], v[n], ADAM_ROWS[n], "adamw_" + n))
    packed = _adamw(small_all, _pack_small(w), _pack_small(m), _pack_small(v), SMALL_ROWS, "adamw_small")
    for dst, pk in zip((grad, delta, new_m, new_v), packed):
        vals, lanes = _unpack_small(pk, w)
        dst.update(vals)
        if dst is grad:
            loss = jnp.sum(lanes)

    order = ("w_ada", "b_ada", "w_in", "b_fgate", "gn_a", "gn_b", "w_out", "ln1_g", "ln1_b", "w_up", "conv_w", "conv_b",
             "w_down", "ln2_g", "ln2_b")
    return (loss, grad_x, *[grad[n] for n in order], *[delta[n] for n in order], *[new_m[n] for n in order],
            *[new_v[n] for n in order])
```

```python
import functools

import numpy as np
import jax
import jax.numpy as jnp
from jax import lax
from jax.experimental import pallas as pl
from jax.experimental.pallas import tpu as pltpu

F32, BF16 = jnp.float32, jnp.bfloat16
HIGHEST = lax.Precision.HIGHEST
MESH = pl.DeviceIdType.MESH
ANY = pl.BlockSpec(memory_space=pl.ANY)

D_MODEL = 1024
N_HEADS = 8
HEAD_DIM = 64
WIDTH = 512
D_FF = 2816
N_DEV = 8
ROPE_DIMS = 16
ROPE_THETA = 500000.0
ALPHA = 2.0 ** 0.25
LN_EPS = 1e-5
RMS_EPS = 1e-6
NEG = -1e30
Q_SCALE = 0.125
BLK = 128
LANES = 128
VMEM_LIMIT_BYTES = 56 * 1024 * 1024

ADAM_LR, ADAM_B1, ADAM_B2, ADAM_EPS, ADAM_WD, ADAM_STEP = 0.001, 0.9, 0.999, 1e-08, 0.01, 10


def _params(vmem=VMEM_LIMIT_BYTES):
    return pltpu.CompilerParams(vmem_limit_bytes=vmem)


def _nn(a, b):
    return jnp.dot(a, b, preferred_element_type=F32)


def _nt(a, b):
    return lax.dot_general(a, b, (((1,), (1,)), ((), ())), preferred_element_type=F32)


def _tn(a, b):
    return lax.dot_general(a, b, (((0,), (0,)), ((), ())), preferred_element_type=F32)


def _head_mats():
    r = lax.broadcasted_iota(jnp.int32, (LANES, WIDTH), 0)
    c = lax.broadcasted_iota(jnp.int32, (LANES, WIDTH), 1)
    e = ((c >> 6) == r).astype(BF16)
    r2 = lax.broadcasted_iota(jnp.int32, (WIDTH, LANES), 0)
    c2 = lax.broadcasted_iota(jnp.int32, (WIDTH, LANES), 1)
    et = ((r2 >> 6) == c2).astype(BF16)
    return e, et


def _split3(x):
    hi = x.astype(BF16)
    r = x - hi.astype(F32)
    mid = r.astype(BF16)
    return hi, mid, (r - mid.astype(F32)).astype(BF16)


def _hexp(w, e):
    return sum(_nn(part, e) for part in _split3(w))


def _hsum(x, et):
    return sum(_nn(part, et) for part in _split3(x))


def _perm_matrix(rows, d, transpose):
    i = np.arange(rows)
    j = (i % (rows // d)) * d + i // (rows // d)
    p = np.zeros((rows, rows), np.float32)
    p[i, j] = 1.0
    return jnp.asarray(p.T if transpose else p, BF16)


def _permute_f32(p, x):
    return sum(_nn(p, part) for part in _split3(x))


def _store_classes(ref, y, d):
    n = y.shape[0] // d
    for r in range(d):
        ref[r] = y[r * n:(r + 1) * n, :]


def _load_classes(ref, d):
    return jnp.concatenate([ref[r] for r in range(d)], axis=0)


def _rope_tabs(pos_ref, fr_ref, sign):
    ang = pos_ref[...].astype(F32) * fr_ref[...]
    lane = lax.broadcasted_iota(jnp.int32, ang.shape, 1) & (HEAD_DIM - 1)
    m1 = lane < ROPE_DIMS // 2
    m2 = (lane >= ROPE_DIMS // 2) & (lane < ROPE_DIMS)
    cos = jnp.cos(ang)
    sin = jnp.sin(ang) * sign
    return (jnp.where(m1 | m2, cos, 1.0), jnp.where(m1, -sin, 0.0), jnp.where(m2, sin, 0.0))


def _rope(z, tabs):
    c, s1, s2 = tabs
    parts = []
    for p in range(z.shape[1] // LANES):
        zp = z[:, LANES * p:LANES * (p + 1)]
        parts.append(zp * c + pltpu.roll(zp, LANES - 8, 1) * s1 + pltpu.roll(zp, 8, 1) * s2)
    return jnp.concatenate(parts, axis=1)


def _half_masks(rows):
    lane = lax.broadcasted_iota(jnp.int32, (rows, LANES), 1)
    lo = lane < HEAD_DIM
    return lo, jnp.logical_not(lo)


def _layer_norm_bwd(dxh, xh, rstd):
    m1 = jnp.mean(dxh, axis=1, keepdims=True)
    m2 = jnp.mean(dxh * xh, axis=1, keepdims=True)
    return rstd * (dxh - m1 - xh * m2)


def _coords():
    return lax.axis_index("x"), lax.axis_index("y"), lax.axis_index("c")


def _peer(x, y, c, k):
    return (1 - x if k & 4 else x, 1 - y if k & 2 else y, 1 - c if k & 1 else c)


def _comm_sems(n):
    return [pltpu.SemaphoreType.DMA((N_DEV - 1, n)), pltpu.SemaphoreType.DMA((N_DEV - 1, n)),
            pltpu.SemaphoreType.DMA((n,))]


def _comm_copies(ins, outs, to_all, sems):
    send_sems, recv_sems, local_sems = sems
    x, y, c = _coords()
    me = 4 * x + 2 * y + c
    copies = [pltpu.make_async_copy(ins[t] if to_all[t] else ins[t].at[me], outs[t].at[me], local_sems.at[t])
              for t in range(len(ins))]
    for k in range(1, N_DEV):
        px, py, pc = _peer(x, y, c, k)
        dest = 4 * px + 2 * py + pc
        for t in range(len(ins)):
            copies.append(pltpu.make_async_remote_copy(
                src_ref=ins[t] if to_all[t] else ins[t].at[dest], dst_ref=outs[t].at[me],
                send_sem=send_sems.at[k - 1, t], recv_sem=recv_sems.at[k - 1, t],
                device_id=(px, py, pc), device_id_type=MESH))
    return copies


def _comm_out_shapes(ins, to_all):
    return [jax.ShapeDtypeStruct(((N_DEV,) + a.shape) if ta else a.shape, a.dtype) for a, ta in zip(ins, to_all)]


def _exchange(ins, to_all, name):
    n = len(ins)

    def body(*refs):
        copies = _comm_copies(refs[:n], refs[n:2 * n], to_all, refs[2 * n:])
        for cp in copies:
            cp.start()
        for cp in copies:
            cp.wait()

    return pl.pallas_call(
        body, name=name, out_shape=_comm_out_shapes(ins, to_all), in_specs=[ANY] * n, out_specs=[ANY] * n,
        scratch_shapes=_comm_sems(n),
    )(*ins)


def _adamw(parts, w, m, v, rows, name):
    n_parts, r_all, cols = parts.shape
    c1 = 1.0 - ADAM_B1 ** ADAM_STEP
    c2 = 1.0 - ADAM_B2 ** ADAM_STEP

    def body(p_ref, w_ref, m_ref, v_ref, g_ref, d_ref, mo_ref, vo_ref):
        g = p_ref[0].astype(F32)
        for s in range(1, n_parts):
            g = g + p_ref[s].astype(F32)
        mn = ADAM_B1 * m_ref[...] + (1.0 - ADAM_B1) * g
        vn = ADAM_B2 * v_ref[...] + (1.0 - ADAM_B2) * (g * g)
        m_hat = mn / c1
        v_hat = vn / c2
        g_ref[...] = g
        d_ref[...] = -ADAM_LR * (m_hat / (jnp.sqrt(v_hat) + ADAM_EPS) + ADAM_WD * w_ref[...])
        mo_ref[...] = mn
        vo_ref[...] = vn

    spec = pl.BlockSpec((rows, cols), lambda i: (i, 0))
    return pl.pallas_call(
        body, name=name, grid=(r_all // rows,),
        in_specs=[pl.BlockSpec((n_parts, rows, cols), lambda i: (0, i, 0)), spec, spec, spec],
        out_specs=[spec] * 4, out_shape=[jax.ShapeDtypeStruct((r_all, cols), F32)] * 4,
        compiler_params=_params(),
    )(parts, w, m, v)


def _matmul(a, w, transposed_w, out_dtype, tm, chunk, name):
    t_all, k = a.shape
    n = w.shape[0] if transposed_w else w.shape[1]

    def body(a_ref, w_ref, o_ref):
        av = a_ref[...]
        for j in range(n // chunk):
            cs = slice(j * chunk, (j + 1) * chunk)
            r = _nt(av, w_ref[cs, :]) if transposed_w else _nn(av, w_ref[:, cs])
            o_ref[:, cs] = r.astype(out_dtype)

    return pl.pallas_call(
        body, name=name, grid=(t_all // tm,),
        in_specs=[pl.BlockSpec((tm, k), lambda i: (i, 0)), pl.BlockSpec(w.shape, lambda i: (0, 0))],
        out_specs=pl.BlockSpec((tm, n), lambda i: (i, 0)),
        out_shape=jax.ShapeDtypeStruct((t_all, n), out_dtype), compiler_params=_params(),
    )(a, w)


def _matmul_tn(a, b, tn, tk, name):
    t_all, k1 = a.shape
    n = b.shape[1]

    def body(a_ref, b_ref, o_ref):
        @pl.when(pl.program_id(1) == 0)
        def _():
            o_ref[...] = jnp.zeros_like(o_ref)
        o_ref[...] += _tn(a_ref[...], b_ref[...])

    return pl.pallas_call(
        body, name=name, grid=(n // tn, t_all // tk),
        in_specs=[pl.BlockSpec((tk, k1), lambda j, t: (t, 0)), pl.BlockSpec((tk, tn), lambda j, t: (t, j))],
        out_specs=pl.BlockSpec((k1, tn), lambda j, t: (0, j)),
        out_shape=jax.ShapeDtypeStruct((k1, n), F32), compiler_params=_params(),
    )(a, b)


def _matmul_rows(a, b, tk, name):
    r, t_all = a.shape
    n = b.shape[1]

    def body(a_ref, b_ref, o_ref):
        @pl.when(pl.program_id(0) == 0)
        def _():
            o_ref[...] = jnp.zeros_like(o_ref)
        o_ref[...] += _nn(a_ref[...], b_ref[...])

    return pl.pallas_call(
        body, name=name, grid=(t_all // tk,),
        in_specs=[pl.BlockSpec((r, tk), lambda t: (0, t)), pl.BlockSpec((tk, n), lambda t: (t, 0))],
        out_specs=pl.BlockSpec((r, n), lambda t: (0, 0)),
        out_shape=jax.ShapeDtypeStruct((r, n), F32), compiler_params=_params(),
    )(a, b)


def _ada_fwd(c_all, w_ada, b_ada):
    whole = lambda a: pl.BlockSpec(a.shape, lambda j: (0, 0))

    def body(c_ref, w_ref, b_ref, o_ref):
        cv = c_ref[...]
        s = (cv * jax.nn.sigmoid(cv)).astype(BF16)
        o_ref[...] = _nn(s, w_ref[...].astype(BF16)) + b_ref[...]

    out = jax.ShapeDtypeStruct((c_all.shape[0], w_ada.shape[1]), F32)
    return pl.pallas_call(
        body, name="ada_fwd", grid=(1,), in_specs=[whole(c_all), whole(w_ada), whole(b_ada)], out_specs=whole(out),
        out_shape=out, compiler_params=_params(),
    )(c_all, w_ada, b_ada)


def _ada_bwd(c_all, dada):
    whole = lambda a: pl.BlockSpec(a.shape, lambda j: (0, 0))

    def body(c_ref, d_ref, o_ref):
        cv = c_ref[...]
        s = (cv * jax.nn.sigmoid(cv)).astype(BF16)
        o_ref[...] = _tn(s, d_ref[...].astype(BF16))

    out = jax.ShapeDtypeStruct((D_MODEL, dada.shape[1]), F32)
    return pl.pallas_call(
        body, name="ada_bwd", grid=(1,), in_specs=[whole(c_all), whole(dada)], out_specs=whole(out), out_shape=out,
        compiler_params=_params(),
    )(c_all, dada)


TOK_TM = 256
DILATIONS = (1, 4, 16)


def _class_spec(d, width, nts):
    return pl.BlockSpec((d, TOK_TM // d, width), lambda i: (i // nts, i % nts, 0))


def _class_shape(t_all, seq, d, width, dtype):
    return jax.ShapeDtypeStruct((t_all // seq * d, seq // d, width), dtype)


def _inproj(x, ada3, pos, wqkv, wf16, freq, perms, seq):
    t_all = x.shape[0]
    tm = TOK_TM
    nts = seq // tm

    def body(x_ref, ada_ref, pos_ref, w_ref, wf_ref, fr_ref, p4_ref, p16_ref, h1_ref, za_ref, zb_ref, zb4_ref,
             zb16_ref, vt_ref, fa_ref):
        h1 = (x_ref[...] * (1.0 + ada_ref[0, 1:2, :]) + ada_ref[0, 0:1, :]).astype(BF16)
        h1_ref[...] = h1
        tabs = _rope_tabs(pos_ref, fr_ref, 1.0)
        for n in range(6):
            z = _nn(h1, w_ref[:, n * WIDTH:(n + 1) * WIDTH])
            if n in (3, 4):
                z = _rope(z, tabs)
            if n in (0, 3):
                z = z * Q_SCALE
            if n == 2:
                vt_ref[...] = z.T.astype(BF16)
            dst = za_ref if n < 3 else zb_ref
            dst[:, (n % 3) * WIDTH:(n % 3 + 1) * WIDTH] = z.astype(BF16)
        fa_ref[...] = _nt(wf_ref[...], h1)[:N_HEADS]
        zb = zb_ref[...]
        _store_classes(zb4_ref, _nn(p4_ref[...], zb).astype(BF16), 4)
        _store_classes(zb16_ref, _nn(p16_ref[...], zb).astype(BF16), 16)

    tok = lambda w: pl.BlockSpec((tm, w), lambda i: (i, 0))
    whole = lambda a: pl.BlockSpec(a.shape, lambda i: (0, 0))
    return pl.pallas_call(
        body, name="inproj", grid=(t_all // tm,),
        in_specs=[tok(D_MODEL), pl.BlockSpec((1, 6, D_MODEL), lambda i: (i // nts, 0, 0)), tok(1), whole(wqkv),
                  whole(wf16), pl.BlockSpec((1, LANES), lambda i: (0, 0)), whole(perms[0]), whole(perms[1])],
        out_specs=[tok(D_MODEL), tok(3 * WIDTH), tok(3 * WIDTH), _class_spec(4, 3 * WIDTH, nts),
                   _class_spec(16, 3 * WIDTH, nts), pl.BlockSpec((WIDTH, tm), lambda i: (i // nts, i % nts)),
                   pl.BlockSpec((N_HEADS, tm), lambda i: (0, i))],
        out_shape=[jax.ShapeDtypeStruct((t_all, D_MODEL), BF16), jax.ShapeDtypeStruct((t_all, 3 * WIDTH), BF16),
                   jax.ShapeDtypeStruct((t_all, 3 * WIDTH), BF16), _class_shape(t_all, seq, 4, 3 * WIDTH, BF16),
                   _class_shape(t_all, seq, 16, 3 * WIDTH, BF16),
                   jax.ShapeDtypeStruct((t_all // seq * WIDTH, seq), BF16),
                   jax.ShapeDtypeStruct((N_HEADS, t_all), F32)],
        compiler_params=_params(),
    )(x, ada3, pos, wqkv, wf16, freq, perms[0], perms[1])


def _fgate_fwd(fa_t, bf, seq):
    t_all = fa_t.shape[1]

    def body(fa_ref, b_ref, f_ref):
        lane = lax.broadcasted_iota(jnp.int32, (N_HEADS, LANES), 1)

        def chunk(j, carry):
            sl = pl.ds(pl.multiple_of(j * LANES, LANES), LANES)
            xv = fa_ref[:, sl] + b_ref[...]
            lf = jnp.minimum(xv, 0.0) - jnp.log(1.0 + jnp.exp(-jnp.abs(xv)))
            for s in (1, 2, 4, 8, 16, 32, 64):
                lf = lf + jnp.where(lane >= s, pltpu.roll(lf, s, 1), 0.0)
            lf = lf + carry
            f_ref[:, sl] = lf
            return lf[:, LANES - 1:LANES]

        lax.fori_loop(0, seq // LANES, chunk, jnp.zeros((N_HEADS, 1), F32))

    return pl.pallas_call(
        body, name="fgate_fwd", grid=(t_all // seq,),
        in_specs=[pl.BlockSpec((N_HEADS, seq), lambda b: (0, b)), pl.BlockSpec((N_HEADS, 1), lambda b: (0, 0))],
        out_specs=pl.BlockSpec((N_HEADS, seq), lambda b: (0, b)),
        out_shape=jax.ShapeDtypeStruct((N_HEADS, t_all), F32), compiler_params=_params(),
    )(fa_t, bf)


def _fgate_bwd(df_t, fa_t, bf, seq):
    t_all = fa_t.shape[1]

    def body(df_ref, fa_ref, b_ref, o_ref, s_ref):
        lane = lax.broadcasted_iota(jnp.int32, (N_HEADS, LANES), 1)

        @pl.when(pl.program_id(0) == 0)
        def _():
            s_ref[...] = jnp.zeros_like(s_ref)

        def chunk(jj, carry):
            car, tot = carry
            j = seq // LANES - 1 - jj
            sl = pl.ds(pl.multiple_of(j * LANES, LANES), LANES)
            d = df_ref[:, sl]
            for s in (1, 2, 4, 8, 16, 32, 64):
                d = d + jnp.where(lane < LANES - s, pltpu.roll(d, LANES - s, 1), 0.0)
            d = d + car
            dfa = d * jax.nn.sigmoid(-(fa_ref[:, sl] + b_ref[...]))
            o_ref[:, sl] = dfa
            return d[:, 0:1], tot + jnp.sum(dfa, axis=1, keepdims=True)

        z = jnp.zeros((N_HEADS, 1), F32)
        _, tot = lax.fori_loop(0, seq // LANES, chunk, (z, z))
        s_ref[...] += jnp.broadcast_to(tot, (N_HEADS, LANES))

    row = pl.BlockSpec((N_HEADS, seq), lambda b: (0, b))
    return pl.pallas_call(
        body, name="fgate_bwd", grid=(t_all // seq,),
        in_specs=[row, row, pl.BlockSpec((N_HEADS, 1), lambda b: (0, 0))],
        out_specs=[row, pl.BlockSpec((N_HEADS, LANES), lambda b: (0, 0))],
        out_shape=[jax.ShapeDtypeStruct((N_HEADS, t_all), F32), jax.ShapeDtypeStruct((N_HEADS, LANES), F32)],
        compiler_params=_params(),
    )(df_t, fa_t, bf)


FOX_T = 256


def _fox_prep(dst, src_ref, lo, hi):
    for p in range(4):
        v = src_ref[:, LANES * p:LANES * (p + 1)]
        dst[2 * p] = jnp.where(lo, v, jnp.zeros_like(v))
        dst[2 * p + 1] = jnp.where(hi, v, jnp.zeros_like(v))


def _fox_fwd(za, vt, f_col, seq, shards):
    t_all = za.shape[0]
    tq = FOX_T
    nq = seq // tq
    nbat = t_all // seq
    n = len(shards)
    to_all = [True] * n

    def body(*refs):
        q_ref, k_ref, vt_ref, fc_ref = refs[:4]
        o_ref, lse_ref = refs[4 + n:6 + n]
        qm_sc, m_sc, l_sc, acc_sc, a_sc, st_sc, pe_sc = refs[6 + 2 * n:13 + 2 * n]
        comm = (refs[4:4 + n], refs[6 + n:6 + 2 * n], to_all, refs[13 + 2 * n:])
        i = pl.program_id(1)

        @pl.when((pl.program_id(0) == 0) & (i == 0))
        def _():
            for cp in _comm_copies(*comm):
                cp.start()
        lo, hi = _half_masks(tq)
        r = lax.broadcasted_iota(jnp.int32, (tq, tq), 0)
        c = lax.broadcasted_iota(jnp.int32, (tq, tq), 1)
        tri = c >= r
        _fox_prep(qm_sc, q_ref, lo, hi)
        m_sc[...] = jnp.full(m_sc.shape, NEG, F32)
        l_sc[...] = jnp.zeros_like(l_sc)
        acc_sc[...] = jnp.zeros_like(acc_sc)

        def block(j, masked):
            sl = pl.ds(pl.multiple_of(j * tq, tq), tq)
            for p in range(4):
                kj = k_ref[sl, LANES * p:LANES * (p + 1)]
                for h in (2 * p, 2 * p + 1):
                    st = _nt(kj, qm_sc[h]) - fc_ref[sl, h:h + 1]
                    st_sc[h] = jnp.where(tri, st, NEG) if masked else st
            for h in range(N_HEADS):
                st = st_sc[h]
                m = m_sc[h:h + 1, :]
                mn = jnp.maximum(m, jnp.max(st, axis=0, keepdims=True))
                a = jnp.exp(m - mn)
                pe = jnp.exp(st - mn)
                m_sc[h:h + 1, :] = mn
                a_sc[h:h + 1, :] = a
                l_sc[h:h + 1, :] = a * l_sc[h:h + 1, :] + jnp.sum(pe, axis=0, keepdims=True)
                pe_sc[h] = pe.astype(BF16)
            for h in range(N_HEADS):
                acc_sc[h] = a_sc[h:h + 1, :] * acc_sc[h] + _nn(vt_ref[HEAD_DIM * h:HEAD_DIM * (h + 1), sl], pe_sc[h])

        def step(j, carry):
            block(j, False)
            return carry

        lax.fori_loop(0, i, step, 0)
        block(i, True)
        lse_ref[...] = m_sc[...] + jnp.log(l_sc[...])
        for p in range(4):
            ot = jnp.concatenate([acc_sc[h] / l_sc[h:h + 1, :] for h in (2 * p, 2 * p + 1)], axis=0)
            o_ref[:, LANES * p:LANES * (p + 1)] = ot.T

        @pl.when((pl.program_id(0) == nbat - 1) & (i == nq - 1))
        def _():
            for cp in _comm_copies(*comm):
                cp.wait()

    res = pl.pallas_call(
        body, name="fox_fwd", grid=(nbat, nq),
        in_specs=[pl.BlockSpec((tq, WIDTH), lambda b, i: (b * nq + i, 0)),
                  pl.BlockSpec((seq, WIDTH), lambda b, i: (b, 1)), pl.BlockSpec((WIDTH, seq), lambda b, i: (b, 0)),
                  pl.BlockSpec((seq, LANES), lambda b, i: (b, 0))] + [ANY] * n,
        out_specs=[pl.BlockSpec((tq, WIDTH), lambda b, i: (b * nq + i, 0)),
                   pl.BlockSpec((N_HEADS, tq), lambda b, i: (0, b * nq + i))] + [ANY] * n,
        out_shape=[jax.ShapeDtypeStruct((t_all, WIDTH), F32), jax.ShapeDtypeStruct((N_HEADS, t_all), F32)]
        + _comm_out_shapes(shards, to_all),
        scratch_shapes=[pltpu.VMEM((N_HEADS, tq, LANES), BF16), pltpu.VMEM((N_HEADS, tq), F32),
                        pltpu.VMEM((N_HEADS, tq), F32), pltpu.VMEM((N_HEADS, HEAD_DIM, tq), F32),
                        pltpu.VMEM((N_HEADS, tq), F32), pltpu.VMEM((N_HEADS, tq, tq), F32),
                        pltpu.VMEM((N_HEADS, tq, tq), BF16)] + _comm_sems(n),
        compiler_params=_params(),
    )(za, za, vt, f_col, *shards)
    return res[0], res[1], res[2:]


def _fox_bwd(za, do, f_col, lse_row, dl_row, seq, grads):
    t_all = za.shape[0]
    tk = FOX_T
    nk = seq // tk
    nbat = t_all // seq
    n = len(grads)
    to_all = [False] * n

    def body(*refs):
        k_ref, v_ref, q_ref, do_ref, fc_ref, lr_ref, dr_ref = refs[:7]
        dk_ref, dv_ref, df_ref, dqt_ref, dfq_ref = refs[7 + n:12 + n]
        km_sc, vm_sc, fk_sc, dk_sc, dv_sc, cs_sc, kt_sc, st_sc, dp_sc, pt_sc, ds_sc = refs[12 + 2 * n:23 + 2 * n]
        comm = (refs[7:7 + n], refs[12 + n:12 + 2 * n], to_all, refs[23 + 2 * n:])
        j = pl.program_id(1)

        @pl.when(j == 0)
        def _():
            dqt_ref[...] = jnp.zeros_like(dqt_ref)
            dfq_ref[...] = jnp.zeros_like(dfq_ref)

        @pl.when((pl.program_id(0) == 0) & (j == 0))
        def _():
            for cp in _comm_copies(*comm):
                cp.start()
        lo, hi = _half_masks(tk)
        r = lax.broadcasted_iota(jnp.int32, (tk, tk), 0)
        c = lax.broadcasted_iota(jnp.int32, (tk, tk), 1)
        tri = c >= r
        _fox_prep(km_sc, k_ref, lo, hi)
        _fox_prep(vm_sc, v_ref, lo, hi)
        for h in range(N_HEADS):
            fk_sc[h] = jnp.broadcast_to(fc_ref[:, h:h + 1], (tk, tk))
        for p in range(4):
            kt_sc[p] = k_ref[:, LANES * p:LANES * (p + 1)].astype(F32).T.astype(BF16)
        dk_sc[...] = jnp.zeros_like(dk_sc)
        dv_sc[...] = jnp.zeros_like(dv_sc)
        cs_sc[...] = jnp.zeros_like(cs_sc)

        def block(i, masked):
            sl = pl.ds(pl.multiple_of(i * tk, tk), tk)
            for p in range(4):
                cs = slice(LANES * p, LANES * (p + 1))
                qi = q_ref[sl, cs]
                doi = do_ref[sl, cs]
                for h in (2 * p, 2 * p + 1):
                    st = _nt(km_sc[h], qi) - fk_sc[h] - lr_ref[h:h + 1, sl]
                    st_sc[h] = jnp.where(tri, st, NEG) if masked else st
                    dp_sc[h] = _nt(vm_sc[h], doi) - dr_ref[h:h + 1, sl]
            for h in range(N_HEADS):
                pt = jnp.exp(st_sc[h])
                dst = pt * dp_sc[h]
                pt_sc[h] = pt.astype(BF16)
                ds_sc[h] = dst.astype(BF16)
                cs_sc[h] += dst[:, :LANES] + dst[:, LANES:]
                dfq_ref[h:h + 1, sl] += jnp.sum(dst, axis=0, keepdims=True)
            for p in range(4):
                cs = slice(LANES * p, LANES * (p + 1))
                qi = q_ref[sl, cs]
                doi = do_ref[sl, cs]
                for h in (2 * p, 2 * p + 1):
                    dv_sc[h] += _nn(pt_sc[h], doi)
                    dk_sc[h] += _nn(ds_sc[h], qi)
                    kt = kt_sc[p, HEAD_DIM * (h % 2):HEAD_DIM * (h % 2 + 1), :]
                    dqt_ref[HEAD_DIM * h:HEAD_DIM * (h + 1), sl] += _nn(kt, ds_sc[h])

        def step(i, carry):
            block(i, False)
            return carry

        block(j, True)
        lax.fori_loop(j + 1, nk, step, 0)
        df_ref[...] = jnp.zeros_like(df_ref)
        for p in range(4):
            cs = slice(LANES * p, LANES * (p + 1))
            dk_ref[:, cs] = jnp.where(lo, dk_sc[2 * p], dk_sc[2 * p + 1]).astype(BF16)
            dv_ref[:, cs] = jnp.where(lo, dv_sc[2 * p], dv_sc[2 * p + 1]).astype(BF16)
            for h in (2 * p, 2 * p + 1):
                df_ref[:, h:h + 1] = -jnp.sum(cs_sc[h], axis=1, keepdims=True)

        @pl.when(j == nk - 1)
        def _():
            dqt_ref[...] = dqt_ref[...] * Q_SCALE

        @pl.when((pl.program_id(0) == nbat - 1) & (j == nk - 1))
        def _():
            for cp in _comm_copies(*comm):
                cp.wait()

    tile = lambda w, col: pl.BlockSpec((tk, w), lambda b, j: (b * nk + j, col))
    full = lambda col: pl.BlockSpec((seq, WIDTH), lambda b, j: (b, col))
    row = pl.BlockSpec((N_HEADS, seq), lambda b, j: (0, b))
    acc = pltpu.VMEM((N_HEADS, tk, LANES), F32)
    res = pl.pallas_call(
        body, name="fox_bwd", grid=(nbat, nk),
        in_specs=[tile(WIDTH, 1), tile(WIDTH, 2), full(0), full(0), tile(LANES, 0), row, row] + [ANY] * n,
        out_specs=[tile(WIDTH, 0), tile(WIDTH, 0), tile(LANES, 0), pl.BlockSpec((WIDTH, seq), lambda b, j: (b, 0)),
                   row] + [ANY] * n,
        out_shape=[jax.ShapeDtypeStruct((t_all, WIDTH), BF16), jax.ShapeDtypeStruct((t_all, WIDTH), BF16),
                   jax.ShapeDtypeStruct((t_all, LANES), F32), jax.ShapeDtypeStruct((nbat * WIDTH, seq), F32),
                   jax.ShapeDtypeStruct((N_HEADS, t_all), F32)] + _comm_out_shapes(grads, to_all),
        scratch_shapes=[pltpu.VMEM((N_HEADS, tk, LANES), BF16), pltpu.VMEM((N_HEADS, tk, LANES), BF16),
                        pltpu.VMEM((N_HEADS, tk, tk), F32), acc, acc, acc, pltpu.VMEM((4, LANES, tk), BF16),
                        pltpu.VMEM((N_HEADS, tk, tk), F32), pltpu.VMEM((N_HEADS, tk, tk), F32),
                        pltpu.VMEM((N_HEADS, tk, tk), BF16), pltpu.VMEM((N_HEADS, tk, tk), BF16)]
        + _comm_sems(n),
        compiler_params=_params(),
    )(za, za, za, do, f_col, lse_row, dl_row, *grads)
    return res[0], res[1], res[2], res[3], res[4], res[5:]


def _dil_mask(has_prev):
    qi = lax.broadcasted_iota(jnp.int32, (BLK, 2 * BLK), 0)
    kj = lax.broadcasted_iota(jnp.int32, (BLK, 2 * BLK), 1)
    dist = qi + BLK - kj
    return (dist >= 0) & (dist <= BLK) & ((kj >= BLK) | has_prev)


def _dil_specs(t_all):
    nb = t_all // BLK
    cur = lambda col: pl.BlockSpec((BLK, WIDTH), lambda n: (n, col))
    prev = lambda col: pl.BlockSpec((BLK, WIDTH), lambda n: (jnp.maximum(n - 1, 0), col))
    nxt = lambda col: pl.BlockSpec((BLK, WIDTH), lambda n: (jnp.minimum(n + 1, nb - 1), col))
    stat = pl.BlockSpec((BLK, LANES), lambda n: (n, 0))
    stat_nxt = pl.BlockSpec((BLK, LANES), lambda n: (jnp.minimum(n + 1, nb - 1), 0))
    return nb, cur, prev, nxt, stat, stat_nxt


def _dil_fwd(zb, seq, d):
    t_all = zb.shape[0]
    nbs = seq // d // BLK
    nb, cur, prev, _, stat, _ = _dil_specs(t_all)

    def body(q_ref, kp_ref, kc_ref, vp_ref, vc_ref, o_ref, lse_ref, s_sc, p_sc):
        mask = _dil_mask((pl.program_id(0) & (nbs - 1)) != 0)
        lo, hi = _half_masks(BLK)
        lse_ref[...] = jnp.zeros_like(lse_ref)
        for p in range(4):
            cs = slice(LANES * p, LANES * (p + 1))
            qp = q_ref[:,cs]
            kcat = jnp.concatenate([kp_ref[:,cs], kc_ref[:,cs]], axis=0)
            for e in (0, 1):
                qe = jnp.where(lo if e == 0 else hi, qp, jnp.zeros_like(qp))
                s_sc[2 * p + e] = jnp.where(mask, _nt(qe, kcat), NEG)
        inv = []
        for h in range(N_HEADS):
            s = s_sc[h]
            m = jnp.max(s, axis=1, keepdims=True)
            pe = jnp.exp(s - m)
            l = jnp.sum(pe, axis=1, keepdims=True)
            p_sc[h] = pe.astype(BF16)
            inv.append(1.0 / l)
            lse_ref[:,h:h + 1] = m + jnp.log(l)
        for p in range(4):
            cs = slice(LANES * p, LANES * (p + 1))
            vcat = jnp.concatenate([vp_ref[:,cs], vc_ref[:,cs]], axis=0)
            res = [_nn(p_sc[h], vcat) * inv[h] for h in (2 * p, 2 * p + 1)]
            o_ref[:,cs] = jnp.where(lo, res[0], res[1])

    return pl.pallas_call(
        body, name=f"dil_fwd_{d}", grid=(nb,), in_specs=[cur(0), prev(1), cur(1), prev(2), cur(2)],
        out_specs=[cur(0), stat],
        out_shape=[jax.ShapeDtypeStruct((t_all, WIDTH), F32), jax.ShapeDtypeStruct((t_all, LANES), F32)],
        scratch_shapes=[pltpu.VMEM((N_HEADS, BLK, 2 * BLK), F32), pltpu.VMEM((N_HEADS, BLK, 2 * BLK), BF16)],
        compiler_params=_params(),
    )(zb, zb, zb, zb, zb)


def _dil_bwd_dq(zb, do, lse, dl, seq, d):
    t_all = zb.shape[0]
    nbs = seq // d // BLK
    nb, cur, prev, _, stat, _ = _dil_specs(t_all)

    def body(q_ref, kp_ref, kc_ref, vp_ref, vc_ref, do_ref, lse_ref, dl_ref, dq_ref, s_sc, dp_sc, ds_sc):
        mask = _dil_mask((pl.program_id(0) & (nbs - 1)) != 0)
        lo, hi = _half_masks(BLK)
        for p in range(4):
            cs = slice(LANES * p, LANES * (p + 1))
            qp = q_ref[:,cs]
            dop = do_ref[:,cs]
            kcat = jnp.concatenate([kp_ref[:,cs], kc_ref[:,cs]], axis=0)
            vcat = jnp.concatenate([vp_ref[:,cs], vc_ref[:,cs]], axis=0)
            for e in (0, 1):
                h = 2 * p + e
                sel = lo if e == 0 else hi
                qe = jnp.where(sel, qp, jnp.zeros_like(qp))
                doe = jnp.where(sel, dop, jnp.zeros_like(dop))
                s_sc[h] = jnp.where(mask, _nt(qe, kcat) - lse_ref[:,h:h + 1], NEG)
                dp_sc[h] = _nt(doe, vcat) - dl_ref[:,h:h + 1]
        for h in range(N_HEADS):
            ds_sc[h] = (jnp.exp(s_sc[h]) * dp_sc[h]).astype(BF16)
        for p in range(4):
            cs = slice(LANES * p, LANES * (p + 1))
            kcat = jnp.concatenate([kp_ref[:,cs], kc_ref[:,cs]], axis=0)
            dq_ref[:,cs] = (jnp.where(lo, _nn(ds_sc[2 * p], kcat), _nn(ds_sc[2 * p + 1], kcat))
                             * Q_SCALE).astype(BF16)

    wide = pltpu.VMEM((N_HEADS, BLK, 2 * BLK), F32)
    return pl.pallas_call(
        body, name=f"dil_bwd_dq_{d}", grid=(nb,),
        in_specs=[cur(0), prev(1), cur(1), prev(2), cur(2), pl.BlockSpec((BLK, WIDTH), lambda n: (n, 0)), stat, stat],
        out_specs=pl.BlockSpec((BLK, WIDTH), lambda n: (n, 0)),
        out_shape=jax.ShapeDtypeStruct((t_all, WIDTH), BF16),
        scratch_shapes=[wide, wide, pltpu.VMEM((N_HEADS, BLK, 2 * BLK), BF16)], compiler_params=_params(),
    )(zb, zb, zb, zb, zb, do, lse, dl)


def _dil_bwd_dkv(zb, do, lse, dl, seq, d):
    t_all = zb.shape[0]
    nbs = seq // d // BLK
    nb, cur, _, nxt, stat, stat_nxt = _dil_specs(t_all)

    def body(k_ref, v_ref, qc_ref, qn_ref, dc_ref, dn_ref, lc_ref, ln_ref, ec_ref, en_ref, dk_ref, dv_ref, s_sc, dp_sc,
             pt_sc, ds_sc):
        has_next = ((pl.program_id(0) + 1) & (nbs - 1)) != 0
        r = lax.broadcasted_iota(jnp.int32, (BLK, 2 * BLK), 0)
        c = lax.broadcasted_iota(jnp.int32, (BLK, 2 * BLK), 1)
        mask = ((c < BLK) & (c >= r)) | ((c >= BLK) & (c - BLK <= r) & has_next)
        lo, hi = _half_masks(BLK)
        lrows = jnp.concatenate([lc_ref[...].T, ln_ref[...].T], axis=1)
        erows = jnp.concatenate([ec_ref[...].T, en_ref[...].T], axis=1)
        for p in range(4):
            cs = slice(LANES * p, LANES * (p + 1))
            kp = k_ref[:,cs]
            vp = v_ref[:,cs]
            qcat = jnp.concatenate([qc_ref[:,cs], qn_ref[:,cs]], axis=0)
            dcat = jnp.concatenate([dc_ref[:,cs], dn_ref[:,cs]], axis=0)
            for e in (0, 1):
                h = 2 * p + e
                sel = lo if e == 0 else hi
                ke = jnp.where(sel, kp, jnp.zeros_like(kp))
                ve = jnp.where(sel, vp, jnp.zeros_like(vp))
                lrow = lrows[h:h + 1, :]
                erow = erows[h:h + 1, :]
                s_sc[h] = jnp.where(mask, _nt(ke, qcat) - lrow, NEG)
                dp_sc[h] = _nt(ve, dcat) - erow
        for h in range(N_HEADS):
            pt = jnp.exp(s_sc[h])
            pt_sc[h] = pt.astype(BF16)
            ds_sc[h] = (pt * dp_sc[h]).astype(BF16)
        for p in range(4):
            cs = slice(LANES * p, LANES * (p + 1))
            qcat = jnp.concatenate([qc_ref[:,cs], qn_ref[:,cs]], axis=0)
            dcat = jnp.concatenate([dc_ref[:,cs], dn_ref[:,cs]], axis=0)
            dk_ref[:,cs] = jnp.where(lo, _nn(ds_sc[2 * p], qcat), _nn(ds_sc[2 * p + 1], qcat)).astype(BF16)
            dv_ref[:,cs] = jnp.where(lo, _nn(pt_sc[2 * p], dcat), _nn(pt_sc[2 * p + 1], dcat)).astype(BF16)

    wide = pltpu.VMEM((N_HEADS, BLK, 2 * BLK), F32)
    half = pltpu.VMEM((N_HEADS, BLK, 2 * BLK), BF16)
    rows = pl.BlockSpec((BLK, WIDTH), lambda n: (n, 0))
    rows_nxt = pl.BlockSpec((BLK, WIDTH), lambda n: (jnp.minimum(n + 1, nb - 1), 0))
    return pl.pallas_call(
        body, name=f"dil_bwd_dkv_{d}", grid=(nb,),
        in_specs=[cur(1), cur(2), cur(0), nxt(0), rows, rows_nxt, stat, stat_nxt, stat, stat_nxt],
        out_specs=[rows, rows], out_shape=[jax.ShapeDtypeStruct((t_all, WIDTH), BF16)] * 2,
        scratch_shapes=[wide, wide, half, half], compiler_params=_params(),
    )(zb, zb, zb, zb, do, do, lse, lse, dl, dl)


def _mix_out(oa, o3, l3, gn_a, gn_b, w_out, x, ada3, ln_g, ln_b, perms, seq):
    t_all = x.shape[0]
    tm = TOK_TM
    nts = seq // tm

    def body(oa_ref, o1_ref, o2_ref, o3_ref, l1_ref, l2_ref, l3_ref, ga_ref, gb_ref, w_ref, x_ref, ada_ref, g_ref,
             b_ref, p4_ref, p16_ref, pt4_ref, pt16_ref, ob_ref, lse_ref, lse4_ref, lse16_ref, mg_ref, mix_ref, xh_ref,
             rs_ref, h2_ref):
        e, et = _head_mats()
        la = l1_ref[...]
        lb = _permute_f32(pt4_ref[...], _load_classes(l2_ref, 4))
        lc = _permute_f32(pt16_ref[...], _load_classes(l3_ref, 16))
        mx = jnp.maximum(jnp.maximum(la, lb), lc)
        ea, eb, ec = jnp.exp(la - mx), jnp.exp(lb - mx), jnp.exp(lc - mx)
        tot = ea + eb + ec
        lse = mx + jnp.log(tot)
        lse_ref[...] = lse
        _store_classes(lse4_ref, _permute_f32(p4_ref[...], lse), 4)
        _store_classes(lse16_ref, _permute_f32(p16_ref[...], lse), 16)
        ob = (o1_ref[...] * _hexp(ea / tot, e)
              + _permute_f32(pt4_ref[...], _load_classes(o2_ref, 4)) * _hexp(eb / tot, e)
              + _permute_f32(pt16_ref[...], _load_classes(o3_ref, 16)) * _hexp(ec / tot, e))
        ob_ref[...] = ob

        def rms(o, gain):
            rr = lax.rsqrt(_hsum(o * o, et) * (1.0 / HEAD_DIM) + RMS_EPS)
            return o * _hexp(rr, e) * gain

        merged = jnp.concatenate([rms(oa_ref[...], ga_ref[...]), rms(ob, gb_ref[...])], axis=1).astype(BF16)
        mg_ref[...] = merged
        mix = _nn(merged, w_ref[...])
        mix_ref[...] = mix.astype(BF16)
        r1 = ALPHA * x_ref[...] + ada_ref[0, 2:3, :] * mix
        d = r1 - jnp.mean(r1, axis=1, keepdims=True)
        rstd = lax.rsqrt(jnp.mean(d * d, axis=1, keepdims=True) + LN_EPS)
        xh = d * rstd
        xh_ref[...] = xh
        rs_ref[...] = jnp.broadcast_to(rstd, (tm, LANES))
        x1 = xh * g_ref[...] + b_ref[...]
        h2_ref[...] = (x1 * (1.0 + ada_ref[0, 4:5, :]) + ada_ref[0, 3:4, :]).astype(BF16)

    tok = lambda w: pl.BlockSpec((tm, w), lambda i: (i, 0))
    vec = lambda w: pl.BlockSpec((1, w), lambda i: (0, 0))
    whole = lambda a: pl.BlockSpec(a.shape, lambda i: (0, 0))
    classes = lambda a, d: a.reshape(t_all // seq * d, seq // d, a.shape[-1])
    return pl.pallas_call(
        body, name="mix_out", grid=(t_all // tm,),
        in_specs=[tok(WIDTH), tok(WIDTH), _class_spec(4, WIDTH, nts), _class_spec(16, WIDTH, nts), tok(LANES),
                  _class_spec(4, LANES, nts), _class_spec(16, LANES, nts), vec(WIDTH), vec(WIDTH), whole(w_out),
                  tok(D_MODEL), pl.BlockSpec((1, 6, D_MODEL), lambda i: (i // nts, 0, 0)), vec(D_MODEL), vec(D_MODEL)]
        + [whole(p) for p in perms],
        out_specs=[tok(WIDTH), tok(LANES), _class_spec(4, LANES, nts), _class_spec(16, LANES, nts), tok(D_MODEL),
                   tok(D_MODEL), tok(D_MODEL), tok(LANES), tok(D_MODEL)],
        out_shape=[jax.ShapeDtypeStruct((t_all, WIDTH), F32), jax.ShapeDtypeStruct((t_all, LANES), F32),
                   _class_shape(t_all, seq, 4, LANES, F32), _class_shape(t_all, seq, 16, LANES, F32),
                   jax.ShapeDtypeStruct((t_all, D_MODEL), BF16), jax.ShapeDtypeStruct((t_all, D_MODEL), BF16),
                   jax.ShapeDtypeStruct((t_all, D_MODEL), F32), jax.ShapeDtypeStruct((t_all, LANES), F32),
                   jax.ShapeDtypeStruct((t_all, D_MODEL), BF16)],
        compiler_params=_params(),
    )(oa, o3[0], classes(o3[1], 4), classes(o3[2], 16), l3[0], classes(l3[1], 4), classes(l3[2], 16), gn_a, gn_b,
      w_out, x, ada3, ln_g, ln_b, *perms)


def _mix_out_bwd(dmix, w_out, oa, ob, gn_a, gn_b, perms, seq):
    t_all = dmix.shape[0]
    tm = TOK_TM
    nts = seq // tm

    def body(dm_ref, w_ref, oa_ref, ob_ref, ga_ref, gb_ref, p4_ref, p16_ref, doa_ref, dob_ref, dob4_ref, dob16_ref,
             dla_ref, dlb_ref, dlb4_ref, dlb16_ref, acc_ref):
        @pl.when(pl.program_id(0) == 0)
        def _():
            acc_ref[...] = jnp.zeros_like(acc_ref)
        e, et = _head_mats()
        dmg = _nt(dm_ref[...], w_ref[...])

        def group(o, dn, gain):
            rr = lax.rsqrt(_hsum(o * o, et) * (1.0 / HEAD_DIM) + RMS_EPS)
            re = _hexp(rr, e)
            dgain = jnp.sum(dn * o * re, axis=0, keepdims=True)
            dxn = dn * gain
            tt = _hsum(dxn * o, et) * (rr * rr * rr) * (1.0 / HEAD_DIM)
            do = re * dxn - o * _hexp(tt, e)
            return do, _hsum(do * o, et), dgain

        doa, dla, dga = group(oa_ref[...], dmg[:, :WIDTH], ga_ref[...])
        dob, dlb, dgb = group(ob_ref[...], dmg[:, WIDTH:], gb_ref[...])
        dob = dob.astype(BF16)
        doa_ref[...] = doa.astype(BF16)
        dob_ref[...] = dob
        _store_classes(dob4_ref, _nn(p4_ref[...], dob).astype(BF16), 4)
        _store_classes(dob16_ref, _nn(p16_ref[...], dob).astype(BF16), 16)
        dla_ref[...] = dla
        dlb_ref[...] = dlb
        _store_classes(dlb4_ref, _permute_f32(p4_ref[...], dlb), 4)
        _store_classes(dlb16_ref, _permute_f32(p16_ref[...], dlb), 16)
        acc_ref[0:1, :] += jnp.concatenate([dga, dgb], axis=1)

    tok = lambda w: pl.BlockSpec((tm, w), lambda i: (i, 0))
    vec = lambda w: pl.BlockSpec((1, w), lambda i: (0, 0))
    return pl.pallas_call(
        body, name="mix_out_bwd", grid=(t_all // tm,),
        in_specs=[tok(D_MODEL), pl.BlockSpec(w_out.shape, lambda i: (0, 0)), tok(WIDTH), tok(WIDTH), vec(WIDTH),
                  vec(WIDTH), pl.BlockSpec(perms[0].shape, lambda i: (0, 0)),
                  pl.BlockSpec(perms[1].shape, lambda i: (0, 0))],
        out_specs=[tok(WIDTH), tok(WIDTH), _class_spec(4, WIDTH, nts), _class_spec(16, WIDTH, nts), tok(LANES),
                   tok(LANES), _class_spec(4, LANES, nts), _class_spec(16, LANES, nts),
                   pl.BlockSpec((8, D_MODEL), lambda i: (0, 0))],
        out_shape=[jax.ShapeDtypeStruct((t_all, WIDTH), BF16), jax.ShapeDtypeStruct((t_all, WIDTH), BF16),
                   _class_shape(t_all, seq, 4, WIDTH, BF16), _class_shape(t_all, seq, 16, WIDTH, BF16),
                   jax.ShapeDtypeStruct((t_all, LANES), F32), jax.ShapeDtypeStruct((t_all, LANES), F32),
                   _class_shape(t_all, seq, 4, LANES, F32), _class_shape(t_all, seq, 16, LANES, F32),
                   jax.ShapeDtypeStruct((8, D_MODEL), F32)],
        compiler_params=_params(),
    )(dmix, w_out, oa, ob, gn_a, gn_b, perms[0], perms[1])


def _inproj_bwd(dqt, dka, dva, dil1, dil4, dil16, dfa16, pos, wqkv, wf16, freq, perms, dr1, x, ada3, seq):
    t_all = x.shape[0]
    tm = TOK_TM
    nts = seq // tm

    def body(dqt_ref, dka_ref, dva_ref, q1_ref, k1_ref, v1_ref, q4_ref, k4_ref, v4_ref, q16_ref, k16_ref, v16_ref,
             dfa_ref, pos_ref, w_ref, wf_ref, fr_ref, pt4_ref, pt16_ref, dr1_ref, x_ref, ada_ref, gx_ref, dz_ref,
             acc_ref):
        i = pl.program_id(0)

        @pl.when(i == 0)
        def _():
            acc_ref[...] = jnp.zeros_like(acc_ref)
        tabs = _rope_tabs(pos_ref, fr_ref, -1.0)
        dz_ref[:, :WIDTH] = dqt_ref[...].T.astype(BF16)
        dz_ref[:, WIDTH:2 * WIDTH] = dka_ref[...]
        dz_ref[:, 2 * WIDTH:3 * WIDTH] = dva_ref[...]
        for t, (n1, n4, n16) in enumerate(((q1_ref, q4_ref, q16_ref), (k1_ref, k4_ref, k16_ref),
                                           (v1_ref, v4_ref, v16_ref))):
            tot = (n1[...].astype(F32) + _nn(pt4_ref[...], _load_classes(n4, 4))
                   + _nn(pt16_ref[...], _load_classes(n16, 16)))
            if t < 2:
                tot = _rope(tot, tabs)
            dz_ref[:, (3 + t) * WIDTH:(4 + t) * WIDTH] = tot.astype(BF16)
        dh1 = _tn(dfa_ref[...], wf_ref[...])
        for n in range(6):
            cs = slice(n * WIDTH, (n + 1) * WIDTH)
            dh1 = dh1 + _nt(dz_ref[:, cs], w_ref[:, cs])
        xv = x_ref[...]
        gx_ref[...] = ALPHA * dr1_ref[...] + dh1 * (1.0 + ada_ref[0, 1:2, :])
        b = i // nts
        acc_ref[pl.ds(b, 1), :] += jnp.sum(dh1 * xv, axis=0, keepdims=True)
        acc_ref[pl.ds(8 + b, 1), :] += jnp.sum(dh1, axis=0, keepdims=True)

    tok = lambda w: pl.BlockSpec((tm, w), lambda i: (i, 0))
    whole = lambda a: pl.BlockSpec(a.shape, lambda i: (0, 0))
    classes = lambda a, d: a.reshape(t_all // seq * d, seq // d, a.shape[-1])
    return pl.pallas_call(
        body, name="inproj_bwd", grid=(t_all // tm,),
        in_specs=[pl.BlockSpec((WIDTH, tm), lambda i: (i // nts, i % nts)), tok(WIDTH), tok(WIDTH)]
        + [tok(WIDTH)] * 3 + [_class_spec(4, WIDTH, nts)] * 3 + [_class_spec(16, WIDTH, nts)] * 3
        + [pl.BlockSpec((16, tm), lambda i: (0, i)), tok(1), whole(wqkv), whole(wf16),
           pl.BlockSpec((1, LANES), lambda i: (0, 0)), whole(perms[2]), whole(perms[3]), tok(D_MODEL), tok(D_MODEL),
           pl.BlockSpec((1, 6, D_MODEL), lambda i: (i // nts, 0, 0))],
        out_specs=[tok(D_MODEL), tok(6 * WIDTH), pl.BlockSpec((16, D_MODEL), lambda i: (0, 0))],
        out_shape=[jax.ShapeDtypeStruct((t_all, D_MODEL), F32), jax.ShapeDtypeStruct((t_all, 6 * WIDTH), BF16),
                   jax.ShapeDtypeStruct((16, D_MODEL), F32)],
        compiler_params=_params(),
    )(dqt, dka, dva, *dil1, *[classes(a, 4) for a in dil4], *[classes(a, 16) for a in dil16], dfa16, pos, wqkv, wf16,
      freq, perms[2], perms[3], dr1, x, ada3)


FFN_TM = 512
FFN_TN = 256
HALO = 8


FFN_CHUNK = 64


def _conv(cat_ref, w_ref, b_ref, start, rows):
    return (b_ref[...] + w_ref[0:1, :] * cat_ref[pl.ds(start + HALO - 2, rows), :]
            + w_ref[1:2, :] * cat_ref[pl.ds(start + HALO - 1, rows), :]
            + w_ref[2:3, :] * cat_ref[pl.ds(start + HALO, rows), :])


def _ffn_gate(u, conv_w, conv_b, seq):
    t_all = u.shape[0]
    tm, tn = FFN_TM, FFN_TN
    nc = D_FF // tn
    nts = seq // tm

    def body(ua_ref, uap_ref, ug_ref, ugp_ref, wa_ref, wg_ref, ba_ref, bg_ref, o_ref, ca_ref, cg_ref):
        first = (pl.program_id(0) % nts) == 0
        zero = jnp.zeros((HALO, tn), F32)
        ca_ref[0:HALO, :] = jnp.where(first, zero, uap_ref[...])
        cg_ref[0:HALO, :] = jnp.where(first, zero, ugp_ref[...])
        ca_ref[HALO:, :] = ua_ref[...]
        cg_ref[HALO:, :] = ug_ref[...]
        for c0 in range(0, tm, FFN_CHUNK):
            ya = _conv(ca_ref, wa_ref, ba_ref, c0, FFN_CHUNK)
            yg = _conv(cg_ref, wg_ref, bg_ref, c0, FFN_CHUNK)
            o_ref[c0:c0 + FFN_CHUNK, :] = (yg * jax.nn.sigmoid(yg) * ya).astype(BF16)

    cur = lambda off: pl.BlockSpec((tm, tn), lambda t, n: (t, n + off))
    prev = lambda off: pl.BlockSpec((HALO, tn), lambda t, n: (jnp.maximum(t * (tm // HALO) - 1, 0), n + off))
    vec = lambda r, off: pl.BlockSpec((r, tn), lambda t, n: (0, n + off))
    return pl.pallas_call(
        body, name="ffn_gate", grid=(t_all // tm, nc),
        in_specs=[cur(0), prev(0), cur(nc), prev(nc), vec(3, 0), vec(3, nc), vec(1, 0), vec(1, nc)],
        out_specs=pl.BlockSpec((tm, tn), lambda t, n: (t, n)),
        out_shape=jax.ShapeDtypeStruct((t_all, D_FF), BF16),
        scratch_shapes=[pltpu.VMEM((tm + HALO, tn), F32)] * 2, compiler_params=_params(),
    )(u, u, u, u, conv_w, conv_w, conv_b, conv_b)


def _ffn_gate_bwd(u, dfi, conv_w, conv_b, seq):
    t_all = u.shape[0]
    tm, tn = FFN_TM, FFN_TN
    nc = D_FF // tn
    nts = seq // tm

    def body(ua_ref, uap_ref, uan_ref, ug_ref, ugp_ref, ugn_ref, df_ref, dfn_ref, wa_ref, wg_ref, ba_ref, bg_ref,
             dua_ref, dug_ref, acca_ref, accg_ref, ca_ref, cg_ref, ya_ref, yg_ref):
        t = pl.program_id(1)
        first = (t % nts) == 0
        last = (t % nts) == nts - 1

        @pl.when(t == 0)
        def _():
            acca_ref[...] = jnp.zeros_like(acca_ref)
            accg_ref[...] = jnp.zeros_like(accg_ref)
        zero = jnp.zeros((HALO, tn), F32)
        for cat, cur, prv, nxt in ((ca_ref, ua_ref, uap_ref, uan_ref), (cg_ref, ug_ref, ugp_ref, ugn_ref)):
            cat[0:HALO, :] = jnp.where(first, zero, prv[...])
            cat[HALO:HALO + tm, :] = cur[...]
            cat[HALO + tm:, :] = nxt[...]
        ch = FFN_CHUNK
        sums = [[jnp.zeros((1, tn), F32) for _ in range(4)] for _ in range(2)]
        for ci, c0 in enumerate(range(0, tm, ch)):
            ya = _conv(ca_ref, wa_ref, ba_ref, c0, ch + HALO)
            yg = _conv(cg_ref, wg_ref, bg_ref, c0, ch + HALO)
            if c0 + ch < tm:
                beyond = df_ref[c0 + ch:c0 + ch + 16, :].astype(F32)[:HALO]
            else:
                beyond = jnp.where(last, 0.0, dfn_ref[...].astype(F32)[:HALO])
            dfe = jnp.concatenate([df_ref[c0:c0 + ch, :].astype(F32), beyond], axis=0)
            sg = jax.nn.sigmoid(yg)
            ya_ref[ci] = dfe * (yg * sg)
            yg_ref[ci] = dfe * ya * (sg * (1.0 + yg * (1.0 - sg)))
            for half, (dy, cat, w_ref, du_ref) in enumerate(((ya_ref, ca_ref, wa_ref, dua_ref),
                                                             (yg_ref, cg_ref, wg_ref, dug_ref))):
                d0 = dy[ci, 0:ch, :]
                du = (w_ref[2:3, :] * d0 + w_ref[1:2, :] * dy[ci, pl.ds(1, ch), :]
                      + w_ref[0:1, :] * dy[ci, pl.ds(2, ch), :])
                du_ref[c0:c0 + ch, :] = du.astype(BF16)
                for k in range(3):
                    sums[half][k] += jnp.sum(d0 * cat[pl.ds(c0 + HALO - 2 + k, ch), :], axis=0, keepdims=True)
                sums[half][3] += jnp.sum(d0, axis=0, keepdims=True)
        for half, acc in enumerate((acca_ref, accg_ref)):
            for k in range(4):
                acc[k:k + 1, :] += sums[half][k]

    nrow = t_all // HALO
    cur = lambda off: pl.BlockSpec((tm, tn), lambda n, t: (t, n + off))
    prev = lambda off: pl.BlockSpec((HALO, tn), lambda n, t: (jnp.maximum(t * (tm // HALO) - 1, 0), n + off))
    nxt = lambda off: pl.BlockSpec((HALO, tn), lambda n, t: (jnp.minimum((t + 1) * (tm // HALO), nrow - 1), n + off))
    vec = lambda r, off: pl.BlockSpec((r, tn), lambda n, t: (0, n + off))
    dcur = pl.BlockSpec((tm, tn), lambda n, t: (t, n))
    dnxt = pl.BlockSpec((16, tn), lambda n, t: (jnp.minimum((t + 1) * (tm // 16), t_all // 16 - 1), n))
    acc = pl.BlockSpec((8, tn), lambda n, t: (0, n))
    return pl.pallas_call(
        body, name="ffn_gate_bwd", grid=(nc, t_all // tm),
        in_specs=[cur(0), prev(0), nxt(0), cur(nc), prev(nc), nxt(nc), dcur, dnxt, vec(3, 0), vec(3, nc), vec(1, 0),
                  vec(1, nc)],
        out_specs=[dcur, dcur, acc, acc],
        out_shape=[jax.ShapeDtypeStruct((t_all, D_FF), BF16), jax.ShapeDtypeStruct((t_all, D_FF), BF16),
                   jax.ShapeDtypeStruct((8, D_FF), F32), jax.ShapeDtypeStruct((8, D_FF), F32)],
        scratch_shapes=[pltpu.VMEM((tm + 2 * HALO, tn), F32)] * 2
        + [pltpu.VMEM((tm // FFN_CHUNK, FFN_CHUNK + HALO, tn), F32)] * 2,
        compiler_params=_params(),
    )(u, u, u, u, u, u, dfi, dfi, conv_w, conv_w, conv_b, conv_b)


def _ffn_down(ffn_in, w_down, xh1, ln1_g, ln1_b, ada3, ln2_g, ln2_b, target, seq):
    t_all = xh1.shape[0]
    tm = 256
    nts = seq // tm

    def body(f_ref, w_ref, xh_ref, g1_ref, b1_ref, ada_ref, g2_ref, b2_ref, tg_ref, dr2_ref, acc_ref):
        i = pl.program_id(0)

        @pl.when(i == 0)
        def _():
            acc_ref[...] = jnp.zeros_like(acc_ref)
        ffn = _nn(f_ref[...], w_ref[...])
        x1 = xh_ref[...] * g1_ref[...] + b1_ref[...]
        r2 = ALPHA * x1 + ada_ref[0, 5:6, :] * ffn
        d = r2 - jnp.mean(r2, axis=1, keepdims=True)
        rstd = lax.rsqrt(jnp.mean(d * d, axis=1, keepdims=True) + LN_EPS)
        xh2 = d * rstd
        diff = xh2 * g2_ref[...] + b2_ref[...] - tg_ref[...]
        dy = diff * (1.0 / D_MODEL)
        dr2 = _layer_norm_bwd(dy * g2_ref[...], xh2, rstd)
        dr2_ref[...] = dr2
        acc_ref[0:1, :] += jnp.sum(dy * xh2, axis=0, keepdims=True)
        acc_ref[1:2, :] += jnp.sum(dy, axis=0, keepdims=True)
        acc_ref[2:3, :] += jnp.sum(diff * diff, axis=0, keepdims=True) * (0.5 / D_MODEL)
        acc_ref[pl.ds(8 + i // nts, 1), :] += jnp.sum(dr2 * ffn, axis=0, keepdims=True)

    tok = lambda w: pl.BlockSpec((tm, w), lambda i: (i, 0))
    vec = pl.BlockSpec((1, D_MODEL), lambda i: (0, 0))
    return pl.pallas_call(
        body, name="ffn_down", grid=(t_all // tm,),
        in_specs=[tok(D_FF), pl.BlockSpec(w_down.shape, lambda i: (0, 0)), tok(D_MODEL), vec, vec,
                  pl.BlockSpec((1, 6, D_MODEL), lambda i: (i // nts, 0, 0)), vec, vec, tok(D_MODEL)],
        out_specs=[tok(D_MODEL), pl.BlockSpec((16, D_MODEL), lambda i: (0, 0))],
        out_shape=[jax.ShapeDtypeStruct((t_all, D_MODEL), F32), jax.ShapeDtypeStruct((16, D_MODEL), F32)],
        compiler_params=_params(),
    )(ffn_in, w_down, xh1, ln1_g, ln1_b, ada3, ln2_g, ln2_b, target)


def _ffn_down_bwd(dr2, ada3, w_down, seq):
    t_all = dr2.shape[0]
    tm = 256
    nts = seq // tm

    def body(d_ref, ada_ref, w_ref, dffn_ref, dfi_ref):
        dffn = (d_ref[...] * ada_ref[0, 5:6, :]).astype(BF16)
        dffn_ref[...] = dffn
        dfi_ref[...] = _nt(dffn, w_ref[...]).astype(BF16)

    tok = lambda w: pl.BlockSpec((tm, w), lambda i: (i, 0))
    return pl.pallas_call(
        body, name="ffn_down_bwd", grid=(t_all // tm,),
        in_specs=[tok(D_MODEL), pl.BlockSpec((1, 6, D_MODEL), lambda i: (i // nts, 0, 0)),
                  pl.BlockSpec(w_down.shape, lambda i: (0, 0))],
        out_specs=[tok(D_MODEL), tok(D_FF)],
        out_shape=[jax.ShapeDtypeStruct((t_all, D_MODEL), BF16), jax.ShapeDtypeStruct((t_all, D_FF), BF16)],
        compiler_params=_params(),
    )(dr2, ada3, w_down)


def _ffn_up_bwd(du_a, du_g, w_up, dr2, xh1, rs1, mix, ada3, ln1_g, ln1_b, seq):
    t_all = dr2.shape[0]
    tm = 256
    nts = seq // tm

    def body(da_ref, dg_ref, w_ref, dr2_ref, xh_ref, rs_ref, mix_ref, ada_ref, g_ref, b_ref, dr1_ref, dmix_ref,
             acc_ref):
        i = pl.program_id(0)

        @pl.when(i == 0)
        def _():
            acc_ref[...] = jnp.zeros_like(acc_ref)
        dh2 = _nt(da_ref[...], w_ref[:, :D_FF]) + _nt(dg_ref[...], w_ref[:, D_FF:])
        xh = xh_ref[...]
        x1 = xh * g_ref[...] + b_ref[...]
        dx1 = ALPHA * dr2_ref[...] + dh2 * (1.0 + ada_ref[0, 4:5, :])
        dr1 = _layer_norm_bwd(dx1 * g_ref[...], xh, rs_ref[:, 0:1])
        dr1_ref[...] = dr1
        dmix_ref[...] = (dr1 * ada_ref[0, 2:3, :]).astype(BF16)
        b = i // nts
        acc_ref[0:1, :] += jnp.sum(dx1 * xh, axis=0, keepdims=True)
        acc_ref[1:2, :] += jnp.sum(dx1, axis=0, keepdims=True)
        acc_ref[pl.ds(8 + b, 1), :] += jnp.sum(dh2 * x1, axis=0, keepdims=True)
        acc_ref[pl.ds(16 + b, 1), :] += jnp.sum(dh2, axis=0, keepdims=True)
        acc_ref[pl.ds(24 + b, 1), :] += jnp.sum(dr1 * mix_ref[...].astype(F32), axis=0, keepdims=True)

    tok = lambda w: pl.BlockSpec((tm, w), lambda i: (i, 0))
    vec = pl.BlockSpec((1, D_MODEL), lambda i: (0, 0))
    return pl.pallas_call(
        body, name="ffn_up_bwd", grid=(t_all // tm,),
        in_specs=[tok(D_FF), tok(D_FF), pl.BlockSpec(w_up.shape, lambda i: (0, 0)), tok(D_MODEL), tok(D_MODEL),
                  tok(LANES), tok(D_MODEL), pl.BlockSpec((1, 6, D_MODEL), lambda i: (i // nts, 0, 0)), vec, vec],
        out_specs=[tok(D_MODEL), tok(D_MODEL), pl.BlockSpec((32, D_MODEL), lambda i: (0, 0))],
        out_shape=[jax.ShapeDtypeStruct((t_all, D_MODEL), F32), jax.ShapeDtypeStruct((t_all, D_MODEL), BF16),
                   jax.ShapeDtypeStruct((32, D_MODEL), F32)],
        compiler_params=_params(),
    )(du_a, du_g, w_up, dr2, xh1, rs1, mix, ada3, ln1_g, ln1_b)


def _rows(a):
    return a[:, :N_HEADS].T


def _rope_freq():
    f = np.float32(ROPE_THETA) ** (-np.arange(0, ROPE_DIMS, 2, dtype=np.float32) / np.float32(ROPE_DIMS))
    return jnp.asarray(np.tile(f.astype(np.float32), LANES // (ROPE_DIMS // 2))[None, :])


def _local_step(x, positions, target, ada3, w_in, b_fgate, gn_a, gn_b, ln1_g, ln1_b, conv_b, ln2_g, ln2_b,
                late_shards):
    nbat, seq, _ = x.shape
    t_all = nbat * seq
    xf = x.reshape(t_all, D_MODEL)
    tg = target.reshape(t_all, D_MODEL)
    pos = positions.reshape(t_all, 1)
    freq = _rope_freq()

    wqkv = jnp.concatenate([w_in[:, :3 * WIDTH], w_in[:, 3 * WIDTH + N_HEADS:]], axis=1)
    wf16 = jnp.zeros((16, D_MODEL), BF16).at[:N_HEADS].set(w_in[:, 3 * WIDTH:3 * WIDTH + N_HEADS].T)
    bf = b_fgate.reshape(N_HEADS, 1)

    perms = [_perm_matrix(TOK_TM, d, tr) for tr in (False, True) for d in DILATIONS[1:]]
    h1, za, zb1, zb4, zb16, vt, fa_t = _inproj(xf, ada3, pos, wqkv, wf16, freq, perms, seq)
    zbs = [zb1, zb4.reshape(t_all, 3 * WIDTH), zb16.reshape(t_all, 3 * WIDTH)]
    f_row = _fgate_fwd(fa_t, bf, seq)
    f_col = jnp.zeros((t_all, LANES), F32).at[:, :N_HEADS].set(f_row.T)
    oa, lse_row_a, gathered = _fox_fwd(za, vt, f_col, seq, [late_shards[n] for n in LATE])
    w_out, w_up, conv_w, w_down = (_full_from_gathered(n, g) for n, g in zip(LATE, gathered))
    o3, l3 = zip(*[_dil_fwd(zb, seq, d) for zb, d in zip(zbs, DILATIONS)])
    ob, lse_b, lse_b4, lse_b16, merged, mix, xh1, rs1, h2 = _mix_out(oa, o3, l3, gn_a, gn_b, w_out, xf, ada3, ln1_g,
                                                                      ln1_b, perms, seq)
    u = _matmul(h2, w_up, False, F32, 256, 512, "ffn_up")
    ffn_in = _ffn_gate(u, conv_w, conv_b, seq)
    dr2, acc2 = _ffn_down(ffn_in, w_down, xh1, ln1_g, ln1_b, ada3, ln2_g, ln2_b, tg, seq)

    dffn, dfi = _ffn_down_bwd(dr2, ada3, w_down, seq)
    d_w_down = _matmul_tn(ffn_in, dffn, 512, 512, "dw_down")
    du_a, du_g, acc_ca, acc_cg = _ffn_gate_bwd(u, dfi, conv_w, conv_b, seq)
    dr1, dmix, acc1 = _ffn_up_bwd(du_a, du_g, w_up, dr2, xh1, rs1, mix, ada3, ln1_g, ln1_b, seq)
    d_w_up = jnp.concatenate([_matmul_tn(h2, du_a, 256, 512, "dw_up_a"), _matmul_tn(h2, du_g, 256, 512, "dw_up_g")],
                             axis=1)

    doa, dob, dob4, dob16, dl_a, dl_b, dl_b4, dl_b16, acc_gn = _mix_out_bwd(dmix, w_out, oa, ob, gn_a, gn_b, perms, seq)
    d_w_out = _matmul_tn(merged, dmix, 512, 512, "dw_out")
    late_grads = dict(w_out=d_w_out, w_up=d_w_up, conv_w=jnp.concatenate([acc_ca[0:3], acc_cg[0:3]], axis=1),
                      w_down=d_w_down)
    dka, dva, df_k, dqt, df_q, late_parts = _fox_bwd(za, doa, f_col, lse_row_a, _rows(dl_a), seq,
                                                     [_payload(n, _dest_major(n, late_grads[n])) for n in LATE])
    dfa_t, dbf = _fgate_bwd(_rows(df_k) + df_q, fa_t, bf, seq)
    flat = lambda a: a.reshape(t_all, a.shape[-1])
    dil = []
    for zb, d, do, lse, dl in zip(zbs, DILATIONS, (dob, flat(dob4), flat(dob16)),
                                  (lse_b, flat(lse_b4), flat(lse_b16)), (dl_b, flat(dl_b4), flat(dl_b16))):
        dil.append((_dil_bwd_dq(zb, do, lse, dl, seq, d), *_dil_bwd_dkv(zb, do, lse, dl, seq, d)))
    dfa16 = jnp.zeros((16, t_all), BF16).at[:N_HEADS].set(dfa_t.astype(BF16))
    grad_x, dz, acc0 = _inproj_bwd(dqt, dka, dva, dil[0], dil[1], dil[2], dfa16, pos, wqkv, wf16, freq, perms, dr1, xf,
                                   ada3, seq)
    d_wqkv = _matmul_tn(h1, dz, 512, 512, "dw_in")
    d_wf = _matmul_rows(dfa16, h1, 512, "dw_fgate")[:N_HEADS].T
    d_w_in = jnp.concatenate([d_wqkv[:, :3 * WIDTH], d_wf, d_wqkv[:, 3 * WIDTH:]], axis=1)

    dada = jnp.concatenate([acc0[8:8 + nbat], acc0[:nbat], acc1[24:24 + nbat], acc1[16:16 + nbat], acc1[8:8 + nbat],
                            acc2[8:8 + nbat]], axis=1)

    grads = dict(
        dada=dada, b_ada=jnp.sum(dada, axis=0, keepdims=True), w_in=d_w_in, b_fgate=dbf[:, 0][None, :],
        gn_a=acc_gn[0:1, :WIDTH], gn_b=acc_gn[0:1, WIDTH:], ln1_g=acc1[0:1], ln1_b=acc1[1:2],
        conv_b=jnp.concatenate([acc_ca[3:4], acc_cg[3:4]], axis=1), ln2_g=acc2[0:1], ln2_b=acc2[1:2])
    return acc2[2:3], grad_x.reshape(x.shape), grads, dict(zip(LATE, late_parts))


LATE = ("w_out", "w_up", "conv_w", "w_down")
BIG = ("w_ada", "w_in") + LATE
COLUMN_SHARDED = ("w_ada", "w_in", "w_up", "conv_w")


def _payload(name, a):
    return a if name == "conv_w" else a.astype(BF16)
SMALL = ("b_ada", "b_fgate", "gn_a", "gn_b", "ln1_g", "ln1_b", "conv_b", "ln2_g", "ln2_b")
ADAM_ROWS = dict(w_ada=256, w_in=256, w_out=128, w_up=256, conv_w=3, w_down=176)
SMALL_ROWS = 24


def _full_from_gathered(name, g):
    if name in COLUMN_SHARDED:
        return g.transpose(1, 0, 2).reshape(g.shape[1], N_DEV * g.shape[2])
    return g.reshape(N_DEV * g.shape[1], g.shape[2])


def _dest_major(name, full):
    if name in COLUMN_SHARDED:
        r, cfull = full.shape
        return full.reshape(r, N_DEV, cfull // N_DEV).transpose(1, 0, 2)
    return full.reshape(N_DEV, full.shape[0] // N_DEV, full.shape[1])


def _pack_small(vals, extra=None):
    parts = [vals[n].reshape(-1) for n in SMALL]
    if extra is not None:
        parts.append(extra.reshape(-1))
    flat = jnp.concatenate(parts)
    return jnp.pad(flat, (0, SMALL_ROWS * D_MODEL - flat.shape[0])).reshape(SMALL_ROWS, D_MODEL)


def _unpack_small(packed, like):
    flat = packed.reshape(-1)
    out, off = {}, 0
    for n in SMALL:
        size = like[n].size
        out[n] = flat[off:off + size].reshape(like[n].shape)
        off += size
    return out, flat[off:off + D_MODEL]


def kernel(x, c, positions, w_ada, b_ada, w_in, b_fgate, gn_a, gn_b, w_out, ln1_g, ln1_b, w_up, conv_w, conv_b, w_down, ln2_g, ln2_b, loss_target, m_w_ada, m_b_ada, m_w_in, m_b_fgate, m_gn_a, m_gn_b, m_w_out, m_ln1_g, m_ln1_b, m_w_up, m_conv_w, m_conv_b, m_w_down, m_ln2_g, m_ln2_b, v_w_ada, v_b_ada, v_w_in, v_b_fgate, v_gn_a, v_gn_b, v_w_out, v_ln1_g, v_ln1_b, v_w_up, v_conv_w, v_conv_b, v_w_down, v_ln2_g, v_ln2_b):
    w = dict(w_ada=w_ada[0], b_ada=b_ada, w_in=w_in[0], b_fgate=b_fgate, gn_a=gn_a, gn_b=gn_b, w_out=w_out[0],
             ln1_g=ln1_g, ln1_b=ln1_b, w_up=w_up[0], conv_w=conv_w[0], conv_b=conv_b, w_down=w_down[0], ln2_g=ln2_g,
             ln2_b=ln2_b)
    m = dict(w_ada=m_w_ada[0], b_ada=m_b_ada, w_in=m_w_in[0], b_fgate=m_b_fgate, gn_a=m_gn_a, gn_b=m_gn_b,
             w_out=m_w_out[0], ln1_g=m_ln1_g, ln1_b=m_ln1_b, w_up=m_w_up[0], conv_w=m_conv_w[0], conv_b=m_conv_b,
             w_down=m_w_down[0], ln2_g=m_ln2_g, ln2_b=m_ln2_b)
    v = dict(w_ada=v_w_ada[0], b_ada=v_b_ada, w_in=v_w_in[0], b_fgate=v_b_fgate, gn_a=v_gn_a, gn_b=v_gn_b,
             w_out=v_w_out[0], ln1_g=v_ln1_g, ln1_b=v_ln1_b, w_up=v_w_up[0], conv_w=v_conv_w[0], conv_b=v_conv_b,
             w_down=v_w_down[0], ln2_g=v_ln2_g, ln2_b=v_ln2_b)

    nbat = x.shape[0]
    me = 4 * lax.axis_index("x") + 2 * lax.axis_index("y") + lax.axis_index("c")
    ada_cols = w["w_ada"].shape[1]

    c_all, w_in_all = _exchange([c, _payload("w_in", w["w_in"])], [True, True], "weight_gather")
    c_all = c_all.reshape(N_DEV * nbat, D_MODEL)
    ada_mine = _ada_fwd(c_all, w["w_ada"], lax.dynamic_slice(b_ada, (0, me * ada_cols), (1, ada_cols)))
    (ada_parts,) = _exchange([ada_mine.reshape(N_DEV, nbat, ada_cols)], [False], "ada_exchange")
    ada3 = ada_parts.transpose(1, 0, 2).reshape(nbat, 6, D_MODEL)

    loss_lanes, grad_x, g_local, parts = _local_step(
        x, positions, loss_target, ada3, _full_from_gathered("w_in", w_in_all), b_fgate, gn_a, gn_b, ln1_g, ln1_b,
        conv_b, ln2_g, ln2_b, {n: _payload(n, w[n]) for n in LATE})

    parts["w_in"], dada_all, small_all = _exchange(
        [_payload("w_in", _dest_major("w_in", g_local["w_in"])), g_local["dada"], _pack_small(g_local, loss_lanes)],
        [False, True, True], "grad_exchange")
    dada_cols = lax.dynamic_slice(dada_all.reshape(N_DEV * nbat, 6 * D_MODEL), (0, me * ada_cols),
                                  (N_DEV * nbat, ada_cols))
    parts["w_ada"] = _ada_bwd(c_all, dada_cols)[None]

    grad, delta, new_m, new_v = {}, {}, {}, {}
    for n in BIG:
        grad[n], delta[n], new_m[n], new_v[n] = (
            a[None] for a in _adamw(parts[n], w[n], m[n], v[n], ADAM_ROWS[n], "adamw_" + n))
    packed = _adamw(small_all, _pack_small(w), _pack_small(m), _pack_small(v), SMALL_ROWS, "adamw_small")
    for dst, pk in zip((grad, delta, new_m, new_v), packed):
        vals, lanes = _unpack_small(pk, w)
        dst.update(vals)
        if dst is grad:
            loss = jnp.sum(lanes)

    order = ("w_ada", "b_ada", "w_in", "b_fgate", "gn_a", "gn_b", "w_out", "ln1_g", "ln1_b", "w_up", "conv_w", "conv_b",
             "w_down", "ln2_g", "ln2_b")
    return (loss, grad_x, *[grad[n] for n in order], *[delta[n] for n in order], *[new_m[n] for n in order],
            *[new_v[n] for n in order])
```

```python
import functools

import numpy as np
import jax
import jax.numpy as jnp
from jax import lax
from jax.experimental import pallas as pl
from jax.experimental.pallas import tpu as pltpu

F32, BF16 = jnp.float32, jnp.bfloat16
HIGHEST = lax.Precision.HIGHEST
MESH = pl.DeviceIdType.MESH
ANY = pl.BlockSpec(memory_space=pl.ANY)

D_MODEL = 1024
N_HEADS = 8
HEAD_DIM = 64
WIDTH = 512
D_FF = 2816
N_DEV = 8
ROPE_DIMS = 16
ROPE_THETA = 500000.0
ALPHA = 2.0 ** 0.25
LN_EPS = 1e-5
RMS_EPS = 1e-6
NEG = -1e30
Q_SCALE = 0.125
BLK = 128
LANES = 128
VMEM_LIMIT_BYTES = 56 * 1024 * 1024

ADAM_LR, ADAM_B1, ADAM_B2, ADAM_EPS, ADAM_WD, ADAM_STEP = 0.001, 0.9, 0.999, 1e-08, 0.01, 10


def _params(vmem=VMEM_LIMIT_BYTES):
    return pltpu.CompilerParams(vmem_limit_bytes=vmem)


def _nn(a, b):
    return jnp.dot(a, b, preferred_element_type=F32)


def _nt(a, b):
    return lax.dot_general(a, b, (((1,), (1,)), ((), ())), preferred_element_type=F32)


def _tn(a, b):
    return lax.dot_general(a, b, (((0,), (0,)), ((), ())), preferred_element_type=F32)


def _head_mats():
    r = lax.broadcasted_iota(jnp.int32, (LANES, WIDTH), 0)
    c = lax.broadcasted_iota(jnp.int32, (LANES, WIDTH), 1)
    e = ((c >> 6) == r).astype(BF16)
    r2 = lax.broadcasted_iota(jnp.int32, (WIDTH, LANES), 0)
    c2 = lax.broadcasted_iota(jnp.int32, (WIDTH, LANES), 1)
    et = ((r2 >> 6) == c2).astype(BF16)
    return e, et


def _split3(x):
    hi = x.astype(BF16)
    r = x - hi.astype(F32)
    mid = r.astype(BF16)
    return hi, mid, (r - mid.astype(F32)).astype(BF16)


def _hexp(w, e):
    return sum(_nn(part, e) for part in _split3(w))


def _hsum(x, et):
    return sum(_nn(part, et) for part in _split3(x))


def _perm_matrix(rows, d, transpose):
    i = np.arange(rows)
    j = (i % (rows // d)) * d + i // (rows // d)
    p = np.zeros((rows, rows), np.float32)
    p[i, j] = 1.0
    return jnp.asarray(p.T if transpose else p, BF16)


def _permute_f32(p, x):
    return sum(_nn(p, part) for part in _split3(x))


def _store_classes(ref, y, d):
    n = y.shape[0] // d
    for r in range(d):
        ref[r] = y[r * n:(r + 1) * n, :]


def _load_classes(ref, d):
    return jnp.concatenate([ref[r] for r in range(d)], axis=0)


def _rope_tabs(pos_ref, fr_ref, sign):
    ang = pos_ref[...].astype(F32) * fr_ref[...]
    lane = lax.broadcasted_iota(jnp.int32, ang.shape, 1) & (HEAD_DIM - 1)
    m1 = lane < ROPE_DIMS // 2
    m2 = (lane >= ROPE_DIMS // 2) & (lane < ROPE_DIMS)
    cos = jnp.cos(ang)
    sin = jnp.sin(ang) * sign
    return (jnp.where(m1 | m2, cos, 1.0), jnp.where(m1, -sin, 0.0), jnp.where(m2, sin, 0.0))


def _rope(z, tabs):
    c, s1, s2 = tabs
    parts = []
    for p in range(z.shape[1] // LANES):
        zp = z[:, LANES * p:LANES * (p + 1)]
        parts.append(zp * c + pltpu.roll(zp, LANES - 8, 1) * s1 + pltpu.roll(zp, 8, 1) * s2)
    return jnp.concatenate(parts, axis=1)


def _half_masks(rows):
    lane = lax.broadcasted_iota(jnp.int32, (rows, LANES), 1)
    lo = lane < HEAD_DIM
    return lo, jnp.logical_not(lo)


def _layer_norm_bwd(dxh, xh, rstd):
    m1 = jnp.mean(dxh, axis=1, keepdims=True)
    m2 = jnp.mean(dxh * xh, axis=1, keepdims=True)
    return rstd * (dxh - m1 - xh * m2)


def _coords():
    return lax.axis_index("x"), lax.axis_index("y"), lax.axis_index("c")


def _peer(x, y, c, k):
    return (1 - x if k & 4 else x, 1 - y if k & 2 else y, 1 - c if k & 1 else c)


def _comm_sems(n):
    return [pltpu.SemaphoreType.DMA((N_DEV - 1, n)), pltpu.SemaphoreType.DMA((N_DEV - 1, n)),
            pltpu.SemaphoreType.DMA((n,))]


def _comm_copies(ins, outs, to_all, sems):
    send_sems, recv_sems, local_sems = sems
    x, y, c = _coords()
    me = 4 * x + 2 * y + c
    copies = [pltpu.make_async_copy(ins[t] if to_all[t] else ins[t].at[me], outs[t].at[me], local_sems.at[t])
              for t in range(len(ins))]
    for k in range(1, N_DEV):
        px, py, pc = _peer(x, y, c, k)
        dest = 4 * px + 2 * py + pc
        for t in range(len(ins)):
            copies.append(pltpu.make_async_remote_copy(
                src_ref=ins[t] if to_all[t] else ins[t].at[dest], dst_ref=outs[t].at[me],
                send_sem=send_sems.at[k - 1, t], recv_sem=recv_sems.at[k - 1, t],
                device_id=(px, py, pc), device_id_type=MESH))
    return copies


def _comm_out_shapes(ins, to_all):
    return [jax.ShapeDtypeStruct(((N_DEV,) + a.shape) if ta else a.shape, a.dtype) for a, ta in zip(ins, to_all)]


def _exchange(ins, to_all, name):
    n = len(ins)

    def body(*refs):
        copies = _comm_copies(refs[:n], refs[n:2 * n], to_all, refs[2 * n:])
        for cp in copies:
            cp.start()
        for cp in copies:
            cp.wait()

    return pl.pallas_call(
        body, name=name, out_shape=_comm_out_shapes(ins, to_all), in_specs=[ANY] * n, out_specs=[ANY] * n,
        scratch_shapes=_comm_sems(n),
    )(*ins)


def _adamw(parts, w, m, v, rows, name):
    n_parts, r_all, cols = parts.shape
    c1 = 1.0 - ADAM_B1 ** ADAM_STEP
    c2 = 1.0 - ADAM_B2 ** ADAM_STEP

    def body(p_ref, w_ref, m_ref, v_ref, g_ref, d_ref, mo_ref, vo_ref):
        g = p_ref[0].astype(F32)
        for s in range(1, n_parts):
            g = g + p_ref[s].astype(F32)
        mn = ADAM_B1 * m_ref[...] + (1.0 - ADAM_B1) * g
        vn = ADAM_B2 * v_ref[...] + (1.0 - ADAM_B2) * (g * g)
        m_hat = mn / c1
        v_hat = vn / c2
        g_ref[...] = g
        d_ref[...] = -ADAM_LR * (m_hat / (jnp.sqrt(v_hat) + ADAM_EPS) + ADAM_WD * w_ref[...])
        mo_ref[...] = mn
        vo_ref[...] = vn

    spec = pl.BlockSpec((rows, cols), lambda i: (i, 0))
    return pl.pallas_call(
        body, name=name, grid=(r_all // rows,),
        in_specs=[pl.BlockSpec((n_parts, rows, cols), lambda i: (0, i, 0)), spec, spec, spec],
        out_specs=[spec] * 4, out_shape=[jax.ShapeDtypeStruct((r_all, cols), F32)] * 4,
        compiler_params=_params(),
    )(parts, w, m, v)


def _matmul(a, w, transposed_w, out_dtype, tm, chunk, name):
    t_all, k = a.shape
    n = w.shape[0] if transposed_w else w.shape[1]

    def body(a_ref, w_ref, o_ref):
        av = a_ref[...]
        for j in range(n // chunk):
            cs = slice(j * chunk, (j + 1) * chunk)
            r = _nt(av, w_ref[cs, :]) if transposed_w else _nn(av, w_ref[:, cs])
            o_ref[:, cs] = r.astype(out_dtype)

    return pl.pallas_call(
        body, name=name, grid=(t_all // tm,),
        in_specs=[pl.BlockSpec((tm, k), lambda i: (i, 0)), pl.BlockSpec(w.shape, lambda i: (0, 0))],
        out_specs=pl.BlockSpec((tm, n), lambda i: (i, 0)),
        out_shape=jax.ShapeDtypeStruct((t_all, n), out_dtype), compiler_params=_params(),
    )(a, w)


def _matmul_tn(a, b, chunk, tk, name):
    t_all, k1 = a.shape
    n = b.shape[1]

    def body(a_ref, b_ref, o_ref):
        @pl.when(pl.program_id(0) == 0)
        def _():
            o_ref[...] = jnp.zeros_like(o_ref)
        at = a_ref[...].astype(F32).T.astype(BF16)
        for j in range(0, n, chunk):
            cs = slice(j, min(j + chunk, n))
            o_ref[:, cs] += _nn(at, b_ref[:, cs])

    return pl.pallas_call(
        body, name=name, grid=(t_all // tk,),
        in_specs=[pl.BlockSpec((tk, k1), lambda t: (t, 0)), pl.BlockSpec((tk, n), lambda t: (t, 0))],
        out_specs=pl.BlockSpec((k1, n), lambda t: (0, 0)),
        out_shape=jax.ShapeDtypeStruct((k1, n), F32), compiler_params=_params(),
    )(a, b)


def _matmul_rows(a, b, tk, name):
    r, t_all = a.shape
    n = b.shape[1]

    def body(a_ref, b_ref, o_ref):
        @pl.when(pl.program_id(0) == 0)
        def _():
            o_ref[...] = jnp.zeros_like(o_ref)
        o_ref[...] += _nn(a_ref[...], b_ref[...])

    return pl.pallas_call(
        body, name=name, grid=(t_all // tk,),
        in_specs=[pl.BlockSpec((r, tk), lambda t: (0, t)), pl.BlockSpec((tk, n), lambda t: (t, 0))],
        out_specs=pl.BlockSpec((r, n), lambda t: (0, 0)),
        out_shape=jax.ShapeDtypeStruct((r, n), F32), compiler_params=_params(),
    )(a, b)


def _ada_fwd(c_all, w_ada, b_ada):
    whole = lambda a: pl.BlockSpec(a.shape, lambda j: (0, 0))

    def body(c_ref, w_ref, b_ref, o_ref):
        cv = c_ref[...]
        s = (cv * jax.nn.sigmoid(cv)).astype(BF16)
        o_ref[...] = _nn(s, w_ref[...].astype(BF16)) + b_ref[...]

    out = jax.ShapeDtypeStruct((c_all.shape[0], w_ada.shape[1]), F32)
    return pl.pallas_call(
        body, name="ada_fwd", grid=(1,), in_specs=[whole(c_all), whole(w_ada), whole(b_ada)], out_specs=whole(out),
        out_shape=out, compiler_params=_params(),
    )(c_all, w_ada, b_ada)


def _ada_bwd(c_all, dada):
    whole = lambda a: pl.BlockSpec(a.shape, lambda j: (0, 0))

    def body(c_ref, d_ref, o_ref):
        cv = c_ref[...]
        s = (cv * jax.nn.sigmoid(cv)).astype(BF16)
        o_ref[...] = _tn(s, d_ref[...].astype(BF16))

    out = jax.ShapeDtypeStruct((D_MODEL, dada.shape[1]), F32)
    return pl.pallas_call(
        body, name="ada_bwd", grid=(1,), in_specs=[whole(c_all), whole(dada)], out_specs=whole(out), out_shape=out,
        compiler_params=_params(),
    )(c_all, dada)


TOK_TM = 256
DILATIONS = (1, 4, 16)


def _class_spec(d, width, nts):
    return pl.BlockSpec((d, TOK_TM // d, width), lambda i: (i // nts, i % nts, 0))


def _class_shape(t_all, seq, d, width, dtype):
    return jax.ShapeDtypeStruct((t_all // seq * d, seq // d, width), dtype)


def _inproj(x, ada3, pos, wqkv, wf16, freq, perms, seq):
    t_all = x.shape[0]
    tm = TOK_TM
    nts = seq // tm

    def body(x_ref, ada_ref, pos_ref, w_ref, wf_ref, fr_ref, p4_ref, p16_ref, h1_ref, za_ref, zb_ref, zb4_ref,
             zb16_ref, vt_ref, fa_ref):
        h1 = (x_ref[...] * (1.0 + ada_ref[0, 1:2, :]) + ada_ref[0, 0:1, :]).astype(BF16)
        h1_ref[...] = h1
        tabs = _rope_tabs(pos_ref, fr_ref, 1.0)
        for n in range(6):
            z = _nn(h1, w_ref[:, n * WIDTH:(n + 1) * WIDTH])
            if n in (3, 4):
                z = _rope(z, tabs)
            if n in (0, 3):
                z = z * Q_SCALE
            if n == 2:
                vt_ref[...] = z.T.astype(BF16)
            dst = za_ref if n < 3 else zb_ref
            dst[:, (n % 3) * WIDTH:(n % 3 + 1) * WIDTH] = z.astype(BF16)
        fa_ref[...] = _nt(wf_ref[...], h1)[:N_HEADS]
        zb = zb_ref[...]
        _store_classes(zb4_ref, _nn(p4_ref[...], zb).astype(BF16), 4)
        _store_classes(zb16_ref, _nn(p16_ref[...], zb).astype(BF16), 16)

    tok = lambda w: pl.BlockSpec((tm, w), lambda i: (i, 0))
    whole = lambda a: pl.BlockSpec(a.shape, lambda i: (0, 0))
    return pl.pallas_call(
        body, name="inproj", grid=(t_all // tm,),
        in_specs=[tok(D_MODEL), pl.BlockSpec((1, 6, D_MODEL), lambda i: (i // nts, 0, 0)), tok(1), whole(wqkv),
                  whole(wf16), pl.BlockSpec((1, LANES), lambda i: (0, 0)), whole(perms[0]), whole(perms[1])],
        out_specs=[tok(D_MODEL), tok(3 * WIDTH), tok(3 * WIDTH), _class_spec(4, 3 * WIDTH, nts),
                   _class_spec(16, 3 * WIDTH, nts), pl.BlockSpec((WIDTH, tm), lambda i: (i // nts, i % nts)),
                   pl.BlockSpec((N_HEADS, tm), lambda i: (0, i))],
        out_shape=[jax.ShapeDtypeStruct((t_all, D_MODEL), BF16), jax.ShapeDtypeStruct((t_all, 3 * WIDTH), BF16),
                   jax.ShapeDtypeStruct((t_all, 3 * WIDTH), BF16), _class_shape(t_all, seq, 4, 3 * WIDTH, BF16),
                   _class_shape(t_all, seq, 16, 3 * WIDTH, BF16),
                   jax.ShapeDtypeStruct((t_all // seq * WIDTH, seq), BF16),
                   jax.ShapeDtypeStruct((N_HEADS, t_all), F32)],
        compiler_params=_params(),
    )(x, ada3, pos, wqkv, wf16, freq, perms[0], perms[1])


def _fgate_fwd(fa_t, bf, seq):
    t_all = fa_t.shape[1]

    def body(fa_ref, b_ref, f_ref):
        lane = lax.broadcasted_iota(jnp.int32, (N_HEADS, LANES), 1)

        def chunk(j, carry):
            sl = pl.ds(pl.multiple_of(j * LANES, LANES), LANES)
            xv = fa_ref[:, sl] + b_ref[...]
            lf = jnp.minimum(xv, 0.0) - jnp.log(1.0 + jnp.exp(-jnp.abs(xv)))
            for s in (1, 2, 4, 8, 16, 32, 64):
                lf = lf + jnp.where(lane >= s, pltpu.roll(lf, s, 1), 0.0)
            lf = lf + carry
            f_ref[:, sl] = lf
            return lf[:, LANES - 1:LANES]

        lax.fori_loop(0, seq // LANES, chunk, jnp.zeros((N_HEADS, 1), F32))

    return pl.pallas_call(
        body, name="fgate_fwd", grid=(t_all // seq,),
        in_specs=[pl.BlockSpec((N_HEADS, seq), lambda b: (0, b)), pl.BlockSpec((N_HEADS, 1), lambda b: (0, 0))],
        out_specs=pl.BlockSpec((N_HEADS, seq), lambda b: (0, b)),
        out_shape=jax.ShapeDtypeStruct((N_HEADS, t_all), F32), compiler_params=_params(),
    )(fa_t, bf)


def _fgate_bwd(df_t, fa_t, bf, seq):
    t_all = fa_t.shape[1]

    def body(df_ref, fa_ref, b_ref, o_ref, s_ref):
        lane = lax.broadcasted_iota(jnp.int32, (N_HEADS, LANES), 1)

        @pl.when(pl.program_id(0) == 0)
        def _():
            s_ref[...] = jnp.zeros_like(s_ref)

        def chunk(jj, carry):
            car, tot = carry
            j = seq // LANES - 1 - jj
            sl = pl.ds(pl.multiple_of(j * LANES, LANES), LANES)
            d = df_ref[:, sl]
            for s in (1, 2, 4, 8, 16, 32, 64):
                d = d + jnp.where(lane < LANES - s, pltpu.roll(d, LANES - s, 1), 0.0)
            d = d + car
            dfa = d * jax.nn.sigmoid(-(fa_ref[:, sl] + b_ref[...]))
            o_ref[:, sl] = dfa
            return d[:, 0:1], tot + jnp.sum(dfa, axis=1, keepdims=True)

        z = jnp.zeros((N_HEADS, 1), F32)
        _, tot = lax.fori_loop(0, seq // LANES, chunk, (z, z))
        s_ref[...] += jnp.broadcast_to(tot, (N_HEADS, LANES))

    row = pl.BlockSpec((N_HEADS, seq), lambda b: (0, b))
    return pl.pallas_call(
        body, name="fgate_bwd", grid=(t_all // seq,),
        in_specs=[row, row, pl.BlockSpec((N_HEADS, 1), lambda b: (0, 0))],
        out_specs=[row, pl.BlockSpec((N_HEADS, LANES), lambda b: (0, 0))],
        out_shape=[jax.ShapeDtypeStruct((N_HEADS, t_all), F32), jax.ShapeDtypeStruct((N_HEADS, LANES), F32)],
        compiler_params=_params(),
    )(df_t, fa_t, bf)


FOX_T = 256


def _fox_prep(dst, src_ref, lo, hi):
    for p in range(4):
        v = src_ref[:, LANES * p:LANES * (p + 1)]
        dst[2 * p] = jnp.where(lo, v, jnp.zeros_like(v))
        dst[2 * p + 1] = jnp.where(hi, v, jnp.zeros_like(v))


def _fox_fwd(za, vt, f_col, seq, shards):
    t_all = za.shape[0]
    tq = FOX_T
    nq = seq // tq
    nbat = t_all // seq
    n = len(shards)
    to_all = [True] * n

    def body(*refs):
        q_ref, k_ref, vt_ref, fc_ref = refs[:4]
        o_ref, lse_ref = refs[4 + n:6 + n]
        qm_sc, m_sc, l_sc, acc_sc, a_sc, st_sc, pe_sc = refs[6 + 2 * n:13 + 2 * n]
        comm = (refs[4:4 + n], refs[6 + n:6 + 2 * n], to_all, refs[13 + 2 * n:])
        i = pl.program_id(1)

        @pl.when((pl.program_id(0) == 0) & (i == 0))
        def _():
            for cp in _comm_copies(*comm):
                cp.start()
        lo, hi = _half_masks(tq)
        r = lax.broadcasted_iota(jnp.int32, (tq, tq), 0)
        c = lax.broadcasted_iota(jnp.int32, (tq, tq), 1)
        tri = c >= r
        _fox_prep(qm_sc, q_ref, lo, hi)
        m_sc[...] = jnp.full(m_sc.shape, NEG, F32)
        l_sc[...] = jnp.zeros_like(l_sc)
        acc_sc[...] = jnp.zeros_like(acc_sc)

        def block(j, masked):
            sl = pl.ds(pl.multiple_of(j * tq, tq), tq)
            for p in range(4):
                kj = k_ref[sl, LANES * p:LANES * (p + 1)]
                for h in (2 * p, 2 * p + 1):
                    st = _nt(kj, qm_sc[h]) - fc_ref[sl, h:h + 1]
                    st_sc[h] = jnp.where(tri, st, NEG) if masked else st
            for h in range(N_HEADS):
                st = st_sc[h]
                m = m_sc[h:h + 1, :]
                mn = jnp.maximum(m, jnp.max(st, axis=0, keepdims=True))
                a = jnp.exp(m - mn)
                pe = jnp.exp(st - mn)
                m_sc[h:h + 1, :] = mn
                a_sc[h:h + 1, :] = a
                l_sc[h:h + 1, :] = a * l_sc[h:h + 1, :] + jnp.sum(pe, axis=0, keepdims=True)
                pe_sc[h] = pe.astype(BF16)
            for h in range(N_HEADS):
                acc_sc[h] = a_sc[h:h + 1, :] * acc_sc[h] + _nn(vt_ref[HEAD_DIM * h:HEAD_DIM * (h + 1), sl], pe_sc[h])

        def step(j, carry):
            block(j, False)
            return carry

        lax.fori_loop(0, i, step, 0)
        block(i, True)
        lse_ref[...] = m_sc[...] + jnp.log(l_sc[...])
        for p in range(4):
            ot = jnp.concatenate([acc_sc[h] / l_sc[h:h + 1, :] for h in (2 * p, 2 * p + 1)], axis=0)
            o_ref[:, LANES * p:LANES * (p + 1)] = ot.T

        @pl.when((pl.program_id(0) == nbat - 1) & (i == nq - 1))
        def _():
            for cp in _comm_copies(*comm):
                cp.wait()

    res = pl.pallas_call(
        body, name="fox_fwd", grid=(nbat, nq),
        in_specs=[pl.BlockSpec((tq, WIDTH), lambda b, i: (b * nq + i, 0)),
                  pl.BlockSpec((seq, WIDTH), lambda b, i: (b, 1)), pl.BlockSpec((WIDTH, seq), lambda b, i: (b, 0)),
                  pl.BlockSpec((seq, LANES), lambda b, i: (b, 0))] + [ANY] * n,
        out_specs=[pl.BlockSpec((tq, WIDTH), lambda b, i: (b * nq + i, 0)),
                   pl.BlockSpec((N_HEADS, tq), lambda b, i: (0, b * nq + i))] + [ANY] * n,
        out_shape=[jax.ShapeDtypeStruct((t_all, WIDTH), F32), jax.ShapeDtypeStruct((N_HEADS, t_all), F32)]
        + _comm_out_shapes(shards, to_all),
        scratch_shapes=[pltpu.VMEM((N_HEADS, tq, LANES), BF16), pltpu.VMEM((N_HEADS, tq), F32),
                        pltpu.VMEM((N_HEADS, tq), F32), pltpu.VMEM((N_HEADS, HEAD_DIM, tq), F32),
                        pltpu.VMEM((N_HEADS, tq), F32), pltpu.VMEM((N_HEADS, tq, tq), F32),
                        pltpu.VMEM((N_HEADS, tq, tq), BF16)] + _comm_sems(n),
        compiler_params=_params(),
    )(za, za, vt, f_col, *shards)
    return res[0], res[1], res[2:]


def _fox_bwd(za, do, f_col, lse_row, dl_row, seq, grads):
    t_all = za.shape[0]
    tk = FOX_T
    nk = seq // tk
    nbat = t_all // seq
    n = len(grads)
    to_all = [False] * n

    def body(*refs):
        k_ref, v_ref, q_ref, do_ref, fc_ref, lr_ref, dr_ref = refs[:7]
        dk_ref, dv_ref, df_ref, dqt_ref, dfq_ref = refs[7 + n:12 + n]
        km_sc, vm_sc, fk_sc, dk_sc, dv_sc, cs_sc, kt_sc, st_sc, dp_sc, pt_sc, ds_sc = refs[12 + 2 * n:23 + 2 * n]
        comm = (refs[7:7 + n], refs[12 + n:12 + 2 * n], to_all, refs[23 + 2 * n:])
        j = pl.program_id(1)

        @pl.when(j == 0)
        def _():
            dqt_ref[...] = jnp.zeros_like(dqt_ref)
            dfq_ref[...] = jnp.zeros_like(dfq_ref)

        @pl.when((pl.program_id(0) == 0) & (j == 0))
        def _():
            for cp in _comm_copies(*comm):
                cp.start()
        lo, hi = _half_masks(tk)
        r = lax.broadcasted_iota(jnp.int32, (tk, tk), 0)
        c = lax.broadcasted_iota(jnp.int32, (tk, tk), 1)
        tri = c >= r
        _fox_prep(km_sc, k_ref, lo, hi)
        _fox_prep(vm_sc, v_ref, lo, hi)
        for h in range(N_HEADS):
            fk_sc[h] = jnp.broadcast_to(fc_ref[:, h:h + 1], (tk, tk))
        for p in range(4):
            kt_sc[p] = k_ref[:, LANES * p:LANES * (p + 1)].astype(F32).T.astype(BF16)
        dk_sc[...] = jnp.zeros_like(dk_sc)
        dv_sc[...] = jnp.zeros_like(dv_sc)
        cs_sc[...] = jnp.zeros_like(cs_sc)

        def block(i, masked):
            sl = pl.ds(pl.multiple_of(i * tk, tk), tk)
            for p in range(4):
                cs = slice(LANES * p, LANES * (p + 1))
                qi = q_ref[sl, cs]
                doi = do_ref[sl, cs]
                for h in (2 * p, 2 * p + 1):
                    st = _nt(km_sc[h], qi) - fk_sc[h] - lr_ref[h:h + 1, sl]
                    st_sc[h] = jnp.where(tri, st, NEG) if masked else st
                    dp_sc[h] = _nt(vm_sc[h], doi) - dr_ref[h:h + 1, sl]
            for h in range(N_HEADS):
                pt = jnp.exp(st_sc[h])
                dst = pt * dp_sc[h]
                pt_sc[h] = pt.astype(BF16)
                ds_sc[h] = dst.astype(BF16)
                cs_sc[h] += dst[:, :LANES] + dst[:, LANES:]
                dfq_ref[h:h + 1, sl] += jnp.sum(dst, axis=0, keepdims=True)
            for p in range(4):
                cs = slice(LANES * p, LANES * (p + 1))
                qi = q_ref[sl, cs]
                doi = do_ref[sl, cs]
                for h in (2 * p, 2 * p + 1):
                    dv_sc[h] += _nn(pt_sc[h], doi)
                    dk_sc[h] += _nn(ds_sc[h], qi)
                    kt = kt_sc[p, HEAD_DIM * (h % 2):HEAD_DIM * (h % 2 + 1), :]
                    dqt_ref[HEAD_DIM * h:HEAD_DIM * (h + 1), sl] += _nn(kt, ds_sc[h])

        def step(i, carry):
            block(i, False)
            return carry

        block(j, True)
        lax.fori_loop(j + 1, nk, step, 0)
        df_ref[...] = jnp.zeros_like(df_ref)
        for p in range(4):
            cs = slice(LANES * p, LANES * (p + 1))
            dk_ref[:, cs] = jnp.where(lo, dk_sc[2 * p], dk_sc[2 * p + 1]).astype(BF16)
            dv_ref[:, cs] = jnp.where(lo, dv_sc[2 * p], dv_sc[2 * p + 1]).astype(BF16)
            for h in (2 * p, 2 * p + 1):
                df_ref[:, h:h + 1] = -jnp.sum(cs_sc[h], axis=1, keepdims=True)

        @pl.when(j == nk - 1)
        def _():
            dqt_ref[...] = dqt_ref[...] * Q_SCALE

        @pl.when((pl.program_id(0) == nbat - 1) & (j == nk - 1))
        def _():
            for cp in _comm_copies(*comm):
                cp.wait()

    tile = lambda w, col: pl.BlockSpec((tk, w), lambda b, j: (b * nk + j, col))
    full = lambda col: pl.BlockSpec((seq, WIDTH), lambda b, j: (b, col))
    row = pl.BlockSpec((N_HEADS, seq), lambda b, j: (0, b))
    acc = pltpu.VMEM((N_HEADS, tk, LANES), F32)
    res = pl.pallas_call(
        body, name="fox_bwd", grid=(nbat, nk),
        in_specs=[tile(WIDTH, 1), tile(WIDTH, 2), full(0), full(0), tile(LANES, 0), row, row] + [ANY] * n,
        out_specs=[tile(WIDTH, 0), tile(WIDTH, 0), tile(LANES, 0), pl.BlockSpec((WIDTH, seq), lambda b, j: (b, 0)),
                   row] + [ANY] * n,
        out_shape=[jax.ShapeDtypeStruct((t_all, WIDTH), BF16), jax.ShapeDtypeStruct((t_all, WIDTH), BF16),
                   jax.ShapeDtypeStruct((t_all, LANES), F32), jax.ShapeDtypeStruct((nbat * WIDTH, seq), F32),
                   jax.ShapeDtypeStruct((N_HEADS, t_all), F32)] + _comm_out_shapes(grads, to_all),
        scratch_shapes=[pltpu.VMEM((N_HEADS, tk, LANES), BF16), pltpu.VMEM((N_HEADS, tk, LANES), BF16),
                        pltpu.VMEM((N_HEADS, tk, tk), F32), acc, acc, acc, pltpu.VMEM((4, LANES, tk), BF16),
                        pltpu.VMEM((N_HEADS, tk, tk), F32), pltpu.VMEM((N_HEADS, tk, tk), F32),
                        pltpu.VMEM((N_HEADS, tk, tk), BF16), pltpu.VMEM((N_HEADS, tk, tk), BF16)]
        + _comm_sems(n),
        compiler_params=_params(),
    )(za, za, za, do, f_col, lse_row, dl_row, *grads)
    return res[0], res[1], res[2], res[3], res[4], res[5:]


DIL_SUB = 4


def _dil_mask(has_prev):
    qi = lax.broadcasted_iota(jnp.int32, (BLK, 2 * BLK), 0)
    kj = lax.broadcasted_iota(jnp.int32, (BLK, 2 * BLK), 1)
    dist = qi + BLK - kj
    band = (dist >= 0) & (dist <= BLK)
    return band if has_prev is True else band & ((kj >= BLK) | has_prev)


def _dil_geometry(t_all, seq, d):
    length = seq // d
    nbs = length // BLK
    sub = min(DIL_SUB, nbs)
    spb = nbs // sub
    tile = lambda width, col: pl.BlockSpec((BLK * sub, width), lambda s: (s, col))
    whole = lambda width, col: pl.BlockSpec((length, width), lambda s: (s // spb, col))
    return nbs, sub, spb, t_all // (BLK * sub), tile, whole


def _blk(i):
    return pl.ds(pl.multiple_of(i * BLK, BLK), BLK)


def _dil_fwd(zb, seq, d):
    t_all = zb.shape[0]
    nbs, sub, spb, steps, tile, whole = _dil_geometry(t_all, seq, d)

    def body(q_ref, k_ref, v_ref, o_ref, lse_ref, s_sc, p_sc):
        first = (pl.program_id(0) % spb) * sub
        lo, hi = _half_masks(BLK)
        lse_ref[...] = jnp.zeros_like(lse_ref)
        for j in range(sub):
            blk = first + j
            mask = _dil_mask(blk != 0 if j == 0 else True)
            for p in range(4):
                cs = slice(LANES * p, LANES * (p + 1))
                qp = q_ref[BLK * j:BLK * (j + 1), cs]
                kcat = jnp.concatenate([k_ref[_blk(jnp.maximum(blk - 1, 0)), cs], k_ref[_blk(blk), cs]], axis=0)
                for e in (0, 1):
                    qe = jnp.where(lo if e == 0 else hi, qp, jnp.zeros_like(qp))
                    s_sc[N_HEADS * j + 2 * p + e] = jnp.where(mask, _nt(qe, kcat), NEG)
        inv = []
        for i in range(N_HEADS * sub):
            s = s_sc[i]
            m = jnp.max(s, axis=1, keepdims=True)
            pe = jnp.exp(s - m)
            l = jnp.sum(pe, axis=1, keepdims=True)
            p_sc[i] = pe.astype(BF16)
            inv.append(1.0 / l)
            j, h = divmod(i, N_HEADS)
            lse_ref[BLK * j:BLK * (j + 1), h:h + 1] = m + jnp.log(l)
        for j in range(sub):
            blk = first + j
            for p in range(4):
                cs = slice(LANES * p, LANES * (p + 1))
                vcat = jnp.concatenate([v_ref[_blk(jnp.maximum(blk - 1, 0)), cs], v_ref[_blk(blk), cs]], axis=0)
                res = [_nn(p_sc[N_HEADS * j + h], vcat) * inv[N_HEADS * j + h] for h in (2 * p, 2 * p + 1)]
                o_ref[BLK * j:BLK * (j + 1), cs] = jnp.where(lo, res[0], res[1])

    return pl.pallas_call(
        body, name=f"dil_fwd_{d}", grid=(steps,), in_specs=[tile(WIDTH, 0), whole(WIDTH, 1), whole(WIDTH, 2)],
        out_specs=[tile(WIDTH, 0), tile(LANES, 0)],
        out_shape=[jax.ShapeDtypeStruct((t_all, WIDTH), F32), jax.ShapeDtypeStruct((t_all, LANES), F32)],
        scratch_shapes=[pltpu.VMEM((N_HEADS * sub, BLK, 2 * BLK), F32),
                        pltpu.VMEM((N_HEADS * sub, BLK, 2 * BLK), BF16)],
        compiler_params=_params(),
    )(zb, zb, zb)


def _dil_bwd_dq(zb, do, lse, dl, seq, d):
    t_all = zb.shape[0]
    nbs, sub, spb, steps, tile, whole = _dil_geometry(t_all, seq, d)

    def body(q_ref, k_ref, v_ref, do_ref, lse_ref, dl_ref, dq_ref, s_sc, dp_sc, ds_sc):
        first = (pl.program_id(0) % spb) * sub
        lo, hi = _half_masks(BLK)
        for j in range(sub):
            blk = first + j
            rows = slice(BLK * j, BLK * (j + 1))
            mask = _dil_mask(blk != 0 if j == 0 else True)
            prev = _blk(jnp.maximum(blk - 1, 0))
            for p in range(4):
                cs = slice(LANES * p, LANES * (p + 1))
                qp = q_ref[rows, cs]
                dop = do_ref[rows, cs]
                kcat = jnp.concatenate([k_ref[prev, cs], k_ref[_blk(blk), cs]], axis=0)
                vcat = jnp.concatenate([v_ref[prev, cs], v_ref[_blk(blk), cs]], axis=0)
                for e in (0, 1):
                    h = 2 * p + e
                    sel = lo if e == 0 else hi
                    qe = jnp.where(sel, qp, jnp.zeros_like(qp))
                    doe = jnp.where(sel, dop, jnp.zeros_like(dop))
                    s_sc[N_HEADS * j + h] = jnp.where(mask, _nt(qe, kcat) - lse_ref[rows, h:h + 1], NEG)
                    dp_sc[N_HEADS * j + h] = _nt(doe, vcat) - dl_ref[rows, h:h + 1]
        for i in range(N_HEADS * sub):
            ds_sc[i] = (jnp.exp(s_sc[i]) * dp_sc[i]).astype(BF16)
        for j in range(sub):
            blk = first + j
            for p in range(4):
                cs = slice(LANES * p, LANES * (p + 1))
                kcat = jnp.concatenate([k_ref[_blk(jnp.maximum(blk - 1, 0)), cs], k_ref[_blk(blk), cs]], axis=0)
                i = N_HEADS * j + 2 * p
                dq_ref[BLK * j:BLK * (j + 1), cs] = (
                    jnp.where(lo, _nn(ds_sc[i], kcat), _nn(ds_sc[i + 1], kcat)) * Q_SCALE).astype(BF16)

    wide = pltpu.VMEM((N_HEADS * sub, BLK, 2 * BLK), F32)
    return pl.pallas_call(
        body, name=f"dil_bwd_dq_{d}", grid=(steps,),
        in_specs=[tile(WIDTH, 0), whole(WIDTH, 1), whole(WIDTH, 2), tile(WIDTH, 0), tile(LANES, 0), tile(LANES, 0)],
        out_specs=tile(WIDTH, 0), out_shape=jax.ShapeDtypeStruct((t_all, WIDTH), BF16),
        scratch_shapes=[wide, wide, pltpu.VMEM((N_HEADS * sub, BLK, 2 * BLK), BF16)], compiler_params=_params(),
    )(zb, zb, zb, do, lse, dl)


def _dil_bwd_dkv(zb, do, lse, dl, seq, d):
    t_all = zb.shape[0]
    nbs, sub, spb, steps, tile, whole = _dil_geometry(t_all, seq, d)

    def body(k_ref, v_ref, q_ref, do_ref, lse_ref, dl_ref, dk_ref, dv_ref, s_sc, dp_sc, pt_sc, ds_sc):
        first = (pl.program_id(0) % spb) * sub
        r = lax.broadcasted_iota(jnp.int32, (BLK, 2 * BLK), 0)
        c = lax.broadcasted_iota(jnp.int32, (BLK, 2 * BLK), 1)
        same = (c < BLK) & (c >= r)
        later = (c >= BLK) & (c - BLK <= r)
        lo, hi = _half_masks(BLK)
        for j in range(sub):
            blk = first + j
            rows = slice(BLK * j, BLK * (j + 1))
            nxt = _blk(jnp.minimum(blk + 1, nbs - 1))
            mask = same | (later & (blk + 1 != nbs)) if j == sub - 1 else same | later
            lrows = jnp.concatenate([lse_ref[_blk(blk), :].T, lse_ref[nxt, :].T], axis=1)
            erows = jnp.concatenate([dl_ref[_blk(blk), :].T, dl_ref[nxt, :].T], axis=1)
            for p in range(4):
                cs = slice(LANES * p, LANES * (p + 1))
                kp = k_ref[rows, cs]
                vp = v_ref[rows, cs]
                qcat = jnp.concatenate([q_ref[_blk(blk), cs], q_ref[nxt, cs]], axis=0)
                dcat = jnp.concatenate([do_ref[_blk(blk), cs], do_ref[nxt, cs]], axis=0)
                for e in (0, 1):
                    h = 2 * p + e
                    sel = lo if e == 0 else hi
                    ke = jnp.where(sel, kp, jnp.zeros_like(kp))
                    ve = jnp.where(sel, vp, jnp.zeros_like(vp))
                    s_sc[N_HEADS * j + h] = jnp.where(mask, _nt(ke, qcat) - lrows[h:h + 1, :], NEG)
                    dp_sc[N_HEADS * j + h] = _nt(ve, dcat) - erows[h:h + 1, :]
        for i in range(N_HEADS * sub):
            pt = jnp.exp(s_sc[i])
            pt_sc[i] = pt.astype(BF16)
            ds_sc[i] = (pt * dp_sc[i]).astype(BF16)
        for j in range(sub):
            blk = first + j
            rows = slice(BLK * j, BLK * (j + 1))
            nxt = _blk(jnp.minimum(blk + 1, nbs - 1))
            for p in range(4):
                cs = slice(LANES * p, LANES * (p + 1))
                qcat = jnp.concatenate([q_ref[_blk(blk), cs], q_ref[nxt, cs]], axis=0)
                dcat = jnp.concatenate([do_ref[_blk(blk), cs], do_ref[nxt, cs]], axis=0)
                i = N_HEADS * j + 2 * p
                dk_ref[rows, cs] = jnp.where(lo, _nn(ds_sc[i], qcat), _nn(ds_sc[i + 1], qcat)).astype(BF16)
                dv_ref[rows, cs] = jnp.where(lo, _nn(pt_sc[i], dcat), _nn(pt_sc[i + 1], dcat)).astype(BF16)

    wide = pltpu.VMEM((N_HEADS * sub, BLK, 2 * BLK), F32)
    half = pltpu.VMEM((N_HEADS * sub, BLK, 2 * BLK), BF16)
    return pl.pallas_call(
        body, name=f"dil_bwd_dkv_{d}", grid=(steps,),
        in_specs=[tile(WIDTH, 1), tile(WIDTH, 2), whole(WIDTH, 0), whole(WIDTH, 0), whole(LANES, 0), whole(LANES, 0)],
        out_specs=[tile(WIDTH, 0), tile(WIDTH, 0)], out_shape=[jax.ShapeDtypeStruct((t_all, WIDTH), BF16)] * 2,
        scratch_shapes=[wide, wide, half, half], compiler_params=_params(),
    )(zb, zb, zb, do, lse, dl)


def _mix_out(oa, o3, l3, gn_a, gn_b, w_out, x, ada3, ln_g, ln_b, perms, seq):
    t_all = x.shape[0]
    tm = TOK_TM
    nts = seq // tm

    def body(oa_ref, o1_ref, o2_ref, o3_ref, l1_ref, l2_ref, l3_ref, ga_ref, gb_ref, w_ref, x_ref, ada_ref, g_ref,
             b_ref, p4_ref, p16_ref, pt4_ref, pt16_ref, ob_ref, lse_ref, lse4_ref, lse16_ref, mg_ref, mix_ref, xh_ref,
             rs_ref, h2_ref):
        e, et = _head_mats()
        la = l1_ref[...]
        lb = _permute_f32(pt4_ref[...], _load_classes(l2_ref, 4))
        lc = _permute_f32(pt16_ref[...], _load_classes(l3_ref, 16))
        mx = jnp.maximum(jnp.maximum(la, lb), lc)
        ea, eb, ec = jnp.exp(la - mx), jnp.exp(lb - mx), jnp.exp(lc - mx)
        tot = ea + eb + ec
        lse = mx + jnp.log(tot)
        lse_ref[...] = lse
        _store_classes(lse4_ref, _permute_f32(p4_ref[...], lse), 4)
        _store_classes(lse16_ref, _permute_f32(p16_ref[...], lse), 16)
        ob = (o1_ref[...] * _hexp(ea / tot, e)
              + _permute_f32(pt4_ref[...], _load_classes(o2_ref, 4)) * _hexp(eb / tot, e)
              + _permute_f32(pt16_ref[...], _load_classes(o3_ref, 16)) * _hexp(ec / tot, e))
        ob_ref[...] = ob

        def rms(o, gain):
            rr = lax.rsqrt(_hsum(o * o, et) * (1.0 / HEAD_DIM) + RMS_EPS)
            return o * _hexp(rr, e) * gain

        merged = jnp.concatenate([rms(oa_ref[...], ga_ref[...]), rms(ob, gb_ref[...])], axis=1).astype(BF16)
        mg_ref[...] = merged
        mix = _nn(merged, w_ref[...])
        mix_ref[...] = mix.astype(BF16)
        r1 = ALPHA * x_ref[...] + ada_ref[0, 2:3, :] * mix
        d = r1 - jnp.mean(r1, axis=1, keepdims=True)
        rstd = lax.rsqrt(jnp.mean(d * d, axis=1, keepdims=True) + LN_EPS)
        xh = d * rstd
        xh_ref[...] = xh
        rs_ref[...] = jnp.broadcast_to(rstd, (tm, LANES))
        x1 = xh * g_ref[...] + b_ref[...]
        h2_ref[...] = (x1 * (1.0 + ada_ref[0, 4:5, :]) + ada_ref[0, 3:4, :]).astype(BF16)

    tok = lambda w: pl.BlockSpec((tm, w), lambda i: (i, 0))
    vec = lambda w: pl.BlockSpec((1, w), lambda i: (0, 0))
    whole = lambda a: pl.BlockSpec(a.shape, lambda i: (0, 0))
    classes = lambda a, d: a.reshape(t_all // seq * d, seq // d, a.shape[-1])
    return pl.pallas_call(
        body, name="mix_out", grid=(t_all // tm,),
        in_specs=[tok(WIDTH), tok(WIDTH), _class_spec(4, WIDTH, nts), _class_spec(16, WIDTH, nts), tok(LANES),
                  _class_spec(4, LANES, nts), _class_spec(16, LANES, nts), vec(WIDTH), vec(WIDTH), whole(w_out),
                  tok(D_MODEL), pl.BlockSpec((1, 6, D_MODEL), lambda i: (i // nts, 0, 0)), vec(D_MODEL), vec(D_MODEL)]
        + [whole(p) for p in perms],
        out_specs=[tok(WIDTH), tok(LANES), _class_spec(4, LANES, nts), _class_spec(16, LANES, nts), tok(D_MODEL),
                   tok(D_MODEL), tok(D_MODEL), tok(LANES), tok(D_MODEL)],
        out_shape=[jax.ShapeDtypeStruct((t_all, WIDTH), F32), jax.ShapeDtypeStruct((t_all, LANES), F32),
                   _class_shape(t_all, seq, 4, LANES, F32), _class_shape(t_all, seq, 16, LANES, F32),
                   jax.ShapeDtypeStruct((t_all, D_MODEL), BF16), jax.ShapeDtypeStruct((t_all, D_MODEL), BF16),
                   jax.ShapeDtypeStruct((t_all, D_MODEL), F32), jax.ShapeDtypeStruct((t_all, LANES), F32),
                   jax.ShapeDtypeStruct((t_all, D_MODEL), BF16)],
        compiler_params=_params(),
    )(oa, o3[0], classes(o3[1], 4), classes(o3[2], 16), l3[0], classes(l3[1], 4), classes(l3[2], 16), gn_a, gn_b,
      w_out, x, ada3, ln_g, ln_b, *perms)


def _mix_out_bwd(dmix, w_out, oa, ob, gn_a, gn_b, perms, seq):
    t_all = dmix.shape[0]
    tm = TOK_TM
    nts = seq // tm

    def body(dm_ref, w_ref, oa_ref, ob_ref, ga_ref, gb_ref, p4_ref, p16_ref, doa_ref, dob_ref, dob4_ref, dob16_ref,
             dla_ref, dlb_ref, dlb4_ref, dlb16_ref, acc_ref):
        @pl.when(pl.program_id(0) == 0)
        def _():
            acc_ref[...] = jnp.zeros_like(acc_ref)
        e, et = _head_mats()
        dmg = _nt(dm_ref[...], w_ref[...])

        def group(o, dn, gain):
            rr = lax.rsqrt(_hsum(o * o, et) * (1.0 / HEAD_DIM) + RMS_EPS)
            re = _hexp(rr, e)
            dgain = jnp.sum(dn * o * re, axis=0, keepdims=True)
            dxn = dn * gain
            tt = _hsum(dxn * o, et) * (rr * rr * rr) * (1.0 / HEAD_DIM)
            do = re * dxn - o * _hexp(tt, e)
            return do, _hsum(do * o, et), dgain

        doa, dla, dga = group(oa_ref[...], dmg[:, :WIDTH], ga_ref[...])
        dob, dlb, dgb = group(ob_ref[...], dmg[:, WIDTH:], gb_ref[...])
        dob = dob.astype(BF16)
        doa_ref[...] = doa.astype(BF16)
        dob_ref[...] = dob
        _store_classes(dob4_ref, _nn(p4_ref[...], dob).astype(BF16), 4)
        _store_classes(dob16_ref, _nn(p16_ref[...], dob).astype(BF16), 16)
        dla_ref[...] = dla
        dlb_ref[...] = dlb
        _store_classes(dlb4_ref, _permute_f32(p4_ref[...], dlb), 4)
        _store_classes(dlb16_ref, _permute_f32(p16_ref[...], dlb), 16)
        acc_ref[0:1, :] += jnp.concatenate([dga, dgb], axis=1)

    tok = lambda w: pl.BlockSpec((tm, w), lambda i: (i, 0))
    vec = lambda w: pl.BlockSpec((1, w), lambda i: (0, 0))
    return pl.pallas_call(
        body, name="mix_out_bwd", grid=(t_all // tm,),
        in_specs=[tok(D_MODEL), pl.BlockSpec(w_out.shape, lambda i: (0, 0)), tok(WIDTH), tok(WIDTH), vec(WIDTH),
                  vec(WIDTH), pl.BlockSpec(perms[0].shape, lambda i: (0, 0)),
                  pl.BlockSpec(perms[1].shape, lambda i: (0, 0))],
        out_specs=[tok(WIDTH), tok(WIDTH), _class_spec(4, WIDTH, nts), _class_spec(16, WIDTH, nts), tok(LANES),
                   tok(LANES), _class_spec(4, LANES, nts), _class_spec(16, LANES, nts),
                   pl.BlockSpec((8, D_MODEL), lambda i: (0, 0))],
        out_shape=[jax.ShapeDtypeStruct((t_all, WIDTH), BF16), jax.ShapeDtypeStruct((t_all, WIDTH), BF16),
                   _class_shape(t_all, seq, 4, WIDTH, BF16), _class_shape(t_all, seq, 16, WIDTH, BF16),
                   jax.ShapeDtypeStruct((t_all, LANES), F32), jax.ShapeDtypeStruct((t_all, LANES), F32),
                   _class_shape(t_all, seq, 4, LANES, F32), _class_shape(t_all, seq, 16, LANES, F32),
                   jax.ShapeDtypeStruct((8, D_MODEL), F32)],
        compiler_params=_params(),
    )(dmix, w_out, oa, ob, gn_a, gn_b, perms[0], perms[1])


def _inproj_bwd(dqt, dka, dva, dil1, dil4, dil16, dfa16, pos, wqkv, wf16, freq, perms, dr1, x, ada3, seq):
    t_all = x.shape[0]
    tm = TOK_TM
    nts = seq // tm

    def body(dqt_ref, dka_ref, dva_ref, q1_ref, k1_ref, v1_ref, q4_ref, k4_ref, v4_ref, q16_ref, k16_ref, v16_ref,
             dfa_ref, pos_ref, w_ref, wf_ref, fr_ref, pt4_ref, pt16_ref, dr1_ref, x_ref, ada_ref, gx_ref, dz_ref,
             acc_ref):
        i = pl.program_id(0)

        @pl.when(i == 0)
        def _():
            acc_ref[...] = jnp.zeros_like(acc_ref)
        tabs = _rope_tabs(pos_ref, fr_ref, -1.0)
        dz_ref[:, :WIDTH] = dqt_ref[...].T.astype(BF16)
        dz_ref[:, WIDTH:2 * WIDTH] = dka_ref[...]
        dz_ref[:, 2 * WIDTH:3 * WIDTH] = dva_ref[...]
        for t, (n1, n4, n16) in enumerate(((q1_ref, q4_ref, q16_ref), (k1_ref, k4_ref, k16_ref),
                                           (v1_ref, v4_ref, v16_ref))):
            tot = (n1[...].astype(F32) + _nn(pt4_ref[...], _load_classes(n4, 4))
                   + _nn(pt16_ref[...], _load_classes(n16, 16)))
            if t < 2:
                tot = _rope(tot, tabs)
            dz_ref[:, (3 + t) * WIDTH:(4 + t) * WIDTH] = tot.astype(BF16)
        dh1 = _tn(dfa_ref[...], wf_ref[...])
        for n in range(6):
            cs = slice(n * WIDTH, (n + 1) * WIDTH)
            dh1 = dh1 + _nt(dz_ref[:, cs], w_ref[:, cs])
        xv = x_ref[...]
        gx_ref[...] = ALPHA * dr1_ref[...] + dh1 * (1.0 + ada_ref[0, 1:2, :])
        b = i // nts
        acc_ref[pl.ds(b, 1), :] += jnp.sum(dh1 * xv, axis=0, keepdims=True)
        acc_ref[pl.ds(8 + b, 1), :] += jnp.sum(dh1, axis=0, keepdims=True)

    tok = lambda w: pl.BlockSpec((tm, w), lambda i: (i, 0))
    whole = lambda a: pl.BlockSpec(a.shape, lambda i: (0, 0))
    classes = lambda a, d: a.reshape(t_all // seq * d, seq // d, a.shape[-1])
    return pl.pallas_call(
        body, name="inproj_bwd", grid=(t_all // tm,),
        in_specs=[pl.BlockSpec((WIDTH, tm), lambda i: (i // nts, i % nts)), tok(WIDTH), tok(WIDTH)]
        + [tok(WIDTH)] * 3 + [_class_spec(4, WIDTH, nts)] * 3 + [_class_spec(16, WIDTH, nts)] * 3
        + [pl.BlockSpec((16, tm), lambda i: (0, i)), tok(1), whole(wqkv), whole(wf16),
           pl.BlockSpec((1, LANES), lambda i: (0, 0)), whole(perms[2]), whole(perms[3]), tok(D_MODEL), tok(D_MODEL),
           pl.BlockSpec((1, 6, D_MODEL), lambda i: (i // nts, 0, 0))],
        out_specs=[tok(D_MODEL), tok(6 * WIDTH), pl.BlockSpec((16, D_MODEL), lambda i: (0, 0))],
        out_shape=[jax.ShapeDtypeStruct((t_all, D_MODEL), F32), jax.ShapeDtypeStruct((t_all, 6 * WIDTH), BF16),
                   jax.ShapeDtypeStruct((16, D_MODEL), F32)],
        compiler_params=_params(),
    )(dqt, dka, dva, *dil1, *[classes(a, 4) for a in dil4], *[classes(a, 16) for a in dil16], dfa16, pos, wqkv, wf16,
      freq, perms[2], perms[3], dr1, x, ada3)


FFN_TM = 512
FFN_TN = 256
HALO = 8


FFN_CHUNK = 64


def _conv(cat_ref, w_ref, b_ref, start, rows):
    return (b_ref[...] + w_ref[0:1, :] * cat_ref[pl.ds(start + HALO - 2, rows), :]
            + w_ref[1:2, :] * cat_ref[pl.ds(start + HALO - 1, rows), :]
            + w_ref[2:3, :] * cat_ref[pl.ds(start + HALO, rows), :])


def _ffn_gate(u, conv_w, conv_b, seq):
    t_all = u.shape[0]
    tm, tn = FFN_TM, FFN_TN
    nc = D_FF // tn
    nts = seq // tm

    def body(ua_ref, uap_ref, ug_ref, ugp_ref, wa_ref, wg_ref, ba_ref, bg_ref, o_ref, ca_ref, cg_ref):
        first = (pl.program_id(0) % nts) == 0
        zero = jnp.zeros((HALO, tn), F32)
        ca_ref[0:HALO, :] = jnp.where(first, zero, uap_ref[...])
        cg_ref[0:HALO, :] = jnp.where(first, zero, ugp_ref[...])
        ca_ref[HALO:, :] = ua_ref[...]
        cg_ref[HALO:, :] = ug_ref[...]
        for c0 in range(0, tm, FFN_CHUNK):
            ya = _conv(ca_ref, wa_ref, ba_ref, c0, FFN_CHUNK)
            yg = _conv(cg_ref, wg_ref, bg_ref, c0, FFN_CHUNK)
            o_ref[c0:c0 + FFN_CHUNK, :] = (yg * jax.nn.sigmoid(yg) * ya).astype(BF16)

    cur = lambda off: pl.BlockSpec((tm, tn), lambda t, n: (t, n + off))
    prev = lambda off: pl.BlockSpec((HALO, tn), lambda t, n: (jnp.maximum(t * (tm // HALO) - 1, 0), n + off))
    vec = lambda r, off: pl.BlockSpec((r, tn), lambda t, n: (0, n + off))
    return pl.pallas_call(
        body, name="ffn_gate", grid=(t_all // tm, nc),
        in_specs=[cur(0), prev(0), cur(nc), prev(nc), vec(3, 0), vec(3, nc), vec(1, 0), vec(1, nc)],
        out_specs=pl.BlockSpec((tm, tn), lambda t, n: (t, n)),
        out_shape=jax.ShapeDtypeStruct((t_all, D_FF), BF16),
        scratch_shapes=[pltpu.VMEM((tm + HALO, tn), F32)] * 2, compiler_params=_params(),
    )(u, u, u, u, conv_w, conv_w, conv_b, conv_b)


def _ffn_gate_bwd(u, dfi, conv_w, conv_b, seq):
    t_all = u.shape[0]
    tm, tn = FFN_TM, FFN_TN
    nc = D_FF // tn
    nts = seq // tm

    def body(ua_ref, uap_ref, uan_ref, ug_ref, ugp_ref, ugn_ref, df_ref, dfn_ref, wa_ref, wg_ref, ba_ref, bg_ref,
             dua_ref, dug_ref, acca_ref, accg_ref, ca_ref, cg_ref, ya_ref, yg_ref):
        t = pl.program_id(1)
        first = (t % nts) == 0
        last = (t % nts) == nts - 1

        @pl.when(t == 0)
        def _():
            acca_ref[...] = jnp.zeros_like(acca_ref)
            accg_ref[...] = jnp.zeros_like(accg_ref)
        zero = jnp.zeros((HALO, tn), F32)
        for cat, cur, prv, nxt in ((ca_ref, ua_ref, uap_ref, uan_ref), (cg_ref, ug_ref, ugp_ref, ugn_ref)):
            cat[0:HALO, :] = jnp.where(first, zero, prv[...])
            cat[HALO:HALO + tm, :] = cur[...]
            cat[HALO + tm:, :] = nxt[...]
        ch = FFN_CHUNK
        sums = [[jnp.zeros((1, tn), F32) for _ in range(4)] for _ in range(2)]
        for ci, c0 in enumerate(range(0, tm, ch)):
            ya = _conv(ca_ref, wa_ref, ba_ref, c0, ch + HALO)
            yg = _conv(cg_ref, wg_ref, bg_ref, c0, ch + HALO)
            if c0 + ch < tm:
                beyond = df_ref[c0 + ch:c0 + ch + 16, :].astype(F32)[:HALO]
            else:
                beyond = jnp.where(last, 0.0, dfn_ref[...].astype(F32)[:HALO])
            dfe = jnp.concatenate([df_ref[c0:c0 + ch, :].astype(F32), beyond], axis=0)
            sg = jax.nn.sigmoid(yg)
            ya_ref[ci] = dfe * (yg * sg)
            yg_ref[ci] = dfe * ya * (sg * (1.0 + yg * (1.0 - sg)))
            for half, (dy, cat, w_ref, du_ref) in enumerate(((ya_ref, ca_ref, wa_ref, dua_ref),
                                                             (yg_ref, cg_ref, wg_ref, dug_ref))):
                shifted = (dy[ci, pl.ds(2, ch), :], dy[ci, pl.ds(1, ch), :], dy[ci, 0:ch, :])
                du = w_ref[0:1, :] * shifted[0] + w_ref[1:2, :] * shifted[1] + w_ref[2:3, :] * shifted[2]
                du_ref[c0:c0 + ch, :] = du.astype(BF16)
                here = cat[pl.ds(c0 + HALO, ch), :]
                for k in range(3):
                    sums[half][k] += jnp.sum(shifted[k] * here, axis=0, keepdims=True)
                sums[half][3] += jnp.sum(shifted[2], axis=0, keepdims=True)
        for half, acc in enumerate((acca_ref, accg_ref)):
            for k in range(4):
                acc[k:k + 1, :] += sums[half][k]

    nrow = t_all // HALO
    cur = lambda off: pl.BlockSpec((tm, tn), lambda n, t: (t, n + off))
    prev = lambda off: pl.BlockSpec((HALO, tn), lambda n, t: (jnp.maximum(t * (tm // HALO) - 1, 0), n + off))
    nxt = lambda off: pl.BlockSpec((HALO, tn), lambda n, t: (jnp.minimum((t + 1) * (tm // HALO), nrow - 1), n + off))
    vec = lambda r, off: pl.BlockSpec((r, tn), lambda n, t: (0, n + off))
    dcur = pl.BlockSpec((tm, tn), lambda n, t: (t, n))
    dnxt = pl.BlockSpec((16, tn), lambda n, t: (jnp.minimum((t + 1) * (tm // 16), t_all // 16 - 1), n))
    acc = pl.BlockSpec((8, tn), lambda n, t: (0, n))
    return pl.pallas_call(
        body, name="ffn_gate_bwd", grid=(nc, t_all // tm),
        in_specs=[cur(0), prev(0), nxt(0), cur(nc), prev(nc), nxt(nc), dcur, dnxt, vec(3, 0), vec(3, nc), vec(1, 0),
                  vec(1, nc)],
        out_specs=[dcur, dcur, acc, acc],
        out_shape=[jax.ShapeDtypeStruct((t_all, D_FF), BF16), jax.ShapeDtypeStruct((t_all, D_FF), BF16),
                   jax.ShapeDtypeStruct((8, D_FF), F32), jax.ShapeDtypeStruct((8, D_FF), F32)],
        scratch_shapes=[pltpu.VMEM((tm + 2 * HALO, tn), F32)] * 2
        + [pltpu.VMEM((tm // FFN_CHUNK, FFN_CHUNK + HALO, tn), F32)] * 2,
        compiler_params=_params(),
    )(u, u, u, u, u, u, dfi, dfi, conv_w, conv_w, conv_b, conv_b)


def _ffn_down(ffn_in, w_down, xh1, ln1_g, ln1_b, ada3, ln2_g, ln2_b, target, seq):
    t_all = xh1.shape[0]
    tm = 256
    nts = seq // tm

    def body(f_ref, w_ref, xh_ref, g1_ref, b1_ref, ada_ref, g2_ref, b2_ref, tg_ref, dr2_ref, acc_ref):
        i = pl.program_id(0)

        @pl.when(i == 0)
        def _():
            acc_ref[...] = jnp.zeros_like(acc_ref)
        ffn = _nn(f_ref[...], w_ref[...])
        x1 = xh_ref[...] * g1_ref[...] + b1_ref[...]
        r2 = ALPHA * x1 + ada_ref[0, 5:6, :] * ffn
        d = r2 - jnp.mean(r2, axis=1, keepdims=True)
        rstd = lax.rsqrt(jnp.mean(d * d, axis=1, keepdims=True) + LN_EPS)
        xh2 = d * rstd
        diff = xh2 * g2_ref[...] + b2_ref[...] - tg_ref[...]
        dy = diff * (1.0 / D_MODEL)
        dr2 = _layer_norm_bwd(dy * g2_ref[...], xh2, rstd)
        dr2_ref[...] = dr2
        acc_ref[0:1, :] += jnp.sum(dy * xh2, axis=0, keepdims=True)
        acc_ref[1:2, :] += jnp.sum(dy, axis=0, keepdims=True)
        acc_ref[2:3, :] += jnp.sum(diff * diff, axis=0, keepdims=True) * (0.5 / D_MODEL)
        acc_ref[pl.ds(8 + i // nts, 1), :] += jnp.sum(dr2 * ffn, axis=0, keepdims=True)

    tok = lambda w: pl.BlockSpec((tm, w), lambda i: (i, 0))
    vec = pl.BlockSpec((1, D_MODEL), lambda i: (0, 0))
    return pl.pallas_call(
        body, name="ffn_down", grid=(t_all // tm,),
        in_specs=[tok(D_FF), pl.BlockSpec(w_down.shape, lambda i: (0, 0)), tok(D_MODEL), vec, vec,
                  pl.BlockSpec((1, 6, D_MODEL), lambda i: (i // nts, 0, 0)), vec, vec, tok(D_MODEL)],
        out_specs=[tok(D_MODEL), pl.BlockSpec((16, D_MODEL), lambda i: (0, 0))],
        out_shape=[jax.ShapeDtypeStruct((t_all, D_MODEL), F32), jax.ShapeDtypeStruct((16, D_MODEL), F32)],
        compiler_params=_params(),
    )(ffn_in, w_down, xh1, ln1_g, ln1_b, ada3, ln2_g, ln2_b, target)


def _ffn_down_bwd(dr2, ada3, w_down, seq):
    t_all = dr2.shape[0]
    tm = 256
    nts = seq // tm

    def body(d_ref, ada_ref, w_ref, dffn_ref, dfi_ref):
        dffn = (d_ref[...] * ada_ref[0, 5:6, :]).astype(BF16)
        dffn_ref[...] = dffn
        dfi_ref[...] = _nt(dffn, w_ref[...]).astype(BF16)

    tok = lambda w: pl.BlockSpec((tm, w), lambda i: (i, 0))
    return pl.pallas_call(
        body, name="ffn_down_bwd", grid=(t_all // tm,),
        in_specs=[tok(D_MODEL), pl.BlockSpec((1, 6, D_MODEL), lambda i: (i // nts, 0, 0)),
                  pl.BlockSpec(w_down.shape, lambda i: (0, 0))],
        out_specs=[tok(D_MODEL), tok(D_FF)],
        out_shape=[jax.ShapeDtypeStruct((t_all, D_MODEL), BF16), jax.ShapeDtypeStruct((t_all, D_FF), BF16)],
        compiler_params=_params(),
    )(dr2, ada3, w_down)


def _ffn_up_bwd(du_a, du_g, w_up, dr2, xh1, rs1, mix, ada3, ln1_g, ln1_b, seq):
    t_all = dr2.shape[0]
    tm = 256
    nts = seq // tm

    def body(da_ref, dg_ref, w_ref, dr2_ref, xh_ref, rs_ref, mix_ref, ada_ref, g_ref, b_ref, dr1_ref, dmix_ref,
             acc_ref):
        i = pl.program_id(0)

        @pl.when(i == 0)
        def _():
            acc_ref[...] = jnp.zeros_like(acc_ref)
        dh2 = _nt(da_ref[...], w_ref[:, :D_FF]) + _nt(dg_ref[...], w_ref[:, D_FF:])
        xh = xh_ref[...]
        x1 = xh * g_ref[...] + b_ref[...]
        dx1 = ALPHA * dr2_ref[...] + dh2 * (1.0 + ada_ref[0, 4:5, :])
        dr1 = _layer_norm_bwd(dx1 * g_ref[...], xh, rs_ref[:, 0:1])
        dr1_ref[...] = dr1
        dmix_ref[...] = (dr1 * ada_ref[0, 2:3, :]).astype(BF16)
        b = i // nts
        acc_ref[0:1, :] += jnp.sum(dx1 * xh, axis=0, keepdims=True)
        acc_ref[1:2, :] += jnp.sum(dx1, axis=0, keepdims=True)
        acc_ref[pl.ds(8 + b, 1), :] += jnp.sum(dh2 * x1, axis=0, keepdims=True)
        acc_ref[pl.ds(16 + b, 1), :] += jnp.sum(dh2, axis=0, keepdims=True)
        acc_ref[pl.ds(24 + b, 1), :] += jnp.sum(dr1 * mix_ref[...].astype(F32), axis=0, keepdims=True)

    tok = lambda w: pl.BlockSpec((tm, w), lambda i: (i, 0))
    vec = pl.BlockSpec((1, D_MODEL), lambda i: (0, 0))
    return pl.pallas_call(
        body, name="ffn_up_bwd", grid=(t_all // tm,),
        in_specs=[tok(D_FF), tok(D_FF), pl.BlockSpec(w_up.shape, lambda i: (0, 0)), tok(D_MODEL), tok(D_MODEL),
                  tok(LANES), tok(D_MODEL), pl.BlockSpec((1, 6, D_MODEL), lambda i: (i // nts, 0, 0)), vec, vec],
        out_specs=[tok(D_MODEL), tok(D_MODEL), pl.BlockSpec((32, D_MODEL), lambda i: (0, 0))],
        out_shape=[jax.ShapeDtypeStruct((t_all, D_MODEL), F32), jax.ShapeDtypeStruct((t_all, D_MODEL), BF16),
                   jax.ShapeDtypeStruct((32, D_MODEL), F32)],
        compiler_params=_params(),
    )(du_a, du_g, w_up, dr2, xh1, rs1, mix, ada3, ln1_g, ln1_b)


def _rows(a):
    return a[:, :N_HEADS].T


def _rope_freq():
    f = np.float32(ROPE_THETA) ** (-np.arange(0, ROPE_DIMS, 2, dtype=np.float32) / np.float32(ROPE_DIMS))
    return jnp.asarray(np.tile(f.astype(np.float32), LANES // (ROPE_DIMS // 2))[None, :])


def _local_step(x, positions, target, ada3, w_in, b_fgate, gn_a, gn_b, ln1_g, ln1_b, conv_b, ln2_g, ln2_b,
                late_shards):
    nbat, seq, _ = x.shape
    t_all = nbat * seq
    xf = x.reshape(t_all, D_MODEL)
    tg = target.reshape(t_all, D_MODEL)
    pos = positions.reshape(t_all, 1)
    freq = _rope_freq()

    wqkv = jnp.concatenate([w_in[:, :3 * WIDTH], w_in[:, 3 * WIDTH + N_HEADS:]], axis=1)
    wf16 = jnp.zeros((16, D_MODEL), BF16).at[:N_HEADS].set(w_in[:, 3 * WIDTH:3 * WIDTH + N_HEADS].T)
    bf = b_fgate.reshape(N_HEADS, 1)

    perms = [_perm_matrix(TOK_TM, d, tr) for tr in (False, True) for d in DILATIONS[1:]]
    h1, za, zb1, zb4, zb16, vt, fa_t = _inproj(xf, ada3, pos, wqkv, wf16, freq, perms, seq)
    zbs = [zb1, zb4.reshape(t_all, 3 * WIDTH), zb16.reshape(t_all, 3 * WIDTH)]
    f_row = _fgate_fwd(fa_t, bf, seq)
    f_col = jnp.zeros((t_all, LANES), F32).at[:, :N_HEADS].set(f_row.T)
    oa, lse_row_a, gathered = _fox_fwd(za, vt, f_col, seq, [late_shards[n] for n in LATE])
    w_out, w_up, conv_w, w_down = (_full_from_gathered(n, g) for n, g in zip(LATE, gathered))
    o3, l3 = zip(*[_dil_fwd(zb, seq, d) for zb, d in zip(zbs, DILATIONS)])
    ob, lse_b, lse_b4, lse_b16, merged, mix, xh1, rs1, h2 = _mix_out(oa, o3, l3, gn_a, gn_b, w_out, xf, ada3, ln1_g,
                                                                      ln1_b, perms, seq)
    u = _matmul(h2, w_up, False, F32, 256, 512, "ffn_up")
    ffn_in = _ffn_gate(u, conv_w, conv_b, seq)
    dr2, acc2 = _ffn_down(ffn_in, w_down, xh1, ln1_g, ln1_b, ada3, ln2_g, ln2_b, tg, seq)

    dffn, dfi = _ffn_down_bwd(dr2, ada3, w_down, seq)
    d_w_down = _matmul_tn(dffn, ffn_in, 512, 512, "dw_down").T
    du_a, du_g, acc_ca, acc_cg = _ffn_gate_bwd(u, dfi, conv_w, conv_b, seq)
    dr1, dmix, acc1 = _ffn_up_bwd(du_a, du_g, w_up, dr2, xh1, rs1, mix, ada3, ln1_g, ln1_b, seq)
    d_w_up = jnp.concatenate([_matmul_tn(h2, du_a, 512, 512, "dw_up_a"), _matmul_tn(h2, du_g, 512, 512, "dw_up_g")],
                             axis=1)

    doa, dob, dob4, dob16, dl_a, dl_b, dl_b4, dl_b16, acc_gn = _mix_out_bwd(dmix, w_out, oa, ob, gn_a, gn_b, perms, seq)
    d_w_out = _matmul_tn(merged, dmix, 512, 512, "dw_out")
    late_grads = dict(w_out=d_w_out, w_up=d_w_up, conv_w=jnp.concatenate([acc_ca[0:3], acc_cg[0:3]], axis=1),
                      w_down=d_w_down)
    dka, dva, df_k, dqt, df_q, late_parts = _fox_bwd(za, doa, f_col, lse_row_a, _rows(dl_a), seq,
                                                     [_payload(n, _dest_major(n, late_grads[n])) for n in LATE])
    dfa_t, dbf = _fgate_bwd(_rows(df_k) + df_q, fa_t, bf, seq)
    flat = lambda a: a.reshape(t_all, a.shape[-1])
    dil = []
    for zb, d, do, lse, dl in zip(zbs, DILATIONS, (dob, flat(dob4), flat(dob16)),
                                  (lse_b, flat(lse_b4), flat(lse_b16)), (dl_b, flat(dl_b4), flat(dl_b16))):
        dil.append((_dil_bwd_dq(zb, do, lse, dl, seq, d), *_dil_bwd_dkv(zb, do, lse, dl, seq, d)))
    dfa16 = jnp.zeros((16, t_all), BF16).at[:N_HEADS].set(dfa_t.astype(BF16))
    grad_x, dz, acc0 = _inproj_bwd(dqt, dka, dva, dil[0], dil[1], dil[2], dfa16, pos, wqkv, wf16, freq, perms, dr1, xf,
                                   ada3, seq)
    d_wqkv = _matmul_tn(h1, dz, 512, 512, "dw_in")
    d_wf = _matmul_rows(dfa16, h1, 512, "dw_fgate")[:N_HEADS].T
    d_w_in = jnp.concatenate([d_wqkv[:, :3 * WIDTH], d_wf, d_wqkv[:, 3 * WIDTH:]], axis=1)

    dada = jnp.concatenate([acc0[8:8 + nbat], acc0[:nbat], acc1[24:24 + nbat], acc1[16:16 + nbat], acc1[8:8 + nbat],
                            acc2[8:8 + nbat]], axis=1)

    grads = dict(
        dada=dada, b_ada=jnp.sum(dada, axis=0, keepdims=True), w_in=d_w_in, b_fgate=dbf[:, 0][None, :],
        gn_a=acc_gn[0:1, :WIDTH], gn_b=acc_gn[0:1, WIDTH:], ln1_g=acc1[0:1], ln1_b=acc1[1:2],
        conv_b=jnp.concatenate([acc_ca[3:4], acc_cg[3:4]], axis=1), ln2_g=acc2[0:1], ln2_b=acc2[1:2])
    return acc2[2:3], grad_x.reshape(x.shape), grads, dict(zip(LATE, late_parts))


LATE = ("w_out", "w_up", "conv_w", "w_down")
BIG = ("w_ada", "w_in") + LATE
COLUMN_SHARDED = ("w_ada", "w_in", "w_up", "conv_w")


def _payload(name, a):
    return a if name == "conv_w" else a.astype(BF16)
SMALL = ("b_ada", "b_fgate", "gn_a", "gn_b", "ln1_g", "ln1_b", "conv_b", "ln2_g", "ln2_b")
ADAM_ROWS = dict(w_ada=256, w_in=256, w_out=128, w_up=256, conv_w=3, w_down=176)
SMALL_ROWS = 24


def _full_from_gathered(name, g):
    if name in COLUMN_SHARDED:
        return g.transpose(1, 0, 2).reshape(g.shape[1], N_DEV * g.shape[2])
    return g.reshape(N_DEV * g.shape[1], g.shape[2])


def _dest_major(name, full):
    if name in COLUMN_SHARDED:
        r, cfull = full.shape
        return full.reshape(r, N_DEV, cfull // N_DEV).transpose(1, 0, 2)
    return full.reshape(N_DEV, full.shape[0] // N_DEV, full.shape[1])


def _pack_small(vals, extra=None):
    parts = [vals[n].reshape(-1) for n in SMALL]
    if extra is not None:
        parts.append(extra.reshape(-1))
    flat = jnp.concatenate(parts)
    return jnp.pad(flat, (0, SMALL_ROWS * D_MODEL - flat.shape[0])).reshape(SMALL_ROWS, D_MODEL)


def _unpack_small(packed, like):
    flat = packed.reshape(-1)
    out, off = {}, 0
    for n in SMALL:
        size = like[n].size
        out[n] = flat[off:off + size].reshape(like[n].shape)
        off += size
    return out, flat[off:off + D_MODEL]


def kernel(x, c, positions, w_ada, b_ada, w_in, b_fgate, gn_a, gn_b, w_out, ln1_g, ln1_b, w_up, conv_w, conv_b, w_down, ln2_g, ln2_b, loss_target, m_w_ada, m_b_ada, m_w_in, m_b_fgate, m_gn_a, m_gn_b, m_w_out, m_ln1_g, m_ln1_b, m_w_up, m_conv_w, m_conv_b, m_w_down, m_ln2_g, m_ln2_b, v_w_ada, v_b_ada, v_w_in, v_b_fgate, v_gn_a, v_gn_b, v_w_out, v_ln1_g, v_ln1_b, v_w_up, v_conv_w, v_conv_b, v_w_down, v_ln2_g, v_ln2_b):
    w = dict(w_ada=w_ada[0], b_ada=b_ada, w_in=w_in[0], b_fgate=b_fgate, gn_a=gn_a, gn_b=gn_b, w_out=w_out[0],
             ln1_g=ln1_g, ln1_b=ln1_b, w_up=w_up[0], conv_w=conv_w[0], conv_b=conv_b, w_down=w_down[0], ln2_g=ln2_g,
             ln2_b=ln2_b)
    m = dict(w_ada=m_w_ada[0], b_ada=m_b_ada, w_in=m_w_in[0], b_fgate=m_b_fgate, gn_a=m_gn_a, gn_b=m_gn_b,
             w_out=m_w_out[0], ln1_g=m_ln1_g, ln1_b=m_ln1_b, w_up=m_w_up[0], conv_w=m_conv_w[0], conv_b=m_conv_b,
             w_down=m_w_down[0], ln2_g=m_ln2_g, ln2_b=m_ln2_b)
    v = dict(w_ada=v_w_ada[0], b_ada=v_b_ada, w_in=v_w_in[0], b_fgate=v_b_fgate, gn_a=v_gn_a, gn_b=v_gn_b,
             w_out=v_w_out[0], ln1_g=v_ln1_g, ln1_b=v_ln1_b, w_up=v_w_up[0], conv_w=v_conv_w[0], conv_b=v_conv_b,
             w_down=v_w_down[0], ln2_g=v_ln2_g, ln2_b=v_ln2_b)

    nbat = x.shape[0]
    me = 4 * lax.axis_index("x") + 2 * lax.axis_index("y") + lax.axis_index("c")
    ada_cols = w["w_ada"].shape[1]

    c_all, w_in_all = _exchange([c, _payload("w_in", w["w_in"])], [True, True], "weight_gather")
    c_all = c_all.reshape(N_DEV * nbat, D_MODEL)
    ada_mine = _ada_fwd(c_all, w["w_ada"], lax.dynamic_slice(b_ada, (0, me * ada_cols), (1, ada_cols)))
    (ada_parts,) = _exchange([ada_mine.reshape(N_DEV, nbat, ada_cols)], [False], "ada_exchange")
    ada3 = ada_parts.transpose(1, 0, 2).reshape(nbat, 6, D_MODEL)

    loss_lanes, grad_x, g_local, parts = _local_step(
        x, positions, loss_target, ada3, _full_from_gathered("w_in", w_in_all), b_fgate, gn_a, gn_b, ln1_g, ln1_b,
        conv_b, ln2_g, ln2_b, {n: _payload(n, w[n]) for n in LATE})

    parts["w_in"], dada_all, small_all = _exchange(
        [_payload("w_in", _dest_major("w_in", g_local["w_in"])), g_local["dada"], _pack_small(g_local, loss_lanes)],
        [False, True, True], "grad_exchange")
    dada_cols = lax.dynamic_slice(dada_all.reshape(N_DEV * nbat, 6 * D_MODEL), (0, me * ada_cols),
                                  (N_DEV * nbat, ada_cols))
    parts["w_ada"] = _ada_bwd(c_all, dada_cols)[None]

    grad, delta, new_m, new_v = {}, {}, {}, {}
    for n in BIG:
        grad[n], delta[n], new_m[n], new_v[n] = (
            a[None] for a in _adamw(parts[n], w[n], m[n], v[n], ADAM_ROWS[n], "adamw_" + n))
    packed = _adamw(small_all, _pack_small(w), _pack_small(m), _pack_small(v), SMALL_ROWS, "adamw_small")
    for dst, pk in zip((grad, delta, new_m, new_v), packed):
        vals, lanes = _unpack_small(pk, w)
        dst.update(vals)
        if dst is grad:
            loss = jnp.sum(lanes)

    order = ("w_ada", "b_ada", "w_in", "b_fgate", "gn_a", "gn_b", "w_out", "ln1_g", "ln1_b", "w_up", "conv_w", "conv_b",
             "w_down", "ln2_g", "ln2_b")
    return (loss, grad_x, *[grad[n] for n in order], *[delta[n] for n in order], *[new_m[n] for n in order],
            *[new_v[n] for n in order])
```

```python
import functools

import numpy as np
import jax
import jax.numpy as jnp
from jax import lax
from jax.experimental import pallas as pl
from jax.experimental.pallas import tpu as pltpu

F32, BF16 = jnp.float32, jnp.bfloat16
HIGHEST = lax.Precision.HIGHEST
MESH = pl.DeviceIdType.MESH
ANY = pl.BlockSpec(memory_space=pl.ANY)

D_MODEL = 1024
N_HEADS = 8
HEAD_DIM = 64
WIDTH = 512
D_FF = 2816
N_DEV = 8
ROPE_DIMS = 16
ROPE_THETA = 500000.0
ALPHA = 2.0 ** 0.25
LN_EPS = 1e-5
RMS_EPS = 1e-6
NEG = -1e30
Q_SCALE = 0.125
BLK = 128
LANES = 128
VMEM_LIMIT_BYTES = 56 * 1024 * 1024

ADAM_LR, ADAM_B1, ADAM_B2, ADAM_EPS, ADAM_WD, ADAM_STEP = 0.001, 0.9, 0.999, 1e-08, 0.01, 10


def _params(vmem=VMEM_LIMIT_BYTES):
    return pltpu.CompilerParams(vmem_limit_bytes=vmem)


def _nn(a, b):
    return jnp.dot(a, b, preferred_element_type=F32)


def _nt(a, b):
    return lax.dot_general(a, b, (((1,), (1,)), ((), ())), preferred_element_type=F32)


def _tn(a, b):
    return lax.dot_general(a, b, (((0,), (0,)), ((), ())), preferred_element_type=F32)


def _head_mats():
    r = lax.broadcasted_iota(jnp.int32, (LANES, WIDTH), 0)
    c = lax.broadcasted_iota(jnp.int32, (LANES, WIDTH), 1)
    e = ((c >> 6) == r).astype(BF16)
    r2 = lax.broadcasted_iota(jnp.int32, (WIDTH, LANES), 0)
    c2 = lax.broadcasted_iota(jnp.int32, (WIDTH, LANES), 1)
    et = ((r2 >> 6) == c2).astype(BF16)
    return e, et


def _split3(x):
    hi = x.astype(BF16)
    r = x - hi.astype(F32)
    mid = r.astype(BF16)
    return hi, mid, (r - mid.astype(F32)).astype(BF16)


def _hexp(w, e):
    return sum(_nn(part, e) for part in _split3(w)[:2])


def _hsum(x, et):
    return sum(_nn(part, et) for part in _split3(x)[:2])


def _perm_matrix(rows, d, transpose):
    i = np.arange(rows)
    j = (i % (rows // d)) * d + i // (rows // d)
    p = np.zeros((rows, rows), np.float32)
    p[i, j] = 1.0
    return jnp.asarray(p.T if transpose else p, BF16)


def _permute_f32(p, x):
    return sum(_nn(p, part) for part in _split3(x))


def _store_classes(ref, y, d):
    n = y.shape[0] // d
    for r in range(d):
        ref[r] = y[r * n:(r + 1) * n, :]


def _load_classes(ref, d):
    return jnp.concatenate([ref[r] for r in range(d)], axis=0)


def _rope_tabs(pos_ref, fr_ref, sign):
    ang = pos_ref[...].astype(F32) * fr_ref[...]
    lane = lax.broadcasted_iota(jnp.int32, ang.shape, 1) & (HEAD_DIM - 1)
    m1 = lane < ROPE_DIMS // 2
    m2 = (lane >= ROPE_DIMS // 2) & (lane < ROPE_DIMS)
    cos = jnp.cos(ang)
    sin = jnp.sin(ang) * sign
    return (jnp.where(m1 | m2, cos, 1.0), jnp.where(m1, -sin, 0.0), jnp.where(m2, sin, 0.0))


def _rope(z, tabs):
    c, s1, s2 = tabs
    parts = []
    for p in range(z.shape[1] // LANES):
        zp = z[:, LANES * p:LANES * (p + 1)]
        parts.append(zp * c + pltpu.roll(zp, LANES - 8, 1) * s1 + pltpu.roll(zp, 8, 1) * s2)
    return jnp.concatenate(parts, axis=1)


def _half_masks(rows):
    lane = lax.broadcasted_iota(jnp.int32, (rows, LANES), 1)
    lo = lane < HEAD_DIM
    return lo, jnp.logical_not(lo)


def _layer_norm_bwd(dxh, xh, rstd):
    m1 = jnp.mean(dxh, axis=1, keepdims=True)
    m2 = jnp.mean(dxh * xh, axis=1, keepdims=True)
    return rstd * (dxh - m1 - xh * m2)


def _coords():
    return lax.axis_index("x"), lax.axis_index("y"), lax.axis_index("c")


def _peer(x, y, c, k):
    return (1 - x if k & 4 else x, 1 - y if k & 2 else y, 1 - c if k & 1 else c)


def _comm_sems(n):
    return [pltpu.SemaphoreType.DMA((N_DEV - 1, n)), pltpu.SemaphoreType.DMA((N_DEV - 1, n)),
            pltpu.SemaphoreType.DMA((n,))]


def _comm_copies(ins, outs, to_all, sems):
    send_sems, recv_sems, local_sems = sems
    x, y, c = _coords()
    me = 4 * x + 2 * y + c
    copies = [pltpu.make_async_copy(ins[t] if to_all[t] else ins[t].at[me], outs[t].at[me], local_sems.at[t])
              for t in range(len(ins))]
    for k in range(1, N_DEV):
        px, py, pc = _peer(x, y, c, k)
        dest = 4 * px + 2 * py + pc
        for t in range(len(ins)):
            copies.append(pltpu.make_async_remote_copy(
                src_ref=ins[t] if to_all[t] else ins[t].at[dest], dst_ref=outs[t].at[me],
                send_sem=send_sems.at[k - 1, t], recv_sem=recv_sems.at[k - 1, t],
                device_id=(px, py, pc), device_id_type=MESH))
    return copies


def _comm_out_shapes(ins, to_all):
    return [jax.ShapeDtypeStruct(((N_DEV,) + a.shape) if ta else a.shape, a.dtype) for a, ta in zip(ins, to_all)]


def _exchange(ins, to_all, name):
    n = len(ins)

    def body(*refs):
        copies = _comm_copies(refs[:n], refs[n:2 * n], to_all, refs[2 * n:])
        for cp in copies:
            cp.start()
        for cp in copies:
            cp.wait()

    return pl.pallas_call(
        body, name=name, out_shape=_comm_out_shapes(ins, to_all), in_specs=[ANY] * n, out_specs=[ANY] * n,
        scratch_shapes=_comm_sems(n),
    )(*ins)


def _adamw(parts, w, m, v, rows, name):
    n_parts, r_all, cols = parts.shape
    c1 = 1.0 - ADAM_B1 ** ADAM_STEP
    c2 = 1.0 - ADAM_B2 ** ADAM_STEP

    def body(p_ref, w_ref, m_ref, v_ref, g_ref, d_ref, mo_ref, vo_ref):
        g = p_ref[0].astype(F32)
        for s in range(1, n_parts):
            g = g + p_ref[s].astype(F32)
        mn = ADAM_B1 * m_ref[...] + (1.0 - ADAM_B1) * g
        vn = ADAM_B2 * v_ref[...] + (1.0 - ADAM_B2) * (g * g)
        m_hat = mn / c1
        v_hat = vn / c2
        g_ref[...] = g
        d_ref[...] = -ADAM_LR * (m_hat / (jnp.sqrt(v_hat) + ADAM_EPS) + ADAM_WD * w_ref[...])
        mo_ref[...] = mn
        vo_ref[...] = vn

    spec = pl.BlockSpec((rows, cols), lambda i: (i, 0))
    return pl.pallas_call(
        body, name=name, grid=(r_all // rows,),
        in_specs=[pl.BlockSpec((n_parts, rows, cols), lambda i: (0, i, 0)), spec, spec, spec],
        out_specs=[spec] * 4, out_shape=[jax.ShapeDtypeStruct((r_all, cols), F32)] * 4,
        compiler_params=_params(),
    )(parts, w, m, v)


def _matmul_tn(a, b, chunk, tk, name):
    t_all, k1 = a.shape
    n = b.shape[1]

    def body(a_ref, b_ref, o_ref):
        @pl.when(pl.program_id(0) == 0)
        def _():
            o_ref[...] = jnp.zeros_like(o_ref)
        at = a_ref[...].astype(F32).T.astype(BF16)
        for j in range(0, n, chunk):
            cs = slice(j, min(j + chunk, n))
            o_ref[:, cs] += _nn(at, b_ref[:, cs])

    return pl.pallas_call(
        body, name=name, grid=(t_all // tk,),
        in_specs=[pl.BlockSpec((tk, k1), lambda t: (t, 0)), pl.BlockSpec((tk, n), lambda t: (t, 0))],
        out_specs=pl.BlockSpec((k1, n), lambda t: (0, 0)),
        out_shape=jax.ShapeDtypeStruct((k1, n), F32), compiler_params=_params(),
    )(a, b)


def _matmul_rows(a, b, tk, name):
    r, t_all = a.shape
    n = b.shape[1]

    def body(a_ref, b_ref, o_ref):
        @pl.when(pl.program_id(0) == 0)
        def _():
            o_ref[...] = jnp.zeros_like(o_ref)
        o_ref[...] += _nn(a_ref[...], b_ref[...])

    return pl.pallas_call(
        body, name=name, grid=(t_all // tk,),
        in_specs=[pl.BlockSpec((r, tk), lambda t: (0, t)), pl.BlockSpec((tk, n), lambda t: (t, 0))],
        out_specs=pl.BlockSpec((r, n), lambda t: (0, 0)),
        out_shape=jax.ShapeDtypeStruct((r, n), F32), compiler_params=_params(),
    )(a, b)


def _ada_fwd(c_all, w_ada, b_ada):
    whole = lambda a: pl.BlockSpec(a.shape, lambda j: (0, 0))

    def body(c_ref, w_ref, b_ref, o_ref):
        cv = c_ref[...]
        s = (cv * jax.nn.sigmoid(cv)).astype(BF16)
        o_ref[...] = _nn(s, w_ref[...].astype(BF16)) + b_ref[...]

    out = jax.ShapeDtypeStruct((c_all.shape[0], w_ada.shape[1]), F32)
    return pl.pallas_call(
        body, name="ada_fwd", grid=(1,), in_specs=[whole(c_all), whole(w_ada), whole(b_ada)], out_specs=whole(out),
        out_shape=out, compiler_params=_params(),
    )(c_all, w_ada, b_ada)


def _ada_bwd(c_all, dada):
    whole = lambda a: pl.BlockSpec(a.shape, lambda j: (0, 0))

    def body(c_ref, d_ref, o_ref):
        cv = c_ref[...]
        s = (cv * jax.nn.sigmoid(cv)).astype(BF16)
        o_ref[...] = _tn(s, d_ref[...].astype(BF16))

    out = jax.ShapeDtypeStruct((D_MODEL, dada.shape[1]), F32)
    return pl.pallas_call(
        body, name="ada_bwd", grid=(1,), in_specs=[whole(c_all), whole(dada)], out_specs=whole(out), out_shape=out,
        compiler_params=_params(),
    )(c_all, dada)


TOK_TM = 256
DILATIONS = (1, 4, 16)


def _class_spec(d, width, nts):
    return pl.BlockSpec((d, TOK_TM // d, width), lambda i: (i // nts, i % nts, 0))


def _class_shape(t_all, seq, d, width, dtype):
    return jax.ShapeDtypeStruct((t_all // seq * d, seq // d, width), dtype)


def _inproj(x, ada3, pos, wqkv, wf16, freq, perms, seq):
    t_all = x.shape[0]
    tm = TOK_TM
    nts = seq // tm

    def body(x_ref, ada_ref, pos_ref, w_ref, wf_ref, fr_ref, p4_ref, p16_ref, h1_ref, za_ref, zb_ref, zb4_ref,
             zb16_ref, vt_ref, fa_ref):
        h1 = (x_ref[...] * (1.0 + ada_ref[0, 1:2, :]) + ada_ref[0, 0:1, :]).astype(BF16)
        h1_ref[...] = h1
        tabs = _rope_tabs(pos_ref, fr_ref, 1.0)
        for n in range(6):
            z = _nn(h1, w_ref[:, n * WIDTH:(n + 1) * WIDTH])
            if n in (3, 4):
                z = _rope(z, tabs)
            if n in (0, 3):
                z = z * Q_SCALE
            if n == 2:
                vt_ref[...] = z.T.astype(BF16)
            dst = za_ref if n < 3 else zb_ref
            dst[:, (n % 3) * WIDTH:(n % 3 + 1) * WIDTH] = z.astype(BF16)
        fa_ref[...] = _nt(wf_ref[...], h1)[:N_HEADS]
        zb = zb_ref[...]
        _store_classes(zb4_ref, _nn(p4_ref[...], zb).astype(BF16), 4)
        _store_classes(zb16_ref, _nn(p16_ref[...], zb).astype(BF16), 16)

    tok = lambda w: pl.BlockSpec((tm, w), lambda i: (i, 0))
    whole = lambda a: pl.BlockSpec(a.shape, lambda i: (0, 0))
    return pl.pallas_call(
        body, name="inproj", grid=(t_all // tm,),
        in_specs=[tok(D_MODEL), pl.BlockSpec((1, 6, D_MODEL), lambda i: (i // nts, 0, 0)), tok(1), whole(wqkv),
                  whole(wf16), pl.BlockSpec((1, LANES), lambda i: (0, 0)), whole(perms[0]), whole(perms[1])],
        out_specs=[tok(D_MODEL), tok(3 * WIDTH), tok(3 * WIDTH), _class_spec(4, 3 * WIDTH, nts),
                   _class_spec(16, 3 * WIDTH, nts), pl.BlockSpec((WIDTH, tm), lambda i: (i // nts, i % nts)),
                   pl.BlockSpec((N_HEADS, tm), lambda i: (0, i))],
        out_shape=[jax.ShapeDtypeStruct((t_all, D_MODEL), BF16), jax.ShapeDtypeStruct((t_all, 3 * WIDTH), BF16),
                   jax.ShapeDtypeStruct((t_all, 3 * WIDTH), BF16), _class_shape(t_all, seq, 4, 3 * WIDTH, BF16),
                   _class_shape(t_all, seq, 16, 3 * WIDTH, BF16),
                   jax.ShapeDtypeStruct((t_all // seq * WIDTH, seq), BF16),
                   jax.ShapeDtypeStruct((N_HEADS, t_all), F32)],
        compiler_params=_params(),
    )(x, ada3, pos, wqkv, wf16, freq, perms[0], perms[1])


def _fgate_fwd(fa_t, bf, seq):
    t_all = fa_t.shape[1]

    def body(fa_ref, b_ref, f_ref):
        lane = lax.broadcasted_iota(jnp.int32, (N_HEADS, LANES), 1)

        def chunk(j, carry):
            sl = pl.ds(pl.multiple_of(j * LANES, LANES), LANES)
            xv = fa_ref[:, sl] + b_ref[...]
            lf = jnp.minimum(xv, 0.0) - jnp.log(1.0 + jnp.exp(-jnp.abs(xv)))
            for s in (1, 2, 4, 8, 16, 32, 64):
                lf = lf + jnp.where(lane >= s, pltpu.roll(lf, s, 1), 0.0)
            lf = lf + carry
            f_ref[:, sl] = lf
            return lf[:, LANES - 1:LANES]

        lax.fori_loop(0, seq // LANES, chunk, jnp.zeros((N_HEADS, 1), F32))

    return pl.pallas_call(
        body, name="fgate_fwd", grid=(t_all // seq,),
        in_specs=[pl.BlockSpec((N_HEADS, seq), lambda b: (0, b)), pl.BlockSpec((N_HEADS, 1), lambda b: (0, 0))],
        out_specs=pl.BlockSpec((N_HEADS, seq), lambda b: (0, b)),
        out_shape=jax.ShapeDtypeStruct((N_HEADS, t_all), F32), compiler_params=_params(),
    )(fa_t, bf)


def _fgate_bwd(df_t, fa_t, bf, seq):
    t_all = fa_t.shape[1]

    def body(df_ref, fa_ref, b_ref, o_ref, s_ref):
        lane = lax.broadcasted_iota(jnp.int32, (N_HEADS, LANES), 1)

        @pl.when(pl.program_id(0) == 0)
        def _():
            s_ref[...] = jnp.zeros_like(s_ref)

        def chunk(jj, carry):
            car, tot = carry
            j = seq // LANES - 1 - jj
            sl = pl.ds(pl.multiple_of(j * LANES, LANES), LANES)
            d = df_ref[:, sl]
            for s in (1, 2, 4, 8, 16, 32, 64):
                d = d + jnp.where(lane < LANES - s, pltpu.roll(d, LANES - s, 1), 0.0)
            d = d + car
            dfa = d * jax.nn.sigmoid(-(fa_ref[:, sl] + b_ref[...]))
            o_ref[:, sl] = dfa
            return d[:, 0:1], tot + jnp.sum(dfa, axis=1, keepdims=True)

        z = jnp.zeros((N_HEADS, 1), F32)
        _, tot = lax.fori_loop(0, seq // LANES, chunk, (z, z))
        s_ref[...] += jnp.broadcast_to(tot, (N_HEADS, LANES))

    row = pl.BlockSpec((N_HEADS, seq), lambda b: (0, b))
    return pl.pallas_call(
        body, name="fgate_bwd", grid=(t_all // seq,),
        in_specs=[row, row, pl.BlockSpec((N_HEADS, 1), lambda b: (0, 0))],
        out_specs=[row, pl.BlockSpec((N_HEADS, LANES), lambda b: (0, 0))],
        out_shape=[jax.ShapeDtypeStruct((N_HEADS, t_all), F32), jax.ShapeDtypeStruct((N_HEADS, LANES), F32)],
        compiler_params=_params(),
    )(df_t, fa_t, bf)


FOX_T = 256


def _fox_prep(dst, src_ref, lo, hi):
    for p in range(4):
        v = src_ref[:, LANES * p:LANES * (p + 1)]
        dst[2 * p] = jnp.where(lo, v, jnp.zeros_like(v))
        dst[2 * p + 1] = jnp.where(hi, v, jnp.zeros_like(v))


def _fox_fwd(za, vt, f_col, seq, shards):
    t_all = za.shape[0]
    tq = FOX_T
    nq = seq // tq
    nbat = t_all // seq
    n = len(shards)
    to_all = [True] * n

    def body(*refs):
        q_ref, k_ref, vt_ref, fc_ref = refs[:4]
        o_ref, lse_ref = refs[4 + n:6 + n]
        qm_sc, m_sc, l_sc, acc_sc, a_sc, st_sc, pe_sc = refs[6 + 2 * n:13 + 2 * n]
        comm = (refs[4:4 + n], refs[6 + n:6 + 2 * n], to_all, refs[13 + 2 * n:])
        i = pl.program_id(1)

        @pl.when((pl.program_id(0) == 0) & (i == 0))
        def _():
            for cp in _comm_copies(*comm):
                cp.start()
        lo, hi = _half_masks(tq)
        r = lax.broadcasted_iota(jnp.int32, (tq, tq), 0)
        c = lax.broadcasted_iota(jnp.int32, (tq, tq), 1)
        tri = c >= r
        _fox_prep(qm_sc, q_ref, lo, hi)
        m_sc[...] = jnp.full(m_sc.shape, NEG, F32)
        l_sc[...] = jnp.zeros_like(l_sc)
        acc_sc[...] = jnp.zeros_like(acc_sc)

        def block(j, masked):
            sl = pl.ds(pl.multiple_of(j * tq, tq), tq)
            for p in range(4):
                kj = k_ref[sl, LANES * p:LANES * (p + 1)]
                for h in (2 * p, 2 * p + 1):
                    st = _nt(kj, qm_sc[h]) - fc_ref[sl, h:h + 1]
                    st_sc[h] = jnp.where(tri, st, NEG) if masked else st
            for h in range(N_HEADS):
                st = st_sc[h]
                m = m_sc[h:h + 1, :]
                mn = jnp.maximum(m, jnp.max(st, axis=0, keepdims=True))
                a = jnp.exp(m - mn)
                pe = jnp.exp(st - mn)
                m_sc[h:h + 1, :] = mn
                a_sc[h:h + 1, :] = a
                l_sc[h:h + 1, :] = a * l_sc[h:h + 1, :] + jnp.sum(pe, axis=0, keepdims=True)
                pe_sc[h] = pe.astype(BF16)
            for h in range(N_HEADS):
                acc_sc[h] = a_sc[h:h + 1, :] * acc_sc[h] + _nn(vt_ref[HEAD_DIM * h:HEAD_DIM * (h + 1), sl], pe_sc[h])

        def step(j, carry):
            block(j, False)
            return carry

        lax.fori_loop(0, i, step, 0)
        block(i, True)
        lse_ref[...] = m_sc[...] + jnp.log(l_sc[...])
        for p in range(4):
            ot = jnp.concatenate([acc_sc[h] / l_sc[h:h + 1, :] for h in (2 * p, 2 * p + 1)], axis=0)
            o_ref[:, LANES * p:LANES * (p + 1)] = ot.T

        @pl.when((pl.program_id(0) == nbat - 1) & (i == nq - 1))
        def _():
            for cp in _comm_copies(*comm):
                cp.wait()

    res = pl.pallas_call(
        body, name="fox_fwd", grid=(nbat, nq),
        in_specs=[pl.BlockSpec((tq, WIDTH), lambda b, i: (b * nq + i, 0)),
                  pl.BlockSpec((seq, WIDTH), lambda b, i: (b, 1)), pl.BlockSpec((WIDTH, seq), lambda b, i: (b, 0)),
                  pl.BlockSpec((seq, LANES), lambda b, i: (b, 0))] + [ANY] * n,
        out_specs=[pl.BlockSpec((tq, WIDTH), lambda b, i: (b * nq + i, 0)),
                   pl.BlockSpec((N_HEADS, tq), lambda b, i: (0, b * nq + i))] + [ANY] * n,
        out_shape=[jax.ShapeDtypeStruct((t_all, WIDTH), F32), jax.ShapeDtypeStruct((N_HEADS, t_all), F32)]
        + _comm_out_shapes(shards, to_all),
        scratch_shapes=[pltpu.VMEM((N_HEADS, tq, LANES), BF16), pltpu.VMEM((N_HEADS, tq), F32),
                        pltpu.VMEM((N_HEADS, tq), F32), pltpu.VMEM((N_HEADS, HEAD_DIM, tq), F32),
                        pltpu.VMEM((N_HEADS, tq), F32), pltpu.VMEM((N_HEADS, tq, tq), F32),
                        pltpu.VMEM((N_HEADS, tq, tq), BF16)] + _comm_sems(n),
        compiler_params=_params(),
    )(za, za, vt, f_col, *shards)
    return res[0], res[1], res[2:]


def _fox_bwd(za, do, f_col, lse_row, dl_row, seq, grads):
    t_all = za.shape[0]
    tk = FOX_T
    nk = seq // tk
    nbat = t_all // seq
    n = len(grads)
    to_all = [False] * n

    def body(*refs):
        k_ref, v_ref, q_ref, do_ref, fc_ref, lr_ref, dr_ref = refs[:7]
        dk_ref, dv_ref, df_ref, dqt_ref, dfq_ref = refs[7 + n:12 + n]
        km_sc, vm_sc, fk_sc, dk_sc, dv_sc, cs_sc, kt_sc, st_sc, dp_sc, pt_sc, ds_sc = refs[12 + 2 * n:23 + 2 * n]
        comm = (refs[7:7 + n], refs[12 + n:12 + 2 * n], to_all, refs[23 + 2 * n:])
        j = pl.program_id(1)

        @pl.when(j == 0)
        def _():
            dqt_ref[...] = jnp.zeros_like(dqt_ref)
            dfq_ref[...] = jnp.zeros_like(dfq_ref)

        @pl.when((pl.program_id(0) == 0) & (j == 0))
        def _():
            for cp in _comm_copies(*comm):
                cp.start()
        lo, hi = _half_masks(tk)
        r = lax.broadcasted_iota(jnp.int32, (tk, tk), 0)
        c = lax.broadcasted_iota(jnp.int32, (tk, tk), 1)
        tri = c >= r
        _fox_prep(km_sc, k_ref, lo, hi)
        _fox_prep(vm_sc, v_ref, lo, hi)
        for h in range(N_HEADS):
            fk_sc[h] = jnp.broadcast_to(fc_ref[:, h:h + 1], (tk, tk))
        for p in range(4):
            kt_sc[p] = k_ref[:, LANES * p:LANES * (p + 1)].astype(F32).T.astype(BF16)
        dk_sc[...] = jnp.zeros_like(dk_sc)
        dv_sc[...] = jnp.zeros_like(dv_sc)
        cs_sc[...] = jnp.zeros_like(cs_sc)

        def block(i, masked):
            sl = pl.ds(pl.multiple_of(i * tk, tk), tk)
            for p in range(4):
                cs = slice(LANES * p, LANES * (p + 1))
                qi = q_ref[sl, cs]
                doi = do_ref[sl, cs]
                for h in (2 * p, 2 * p + 1):
                    st = _nt(km_sc[h], qi) - fk_sc[h] - lr_ref[h:h + 1, sl]
                    st_sc[h] = jnp.where(tri, st, NEG) if masked else st
                    dp_sc[h] = _nt(vm_sc[h], doi) - dr_ref[h:h + 1, sl]
            for h in range(N_HEADS):
                pt = jnp.exp(st_sc[h])
                dst = pt * dp_sc[h]
                pt_sc[h] = pt.astype(BF16)
                ds_sc[h] = dst.astype(BF16)
                cs_sc[h] += dst[:, :LANES] + dst[:, LANES:]
                dfq_ref[h:h + 1, sl] += jnp.sum(dst, axis=0, keepdims=True)
            for p in range(4):
                cs = slice(LANES * p, LANES * (p + 1))
                qi = q_ref[sl, cs]
                doi = do_ref[sl, cs]
                for h in (2 * p, 2 * p + 1):
                    dv_sc[h] += _nn(pt_sc[h], doi)
                    dk_sc[h] += _nn(ds_sc[h], qi)
                    kt = kt_sc[p, HEAD_DIM * (h % 2):HEAD_DIM * (h % 2 + 1), :]
                    dqt_ref[HEAD_DIM * h:HEAD_DIM * (h + 1), sl] += _nn(kt, ds_sc[h])

        def step(i, carry):
            block(i, False)
            return carry

        block(j, True)
        lax.fori_loop(j + 1, nk, step, 0)
        df_ref[...] = jnp.zeros_like(df_ref)
        for p in range(4):
            cs = slice(LANES * p, LANES * (p + 1))
            dk_ref[:, cs] = jnp.where(lo, dk_sc[2 * p], dk_sc[2 * p + 1]).astype(BF16)
            dv_ref[:, cs] = jnp.where(lo, dv_sc[2 * p], dv_sc[2 * p + 1]).astype(BF16)
            for h in (2 * p, 2 * p + 1):
                df_ref[:, h:h + 1] = -jnp.sum(cs_sc[h], axis=1, keepdims=True)

        @pl.when(j == nk - 1)
        def _():
            dqt_ref[...] = dqt_ref[...] * Q_SCALE

        @pl.when((pl.program_id(0) == nbat - 1) & (j == nk - 1))
        def _():
            for cp in _comm_copies(*comm):
                cp.wait()

    tile = lambda w, col: pl.BlockSpec((tk, w), lambda b, j: (b * nk + j, col))
    full = lambda col: pl.BlockSpec((seq, WIDTH), lambda b, j: (b, col))
    row = pl.BlockSpec((N_HEADS, seq), lambda b, j: (0, b))
    acc = pltpu.VMEM((N_HEADS, tk, LANES), F32)
    res = pl.pallas_call(
        body, name="fox_bwd", grid=(nbat, nk),
        in_specs=[tile(WIDTH, 1), tile(WIDTH, 2), full(0), full(0), tile(LANES, 0), row, row] + [ANY] * n,
        out_specs=[tile(WIDTH, 0), tile(WIDTH, 0), tile(LANES, 0), pl.BlockSpec((WIDTH, seq), lambda b, j: (b, 0)),
                   row] + [ANY] * n,
        out_shape=[jax.ShapeDtypeStruct((t_all, WIDTH), BF16), jax.ShapeDtypeStruct((t_all, WIDTH), BF16),
                   jax.ShapeDtypeStruct((t_all, LANES), F32), jax.ShapeDtypeStruct((nbat * WIDTH, seq), F32),
                   jax.ShapeDtypeStruct((N_HEADS, t_all), F32)] + _comm_out_shapes(grads, to_all),
        scratch_shapes=[pltpu.VMEM((N_HEADS, tk, LANES), BF16), pltpu.VMEM((N_HEADS, tk, LANES), BF16),
                        pltpu.VMEM((N_HEADS, tk, tk), F32), acc, acc, acc, pltpu.VMEM((4, LANES, tk), BF16),
                        pltpu.VMEM((N_HEADS, tk, tk), F32), pltpu.VMEM((N_HEADS, tk, tk), F32),
                        pltpu.VMEM((N_HEADS, tk, tk), BF16), pltpu.VMEM((N_HEADS, tk, tk), BF16)]
        + _comm_sems(n),
        compiler_params=_params(),
    )(za, za, za, do, f_col, lse_row, dl_row, *grads)
    return res[0], res[1], res[2], res[3], res[4], res[5:]


DIL_SUB = 4


def _dil_mask(has_prev):
    qi = lax.broadcasted_iota(jnp.int32, (BLK, 2 * BLK), 0)
    kj = lax.broadcasted_iota(jnp.int32, (BLK, 2 * BLK), 1)
    dist = qi + BLK - kj
    band = (dist >= 0) & (dist <= BLK)
    return band if has_prev is True else band & ((kj >= BLK) | has_prev)


def _dil_geometry(t_all, seq, d):
    length = seq // d
    nbs = length // BLK
    sub = min(DIL_SUB, nbs)
    spb = nbs // sub
    tile = lambda width, col: pl.BlockSpec((BLK * sub, width), lambda s: (s, col))
    whole = lambda width, col: pl.BlockSpec((length, width), lambda s: (s // spb, col))
    return nbs, sub, spb, t_all // (BLK * sub), tile, whole


def _blk(i):
    return pl.ds(pl.multiple_of(i * BLK, BLK), BLK)


def _dil_fwd(zb, seq, d):
    t_all = zb.shape[0]
    nbs, sub, spb, steps, tile, whole = _dil_geometry(t_all, seq, d)

    def body(q_ref, k_ref, v_ref, o_ref, lse_ref, s_sc, p_sc):
        first = (pl.program_id(0) % spb) * sub
        lo, hi = _half_masks(BLK)
        lse_ref[...] = jnp.zeros_like(lse_ref)
        for j in range(sub):
            blk = first + j
            mask = _dil_mask(blk != 0 if j == 0 else True)
            for p in range(4):
                cs = slice(LANES * p, LANES * (p + 1))
                qp = q_ref[BLK * j:BLK * (j + 1), cs]
                kcat = jnp.concatenate([k_ref[_blk(jnp.maximum(blk - 1, 0)), cs], k_ref[_blk(blk), cs]], axis=0)
                for e in (0, 1):
                    qe = jnp.where(lo if e == 0 else hi, qp, jnp.zeros_like(qp))
                    s_sc[N_HEADS * j + 2 * p + e] = jnp.where(mask, _nt(qe, kcat), NEG)
        inv = []
        for i in range(N_HEADS * sub):
            s = s_sc[i]
            m = jnp.max(s, axis=1, keepdims=True)
            pe = jnp.exp(s - m)
            l = jnp.sum(pe, axis=1, keepdims=True)
            p_sc[i] = pe.astype(BF16)
            inv.append(1.0 / l)
            j, h = divmod(i, N_HEADS)
            lse_ref[BLK * j:BLK * (j + 1), h:h + 1] = m + jnp.log(l)
        for j in range(sub):
            blk = first + j
            for p in range(4):
                cs = slice(LANES * p, LANES * (p + 1))
                vcat = jnp.concatenate([v_ref[_blk(jnp.maximum(blk - 1, 0)), cs], v_ref[_blk(blk), cs]], axis=0)
                res = [_nn(p_sc[N_HEADS * j + h], vcat) * inv[N_HEADS * j + h] for h in (2 * p, 2 * p + 1)]
                o_ref[BLK * j:BLK * (j + 1), cs] = jnp.where(lo, res[0], res[1])

    return pl.pallas_call(
        body, name=f"dil_fwd_{d}", grid=(steps,), in_specs=[tile(WIDTH, 0), whole(WIDTH, 1), whole(WIDTH, 2)],
        out_specs=[tile(WIDTH, 0), tile(LANES, 0)],
        out_shape=[jax.ShapeDtypeStruct((t_all, WIDTH), F32), jax.ShapeDtypeStruct((t_all, LANES), F32)],
        scratch_shapes=[pltpu.VMEM((N_HEADS * sub, BLK, 2 * BLK), F32),
                        pltpu.VMEM((N_HEADS * sub, BLK, 2 * BLK), BF16)],
        compiler_params=_params(),
    )(zb, zb, zb)


def _dil_bwd_dq(zb, do, lse, dl, seq, d):
    t_all = zb.shape[0]
    nbs, sub, spb, steps, tile, whole = _dil_geometry(t_all, seq, d)

    def body(q_ref, k_ref, v_ref, do_ref, lse_ref, dl_ref, dq_ref, s_sc, dp_sc, ds_sc):
        first = (pl.program_id(0) % spb) * sub
        lo, hi = _half_masks(BLK)
        for j in range(sub):
            blk = first + j
            rows = slice(BLK * j, BLK * (j + 1))
            mask = _dil_mask(blk != 0 if j == 0 else True)
            prev = _blk(jnp.maximum(blk - 1, 0))
            for p in range(4):
                cs = slice(LANES * p, LANES * (p + 1))
                qp = q_ref[rows, cs]
                dop = do_ref[rows, cs]
                kcat = jnp.concatenate([k_ref[prev, cs], k_ref[_blk(blk), cs]], axis=0)
                vcat = jnp.concatenate([v_ref[prev, cs], v_ref[_blk(blk), cs]], axis=0)
                for e in (0, 1):
                    h = 2 * p + e
                    sel = lo if e == 0 else hi
                    qe = jnp.where(sel, qp, jnp.zeros_like(qp))
                    doe = jnp.where(sel, dop, jnp.zeros_like(dop))
                    s_sc[N_HEADS * j + h] = jnp.where(mask, _nt(qe, kcat) - lse_ref[rows, h:h + 1], NEG)
                    dp_sc[N_HEADS * j + h] = _nt(doe, vcat) - dl_ref[rows, h:h + 1]
        for i in range(N_HEADS * sub):
            ds_sc[i] = (jnp.exp(s_sc[i]) * dp_sc[i]).astype(BF16)
        for j in range(sub):
            blk = first + j
            for p in range(4):
                cs = slice(LANES * p, LANES * (p + 1))
                kcat = jnp.concatenate([k_ref[_blk(jnp.maximum(blk - 1, 0)), cs], k_ref[_blk(blk), cs]], axis=0)
                i = N_HEADS * j + 2 * p
                dq_ref[BLK * j:BLK * (j + 1), cs] = (
                    jnp.where(lo, _nn(ds_sc[i], kcat), _nn(ds_sc[i + 1], kcat)) * Q_SCALE).astype(BF16)

    wide = pltpu.VMEM((N_HEADS * sub, BLK, 2 * BLK), F32)
    return pl.pallas_call(
        body, name=f"dil_bwd_dq_{d}", grid=(steps,),
        in_specs=[tile(WIDTH, 0), whole(WIDTH, 1), whole(WIDTH, 2), tile(WIDTH, 0), tile(LANES, 0), tile(LANES, 0)],
        out_specs=tile(WIDTH, 0), out_shape=jax.ShapeDtypeStruct((t_all, WIDTH), BF16),
        scratch_shapes=[wide, wide, pltpu.VMEM((N_HEADS * sub, BLK, 2 * BLK), BF16)], compiler_params=_params(),
    )(zb, zb, zb, do, lse, dl)


def _dil_bwd_dkv(zb, do, lse, dl, seq, d):
    t_all = zb.shape[0]
    nbs, sub, spb, steps, tile, whole = _dil_geometry(t_all, seq, d)

    def body(k_ref, v_ref, q_ref, do_ref, lse_ref, dl_ref, dk_ref, dv_ref, s_sc, dp_sc, pt_sc, ds_sc):
        first = (pl.program_id(0) % spb) * sub
        r = lax.broadcasted_iota(jnp.int32, (BLK, 2 * BLK), 0)
        c = lax.broadcasted_iota(jnp.int32, (BLK, 2 * BLK), 1)
        same = (c < BLK) & (c >= r)
        later = (c >= BLK) & (c - BLK <= r)
        lo, hi = _half_masks(BLK)
        for j in range(sub):
            blk = first + j
            rows = slice(BLK * j, BLK * (j + 1))
            nxt = _blk(jnp.minimum(blk + 1, nbs - 1))
            mask = same | (later & (blk + 1 != nbs)) if j == sub - 1 else same | later
            lrows = jnp.concatenate([lse_ref[_blk(blk), :].T, lse_ref[nxt, :].T], axis=1)
            erows = jnp.concatenate([dl_ref[_blk(blk), :].T, dl_ref[nxt, :].T], axis=1)
            for p in range(4):
                cs = slice(LANES * p, LANES * (p + 1))
                kp = k_ref[rows, cs]
                vp = v_ref[rows, cs]
                qcat = jnp.concatenate([q_ref[_blk(blk), cs], q_ref[nxt, cs]], axis=0)
                dcat = jnp.concatenate([do_ref[_blk(blk), cs], do_ref[nxt, cs]], axis=0)
                for e in (0, 1):
                    h = 2 * p + e
                    sel = lo if e == 0 else hi
                    ke = jnp.where(sel, kp, jnp.zeros_like(kp))
                    ve = jnp.where(sel, vp, jnp.zeros_like(vp))
                    s_sc[N_HEADS * j + h] = jnp.where(mask, _nt(ke, qcat) - lrows[h:h + 1, :], NEG)
                    dp_sc[N_HEADS * j + h] = _nt(ve, dcat) - erows[h:h + 1, :]
        for i in range(N_HEADS * sub):
            pt = jnp.exp(s_sc[i])
            pt_sc[i] = pt.astype(BF16)
            ds_sc[i] = (pt * dp_sc[i]).astype(BF16)
        for j in range(sub):
            blk = first + j
            rows = slice(BLK * j, BLK * (j + 1))
            nxt = _blk(jnp.minimum(blk + 1, nbs - 1))
            for p in range(4):
                cs = slice(LANES * p, LANES * (p + 1))
                qcat = jnp.concatenate([q_ref[_blk(blk), cs], q_ref[nxt, cs]], axis=0)
                dcat = jnp.concatenate([do_ref[_blk(blk), cs], do_ref[nxt, cs]], axis=0)
                i = N_HEADS * j + 2 * p
                dk_ref[rows, cs] = jnp.where(lo, _nn(ds_sc[i], qcat), _nn(ds_sc[i + 1], qcat)).astype(BF16)
                dv_ref[rows, cs] = jnp.where(lo, _nn(pt_sc[i], dcat), _nn(pt_sc[i + 1], dcat)).astype(BF16)

    wide = pltpu.VMEM((N_HEADS * sub, BLK, 2 * BLK), F32)
    half = pltpu.VMEM((N_HEADS * sub, BLK, 2 * BLK), BF16)
    return pl.pallas_call(
        body, name=f"dil_bwd_dkv_{d}", grid=(steps,),
        in_specs=[tile(WIDTH, 1), tile(WIDTH, 2), whole(WIDTH, 0), whole(WIDTH, 0), whole(LANES, 0), whole(LANES, 0)],
        out_specs=[tile(WIDTH, 0), tile(WIDTH, 0)], out_shape=[jax.ShapeDtypeStruct((t_all, WIDTH), BF16)] * 2,
        scratch_shapes=[wide, wide, half, half], compiler_params=_params(),
    )(zb, zb, zb, do, lse, dl)


def _mix_out(oa, o3, l3, gn_a, gn_b, w_out, x, ada3, ln_g, ln_b, perms, seq):
    t_all = x.shape[0]
    tm = TOK_TM
    nts = seq // tm

    def body(oa_ref, o1_ref, o2_ref, o3_ref, l1_ref, l2_ref, l3_ref, ga_ref, gb_ref, w_ref, x_ref, ada_ref, g_ref,
             b_ref, p4_ref, p16_ref, pt4_ref, pt16_ref, ob_ref, lse_ref, lse4_ref, lse16_ref, mg_ref, mix_ref, xh_ref,
             rs_ref, h2_ref):
        e, et = _head_mats()
        la = l1_ref[...]
        lb = _permute_f32(pt4_ref[...], _load_classes(l2_ref, 4))
        lc = _permute_f32(pt16_ref[...], _load_classes(l3_ref, 16))
        mx = jnp.maximum(jnp.maximum(la, lb), lc)
        ea, eb, ec = jnp.exp(la - mx), jnp.exp(lb - mx), jnp.exp(lc - mx)
        tot = ea + eb + ec
        lse = mx + jnp.log(tot)
        lse_ref[...] = lse
        _store_classes(lse4_ref, _permute_f32(p4_ref[...], lse), 4)
        _store_classes(lse16_ref, _permute_f32(p16_ref[...], lse), 16)
        ob = (o1_ref[...] * _hexp(ea / tot, e)
              + _permute_f32(pt4_ref[...], _load_classes(o2_ref, 4)) * _hexp(eb / tot, e)
              + _permute_f32(pt16_ref[...], _load_classes(o3_ref, 16)) * _hexp(ec / tot, e))
        ob_ref[...] = ob

        def rms(o, gain):
            rr = lax.rsqrt(_hsum(o * o, et) * (1.0 / HEAD_DIM) + RMS_EPS)
            return o * _hexp(rr, e) * gain

        merged = jnp.concatenate([rms(oa_ref[...], ga_ref[...]), rms(ob, gb_ref[...])], axis=1).astype(BF16)
        mg_ref[...] = merged
        mix = _nn(merged, w_ref[...])
        mix_ref[...] = mix.astype(BF16)
        r1 = ALPHA * x_ref[...] + ada_ref[0, 2:3, :] * mix
        d = r1 - jnp.mean(r1, axis=1, keepdims=True)
        rstd = lax.rsqrt(jnp.mean(d * d, axis=1, keepdims=True) + LN_EPS)
        xh = d * rstd
        xh_ref[...] = xh
        rs_ref[...] = jnp.broadcast_to(rstd, (tm, LANES))
        x1 = xh * g_ref[...] + b_ref[...]
        h2_ref[...] = (x1 * (1.0 + ada_ref[0, 4:5, :]) + ada_ref[0, 3:4, :]).astype(BF16)

    tok = lambda w: pl.BlockSpec((tm, w), lambda i: (i, 0))
    vec = lambda w: pl.BlockSpec((1, w), lambda i: (0, 0))
    whole = lambda a: pl.BlockSpec(a.shape, lambda i: (0, 0))
    classes = lambda a, d: a.reshape(t_all // seq * d, seq // d, a.shape[-1])
    return pl.pallas_call(
        body, name="mix_out", grid=(t_all // tm,),
        in_specs=[tok(WIDTH), tok(WIDTH), _class_spec(4, WIDTH, nts), _class_spec(16, WIDTH, nts), tok(LANES),
                  _class_spec(4, LANES, nts), _class_spec(16, LANES, nts), vec(WIDTH), vec(WIDTH), whole(w_out),
                  tok(D_MODEL), pl.BlockSpec((1, 6, D_MODEL), lambda i: (i // nts, 0, 0)), vec(D_MODEL), vec(D_MODEL)]
        + [whole(p) for p in perms],
        out_specs=[tok(WIDTH), tok(LANES), _class_spec(4, LANES, nts), _class_spec(16, LANES, nts), tok(D_MODEL),
                   tok(D_MODEL), tok(D_MODEL), tok(LANES), tok(D_MODEL)],
        out_shape=[jax.ShapeDtypeStruct((t_all, WIDTH), F32), jax.ShapeDtypeStruct((t_all, LANES), F32),
                   _class_shape(t_all, seq, 4, LANES, F32), _class_shape(t_all, seq, 16, LANES, F32),
                   jax.ShapeDtypeStruct((t_all, D_MODEL), BF16), jax.ShapeDtypeStruct((t_all, D_MODEL), BF16),
                   jax.ShapeDtypeStruct((t_all, D_MODEL), F32), jax.ShapeDtypeStruct((t_all, LANES), F32),
                   jax.ShapeDtypeStruct((t_all, D_MODEL), BF16)],
        compiler_params=_params(),
    )(oa, o3[0], classes(o3[1], 4), classes(o3[2], 16), l3[0], classes(l3[1], 4), classes(l3[2], 16), gn_a, gn_b,
      w_out, x, ada3, ln_g, ln_b, *perms)


def _mix_out_bwd(dmix, w_out, oa, ob, gn_a, gn_b, perms, seq):
    t_all = dmix.shape[0]
    tm = TOK_TM
    nts = seq // tm

    def body(dm_ref, w_ref, oa_ref, ob_ref, ga_ref, gb_ref, p4_ref, p16_ref, doa_ref, dob_ref, dob4_ref, dob16_ref,
             dla_ref, dlb_ref, dlb4_ref, dlb16_ref, acc_ref):
        @pl.when(pl.program_id(0) == 0)
        def _():
            acc_ref[...] = jnp.zeros_like(acc_ref)
        e, et = _head_mats()
        dmg = _nt(dm_ref[...], w_ref[...])

        def group(o, dn, gain):
            rr = lax.rsqrt(_hsum(o * o, et) * (1.0 / HEAD_DIM) + RMS_EPS)
            re = _hexp(rr, e)
            dgain = jnp.sum(dn * o * re, axis=0, keepdims=True)
            dxn = dn * gain
            tt = _hsum(dxn * o, et) * (rr * rr * rr) * (1.0 / HEAD_DIM)
            do = re * dxn - o * _hexp(tt, e)
            return do, _hsum(do * o, et), dgain

        doa, dla, dga = group(oa_ref[...], dmg[:, :WIDTH], ga_ref[...])
        dob, dlb, dgb = group(ob_ref[...], dmg[:, WIDTH:], gb_ref[...])
        dob = dob.astype(BF16)
        doa_ref[...] = doa.astype(BF16)
        dob_ref[...] = dob
        _store_classes(dob4_ref, _nn(p4_ref[...], dob).astype(BF16), 4)
        _store_classes(dob16_ref, _nn(p16_ref[...], dob).astype(BF16), 16)
        dla_ref[...] = dla
        dlb_ref[...] = dlb
        _store_classes(dlb4_ref, _permute_f32(p4_ref[...], dlb), 4)
        _store_classes(dlb16_ref, _permute_f32(p16_ref[...], dlb), 16)
        acc_ref[0:1, :] += jnp.concatenate([dga, dgb], axis=1)

    tok = lambda w: pl.BlockSpec((tm, w), lambda i: (i, 0))
    vec = lambda w: pl.BlockSpec((1, w), lambda i: (0, 0))
    return pl.pallas_call(
        body, name="mix_out_bwd", grid=(t_all // tm,),
        in_specs=[tok(D_MODEL), pl.BlockSpec(w_out.shape, lambda i: (0, 0)), tok(WIDTH), tok(WIDTH), vec(WIDTH),
                  vec(WIDTH), pl.BlockSpec(perms[0].shape, lambda i: (0, 0)),
                  pl.BlockSpec(perms[1].shape, lambda i: (0, 0))],
        out_specs=[tok(WIDTH), tok(WIDTH), _class_spec(4, WIDTH, nts), _class_spec(16, WIDTH, nts), tok(LANES),
                   tok(LANES), _class_spec(4, LANES, nts), _class_spec(16, LANES, nts),
                   pl.BlockSpec((8, D_MODEL), lambda i: (0, 0))],
        out_shape=[jax.ShapeDtypeStruct((t_all, WIDTH), BF16), jax.ShapeDtypeStruct((t_all, WIDTH), BF16),
                   _class_shape(t_all, seq, 4, WIDTH, BF16), _class_shape(t_all, seq, 16, WIDTH, BF16),
                   jax.ShapeDtypeStruct((t_all, LANES), F32), jax.ShapeDtypeStruct((t_all, LANES), F32),
                   _class_shape(t_all, seq, 4, LANES, F32), _class_shape(t_all, seq, 16, LANES, F32),
                   jax.ShapeDtypeStruct((8, D_MODEL), F32)],
        compiler_params=_params(),
    )(dmix, w_out, oa, ob, gn_a, gn_b, perms[0], perms[1])


def _inproj_bwd(dqt, dka, dva, dil1, dil4, dil16, dfa16, pos, wqkv, wf16, freq, perms, dr1, x, ada3, seq):
    t_all = x.shape[0]
    tm = TOK_TM
    nts = seq // tm

    def body(dqt_ref, dka_ref, dva_ref, q1_ref, k1_ref, v1_ref, q4_ref, k4_ref, v4_ref, q16_ref, k16_ref, v16_ref,
             dfa_ref, pos_ref, w_ref, wf_ref, fr_ref, pt4_ref, pt16_ref, dr1_ref, x_ref, ada_ref, gx_ref, dz_ref,
             acc_ref):
        i = pl.program_id(0)

        @pl.when(i == 0)
        def _():
            acc_ref[...] = jnp.zeros_like(acc_ref)
        tabs = _rope_tabs(pos_ref, fr_ref, -1.0)
        dz_ref[:, :WIDTH] = dqt_ref[...].T.astype(BF16)
        dz_ref[:, WIDTH:2 * WIDTH] = dka_ref[...]
        dz_ref[:, 2 * WIDTH:3 * WIDTH] = dva_ref[...]
        for t, (n1, n4, n16) in enumerate(((q1_ref, q4_ref, q16_ref), (k1_ref, k4_ref, k16_ref),
                                           (v1_ref, v4_ref, v16_ref))):
            tot = (n1[...].astype(F32) + _nn(pt4_ref[...], _load_classes(n4, 4))
                   + _nn(pt16_ref[...], _load_classes(n16, 16)))
            if t < 2:
                tot = _rope(tot, tabs)
            dz_ref[:, (3 + t) * WIDTH:(4 + t) * WIDTH] = tot.astype(BF16)
        dh1 = _tn(dfa_ref[...], wf_ref[...])
        for n in range(6):
            cs = slice(n * WIDTH, (n + 1) * WIDTH)
            dh1 = dh1 + _nt(dz_ref[:, cs], w_ref[:, cs])
        xv = x_ref[...]
        gx_ref[...] = ALPHA * dr1_ref[...] + dh1 * (1.0 + ada_ref[0, 1:2, :])
        b = i // nts
        acc_ref[pl.ds(b, 1), :] += jnp.sum(dh1 * xv, axis=0, keepdims=True)
        acc_ref[pl.ds(8 + b, 1), :] += jnp.sum(dh1, axis=0, keepdims=True)

    tok = lambda w: pl.BlockSpec((tm, w), lambda i: (i, 0))
    whole = lambda a: pl.BlockSpec(a.shape, lambda i: (0, 0))
    classes = lambda a, d: a.reshape(t_all // seq * d, seq // d, a.shape[-1])
    return pl.pallas_call(
        body, name="inproj_bwd", grid=(t_all // tm,),
        in_specs=[pl.BlockSpec((WIDTH, tm), lambda i: (i // nts, i % nts)), tok(WIDTH), tok(WIDTH)]
        + [tok(WIDTH)] * 3 + [_class_spec(4, WIDTH, nts)] * 3 + [_class_spec(16, WIDTH, nts)] * 3
        + [pl.BlockSpec((16, tm), lambda i: (0, i)), tok(1), whole(wqkv), whole(wf16),
           pl.BlockSpec((1, LANES), lambda i: (0, 0)), whole(perms[2]), whole(perms[3]), tok(D_MODEL), tok(D_MODEL),
           pl.BlockSpec((1, 6, D_MODEL), lambda i: (i // nts, 0, 0))],
        out_specs=[tok(D_MODEL), tok(6 * WIDTH), pl.BlockSpec((16, D_MODEL), lambda i: (0, 0))],
        out_shape=[jax.ShapeDtypeStruct((t_all, D_MODEL), F32), jax.ShapeDtypeStruct((t_all, 6 * WIDTH), BF16),
                   jax.ShapeDtypeStruct((16, D_MODEL), F32)],
        compiler_params=_params(),
    )(dqt, dka, dva, *dil1, *[classes(a, 4) for a in dil4], *[classes(a, 16) for a in dil16], dfa16, pos, wqkv, wf16,
      freq, perms[2], perms[3], dr1, x, ada3)


FFN_TM = 512
FFN_TN = 256
HALO = 8


FFN_CHUNK = 64


def _conv(cat_ref, w_ref, b_ref, start, rows, halo=HALO):
    return (b_ref[...] + w_ref[0:1, :] * cat_ref[pl.ds(start + halo - 2, rows), :]
            + w_ref[1:2, :] * cat_ref[pl.ds(start + halo - 1, rows), :]
            + w_ref[2:3, :] * cat_ref[pl.ds(start + halo, rows), :])


def _ffn_up_gate(h2, w_up, conv_w, conv_b, seq):
    t_all = h2.shape[0]
    tm, tn = FFN_TM, FFN_TN
    nc = D_FF // tn
    nts = seq // tm
    pre = 16

    def body(h_ref, hp_ref, wua_ref, wug_ref, wa_ref, wg_ref, ba_ref, bg_ref, ua_ref, ug_ref, o_ref, ca_ref, cg_ref):
        first = (pl.program_id(1) % nts) == 0
        hcat = jnp.concatenate([hp_ref[...], h_ref[...]], axis=0)
        zero = jnp.zeros((pre, tn), F32)
        for w_ref, cat, u_ref in ((wua_ref, ca_ref, ua_ref), (wug_ref, cg_ref, ug_ref)):
            ue = _nn(hcat, w_ref[...])
            cat[0:pre, :] = jnp.where(first, zero, ue[0:pre])
            cat[pre:, :] = ue[pre:]
            u_ref[...] = ue[pre:]
        for c0 in range(0, tm, FFN_CHUNK):
            ya = _conv(ca_ref, wa_ref, ba_ref, c0, FFN_CHUNK, pre)
            yg = _conv(cg_ref, wg_ref, bg_ref, c0, FFN_CHUNK, pre)
            o_ref[c0:c0 + FFN_CHUNK, :] = (yg * jax.nn.sigmoid(yg) * ya).astype(BF16)

    vec = lambda r, off: pl.BlockSpec((r, tn), lambda n, t: (0, n + off))
    wcol = lambda off: pl.BlockSpec((D_MODEL, tn), lambda n, t: (0, n + off))
    tile = pl.BlockSpec((tm, tn), lambda n, t: (t, n))
    return pl.pallas_call(
        body, name="ffn_up_gate", grid=(nc, t_all // tm),
        in_specs=[pl.BlockSpec((tm, D_MODEL), lambda n, t: (t, 0)),
                  pl.BlockSpec((pre, D_MODEL), lambda n, t: (jnp.maximum(t * (tm // pre) - 1, 0), 0)),
                  wcol(0), wcol(nc), vec(3, 0), vec(3, nc), vec(1, 0), vec(1, nc)],
        out_specs=[tile, tile, tile],
        out_shape=[jax.ShapeDtypeStruct((t_all, D_FF), F32), jax.ShapeDtypeStruct((t_all, D_FF), F32),
                   jax.ShapeDtypeStruct((t_all, D_FF), BF16)],
        scratch_shapes=[pltpu.VMEM((tm + pre, tn), F32)] * 2, compiler_params=_params(),
    )(h2, h2, w_up, w_up, conv_w, conv_w, conv_b, conv_b)


def _ffn_gate_bwd(u_a, u_g, dfi, conv_w, conv_b, h2, seq):
    t_all = u_a.shape[0]
    tm, tn = FFN_TM, FFN_TN
    nc = D_FF // tn
    nts = seq // tm

    def body(ua_ref, uap_ref, uan_ref, ug_ref, ugp_ref, ugn_ref, df_ref, dfn_ref, wa_ref, wg_ref, ba_ref, bg_ref, h_ref,
             dua_ref, dug_ref, acca_ref, accg_ref, dwa_ref, dwg_ref, ca_ref, cg_ref, ya_ref, yg_ref):
        t = pl.program_id(1)
        first = (t % nts) == 0
        last = (t % nts) == nts - 1

        @pl.when(t == 0)
        def _():
            acca_ref[...] = jnp.zeros_like(acca_ref)
            accg_ref[...] = jnp.zeros_like(accg_ref)
            dwa_ref[...] = jnp.zeros_like(dwa_ref)
            dwg_ref[...] = jnp.zeros_like(dwg_ref)
        zero = jnp.zeros((HALO, tn), F32)
        for cat, cur, prv, nxt in ((ca_ref, ua_ref, uap_ref, uan_ref), (cg_ref, ug_ref, ugp_ref, ugn_ref)):
            cat[0:HALO, :] = jnp.where(first, zero, prv[...])
            cat[HALO:HALO + tm, :] = cur[...]
            cat[HALO + tm:, :] = nxt[...]
        ch = FFN_CHUNK
        sums = [[jnp.zeros((1, tn), F32) for _ in range(4)] for _ in range(2)]
        for ci, c0 in enumerate(range(0, tm, ch)):
            ya = _conv(ca_ref, wa_ref, ba_ref, c0, ch + HALO)
            yg = _conv(cg_ref, wg_ref, bg_ref, c0, ch + HALO)
            if c0 + ch < tm:
                beyond = df_ref[c0 + ch:c0 + ch + 16, :].astype(F32)[:HALO]
            else:
                beyond = jnp.where(last, 0.0, dfn_ref[...].astype(F32)[:HALO])
            dfe = jnp.concatenate([df_ref[c0:c0 + ch, :].astype(F32), beyond], axis=0)
            sg = jax.nn.sigmoid(yg)
            ya_ref[ci] = dfe * (yg * sg)
            yg_ref[ci] = dfe * ya * (sg * (1.0 + yg * (1.0 - sg)))
            for half, (dy, cat, w_ref, du_ref) in enumerate(((ya_ref, ca_ref, wa_ref, dua_ref),
                                                             (yg_ref, cg_ref, wg_ref, dug_ref))):
                d0 = dy[ci, 0:ch, :]
                du = (w_ref[2:3, :] * d0 + w_ref[1:2, :] * dy[ci, pl.ds(1, ch), :]
                      + w_ref[0:1, :] * dy[ci, pl.ds(2, ch), :])
                du_ref[c0:c0 + ch, :] = du.astype(BF16)
                for k in range(3):
                    sums[half][k] += jnp.sum(d0 * cat[pl.ds(c0 + HALO - 2 + k, ch), :], axis=0, keepdims=True)
                sums[half][3] += jnp.sum(d0, axis=0, keepdims=True)
        for half, acc in enumerate((acca_ref, accg_ref)):
            for k in range(4):
                acc[k:k + 1, :] += sums[half][k]
        ht = h_ref[...].astype(F32).T.astype(BF16)
        dwa_ref[...] += _nn(ht, dua_ref[...])
        dwg_ref[...] += _nn(ht, dug_ref[...])

    nrow = t_all // HALO
    cur = pl.BlockSpec((tm, tn), lambda n, t: (t, n))
    prev = pl.BlockSpec((HALO, tn), lambda n, t: (jnp.maximum(t * (tm // HALO) - 1, 0), n))
    nxt = pl.BlockSpec((HALO, tn), lambda n, t: (jnp.minimum((t + 1) * (tm // HALO), nrow - 1), n))
    vec = lambda r, off: pl.BlockSpec((r, tn), lambda n, t: (0, n + off))
    dnxt = pl.BlockSpec((16, tn), lambda n, t: (jnp.minimum((t + 1) * (tm // 16), t_all // 16 - 1), n))
    acc = pl.BlockSpec((8, tn), lambda n, t: (0, n))
    dw = pl.BlockSpec((D_MODEL, tn), lambda n, t: (0, n))
    return pl.pallas_call(
        body, name="ffn_gate_bwd", grid=(nc, t_all // tm),
        in_specs=[cur, prev, nxt, cur, prev, nxt, cur, dnxt, vec(3, 0), vec(3, nc), vec(1, 0), vec(1, nc),
                  pl.BlockSpec((tm, D_MODEL), lambda n, t: (t, 0))],
        out_specs=[cur, cur, acc, acc, dw, dw],
        out_shape=[jax.ShapeDtypeStruct((t_all, D_FF), BF16), jax.ShapeDtypeStruct((t_all, D_FF), BF16),
                   jax.ShapeDtypeStruct((8, D_FF), F32), jax.ShapeDtypeStruct((8, D_FF), F32),
                   jax.ShapeDtypeStruct((D_MODEL, D_FF), F32), jax.ShapeDtypeStruct((D_MODEL, D_FF), F32)],
        scratch_shapes=[pltpu.VMEM((tm + 2 * HALO, tn), F32)] * 2
        + [pltpu.VMEM((tm // FFN_CHUNK, FFN_CHUNK + HALO, tn), F32)] * 2,
        compiler_params=_params(),
    )(u_a, u_a, u_a, u_g, u_g, u_g, dfi, dfi, conv_w, conv_w, conv_b, conv_b, h2)


def _ffn_down(ffn_in, w_down, xh1, ln1_g, ln1_b, ada3, ln2_g, ln2_b, target, seq):
    t_all = xh1.shape[0]
    tm = 256
    nts = seq // tm

    def body(f_ref, w_ref, xh_ref, g1_ref, b1_ref, ada_ref, g2_ref, b2_ref, tg_ref, dr2_ref, acc_ref):
        i = pl.program_id(0)

        @pl.when(i == 0)
        def _():
            acc_ref[...] = jnp.zeros_like(acc_ref)
        ffn = _nn(f_ref[...], w_ref[...])
        x1 = xh_ref[...] * g1_ref[...] + b1_ref[...]
        r2 = ALPHA * x1 + ada_ref[0, 5:6, :] * ffn
        d = r2 - jnp.mean(r2, axis=1, keepdims=True)
        rstd = lax.rsqrt(jnp.mean(d * d, axis=1, keepdims=True) + LN_EPS)
        xh2 = d * rstd
        diff = xh2 * g2_ref[...] + b2_ref[...] - tg_ref[...]
        dy = diff * (1.0 / D_MODEL)
        dr2 = _layer_norm_bwd(dy * g2_ref[...], xh2, rstd)
        dr2_ref[...] = dr2
        acc_ref[0:1, :] += jnp.sum(dy * xh2, axis=0, keepdims=True)
        acc_ref[1:2, :] += jnp.sum(dy, axis=0, keepdims=True)
        acc_ref[2:3, :] += jnp.sum(diff * diff, axis=0, keepdims=True) * (0.5 / D_MODEL)
        acc_ref[pl.ds(8 + i // nts, 1), :] += jnp.sum(dr2 * ffn, axis=0, keepdims=True)

    tok = lambda w: pl.BlockSpec((tm, w), lambda i: (i, 0))
    vec = pl.BlockSpec((1, D_MODEL), lambda i: (0, 0))
    return pl.pallas_call(
        body, name="ffn_down", grid=(t_all // tm,),
        in_specs=[tok(D_FF), pl.BlockSpec(w_down.shape, lambda i: (0, 0)), tok(D_MODEL), vec, vec,
                  pl.BlockSpec((1, 6, D_MODEL), lambda i: (i // nts, 0, 0)), vec, vec, tok(D_MODEL)],
        out_specs=[tok(D_MODEL), pl.BlockSpec((16, D_MODEL), lambda i: (0, 0))],
        out_shape=[jax.ShapeDtypeStruct((t_all, D_MODEL), F32), jax.ShapeDtypeStruct((16, D_MODEL), F32)],
        compiler_params=_params(),
    )(ffn_in, w_down, xh1, ln1_g, ln1_b, ada3, ln2_g, ln2_b, target)


def _ffn_down_bwd(dr2, ada3, w_down, seq):
    t_all = dr2.shape[0]
    tm = 256
    nts = seq // tm

    def body(d_ref, ada_ref, w_ref, dffn_ref, dfi_ref):
        dffn = (d_ref[...] * ada_ref[0, 5:6, :]).astype(BF16)
        dffn_ref[...] = dffn
        dfi_ref[...] = _nt(dffn, w_ref[...]).astype(BF16)

    tok = lambda w: pl.BlockSpec((tm, w), lambda i: (i, 0))
    return pl.pallas_call(
        body, name="ffn_down_bwd", grid=(t_all // tm,),
        in_specs=[tok(D_MODEL), pl.BlockSpec((1, 6, D_MODEL), lambda i: (i // nts, 0, 0)),
                  pl.BlockSpec(w_down.shape, lambda i: (0, 0))],
        out_specs=[tok(D_MODEL), tok(D_FF)],
        out_shape=[jax.ShapeDtypeStruct((t_all, D_MODEL), BF16), jax.ShapeDtypeStruct((t_all, D_FF), BF16)],
        compiler_params=_params(),
    )(dr2, ada3, w_down)


def _ffn_up_bwd(du_a, du_g, w_up, dr2, xh1, rs1, mix, ada3, ln1_g, ln1_b, seq):
    t_all = dr2.shape[0]
    tm = 256
    nts = seq // tm

    def body(da_ref, dg_ref, w_ref, dr2_ref, xh_ref, rs_ref, mix_ref, ada_ref, g_ref, b_ref, dr1_ref, dmix_ref,
             acc_ref):
        i = pl.program_id(0)

        @pl.when(i == 0)
        def _():
            acc_ref[...] = jnp.zeros_like(acc_ref)
        dh2 = _nt(da_ref[...], w_ref[:, :D_FF]) + _nt(dg_ref[...], w_ref[:, D_FF:])
        xh = xh_ref[...]
        x1 = xh * g_ref[...] + b_ref[...]
        dx1 = ALPHA * dr2_ref[...] + dh2 * (1.0 + ada_ref[0, 4:5, :])
        dr1 = _layer_norm_bwd(dx1 * g_ref[...], xh, rs_ref[:, 0:1])
        dr1_ref[...] = dr1
        dmix_ref[...] = (dr1 * ada_ref[0, 2:3, :]).astype(BF16)
        b = i // nts
        acc_ref[0:1, :] += jnp.sum(dx1 * xh, axis=0, keepdims=True)
        acc_ref[1:2, :] += jnp.sum(dx1, axis=0, keepdims=True)
        acc_ref[pl.ds(8 + b, 1), :] += jnp.sum(dh2 * x1, axis=0, keepdims=True)
        acc_ref[pl.ds(16 + b, 1), :] += jnp.sum(dh2, axis=0, keepdims=True)
        acc_ref[pl.ds(24 + b, 1), :] += jnp.sum(dr1 * mix_ref[...].astype(F32), axis=0, keepdims=True)

    tok = lambda w: pl.BlockSpec((tm, w), lambda i: (i, 0))
    vec = pl.BlockSpec((1, D_MODEL), lambda i: (0, 0))
    return pl.pallas_call(
        body, name="ffn_up_bwd", grid=(t_all // tm,),
        in_specs=[tok(D_FF), tok(D_FF), pl.BlockSpec(w_up.shape, lambda i: (0, 0)), tok(D_MODEL), tok(D_MODEL),
                  tok(LANES), tok(D_MODEL), pl.BlockSpec((1, 6, D_MODEL), lambda i: (i // nts, 0, 0)), vec, vec],
        out_specs=[tok(D_MODEL), tok(D_MODEL), pl.BlockSpec((32, D_MODEL), lambda i: (0, 0))],
        out_shape=[jax.ShapeDtypeStruct((t_all, D_MODEL), F32), jax.ShapeDtypeStruct((t_all, D_MODEL), BF16),
                   jax.ShapeDtypeStruct((32, D_MODEL), F32)],
        compiler_params=_params(),
    )(du_a, du_g, w_up, dr2, xh1, rs1, mix, ada3, ln1_g, ln1_b)


def _rows(a):
    return a[:, :N_HEADS].T


def _rope_freq():
    f = np.float32(ROPE_THETA) ** (-np.arange(0, ROPE_DIMS, 2, dtype=np.float32) / np.float32(ROPE_DIMS))
    return jnp.asarray(np.tile(f.astype(np.float32), LANES // (ROPE_DIMS // 2))[None, :])


def _local_step(x, positions, target, ada3, w_in, b_fgate, gn_a, gn_b, ln1_g, ln1_b, conv_b, ln2_g, ln2_b,
                late_shards):
    nbat, seq, _ = x.shape
    t_all = nbat * seq
    xf = x.reshape(t_all, D_MODEL)
    tg = target.reshape(t_all, D_MODEL)
    pos = positions.reshape(t_all, 1)
    freq = _rope_freq()

    wqkv = jnp.concatenate([w_in[:, :3 * WIDTH], w_in[:, 3 * WIDTH + N_HEADS:]], axis=1)
    wf16 = jnp.zeros((16, D_MODEL), BF16).at[:N_HEADS].set(w_in[:, 3 * WIDTH:3 * WIDTH + N_HEADS].T)
    bf = b_fgate.reshape(N_HEADS, 1)

    perms = [_perm_matrix(TOK_TM, d, tr) for tr in (False, True) for d in DILATIONS[1:]]
    h1, za, zb1, zb4, zb16, vt, fa_t = _inproj(xf, ada3, pos, wqkv, wf16, freq, perms, seq)
    zbs = [zb1, zb4.reshape(t_all, 3 * WIDTH), zb16.reshape(t_all, 3 * WIDTH)]
    f_row = _fgate_fwd(fa_t, bf, seq)
    f_col = jnp.zeros((t_all, LANES), F32).at[:, :N_HEADS].set(f_row.T)
    oa, lse_row_a, gathered = _fox_fwd(za, vt, f_col, seq, [late_shards[n] for n in LATE])
    w_out, w_up, conv_w, w_down = (_full_from_gathered(n, g) for n, g in zip(LATE, gathered))
    o3, l3 = zip(*[_dil_fwd(zb, seq, d) for zb, d in zip(zbs, DILATIONS)])
    ob, lse_b, lse_b4, lse_b16, merged, mix, xh1, rs1, h2 = _mix_out(oa, o3, l3, gn_a, gn_b, w_out, xf, ada3, ln1_g,
                                                                      ln1_b, perms, seq)
    u_a, u_g, ffn_in = _ffn_up_gate(h2, w_up, conv_w, conv_b, seq)
    dr2, acc2 = _ffn_down(ffn_in, w_down, xh1, ln1_g, ln1_b, ada3, ln2_g, ln2_b, tg, seq)

    dffn, dfi = _ffn_down_bwd(dr2, ada3, w_down, seq)
    d_w_down = _matmul_tn(dffn, ffn_in, 512, 512, "dw_down").T
    du_a, du_g, acc_ca, acc_cg, dw_up_a, dw_up_g = _ffn_gate_bwd(u_a, u_g, dfi, conv_w, conv_b, h2, seq)
    dr1, dmix, acc1 = _ffn_up_bwd(du_a, du_g, w_up, dr2, xh1, rs1, mix, ada3, ln1_g, ln1_b, seq)
    d_w_up = jnp.concatenate([dw_up_a, dw_up_g], axis=1)

    doa, dob, dob4, dob16, dl_a, dl_b, dl_b4, dl_b16, acc_gn = _mix_out_bwd(dmix, w_out, oa, ob, gn_a, gn_b, perms, seq)
    d_w_out = _matmul_tn(merged, dmix, 512, 512, "dw_out")
    late_grads = dict(w_out=d_w_out, w_up=d_w_up, conv_w=jnp.concatenate([acc_ca[0:3], acc_cg[0:3]], axis=1),
                      w_down=d_w_down)
    dka, dva, df_k, dqt, df_q, late_parts = _fox_bwd(za, doa, f_col, lse_row_a, _rows(dl_a), seq,
                                                     [_payload(n, _dest_major(n, late_grads[n])) for n in LATE])
    dfa_t, dbf = _fgate_bwd(_rows(df_k) + df_q, fa_t, bf, seq)
    flat = lambda a: a.reshape(t_all, a.shape[-1])
    dil = []
    for zb, d, do, lse, dl in zip(zbs, DILATIONS, (dob, flat(dob4), flat(dob16)),
                                  (lse_b, flat(lse_b4), flat(lse_b16)), (dl_b, flat(dl_b4), flat(dl_b16))):
        dil.append((_dil_bwd_dq(zb, do, lse, dl, seq, d), *_dil_bwd_dkv(zb, do, lse, dl, seq, d)))
    dfa16 = jnp.zeros((16, t_all), BF16).at[:N_HEADS].set(dfa_t.astype(BF16))
    grad_x, dz, acc0 = _inproj_bwd(dqt, dka, dva, dil[0], dil[1], dil[2], dfa16, pos, wqkv, wf16, freq, perms, dr1, xf,
                                   ada3, seq)
    d_wqkv = _matmul_tn(h1, dz, 512, 512, "dw_in")
    d_wf = _matmul_rows(dfa16, h1, 512, "dw_fgate")[:N_HEADS].T
    d_w_in = jnp.concatenate([d_wqkv[:, :3 * WIDTH], d_wf, d_wqkv[:, 3 * WIDTH:]], axis=1)

    dada = jnp.concatenate([acc0[8:8 + nbat], acc0[:nbat], acc1[24:24 + nbat], acc1[16:16 + nbat], acc1[8:8 + nbat],
                            acc2[8:8 + nbat]], axis=1)

    grads = dict(
        dada=dada, b_ada=jnp.sum(dada, axis=0, keepdims=True), w_in=d_w_in, b_fgate=dbf[:, 0][None, :],
        gn_a=acc_gn[0:1, :WIDTH], gn_b=acc_gn[0:1, WIDTH:], ln1_g=acc1[0:1], ln1_b=acc1[1:2],
        conv_b=jnp.concatenate([acc_ca[3:4], acc_cg[3:4]], axis=1), ln2_g=acc2[0:1], ln2_b=acc2[1:2])
    return acc2[2:3], grad_x.reshape(x.shape), grads, dict(zip(LATE, late_parts))


LATE = ("w_out", "w_up", "conv_w", "w_down")
BIG = ("w_ada", "w_in") + LATE
COLUMN_SHARDED = ("w_ada", "w_in", "w_up", "conv_w")


def _payload(name, a):
    return a if name == "conv_w" else a.astype(BF16)
SMALL = ("b_ada", "b_fgate", "gn_a", "gn_b", "ln1_g", "ln1_b", "conv_b", "ln2_g", "ln2_b")
ADAM_ROWS = dict(w_ada=256, w_in=256, w_out=128, w_up=256, conv_w=3, w_down=176)
SMALL_ROWS = 24


def _full_from_gathered(name, g):
    if name in COLUMN_SHARDED:
        return g.transpose(1, 0, 2).reshape(g.shape[1], N_DEV * g.shape[2])
    return g.reshape(N_DEV * g.shape[1], g.shape[2])


def _dest_major(name, full):
    if name in COLUMN_SHARDED:
        r, cfull = full.shape
        return full.reshape(r, N_DEV, cfull // N_DEV).transpose(1, 0, 2)
    return full.reshape(N_DEV, full.shape[0] // N_DEV, full.shape[1])


def _pack_small(vals, extra=None):
    parts = [vals[n].reshape(-1) for n in SMALL]
    if extra is not None:
        parts.append(extra.reshape(-1))
    flat = jnp.concatenate(parts)
    return jnp.pad(flat, (0, SMALL_ROWS * D_MODEL - flat.shape[0])).reshape(SMALL_ROWS, D_MODEL)


def _unpack_small(packed, like):
    flat = packed.reshape(-1)
    out, off = {}, 0
    for n in SMALL:
        size = like[n].size
        out[n] = flat[off:off + size].reshape(like[n].shape)
        off += size
    return out, flat[off:off + D_MODEL]


def kernel(x, c, positions, w_ada, b_ada, w_in, b_fgate, gn_a, gn_b, w_out, ln1_g, ln1_b, w_up, conv_w, conv_b, w_down, ln2_g, ln2_b, loss_target, m_w_ada, m_b_ada, m_w_in, m_b_fgate, m_gn_a, m_gn_b, m_w_out, m_ln1_g, m_ln1_b, m_w_up, m_conv_w, m_conv_b, m_w_down, m_ln2_g, m_ln2_b, v_w_ada, v_b_ada, v_w_in, v_b_fgate, v_gn_a, v_gn_b, v_w_out, v_ln1_g, v_ln1_b, v_w_up, v_conv_w, v_conv_b, v_w_down, v_ln2_g, v_ln2_b):
    w = dict(w_ada=w_ada[0], b_ada=b_ada, w_in=w_in[0], b_fgate=b_fgate, gn_a=gn_a, gn_b=gn_b, w_out=w_out[0],
             ln1_g=ln1_g, ln1_b=ln1_b, w_up=w_up[0], conv_w=conv_w[0], conv_b=conv_b, w_down=w_down[0], ln2_g=ln2_g,
             ln2_b=ln2_b)
    m = dict(w_ada=m_w_ada[0], b_ada=m_b_ada, w_in=m_w_in[0], b_fgate=m_b_fgate, gn_a=m_gn_a, gn_b=m_gn_b,
             w_out=m_w_out[0], ln1_g=m_ln1_g, ln1_b=m_ln1_b, w_up=m_w_up[0], conv_w=m_conv_w[0], conv_b=m_conv_b,
             w_down=m_w_down[0], ln2_g=m_ln2_g, ln2_b=m_ln2_b)
    v = dict(w_ada=v_w_ada[0], b_ada=v_b_ada, w_in=v_w_in[0], b_fgate=v_b_fgate, gn_a=v_gn_a, gn_b=v_gn_b,
             w_out=v_w_out[0], ln1_g=v_ln1_g, ln1_b=v_ln1_b, w_up=v_w_up[0], conv_w=v_conv_w[0], conv_b=v_conv_b,
             w_down=v_w_down[0], ln2_g=v_ln2_g, ln2_b=v_ln2_b)

    nbat = x.shape[0]
    me = 4 * lax.axis_index("x") + 2 * lax.axis_index("y") + lax.axis_index("c")
    ada_cols = w["w_ada"].shape[1]

    c_all, w_in_all = _exchange([c, _payload("w_in", w["w_in"])], [True, True], "weight_gather")
    c_all = c_all.reshape(N_DEV * nbat, D_MODEL)
    ada_mine = _ada_fwd(c_all, w["w_ada"], lax.dynamic_slice(b_ada, (0, me * ada_cols), (1, ada_cols)))
    (ada_parts,) = _exchange([ada_mine.reshape(N_DEV, nbat, ada_cols)], [False], "ada_exchange")
    ada3 = ada_parts.transpose(1, 0, 2).reshape(nbat, 6, D_MODEL)

    loss_lanes, grad_x, g_local, parts = _local_step(
        x, positions, loss_target, ada3, _full_from_gathered("w_in", w_in_all), b_fgate, gn_a, gn_b, ln1_g, ln1_b,
        conv_b, ln2_g, ln2_b, {n: _payload(n, w[n]) for n in LATE})

    parts["w_in"], dada_all, small_all = _exchange(
        [_payload("w_in", _dest_major("w_in", g_local["w_in"])), g_local["dada"], _pack_small(g_local, loss_lanes)],
        [False, True, True], "grad_exchange")
    dada_cols = lax.dynamic_slice(dada_all.reshape(N_DEV * nbat, 6 * D_MODEL), (0, me * ada_cols),
                                  (N_DEV * nbat, ada_cols))
    parts["w_ada"] = _ada_bwd(c_all, dada_cols)[None]

    grad, delta, new_m, new_v = {}, {}, {}, {}
    for n in BIG:
        grad[n], delta[n], new_m[n], new_v[n] = (
            a[None] for a in _adamw(parts[n], w[n], m[n], v[n], ADAM_ROWS[n], "adamw_" + n))
    packed = _adamw(small_all, _pack_small(w), _pack_small(m), _pack_small(v), SMALL_ROWS, "adamw_small")
    for dst, pk in zip((grad, delta, new_m, new_v), packed):
        vals, lanes = _unpack_small(pk, w)
        dst.update(vals)
        if dst is grad:
            loss = jnp.sum(lanes)

    order = ("w_ada", "b_ada", "w_in", "b_fgate", "gn_a", "gn_b", "w_out", "ln1_g", "ln1_b", "w_up", "conv_w", "conv_b",
             "w_down", "ln2_g", "ln2_b")
    return (loss, grad_x, *[grad[n] for n in order], *[delta[n] for n in order], *[new_m[n] for n in order],
            *[new_v[n] for n in order])
```

```python
import functools

import numpy as np
import jax
import jax.numpy as jnp
from jax import lax
from jax.experimental import pallas as pl
from jax.experimental.pallas import tpu as pltpu

F32, BF16 = jnp.float32, jnp.bfloat16
HIGHEST = lax.Precision.HIGHEST
MESH = pl.DeviceIdType.MESH
ANY = pl.BlockSpec(memory_space=pl.ANY)

D_MODEL = 1024
N_HEADS = 8
HEAD_DIM = 64
WIDTH = 512
D_FF = 2816
N_DEV = 8
ROPE_DIMS = 16
ROPE_THETA = 500000.0
ALPHA = 2.0 ** 0.25
LN_EPS = 1e-5
RMS_EPS = 1e-6
NEG = -1e30
Q_SCALE = 0.125
BLK = 128
LANES = 128
VMEM_LIMIT_BYTES = 56 * 1024 * 1024

ADAM_LR, ADAM_B1, ADAM_B2, ADAM_EPS, ADAM_WD, ADAM_STEP = 0.001, 0.9, 0.999, 1e-08, 0.01, 10


def _params(vmem=VMEM_LIMIT_BYTES):
    return pltpu.CompilerParams(vmem_limit_bytes=vmem)


def _nn(a, b):
    return jnp.dot(a, b, preferred_element_type=F32)


def _nt(a, b):
    return lax.dot_general(a, b, (((1,), (1,)), ((), ())), preferred_element_type=F32)


def _tn(a, b):
    return lax.dot_general(a, b, (((0,), (0,)), ((), ())), preferred_element_type=F32)


def _head_mats():
    r = lax.broadcasted_iota(jnp.int32, (LANES, WIDTH), 0)
    c = lax.broadcasted_iota(jnp.int32, (LANES, WIDTH), 1)
    e = ((c >> 6) == r).astype(BF16)
    r2 = lax.broadcasted_iota(jnp.int32, (WIDTH, LANES), 0)
    c2 = lax.broadcasted_iota(jnp.int32, (WIDTH, LANES), 1)
    et = ((r2 >> 6) == c2).astype(BF16)
    return e, et


def _split3(x):
    hi = x.astype(BF16)
    r = x - hi.astype(F32)
    mid = r.astype(BF16)
    return hi, mid, (r - mid.astype(F32)).astype(BF16)


def _hexp(w, e):
    return sum(_nn(part, e) for part in _split3(w)[:2])


def _hsum(x, et):
    return sum(_nn(part, et) for part in _split3(x)[:2])


def _perm_matrix(rows, d, transpose):
    i = np.arange(rows)
    j = (i % (rows // d)) * d + i // (rows // d)
    p = np.zeros((rows, rows), np.float32)
    p[i, j] = 1.0
    return jnp.asarray(p.T if transpose else p, BF16)


def _permute_f32(p, x):
    return sum(_nn(p, part) for part in _split3(x))


def _store_classes(ref, y, d):
    n = y.shape[0] // d
    for r in range(d):
        ref[r] = y[r * n:(r + 1) * n, :]


def _load_classes(ref, d):
    return jnp.concatenate([ref[r] for r in range(d)], axis=0)


def _rope_tabs(pos_ref, fr_ref, sign):
    ang = pos_ref[...].astype(F32) * fr_ref[...]
    lane = lax.broadcasted_iota(jnp.int32, ang.shape, 1) & (HEAD_DIM - 1)
    m1 = lane < ROPE_DIMS // 2
    m2 = (lane >= ROPE_DIMS // 2) & (lane < ROPE_DIMS)
    cos = jnp.cos(ang)
    sin = jnp.sin(ang) * sign
    return (jnp.where(m1 | m2, cos, 1.0), jnp.where(m1, -sin, 0.0), jnp.where(m2, sin, 0.0))


def _rope(z, tabs):
    c, s1, s2 = tabs
    parts = []
    for p in range(z.shape[1] // LANES):
        zp = z[:, LANES * p:LANES * (p + 1)]
        parts.append(zp * c + pltpu.roll(zp, LANES - 8, 1) * s1 + pltpu.roll(zp, 8, 1) * s2)
    return jnp.concatenate(parts, axis=1)


def _half_masks(rows):
    lane = lax.broadcasted_iota(jnp.int32, (rows, LANES), 1)
    lo = lane < HEAD_DIM
    return lo, jnp.logical_not(lo)


def _layer_norm_bwd(dxh, xh, rstd):
    m1 = jnp.mean(dxh, axis=1, keepdims=True)
    m2 = jnp.mean(dxh * xh, axis=1, keepdims=True)
    return rstd * (dxh - m1 - xh * m2)


def _coords():
    return lax.axis_index("x"), lax.axis_index("y"), lax.axis_index("c")


def _peer(x, y, c, k):
    return (1 - x if k & 4 else x, 1 - y if k & 2 else y, 1 - c if k & 1 else c)


def _comm_sems(n):
    return [pltpu.SemaphoreType.DMA((N_DEV - 1, n)), pltpu.SemaphoreType.DMA((N_DEV - 1, n)),
            pltpu.SemaphoreType.DMA((n,))]


def _comm_copies(ins, outs, to_all, sems):
    send_sems, recv_sems, local_sems = sems
    x, y, c = _coords()
    me = 4 * x + 2 * y + c
    copies = [pltpu.make_async_copy(ins[t] if to_all[t] else ins[t].at[me], outs[t].at[me], local_sems.at[t])
              for t in range(len(ins))]
    for k in range(1, N_DEV):
        px, py, pc = _peer(x, y, c, k)
        dest = 4 * px + 2 * py + pc
        for t in range(len(ins)):
            copies.append(pltpu.make_async_remote_copy(
                src_ref=ins[t] if to_all[t] else ins[t].at[dest], dst_ref=outs[t].at[me],
                send_sem=send_sems.at[k - 1, t], recv_sem=recv_sems.at[k - 1, t],
                device_id=(px, py, pc), device_id_type=MESH))
    return copies


def _comm_out_shapes(ins, to_all):
    return [jax.ShapeDtypeStruct(((N_DEV,) + a.shape) if ta else a.shape, a.dtype) for a, ta in zip(ins, to_all)]


def _exchange(ins, to_all, name):
    n = len(ins)

    def body(*refs):
        copies = _comm_copies(refs[:n], refs[n:2 * n], to_all, refs[2 * n:])
        for cp in copies:
            cp.start()
        for cp in copies:
            cp.wait()

    return pl.pallas_call(
        body, name=name, out_shape=_comm_out_shapes(ins, to_all), in_specs=[ANY] * n, out_specs=[ANY] * n,
        scratch_shapes=_comm_sems(n),
    )(*ins)


def _adamw(parts, w, m, v, rows, name):
    n_parts, r_all, cols = parts.shape
    c1 = 1.0 - ADAM_B1 ** ADAM_STEP
    c2 = 1.0 - ADAM_B2 ** ADAM_STEP

    def body(p_ref, w_ref, m_ref, v_ref, g_ref, d_ref, mo_ref, vo_ref):
        g = p_ref[0].astype(F32)
        for s in range(1, n_parts):
            g = g + p_ref[s].astype(F32)
        mn = ADAM_B1 * m_ref[...] + (1.0 - ADAM_B1) * g
        vn = ADAM_B2 * v_ref[...] + (1.0 - ADAM_B2) * (g * g)
        m_hat = mn / c1
        v_hat = vn / c2
        g_ref[...] = g
        d_ref[...] = -ADAM_LR * (m_hat / (jnp.sqrt(v_hat) + ADAM_EPS) + ADAM_WD * w_ref[...])
        mo_ref[...] = mn
        vo_ref[...] = vn

    spec = pl.BlockSpec((rows, cols), lambda i: (i, 0))
    return pl.pallas_call(
        body, name=name, grid=(r_all // rows,),
        in_specs=[pl.BlockSpec((n_parts, rows, cols), lambda i: (0, i, 0)), spec, spec, spec],
        out_specs=[spec] * 4, out_shape=[jax.ShapeDtypeStruct((r_all, cols), F32)] * 4,
        compiler_params=_params(),
    )(parts, w, m, v)


def _matmul_tn(a, b, chunk, tk, name):
    t_all, k1 = a.shape
    n = b.shape[1]

    def body(a_ref, b_ref, o_ref):
        @pl.when(pl.program_id(0) == 0)
        def _():
            o_ref[...] = jnp.zeros_like(o_ref)
        at = a_ref[...].astype(F32).T.astype(BF16)
        for j in range(0, n, chunk):
            cs = slice(j, min(j + chunk, n))
            o_ref[:, cs] += _nn(at, b_ref[:, cs])

    return pl.pallas_call(
        body, name=name, grid=(t_all // tk,),
        in_specs=[pl.BlockSpec((tk, k1), lambda t: (t, 0)), pl.BlockSpec((tk, n), lambda t: (t, 0))],
        out_specs=pl.BlockSpec((k1, n), lambda t: (0, 0)),
        out_shape=jax.ShapeDtypeStruct((k1, n), F32), compiler_params=_params(),
    )(a, b)


def _matmul_rows(a, b, tk, name):
    r, t_all = a.shape
    n = b.shape[1]

    def body(a_ref, b_ref, o_ref):
        @pl.when(pl.program_id(0) == 0)
        def _():
            o_ref[...] = jnp.zeros_like(o_ref)
        o_ref[...] += _nn(a_ref[...], b_ref[...])

    return pl.pallas_call(
        body, name=name, grid=(t_all // tk,),
        in_specs=[pl.BlockSpec((r, tk), lambda t: (0, t)), pl.BlockSpec((tk, n), lambda t: (t, 0))],
        out_specs=pl.BlockSpec((r, n), lambda t: (0, 0)),
        out_shape=jax.ShapeDtypeStruct((r, n), F32), compiler_params=_params(),
    )(a, b)


def _ada_fwd(c_all, w_ada, b_ada):
    whole = lambda a: pl.BlockSpec(a.shape, lambda j: (0, 0))

    def body(c_ref, w_ref, b_ref, o_ref):
        cv = c_ref[...]
        s = (cv * jax.nn.sigmoid(cv)).astype(BF16)
        o_ref[...] = _nn(s, w_ref[...].astype(BF16)) + b_ref[...]

    out = jax.ShapeDtypeStruct((c_all.shape[0], w_ada.shape[1]), F32)
    return pl.pallas_call(
        body, name="ada_fwd", grid=(1,), in_specs=[whole(c_all), whole(w_ada), whole(b_ada)], out_specs=whole(out),
        out_shape=out, compiler_params=_params(),
    )(c_all, w_ada, b_ada)


def _ada_bwd(c_all, dada):
    whole = lambda a: pl.BlockSpec(a.shape, lambda j: (0, 0))

    def body(c_ref, d_ref, o_ref):
        cv = c_ref[...]
        s = (cv * jax.nn.sigmoid(cv)).astype(BF16)
        o_ref[...] = _tn(s, d_ref[...].astype(BF16))

    out = jax.ShapeDtypeStruct((D_MODEL, dada.shape[1]), F32)
    return pl.pallas_call(
        body, name="ada_bwd", grid=(1,), in_specs=[whole(c_all), whole(dada)], out_specs=whole(out), out_shape=out,
        compiler_params=_params(),
    )(c_all, dada)


TOK_TM = 256
DILATIONS = (1, 4, 16)


def _class_spec(d, width, nts):
    return pl.BlockSpec((d, TOK_TM // d, width), lambda i: (i // nts, i % nts, 0))


def _class_shape(t_all, seq, d, width, dtype):
    return jax.ShapeDtypeStruct((t_all // seq * d, seq // d, width), dtype)


def _inproj(x, ada3, pos, wqkv, wf16, freq, perms, seq):
    t_all = x.shape[0]
    tm = TOK_TM
    nts = seq // tm

    def body(x_ref, ada_ref, pos_ref, w_ref, wf_ref, fr_ref, p4_ref, p16_ref, h1_ref, za_ref, zb_ref, zb4_ref,
             zb16_ref, vt_ref, fa_ref):
        h1 = (x_ref[...] * (1.0 + ada_ref[0, 1:2, :]) + ada_ref[0, 0:1, :]).astype(BF16)
        h1_ref[...] = h1
        tabs = _rope_tabs(pos_ref, fr_ref, 1.0)
        for n in range(6):
            z = _nn(h1, w_ref[:, n * WIDTH:(n + 1) * WIDTH])
            if n in (3, 4):
                z = _rope(z, tabs)
            if n in (0, 3):
                z = z * Q_SCALE
            if n == 2:
                vt_ref[...] = z.T.astype(BF16)
            dst = za_ref if n < 3 else zb_ref
            dst[:, (n % 3) * WIDTH:(n % 3 + 1) * WIDTH] = z.astype(BF16)
        fa_ref[...] = _nt(wf_ref[...], h1)[:N_HEADS]
        zb = zb_ref[...]
        _store_classes(zb4_ref, _nn(p4_ref[...], zb).astype(BF16), 4)
        _store_classes(zb16_ref, _nn(p16_ref[...], zb).astype(BF16), 16)

    tok = lambda w: pl.BlockSpec((tm, w), lambda i: (i, 0))
    whole = lambda a: pl.BlockSpec(a.shape, lambda i: (0, 0))
    return pl.pallas_call(
        body, name="inproj", grid=(t_all // tm,),
        in_specs=[tok(D_MODEL), pl.BlockSpec((1, 6, D_MODEL), lambda i: (i // nts, 0, 0)), tok(1), whole(wqkv),
                  whole(wf16), pl.BlockSpec((1, LANES), lambda i: (0, 0)), whole(perms[0]), whole(perms[1])],
        out_specs=[tok(D_MODEL), tok(3 * WIDTH), tok(3 * WIDTH), _class_spec(4, 3 * WIDTH, nts),
                   _class_spec(16, 3 * WIDTH, nts), pl.BlockSpec((WIDTH, tm), lambda i: (i // nts, i % nts)),
                   pl.BlockSpec((N_HEADS, tm), lambda i: (0, i))],
        out_shape=[jax.ShapeDtypeStruct((t_all, D_MODEL), BF16), jax.ShapeDtypeStruct((t_all, 3 * WIDTH), BF16),
                   jax.ShapeDtypeStruct((t_all, 3 * WIDTH), BF16), _class_shape(t_all, seq, 4, 3 * WIDTH, BF16),
                   _class_shape(t_all, seq, 16, 3 * WIDTH, BF16),
                   jax.ShapeDtypeStruct((t_all // seq * WIDTH, seq), BF16),
                   jax.ShapeDtypeStruct((N_HEADS, t_all), F32)],
        compiler_params=_params(),
    )(x, ada3, pos, wqkv, wf16, freq, perms[0], perms[1])


def _chunk_rows(a_t, seq):
    t_all = a_t.shape[1]
    return a_t.reshape(N_HEADS, t_all // seq, seq // LANES, LANES).transpose(1, 0, 2, 3).reshape(-1, LANES)


def _unchunk_rows(a, seq):
    nbat = a.shape[0] * LANES // (N_HEADS * seq)
    return a.reshape(nbat, N_HEADS, seq // LANES, LANES).transpose(1, 0, 2, 3).reshape(N_HEADS, nbat * seq)


def _chunk_carry(tot, nchunk, later):
    rows = tot.shape[0]
    r = lax.broadcasted_iota(jnp.int32, (rows, rows), 0)
    c = lax.broadcasted_iota(jnp.int32, (rows, rows), 1)
    sel = ((r // nchunk) == (c // nchunk)) & ((c > r) if later else (c < r))
    mat = sel.astype(BF16)
    return sum(_nn(mat, part) for part in _split3(jnp.broadcast_to(tot, (rows, LANES))))


def _fgate_fwd(fa_t, bf, seq):
    x = _chunk_rows(fa_t, seq)
    rows = x.shape[0]
    nchunk = seq // LANES
    bias = jnp.broadcast_to(bf.reshape(1, N_HEADS, 1), (rows // (N_HEADS * nchunk), N_HEADS, nchunk)).reshape(rows, 1)

    def body(x_ref, b_ref, f_ref):
        lane = lax.broadcasted_iota(jnp.int32, (rows, LANES), 1)
        xv = x_ref[...] + b_ref[...]
        lf = jnp.minimum(xv, 0.0) - jnp.log(1.0 + jnp.exp(-jnp.abs(xv)))
        for s in (1, 2, 4, 8, 16, 32, 64):
            lf = lf + jnp.where(lane >= s, pltpu.roll(lf, s, 1), 0.0)
        f_ref[...] = lf + _chunk_carry(lf[:, LANES - 1:LANES], nchunk, False)

    whole = lambda a: pl.BlockSpec(a.shape, lambda i: (0, 0))
    out = pl.pallas_call(
        body, name="fgate_fwd", grid=(1,), in_specs=[whole(x), whole(bias)], out_specs=whole(x),
        out_shape=jax.ShapeDtypeStruct(x.shape, F32), compiler_params=_params(),
    )(x, bias)
    return _unchunk_rows(out, seq)


def _fgate_bwd(df_t, fa_t, bf, seq):
    d_in = _chunk_rows(df_t, seq)
    x = _chunk_rows(fa_t, seq)
    rows = x.shape[0]
    nchunk = seq // LANES
    bias = jnp.broadcast_to(bf.reshape(1, N_HEADS, 1), (rows // (N_HEADS * nchunk), N_HEADS, nchunk)).reshape(rows, 1)

    def body(d_ref, x_ref, b_ref, o_ref, s_ref):
        lane = lax.broadcasted_iota(jnp.int32, (rows, LANES), 1)
        d = d_ref[...]
        for s in (1, 2, 4, 8, 16, 32, 64):
            d = d + jnp.where(lane < LANES - s, pltpu.roll(d, LANES - s, 1), 0.0)
        d = d + _chunk_carry(d[:, 0:1], nchunk, True)
        dfa = d * jax.nn.sigmoid(-(x_ref[...] + b_ref[...]))
        o_ref[...] = dfa
        g = lax.broadcasted_iota(jnp.int32, (2 * N_HEADS, rows), 0)
        r = lax.broadcasted_iota(jnp.int32, (2 * N_HEADS, rows), 1)
        group = (((r // nchunk) % N_HEADS) == g).astype(BF16)
        per_head = sum(_nn(group, part) for part in _split3(dfa))[:N_HEADS]
        s_ref[...] = jnp.broadcast_to(jnp.sum(per_head, axis=1, keepdims=True), (N_HEADS, LANES))

    whole = lambda a: pl.BlockSpec(a.shape, lambda i: (0, 0))
    dfa, sums = pl.pallas_call(
        body, name="fgate_bwd", grid=(1,), in_specs=[whole(d_in), whole(x), whole(bias)],
        out_specs=[whole(x), pl.BlockSpec((N_HEADS, LANES), lambda i: (0, 0))],
        out_shape=[jax.ShapeDtypeStruct(x.shape, F32), jax.ShapeDtypeStruct((N_HEADS, LANES), F32)],
        compiler_params=_params(),
    )(d_in, x, bias)
    return _unchunk_rows(dfa, seq), sums


FOX_T = 256


def _fox_prep(dst, src_ref, lo, hi):
    for p in range(4):
        v = src_ref[:, LANES * p:LANES * (p + 1)]
        dst[2 * p] = jnp.where(lo, v, jnp.zeros_like(v))
        dst[2 * p + 1] = jnp.where(hi, v, jnp.zeros_like(v))


def _fox_fwd(za, vt, f_col, seq, shards):
    t_all = za.shape[0]
    tq = FOX_T
    nq = seq // tq
    nbat = t_all // seq
    n = len(shards)
    to_all = [True] * n

    def body(*refs):
        q_ref, k_ref, vt_ref, fc_ref = refs[:4]
        o_ref, lse_ref = refs[4 + n:6 + n]
        qm_sc, m_sc, l_sc, acc_sc, a_sc, st_sc, pe_sc = refs[6 + 2 * n:13 + 2 * n]
        comm = (refs[4:4 + n], refs[6 + n:6 + 2 * n], to_all, refs[13 + 2 * n:])
        i = pl.program_id(1)

        @pl.when((pl.program_id(0) == 0) & (i == 0))
        def _():
            for cp in _comm_copies(*comm):
                cp.start()
        lo, hi = _half_masks(tq)
        r = lax.broadcasted_iota(jnp.int32, (tq, tq), 0)
        c = lax.broadcasted_iota(jnp.int32, (tq, tq), 1)
        tri = c >= r
        _fox_prep(qm_sc, q_ref, lo, hi)
        m_sc[...] = jnp.full(m_sc.shape, NEG, F32)
        l_sc[...] = jnp.zeros_like(l_sc)
        acc_sc[...] = jnp.zeros_like(acc_sc)

        def block(j, masked):
            sl = pl.ds(pl.multiple_of(j * tq, tq), tq)
            for p in range(4):
                kj = k_ref[sl, LANES * p:LANES * (p + 1)]
                for h in (2 * p, 2 * p + 1):
                    st = _nt(kj, qm_sc[h]) - fc_ref[sl, h:h + 1]
                    st_sc[h] = jnp.where(tri, st, NEG) if masked else st
            for h in range(N_HEADS):
                st = st_sc[h]
                m = m_sc[h:h + 1, :]
                mn = jnp.maximum(m, jnp.max(st, axis=0, keepdims=True))
                a = jnp.exp(m - mn)
                pe = jnp.exp(st - mn)
                m_sc[h:h + 1, :] = mn
                a_sc[h:h + 1, :] = a
                l_sc[h:h + 1, :] = a * l_sc[h:h + 1, :] + jnp.sum(pe, axis=0, keepdims=True)
                pe_sc[h] = pe.astype(BF16)
            for h in range(N_HEADS):
                acc_sc[h] = a_sc[h:h + 1, :] * acc_sc[h] + _nn(vt_ref[HEAD_DIM * h:HEAD_DIM * (h + 1), sl], pe_sc[h])

        def step(j, carry):
            block(j, False)
            return carry

        lax.fori_loop(0, i, step, 0)
        block(i, True)
        lse_ref[...] = m_sc[...] + jnp.log(l_sc[...])
        for p in range(4):
            ot = jnp.concatenate([acc_sc[h] / l_sc[h:h + 1, :] for h in (2 * p, 2 * p + 1)], axis=0)
            o_ref[:, LANES * p:LANES * (p + 1)] = ot.T

        @pl.when((pl.program_id(0) == nbat - 1) & (i == nq - 1))
        def _():
            for cp in _comm_copies(*comm):
                cp.wait()

    res = pl.pallas_call(
        body, name="fox_fwd", grid=(nbat, nq),
        in_specs=[pl.BlockSpec((tq, WIDTH), lambda b, i: (b * nq + i, 0)),
                  pl.BlockSpec((seq, WIDTH), lambda b, i: (b, 1)), pl.BlockSpec((WIDTH, seq), lambda b, i: (b, 0)),
                  pl.BlockSpec((seq, LANES), lambda b, i: (b, 0))] + [ANY] * n,
        out_specs=[pl.BlockSpec((tq, WIDTH), lambda b, i: (b * nq + i, 0)),
                   pl.BlockSpec((N_HEADS, tq), lambda b, i: (0, b * nq + i))] + [ANY] * n,
        out_shape=[jax.ShapeDtypeStruct((t_all, WIDTH), F32), jax.ShapeDtypeStruct((N_HEADS, t_all), F32)]
        + _comm_out_shapes(shards, to_all),
        scratch_shapes=[pltpu.VMEM((N_HEADS, tq, LANES), BF16), pltpu.VMEM((N_HEADS, tq), F32),
                        pltpu.VMEM((N_HEADS, tq), F32), pltpu.VMEM((N_HEADS, HEAD_DIM, tq), F32),
                        pltpu.VMEM((N_HEADS, tq), F32), pltpu.VMEM((N_HEADS, tq, tq), F32),
                        pltpu.VMEM((N_HEADS, tq, tq), BF16)] + _comm_sems(n),
        compiler_params=_params(),
    )(za, za, vt, f_col, *shards)
    return res[0], res[1], res[2:]


def _fox_bwd(za, do, f_col, lse_row, dl_row, seq, grads):
    t_all = za.shape[0]
    tk = FOX_T
    nk = seq // tk
    nbat = t_all // seq
    n = len(grads)
    to_all = [False] * n

    def body(*refs):
        k_ref, v_ref, q_ref, do_ref, fc_ref, lr_ref, dr_ref = refs[:7]
        dk_ref, dv_ref, df_ref, dqt_ref, dfq_ref = refs[7 + n:12 + n]
        km_sc, vm_sc, fk_sc, dk_sc, dv_sc, cs_sc, kt_sc, st_sc, dp_sc, pt_sc, ds_sc = refs[12 + 2 * n:23 + 2 * n]
        comm = (refs[7:7 + n], refs[12 + n:12 + 2 * n], to_all, refs[23 + 2 * n:])
        j = pl.program_id(1)

        @pl.when(j == 0)
        def _():
            dqt_ref[...] = jnp.zeros_like(dqt_ref)
            dfq_ref[...] = jnp.zeros_like(dfq_ref)

        @pl.when((pl.program_id(0) == 0) & (j == 0))
        def _():
            for cp in _comm_copies(*comm):
                cp.start()
        lo, hi = _half_masks(tk)
        r = lax.broadcasted_iota(jnp.int32, (tk, tk), 0)
        c = lax.broadcasted_iota(jnp.int32, (tk, tk), 1)
        tri = c >= r
        _fox_prep(km_sc, k_ref, lo, hi)
        _fox_prep(vm_sc, v_ref, lo, hi)
        for h in range(N_HEADS):
            fk_sc[h] = jnp.broadcast_to(fc_ref[:, h:h + 1], (tk, tk))
        for p in range(4):
            kt_sc[p] = k_ref[:, LANES * p:LANES * (p + 1)].astype(F32).T.astype(BF16)
        dk_sc[...] = jnp.zeros_like(dk_sc)
        dv_sc[...] = jnp.zeros_like(dv_sc)
        cs_sc[...] = jnp.zeros_like(cs_sc)

        def block(i, masked):
            sl = pl.ds(pl.multiple_of(i * tk, tk), tk)
            for p in range(4):
                cs = slice(LANES * p, LANES * (p + 1))
                qi = q_ref[sl, cs]
                doi = do_ref[sl, cs]
                for h in (2 * p, 2 * p + 1):
                    st = _nt(km_sc[h], qi) - fk_sc[h] - lr_ref[h:h + 1, sl]
                    st_sc[h] = jnp.where(tri, st, NEG) if masked else st
                    dp_sc[h] = _nt(vm_sc[h], doi) - dr_ref[h:h + 1, sl]
            for h in range(N_HEADS):
                pt = jnp.exp(st_sc[h])
                dst = pt * dp_sc[h]
                pt_sc[h] = pt.astype(BF16)
                ds_sc[h] = dst.astype(BF16)
                cs_sc[h] += dst[:, :LANES] + dst[:, LANES:]
                dfq_ref[h:h + 1, sl] += jnp.sum(dst, axis=0, keepdims=True)
            for p in range(4):
                cs = slice(LANES * p, LANES * (p + 1))
                qi = q_ref[sl, cs]
                doi = do_ref[sl, cs]
                for h in (2 * p, 2 * p + 1):
                    dv_sc[h] += _nn(pt_sc[h], doi)
                    dk_sc[h] += _nn(ds_sc[h], qi)
                    kt = kt_sc[p, HEAD_DIM * (h % 2):HEAD_DIM * (h % 2 + 1), :]
                    dqt_ref[HEAD_DIM * h:HEAD_DIM * (h + 1), sl] += _nn(kt, ds_sc[h])

        def step(i, carry):
            block(i, False)
            return carry

        block(j, True)
        lax.fori_loop(j + 1, nk, step, 0)
        df_ref[...] = jnp.zeros_like(df_ref)
        for p in range(4):
            cs = slice(LANES * p, LANES * (p + 1))
            dk_ref[:, cs] = jnp.where(lo, dk_sc[2 * p], dk_sc[2 * p + 1]).astype(BF16)
            dv_ref[:, cs] = jnp.where(lo, dv_sc[2 * p], dv_sc[2 * p + 1]).astype(BF16)
            for h in (2 * p, 2 * p + 1):
                df_ref[:, h:h + 1] = -jnp.sum(cs_sc[h], axis=1, keepdims=True)

        @pl.when(j == nk - 1)
        def _():
            dqt_ref[...] = dqt_ref[...] * Q_SCALE

        @pl.when((pl.program_id(0) == nbat - 1) & (j == nk - 1))
        def _():
            for cp in _comm_copies(*comm):
                cp.wait()

    tile = lambda w, col: pl.BlockSpec((tk, w), lambda b, j: (b * nk + j, col))
    full = lambda col: pl.BlockSpec((seq, WIDTH), lambda b, j: (b, col))
    row = pl.BlockSpec((N_HEADS, seq), lambda b, j: (0, b))
    acc = pltpu.VMEM((N_HEADS, tk, LANES), F32)
    res = pl.pallas_call(
        body, name="fox_bwd", grid=(nbat, nk),
        in_specs=[tile(WIDTH, 1), tile(WIDTH, 2), full(0), full(0), tile(LANES, 0), row, row] + [ANY] * n,
        out_specs=[tile(WIDTH, 0), tile(WIDTH, 0), tile(LANES, 0), pl.BlockSpec((WIDTH, seq), lambda b, j: (b, 0)),
                   row] + [ANY] * n,
        out_shape=[jax.ShapeDtypeStruct((t_all, WIDTH), BF16), jax.ShapeDtypeStruct((t_all, WIDTH), BF16),
                   jax.ShapeDtypeStruct((t_all, LANES), F32), jax.ShapeDtypeStruct((nbat * WIDTH, seq), F32),
                   jax.ShapeDtypeStruct((N_HEADS, t_all), F32)] + _comm_out_shapes(grads, to_all),
        scratch_shapes=[pltpu.VMEM((N_HEADS, tk, LANES), BF16), pltpu.VMEM((N_HEADS, tk, LANES), BF16),
                        pltpu.VMEM((N_HEADS, tk, tk), F32), acc, acc, acc, pltpu.VMEM((4, LANES, tk), BF16),
                        pltpu.VMEM((N_HEADS, tk, tk), F32), pltpu.VMEM((N_HEADS, tk, tk), F32),
                        pltpu.VMEM((N_HEADS, tk, tk), BF16), pltpu.VMEM((N_HEADS, tk, tk), BF16)]
        + _comm_sems(n),
        compiler_params=_params(),
    )(za, za, za, do, f_col, lse_row, dl_row, *grads)
    return res[0], res[1], res[2], res[3], res[4], res[5:]


DIL_SUB = 4


def _dil_mask(has_prev):
    qi = lax.broadcasted_iota(jnp.int32, (BLK, 2 * BLK), 0)
    kj = lax.broadcasted_iota(jnp.int32, (BLK, 2 * BLK), 1)
    dist = qi + BLK - kj
    band = (dist >= 0) & (dist <= BLK)
    return band if has_prev is True else band & ((kj >= BLK) | has_prev)


def _dil_geometry(t_all, seq, d):
    length = seq // d
    nbs = length // BLK
    sub = min(DIL_SUB, nbs)
    spb = nbs // sub
    tile = lambda width, col: pl.BlockSpec((BLK * sub, width), lambda s: (s, col))
    whole = lambda width, col: pl.BlockSpec((length, width), lambda s: (s // spb, col))
    return nbs, sub, spb, t_all // (BLK * sub), tile, whole


def _blk(i):
    return pl.ds(pl.multiple_of(i * BLK, BLK), BLK)


def _dil_fwd(zb, seq, d):
    t_all = zb.shape[0]
    nbs, sub, spb, steps, tile, whole = _dil_geometry(t_all, seq, d)

    def body(q_ref, k_ref, v_ref, o_ref, lse_ref, s_sc, p_sc):
        first = (pl.program_id(0) % spb) * sub
        lo, hi = _half_masks(BLK)
        lse_ref[...] = jnp.zeros_like(lse_ref)
        for j in range(sub):
            blk = first + j
            mask = _dil_mask(blk != 0 if j == 0 else True)
            for p in range(4):
                cs = slice(LANES * p, LANES * (p + 1))
                qp = q_ref[BLK * j:BLK * (j + 1), cs]
                kcat = jnp.concatenate([k_ref[_blk(jnp.maximum(blk - 1, 0)), cs], k_ref[_blk(blk), cs]], axis=0)
                for e in (0, 1):
                    qe = jnp.where(lo if e == 0 else hi, qp, jnp.zeros_like(qp))
                    s_sc[N_HEADS * j + 2 * p + e] = jnp.where(mask, _nt(qe, kcat), NEG)
        inv = []
        for i in range(N_HEADS * sub):
            s = s_sc[i]
            m = jnp.max(s, axis=1, keepdims=True)
            pe = jnp.exp(s - m)
            l = jnp.sum(pe, axis=1, keepdims=True)
            p_sc[i] = pe.astype(BF16)
            inv.append(1.0 / l)
            j, h = divmod(i, N_HEADS)
            lse_ref[BLK * j:BLK * (j + 1), h:h + 1] = m + jnp.log(l)
        for j in range(sub):
            blk = first + j
            for p in range(4):
                cs = slice(LANES * p, LANES * (p + 1))
                vcat = jnp.concatenate([v_ref[_blk(jnp.maximum(blk - 1, 0)), cs], v_ref[_blk(blk), cs]], axis=0)
                res = [_nn(p_sc[N_HEADS * j + h], vcat) * inv[N_HEADS * j + h] for h in (2 * p, 2 * p + 1)]
                o_ref[BLK * j:BLK * (j + 1), cs] = jnp.where(lo, res[0], res[1])

    return pl.pallas_call(
        body, name=f"dil_fwd_{d}", grid=(steps,), in_specs=[tile(WIDTH, 0), whole(WIDTH, 1), whole(WIDTH, 2)],
        out_specs=[tile(WIDTH, 0), tile(LANES, 0)],
        out_shape=[jax.ShapeDtypeStruct((t_all, WIDTH), F32), jax.ShapeDtypeStruct((t_all, LANES), F32)],
        scratch_shapes=[pltpu.VMEM((N_HEADS * sub, BLK, 2 * BLK), F32),
                        pltpu.VMEM((N_HEADS * sub, BLK, 2 * BLK), BF16)],
        compiler_params=_params(),
    )(zb, zb, zb)


def _dil_bwd_dq(zb, do, lse, dl, seq, d):
    t_all = zb.shape[0]
    nbs, sub, spb, steps, tile, whole = _dil_geometry(t_all, seq, d)

    def body(q_ref, k_ref, v_ref, do_ref, lse_ref, dl_ref, dq_ref, s_sc, dp_sc, ds_sc):
        first = (pl.program_id(0) % spb) * sub
        lo, hi = _half_masks(BLK)
        for j in range(sub):
            blk = first + j
            rows = slice(BLK * j, BLK * (j + 1))
            mask = _dil_mask(blk != 0 if j == 0 else True)
            prev = _blk(jnp.maximum(blk - 1, 0))
            for p in range(4):
                cs = slice(LANES * p, LANES * (p + 1))
                qp = q_ref[rows, cs]
                dop = do_ref[rows, cs]
                kcat = jnp.concatenate([k_ref[prev, cs], k_ref[_blk(blk), cs]], axis=0)
                vcat = jnp.concatenate([v_ref[prev, cs], v_ref[_blk(blk), cs]], axis=0)
                for e in (0, 1):
                    h = 2 * p + e
                    sel = lo if e == 0 else hi
                    qe = jnp.where(sel, qp, jnp.zeros_like(qp))
                    doe = jnp.where(sel, dop, jnp.zeros_like(dop))
                    s_sc[N_HEADS * j + h] = jnp.where(mask, _nt(qe, kcat) - lse_ref[rows, h:h + 1], NEG)
                    dp_sc[N_HEADS * j + h] = _nt(doe, vcat) - dl_ref[rows, h:h + 1]
        for i in range(N_HEADS * sub):
            ds_sc[i] = (jnp.exp(s_sc[i]) * dp_sc[i]).astype(BF16)
        for j in range(sub):
            blk = first + j
            for p in range(4):
                cs = slice(LANES * p, LANES * (p + 1))
                kcat = jnp.concatenate([k_ref[_blk(jnp.maximum(blk - 1, 0)), cs], k_ref[_blk(blk), cs]], axis=0)
                i = N_HEADS * j + 2 * p
                dq_ref[BLK * j:BLK * (j + 1), cs] = (
                    jnp.where(lo, _nn(ds_sc[i], kcat), _nn(ds_sc[i + 1], kcat)) * Q_SCALE).astype(BF16)

    wide = pltpu.VMEM((N_HEADS * sub, BLK, 2 * BLK), F32)
    return pl.pallas_call(
        body, name=f"dil_bwd_dq_{d}", grid=(steps,),
        in_specs=[tile(WIDTH, 0), whole(WIDTH, 1), whole(WIDTH, 2), tile(WIDTH, 0), tile(LANES, 0), tile(LANES, 0)],
        out_specs=tile(WIDTH, 0), out_shape=jax.ShapeDtypeStruct((t_all, WIDTH), BF16),
        scratch_shapes=[wide, wide, pltpu.VMEM((N_HEADS * sub, BLK, 2 * BLK), BF16)], compiler_params=_params(),
    )(zb, zb, zb, do, lse, dl)


def _dil_bwd_dkv(zb, do, lse, dl, seq, d):
    t_all = zb.shape[0]
    nbs, sub, spb, steps, tile, whole = _dil_geometry(t_all, seq, d)

    def body(k_ref, v_ref, q_ref, do_ref, lse_ref, dl_ref, dk_ref, dv_ref, s_sc, dp_sc, pt_sc, ds_sc):
        first = (pl.program_id(0) % spb) * sub
        r = lax.broadcasted_iota(jnp.int32, (BLK, 2 * BLK), 0)
        c = lax.broadcasted_iota(jnp.int32, (BLK, 2 * BLK), 1)
        same = (c < BLK) & (c >= r)
        later = (c >= BLK) & (c - BLK <= r)
        lo, hi = _half_masks(BLK)
        for j in range(sub):
            blk = first + j
            rows = slice(BLK * j, BLK * (j + 1))
            nxt = _blk(jnp.minimum(blk + 1, nbs - 1))
            mask = same | (later & (blk + 1 != nbs)) if j == sub - 1 else same | later
            lrows = jnp.concatenate([lse_ref[_blk(blk), :].T, lse_ref[nxt, :].T], axis=1)
            erows = jnp.concatenate([dl_ref[_blk(blk), :].T, dl_ref[nxt, :].T], axis=1)
            for p in range(4):
                cs = slice(LANES * p, LANES * (p + 1))
                kp = k_ref[rows, cs]
                vp = v_ref[rows, cs]
                qcat = jnp.concatenate([q_ref[_blk(blk), cs], q_ref[nxt, cs]], axis=0)
                dcat = jnp.concatenate([do_ref[_blk(blk), cs], do_ref[nxt, cs]], axis=0)
                for e in (0, 1):
                    h = 2 * p + e
                    sel = lo if e == 0 else hi
                    ke = jnp.where(sel, kp, jnp.zeros_like(kp))
                    ve = jnp.where(sel, vp, jnp.zeros_like(vp))
                    s_sc[N_HEADS * j + h] = jnp.where(mask, _nt(ke, qcat) - lrows[h:h + 1, :], NEG)
                    dp_sc[N_HEADS * j + h] = _nt(ve, dcat) - erows[h:h + 1, :]
        for i in range(N_HEADS * sub):
            pt = jnp.exp(s_sc[i])
            pt_sc[i] = pt.astype(BF16)
            ds_sc[i] = (pt * dp_sc[i]).astype(BF16)
        for j in range(sub):
            blk = first + j
            rows = slice(BLK * j, BLK * (j + 1))
            nxt = _blk(jnp.minimum(blk + 1, nbs - 1))
            for p in range(4):
                cs = slice(LANES * p, LANES * (p + 1))
                qcat = jnp.concatenate([q_ref[_blk(blk), cs], q_ref[nxt, cs]], axis=0)
                dcat = jnp.concatenate([do_ref[_blk(blk), cs], do_ref[nxt, cs]], axis=0)
                i = N_HEADS * j + 2 * p
                dk_ref[rows, cs] = jnp.where(lo, _nn(ds_sc[i], qcat), _nn(ds_sc[i + 1], qcat)).astype(BF16)
                dv_ref[rows, cs] = jnp.where(lo, _nn(pt_sc[i], dcat), _nn(pt_sc[i + 1], dcat)).astype(BF16)

    wide = pltpu.VMEM((N_HEADS * sub, BLK, 2 * BLK), F32)
    half = pltpu.VMEM((N_HEADS * sub, BLK, 2 * BLK), BF16)
    return pl.pallas_call(
        body, name=f"dil_bwd_dkv_{d}", grid=(steps,),
        in_specs=[tile(WIDTH, 1), tile(WIDTH, 2), whole(WIDTH, 0), whole(WIDTH, 0), whole(LANES, 0), whole(LANES, 0)],
        out_specs=[tile(WIDTH, 0), tile(WIDTH, 0)], out_shape=[jax.ShapeDtypeStruct((t_all, WIDTH), BF16)] * 2,
        scratch_shapes=[wide, wide, half, half], compiler_params=_params(),
    )(zb, zb, zb, do, lse, dl)


def _mix_out(oa, o3, l3, gn_a, gn_b, w_out, x, ada3, ln_g, ln_b, perms, seq):
    t_all = x.shape[0]
    tm = TOK_TM
    nts = seq // tm

    def body(oa_ref, o1_ref, o2_ref, o3_ref, l1_ref, l2_ref, l3_ref, ga_ref, gb_ref, w_ref, x_ref, ada_ref, g_ref,
             b_ref, p4_ref, p16_ref, pt4_ref, pt16_ref, ob_ref, lse_ref, lse4_ref, lse16_ref, mg_ref, mix_ref, xh_ref,
             rs_ref, h2_ref, h2t_ref):
        e, et = _head_mats()
        la = l1_ref[...]
        lb = _permute_f32(pt4_ref[...], _load_classes(l2_ref, 4))
        lc = _permute_f32(pt16_ref[...], _load_classes(l3_ref, 16))
        mx = jnp.maximum(jnp.maximum(la, lb), lc)
        ea, eb, ec = jnp.exp(la - mx), jnp.exp(lb - mx), jnp.exp(lc - mx)
        tot = ea + eb + ec
        lse = mx + jnp.log(tot)
        lse_ref[...] = lse
        _store_classes(lse4_ref, _permute_f32(p4_ref[...], lse), 4)
        _store_classes(lse16_ref, _permute_f32(p16_ref[...], lse), 16)
        ob = (o1_ref[...] * _hexp(ea / tot, e)
              + _permute_f32(pt4_ref[...], _load_classes(o2_ref, 4)) * _hexp(eb / tot, e)
              + _permute_f32(pt16_ref[...], _load_classes(o3_ref, 16)) * _hexp(ec / tot, e))
        ob_ref[...] = ob

        def rms(o, gain):
            rr = lax.rsqrt(_hsum(o * o, et) * (1.0 / HEAD_DIM) + RMS_EPS)
            return o * _hexp(rr, e) * gain

        merged = jnp.concatenate([rms(oa_ref[...], ga_ref[...]), rms(ob, gb_ref[...])], axis=1).astype(BF16)
        mg_ref[...] = merged
        mix = _nn(merged, w_ref[...])
        mix_ref[...] = mix.astype(BF16)
        r1 = ALPHA * x_ref[...] + ada_ref[0, 2:3, :] * mix
        d = r1 - jnp.mean(r1, axis=1, keepdims=True)
        rstd = lax.rsqrt(jnp.mean(d * d, axis=1, keepdims=True) + LN_EPS)
        xh = d * rstd
        xh_ref[...] = xh
        rs_ref[...] = jnp.broadcast_to(rstd, (tm, LANES))
        x1 = xh * g_ref[...] + b_ref[...]
        h2 = x1 * (1.0 + ada_ref[0, 4:5, :]) + ada_ref[0, 3:4, :]
        h2_ref[...] = h2.astype(BF16)
        h2t_ref[...] = h2.T.astype(BF16)

    tok = lambda w: pl.BlockSpec((tm, w), lambda i: (i, 0))
    vec = lambda w: pl.BlockSpec((1, w), lambda i: (0, 0))
    whole = lambda a: pl.BlockSpec(a.shape, lambda i: (0, 0))
    classes = lambda a, d: a.reshape(t_all // seq * d, seq // d, a.shape[-1])
    return pl.pallas_call(
        body, name="mix_out", grid=(t_all // tm,),
        in_specs=[tok(WIDTH), tok(WIDTH), _class_spec(4, WIDTH, nts), _class_spec(16, WIDTH, nts), tok(LANES),
                  _class_spec(4, LANES, nts), _class_spec(16, LANES, nts), vec(WIDTH), vec(WIDTH), whole(w_out),
                  tok(D_MODEL), pl.BlockSpec((1, 6, D_MODEL), lambda i: (i // nts, 0, 0)), vec(D_MODEL), vec(D_MODEL)]
        + [whole(p) for p in perms],
        out_specs=[tok(WIDTH), tok(LANES), _class_spec(4, LANES, nts), _class_spec(16, LANES, nts), tok(D_MODEL),
                   tok(D_MODEL), tok(D_MODEL), tok(LANES), tok(D_MODEL), pl.BlockSpec((D_MODEL, tm), lambda i: (0, i))],
        out_shape=[jax.ShapeDtypeStruct((t_all, WIDTH), F32), jax.ShapeDtypeStruct((t_all, LANES), F32),
                   _class_shape(t_all, seq, 4, LANES, F32), _class_shape(t_all, seq, 16, LANES, F32),
                   jax.ShapeDtypeStruct((t_all, D_MODEL), BF16), jax.ShapeDtypeStruct((t_all, D_MODEL), BF16),
                   jax.ShapeDtypeStruct((t_all, D_MODEL), F32), jax.ShapeDtypeStruct((t_all, LANES), F32),
                   jax.ShapeDtypeStruct((t_all, D_MODEL), BF16), jax.ShapeDtypeStruct((D_MODEL, t_all), BF16)],
        compiler_params=_params(),
    )(oa, o3[0], classes(o3[1], 4), classes(o3[2], 16), l3[0], classes(l3[1], 4), classes(l3[2], 16), gn_a, gn_b,
      w_out, x, ada3, ln_g, ln_b, *perms)


def _mix_out_bwd(dmix, w_out, oa, ob, gn_a, gn_b, perms, seq):
    t_all = dmix.shape[0]
    tm = TOK_TM
    nts = seq // tm

    def body(dm_ref, w_ref, oa_ref, ob_ref, ga_ref, gb_ref, p4_ref, p16_ref, doa_ref, dob_ref, dob4_ref, dob16_ref,
             dla_ref, dlb_ref, dlb4_ref, dlb16_ref, acc_ref):
        @pl.when(pl.program_id(0) == 0)
        def _():
            acc_ref[...] = jnp.zeros_like(acc_ref)
        e, et = _head_mats()
        dmg = _nt(dm_ref[...], w_ref[...])

        def group(o, dn, gain):
            rr = lax.rsqrt(_hsum(o * o, et) * (1.0 / HEAD_DIM) + RMS_EPS)
            re = _hexp(rr, e)
            dgain = jnp.sum(dn * o * re, axis=0, keepdims=True)
            dxn = dn * gain
            tt = _hsum(dxn * o, et) * (rr * rr * rr) * (1.0 / HEAD_DIM)
            do = re * dxn - o * _hexp(tt, e)
            return do, _hsum(do * o, et), dgain

        doa, dla, dga = group(oa_ref[...], dmg[:, :WIDTH], ga_ref[...])
        dob, dlb, dgb = group(ob_ref[...], dmg[:, WIDTH:], gb_ref[...])
        dob = dob.astype(BF16)
        doa_ref[...] = doa.astype(BF16)
        dob_ref[...] = dob
        _store_classes(dob4_ref, _nn(p4_ref[...], dob).astype(BF16), 4)
        _store_classes(dob16_ref, _nn(p16_ref[...], dob).astype(BF16), 16)
        dla_ref[...] = dla
        dlb_ref[...] = dlb
        _store_classes(dlb4_ref, _permute_f32(p4_ref[...], dlb), 4)
        _store_classes(dlb16_ref, _permute_f32(p16_ref[...], dlb), 16)
        acc_ref[0:1, :] += jnp.concatenate([dga, dgb], axis=1)

    tok = lambda w: pl.BlockSpec((tm, w), lambda i: (i, 0))
    vec = lambda w: pl.BlockSpec((1, w), lambda i: (0, 0))
    return pl.pallas_call(
        body, name="mix_out_bwd", grid=(t_all // tm,),
        in_specs=[tok(D_MODEL), pl.BlockSpec(w_out.shape, lambda i: (0, 0)), tok(WIDTH), tok(WIDTH), vec(WIDTH),
                  vec(WIDTH), pl.BlockSpec(perms[0].shape, lambda i: (0, 0)),
                  pl.BlockSpec(perms[1].shape, lambda i: (0, 0))],
        out_specs=[tok(WIDTH), tok(WIDTH), _class_spec(4, WIDTH, nts), _class_spec(16, WIDTH, nts), tok(LANES),
                   tok(LANES), _class_spec(4, LANES, nts), _class_spec(16, LANES, nts),
                   pl.BlockSpec((8, D_MODEL), lambda i: (0, 0))],
        out_shape=[jax.ShapeDtypeStruct((t_all, WIDTH), BF16), jax.ShapeDtypeStruct((t_all, WIDTH), BF16),
                   _class_shape(t_all, seq, 4, WIDTH, BF16), _class_shape(t_all, seq, 16, WIDTH, BF16),
                   jax.ShapeDtypeStruct((t_all, LANES), F32), jax.ShapeDtypeStruct((t_all, LANES), F32),
                   _class_shape(t_all, seq, 4, LANES, F32), _class_shape(t_all, seq, 16, LANES, F32),
                   jax.ShapeDtypeStruct((8, D_MODEL), F32)],
        compiler_params=_params(),
    )(dmix, w_out, oa, ob, gn_a, gn_b, perms[0], perms[1])


def _inproj_bwd(dqt, dka, dva, dil1, dil4, dil16, dfa16, pos, wqkv, wf16, freq, perms, dr1, x, ada3, seq):
    t_all = x.shape[0]
    tm = TOK_TM
    nts = seq // tm

    def body(dqt_ref, dka_ref, dva_ref, q1_ref, k1_ref, v1_ref, q4_ref, k4_ref, v4_ref, q16_ref, k16_ref, v16_ref,
             dfa_ref, pos_ref, w_ref, wf_ref, fr_ref, pt4_ref, pt16_ref, dr1_ref, x_ref, ada_ref, gx_ref, dz_ref,
             acc_ref):
        i = pl.program_id(0)

        @pl.when(i == 0)
        def _():
            acc_ref[...] = jnp.zeros_like(acc_ref)
        tabs = _rope_tabs(pos_ref, fr_ref, -1.0)
        dz_ref[:, :WIDTH] = dqt_ref[...].T.astype(BF16)
        dz_ref[:, WIDTH:2 * WIDTH] = dka_ref[...]
        dz_ref[:, 2 * WIDTH:3 * WIDTH] = dva_ref[...]
        for t, (n1, n4, n16) in enumerate(((q1_ref, q4_ref, q16_ref), (k1_ref, k4_ref, k16_ref),
                                           (v1_ref, v4_ref, v16_ref))):
            tot = (n1[...].astype(F32) + _nn(pt4_ref[...], _load_classes(n4, 4))
                   + _nn(pt16_ref[...], _load_classes(n16, 16)))
            if t < 2:
                tot = _rope(tot, tabs)
            dz_ref[:, (3 + t) * WIDTH:(4 + t) * WIDTH] = tot.astype(BF16)
        dh1 = _tn(dfa_ref[...], wf_ref[...])
        for n in range(6):
            cs = slice(n * WIDTH, (n + 1) * WIDTH)
            dh1 = dh1 + _nt(dz_ref[:, cs], w_ref[:, cs])
        xv = x_ref[...]
        gx_ref[...] = ALPHA * dr1_ref[...] + dh1 * (1.0 + ada_ref[0, 1:2, :])
        b = i // nts
        acc_ref[pl.ds(b, 1), :] += jnp.sum(dh1 * xv, axis=0, keepdims=True)
        acc_ref[pl.ds(8 + b, 1), :] += jnp.sum(dh1, axis=0, keepdims=True)

    tok = lambda w: pl.BlockSpec((tm, w), lambda i: (i, 0))
    whole = lambda a: pl.BlockSpec(a.shape, lambda i: (0, 0))
    classes = lambda a, d: a.reshape(t_all // seq * d, seq // d, a.shape[-1])
    return pl.pallas_call(
        body, name="inproj_bwd", grid=(t_all // tm,),
        in_specs=[pl.BlockSpec((WIDTH, tm), lambda i: (i // nts, i % nts)), tok(WIDTH), tok(WIDTH)]
        + [tok(WIDTH)] * 3 + [_class_spec(4, WIDTH, nts)] * 3 + [_class_spec(16, WIDTH, nts)] * 3
        + [pl.BlockSpec((16, tm), lambda i: (0, i)), tok(1), whole(wqkv), whole(wf16),
           pl.BlockSpec((1, LANES), lambda i: (0, 0)), whole(perms[2]), whole(perms[3]), tok(D_MODEL), tok(D_MODEL),
           pl.BlockSpec((1, 6, D_MODEL), lambda i: (i // nts, 0, 0))],
        out_specs=[tok(D_MODEL), tok(6 * WIDTH), pl.BlockSpec((16, D_MODEL), lambda i: (0, 0))],
        out_shape=[jax.ShapeDtypeStruct((t_all, D_MODEL), F32), jax.ShapeDtypeStruct((t_all, 6 * WIDTH), BF16),
                   jax.ShapeDtypeStruct((16, D_MODEL), F32)],
        compiler_params=_params(),
    )(dqt, dka, dva, *dil1, *[classes(a, 4) for a in dil4], *[classes(a, 16) for a in dil16], dfa16, pos, wqkv, wf16,
      freq, perms[2], perms[3], dr1, x, ada3)


FFN_TM = 512
FFN_TN = 256
HALO = 8


FFN_CHUNK = 64


def _conv(cat_ref, w_ref, b_ref, start, rows, halo=HALO):
    return (b_ref[...] + w_ref[0:1, :] * cat_ref[pl.ds(start + halo - 2, rows), :]
            + w_ref[1:2, :] * cat_ref[pl.ds(start + halo - 1, rows), :]
            + w_ref[2:3, :] * cat_ref[pl.ds(start + halo, rows), :])


def _ffn_up_gate(h2, w_up, conv_w, conv_b, seq):
    t_all = h2.shape[0]
    tm, tn = FFN_TM, FFN_TN
    nc = D_FF // tn
    nts = seq // tm
    pre = 16

    def body(h_ref, hp_ref, wua_ref, wug_ref, wa_ref, wg_ref, ba_ref, bg_ref, ua_ref, ug_ref, o_ref, ca_ref, cg_ref):
        first = (pl.program_id(1) % nts) == 0
        hcat = jnp.concatenate([hp_ref[...], h_ref[...]], axis=0)
        zero = jnp.zeros((pre, tn), F32)
        for w_ref, cat, u_ref in ((wua_ref, ca_ref, ua_ref), (wug_ref, cg_ref, ug_ref)):
            ue = _nn(hcat, w_ref[...])
            cat[0:pre, :] = jnp.where(first, zero, ue[0:pre])
            cat[pre:, :] = ue[pre:]
            u_ref[...] = ue[pre:]
        for c0 in range(0, tm, FFN_CHUNK):
            ya = _conv(ca_ref, wa_ref, ba_ref, c0, FFN_CHUNK, pre)
            yg = _conv(cg_ref, wg_ref, bg_ref, c0, FFN_CHUNK, pre)
            o_ref[c0:c0 + FFN_CHUNK, :] = (yg * jax.nn.sigmoid(yg) * ya).astype(BF16)

    vec = lambda r, off: pl.BlockSpec((r, tn), lambda n, t: (0, n + off))
    wcol = lambda off: pl.BlockSpec((D_MODEL, tn), lambda n, t: (0, n + off))
    tile = pl.BlockSpec((tm, tn), lambda n, t: (t, n))
    return pl.pallas_call(
        body, name="ffn_up_gate", grid=(nc, t_all // tm),
        in_specs=[pl.BlockSpec((tm, D_MODEL), lambda n, t: (t, 0)),
                  pl.BlockSpec((pre, D_MODEL), lambda n, t: (jnp.maximum(t * (tm // pre) - 1, 0), 0)),
                  wcol(0), wcol(nc), vec(3, 0), vec(3, nc), vec(1, 0), vec(1, nc)],
        out_specs=[tile, tile, tile],
        out_shape=[jax.ShapeDtypeStruct((t_all, D_FF), F32), jax.ShapeDtypeStruct((t_all, D_FF), F32),
                   jax.ShapeDtypeStruct((t_all, D_FF), BF16)],
        scratch_shapes=[pltpu.VMEM((tm + pre, tn), F32)] * 2, compiler_params=_params(),
    )(h2, h2, w_up, w_up, conv_w, conv_w, conv_b, conv_b)


def _ffn_gate_bwd(u_a, u_g, dfi, conv_w, conv_b, h2t, seq):
    t_all = u_a.shape[0]
    tm, tn = FFN_TM, FFN_TN
    nc = D_FF // tn
    nts = seq // tm

    def body(ua_ref, uap_ref, uan_ref, ug_ref, ugp_ref, ugn_ref, df_ref, dfn_ref, wa_ref, wg_ref, ba_ref, bg_ref, h_ref,
             dua_ref, dug_ref, acca_ref, accg_ref, dwa_ref, dwg_ref, ca_ref, cg_ref, ya_ref, yg_ref):
        t = pl.program_id(1)
        first = (t % nts) == 0
        last = (t % nts) == nts - 1

        @pl.when(t == 0)
        def _():
            acca_ref[...] = jnp.zeros_like(acca_ref)
            accg_ref[...] = jnp.zeros_like(accg_ref)
            dwa_ref[...] = jnp.zeros_like(dwa_ref)
            dwg_ref[...] = jnp.zeros_like(dwg_ref)
        zero = jnp.zeros((HALO, tn), F32)
        for cat, cur, prv, nxt in ((ca_ref, ua_ref, uap_ref, uan_ref), (cg_ref, ug_ref, ugp_ref, ugn_ref)):
            cat[0:HALO, :] = jnp.where(first, zero, prv[...])
            cat[HALO:HALO + tm, :] = cur[...]
            cat[HALO + tm:, :] = nxt[...]
        ch = FFN_CHUNK
        sums = [[jnp.zeros((1, tn), F32) for _ in range(4)] for _ in range(2)]
        for ci, c0 in enumerate(range(0, tm, ch)):
            ya = _conv(ca_ref, wa_ref, ba_ref, c0, ch + HALO)
            yg = _conv(cg_ref, wg_ref, bg_ref, c0, ch + HALO)
            if c0 + ch < tm:
                beyond = df_ref[c0 + ch:c0 + ch + 16, :].astype(F32)[:HALO]
            else:
                beyond = jnp.where(last, 0.0, dfn_ref[...].astype(F32)[:HALO])
            dfe = jnp.concatenate([df_ref[c0:c0 + ch, :].astype(F32), beyond], axis=0)
            sg = jax.nn.sigmoid(yg)
            ya_ref[ci] = dfe * (yg * sg)
            yg_ref[ci] = dfe * ya * (sg * (1.0 + yg * (1.0 - sg)))
            for half, (dy, cat, w_ref, du_ref) in enumerate(((ya_ref, ca_ref, wa_ref, dua_ref),
                                                             (yg_ref, cg_ref, wg_ref, dug_ref))):
                d0 = dy[ci, 0:ch, :]
                du = (w_ref[2:3, :] * d0 + w_ref[1:2, :] * dy[ci, pl.ds(1, ch), :]
                      + w_ref[0:1, :] * dy[ci, pl.ds(2, ch), :])
                du_ref[c0:c0 + ch, :] = du.astype(BF16)
                for k in range(3):
                    sums[half][k] += jnp.sum(d0 * cat[pl.ds(c0 + HALO - 2 + k, ch), :], axis=0, keepdims=True)
                sums[half][3] += jnp.sum(d0, axis=0, keepdims=True)
        for half, acc in enumerate((acca_ref, accg_ref)):
            for k in range(4):
                acc[k:k + 1, :] += sums[half][k]
        ht = h_ref[...]
        dwa_ref[...] += _nn(ht, dua_ref[...])
        dwg_ref[...] += _nn(ht, dug_ref[...])

    nrow = t_all // HALO
    cur = pl.BlockSpec((tm, tn), lambda n, t: (t, n))
    prev = pl.BlockSpec((HALO, tn), lambda n, t: (jnp.maximum(t * (tm // HALO) - 1, 0), n))
    nxt = pl.BlockSpec((HALO, tn), lambda n, t: (jnp.minimum((t + 1) * (tm // HALO), nrow - 1), n))
    vec = lambda r, off: pl.BlockSpec((r, tn), lambda n, t: (0, n + off))
    dnxt = pl.BlockSpec((16, tn), lambda n, t: (jnp.minimum((t + 1) * (tm // 16), t_all // 16 - 1), n))
    acc = pl.BlockSpec((8, tn), lambda n, t: (0, n))
    dw = pl.BlockSpec((D_MODEL, tn), lambda n, t: (0, n))
    return pl.pallas_call(
        body, name="ffn_gate_bwd", grid=(nc, t_all // tm),
        in_specs=[cur, prev, nxt, cur, prev, nxt, cur, dnxt, vec(3, 0), vec(3, nc), vec(1, 0), vec(1, nc),
                  pl.BlockSpec((D_MODEL, tm), lambda n, t: (0, t))],
        out_specs=[cur, cur, acc, acc, dw, dw],
        out_shape=[jax.ShapeDtypeStruct((t_all, D_FF), BF16), jax.ShapeDtypeStruct((t_all, D_FF), BF16),
                   jax.ShapeDtypeStruct((8, D_FF), F32), jax.ShapeDtypeStruct((8, D_FF), F32),
                   jax.ShapeDtypeStruct((D_MODEL, D_FF), F32), jax.ShapeDtypeStruct((D_MODEL, D_FF), F32)],
        scratch_shapes=[pltpu.VMEM((tm + 2 * HALO, tn), F32)] * 2
        + [pltpu.VMEM((tm // FFN_CHUNK, FFN_CHUNK + HALO, tn), F32)] * 2,
        compiler_params=_params(),
    )(u_a, u_a, u_a, u_g, u_g, u_g, dfi, dfi, conv_w, conv_w, conv_b, conv_b, h2t)


def _ffn_down(ffn_in, w_down, xh1, ln1_g, ln1_b, ada3, ln2_g, ln2_b, target, seq):
    t_all = xh1.shape[0]
    tm = 256
    nts = seq // tm

    def body(f_ref, w_ref, xh_ref, g1_ref, b1_ref, ada_ref, g2_ref, b2_ref, tg_ref, dr2_ref, acc_ref):
        i = pl.program_id(0)

        @pl.when(i == 0)
        def _():
            acc_ref[...] = jnp.zeros_like(acc_ref)
        ffn = _nn(f_ref[...], w_ref[...])
        x1 = xh_ref[...] * g1_ref[...] + b1_ref[...]
        r2 = ALPHA * x1 + ada_ref[0, 5:6, :] * ffn
        d = r2 - jnp.mean(r2, axis=1, keepdims=True)
        rstd = lax.rsqrt(jnp.mean(d * d, axis=1, keepdims=True) + LN_EPS)
        xh2 = d * rstd
        diff = xh2 * g2_ref[...] + b2_ref[...] - tg_ref[...]
        dy = diff * (1.0 / D_MODEL)
        dr2 = _layer_norm_bwd(dy * g2_ref[...], xh2, rstd)
        dr2_ref[...] = dr2
        acc_ref[0:1, :] += jnp.sum(dy * xh2, axis=0, keepdims=True)
        acc_ref[1:2, :] += jnp.sum(dy, axis=0, keepdims=True)
        acc_ref[2:3, :] += jnp.sum(diff * diff, axis=0, keepdims=True) * (0.5 / D_MODEL)
        acc_ref[pl.ds(8 + i // nts, 1), :] += jnp.sum(dr2 * ffn, axis=0, keepdims=True)

    tok = lambda w: pl.BlockSpec((tm, w), lambda i: (i, 0))
    vec = pl.BlockSpec((1, D_MODEL), lambda i: (0, 0))
    return pl.pallas_call(
        body, name="ffn_down", grid=(t_all // tm,),
        in_specs=[tok(D_FF), pl.BlockSpec(w_down.shape, lambda i: (0, 0)), tok(D_MODEL), vec, vec,
                  pl.BlockSpec((1, 6, D_MODEL), lambda i: (i // nts, 0, 0)), vec, vec, tok(D_MODEL)],
        out_specs=[tok(D_MODEL), pl.BlockSpec((16, D_MODEL), lambda i: (0, 0))],
        out_shape=[jax.ShapeDtypeStruct((t_all, D_MODEL), F32), jax.ShapeDtypeStruct((16, D_MODEL), F32)],
        compiler_params=_params(),
    )(ffn_in, w_down, xh1, ln1_g, ln1_b, ada3, ln2_g, ln2_b, target)


def _ffn_down_bwd(dr2, ada3, w_down, seq):
    t_all = dr2.shape[0]
    tm = 256
    nts = seq // tm

    def body(d_ref, ada_ref, w_ref, dffn_ref, dfi_ref):
        dffn = (d_ref[...] * ada_ref[0, 5:6, :]).astype(BF16)
        dffn_ref[...] = dffn
        dfi_ref[...] = _nt(dffn, w_ref[...]).astype(BF16)

    tok = lambda w: pl.BlockSpec((tm, w), lambda i: (i, 0))
    return pl.pallas_call(
        body, name="ffn_down_bwd", grid=(t_all // tm,),
        in_specs=[tok(D_MODEL), pl.BlockSpec((1, 6, D_MODEL), lambda i: (i // nts, 0, 0)),
                  pl.BlockSpec(w_down.shape, lambda i: (0, 0))],
        out_specs=[tok(D_MODEL), tok(D_FF)],
        out_shape=[jax.ShapeDtypeStruct((t_all, D_MODEL), BF16), jax.ShapeDtypeStruct((t_all, D_FF), BF16)],
        compiler_params=_params(),
    )(dr2, ada3, w_down)


def _ffn_up_bwd(du_a, du_g, w_up, dr2, xh1, rs1, mix, ada3, ln1_g, ln1_b, seq):
    t_all = dr2.shape[0]
    tm = 256
    nts = seq // tm

    def body(da_ref, dg_ref, w_ref, dr2_ref, xh_ref, rs_ref, mix_ref, ada_ref, g_ref, b_ref, dr1_ref, dmix_ref,
             acc_ref):
        i = pl.program_id(0)

        @pl.when(i == 0)
        def _():
            acc_ref[...] = jnp.zeros_like(acc_ref)
        dh2 = _nt(da_ref[...], w_ref[:, :D_FF]) + _nt(dg_ref[...], w_ref[:, D_FF:])
        xh = xh_ref[...]
        x1 = xh * g_ref[...] + b_ref[...]
        dx1 = ALPHA * dr2_ref[...] + dh2 * (1.0 + ada_ref[0, 4:5, :])
        dr1 = _layer_norm_bwd(dx1 * g_ref[...], xh, rs_ref[:, 0:1])
        dr1_ref[...] = dr1
        dmix_ref[...] = (dr1 * ada_ref[0, 2:3, :]).astype(BF16)
        b = i // nts
        acc_ref[0:1, :] += jnp.sum(dx1 * xh, axis=0, keepdims=True)
        acc_ref[1:2, :] += jnp.sum(dx1, axis=0, keepdims=True)
        acc_ref[pl.ds(8 + b, 1), :] += jnp.sum(dh2 * x1, axis=0, keepdims=True)
        acc_ref[pl.ds(16 + b, 1), :] += jnp.sum(dh2, axis=0, keepdims=True)
        acc_ref[pl.ds(24 + b, 1), :] += jnp.sum(dr1 * mix_ref[...].astype(F32), axis=0, keepdims=True)

    tok = lambda w: pl.BlockSpec((tm, w), lambda i: (i, 0))
    vec = pl.BlockSpec((1, D_MODEL), lambda i: (0, 0))
    return pl.pallas_call(
        body, name="ffn_up_bwd", grid=(t_all // tm,),
        in_specs=[tok(D_FF), tok(D_FF), pl.BlockSpec(w_up.shape, lambda i: (0, 0)), tok(D_MODEL), tok(D_MODEL),
                  tok(LANES), tok(D_MODEL), pl.BlockSpec((1, 6, D_MODEL), lambda i: (i // nts, 0, 0)), vec, vec],
        out_specs=[tok(D_MODEL), tok(D_MODEL), pl.BlockSpec((32, D_MODEL), lambda i: (0, 0))],
        out_shape=[jax.ShapeDtypeStruct((t_all, D_MODEL), F32), jax.ShapeDtypeStruct((t_all, D_MODEL), BF16),
                   jax.ShapeDtypeStruct((32, D_MODEL), F32)],
        compiler_params=_params(),
    )(du_a, du_g, w_up, dr2, xh1, rs1, mix, ada3, ln1_g, ln1_b)


def _rows(a):
    return a[:, :N_HEADS].T


def _rope_freq():
    f = np.float32(ROPE_THETA) ** (-np.arange(0, ROPE_DIMS, 2, dtype=np.float32) / np.float32(ROPE_DIMS))
    return jnp.asarray(np.tile(f.astype(np.float32), LANES // (ROPE_DIMS // 2))[None, :])


def _local_step(x, positions, target, ada3, w_in, b_fgate, gn_a, gn_b, ln1_g, ln1_b, conv_b, ln2_g, ln2_b,
                late_shards):
    nbat, seq, _ = x.shape
    t_all = nbat * seq
    xf = x.reshape(t_all, D_MODEL)
    tg = target.reshape(t_all, D_MODEL)
    pos = positions.reshape(t_all, 1)
    freq = _rope_freq()

    wqkv = jnp.concatenate([w_in[:, :3 * WIDTH], w_in[:, 3 * WIDTH + N_HEADS:]], axis=1)
    wf16 = jnp.zeros((16, D_MODEL), BF16).at[:N_HEADS].set(w_in[:, 3 * WIDTH:3 * WIDTH + N_HEADS].T)
    bf = b_fgate.reshape(N_HEADS, 1)

    perms = [_perm_matrix(TOK_TM, d, tr) for tr in (False, True) for d in DILATIONS[1:]]
    h1, za, zb1, zb4, zb16, vt, fa_t = _inproj(xf, ada3, pos, wqkv, wf16, freq, perms, seq)
    zbs = [zb1, zb4.reshape(t_all, 3 * WIDTH), zb16.reshape(t_all, 3 * WIDTH)]
    f_row = _fgate_fwd(fa_t, bf, seq)
    f_col = jnp.zeros((t_all, LANES), F32).at[:, :N_HEADS].set(f_row.T)
    oa, lse_row_a, gathered = _fox_fwd(za, vt, f_col, seq, [late_shards[n] for n in LATE])
    w_out, w_up, conv_w, w_down = (_full_from_gathered(n, g) for n, g in zip(LATE, gathered))
    o3, l3 = zip(*[_dil_fwd(zb, seq, d) for zb, d in zip(zbs, DILATIONS)])
    ob, lse_b, lse_b4, lse_b16, merged, mix, xh1, rs1, h2, h2t = _mix_out(oa, o3, l3, gn_a, gn_b, w_out, xf, ada3, ln1_g,
                                                                      ln1_b, perms, seq)
    u_a, u_g, ffn_in = _ffn_up_gate(h2, w_up, conv_w, conv_b, seq)
    dr2, acc2 = _ffn_down(ffn_in, w_down, xh1, ln1_g, ln1_b, ada3, ln2_g, ln2_b, tg, seq)

    dffn, dfi = _ffn_down_bwd(dr2, ada3, w_down, seq)
    d_w_down = _matmul_tn(dffn, ffn_in, 512, 512, "dw_down").T
    du_a, du_g, acc_ca, acc_cg, dw_up_a, dw_up_g = _ffn_gate_bwd(u_a, u_g, dfi, conv_w, conv_b, h2t, seq)
    dr1, dmix, acc1 = _ffn_up_bwd(du_a, du_g, w_up, dr2, xh1, rs1, mix, ada3, ln1_g, ln1_b, seq)
    d_w_up = jnp.concatenate([dw_up_a, dw_up_g], axis=1)

    doa, dob, dob4, dob16, dl_a, dl_b, dl_b4, dl_b16, acc_gn = _mix_out_bwd(dmix, w_out, oa, ob, gn_a, gn_b, perms, seq)
    d_w_out = _matmul_tn(merged, dmix, 512, 512, "dw_out")
    late_grads = dict(w_out=d_w_out, w_up=d_w_up, conv_w=jnp.concatenate([acc_ca[0:3], acc_cg[0:3]], axis=1),
                      w_down=d_w_down)
    dka, dva, df_k, dqt, df_q, late_parts = _fox_bwd(za, doa, f_col, lse_row_a, _rows(dl_a), seq,
                                                     [_payload(n, _dest_major(n, late_grads[n])) for n in LATE])
    dfa_t, dbf = _fgate_bwd(_rows(df_k) + df_q, fa_t, bf, seq)
    flat = lambda a: a.reshape(t_all, a.shape[-1])
    dil = []
    for zb, d, do, lse, dl in zip(zbs, DILATIONS, (dob, flat(dob4), flat(dob16)),
                                  (lse_b, flat(lse_b4), flat(lse_b16)), (dl_b, flat(dl_b4), flat(dl_b16))):
        dil.append((_dil_bwd_dq(zb, do, lse, dl, seq, d), *_dil_bwd_dkv(zb, do, lse, dl, seq, d)))
    dfa16 = jnp.zeros((16, t_all), BF16).at[:N_HEADS].set(dfa_t.astype(BF16))
    grad_x, dz, acc0 = _inproj_bwd(dqt, dka, dva, dil[0], dil[1], dil[2], dfa16, pos, wqkv, wf16, freq, perms, dr1, xf,
                                   ada3, seq)
    d_wqkv = _matmul_tn(h1, dz, 512, 512, "dw_in")
    d_wf = _matmul_rows(dfa16, h1, 512, "dw_fgate")[:N_HEADS].T
    d_w_in = jnp.concatenate([d_wqkv[:, :3 * WIDTH], d_wf, d_wqkv[:, 3 * WIDTH:]], axis=1)

    dada = jnp.concatenate([acc0[8:8 + nbat], acc0[:nbat], acc1[24:24 + nbat], acc1[16:16 + nbat], acc1[8:8 + nbat],
                            acc2[8:8 + nbat]], axis=1)

    grads = dict(
        dada=dada, b_ada=jnp.sum(dada, axis=0, keepdims=True), w_in=d_w_in, b_fgate=dbf[:, 0][None, :],
        gn_a=acc_gn[0:1, :WIDTH], gn_b=acc_gn[0:1, WIDTH:], ln1_g=acc1[0:1], ln1_b=acc1[1:2],
        conv_b=jnp.concatenate([acc_ca[3:4], acc_cg[3:4]], axis=1), ln2_g=acc2[0:1], ln2_b=acc2[1:2])
    return acc2[2:3], grad_x.reshape(x.shape), grads, dict(zip(LATE, late_parts))


LATE = ("w_out", "w_up", "conv_w", "w_down")
BIG = ("w_ada", "w_in") + LATE
COLUMN_SHARDED = ("w_ada", "w_in", "w_up", "conv_w")


def _payload(name, a):
    return a if name == "conv_w" else a.astype(BF16)
SMALL = ("b_ada", "b_fgate", "gn_a", "gn_b", "ln1_g", "ln1_b", "conv_b", "ln2_g", "ln2_b")
ADAM_ROWS = dict(w_ada=256, w_in=256, w_out=128, w_up=256, conv_w=3, w_down=176)
SMALL_ROWS = 24


def _full_from_gathered(name, g):
    if name in COLUMN_SHARDED:
        return g.transpose(1, 0, 2).reshape(g.shape[1], N_DEV * g.shape[2])
    return g.reshape(N_DEV * g.shape[1], g.shape[2])


def _dest_major(name, full):
    if name in COLUMN_SHARDED:
        r, cfull = full.shape
        return full.reshape(r, N_DEV, cfull // N_DEV).transpose(1, 0, 2)
    return full.reshape(N_DEV, full.shape[0] // N_DEV, full.shape[1])


def _pack_small(vals, extra=None):
    parts = [vals[n].reshape(-1) for n in SMALL]
    if extra is not None:
        parts.append(extra.reshape(-1))
    flat = jnp.concatenate(parts)
    return jnp.pad(flat, (0, SMALL_ROWS * D_MODEL - flat.shape[0])).reshape(SMALL_ROWS, D_MODEL)


def _unpack_small(packed, like):
    flat = packed.reshape(-1)
    out, off = {}, 0
    for n in SMALL:
        size = like[n].size
        out[n] = flat[off:off + size].reshape(like[n].shape)
        off += size
    return out, flat[off:off + D_MODEL]


def kernel(x, c, positions, w_ada, b_ada, w_in, b_fgate, gn_a, gn_b, w_out, ln1_g, ln1_b, w_up, conv_w, conv_b, w_down, ln2_g, ln2_b, loss_target, m_w_ada, m_b_ada, m_w_in, m_b_fgate, m_gn_a, m_gn_b, m_w_out, m_ln1_g, m_ln1_b, m_w_up, m_conv_w, m_conv_b, m_w_down, m_ln2_g, m_ln2_b, v_w_ada, v_b_ada, v_w_in, v_b_fgate, v_gn_a, v_gn_b, v_w_out, v_ln1_g, v_ln1_b, v_w_up, v_conv_w, v_conv_b, v_w_down, v_ln2_g, v_ln2_b):
    w = dict(w_ada=w_ada[0], b_ada=b_ada, w_in=w_in[0], b_fgate=b_fgate, gn_a=gn_a, gn_b=gn_b, w_out=w_out[0],
             ln1_g=ln1_g, ln1_b=ln1_b, w_up=w_up[0], conv_w=conv_w[0], conv_b=conv_b, w_down=w_down[0], ln2_g=ln2_g,
             ln2_b=ln2_b)
    m = dict(w_ada=m_w_ada[0], b_ada=m_b_ada, w_in=m_w_in[0], b_fgate=m_b_fgate, gn_a=m_gn_a, gn_b=m_gn_b,
             w_out=m_w_out[0], ln1_g=m_ln1_g, ln1_b=m_ln1_b, w_up=m_w_up[0], conv_w=m_conv_w[0], conv_b=m_conv_b,
             w_down=m_w_down[0], ln2_g=m_ln2_g, ln2_b=m_ln2_b)
    v = dict(w_ada=v_w_ada[0], b_ada=v_b_ada, w_in=v_w_in[0], b_fgate=v_b_fgate, gn_a=v_gn_a, gn_b=v_gn_b,
             w_out=v_w_out[0], ln1_g=v_ln1_g, ln1_b=v_ln1_b, w_up=v_w_up[0], conv_w=v_conv_w[0], conv_b=v_conv_b,
             w_down=v_w_down[0], ln2_g=v_ln2_g, ln2_b=v_ln2_b)

    nbat = x.shape[0]
    me = 4 * lax.axis_index("x") + 2 * lax.axis_index("y") + lax.axis_index("c")
    ada_cols = w["w_ada"].shape[1]

    c_all, w_in_all = _exchange([c, _payload("w_in", w["w_in"])], [True, True], "weight_gather")
    c_all = c_all.reshape(N_DEV * nbat, D_MODEL)
    ada_mine = _ada_fwd(c_all, w["w_ada"], lax.dynamic_slice(b_ada, (0, me * ada_cols), (1, ada_cols)))
    (ada_parts,) = _exchange([ada_mine.reshape(N_DEV, nbat, ada_cols)], [False], "ada_exchange")
    ada3 = ada_parts.transpose(1, 0, 2).reshape(nbat, 6, D_MODEL)

    loss_lanes, grad_x, g_local, parts = _local_step(
        x, positions, loss_target, ada3, _full_from_gathered("w_in", w_in_all), b_fgate, gn_a, gn_b, ln1_g, ln1_b,
        conv_b, ln2_g, ln2_b, {n: _payload(n, w[n]) for n in LATE})

    parts["w_in"], dada_all, small_all = _exchange(
        [_payload("w_in", _dest_major("w_in", g_local["w_in"])), g_local["dada"], _pack_small(g_local, loss_lanes)],
        [False, True, True], "grad_exchange")
    dada_cols = lax.dynamic_slice(dada_all.reshape(N_DEV * nbat, 6 * D_MODEL), (0, me * ada_cols),
                                  (N_DEV * nbat, ada_cols))
    parts["w_ada"] = _ada_bwd(c_all, dada_cols)[None]

    grad, delta, new_m, new_v = {}, {}, {}, {}
    for n in BIG:
        grad[n], delta[n], new_m[n], new_v[n] = (
            a[None] for a in _adamw(parts[n], w[n], m[n], v[n], ADAM_ROWS[n], "adamw_" + n))
    packed = _adamw(small_all, _pack_small(w), _pack_small(m), _pack_small(v), SMALL_ROWS, "adamw_small")
    for dst, pk in zip((grad, delta, new_m, new_v), packed):
        vals, lanes = _unpack_small(pk, w)
        dst.update(vals)
        if dst is grad:
            loss = jnp.sum(lanes)

    order = ("w_ada", "b_ada", "w_in", "b_fgate", "gn_a", "gn_b", "w_out", "ln1_g", "ln1_b", "w_up", "conv_w", "conv_b",
             "w_down", "ln2_g", "ln2_b")
    return (loss, grad_x, *[grad[n] for n in order], *[delta[n] for n in order], *[new_m[n] for n in order],
            *[new_v[n] for n in order])
```

```python
import functools

import numpy as np
import jax
import jax.numpy as jnp
from jax import lax
from jax.experimental import pallas as pl
from jax.experimental.pallas import tpu as pltpu

F32, BF16 = jnp.float32, jnp.bfloat16
HIGHEST = lax.Precision.HIGHEST
MESH = pl.DeviceIdType.MESH
ANY = pl.BlockSpec(memory_space=pl.ANY)

D_MODEL = 1024
N_HEADS = 8
HEAD_DIM = 64
WIDTH = 512
D_FF = 2816
N_DEV = 8
ROPE_DIMS = 16
ROPE_THETA = 500000.0
ALPHA = 2.0 ** 0.25
LN_EPS = 1e-5
RMS_EPS = 1e-6
NEG = -1e30
Q_SCALE = 0.125
BLK = 128
LANES = 128
VMEM_LIMIT_BYTES = 56 * 1024 * 1024

ADAM_LR, ADAM_B1, ADAM_B2, ADAM_EPS, ADAM_WD, ADAM_STEP = 0.001, 0.9, 0.999, 1e-08, 0.01, 10


def _params(vmem=VMEM_LIMIT_BYTES):
    return pltpu.CompilerParams(vmem_limit_bytes=vmem)


def _nn(a, b):
    return jnp.dot(a, b, preferred_element_type=F32)


def _nt(a, b):
    return lax.dot_general(a, b, (((1,), (1,)), ((), ())), preferred_element_type=F32)


def _tn(a, b):
    return lax.dot_general(a, b, (((0,), (0,)), ((), ())), preferred_element_type=F32)


def _head_mats():
    r = lax.broadcasted_iota(jnp.int32, (LANES, WIDTH), 0)
    c = lax.broadcasted_iota(jnp.int32, (LANES, WIDTH), 1)
    e = ((c >> 6) == r).astype(BF16)
    r2 = lax.broadcasted_iota(jnp.int32, (WIDTH, LANES), 0)
    c2 = lax.broadcasted_iota(jnp.int32, (WIDTH, LANES), 1)
    et = ((r2 >> 6) == c2).astype(BF16)
    return e, et


def _split3(x):
    hi = x.astype(BF16)
    r = x - hi.astype(F32)
    mid = r.astype(BF16)
    return hi, mid, (r - mid.astype(F32)).astype(BF16)


def _hexp(w, e):
    return sum(_nn(part, e) for part in _split3(w)[:2])


def _hsum(x, et):
    return sum(_nn(part, et) for part in _split3(x)[:2])


def _perm_matrix(rows, d, transpose):
    i = np.arange(rows)
    j = (i % (rows // d)) * d + i // (rows // d)
    p = np.zeros((rows, rows), np.float32)
    p[i, j] = 1.0
    return jnp.asarray(p.T if transpose else p, BF16)


def _permute_f32(p, x):
    return sum(_nn(p, part) for part in _split3(x))


def _store_classes(ref, y, d):
    n = y.shape[0] // d
    for r in range(d):
        ref[r] = y[r * n:(r + 1) * n, :]


def _load_classes(ref, d):
    return jnp.concatenate([ref[r] for r in range(d)], axis=0)


def _rope_tabs(pos_ref, fr_ref, sign):
    ang = pos_ref[...].astype(F32) * fr_ref[...]
    lane = lax.broadcasted_iota(jnp.int32, ang.shape, 1) & (HEAD_DIM - 1)
    m1 = lane < ROPE_DIMS // 2
    m2 = (lane >= ROPE_DIMS // 2) & (lane < ROPE_DIMS)
    cos = jnp.cos(ang)
    sin = jnp.sin(ang) * sign
    return (jnp.where(m1 | m2, cos, 1.0), jnp.where(m1, -sin, 0.0), jnp.where(m2, sin, 0.0))


def _rope(z, tabs):
    c, s1, s2 = tabs
    parts = []
    for p in range(z.shape[1] // LANES):
        zp = z[:, LANES * p:LANES * (p + 1)]
        parts.append(zp * c + pltpu.roll(zp, LANES - 8, 1) * s1 + pltpu.roll(zp, 8, 1) * s2)
    return jnp.concatenate(parts, axis=1)


def _half_masks(rows):
    lane = lax.broadcasted_iota(jnp.int32, (rows, LANES), 1)
    lo = lane < HEAD_DIM
    return lo, jnp.logical_not(lo)


def _layer_norm_bwd(dxh, xh, rstd):
    m1 = jnp.mean(dxh, axis=1, keepdims=True)
    m2 = jnp.mean(dxh * xh, axis=1, keepdims=True)
    return rstd * (dxh - m1 - xh * m2)


def _coords():
    return lax.axis_index("x"), lax.axis_index("y"), lax.axis_index("c")


def _peer(x, y, c, k):
    return (1 - x if k & 4 else x, 1 - y if k & 2 else y, 1 - c if k & 1 else c)


def _comm_sems(n):
    return [pltpu.SemaphoreType.DMA((N_DEV - 1, n)), pltpu.SemaphoreType.DMA((N_DEV - 1, n)),
            pltpu.SemaphoreType.DMA((n,))]


def _comm_copies(ins, outs, to_all, sems):
    send_sems, recv_sems, local_sems = sems
    x, y, c = _coords()
    me = 4 * x + 2 * y + c
    copies = [pltpu.make_async_copy(ins[t] if to_all[t] else ins[t].at[me], outs[t].at[me], local_sems.at[t])
              for t in range(len(ins))]
    for k in range(1, N_DEV):
        px, py, pc = _peer(x, y, c, k)
        dest = 4 * px + 2 * py + pc
        for t in range(len(ins)):
            copies.append(pltpu.make_async_remote_copy(
                src_ref=ins[t] if to_all[t] else ins[t].at[dest], dst_ref=outs[t].at[me],
                send_sem=send_sems.at[k - 1, t], recv_sem=recv_sems.at[k - 1, t],
                device_id=(px, py, pc), device_id_type=MESH))
    return copies


def _comm_out_shapes(ins, to_all):
    return [jax.ShapeDtypeStruct(((N_DEV,) + a.shape) if ta else a.shape, a.dtype) for a, ta in zip(ins, to_all)]


def _exchange(ins, to_all, name):
    n = len(ins)

    def body(*refs):
        copies = _comm_copies(refs[:n], refs[n:2 * n], to_all, refs[2 * n:])
        for cp in copies:
            cp.start()
        for cp in copies:
            cp.wait()

    return pl.pallas_call(
        body, name=name, out_shape=_comm_out_shapes(ins, to_all), in_specs=[ANY] * n, out_specs=[ANY] * n,
        scratch_shapes=_comm_sems(n),
    )(*ins)


def _gather_two_level(ins, name):
    n = len(ins)

    def body(*refs):
        srcs, outs = refs[:n], refs[n:2 * n]
        send_sems, recv_sems, local_sems = refs[2 * n:]
        x, y, c = _coords()
        me = 4 * x + 2 * y + c
        sibling = (x, y, 1 - c)
        chips = [(1 - x, y), (x, 1 - y), (1 - x, 1 - y)]
        slot = lambda px, py, pc: 4 * px + 2 * py + pc

        def copy(k, t, block, to, own=False):
            return pltpu.make_async_remote_copy(
                src_ref=srcs[t] if own else outs[t].at[block], dst_ref=outs[t].at[block],
                send_sem=send_sems.at[k, t], recv_sem=recv_sems.at[k, t], device_id=to, device_id_type=MESH)

        local = [pltpu.make_async_copy(srcs[t], outs[t].at[me], local_sems.at[t]) for t in range(n)]
        first = [copy(0, t, me, sibling, own=True) for t in range(n)]
        first += [copy(1 + j, t, me, (*chip, c), own=True) for j, chip in enumerate(chips) for t in range(n)]
        for cp in local + first:
            cp.start()
        passed = []
        for j, chip in enumerate(chips):
            for t in range(n):
                copy(1 + j, t, slot(*chip, c), (x, y, c)).wait_recv()
                cp = copy(4 + j, t, slot(*chip, c), sibling)
                cp.start()
                passed.append(cp)
        for t in range(n):
            copy(0, t, slot(x, y, 1 - c), (x, y, c)).wait_recv()
            for j, chip in enumerate(chips):
                copy(4 + j, t, slot(*chip, 1 - c), (x, y, c)).wait_recv()
        for cp in first + passed:
            cp.wait_send()
        for cp in local:
            cp.wait()

    return pl.pallas_call(
        body, name=name, out_shape=_comm_out_shapes(ins, [True] * n), in_specs=[ANY] * n, out_specs=[ANY] * n,
        scratch_shapes=_comm_sems(n),
    )(*ins)


def _adamw(parts, w, m, v, rows, name):
    n_parts, r_all, cols = parts.shape
    c1 = 1.0 - ADAM_B1 ** ADAM_STEP
    c2 = 1.0 - ADAM_B2 ** ADAM_STEP

    def body(p_ref, w_ref, m_ref, v_ref, g_ref, d_ref, mo_ref, vo_ref):
        g = p_ref[0].astype(F32)
        for s in range(1, n_parts):
            g = g + p_ref[s].astype(F32)
        mn = ADAM_B1 * m_ref[...] + (1.0 - ADAM_B1) * g
        vn = ADAM_B2 * v_ref[...] + (1.0 - ADAM_B2) * (g * g)
        m_hat = mn / c1
        v_hat = vn / c2
        g_ref[...] = g
        d_ref[...] = -ADAM_LR * (m_hat / (jnp.sqrt(v_hat) + ADAM_EPS) + ADAM_WD * w_ref[...])
        mo_ref[...] = mn
        vo_ref[...] = vn

    spec = pl.BlockSpec((rows, cols), lambda i: (i, 0))
    return pl.pallas_call(
        body, name=name, grid=(r_all // rows,),
        in_specs=[pl.BlockSpec((n_parts, rows, cols), lambda i: (0, i, 0)), spec, spec, spec],
        out_specs=[spec] * 4, out_shape=[jax.ShapeDtypeStruct((r_all, cols), F32)] * 4,
        compiler_params=_params(),
    )(parts, w, m, v)


def _matmul_tn(a, b, chunk, tk, name):
    t_all, k1 = a.shape
    n = b.shape[1]

    def body(a_ref, b_ref, o_ref):
        @pl.when(pl.program_id(0) == 0)
        def _():
            o_ref[...] = jnp.zeros_like(o_ref)
        at = a_ref[...].astype(F32).T.astype(BF16)
        for j in range(0, n, chunk):
            cs = slice(j, min(j + chunk, n))
            o_ref[:, cs] += _nn(at, b_ref[:, cs])

    return pl.pallas_call(
        body, name=name, grid=(t_all // tk,),
        in_specs=[pl.BlockSpec((tk, k1), lambda t: (t, 0)), pl.BlockSpec((tk, n), lambda t: (t, 0))],
        out_specs=pl.BlockSpec((k1, n), lambda t: (0, 0)),
        out_shape=jax.ShapeDtypeStruct((k1, n), F32), compiler_params=_params(),
    )(a, b)


def _matmul_rows(a, b, tk, name):
    r, t_all = a.shape
    n = b.shape[1]

    def body(a_ref, b_ref, o_ref):
        @pl.when(pl.program_id(0) == 0)
        def _():
            o_ref[...] = jnp.zeros_like(o_ref)
        o_ref[...] += _nn(a_ref[...], b_ref[...])

    return pl.pallas_call(
        body, name=name, grid=(t_all // tk,),
        in_specs=[pl.BlockSpec((r, tk), lambda t: (0, t)), pl.BlockSpec((tk, n), lambda t: (t, 0))],
        out_specs=pl.BlockSpec((r, n), lambda t: (0, 0)),
        out_shape=jax.ShapeDtypeStruct((r, n), F32), compiler_params=_params(),
    )(a, b)


def _ada_fwd(c_all, w_ada, b_ada):
    whole = lambda a: pl.BlockSpec(a.shape, lambda j: (0, 0))

    def body(c_ref, w_ref, b_ref, o_ref):
        cv = c_ref[...]
        s = (cv * jax.nn.sigmoid(cv)).astype(BF16)
        o_ref[...] = _nn(s, w_ref[...].astype(BF16)) + b_ref[...]

    out = jax.ShapeDtypeStruct((c_all.shape[0], w_ada.shape[1]), F32)
    return pl.pallas_call(
        body, name="ada_fwd", grid=(1,), in_specs=[whole(c_all), whole(w_ada), whole(b_ada)], out_specs=whole(out),
        out_shape=out, compiler_params=_params(),
    )(c_all, w_ada, b_ada)


def _ada_bwd(c_all, dada):
    whole = lambda a: pl.BlockSpec(a.shape, lambda j: (0, 0))

    def body(c_ref, d_ref, o_ref):
        cv = c_ref[...]
        s = (cv * jax.nn.sigmoid(cv)).astype(BF16)
        o_ref[...] = _tn(s, d_ref[...].astype(BF16))

    out = jax.ShapeDtypeStruct((D_MODEL, dada.shape[1]), F32)
    return pl.pallas_call(
        body, name="ada_bwd", grid=(1,), in_specs=[whole(c_all), whole(dada)], out_specs=whole(out), out_shape=out,
        compiler_params=_params(),
    )(c_all, dada)


TOK_TM = 256
DILATIONS = (1, 4, 16)


def _class_spec(d, width, nts):
    return pl.BlockSpec((d, TOK_TM // d, width), lambda i: (i // nts, i % nts, 0))


def _class_shape(t_all, seq, d, width, dtype):
    return jax.ShapeDtypeStruct((t_all // seq * d, seq // d, width), dtype)


def _inproj(x, ada3, pos, wqkv, wf16, freq, perms, seq):
    t_all = x.shape[0]
    tm = TOK_TM
    nts = seq // tm

    def body(x_ref, ada_ref, pos_ref, w_ref, wf_ref, fr_ref, p4_ref, p16_ref, h1_ref, za_ref, zb_ref, zb4_ref,
             zb16_ref, vt_ref, fa_ref):
        h1 = (x_ref[...] * (1.0 + ada_ref[0, 1:2, :]) + ada_ref[0, 0:1, :]).astype(BF16)
        h1_ref[...] = h1
        tabs = _rope_tabs(pos_ref, fr_ref, 1.0)
        for n in range(6):
            z = _nn(h1, w_ref[:, n * WIDTH:(n + 1) * WIDTH])
            if n in (3, 4):
                z = _rope(z, tabs)
            if n in (0, 3):
                z = z * Q_SCALE
            if n == 2:
                vt_ref[...] = z.T.astype(BF16)
            dst = za_ref if n < 3 else zb_ref
            dst[:, (n % 3) * WIDTH:(n % 3 + 1) * WIDTH] = z.astype(BF16)
        fa_ref[...] = _nt(wf_ref[...], h1)[:N_HEADS]
        zb = zb_ref[...]
        _store_classes(zb4_ref, _nn(p4_ref[...], zb).astype(BF16), 4)
        _store_classes(zb16_ref, _nn(p16_ref[...], zb).astype(BF16), 16)

    tok = lambda w: pl.BlockSpec((tm, w), lambda i: (i, 0))
    whole = lambda a: pl.BlockSpec(a.shape, lambda i: (0, 0))
    return pl.pallas_call(
        body, name="inproj", grid=(t_all // tm,),
        in_specs=[tok(D_MODEL), pl.BlockSpec((1, 6, D_MODEL), lambda i: (i // nts, 0, 0)), tok(1), whole(wqkv),
                  whole(wf16), pl.BlockSpec((1, LANES), lambda i: (0, 0)), whole(perms[0]), whole(perms[1])],
        out_specs=[tok(D_MODEL), tok(3 * WIDTH), tok(3 * WIDTH), _class_spec(4, 3 * WIDTH, nts),
                   _class_spec(16, 3 * WIDTH, nts), pl.BlockSpec((WIDTH, tm), lambda i: (i // nts, i % nts)),
                   pl.BlockSpec((N_HEADS, tm), lambda i: (0, i))],
        out_shape=[jax.ShapeDtypeStruct((t_all, D_MODEL), BF16), jax.ShapeDtypeStruct((t_all, 3 * WIDTH), BF16),
                   jax.ShapeDtypeStruct((t_all, 3 * WIDTH), BF16), _class_shape(t_all, seq, 4, 3 * WIDTH, BF16),
                   _class_shape(t_all, seq, 16, 3 * WIDTH, BF16),
                   jax.ShapeDtypeStruct((t_all // seq * WIDTH, seq), BF16),
                   jax.ShapeDtypeStruct((N_HEADS, t_all), F32)],
        compiler_params=_params(),
    )(x, ada3, pos, wqkv, wf16, freq, perms[0], perms[1])


def _chunk_rows(a_t, seq):
    t_all = a_t.shape[1]
    return a_t.reshape(N_HEADS, t_all // seq, seq // LANES, LANES).transpose(1, 0, 2, 3).reshape(-1, LANES)


def _unchunk_rows(a, seq):
    nbat = a.shape[0] * LANES // (N_HEADS * seq)
    return a.reshape(nbat, N_HEADS, seq // LANES, LANES).transpose(1, 0, 2, 3).reshape(N_HEADS, nbat * seq)


def _chunk_carry(tot, nchunk, later):
    rows = tot.shape[0]
    r = lax.broadcasted_iota(jnp.int32, (rows, rows), 0)
    c = lax.broadcasted_iota(jnp.int32, (rows, rows), 1)
    sel = ((r // nchunk) == (c // nchunk)) & ((c > r) if later else (c < r))
    mat = sel.astype(BF16)
    return sum(_nn(mat, part) for part in _split3(jnp.broadcast_to(tot, (rows, LANES))))


def _fgate_fwd(fa_t, bf, seq):
    x = _chunk_rows(fa_t, seq)
    rows = x.shape[0]
    nchunk = seq // LANES
    bias = jnp.broadcast_to(bf.reshape(1, N_HEADS, 1), (rows // (N_HEADS * nchunk), N_HEADS, nchunk)).reshape(rows, 1)

    def body(x_ref, b_ref, f_ref):
        lane = lax.broadcasted_iota(jnp.int32, (rows, LANES), 1)
        xv = x_ref[...] + b_ref[...]
        lf = jnp.minimum(xv, 0.0) - jnp.log(1.0 + jnp.exp(-jnp.abs(xv)))
        for s in (1, 2, 4, 8, 16, 32, 64):
            lf = lf + jnp.where(lane >= s, pltpu.roll(lf, s, 1), 0.0)
        f_ref[...] = lf + _chunk_carry(lf[:, LANES - 1:LANES], nchunk, False)

    whole = lambda a: pl.BlockSpec(a.shape, lambda i: (0, 0))
    out = pl.pallas_call(
        body, name="fgate_fwd", grid=(1,), in_specs=[whole(x), whole(bias)], out_specs=whole(x),
        out_shape=jax.ShapeDtypeStruct(x.shape, F32), compiler_params=_params(),
    )(x, bias)
    return _unchunk_rows(out, seq)


def _fgate_bwd(df_t, fa_t, bf, seq):
    d_in = _chunk_rows(df_t, seq)
    x = _chunk_rows(fa_t, seq)
    rows = x.shape[0]
    nchunk = seq // LANES
    bias = jnp.broadcast_to(bf.reshape(1, N_HEADS, 1), (rows // (N_HEADS * nchunk), N_HEADS, nchunk)).reshape(rows, 1)

    def body(d_ref, x_ref, b_ref, o_ref, s_ref):
        lane = lax.broadcasted_iota(jnp.int32, (rows, LANES), 1)
        d = d_ref[...]
        for s in (1, 2, 4, 8, 16, 32, 64):
            d = d + jnp.where(lane < LANES - s, pltpu.roll(d, LANES - s, 1), 0.0)
        d = d + _chunk_carry(d[:, 0:1], nchunk, True)
        dfa = d * jax.nn.sigmoid(-(x_ref[...] + b_ref[...]))
        o_ref[...] = dfa
        g = lax.broadcasted_iota(jnp.int32, (2 * N_HEADS, rows), 0)
        r = lax.broadcasted_iota(jnp.int32, (2 * N_HEADS, rows), 1)
        group = (((r // nchunk) % N_HEADS) == g).astype(BF16)
        per_head = sum(_nn(group, part) for part in _split3(dfa))[:N_HEADS]
        s_ref[...] = jnp.broadcast_to(jnp.sum(per_head, axis=1, keepdims=True), (N_HEADS, LANES))

    whole = lambda a: pl.BlockSpec(a.shape, lambda i: (0, 0))
    dfa, sums = pl.pallas_call(
        body, name="fgate_bwd", grid=(1,), in_specs=[whole(d_in), whole(x), whole(bias)],
        out_specs=[whole(x), pl.BlockSpec((N_HEADS, LANES), lambda i: (0, 0))],
        out_shape=[jax.ShapeDtypeStruct(x.shape, F32), jax.ShapeDtypeStruct((N_HEADS, LANES), F32)],
        compiler_params=_params(),
    )(d_in, x, bias)
    return _unchunk_rows(dfa, seq), sums


FOX_T = 256


def _fox_prep(dst, src_ref, lo, hi):
    for p in range(4):
        v = src_ref[:, LANES * p:LANES * (p + 1)]
        dst[2 * p] = jnp.where(lo, v, jnp.zeros_like(v))
        dst[2 * p + 1] = jnp.where(hi, v, jnp.zeros_like(v))


def _fox_fwd(za, vt, f_col, seq, shards):
    t_all = za.shape[0]
    tq = FOX_T
    nq = seq // tq
    nbat = t_all // seq
    n = len(shards)
    to_all = [True] * n

    def body(*refs):
        q_ref, k_ref, vt_ref, fc_ref = refs[:4]
        o_ref, lse_ref = refs[4 + n:6 + n]
        qm_sc, m_sc, l_sc, acc_sc, a_sc, st_sc, pe_sc = refs[6 + 2 * n:13 + 2 * n]
        comm = (refs[4:4 + n], refs[6 + n:6 + 2 * n], to_all, refs[13 + 2 * n:])
        i = pl.program_id(1)

        @pl.when((pl.program_id(0) == 0) & (i == 0))
        def _():
            for cp in _comm_copies(*comm):
                cp.start()
        lo, hi = _half_masks(tq)
        r = lax.broadcasted_iota(jnp.int32, (tq, tq), 0)
        c = lax.broadcasted_iota(jnp.int32, (tq, tq), 1)
        tri = c >= r
        _fox_prep(qm_sc, q_ref, lo, hi)
        m_sc[...] = jnp.full(m_sc.shape, NEG, F32)
        l_sc[...] = jnp.zeros_like(l_sc)
        acc_sc[...] = jnp.zeros_like(acc_sc)

        def block(j, masked):
            sl = pl.ds(pl.multiple_of(j * tq, tq), tq)
            for p in range(4):
                kj = k_ref[sl, LANES * p:LANES * (p + 1)]
                for h in (2 * p, 2 * p + 1):
                    st = _nt(kj, qm_sc[h]) - fc_ref[sl, h:h + 1]
                    st_sc[h] = jnp.where(tri, st, NEG) if masked else st
            for h in range(N_HEADS):
                st = st_sc[h]
                m = m_sc[h:h + 1, :]
                mn = jnp.maximum(m, jnp.max(st, axis=0, keepdims=True))
                a = jnp.exp(m - mn)
                pe = jnp.exp(st - mn)
                m_sc[h:h + 1, :] = mn
                a_sc[h:h + 1, :] = a
                l_sc[h:h + 1, :] = a * l_sc[h:h + 1, :] + jnp.sum(pe, axis=0, keepdims=True)
                pe_sc[h] = pe.astype(BF16)
            for h in range(N_HEADS):
                acc_sc[h] = a_sc[h:h + 1, :] * acc_sc[h] + _nn(vt_ref[HEAD_DIM * h:HEAD_DIM * (h + 1), sl], pe_sc[h])

        def step(j, carry):
            block(j, False)
            return carry

        lax.fori_loop(0, i, step, 0)
        block(i, True)
        lse_ref[...] = m_sc[...] + jnp.log(l_sc[...])
        for p in range(4):
            ot = jnp.concatenate([acc_sc[h] / l_sc[h:h + 1, :] for h in (2 * p, 2 * p + 1)], axis=0)
            o_ref[:, LANES * p:LANES * (p + 1)] = ot.T

        @pl.when((pl.program_id(0) == nbat - 1) & (i == nq - 1))
        def _():
            for cp in _comm_copies(*comm):
                cp.wait()

    res = pl.pallas_call(
        body, name="fox_fwd", grid=(nbat, nq),
        in_specs=[pl.BlockSpec((tq, WIDTH), lambda b, i: (b * nq + i, 0)),
                  pl.BlockSpec((seq, WIDTH), lambda b, i: (b, 1)), pl.BlockSpec((WIDTH, seq), lambda b, i: (b, 0)),
                  pl.BlockSpec((seq, LANES), lambda b, i: (b, 0))] + [ANY] * n,
        out_specs=[pl.BlockSpec((tq, WIDTH), lambda b, i: (b * nq + i, 0)),
                   pl.BlockSpec((N_HEADS, tq), lambda b, i: (0, b * nq + i))] + [ANY] * n,
        out_shape=[jax.ShapeDtypeStruct((t_all, WIDTH), F32), jax.ShapeDtypeStruct((N_HEADS, t_all), F32)]
        + _comm_out_shapes(shards, to_all),
        scratch_shapes=[pltpu.VMEM((N_HEADS, tq, LANES), BF16), pltpu.VMEM((N_HEADS, tq), F32),
                        pltpu.VMEM((N_HEADS, tq), F32), pltpu.VMEM((N_HEADS, HEAD_DIM, tq), F32),
                        pltpu.VMEM((N_HEADS, tq), F32), pltpu.VMEM((N_HEADS, tq, tq), F32),
                        pltpu.VMEM((N_HEADS, tq, tq), BF16)] + _comm_sems(n),
        compiler_params=_params(),
    )(za, za, vt, f_col, *shards)
    return res[0], res[1], res[2:]


def _fox_bwd(za, do, f_col, lse_row, dl_row, seq, grads):
    t_all = za.shape[0]
    tk = FOX_T
    nk = seq // tk
    nbat = t_all // seq
    n = len(grads)
    to_all = [False] * n

    def body(*refs):
        k_ref, v_ref, q_ref, do_ref, fc_ref, lr_ref, dr_ref = refs[:7]
        dk_ref, dv_ref, df_ref, dqt_ref, dfq_ref = refs[7 + n:12 + n]
        km_sc, vm_sc, fk_sc, dk_sc, dv_sc, cs_sc, kt_sc, st_sc, dp_sc, pt_sc, ds_sc = refs[12 + 2 * n:23 + 2 * n]
        comm = (refs[7:7 + n], refs[12 + n:12 + 2 * n], to_all, refs[23 + 2 * n:])
        j = pl.program_id(1)

        @pl.when(j == 0)
        def _():
            dqt_ref[...] = jnp.zeros_like(dqt_ref)
            dfq_ref[...] = jnp.zeros_like(dfq_ref)

        @pl.when((pl.program_id(0) == 0) & (j == 0))
        def _():
            for cp in _comm_copies(*comm):
                cp.start()
        lo, hi = _half_masks(tk)
        r = lax.broadcasted_iota(jnp.int32, (tk, tk), 0)
        c = lax.broadcasted_iota(jnp.int32, (tk, tk), 1)
        tri = c >= r
        _fox_prep(km_sc, k_ref, lo, hi)
        _fox_prep(vm_sc, v_ref, lo, hi)
        for h in range(N_HEADS):
            fk_sc[h] = jnp.broadcast_to(fc_ref[:, h:h + 1], (tk, tk))
        for p in range(4):
            kt_sc[p] = k_ref[:, LANES * p:LANES * (p + 1)].astype(F32).T.astype(BF16)
        dk_sc[...] = jnp.zeros_like(dk_sc)
        dv_sc[...] = jnp.zeros_like(dv_sc)
        cs_sc[...] = jnp.zeros_like(cs_sc)

        def block(i, masked):
            sl = pl.ds(pl.multiple_of(i * tk, tk), tk)
            for p in range(4):
                cs = slice(LANES * p, LANES * (p + 1))
                qi = q_ref[sl, cs]
                doi = do_ref[sl, cs]
                for h in (2 * p, 2 * p + 1):
                    st = _nt(km_sc[h], qi) - fk_sc[h] - lr_ref[h:h + 1, sl]
                    st_sc[h] = jnp.where(tri, st, NEG) if masked else st
                    dp_sc[h] = _nt(vm_sc[h], doi) - dr_ref[h:h + 1, sl]
            for h in range(N_HEADS):
                pt = jnp.exp(st_sc[h])
                dst = pt * dp_sc[h]
                pt_sc[h] = pt.astype(BF16)
                ds_sc[h] = dst.astype(BF16)
                cs_sc[h] += dst[:, :LANES] + dst[:, LANES:]
                dfq_ref[h:h + 1, sl] += jnp.sum(dst, axis=0, keepdims=True)
            for p in range(4):
                cs = slice(LANES * p, LANES * (p + 1))
                qi = q_ref[sl, cs]
                doi = do_ref[sl, cs]
                for h in (2 * p, 2 * p + 1):
                    dv_sc[h] += _nn(pt_sc[h], doi)
                    dk_sc[h] += _nn(ds_sc[h], qi)
                    kt = kt_sc[p, HEAD_DIM * (h % 2):HEAD_DIM * (h % 2 + 1), :]
                    dqt_ref[HEAD_DIM * h:HEAD_DIM * (h + 1), sl] += _nn(kt, ds_sc[h])

        def step(i, carry):
            block(i, False)
            return carry

        block(j, True)
        lax.fori_loop(j + 1, nk, step, 0)
        df_ref[...] = jnp.zeros_like(df_ref)
        for p in range(4):
            cs = slice(LANES * p, LANES * (p + 1))
            dk_ref[:, cs] = jnp.where(lo, dk_sc[2 * p], dk_sc[2 * p + 1]).astype(BF16)
            dv_ref[:, cs] = jnp.where(lo, dv_sc[2 * p], dv_sc[2 * p + 1]).astype(BF16)
            for h in (2 * p, 2 * p + 1):
                df_ref[:, h:h + 1] = -jnp.sum(cs_sc[h], axis=1, keepdims=True)

        @pl.when(j == nk - 1)
        def _():
            dqt_ref[...] = dqt_ref[...] * Q_SCALE

        @pl.when((pl.program_id(0) == nbat - 1) & (j == nk - 1))
        def _():
            for cp in _comm_copies(*comm):
                cp.wait()

    tile = lambda w, col: pl.BlockSpec((tk, w), lambda b, j: (b * nk + j, col))
    full = lambda col: pl.BlockSpec((seq, WIDTH), lambda b, j: (b, col))
    row = pl.BlockSpec((N_HEADS, seq), lambda b, j: (0, b))
    acc = pltpu.VMEM((N_HEADS, tk, LANES), F32)
    res = pl.pallas_call(
        body, name="fox_bwd", grid=(nbat, nk),
        in_specs=[tile(WIDTH, 1), tile(WIDTH, 2), full(0), full(0), tile(LANES, 0), row, row] + [ANY] * n,
        out_specs=[tile(WIDTH, 0), tile(WIDTH, 0), tile(LANES, 0), pl.BlockSpec((WIDTH, seq), lambda b, j: (b, 0)),
                   row] + [ANY] * n,
        out_shape=[jax.ShapeDtypeStruct((t_all, WIDTH), BF16), jax.ShapeDtypeStruct((t_all, WIDTH), BF16),
                   jax.ShapeDtypeStruct((t_all, LANES), F32), jax.ShapeDtypeStruct((nbat * WIDTH, seq), F32),
                   jax.ShapeDtypeStruct((N_HEADS, t_all), F32)] + _comm_out_shapes(grads, to_all),
        scratch_shapes=[pltpu.VMEM((N_HEADS, tk, LANES), BF16), pltpu.VMEM((N_HEADS, tk, LANES), BF16),
                        pltpu.VMEM((N_HEADS, tk, tk), F32), acc, acc, acc, pltpu.VMEM((4, LANES, tk), BF16),
                        pltpu.VMEM((N_HEADS, tk, tk), F32), pltpu.VMEM((N_HEADS, tk, tk), F32),
                        pltpu.VMEM((N_HEADS, tk, tk), BF16), pltpu.VMEM((N_HEADS, tk, tk), BF16)]
        + _comm_sems(n),
        compiler_params=_params(),
    )(za, za, za, do, f_col, lse_row, dl_row, *grads)
    return res[0], res[1], res[2], res[3], res[4], res[5:]


DIL_SUB = 4


def _dil_mask(has_prev):
    qi = lax.broadcasted_iota(jnp.int32, (BLK, 2 * BLK), 0)
    kj = lax.broadcasted_iota(jnp.int32, (BLK, 2 * BLK), 1)
    dist = qi + BLK - kj
    band = (dist >= 0) & (dist <= BLK)
    return band if has_prev is True else band & ((kj >= BLK) | has_prev)


def _dil_geometry(t_all, seq, d):
    length = seq // d
    nbs = length // BLK
    sub = min(DIL_SUB, nbs)
    spb = nbs // sub
    tile = lambda width, col: pl.BlockSpec((BLK * sub, width), lambda s: (s, col))
    whole = lambda width, col: pl.BlockSpec((length, width), lambda s: (s // spb, col))
    return nbs, sub, spb, t_all // (BLK * sub), tile, whole


def _blk(i):
    return pl.ds(pl.multiple_of(i * BLK, BLK), BLK)


def _dil_fwd(zb, seq, d):
    t_all = zb.shape[0]
    nbs, sub, spb, steps, tile, whole = _dil_geometry(t_all, seq, d)

    def body(q_ref, k_ref, v_ref, o_ref, lse_ref, s_sc, p_sc):
        first = (pl.program_id(0) % spb) * sub
        lo, hi = _half_masks(BLK)
        lse_ref[...] = jnp.zeros_like(lse_ref)
        for j in range(sub):
            blk = first + j
            mask = _dil_mask(blk != 0 if j == 0 else True)
            for p in range(4):
                cs = slice(LANES * p, LANES * (p + 1))
                qp = q_ref[BLK * j:BLK * (j + 1), cs]
                kcat = jnp.concatenate([k_ref[_blk(jnp.maximum(blk - 1, 0)), cs], k_ref[_blk(blk), cs]], axis=0)
                for e in (0, 1):
                    qe = jnp.where(lo if e == 0 else hi, qp, jnp.zeros_like(qp))
                    s_sc[N_HEADS * j + 2 * p + e] = jnp.where(mask, _nt(qe, kcat), NEG)
        inv = []
        for i in range(N_HEADS * sub):
            s = s_sc[i]
            m = jnp.max(s, axis=1, keepdims=True)
            pe = jnp.exp(s - m)
            l = jnp.sum(pe, axis=1, keepdims=True)
            p_sc[i] = pe.astype(BF16)
            inv.append(1.0 / l)
            j, h = divmod(i, N_HEADS)
            lse_ref[BLK * j:BLK * (j + 1), h:h + 1] = m + jnp.log(l)
        for j in range(sub):
            blk = first + j
            for p in range(4):
                cs = slice(LANES * p, LANES * (p + 1))
                vcat = jnp.concatenate([v_ref[_blk(jnp.maximum(blk - 1, 0)), cs], v_ref[_blk(blk), cs]], axis=0)
                res = [_nn(p_sc[N_HEADS * j + h], vcat) * inv[N_HEADS * j + h] for h in (2 * p, 2 * p + 1)]
                o_ref[BLK * j:BLK * (j + 1), cs] = jnp.where(lo, res[0], res[1])

    return pl.pallas_call(
        body, name=f"dil_fwd_{d}", grid=(steps,), in_specs=[tile(WIDTH, 0), whole(WIDTH, 1), whole(WIDTH, 2)],
        out_specs=[tile(WIDTH, 0), tile(LANES, 0)],
        out_shape=[jax.ShapeDtypeStruct((t_all, WIDTH), F32), jax.ShapeDtypeStruct((t_all, LANES), F32)],
        scratch_shapes=[pltpu.VMEM((N_HEADS * sub, BLK, 2 * BLK), F32),
                        pltpu.VMEM((N_HEADS * sub, BLK, 2 * BLK), BF16)],
        compiler_params=_params(),
    )(zb, zb, zb)


def _dil_bwd_dq(zb, do, lse, dl, seq, d):
    t_all = zb.shape[0]
    nbs, sub, spb, steps, tile, whole = _dil_geometry(t_all, seq, d)

    def body(q_ref, k_ref, v_ref, do_ref, lse_ref, dl_ref, dq_ref, s_sc, dp_sc, ds_sc):
        first = (pl.program_id(0) % spb) * sub
        lo, hi = _half_masks(BLK)
        for j in range(sub):
            blk = first + j
            rows = slice(BLK * j, BLK * (j + 1))
            mask = _dil_mask(blk != 0 if j == 0 else True)
            prev = _blk(jnp.maximum(blk - 1, 0))
            for p in range(4):
                cs = slice(LANES * p, LANES * (p + 1))
                qp = q_ref[rows, cs]
                dop = do_ref[rows, cs]
                kcat = jnp.concatenate([k_ref[prev, cs], k_ref[_blk(blk), cs]], axis=0)
                vcat = jnp.concatenate([v_ref[prev, cs], v_ref[_blk(blk), cs]], axis=0)
                for e in (0, 1):
                    h = 2 * p + e
                    sel = lo if e == 0 else hi
                    qe = jnp.where(sel, qp, jnp.zeros_like(qp))
                    doe = jnp.where(sel, dop, jnp.zeros_like(dop))
                    s_sc[N_HEADS * j + h] = jnp.where(mask, _nt(qe, kcat) - lse_ref[rows, h:h + 1], NEG)
                    dp_sc[N_HEADS * j + h] = _nt(doe, vcat) - dl_ref[rows, h:h + 1]
        for i in range(N_HEADS * sub):
            ds_sc[i] = (jnp.exp(s_sc[i]) * dp_sc[i]).astype(BF16)
        for j in range(sub):
            blk = first + j
            for p in range(4):
                cs = slice(LANES * p, LANES * (p + 1))
                kcat = jnp.concatenate([k_ref[_blk(jnp.maximum(blk - 1, 0)), cs], k_ref[_blk(blk), cs]], axis=0)
                i = N_HEADS * j + 2 * p
                dq_ref[BLK * j:BLK * (j + 1), cs] = (
                    jnp.where(lo, _nn(ds_sc[i], kcat), _nn(ds_sc[i + 1], kcat)) * Q_SCALE).astype(BF16)

    wide = pltpu.VMEM((N_HEADS * sub, BLK, 2 * BLK), F32)
    return pl.pallas_call(
        body, name=f"dil_bwd_dq_{d}", grid=(steps,),
        in_specs=[tile(WIDTH, 0), whole(WIDTH, 1), whole(WIDTH, 2), tile(WIDTH, 0), tile(LANES, 0), tile(LANES, 0)],
        out_specs=tile(WIDTH, 0), out_shape=jax.ShapeDtypeStruct((t_all, WIDTH), BF16),
        scratch_shapes=[wide, wide, pltpu.VMEM((N_HEADS * sub, BLK, 2 * BLK), BF16)], compiler_params=_params(),
    )(zb, zb, zb, do, lse, dl)


def _dil_bwd_dkv(zb, do, lse, dl, seq, d):
    t_all = zb.shape[0]
    nbs, sub, spb, steps, tile, whole = _dil_geometry(t_all, seq, d)

    def body(k_ref, v_ref, q_ref, do_ref, lse_ref, dl_ref, dk_ref, dv_ref, s_sc, dp_sc, pt_sc, ds_sc):
        first = (pl.program_id(0) % spb) * sub
        r = lax.broadcasted_iota(jnp.int32, (BLK, 2 * BLK), 0)
        c = lax.broadcasted_iota(jnp.int32, (BLK, 2 * BLK), 1)
        same = (c < BLK) & (c >= r)
        later = (c >= BLK) & (c - BLK <= r)
        lo, hi = _half_masks(BLK)
        for j in range(sub):
            blk = first + j
            rows = slice(BLK * j, BLK * (j + 1))
            nxt = _blk(jnp.minimum(blk + 1, nbs - 1))
            mask = same | (later & (blk + 1 != nbs)) if j == sub - 1 else same | later
            lrows = jnp.concatenate([lse_ref[_blk(blk), :].T, lse_ref[nxt, :].T], axis=1)
            erows = jnp.concatenate([dl_ref[_blk(blk), :].T, dl_ref[nxt, :].T], axis=1)
            for p in range(4):
                cs = slice(LANES * p, LANES * (p + 1))
                kp = k_ref[rows, cs]
                vp = v_ref[rows, cs]
                qcat = jnp.concatenate([q_ref[_blk(blk), cs], q_ref[nxt, cs]], axis=0)
                dcat = jnp.concatenate([do_ref[_blk(blk), cs], do_ref[nxt, cs]], axis=0)
                for e in (0, 1):
                    h = 2 * p + e
                    sel = lo if e == 0 else hi
                    ke = jnp.where(sel, kp, jnp.zeros_like(kp))
                    ve = jnp.where(sel, vp, jnp.zeros_like(vp))
                    s_sc[N_HEADS * j + h] = jnp.where(mask, _nt(ke, qcat) - lrows[h:h + 1, :], NEG)
                    dp_sc[N_HEADS * j + h] = _nt(ve, dcat) - erows[h:h + 1, :]
        for i in range(N_HEADS * sub):
            pt = jnp.exp(s_sc[i])
            pt_sc[i] = pt.astype(BF16)
            ds_sc[i] = (pt * dp_sc[i]).astype(BF16)
        for j in range(sub):
            blk = first + j
            rows = slice(BLK * j, BLK * (j + 1))
            nxt = _blk(jnp.minimum(blk + 1, nbs - 1))
            for p in range(4):
                cs = slice(LANES * p, LANES * (p + 1))
                qcat = jnp.concatenate([q_ref[_blk(blk), cs], q_ref[nxt, cs]], axis=0)
                dcat = jnp.concatenate([do_ref[_blk(blk), cs], do_ref[nxt, cs]], axis=0)
                i = N_HEADS * j + 2 * p
                dk_ref[rows, cs] = jnp.where(lo, _nn(ds_sc[i], qcat), _nn(ds_sc[i + 1], qcat)).astype(BF16)
                dv_ref[rows, cs] = jnp.where(lo, _nn(pt_sc[i], dcat), _nn(pt_sc[i + 1], dcat)).astype(BF16)

    wide = pltpu.VMEM((N_HEADS * sub, BLK, 2 * BLK), F32)
    half = pltpu.VMEM((N_HEADS * sub, BLK, 2 * BLK), BF16)
    return pl.pallas_call(
        body, name=f"dil_bwd_dkv_{d}", grid=(steps,),
        in_specs=[tile(WIDTH, 1), tile(WIDTH, 2), whole(WIDTH, 0), whole(WIDTH, 0), whole(LANES, 0), whole(LANES, 0)],
        out_specs=[tile(WIDTH, 0), tile(WIDTH, 0)], out_shape=[jax.ShapeDtypeStruct((t_all, WIDTH), BF16)] * 2,
        scratch_shapes=[wide, wide, half, half], compiler_params=_params(),
    )(zb, zb, zb, do, lse, dl)


def _mix_out(oa, o3, l3, gn_a, gn_b, w_out, x, ada3, ln_g, ln_b, perms, seq):
    t_all = x.shape[0]
    tm = TOK_TM
    nts = seq // tm

    def body(oa_ref, o1_ref, o2_ref, o3_ref, l1_ref, l2_ref, l3_ref, ga_ref, gb_ref, w_ref, x_ref, ada_ref, g_ref,
             b_ref, p4_ref, p16_ref, pt4_ref, pt16_ref, ob_ref, lse_ref, lse4_ref, lse16_ref, mg_ref, mix_ref, xh_ref,
             rs_ref, h2_ref, h2t_ref):
        e, et = _head_mats()
        la = l1_ref[...]
        lb = _permute_f32(pt4_ref[...], _load_classes(l2_ref, 4))
        lc = _permute_f32(pt16_ref[...], _load_classes(l3_ref, 16))
        mx = jnp.maximum(jnp.maximum(la, lb), lc)
        ea, eb, ec = jnp.exp(la - mx), jnp.exp(lb - mx), jnp.exp(lc - mx)
        tot = ea + eb + ec
        lse = mx + jnp.log(tot)
        lse_ref[...] = lse
        _store_classes(lse4_ref, _permute_f32(p4_ref[...], lse), 4)
        _store_classes(lse16_ref, _permute_f32(p16_ref[...], lse), 16)
        ob = (o1_ref[...] * _hexp(ea / tot, e)
              + _permute_f32(pt4_ref[...], _load_classes(o2_ref, 4)) * _hexp(eb / tot, e)
              + _permute_f32(pt16_ref[...], _load_classes(o3_ref, 16)) * _hexp(ec / tot, e))
        ob_ref[...] = ob

        def rms(o, gain):
            rr = lax.rsqrt(_hsum(o * o, et) * (1.0 / HEAD_DIM) + RMS_EPS)
            return o * _hexp(rr, e) * gain

        merged = jnp.concatenate([rms(oa_ref[...], ga_ref[...]), rms(ob, gb_ref[...])], axis=1).astype(BF16)
        mg_ref[...] = merged
        mix = _nn(merged, w_ref[...])
        mix_ref[...] = mix.astype(BF16)
        r1 = ALPHA * x_ref[...] + ada_ref[0, 2:3, :] * mix
        d = r1 - jnp.mean(r1, axis=1, keepdims=True)
        rstd = lax.rsqrt(jnp.mean(d * d, axis=1, keepdims=True) + LN_EPS)
        xh = d * rstd
        xh_ref[...] = xh
        rs_ref[...] = jnp.broadcast_to(rstd, (tm, LANES))
        x1 = xh * g_ref[...] + b_ref[...]
        h2 = x1 * (1.0 + ada_ref[0, 4:5, :]) + ada_ref[0, 3:4, :]
        h2_ref[...] = h2.astype(BF16)
        h2t_ref[...] = h2.T.astype(BF16)

    tok = lambda w: pl.BlockSpec((tm, w), lambda i: (i, 0))
    vec = lambda w: pl.BlockSpec((1, w), lambda i: (0, 0))
    whole = lambda a: pl.BlockSpec(a.shape, lambda i: (0, 0))
    classes = lambda a, d: a.reshape(t_all // seq * d, seq // d, a.shape[-1])
    return pl.pallas_call(
        body, name="mix_out", grid=(t_all // tm,),
        in_specs=[tok(WIDTH), tok(WIDTH), _class_spec(4, WIDTH, nts), _class_spec(16, WIDTH, nts), tok(LANES),
                  _class_spec(4, LANES, nts), _class_spec(16, LANES, nts), vec(WIDTH), vec(WIDTH), whole(w_out),
                  tok(D_MODEL), pl.BlockSpec((1, 6, D_MODEL), lambda i: (i // nts, 0, 0)), vec(D_MODEL), vec(D_MODEL)]
        + [whole(p) for p in perms],
        out_specs=[tok(WIDTH), tok(LANES), _class_spec(4, LANES, nts), _class_spec(16, LANES, nts), tok(D_MODEL),
                   tok(D_MODEL), tok(D_MODEL), tok(LANES), tok(D_MODEL), pl.BlockSpec((D_MODEL, tm), lambda i: (0, i))],
        out_shape=[jax.ShapeDtypeStruct((t_all, WIDTH), F32), jax.ShapeDtypeStruct((t_all, LANES), F32),
                   _class_shape(t_all, seq, 4, LANES, F32), _class_shape(t_all, seq, 16, LANES, F32),
                   jax.ShapeDtypeStruct((t_all, D_MODEL), BF16), jax.ShapeDtypeStruct((t_all, D_MODEL), BF16),
                   jax.ShapeDtypeStruct((t_all, D_MODEL), F32), jax.ShapeDtypeStruct((t_all, LANES), F32),
                   jax.ShapeDtypeStruct((t_all, D_MODEL), BF16), jax.ShapeDtypeStruct((D_MODEL, t_all), BF16)],
        compiler_params=_params(),
    )(oa, o3[0], classes(o3[1], 4), classes(o3[2], 16), l3[0], classes(l3[1], 4), classes(l3[2], 16), gn_a, gn_b,
      w_out, x, ada3, ln_g, ln_b, *perms)


def _mix_out_bwd(dmix, w_out, oa, ob, gn_a, gn_b, perms, seq):
    t_all = dmix.shape[0]
    tm = TOK_TM
    nts = seq // tm

    def body(dm_ref, w_ref, oa_ref, ob_ref, ga_ref, gb_ref, p4_ref, p16_ref, doa_ref, dob_ref, dob4_ref, dob16_ref,
             dla_ref, dlb_ref, dlb4_ref, dlb16_ref, acc_ref):
        @pl.when(pl.program_id(0) == 0)
        def _():
            acc_ref[...] = jnp.zeros_like(acc_ref)
        e, et = _head_mats()
        dmg = _nt(dm_ref[...], w_ref[...])

        def group(o, dn, gain):
            rr = lax.rsqrt(_hsum(o * o, et) * (1.0 / HEAD_DIM) + RMS_EPS)
            re = _hexp(rr, e)
            dgain = jnp.sum(dn * o * re, axis=0, keepdims=True)
            dxn = dn * gain
            tt = _hsum(dxn * o, et) * (rr * rr * rr) * (1.0 / HEAD_DIM)
            do = re * dxn - o * _hexp(tt, e)
            return do, _hsum(do * o, et), dgain

        doa, dla, dga = group(oa_ref[...], dmg[:, :WIDTH], ga_ref[...])
        dob, dlb, dgb = group(ob_ref[...], dmg[:, WIDTH:], gb_ref[...])
        dob = dob.astype(BF16)
        doa_ref[...] = doa.astype(BF16)
        dob_ref[...] = dob
        _store_classes(dob4_ref, _nn(p4_ref[...], dob).astype(BF16), 4)
        _store_classes(dob16_ref, _nn(p16_ref[...], dob).astype(BF16), 16)
        dla_ref[...] = dla
        dlb_ref[...] = dlb
        _store_classes(dlb4_ref, _permute_f32(p4_ref[...], dlb), 4)
        _store_classes(dlb16_ref, _permute_f32(p16_ref[...], dlb), 16)
        acc_ref[0:1, :] += jnp.concatenate([dga, dgb], axis=1)

    tok = lambda w: pl.BlockSpec((tm, w), lambda i: (i, 0))
    vec = lambda w: pl.BlockSpec((1, w), lambda i: (0, 0))
    return pl.pallas_call(
        body, name="mix_out_bwd", grid=(t_all // tm,),
        in_specs=[tok(D_MODEL), pl.BlockSpec(w_out.shape, lambda i: (0, 0)), tok(WIDTH), tok(WIDTH), vec(WIDTH),
                  vec(WIDTH), pl.BlockSpec(perms[0].shape, lambda i: (0, 0)),
                  pl.BlockSpec(perms[1].shape, lambda i: (0, 0))],
        out_specs=[tok(WIDTH), tok(WIDTH), _class_spec(4, WIDTH, nts), _class_spec(16, WIDTH, nts), tok(LANES),
                   tok(LANES), _class_spec(4, LANES, nts), _class_spec(16, LANES, nts),
                   pl.BlockSpec((8, D_MODEL), lambda i: (0, 0))],
        out_shape=[jax.ShapeDtypeStruct((t_all, WIDTH), BF16), jax.ShapeDtypeStruct((t_all, WIDTH), BF16),
                   _class_shape(t_all, seq, 4, WIDTH, BF16), _class_shape(t_all, seq, 16, WIDTH, BF16),
                   jax.ShapeDtypeStruct((t_all, LANES), F32), jax.ShapeDtypeStruct((t_all, LANES), F32),
                   _class_shape(t_all, seq, 4, LANES, F32), _class_shape(t_all, seq, 16, LANES, F32),
                   jax.ShapeDtypeStruct((8, D_MODEL), F32)],
        compiler_params=_params(),
    )(dmix, w_out, oa, ob, gn_a, gn_b, perms[0], perms[1])


def _inproj_bwd(dqt, dka, dva, dil1, dil4, dil16, dfa16, pos, wqkv, wf16, freq, perms, dr1, x, ada3, seq):
    t_all = x.shape[0]
    tm = TOK_TM
    nts = seq // tm

    def body(dqt_ref, dka_ref, dva_ref, q1_ref, k1_ref, v1_ref, q4_ref, k4_ref, v4_ref, q16_ref, k16_ref, v16_ref,
             dfa_ref, pos_ref, w_ref, wf_ref, fr_ref, pt4_ref, pt16_ref, dr1_ref, x_ref, ada_ref, gx_ref, dz_ref,
             acc_ref):
        i = pl.program_id(0)

        @pl.when(i == 0)
        def _():
            acc_ref[...] = jnp.zeros_like(acc_ref)
        tabs = _rope_tabs(pos_ref, fr_ref, -1.0)
        dz_ref[:, :WIDTH] = dqt_ref[...].T.astype(BF16)
        dz_ref[:, WIDTH:2 * WIDTH] = dka_ref[...]
        dz_ref[:, 2 * WIDTH:3 * WIDTH] = dva_ref[...]
        for t, (n1, n4, n16) in enumerate(((q1_ref, q4_ref, q16_ref), (k1_ref, k4_ref, k16_ref),
                                           (v1_ref, v4_ref, v16_ref))):
            tot = (n1[...].astype(F32) + _nn(pt4_ref[...], _load_classes(n4, 4))
                   + _nn(pt16_ref[...], _load_classes(n16, 16)))
            if t < 2:
                tot = _rope(tot, tabs)
            dz_ref[:, (3 + t) * WIDTH:(4 + t) * WIDTH] = tot.astype(BF16)
        dh1 = _tn(dfa_ref[...], wf_ref[...])
        for n in range(6):
            cs = slice(n * WIDTH, (n + 1) * WIDTH)
            dh1 = dh1 + _nt(dz_ref[:, cs], w_ref[:, cs])
        xv = x_ref[...]
        gx_ref[...] = ALPHA * dr1_ref[...] + dh1 * (1.0 + ada_ref[0, 1:2, :])
        b = i // nts
        acc_ref[pl.ds(b, 1), :] += jnp.sum(dh1 * xv, axis=0, keepdims=True)
        acc_ref[pl.ds(8 + b, 1), :] += jnp.sum(dh1, axis=0, keepdims=True)

    tok = lambda w: pl.BlockSpec((tm, w), lambda i: (i, 0))
    whole = lambda a: pl.BlockSpec(a.shape, lambda i: (0, 0))
    classes = lambda a, d: a.reshape(t_all // seq * d, seq // d, a.shape[-1])
    return pl.pallas_call(
        body, name="inproj_bwd", grid=(t_all // tm,),
        in_specs=[pl.BlockSpec((WIDTH, tm), lambda i: (i // nts, i % nts)), tok(WIDTH), tok(WIDTH)]
        + [tok(WIDTH)] * 3 + [_class_spec(4, WIDTH, nts)] * 3 + [_class_spec(16, WIDTH, nts)] * 3
        + [pl.BlockSpec((16, tm), lambda i: (0, i)), tok(1), whole(wqkv), whole(wf16),
           pl.BlockSpec((1, LANES), lambda i: (0, 0)), whole(perms[2]), whole(perms[3]), tok(D_MODEL), tok(D_MODEL),
           pl.BlockSpec((1, 6, D_MODEL), lambda i: (i // nts, 0, 0))],
        out_specs=[tok(D_MODEL), tok(6 * WIDTH), pl.BlockSpec((16, D_MODEL), lambda i: (0, 0))],
        out_shape=[jax.ShapeDtypeStruct((t_all, D_MODEL), F32), jax.ShapeDtypeStruct((t_all, 6 * WIDTH), BF16),
                   jax.ShapeDtypeStruct((16, D_MODEL), F32)],
        compiler_params=_params(),
    )(dqt, dka, dva, *dil1, *[classes(a, 4) for a in dil4], *[classes(a, 16) for a in dil16], dfa16, pos, wqkv, wf16,
      freq, perms[2], perms[3], dr1, x, ada3)


FFN_TM = 512
FFN_TN = 256
HALO = 8


FFN_CHUNK = 64


def _conv(cat_ref, w_ref, b_ref, start, rows, halo=HALO):
    return (b_ref[...] + w_ref[0:1, :] * cat_ref[pl.ds(start + halo - 2, rows), :]
            + w_ref[1:2, :] * cat_ref[pl.ds(start + halo - 1, rows), :]
            + w_ref[2:3, :] * cat_ref[pl.ds(start + halo, rows), :])


def _ffn_up_gate(h2, w_up, conv_w, conv_b, seq):
    t_all = h2.shape[0]
    tm, tn = FFN_TM, FFN_TN
    nc = D_FF // tn
    nts = seq // tm
    pre = 16

    def body(h_ref, hp_ref, wua_ref, wug_ref, wa_ref, wg_ref, ba_ref, bg_ref, ua_ref, ug_ref, o_ref, ca_ref, cg_ref):
        first = (pl.program_id(1) % nts) == 0
        hcat = jnp.concatenate([hp_ref[...], h_ref[...]], axis=0)
        zero = jnp.zeros((pre, tn), F32)
        for w_ref, cat, u_ref in ((wua_ref, ca_ref, ua_ref), (wug_ref, cg_ref, ug_ref)):
            ub = _nn(hcat, w_ref[...]).astype(BF16)
            ue = ub.astype(F32)
            cat[0:pre, :] = jnp.where(first, zero, ue[0:pre])
            cat[pre:, :] = ue[pre:]
            u_ref[...] = ub[pre:]
        for c0 in range(0, tm, FFN_CHUNK):
            ya = _conv(ca_ref, wa_ref, ba_ref, c0, FFN_CHUNK, pre)
            yg = _conv(cg_ref, wg_ref, bg_ref, c0, FFN_CHUNK, pre)
            o_ref[c0:c0 + FFN_CHUNK, :] = (yg * jax.nn.sigmoid(yg) * ya).astype(BF16)

    vec = lambda r, off: pl.BlockSpec((r, tn), lambda n, t: (0, n + off))
    wcol = lambda off: pl.BlockSpec((D_MODEL, tn), lambda n, t: (0, n + off))
    tile = pl.BlockSpec((tm, tn), lambda n, t: (t, n))
    return pl.pallas_call(
        body, name="ffn_up_gate", grid=(nc, t_all // tm),
        in_specs=[pl.BlockSpec((tm, D_MODEL), lambda n, t: (t, 0)),
                  pl.BlockSpec((pre, D_MODEL), lambda n, t: (jnp.maximum(t * (tm // pre) - 1, 0), 0)),
                  wcol(0), wcol(nc), vec(3, 0), vec(3, nc), vec(1, 0), vec(1, nc)],
        out_specs=[tile, tile, tile],
        out_shape=[jax.ShapeDtypeStruct((t_all, D_FF), BF16)] * 3,
        scratch_shapes=[pltpu.VMEM((tm + pre, tn), F32)] * 2, compiler_params=_params(),
    )(h2, h2, w_up, w_up, conv_w, conv_w, conv_b, conv_b)


def _ffn_gate_bwd(u_a, u_g, dfi, conv_w, conv_b, h2t, seq):
    t_all = u_a.shape[0]
    tm, tn = FFN_TM, FFN_TN
    nc = D_FF // tn
    nts = seq // tm

    def body(ua_ref, uap_ref, uan_ref, ug_ref, ugp_ref, ugn_ref, df_ref, dfn_ref, wa_ref, wg_ref, ba_ref, bg_ref, h_ref,
             dua_ref, dug_ref, acca_ref, accg_ref, dwa_ref, dwg_ref, ca_ref, cg_ref, ya_ref, yg_ref):
        t = pl.program_id(1)
        first = (t % nts) == 0
        last = (t % nts) == nts - 1

        @pl.when(t == 0)
        def _():
            acca_ref[...] = jnp.zeros_like(acca_ref)
            accg_ref[...] = jnp.zeros_like(accg_ref)
            dwa_ref[...] = jnp.zeros_like(dwa_ref)
            dwg_ref[...] = jnp.zeros_like(dwg_ref)
        zero = jnp.zeros((HALO, tn), F32)
        for cat, cur, prv, nxt in ((ca_ref, ua_ref, uap_ref, uan_ref), (cg_ref, ug_ref, ugp_ref, ugn_ref)):
            cat[0:HALO, :] = jnp.where(first, zero, prv[...].astype(F32)[HALO:])
            cat[HALO:HALO + tm, :] = cur[...].astype(F32)
            cat[HALO + tm:, :] = nxt[...].astype(F32)[:HALO]
        ch = FFN_CHUNK
        sums = [[jnp.zeros((1, tn), F32) for _ in range(4)] for _ in range(2)]
        for ci, c0 in enumerate(range(0, tm, ch)):
            ya = _conv(ca_ref, wa_ref, ba_ref, c0, ch + HALO)
            yg = _conv(cg_ref, wg_ref, bg_ref, c0, ch + HALO)
            if c0 + ch < tm:
                beyond = df_ref[c0 + ch:c0 + ch + 16, :].astype(F32)[:HALO]
            else:
                beyond = jnp.where(last, 0.0, dfn_ref[...].astype(F32)[:HALO])
            dfe = jnp.concatenate([df_ref[c0:c0 + ch, :].astype(F32), beyond], axis=0)
            sg = jax.nn.sigmoid(yg)
            ya_ref[ci] = dfe * (yg * sg)
            yg_ref[ci] = dfe * ya * (sg * (1.0 + yg * (1.0 - sg)))
            for half, (dy, cat, w_ref, du_ref) in enumerate(((ya_ref, ca_ref, wa_ref, dua_ref),
                                                             (yg_ref, cg_ref, wg_ref, dug_ref))):
                d0 = dy[ci, 0:ch, :]
                du = (w_ref[2:3, :] * d0 + w_ref[1:2, :] * dy[ci, pl.ds(1, ch), :]
                      + w_ref[0:1, :] * dy[ci, pl.ds(2, ch), :])
                du_ref[c0:c0 + ch, :] = du.astype(BF16)
                for k in range(3):
                    sums[half][k] += jnp.sum(d0 * cat[pl.ds(c0 + HALO - 2 + k, ch), :], axis=0, keepdims=True)
                sums[half][3] += jnp.sum(d0, axis=0, keepdims=True)
        for half, acc in enumerate((acca_ref, accg_ref)):
            for k in range(4):
                acc[k:k + 1, :] += sums[half][k]
        ht = h_ref[...]
        dwa_ref[...] += _nn(ht, dua_ref[...])
        dwg_ref[...] += _nn(ht, dug_ref[...])

    nrow = t_all // 16
    cur = pl.BlockSpec((tm, tn), lambda n, t: (t, n))
    prev = pl.BlockSpec((16, tn), lambda n, t: (jnp.maximum(t * (tm // 16) - 1, 0), n))
    nxt = pl.BlockSpec((16, tn), lambda n, t: (jnp.minimum((t + 1) * (tm // 16), nrow - 1), n))
    vec = lambda r, off: pl.BlockSpec((r, tn), lambda n, t: (0, n + off))
    acc = pl.BlockSpec((8, tn), lambda n, t: (0, n))
    dw = pl.BlockSpec((D_MODEL, tn), lambda n, t: (0, n))
    return pl.pallas_call(
        body, name="ffn_gate_bwd", grid=(nc, t_all // tm),
        in_specs=[cur, prev, nxt, cur, prev, nxt, cur, nxt, vec(3, 0), vec(3, nc), vec(1, 0), vec(1, nc),
                  pl.BlockSpec((D_MODEL, tm), lambda n, t: (0, t))],
        out_specs=[cur, cur, acc, acc, dw, dw],
        out_shape=[jax.ShapeDtypeStruct((t_all, D_FF), BF16), jax.ShapeDtypeStruct((t_all, D_FF), BF16),
                   jax.ShapeDtypeStruct((8, D_FF), F32), jax.ShapeDtypeStruct((8, D_FF), F32),
                   jax.ShapeDtypeStruct((D_MODEL, D_FF), F32), jax.ShapeDtypeStruct((D_MODEL, D_FF), F32)],
        scratch_shapes=[pltpu.VMEM((tm + 2 * HALO, tn), F32)] * 2
        + [pltpu.VMEM((tm // FFN_CHUNK, FFN_CHUNK + HALO, tn), F32)] * 2,
        compiler_params=_params(),
    )(u_a, u_a, u_a, u_g, u_g, u_g, dfi, dfi, conv_w, conv_w, conv_b, conv_b, h2t)


def _ffn_down(ffn_in, w_down, xh1, ln1_g, ln1_b, ada3, ln2_g, ln2_b, target, seq):
    t_all = xh1.shape[0]
    tm = 256
    nts = seq // tm

    def body(f_ref, w_ref, xh_ref, g1_ref, b1_ref, ada_ref, g2_ref, b2_ref, tg_ref, dr2_ref, acc_ref):
        i = pl.program_id(0)

        @pl.when(i == 0)
        def _():
            acc_ref[...] = jnp.zeros_like(acc_ref)
        ffn = _nn(f_ref[...], w_ref[...])
        x1 = xh_ref[...] * g1_ref[...] + b1_ref[...]
        r2 = ALPHA * x1 + ada_ref[0, 5:6, :] * ffn
        d = r2 - jnp.mean(r2, axis=1, keepdims=True)
        rstd = lax.rsqrt(jnp.mean(d * d, axis=1, keepdims=True) + LN_EPS)
        xh2 = d * rstd
        diff = xh2 * g2_ref[...] + b2_ref[...] - tg_ref[...]
        dy = diff * (1.0 / D_MODEL)
        dr2 = _layer_norm_bwd(dy * g2_ref[...], xh2, rstd)
        dr2_ref[...] = dr2
        acc_ref[0:1, :] += jnp.sum(dy * xh2, axis=0, keepdims=True)
        acc_ref[1:2, :] += jnp.sum(dy, axis=0, keepdims=True)
        acc_ref[2:3, :] += jnp.sum(diff * diff, axis=0, keepdims=True) * (0.5 / D_MODEL)
        acc_ref[pl.ds(8 + i // nts, 1), :] += jnp.sum(dr2 * ffn, axis=0, keepdims=True)

    tok = lambda w: pl.BlockSpec((tm, w), lambda i: (i, 0))
    vec = pl.BlockSpec((1, D_MODEL), lambda i: (0, 0))
    return pl.pallas_call(
        body, name="ffn_down", grid=(t_all // tm,),
        in_specs=[tok(D_FF), pl.BlockSpec(w_down.shape, lambda i: (0, 0)), tok(D_MODEL), vec, vec,
                  pl.BlockSpec((1, 6, D_MODEL), lambda i: (i // nts, 0, 0)), vec, vec, tok(D_MODEL)],
        out_specs=[tok(D_MODEL), pl.BlockSpec((16, D_MODEL), lambda i: (0, 0))],
        out_shape=[jax.ShapeDtypeStruct((t_all, D_MODEL), F32), jax.ShapeDtypeStruct((16, D_MODEL), F32)],
        compiler_params=_params(),
    )(ffn_in, w_down, xh1, ln1_g, ln1_b, ada3, ln2_g, ln2_b, target)


def _ffn_down_bwd(dr2, ada3, w_down, seq):
    t_all = dr2.shape[0]
    tm = 256
    nts = seq // tm

    def body(d_ref, ada_ref, w_ref, dffn_ref, dfi_ref):
        dffn = (d_ref[...] * ada_ref[0, 5:6, :]).astype(BF16)
        dffn_ref[...] = dffn
        dfi_ref[...] = _nt(dffn, w_ref[...]).astype(BF16)

    tok = lambda w: pl.BlockSpec((tm, w), lambda i: (i, 0))
    return pl.pallas_call(
        body, name="ffn_down_bwd", grid=(t_all // tm,),
        in_specs=[tok(D_MODEL), pl.BlockSpec((1, 6, D_MODEL), lambda i: (i // nts, 0, 0)),
                  pl.BlockSpec(w_down.shape, lambda i: (0, 0))],
        out_specs=[tok(D_MODEL), tok(D_FF)],
        out_shape=[jax.ShapeDtypeStruct((t_all, D_MODEL), BF16), jax.ShapeDtypeStruct((t_all, D_FF), BF16)],
        compiler_params=_params(),
    )(dr2, ada3, w_down)


def _ffn_up_bwd(du_a, du_g, w_up, dr2, xh1, rs1, mix, ada3, ln1_g, ln1_b, seq):
    t_all = dr2.shape[0]
    tm = 256
    nts = seq // tm

    def body(da_ref, dg_ref, w_ref, dr2_ref, xh_ref, rs_ref, mix_ref, ada_ref, g_ref, b_ref, dr1_ref, dmix_ref,
             acc_ref):
        i = pl.program_id(0)

        @pl.when(i == 0)
        def _():
            acc_ref[...] = jnp.zeros_like(acc_ref)
        dh2 = _nt(da_ref[...], w_ref[:, :D_FF]) + _nt(dg_ref[...], w_ref[:, D_FF:])
        xh = xh_ref[...]
        x1 = xh * g_ref[...] + b_ref[...]
        dx1 = ALPHA * dr2_ref[...] + dh2 * (1.0 + ada_ref[0, 4:5, :])
        dr1 = _layer_norm_bwd(dx1 * g_ref[...], xh, rs_ref[:, 0:1])
        dr1_ref[...] = dr1
        dmix_ref[...] = (dr1 * ada_ref[0, 2:3, :]).astype(BF16)
        b = i // nts
        acc_ref[0:1, :] += jnp.sum(dx1 * xh, axis=0, keepdims=True)
        acc_ref[1:2, :] += jnp.sum(dx1, axis=0, keepdims=True)
        acc_ref[pl.ds(8 + b, 1), :] += jnp.sum(dh2 * x1, axis=0, keepdims=True)
        acc_ref[pl.ds(16 + b, 1), :] += jnp.sum(dh2, axis=0, keepdims=True)
        acc_ref[pl.ds(24 + b, 1), :] += jnp.sum(dr1 * mix_ref[...].astype(F32), axis=0, keepdims=True)

    tok = lambda w: pl.BlockSpec((tm, w), lambda i: (i, 0))
    vec = pl.BlockSpec((1, D_MODEL), lambda i: (0, 0))
    return pl.pallas_call(
        body, name="ffn_up_bwd", grid=(t_all // tm,),
        in_specs=[tok(D_FF), tok(D_FF), pl.BlockSpec(w_up.shape, lambda i: (0, 0)), tok(D_MODEL), tok(D_MODEL),
                  tok(LANES), tok(D_MODEL), pl.BlockSpec((1, 6, D_MODEL), lambda i: (i // nts, 0, 0)), vec, vec],
        out_specs=[tok(D_MODEL), tok(D_MODEL), pl.BlockSpec((32, D_MODEL), lambda i: (0, 0))],
        out_shape=[jax.ShapeDtypeStruct((t_all, D_MODEL), F32), jax.ShapeDtypeStruct((t_all, D_MODEL), BF16),
                   jax.ShapeDtypeStruct((32, D_MODEL), F32)],
        compiler_params=_params(),
    )(du_a, du_g, w_up, dr2, xh1, rs1, mix, ada3, ln1_g, ln1_b)


def _rows(a):
    return a[:, :N_HEADS].T


def _rope_freq():
    f = np.float32(ROPE_THETA) ** (-np.arange(0, ROPE_DIMS, 2, dtype=np.float32) / np.float32(ROPE_DIMS))
    return jnp.asarray(np.tile(f.astype(np.float32), LANES // (ROPE_DIMS // 2))[None, :])


def _local_step(x, positions, target, ada3, w_in, b_fgate, gn_a, gn_b, ln1_g, ln1_b, conv_b, ln2_g, ln2_b,
                late_shards):
    nbat, seq, _ = x.shape
    t_all = nbat * seq
    xf = x.reshape(t_all, D_MODEL)
    tg = target.reshape(t_all, D_MODEL)
    pos = positions.reshape(t_all, 1)
    freq = _rope_freq()

    wqkv = jnp.concatenate([w_in[:, :3 * WIDTH], w_in[:, 3 * WIDTH + N_HEADS:]], axis=1)
    wf16 = jnp.zeros((16, D_MODEL), BF16).at[:N_HEADS].set(w_in[:, 3 * WIDTH:3 * WIDTH + N_HEADS].T)
    bf = b_fgate.reshape(N_HEADS, 1)

    perms = [_perm_matrix(TOK_TM, d, tr) for tr in (False, True) for d in DILATIONS[1:]]
    h1, za, zb1, zb4, zb16, vt, fa_t = _inproj(xf, ada3, pos, wqkv, wf16, freq, perms, seq)
    zbs = [zb1, zb4.reshape(t_all, 3 * WIDTH), zb16.reshape(t_all, 3 * WIDTH)]
    f_row = _fgate_fwd(fa_t, bf, seq)
    f_col = jnp.zeros((t_all, LANES), F32).at[:, :N_HEADS].set(f_row.T)
    oa, lse_row_a, gathered = _fox_fwd(za, vt, f_col, seq, [late_shards[n] for n in LATE])
    w_out, w_up, conv_w, w_down = (_full_from_gathered(n, g) for n, g in zip(LATE, gathered))
    o3, l3 = zip(*[_dil_fwd(zb, seq, d) for zb, d in zip(zbs, DILATIONS)])
    ob, lse_b, lse_b4, lse_b16, merged, mix, xh1, rs1, h2, h2t = _mix_out(oa, o3, l3, gn_a, gn_b, w_out, xf, ada3, ln1_g,
                                                                      ln1_b, perms, seq)
    u_a, u_g, ffn_in = _ffn_up_gate(h2, w_up, conv_w, conv_b, seq)
    dr2, acc2 = _ffn_down(ffn_in, w_down, xh1, ln1_g, ln1_b, ada3, ln2_g, ln2_b, tg, seq)

    dffn, dfi = _ffn_down_bwd(dr2, ada3, w_down, seq)
    d_w_down = _matmul_tn(dffn, ffn_in, 512, 512, "dw_down").T
    du_a, du_g, acc_ca, acc_cg, dw_up_a, dw_up_g = _ffn_gate_bwd(u_a, u_g, dfi, conv_w, conv_b, h2t, seq)
    dr1, dmix, acc1 = _ffn_up_bwd(du_a, du_g, w_up, dr2, xh1, rs1, mix, ada3, ln1_g, ln1_b, seq)
    d_w_up = jnp.concatenate([dw_up_a, dw_up_g], axis=1)

    doa, dob, dob4, dob16, dl_a, dl_b, dl_b4, dl_b16, acc_gn = _mix_out_bwd(dmix, w_out, oa, ob, gn_a, gn_b, perms, seq)
    d_w_out = _matmul_tn(merged, dmix, 512, 512, "dw_out")
    late_grads = dict(w_out=d_w_out, w_up=d_w_up, conv_w=jnp.concatenate([acc_ca[0:3], acc_cg[0:3]], axis=1),
                      w_down=d_w_down)
    dka, dva, df_k, dqt, df_q, late_parts = _fox_bwd(za, doa, f_col, lse_row_a, _rows(dl_a), seq,
                                                     [_payload(n, _dest_major(n, late_grads[n])) for n in LATE])
    dfa_t, dbf = _fgate_bwd(_rows(df_k) + df_q, fa_t, bf, seq)
    flat = lambda a: a.reshape(t_all, a.shape[-1])
    dil = []
    for zb, d, do, lse, dl in zip(zbs, DILATIONS, (dob, flat(dob4), flat(dob16)),
                                  (lse_b, flat(lse_b4), flat(lse_b16)), (dl_b, flat(dl_b4), flat(dl_b16))):
        dil.append((_dil_bwd_dq(zb, do, lse, dl, seq, d), *_dil_bwd_dkv(zb, do, lse, dl, seq, d)))
    dfa16 = jnp.zeros((16, t_all), BF16).at[:N_HEADS].set(dfa_t.astype(BF16))
    grad_x, dz, acc0 = _inproj_bwd(dqt, dka, dva, dil[0], dil[1], dil[2], dfa16, pos, wqkv, wf16, freq, perms, dr1, xf,
                                   ada3, seq)
    d_wqkv = _matmul_tn(h1, dz, 512, 512, "dw_in")
    d_wf = _matmul_rows(dfa16, h1, 512, "dw_fgate")[:N_HEADS].T
    d_w_in = jnp.concatenate([d_wqkv[:, :3 * WIDTH], d_wf, d_wqkv[:, 3 * WIDTH:]], axis=1)

    dada = jnp.concatenate([acc0[8:8 + nbat], acc0[:nbat], acc1[24:24 + nbat], acc1[16:16 + nbat], acc1[8:8 + nbat],
                            acc2[8:8 + nbat]], axis=1)

    grads = dict(
        dada=dada, b_ada=jnp.sum(dada, axis=0, keepdims=True), w_in=d_w_in, b_fgate=dbf[:, 0][None, :],
        gn_a=acc_gn[0:1, :WIDTH], gn_b=acc_gn[0:1, WIDTH:], ln1_g=acc1[0:1], ln1_b=acc1[1:2],
        conv_b=jnp.concatenate([acc_ca[3:4], acc_cg[3:4]], axis=1), ln2_g=acc2[0:1], ln2_b=acc2[1:2])
    return acc2[2:3], grad_x.reshape(x.shape), grads, dict(zip(LATE, late_parts))


LATE = ("w_out", "w_up", "conv_w", "w_down")
BIG = ("w_ada", "w_in") + LATE
COLUMN_SHARDED = ("w_ada", "w_in", "w_up", "conv_w")


def _payload(name, a):
    return a if name == "conv_w" else a.astype(BF16)
SMALL = ("b_ada", "b_fgate", "gn_a", "gn_b", "ln1_g", "ln1_b", "conv_b", "ln2_g", "ln2_b")
ADAM_ROWS = dict(w_ada=256, w_in=256, w_out=128, w_up=256, conv_w=3, w_down=176)
SMALL_ROWS = 24


def _full_from_gathered(name, g):
    if name in COLUMN_SHARDED:
        return g.transpose(1, 0, 2).reshape(g.shape[1], N_DEV * g.shape[2])
    return g.reshape(N_DEV * g.shape[1], g.shape[2])


def _dest_major(name, full):
    if name in COLUMN_SHARDED:
        r, cfull = full.shape
        return full.reshape(r, N_DEV, cfull // N_DEV).transpose(1, 0, 2)
    return full.reshape(N_DEV, full.shape[0] // N_DEV, full.shape[1])


def _pack_small(vals, extra=None):
    parts = [vals[n].reshape(-1) for n in SMALL]
    if extra is not None:
        parts.append(extra.reshape(-1))
    flat = jnp.concatenate(parts)
    return jnp.pad(flat, (0, SMALL_ROWS * D_MODEL - flat.shape[0])).reshape(SMALL_ROWS, D_MODEL)


def _unpack_small(packed, like):
    flat = packed.reshape(-1)
    out, off = {}, 0
    for n in SMALL:
        size = like[n].size
        out[n] = flat[off:off + size].reshape(like[n].shape)
        off += size
    return out, flat[off:off + D_MODEL]


def kernel(x, c, positions, w_ada, b_ada, w_in, b_fgate, gn_a, gn_b, w_out, ln1_g, ln1_b, w_up, conv_w, conv_b, w_down, ln2_g, ln2_b, loss_target, m_w_ada, m_b_ada, m_w_in, m_b_fgate, m_gn_a, m_gn_b, m_w_out, m_ln1_g, m_ln1_b, m_w_up, m_conv_w, m_conv_b, m_w_down, m_ln2_g, m_ln2_b, v_w_ada, v_b_ada, v_w_in, v_b_fgate, v_gn_a, v_gn_b, v_w_out, v_ln1_g, v_ln1_b, v_w_up, v_conv_w, v_conv_b, v_w_down, v_ln2_g, v_ln2_b):
    w = dict(w_ada=w_ada[0], b_ada=b_ada, w_in=w_in[0], b_fgate=b_fgate, gn_a=gn_a, gn_b=gn_b, w_out=w_out[0],
             ln1_g=ln1_g, ln1_b=ln1_b, w_up=w_up[0], conv_w=conv_w[0], conv_b=conv_b, w_down=w_down[0], ln2_g=ln2_g,
             ln2_b=ln2_b)
    m = dict(w_ada=m_w_ada[0], b_ada=m_b_ada, w_in=m_w_in[0], b_fgate=m_b_fgate, gn_a=m_gn_a, gn_b=m_gn_b,
             w_out=m_w_out[0], ln1_g=m_ln1_g, ln1_b=m_ln1_b, w_up=m_w_up[0], conv_w=m_conv_w[0], conv_b=m_conv_b,
             w_down=m_w_down[0], ln2_g=m_ln2_g, ln2_b=m_ln2_b)
    v = dict(w_ada=v_w_ada[0], b_ada=v_b_ada, w_in=v_w_in[0], b_fgate=v_b_fgate, gn_a=v_gn_a, gn_b=v_gn_b,
             w_out=v_w_out[0], ln1_g=v_ln1_g, ln1_b=v_ln1_b, w_up=v_w_up[0], conv_w=v_conv_w[0], conv_b=v_conv_b,
             w_down=v_w_down[0], ln2_g=v_ln2_g, ln2_b=v_ln2_b)

    nbat = x.shape[0]
    me = 4 * lax.axis_index("x") + 2 * lax.axis_index("y") + lax.axis_index("c")
    ada_cols = w["w_ada"].shape[1]

    c_all, w_in_all = _gather_two_level([c, _payload("w_in", w["w_in"])], "weight_gather")
    c_all = c_all.reshape(N_DEV * nbat, D_MODEL)
    ada_mine = _ada_fwd(c_all, w["w_ada"], lax.dynamic_slice(b_ada, (0, me * ada_cols), (1, ada_cols)))
    (ada_parts,) = _exchange([ada_mine.reshape(N_DEV, nbat, ada_cols)], [False], "ada_exchange")
    ada3 = ada_parts.transpose(1, 0, 2).reshape(nbat, 6, D_MODEL)

    loss_lanes, grad_x, g_local, parts = _local_step(
        x, positions, loss_target, ada3, _full_from_gathered("w_in", w_in_all), b_fgate, gn_a, gn_b, ln1_g, ln1_b,
        conv_b, ln2_g, ln2_b, {n: _payload(n, w[n]) for n in LATE})

    parts["w_in"], dada_all, small_all = _exchange(
        [_payload("w_in", _dest_major("w_in", g_local["w_in"])), g_local["dada"], _pack_small(g_local, loss_lanes)],
        [False, True, True], "grad_exchange")
    dada_cols = lax.dynamic_slice(dada_all.reshape(N_DEV * nbat, 6 * D_MODEL), (0, me * ada_cols),
                                  (N_DEV * nbat, ada_cols))
    parts["w_ada"] = _ada_bwd(c_all, dada_cols)[None]

    grad, delta, new_m, new_v = {}, {}, {}, {}
    for n in BIG:
        grad[n], delta[n], new_m[n], new_v[n] = (
            a[None] for a in _adamw(parts[n], w[n], m[n], v[n], ADAM_ROWS[n], "adamw_" + n))
    packed = _adamw(small_all, _pack_small(w), _pack_small(m), _pack_small(v), SMALL_ROWS, "adamw_small")
    for dst, pk in zip((grad, delta, new_m, new_v), packed):
        vals, lanes = _unpack_small(pk, w)
        dst.update(vals)
        if dst is grad:
            loss = jnp.sum(lanes)

    order = ("w_ada", "b_ada", "w_in", "b_fgate", "gn_a", "gn_b", "w_out", "ln1_g", "ln1_b", "w_up", "conv_w", "conv_b",
             "w_down", "ln2_g", "ln2_b")
    return (loss, grad_x, *[grad[n] for n in order], *[delta[n] for n in order], *[new_m[n] for n in order],
            *[new_v[n] for n in order])
```

```python
import functools

import numpy as np
import jax
import jax.numpy as jnp
from jax import lax
from jax.experimental import pallas as pl
from jax.experimental.pallas import tpu as pltpu

F32, BF16 = jnp.float32, jnp.bfloat16
HIGHEST = lax.Precision.HIGHEST
MESH = pl.DeviceIdType.MESH
ANY = pl.BlockSpec(memory_space=pl.ANY)

D_MODEL = 1024
N_HEADS = 8
HEAD_DIM = 64
WIDTH = 512
D_FF = 2816
N_DEV = 8
ROPE_DIMS = 16
ROPE_THETA = 500000.0
ALPHA = 2.0 ** 0.25
LN_EPS = 1e-5
RMS_EPS = 1e-6
NEG = -1e30
Q_SCALE = 0.125
BLK = 128
LANES = 128
VMEM_LIMIT_BYTES = 56 * 1024 * 1024

ADAM_LR, ADAM_B1, ADAM_B2, ADAM_EPS, ADAM_WD, ADAM_STEP = 0.001, 0.9, 0.999, 1e-08, 0.01, 10


def _params(vmem=VMEM_LIMIT_BYTES):
    return pltpu.CompilerParams(vmem_limit_bytes=vmem)


def _nn(a, b):
    return jnp.dot(a, b, preferred_element_type=F32)


def _nt(a, b):
    return lax.dot_general(a, b, (((1,), (1,)), ((), ())), preferred_element_type=F32)


def _tn(a, b):
    return lax.dot_general(a, b, (((0,), (0,)), ((), ())), preferred_element_type=F32)


def _head_mats():
    r = lax.broadcasted_iota(jnp.int32, (LANES, WIDTH), 0)
    c = lax.broadcasted_iota(jnp.int32, (LANES, WIDTH), 1)
    e = ((c >> 6) == r).astype(BF16)
    r2 = lax.broadcasted_iota(jnp.int32, (WIDTH, LANES), 0)
    c2 = lax.broadcasted_iota(jnp.int32, (WIDTH, LANES), 1)
    et = ((r2 >> 6) == c2).astype(BF16)
    return e, et


def _split3(x):
    hi = x.astype(BF16)
    r = x - hi.astype(F32)
    mid = r.astype(BF16)
    return hi, mid, (r - mid.astype(F32)).astype(BF16)


def _hexp(w, e):
    return sum(_nn(part, e) for part in _split3(w)[:2])


def _hsum(x, et):
    return sum(_nn(part, et) for part in _split3(x)[:2])


def _perm_matrix(rows, d, transpose):
    i = np.arange(rows)
    j = (i % (rows // d)) * d + i // (rows // d)
    p = np.zeros((rows, rows), np.float32)
    p[i, j] = 1.0
    return jnp.asarray(p.T if transpose else p, BF16)


def _permute_f32(p, x):
    return sum(_nn(p, part) for part in _split3(x))


def _store_classes(ref, y, d):
    n = y.shape[0] // d
    for r in range(d):
        ref[r] = y[r * n:(r + 1) * n, :]


def _load_classes(ref, d):
    return jnp.concatenate([ref[r] for r in range(d)], axis=0)


def _rope_tabs(pos_ref, fr_ref, sign):
    ang = pos_ref[...].astype(F32) * fr_ref[...]
    lane = lax.broadcasted_iota(jnp.int32, ang.shape, 1) & (HEAD_DIM - 1)
    m1 = lane < ROPE_DIMS // 2
    m2 = (lane >= ROPE_DIMS // 2) & (lane < ROPE_DIMS)
    cos = jnp.cos(ang)
    sin = jnp.sin(ang) * sign
    return (jnp.where(m1 | m2, cos, 1.0), jnp.where(m1, -sin, 0.0), jnp.where(m2, sin, 0.0))


def _rope(z, tabs):
    c, s1, s2 = tabs
    parts = []
    for p in range(z.shape[1] // LANES):
        zp = z[:, LANES * p:LANES * (p + 1)]
        parts.append(zp * c + pltpu.roll(zp, LANES - 8, 1) * s1 + pltpu.roll(zp, 8, 1) * s2)
    return jnp.concatenate(parts, axis=1)


def _half_masks(rows):
    lane = lax.broadcasted_iota(jnp.int32, (rows, LANES), 1)
    lo = lane < HEAD_DIM
    return lo, jnp.logical_not(lo)


def _layer_norm_bwd(dxh, xh, rstd):
    m1 = jnp.mean(dxh, axis=1, keepdims=True)
    m2 = jnp.mean(dxh * xh, axis=1, keepdims=True)
    return rstd * (dxh - m1 - xh * m2)


def _coords():
    return lax.axis_index("x"), lax.axis_index("y"), lax.axis_index("c")


def _peer(x, y, c, k):
    return (1 - x if k & 4 else x, 1 - y if k & 2 else y, 1 - c if k & 1 else c)


def _comm_sems(n):
    return [pltpu.SemaphoreType.DMA((N_DEV - 1, n)), pltpu.SemaphoreType.DMA((N_DEV - 1, n)),
            pltpu.SemaphoreType.DMA((n,))]


def _comm_copies(ins, outs, to_all, sems):
    send_sems, recv_sems, local_sems = sems
    x, y, c = _coords()
    me = 4 * x + 2 * y + c
    copies = [pltpu.make_async_copy(ins[t] if to_all[t] else ins[t].at[me], outs[t].at[me], local_sems.at[t])
              for t in range(len(ins))]
    for k in range(1, N_DEV):
        px, py, pc = _peer(x, y, c, k)
        dest = 4 * px + 2 * py + pc
        for t in range(len(ins)):
            copies.append(pltpu.make_async_remote_copy(
                src_ref=ins[t] if to_all[t] else ins[t].at[dest], dst_ref=outs[t].at[me],
                send_sem=send_sems.at[k - 1, t], recv_sem=recv_sems.at[k - 1, t],
                device_id=(px, py, pc), device_id_type=MESH))
    return copies


def _comm_out_shapes(ins, to_all):
    return [jax.ShapeDtypeStruct(((N_DEV,) + a.shape) if ta else a.shape, a.dtype) for a, ta in zip(ins, to_all)]


def _exchange(ins, to_all, name):
    n = len(ins)

    def body(*refs):
        copies = _comm_copies(refs[:n], refs[n:2 * n], to_all, refs[2 * n:])
        for cp in copies:
            cp.start()
        for cp in copies:
            cp.wait()

    return pl.pallas_call(
        body, name=name, out_shape=_comm_out_shapes(ins, to_all), in_specs=[ANY] * n, out_specs=[ANY] * n,
        scratch_shapes=_comm_sems(n),
    )(*ins)


def _gather_two_level(ins, name):
    n = len(ins)

    def body(*refs):
        srcs, outs = refs[:n], refs[n:2 * n]
        send_sems, recv_sems, local_sems = refs[2 * n:]
        x, y, c = _coords()
        me = 4 * x + 2 * y + c
        sibling = (x, y, 1 - c)
        chips = [(1 - x, y), (x, 1 - y), (1 - x, 1 - y)]
        slot = lambda px, py, pc: 4 * px + 2 * py + pc

        def copy(k, t, block, to, own=False):
            return pltpu.make_async_remote_copy(
                src_ref=srcs[t] if own else outs[t].at[block], dst_ref=outs[t].at[block],
                send_sem=send_sems.at[k, t], recv_sem=recv_sems.at[k, t], device_id=to, device_id_type=MESH)

        local = [pltpu.make_async_copy(srcs[t], outs[t].at[me], local_sems.at[t]) for t in range(n)]
        first = [copy(0, t, me, sibling, own=True) for t in range(n)]
        first += [copy(1 + j, t, me, (*chip, c), own=True) for j, chip in enumerate(chips) for t in range(n)]
        for cp in local + first:
            cp.start()
        passed = []
        for j, chip in enumerate(chips):
            for t in range(n):
                copy(1 + j, t, slot(*chip, c), (x, y, c)).wait_recv()
                cp = copy(4 + j, t, slot(*chip, c), sibling)
                cp.start()
                passed.append(cp)
        for t in range(n):
            copy(0, t, slot(x, y, 1 - c), (x, y, c)).wait_recv()
            for j, chip in enumerate(chips):
                copy(4 + j, t, slot(*chip, 1 - c), (x, y, c)).wait_recv()
        for cp in first + passed:
            cp.wait_send()
        for cp in local:
            cp.wait()

    return pl.pallas_call(
        body, name=name, out_shape=_comm_out_shapes(ins, [True] * n), in_specs=[ANY] * n, out_specs=[ANY] * n,
        scratch_shapes=_comm_sems(n),
    )(*ins)


def _adamw(parts, w, m, v, rows, name):
    n_parts, r_all, cols = parts.shape
    c1 = 1.0 - ADAM_B1 ** ADAM_STEP
    c2 = 1.0 - ADAM_B2 ** ADAM_STEP

    def body(p_ref, w_ref, m_ref, v_ref, g_ref, d_ref, mo_ref, vo_ref):
        g = p_ref[0].astype(F32)
        for s in range(1, n_parts):
            g = g + p_ref[s].astype(F32)
        mn = ADAM_B1 * m_ref[...] + (1.0 - ADAM_B1) * g
        vn = ADAM_B2 * v_ref[...] + (1.0 - ADAM_B2) * (g * g)
        m_hat = mn / c1
        v_hat = vn / c2
        g_ref[...] = g
        d_ref[...] = -ADAM_LR * (m_hat / (jnp.sqrt(v_hat) + ADAM_EPS) + ADAM_WD * w_ref[...])
        mo_ref[...] = mn
        vo_ref[...] = vn

    spec = pl.BlockSpec((rows, cols), lambda i: (i, 0))
    return pl.pallas_call(
        body, name=name, grid=(r_all // rows,),
        in_specs=[pl.BlockSpec((n_parts, rows, cols), lambda i: (0, i, 0)), spec, spec, spec],
        out_specs=[spec] * 4, out_shape=[jax.ShapeDtypeStruct((r_all, cols), F32)] * 4,
        compiler_params=_params(),
    )(parts, w, m, v)


def _matmul_tn(a, b, chunk, tk, name):
    t_all, k1 = a.shape
    n = b.shape[1]

    def body(a_ref, b_ref, o_ref):
        @pl.when(pl.program_id(0) == 0)
        def _():
            o_ref[...] = jnp.zeros_like(o_ref)
        at = a_ref[...].astype(F32).T.astype(BF16)
        for j in range(0, n, chunk):
            cs = slice(j, min(j + chunk, n))
            o_ref[:, cs] += _nn(at, b_ref[:, cs])

    return pl.pallas_call(
        body, name=name, grid=(t_all // tk,),
        in_specs=[pl.BlockSpec((tk, k1), lambda t: (t, 0)), pl.BlockSpec((tk, n), lambda t: (t, 0))],
        out_specs=pl.BlockSpec((k1, n), lambda t: (0, 0)),
        out_shape=jax.ShapeDtypeStruct((k1, n), F32), compiler_params=_params(),
    )(a, b)


def _matmul_rows(a, b, tk, name):
    r, t_all = a.shape
    n = b.shape[1]

    def body(a_ref, b_ref, o_ref):
        @pl.when(pl.program_id(0) == 0)
        def _():
            o_ref[...] = jnp.zeros_like(o_ref)
        o_ref[...] += _nn(a_ref[...], b_ref[...])

    return pl.pallas_call(
        body, name=name, grid=(t_all // tk,),
        in_specs=[pl.BlockSpec((r, tk), lambda t: (0, t)), pl.BlockSpec((tk, n), lambda t: (t, 0))],
        out_specs=pl.BlockSpec((r, n), lambda t: (0, 0)),
        out_shape=jax.ShapeDtypeStruct((r, n), F32), compiler_params=_params(),
    )(a, b)


def _ada_fwd(c_all, w_ada, b_ada):
    whole = lambda a: pl.BlockSpec(a.shape, lambda j: (0, 0))

    def body(c_ref, w_ref, b_ref, o_ref):
        cv = c_ref[...]
        s = (cv * jax.nn.sigmoid(cv)).astype(BF16)
        o_ref[...] = _nn(s, w_ref[...].astype(BF16)) + b_ref[...]

    out = jax.ShapeDtypeStruct((c_all.shape[0], w_ada.shape[1]), F32)
    return pl.pallas_call(
        body, name="ada_fwd", grid=(1,), in_specs=[whole(c_all), whole(w_ada), whole(b_ada)], out_specs=whole(out),
        out_shape=out, compiler_params=_params(),
    )(c_all, w_ada, b_ada)


def _ada_bwd(c_all, dada):
    whole = lambda a: pl.BlockSpec(a.shape, lambda j: (0, 0))

    def body(c_ref, d_ref, o_ref):
        cv = c_ref[...]
        s = (cv * jax.nn.sigmoid(cv)).astype(BF16)
        o_ref[...] = _tn(s, d_ref[...].astype(BF16))

    out = jax.ShapeDtypeStruct((D_MODEL, dada.shape[1]), F32)
    return pl.pallas_call(
        body, name="ada_bwd", grid=(1,), in_specs=[whole(c_all), whole(dada)], out_specs=whole(out), out_shape=out,
        compiler_params=_params(),
    )(c_all, dada)


TOK_TM = 256
DILATIONS = (1, 4, 16)


def _class_spec(d, width, nts):
    return pl.BlockSpec((d, TOK_TM // d, width), lambda i: (i // nts, i % nts, 0))


def _class_shape(t_all, seq, d, width, dtype):
    return jax.ShapeDtypeStruct((t_all // seq * d, seq // d, width), dtype)


def _inproj(x, ada3, pos, wqkv, wf16, freq, perms, seq):
    t_all = x.shape[0]
    tm = TOK_TM
    nts = seq // tm

    def body(x_ref, ada_ref, pos_ref, w_ref, wf_ref, fr_ref, p4_ref, p16_ref, h1_ref, za_ref, zb_ref, zb4_ref,
             zb16_ref, vt_ref, fa_ref):
        h1 = (x_ref[...] * (1.0 + ada_ref[0, 1:2, :]) + ada_ref[0, 0:1, :]).astype(BF16)
        h1_ref[...] = h1
        tabs = _rope_tabs(pos_ref, fr_ref, 1.0)
        for n in range(6):
            z = _nn(h1, w_ref[:, n * WIDTH:(n + 1) * WIDTH])
            if n in (3, 4):
                z = _rope(z, tabs)
            if n in (0, 3):
                z = z * Q_SCALE
            if n == 2:
                vt_ref[...] = z.T.astype(BF16)
            dst = za_ref if n < 3 else zb_ref
            dst[:, (n % 3) * WIDTH:(n % 3 + 1) * WIDTH] = z.astype(BF16)
        fa_ref[...] = _nt(wf_ref[...], h1)[:N_HEADS]
        zb = zb_ref[...]
        _store_classes(zb4_ref, _nn(p4_ref[...], zb).astype(BF16), 4)
        _store_classes(zb16_ref, _nn(p16_ref[...], zb).astype(BF16), 16)

    tok = lambda w: pl.BlockSpec((tm, w), lambda i: (i, 0))
    whole = lambda a: pl.BlockSpec(a.shape, lambda i: (0, 0))
    return pl.pallas_call(
        body, name="inproj", grid=(t_all // tm,),
        in_specs=[tok(D_MODEL), pl.BlockSpec((1, 6, D_MODEL), lambda i: (i // nts, 0, 0)), tok(1), whole(wqkv),
                  whole(wf16), pl.BlockSpec((1, LANES), lambda i: (0, 0)), whole(perms[0]), whole(perms[1])],
        out_specs=[tok(D_MODEL), tok(3 * WIDTH), tok(3 * WIDTH), _class_spec(4, 3 * WIDTH, nts),
                   _class_spec(16, 3 * WIDTH, nts), pl.BlockSpec((WIDTH, tm), lambda i: (i // nts, i % nts)),
                   pl.BlockSpec((N_HEADS, tm), lambda i: (0, i))],
        out_shape=[jax.ShapeDtypeStruct((t_all, D_MODEL), BF16), jax.ShapeDtypeStruct((t_all, 3 * WIDTH), BF16),
                   jax.ShapeDtypeStruct((t_all, 3 * WIDTH), BF16), _class_shape(t_all, seq, 4, 3 * WIDTH, BF16),
                   _class_shape(t_all, seq, 16, 3 * WIDTH, BF16),
                   jax.ShapeDtypeStruct((t_all // seq * WIDTH, seq), BF16),
                   jax.ShapeDtypeStruct((N_HEADS, t_all), F32)],
        compiler_params=_params(),
    )(x, ada3, pos, wqkv, wf16, freq, perms[0], perms[1])


def _chunk_rows(a_t, seq):
    t_all = a_t.shape[1]
    return a_t.reshape(N_HEADS, t_all // seq, seq // LANES, LANES).transpose(1, 0, 2, 3).reshape(-1, LANES)


def _unchunk_rows(a, seq):
    nbat = a.shape[0] * LANES // (N_HEADS * seq)
    return a.reshape(nbat, N_HEADS, seq // LANES, LANES).transpose(1, 0, 2, 3).reshape(N_HEADS, nbat * seq)


def _chunk_carry(tot, nchunk, later):
    rows = tot.shape[0]
    r = lax.broadcasted_iota(jnp.int32, (rows, rows), 0)
    c = lax.broadcasted_iota(jnp.int32, (rows, rows), 1)
    sel = ((r // nchunk) == (c // nchunk)) & ((c > r) if later else (c < r))
    mat = sel.astype(BF16)
    return sum(_nn(mat, part) for part in _split3(jnp.broadcast_to(tot, (rows, LANES))))


def _fgate_fwd(fa_t, bf, seq):
    x = _chunk_rows(fa_t, seq)
    rows = x.shape[0]
    nchunk = seq // LANES
    bias = jnp.broadcast_to(bf.reshape(1, N_HEADS, 1), (rows // (N_HEADS * nchunk), N_HEADS, nchunk)).reshape(rows, 1)

    def body(x_ref, b_ref, f_ref):
        lane = lax.broadcasted_iota(jnp.int32, (rows, LANES), 1)
        xv = x_ref[...] + b_ref[...]
        lf = jnp.minimum(xv, 0.0) - jnp.log(1.0 + jnp.exp(-jnp.abs(xv)))
        for s in (1, 2, 4, 8, 16, 32, 64):
            lf = lf + jnp.where(lane >= s, pltpu.roll(lf, s, 1), 0.0)
        f_ref[...] = lf + _chunk_carry(lf[:, LANES - 1:LANES], nchunk, False)

    whole = lambda a: pl.BlockSpec(a.shape, lambda i: (0, 0))
    out = pl.pallas_call(
        body, name="fgate_fwd", grid=(1,), in_specs=[whole(x), whole(bias)], out_specs=whole(x),
        out_shape=jax.ShapeDtypeStruct(x.shape, F32), compiler_params=_params(),
    )(x, bias)
    return _unchunk_rows(out, seq)


def _fgate_bwd(df_t, fa_t, bf, seq):
    d_in = _chunk_rows(df_t, seq)
    x = _chunk_rows(fa_t, seq)
    rows = x.shape[0]
    nchunk = seq // LANES
    bias = jnp.broadcast_to(bf.reshape(1, N_HEADS, 1), (rows // (N_HEADS * nchunk), N_HEADS, nchunk)).reshape(rows, 1)

    def body(d_ref, x_ref, b_ref, o_ref, s_ref):
        lane = lax.broadcasted_iota(jnp.int32, (rows, LANES), 1)
        d = d_ref[...]
        for s in (1, 2, 4, 8, 16, 32, 64):
            d = d + jnp.where(lane < LANES - s, pltpu.roll(d, LANES - s, 1), 0.0)
        d = d + _chunk_carry(d[:, 0:1], nchunk, True)
        dfa = d * jax.nn.sigmoid(-(x_ref[...] + b_ref[...]))
        o_ref[...] = dfa
        g = lax.broadcasted_iota(jnp.int32, (2 * N_HEADS, rows), 0)
        r = lax.broadcasted_iota(jnp.int32, (2 * N_HEADS, rows), 1)
        group = (((r // nchunk) % N_HEADS) == g).astype(BF16)
        per_head = sum(_nn(group, part) for part in _split3(dfa))[:N_HEADS]
        s_ref[...] = jnp.broadcast_to(jnp.sum(per_head, axis=1, keepdims=True), (N_HEADS, LANES))

    whole = lambda a: pl.BlockSpec(a.shape, lambda i: (0, 0))
    dfa, sums = pl.pallas_call(
        body, name="fgate_bwd", grid=(1,), in_specs=[whole(d_in), whole(x), whole(bias)],
        out_specs=[whole(x), pl.BlockSpec((N_HEADS, LANES), lambda i: (0, 0))],
        out_shape=[jax.ShapeDtypeStruct(x.shape, F32), jax.ShapeDtypeStruct((N_HEADS, LANES), F32)],
        compiler_params=_params(),
    )(d_in, x, bias)
    return _unchunk_rows(dfa, seq), sums


FOX_T = 256


def _fox_prep(dst, src_ref, lo, hi):
    for p in range(4):
        v = src_ref[:, LANES * p:LANES * (p + 1)]
        dst[2 * p] = jnp.where(lo, v, jnp.zeros_like(v))
        dst[2 * p + 1] = jnp.where(hi, v, jnp.zeros_like(v))


def _fox_fwd(za, vt, f_col, seq, shards):
    t_all = za.shape[0]
    tq = FOX_T
    nq = seq // tq
    nbat = t_all // seq
    n = len(shards)
    to_all = [True] * n

    def body(*refs):
        q_ref, k_ref, vt_ref, fc_ref = refs[:4]
        o_ref, lse_ref = refs[4 + n:6 + n]
        qm_sc, m_sc, l_sc, acc_sc, a_sc, st_sc, pe_sc = refs[6 + 2 * n:13 + 2 * n]
        comm = (refs[4:4 + n], refs[6 + n:6 + 2 * n], to_all, refs[13 + 2 * n:])
        i = pl.program_id(1)

        @pl.when((pl.program_id(0) == 0) & (i == 0))
        def _():
            for cp in _comm_copies(*comm):
                cp.start()
        lo, hi = _half_masks(tq)
        r = lax.broadcasted_iota(jnp.int32, (tq, tq), 0)
        c = lax.broadcasted_iota(jnp.int32, (tq, tq), 1)
        tri = c >= r
        _fox_prep(qm_sc, q_ref, lo, hi)
        m_sc[...] = jnp.full(m_sc.shape, NEG, F32)
        l_sc[...] = jnp.zeros_like(l_sc)
        acc_sc[...] = jnp.zeros_like(acc_sc)

        def block(j, masked):
            sl = pl.ds(pl.multiple_of(j * tq, tq), tq)
            for p in range(4):
                kj = k_ref[sl, LANES * p:LANES * (p + 1)]
                for h in (2 * p, 2 * p + 1):
                    st = _nt(kj, qm_sc[h]) - fc_ref[sl, h:h + 1]
                    st_sc[h] = jnp.where(tri, st, NEG) if masked else st
            for h in range(N_HEADS):
                st = st_sc[h]
                m = m_sc[h:h + 1, :]
                mn = jnp.maximum(m, jnp.max(st, axis=0, keepdims=True))
                a = jnp.exp(m - mn)
                pe = jnp.exp(st - mn)
                m_sc[h:h + 1, :] = mn
                a_sc[h:h + 1, :] = a
                l_sc[h:h + 1, :] = a * l_sc[h:h + 1, :] + jnp.sum(pe, axis=0, keepdims=True)
                pe_sc[h] = pe.astype(BF16)
            for h in range(N_HEADS):
                acc_sc[h] = a_sc[h:h + 1, :] * acc_sc[h] + _nn(vt_ref[HEAD_DIM * h:HEAD_DIM * (h + 1), sl], pe_sc[h])

        def step(j, carry):
            block(j, False)
            return carry

        lax.fori_loop(0, i, step, 0)
        block(i, True)
        lse_ref[...] = m_sc[...] + jnp.log(l_sc[...])
        for p in range(4):
            ot = jnp.concatenate([acc_sc[h] / l_sc[h:h + 1, :] for h in (2 * p, 2 * p + 1)], axis=0)
            o_ref[:, LANES * p:LANES * (p + 1)] = ot.T

        @pl.when((pl.program_id(0) == nbat - 1) & (i == nq - 1))
        def _():
            for cp in _comm_copies(*comm):
                cp.wait()

    res = pl.pallas_call(
        body, name="fox_fwd", grid=(nbat, nq),
        in_specs=[pl.BlockSpec((tq, WIDTH), lambda b, i: (b * nq + i, 0)),
                  pl.BlockSpec((seq, WIDTH), lambda b, i: (b, 1)), pl.BlockSpec((WIDTH, seq), lambda b, i: (b, 0)),
                  pl.BlockSpec((seq, LANES), lambda b, i: (b, 0))] + [ANY] * n,
        out_specs=[pl.BlockSpec((tq, WIDTH), lambda b, i: (b * nq + i, 0)),
                   pl.BlockSpec((N_HEADS, tq), lambda b, i: (0, b * nq + i))] + [ANY] * n,
        out_shape=[jax.ShapeDtypeStruct((t_all, WIDTH), F32), jax.ShapeDtypeStruct((N_HEADS, t_all), F32)]
        + _comm_out_shapes(shards, to_all),
        scratch_shapes=[pltpu.VMEM((N_HEADS, tq, LANES), BF16), pltpu.VMEM((N_HEADS, tq), F32),
                        pltpu.VMEM((N_HEADS, tq), F32), pltpu.VMEM((N_HEADS, HEAD_DIM, tq), F32),
                        pltpu.VMEM((N_HEADS, tq), F32), pltpu.VMEM((N_HEADS, tq, tq), F32),
                        pltpu.VMEM((N_HEADS, tq, tq), BF16)] + _comm_sems(n),
        compiler_params=_params(),
    )(za, za, vt, f_col, *shards)
    return res[0], res[1], res[2:]


def _fox_bwd(za, do, f_col, lse_row, dl_row, seq, grads):
    t_all = za.shape[0]
    tk = FOX_T
    nk = seq // tk
    nbat = t_all // seq
    n = len(grads)
    to_all = [False] * n

    def body(*refs):
        k_ref, v_ref, q_ref, do_ref, fc_ref, lr_ref, dr_ref = refs[:7]
        dk_ref, dv_ref, df_ref, dqt_ref, dfq_ref = refs[7 + n:12 + n]
        km_sc, vm_sc, fk_sc, dk_sc, dv_sc, cs_sc, kt_sc, st_sc, dp_sc, pt_sc, ds_sc = refs[12 + 2 * n:23 + 2 * n]
        comm = (refs[7:7 + n], refs[12 + n:12 + 2 * n], to_all, refs[23 + 2 * n:])
        j = pl.program_id(1)

        @pl.when(j == 0)
        def _():
            dqt_ref[...] = jnp.zeros_like(dqt_ref)
            dfq_ref[...] = jnp.zeros_like(dfq_ref)

        @pl.when((pl.program_id(0) == 0) & (j == 0))
        def _():
            for cp in _comm_copies(*comm):
                cp.start()
        lo, hi = _half_masks(tk)
        r = lax.broadcasted_iota(jnp.int32, (tk, tk), 0)
        c = lax.broadcasted_iota(jnp.int32, (tk, tk), 1)
        tri = c >= r
        _fox_prep(km_sc, k_ref, lo, hi)
        _fox_prep(vm_sc, v_ref, lo, hi)
        for h in range(N_HEADS):
            fk_sc[h] = jnp.broadcast_to(fc_ref[:, h:h + 1], (tk, tk))
        for p in range(4):
            kt_sc[p] = k_ref[:, LANES * p:LANES * (p + 1)].astype(F32).T.astype(BF16)
        dk_sc[...] = jnp.zeros_like(dk_sc)
        dv_sc[...] = jnp.zeros_like(dv_sc)
        cs_sc[...] = jnp.zeros_like(cs_sc)

        def block(i, masked):
            sl = pl.ds(pl.multiple_of(i * tk, tk), tk)
            for p in range(4):
                cs = slice(LANES * p, LANES * (p + 1))
                qi = q_ref[sl, cs]
                doi = do_ref[sl, cs]
                for h in (2 * p, 2 * p + 1):
                    st = _nt(km_sc[h], qi) - fk_sc[h] - lr_ref[h:h + 1, sl]
                    st_sc[h] = jnp.where(tri, st, NEG) if masked else st
                    dp_sc[h] = _nt(vm_sc[h], doi) - dr_ref[h:h + 1, sl]
            for h in range(N_HEADS):
                pt = jnp.exp(st_sc[h])
                dst = pt * dp_sc[h]
                pt_sc[h] = pt.astype(BF16)
                ds_sc[h] = dst.astype(BF16)
                cs_sc[h] += dst[:, :LANES] + dst[:, LANES:]
                dfq_ref[h:h + 1, sl] += jnp.sum(dst, axis=0, keepdims=True)
            for p in range(4):
                cs = slice(LANES * p, LANES * (p + 1))
                qi = q_ref[sl, cs]
                doi = do_ref[sl, cs]
                for h in (2 * p, 2 * p + 1):
                    dv_sc[h] += _nn(pt_sc[h], doi)
                    dk_sc[h] += _nn(ds_sc[h], qi)
                    kt = kt_sc[p, HEAD_DIM * (h % 2):HEAD_DIM * (h % 2 + 1), :]
                    dqt_ref[HEAD_DIM * h:HEAD_DIM * (h + 1), sl] += _nn(kt, ds_sc[h])

        def step(i, carry):
            block(i, False)
            return carry

        block(j, True)
        lax.fori_loop(j + 1, nk, step, 0)
        df_ref[...] = jnp.zeros_like(df_ref)
        for p in range(4):
            cs = slice(LANES * p, LANES * (p + 1))
            dk_ref[:, cs] = jnp.where(lo, dk_sc[2 * p], dk_sc[2 * p + 1]).astype(BF16)
            dv_ref[:, cs] = jnp.where(lo, dv_sc[2 * p], dv_sc[2 * p + 1]).astype(BF16)
            for h in (2 * p, 2 * p + 1):
                df_ref[:, h:h + 1] = -jnp.sum(cs_sc[h], axis=1, keepdims=True)

        @pl.when(j == nk - 1)
        def _():
            dqt_ref[...] = dqt_ref[...] * Q_SCALE

        @pl.when((pl.program_id(0) == nbat - 1) & (j == nk - 1))
        def _():
            for cp in _comm_copies(*comm):
                cp.wait()

    tile = lambda w, col: pl.BlockSpec((tk, w), lambda b, j: (b * nk + j, col))
    full = lambda col: pl.BlockSpec((seq, WIDTH), lambda b, j: (b, col))
    row = pl.BlockSpec((N_HEADS, seq), lambda b, j: (0, b))
    acc = pltpu.VMEM((N_HEADS, tk, LANES), F32)
    res = pl.pallas_call(
        body, name="fox_bwd", grid=(nbat, nk),
        in_specs=[tile(WIDTH, 1), tile(WIDTH, 2), full(0), full(0), tile(LANES, 0), row, row] + [ANY] * n,
        out_specs=[tile(WIDTH, 0), tile(WIDTH, 0), tile(LANES, 0), pl.BlockSpec((WIDTH, seq), lambda b, j: (b, 0)),
                   row] + [ANY] * n,
        out_shape=[jax.ShapeDtypeStruct((t_all, WIDTH), BF16), jax.ShapeDtypeStruct((t_all, WIDTH), BF16),
                   jax.ShapeDtypeStruct((t_all, LANES), F32), jax.ShapeDtypeStruct((nbat * WIDTH, seq), F32),
                   jax.ShapeDtypeStruct((N_HEADS, t_all), F32)] + _comm_out_shapes(grads, to_all),
        scratch_shapes=[pltpu.VMEM((N_HEADS, tk, LANES), BF16), pltpu.VMEM((N_HEADS, tk, LANES), BF16),
                        pltpu.VMEM((N_HEADS, tk, tk), F32), acc, acc, acc, pltpu.VMEM((4, LANES, tk), BF16),
                        pltpu.VMEM((N_HEADS, tk, tk), F32), pltpu.VMEM((N_HEADS, tk, tk), F32),
                        pltpu.VMEM((N_HEADS, tk, tk), BF16), pltpu.VMEM((N_HEADS, tk, tk), BF16)]
        + _comm_sems(n),
        compiler_params=_params(),
    )(za, za, za, do, f_col, lse_row, dl_row, *grads)
    return res[0], res[1], res[2], res[3], res[4], res[5:]


DIL_SUB = 4


def _dil_mask(has_prev):
    qi = lax.broadcasted_iota(jnp.int32, (BLK, 2 * BLK), 0)
    kj = lax.broadcasted_iota(jnp.int32, (BLK, 2 * BLK), 1)
    dist = qi + BLK - kj
    band = (dist >= 0) & (dist <= BLK)
    return band if has_prev is True else band & ((kj >= BLK) | has_prev)


def _dil_geometry(t_all, seq, d, max_sub=DIL_SUB):
    length = seq // d
    nbs = length // BLK
    sub = min(max_sub, nbs)
    spb = nbs // sub
    tile = lambda width, col: pl.BlockSpec((BLK * sub, width), lambda s: (s, col))
    whole = lambda width, col: pl.BlockSpec((length, width), lambda s: (s // spb, col))
    return nbs, sub, spb, t_all // (BLK * sub), tile, whole


def _blk(i):
    return pl.ds(pl.multiple_of(i * BLK, BLK), BLK)


def _dil_fwd(zb, seq, d):
    t_all = zb.shape[0]
    nbs, sub, spb, steps, tile, whole = _dil_geometry(t_all, seq, d)

    def body(q_ref, k_ref, v_ref, o_ref, lse_ref, s_sc, p_sc):
        first = (pl.program_id(0) % spb) * sub
        lo, hi = _half_masks(BLK)
        lse_ref[...] = jnp.zeros_like(lse_ref)
        for j in range(sub):
            blk = first + j
            mask = _dil_mask(blk != 0 if j == 0 else True)
            for p in range(4):
                cs = slice(LANES * p, LANES * (p + 1))
                qp = q_ref[BLK * j:BLK * (j + 1), cs]
                kcat = jnp.concatenate([k_ref[_blk(jnp.maximum(blk - 1, 0)), cs], k_ref[_blk(blk), cs]], axis=0)
                for e in (0, 1):
                    qe = jnp.where(lo if e == 0 else hi, qp, jnp.zeros_like(qp))
                    s_sc[N_HEADS * j + 2 * p + e] = jnp.where(mask, _nt(qe, kcat), NEG)
        inv = []
        for i in range(N_HEADS * sub):
            s = s_sc[i]
            m = jnp.max(s, axis=1, keepdims=True)
            pe = jnp.exp(s - m)
            l = jnp.sum(pe, axis=1, keepdims=True)
            p_sc[i] = pe.astype(BF16)
            inv.append(1.0 / l)
            j, h = divmod(i, N_HEADS)
            lse_ref[BLK * j:BLK * (j + 1), h:h + 1] = m + jnp.log(l)
        for j in range(sub):
            blk = first + j
            for p in range(4):
                cs = slice(LANES * p, LANES * (p + 1))
                vcat = jnp.concatenate([v_ref[_blk(jnp.maximum(blk - 1, 0)), cs], v_ref[_blk(blk), cs]], axis=0)
                res = [_nn(p_sc[N_HEADS * j + h], vcat) * inv[N_HEADS * j + h] for h in (2 * p, 2 * p + 1)]
                o_ref[BLK * j:BLK * (j + 1), cs] = jnp.where(lo, res[0], res[1])

    return pl.pallas_call(
        body, name=f"dil_fwd_{d}", grid=(steps,), in_specs=[tile(WIDTH, 0), whole(WIDTH, 1), whole(WIDTH, 2)],
        out_specs=[tile(WIDTH, 0), tile(LANES, 0)],
        out_shape=[jax.ShapeDtypeStruct((t_all, WIDTH), F32), jax.ShapeDtypeStruct((t_all, LANES), F32)],
        scratch_shapes=[pltpu.VMEM((N_HEADS * sub, BLK, 2 * BLK), F32),
                        pltpu.VMEM((N_HEADS * sub, BLK, 2 * BLK), BF16)],
        compiler_params=_params(),
    )(zb, zb, zb)


def _dil_bwd(zb, do, lse, dl, seq, d):
    t_all = zb.shape[0]
    length = seq // d
    nbs, sub, spb, steps, tile, whole = _dil_geometry(t_all, seq, d, 2 if length >= 4096 else DIL_SUB)

    def body(k_ref, v_ref, q_ref, do_ref, lse_ref, dl_ref, dq_ref, dk_ref, dv_ref, s_sc, dp_sc, pt_sc, ds_sc, kt_sc,
             dqt_sc):
        step = pl.program_id(0) % spb
        first = step * sub

        @pl.when(step == 0)
        def _():
            dqt_sc[...] = jnp.zeros_like(dqt_sc)
        r = lax.broadcasted_iota(jnp.int32, (BLK, 2 * BLK), 0)
        c = lax.broadcasted_iota(jnp.int32, (BLK, 2 * BLK), 1)
        same = (c < BLK) & (c >= r)
        later = (c >= BLK) & (c - BLK <= r)
        lo, hi = _half_masks(BLK)
        for j in range(sub):
            blk = first + j
            rows = slice(BLK * j, BLK * (j + 1))
            nxt = _blk(jnp.minimum(blk + 1, nbs - 1))
            mask = same | (later & (blk + 1 != nbs)) if j == sub - 1 else same | later
            lrows = jnp.concatenate([lse_ref[_blk(blk), :].T, lse_ref[nxt, :].T], axis=1)
            erows = jnp.concatenate([dl_ref[_blk(blk), :].T, dl_ref[nxt, :].T], axis=1)
            for p in range(4):
                cs = slice(LANES * p, LANES * (p + 1))
                kp = k_ref[rows, cs]
                vp = v_ref[rows, cs]
                kt_sc[4 * j + p] = kp.astype(F32).T.astype(BF16)
                qcat = jnp.concatenate([q_ref[_blk(blk), cs], q_ref[nxt, cs]], axis=0)
                dcat = jnp.concatenate([do_ref[_blk(blk), cs], do_ref[nxt, cs]], axis=0)
                for e in (0, 1):
                    h = 2 * p + e
                    sel = lo if e == 0 else hi
                    ke = jnp.where(sel, kp, jnp.zeros_like(kp))
                    ve = jnp.where(sel, vp, jnp.zeros_like(vp))
                    s_sc[N_HEADS * j + h] = jnp.where(mask, _nt(ke, qcat) - lrows[h:h + 1, :], NEG)
                    dp_sc[N_HEADS * j + h] = _nt(ve, dcat) - erows[h:h + 1, :]
        for i in range(N_HEADS * sub):
            pt = jnp.exp(s_sc[i])
            pt_sc[i] = pt.astype(BF16)
            ds_sc[i] = (pt * dp_sc[i]).astype(BF16)
        for j in range(sub):
            blk = first + j
            rows = slice(BLK * j, BLK * (j + 1))
            nxt = _blk(jnp.minimum(blk + 1, nbs - 1))
            cols = pl.ds(pl.multiple_of(blk * BLK, BLK), 2 * BLK)
            for p in range(4):
                cs = slice(LANES * p, LANES * (p + 1))
                qcat = jnp.concatenate([q_ref[_blk(blk), cs], q_ref[nxt, cs]], axis=0)
                dcat = jnp.concatenate([do_ref[_blk(blk), cs], do_ref[nxt, cs]], axis=0)
                i = N_HEADS * j + 2 * p
                dk_ref[rows, cs] = jnp.where(lo, _nn(ds_sc[i], qcat), _nn(ds_sc[i + 1], qcat)).astype(BF16)
                dv_ref[rows, cs] = jnp.where(lo, _nn(pt_sc[i], dcat), _nn(pt_sc[i + 1], dcat)).astype(BF16)
                for e in (0, 1):
                    kt = kt_sc[4 * j + p, HEAD_DIM * e:HEAD_DIM * (e + 1), :]
                    dqt_sc[HEAD_DIM * (2 * p + e):HEAD_DIM * (2 * p + e + 1), cols] += _nn(kt, ds_sc[i + e])

        @pl.when(step == spb - 1)
        def _():
            for p in range(4):
                cs = slice(LANES * p, LANES * (p + 1))
                dq_ref[:, cs] = (dqt_sc[cs, 0:length].T * Q_SCALE).astype(BF16)

    wide = pltpu.VMEM((N_HEADS * sub, BLK, 2 * BLK), F32)
    half = pltpu.VMEM((N_HEADS * sub, BLK, 2 * BLK), BF16)
    return pl.pallas_call(
        body, name=f"dil_bwd_{d}", grid=(steps,),
        in_specs=[tile(WIDTH, 1), tile(WIDTH, 2), whole(WIDTH, 0), whole(WIDTH, 0), whole(LANES, 0), whole(LANES, 0)],
        out_specs=[whole(WIDTH, 0), tile(WIDTH, 0), tile(WIDTH, 0)],
        out_shape=[jax.ShapeDtypeStruct((t_all, WIDTH), BF16)] * 3,
        scratch_shapes=[wide, wide, half, half, pltpu.VMEM((4 * sub, LANES, BLK), BF16),
                        pltpu.VMEM((WIDTH, length + BLK), F32)],
        compiler_params=_params(),
    )(zb, zb, zb, do, lse, dl)


def _mix_out(oa, o3, l3, gn_a, gn_b, w_out, x, ada3, ln_g, ln_b, perms, seq):
    t_all = x.shape[0]
    tm = TOK_TM
    nts = seq // tm

    def body(oa_ref, o1_ref, o2_ref, o3_ref, l1_ref, l2_ref, l3_ref, ga_ref, gb_ref, w_ref, x_ref, ada_ref, g_ref,
             b_ref, p4_ref, p16_ref, pt4_ref, pt16_ref, ob_ref, lse_ref, lse4_ref, lse16_ref, mg_ref, mix_ref, xh_ref,
             rs_ref, h2_ref, h2t_ref):
        e, et = _head_mats()
        la = l1_ref[...]
        lb = _permute_f32(pt4_ref[...], _load_classes(l2_ref, 4))
        lc = _permute_f32(pt16_ref[...], _load_classes(l3_ref, 16))
        mx = jnp.maximum(jnp.maximum(la, lb), lc)
        ea, eb, ec = jnp.exp(la - mx), jnp.exp(lb - mx), jnp.exp(lc - mx)
        tot = ea + eb + ec
        lse = mx + jnp.log(tot)
        lse_ref[...] = lse
        _store_classes(lse4_ref, _permute_f32(p4_ref[...], lse), 4)
        _store_classes(lse16_ref, _permute_f32(p16_ref[...], lse), 16)
        ob = (o1_ref[...] * _hexp(ea / tot, e)
              + _permute_f32(pt4_ref[...], _load_classes(o2_ref, 4)) * _hexp(eb / tot, e)
              + _permute_f32(pt16_ref[...], _load_classes(o3_ref, 16)) * _hexp(ec / tot, e))
        ob_ref[...] = ob

        def rms(o, gain):
            rr = lax.rsqrt(_hsum(o * o, et) * (1.0 / HEAD_DIM) + RMS_EPS)
            return o * _hexp(rr, e) * gain

        merged = jnp.concatenate([rms(oa_ref[...], ga_ref[...]), rms(ob, gb_ref[...])], axis=1).astype(BF16)
        mg_ref[...] = merged
        mix = _nn(merged, w_ref[...])
        mix_ref[...] = mix.astype(BF16)
        r1 = ALPHA * x_ref[...] + ada_ref[0, 2:3, :] * mix
        d = r1 - jnp.mean(r1, axis=1, keepdims=True)
        rstd = lax.rsqrt(jnp.mean(d * d, axis=1, keepdims=True) + LN_EPS)
        xh = d * rstd
        xh_ref[...] = xh
        rs_ref[...] = jnp.broadcast_to(rstd, (tm, LANES))
        x1 = xh * g_ref[...] + b_ref[...]
        h2 = x1 * (1.0 + ada_ref[0, 4:5, :]) + ada_ref[0, 3:4, :]
        h2_ref[...] = h2.astype(BF16)
        h2t_ref[...] = h2.T.astype(BF16)

    tok = lambda w: pl.BlockSpec((tm, w), lambda i: (i, 0))
    vec = lambda w: pl.BlockSpec((1, w), lambda i: (0, 0))
    whole = lambda a: pl.BlockSpec(a.shape, lambda i: (0, 0))
    classes = lambda a, d: a.reshape(t_all // seq * d, seq // d, a.shape[-1])
    return pl.pallas_call(
        body, name="mix_out", grid=(t_all // tm,),
        in_specs=[tok(WIDTH), tok(WIDTH), _class_spec(4, WIDTH, nts), _class_spec(16, WIDTH, nts), tok(LANES),
                  _class_spec(4, LANES, nts), _class_spec(16, LANES, nts), vec(WIDTH), vec(WIDTH), whole(w_out),
                  tok(D_MODEL), pl.BlockSpec((1, 6, D_MODEL), lambda i: (i // nts, 0, 0)), vec(D_MODEL), vec(D_MODEL)]
        + [whole(p) for p in perms],
        out_specs=[tok(WIDTH), tok(LANES), _class_spec(4, LANES, nts), _class_spec(16, LANES, nts), tok(D_MODEL),
                   tok(D_MODEL), tok(D_MODEL), tok(LANES), tok(D_MODEL), pl.BlockSpec((D_MODEL, tm), lambda i: (0, i))],
        out_shape=[jax.ShapeDtypeStruct((t_all, WIDTH), F32), jax.ShapeDtypeStruct((t_all, LANES), F32),
                   _class_shape(t_all, seq, 4, LANES, F32), _class_shape(t_all, seq, 16, LANES, F32),
                   jax.ShapeDtypeStruct((t_all, D_MODEL), BF16), jax.ShapeDtypeStruct((t_all, D_MODEL), BF16),
                   jax.ShapeDtypeStruct((t_all, D_MODEL), F32), jax.ShapeDtypeStruct((t_all, LANES), F32),
                   jax.ShapeDtypeStruct((t_all, D_MODEL), BF16), jax.ShapeDtypeStruct((D_MODEL, t_all), BF16)],
        compiler_params=_params(),
    )(oa, o3[0], classes(o3[1], 4), classes(o3[2], 16), l3[0], classes(l3[1], 4), classes(l3[2], 16), gn_a, gn_b,
      w_out, x, ada3, ln_g, ln_b, *perms)


def _mix_out_bwd(dmix, w_out, oa, ob, gn_a, gn_b, perms, seq):
    t_all = dmix.shape[0]
    tm = TOK_TM
    nts = seq // tm

    def body(dm_ref, w_ref, oa_ref, ob_ref, ga_ref, gb_ref, p4_ref, p16_ref, doa_ref, dob_ref, dob4_ref, dob16_ref,
             dla_ref, dlb_ref, dlb4_ref, dlb16_ref, acc_ref):
        @pl.when(pl.program_id(0) == 0)
        def _():
            acc_ref[...] = jnp.zeros_like(acc_ref)
        e, et = _head_mats()
        dmg = _nt(dm_ref[...], w_ref[...])

        def group(o, dn, gain):
            rr = lax.rsqrt(_hsum(o * o, et) * (1.0 / HEAD_DIM) + RMS_EPS)
            re = _hexp(rr, e)
            dgain = jnp.sum(dn * o * re, axis=0, keepdims=True)
            dxn = dn * gain
            tt = _hsum(dxn * o, et) * (rr * rr * rr) * (1.0 / HEAD_DIM)
            do = re * dxn - o * _hexp(tt, e)
            return do, _hsum(do * o, et), dgain

        doa, dla, dga = group(oa_ref[...], dmg[:, :WIDTH], ga_ref[...])
        dob, dlb, dgb = group(ob_ref[...], dmg[:, WIDTH:], gb_ref[...])
        dob = dob.astype(BF16)
        doa_ref[...] = doa.astype(BF16)
        dob_ref[...] = dob
        _store_classes(dob4_ref, _nn(p4_ref[...], dob).astype(BF16), 4)
        _store_classes(dob16_ref, _nn(p16_ref[...], dob).astype(BF16), 16)
        dla_ref[...] = dla
        dlb_ref[...] = dlb
        _store_classes(dlb4_ref, _permute_f32(p4_ref[...], dlb), 4)
        _store_classes(dlb16_ref, _permute_f32(p16_ref[...], dlb), 16)
        acc_ref[0:1, :] += jnp.concatenate([dga, dgb], axis=1)

    tok = lambda w: pl.BlockSpec((tm, w), lambda i: (i, 0))
    vec = lambda w: pl.BlockSpec((1, w), lambda i: (0, 0))
    return pl.pallas_call(
        body, name="mix_out_bwd", grid=(t_all // tm,),
        in_specs=[tok(D_MODEL), pl.BlockSpec(w_out.shape, lambda i: (0, 0)), tok(WIDTH), tok(WIDTH), vec(WIDTH),
                  vec(WIDTH), pl.BlockSpec(perms[0].shape, lambda i: (0, 0)),
                  pl.BlockSpec(perms[1].shape, lambda i: (0, 0))],
        out_specs=[tok(WIDTH), tok(WIDTH), _class_spec(4, WIDTH, nts), _class_spec(16, WIDTH, nts), tok(LANES),
                   tok(LANES), _class_spec(4, LANES, nts), _class_spec(16, LANES, nts),
                   pl.BlockSpec((8, D_MODEL), lambda i: (0, 0))],
        out_shape=[jax.ShapeDtypeStruct((t_all, WIDTH), BF16), jax.ShapeDtypeStruct((t_all, WIDTH), BF16),
                   _class_shape(t_all, seq, 4, WIDTH, BF16), _class_shape(t_all, seq, 16, WIDTH, BF16),
                   jax.ShapeDtypeStruct((t_all, LANES), F32), jax.ShapeDtypeStruct((t_all, LANES), F32),
                   _class_shape(t_all, seq, 4, LANES, F32), _class_shape(t_all, seq, 16, LANES, F32),
                   jax.ShapeDtypeStruct((8, D_MODEL), F32)],
        compiler_params=_params(),
    )(dmix, w_out, oa, ob, gn_a, gn_b, perms[0], perms[1])


def _inproj_bwd(dqt, dka, dva, dil1, dil4, dil16, dfa16, pos, wqkv, wf16, freq, perms, dr1, x, ada3, seq):
    t_all = x.shape[0]
    tm = TOK_TM
    nts = seq // tm

    def body(dqt_ref, dka_ref, dva_ref, q1_ref, k1_ref, v1_ref, q4_ref, k4_ref, v4_ref, q16_ref, k16_ref, v16_ref,
             dfa_ref, pos_ref, w_ref, wf_ref, fr_ref, pt4_ref, pt16_ref, dr1_ref, x_ref, ada_ref, gx_ref, dz_ref,
             acc_ref):
        i = pl.program_id(0)

        @pl.when(i == 0)
        def _():
            acc_ref[...] = jnp.zeros_like(acc_ref)
        tabs = _rope_tabs(pos_ref, fr_ref, -1.0)
        dz_ref[:, :WIDTH] = dqt_ref[...].T.astype(BF16)
        dz_ref[:, WIDTH:2 * WIDTH] = dka_ref[...]
        dz_ref[:, 2 * WIDTH:3 * WIDTH] = dva_ref[...]
        for t, (n1, n4, n16) in enumerate(((q1_ref, q4_ref, q16_ref), (k1_ref, k4_ref, k16_ref),
                                           (v1_ref, v4_ref, v16_ref))):
            tot = (n1[...].astype(F32) + _nn(pt4_ref[...], _load_classes(n4, 4))
                   + _nn(pt16_ref[...], _load_classes(n16, 16)))
            if t < 2:
                tot = _rope(tot, tabs)
            dz_ref[:, (3 + t) * WIDTH:(4 + t) * WIDTH] = tot.astype(BF16)
        dh1 = _tn(dfa_ref[...], wf_ref[...])
        for n in range(6):
            cs = slice(n * WIDTH, (n + 1) * WIDTH)
            dh1 = dh1 + _nt(dz_ref[:, cs], w_ref[:, cs])
        xv = x_ref[...]
        gx_ref[...] = ALPHA * dr1_ref[...] + dh1 * (1.0 + ada_ref[0, 1:2, :])
        b = i // nts
        acc_ref[pl.ds(b, 1), :] += jnp.sum(dh1 * xv, axis=0, keepdims=True)
        acc_ref[pl.ds(8 + b, 1), :] += jnp.sum(dh1, axis=0, keepdims=True)

    tok = lambda w: pl.BlockSpec((tm, w), lambda i: (i, 0))
    whole = lambda a: pl.BlockSpec(a.shape, lambda i: (0, 0))
    classes = lambda a, d: a.reshape(t_all // seq * d, seq // d, a.shape[-1])
    return pl.pallas_call(
        body, name="inproj_bwd", grid=(t_all // tm,),
        in_specs=[pl.BlockSpec((WIDTH, tm), lambda i: (i // nts, i % nts)), tok(WIDTH), tok(WIDTH)]
        + [tok(WIDTH)] * 3 + [_class_spec(4, WIDTH, nts)] * 3 + [_class_spec(16, WIDTH, nts)] * 3
        + [pl.BlockSpec((16, tm), lambda i: (0, i)), tok(1), whole(wqkv), whole(wf16),
           pl.BlockSpec((1, LANES), lambda i: (0, 0)), whole(perms[2]), whole(perms[3]), tok(D_MODEL), tok(D_MODEL),
           pl.BlockSpec((1, 6, D_MODEL), lambda i: (i // nts, 0, 0))],
        out_specs=[tok(D_MODEL), tok(6 * WIDTH), pl.BlockSpec((16, D_MODEL), lambda i: (0, 0))],
        out_shape=[jax.ShapeDtypeStruct((t_all, D_MODEL), F32), jax.ShapeDtypeStruct((t_all, 6 * WIDTH), BF16),
                   jax.ShapeDtypeStruct((16, D_MODEL), F32)],
        compiler_params=_params(),
    )(dqt, dka, dva, *dil1, *[classes(a, 4) for a in dil4], *[classes(a, 16) for a in dil16], dfa16, pos, wqkv, wf16,
      freq, perms[2], perms[3], dr1, x, ada3)


FFN_TM = 1024
FFN_TN = 256
HALO = 8


FFN_CHUNK = 64


def _conv(cat_ref, w_ref, b_ref, start, rows, halo=HALO):
    return (b_ref[...] + w_ref[0:1, :] * cat_ref[pl.ds(start + halo - 2, rows), :]
            + w_ref[1:2, :] * cat_ref[pl.ds(start + halo - 1, rows), :]
            + w_ref[2:3, :] * cat_ref[pl.ds(start + halo, rows), :])


def _ffn_up_gate(h2, w_up, conv_w, conv_b, seq):
    t_all = h2.shape[0]
    tm, tn = FFN_TM, FFN_TN
    nc = D_FF // tn
    nts = seq // tm
    pre = 16

    def body(h_ref, hp_ref, wua_ref, wug_ref, wa_ref, wg_ref, ba_ref, bg_ref, ua_ref, ug_ref, o_ref, ca_ref, cg_ref):
        first = (pl.program_id(1) % nts) == 0
        hcat = jnp.concatenate([hp_ref[...], h_ref[...]], axis=0)
        zero = jnp.zeros((pre, tn), F32)
        for w_ref, cat, u_ref in ((wua_ref, ca_ref, ua_ref), (wug_ref, cg_ref, ug_ref)):
            ub = _nn(hcat, w_ref[...]).astype(BF16)
            ue = ub.astype(F32)
            cat[0:pre, :] = jnp.where(first, zero, ue[0:pre])
            cat[pre:, :] = ue[pre:]
            u_ref[...] = ub[pre:]
        for c0 in range(0, tm, FFN_CHUNK):
            ya = _conv(ca_ref, wa_ref, ba_ref, c0, FFN_CHUNK, pre)
            yg = _conv(cg_ref, wg_ref, bg_ref, c0, FFN_CHUNK, pre)
            o_ref[c0:c0 + FFN_CHUNK, :] = (yg * jax.nn.sigmoid(yg) * ya).astype(BF16)

    vec = lambda r, off: pl.BlockSpec((r, tn), lambda n, t: (0, n + off))
    wcol = lambda off: pl.BlockSpec((D_MODEL, tn), lambda n, t: (0, n + off))
    tile = pl.BlockSpec((tm, tn), lambda n, t: (t, n))
    return pl.pallas_call(
        body, name="ffn_up_gate", grid=(nc, t_all // tm),
        in_specs=[pl.BlockSpec((tm, D_MODEL), lambda n, t: (t, 0)),
                  pl.BlockSpec((pre, D_MODEL), lambda n, t: (jnp.maximum(t * (tm // pre) - 1, 0), 0)),
                  wcol(0), wcol(nc), vec(3, 0), vec(3, nc), vec(1, 0), vec(1, nc)],
        out_specs=[tile, tile, tile],
        out_shape=[jax.ShapeDtypeStruct((t_all, D_FF), BF16)] * 3,
        scratch_shapes=[pltpu.VMEM((tm + pre, tn), F32)] * 2, compiler_params=_params(),
    )(h2, h2, w_up, w_up, conv_w, conv_w, conv_b, conv_b)


def _ffn_gate_bwd(u_a, u_g, dfi, conv_w, conv_b, h2t, seq):
    t_all = u_a.shape[0]
    tm, tn = FFN_TM, FFN_TN
    nc = D_FF // tn
    nts = seq // tm

    def body(ua_ref, uap_ref, uan_ref, ug_ref, ugp_ref, ugn_ref, df_ref, dfn_ref, wa_ref, wg_ref, ba_ref, bg_ref, h_ref,
             dua_ref, dug_ref, acca_ref, accg_ref, dwa_ref, dwg_ref, ca_ref, cg_ref, ya_ref, yg_ref):
        t = pl.program_id(1)
        first = (t % nts) == 0
        last = (t % nts) == nts - 1

        @pl.when(t == 0)
        def _():
            acca_ref[...] = jnp.zeros_like(acca_ref)
            accg_ref[...] = jnp.zeros_like(accg_ref)
            dwa_ref[...] = jnp.zeros_like(dwa_ref)
            dwg_ref[...] = jnp.zeros_like(dwg_ref)
        zero = jnp.zeros((HALO, tn), F32)
        for cat, cur, prv, nxt in ((ca_ref, ua_ref, uap_ref, uan_ref), (cg_ref, ug_ref, ugp_ref, ugn_ref)):
            cat[0:HALO, :] = jnp.where(first, zero, prv[...].astype(F32)[HALO:])
            cat[HALO:HALO + tm, :] = cur[...].astype(F32)
            cat[HALO + tm:, :] = nxt[...].astype(F32)[:HALO]
        ch = FFN_CHUNK
        sums = [[jnp.zeros((1, tn), F32) for _ in range(4)] for _ in range(2)]
        for ci, c0 in enumerate(range(0, tm, ch)):
            ya = _conv(ca_ref, wa_ref, ba_ref, c0, ch + HALO)
            yg = _conv(cg_ref, wg_ref, bg_ref, c0, ch + HALO)
            if c0 + ch < tm:
                beyond = df_ref[c0 + ch:c0 + ch + 16, :].astype(F32)[:HALO]
            else:
                beyond = jnp.where(last, 0.0, dfn_ref[...].astype(F32)[:HALO])
            dfe = jnp.concatenate([df_ref[c0:c0 + ch, :].astype(F32), beyond], axis=0)
            sg = jax.nn.sigmoid(yg)
            ya_ref[ci] = dfe * (yg * sg)
            yg_ref[ci] = dfe * ya * (sg * (1.0 + yg * (1.0 - sg)))
            for half, (dy, cat, w_ref, du_ref) in enumerate(((ya_ref, ca_ref, wa_ref, dua_ref),
                                                             (yg_ref, cg_ref, wg_ref, dug_ref))):
                d0 = dy[ci, 0:ch, :]
                du = (w_ref[2:3, :] * d0 + w_ref[1:2, :] * dy[ci, pl.ds(1, ch), :]
                      + w_ref[0:1, :] * dy[ci, pl.ds(2, ch), :])
                du_ref[c0:c0 + ch, :] = du.astype(BF16)
                for k in range(3):
                    sums[half][k] += jnp.sum(d0 * cat[pl.ds(c0 + HALO - 2 + k, ch), :], axis=0, keepdims=True)
                sums[half][3] += jnp.sum(d0, axis=0, keepdims=True)
        for half, acc in enumerate((acca_ref, accg_ref)):
            for k in range(4):
                acc[k:k + 1, :] += sums[half][k]
        ht = h_ref[...]
        dwa_ref[...] += _nn(ht, dua_ref[...])
        dwg_ref[...] += _nn(ht, dug_ref[...])

    nrow = t_all // 16
    cur = pl.BlockSpec((tm, tn), lambda n, t: (t, n))
    prev = pl.BlockSpec((16, tn), lambda n, t: (jnp.maximum(t * (tm // 16) - 1, 0), n))
    nxt = pl.BlockSpec((16, tn), lambda n, t: (jnp.minimum((t + 1) * (tm // 16), nrow - 1), n))
    vec = lambda r, off: pl.BlockSpec((r, tn), lambda n, t: (0, n + off))
    acc = pl.BlockSpec((8, tn), lambda n, t: (0, n))
    dw = pl.BlockSpec((D_MODEL, tn), lambda n, t: (0, n))
    return pl.pallas_call(
        body, name="ffn_gate_bwd", grid=(nc, t_all // tm),
        in_specs=[cur, prev, nxt, cur, prev, nxt, cur, nxt, vec(3, 0), vec(3, nc), vec(1, 0), vec(1, nc),
                  pl.BlockSpec((D_MODEL, tm), lambda n, t: (0, t))],
        out_specs=[cur, cur, acc, acc, dw, dw],
        out_shape=[jax.ShapeDtypeStruct((t_all, D_FF), BF16), jax.ShapeDtypeStruct((t_all, D_FF), BF16),
                   jax.ShapeDtypeStruct((8, D_FF), F32), jax.ShapeDtypeStruct((8, D_FF), F32),
                   jax.ShapeDtypeStruct((D_MODEL, D_FF), F32), jax.ShapeDtypeStruct((D_MODEL, D_FF), F32)],
        scratch_shapes=[pltpu.VMEM((tm + 2 * HALO, tn), F32)] * 2
        + [pltpu.VMEM((tm // FFN_CHUNK, FFN_CHUNK + HALO, tn), F32)] * 2,
        compiler_params=_params(),
    )(u_a, u_a, u_a, u_g, u_g, u_g, dfi, dfi, conv_w, conv_w, conv_b, conv_b, h2t)


def _ffn_down(ffn_in, w_down, xh1, ln1_g, ln1_b, ada3, ln2_g, ln2_b, target, seq):
    t_all = xh1.shape[0]
    tm = 256
    nts = seq // tm

    def body(f_ref, w_ref, xh_ref, g1_ref, b1_ref, ada_ref, g2_ref, b2_ref, tg_ref, dr2_ref, acc_ref):
        i = pl.program_id(0)

        @pl.when(i == 0)
        def _():
            acc_ref[...] = jnp.zeros_like(acc_ref)
        ffn = _nn(f_ref[...], w_ref[...])
        x1 = xh_ref[...] * g1_ref[...] + b1_ref[...]
        r2 = ALPHA * x1 + ada_ref[0, 5:6, :] * ffn
        d = r2 - jnp.mean(r2, axis=1, keepdims=True)
        rstd = lax.rsqrt(jnp.mean(d * d, axis=1, keepdims=True) + LN_EPS)
        xh2 = d * rstd
        diff = xh2 * g2_ref[...] + b2_ref[...] - tg_ref[...]
        dy = diff * (1.0 / D_MODEL)
        dr2 = _layer_norm_bwd(dy * g2_ref[...], xh2, rstd)
        dr2_ref[...] = dr2
        acc_ref[0:1, :] += jnp.sum(dy * xh2, axis=0, keepdims=True)
        acc_ref[1:2, :] += jnp.sum(dy, axis=0, keepdims=True)
        acc_ref[2:3, :] += jnp.sum(diff * diff, axis=0, keepdims=True) * (0.5 / D_MODEL)
        acc_ref[pl.ds(8 + i // nts, 1), :] += jnp.sum(dr2 * ffn, axis=0, keepdims=True)

    tok = lambda w: pl.BlockSpec((tm, w), lambda i: (i, 0))
    vec = pl.BlockSpec((1, D_MODEL), lambda i: (0, 0))
    return pl.pallas_call(
        body, name="ffn_down", grid=(t_all // tm,),
        in_specs=[tok(D_FF), pl.BlockSpec(w_down.shape, lambda i: (0, 0)), tok(D_MODEL), vec, vec,
                  pl.BlockSpec((1, 6, D_MODEL), lambda i: (i // nts, 0, 0)), vec, vec, tok(D_MODEL)],
        out_specs=[tok(D_MODEL), pl.BlockSpec((16, D_MODEL), lambda i: (0, 0))],
        out_shape=[jax.ShapeDtypeStruct((t_all, D_MODEL), F32), jax.ShapeDtypeStruct((16, D_MODEL), F32)],
        compiler_params=_params(),
    )(ffn_in, w_down, xh1, ln1_g, ln1_b, ada3, ln2_g, ln2_b, target)


def _ffn_down_bwd(dr2, ada3, w_down, seq):
    t_all = dr2.shape[0]
    tm = 256
    nts = seq // tm

    def body(d_ref, ada_ref, w_ref, dffn_ref, dfi_ref):
        dffn = (d_ref[...] * ada_ref[0, 5:6, :]).astype(BF16)
        dffn_ref[...] = dffn
        dfi_ref[...] = _nt(dffn, w_ref[...]).astype(BF16)

    tok = lambda w: pl.BlockSpec((tm, w), lambda i: (i, 0))
    return pl.pallas_call(
        body, name="ffn_down_bwd", grid=(t_all // tm,),
        in_specs=[tok(D_MODEL), pl.BlockSpec((1, 6, D_MODEL), lambda i: (i // nts, 0, 0)),
                  pl.BlockSpec(w_down.shape, lambda i: (0, 0))],
        out_specs=[tok(D_MODEL), tok(D_FF)],
        out_shape=[jax.ShapeDtypeStruct((t_all, D_MODEL), BF16), jax.ShapeDtypeStruct((t_all, D_FF), BF16)],
        compiler_params=_params(),
    )(dr2, ada3, w_down)


def _ffn_up_bwd(du_a, du_g, w_up, dr2, xh1, rs1, mix, ada3, ln1_g, ln1_b, seq):
    t_all = dr2.shape[0]
    tm = 256
    nts = seq // tm

    def body(da_ref, dg_ref, w_ref, dr2_ref, xh_ref, rs_ref, mix_ref, ada_ref, g_ref, b_ref, dr1_ref, dmix_ref,
             acc_ref):
        i = pl.program_id(0)

        @pl.when(i == 0)
        def _():
            acc_ref[...] = jnp.zeros_like(acc_ref)
        dh2 = _nt(da_ref[...], w_ref[:, :D_FF]) + _nt(dg_ref[...], w_ref[:, D_FF:])
        xh = xh_ref[...]
        x1 = xh * g_ref[...] + b_ref[...]
        dx1 = ALPHA * dr2_ref[...] + dh2 * (1.0 + ada_ref[0, 4:5, :])
        dr1 = _layer_norm_bwd(dx1 * g_ref[...], xh, rs_ref[:, 0:1])
        dr1_ref[...] = dr1
        dmix_ref[...] = (dr1 * ada_ref[0, 2:3, :]).astype(BF16)
        b = i // nts
        acc_ref[0:1, :] += jnp.sum(dx1 * xh, axis=0, keepdims=True)
        acc_ref[1:2, :] += jnp.sum(dx1, axis=0, keepdims=True)
        acc_ref[pl.ds(8 + b, 1), :] += jnp.sum(dh2 * x1, axis=0, keepdims=True)
        acc_ref[pl.ds(16 + b, 1), :] += jnp.sum(dh2, axis=0, keepdims=True)
        acc_ref[pl.ds(24 + b, 1), :] += jnp.sum(dr1 * mix_ref[...].astype(F32), axis=0, keepdims=True)

    tok = lambda w: pl.BlockSpec((tm, w), lambda i: (i, 0))
    vec = pl.BlockSpec((1, D_MODEL), lambda i: (0, 0))
    return pl.pallas_call(
        body, name="ffn_up_bwd", grid=(t_all // tm,),
        in_specs=[tok(D_FF), tok(D_FF), pl.BlockSpec(w_up.shape, lambda i: (0, 0)), tok(D_MODEL), tok(D_MODEL),
                  tok(LANES), tok(D_MODEL), pl.BlockSpec((1, 6, D_MODEL), lambda i: (i // nts, 0, 0)), vec, vec],
        out_specs=[tok(D_MODEL), tok(D_MODEL), pl.BlockSpec((32, D_MODEL), lambda i: (0, 0))],
        out_shape=[jax.ShapeDtypeStruct((t_all, D_MODEL), F32), jax.ShapeDtypeStruct((t_all, D_MODEL), BF16),
                   jax.ShapeDtypeStruct((32, D_MODEL), F32)],
        compiler_params=_params(),
    )(du_a, du_g, w_up, dr2, xh1, rs1, mix, ada3, ln1_g, ln1_b)


def _rows(a):
    return a[:, :N_HEADS].T


def _rope_freq():
    f = np.float32(ROPE_THETA) ** (-np.arange(0, ROPE_DIMS, 2, dtype=np.float32) / np.float32(ROPE_DIMS))
    return jnp.asarray(np.tile(f.astype(np.float32), LANES // (ROPE_DIMS // 2))[None, :])


def _local_step(x, positions, target, ada3, w_in, b_fgate, gn_a, gn_b, ln1_g, ln1_b, conv_b, ln2_g, ln2_b,
                late_shards):
    nbat, seq, _ = x.shape
    t_all = nbat * seq
    xf = x.reshape(t_all, D_MODEL)
    tg = target.reshape(t_all, D_MODEL)
    pos = positions.reshape(t_all, 1)
    freq = _rope_freq()

    wqkv = jnp.concatenate([w_in[:, :3 * WIDTH], w_in[:, 3 * WIDTH + N_HEADS:]], axis=1)
    wf16 = jnp.zeros((16, D_MODEL), BF16).at[:N_HEADS].set(w_in[:, 3 * WIDTH:3 * WIDTH + N_HEADS].T)
    bf = b_fgate.reshape(N_HEADS, 1)

    perms = [_perm_matrix(TOK_TM, d, tr) for tr in (False, True) for d in DILATIONS[1:]]
    h1, za, zb1, zb4, zb16, vt, fa_t = _inproj(xf, ada3, pos, wqkv, wf16, freq, perms, seq)
    zbs = [zb1, zb4.reshape(t_all, 3 * WIDTH), zb16.reshape(t_all, 3 * WIDTH)]
    f_row = _fgate_fwd(fa_t, bf, seq)
    f_col = jnp.zeros((t_all, LANES), F32).at[:, :N_HEADS].set(f_row.T)
    oa, lse_row_a, gathered = _fox_fwd(za, vt, f_col, seq, [late_shards[n] for n in LATE])
    w_out, w_up, conv_w, w_down = (_full_from_gathered(n, g) for n, g in zip(LATE, gathered))
    o3, l3 = zip(*[_dil_fwd(zb, seq, d) for zb, d in zip(zbs, DILATIONS)])
    ob, lse_b, lse_b4, lse_b16, merged, mix, xh1, rs1, h2, h2t = _mix_out(oa, o3, l3, gn_a, gn_b, w_out, xf, ada3, ln1_g,
                                                                      ln1_b, perms, seq)
    u_a, u_g, ffn_in = _ffn_up_gate(h2, w_up, conv_w, conv_b, seq)
    dr2, acc2 = _ffn_down(ffn_in, w_down, xh1, ln1_g, ln1_b, ada3, ln2_g, ln2_b, tg, seq)

    dffn, dfi = _ffn_down_bwd(dr2, ada3, w_down, seq)
    d_w_down = _matmul_tn(dffn, ffn_in, 512, 512, "dw_down").T
    du_a, du_g, acc_ca, acc_cg, dw_up_a, dw_up_g = _ffn_gate_bwd(u_a, u_g, dfi, conv_w, conv_b, h2t, seq)
    dr1, dmix, acc1 = _ffn_up_bwd(du_a, du_g, w_up, dr2, xh1, rs1, mix, ada3, ln1_g, ln1_b, seq)
    d_w_up = jnp.concatenate([dw_up_a, dw_up_g], axis=1)

    doa, dob, dob4, dob16, dl_a, dl_b, dl_b4, dl_b16, acc_gn = _mix_out_bwd(dmix, w_out, oa, ob, gn_a, gn_b, perms, seq)
    d_w_out = _matmul_tn(merged, dmix, 512, 512, "dw_out")
    late_grads = dict(w_out=d_w_out, w_up=d_w_up, conv_w=jnp.concatenate([acc_ca[0:3], acc_cg[0:3]], axis=1),
                      w_down=d_w_down)
    dka, dva, df_k, dqt, df_q, late_parts = _fox_bwd(za, doa, f_col, lse_row_a, _rows(dl_a), seq,
                                                     [_payload(n, _dest_major(n, late_grads[n])) for n in LATE])
    dfa_t, dbf = _fgate_bwd(_rows(df_k) + df_q, fa_t, bf, seq)
    flat = lambda a: a.reshape(t_all, a.shape[-1])
    dil = []
    for zb, d, do, lse, dl in zip(zbs, DILATIONS, (dob, flat(dob4), flat(dob16)),
                                  (lse_b, flat(lse_b4), flat(lse_b16)), (dl_b, flat(dl_b4), flat(dl_b16))):
        dil.append(_dil_bwd(zb, do, lse, dl, seq, d))
    dfa16 = jnp.zeros((16, t_all), BF16).at[:N_HEADS].set(dfa_t.astype(BF16))
    grad_x, dz, acc0 = _inproj_bwd(dqt, dka, dva, dil[0], dil[1], dil[2], dfa16, pos, wqkv, wf16, freq, perms, dr1, xf,
                                   ada3, seq)
    d_wqkv = _matmul_tn(h1, dz, 512, 512, "dw_in")
    d_wf = _matmul_rows(dfa16, h1, 512, "dw_fgate")[:N_HEADS].T
    d_w_in = jnp.concatenate([d_wqkv[:, :3 * WIDTH], d_wf, d_wqkv[:, 3 * WIDTH:]], axis=1)

    dada = jnp.concatenate([acc0[8:8 + nbat], acc0[:nbat], acc1[24:24 + nbat], acc1[16:16 + nbat], acc1[8:8 + nbat],
                            acc2[8:8 + nbat]], axis=1)

    grads = dict(
        dada=dada, b_ada=jnp.sum(dada, axis=0, keepdims=True), w_in=d_w_in, b_fgate=dbf[:, 0][None, :],
        gn_a=acc_gn[0:1, :WIDTH], gn_b=acc_gn[0:1, WIDTH:], ln1_g=acc1[0:1], ln1_b=acc1[1:2],
        conv_b=jnp.concatenate([acc_ca[3:4], acc_cg[3:4]], axis=1), ln2_g=acc2[0:1], ln2_b=acc2[1:2])
    return acc2[2:3], grad_x.reshape(x.shape), grads, dict(zip(LATE, late_parts))


LATE = ("w_out", "w_up", "conv_w", "w_down")
BIG = ("w_ada", "w_in") + LATE
COLUMN_SHARDED = ("w_ada", "w_in", "w_up", "conv_w")


def _payload(name, a):
    return a if name == "conv_w" else a.astype(BF16)
SMALL = ("b_ada", "b_fgate", "gn_a", "gn_b", "ln1_g", "ln1_b", "conv_b", "ln2_g", "ln2_b")
ADAM_ROWS = dict(w_ada=256, w_in=256, w_out=128, w_up=256, conv_w=3, w_down=176)
SMALL_ROWS = 24


def _full_from_gathered(name, g):
    if name in COLUMN_SHARDED:
        return g.transpose(1, 0, 2).reshape(g.shape[1], N_DEV * g.shape[2])
    return g.reshape(N_DEV * g.shape[1], g.shape[2])


def _dest_major(name, full):
    if name in COLUMN_SHARDED:
        r, cfull = full.shape
        return full.reshape(r, N_DEV, cfull // N_DEV).transpose(1, 0, 2)
    return full.reshape(N_DEV, full.shape[0] // N_DEV, full.shape[1])


def _pack_small(vals, extra=None):
    parts = [vals[n].reshape(-1) for n in SMALL]
    if extra is not None:
        parts.append(extra.reshape(-1))
    flat = jnp.concatenate(parts)
    return jnp.pad(flat, (0, SMALL_ROWS * D_MODEL - flat.shape[0])).reshape(SMALL_ROWS, D_MODEL)


def _unpack_small(packed, like):
    flat = packed.reshape(-1)
    out, off = {}, 0
    for n in SMALL:
        size = like[n].size
        out[n] = flat[off:off + size].reshape(like[n].shape)
        off += size
    return out, flat[off:off + D_MODEL]


def kernel(x, c, positions, w_ada, b_ada, w_in, b_fgate, gn_a, gn_b, w_out, ln1_g, ln1_b, w_up, conv_w, conv_b, w_down, ln2_g, ln2_b, loss_target, m_w_ada, m_b_ada, m_w_in, m_b_fgate, m_gn_a, m_gn_b, m_w_out, m_ln1_g, m_ln1_b, m_w_up, m_conv_w, m_conv_b, m_w_down, m_ln2_g, m_ln2_b, v_w_ada, v_b_ada, v_w_in, v_b_fgate, v_gn_a, v_gn_b, v_w_out, v_ln1_g, v_ln1_b, v_w_up, v_conv_w, v_conv_b, v_w_down, v_ln2_g, v_ln2_b):
    w = dict(w_ada=w_ada[0], b_ada=b_ada, w_in=w_in[0], b_fgate=b_fgate, gn_a=gn_a, gn_b=gn_b, w_out=w_out[0],
             ln1_g=ln1_g, ln1_b=ln1_b, w_up=w_up[0], conv_w=conv_w[0], conv_b=conv_b, w_down=w_down[0], ln2_g=ln2_g,
             ln2_b=ln2_b)
    m = dict(w_ada=m_w_ada[0], b_ada=m_b_ada, w_in=m_w_in[0], b_fgate=m_b_fgate, gn_a=m_gn_a, gn_b=m_gn_b,
             w_out=m_w_out[0], ln1_g=m_ln1_g, ln1_b=m_ln1_b, w_up=m_w_up[0], conv_w=m_conv_w[0], conv_b=m_conv_b,
             w_down=m_w_down[0], ln2_g=m_ln2_g, ln2_b=m_ln2_b)
    v = dict(w_ada=v_w_ada[0], b_ada=v_b_ada, w_in=v_w_in[0], b_fgate=v_b_fgate, gn_a=v_gn_a, gn_b=v_gn_b,
             w_out=v_w_out[0], ln1_g=v_ln1_g, ln1_b=v_ln1_b, w_up=v_w_up[0], conv_w=v_conv_w[0], conv_b=v_conv_b,
             w_down=v_w_down[0], ln2_g=v_ln2_g, ln2_b=v_ln2_b)

    nbat = x.shape[0]
    me = 4 * lax.axis_index("x") + 2 * lax.axis_index("y") + lax.axis_index("c")
    ada_cols = w["w_ada"].shape[1]

    c_all, w_in_all = _gather_two_level([c, _payload("w_in", w["w_in"])], "weight_gather")
    c_all = c_all.reshape(N_DEV * nbat, D_MODEL)
    ada_mine = _ada_fwd(c_all, w["w_ada"], lax.dynamic_slice(b_ada, (0, me * ada_cols), (1, ada_cols)))
    (ada_parts,) = _exchange([ada_mine.reshape(N_DEV, nbat, ada_cols)], [False], "ada_exchange")
    ada3 = ada_parts.transpose(1, 0, 2).reshape(nbat, 6, D_MODEL)

    loss_lanes, grad_x, g_local, parts = _local_step(
        x, positions, loss_target, ada3, _full_from_gathered("w_in", w_in_all), b_fgate, gn_a, gn_b, ln1_g, ln1_b,
        conv_b, ln2_g, ln2_b, {n: _payload(n, w[n]) for n in LATE})

    parts["w_in"], dada_all, small_all = _exchange(
        [_payload("w_in", _dest_major("w_in", g_local["w_in"])), g_local["dada"], _pack_small(g_local, loss_lanes)],
        [False, True, True], "grad_exchange")
    dada_cols = lax.dynamic_slice(dada_all.reshape(N_DEV * nbat, 6 * D_MODEL), (0, me * ada_cols),
                                  (N_DEV * nbat, ada_cols))
    parts["w_ada"] = _ada_bwd(c_all, dada_cols)[None]

    grad, delta, new_m, new_v = {}, {}, {}, {}
    for n in BIG:
        grad[n], delta[n], new_m[n], new_v[n] = (
            a[None] for a in _adamw(parts[n], w[n], m[n], v[n], ADAM_ROWS[n], "adamw_" + n))
    packed = _adamw(small_all, _pack_small(w), _pack_small(m), _pack_small(v), SMALL_ROWS, "adamw_small")
    for dst, pk in zip((grad, delta, new_m, new_v), packed):
        vals, lanes = _unpack_small(pk, w)
        dst.update(vals)
        if dst is grad:
            loss = jnp.sum(lanes)

    order = ("w_ada", "b_ada", "w_in", "b_fgate", "gn_a", "gn_b", "w_out", "ln1_g", "ln1_b", "w_up", "conv_w", "conv_b",
             "w_down", "ln2_g", "ln2_b")
    return (loss, grad_x, *[grad[n] for n in order], *[delta[n] for n in order], *[new_m[n] for n in order],
            *[new_v[n] for n in order])
```

```python
import functools

import numpy as np
import jax
import jax.numpy as jnp
from jax import lax
from jax.experimental import pallas as pl
from jax.experimental.pallas import tpu as pltpu

F32, BF16 = jnp.float32, jnp.bfloat16
HIGHEST = lax.Precision.HIGHEST
MESH = pl.DeviceIdType.MESH
ANY = pl.BlockSpec(memory_space=pl.ANY)

D_MODEL = 1024
N_HEADS = 8
HEAD_DIM = 64
WIDTH = 512
D_FF = 2816
N_DEV = 8
ROPE_DIMS = 16
ROPE_THETA = 500000.0
ALPHA = 2.0 ** 0.25
LN_EPS = 1e-5
RMS_EPS = 1e-6
NEG = -1e30
Q_SCALE = 0.125
BLK = 128
LANES = 128
VMEM_LIMIT_BYTES = 56 * 1024 * 1024

ADAM_LR, ADAM_B1, ADAM_B2, ADAM_EPS, ADAM_WD, ADAM_STEP = 0.001, 0.9, 0.999, 1e-08, 0.01, 10


def _params(vmem=VMEM_LIMIT_BYTES):
    return pltpu.CompilerParams(vmem_limit_bytes=vmem)


def _nn(a, b):
    return jnp.dot(a, b, preferred_element_type=F32)


def _nt(a, b):
    return lax.dot_general(a, b, (((1,), (1,)), ((), ())), preferred_element_type=F32)


def _tn(a, b):
    return lax.dot_general(a, b, (((0,), (0,)), ((), ())), preferred_element_type=F32)


def _head_mats():
    r = lax.broadcasted_iota(jnp.int32, (LANES, WIDTH), 0)
    c = lax.broadcasted_iota(jnp.int32, (LANES, WIDTH), 1)
    e = ((c >> 6) == r).astype(BF16)
    r2 = lax.broadcasted_iota(jnp.int32, (WIDTH, LANES), 0)
    c2 = lax.broadcasted_iota(jnp.int32, (WIDTH, LANES), 1)
    et = ((r2 >> 6) == c2).astype(BF16)
    return e, et


def _split3(x):
    hi = x.astype(BF16)
    r = x - hi.astype(F32)
    mid = r.astype(BF16)
    return hi, mid, (r - mid.astype(F32)).astype(BF16)


def _hexp(w, e):
    return sum(_nn(part, e) for part in _split3(w)[:2])


def _hsum(x, et):
    return sum(_nn(part, et) for part in _split3(x)[:2])


def _perm_matrix(rows, d, transpose):
    i = np.arange(rows)
    j = (i % (rows // d)) * d + i // (rows // d)
    p = np.zeros((rows, rows), np.float32)
    p[i, j] = 1.0
    return jnp.asarray(p.T if transpose else p, BF16)


def _permute_f32(p, x):
    return sum(_nn(p, part) for part in _split3(x))


def _store_classes(ref, y, d):
    n = y.shape[0] // d
    for r in range(d):
        ref[r] = y[r * n:(r + 1) * n, :]


def _load_classes(ref, d):
    return jnp.concatenate([ref[r] for r in range(d)], axis=0)


def _rope_tabs(pos_ref, fr_ref, sign):
    ang = pos_ref[...].astype(F32) * fr_ref[...]
    lane = lax.broadcasted_iota(jnp.int32, ang.shape, 1) & (HEAD_DIM - 1)
    m1 = lane < ROPE_DIMS // 2
    m2 = (lane >= ROPE_DIMS // 2) & (lane < ROPE_DIMS)
    cos = jnp.cos(ang)
    sin = jnp.sin(ang) * sign
    return (jnp.where(m1 | m2, cos, 1.0), jnp.where(m1, -sin, 0.0), jnp.where(m2, sin, 0.0))


def _rope(z, tabs):
    c, s1, s2 = tabs
    parts = []
    for p in range(z.shape[1] // LANES):
        zp = z[:, LANES * p:LANES * (p + 1)]
        parts.append(zp * c + pltpu.roll(zp, LANES - 8, 1) * s1 + pltpu.roll(zp, 8, 1) * s2)
    return jnp.concatenate(parts, axis=1)


def _half_masks(rows):
    lane = lax.broadcasted_iota(jnp.int32, (rows, LANES), 1)
    lo = lane < HEAD_DIM
    return lo, jnp.logical_not(lo)


def _layer_norm_bwd(dxh, xh, rstd):
    m1 = jnp.mean(dxh, axis=1, keepdims=True)
    m2 = jnp.mean(dxh * xh, axis=1, keepdims=True)
    return rstd * (dxh - m1 - xh * m2)


def _coords():
    return lax.axis_index("x"), lax.axis_index("y"), lax.axis_index("c")


def _peer(x, y, c, k):
    return (1 - x if k & 4 else x, 1 - y if k & 2 else y, 1 - c if k & 1 else c)


def _comm_sems(n):
    return [pltpu.SemaphoreType.DMA((N_DEV - 1, n)), pltpu.SemaphoreType.DMA((N_DEV - 1, n)),
            pltpu.SemaphoreType.DMA((n,))]


def _comm_copies(ins, outs, to_all, sems):
    send_sems, recv_sems, local_sems = sems
    x, y, c = _coords()
    me = 4 * x + 2 * y + c
    copies = [pltpu.make_async_copy(ins[t] if to_all[t] else ins[t].at[me], outs[t].at[me], local_sems.at[t])
              for t in range(len(ins))]
    for k in range(1, N_DEV):
        px, py, pc = _peer(x, y, c, k)
        dest = 4 * px + 2 * py + pc
        for t in range(len(ins)):
            copies.append(pltpu.make_async_remote_copy(
                src_ref=ins[t] if to_all[t] else ins[t].at[dest], dst_ref=outs[t].at[me],
                send_sem=send_sems.at[k - 1, t], recv_sem=recv_sems.at[k - 1, t],
                device_id=(px, py, pc), device_id_type=MESH))
    return copies


def _comm_out_shapes(ins, to_all):
    return [jax.ShapeDtypeStruct(((N_DEV,) + a.shape) if ta else a.shape, a.dtype) for a, ta in zip(ins, to_all)]


def _exchange(ins, to_all, name):
    n = len(ins)

    def body(*refs):
        copies = _comm_copies(refs[:n], refs[n:2 * n], to_all, refs[2 * n:])
        for cp in copies:
            cp.start()
        for cp in copies:
            cp.wait()

    return pl.pallas_call(
        body, name=name, out_shape=_comm_out_shapes(ins, to_all), in_specs=[ANY] * n, out_specs=[ANY] * n,
        scratch_shapes=_comm_sems(n),
    )(*ins)


def _gather_two_level(ins, name):
    n = len(ins)

    def body(*refs):
        srcs, outs = refs[:n], refs[n:2 * n]
        send_sems, recv_sems, local_sems = refs[2 * n:]
        x, y, c = _coords()
        me = 4 * x + 2 * y + c
        sibling = (x, y, 1 - c)
        chips = [(1 - x, y), (x, 1 - y), (1 - x, 1 - y)]
        slot = lambda px, py, pc: 4 * px + 2 * py + pc

        def copy(k, t, block, to, own=False):
            return pltpu.make_async_remote_copy(
                src_ref=srcs[t] if own else outs[t].at[block], dst_ref=outs[t].at[block],
                send_sem=send_sems.at[k, t], recv_sem=recv_sems.at[k, t], device_id=to, device_id_type=MESH)

        local = [pltpu.make_async_copy(srcs[t], outs[t].at[me], local_sems.at[t]) for t in range(n)]
        first = [copy(0, t, me, sibling, own=True) for t in range(n)]
        first += [copy(1 + j, t, me, (*chip, c), own=True) for j, chip in enumerate(chips) for t in range(n)]
        for cp in local + first:
            cp.start()
        passed = []
        for j, chip in enumerate(chips):
            for t in range(n):
                copy(1 + j, t, slot(*chip, c), (x, y, c)).wait_recv()
                cp = copy(4 + j, t, slot(*chip, c), sibling)
                cp.start()
                passed.append(cp)
        for t in range(n):
            copy(0, t, slot(x, y, 1 - c), (x, y, c)).wait_recv()
            for j, chip in enumerate(chips):
                copy(4 + j, t, slot(*chip, 1 - c), (x, y, c)).wait_recv()
        for cp in first + passed:
            cp.wait_send()
        for cp in local:
            cp.wait()

    return pl.pallas_call(
        body, name=name, out_shape=_comm_out_shapes(ins, [True] * n), in_specs=[ANY] * n, out_specs=[ANY] * n,
        scratch_shapes=_comm_sems(n),
    )(*ins)


def _adamw(parts, w, m, v, rows, name):
    n_parts, r_all, cols = parts.shape
    c1 = 1.0 - ADAM_B1 ** ADAM_STEP
    c2 = 1.0 - ADAM_B2 ** ADAM_STEP

    def body(p_ref, w_ref, m_ref, v_ref, g_ref, d_ref, mo_ref, vo_ref):
        g = p_ref[0].astype(F32)
        for s in range(1, n_parts):
            g = g + p_ref[s].astype(F32)
        mn = ADAM_B1 * m_ref[...] + (1.0 - ADAM_B1) * g
        vn = ADAM_B2 * v_ref[...] + (1.0 - ADAM_B2) * (g * g)
        m_hat = mn / c1
        v_hat = vn / c2
        g_ref[...] = g
        d_ref[...] = -ADAM_LR * (m_hat / (jnp.sqrt(v_hat) + ADAM_EPS) + ADAM_WD * w_ref[...])
        mo_ref[...] = mn
        vo_ref[...] = vn

    spec = pl.BlockSpec((rows, cols), lambda i: (i, 0))
    return pl.pallas_call(
        body, name=name, grid=(r_all // rows,),
        in_specs=[pl.BlockSpec((n_parts, rows, cols), lambda i: (0, i, 0)), spec, spec, spec],
        out_specs=[spec] * 4, out_shape=[jax.ShapeDtypeStruct((r_all, cols), F32)] * 4,
        compiler_params=_params(),
    )(parts, w, m, v)


def _matmul_tn(a, b, chunk, tk, name):
    t_all, k1 = a.shape
    n = b.shape[1]

    def body(a_ref, b_ref, o_ref):
        @pl.when(pl.program_id(0) == 0)
        def _():
            o_ref[...] = jnp.zeros_like(o_ref)
        at = a_ref[...].astype(F32).T.astype(BF16)
        for j in range(0, n, chunk):
            cs = slice(j, min(j + chunk, n))
            o_ref[:, cs] += _nn(at, b_ref[:, cs])

    return pl.pallas_call(
        body, name=name, grid=(t_all // tk,),
        in_specs=[pl.BlockSpec((tk, k1), lambda t: (t, 0)), pl.BlockSpec((tk, n), lambda t: (t, 0))],
        out_specs=pl.BlockSpec((k1, n), lambda t: (0, 0)),
        out_shape=jax.ShapeDtypeStruct((k1, n), F32), compiler_params=_params(),
    )(a, b)


def _matmul_rows(a, b, tk, name):
    r, t_all = a.shape
    n = b.shape[1]

    def body(a_ref, b_ref, o_ref):
        @pl.when(pl.program_id(0) == 0)
        def _():
            o_ref[...] = jnp.zeros_like(o_ref)
        o_ref[...] += _nn(a_ref[...], b_ref[...])

    return pl.pallas_call(
        body, name=name, grid=(t_all // tk,),
        in_specs=[pl.BlockSpec((r, tk), lambda t: (0, t)), pl.BlockSpec((tk, n), lambda t: (t, 0))],
        out_specs=pl.BlockSpec((r, n), lambda t: (0, 0)),
        out_shape=jax.ShapeDtypeStruct((r, n), F32), compiler_params=_params(),
    )(a, b)


def _ada_fwd(c_all, w_ada, b_ada):
    whole = lambda a: pl.BlockSpec(a.shape, lambda j: (0, 0))

    def body(c_ref, w_ref, b_ref, o_ref):
        cv = c_ref[...]
        s = (cv * jax.nn.sigmoid(cv)).astype(BF16)
        o_ref[...] = _nn(s, w_ref[...].astype(BF16)) + b_ref[...]

    out = jax.ShapeDtypeStruct((c_all.shape[0], w_ada.shape[1]), F32)
    return pl.pallas_call(
        body, name="ada_fwd", grid=(1,), in_specs=[whole(c_all), whole(w_ada), whole(b_ada)], out_specs=whole(out),
        out_shape=out, compiler_params=_params(),
    )(c_all, w_ada, b_ada)


def _ada_bwd(c_all, dada):
    whole = lambda a: pl.BlockSpec(a.shape, lambda j: (0, 0))

    def body(c_ref, d_ref, o_ref):
        cv = c_ref[...]
        s = (cv * jax.nn.sigmoid(cv)).astype(BF16)
        o_ref[...] = _tn(s, d_ref[...].astype(BF16))

    out = jax.ShapeDtypeStruct((D_MODEL, dada.shape[1]), F32)
    return pl.pallas_call(
        body, name="ada_bwd", grid=(1,), in_specs=[whole(c_all), whole(dada)], out_specs=whole(out), out_shape=out,
        compiler_params=_params(),
    )(c_all, dada)


TOK_TM = 256
DILATIONS = (1, 4, 16)


def _class_spec(d, width, nts):
    return pl.BlockSpec((d, TOK_TM // d, width), lambda i: (i // nts, i % nts, 0))


def _class_shape(t_all, seq, d, width, dtype):
    return jax.ShapeDtypeStruct((t_all // seq * d, seq // d, width), dtype)


def _inproj(x, ada3, pos, wqkv, wf16, freq, perms, seq):
    t_all = x.shape[0]
    tm = TOK_TM
    nts = seq // tm

    def body(x_ref, ada_ref, pos_ref, w_ref, wf_ref, fr_ref, p4_ref, p16_ref, h1_ref, za_ref, zb_ref, zb4_ref,
             zb16_ref, vt_ref, fa_ref):
        h1 = (x_ref[...] * (1.0 + ada_ref[0, 1:2, :]) + ada_ref[0, 0:1, :]).astype(BF16)
        h1_ref[...] = h1
        tabs = _rope_tabs(pos_ref, fr_ref, 1.0)
        for n in range(6):
            z = _nn(h1, w_ref[:, n * WIDTH:(n + 1) * WIDTH])
            if n in (3, 4):
                z = _rope(z, tabs)
            if n in (0, 3):
                z = z * Q_SCALE
            if n == 2:
                vt_ref[...] = z.T.astype(BF16)
            dst = za_ref if n < 3 else zb_ref
            dst[:, (n % 3) * WIDTH:(n % 3 + 1) * WIDTH] = z.astype(BF16)
        fa_ref[...] = _nt(wf_ref[...], h1)[:N_HEADS]
        zb = zb_ref[...]
        _store_classes(zb4_ref, _nn(p4_ref[...], zb).astype(BF16), 4)
        _store_classes(zb16_ref, _nn(p16_ref[...], zb).astype(BF16), 16)

    tok = lambda w: pl.BlockSpec((tm, w), lambda i: (i, 0))
    whole = lambda a: pl.BlockSpec(a.shape, lambda i: (0, 0))
    return pl.pallas_call(
        body, name="inproj", grid=(t_all // tm,),
        in_specs=[tok(D_MODEL), pl.BlockSpec((1, 6, D_MODEL), lambda i: (i // nts, 0, 0)), tok(1), whole(wqkv),
                  whole(wf16), pl.BlockSpec((1, LANES), lambda i: (0, 0)), whole(perms[0]), whole(perms[1])],
        out_specs=[tok(D_MODEL), tok(3 * WIDTH), tok(3 * WIDTH), _class_spec(4, 3 * WIDTH, nts),
                   _class_spec(16, 3 * WIDTH, nts), pl.BlockSpec((WIDTH, tm), lambda i: (i // nts, i % nts)),
                   pl.BlockSpec((N_HEADS, tm), lambda i: (0, i))],
        out_shape=[jax.ShapeDtypeStruct((t_all, D_MODEL), BF16), jax.ShapeDtypeStruct((t_all, 3 * WIDTH), BF16),
                   jax.ShapeDtypeStruct((t_all, 3 * WIDTH), BF16), _class_shape(t_all, seq, 4, 3 * WIDTH, BF16),
                   _class_shape(t_all, seq, 16, 3 * WIDTH, BF16),
                   jax.ShapeDtypeStruct((t_all // seq * WIDTH, seq), BF16),
                   jax.ShapeDtypeStruct((N_HEADS, t_all), F32)],
        compiler_params=_params(),
    )(x, ada3, pos, wqkv, wf16, freq, perms[0], perms[1])


def _chunk_rows(a_t, seq):
    t_all = a_t.shape[1]
    return a_t.reshape(N_HEADS, t_all // seq, seq // LANES, LANES).transpose(1, 0, 2, 3).reshape(-1, LANES)


def _unchunk_rows(a, seq):
    nbat = a.shape[0] * LANES // (N_HEADS * seq)
    return a.reshape(nbat, N_HEADS, seq // LANES, LANES).transpose(1, 0, 2, 3).reshape(N_HEADS, nbat * seq)


def _chunk_carry(tot, nchunk, later):
    rows = tot.shape[0]
    r = lax.broadcasted_iota(jnp.int32, (rows, rows), 0)
    c = lax.broadcasted_iota(jnp.int32, (rows, rows), 1)
    sel = ((r // nchunk) == (c // nchunk)) & ((c > r) if later else (c < r))
    mat = sel.astype(BF16)
    return sum(_nn(mat, part) for part in _split3(jnp.broadcast_to(tot, (rows, LANES))))


def _fgate_fwd(fa_t, bf, seq):
    x = _chunk_rows(fa_t, seq)
    rows = x.shape[0]
    nchunk = seq // LANES
    bias = jnp.broadcast_to(bf.reshape(1, N_HEADS, 1), (rows // (N_HEADS * nchunk), N_HEADS, nchunk)).reshape(rows, 1)

    def body(x_ref, b_ref, f_ref):
        lane = lax.broadcasted_iota(jnp.int32, (rows, LANES), 1)
        xv = x_ref[...] + b_ref[...]
        lf = jnp.minimum(xv, 0.0) - jnp.log(1.0 + jnp.exp(-jnp.abs(xv)))
        for s in (1, 2, 4, 8, 16, 32, 64):
            lf = lf + jnp.where(lane >= s, pltpu.roll(lf, s, 1), 0.0)
        f_ref[...] = lf + _chunk_carry(lf[:, LANES - 1:LANES], nchunk, False)

    whole = lambda a: pl.BlockSpec(a.shape, lambda i: (0, 0))
    out = pl.pallas_call(
        body, name="fgate_fwd", grid=(1,), in_specs=[whole(x), whole(bias)], out_specs=whole(x),
        out_shape=jax.ShapeDtypeStruct(x.shape, F32), compiler_params=_params(),
    )(x, bias)
    return _unchunk_rows(out, seq)


def _fgate_bwd(df_t, fa_t, bf, seq):
    d_in = _chunk_rows(df_t, seq)
    x = _chunk_rows(fa_t, seq)
    rows = x.shape[0]
    nchunk = seq // LANES
    bias = jnp.broadcast_to(bf.reshape(1, N_HEADS, 1), (rows // (N_HEADS * nchunk), N_HEADS, nchunk)).reshape(rows, 1)

    def body(d_ref, x_ref, b_ref, o_ref, s_ref):
        lane = lax.broadcasted_iota(jnp.int32, (rows, LANES), 1)
        d = d_ref[...]
        for s in (1, 2, 4, 8, 16, 32, 64):
            d = d + jnp.where(lane < LANES - s, pltpu.roll(d, LANES - s, 1), 0.0)
        d = d + _chunk_carry(d[:, 0:1], nchunk, True)
        dfa = d * jax.nn.sigmoid(-(x_ref[...] + b_ref[...]))
        o_ref[...] = dfa
        g = lax.broadcasted_iota(jnp.int32, (2 * N_HEADS, rows), 0)
        r = lax.broadcasted_iota(jnp.int32, (2 * N_HEADS, rows), 1)
        group = (((r // nchunk) % N_HEADS) == g).astype(BF16)
        per_head = sum(_nn(group, part) for part in _split3(dfa))[:N_HEADS]
        s_ref[...] = jnp.broadcast_to(jnp.sum(per_head, axis=1, keepdims=True), (N_HEADS, LANES))

    whole = lambda a: pl.BlockSpec(a.shape, lambda i: (0, 0))
    dfa, sums = pl.pallas_call(
        body, name="fgate_bwd", grid=(1,), in_specs=[whole(d_in), whole(x), whole(bias)],
        out_specs=[whole(x), pl.BlockSpec((N_HEADS, LANES), lambda i: (0, 0))],
        out_shape=[jax.ShapeDtypeStruct(x.shape, F32), jax.ShapeDtypeStruct((N_HEADS, LANES), F32)],
        compiler_params=_params(),
    )(d_in, x, bias)
    return _unchunk_rows(dfa, seq), sums


FOX_T = 256


def _fox_prep(dst, src_ref, lo, hi):
    for p in range(4):
        v = src_ref[:, LANES * p:LANES * (p + 1)]
        dst[2 * p] = jnp.where(lo, v, jnp.zeros_like(v))
        dst[2 * p + 1] = jnp.where(hi, v, jnp.zeros_like(v))


def _fox_fwd(za, vt, f_col, seq, shards):
    t_all = za.shape[0]
    tq = FOX_T
    nq = seq // tq
    nbat = t_all // seq
    n = len(shards)
    to_all = [True] * n

    def body(*refs):
        q_ref, k_ref, vt_ref, fc_ref = refs[:4]
        o_ref, lse_ref = refs[4 + n:6 + n]
        qm_sc, m_sc, l_sc, acc_sc, a_sc, st_sc, pe_sc = refs[6 + 2 * n:13 + 2 * n]
        comm = (refs[4:4 + n], refs[6 + n:6 + 2 * n], to_all, refs[13 + 2 * n:])
        i = pl.program_id(1)

        @pl.when((pl.program_id(0) == 0) & (i == 0))
        def _():
            for cp in _comm_copies(*comm):
                cp.start()
        lo, hi = _half_masks(tq)
        r = lax.broadcasted_iota(jnp.int32, (tq, tq), 0)
        c = lax.broadcasted_iota(jnp.int32, (tq, tq), 1)
        tri = c >= r
        _fox_prep(qm_sc, q_ref, lo, hi)
        m_sc[...] = jnp.full(m_sc.shape, NEG, F32)
        l_sc[...] = jnp.zeros_like(l_sc)
        acc_sc[...] = jnp.zeros_like(acc_sc)

        def block(j, masked):
            sl = pl.ds(pl.multiple_of(j * tq, tq), tq)
            for p in range(4):
                kj = k_ref[sl, LANES * p:LANES * (p + 1)]
                for h in (2 * p, 2 * p + 1):
                    st = _nt(kj, qm_sc[h]) - fc_ref[sl, h:h + 1]
                    st_sc[h] = jnp.where(tri, st, NEG) if masked else st
            for h in range(N_HEADS):
                st = st_sc[h]
                m = m_sc[h:h + 1, :]
                mn = jnp.maximum(m, jnp.max(st, axis=0, keepdims=True))
                a = jnp.exp(m - mn)
                pe = jnp.exp(st - mn)
                m_sc[h:h + 1, :] = mn
                a_sc[h:h + 1, :] = a
                l_sc[h:h + 1, :] = a * l_sc[h:h + 1, :] + jnp.sum(pe, axis=0, keepdims=True)
                pe_sc[h] = pe.astype(BF16)
            for h in range(N_HEADS):
                acc_sc[h] = a_sc[h:h + 1, :] * acc_sc[h] + _nn(vt_ref[HEAD_DIM * h:HEAD_DIM * (h + 1), sl], pe_sc[h])

        def step(j, carry):
            block(j, False)
            return carry

        lax.fori_loop(0, i, step, 0)
        block(i, True)
        lse_ref[...] = m_sc[...] + jnp.log(l_sc[...])
        for p in range(4):
            ot = jnp.concatenate([acc_sc[h] / l_sc[h:h + 1, :] for h in (2 * p, 2 * p + 1)], axis=0)
            o_ref[:, LANES * p:LANES * (p + 1)] = ot.T

        @pl.when((pl.program_id(0) == nbat - 1) & (i == nq - 1))
        def _():
            for cp in _comm_copies(*comm):
                cp.wait()

    res = pl.pallas_call(
        body, name="fox_fwd", grid=(nbat, nq),
        in_specs=[pl.BlockSpec((tq, WIDTH), lambda b, i: (b * nq + i, 0)),
                  pl.BlockSpec((seq, WIDTH), lambda b, i: (b, 1)), pl.BlockSpec((WIDTH, seq), lambda b, i: (b, 0)),
                  pl.BlockSpec((seq, LANES), lambda b, i: (b, 0))] + [ANY] * n,
        out_specs=[pl.BlockSpec((tq, WIDTH), lambda b, i: (b * nq + i, 0)),
                   pl.BlockSpec((N_HEADS, tq), lambda b, i: (0, b * nq + i))] + [ANY] * n,
        out_shape=[jax.ShapeDtypeStruct((t_all, WIDTH), F32), jax.ShapeDtypeStruct((N_HEADS, t_all), F32)]
        + _comm_out_shapes(shards, to_all),
        scratch_shapes=[pltpu.VMEM((N_HEADS, tq, LANES), BF16), pltpu.VMEM((N_HEADS, tq), F32),
                        pltpu.VMEM((N_HEADS, tq), F32), pltpu.VMEM((N_HEADS, HEAD_DIM, tq), F32),
                        pltpu.VMEM((N_HEADS, tq), F32), pltpu.VMEM((N_HEADS, tq, tq), F32),
                        pltpu.VMEM((N_HEADS, tq, tq), BF16)] + _comm_sems(n),
        compiler_params=_params(),
    )(za, za, vt, f_col, *shards)
    return res[0], res[1], res[2:]


def _fox_bwd(za, do, f_col, lse_row, dl_row, seq, grads):
    t_all = za.shape[0]
    tk = FOX_T
    nk = seq // tk
    nbat = t_all // seq
    n = len(grads)
    to_all = [False] * n

    def body(*refs):
        k_ref, v_ref, q_ref, do_ref, fc_ref, lr_ref, dr_ref = refs[:7]
        dk_ref, dv_ref, df_ref, dqt_ref, dfq_ref = refs[7 + n:12 + n]
        km_sc, vm_sc, fk_sc, dk_sc, dv_sc, cs_sc, kt_sc, st_sc, dp_sc, pt_sc, ds_sc = refs[12 + 2 * n:23 + 2 * n]
        comm = (refs[7:7 + n], refs[12 + n:12 + 2 * n], to_all, refs[23 + 2 * n:])
        j = pl.program_id(1)

        @pl.when(j == 0)
        def _():
            dqt_ref[...] = jnp.zeros_like(dqt_ref)
            dfq_ref[...] = jnp.zeros_like(dfq_ref)

        @pl.when((pl.program_id(0) == 0) & (j == 0))
        def _():
            for cp in _comm_copies(*comm):
                cp.start()
        lo, hi = _half_masks(tk)
        r = lax.broadcasted_iota(jnp.int32, (tk, tk), 0)
        c = lax.broadcasted_iota(jnp.int32, (tk, tk), 1)
        tri = c >= r
        _fox_prep(km_sc, k_ref, lo, hi)
        _fox_prep(vm_sc, v_ref, lo, hi)
        for h in range(N_HEADS):
            fk_sc[h] = jnp.broadcast_to(fc_ref[:, h:h + 1], (tk, tk))
        for p in range(4):
            kt_sc[p] = k_ref[:, LANES * p:LANES * (p + 1)].astype(F32).T.astype(BF16)
        dk_sc[...] = jnp.zeros_like(dk_sc)
        dv_sc[...] = jnp.zeros_like(dv_sc)
        cs_sc[...] = jnp.zeros_like(cs_sc)

        def block(i, masked):
            sl = pl.ds(pl.multiple_of(i * tk, tk), tk)
            for p in range(4):
                cs = slice(LANES * p, LANES * (p + 1))
                qi = q_ref[sl, cs]
                doi = do_ref[sl, cs]
                for h in (2 * p, 2 * p + 1):
                    st = _nt(km_sc[h], qi) - fk_sc[h] - lr_ref[h:h + 1, sl]
                    st_sc[h] = jnp.where(tri, st, NEG) if masked else st
                    dp_sc[h] = _nt(vm_sc[h], doi) - dr_ref[h:h + 1, sl]
            for h in range(N_HEADS):
                pt = jnp.exp(st_sc[h])
                dst = pt * dp_sc[h]
                pt_sc[h] = pt.astype(BF16)
                ds_sc[h] = dst.astype(BF16)
                cs_sc[h] += dst[:, :LANES] + dst[:, LANES:]
                dfq_ref[h:h + 1, sl] += jnp.sum(dst, axis=0, keepdims=True)
            for p in range(4):
                cs = slice(LANES * p, LANES * (p + 1))
                qi = q_ref[sl, cs]
                doi = do_ref[sl, cs]
                for h in (2 * p, 2 * p + 1):
                    dv_sc[h] += _nn(pt_sc[h], doi)
                    dk_sc[h] += _nn(ds_sc[h], qi)
                    kt = kt_sc[p, HEAD_DIM * (h % 2):HEAD_DIM * (h % 2 + 1), :]
                    dqt_ref[HEAD_DIM * h:HEAD_DIM * (h + 1), sl] += _nn(kt, ds_sc[h])

        def step(i, carry):
            block(i, False)
            return carry

        block(j, True)
        lax.fori_loop(j + 1, nk, step, 0)
        df_ref[...] = jnp.zeros_like(df_ref)
        for p in range(4):
            cs = slice(LANES * p, LANES * (p + 1))
            dk_ref[:, cs] = jnp.where(lo, dk_sc[2 * p], dk_sc[2 * p + 1]).astype(BF16)
            dv_ref[:, cs] = jnp.where(lo, dv_sc[2 * p], dv_sc[2 * p + 1]).astype(BF16)
            for h in (2 * p, 2 * p + 1):
                df_ref[:, h:h + 1] = -jnp.sum(cs_sc[h], axis=1, keepdims=True)

        @pl.when(j == nk - 1)
        def _():
            dqt_ref[...] = dqt_ref[...] * Q_SCALE

        @pl.when((pl.program_id(0) == nbat - 1) & (j == nk - 1))
        def _():
            for cp in _comm_copies(*comm):
                cp.wait()

    tile = lambda w, col: pl.BlockSpec((tk, w), lambda b, j: (b * nk + j, col))
    full = lambda col: pl.BlockSpec((seq, WIDTH), lambda b, j: (b, col))
    row = pl.BlockSpec((N_HEADS, seq), lambda b, j: (0, b))
    acc = pltpu.VMEM((N_HEADS, tk, LANES), F32)
    res = pl.pallas_call(
        body, name="fox_bwd", grid=(nbat, nk),
        in_specs=[tile(WIDTH, 1), tile(WIDTH, 2), full(0), full(0), tile(LANES, 0), row, row] + [ANY] * n,
        out_specs=[tile(WIDTH, 0), tile(WIDTH, 0), tile(LANES, 0), pl.BlockSpec((WIDTH, seq), lambda b, j: (b, 0)),
                   row] + [ANY] * n,
        out_shape=[jax.ShapeDtypeStruct((t_all, WIDTH), BF16), jax.ShapeDtypeStruct((t_all, WIDTH), BF16),
                   jax.ShapeDtypeStruct((t_all, LANES), F32), jax.ShapeDtypeStruct((nbat * WIDTH, seq), F32),
                   jax.ShapeDtypeStruct((N_HEADS, t_all), F32)] + _comm_out_shapes(grads, to_all),
        scratch_shapes=[pltpu.VMEM((N_HEADS, tk, LANES), BF16), pltpu.VMEM((N_HEADS, tk, LANES), BF16),
                        pltpu.VMEM((N_HEADS, tk, tk), F32), acc, acc, acc, pltpu.VMEM((4, LANES, tk), BF16),
                        pltpu.VMEM((N_HEADS, tk, tk), F32), pltpu.VMEM((N_HEADS, tk, tk), F32),
                        pltpu.VMEM((N_HEADS, tk, tk), BF16), pltpu.VMEM((N_HEADS, tk, tk), BF16)]
        + _comm_sems(n),
        compiler_params=_params(),
    )(za, za, za, do, f_col, lse_row, dl_row, *grads)
    return res[0], res[1], res[2], res[3], res[4], res[5:]


DIL_SUB = 4


def _dil_mask(has_prev):
    qi = lax.broadcasted_iota(jnp.int32, (BLK, 2 * BLK), 0)
    kj = lax.broadcasted_iota(jnp.int32, (BLK, 2 * BLK), 1)
    dist = qi + BLK - kj
    band = (dist >= 0) & (dist <= BLK)
    return band if has_prev is True else band & ((kj >= BLK) | has_prev)


def _dil_geometry(t_all, seq, d, max_sub=DIL_SUB):
    length = seq // d
    nbs = length // BLK
    sub = min(max_sub, nbs)
    spb = nbs // sub
    tile = lambda width, col: pl.BlockSpec((BLK * sub, width), lambda s: (s, col))
    whole = lambda width, col: pl.BlockSpec((length, width), lambda s: (s // spb, col))
    return nbs, sub, spb, t_all // (BLK * sub), tile, whole


def _blk(i):
    return pl.ds(pl.multiple_of(i * BLK, BLK), BLK)


def _dil_fwd(zb, seq, d):
    t_all = zb.shape[0]
    nbs, sub, spb, steps, tile, whole = _dil_geometry(t_all, seq, d)

    def body(q_ref, k_ref, v_ref, o_ref, lse_ref, s_sc, p_sc):
        first = (pl.program_id(0) % spb) * sub
        lo, hi = _half_masks(BLK)
        lse_ref[...] = jnp.zeros_like(lse_ref)
        for j in range(sub):
            blk = first + j
            mask = _dil_mask(blk != 0 if j == 0 else True)
            for p in range(4):
                cs = slice(LANES * p, LANES * (p + 1))
                qp = q_ref[BLK * j:BLK * (j + 1), cs]
                kcat = jnp.concatenate([k_ref[_blk(jnp.maximum(blk - 1, 0)), cs], k_ref[_blk(blk), cs]], axis=0)
                for e in (0, 1):
                    qe = jnp.where(lo if e == 0 else hi, qp, jnp.zeros_like(qp))
                    s_sc[N_HEADS * j + 2 * p + e] = jnp.where(mask, _nt(qe, kcat), NEG)
        inv = []
        for i in range(N_HEADS * sub):
            s = s_sc[i]
            m = jnp.max(s, axis=1, keepdims=True)
            pe = jnp.exp(s - m)
            l = jnp.sum(pe, axis=1, keepdims=True)
            p_sc[i] = pe.astype(BF16)
            inv.append(1.0 / l)
            j, h = divmod(i, N_HEADS)
            lse_ref[BLK * j:BLK * (j + 1), h:h + 1] = m + jnp.log(l)
        for j in range(sub):
            blk = first + j
            for p in range(4):
                cs = slice(LANES * p, LANES * (p + 1))
                vcat = jnp.concatenate([v_ref[_blk(jnp.maximum(blk - 1, 0)), cs], v_ref[_blk(blk), cs]], axis=0)
                res = [_nn(p_sc[N_HEADS * j + h], vcat) * inv[N_HEADS * j + h] for h in (2 * p, 2 * p + 1)]
                o_ref[BLK * j:BLK * (j + 1), cs] = jnp.where(lo, res[0], res[1])

    return pl.pallas_call(
        body, name=f"dil_fwd_{d}", grid=(steps,), in_specs=[tile(WIDTH, 0), whole(WIDTH, 1), whole(WIDTH, 2)],
        out_specs=[tile(WIDTH, 0), tile(LANES, 0)],
        out_shape=[jax.ShapeDtypeStruct((t_all, WIDTH), F32), jax.ShapeDtypeStruct((t_all, LANES), F32)],
        scratch_shapes=[pltpu.VMEM((N_HEADS * sub, BLK, 2 * BLK), F32),
                        pltpu.VMEM((N_HEADS * sub, BLK, 2 * BLK), BF16)],
        compiler_params=_params(),
    )(zb, zb, zb)


def _dil_bwd(zb, do, lse, dl, seq, d):
    t_all = zb.shape[0]
    length = seq // d
    nbs, sub, spb, steps, tile, whole = _dil_geometry(t_all, seq, d, 2 if length >= 4096 else DIL_SUB)

    def body(k_ref, v_ref, q_ref, do_ref, lse_ref, dl_ref, dq_ref, dk_ref, dv_ref, s_sc, dp_sc, pt_sc, ds_sc, kt_sc,
             dqt_sc):
        step = pl.program_id(0) % spb
        first = step * sub

        @pl.when(step == 0)
        def _():
            dqt_sc[...] = jnp.zeros_like(dqt_sc)
        r = lax.broadcasted_iota(jnp.int32, (BLK, 2 * BLK), 0)
        c = lax.broadcasted_iota(jnp.int32, (BLK, 2 * BLK), 1)
        same = (c < BLK) & (c >= r)
        later = (c >= BLK) & (c - BLK <= r)
        lo, hi = _half_masks(BLK)
        for j in range(sub):
            blk = first + j
            rows = slice(BLK * j, BLK * (j + 1))
            nxt = _blk(jnp.minimum(blk + 1, nbs - 1))
            mask = same | (later & (blk + 1 != nbs)) if j == sub - 1 else same | later
            lrows = jnp.concatenate([lse_ref[_blk(blk), :].T, lse_ref[nxt, :].T], axis=1)
            erows = jnp.concatenate([dl_ref[_blk(blk), :].T, dl_ref[nxt, :].T], axis=1)
            for p in range(4):
                cs = slice(LANES * p, LANES * (p + 1))
                kp = k_ref[rows, cs]
                vp = v_ref[rows, cs]
                kt_sc[4 * j + p] = kp.astype(F32).T.astype(BF16)
                qcat = jnp.concatenate([q_ref[_blk(blk), cs], q_ref[nxt, cs]], axis=0)
                dcat = jnp.concatenate([do_ref[_blk(blk), cs], do_ref[nxt, cs]], axis=0)
                for e in (0, 1):
                    h = 2 * p + e
                    sel = lo if e == 0 else hi
                    ke = jnp.where(sel, kp, jnp.zeros_like(kp))
                    ve = jnp.where(sel, vp, jnp.zeros_like(vp))
                    s_sc[N_HEADS * j + h] = jnp.where(mask, _nt(ke, qcat) - lrows[h:h + 1, :], NEG)
                    dp_sc[N_HEADS * j + h] = _nt(ve, dcat) - erows[h:h + 1, :]
        for i in range(N_HEADS * sub):
            pt = jnp.exp(s_sc[i])
            pt_sc[i] = pt.astype(BF16)
            ds_sc[i] = (pt * dp_sc[i]).astype(BF16)
        for j in range(sub):
            blk = first + j
            rows = slice(BLK * j, BLK * (j + 1))
            nxt = _blk(jnp.minimum(blk + 1, nbs - 1))
            cols = pl.ds(pl.multiple_of(blk * BLK, BLK), 2 * BLK)
            for p in range(4):
                cs = slice(LANES * p, LANES * (p + 1))
                qcat = jnp.concatenate([q_ref[_blk(blk), cs], q_ref[nxt, cs]], axis=0)
                dcat = jnp.concatenate([do_ref[_blk(blk), cs], do_ref[nxt, cs]], axis=0)
                i = N_HEADS * j + 2 * p
                dk_ref[rows, cs] = jnp.where(lo, _nn(ds_sc[i], qcat), _nn(ds_sc[i + 1], qcat)).astype(BF16)
                dv_ref[rows, cs] = jnp.where(lo, _nn(pt_sc[i], dcat), _nn(pt_sc[i + 1], dcat)).astype(BF16)
                for e in (0, 1):
                    kt = kt_sc[4 * j + p, HEAD_DIM * e:HEAD_DIM * (e + 1), :]
                    dqt_sc[HEAD_DIM * (2 * p + e):HEAD_DIM * (2 * p + e + 1), cols] += _nn(kt, ds_sc[i + e])

        @pl.when(step == spb - 1)
        def _():
            for p in range(4):
                cs = slice(LANES * p, LANES * (p + 1))
                dq_ref[:, cs] = (dqt_sc[cs, 0:length].T * Q_SCALE).astype(BF16)

    wide = pltpu.VMEM((N_HEADS * sub, BLK, 2 * BLK), F32)
    half = pltpu.VMEM((N_HEADS * sub, BLK, 2 * BLK), BF16)
    return pl.pallas_call(
        body, name=f"dil_bwd_{d}", grid=(steps,),
        in_specs=[tile(WIDTH, 1), tile(WIDTH, 2), whole(WIDTH, 0), whole(WIDTH, 0), whole(LANES, 0), whole(LANES, 0)],
        out_specs=[whole(WIDTH, 0), tile(WIDTH, 0), tile(WIDTH, 0)],
        out_shape=[jax.ShapeDtypeStruct((t_all, WIDTH), BF16)] * 3,
        scratch_shapes=[wide, wide, half, half, pltpu.VMEM((4 * sub, LANES, BLK), BF16),
                        pltpu.VMEM((WIDTH, length + BLK), F32)],
        compiler_params=_params(),
    )(zb, zb, zb, do, lse, dl)


def _mix_out(oa, o3, l3, gn_a, gn_b, w_out, x, ada3, ln_g, ln_b, perms, seq):
    t_all = x.shape[0]
    tm = TOK_TM
    nts = seq // tm

    def body(oa_ref, o1_ref, o2_ref, o3_ref, l1_ref, l2_ref, l3_ref, ga_ref, gb_ref, w_ref, x_ref, ada_ref, g_ref,
             b_ref, p4_ref, p16_ref, pt4_ref, pt16_ref, ob_ref, lse_ref, lse4_ref, lse16_ref, mg_ref, mix_ref, xh_ref,
             rs_ref, h2_ref, h2t_ref):
        e, et = _head_mats()
        la = l1_ref[...]
        lb = _permute_f32(pt4_ref[...], _load_classes(l2_ref, 4))
        lc = _permute_f32(pt16_ref[...], _load_classes(l3_ref, 16))
        mx = jnp.maximum(jnp.maximum(la, lb), lc)
        ea, eb, ec = jnp.exp(la - mx), jnp.exp(lb - mx), jnp.exp(lc - mx)
        tot = ea + eb + ec
        lse = mx + jnp.log(tot)
        lse_ref[...] = lse
        _store_classes(lse4_ref, _permute_f32(p4_ref[...], lse), 4)
        _store_classes(lse16_ref, _permute_f32(p16_ref[...], lse), 16)
        ob = (o1_ref[...] * _hexp(ea / tot, e)
              + _permute_f32(pt4_ref[...], _load_classes(o2_ref, 4)) * _hexp(eb / tot, e)
              + _permute_f32(pt16_ref[...], _load_classes(o3_ref, 16)) * _hexp(ec / tot, e))
        ob_ref[...] = ob

        def rms(o, gain):
            rr = lax.rsqrt(_hsum(o * o, et) * (1.0 / HEAD_DIM) + RMS_EPS)
            return o * _hexp(rr, e) * gain

        merged = jnp.concatenate([rms(oa_ref[...], ga_ref[...]), rms(ob, gb_ref[...])], axis=1).astype(BF16)
        mg_ref[...] = merged
        mix = _nn(merged, w_ref[...])
        mix_ref[...] = mix.astype(BF16)
        r1 = ALPHA * x_ref[...] + ada_ref[0, 2:3, :] * mix
        d = r1 - jnp.mean(r1, axis=1, keepdims=True)
        rstd = lax.rsqrt(jnp.mean(d * d, axis=1, keepdims=True) + LN_EPS)
        xh = d * rstd
        xh_ref[...] = xh
        rs_ref[...] = jnp.broadcast_to(rstd, (tm, LANES))
        x1 = xh * g_ref[...] + b_ref[...]
        h2 = x1 * (1.0 + ada_ref[0, 4:5, :]) + ada_ref[0, 3:4, :]
        h2_ref[...] = h2.astype(BF16)
        h2t_ref[0] = h2.T.astype(BF16)

    tok = lambda w: pl.BlockSpec((tm, w), lambda i: (i, 0))
    vec = lambda w: pl.BlockSpec((1, w), lambda i: (0, 0))
    whole = lambda a: pl.BlockSpec(a.shape, lambda i: (0, 0))
    classes = lambda a, d: a.reshape(t_all // seq * d, seq // d, a.shape[-1])
    return pl.pallas_call(
        body, name="mix_out", grid=(t_all // tm,),
        in_specs=[tok(WIDTH), tok(WIDTH), _class_spec(4, WIDTH, nts), _class_spec(16, WIDTH, nts), tok(LANES),
                  _class_spec(4, LANES, nts), _class_spec(16, LANES, nts), vec(WIDTH), vec(WIDTH), whole(w_out),
                  tok(D_MODEL), pl.BlockSpec((1, 6, D_MODEL), lambda i: (i // nts, 0, 0)), vec(D_MODEL), vec(D_MODEL)]
        + [whole(p) for p in perms],
        out_specs=[tok(WIDTH), tok(LANES), _class_spec(4, LANES, nts), _class_spec(16, LANES, nts), tok(D_MODEL),
                   tok(D_MODEL), tok(D_MODEL), tok(LANES), tok(D_MODEL), pl.BlockSpec((1, D_MODEL, tm), lambda i: (i // (FFN_TM // tm), 0, i % (FFN_TM // tm)))],
        out_shape=[jax.ShapeDtypeStruct((t_all, WIDTH), F32), jax.ShapeDtypeStruct((t_all, LANES), F32),
                   _class_shape(t_all, seq, 4, LANES, F32), _class_shape(t_all, seq, 16, LANES, F32),
                   jax.ShapeDtypeStruct((t_all, D_MODEL), BF16), jax.ShapeDtypeStruct((t_all, D_MODEL), BF16),
                   jax.ShapeDtypeStruct((t_all, D_MODEL), F32), jax.ShapeDtypeStruct((t_all, LANES), F32),
                   jax.ShapeDtypeStruct((t_all, D_MODEL), BF16),
                   jax.ShapeDtypeStruct((t_all // FFN_TM, D_MODEL, FFN_TM), BF16)],
        compiler_params=_params(),
    )(oa, o3[0], classes(o3[1], 4), classes(o3[2], 16), l3[0], classes(l3[1], 4), classes(l3[2], 16), gn_a, gn_b,
      w_out, x, ada3, ln_g, ln_b, *perms)


def _mix_out_bwd(dmix, w_out, oa, ob, gn_a, gn_b, perms, seq):
    t_all = dmix.shape[0]
    tm = TOK_TM
    nts = seq // tm

    def body(dm_ref, w_ref, oa_ref, ob_ref, ga_ref, gb_ref, p4_ref, p16_ref, doa_ref, dob_ref, dob4_ref, dob16_ref,
             dla_ref, dlb_ref, dlb4_ref, dlb16_ref, acc_ref):
        @pl.when(pl.program_id(0) == 0)
        def _():
            acc_ref[...] = jnp.zeros_like(acc_ref)
        e, et = _head_mats()
        dmg = _nt(dm_ref[...], w_ref[...])

        def group(o, dn, gain):
            rr = lax.rsqrt(_hsum(o * o, et) * (1.0 / HEAD_DIM) + RMS_EPS)
            re = _hexp(rr, e)
            dgain = jnp.sum(dn * o * re, axis=0, keepdims=True)
            dxn = dn * gain
            tt = _hsum(dxn * o, et) * (rr * rr * rr) * (1.0 / HEAD_DIM)
            do = re * dxn - o * _hexp(tt, e)
            return do, _hsum(do * o, et), dgain

        doa, dla, dga = group(oa_ref[...], dmg[:, :WIDTH], ga_ref[...])
        dob, dlb, dgb = group(ob_ref[...], dmg[:, WIDTH:], gb_ref[...])
        dob = dob.astype(BF16)
        doa_ref[...] = doa.astype(BF16)
        dob_ref[...] = dob
        _store_classes(dob4_ref, _nn(p4_ref[...], dob).astype(BF16), 4)
        _store_classes(dob16_ref, _nn(p16_ref[...], dob).astype(BF16), 16)
        dla_ref[...] = dla
        dlb_ref[...] = dlb
        _store_classes(dlb4_ref, _permute_f32(p4_ref[...], dlb), 4)
        _store_classes(dlb16_ref, _permute_f32(p16_ref[...], dlb), 16)
        acc_ref[0:1, :] += jnp.concatenate([dga, dgb], axis=1)

    tok = lambda w: pl.BlockSpec((tm, w), lambda i: (i, 0))
    vec = lambda w: pl.BlockSpec((1, w), lambda i: (0, 0))
    return pl.pallas_call(
        body, name="mix_out_bwd", grid=(t_all // tm,),
        in_specs=[tok(D_MODEL), pl.BlockSpec(w_out.shape, lambda i: (0, 0)), tok(WIDTH), tok(WIDTH), vec(WIDTH),
                  vec(WIDTH), pl.BlockSpec(perms[0].shape, lambda i: (0, 0)),
                  pl.BlockSpec(perms[1].shape, lambda i: (0, 0))],
        out_specs=[tok(WIDTH), tok(WIDTH), _class_spec(4, WIDTH, nts), _class_spec(16, WIDTH, nts), tok(LANES),
                   tok(LANES), _class_spec(4, LANES, nts), _class_spec(16, LANES, nts),
                   pl.BlockSpec((8, D_MODEL), lambda i: (0, 0))],
        out_shape=[jax.ShapeDtypeStruct((t_all, WIDTH), BF16), jax.ShapeDtypeStruct((t_all, WIDTH), BF16),
                   _class_shape(t_all, seq, 4, WIDTH, BF16), _class_shape(t_all, seq, 16, WIDTH, BF16),
                   jax.ShapeDtypeStruct((t_all, LANES), F32), jax.ShapeDtypeStruct((t_all, LANES), F32),
                   _class_shape(t_all, seq, 4, LANES, F32), _class_shape(t_all, seq, 16, LANES, F32),
                   jax.ShapeDtypeStruct((8, D_MODEL), F32)],
        compiler_params=_params(),
    )(dmix, w_out, oa, ob, gn_a, gn_b, perms[0], perms[1])


def _inproj_bwd(dqt, dka, dva, dil1, dil4, dil16, dfa16, pos, wqkv, wf16, freq, perms, dr1, x, ada3, seq):
    t_all = x.shape[0]
    tm = TOK_TM
    nts = seq // tm

    def body(dqt_ref, dka_ref, dva_ref, q1_ref, k1_ref, v1_ref, q4_ref, k4_ref, v4_ref, q16_ref, k16_ref, v16_ref,
             dfa_ref, pos_ref, w_ref, wf_ref, fr_ref, pt4_ref, pt16_ref, dr1_ref, x_ref, ada_ref, gx_ref, dz_ref,
             acc_ref):
        i = pl.program_id(0)

        @pl.when(i == 0)
        def _():
            acc_ref[...] = jnp.zeros_like(acc_ref)
        tabs = _rope_tabs(pos_ref, fr_ref, -1.0)
        dz_ref[:, :WIDTH] = dqt_ref[...].T.astype(BF16)
        dz_ref[:, WIDTH:2 * WIDTH] = dka_ref[...]
        dz_ref[:, 2 * WIDTH:3 * WIDTH] = dva_ref[...]
        for t, (n1, n4, n16) in enumerate(((q1_ref, q4_ref, q16_ref), (k1_ref, k4_ref, k16_ref),
                                           (v1_ref, v4_ref, v16_ref))):
            tot = (n1[...].astype(F32) + _nn(pt4_ref[...], _load_classes(n4, 4))
                   + _nn(pt16_ref[...], _load_classes(n16, 16)))
            if t < 2:
                tot = _rope(tot, tabs)
            dz_ref[:, (3 + t) * WIDTH:(4 + t) * WIDTH] = tot.astype(BF16)
        dh1 = _tn(dfa_ref[...], wf_ref[...])
        for n in range(6):
            cs = slice(n * WIDTH, (n + 1) * WIDTH)
            dh1 = dh1 + _nt(dz_ref[:, cs], w_ref[:, cs])
        xv = x_ref[...]
        gx_ref[...] = ALPHA * dr1_ref[...] + dh1 * (1.0 + ada_ref[0, 1:2, :])
        b = i // nts
        acc_ref[pl.ds(b, 1), :] += jnp.sum(dh1 * xv, axis=0, keepdims=True)
        acc_ref[pl.ds(8 + b, 1), :] += jnp.sum(dh1, axis=0, keepdims=True)

    tok = lambda w: pl.BlockSpec((tm, w), lambda i: (i, 0))
    whole = lambda a: pl.BlockSpec(a.shape, lambda i: (0, 0))
    classes = lambda a, d: a.reshape(t_all // seq * d, seq // d, a.shape[-1])
    return pl.pallas_call(
        body, name="inproj_bwd", grid=(t_all // tm,),
        in_specs=[pl.BlockSpec((WIDTH, tm), lambda i: (i // nts, i % nts)), tok(WIDTH), tok(WIDTH)]
        + [tok(WIDTH)] * 3 + [_class_spec(4, WIDTH, nts)] * 3 + [_class_spec(16, WIDTH, nts)] * 3
        + [pl.BlockSpec((16, tm), lambda i: (0, i)), tok(1), whole(wqkv), whole(wf16),
           pl.BlockSpec((1, LANES), lambda i: (0, 0)), whole(perms[2]), whole(perms[3]), tok(D_MODEL), tok(D_MODEL),
           pl.BlockSpec((1, 6, D_MODEL), lambda i: (i // nts, 0, 0))],
        out_specs=[tok(D_MODEL), tok(6 * WIDTH), pl.BlockSpec((16, D_MODEL), lambda i: (0, 0))],
        out_shape=[jax.ShapeDtypeStruct((t_all, D_MODEL), F32), jax.ShapeDtypeStruct((t_all, 6 * WIDTH), BF16),
                   jax.ShapeDtypeStruct((16, D_MODEL), F32)],
        compiler_params=_params(),
    )(dqt, dka, dva, *dil1, *[classes(a, 4) for a in dil4], *[classes(a, 16) for a in dil16], dfa16, pos, wqkv, wf16,
      freq, perms[2], perms[3], dr1, x, ada3)


FFN_TM = 1024
FFN_TN = 256
HALO = 8


FFN_CHUNK = 64


def _conv(cat_ref, w_ref, b_ref, start, rows, halo=HALO):
    return (b_ref[...] + w_ref[0:1, :] * cat_ref[pl.ds(start + halo - 2, rows), :]
            + w_ref[1:2, :] * cat_ref[pl.ds(start + halo - 1, rows), :]
            + w_ref[2:3, :] * cat_ref[pl.ds(start + halo, rows), :])


def _ffn_up_gate(h2, w_up, conv_w, conv_b, seq):
    t_all = h2.shape[0]
    tm, tn = FFN_TM, FFN_TN
    nc = D_FF // tn
    nts = seq // tm
    pre = 16

    def body(h_ref, hp_ref, wua_ref, wug_ref, wa_ref, wg_ref, ba_ref, bg_ref, ua_ref, ug_ref, o_ref, ca_ref, cg_ref):
        first = (pl.program_id(1) % nts) == 0
        hcat = jnp.concatenate([hp_ref[...], h_ref[...]], axis=0)
        zero = jnp.zeros((pre, tn), F32)
        for w_ref, cat, u_ref in ((wua_ref, ca_ref, ua_ref), (wug_ref, cg_ref, ug_ref)):
            ub = _nn(hcat, w_ref[...]).astype(BF16)
            ue = ub.astype(F32)
            cat[0:pre, :] = jnp.where(first, zero, ue[0:pre])
            cat[pre:, :] = ue[pre:]
            u_ref[...] = ub[pre:]
        for c0 in range(0, tm, FFN_CHUNK):
            ya = _conv(ca_ref, wa_ref, ba_ref, c0, FFN_CHUNK, pre)
            yg = _conv(cg_ref, wg_ref, bg_ref, c0, FFN_CHUNK, pre)
            o_ref[c0:c0 + FFN_CHUNK, :] = (yg * jax.nn.sigmoid(yg) * ya).astype(BF16)

    vec = lambda r, off: pl.BlockSpec((r, tn), lambda n, t: (0, n + off))
    wcol = lambda off: pl.BlockSpec((D_MODEL, tn), lambda n, t: (0, n + off))
    tile = pl.BlockSpec((tm, tn), lambda n, t: (t, n))
    return pl.pallas_call(
        body, name="ffn_up_gate", grid=(nc, t_all // tm),
        in_specs=[pl.BlockSpec((tm, D_MODEL), lambda n, t: (t, 0)),
                  pl.BlockSpec((pre, D_MODEL), lambda n, t: (jnp.maximum(t * (tm // pre) - 1, 0), 0)),
                  wcol(0), wcol(nc), vec(3, 0), vec(3, nc), vec(1, 0), vec(1, nc)],
        out_specs=[tile, tile, tile],
        out_shape=[jax.ShapeDtypeStruct((t_all, D_FF), BF16)] * 3,
        scratch_shapes=[pltpu.VMEM((tm + pre, tn), F32)] * 2, compiler_params=_params(),
    )(h2, h2, w_up, w_up, conv_w, conv_w, conv_b, conv_b)


def _ffn_gate_bwd(u_a, u_g, dfi, conv_w, conv_b, h2t, seq):
    t_all = u_a.shape[0]
    tm, tn = FFN_TM, FFN_TN
    nc = D_FF // tn
    nts = seq // tm

    def body(ua_ref, uap_ref, uan_ref, ug_ref, ugp_ref, ugn_ref, df_ref, dfn_ref, wa_ref, wg_ref, ba_ref, bg_ref, h_ref,
             dua_ref, dug_ref, acca_ref, accg_ref, dwa_ref, dwg_ref, ca_ref, cg_ref, ya_ref, yg_ref):
        t = pl.program_id(1)
        first = (t % nts) == 0
        last = (t % nts) == nts - 1

        @pl.when(t == 0)
        def _():
            acca_ref[...] = jnp.zeros_like(acca_ref)
            accg_ref[...] = jnp.zeros_like(accg_ref)
            dwa_ref[...] = jnp.zeros_like(dwa_ref)
            dwg_ref[...] = jnp.zeros_like(dwg_ref)
        zero = jnp.zeros((HALO, tn), F32)
        for cat, cur, prv, nxt in ((ca_ref, ua_ref, uap_ref, uan_ref), (cg_ref, ug_ref, ugp_ref, ugn_ref)):
            cat[0:HALO, :] = jnp.where(first, zero, prv[...].astype(F32)[HALO:])
            cat[HALO:HALO + tm, :] = cur[...].astype(F32)
            cat[HALO + tm:, :] = nxt[...].astype(F32)[:HALO]
        ch = FFN_CHUNK
        sums = [[jnp.zeros((1, tn), F32) for _ in range(4)] for _ in range(2)]
        for ci, c0 in enumerate(range(0, tm, ch)):
            ya = _conv(ca_ref, wa_ref, ba_ref, c0, ch + HALO)
            yg = _conv(cg_ref, wg_ref, bg_ref, c0, ch + HALO)
            if c0 + ch < tm:
                beyond = df_ref[c0 + ch:c0 + ch + 16, :].astype(F32)[:HALO]
            else:
                beyond = jnp.where(last, 0.0, dfn_ref[...].astype(F32)[:HALO])
            dfe = jnp.concatenate([df_ref[c0:c0 + ch, :].astype(F32), beyond], axis=0)
            sg = jax.nn.sigmoid(yg)
            ya_ref[ci] = dfe * (yg * sg)
            yg_ref[ci] = dfe * ya * (sg * (1.0 + yg * (1.0 - sg)))
            for half, (dy, cat, w_ref, du_ref) in enumerate(((ya_ref, ca_ref, wa_ref, dua_ref),
                                                             (yg_ref, cg_ref, wg_ref, dug_ref))):
                d0 = dy[ci, 0:ch, :]
                du = (w_ref[2:3, :] * d0 + w_ref[1:2, :] * dy[ci, pl.ds(1, ch), :]
                      + w_ref[0:1, :] * dy[ci, pl.ds(2, ch), :])
                du_ref[c0:c0 + ch, :] = du.astype(BF16)
                for k in range(3):
                    sums[half][k] += jnp.sum(d0 * cat[pl.ds(c0 + HALO - 2 + k, ch), :], axis=0, keepdims=True)
                sums[half][3] += jnp.sum(d0, axis=0, keepdims=True)
        for half, acc in enumerate((acca_ref, accg_ref)):
            for k in range(4):
                acc[k:k + 1, :] += sums[half][k]
        ht = h_ref[0]
        dwa_ref[...] += _nn(ht, dua_ref[...])
        dwg_ref[...] += _nn(ht, dug_ref[...])

    nrow = t_all // 16
    cur = pl.BlockSpec((tm, tn), lambda n, t: (t, n))
    prev = pl.BlockSpec((16, tn), lambda n, t: (jnp.maximum(t * (tm // 16) - 1, 0), n))
    nxt = pl.BlockSpec((16, tn), lambda n, t: (jnp.minimum((t + 1) * (tm // 16), nrow - 1), n))
    vec = lambda r, off: pl.BlockSpec((r, tn), lambda n, t: (0, n + off))
    acc = pl.BlockSpec((8, tn), lambda n, t: (0, n))
    dw = pl.BlockSpec((D_MODEL, tn), lambda n, t: (0, n))
    return pl.pallas_call(
        body, name="ffn_gate_bwd", grid=(nc, t_all // tm),
        in_specs=[cur, prev, nxt, cur, prev, nxt, cur, nxt, vec(3, 0), vec(3, nc), vec(1, 0), vec(1, nc),
                  pl.BlockSpec((1, D_MODEL, tm), lambda n, t: (t, 0, 0))],
        out_specs=[cur, cur, acc, acc, dw, dw],
        out_shape=[jax.ShapeDtypeStruct((t_all, D_FF), BF16), jax.ShapeDtypeStruct((t_all, D_FF), BF16),
                   jax.ShapeDtypeStruct((8, D_FF), F32), jax.ShapeDtypeStruct((8, D_FF), F32),
                   jax.ShapeDtypeStruct((D_MODEL, D_FF), F32), jax.ShapeDtypeStruct((D_MODEL, D_FF), F32)],
        scratch_shapes=[pltpu.VMEM((tm + 2 * HALO, tn), F32)] * 2
        + [pltpu.VMEM((tm // FFN_CHUNK, FFN_CHUNK + HALO, tn), F32)] * 2,
        compiler_params=_params(),
    )(u_a, u_a, u_a, u_g, u_g, u_g, dfi, dfi, conv_w, conv_w, conv_b, conv_b, h2t)


def _ffn_down(ffn_in, w_down, xh1, ln1_g, ln1_b, ada3, ln2_g, ln2_b, target, seq):
    t_all = xh1.shape[0]
    tm = 256
    nts = seq // tm

    def body(f_ref, w_ref, xh_ref, g1_ref, b1_ref, ada_ref, g2_ref, b2_ref, tg_ref, dr2_ref, acc_ref):
        i = pl.program_id(0)

        @pl.when(i == 0)
        def _():
            acc_ref[...] = jnp.zeros_like(acc_ref)
        ffn = _nn(f_ref[...], w_ref[...])
        x1 = xh_ref[...] * g1_ref[...] + b1_ref[...]
        r2 = ALPHA * x1 + ada_ref[0, 5:6, :] * ffn
        d = r2 - jnp.mean(r2, axis=1, keepdims=True)
        rstd = lax.rsqrt(jnp.mean(d * d, axis=1, keepdims=True) + LN_EPS)
        xh2 = d * rstd
        diff = xh2 * g2_ref[...] + b2_ref[...] - tg_ref[...]
        dy = diff * (1.0 / D_MODEL)
        dr2 = _layer_norm_bwd(dy * g2_ref[...], xh2, rstd)
        dr2_ref[...] = dr2
        acc_ref[0:1, :] += jnp.sum(dy * xh2, axis=0, keepdims=True)
        acc_ref[1:2, :] += jnp.sum(dy, axis=0, keepdims=True)
        acc_ref[2:3, :] += jnp.sum(diff * diff, axis=0, keepdims=True) * (0.5 / D_MODEL)
        acc_ref[pl.ds(8 + i // nts, 1), :] += jnp.sum(dr2 * ffn, axis=0, keepdims=True)

    tok = lambda w: pl.BlockSpec((tm, w), lambda i: (i, 0))
    vec = pl.BlockSpec((1, D_MODEL), lambda i: (0, 0))
    return pl.pallas_call(
        body, name="ffn_down", grid=(t_all // tm,),
        in_specs=[tok(D_FF), pl.BlockSpec(w_down.shape, lambda i: (0, 0)), tok(D_MODEL), vec, vec,
                  pl.BlockSpec((1, 6, D_MODEL), lambda i: (i // nts, 0, 0)), vec, vec, tok(D_MODEL)],
        out_specs=[tok(D_MODEL), pl.BlockSpec((16, D_MODEL), lambda i: (0, 0))],
        out_shape=[jax.ShapeDtypeStruct((t_all, D_MODEL), F32), jax.ShapeDtypeStruct((16, D_MODEL), F32)],
        compiler_params=_params(),
    )(ffn_in, w_down, xh1, ln1_g, ln1_b, ada3, ln2_g, ln2_b, target)


def _ffn_down_bwd(dr2, ada3, w_down, seq):
    t_all = dr2.shape[0]
    tm = 256
    nts = seq // tm

    def body(d_ref, ada_ref, w_ref, dffn_ref, dfi_ref):
        dffn = (d_ref[...] * ada_ref[0, 5:6, :]).astype(BF16)
        dffn_ref[...] = dffn
        dfi_ref[...] = _nt(dffn, w_ref[...]).astype(BF16)

    tok = lambda w: pl.BlockSpec((tm, w), lambda i: (i, 0))
    return pl.pallas_call(
        body, name="ffn_down_bwd", grid=(t_all // tm,),
        in_specs=[tok(D_MODEL), pl.BlockSpec((1, 6, D_MODEL), lambda i: (i // nts, 0, 0)),
                  pl.BlockSpec(w_down.shape, lambda i: (0, 0))],
        out_specs=[tok(D_MODEL), tok(D_FF)],
        out_shape=[jax.ShapeDtypeStruct((t_all, D_MODEL), BF16), jax.ShapeDtypeStruct((t_all, D_FF), BF16)],
        compiler_params=_params(),
    )(dr2, ada3, w_down)


def _ffn_up_bwd(du_a, du_g, w_up, dr2, xh1, rs1, mix, ada3, ln1_g, ln1_b, seq):
    t_all = dr2.shape[0]
    tm = 256
    nts = seq // tm

    def body(da_ref, dg_ref, w_ref, dr2_ref, xh_ref, rs_ref, mix_ref, ada_ref, g_ref, b_ref, dr1_ref, dmix_ref,
             acc_ref):
        i = pl.program_id(0)

        @pl.when(i == 0)
        def _():
            acc_ref[...] = jnp.zeros_like(acc_ref)
        dh2 = _nt(da_ref[...], w_ref[:, :D_FF]) + _nt(dg_ref[...], w_ref[:, D_FF:])
        xh = xh_ref[...]
        x1 = xh * g_ref[...] + b_ref[...]
        dx1 = ALPHA * dr2_ref[...] + dh2 * (1.0 + ada_ref[0, 4:5, :])
        dr1 = _layer_norm_bwd(dx1 * g_ref[...], xh, rs_ref[:, 0:1])
        dr1_ref[...] = dr1
        dmix_ref[...] = (dr1 * ada_ref[0, 2:3, :]).astype(BF16)
        b = i // nts
        acc_ref[0:1, :] += jnp.sum(dx1 * xh, axis=0, keepdims=True)
        acc_ref[1:2, :] += jnp.sum(dx1, axis=0, keepdims=True)
        acc_ref[pl.ds(8 + b, 1), :] += jnp.sum(dh2 * x1, axis=0, keepdims=True)
        acc_ref[pl.ds(16 + b, 1), :] += jnp.sum(dh2, axis=0, keepdims=True)
        acc_ref[pl.ds(24 + b, 1), :] += jnp.sum(dr1 * mix_ref[...].astype(F32), axis=0, keepdims=True)

    tok = lambda w: pl.BlockSpec((tm, w), lambda i: (i, 0))
    vec = pl.BlockSpec((1, D_MODEL), lambda i: (0, 0))
    return pl.pallas_call(
        body, name="ffn_up_bwd", grid=(t_all // tm,),
        in_specs=[tok(D_FF), tok(D_FF), pl.BlockSpec(w_up.shape, lambda i: (0, 0)), tok(D_MODEL), tok(D_MODEL),
                  tok(LANES), tok(D_MODEL), pl.BlockSpec((1, 6, D_MODEL), lambda i: (i // nts, 0, 0)), vec, vec],
        out_specs=[tok(D_MODEL), tok(D_MODEL), pl.BlockSpec((32, D_MODEL), lambda i: (0, 0))],
        out_shape=[jax.ShapeDtypeStruct((t_all, D_MODEL), F32), jax.ShapeDtypeStruct((t_all, D_MODEL), BF16),
                   jax.ShapeDtypeStruct((32, D_MODEL), F32)],
        compiler_params=_params(),
    )(du_a, du_g, w_up, dr2, xh1, rs1, mix, ada3, ln1_g, ln1_b)


def _rows(a):
    return a[:, :N_HEADS].T


def _rope_freq():
    f = np.float32(ROPE_THETA) ** (-np.arange(0, ROPE_DIMS, 2, dtype=np.float32) / np.float32(ROPE_DIMS))
    return jnp.asarray(np.tile(f.astype(np.float32), LANES // (ROPE_DIMS // 2))[None, :])


def _local_step(x, positions, target, ada3, w_in, b_fgate, gn_a, gn_b, ln1_g, ln1_b, conv_b, ln2_g, ln2_b,
                late_shards):
    nbat, seq, _ = x.shape
    t_all = nbat * seq
    xf = x.reshape(t_all, D_MODEL)
    tg = target.reshape(t_all, D_MODEL)
    pos = positions.reshape(t_all, 1)
    freq = _rope_freq()

    wqkv = jnp.concatenate([w_in[:, :3 * WIDTH], w_in[:, 3 * WIDTH + N_HEADS:]], axis=1)
    wf16 = jnp.zeros((16, D_MODEL), BF16).at[:N_HEADS].set(w_in[:, 3 * WIDTH:3 * WIDTH + N_HEADS].T)
    bf = b_fgate.reshape(N_HEADS, 1)

    perms = [_perm_matrix(TOK_TM, d, tr) for tr in (False, True) for d in DILATIONS[1:]]
    h1, za, zb1, zb4, zb16, vt, fa_t = _inproj(xf, ada3, pos, wqkv, wf16, freq, perms, seq)
    zbs = [zb1, zb4.reshape(t_all, 3 * WIDTH), zb16.reshape(t_all, 3 * WIDTH)]
    f_row = _fgate_fwd(fa_t, bf, seq)
    f_col = jnp.zeros((t_all, LANES), F32).at[:, :N_HEADS].set(f_row.T)
    oa, lse_row_a, gathered = _fox_fwd(za, vt, f_col, seq, [late_shards[n] for n in LATE])
    w_out, w_up, conv_w, w_down = (_full_from_gathered(n, g) for n, g in zip(LATE, gathered))
    o3, l3 = zip(*[_dil_fwd(zb, seq, d) for zb, d in zip(zbs, DILATIONS)])
    ob, lse_b, lse_b4, lse_b16, merged, mix, xh1, rs1, h2, h2t = _mix_out(oa, o3, l3, gn_a, gn_b, w_out, xf, ada3, ln1_g,
                                                                      ln1_b, perms, seq)
    u_a, u_g, ffn_in = _ffn_up_gate(h2, w_up, conv_w, conv_b, seq)
    dr2, acc2 = _ffn_down(ffn_in, w_down, xh1, ln1_g, ln1_b, ada3, ln2_g, ln2_b, tg, seq)

    dffn, dfi = _ffn_down_bwd(dr2, ada3, w_down, seq)
    d_w_down = _matmul_tn(dffn, ffn_in, 512, 512, "dw_down").T
    du_a, du_g, acc_ca, acc_cg, dw_up_a, dw_up_g = _ffn_gate_bwd(u_a, u_g, dfi, conv_w, conv_b, h2t, seq)
    dr1, dmix, acc1 = _ffn_up_bwd(du_a, du_g, w_up, dr2, xh1, rs1, mix, ada3, ln1_g, ln1_b, seq)
    d_w_up = jnp.concatenate([dw_up_a, dw_up_g], axis=1)

    doa, dob, dob4, dob16, dl_a, dl_b, dl_b4, dl_b16, acc_gn = _mix_out_bwd(dmix, w_out, oa, ob, gn_a, gn_b, perms, seq)
    d_w_out = _matmul_tn(merged, dmix, 512, 512, "dw_out")
    late_grads = dict(w_out=d_w_out, w_up=d_w_up, conv_w=jnp.concatenate([acc_ca[0:3], acc_cg[0:3]], axis=1),
                      w_down=d_w_down)
    dka, dva, df_k, dqt, df_q, late_parts = _fox_bwd(za, doa, f_col, lse_row_a, _rows(dl_a), seq,
                                                     [_payload(n, _dest_major(n, late_grads[n])) for n in LATE])
    dfa_t, dbf = _fgate_bwd(_rows(df_k) + df_q, fa_t, bf, seq)
    flat = lambda a: a.reshape(t_all, a.shape[-1])
    dil = []
    for zb, d, do, lse, dl in zip(zbs, DILATIONS, (dob, flat(dob4), flat(dob16)),
                                  (lse_b, flat(lse_b4), flat(lse_b16)), (dl_b, flat(dl_b4), flat(dl_b16))):
        dil.append(_dil_bwd(zb, do, lse, dl, seq, d))
    dfa16 = jnp.zeros((16, t_all), BF16).at[:N_HEADS].set(dfa_t.astype(BF16))
    grad_x, dz, acc0 = _inproj_bwd(dqt, dka, dva, dil[0], dil[1], dil[2], dfa16, pos, wqkv, wf16, freq, perms, dr1, xf,
                                   ada3, seq)
    d_wqkv = _matmul_tn(h1, dz, 512, 512, "dw_in")
    d_wf = _matmul_rows(dfa16, h1, 512, "dw_fgate")[:N_HEADS].T
    d_w_in = jnp.concatenate([d_wqkv[:, :3 * WIDTH], d_wf, d_wqkv[:, 3 * WIDTH:]], axis=1)

    dada = jnp.concatenate([acc0[8:8 + nbat], acc0[:nbat], acc1[24:24 + nbat], acc1[16:16 + nbat], acc1[8:8 + nbat],
                            acc2[8:8 + nbat]], axis=1)

    grads = dict(
        dada=dada, b_ada=jnp.sum(dada, axis=0, keepdims=True), w_in=d_w_in, b_fgate=dbf[:, 0][None, :],
        gn_a=acc_gn[0:1, :WIDTH], gn_b=acc_gn[0:1, WIDTH:], ln1_g=acc1[0:1], ln1_b=acc1[1:2],
        conv_b=jnp.concatenate([acc_ca[3:4], acc_cg[3:4]], axis=1), ln2_g=acc2[0:1], ln2_b=acc2[1:2])
    return acc2[2:3], grad_x.reshape(x.shape), grads, dict(zip(LATE, late_parts))


LATE = ("w_out", "w_up", "conv_w", "w_down")
BIG = ("w_ada", "w_in") + LATE
COLUMN_SHARDED = ("w_ada", "w_in", "w_up", "conv_w")


def _payload(name, a):
    return a if name == "conv_w" else a.astype(BF16)
SMALL = ("b_ada", "b_fgate", "gn_a", "gn_b", "ln1_g", "ln1_b", "conv_b", "ln2_g", "ln2_b")
ADAM_ROWS = dict(w_ada=256, w_in=256, w_out=128, w_up=256, conv_w=3, w_down=176)
SMALL_ROWS = 24


def _full_from_gathered(name, g):
    if name in COLUMN_SHARDED:
        return g.transpose(1, 0, 2).reshape(g.shape[1], N_DEV * g.shape[2])
    return g.reshape(N_DEV * g.shape[1], g.shape[2])


def _dest_major(name, full):
    if name in COLUMN_SHARDED:
        r, cfull = full.shape
        return full.reshape(r, N_DEV, cfull // N_DEV).transpose(1, 0, 2)
    return full.reshape(N_DEV, full.shape[0] // N_DEV, full.shape[1])


def _pack_small(vals, extra=None):
    parts = [vals[n].reshape(-1) for n in SMALL]
    if extra is not None:
        parts.append(extra.reshape(-1))
    flat = jnp.concatenate(parts)
    return jnp.pad(flat, (0, SMALL_ROWS * D_MODEL - flat.shape[0])).reshape(SMALL_ROWS, D_MODEL)


def _unpack_small(packed, like):
    flat = packed.reshape(-1)
    out, off = {}, 0
    for n in SMALL:
        size = like[n].size
        out[n] = flat[off:off + size].reshape(like[n].shape)
        off += size
    return out, flat[off:off + D_MODEL]


def kernel(x, c, positions, w_ada, b_ada, w_in, b_fgate, gn_a, gn_b, w_out, ln1_g, ln1_b, w_up, conv_w, conv_b, w_down, ln2_g, ln2_b, loss_target, m_w_ada, m_b_ada, m_w_in, m_b_fgate, m_gn_a, m_gn_b, m_w_out, m_ln1_g, m_ln1_b, m_w_up, m_conv_w, m_conv_b, m_w_down, m_ln2_g, m_ln2_b, v_w_ada, v_b_ada, v_w_in, v_b_fgate, v_gn_a, v_gn_b, v_w_out, v_ln1_g, v_ln1_b, v_w_up, v_conv_w, v_conv_b, v_w_down, v_ln2_g, v_ln2_b):
    w = dict(w_ada=w_ada[0], b_ada=b_ada, w_in=w_in[0], b_fgate=b_fgate, gn_a=gn_a, gn_b=gn_b, w_out=w_out[0],
             ln1_g=ln1_g, ln1_b=ln1_b, w_up=w_up[0], conv_w=conv_w[0], conv_b=conv_b, w_down=w_down[0], ln2_g=ln2_g,
             ln2_b=ln2_b)
    m = dict(w_ada=m_w_ada[0], b_ada=m_b_ada, w_in=m_w_in[0], b_fgate=m_b_fgate, gn_a=m_gn_a, gn_b=m_gn_b,
             w_out=m_w_out[0], ln1_g=m_ln1_g, ln1_b=m_ln1_b, w_up=m_w_up[0], conv_w=m_conv_w[0], conv_b=m_conv_b,
             w_down=m_w_down[0], ln2_g=m_ln2_g, ln2_b=m_ln2_b)
    v = dict(w_ada=v_w_ada[0], b_ada=v_b_ada, w_in=v_w_in[0], b_fgate=v_b_fgate, gn_a=v_gn_a, gn_b=v_gn_b,
             w_out=v_w_out[0], ln1_g=v_ln1_g, ln1_b=v_ln1_b, w_up=v_w_up[0], conv_w=v_conv_w[0], conv_b=v_conv_b,
             w_down=v_w_down[0], ln2_g=v_ln2_g, ln2_b=v_ln2_b)

    nbat = x.shape[0]
    me = 4 * lax.axis_index("x") + 2 * lax.axis_index("y") + lax.axis_index("c")
    ada_cols = w["w_ada"].shape[1]

    c_all, w_in_all = _gather_two_level([c, _payload("w_in", w["w_in"])], "weight_gather")
    c_all = c_all.reshape(N_DEV * nbat, D_MODEL)
    ada_mine = _ada_fwd(c_all, w["w_ada"], lax.dynamic_slice(b_ada, (0, me * ada_cols), (1, ada_cols)))
    (ada_parts,) = _exchange([ada_mine.reshape(N_DEV, nbat, ada_cols)], [False], "ada_exchange")
    ada3 = ada_parts.transpose(1, 0, 2).reshape(nbat, 6, D_MODEL)

    loss_lanes, grad_x, g_local, parts = _local_step(
        x, positions, loss_target, ada3, _full_from_gathered("w_in", w_in_all), b_fgate, gn_a, gn_b, ln1_g, ln1_b,
        conv_b, ln2_g, ln2_b, {n: _payload(n, w[n]) for n in LATE})

    parts["w_in"], dada_all, small_all = _exchange(
        [_payload("w_in", _dest_major("w_in", g_local["w_in"])), g_local["dada"], _pack_small(g_local, loss_lanes)],
        [False, True, True], "grad_exchange")
    dada_cols = lax.dynamic_slice(dada_all.reshape(N_DEV * nbat, 6 * D_MODEL), (0, me * ada_cols),
                                  (N_DEV * nbat, ada_cols))
    parts["w_ada"] = _ada_bwd(c_all, dada_cols)[None]

    grad, delta, new_m, new_v = {}, {}, {}, {}
    for n in BIG:
        grad[n], delta[n], new_m[n], new_v[n] = (
            a[None] for a in _adamw(parts[n], w[n], m[n], v[n], ADAM_ROWS[n], "adamw_" + n))
    packed = _adamw(small_all, _pack_small(w), _pack_small(m), _pack_small(v), SMALL_ROWS, "adamw_small")
    for dst, pk in zip((grad, delta, new_m, new_v), packed):
        vals, lanes = _unpack_small(pk, w)
        dst.update(vals)
        if dst is grad:
            loss = jnp.sum(lanes)

    order = ("w_ada", "b_ada", "w_in", "b_fgate", "gn_a", "gn_b", "w_out", "ln1_g", "ln1_b", "w_up", "conv_w", "conv_b",
             "w_down", "ln2_g", "ln2_b")
    return (loss, grad_x, *[grad[n] for n in order], *[delta[n] for n in order], *[new_m[n] for n in order],
            *[new_v[n] for n in order])
```

```python
import functools

import numpy as np
import jax
import jax.numpy as jnp
from jax import lax
from jax.experimental import pallas as pl
from jax.experimental.pallas import tpu as pltpu

F32, BF16 = jnp.float32, jnp.bfloat16
MESH = pl.DeviceIdType.MESH
ANY = pl.BlockSpec(memory_space=pl.ANY)

D_MODEL = 1024
N_HEADS = 8
HEAD_DIM = 64
WIDTH = 512
D_FF = 2816
N_DEV = 8
ROPE_DIMS = 16
ROPE_THETA = 500000.0
ALPHA = 2.0 ** 0.25
LN_EPS = 1e-5
RMS_EPS = 1e-6
NEG = -1e30
Q_SCALE = 0.125
BLK = 128
LANES = 128
VMEM_LIMIT_BYTES = 56 * 1024 * 1024

ADAM_LR, ADAM_B1, ADAM_B2, ADAM_EPS, ADAM_WD, ADAM_STEP = 0.001, 0.9, 0.999, 1e-08, 0.01, 10


def _params(vmem=VMEM_LIMIT_BYTES):
    return pltpu.CompilerParams(vmem_limit_bytes=vmem)


def _nn(a, b):
    return jnp.dot(a, b, preferred_element_type=F32)


def _nt(a, b):
    return lax.dot_general(a, b, (((1,), (1,)), ((), ())), preferred_element_type=F32)


def _tn(a, b):
    return lax.dot_general(a, b, (((0,), (0,)), ((), ())), preferred_element_type=F32)


def _head_mats():
    r = lax.broadcasted_iota(jnp.int32, (LANES, WIDTH), 0)
    c = lax.broadcasted_iota(jnp.int32, (LANES, WIDTH), 1)
    e = ((c >> 6) == r).astype(BF16)
    r2 = lax.broadcasted_iota(jnp.int32, (WIDTH, LANES), 0)
    c2 = lax.broadcasted_iota(jnp.int32, (WIDTH, LANES), 1)
    et = ((r2 >> 6) == c2).astype(BF16)
    return e, et


def _split3(x):
    hi = x.astype(BF16)
    r = x - hi.astype(F32)
    mid = r.astype(BF16)
    return hi, mid, (r - mid.astype(F32)).astype(BF16)


def _hexp(w, e):
    return sum(_nn(part, e) for part in _split3(w)[:2])


def _hsum(x, et):
    return sum(_nn(part, et) for part in _split3(x)[:2])


def _perm_matrix(rows, d, transpose):
    i = np.arange(rows)
    j = (i % (rows // d)) * d + i // (rows // d)
    p = np.zeros((rows, rows), np.float32)
    p[i, j] = 1.0
    return jnp.asarray(p.T if transpose else p, BF16)


def _permute_f32(p, x):
    return sum(_nn(p, part) for part in _split3(x))


def _store_classes(ref, y, d):
    n = y.shape[0] // d
    for r in range(d):
        ref[r] = y[r * n:(r + 1) * n, :]


def _load_classes(ref, d):
    return jnp.concatenate([ref[r] for r in range(d)], axis=0)


def _rope_tabs(pos_ref, fr_ref, sign):
    ang = pos_ref[...].astype(F32) * fr_ref[...]
    lane = lax.broadcasted_iota(jnp.int32, ang.shape, 1) & (HEAD_DIM - 1)
    m1 = lane < ROPE_DIMS // 2
    m2 = (lane >= ROPE_DIMS // 2) & (lane < ROPE_DIMS)
    cos = jnp.cos(ang)
    sin = jnp.sin(ang) * sign
    return (jnp.where(m1 | m2, cos, 1.0), jnp.where(m1, -sin, 0.0), jnp.where(m2, sin, 0.0))


def _rope(z, tabs):
    c, s1, s2 = tabs
    parts = []
    for p in range(z.shape[1] // LANES):
        zp = z[:, LANES * p:LANES * (p + 1)]
        parts.append(zp * c + pltpu.roll(zp, LANES - 8, 1) * s1 + pltpu.roll(zp, 8, 1) * s2)
    return jnp.concatenate(parts, axis=1)


def _half_masks(rows):
    lane = lax.broadcasted_iota(jnp.int32, (rows, LANES), 1)
    lo = lane < HEAD_DIM
    return lo, jnp.logical_not(lo)


def _layer_norm_bwd(dxh, xh, rstd):
    m1 = jnp.mean(dxh, axis=1, keepdims=True)
    m2 = jnp.mean(dxh * xh, axis=1, keepdims=True)
    return rstd * (dxh - m1 - xh * m2)


def _coords():
    return lax.axis_index("x"), lax.axis_index("y"), lax.axis_index("c")


def _peer(x, y, c, k):
    return (1 - x if k & 4 else x, 1 - y if k & 2 else y, 1 - c if k & 1 else c)


def _comm_sems(n):
    return [pltpu.SemaphoreType.DMA((N_DEV - 1, n)), pltpu.SemaphoreType.DMA((N_DEV - 1, n)),
            pltpu.SemaphoreType.DMA((n,))]


def _comm_copies(ins, outs, to_all, sems):
    send_sems, recv_sems, local_sems = sems
    x, y, c = _coords()
    me = 4 * x + 2 * y + c
    copies = [pltpu.make_async_copy(ins[t] if to_all[t] else ins[t].at[me], outs[t].at[me], local_sems.at[t])
              for t in range(len(ins))]
    for k in range(1, N_DEV):
        px, py, pc = _peer(x, y, c, k)
        dest = 4 * px + 2 * py + pc
        for t in range(len(ins)):
            copies.append(pltpu.make_async_remote_copy(
                src_ref=ins[t] if to_all[t] else ins[t].at[dest], dst_ref=outs[t].at[me],
                send_sem=send_sems.at[k - 1, t], recv_sem=recv_sems.at[k - 1, t],
                device_id=(px, py, pc), device_id_type=MESH))
    return copies


def _comm_out_shapes(ins, to_all):
    return [jax.ShapeDtypeStruct(((N_DEV,) + a.shape) if ta else a.shape, a.dtype) for a, ta in zip(ins, to_all)]


def _exchange(ins, to_all, name):
    n = len(ins)

    def body(*refs):
        copies = _comm_copies(refs[:n], refs[n:2 * n], to_all, refs[2 * n:])
        for cp in copies:
            cp.start()
        for cp in copies:
            cp.wait()

    return pl.pallas_call(
        body, name=name, out_shape=_comm_out_shapes(ins, to_all), in_specs=[ANY] * n, out_specs=[ANY] * n,
        scratch_shapes=_comm_sems(n),
    )(*ins)


def _gather_two_level(ins, name):
    n = len(ins)

    def body(*refs):
        srcs, outs = refs[:n], refs[n:2 * n]
        send_sems, recv_sems, local_sems = refs[2 * n:]
        x, y, c = _coords()
        me = 4 * x + 2 * y + c
        sibling = (x, y, 1 - c)
        chips = [(1 - x, y), (x, 1 - y), (1 - x, 1 - y)]
        slot = lambda px, py, pc: 4 * px + 2 * py + pc

        def copy(k, t, block, to, own=False):
            return pltpu.make_async_remote_copy(
                src_ref=srcs[t] if own else outs[t].at[block], dst_ref=outs[t].at[block],
                send_sem=send_sems.at[k, t], recv_sem=recv_sems.at[k, t], device_id=to, device_id_type=MESH)

        local = [pltpu.make_async_copy(srcs[t], outs[t].at[me], local_sems.at[t]) for t in range(n)]
        first = [copy(0, t, me, sibling, own=True) for t in range(n)]
        first += [copy(1 + j, t, me, (*chip, c), own=True) for j, chip in enumerate(chips) for t in range(n)]
        for cp in local + first:
            cp.start()
        passed = []
        for j, chip in enumerate(chips):
            for t in range(n):
                copy(1 + j, t, slot(*chip, c), (x, y, c)).wait_recv()
                cp = copy(4 + j, t, slot(*chip, c), sibling)
                cp.start()
                passed.append(cp)
        for t in range(n):
            copy(0, t, slot(x, y, 1 - c), (x, y, c)).wait_recv()
            for j, chip in enumerate(chips):
                copy(4 + j, t, slot(*chip, 1 - c), (x, y, c)).wait_recv()
        for cp in first + passed:
            cp.wait_send()
        for cp in local:
            cp.wait()

    return pl.pallas_call(
        body, name=name, out_shape=_comm_out_shapes(ins, [True] * n), in_specs=[ANY] * n, out_specs=[ANY] * n,
        scratch_shapes=_comm_sems(n),
    )(*ins)


def _adamw(parts, w, m, v, rows, name):
    n_parts, r_all, cols = parts.shape
    c1 = 1.0 - ADAM_B1 ** ADAM_STEP
    c2 = 1.0 - ADAM_B2 ** ADAM_STEP

    def body(p_ref, w_ref, m_ref, v_ref, g_ref, d_ref, mo_ref, vo_ref):
        g = p_ref[0].astype(F32)
        for s in range(1, n_parts):
            g = g + p_ref[s].astype(F32)
        mn = ADAM_B1 * m_ref[...] + (1.0 - ADAM_B1) * g
        vn = ADAM_B2 * v_ref[...] + (1.0 - ADAM_B2) * (g * g)
        m_hat = mn / c1
        v_hat = vn / c2
        g_ref[...] = g
        d_ref[...] = -ADAM_LR * (m_hat / (jnp.sqrt(v_hat) + ADAM_EPS) + ADAM_WD * w_ref[...])
        mo_ref[...] = mn
        vo_ref[...] = vn

    spec = pl.BlockSpec((rows, cols), lambda i: (i, 0))
    return pl.pallas_call(
        body, name=name, grid=(r_all // rows,),
        in_specs=[pl.BlockSpec((n_parts, rows, cols), lambda i: (0, i, 0)), spec, spec, spec],
        out_specs=[spec] * 4, out_shape=[jax.ShapeDtypeStruct((r_all, cols), F32)] * 4,
        compiler_params=_params(),
    )(parts, w, m, v)


def _matmul_tn(a, b, chunk, tk, name):
    t_all, k1 = a.shape
    n = b.shape[1]

    def body(a_ref, b_ref, o_ref):
        @pl.when(pl.program_id(0) == 0)
        def _():
            o_ref[...] = jnp.zeros_like(o_ref)
        at = a_ref[...].astype(F32).T.astype(BF16)
        for j in range(0, n, chunk):
            cs = slice(j, min(j + chunk, n))
            o_ref[:, cs] += _nn(at, b_ref[:, cs])

    return pl.pallas_call(
        body, name=name, grid=(t_all // tk,),
        in_specs=[pl.BlockSpec((tk, k1), lambda t: (t, 0)), pl.BlockSpec((tk, n), lambda t: (t, 0))],
        out_specs=pl.BlockSpec((k1, n), lambda t: (0, 0)),
        out_shape=jax.ShapeDtypeStruct((k1, n), F32), compiler_params=_params(),
    )(a, b)


def _matmul_rows(a, b, tk, name):
    r, t_all = a.shape
    n = b.shape[1]

    def body(a_ref, b_ref, o_ref):
        @pl.when(pl.program_id(0) == 0)
        def _():
            o_ref[...] = jnp.zeros_like(o_ref)
        o_ref[...] += _nn(a_ref[...], b_ref[...])

    return pl.pallas_call(
        body, name=name, grid=(t_all // tk,),
        in_specs=[pl.BlockSpec((r, tk), lambda t: (0, t)), pl.BlockSpec((tk, n), lambda t: (t, 0))],
        out_specs=pl.BlockSpec((r, n), lambda t: (0, 0)),
        out_shape=jax.ShapeDtypeStruct((r, n), F32), compiler_params=_params(),
    )(a, b)


def _ada_fwd(c_all, w_ada, b_ada):
    whole = lambda a: pl.BlockSpec(a.shape, lambda j: (0, 0))

    def body(c_ref, w_ref, b_ref, o_ref):
        cv = c_ref[...]
        s = (cv * jax.nn.sigmoid(cv)).astype(BF16)
        o_ref[...] = _nn(s, w_ref[...].astype(BF16)) + b_ref[...]

    out = jax.ShapeDtypeStruct((c_all.shape[0], w_ada.shape[1]), F32)
    return pl.pallas_call(
        body, name="ada_fwd", grid=(1,), in_specs=[whole(c_all), whole(w_ada), whole(b_ada)], out_specs=whole(out),
        out_shape=out, compiler_params=_params(),
    )(c_all, w_ada, b_ada)


def _ada_bwd(c_all, dada):
    whole = lambda a: pl.BlockSpec(a.shape, lambda j: (0, 0))

    def body(c_ref, d_ref, o_ref):
        cv = c_ref[...]
        s = (cv * jax.nn.sigmoid(cv)).astype(BF16)
        o_ref[...] = _tn(s, d_ref[...].astype(BF16))

    out = jax.ShapeDtypeStruct((D_MODEL, dada.shape[1]), F32)
    return pl.pallas_call(
        body, name="ada_bwd", grid=(1,), in_specs=[whole(c_all), whole(dada)], out_specs=whole(out), out_shape=out,
        compiler_params=_params(),
    )(c_all, dada)


TOK_TM = 256
DILATIONS = (1, 4, 16)


def _class_spec(d, width, nts):
    return pl.BlockSpec((d, TOK_TM // d, width), lambda i: (i // nts, i % nts, 0))


def _class_shape(t_all, seq, d, width, dtype):
    return jax.ShapeDtypeStruct((t_all // seq * d, seq // d, width), dtype)


def _inproj(x, ada3, pos, wqkv, wf16, freq, perms, seq):
    t_all = x.shape[0]
    tm = TOK_TM
    nts = seq // tm

    def body(x_ref, ada_ref, pos_ref, w_ref, wf_ref, fr_ref, p4_ref, p16_ref, h1_ref, za_ref, zb_ref, zb4_ref,
             zb16_ref, vt_ref, fa_ref):
        h1 = (x_ref[...] * (1.0 + ada_ref[0, 1:2, :]) + ada_ref[0, 0:1, :]).astype(BF16)
        h1_ref[...] = h1
        tabs = _rope_tabs(pos_ref, fr_ref, 1.0)
        for n in range(6):
            z = _nn(h1, w_ref[:, n * WIDTH:(n + 1) * WIDTH])
            if n in (3, 4):
                z = _rope(z, tabs)
            if n in (0, 3):
                z = z * Q_SCALE
            if n == 2:
                vt_ref[...] = z.T.astype(BF16)
            dst = za_ref if n < 3 else zb_ref
            dst[:, (n % 3) * WIDTH:(n % 3 + 1) * WIDTH] = z.astype(BF16)
        fa_ref[...] = _nt(wf_ref[...], h1)[:N_HEADS]
        zb = zb_ref[...]
        _store_classes(zb4_ref, _nn(p4_ref[...], zb).astype(BF16), 4)
        _store_classes(zb16_ref, _nn(p16_ref[...], zb).astype(BF16), 16)

    tok = lambda w: pl.BlockSpec((tm, w), lambda i: (i, 0))
    whole = lambda a: pl.BlockSpec(a.shape, lambda i: (0, 0))
    return pl.pallas_call(
        body, name="inproj", grid=(t_all // tm,),
        in_specs=[tok(D_MODEL), pl.BlockSpec((1, 6, D_MODEL), lambda i: (i // nts, 0, 0)), tok(1), whole(wqkv),
                  whole(wf16), pl.BlockSpec((1, LANES), lambda i: (0, 0)), whole(perms[0]), whole(perms[1])],
        out_specs=[tok(D_MODEL), tok(3 * WIDTH), tok(3 * WIDTH), _class_spec(4, 3 * WIDTH, nts),
                   _class_spec(16, 3 * WIDTH, nts), pl.BlockSpec((WIDTH, tm), lambda i: (i // nts, i % nts)),
                   pl.BlockSpec((N_HEADS, tm), lambda i: (0, i))],
        out_shape=[jax.ShapeDtypeStruct((t_all, D_MODEL), BF16), jax.ShapeDtypeStruct((t_all, 3 * WIDTH), BF16),
                   jax.ShapeDtypeStruct((t_all, 3 * WIDTH), BF16), _class_shape(t_all, seq, 4, 3 * WIDTH, BF16),
                   _class_shape(t_all, seq, 16, 3 * WIDTH, BF16),
                   jax.ShapeDtypeStruct((t_all // seq * WIDTH, seq), BF16),
                   jax.ShapeDtypeStruct((N_HEADS, t_all), F32)],
        compiler_params=_params(),
    )(x, ada3, pos, wqkv, wf16, freq, perms[0], perms[1])


def _chunk_rows(a_t, seq):
    t_all = a_t.shape[1]
    return a_t.reshape(N_HEADS, t_all // seq, seq // LANES, LANES).transpose(1, 0, 2, 3).reshape(-1, LANES)


def _unchunk_rows(a, seq):
    nbat = a.shape[0] * LANES // (N_HEADS * seq)
    return a.reshape(nbat, N_HEADS, seq // LANES, LANES).transpose(1, 0, 2, 3).reshape(N_HEADS, nbat * seq)


def _chunk_carry(tot, nchunk, later):
    rows = tot.shape[0]
    r = lax.broadcasted_iota(jnp.int32, (rows, rows), 0)
    c = lax.broadcasted_iota(jnp.int32, (rows, rows), 1)
    sel = ((r // nchunk) == (c // nchunk)) & ((c > r) if later else (c < r))
    mat = sel.astype(BF16)
    return sum(_nn(mat, part) for part in _split3(jnp.broadcast_to(tot, (rows, LANES))))


def _fgate_fwd(fa_t, bf, seq):
    x = _chunk_rows(fa_t, seq)
    rows = x.shape[0]
    nchunk = seq // LANES
    bias = jnp.broadcast_to(bf.reshape(1, N_HEADS, 1), (rows // (N_HEADS * nchunk), N_HEADS, nchunk)).reshape(rows, 1)

    def body(x_ref, b_ref, f_ref):
        lane = lax.broadcasted_iota(jnp.int32, (rows, LANES), 1)
        xv = x_ref[...] + b_ref[...]
        lf = jnp.minimum(xv, 0.0) - jnp.log(1.0 + jnp.exp(-jnp.abs(xv)))
        for s in (1, 2, 4, 8, 16, 32, 64):
            lf = lf + jnp.where(lane >= s, pltpu.roll(lf, s, 1), 0.0)
        f_ref[...] = lf + _chunk_carry(lf[:, LANES - 1:LANES], nchunk, False)

    whole = lambda a: pl.BlockSpec(a.shape, lambda i: (0, 0))
    out = pl.pallas_call(
        body, name="fgate_fwd", grid=(1,), in_specs=[whole(x), whole(bias)], out_specs=whole(x),
        out_shape=jax.ShapeDtypeStruct(x.shape, F32), compiler_params=_params(),
    )(x, bias)
    return _unchunk_rows(out, seq)


def _fgate_bwd(df_t, fa_t, bf, seq):
    d_in = _chunk_rows(df_t, seq)
    x = _chunk_rows(fa_t, seq)
    rows = x.shape[0]
    nchunk = seq // LANES
    bias = jnp.broadcast_to(bf.reshape(1, N_HEADS, 1), (rows // (N_HEADS * nchunk), N_HEADS, nchunk)).reshape(rows, 1)

    def body(d_ref, x_ref, b_ref, o_ref, s_ref):
        lane = lax.broadcasted_iota(jnp.int32, (rows, LANES), 1)
        d = d_ref[...]
        for s in (1, 2, 4, 8, 16, 32, 64):
            d = d + jnp.where(lane < LANES - s, pltpu.roll(d, LANES - s, 1), 0.0)
        d = d + _chunk_carry(d[:, 0:1], nchunk, True)
        dfa = d * jax.nn.sigmoid(-(x_ref[...] + b_ref[...]))
        o_ref[...] = dfa
        g = lax.broadcasted_iota(jnp.int32, (2 * N_HEADS, rows), 0)
        r = lax.broadcasted_iota(jnp.int32, (2 * N_HEADS, rows), 1)
        group = (((r // nchunk) % N_HEADS) == g).astype(BF16)
        per_head = sum(_nn(group, part) for part in _split3(dfa))[:N_HEADS]
        s_ref[...] = jnp.broadcast_to(jnp.sum(per_head, axis=1, keepdims=True), (N_HEADS, LANES))

    whole = lambda a: pl.BlockSpec(a.shape, lambda i: (0, 0))
    dfa, sums = pl.pallas_call(
        body, name="fgate_bwd", grid=(1,), in_specs=[whole(d_in), whole(x), whole(bias)],
        out_specs=[whole(x), pl.BlockSpec((N_HEADS, LANES), lambda i: (0, 0))],
        out_shape=[jax.ShapeDtypeStruct(x.shape, F32), jax.ShapeDtypeStruct((N_HEADS, LANES), F32)],
        compiler_params=_params(),
    )(d_in, x, bias)
    return _unchunk_rows(dfa, seq), sums


FOX_T = 256


def _fox_prep(dst, src_ref, lo, hi):
    for p in range(4):
        v = src_ref[:, LANES * p:LANES * (p + 1)]
        dst[2 * p] = jnp.where(lo, v, jnp.zeros_like(v))
        dst[2 * p + 1] = jnp.where(hi, v, jnp.zeros_like(v))


def _fox_fwd(za, vt, f_col, seq, shards):
    t_all = za.shape[0]
    tq = FOX_T
    nq = seq // tq
    nbat = t_all // seq
    n = len(shards)
    to_all = [True] * n

    def body(*refs):
        q_ref, k_ref, vt_ref, fc_ref = refs[:4]
        o_ref, lse_ref = refs[4 + n:6 + n]
        qm_sc, m_sc, l_sc, acc_sc, a_sc, st_sc, pe_sc = refs[6 + 2 * n:13 + 2 * n]
        comm = (refs[4:4 + n], refs[6 + n:6 + 2 * n], to_all, refs[13 + 2 * n:])
        i = pl.program_id(1)

        @pl.when((pl.program_id(0) == 0) & (i == 0))
        def _():
            for cp in _comm_copies(*comm):
                cp.start()
        lo, hi = _half_masks(tq)
        r = lax.broadcasted_iota(jnp.int32, (tq, tq), 0)
        c = lax.broadcasted_iota(jnp.int32, (tq, tq), 1)
        tri = c >= r
        _fox_prep(qm_sc, q_ref, lo, hi)
        m_sc[...] = jnp.full(m_sc.shape, NEG, F32)
        l_sc[...] = jnp.zeros_like(l_sc)
        acc_sc[...] = jnp.zeros_like(acc_sc)

        def block(j, masked):
            sl = pl.ds(pl.multiple_of(j * tq, tq), tq)
            for p in range(4):
                kj = k_ref[sl, LANES * p:LANES * (p + 1)]
                for h in (2 * p, 2 * p + 1):
                    st = _nt(kj, qm_sc[h]) - fc_ref[sl, h:h + 1]
                    st_sc[h] = jnp.where(tri, st, NEG) if masked else st
            for h in range(N_HEADS):
                st = st_sc[h]
                m = m_sc[h:h + 1, :]
                mn = jnp.maximum(m, jnp.max(st, axis=0, keepdims=True))
                a = jnp.exp(m - mn)
                pe = jnp.exp(st - mn)
                m_sc[h:h + 1, :] = mn
                a_sc[h:h + 1, :] = a
                l_sc[h:h + 1, :] = a * l_sc[h:h + 1, :] + jnp.sum(pe, axis=0, keepdims=True)
                pe_sc[h] = pe.astype(BF16)
            for h in range(N_HEADS):
                acc_sc[h] = a_sc[h:h + 1, :] * acc_sc[h] + _nn(vt_ref[HEAD_DIM * h:HEAD_DIM * (h + 1), sl], pe_sc[h])

        def step(j, carry):
            block(j, False)
            return carry

        lax.fori_loop(0, i, step, 0)
        block(i, True)
        lse_ref[...] = m_sc[...] + jnp.log(l_sc[...])
        for p in range(4):
            ot = jnp.concatenate([acc_sc[h] / l_sc[h:h + 1, :] for h in (2 * p, 2 * p + 1)], axis=0)
            o_ref[:, LANES * p:LANES * (p + 1)] = ot.T

        @pl.when((pl.program_id(0) == nbat - 1) & (i == nq - 1))
        def _():
            for cp in _comm_copies(*comm):
                cp.wait()

    res = pl.pallas_call(
        body, name="fox_fwd", grid=(nbat, nq),
        in_specs=[pl.BlockSpec((tq, WIDTH), lambda b, i: (b * nq + i, 0)),
                  pl.BlockSpec((seq, WIDTH), lambda b, i: (b, 1)), pl.BlockSpec((WIDTH, seq), lambda b, i: (b, 0)),
                  pl.BlockSpec((seq, LANES), lambda b, i: (b, 0))] + [ANY] * n,
        out_specs=[pl.BlockSpec((tq, WIDTH), lambda b, i: (b * nq + i, 0)),
                   pl.BlockSpec((N_HEADS, tq), lambda b, i: (0, b * nq + i))] + [ANY] * n,
        out_shape=[jax.ShapeDtypeStruct((t_all, WIDTH), F32), jax.ShapeDtypeStruct((N_HEADS, t_all), F32)]
        + _comm_out_shapes(shards, to_all),
        scratch_shapes=[pltpu.VMEM((N_HEADS, tq, LANES), BF16), pltpu.VMEM((N_HEADS, tq), F32),
                        pltpu.VMEM((N_HEADS, tq), F32), pltpu.VMEM((N_HEADS, HEAD_DIM, tq), F32),
                        pltpu.VMEM((N_HEADS, tq), F32), pltpu.VMEM((N_HEADS, tq, tq), F32),
                        pltpu.VMEM((N_HEADS, tq, tq), BF16)] + _comm_sems(n),
        compiler_params=_params(),
    )(za, za, vt, f_col, *shards)
    return res[0], res[1], res[2:]


def _fox_bwd(za, do, f_col, lse_row, dl_row, seq, grads):
    t_all = za.shape[0]
    tk = FOX_T
    nk = seq // tk
    nbat = t_all // seq
    n = len(grads)
    to_all = [False] * n

    def body(*refs):
        k_ref, v_ref, q_ref, do_ref, fc_ref, lr_ref, dr_ref = refs[:7]
        dk_ref, dv_ref, df_ref, dqt_ref, dfq_ref = refs[7 + n:12 + n]
        km_sc, vm_sc, fk_sc, dk_sc, dv_sc, cs_sc, kt_sc, st_sc, dp_sc, pt_sc, ds_sc = refs[12 + 2 * n:23 + 2 * n]
        comm = (refs[7:7 + n], refs[12 + n:12 + 2 * n], to_all, refs[23 + 2 * n:])
        j = pl.program_id(1)

        @pl.when(j == 0)
        def _():
            dqt_ref[...] = jnp.zeros_like(dqt_ref)
            dfq_ref[...] = jnp.zeros_like(dfq_ref)

        @pl.when((pl.program_id(0) == 0) & (j == 0))
        def _():
            for cp in _comm_copies(*comm):
                cp.start()
        lo, hi = _half_masks(tk)
        r = lax.broadcasted_iota(jnp.int32, (tk, tk), 0)
        c = lax.broadcasted_iota(jnp.int32, (tk, tk), 1)
        tri = c >= r
        _fox_prep(km_sc, k_ref, lo, hi)
        _fox_prep(vm_sc, v_ref, lo, hi)
        for h in range(N_HEADS):
            fk_sc[h] = jnp.broadcast_to(fc_ref[:, h:h + 1], (tk, tk))
        for p in range(4):
            kt_sc[p] = k_ref[:, LANES * p:LANES * (p + 1)].astype(F32).T.astype(BF16)
        dk_sc[...] = jnp.zeros_like(dk_sc)
        dv_sc[...] = jnp.zeros_like(dv_sc)
        cs_sc[...] = jnp.zeros_like(cs_sc)

        def block(i, masked):
            sl = pl.ds(pl.multiple_of(i * tk, tk), tk)
            for p in range(4):
                cs = slice(LANES * p, LANES * (p + 1))
                qi = q_ref[sl, cs]
                doi = do_ref[sl, cs]
                for h in (2 * p, 2 * p + 1):
                    st = _nt(km_sc[h], qi) - fk_sc[h] - lr_ref[h:h + 1, sl]
                    st_sc[h] = jnp.where(tri, st, NEG) if masked else st
                    dp_sc[h] = _nt(vm_sc[h], doi) - dr_ref[h:h + 1, sl]
            for h in range(N_HEADS):
                pt = jnp.exp(st_sc[h])
                dst = pt * dp_sc[h]
                pt_sc[h] = pt.astype(BF16)
                ds_sc[h] = dst.astype(BF16)
                cs_sc[h] += dst[:, :LANES] + dst[:, LANES:]
                dfq_ref[h:h + 1, sl] += jnp.sum(dst, axis=0, keepdims=True)
            for p in range(4):
                cs = slice(LANES * p, LANES * (p + 1))
                qi = q_ref[sl, cs]
                doi = do_ref[sl, cs]
                for h in (2 * p, 2 * p + 1):
                    dv_sc[h] += _nn(pt_sc[h], doi)
                    dk_sc[h] += _nn(ds_sc[h], qi)
                    kt = kt_sc[p, HEAD_DIM * (h % 2):HEAD_DIM * (h % 2 + 1), :]
                    dqt_ref[HEAD_DIM * h:HEAD_DIM * (h + 1), sl] += _nn(kt, ds_sc[h])

        def step(i, carry):
            block(i, False)
            return carry

        block(j, True)
        lax.fori_loop(j + 1, nk, step, 0)
        df_ref[...] = jnp.zeros_like(df_ref)
        for p in range(4):
            cs = slice(LANES * p, LANES * (p + 1))
            dk_ref[:, cs] = jnp.where(lo, dk_sc[2 * p], dk_sc[2 * p + 1]).astype(BF16)
            dv_ref[:, cs] = jnp.where(lo, dv_sc[2 * p], dv_sc[2 * p + 1]).astype(BF16)
            for h in (2 * p, 2 * p + 1):
                df_ref[:, h:h + 1] = -jnp.sum(cs_sc[h], axis=1, keepdims=True)

        @pl.when(j == nk - 1)
        def _():
            dqt_ref[...] = dqt_ref[...] * Q_SCALE

        @pl.when((pl.program_id(0) == nbat - 1) & (j == nk - 1))
        def _():
            for cp in _comm_copies(*comm):
                cp.wait()

    tile = lambda w, col: pl.BlockSpec((tk, w), lambda b, j: (b * nk + j, col))
    full = lambda col: pl.BlockSpec((seq, WIDTH), lambda b, j: (b, col))
    row = pl.BlockSpec((N_HEADS, seq), lambda b, j: (0, b))
    acc = pltpu.VMEM((N_HEADS, tk, LANES), F32)
    res = pl.pallas_call(
        body, name="fox_bwd", grid=(nbat, nk),
        in_specs=[tile(WIDTH, 1), tile(WIDTH, 2), full(0), full(0), tile(LANES, 0), row, row] + [ANY] * n,
        out_specs=[tile(WIDTH, 0), tile(WIDTH, 0), tile(LANES, 0), pl.BlockSpec((WIDTH, seq), lambda b, j: (b, 0)),
                   row] + [ANY] * n,
        out_shape=[jax.ShapeDtypeStruct((t_all, WIDTH), BF16), jax.ShapeDtypeStruct((t_all, WIDTH), BF16),
                   jax.ShapeDtypeStruct((t_all, LANES), F32), jax.ShapeDtypeStruct((nbat * WIDTH, seq), F32),
                   jax.ShapeDtypeStruct((N_HEADS, t_all), F32)] + _comm_out_shapes(grads, to_all),
        scratch_shapes=[pltpu.VMEM((N_HEADS, tk, LANES), BF16), pltpu.VMEM((N_HEADS, tk, LANES), BF16),
                        pltpu.VMEM((N_HEADS, tk, tk), F32), acc, acc, acc, pltpu.VMEM((4, LANES, tk), BF16),
                        pltpu.VMEM((N_HEADS, tk, tk), F32), pltpu.VMEM((N_HEADS, tk, tk), F32),
                        pltpu.VMEM((N_HEADS, tk, tk), BF16), pltpu.VMEM((N_HEADS, tk, tk), BF16)]
        + _comm_sems(n),
        compiler_params=_params(),
    )(za, za, za, do, f_col, lse_row, dl_row, *grads)
    return res[0], res[1], res[2], res[3], res[4], res[5:]


DIL_SUB = 4


def _dil_mask(has_prev):
    qi = lax.broadcasted_iota(jnp.int32, (BLK, 2 * BLK), 0)
    kj = lax.broadcasted_iota(jnp.int32, (BLK, 2 * BLK), 1)
    dist = qi + BLK - kj
    band = (dist >= 0) & (dist <= BLK)
    return band if has_prev is True else band & ((kj >= BLK) | has_prev)


def _dil_geometry(t_all, seq, d, max_sub=DIL_SUB):
    length = seq // d
    nbs = length // BLK
    sub = min(max_sub, nbs)
    spb = nbs // sub
    tile = lambda width, col: pl.BlockSpec((BLK * sub, width), lambda s: (s, col))
    whole = lambda width, col: pl.BlockSpec((length, width), lambda s: (s // spb, col))
    return nbs, sub, spb, t_all // (BLK * sub), tile, whole


def _blk(i):
    return pl.ds(pl.multiple_of(i * BLK, BLK), BLK)


def _dil_fwd(zb, seq, d):
    t_all = zb.shape[0]
    nbs, sub, spb, steps, tile, whole = _dil_geometry(t_all, seq, d)

    def body(q_ref, k_ref, v_ref, o_ref, lse_ref, s_sc, p_sc):
        first = (pl.program_id(0) % spb) * sub
        lo, hi = _half_masks(BLK)
        lse_ref[...] = jnp.zeros_like(lse_ref)
        for j in range(sub):
            blk = first + j
            mask = _dil_mask(blk != 0 if j == 0 else True)
            for p in range(4):
                cs = slice(LANES * p, LANES * (p + 1))
                qp = q_ref[BLK * j:BLK * (j + 1), cs]
                kcat = jnp.concatenate([k_ref[_blk(jnp.maximum(blk - 1, 0)), cs], k_ref[_blk(blk), cs]], axis=0)
                for e in (0, 1):
                    qe = jnp.where(lo if e == 0 else hi, qp, jnp.zeros_like(qp))
                    s_sc[N_HEADS * j + 2 * p + e] = jnp.where(mask, _nt(qe, kcat), NEG)
        inv = []
        for i in range(N_HEADS * sub):
            s = s_sc[i]
            m = jnp.max(s, axis=1, keepdims=True)
            pe = jnp.exp(s - m)
            l = jnp.sum(pe, axis=1, keepdims=True)
            p_sc[i] = pe.astype(BF16)
            inv.append(1.0 / l)
            j, h = divmod(i, N_HEADS)
            lse_ref[BLK * j:BLK * (j + 1), h:h + 1] = m + jnp.log(l)
        for j in range(sub):
            blk = first + j
            for p in range(4):
                cs = slice(LANES * p, LANES * (p + 1))
                vcat = jnp.concatenate([v_ref[_blk(jnp.maximum(blk - 1, 0)), cs], v_ref[_blk(blk), cs]], axis=0)
                res = [_nn(p_sc[N_HEADS * j + h], vcat) * inv[N_HEADS * j + h] for h in (2 * p, 2 * p + 1)]
                o_ref[BLK * j:BLK * (j + 1), cs] = jnp.where(lo, res[0], res[1])

    return pl.pallas_call(
        body, name=f"dil_fwd_{d}", grid=(steps,), in_specs=[tile(WIDTH, 0), whole(WIDTH, 1), whole(WIDTH, 2)],
        out_specs=[tile(WIDTH, 0), tile(LANES, 0)],
        out_shape=[jax.ShapeDtypeStruct((t_all, WIDTH), F32), jax.ShapeDtypeStruct((t_all, LANES), F32)],
        scratch_shapes=[pltpu.VMEM((N_HEADS * sub, BLK, 2 * BLK), F32),
                        pltpu.VMEM((N_HEADS * sub, BLK, 2 * BLK), BF16)],
        compiler_params=_params(),
    )(zb, zb, zb)


def _dil_bwd(zb, do, lse, dl, seq, d):
    t_all = zb.shape[0]
    length = seq // d
    nbs, sub, spb, steps, tile, whole = _dil_geometry(t_all, seq, d, 2 if length >= 4096 else DIL_SUB)

    def body(k_ref, v_ref, q_ref, do_ref, lse_ref, dl_ref, dq_ref, dk_ref, dv_ref, s_sc, dp_sc, pt_sc, ds_sc, kt_sc,
             dqt_sc):
        step = pl.program_id(0) % spb
        first = step * sub

        @pl.when(step == 0)
        def _():
            dqt_sc[...] = jnp.zeros_like(dqt_sc)
        r = lax.broadcasted_iota(jnp.int32, (BLK, 2 * BLK), 0)
        c = lax.broadcasted_iota(jnp.int32, (BLK, 2 * BLK), 1)
        same = (c < BLK) & (c >= r)
        later = (c >= BLK) & (c - BLK <= r)
        lo, hi = _half_masks(BLK)
        for j in range(sub):
            blk = first + j
            rows = slice(BLK * j, BLK * (j + 1))
            nxt = _blk(jnp.minimum(blk + 1, nbs - 1))
            mask = same | (later & (blk + 1 != nbs)) if j == sub - 1 else same | later
            lrows = jnp.concatenate([lse_ref[_blk(blk), :].T, lse_ref[nxt, :].T], axis=1)
            erows = jnp.concatenate([dl_ref[_blk(blk), :].T, dl_ref[nxt, :].T], axis=1)
            for p in range(4):
                cs = slice(LANES * p, LANES * (p + 1))
                kp = k_ref[rows, cs]
                vp = v_ref[rows, cs]
                kt_sc[4 * j + p] = kp.astype(F32).T.astype(BF16)
                qcat = jnp.concatenate([q_ref[_blk(blk), cs], q_ref[nxt, cs]], axis=0)
                dcat = jnp.concatenate([do_ref[_blk(blk), cs], do_ref[nxt, cs]], axis=0)
                for e in (0, 1):
                    h = 2 * p + e
                    sel = lo if e == 0 else hi
                    ke = jnp.where(sel, kp, jnp.zeros_like(kp))
                    ve = jnp.where(sel, vp, jnp.zeros_like(vp))
                    s_sc[N_HEADS * j + h] = jnp.where(mask, _nt(ke, qcat) - lrows[h:h + 1, :], NEG)
                    dp_sc[N_HEADS * j + h] = _nt(ve, dcat) - erows[h:h + 1, :]
        for i in range(N_HEADS * sub):
            pt = jnp.exp(s_sc[i])
            pt_sc[i] = pt.astype(BF16)
            ds_sc[i] = (pt * dp_sc[i]).astype(BF16)
        for j in range(sub):
            blk = first + j
            rows = slice(BLK * j, BLK * (j + 1))
            nxt = _blk(jnp.minimum(blk + 1, nbs - 1))
            cols = pl.ds(pl.multiple_of(blk * BLK, BLK), 2 * BLK)
            for p in range(4):
                cs = slice(LANES * p, LANES * (p + 1))
                qcat = jnp.concatenate([q_ref[_blk(blk), cs], q_ref[nxt, cs]], axis=0)
                dcat = jnp.concatenate([do_ref[_blk(blk), cs], do_ref[nxt, cs]], axis=0)
                i = N_HEADS * j + 2 * p
                dk_ref[rows, cs] = jnp.where(lo, _nn(ds_sc[i], qcat), _nn(ds_sc[i + 1], qcat)).astype(BF16)
                dv_ref[rows, cs] = jnp.where(lo, _nn(pt_sc[i], dcat), _nn(pt_sc[i + 1], dcat)).astype(BF16)
                for e in (0, 1):
                    kt = kt_sc[4 * j + p, HEAD_DIM * e:HEAD_DIM * (e + 1), :]
                    dqt_sc[HEAD_DIM * (2 * p + e):HEAD_DIM * (2 * p + e + 1), cols] += _nn(kt, ds_sc[i + e])

        @pl.when(step == spb - 1)
        def _():
            for p in range(4):
                cs = slice(LANES * p, LANES * (p + 1))
                dq_ref[:, cs] = (dqt_sc[cs, 0:length].T * Q_SCALE).astype(BF16)

    wide = pltpu.VMEM((N_HEADS * sub, BLK, 2 * BLK), F32)
    half = pltpu.VMEM((N_HEADS * sub, BLK, 2 * BLK), BF16)
    return pl.pallas_call(
        body, name=f"dil_bwd_{d}", grid=(steps,),
        in_specs=[tile(WIDTH, 1), tile(WIDTH, 2), whole(WIDTH, 0), whole(WIDTH, 0), whole(LANES, 0), whole(LANES, 0)],
        out_specs=[whole(WIDTH, 0), tile(WIDTH, 0), tile(WIDTH, 0)],
        out_shape=[jax.ShapeDtypeStruct((t_all, WIDTH), BF16)] * 3,
        scratch_shapes=[wide, wide, half, half, pltpu.VMEM((4 * sub, LANES, BLK), BF16),
                        pltpu.VMEM((WIDTH, length + BLK), F32)],
        compiler_params=_params(),
    )(zb, zb, zb, do, lse, dl)


def _mix_out(oa, o3, l3, gn_a, gn_b, w_out, x, ada3, ln_g, ln_b, perms, seq):
    t_all = x.shape[0]
    tm = TOK_TM
    nts = seq // tm

    def body(oa_ref, o1_ref, o2_ref, o3_ref, l1_ref, l2_ref, l3_ref, ga_ref, gb_ref, w_ref, x_ref, ada_ref, g_ref,
             b_ref, p4_ref, p16_ref, pt4_ref, pt16_ref, ob_ref, lse_ref, lse4_ref, lse16_ref, mg_ref, mix_ref, xh_ref,
             rs_ref, h2_ref, h2t_ref):
        e, et = _head_mats()
        la = l1_ref[...]
        lb = _permute_f32(pt4_ref[...], _load_classes(l2_ref, 4))
        lc = _permute_f32(pt16_ref[...], _load_classes(l3_ref, 16))
        mx = jnp.maximum(jnp.maximum(la, lb), lc)
        ea, eb, ec = jnp.exp(la - mx), jnp.exp(lb - mx), jnp.exp(lc - mx)
        tot = ea + eb + ec
        lse = mx + jnp.log(tot)
        lse_ref[...] = lse
        _store_classes(lse4_ref, _permute_f32(p4_ref[...], lse), 4)
        _store_classes(lse16_ref, _permute_f32(p16_ref[...], lse), 16)
        ob = (o1_ref[...] * _hexp(ea / tot, e)
              + _permute_f32(pt4_ref[...], _load_classes(o2_ref, 4)) * _hexp(eb / tot, e)
              + _permute_f32(pt16_ref[...], _load_classes(o3_ref, 16)) * _hexp(ec / tot, e))
        ob_ref[...] = ob

        def rms(o, gain):
            rr = lax.rsqrt(_hsum(o * o, et) * (1.0 / HEAD_DIM) + RMS_EPS)
            return o * _hexp(rr, e) * gain

        merged = jnp.concatenate([rms(oa_ref[...], ga_ref[...]), rms(ob, gb_ref[...])], axis=1).astype(BF16)
        mg_ref[...] = merged
        mix = _nn(merged, w_ref[...])
        mix_ref[...] = mix.astype(BF16)
        r1 = ALPHA * x_ref[...] + ada_ref[0, 2:3, :] * mix
        d = r1 - jnp.mean(r1, axis=1, keepdims=True)
        rstd = lax.rsqrt(jnp.mean(d * d, axis=1, keepdims=True) + LN_EPS)
        xh = d * rstd
        xh_ref[...] = xh
        rs_ref[...] = jnp.broadcast_to(rstd, (tm, LANES))
        x1 = xh * g_ref[...] + b_ref[...]
        h2 = x1 * (1.0 + ada_ref[0, 4:5, :]) + ada_ref[0, 3:4, :]
        h2_ref[...] = h2.astype(BF16)
        h2t_ref[0] = h2.T.astype(BF16)

    tok = lambda w: pl.BlockSpec((tm, w), lambda i: (i, 0))
    vec = lambda w: pl.BlockSpec((1, w), lambda i: (0, 0))
    whole = lambda a: pl.BlockSpec(a.shape, lambda i: (0, 0))
    classes = lambda a, d: a.reshape(t_all // seq * d, seq // d, a.shape[-1])
    return pl.pallas_call(
        body, name="mix_out", grid=(t_all // tm,),
        in_specs=[tok(WIDTH), tok(WIDTH), _class_spec(4, WIDTH, nts), _class_spec(16, WIDTH, nts), tok(LANES),
                  _class_spec(4, LANES, nts), _class_spec(16, LANES, nts), vec(WIDTH), vec(WIDTH), whole(w_out),
                  tok(D_MODEL), pl.BlockSpec((1, 6, D_MODEL), lambda i: (i // nts, 0, 0)), vec(D_MODEL), vec(D_MODEL)]
        + [whole(p) for p in perms],
        out_specs=[tok(WIDTH), tok(LANES), _class_spec(4, LANES, nts), _class_spec(16, LANES, nts), tok(D_MODEL),
                   tok(D_MODEL), tok(D_MODEL), tok(LANES), tok(D_MODEL), pl.BlockSpec((1, D_MODEL, tm), lambda i: (i // (FFN_TM // tm), 0, i % (FFN_TM // tm)))],
        out_shape=[jax.ShapeDtypeStruct((t_all, WIDTH), F32), jax.ShapeDtypeStruct((t_all, LANES), F32),
                   _class_shape(t_all, seq, 4, LANES, F32), _class_shape(t_all, seq, 16, LANES, F32),
                   jax.ShapeDtypeStruct((t_all, D_MODEL), BF16), jax.ShapeDtypeStruct((t_all, D_MODEL), BF16),
                   jax.ShapeDtypeStruct((t_all, D_MODEL), F32), jax.ShapeDtypeStruct((t_all, LANES), F32),
                   jax.ShapeDtypeStruct((t_all, D_MODEL), BF16),
                   jax.ShapeDtypeStruct((t_all // FFN_TM, D_MODEL, FFN_TM), BF16)],
        compiler_params=_params(),
    )(oa, o3[0], classes(o3[1], 4), classes(o3[2], 16), l3[0], classes(l3[1], 4), classes(l3[2], 16), gn_a, gn_b,
      w_out, x, ada3, ln_g, ln_b, *perms)


def _mix_out_bwd(dmix, w_out, oa, ob, gn_a, gn_b, perms, seq):
    t_all = dmix.shape[0]
    tm = TOK_TM
    nts = seq // tm

    def body(dm_ref, w_ref, oa_ref, ob_ref, ga_ref, gb_ref, p4_ref, p16_ref, doa_ref, dob_ref, dob4_ref, dob16_ref,
             dla_ref, dlb_ref, dlb4_ref, dlb16_ref, acc_ref):
        @pl.when(pl.program_id(0) == 0)
        def _():
            acc_ref[...] = jnp.zeros_like(acc_ref)
        e, et = _head_mats()
        dmg = _nt(dm_ref[...], w_ref[...])

        def group(o, dn, gain):
            rr = lax.rsqrt(_hsum(o * o, et) * (1.0 / HEAD_DIM) + RMS_EPS)
            re = _hexp(rr, e)
            dgain = jnp.sum(dn * o * re, axis=0, keepdims=True)
            dxn = dn * gain
            tt = _hsum(dxn * o, et) * (rr * rr * rr) * (1.0 / HEAD_DIM)
            do = re * dxn - o * _hexp(tt, e)
            return do, _hsum(do * o, et), dgain

        doa, dla, dga = group(oa_ref[...], dmg[:, :WIDTH], ga_ref[...])
        dob, dlb, dgb = group(ob_ref[...], dmg[:, WIDTH:], gb_ref[...])
        dob = dob.astype(BF16)
        doa_ref[...] = doa.astype(BF16)
        dob_ref[...] = dob
        _store_classes(dob4_ref, _nn(p4_ref[...], dob).astype(BF16), 4)
        _store_classes(dob16_ref, _nn(p16_ref[...], dob).astype(BF16), 16)
        dla_ref[...] = dla
        dlb_ref[...] = dlb
        _store_classes(dlb4_ref, _permute_f32(p4_ref[...], dlb), 4)
        _store_classes(dlb16_ref, _permute_f32(p16_ref[...], dlb), 16)
        acc_ref[0:1, :] += jnp.concatenate([dga, dgb], axis=1)

    tok = lambda w: pl.BlockSpec((tm, w), lambda i: (i, 0))
    vec = lambda w: pl.BlockSpec((1, w), lambda i: (0, 0))
    return pl.pallas_call(
        body, name="mix_out_bwd", grid=(t_all // tm,),
        in_specs=[tok(D_MODEL), pl.BlockSpec(w_out.shape, lambda i: (0, 0)), tok(WIDTH), tok(WIDTH), vec(WIDTH),
                  vec(WIDTH), pl.BlockSpec(perms[0].shape, lambda i: (0, 0)),
                  pl.BlockSpec(perms[1].shape, lambda i: (0, 0))],
        out_specs=[tok(WIDTH), tok(WIDTH), _class_spec(4, WIDTH, nts), _class_spec(16, WIDTH, nts), tok(LANES),
                   tok(LANES), _class_spec(4, LANES, nts), _class_spec(16, LANES, nts),
                   pl.BlockSpec((8, D_MODEL), lambda i: (0, 0))],
        out_shape=[jax.ShapeDtypeStruct((t_all, WIDTH), BF16), jax.ShapeDtypeStruct((t_all, WIDTH), BF16),
                   _class_shape(t_all, seq, 4, WIDTH, BF16), _class_shape(t_all, seq, 16, WIDTH, BF16),
                   jax.ShapeDtypeStruct((t_all, LANES), F32), jax.ShapeDtypeStruct((t_all, LANES), F32),
                   _class_shape(t_all, seq, 4, LANES, F32), _class_shape(t_all, seq, 16, LANES, F32),
                   jax.ShapeDtypeStruct((8, D_MODEL), F32)],
        compiler_params=_params(),
    )(dmix, w_out, oa, ob, gn_a, gn_b, perms[0], perms[1])


def _inproj_bwd(dqt, dka, dva, dil1, dil4, dil16, dfa16, pos, wqkv, wf16, freq, perms, dr1, x, ada3, seq):
    t_all = x.shape[0]
    tm = TOK_TM
    nts = seq // tm

    def body(dqt_ref, dka_ref, dva_ref, q1_ref, k1_ref, v1_ref, q4_ref, k4_ref, v4_ref, q16_ref, k16_ref, v16_ref,
             dfa_ref, pos_ref, w_ref, wf_ref, fr_ref, pt4_ref, pt16_ref, dr1_ref, x_ref, ada_ref, gx_ref, dz_ref,
             acc_ref):
        i = pl.program_id(0)

        @pl.when(i == 0)
        def _():
            acc_ref[...] = jnp.zeros_like(acc_ref)
        tabs = _rope_tabs(pos_ref, fr_ref, -1.0)
        dz_ref[:, :WIDTH] = dqt_ref[...].T.astype(BF16)
        dz_ref[:, WIDTH:2 * WIDTH] = dka_ref[...]
        dz_ref[:, 2 * WIDTH:3 * WIDTH] = dva_ref[...]
        for t, (n1, n4, n16) in enumerate(((q1_ref, q4_ref, q16_ref), (k1_ref, k4_ref, k16_ref),
                                           (v1_ref, v4_ref, v16_ref))):
            tot = (n1[...].astype(F32) + _nn(pt4_ref[...], _load_classes(n4, 4))
                   + _nn(pt16_ref[...], _load_classes(n16, 16)))
            if t < 2:
                tot = _rope(tot, tabs)
            dz_ref[:, (3 + t) * WIDTH:(4 + t) * WIDTH] = tot.astype(BF16)
        dh1 = _tn(dfa_ref[...], wf_ref[...])
        for n in range(6):
            cs = slice(n * WIDTH, (n + 1) * WIDTH)
            dh1 = dh1 + _nt(dz_ref[:, cs], w_ref[:, cs])
        xv = x_ref[...]
        gx_ref[...] = ALPHA * dr1_ref[...] + dh1 * (1.0 + ada_ref[0, 1:2, :])
        b = i // nts
        acc_ref[pl.ds(b, 1), :] += jnp.sum(dh1 * xv, axis=0, keepdims=True)
        acc_ref[pl.ds(8 + b, 1), :] += jnp.sum(dh1, axis=0, keepdims=True)

    tok = lambda w: pl.BlockSpec((tm, w), lambda i: (i, 0))
    whole = lambda a: pl.BlockSpec(a.shape, lambda i: (0, 0))
    classes = lambda a, d: a.reshape(t_all // seq * d, seq // d, a.shape[-1])
    return pl.pallas_call(
        body, name="inproj_bwd", grid=(t_all // tm,),
        in_specs=[pl.BlockSpec((WIDTH, tm), lambda i: (i // nts, i % nts)), tok(WIDTH), tok(WIDTH)]
        + [tok(WIDTH)] * 3 + [_class_spec(4, WIDTH, nts)] * 3 + [_class_spec(16, WIDTH, nts)] * 3
        + [pl.BlockSpec((16, tm), lambda i: (0, i)), tok(1), whole(wqkv), whole(wf16),
           pl.BlockSpec((1, LANES), lambda i: (0, 0)), whole(perms[2]), whole(perms[3]), tok(D_MODEL), tok(D_MODEL),
           pl.BlockSpec((1, 6, D_MODEL), lambda i: (i // nts, 0, 0))],
        out_specs=[tok(D_MODEL), tok(6 * WIDTH), pl.BlockSpec((16, D_MODEL), lambda i: (0, 0))],
        out_shape=[jax.ShapeDtypeStruct((t_all, D_MODEL), F32), jax.ShapeDtypeStruct((t_all, 6 * WIDTH), BF16),
                   jax.ShapeDtypeStruct((16, D_MODEL), F32)],
        compiler_params=_params(),
    )(dqt, dka, dva, *dil1, *[classes(a, 4) for a in dil4], *[classes(a, 16) for a in dil16], dfa16, pos, wqkv, wf16,
      freq, perms[2], perms[3], dr1, x, ada3)


FFN_TM = 1024
FFN_TN = 256
HALO = 8


FFN_CHUNK = 256


def _conv(cat_ref, w_ref, b_ref, start, rows, halo=HALO):
    return (b_ref[...] + w_ref[0:1, :] * cat_ref[pl.ds(start + halo - 2, rows), :]
            + w_ref[1:2, :] * cat_ref[pl.ds(start + halo - 1, rows), :]
            + w_ref[2:3, :] * cat_ref[pl.ds(start + halo, rows), :])


def _ffn_up_gate(h2, w_up, conv_w, conv_b, seq):
    t_all = h2.shape[0]
    tm, tn = FFN_TM, FFN_TN
    nc = D_FF // tn
    nts = seq // tm
    pre = 16

    def body(h_ref, hp_ref, wua_ref, wug_ref, wa_ref, wg_ref, ba_ref, bg_ref, ua_ref, ug_ref, o_ref, ca_ref, cg_ref):
        first = (pl.program_id(1) % nts) == 0
        hcat = jnp.concatenate([hp_ref[...], h_ref[...]], axis=0)
        zero = jnp.zeros((pre, tn), F32)
        for w_ref, cat, u_ref in ((wua_ref, ca_ref, ua_ref), (wug_ref, cg_ref, ug_ref)):
            ub = _nn(hcat, w_ref[...]).astype(BF16)
            ue = ub.astype(F32)
            cat[0:pre, :] = jnp.where(first, zero, ue[0:pre])
            cat[pre:, :] = ue[pre:]
            u_ref[...] = ub[pre:]
        for c0 in range(0, tm, FFN_CHUNK):
            ya = _conv(ca_ref, wa_ref, ba_ref, c0, FFN_CHUNK, pre)
            yg = _conv(cg_ref, wg_ref, bg_ref, c0, FFN_CHUNK, pre)
            o_ref[c0:c0 + FFN_CHUNK, :] = (yg * jax.nn.sigmoid(yg) * ya).astype(BF16)

    vec = lambda r, off: pl.BlockSpec((r, tn), lambda n, t: (0, n + off))
    wcol = lambda off: pl.BlockSpec((D_MODEL, tn), lambda n, t: (0, n + off))
    tile = pl.BlockSpec((tm, tn), lambda n, t: (t, n))
    return pl.pallas_call(
        body, name="ffn_up_gate", grid=(nc, t_all // tm),
        in_specs=[pl.BlockSpec((tm, D_MODEL), lambda n, t: (t, 0)),
                  pl.BlockSpec((pre, D_MODEL), lambda n, t: (jnp.maximum(t * (tm // pre) - 1, 0), 0)),
                  wcol(0), wcol(nc), vec(3, 0), vec(3, nc), vec(1, 0), vec(1, nc)],
        out_specs=[tile, tile, tile],
        out_shape=[jax.ShapeDtypeStruct((t_all, D_FF), BF16)] * 3,
        scratch_shapes=[pltpu.VMEM((tm + pre, tn), F32)] * 2, compiler_params=_params(),
    )(h2, h2, w_up, w_up, conv_w, conv_w, conv_b, conv_b)


def _ffn_gate_bwd(u_a, u_g, dfi, conv_w, conv_b, h2t, seq):
    t_all = u_a.shape[0]
    tm, tn = FFN_TM, FFN_TN
    nc = D_FF // tn
    nts = seq // tm

    def body(ua_ref, uap_ref, uan_ref, ug_ref, ugp_ref, ugn_ref, df_ref, dfn_ref, wa_ref, wg_ref, ba_ref, bg_ref, h_ref,
             dua_ref, dug_ref, acca_ref, accg_ref, dwa_ref, dwg_ref, ca_ref, cg_ref, ya_ref, yg_ref, dwa_sc, dwg_sc,
             out_sems):
        t = pl.program_id(0)
        n = pl.program_id(1)
        cols = pl.ds(pl.multiple_of(n * tn, tn), tn)
        first = (t % nts) == 0
        last = (t % nts) == nts - 1

        @pl.when((t == 0) & (n == 0))
        def _():
            acca_ref[...] = jnp.zeros_like(acca_ref)
            accg_ref[...] = jnp.zeros_like(accg_ref)
            dwa_sc[...] = jnp.zeros_like(dwa_sc)
            dwg_sc[...] = jnp.zeros_like(dwg_sc)
        zero = jnp.zeros((HALO, tn), F32)
        for cat, cur, prv, nxt in ((ca_ref, ua_ref, uap_ref, uan_ref), (cg_ref, ug_ref, ugp_ref, ugn_ref)):
            cat[0:HALO, :] = jnp.where(first, zero, prv[...].astype(F32)[HALO:])
            cat[HALO:HALO + tm, :] = cur[...].astype(F32)
            cat[HALO + tm:, :] = nxt[...].astype(F32)[:HALO]
        ch = FFN_CHUNK
        sums = [[jnp.zeros((1, tn), F32) for _ in range(4)] for _ in range(2)]
        for ci, c0 in enumerate(range(0, tm, ch)):
            ya = _conv(ca_ref, wa_ref, ba_ref, c0, ch + HALO)
            yg = _conv(cg_ref, wg_ref, bg_ref, c0, ch + HALO)
            if c0 + ch < tm:
                beyond = df_ref[c0 + ch:c0 + ch + 16, :].astype(F32)[:HALO]
            else:
                beyond = jnp.where(last, 0.0, dfn_ref[...].astype(F32)[:HALO])
            dfe = jnp.concatenate([df_ref[c0:c0 + ch, :].astype(F32), beyond], axis=0)
            sg = jax.nn.sigmoid(yg)
            ya_ref[ci] = dfe * (yg * sg)
            yg_ref[ci] = dfe * ya * (sg * (1.0 + yg * (1.0 - sg)))
            for half, (dy, cat, w_ref, du_ref) in enumerate(((ya_ref, ca_ref, wa_ref, dua_ref),
                                                             (yg_ref, cg_ref, wg_ref, dug_ref))):
                d0 = dy[ci, 0:ch, :]
                du = (w_ref[2:3, :] * d0 + w_ref[1:2, :] * dy[ci, pl.ds(1, ch), :]
                      + w_ref[0:1, :] * dy[ci, pl.ds(2, ch), :])
                du_ref[c0:c0 + ch, :] = du.astype(BF16)
                for k in range(3):
                    sums[half][k] += jnp.sum(d0 * cat[pl.ds(c0 + HALO - 2 + k, ch), :], axis=0, keepdims=True)
                sums[half][3] += jnp.sum(d0, axis=0, keepdims=True)
        for half, acc in enumerate((acca_ref, accg_ref)):
            for k in range(4):
                acc[k:k + 1, cols] += sums[half][k]
        ht = h_ref[0]
        dwa_sc[:, cols] += _nn(ht, dua_ref[...])
        dwg_sc[:, cols] += _nn(ht, dug_ref[...])

        @pl.when((t == t_all // tm - 1) & (n == nc - 1))
        def _():
            copies = [pltpu.make_async_copy(dwa_sc, dwa_ref, out_sems.at[0]),
                      pltpu.make_async_copy(dwg_sc, dwg_ref, out_sems.at[1])]
            for cp in copies:
                cp.start()
            for cp in copies:
                cp.wait()

    nrow = t_all // 16
    cur = pl.BlockSpec((tm, tn), lambda t, n: (t, n))
    prev = pl.BlockSpec((16, tn), lambda t, n: (jnp.maximum(t * (tm // 16) - 1, 0), n))
    nxt = pl.BlockSpec((16, tn), lambda t, n: (jnp.minimum((t + 1) * (tm // 16), nrow - 1), n))
    vec = lambda r, off: pl.BlockSpec((r, tn), lambda t, n: (0, n + off))
    acc = pl.BlockSpec((8, D_FF), lambda t, n: (0, 0))
    return pl.pallas_call(
        body, name="ffn_gate_bwd", grid=(t_all // tm, nc),
        in_specs=[cur, prev, nxt, cur, prev, nxt, cur, nxt, vec(3, 0), vec(3, nc), vec(1, 0), vec(1, nc),
                  pl.BlockSpec((1, D_MODEL, tm), lambda t, n: (t, 0, 0))],
        out_specs=[cur, cur, acc, acc, ANY, ANY],
        out_shape=[jax.ShapeDtypeStruct((t_all, D_FF), BF16), jax.ShapeDtypeStruct((t_all, D_FF), BF16),
                   jax.ShapeDtypeStruct((8, D_FF), F32), jax.ShapeDtypeStruct((8, D_FF), F32),
                   jax.ShapeDtypeStruct((D_MODEL, D_FF), F32), jax.ShapeDtypeStruct((D_MODEL, D_FF), F32)],
        scratch_shapes=[pltpu.VMEM((tm + 2 * HALO, tn), F32)] * 2
        + [pltpu.VMEM((tm // FFN_CHUNK, FFN_CHUNK + HALO, tn), F32)] * 2
        + [pltpu.VMEM((D_MODEL, D_FF), F32)] * 2 + [pltpu.SemaphoreType.DMA((2,))],
        compiler_params=_params(),
    )(u_a, u_a, u_a, u_g, u_g, u_g, dfi, dfi, conv_w, conv_w, conv_b, conv_b, h2t)


def _ffn_down(ffn_in, w_down, xh1, ln1_g, ln1_b, ada3, ln2_g, ln2_b, target, seq):
    t_all = xh1.shape[0]
    tm = 256
    nts = seq // tm

    def body(f_ref, w_ref, xh_ref, g1_ref, b1_ref, ada_ref, g2_ref, b2_ref, tg_ref, dr2_ref, acc_ref):
        i = pl.program_id(0)

        @pl.when(i == 0)
        def _():
            acc_ref[...] = jnp.zeros_like(acc_ref)
        ffn = _nn(f_ref[...], w_ref[...])
        x1 = xh_ref[...] * g1_ref[...] + b1_ref[...]
        r2 = ALPHA * x1 + ada_ref[0, 5:6, :] * ffn
        d = r2 - jnp.mean(r2, axis=1, keepdims=True)
        rstd = lax.rsqrt(jnp.mean(d * d, axis=1, keepdims=True) + LN_EPS)
        xh2 = d * rstd
        diff = xh2 * g2_ref[...] + b2_ref[...] - tg_ref[...]
        dy = diff * (1.0 / D_MODEL)
        dr2 = _layer_norm_bwd(dy * g2_ref[...], xh2, rstd)
        dr2_ref[...] = dr2
        acc_ref[0:1, :] += jnp.sum(dy * xh2, axis=0, keepdims=True)
        acc_ref[1:2, :] += jnp.sum(dy, axis=0, keepdims=True)
        acc_ref[2:3, :] += jnp.sum(diff * diff, axis=0, keepdims=True) * (0.5 / D_MODEL)
        acc_ref[pl.ds(8 + i // nts, 1), :] += jnp.sum(dr2 * ffn, axis=0, keepdims=True)

    tok = lambda w: pl.BlockSpec((tm, w), lambda i: (i, 0))
    vec = pl.BlockSpec((1, D_MODEL), lambda i: (0, 0))
    return pl.pallas_call(
        body, name="ffn_down", grid=(t_all // tm,),
        in_specs=[tok(D_FF), pl.BlockSpec(w_down.shape, lambda i: (0, 0)), tok(D_MODEL), vec, vec,
                  pl.BlockSpec((1, 6, D_MODEL), lambda i: (i // nts, 0, 0)), vec, vec, tok(D_MODEL)],
        out_specs=[tok(D_MODEL), pl.BlockSpec((16, D_MODEL), lambda i: (0, 0))],
        out_shape=[jax.ShapeDtypeStruct((t_all, D_MODEL), F32), jax.ShapeDtypeStruct((16, D_MODEL), F32)],
        compiler_params=_params(),
    )(ffn_in, w_down, xh1, ln1_g, ln1_b, ada3, ln2_g, ln2_b, target)


def _ffn_down_bwd(dr2, ada3, w_down, seq):
    t_all = dr2.shape[0]
    tm = 256
    nts = seq // tm

    def body(d_ref, ada_ref, w_ref, dffn_ref, dfi_ref):
        dffn = (d_ref[...] * ada_ref[0, 5:6, :]).astype(BF16)
        dffn_ref[...] = dffn
        dfi_ref[...] = _nt(dffn, w_ref[...]).astype(BF16)

    tok = lambda w: pl.BlockSpec((tm, w), lambda i: (i, 0))
    return pl.pallas_call(
        body, name="ffn_down_bwd", grid=(t_all // tm,),
        in_specs=[tok(D_MODEL), pl.BlockSpec((1, 6, D_MODEL), lambda i: (i // nts, 0, 0)),
                  pl.BlockSpec(w_down.shape, lambda i: (0, 0))],
        out_specs=[tok(D_MODEL), tok(D_FF)],
        out_shape=[jax.ShapeDtypeStruct((t_all, D_MODEL), BF16), jax.ShapeDtypeStruct((t_all, D_FF), BF16)],
        compiler_params=_params(),
    )(dr2, ada3, w_down)


def _ffn_up_bwd(du_a, du_g, w_up, dr2, xh1, rs1, mix, ada3, ln1_g, ln1_b, seq):
    t_all = dr2.shape[0]
    tm = 256
    nts = seq // tm

    def body(da_ref, dg_ref, w_ref, dr2_ref, xh_ref, rs_ref, mix_ref, ada_ref, g_ref, b_ref, dr1_ref, dmix_ref,
             acc_ref):
        i = pl.program_id(0)

        @pl.when(i == 0)
        def _():
            acc_ref[...] = jnp.zeros_like(acc_ref)
        dh2 = _nt(da_ref[...], w_ref[:, :D_FF]) + _nt(dg_ref[...], w_ref[:, D_FF:])
        xh = xh_ref[...]
        x1 = xh * g_ref[...] + b_ref[...]
        dx1 = ALPHA * dr2_ref[...] + dh2 * (1.0 + ada_ref[0, 4:5, :])
        dr1 = _layer_norm_bwd(dx1 * g_ref[...], xh, rs_ref[:, 0:1])
        dr1_ref[...] = dr1
        dmix_ref[...] = (dr1 * ada_ref[0, 2:3, :]).astype(BF16)
        b = i // nts
        acc_ref[0:1, :] += jnp.sum(dx1 * xh, axis=0, keepdims=True)
        acc_ref[1:2, :] += jnp.sum(dx1, axis=0, keepdims=True)
        acc_ref[pl.ds(8 + b, 1), :] += jnp.sum(dh2 * x1, axis=0, keepdims=True)
        acc_ref[pl.ds(16 + b, 1), :] += jnp.sum(dh2, axis=0, keepdims=True)
        acc_ref[pl.ds(24 + b, 1), :] += jnp.sum(dr1 * mix_ref[...].astype(F32), axis=0, keepdims=True)

    tok = lambda w: pl.BlockSpec((tm, w), lambda i: (i, 0))
    vec = pl.BlockSpec((1, D_MODEL), lambda i: (0, 0))
    return pl.pallas_call(
        body, name="ffn_up_bwd", grid=(t_all // tm,),
        in_specs=[tok(D_FF), tok(D_FF), pl.BlockSpec(w_up.shape, lambda i: (0, 0)), tok(D_MODEL), tok(D_MODEL),
                  tok(LANES), tok(D_MODEL), pl.BlockSpec((1, 6, D_MODEL), lambda i: (i // nts, 0, 0)), vec, vec],
        out_specs=[tok(D_MODEL), tok(D_MODEL), pl.BlockSpec((32, D_MODEL), lambda i: (0, 0))],
        out_shape=[jax.ShapeDtypeStruct((t_all, D_MODEL), F32), jax.ShapeDtypeStruct((t_all, D_MODEL), BF16),
                   jax.ShapeDtypeStruct((32, D_MODEL), F32)],
        compiler_params=_params(),
    )(du_a, du_g, w_up, dr2, xh1, rs1, mix, ada3, ln1_g, ln1_b)


def _rows(a):
    return a[:, :N_HEADS].T


def _rope_freq():
    f = np.float32(ROPE_THETA) ** (-np.arange(0, ROPE_DIMS, 2, dtype=np.float32) / np.float32(ROPE_DIMS))
    return jnp.asarray(np.tile(f.astype(np.float32), LANES // (ROPE_DIMS // 2))[None, :])


def _local_step(x, positions, target, ada3, w_in, b_fgate, gn_a, gn_b, ln1_g, ln1_b, conv_b, ln2_g, ln2_b,
                late_shards):
    nbat, seq, _ = x.shape
    t_all = nbat * seq
    xf = x.reshape(t_all, D_MODEL)
    tg = target.reshape(t_all, D_MODEL)
    pos = positions.reshape(t_all, 1)
    freq = _rope_freq()

    wqkv = jnp.concatenate([w_in[:, :3 * WIDTH], w_in[:, 3 * WIDTH + N_HEADS:]], axis=1)
    wf16 = jnp.zeros((16, D_MODEL), BF16).at[:N_HEADS].set(w_in[:, 3 * WIDTH:3 * WIDTH + N_HEADS].T)
    bf = b_fgate.reshape(N_HEADS, 1)

    perms = [_perm_matrix(TOK_TM, d, tr) for tr in (False, True) for d in DILATIONS[1:]]
    h1, za, zb1, zb4, zb16, vt, fa_t = _inproj(xf, ada3, pos, wqkv, wf16, freq, perms, seq)
    zbs = [zb1, zb4.reshape(t_all, 3 * WIDTH), zb16.reshape(t_all, 3 * WIDTH)]
    f_row = _fgate_fwd(fa_t, bf, seq)
    f_col = jnp.zeros((t_all, LANES), F32).at[:, :N_HEADS].set(f_row.T)
    oa, lse_row_a, gathered = _fox_fwd(za, vt, f_col, seq, [late_shards[n] for n in LATE])
    w_out, w_up, conv_w, w_down = (_full_from_gathered(n, g) for n, g in zip(LATE, gathered))
    o3, l3 = zip(*[_dil_fwd(zb, seq, d) for zb, d in zip(zbs, DILATIONS)])
    ob, lse_b, lse_b4, lse_b16, merged, mix, xh1, rs1, h2, h2t = _mix_out(oa, o3, l3, gn_a, gn_b, w_out, xf, ada3, ln1_g,
                                                                      ln1_b, perms, seq)
    u_a, u_g, ffn_in = _ffn_up_gate(h2, w_up, conv_w, conv_b, seq)
    dr2, acc2 = _ffn_down(ffn_in, w_down, xh1, ln1_g, ln1_b, ada3, ln2_g, ln2_b, tg, seq)

    dffn, dfi = _ffn_down_bwd(dr2, ada3, w_down, seq)
    d_w_down = _matmul_tn(dffn, ffn_in, 512, 512, "dw_down").T
    du_a, du_g, acc_ca, acc_cg, dw_up_a, dw_up_g = _ffn_gate_bwd(u_a, u_g, dfi, conv_w, conv_b, h2t, seq)
    dr1, dmix, acc1 = _ffn_up_bwd(du_a, du_g, w_up, dr2, xh1, rs1, mix, ada3, ln1_g, ln1_b, seq)
    d_w_up = jnp.concatenate([dw_up_a, dw_up_g], axis=1)

    doa, dob, dob4, dob16, dl_a, dl_b, dl_b4, dl_b16, acc_gn = _mix_out_bwd(dmix, w_out, oa, ob, gn_a, gn_b, perms, seq)
    d_w_out = _matmul_tn(merged, dmix, 512, 512, "dw_out")
    late_grads = dict(w_out=d_w_out, w_up=d_w_up, conv_w=jnp.concatenate([acc_ca[0:3], acc_cg[0:3]], axis=1),
                      w_down=d_w_down)
    dka, dva, df_k, dqt, df_q, late_parts = _fox_bwd(za, doa, f_col, lse_row_a, _rows(dl_a), seq,
                                                     [_payload(n, _dest_major(n, late_grads[n])) for n in LATE])
    dfa_t, dbf = _fgate_bwd(_rows(df_k) + df_q, fa_t, bf, seq)
    flat = lambda a: a.reshape(t_all, a.shape[-1])
    dil = []
    for zb, d, do, lse, dl in zip(zbs, DILATIONS, (dob, flat(dob4), flat(dob16)),
                                  (lse_b, flat(lse_b4), flat(lse_b16)), (dl_b, flat(dl_b4), flat(dl_b16))):
        dil.append(_dil_bwd(zb, do, lse, dl, seq, d))
    dfa16 = jnp.zeros((16, t_all), BF16).at[:N_HEADS].set(dfa_t.astype(BF16))
    grad_x, dz, acc0 = _inproj_bwd(dqt, dka, dva, dil[0], dil[1], dil[2], dfa16, pos, wqkv, wf16, freq, perms, dr1, xf,
                                   ada3, seq)
    d_wqkv = _matmul_tn(h1, dz, 512, 512, "dw_in")
    d_wf = _matmul_rows(dfa16, h1, 512, "dw_fgate")[:N_HEADS].T
    d_w_in = jnp.concatenate([d_wqkv[:, :3 * WIDTH], d_wf, d_wqkv[:, 3 * WIDTH:]], axis=1)

    dada = jnp.concatenate([acc0[8:8 + nbat], acc0[:nbat], acc1[24:24 + nbat], acc1[16:16 + nbat], acc1[8:8 + nbat],
                            acc2[8:8 + nbat]], axis=1)

    grads = dict(
        dada=dada, b_ada=jnp.sum(dada, axis=0, keepdims=True), w_in=d_w_in, b_fgate=dbf[:, 0][None, :],
        gn_a=acc_gn[0:1, :WIDTH], gn_b=acc_gn[0:1, WIDTH:], ln1_g=acc1[0:1], ln1_b=acc1[1:2],
        conv_b=jnp.concatenate([acc_ca[3:4], acc_cg[3:4]], axis=1), ln2_g=acc2[0:1], ln2_b=acc2[1:2])
    return acc2[2:3], grad_x.reshape(x.shape), grads, dict(zip(LATE, late_parts))


LATE = ("w_out", "w_up", "conv_w", "w_down")
BIG = ("w_ada", "w_in") + LATE
COLUMN_SHARDED = ("w_ada", "w_in", "w_up", "conv_w")


def _payload(name, a):
    return a if name == "conv_w" else a.astype(BF16)
SMALL = ("b_ada", "b_fgate", "gn_a", "gn_b", "ln1_g", "ln1_b", "conv_b", "ln2_g", "ln2_b")
ADAM_ROWS = dict(w_ada=256, w_in=256, w_out=128, w_up=256, conv_w=3, w_down=176)
SMALL_ROWS = 24


def _full_from_gathered(name, g):
    if name in COLUMN_SHARDED:
        return g.transpose(1, 0, 2).reshape(g.shape[1], N_DEV * g.shape[2])
    return g.reshape(N_DEV * g.shape[1], g.shape[2])


def _dest_major(name, full):
    if name in COLUMN_SHARDED:
        r, cfull = full.shape
        return full.reshape(r, N_DEV, cfull // N_DEV).transpose(1, 0, 2)
    return full.reshape(N_DEV, full.shape[0] // N_DEV, full.shape[1])


def _pack_small(vals, extra=None):
    parts = [vals[n].reshape(-1) for n in SMALL]
    if extra is not None:
        parts.append(extra.reshape(-1))
    flat = jnp.concatenate(parts)
    return jnp.pad(flat, (0, SMALL_ROWS * D_MODEL - flat.shape[0])).reshape(SMALL_ROWS, D_MODEL)


def _unpack_small(packed, like):
    flat = packed.reshape(-1)
    out, off = {}, 0
    for n in SMALL:
        size = like[n].size
        out[n] = flat[off:off + size].reshape(like[n].shape)
        off += size
    return out, flat[off:off + D_MODEL]


def kernel(x, c, positions, w_ada, b_ada, w_in, b_fgate, gn_a, gn_b, w_out, ln1_g, ln1_b, w_up, conv_w, conv_b, w_down, ln2_g, ln2_b, loss_target, m_w_ada, m_b_ada, m_w_in, m_b_fgate, m_gn_a, m_gn_b, m_w_out, m_ln1_g, m_ln1_b, m_w_up, m_conv_w, m_conv_b, m_w_down, m_ln2_g, m_ln2_b, v_w_ada, v_b_ada, v_w_in, v_b_fgate, v_gn_a, v_gn_b, v_w_out, v_ln1_g, v_ln1_b, v_w_up, v_conv_w, v_conv_b, v_w_down, v_ln2_g, v_ln2_b):
    w = dict(w_ada=w_ada[0], b_ada=b_ada, w_in=w_in[0], b_fgate=b_fgate, gn_a=gn_a, gn_b=gn_b, w_out=w_out[0],
             ln1_g=ln1_g, ln1_b=ln1_b, w_up=w_up[0], conv_w=conv_w[0], conv_b=conv_b, w_down=w_down[0], ln2_g=ln2_g,
             ln2_b=ln2_b)
    m = dict(w_ada=m_w_ada[0], b_ada=m_b_ada, w_in=m_w_in[0], b_fgate=m_b_fgate, gn_a=m_gn_a, gn_b=m_gn_b,
             w_out=m_w_out[0], ln1_g=m_ln1_g, ln1_b=m_ln1_b, w_up=m_w_up[0], conv_w=m_conv_w[0], conv_b=m_conv_b,
             w_down=m_w_down[0], ln2_g=m_ln2_g, ln2_b=m_ln2_b)
    v = dict(w_ada=v_w_ada[0], b_ada=v_b_ada, w_in=v_w_in[0], b_fgate=v_b_fgate, gn_a=v_gn_a, gn_b=v_gn_b,
             w_out=v_w_out[0], ln1_g=v_ln1_g, ln1_b=v_ln1_b, w_up=v_w_up[0], conv_w=v_conv_w[0], conv_b=v_conv_b,
             w_down=v_w_down[0], ln2_g=v_ln2_g, ln2_b=v_ln2_b)

    nbat = x.shape[0]
    me = 4 * lax.axis_index("x") + 2 * lax.axis_index("y") + lax.axis_index("c")
    ada_cols = w["w_ada"].shape[1]

    c_all, w_in_all = _gather_two_level([c, _payload("w_in", w["w_in"])], "weight_gather")
    c_all = c_all.reshape(N_DEV * nbat, D_MODEL)
    ada_mine = _ada_fwd(c_all, w["w_ada"], lax.dynamic_slice(b_ada, (0, me * ada_cols), (1, ada_cols)))
    (ada_parts,) = _exchange([ada_mine.reshape(N_DEV, nbat, ada_cols)], [False], "ada_exchange")
    ada3 = ada_parts.transpose(1, 0, 2).reshape(nbat, 6, D_MODEL)

    loss_lanes, grad_x, g_local, parts = _local_step(
        x, positions, loss_target, ada3, _full_from_gathered("w_in", w_in_all), b_fgate, gn_a, gn_b, ln1_g, ln1_b,
        conv_b, ln2_g, ln2_b, {n: _payload(n, w[n]) for n in LATE})

    parts["w_in"], dada_all, small_all = _exchange(
        [_payload("w_in", _dest_major("w_in", g_local["w_in"])), g_local["dada"], _pack_small(g_local, loss_lanes)],
        [False, True, True], "grad_exchange")
    dada_cols = lax.dynamic_slice(dada_all.reshape(N_DEV * nbat, 6 * D_MODEL), (0, me * ada_cols),
                                  (N_DEV * nbat, ada_cols))
    parts["w_ada"] = _ada_bwd(c_all, dada_cols)[None]

    grad, delta, new_m, new_v = {}, {}, {}, {}
    for n in BIG:
        grad[n], delta[n], new_m[n], new_v[n] = (
            a[None] for a in _adamw(parts[n], w[n], m[n], v[n], ADAM_ROWS[n], "adamw_" + n))
    packed = _adamw(small_all, _pack_small(w), _pack_small(m), _pack_small(v), SMALL_ROWS, "adamw_small")
    for dst, pk in zip((grad, delta, new_m, new_v), packed):
        vals, lanes = _unpack_small(pk, w)
        dst.update(vals)
        if dst is grad:
            loss = jnp.sum(lanes)

    order = ("w_ada", "b_ada", "w_in", "b_fgate", "gn_a", "gn_b", "w_out", "ln1_g", "ln1_b", "w_up", "conv_w", "conv_b",
             "w_down", "ln2_g", "ln2_b")
    return (loss, grad_x, *[grad[n] for n in order], *[delta[n] for n in order], *[new_m[n] for n in order],
            *[new_v[n] for n in order])
```

```python
import functools

import numpy as np
import jax
import jax.numpy as jnp
from jax import lax
from jax.experimental import pallas as pl
from jax.experimental.pallas import tpu as pltpu

F32, BF16 = jnp.float32, jnp.bfloat16
MESH = pl.DeviceIdType.MESH
ANY = pl.BlockSpec(memory_space=pl.ANY)

D_MODEL = 1024
N_HEADS = 8
HEAD_DIM = 64
WIDTH = 512
D_FF = 2816
N_DEV = 8
ROPE_DIMS = 16
ROPE_THETA = 500000.0
ALPHA = 2.0 ** 0.25
LN_EPS = 1e-5
RMS_EPS = 1e-6
NEG = -1e30
Q_SCALE = 0.125
BLK = 128
LANES = 128
VMEM_LIMIT_BYTES = 56 * 1024 * 1024

ADAM_LR, ADAM_B1, ADAM_B2, ADAM_EPS, ADAM_WD, ADAM_STEP = 0.001, 0.9, 0.999, 1e-08, 0.01, 10


def _params(vmem=VMEM_LIMIT_BYTES):
    return pltpu.CompilerParams(vmem_limit_bytes=vmem)


def _nn(a, b):
    return jnp.dot(a, b, preferred_element_type=F32)


def _nt(a, b):
    return lax.dot_general(a, b, (((1,), (1,)), ((), ())), preferred_element_type=F32)


def _tn(a, b):
    return lax.dot_general(a, b, (((0,), (0,)), ((), ())), preferred_element_type=F32)


def _head_mats():
    r = lax.broadcasted_iota(jnp.int32, (LANES, WIDTH), 0)
    c = lax.broadcasted_iota(jnp.int32, (LANES, WIDTH), 1)
    e = ((c >> 6) == r).astype(BF16)
    r2 = lax.broadcasted_iota(jnp.int32, (WIDTH, LANES), 0)
    c2 = lax.broadcasted_iota(jnp.int32, (WIDTH, LANES), 1)
    et = ((r2 >> 6) == c2).astype(BF16)
    return e, et


def _split3(x):
    hi = x.astype(BF16)
    r = x - hi.astype(F32)
    mid = r.astype(BF16)
    return hi, mid, (r - mid.astype(F32)).astype(BF16)


def _hexp(w, e):
    return sum(_nn(part, e) for part in _split3(w)[:2])


def _hsum(x, et):
    return sum(_nn(part, et) for part in _split3(x)[:2])


def _perm_matrix(rows, d, transpose):
    i = np.arange(rows)
    j = (i % (rows // d)) * d + i // (rows // d)
    p = np.zeros((rows, rows), np.float32)
    p[i, j] = 1.0
    return jnp.asarray(p.T if transpose else p, BF16)


def _permute_f32(p, x):
    return sum(_nn(p, part) for part in _split3(x))


def _store_classes(ref, y, d):
    n = y.shape[0] // d
    for r in range(d):
        ref[r] = y[r * n:(r + 1) * n, :]


def _load_classes(ref, d):
    return jnp.concatenate([ref[r] for r in range(d)], axis=0)


def _rope_tabs(pos_ref, fr_ref, sign):
    ang = pos_ref[...].astype(F32) * fr_ref[...]
    lane = lax.broadcasted_iota(jnp.int32, ang.shape, 1) & (HEAD_DIM - 1)
    m1 = lane < ROPE_DIMS // 2
    m2 = (lane >= ROPE_DIMS // 2) & (lane < ROPE_DIMS)
    cos = jnp.cos(ang)
    sin = jnp.sin(ang) * sign
    return (jnp.where(m1 | m2, cos, 1.0), jnp.where(m1, -sin, 0.0), jnp.where(m2, sin, 0.0))


def _rope(z, tabs):
    c, s1, s2 = tabs
    parts = []
    for p in range(z.shape[1] // LANES):
        zp = z[:, LANES * p:LANES * (p + 1)]
        parts.append(zp * c + pltpu.roll(zp, LANES - 8, 1) * s1 + pltpu.roll(zp, 8, 1) * s2)
    return jnp.concatenate(parts, axis=1)


def _half_masks(rows):
    lane = lax.broadcasted_iota(jnp.int32, (rows, LANES), 1)
    lo = lane < HEAD_DIM
    return lo, jnp.logical_not(lo)


def _layer_norm_bwd(dxh, xh, rstd):
    m1 = jnp.mean(dxh, axis=1, keepdims=True)
    m2 = jnp.mean(dxh * xh, axis=1, keepdims=True)
    return rstd * (dxh - m1 - xh * m2)


def _coords():
    return lax.axis_index("x"), lax.axis_index("y"), lax.axis_index("c")


def _peer(x, y, c, k):
    return (1 - x if k & 4 else x, 1 - y if k & 2 else y, 1 - c if k & 1 else c)


def _comm_sems(n):
    return [pltpu.SemaphoreType.DMA((N_DEV - 1, n)), pltpu.SemaphoreType.DMA((N_DEV - 1, n)),
            pltpu.SemaphoreType.DMA((n,))]


def _comm_copies(ins, outs, to_all, sems):
    send_sems, recv_sems, local_sems = sems
    x, y, c = _coords()
    me = 4 * x + 2 * y + c
    copies = [pltpu.make_async_copy(ins[t] if to_all[t] else ins[t].at[me], outs[t].at[me], local_sems.at[t])
              for t in range(len(ins))]
    for k in range(1, N_DEV):
        px, py, pc = _peer(x, y, c, k)
        dest = 4 * px + 2 * py + pc
        for t in range(len(ins)):
            copies.append(pltpu.make_async_remote_copy(
                src_ref=ins[t] if to_all[t] else ins[t].at[dest], dst_ref=outs[t].at[me],
                send_sem=send_sems.at[k - 1, t], recv_sem=recv_sems.at[k - 1, t],
                device_id=(px, py, pc), device_id_type=MESH))
    return copies


def _comm_out_shapes(ins, to_all):
    return [jax.ShapeDtypeStruct(((N_DEV,) + a.shape) if ta else a.shape, a.dtype) for a, ta in zip(ins, to_all)]


def _exchange(ins, to_all, name):
    n = len(ins)

    def body(*refs):
        copies = _comm_copies(refs[:n], refs[n:2 * n], to_all, refs[2 * n:])
        for cp in copies:
            cp.start()
        for cp in copies:
            cp.wait()

    return pl.pallas_call(
        body, name=name, out_shape=_comm_out_shapes(ins, to_all), in_specs=[ANY] * n, out_specs=[ANY] * n,
        scratch_shapes=_comm_sems(n),
    )(*ins)


def _gather_two_level(ins, name):
    n = len(ins)

    def body(*refs):
        srcs, outs = refs[:n], refs[n:2 * n]
        send_sems, recv_sems, local_sems = refs[2 * n:]
        x, y, c = _coords()
        me = 4 * x + 2 * y + c
        sibling = (x, y, 1 - c)
        chips = [(1 - x, y), (x, 1 - y), (1 - x, 1 - y)]
        slot = lambda px, py, pc: 4 * px + 2 * py + pc

        def copy(k, t, block, to, own=False):
            return pltpu.make_async_remote_copy(
                src_ref=srcs[t] if own else outs[t].at[block], dst_ref=outs[t].at[block],
                send_sem=send_sems.at[k, t], recv_sem=recv_sems.at[k, t], device_id=to, device_id_type=MESH)

        local = [pltpu.make_async_copy(srcs[t], outs[t].at[me], local_sems.at[t]) for t in range(n)]
        first = [copy(0, t, me, sibling, own=True) for t in range(n)]
        first += [copy(1 + j, t, me, (*chip, c), own=True) for j, chip in enumerate(chips) for t in range(n)]
        for cp in local + first:
            cp.start()
        passed = []
        for j, chip in enumerate(chips):
            for t in range(n):
                copy(1 + j, t, slot(*chip, c), (x, y, c)).wait_recv()
                cp = copy(4 + j, t, slot(*chip, c), sibling)
                cp.start()
                passed.append(cp)
        for t in range(n):
            copy(0, t, slot(x, y, 1 - c), (x, y, c)).wait_recv()
            for j, chip in enumerate(chips):
                copy(4 + j, t, slot(*chip, 1 - c), (x, y, c)).wait_recv()
        for cp in first + passed:
            cp.wait_send()
        for cp in local:
            cp.wait()

    return pl.pallas_call(
        body, name=name, out_shape=_comm_out_shapes(ins, [True] * n), in_specs=[ANY] * n, out_specs=[ANY] * n,
        scratch_shapes=_comm_sems(n),
    )(*ins)


def _adamw(parts, w, m, v, rows, name):
    n_parts, r_all, cols = parts.shape
    c1 = 1.0 - ADAM_B1 ** ADAM_STEP
    c2 = 1.0 - ADAM_B2 ** ADAM_STEP

    def body(p_ref, w_ref, m_ref, v_ref, g_ref, d_ref, mo_ref, vo_ref):
        g = p_ref[0].astype(F32)
        for s in range(1, n_parts):
            g = g + p_ref[s].astype(F32)
        mn = ADAM_B1 * m_ref[...] + (1.0 - ADAM_B1) * g
        vn = ADAM_B2 * v_ref[...] + (1.0 - ADAM_B2) * (g * g)
        m_hat = mn / c1
        v_hat = vn / c2
        g_ref[...] = g
        d_ref[...] = -ADAM_LR * (m_hat / (jnp.sqrt(v_hat) + ADAM_EPS) + ADAM_WD * w_ref[...])
        mo_ref[...] = mn
        vo_ref[...] = vn

    spec = pl.BlockSpec((rows, cols), lambda i: (i, 0))
    return pl.pallas_call(
        body, name=name, grid=(r_all // rows,),
        in_specs=[pl.BlockSpec((n_parts, rows, cols), lambda i: (0, i, 0)), spec, spec, spec],
        out_specs=[spec] * 4, out_shape=[jax.ShapeDtypeStruct((r_all, cols), F32)] * 4,
        compiler_params=_params(),
    )(parts, w, m, v)


def _matmul_tn(a, b, chunk, tk, name):
    t_all, k1 = a.shape
    n = b.shape[1]

    def body(a_ref, b_ref, o_ref):
        @pl.when(pl.program_id(0) == 0)
        def _():
            o_ref[...] = jnp.zeros_like(o_ref)
        at = a_ref[...].astype(F32).T.astype(BF16)
        for j in range(0, n, chunk):
            cs = slice(j, min(j + chunk, n))
            o_ref[:, cs] += _nn(at, b_ref[:, cs])

    return pl.pallas_call(
        body, name=name, grid=(t_all // tk,),
        in_specs=[pl.BlockSpec((tk, k1), lambda t: (t, 0)), pl.BlockSpec((tk, n), lambda t: (t, 0))],
        out_specs=pl.BlockSpec((k1, n), lambda t: (0, 0)),
        out_shape=jax.ShapeDtypeStruct((k1, n), F32), compiler_params=_params(),
    )(a, b)


def _matmul_rows(a, b, tk, name):
    r, t_all = a.shape
    n = b.shape[1]

    def body(a_ref, b_ref, o_ref):
        @pl.when(pl.program_id(0) == 0)
        def _():
            o_ref[...] = jnp.zeros_like(o_ref)
        o_ref[...] += _nn(a_ref[...], b_ref[...])

    return pl.pallas_call(
        body, name=name, grid=(t_all // tk,),
        in_specs=[pl.BlockSpec((r, tk), lambda t: (0, t)), pl.BlockSpec((tk, n), lambda t: (t, 0))],
        out_specs=pl.BlockSpec((r, n), lambda t: (0, 0)),
        out_shape=jax.ShapeDtypeStruct((r, n), F32), compiler_params=_params(),
    )(a, b)


def _ada_fwd(c_all, w_ada, b_ada):
    whole = lambda a: pl.BlockSpec(a.shape, lambda j: (0, 0))

    def body(c_ref, w_ref, b_ref, o_ref):
        cv = c_ref[...]
        s = (cv * jax.nn.sigmoid(cv)).astype(BF16)
        o_ref[...] = _nn(s, w_ref[...].astype(BF16)) + b_ref[...]

    out = jax.ShapeDtypeStruct((c_all.shape[0], w_ada.shape[1]), F32)
    return pl.pallas_call(
        body, name="ada_fwd", grid=(1,), in_specs=[whole(c_all), whole(w_ada), whole(b_ada)], out_specs=whole(out),
        out_shape=out, compiler_params=_params(),
    )(c_all, w_ada, b_ada)


def _ada_bwd(c_all, dada):
    whole = lambda a: pl.BlockSpec(a.shape, lambda j: (0, 0))

    def body(c_ref, d_ref, o_ref):
        cv = c_ref[...]
        s = (cv * jax.nn.sigmoid(cv)).astype(BF16)
        o_ref[...] = _tn(s, d_ref[...].astype(BF16))

    out = jax.ShapeDtypeStruct((D_MODEL, dada.shape[1]), F32)
    return pl.pallas_call(
        body, name="ada_bwd", grid=(1,), in_specs=[whole(c_all), whole(dada)], out_specs=whole(out), out_shape=out,
        compiler_params=_params(),
    )(c_all, dada)


TOK_TM = 256
DILATIONS = (1, 4, 16)


def _class_spec(d, width, nts):
    return pl.BlockSpec((d, TOK_TM // d, width), lambda i: (i // nts, i % nts, 0))


def _class_shape(t_all, seq, d, width, dtype):
    return jax.ShapeDtypeStruct((t_all // seq * d, seq // d, width), dtype)


def _inproj(x, ada3, pos, wqkv, wf16, freq, perms, seq):
    t_all = x.shape[0]
    tm = TOK_TM
    nts = seq // tm

    def body(x_ref, ada_ref, pos_ref, w_ref, wf_ref, fr_ref, p4_ref, p16_ref, h1_ref, za_ref, zb_ref, zb4_ref,
             zb16_ref, vt_ref, fa_ref):
        h1 = (x_ref[...] * (1.0 + ada_ref[0, 1:2, :]) + ada_ref[0, 0:1, :]).astype(BF16)
        h1_ref[...] = h1
        tabs = _rope_tabs(pos_ref, fr_ref, 1.0)
        for n in range(6):
            z = _nn(h1, w_ref[:, n * WIDTH:(n + 1) * WIDTH])
            if n in (3, 4):
                z = _rope(z, tabs)
            if n in (0, 3):
                z = z * Q_SCALE
            if n == 2:
                vt_ref[...] = z.T.astype(BF16)
            dst = za_ref if n < 3 else zb_ref
            dst[:, (n % 3) * WIDTH:(n % 3 + 1) * WIDTH] = z.astype(BF16)
        fa_ref[...] = _nt(wf_ref[...], h1)[:N_HEADS]
        zb = zb_ref[...]
        _store_classes(zb4_ref, _nn(p4_ref[...], zb).astype(BF16), 4)
        _store_classes(zb16_ref, _nn(p16_ref[...], zb).astype(BF16), 16)

    tok = lambda w: pl.BlockSpec((tm, w), lambda i: (i, 0))
    whole = lambda a: pl.BlockSpec(a.shape, lambda i: (0, 0))
    return pl.pallas_call(
        body, name="inproj", grid=(t_all // tm,),
        in_specs=[tok(D_MODEL), pl.BlockSpec((1, 6, D_MODEL), lambda i: (i // nts, 0, 0)), tok(1), whole(wqkv),
                  whole(wf16), pl.BlockSpec((1, LANES), lambda i: (0, 0)), whole(perms[0]), whole(perms[1])],
        out_specs=[tok(D_MODEL), tok(3 * WIDTH), tok(3 * WIDTH), _class_spec(4, 3 * WIDTH, nts),
                   _class_spec(16, 3 * WIDTH, nts), pl.BlockSpec((WIDTH, tm), lambda i: (i // nts, i % nts)),
                   pl.BlockSpec((N_HEADS, tm), lambda i: (0, i))],
        out_shape=[jax.ShapeDtypeStruct((t_all, D_MODEL), BF16), jax.ShapeDtypeStruct((t_all, 3 * WIDTH), BF16),
                   jax.ShapeDtypeStruct((t_all, 3 * WIDTH), BF16), _class_shape(t_all, seq, 4, 3 * WIDTH, BF16),
                   _class_shape(t_all, seq, 16, 3 * WIDTH, BF16),
                   jax.ShapeDtypeStruct((t_all // seq * WIDTH, seq), BF16),
                   jax.ShapeDtypeStruct((N_HEADS, t_all), F32)],
        compiler_params=_params(),
    )(x, ada3, pos, wqkv, wf16, freq, perms[0], perms[1])


def _chunk_rows(a_t, seq):
    t_all = a_t.shape[1]
    return a_t.reshape(N_HEADS, t_all // seq, seq // LANES, LANES).transpose(1, 0, 2, 3).reshape(-1, LANES)


def _unchunk_rows(a, seq):
    nbat = a.shape[0] * LANES // (N_HEADS * seq)
    return a.reshape(nbat, N_HEADS, seq // LANES, LANES).transpose(1, 0, 2, 3).reshape(N_HEADS, nbat * seq)


def _chunk_carry(tot, nchunk, later):
    rows = tot.shape[0]
    r = lax.broadcasted_iota(jnp.int32, (rows, rows), 0)
    c = lax.broadcasted_iota(jnp.int32, (rows, rows), 1)
    sel = ((r // nchunk) == (c // nchunk)) & ((c > r) if later else (c < r))
    mat = sel.astype(BF16)
    return sum(_nn(mat, part) for part in _split3(jnp.broadcast_to(tot, (rows, LANES))))


def _fgate_fwd(fa_t, bf, seq):
    x = _chunk_rows(fa_t, seq)
    rows = x.shape[0]
    nchunk = seq // LANES
    bias = jnp.broadcast_to(bf.reshape(1, N_HEADS, 1), (rows // (N_HEADS * nchunk), N_HEADS, nchunk)).reshape(rows, 1)

    def body(x_ref, b_ref, f_ref):
        lane = lax.broadcasted_iota(jnp.int32, (rows, LANES), 1)
        xv = x_ref[...] + b_ref[...]
        lf = jnp.minimum(xv, 0.0) - jnp.log(1.0 + jnp.exp(-jnp.abs(xv)))
        for s in (1, 2, 4, 8, 16, 32, 64):
            lf = lf + jnp.where(lane >= s, pltpu.roll(lf, s, 1), 0.0)
        f_ref[...] = lf + _chunk_carry(lf[:, LANES - 1:LANES], nchunk, False)

    whole = lambda a: pl.BlockSpec(a.shape, lambda i: (0, 0))
    out = pl.pallas_call(
        body, name="fgate_fwd", grid=(1,), in_specs=[whole(x), whole(bias)], out_specs=whole(x),
        out_shape=jax.ShapeDtypeStruct(x.shape, F32), compiler_params=_params(),
    )(x, bias)
    return _unchunk_rows(out, seq)


def _fgate_bwd(df_t, fa_t, bf, seq):
    d_in = _chunk_rows(df_t, seq)
    x = _chunk_rows(fa_t, seq)
    rows = x.shape[0]
    nchunk = seq // LANES
    bias = jnp.broadcast_to(bf.reshape(1, N_HEADS, 1), (rows // (N_HEADS * nchunk), N_HEADS, nchunk)).reshape(rows, 1)

    def body(d_ref, x_ref, b_ref, o_ref, s_ref):
        lane = lax.broadcasted_iota(jnp.int32, (rows, LANES), 1)
        d = d_ref[...]
        for s in (1, 2, 4, 8, 16, 32, 64):
            d = d + jnp.where(lane < LANES - s, pltpu.roll(d, LANES - s, 1), 0.0)
        d = d + _chunk_carry(d[:, 0:1], nchunk, True)
        dfa = d * jax.nn.sigmoid(-(x_ref[...] + b_ref[...]))
        o_ref[...] = dfa
        g = lax.broadcasted_iota(jnp.int32, (2 * N_HEADS, rows), 0)
        r = lax.broadcasted_iota(jnp.int32, (2 * N_HEADS, rows), 1)
        group = (((r // nchunk) % N_HEADS) == g).astype(BF16)
        per_head = sum(_nn(group, part) for part in _split3(dfa))[:N_HEADS]
        s_ref[...] = jnp.broadcast_to(jnp.sum(per_head, axis=1, keepdims=True), (N_HEADS, LANES))

    whole = lambda a: pl.BlockSpec(a.shape, lambda i: (0, 0))
    dfa, sums = pl.pallas_call(
        body, name="fgate_bwd", grid=(1,), in_specs=[whole(d_in), whole(x), whole(bias)],
        out_specs=[whole(x), pl.BlockSpec((N_HEADS, LANES), lambda i: (0, 0))],
        out_shape=[jax.ShapeDtypeStruct(x.shape, F32), jax.ShapeDtypeStruct((N_HEADS, LANES), F32)],
        compiler_params=_params(),
    )(d_in, x, bias)
    return _unchunk_rows(dfa, seq), sums


FOX_T = 256


def _fox_prep(dst, src_ref, lo, hi):
    for p in range(4):
        v = src_ref[:, LANES * p:LANES * (p + 1)]
        dst[2 * p] = jnp.where(lo, v, jnp.zeros_like(v))
        dst[2 * p + 1] = jnp.where(hi, v, jnp.zeros_like(v))


def _fox_fwd(za, vt, f_col, seq, shards):
    t_all = za.shape[0]
    tq = FOX_T
    nq = seq // tq
    nbat = t_all // seq
    n = len(shards)
    to_all = [True] * n

    def body(*refs):
        q_ref, k_ref, vt_ref, fc_ref = refs[:4]
        o_ref, lse_ref = refs[4 + n:6 + n]
        qm_sc, m_sc, l_sc, acc_sc, a_sc, st_sc, pe_sc = refs[6 + 2 * n:13 + 2 * n]
        comm = (refs[4:4 + n], refs[6 + n:6 + 2 * n], to_all, refs[13 + 2 * n:])
        i = pl.program_id(1)

        @pl.when((pl.program_id(0) == 0) & (i == 0))
        def _():
            for cp in _comm_copies(*comm):
                cp.start()
        lo, hi = _half_masks(tq)
        r = lax.broadcasted_iota(jnp.int32, (tq, tq), 0)
        c = lax.broadcasted_iota(jnp.int32, (tq, tq), 1)
        tri = c >= r
        _fox_prep(qm_sc, q_ref, lo, hi)
        m_sc[...] = jnp.full(m_sc.shape, NEG, F32)
        l_sc[...] = jnp.zeros_like(l_sc)
        acc_sc[...] = jnp.zeros_like(acc_sc)

        def block(j, masked):
            sl = pl.ds(pl.multiple_of(j * tq, tq), tq)
            for p in range(4):
                kj = k_ref[sl, LANES * p:LANES * (p + 1)]
                for h in (2 * p, 2 * p + 1):
                    st = _nt(kj, qm_sc[h]) - fc_ref[sl, h:h + 1]
                    st_sc[h] = jnp.where(tri, st, NEG) if masked else st
            for h in range(N_HEADS):
                st = st_sc[h]
                m = m_sc[h:h + 1, :]
                mn = jnp.maximum(m, jnp.max(st, axis=0, keepdims=True))
                a = jnp.exp(m - mn)
                pe = jnp.exp(st - mn)
                m_sc[h:h + 1, :] = mn
                a_sc[h:h + 1, :] = a
                l_sc[h:h + 1, :] = a * l_sc[h:h + 1, :] + jnp.sum(pe, axis=0, keepdims=True)
                pe_sc[h] = pe.astype(BF16)
            for h in range(N_HEADS):
                acc_sc[h] = a_sc[h:h + 1, :] * acc_sc[h] + _nn(vt_ref[HEAD_DIM * h:HEAD_DIM * (h + 1), sl], pe_sc[h])

        def step(j, carry):
            block(j, False)
            return carry

        lax.fori_loop(0, i, step, 0)
        block(i, True)
        lse_ref[...] = m_sc[...] + jnp.log(l_sc[...])
        for p in range(4):
            ot = jnp.concatenate([acc_sc[h] / l_sc[h:h + 1, :] for h in (2 * p, 2 * p + 1)], axis=0)
            o_ref[:, LANES * p:LANES * (p + 1)] = ot.T

        @pl.when((pl.program_id(0) == nbat - 1) & (i == nq - 1))
        def _():
            for cp in _comm_copies(*comm):
                cp.wait()

    res = pl.pallas_call(
        body, name="fox_fwd", grid=(nbat, nq),
        in_specs=[pl.BlockSpec((tq, WIDTH), lambda b, i: (b * nq + i, 0)),
                  pl.BlockSpec((seq, WIDTH), lambda b, i: (b, 1)), pl.BlockSpec((WIDTH, seq), lambda b, i: (b, 0)),
                  pl.BlockSpec((seq, LANES), lambda b, i: (b, 0))] + [ANY] * n,
        out_specs=[pl.BlockSpec((tq, WIDTH), lambda b, i: (b * nq + i, 0)),
                   pl.BlockSpec((N_HEADS, tq), lambda b, i: (0, b * nq + i))] + [ANY] * n,
        out_shape=[jax.ShapeDtypeStruct((t_all, WIDTH), F32), jax.ShapeDtypeStruct((N_HEADS, t_all), F32)]
        + _comm_out_shapes(shards, to_all),
        scratch_shapes=[pltpu.VMEM((N_HEADS, tq, LANES), BF16), pltpu.VMEM((N_HEADS, tq), F32),
                        pltpu.VMEM((N_HEADS, tq), F32), pltpu.VMEM((N_HEADS, HEAD_DIM, tq), F32),
                        pltpu.VMEM((N_HEADS, tq), F32), pltpu.VMEM((N_HEADS, tq, tq), F32),
                        pltpu.VMEM((N_HEADS, tq, tq), BF16)] + _comm_sems(n),
        compiler_params=_params(),
    )(za, za, vt, f_col, *shards)
    return res[0], res[1], res[2:]


def _fox_bwd(za, do, f_col, lse_row, dl_row, seq, grads):
    t_all = za.shape[0]
    tk = FOX_T
    nk = seq // tk
    nbat = t_all // seq
    n = len(grads)
    to_all = [False] * n

    def body(*refs):
        k_ref, v_ref, q_ref, do_ref, fc_ref, lr_ref, dr_ref = refs[:7]
        dk_ref, dv_ref, df_ref, dqt_ref, dfq_ref = refs[7 + n:12 + n]
        km_sc, vm_sc, fk_sc, dk_sc, dv_sc, cs_sc, kt_sc, st_sc, dp_sc, pt_sc, ds_sc = refs[12 + 2 * n:23 + 2 * n]
        comm = (refs[7:7 + n], refs[12 + n:12 + 2 * n], to_all, refs[23 + 2 * n:])
        j = pl.program_id(1)

        @pl.when(j == 0)
        def _():
            dqt_ref[...] = jnp.zeros_like(dqt_ref)
            dfq_ref[...] = jnp.zeros_like(dfq_ref)

        @pl.when((pl.program_id(0) == 0) & (j == 0))
        def _():
            for cp in _comm_copies(*comm):
                cp.start()
        lo, hi = _half_masks(tk)
        r = lax.broadcasted_iota(jnp.int32, (tk, tk), 0)
        c = lax.broadcasted_iota(jnp.int32, (tk, tk), 1)
        tri = c >= r
        _fox_prep(km_sc, k_ref, lo, hi)
        _fox_prep(vm_sc, v_ref, lo, hi)
        for h in range(N_HEADS):
            fk_sc[h] = jnp.broadcast_to(fc_ref[:, h:h + 1], (tk, tk))
        for p in range(4):
            kt_sc[p] = k_ref[:, LANES * p:LANES * (p + 1)].astype(F32).T.astype(BF16)
        dk_sc[...] = jnp.zeros_like(dk_sc)
        dv_sc[...] = jnp.zeros_like(dv_sc)
        cs_sc[...] = jnp.zeros_like(cs_sc)

        def block(i, masked):
            sl = pl.ds(pl.multiple_of(i * tk, tk), tk)
            for p in range(4):
                cs = slice(LANES * p, LANES * (p + 1))
                qi = q_ref[sl, cs]
                doi = do_ref[sl, cs]
                for h in (2 * p, 2 * p + 1):
                    st = _nt(km_sc[h], qi) - fk_sc[h] - lr_ref[h:h + 1, sl]
                    st_sc[h] = jnp.where(tri, st, NEG) if masked else st
                    dp_sc[h] = _nt(vm_sc[h], doi) - dr_ref[h:h + 1, sl]
            for h in range(N_HEADS):
                pt = jnp.exp(st_sc[h])
                dst = pt * dp_sc[h]
                pt_sc[h] = pt.astype(BF16)
                ds_sc[h] = dst.astype(BF16)
                cs_sc[h] += dst[:, :LANES] + dst[:, LANES:]
                dfq_ref[h:h + 1, sl] += jnp.sum(dst, axis=0, keepdims=True)
            for p in range(4):
                cs = slice(LANES * p, LANES * (p + 1))
                qi = q_ref[sl, cs]
                doi = do_ref[sl, cs]
                for h in (2 * p, 2 * p + 1):
                    dv_sc[h] += _nn(pt_sc[h], doi)
                    dk_sc[h] += _nn(ds_sc[h], qi)
                    kt = kt_sc[p, HEAD_DIM * (h % 2):HEAD_DIM * (h % 2 + 1), :]
                    dqt_ref[HEAD_DIM * h:HEAD_DIM * (h + 1), sl] += _nn(kt, ds_sc[h])

        def step(i, carry):
            block(i, False)
            return carry

        block(j, True)
        lax.fori_loop(j + 1, nk, step, 0)
        df_ref[...] = jnp.zeros_like(df_ref)
        for p in range(4):
            cs = slice(LANES * p, LANES * (p + 1))
            dk_ref[:, cs] = jnp.where(lo, dk_sc[2 * p], dk_sc[2 * p + 1]).astype(BF16)
            dv_ref[:, cs] = jnp.where(lo, dv_sc[2 * p], dv_sc[2 * p + 1]).astype(BF16)
            for h in (2 * p, 2 * p + 1):
                df_ref[:, h:h + 1] = -jnp.sum(cs_sc[h], axis=1, keepdims=True)

        @pl.when(j == nk - 1)
        def _():
            dqt_ref[...] = dqt_ref[...] * Q_SCALE

        @pl.when((pl.program_id(0) == nbat - 1) & (j == nk - 1))
        def _():
            for cp in _comm_copies(*comm):
                cp.wait()

    tile = lambda w, col: pl.BlockSpec((tk, w), lambda b, j: (b * nk + j, col))
    full = lambda col: pl.BlockSpec((seq, WIDTH), lambda b, j: (b, col))
    row = pl.BlockSpec((N_HEADS, seq), lambda b, j: (0, b))
    acc = pltpu.VMEM((N_HEADS, tk, LANES), F32)
    res = pl.pallas_call(
        body, name="fox_bwd", grid=(nbat, nk),
        in_specs=[tile(WIDTH, 1), tile(WIDTH, 2), full(0), full(0), tile(LANES, 0), row, row] + [ANY] * n,
        out_specs=[tile(WIDTH, 0), tile(WIDTH, 0), tile(LANES, 0), pl.BlockSpec((WIDTH, seq), lambda b, j: (b, 0)),
                   row] + [ANY] * n,
        out_shape=[jax.ShapeDtypeStruct((t_all, WIDTH), BF16), jax.ShapeDtypeStruct((t_all, WIDTH), BF16),
                   jax.ShapeDtypeStruct((t_all, LANES), F32), jax.ShapeDtypeStruct((nbat * WIDTH, seq), F32),
                   jax.ShapeDtypeStruct((N_HEADS, t_all), F32)] + _comm_out_shapes(grads, to_all),
        scratch_shapes=[pltpu.VMEM((N_HEADS, tk, LANES), BF16), pltpu.VMEM((N_HEADS, tk, LANES), BF16),
                        pltpu.VMEM((N_HEADS, tk, tk), F32), acc, acc, acc, pltpu.VMEM((4, LANES, tk), BF16),
                        pltpu.VMEM((N_HEADS, tk, tk), F32), pltpu.VMEM((N_HEADS, tk, tk), F32),
                        pltpu.VMEM((N_HEADS, tk, tk), BF16), pltpu.VMEM((N_HEADS, tk, tk), BF16)]
        + _comm_sems(n),
        compiler_params=_params(),
    )(za, za, za, do, f_col, lse_row, dl_row, *grads)
    return res[0], res[1], res[2], res[3], res[4], res[5:]


DIL_SUB = 4


def _dil_mask(has_prev):
    qi = lax.broadcasted_iota(jnp.int32, (BLK, 2 * BLK), 0)
    kj = lax.broadcasted_iota(jnp.int32, (BLK, 2 * BLK), 1)
    dist = qi + BLK - kj
    band = (dist >= 0) & (dist <= BLK)
    return band if has_prev is True else band & ((kj >= BLK) | has_prev)


def _dil_geometry(t_all, seq, d, max_sub=DIL_SUB):
    length = seq // d
    nbs = length // BLK
    sub = min(max_sub, nbs)
    spb = nbs // sub
    tile = lambda width, col: pl.BlockSpec((BLK * sub, width), lambda s: (s, col))
    whole = lambda width, col: pl.BlockSpec((length, width), lambda s: (s // spb, col))
    return nbs, sub, spb, t_all // (BLK * sub), tile, whole


def _blk(i):
    return pl.ds(pl.multiple_of(i * BLK, BLK), BLK)


def _dil_fwd(zb, seq, d):
    t_all = zb.shape[0]
    nbs, sub, spb, steps, tile, whole = _dil_geometry(t_all, seq, d)

    def body(q_ref, k_ref, v_ref, o_ref, lse_ref, s_sc, p_sc):
        first = (pl.program_id(0) % spb) * sub
        lo, hi = _half_masks(BLK)
        lse_ref[...] = jnp.zeros_like(lse_ref)
        for j in range(sub):
            blk = first + j
            mask = _dil_mask(blk != 0 if j == 0 else True)
            for p in range(4):
                cs = slice(LANES * p, LANES * (p + 1))
                qp = q_ref[BLK * j:BLK * (j + 1), cs]
                kcat = jnp.concatenate([k_ref[_blk(jnp.maximum(blk - 1, 0)), cs], k_ref[_blk(blk), cs]], axis=0)
                for e in (0, 1):
                    qe = jnp.where(lo if e == 0 else hi, qp, jnp.zeros_like(qp))
                    s_sc[N_HEADS * j + 2 * p + e] = jnp.where(mask, _nt(qe, kcat), NEG)
        inv = []
        for i in range(N_HEADS * sub):
            s = s_sc[i]
            m = jnp.max(s, axis=1, keepdims=True)
            pe = jnp.exp(s - m)
            l = jnp.sum(pe, axis=1, keepdims=True)
            p_sc[i] = pe.astype(BF16)
            inv.append(1.0 / l)
            j, h = divmod(i, N_HEADS)
            lse_ref[BLK * j:BLK * (j + 1), h:h + 1] = m + jnp.log(l)
        for j in range(sub):
            blk = first + j
            for p in range(4):
                cs = slice(LANES * p, LANES * (p + 1))
                vcat = jnp.concatenate([v_ref[_blk(jnp.maximum(blk - 1, 0)), cs], v_ref[_blk(blk), cs]], axis=0)
                res = [_nn(p_sc[N_HEADS * j + h], vcat) * inv[N_HEADS * j + h] for h in (2 * p, 2 * p + 1)]
                o_ref[BLK * j:BLK * (j + 1), cs] = jnp.where(lo, res[0], res[1])

    return pl.pallas_call(
        body, name=f"dil_fwd_{d}", grid=(steps,), in_specs=[tile(WIDTH, 0), whole(WIDTH, 1), whole(WIDTH, 2)],
        out_specs=[tile(WIDTH, 0), tile(LANES, 0)],
        out_shape=[jax.ShapeDtypeStruct((t_all, WIDTH), F32), jax.ShapeDtypeStruct((t_all, LANES), F32)],
        scratch_shapes=[pltpu.VMEM((N_HEADS * sub, BLK, 2 * BLK), F32),
                        pltpu.VMEM((N_HEADS * sub, BLK, 2 * BLK), BF16)],
        compiler_params=_params(),
    )(zb, zb, zb)


def _dil_bwd(zb, do, lse, dl, seq, d):
    t_all = zb.shape[0]
    length = seq // d
    nbs, sub, spb, steps, tile, whole = _dil_geometry(t_all, seq, d, 2 if length >= 4096 else DIL_SUB)

    def body(k_ref, v_ref, q_ref, do_ref, lse_ref, dl_ref, dq_ref, dk_ref, dv_ref, s_sc, dp_sc, pt_sc, ds_sc, kt_sc,
             dqt_sc):
        step = pl.program_id(0) % spb
        first = step * sub

        @pl.when(step == 0)
        def _():
            dqt_sc[...] = jnp.zeros_like(dqt_sc)
        r = lax.broadcasted_iota(jnp.int32, (BLK, 2 * BLK), 0)
        c = lax.broadcasted_iota(jnp.int32, (BLK, 2 * BLK), 1)
        same = (c < BLK) & (c >= r)
        later = (c >= BLK) & (c - BLK <= r)
        lo, hi = _half_masks(BLK)
        for j in range(sub):
            blk = first + j
            rows = slice(BLK * j, BLK * (j + 1))
            nxt = _blk(jnp.minimum(blk + 1, nbs - 1))
            mask = same | (later & (blk + 1 != nbs)) if j == sub - 1 else same | later
            lrows = jnp.concatenate([lse_ref[_blk(blk), :].T, lse_ref[nxt, :].T], axis=1)
            erows = jnp.concatenate([dl_ref[_blk(blk), :].T, dl_ref[nxt, :].T], axis=1)
            for p in range(4):
                cs = slice(LANES * p, LANES * (p + 1))
                kp = k_ref[rows, cs]
                vp = v_ref[rows, cs]
                kt_sc[4 * j + p] = kp.astype(F32).T.astype(BF16)
                qcat = jnp.concatenate([q_ref[_blk(blk), cs], q_ref[nxt, cs]], axis=0)
                dcat = jnp.concatenate([do_ref[_blk(blk), cs], do_ref[nxt, cs]], axis=0)
                for e in (0, 1):
                    h = 2 * p + e
                    sel = lo if e == 0 else hi
                    ke = jnp.where(sel, kp, jnp.zeros_like(kp))
                    ve = jnp.where(sel, vp, jnp.zeros_like(vp))
                    s_sc[N_HEADS * j + h] = jnp.where(mask, _nt(ke, qcat) - lrows[h:h + 1, :], NEG)
                    dp_sc[N_HEADS * j + h] = _nt(ve, dcat) - erows[h:h + 1, :]
        for i in range(N_HEADS * sub):
            pt = jnp.exp(s_sc[i])
            pt_sc[i] = pt.astype(BF16)
            ds_sc[i] = (pt * dp_sc[i]).astype(BF16)
        for j in range(sub):
            blk = first + j
            rows = slice(BLK * j, BLK * (j + 1))
            nxt = _blk(jnp.minimum(blk + 1, nbs - 1))
            cols = pl.ds(pl.multiple_of(blk * BLK, BLK), 2 * BLK)
            for p in range(4):
                cs = slice(LANES * p, LANES * (p + 1))
                qcat = jnp.concatenate([q_ref[_blk(blk), cs], q_ref[nxt, cs]], axis=0)
                dcat = jnp.concatenate([do_ref[_blk(blk), cs], do_ref[nxt, cs]], axis=0)
                i = N_HEADS * j + 2 * p
                dk_ref[rows, cs] = jnp.where(lo, _nn(ds_sc[i], qcat), _nn(ds_sc[i + 1], qcat)).astype(BF16)
                dv_ref[rows, cs] = jnp.where(lo, _nn(pt_sc[i], dcat), _nn(pt_sc[i + 1], dcat)).astype(BF16)
                for e in (0, 1):
                    kt = kt_sc[4 * j + p, HEAD_DIM * e:HEAD_DIM * (e + 1), :]
                    dqt_sc[HEAD_DIM * (2 * p + e):HEAD_DIM * (2 * p + e + 1), cols] += _nn(kt, ds_sc[i + e])

        @pl.when(step == spb - 1)
        def _():
            for p in range(4):
                cs = slice(LANES * p, LANES * (p + 1))
                dq_ref[:, cs] = (dqt_sc[cs, 0:length].T * Q_SCALE).astype(BF16)

    wide = pltpu.VMEM((N_HEADS * sub, BLK, 2 * BLK), F32)
    half = pltpu.VMEM((N_HEADS * sub, BLK, 2 * BLK), BF16)
    return pl.pallas_call(
        body, name=f"dil_bwd_{d}", grid=(steps,),
        in_specs=[tile(WIDTH, 1), tile(WIDTH, 2), whole(WIDTH, 0), whole(WIDTH, 0), whole(LANES, 0), whole(LANES, 0)],
        out_specs=[whole(WIDTH, 0), tile(WIDTH, 0), tile(WIDTH, 0)],
        out_shape=[jax.ShapeDtypeStruct((t_all, WIDTH), BF16)] * 3,
        scratch_shapes=[wide, wide, half, half, pltpu.VMEM((4 * sub, LANES, BLK), BF16),
                        pltpu.VMEM((WIDTH, length + BLK), F32)],
        compiler_params=_params(),
    )(zb, zb, zb, do, lse, dl)


def _mix_out(oa, o3, l3, gn_a, gn_b, w_out, x, ada3, ln_g, ln_b, perms, seq):
    t_all = x.shape[0]
    tm = TOK_TM
    nts = seq // tm

    def body(oa_ref, o1_ref, o2_ref, o3_ref, l1_ref, l2_ref, l3_ref, ga_ref, gb_ref, w_ref, x_ref, ada_ref, g_ref,
             b_ref, p4_ref, p16_ref, pt4_ref, pt16_ref, ob_ref, lse_ref, lse4_ref, lse16_ref, mg_ref, mix_ref, xh_ref,
             rs_ref, h2_ref, h2t_ref):
        e, et = _head_mats()
        la = l1_ref[...]
        lb = _permute_f32(pt4_ref[...], _load_classes(l2_ref, 4))
        lc = _permute_f32(pt16_ref[...], _load_classes(l3_ref, 16))
        mx = jnp.maximum(jnp.maximum(la, lb), lc)
        ea, eb, ec = jnp.exp(la - mx), jnp.exp(lb - mx), jnp.exp(lc - mx)
        tot = ea + eb + ec
        lse = mx + jnp.log(tot)
        lse_ref[...] = lse
        _store_classes(lse4_ref, _permute_f32(p4_ref[...], lse), 4)
        _store_classes(lse16_ref, _permute_f32(p16_ref[...], lse), 16)
        ob = (o1_ref[...] * _hexp(ea / tot, e)
              + _permute_f32(pt4_ref[...], _load_classes(o2_ref, 4)) * _hexp(eb / tot, e)
              + _permute_f32(pt16_ref[...], _load_classes(o3_ref, 16)) * _hexp(ec / tot, e))
        ob_ref[...] = ob

        def rms(o, gain):
            rr = lax.rsqrt(_hsum(o * o, et) * (1.0 / HEAD_DIM) + RMS_EPS)
            return o * _hexp(rr, e) * gain

        merged = jnp.concatenate([rms(oa_ref[...], ga_ref[...]), rms(ob, gb_ref[...])], axis=1).astype(BF16)
        mg_ref[...] = merged
        mix = _nn(merged, w_ref[...])
        mix_ref[...] = mix.astype(BF16)
        r1 = ALPHA * x_ref[...] + ada_ref[0, 2:3, :] * mix
        d = r1 - jnp.mean(r1, axis=1, keepdims=True)
        rstd = lax.rsqrt(jnp.mean(d * d, axis=1, keepdims=True) + LN_EPS)
        xh = d * rstd
        xh_ref[...] = xh
        rs_ref[...] = jnp.broadcast_to(rstd, (tm, LANES))
        x1 = xh * g_ref[...] + b_ref[...]
        h2 = x1 * (1.0 + ada_ref[0, 4:5, :]) + ada_ref[0, 3:4, :]
        h2_ref[...] = h2.astype(BF16)
        h2t_ref[0] = h2.T.astype(BF16)

    tok = lambda w: pl.BlockSpec((tm, w), lambda i: (i, 0))
    vec = lambda w: pl.BlockSpec((1, w), lambda i: (0, 0))
    whole = lambda a: pl.BlockSpec(a.shape, lambda i: (0, 0))
    classes = lambda a, d: a.reshape(t_all // seq * d, seq // d, a.shape[-1])
    return pl.pallas_call(
        body, name="mix_out", grid=(t_all // tm,),
        in_specs=[tok(WIDTH), tok(WIDTH), _class_spec(4, WIDTH, nts), _class_spec(16, WIDTH, nts), tok(LANES),
                  _class_spec(4, LANES, nts), _class_spec(16, LANES, nts), vec(WIDTH), vec(WIDTH), whole(w_out),
                  tok(D_MODEL), pl.BlockSpec((1, 6, D_MODEL), lambda i: (i // nts, 0, 0)), vec(D_MODEL), vec(D_MODEL)]
        + [whole(p) for p in perms],
        out_specs=[tok(WIDTH), tok(LANES), _class_spec(4, LANES, nts), _class_spec(16, LANES, nts), tok(D_MODEL),
                   tok(D_MODEL), tok(D_MODEL), tok(LANES), tok(D_MODEL), pl.BlockSpec((1, D_MODEL, tm), lambda i: (i // (FFN_TM // tm), 0, i % (FFN_TM // tm)))],
        out_shape=[jax.ShapeDtypeStruct((t_all, WIDTH), F32), jax.ShapeDtypeStruct((t_all, LANES), F32),
                   _class_shape(t_all, seq, 4, LANES, F32), _class_shape(t_all, seq, 16, LANES, F32),
                   jax.ShapeDtypeStruct((t_all, D_MODEL), BF16), jax.ShapeDtypeStruct((t_all, D_MODEL), BF16),
                   jax.ShapeDtypeStruct((t_all, D_MODEL), F32), jax.ShapeDtypeStruct((t_all, LANES), F32),
                   jax.ShapeDtypeStruct((t_all, D_MODEL), BF16),
                   jax.ShapeDtypeStruct((t_all // FFN_TM, D_MODEL, FFN_TM), BF16)],
        compiler_params=_params(),
    )(oa, o3[0], classes(o3[1], 4), classes(o3[2], 16), l3[0], classes(l3[1], 4), classes(l3[2], 16), gn_a, gn_b,
      w_out, x, ada3, ln_g, ln_b, *perms)


def _mix_out_bwd(dmix, w_out, oa, ob, gn_a, gn_b, perms, seq):
    t_all = dmix.shape[0]
    tm = TOK_TM
    nts = seq // tm

    def body(dm_ref, w_ref, oa_ref, ob_ref, ga_ref, gb_ref, p4_ref, p16_ref, doa_ref, dob_ref, dob4_ref, dob16_ref,
             dla_ref, dlb_ref, dlb4_ref, dlb16_ref, acc_ref):
        @pl.when(pl.program_id(0) == 0)
        def _():
            acc_ref[...] = jnp.zeros_like(acc_ref)
        e, et = _head_mats()
        dmg = _nt(dm_ref[...], w_ref[...])

        def group(o, dn, gain):
            rr = lax.rsqrt(_hsum(o * o, et) * (1.0 / HEAD_DIM) + RMS_EPS)
            re = _hexp(rr, e)
            dgain = jnp.sum(dn * o * re, axis=0, keepdims=True)
            dxn = dn * gain
            tt = _hsum(dxn * o, et) * (rr * rr * rr) * (1.0 / HEAD_DIM)
            do = re * dxn - o * _hexp(tt, e)
            return do, _hsum(do * o, et), dgain

        doa, dla, dga = group(oa_ref[...], dmg[:, :WIDTH], ga_ref[...])
        dob, dlb, dgb = group(ob_ref[...], dmg[:, WIDTH:], gb_ref[...])
        dob = dob.astype(BF16)
        doa_ref[...] = doa.astype(BF16)
        dob_ref[...] = dob
        _store_classes(dob4_ref, _nn(p4_ref[...], dob).astype(BF16), 4)
        _store_classes(dob16_ref, _nn(p16_ref[...], dob).astype(BF16), 16)
        dla_ref[...] = dla
        dlb_ref[...] = dlb
        _store_classes(dlb4_ref, _permute_f32(p4_ref[...], dlb), 4)
        _store_classes(dlb16_ref, _permute_f32(p16_ref[...], dlb), 16)
        acc_ref[0:1, :] += jnp.concatenate([dga, dgb], axis=1)

    tok = lambda w: pl.BlockSpec((tm, w), lambda i: (i, 0))
    vec = lambda w: pl.BlockSpec((1, w), lambda i: (0, 0))
    return pl.pallas_call(
        body, name="mix_out_bwd", grid=(t_all // tm,),
        in_specs=[tok(D_MODEL), pl.BlockSpec(w_out.shape, lambda i: (0, 0)), tok(WIDTH), tok(WIDTH), vec(WIDTH),
                  vec(WIDTH), pl.BlockSpec(perms[0].shape, lambda i: (0, 0)),
                  pl.BlockSpec(perms[1].shape, lambda i: (0, 0))],
        out_specs=[tok(WIDTH), tok(WIDTH), _class_spec(4, WIDTH, nts), _class_spec(16, WIDTH, nts), tok(LANES),
                   tok(LANES), _class_spec(4, LANES, nts), _class_spec(16, LANES, nts),
                   pl.BlockSpec((8, D_MODEL), lambda i: (0, 0))],
        out_shape=[jax.ShapeDtypeStruct((t_all, WIDTH), BF16), jax.ShapeDtypeStruct((t_all, WIDTH), BF16),
                   _class_shape(t_all, seq, 4, WIDTH, BF16), _class_shape(t_all, seq, 16, WIDTH, BF16),
                   jax.ShapeDtypeStruct((t_all, LANES), F32), jax.ShapeDtypeStruct((t_all, LANES), F32),
                   _class_shape(t_all, seq, 4, LANES, F32), _class_shape(t_all, seq, 16, LANES, F32),
                   jax.ShapeDtypeStruct((8, D_MODEL), F32)],
        compiler_params=_params(),
    )(dmix, w_out, oa, ob, gn_a, gn_b, perms[0], perms[1])


def _inproj_bwd(dqt, dka, dva, dil1, dil4, dil16, dfa16, pos, wqkv, wf16, freq, perms, dr1, x, ada3, seq):
    t_all = x.shape[0]
    tm = TOK_TM
    nts = seq // tm

    def body(dqt_ref, dka_ref, dva_ref, q1_ref, k1_ref, v1_ref, q4_ref, k4_ref, v4_ref, q16_ref, k16_ref, v16_ref,
             dfa_ref, pos_ref, w_ref, wf_ref, fr_ref, pt4_ref, pt16_ref, dr1_ref, x_ref, ada_ref, gx_ref, dz_ref,
             acc_ref):
        i = pl.program_id(0)

        @pl.when(i == 0)
        def _():
            acc_ref[...] = jnp.zeros_like(acc_ref)
        tabs = _rope_tabs(pos_ref, fr_ref, -1.0)
        dz_ref[:, :WIDTH] = dqt_ref[...].T.astype(BF16)
        dz_ref[:, WIDTH:2 * WIDTH] = dka_ref[...]
        dz_ref[:, 2 * WIDTH:3 * WIDTH] = dva_ref[...]
        for t, (n1, n4, n16) in enumerate(((q1_ref, q4_ref, q16_ref), (k1_ref, k4_ref, k16_ref),
                                           (v1_ref, v4_ref, v16_ref))):
            tot = (n1[...].astype(F32) + _nn(pt4_ref[...], _load_classes(n4, 4))
                   + _nn(pt16_ref[...], _load_classes(n16, 16)))
            if t < 2:
                tot = _rope(tot, tabs)
            dz_ref[:, (3 + t) * WIDTH:(4 + t) * WIDTH] = tot.astype(BF16)
        dh1 = _tn(dfa_ref[...], wf_ref[...])
        for n in range(6):
            cs = slice(n * WIDTH, (n + 1) * WIDTH)
            dh1 = dh1 + _nt(dz_ref[:, cs], w_ref[:, cs])
        xv = x_ref[...]
        gx_ref[...] = ALPHA * dr1_ref[...] + dh1 * (1.0 + ada_ref[0, 1:2, :])
        b = i // nts
        acc_ref[pl.ds(b, 1), :] += jnp.sum(dh1 * xv, axis=0, keepdims=True)
        acc_ref[pl.ds(8 + b, 1), :] += jnp.sum(dh1, axis=0, keepdims=True)

    tok = lambda w: pl.BlockSpec((tm, w), lambda i: (i, 0))
    whole = lambda a: pl.BlockSpec(a.shape, lambda i: (0, 0))
    classes = lambda a, d: a.reshape(t_all // seq * d, seq // d, a.shape[-1])
    return pl.pallas_call(
        body, name="inproj_bwd", grid=(t_all // tm,),
        in_specs=[pl.BlockSpec((WIDTH, tm), lambda i: (i // nts, i % nts)), tok(WIDTH), tok(WIDTH)]
        + [tok(WIDTH)] * 3 + [_class_spec(4, WIDTH, nts)] * 3 + [_class_spec(16, WIDTH, nts)] * 3
        + [pl.BlockSpec((16, tm), lambda i: (0, i)), tok(1), whole(wqkv), whole(wf16),
           pl.BlockSpec((1, LANES), lambda i: (0, 0)), whole(perms[2]), whole(perms[3]), tok(D_MODEL), tok(D_MODEL),
           pl.BlockSpec((1, 6, D_MODEL), lambda i: (i // nts, 0, 0))],
        out_specs=[tok(D_MODEL), tok(6 * WIDTH), pl.BlockSpec((16, D_MODEL), lambda i: (0, 0))],
        out_shape=[jax.ShapeDtypeStruct((t_all, D_MODEL), F32), jax.ShapeDtypeStruct((t_all, 6 * WIDTH), BF16),
                   jax.ShapeDtypeStruct((16, D_MODEL), F32)],
        compiler_params=_params(),
    )(dqt, dka, dva, *dil1, *[classes(a, 4) for a in dil4], *[classes(a, 16) for a in dil16], dfa16, pos, wqkv, wf16,
      freq, perms[2], perms[3], dr1, x, ada3)


FFN_TM = 1024
FFN_TN = 256
HALO = 8


FFN_CHUNK = 256


def _conv_params(cw_ref, cb_ref, n, tn):
    a = pl.ds(pl.multiple_of(n * tn, tn), tn)
    g = pl.ds(pl.multiple_of(D_FF + n * tn, tn), tn)
    return cw_ref[:, a], cw_ref[:, g], cb_ref[:, a], cb_ref[:, g]


def _conv(cat_ref, w_ref, b_ref, start, rows, halo=HALO):
    return (b_ref[...] + w_ref[0:1, :] * cat_ref[pl.ds(start + halo - 2, rows), :]
            + w_ref[1:2, :] * cat_ref[pl.ds(start + halo - 1, rows), :]
            + w_ref[2:3, :] * cat_ref[pl.ds(start + halo, rows), :])


def _ffn_up_gate(h2, w_up, conv_w, conv_b, seq):
    t_all = h2.shape[0]
    tm, tn = FFN_TM, FFN_TN
    nc = D_FF // tn
    nts = seq // tm
    pre = 16

    def body(h_ref, hp_ref, wua_ref, wug_ref, cw_ref, cb_ref, ua_ref, ug_ref, o_ref, ca_ref, cg_ref):
        first = (pl.program_id(1) % nts) == 0
        wa_ref, wg_ref, ba_ref, bg_ref = _conv_params(cw_ref, cb_ref, pl.program_id(0), tn)
        hcat = jnp.concatenate([hp_ref[...], h_ref[...]], axis=0)
        zero = jnp.zeros((pre, tn), F32)
        for w_ref, cat, u_ref in ((wua_ref, ca_ref, ua_ref), (wug_ref, cg_ref, ug_ref)):
            ub = _nn(hcat, w_ref[...]).astype(BF16)
            ue = ub.astype(F32)
            cat[0:pre, :] = jnp.where(first, zero, ue[0:pre])
            cat[pre:, :] = ue[pre:]
            u_ref[...] = ub[pre:]
        for c0 in range(0, tm, FFN_CHUNK):
            ya = _conv(ca_ref, wa_ref, ba_ref, c0, FFN_CHUNK, pre)
            yg = _conv(cg_ref, wg_ref, bg_ref, c0, FFN_CHUNK, pre)
            o_ref[c0:c0 + FFN_CHUNK, :] = (yg * jax.nn.sigmoid(yg) * ya).astype(BF16)

    wcol = lambda off: pl.BlockSpec((D_MODEL, tn), lambda n, t: (0, n + off))
    tile = pl.BlockSpec((tm, tn), lambda n, t: (t, n))
    return pl.pallas_call(
        body, name="ffn_up_gate", grid=(nc, t_all // tm),
        in_specs=[pl.BlockSpec((tm, D_MODEL), lambda n, t: (t, 0)),
                  pl.BlockSpec((pre, D_MODEL), lambda n, t: (jnp.maximum(t * (tm // pre) - 1, 0), 0)),
                  wcol(0), wcol(nc), pl.BlockSpec(conv_w.shape, lambda n, t: (0, 0)),
                  pl.BlockSpec(conv_b.shape, lambda n, t: (0, 0))],
        out_specs=[tile, tile, tile],
        out_shape=[jax.ShapeDtypeStruct((t_all, D_FF), BF16)] * 3,
        scratch_shapes=[pltpu.VMEM((tm + pre, tn), F32)] * 2, compiler_params=_params(),
    )(h2, h2, w_up, w_up, conv_w, conv_b)


def _ffn_gate_bwd(u_a, u_g, dfi, conv_w, conv_b, h2t, seq):
    t_all = u_a.shape[0]
    tm, tn = FFN_TM, FFN_TN
    nc = D_FF // tn
    nts = seq // tm

    def body(ua_ref, uap_ref, uan_ref, ug_ref, ugp_ref, ugn_ref, df_ref, dfn_ref, cw_ref, cb_ref, h_ref,
             dua_ref, dug_ref, acca_ref, accg_ref, dwa_ref, dwg_ref, ca_ref, cg_ref, ya_ref, yg_ref, dwa_sc, dwg_sc,
             out_sems):
        t = pl.program_id(0)
        n = pl.program_id(1)
        cols = pl.ds(pl.multiple_of(n * tn, tn), tn)
        first = (t % nts) == 0
        last = (t % nts) == nts - 1

        @pl.when((t == 0) & (n == 0))
        def _():
            acca_ref[...] = jnp.zeros_like(acca_ref)
            accg_ref[...] = jnp.zeros_like(accg_ref)
            dwa_sc[...] = jnp.zeros_like(dwa_sc)
            dwg_sc[...] = jnp.zeros_like(dwg_sc)
        wa_ref, wg_ref, ba_ref, bg_ref = _conv_params(cw_ref, cb_ref, n, tn)
        zero = jnp.zeros((HALO, tn), F32)
        for cat, cur, prv, nxt in ((ca_ref, ua_ref, uap_ref, uan_ref), (cg_ref, ug_ref, ugp_ref, ugn_ref)):
            cat[0:HALO, :] = jnp.where(first, zero, prv[...].astype(F32)[HALO:])
            cat[HALO:HALO + tm, :] = cur[...].astype(F32)
            cat[HALO + tm:, :] = nxt[...].astype(F32)[:HALO]
        ch = FFN_CHUNK
        sums = [[jnp.zeros((1, tn), F32) for _ in range(4)] for _ in range(2)]
        for ci, c0 in enumerate(range(0, tm, ch)):
            ya = _conv(ca_ref, wa_ref, ba_ref, c0, ch + HALO)
            yg = _conv(cg_ref, wg_ref, bg_ref, c0, ch + HALO)
            if c0 + ch < tm:
                beyond = df_ref[c0 + ch:c0 + ch + 16, :].astype(F32)[:HALO]
            else:
                beyond = jnp.where(last, 0.0, dfn_ref[...].astype(F32)[:HALO])
            dfe = jnp.concatenate([df_ref[c0:c0 + ch, :].astype(F32), beyond], axis=0)
            sg = jax.nn.sigmoid(yg)
            ya_ref[ci] = dfe * (yg * sg)
            yg_ref[ci] = dfe * ya * (sg * (1.0 + yg * (1.0 - sg)))
            for half, (dy, cat, w_ref, du_ref) in enumerate(((ya_ref, ca_ref, wa_ref, dua_ref),
                                                             (yg_ref, cg_ref, wg_ref, dug_ref))):
                d0 = dy[ci, 0:ch, :]
                du = (w_ref[2:3, :] * d0 + w_ref[1:2, :] * dy[ci, pl.ds(1, ch), :]
                      + w_ref[0:1, :] * dy[ci, pl.ds(2, ch), :])
                du_ref[c0:c0 + ch, :] = du.astype(BF16)
                for k in range(3):
                    sums[half][k] += jnp.sum(d0 * cat[pl.ds(c0 + HALO - 2 + k, ch), :], axis=0, keepdims=True)
                sums[half][3] += jnp.sum(d0, axis=0, keepdims=True)
        for half, acc in enumerate((acca_ref, accg_ref)):
            for k in range(4):
                acc[k:k + 1, cols] += sums[half][k]
        ht = h_ref[0]
        dwa_sc[:, cols] += _nn(ht, dua_ref[...])
        dwg_sc[:, cols] += _nn(ht, dug_ref[...])

        @pl.when((t == t_all // tm - 1) & (n == nc - 1))
        def _():
            copies = [pltpu.make_async_copy(dwa_sc, dwa_ref, out_sems.at[0]),
                      pltpu.make_async_copy(dwg_sc, dwg_ref, out_sems.at[1])]
            for cp in copies:
                cp.start()
            for cp in copies:
                cp.wait()

    nrow = t_all // 16
    cur = pl.BlockSpec((tm, tn), lambda t, n: (t, n))
    prev = pl.BlockSpec((16, tn), lambda t, n: (jnp.maximum(t * (tm // 16) - 1, 0), n))
    nxt = pl.BlockSpec((16, tn), lambda t, n: (jnp.minimum((t + 1) * (tm // 16), nrow - 1), n))
    acc = pl.BlockSpec((8, D_FF), lambda t, n: (0, 0))
    return pl.pallas_call(
        body, name="ffn_gate_bwd", grid=(t_all // tm, nc),
        in_specs=[cur, prev, nxt, cur, prev, nxt, cur, nxt, pl.BlockSpec(conv_w.shape, lambda t, n: (0, 0)),
                  pl.BlockSpec(conv_b.shape, lambda t, n: (0, 0)),
                  pl.BlockSpec((1, D_MODEL, tm), lambda t, n: (t, 0, 0))],
        out_specs=[cur, cur, acc, acc, ANY, ANY],
        out_shape=[jax.ShapeDtypeStruct((t_all, D_FF), BF16), jax.ShapeDtypeStruct((t_all, D_FF), BF16),
                   jax.ShapeDtypeStruct((8, D_FF), F32), jax.ShapeDtypeStruct((8, D_FF), F32),
                   jax.ShapeDtypeStruct((D_MODEL, D_FF), F32), jax.ShapeDtypeStruct((D_MODEL, D_FF), F32)],
        scratch_shapes=[pltpu.VMEM((tm + 2 * HALO, tn), F32)] * 2
        + [pltpu.VMEM((tm // FFN_CHUNK, FFN_CHUNK + HALO, tn), F32)] * 2
        + [pltpu.VMEM((D_MODEL, D_FF), F32)] * 2 + [pltpu.SemaphoreType.DMA((2,))],
        compiler_params=_params(),
    )(u_a, u_a, u_a, u_g, u_g, u_g, dfi, dfi, conv_w, conv_b, h2t)


def _ffn_down(ffn_in, w_down, xh1, ln1_g, ln1_b, ada3, ln2_g, ln2_b, target, seq):
    t_all = xh1.shape[0]
    tm = 256
    nts = seq // tm

    def body(f_ref, w_ref, xh_ref, g1_ref, b1_ref, ada_ref, g2_ref, b2_ref, tg_ref, dr2_ref, acc_ref):
        i = pl.program_id(0)

        @pl.when(i == 0)
        def _():
            acc_ref[...] = jnp.zeros_like(acc_ref)
        ffn = _nn(f_ref[...], w_ref[...])
        x1 = xh_ref[...] * g1_ref[...] + b1_ref[...]
        r2 = ALPHA * x1 + ada_ref[0, 5:6, :] * ffn
        d = r2 - jnp.mean(r2, axis=1, keepdims=True)
        rstd = lax.rsqrt(jnp.mean(d * d, axis=1, keepdims=True) + LN_EPS)
        xh2 = d * rstd
        diff = xh2 * g2_ref[...] + b2_ref[...] - tg_ref[...]
        dy = diff * (1.0 / D_MODEL)
        dr2 = _layer_norm_bwd(dy * g2_ref[...], xh2, rstd)
        dr2_ref[...] = dr2
        acc_ref[0:1, :] += jnp.sum(dy * xh2, axis=0, keepdims=True)
        acc_ref[1:2, :] += jnp.sum(dy, axis=0, keepdims=True)
        acc_ref[2:3, :] += jnp.sum(diff * diff, axis=0, keepdims=True) * (0.5 / D_MODEL)
        acc_ref[pl.ds(8 + i // nts, 1), :] += jnp.sum(dr2 * ffn, axis=0, keepdims=True)

    tok = lambda w: pl.BlockSpec((tm, w), lambda i: (i, 0))
    vec = pl.BlockSpec((1, D_MODEL), lambda i: (0, 0))
    return pl.pallas_call(
        body, name="ffn_down", grid=(t_all // tm,),
        in_specs=[tok(D_FF), pl.BlockSpec(w_down.shape, lambda i: (0, 0)), tok(D_MODEL), vec, vec,
                  pl.BlockSpec((1, 6, D_MODEL), lambda i: (i // nts, 0, 0)), vec, vec, tok(D_MODEL)],
        out_specs=[tok(D_MODEL), pl.BlockSpec((16, D_MODEL), lambda i: (0, 0))],
        out_shape=[jax.ShapeDtypeStruct((t_all, D_MODEL), F32), jax.ShapeDtypeStruct((16, D_MODEL), F32)],
        compiler_params=_params(),
    )(ffn_in, w_down, xh1, ln1_g, ln1_b, ada3, ln2_g, ln2_b, target)


def _ffn_down_bwd(dr2, ada3, w_down, seq):
    t_all = dr2.shape[0]
    tm = 256
    nts = seq // tm

    def body(d_ref, ada_ref, w_ref, dffn_ref, dfi_ref):
        dffn = (d_ref[...] * ada_ref[0, 5:6, :]).astype(BF16)
        dffn_ref[...] = dffn
        dfi_ref[...] = _nt(dffn, w_ref[...]).astype(BF16)

    tok = lambda w: pl.BlockSpec((tm, w), lambda i: (i, 0))
    return pl.pallas_call(
        body, name="ffn_down_bwd", grid=(t_all // tm,),
        in_specs=[tok(D_MODEL), pl.BlockSpec((1, 6, D_MODEL), lambda i: (i // nts, 0, 0)),
                  pl.BlockSpec(w_down.shape, lambda i: (0, 0))],
        out_specs=[tok(D_MODEL), tok(D_FF)],
        out_shape=[jax.ShapeDtypeStruct((t_all, D_MODEL), BF16), jax.ShapeDtypeStruct((t_all, D_FF), BF16)],
        compiler_params=_params(),
    )(dr2, ada3, w_down)


def _ffn_up_bwd(du_a, du_g, w_up, dr2, xh1, rs1, mix, ada3, ln1_g, ln1_b, seq):
    t_all = dr2.shape[0]
    tm = 256
    nts = seq // tm

    def body(da_ref, dg_ref, w_ref, dr2_ref, xh_ref, rs_ref, mix_ref, ada_ref, g_ref, b_ref, dr1_ref, dmix_ref,
             acc_ref):
        i = pl.program_id(0)

        @pl.when(i == 0)
        def _():
            acc_ref[...] = jnp.zeros_like(acc_ref)
        dh2 = _nt(da_ref[...], w_ref[:, :D_FF]) + _nt(dg_ref[...], w_ref[:, D_FF:])
        xh = xh_ref[...]
        x1 = xh * g_ref[...] + b_ref[...]
        dx1 = ALPHA * dr2_ref[...] + dh2 * (1.0 + ada_ref[0, 4:5, :])
        dr1 = _layer_norm_bwd(dx1 * g_ref[...], xh, rs_ref[:, 0:1])
        dr1_ref[...] = dr1
        dmix_ref[...] = (dr1 * ada_ref[0, 2:3, :]).astype(BF16)
        b = i // nts
        acc_ref[0:1, :] += jnp.sum(dx1 * xh, axis=0, keepdims=True)
        acc_ref[1:2, :] += jnp.sum(dx1, axis=0, keepdims=True)
        acc_ref[pl.ds(8 + b, 1), :] += jnp.sum(dh2 * x1, axis=0, keepdims=True)
        acc_ref[pl.ds(16 + b, 1), :] += jnp.sum(dh2, axis=0, keepdims=True)
        acc_ref[pl.ds(24 + b, 1), :] += jnp.sum(dr1 * mix_ref[...].astype(F32), axis=0, keepdims=True)

    tok = lambda w: pl.BlockSpec((tm, w), lambda i: (i, 0))
    vec = pl.BlockSpec((1, D_MODEL), lambda i: (0, 0))
    return pl.pallas_call(
        body, name="ffn_up_bwd", grid=(t_all // tm,),
        in_specs=[tok(D_FF), tok(D_FF), pl.BlockSpec(w_up.shape, lambda i: (0, 0)), tok(D_MODEL), tok(D_MODEL),
                  tok(LANES), tok(D_MODEL), pl.BlockSpec((1, 6, D_MODEL), lambda i: (i // nts, 0, 0)), vec, vec],
        out_specs=[tok(D_MODEL), tok(D_MODEL), pl.BlockSpec((32, D_MODEL), lambda i: (0, 0))],
        out_shape=[jax.ShapeDtypeStruct((t_all, D_MODEL), F32), jax.ShapeDtypeStruct((t_all, D_MODEL), BF16),
                   jax.ShapeDtypeStruct((32, D_MODEL), F32)],
        compiler_params=_params(),
    )(du_a, du_g, w_up, dr2, xh1, rs1, mix, ada3, ln1_g, ln1_b)


def _rows(a):
    return a[:, :N_HEADS].T


def _rope_freq():
    f = np.float32(ROPE_THETA) ** (-np.arange(0, ROPE_DIMS, 2, dtype=np.float32) / np.float32(ROPE_DIMS))
    return jnp.asarray(np.tile(f.astype(np.float32), LANES // (ROPE_DIMS // 2))[None, :])


def _local_step(x, positions, target, ada3, w_in, b_fgate, gn_a, gn_b, ln1_g, ln1_b, conv_b, ln2_g, ln2_b,
                late_shards):
    nbat, seq, _ = x.shape
    t_all = nbat * seq
    xf = x.reshape(t_all, D_MODEL)
    tg = target.reshape(t_all, D_MODEL)
    pos = positions.reshape(t_all, 1)
    freq = _rope_freq()

    wqkv = jnp.concatenate([w_in[:, :3 * WIDTH], w_in[:, 3 * WIDTH + N_HEADS:]], axis=1)
    wf16 = jnp.zeros((16, D_MODEL), BF16).at[:N_HEADS].set(w_in[:, 3 * WIDTH:3 * WIDTH + N_HEADS].T)
    bf = b_fgate.reshape(N_HEADS, 1)

    perms = [_perm_matrix(TOK_TM, d, tr) for tr in (False, True) for d in DILATIONS[1:]]
    h1, za, zb1, zb4, zb16, vt, fa_t = _inproj(xf, ada3, pos, wqkv, wf16, freq, perms, seq)
    zbs = [zb1, zb4.reshape(t_all, 3 * WIDTH), zb16.reshape(t_all, 3 * WIDTH)]
    f_row = _fgate_fwd(fa_t, bf, seq)
    f_col = jnp.zeros((t_all, LANES), F32).at[:, :N_HEADS].set(f_row.T)
    oa, lse_row_a, gathered = _fox_fwd(za, vt, f_col, seq, [late_shards[n] for n in LATE])
    w_out, w_up, conv_w, w_down = (_full_from_gathered(n, g) for n, g in zip(LATE, gathered))
    o3, l3 = zip(*[_dil_fwd(zb, seq, d) for zb, d in zip(zbs, DILATIONS)])
    ob, lse_b, lse_b4, lse_b16, merged, mix, xh1, rs1, h2, h2t = _mix_out(oa, o3, l3, gn_a, gn_b, w_out, xf, ada3, ln1_g,
                                                                      ln1_b, perms, seq)
    u_a, u_g, ffn_in = _ffn_up_gate(h2, w_up, conv_w, conv_b, seq)
    dr2, acc2 = _ffn_down(ffn_in, w_down, xh1, ln1_g, ln1_b, ada3, ln2_g, ln2_b, tg, seq)

    dffn, dfi = _ffn_down_bwd(dr2, ada3, w_down, seq)
    d_w_down = _matmul_tn(dffn, ffn_in, 512, 512, "dw_down").T
    du_a, du_g, acc_ca, acc_cg, dw_up_a, dw_up_g = _ffn_gate_bwd(u_a, u_g, dfi, conv_w, conv_b, h2t, seq)
    dr1, dmix, acc1 = _ffn_up_bwd(du_a, du_g, w_up, dr2, xh1, rs1, mix, ada3, ln1_g, ln1_b, seq)
    d_w_up = jnp.concatenate([dw_up_a, dw_up_g], axis=1)

    doa, dob, dob4, dob16, dl_a, dl_b, dl_b4, dl_b16, acc_gn = _mix_out_bwd(dmix, w_out, oa, ob, gn_a, gn_b, perms, seq)
    d_w_out = _matmul_tn(merged, dmix, 512, 512, "dw_out")
    late_grads = dict(w_out=d_w_out, w_up=d_w_up, conv_w=jnp.concatenate([acc_ca[0:3], acc_cg[0:3]], axis=1),
                      w_down=d_w_down)
    dka, dva, df_k, dqt, df_q, late_parts = _fox_bwd(za, doa, f_col, lse_row_a, _rows(dl_a), seq,
                                                     [_payload(n, _dest_major(n, late_grads[n])) for n in LATE])
    dfa_t, dbf = _fgate_bwd(_rows(df_k) + df_q, fa_t, bf, seq)
    flat = lambda a: a.reshape(t_all, a.shape[-1])
    dil = []
    for zb, d, do, lse, dl in zip(zbs, DILATIONS, (dob, flat(dob4), flat(dob16)),
                                  (lse_b, flat(lse_b4), flat(lse_b16)), (dl_b, flat(dl_b4), flat(dl_b16))):
        dil.append(_dil_bwd(zb, do, lse, dl, seq, d))
    dfa16 = jnp.zeros((16, t_all), BF16).at[:N_HEADS].set(dfa_t.astype(BF16))
    grad_x, dz, acc0 = _inproj_bwd(dqt, dka, dva, dil[0], dil[1], dil[2], dfa16, pos, wqkv, wf16, freq, perms, dr1, xf,
                                   ada3, seq)
    d_wqkv = _matmul_tn(h1, dz, 512, 512, "dw_in")
    d_wf = _matmul_rows(dfa16, h1, 512, "dw_fgate")[:N_HEADS].T
    d_w_in = jnp.concatenate([d_wqkv[:, :3 * WIDTH], d_wf, d_wqkv[:, 3 * WIDTH:]], axis=1)

    dada = jnp.concatenate([acc0[8:8 + nbat], acc0[:nbat], acc1[24:24 + nbat], acc1[16:16 + nbat], acc1[8:8 + nbat],
                            acc2[8:8 + nbat]], axis=1)

    grads = dict(
        dada=dada, b_ada=jnp.sum(dada, axis=0, keepdims=True), w_in=d_w_in, b_fgate=dbf[:, 0][None, :],
        gn_a=acc_gn[0:1, :WIDTH], gn_b=acc_gn[0:1, WIDTH:], ln1_g=acc1[0:1], ln1_b=acc1[1:2],
        conv_b=jnp.concatenate([acc_ca[3:4], acc_cg[3:4]], axis=1), ln2_g=acc2[0:1], ln2_b=acc2[1:2])
    return acc2[2:3], grad_x.reshape(x.shape), grads, dict(zip(LATE, late_parts))


LATE = ("w_out", "w_up", "conv_w", "w_down")
BIG = ("w_ada", "w_in") + LATE
COLUMN_SHARDED = ("w_ada", "w_in", "w_up", "conv_w")


def _payload(name, a):
    return a if name == "conv_w" else a.astype(BF16)
SMALL = ("b_ada", "b_fgate", "gn_a", "gn_b", "ln1_g", "ln1_b", "conv_b", "ln2_g", "ln2_b")
ADAM_ROWS = dict(w_ada=256, w_in=256, w_out=128, w_up=256, conv_w=3, w_down=176)
SMALL_ROWS = 24


def _full_from_gathered(name, g):
    if name in COLUMN_SHARDED:
        return g.transpose(1, 0, 2).reshape(g.shape[1], N_DEV * g.shape[2])
    return g.reshape(N_DEV * g.shape[1], g.shape[2])


def _dest_major(name, full):
    if name in COLUMN_SHARDED:
        r, cfull = full.shape
        return full.reshape(r, N_DEV, cfull // N_DEV).transpose(1, 0, 2)
    return full.reshape(N_DEV, full.shape[0] // N_DEV, full.shape[1])


def _pack_small(vals, extra=None):
    parts = [vals[n].reshape(-1) for n in SMALL]
    if extra is not None:
        parts.append(extra.reshape(-1))
    flat = jnp.concatenate(parts)
    return jnp.pad(flat, (0, SMALL_ROWS * D_MODEL - flat.shape[0])).reshape(SMALL_ROWS, D_MODEL)


def _unpack_small(packed, like):
    flat = packed.reshape(-1)
    out, off = {}, 0
    for n in SMALL:
        size = like[n].size
        out[n] = flat[off:off + size].reshape(like[n].shape)
        off += size
    return out, flat[off:off + D_MODEL]


def kernel(x, c, positions, w_ada, b_ada, w_in, b_fgate, gn_a, gn_b, w_out, ln1_g, ln1_b, w_up, conv_w, conv_b, w_down, ln2_g, ln2_b, loss_target, m_w_ada, m_b_ada, m_w_in, m_b_fgate, m_gn_a, m_gn_b, m_w_out, m_ln1_g, m_ln1_b, m_w_up, m_conv_w, m_conv_b, m_w_down, m_ln2_g, m_ln2_b, v_w_ada, v_b_ada, v_w_in, v_b_fgate, v_gn_a, v_gn_b, v_w_out, v_ln1_g, v_ln1_b, v_w_up, v_conv_w, v_conv_b, v_w_down, v_ln2_g, v_ln2_b):
    w = dict(w_ada=w_ada[0], b_ada=b_ada, w_in=w_in[0], b_fgate=b_fgate, gn_a=gn_a, gn_b=gn_b, w_out=w_out[0],
             ln1_g=ln1_g, ln1_b=ln1_b, w_up=w_up[0], conv_w=conv_w[0], conv_b=conv_b, w_down=w_down[0], ln2_g=ln2_g,
             ln2_b=ln2_b)
    m = dict(w_ada=m_w_ada[0], b_ada=m_b_ada, w_in=m_w_in[0], b_fgate=m_b_fgate, gn_a=m_gn_a, gn_b=m_gn_b,
             w_out=m_w_out[0], ln1_g=m_ln1_g, ln1_b=m_ln1_b, w_up=m_w_up[0], conv_w=m_conv_w[0], conv_b=m_conv_b,
             w_down=m_w_down[0], ln2_g=m_ln2_g, ln2_b=m_ln2_b)
    v = dict(w_ada=v_w_ada[0], b_ada=v_b_ada, w_in=v_w_in[0], b_fgate=v_b_fgate, gn_a=v_gn_a, gn_b=v_gn_b,
             w_out=v_w_out[0], ln1_g=v_ln1_g, ln1_b=v_ln1_b, w_up=v_w_up[0], conv_w=v_conv_w[0], conv_b=v_conv_b,
             w_down=v_w_down[0], ln2_g=v_ln2_g, ln2_b=v_ln2_b)

    nbat = x.shape[0]
    me = 4 * lax.axis_index("x") + 2 * lax.axis_index("y") + lax.axis_index("c")
    ada_cols = w["w_ada"].shape[1]

    c_all, w_in_all = _gather_two_level([c, _payload("w_in", w["w_in"])], "weight_gather")
    c_all = c_all.reshape(N_DEV * nbat, D_MODEL)
    ada_mine = _ada_fwd(c_all, w["w_ada"], lax.dynamic_slice(b_ada, (0, me * ada_cols), (1, ada_cols)))
    (ada_parts,) = _exchange([ada_mine.reshape(N_DEV, nbat, ada_cols)], [False], "ada_exchange")
    ada3 = ada_parts.transpose(1, 0, 2).reshape(nbat, 6, D_MODEL)

    loss_lanes, grad_x, g_local, parts = _local_step(
        x, positions, loss_target, ada3, _full_from_gathered("w_in", w_in_all), b_fgate, gn_a, gn_b, ln1_g, ln1_b,
        conv_b, ln2_g, ln2_b, {n: _payload(n, w[n]) for n in LATE})

    parts["w_in"], dada_all, small_all = _exchange(
        [_payload("w_in", _dest_major("w_in", g_local["w_in"])), g_local["dada"], _pack_small(g_local, loss_lanes)],
        [False, True, True], "grad_exchange")
    dada_cols = lax.dynamic_slice(dada_all.reshape(N_DEV * nbat, 6 * D_MODEL), (0, me * ada_cols),
                                  (N_DEV * nbat, ada_cols))
    parts["w_ada"] = _ada_bwd(c_all, dada_cols)[None]

    grad, delta, new_m, new_v = {}, {}, {}, {}
    for n in BIG:
        grad[n], delta[n], new_m[n], new_v[n] = (
            a[None] for a in _adamw(parts[n], w[n], m[n], v[n], ADAM_ROWS[n], "adamw_" + n))
    packed = _adamw(small_all, _pack_small(w), _pack_small(m), _pack_small(v), SMALL_ROWS, "adamw_small")
    for dst, pk in zip((grad, delta, new_m, new_v), packed):
        vals, lanes = _unpack_small(pk, w)
        dst.update(vals)
        if dst is grad:
            loss = jnp.sum(lanes)

    order = ("w_ada", "b_ada", "w_in", "b_fgate", "gn_a", "gn_b", "w_out", "ln1_g", "ln1_b", "w_up", "conv_w", "conv_b",
             "w_down", "ln2_g", "ln2_b")
    return (loss, grad_x, *[grad[n] for n in order], *[delta[n] for n in order], *[new_m[n] for n in order],
            *[new_v[n] for n in order])
```

```python
import functools

import numpy as np
import jax
import jax.numpy as jnp
from jax import lax
from jax.experimental import pallas as pl
from jax.experimental.pallas import tpu as pltpu

F32, BF16 = jnp.float32, jnp.bfloat16
MESH = pl.DeviceIdType.MESH
ANY = pl.BlockSpec(memory_space=pl.ANY)

D_MODEL = 1024
N_HEADS = 8
HEAD_DIM = 64
WIDTH = 512
D_FF = 2816
N_DEV = 8
ROPE_DIMS = 16
ROPE_THETA = 500000.0
ALPHA = 2.0 ** 0.25
LN_EPS = 1e-5
RMS_EPS = 1e-6
NEG = -1e30
Q_SCALE = 0.125
LOG2E = 1.4426950408889634
BLK = 128
LANES = 128
VMEM_LIMIT_BYTES = 56 * 1024 * 1024

ADAM_LR, ADAM_B1, ADAM_B2, ADAM_EPS, ADAM_WD, ADAM_STEP = 0.001, 0.9, 0.999, 1e-08, 0.01, 10


def _params(vmem=VMEM_LIMIT_BYTES):
    return pltpu.CompilerParams(vmem_limit_bytes=vmem)


def _nn(a, b):
    return jnp.dot(a, b, preferred_element_type=F32)


def _nt(a, b):
    return lax.dot_general(a, b, (((1,), (1,)), ((), ())), preferred_element_type=F32)


def _tn(a, b):
    return lax.dot_general(a, b, (((0,), (0,)), ((), ())), preferred_element_type=F32)


def _head_mats():
    r = lax.broadcasted_iota(jnp.int32, (LANES, WIDTH), 0)
    c = lax.broadcasted_iota(jnp.int32, (LANES, WIDTH), 1)
    e = ((c >> 6) == r).astype(BF16)
    r2 = lax.broadcasted_iota(jnp.int32, (WIDTH, LANES), 0)
    c2 = lax.broadcasted_iota(jnp.int32, (WIDTH, LANES), 1)
    et = ((r2 >> 6) == c2).astype(BF16)
    return e, et


def _split3(x):
    hi = x.astype(BF16)
    r = x - hi.astype(F32)
    mid = r.astype(BF16)
    return hi, mid, (r - mid.astype(F32)).astype(BF16)


def _hexp(w, e):
    return sum(_nn(part, e) for part in _split3(w)[:2])


def _hsum(x, et):
    return sum(_nn(part, et) for part in _split3(x)[:2])


def _perm_matrix(rows, d, transpose):
    i = np.arange(rows)
    j = (i % (rows // d)) * d + i // (rows // d)
    p = np.zeros((rows, rows), np.float32)
    p[i, j] = 1.0
    return jnp.asarray(p.T if transpose else p, BF16)


def _permute_f32(p, x):
    return sum(_nn(p, part) for part in _split3(x))


def _store_classes(ref, y, d):
    n = y.shape[0] // d
    for r in range(d):
        ref[r] = y[r * n:(r + 1) * n, :]


def _load_classes(ref, d):
    return jnp.concatenate([ref[r] for r in range(d)], axis=0)


def _rope_tabs(pos_ref, fr_ref, sign):
    ang = pos_ref[...].astype(F32) * fr_ref[...]
    lane = lax.broadcasted_iota(jnp.int32, ang.shape, 1) & (HEAD_DIM - 1)
    m1 = lane < ROPE_DIMS // 2
    m2 = (lane >= ROPE_DIMS // 2) & (lane < ROPE_DIMS)
    cos = jnp.cos(ang)
    sin = jnp.sin(ang) * sign
    return (jnp.where(m1 | m2, cos, 1.0), jnp.where(m1, -sin, 0.0), jnp.where(m2, sin, 0.0))


def _rope(z, tabs):
    c, s1, s2 = tabs
    parts = []
    for p in range(z.shape[1] // LANES):
        zp = z[:, LANES * p:LANES * (p + 1)]
        parts.append(zp * c + pltpu.roll(zp, LANES - 8, 1) * s1 + pltpu.roll(zp, 8, 1) * s2)
    return jnp.concatenate(parts, axis=1)


def _half_masks(rows):
    lane = lax.broadcasted_iota(jnp.int32, (rows, LANES), 1)
    lo = lane < HEAD_DIM
    return lo, jnp.logical_not(lo)


def _layer_norm_bwd(dxh, xh, rstd):
    m1 = jnp.mean(dxh, axis=1, keepdims=True)
    m2 = jnp.mean(dxh * xh, axis=1, keepdims=True)
    return rstd * (dxh - m1 - xh * m2)


def _coords():
    return lax.axis_index("x"), lax.axis_index("y"), lax.axis_index("c")


def _peer(x, y, c, k):
    return (1 - x if k & 4 else x, 1 - y if k & 2 else y, 1 - c if k & 1 else c)


def _comm_sems(n):
    return [pltpu.SemaphoreType.DMA((N_DEV - 1, n)), pltpu.SemaphoreType.DMA((N_DEV - 1, n)),
            pltpu.SemaphoreType.DMA((n,))]


def _comm_copies(ins, outs, to_all, sems):
    send_sems, recv_sems, local_sems = sems
    x, y, c = _coords()
    me = 4 * x + 2 * y + c
    copies = [pltpu.make_async_copy(ins[t] if to_all[t] else ins[t].at[me], outs[t].at[me], local_sems.at[t])
              for t in range(len(ins))]
    for k in range(1, N_DEV):
        px, py, pc = _peer(x, y, c, k)
        dest = 4 * px + 2 * py + pc
        for t in range(len(ins)):
            copies.append(pltpu.make_async_remote_copy(
                src_ref=ins[t] if to_all[t] else ins[t].at[dest], dst_ref=outs[t].at[me],
                send_sem=send_sems.at[k - 1, t], recv_sem=recv_sems.at[k - 1, t],
                device_id=(px, py, pc), device_id_type=MESH))
    return copies


def _comm_out_shapes(ins, to_all):
    return [jax.ShapeDtypeStruct(((N_DEV,) + a.shape) if ta else a.shape, a.dtype) for a, ta in zip(ins, to_all)]


def _exchange(ins, to_all, name):
    n = len(ins)

    def body(*refs):
        copies = _comm_copies(refs[:n], refs[n:2 * n], to_all, refs[2 * n:])
        for cp in copies:
            cp.start()
        for cp in copies:
            cp.wait()

    return pl.pallas_call(
        body, name=name, out_shape=_comm_out_shapes(ins, to_all), in_specs=[ANY] * n, out_specs=[ANY] * n,
        scratch_shapes=_comm_sems(n),
    )(*ins)


def _gather_two_level(ins, name):
    n = len(ins)

    def body(*refs):
        srcs, outs = refs[:n], refs[n:2 * n]
        send_sems, recv_sems, local_sems = refs[2 * n:]
        x, y, c = _coords()
        me = 4 * x + 2 * y + c
        sibling = (x, y, 1 - c)
        chips = [(1 - x, y), (x, 1 - y), (1 - x, 1 - y)]
        slot = lambda px, py, pc: 4 * px + 2 * py + pc

        def copy(k, t, block, to, own=False):
            return pltpu.make_async_remote_copy(
                src_ref=srcs[t] if own else outs[t].at[block], dst_ref=outs[t].at[block],
                send_sem=send_sems.at[k, t], recv_sem=recv_sems.at[k, t], device_id=to, device_id_type=MESH)

        local = [pltpu.make_async_copy(srcs[t], outs[t].at[me], local_sems.at[t]) for t in range(n)]
        first = [copy(0, t, me, sibling, own=True) for t in range(n)]
        first += [copy(1 + j, t, me, (*chip, c), own=True) for j, chip in enumerate(chips) for t in range(n)]
        for cp in local + first:
            cp.start()
        passed = []
        for j, chip in enumerate(chips):
            for t in range(n):
                copy(1 + j, t, slot(*chip, c), (x, y, c)).wait_recv()
                cp = copy(4 + j, t, slot(*chip, c), sibling)
                cp.start()
                passed.append(cp)
        for t in range(n):
            copy(0, t, slot(x, y, 1 - c), (x, y, c)).wait_recv()
            for j, chip in enumerate(chips):
                copy(4 + j, t, slot(*chip, 1 - c), (x, y, c)).wait_recv()
        for cp in first + passed:
            cp.wait_send()
        for cp in local:
            cp.wait()

    return pl.pallas_call(
        body, name=name, out_shape=_comm_out_shapes(ins, [True] * n), in_specs=[ANY] * n, out_specs=[ANY] * n,
        scratch_shapes=_comm_sems(n),
    )(*ins)


def _adamw(parts, w, m, v, rows, name):
    n_parts, r_all, cols = parts.shape
    c1 = 1.0 - ADAM_B1 ** ADAM_STEP
    c2 = 1.0 - ADAM_B2 ** ADAM_STEP

    def body(p_ref, w_ref, m_ref, v_ref, g_ref, d_ref, mo_ref, vo_ref):
        g = p_ref[0].astype(F32)
        for s in range(1, n_parts):
            g = g + p_ref[s].astype(F32)
        mn = ADAM_B1 * m_ref[...] + (1.0 - ADAM_B1) * g
        vn = ADAM_B2 * v_ref[...] + (1.0 - ADAM_B2) * (g * g)
        m_hat = mn / c1
        v_hat = vn / c2
        g_ref[...] = g
        d_ref[...] = -ADAM_LR * (m_hat / (jnp.sqrt(v_hat) + ADAM_EPS) + ADAM_WD * w_ref[...])
        mo_ref[...] = mn
        vo_ref[...] = vn

    spec = pl.BlockSpec((rows, cols), lambda i: (i, 0))
    return pl.pallas_call(
        body, name=name, grid=(r_all // rows,),
        in_specs=[pl.BlockSpec((n_parts, rows, cols), lambda i: (0, i, 0)), spec, spec, spec],
        out_specs=[spec] * 4, out_shape=[jax.ShapeDtypeStruct((r_all, cols), F32)] * 4,
        compiler_params=_params(),
    )(parts, w, m, v)


def _matmul_tn(a, b, chunk, tk, name):
    t_all, k1 = a.shape
    n = b.shape[1]

    def body(a_ref, b_ref, o_ref):
        @pl.when(pl.program_id(0) == 0)
        def _():
            o_ref[...] = jnp.zeros_like(o_ref)
        at = a_ref[...].astype(F32).T.astype(BF16)
        for j in range(0, n, chunk):
            cs = slice(j, min(j + chunk, n))
            o_ref[:, cs] += _nn(at, b_ref[:, cs])

    return pl.pallas_call(
        body, name=name, grid=(t_all // tk,),
        in_specs=[pl.BlockSpec((tk, k1), lambda t: (t, 0)), pl.BlockSpec((tk, n), lambda t: (t, 0))],
        out_specs=pl.BlockSpec((k1, n), lambda t: (0, 0)),
        out_shape=jax.ShapeDtypeStruct((k1, n), F32), compiler_params=_params(),
    )(a, b)


def _matmul_rows(a, b, tk, name):
    r, t_all = a.shape
    n = b.shape[1]

    def body(a_ref, b_ref, o_ref):
        @pl.when(pl.program_id(0) == 0)
        def _():
            o_ref[...] = jnp.zeros_like(o_ref)
        o_ref[...] += _nn(a_ref[...], b_ref[...])

    return pl.pallas_call(
        body, name=name, grid=(t_all // tk,),
        in_specs=[pl.BlockSpec((r, tk), lambda t: (0, t)), pl.BlockSpec((tk, n), lambda t: (t, 0))],
        out_specs=pl.BlockSpec((r, n), lambda t: (0, 0)),
        out_shape=jax.ShapeDtypeStruct((r, n), F32), compiler_params=_params(),
    )(a, b)


def _ada_fwd(c_all, w_ada, b_ada):
    whole = lambda a: pl.BlockSpec(a.shape, lambda j: (0, 0))

    def body(c_ref, w_ref, b_ref, o_ref):
        cv = c_ref[...]
        s = (cv * jax.nn.sigmoid(cv)).astype(BF16)
        o_ref[...] = _nn(s, w_ref[...].astype(BF16)) + b_ref[...]

    out = jax.ShapeDtypeStruct((c_all.shape[0], w_ada.shape[1]), F32)
    return pl.pallas_call(
        body, name="ada_fwd", grid=(1,), in_specs=[whole(c_all), whole(w_ada), whole(b_ada)], out_specs=whole(out),
        out_shape=out, compiler_params=_params(),
    )(c_all, w_ada, b_ada)


def _ada_bwd(c_all, dada):
    whole = lambda a: pl.BlockSpec(a.shape, lambda j: (0, 0))

    def body(c_ref, d_ref, o_ref):
        cv = c_ref[...]
        s = (cv * jax.nn.sigmoid(cv)).astype(BF16)
        o_ref[...] = _tn(s, d_ref[...].astype(BF16))

    out = jax.ShapeDtypeStruct((D_MODEL, dada.shape[1]), F32)
    return pl.pallas_call(
        body, name="ada_bwd", grid=(1,), in_specs=[whole(c_all), whole(dada)], out_specs=whole(out), out_shape=out,
        compiler_params=_params(),
    )(c_all, dada)


TOK_TM = 256
DILATIONS = (1, 4, 16)


def _class_spec(d, width, nts):
    return pl.BlockSpec((d, TOK_TM // d, width), lambda i: (i // nts, i % nts, 0))


def _class_shape(t_all, seq, d, width, dtype):
    return jax.ShapeDtypeStruct((t_all // seq * d, seq // d, width), dtype)


def _inproj(x, ada3, pos, wqkv, wf16, freq, perms, seq):
    t_all = x.shape[0]
    tm = TOK_TM
    nts = seq // tm

    def body(x_ref, ada_ref, pos_ref, w_ref, wf_ref, fr_ref, p4_ref, p16_ref, h1_ref, za_ref, zb_ref, zb4_ref,
             zb16_ref, vt_ref, fa_ref):
        h1 = (x_ref[...] * (1.0 + ada_ref[0, 1:2, :]) + ada_ref[0, 0:1, :]).astype(BF16)
        h1_ref[...] = h1
        tabs = _rope_tabs(pos_ref, fr_ref, 1.0)
        for n in range(6):
            z = _nn(h1, w_ref[:, n * WIDTH:(n + 1) * WIDTH])
            if n in (3, 4):
                z = _rope(z, tabs)
            if n == 0:
                z = z * (Q_SCALE * LOG2E)
            if n == 3:
                z = z * Q_SCALE
            if n == 2:
                vt_ref[...] = z.T.astype(BF16)
            dst = za_ref if n < 3 else zb_ref
            dst[:, (n % 3) * WIDTH:(n % 3 + 1) * WIDTH] = z.astype(BF16)
        fa_ref[...] = _nt(wf_ref[...], h1)[:N_HEADS]
        zb = zb_ref[...]
        _store_classes(zb4_ref, _nn(p4_ref[...], zb).astype(BF16), 4)
        _store_classes(zb16_ref, _nn(p16_ref[...], zb).astype(BF16), 16)

    tok = lambda w: pl.BlockSpec((tm, w), lambda i: (i, 0))
    whole = lambda a: pl.BlockSpec(a.shape, lambda i: (0, 0))
    return pl.pallas_call(
        body, name="inproj", grid=(t_all // tm,),
        in_specs=[tok(D_MODEL), pl.BlockSpec((1, 6, D_MODEL), lambda i: (i // nts, 0, 0)), tok(1), whole(wqkv),
                  whole(wf16), pl.BlockSpec((1, LANES), lambda i: (0, 0)), whole(perms[0]), whole(perms[1])],
        out_specs=[tok(D_MODEL), tok(3 * WIDTH), tok(3 * WIDTH), _class_spec(4, 3 * WIDTH, nts),
                   _class_spec(16, 3 * WIDTH, nts), pl.BlockSpec((WIDTH, tm), lambda i: (i // nts, i % nts)),
                   pl.BlockSpec((N_HEADS, tm), lambda i: (0, i))],
        out_shape=[jax.ShapeDtypeStruct((t_all, D_MODEL), BF16), jax.ShapeDtypeStruct((t_all, 3 * WIDTH), BF16),
                   jax.ShapeDtypeStruct((t_all, 3 * WIDTH), BF16), _class_shape(t_all, seq, 4, 3 * WIDTH, BF16),
                   _class_shape(t_all, seq, 16, 3 * WIDTH, BF16),
                   jax.ShapeDtypeStruct((t_all // seq * WIDTH, seq), BF16),
                   jax.ShapeDtypeStruct((N_HEADS, t_all), F32)],
        compiler_params=_params(),
    )(x, ada3, pos, wqkv, wf16, freq, perms[0], perms[1])


def _chunk_rows(a_t, seq):
    t_all = a_t.shape[1]
    return a_t.reshape(N_HEADS, t_all // seq, seq // LANES, LANES).transpose(1, 0, 2, 3).reshape(-1, LANES)


def _unchunk_rows(a, seq):
    nbat = a.shape[0] * LANES // (N_HEADS * seq)
    return a.reshape(nbat, N_HEADS, seq // LANES, LANES).transpose(1, 0, 2, 3).reshape(N_HEADS, nbat * seq)


def _chunk_carry(tot, nchunk, later):
    rows = tot.shape[0]
    r = lax.broadcasted_iota(jnp.int32, (rows, rows), 0)
    c = lax.broadcasted_iota(jnp.int32, (rows, rows), 1)
    sel = ((r // nchunk) == (c // nchunk)) & ((c > r) if later else (c < r))
    mat = sel.astype(BF16)
    return sum(_nn(mat, part) for part in _split3(jnp.broadcast_to(tot, (rows, LANES))))


def _fgate_fwd(fa_t, bf, seq):
    x = _chunk_rows(fa_t, seq)
    rows = x.shape[0]
    nchunk = seq // LANES
    bias = jnp.broadcast_to(bf.reshape(1, N_HEADS, 1), (rows // (N_HEADS * nchunk), N_HEADS, nchunk)).reshape(rows, 1)

    def body(x_ref, b_ref, f_ref):
        lane = lax.broadcasted_iota(jnp.int32, (rows, LANES), 1)
        xv = x_ref[...] + b_ref[...]
        lf = jnp.minimum(xv, 0.0) - jnp.log(1.0 + jnp.exp(-jnp.abs(xv)))
        for s in (1, 2, 4, 8, 16, 32, 64):
            lf = lf + jnp.where(lane >= s, pltpu.roll(lf, s, 1), 0.0)
        f_ref[...] = lf + _chunk_carry(lf[:, LANES - 1:LANES], nchunk, False)

    whole = lambda a: pl.BlockSpec(a.shape, lambda i: (0, 0))
    out = pl.pallas_call(
        body, name="fgate_fwd", grid=(1,), in_specs=[whole(x), whole(bias)], out_specs=whole(x),
        out_shape=jax.ShapeDtypeStruct(x.shape, F32), compiler_params=_params(),
    )(x, bias)
    return _unchunk_rows(out, seq)


def _fgate_bwd(df_t, fa_t, bf, seq):
    d_in = _chunk_rows(df_t, seq)
    x = _chunk_rows(fa_t, seq)
    rows = x.shape[0]
    nchunk = seq // LANES
    bias = jnp.broadcast_to(bf.reshape(1, N_HEADS, 1), (rows // (N_HEADS * nchunk), N_HEADS, nchunk)).reshape(rows, 1)

    def body(d_ref, x_ref, b_ref, o_ref, s_ref):
        lane = lax.broadcasted_iota(jnp.int32, (rows, LANES), 1)
        d = d_ref[...]
        for s in (1, 2, 4, 8, 16, 32, 64):
            d = d + jnp.where(lane < LANES - s, pltpu.roll(d, LANES - s, 1), 0.0)
        d = d + _chunk_carry(d[:, 0:1], nchunk, True)
        dfa = d * jax.nn.sigmoid(-(x_ref[...] + b_ref[...]))
        o_ref[...] = dfa
        g = lax.broadcasted_iota(jnp.int32, (2 * N_HEADS, rows), 0)
        r = lax.broadcasted_iota(jnp.int32, (2 * N_HEADS, rows), 1)
        group = (((r // nchunk) % N_HEADS) == g).astype(BF16)
        per_head = sum(_nn(group, part) for part in _split3(dfa))[:N_HEADS]
        s_ref[...] = jnp.broadcast_to(jnp.sum(per_head, axis=1, keepdims=True), (N_HEADS, LANES))

    whole = lambda a: pl.BlockSpec(a.shape, lambda i: (0, 0))
    dfa, sums = pl.pallas_call(
        body, name="fgate_bwd", grid=(1,), in_specs=[whole(d_in), whole(x), whole(bias)],
        out_specs=[whole(x), pl.BlockSpec((N_HEADS, LANES), lambda i: (0, 0))],
        out_shape=[jax.ShapeDtypeStruct(x.shape, F32), jax.ShapeDtypeStruct((N_HEADS, LANES), F32)],
        compiler_params=_params(),
    )(d_in, x, bias)
    return _unchunk_rows(dfa, seq), sums


FOX_T = 256


def _fox_prep(dst, src_ref, lo, hi):
    for p in range(4):
        v = src_ref[:, LANES * p:LANES * (p + 1)]
        dst[2 * p] = jnp.where(lo, v, jnp.zeros_like(v))
        dst[2 * p + 1] = jnp.where(hi, v, jnp.zeros_like(v))


def _fox_fwd(za, vt, f_col, seq, shards):
    t_all = za.shape[0]
    tq = FOX_T
    nq = seq // tq
    nbat = t_all // seq
    n = len(shards)
    to_all = [True] * n

    def body(*refs):
        q_ref, k_ref, vt_ref, fc_ref = refs[:4]
        o_ref, lse_ref = refs[4 + n:6 + n]
        qm_sc, m_sc, l_sc, acc_sc, a_sc, st_sc, pe_sc = refs[6 + 2 * n:13 + 2 * n]
        comm = (refs[4:4 + n], refs[6 + n:6 + 2 * n], to_all, refs[13 + 2 * n:])
        i = pl.program_id(1)

        @pl.when((pl.program_id(0) == 0) & (i == 0))
        def _():
            for cp in _comm_copies(*comm):
                cp.start()
        lo, hi = _half_masks(tq)
        r = lax.broadcasted_iota(jnp.int32, (tq, tq), 0)
        c = lax.broadcasted_iota(jnp.int32, (tq, tq), 1)
        tri = c >= r
        _fox_prep(qm_sc, q_ref, lo, hi)
        m_sc[...] = jnp.full(m_sc.shape, NEG, F32)
        l_sc[...] = jnp.zeros_like(l_sc)
        acc_sc[...] = jnp.zeros_like(acc_sc)

        def block(j, masked):
            sl = pl.ds(pl.multiple_of(j * tq, tq), tq)
            for p in range(4):
                kj = k_ref[sl, LANES * p:LANES * (p + 1)]
                for h in (2 * p, 2 * p + 1):
                    st = _nt(kj, qm_sc[h]) - fc_ref[sl, h:h + 1]
                    st_sc[h] = jnp.where(tri, st, NEG) if masked else st
            for h in range(N_HEADS):
                st = st_sc[h]
                m = m_sc[h:h + 1, :]
                mn = jnp.maximum(m, jnp.max(st, axis=0, keepdims=True))
                a = jnp.exp2(m - mn)
                pe = jnp.exp2(st - mn)
                m_sc[h:h + 1, :] = mn
                a_sc[h:h + 1, :] = a
                l_sc[h:h + 1, :] = a * l_sc[h:h + 1, :] + jnp.sum(pe, axis=0, keepdims=True)
                pe_sc[h] = pe.astype(BF16)
            for h in range(N_HEADS):
                acc_sc[h] = a_sc[h:h + 1, :] * acc_sc[h] + _nn(vt_ref[HEAD_DIM * h:HEAD_DIM * (h + 1), sl], pe_sc[h])

        def step(j, carry):
            block(j, False)
            return carry

        lax.fori_loop(0, i, step, 0)
        block(i, True)
        lse_ref[...] = m_sc[...] + jnp.log(l_sc[...]) * LOG2E
        for p in range(4):
            ot = jnp.concatenate([acc_sc[h] / l_sc[h:h + 1, :] for h in (2 * p, 2 * p + 1)], axis=0)
            o_ref[:, LANES * p:LANES * (p + 1)] = ot.T

        @pl.when((pl.program_id(0) == nbat - 1) & (i == nq - 1))
        def _():
            for cp in _comm_copies(*comm):
                cp.wait()

    res = pl.pallas_call(
        body, name="fox_fwd", grid=(nbat, nq),
        in_specs=[pl.BlockSpec((tq, WIDTH), lambda b, i: (b * nq + i, 0)),
                  pl.BlockSpec((seq, WIDTH), lambda b, i: (b, 1)), pl.BlockSpec((WIDTH, seq), lambda b, i: (b, 0)),
                  pl.BlockSpec((seq, LANES), lambda b, i: (b, 0))] + [ANY] * n,
        out_specs=[pl.BlockSpec((tq, WIDTH), lambda b, i: (b * nq + i, 0)),
                   pl.BlockSpec((N_HEADS, tq), lambda b, i: (0, b * nq + i))] + [ANY] * n,
        out_shape=[jax.ShapeDtypeStruct((t_all, WIDTH), F32), jax.ShapeDtypeStruct((N_HEADS, t_all), F32)]
        + _comm_out_shapes(shards, to_all),
        scratch_shapes=[pltpu.VMEM((N_HEADS, tq, LANES), BF16), pltpu.VMEM((N_HEADS, tq), F32),
                        pltpu.VMEM((N_HEADS, tq), F32), pltpu.VMEM((N_HEADS, HEAD_DIM, tq), F32),
                        pltpu.VMEM((N_HEADS, tq), F32), pltpu.VMEM((N_HEADS, tq, tq), F32),
                        pltpu.VMEM((N_HEADS, tq, tq), BF16)] + _comm_sems(n),
        compiler_params=_params(),
    )(za, za, vt, f_col, *shards)
    return res[0], res[1], res[2:]


def _fox_bwd(za, do, f_col, lse_row, dl_row, seq, grads):
    t_all = za.shape[0]
    tk = FOX_T
    nk = seq // tk
    nbat = t_all // seq
    n = len(grads)
    to_all = [False] * n

    def body(*refs):
        k_ref, v_ref, q_ref, do_ref, fc_ref, lr_ref, dr_ref = refs[:7]
        dk_ref, dv_ref, df_ref, dqt_ref, dfq_ref = refs[7 + n:12 + n]
        km_sc, vm_sc, fk_sc, dk_sc, dv_sc, cs_sc, kt_sc, st_sc, dp_sc, pt_sc, ds_sc = refs[12 + 2 * n:23 + 2 * n]
        comm = (refs[7:7 + n], refs[12 + n:12 + 2 * n], to_all, refs[23 + 2 * n:])
        j = pl.program_id(1)

        @pl.when(j == 0)
        def _():
            dqt_ref[...] = jnp.zeros_like(dqt_ref)
            dfq_ref[...] = jnp.zeros_like(dfq_ref)

        @pl.when((pl.program_id(0) == 0) & (j == 0))
        def _():
            for cp in _comm_copies(*comm):
                cp.start()
        lo, hi = _half_masks(tk)
        r = lax.broadcasted_iota(jnp.int32, (tk, tk), 0)
        c = lax.broadcasted_iota(jnp.int32, (tk, tk), 1)
        tri = c >= r
        _fox_prep(km_sc, k_ref, lo, hi)
        _fox_prep(vm_sc, v_ref, lo, hi)
        for h in range(N_HEADS):
            fk_sc[h] = jnp.broadcast_to(fc_ref[:, h:h + 1], (tk, tk))
        for p in range(4):
            kt_sc[p] = k_ref[:, LANES * p:LANES * (p + 1)].astype(F32).T.astype(BF16)
        dk_sc[...] = jnp.zeros_like(dk_sc)
        dv_sc[...] = jnp.zeros_like(dv_sc)
        cs_sc[...] = jnp.zeros_like(cs_sc)

        def block(i, masked):
            sl = pl.ds(pl.multiple_of(i * tk, tk), tk)
            for p in range(4):
                cs = slice(LANES * p, LANES * (p + 1))
                qi = q_ref[sl, cs]
                doi = do_ref[sl, cs]
                for h in (2 * p, 2 * p + 1):
                    st = _nt(km_sc[h], qi) - fk_sc[h] - lr_ref[h:h + 1, sl]
                    st_sc[h] = jnp.where(tri, st, NEG) if masked else st
                    dp_sc[h] = _nt(vm_sc[h], doi) - dr_ref[h:h + 1, sl]
            for h in range(N_HEADS):
                pt = jnp.exp2(st_sc[h])
                dst = pt * dp_sc[h]
                pt_sc[h] = pt.astype(BF16)
                ds_sc[h] = dst.astype(BF16)
                cs_sc[h] += dst[:, :LANES] + dst[:, LANES:]
                dfq_ref[h:h + 1, sl] += jnp.sum(dst, axis=0, keepdims=True)
            for p in range(4):
                cs = slice(LANES * p, LANES * (p + 1))
                qi = q_ref[sl, cs]
                doi = do_ref[sl, cs]
                for h in (2 * p, 2 * p + 1):
                    dv_sc[h] += _nn(pt_sc[h], doi)
                    dk_sc[h] += _nn(ds_sc[h], qi)
                    kt = kt_sc[p, HEAD_DIM * (h % 2):HEAD_DIM * (h % 2 + 1), :]
                    dqt_ref[HEAD_DIM * h:HEAD_DIM * (h + 1), sl] += _nn(kt, ds_sc[h])

        def step(i, carry):
            block(i, False)
            return carry

        block(j, True)
        lax.fori_loop(j + 1, nk, step, 0)
        df_ref[...] = jnp.zeros_like(df_ref)
        for p in range(4):
            cs = slice(LANES * p, LANES * (p + 1))
            dk_ref[:, cs] = (jnp.where(lo, dk_sc[2 * p], dk_sc[2 * p + 1]) * (1.0 / LOG2E)).astype(BF16)
            dv_ref[:, cs] = jnp.where(lo, dv_sc[2 * p], dv_sc[2 * p + 1]).astype(BF16)
            for h in (2 * p, 2 * p + 1):
                df_ref[:, h:h + 1] = -jnp.sum(cs_sc[h], axis=1, keepdims=True)

        @pl.when(j == nk - 1)
        def _():
            dqt_ref[...] = dqt_ref[...] * Q_SCALE

        @pl.when((pl.program_id(0) == nbat - 1) & (j == nk - 1))
        def _():
            for cp in _comm_copies(*comm):
                cp.wait()

    tile = lambda w, col: pl.BlockSpec((tk, w), lambda b, j: (b * nk + j, col))
    full = lambda col: pl.BlockSpec((seq, WIDTH), lambda b, j: (b, col))
    row = pl.BlockSpec((N_HEADS, seq), lambda b, j: (0, b))
    acc = pltpu.VMEM((N_HEADS, tk, LANES), F32)
    res = pl.pallas_call(
        body, name="fox_bwd", grid=(nbat, nk),
        in_specs=[tile(WIDTH, 1), tile(WIDTH, 2), full(0), full(0), tile(LANES, 0), row, row] + [ANY] * n,
        out_specs=[tile(WIDTH, 0), tile(WIDTH, 0), tile(LANES, 0), pl.BlockSpec((WIDTH, seq), lambda b, j: (b, 0)),
                   row] + [ANY] * n,
        out_shape=[jax.ShapeDtypeStruct((t_all, WIDTH), BF16), jax.ShapeDtypeStruct((t_all, WIDTH), BF16),
                   jax.ShapeDtypeStruct((t_all, LANES), F32), jax.ShapeDtypeStruct((nbat * WIDTH, seq), F32),
                   jax.ShapeDtypeStruct((N_HEADS, t_all), F32)] + _comm_out_shapes(grads, to_all),
        scratch_shapes=[pltpu.VMEM((N_HEADS, tk, LANES), BF16), pltpu.VMEM((N_HEADS, tk, LANES), BF16),
                        pltpu.VMEM((N_HEADS, tk, tk), F32), acc, acc, acc, pltpu.VMEM((4, LANES, tk), BF16),
                        pltpu.VMEM((N_HEADS, tk, tk), F32), pltpu.VMEM((N_HEADS, tk, tk), F32),
                        pltpu.VMEM((N_HEADS, tk, tk), BF16), pltpu.VMEM((N_HEADS, tk, tk), BF16)]
        + _comm_sems(n),
        compiler_params=_params(),
    )(za, za, za, do, f_col, lse_row, dl_row, *grads)
    return res[0], res[1], res[2], res[3], res[4], res[5:]


DIL_SUB = 4


def _dil_mask(has_prev):
    qi = lax.broadcasted_iota(jnp.int32, (BLK, 2 * BLK), 0)
    kj = lax.broadcasted_iota(jnp.int32, (BLK, 2 * BLK), 1)
    dist = qi + BLK - kj
    band = (dist >= 0) & (dist <= BLK)
    return band if has_prev is True else band & ((kj >= BLK) | has_prev)


def _dil_geometry(t_all, seq, d, max_sub=DIL_SUB):
    length = seq // d
    nbs = length // BLK
    sub = min(max_sub, nbs)
    spb = nbs // sub
    tile = lambda width, col: pl.BlockSpec((BLK * sub, width), lambda s: (s, col))
    whole = lambda width, col: pl.BlockSpec((length, width), lambda s: (s // spb, col))
    return nbs, sub, spb, t_all // (BLK * sub), tile, whole


def _blk(i):
    return pl.ds(pl.multiple_of(i * BLK, BLK), BLK)


def _dil_fwd(zb, seq, d):
    t_all = zb.shape[0]
    nbs, sub, spb, steps, tile, whole = _dil_geometry(t_all, seq, d)

    def body(q_ref, k_ref, v_ref, o_ref, lse_ref, s_sc, p_sc):
        first = (pl.program_id(0) % spb) * sub
        lo, hi = _half_masks(BLK)
        lse_ref[...] = jnp.zeros_like(lse_ref)
        for j in range(sub):
            blk = first + j
            mask = _dil_mask(blk != 0 if j == 0 else True)
            for p in range(4):
                cs = slice(LANES * p, LANES * (p + 1))
                qp = q_ref[BLK * j:BLK * (j + 1), cs]
                kcat = jnp.concatenate([k_ref[_blk(jnp.maximum(blk - 1, 0)), cs], k_ref[_blk(blk), cs]], axis=0)
                for e in (0, 1):
                    qe = jnp.where(lo if e == 0 else hi, qp, jnp.zeros_like(qp))
                    s_sc[N_HEADS * j + 2 * p + e] = jnp.where(mask, _nt(qe, kcat), NEG)
        inv = []
        for i in range(N_HEADS * sub):
            s = s_sc[i]
            m = jnp.max(s, axis=1, keepdims=True)
            pe = jnp.exp(s - m)
            l = jnp.sum(pe, axis=1, keepdims=True)
            p_sc[i] = pe.astype(BF16)
            inv.append(1.0 / l)
            j, h = divmod(i, N_HEADS)
            lse_ref[BLK * j:BLK * (j + 1), h:h + 1] = m + jnp.log(l)
        for j in range(sub):
            blk = first + j
            for p in range(4):
                cs = slice(LANES * p, LANES * (p + 1))
                vcat = jnp.concatenate([v_ref[_blk(jnp.maximum(blk - 1, 0)), cs], v_ref[_blk(blk), cs]], axis=0)
                res = [_nn(p_sc[N_HEADS * j + h], vcat) * inv[N_HEADS * j + h] for h in (2 * p, 2 * p + 1)]
                o_ref[BLK * j:BLK * (j + 1), cs] = jnp.where(lo, res[0], res[1])

    return pl.pallas_call(
        body, name=f"dil_fwd_{d}", grid=(steps,), in_specs=[tile(WIDTH, 0), whole(WIDTH, 1), whole(WIDTH, 2)],
        out_specs=[tile(WIDTH, 0), tile(LANES, 0)],
        out_shape=[jax.ShapeDtypeStruct((t_all, WIDTH), F32), jax.ShapeDtypeStruct((t_all, LANES), F32)],
        scratch_shapes=[pltpu.VMEM((N_HEADS * sub, BLK, 2 * BLK), F32),
                        pltpu.VMEM((N_HEADS * sub, BLK, 2 * BLK), BF16)],
        compiler_params=_params(),
    )(zb, zb, zb)


def _dil_bwd(zb, do, lse, dl, seq, d):
    t_all = zb.shape[0]
    length = seq // d
    nbs, sub, spb, steps, tile, whole = _dil_geometry(t_all, seq, d, 2 if length >= 4096 else DIL_SUB)

    def body(k_ref, v_ref, q_ref, do_ref, lse_ref, dl_ref, dq_ref, dk_ref, dv_ref, s_sc, dp_sc, pt_sc, ds_sc, kt_sc,
             dqt_sc):
        step = pl.program_id(0) % spb
        first = step * sub

        @pl.when(step == 0)
        def _():
            dqt_sc[...] = jnp.zeros_like(dqt_sc)
        r = lax.broadcasted_iota(jnp.int32, (BLK, 2 * BLK), 0)
        c = lax.broadcasted_iota(jnp.int32, (BLK, 2 * BLK), 1)
        same = (c < BLK) & (c >= r)
        later = (c >= BLK) & (c - BLK <= r)
        lo, hi = _half_masks(BLK)
        for j in range(sub):
            blk = first + j
            rows = slice(BLK * j, BLK * (j + 1))
            nxt = _blk(jnp.minimum(blk + 1, nbs - 1))
            mask = same | (later & (blk + 1 != nbs)) if j == sub - 1 else same | later
            lrows = jnp.concatenate([lse_ref[_blk(blk), :].T, lse_ref[nxt, :].T], axis=1)
            erows = jnp.concatenate([dl_ref[_blk(blk), :].T, dl_ref[nxt, :].T], axis=1)
            for p in range(4):
                cs = slice(LANES * p, LANES * (p + 1))
                kp = k_ref[rows, cs]
                vp = v_ref[rows, cs]
                kt_sc[4 * j + p] = kp.astype(F32).T.astype(BF16)
                qcat = jnp.concatenate([q_ref[_blk(blk), cs], q_ref[nxt, cs]], axis=0)
                dcat = jnp.concatenate([do_ref[_blk(blk), cs], do_ref[nxt, cs]], axis=0)
                for e in (0, 1):
                    h = 2 * p + e
                    sel = lo if e == 0 else hi
                    ke = jnp.where(sel, kp, jnp.zeros_like(kp))
                    ve = jnp.where(sel, vp, jnp.zeros_like(vp))
                    s_sc[N_HEADS * j + h] = jnp.where(mask, _nt(ke, qcat) - lrows[h:h + 1, :], NEG)
                    dp_sc[N_HEADS * j + h] = _nt(ve, dcat) - erows[h:h + 1, :]
        for i in range(N_HEADS * sub):
            pt = jnp.exp(s_sc[i])
            pt_sc[i] = pt.astype(BF16)
            ds_sc[i] = (pt * dp_sc[i]).astype(BF16)
        for j in range(sub):
            blk = first + j
            rows = slice(BLK * j, BLK * (j + 1))
            nxt = _blk(jnp.minimum(blk + 1, nbs - 1))
            cols = pl.ds(pl.multiple_of(blk * BLK, BLK), 2 * BLK)
            for p in range(4):
                cs = slice(LANES * p, LANES * (p + 1))
                qcat = jnp.concatenate([q_ref[_blk(blk), cs], q_ref[nxt, cs]], axis=0)
                dcat = jnp.concatenate([do_ref[_blk(blk), cs], do_ref[nxt, cs]], axis=0)
                i = N_HEADS * j + 2 * p
                dk_ref[rows, cs] = jnp.where(lo, _nn(ds_sc[i], qcat), _nn(ds_sc[i + 1], qcat)).astype(BF16)
                dv_ref[rows, cs] = jnp.where(lo, _nn(pt_sc[i], dcat), _nn(pt_sc[i + 1], dcat)).astype(BF16)
                for e in (0, 1):
                    kt = kt_sc[4 * j + p, HEAD_DIM * e:HEAD_DIM * (e + 1), :]
                    dqt_sc[HEAD_DIM * (2 * p + e):HEAD_DIM * (2 * p + e + 1), cols] += _nn(kt, ds_sc[i + e])

        @pl.when(step == spb - 1)
        def _():
            for p in range(4):
                cs = slice(LANES * p, LANES * (p + 1))
                dq_ref[:, cs] = (dqt_sc[cs, 0:length].T * Q_SCALE).astype(BF16)

    wide = pltpu.VMEM((N_HEADS * sub, BLK, 2 * BLK), F32)
    half = pltpu.VMEM((N_HEADS * sub, BLK, 2 * BLK), BF16)
    return pl.pallas_call(
        body, name=f"dil_bwd_{d}", grid=(steps,),
        in_specs=[tile(WIDTH, 1), tile(WIDTH, 2), whole(WIDTH, 0), whole(WIDTH, 0), whole(LANES, 0), whole(LANES, 0)],
        out_specs=[whole(WIDTH, 0), tile(WIDTH, 0), tile(WIDTH, 0)],
        out_shape=[jax.ShapeDtypeStruct((t_all, WIDTH), BF16)] * 3,
        scratch_shapes=[wide, wide, half, half, pltpu.VMEM((4 * sub, LANES, BLK), BF16),
                        pltpu.VMEM((WIDTH, length + BLK), F32)],
        compiler_params=_params(),
    )(zb, zb, zb, do, lse, dl)


def _mix_out(oa, o3, l3, gn_a, gn_b, w_out, x, ada3, ln_g, ln_b, perms, seq):
    t_all = x.shape[0]
    tm = TOK_TM
    nts = seq // tm

    def body(oa_ref, o1_ref, o2_ref, o3_ref, l1_ref, l2_ref, l3_ref, ga_ref, gb_ref, w_ref, x_ref, ada_ref, g_ref,
             b_ref, p4_ref, p16_ref, pt4_ref, pt16_ref, ob_ref, lse_ref, lse4_ref, lse16_ref, mg_ref, mix_ref, xh_ref,
             rs_ref, h2_ref, h2t_ref):
        e, et = _head_mats()
        la = l1_ref[...]
        lb = _permute_f32(pt4_ref[...], _load_classes(l2_ref, 4))
        lc = _permute_f32(pt16_ref[...], _load_classes(l3_ref, 16))
        mx = jnp.maximum(jnp.maximum(la, lb), lc)
        ea, eb, ec = jnp.exp(la - mx), jnp.exp(lb - mx), jnp.exp(lc - mx)
        tot = ea + eb + ec
        lse = mx + jnp.log(tot)
        lse_ref[...] = lse
        _store_classes(lse4_ref, _permute_f32(p4_ref[...], lse), 4)
        _store_classes(lse16_ref, _permute_f32(p16_ref[...], lse), 16)
        ob = (o1_ref[...] * _hexp(ea / tot, e)
              + _permute_f32(pt4_ref[...], _load_classes(o2_ref, 4)) * _hexp(eb / tot, e)
              + _permute_f32(pt16_ref[...], _load_classes(o3_ref, 16)) * _hexp(ec / tot, e))
        ob_ref[...] = ob

        def rms(o, gain):
            rr = lax.rsqrt(_hsum(o * o, et) * (1.0 / HEAD_DIM) + RMS_EPS)
            return o * _hexp(rr, e) * gain

        merged = jnp.concatenate([rms(oa_ref[...], ga_ref[...]), rms(ob, gb_ref[...])], axis=1).astype(BF16)
        mg_ref[...] = merged
        mix = _nn(merged, w_ref[...])
        mix_ref[...] = mix.astype(BF16)
        r1 = ALPHA * x_ref[...] + ada_ref[0, 2:3, :] * mix
        d = r1 - jnp.mean(r1, axis=1, keepdims=True)
        rstd = lax.rsqrt(jnp.mean(d * d, axis=1, keepdims=True) + LN_EPS)
        xh = d * rstd
        xh_ref[...] = xh
        rs_ref[...] = jnp.broadcast_to(rstd, (tm, LANES))
        x1 = xh * g_ref[...] + b_ref[...]
        h2 = x1 * (1.0 + ada_ref[0, 4:5, :]) + ada_ref[0, 3:4, :]
        h2_ref[...] = h2.astype(BF16)
        h2t_ref[0] = h2.T.astype(BF16)

    tok = lambda w: pl.BlockSpec((tm, w), lambda i: (i, 0))
    vec = lambda w: pl.BlockSpec((1, w), lambda i: (0, 0))
    whole = lambda a: pl.BlockSpec(a.shape, lambda i: (0, 0))
    classes = lambda a, d: a.reshape(t_all // seq * d, seq // d, a.shape[-1])
    return pl.pallas_call(
        body, name="mix_out", grid=(t_all // tm,),
        in_specs=[tok(WIDTH), tok(WIDTH), _class_spec(4, WIDTH, nts), _class_spec(16, WIDTH, nts), tok(LANES),
                  _class_spec(4, LANES, nts), _class_spec(16, LANES, nts), vec(WIDTH), vec(WIDTH), whole(w_out),
                  tok(D_MODEL), pl.BlockSpec((1, 6, D_MODEL), lambda i: (i // nts, 0, 0)), vec(D_MODEL), vec(D_MODEL)]
        + [whole(p) for p in perms],
        out_specs=[tok(WIDTH), tok(LANES), _class_spec(4, LANES, nts), _class_spec(16, LANES, nts), tok(D_MODEL),
                   tok(D_MODEL), tok(D_MODEL), tok(LANES), tok(D_MODEL), pl.BlockSpec((1, D_MODEL, tm), lambda i: (i // (FFN_TM // tm), 0, i % (FFN_TM // tm)))],
        out_shape=[jax.ShapeDtypeStruct((t_all, WIDTH), F32), jax.ShapeDtypeStruct((t_all, LANES), F32),
                   _class_shape(t_all, seq, 4, LANES, F32), _class_shape(t_all, seq, 16, LANES, F32),
                   jax.ShapeDtypeStruct((t_all, D_MODEL), BF16), jax.ShapeDtypeStruct((t_all, D_MODEL), BF16),
                   jax.ShapeDtypeStruct((t_all, D_MODEL), F32), jax.ShapeDtypeStruct((t_all, LANES), F32),
                   jax.ShapeDtypeStruct((t_all, D_MODEL), BF16),
                   jax.ShapeDtypeStruct((t_all // FFN_TM, D_MODEL, FFN_TM), BF16)],
        compiler_params=_params(),
    )(oa, o3[0], classes(o3[1], 4), classes(o3[2], 16), l3[0], classes(l3[1], 4), classes(l3[2], 16), gn_a, gn_b,
      w_out, x, ada3, ln_g, ln_b, *perms)


def _mix_out_bwd(dmix, w_out, oa, ob, gn_a, gn_b, perms, seq):
    t_all = dmix.shape[0]
    tm = TOK_TM
    nts = seq // tm

    def body(dm_ref, w_ref, oa_ref, ob_ref, ga_ref, gb_ref, p4_ref, p16_ref, doa_ref, dob_ref, dob4_ref, dob16_ref,
             dla_ref, dlb_ref, dlb4_ref, dlb16_ref, acc_ref):
        @pl.when(pl.program_id(0) == 0)
        def _():
            acc_ref[...] = jnp.zeros_like(acc_ref)
        e, et = _head_mats()
        dmg = _nt(dm_ref[...], w_ref[...])

        def group(o, dn, gain):
            rr = lax.rsqrt(_hsum(o * o, et) * (1.0 / HEAD_DIM) + RMS_EPS)
            re = _hexp(rr, e)
            dgain = jnp.sum(dn * o * re, axis=0, keepdims=True)
            dxn = dn * gain
            tt = _hsum(dxn * o, et) * (rr * rr * rr) * (1.0 / HEAD_DIM)
            do = re * dxn - o * _hexp(tt, e)
            return do, _hsum(do * o, et), dgain

        doa, dla, dga = group(oa_ref[...], dmg[:, :WIDTH], ga_ref[...])
        dob, dlb, dgb = group(ob_ref[...], dmg[:, WIDTH:], gb_ref[...])
        dob = dob.astype(BF16)
        doa_ref[...] = doa.astype(BF16)
        dob_ref[...] = dob
        _store_classes(dob4_ref, _nn(p4_ref[...], dob).astype(BF16), 4)
        _store_classes(dob16_ref, _nn(p16_ref[...], dob).astype(BF16), 16)
        dla_ref[...] = dla
        dlb_ref[...] = dlb
        _store_classes(dlb4_ref, _permute_f32(p4_ref[...], dlb), 4)
        _store_classes(dlb16_ref, _permute_f32(p16_ref[...], dlb), 16)
        acc_ref[0:1, :] += jnp.concatenate([dga, dgb], axis=1)

    tok = lambda w: pl.BlockSpec((tm, w), lambda i: (i, 0))
    vec = lambda w: pl.BlockSpec((1, w), lambda i: (0, 0))
    return pl.pallas_call(
        body, name="mix_out_bwd", grid=(t_all // tm,),
        in_specs=[tok(D_MODEL), pl.BlockSpec(w_out.shape, lambda i: (0, 0)), tok(WIDTH), tok(WIDTH), vec(WIDTH),
                  vec(WIDTH), pl.BlockSpec(perms[0].shape, lambda i: (0, 0)),
                  pl.BlockSpec(perms[1].shape, lambda i: (0, 0))],
        out_specs=[tok(WIDTH), tok(WIDTH), _class_spec(4, WIDTH, nts), _class_spec(16, WIDTH, nts), tok(LANES),
                   tok(LANES), _class_spec(4, LANES, nts), _class_spec(16, LANES, nts),
                   pl.BlockSpec((8, D_MODEL), lambda i: (0, 0))],
        out_shape=[jax.ShapeDtypeStruct((t_all, WIDTH), BF16), jax.ShapeDtypeStruct((t_all, WIDTH), BF16),
                   _class_shape(t_all, seq, 4, WIDTH, BF16), _class_shape(t_all, seq, 16, WIDTH, BF16),
                   jax.ShapeDtypeStruct((t_all, LANES), F32), jax.ShapeDtypeStruct((t_all, LANES), F32),
                   _class_shape(t_all, seq, 4, LANES, F32), _class_shape(t_all, seq, 16, LANES, F32),
                   jax.ShapeDtypeStruct((8, D_MODEL), F32)],
        compiler_params=_params(),
    )(dmix, w_out, oa, ob, gn_a, gn_b, perms[0], perms[1])


def _inproj_bwd(dqt, dka, dva, dil1, dil4, dil16, dfa16, pos, wqkv, wf16, freq, perms, dr1, x, ada3, seq):
    t_all = x.shape[0]
    tm = TOK_TM
    nts = seq // tm

    def body(dqt_ref, dka_ref, dva_ref, q1_ref, k1_ref, v1_ref, q4_ref, k4_ref, v4_ref, q16_ref, k16_ref, v16_ref,
             dfa_ref, pos_ref, w_ref, wf_ref, fr_ref, pt4_ref, pt16_ref, dr1_ref, x_ref, ada_ref, gx_ref, dz_ref,
             acc_ref):
        i = pl.program_id(0)

        @pl.when(i == 0)
        def _():
            acc_ref[...] = jnp.zeros_like(acc_ref)
        tabs = _rope_tabs(pos_ref, fr_ref, -1.0)
        dz_ref[:, :WIDTH] = dqt_ref[...].T.astype(BF16)
        dz_ref[:, WIDTH:2 * WIDTH] = dka_ref[...]
        dz_ref[:, 2 * WIDTH:3 * WIDTH] = dva_ref[...]
        for t, (n1, n4, n16) in enumerate(((q1_ref, q4_ref, q16_ref), (k1_ref, k4_ref, k16_ref),
                                           (v1_ref, v4_ref, v16_ref))):
            tot = (n1[...].astype(F32) + _nn(pt4_ref[...], _load_classes(n4, 4))
                   + _nn(pt16_ref[...], _load_classes(n16, 16)))
            if t < 2:
                tot = _rope(tot, tabs)
            dz_ref[:, (3 + t) * WIDTH:(4 + t) * WIDTH] = tot.astype(BF16)
        dh1 = _tn(dfa_ref[...], wf_ref[...])
        for n in range(6):
            cs = slice(n * WIDTH, (n + 1) * WIDTH)
            dh1 = dh1 + _nt(dz_ref[:, cs], w_ref[:, cs])
        xv = x_ref[...]
        gx_ref[...] = ALPHA * dr1_ref[...] + dh1 * (1.0 + ada_ref[0, 1:2, :])
        b = i // nts
        acc_ref[pl.ds(b, 1), :] += jnp.sum(dh1 * xv, axis=0, keepdims=True)
        acc_ref[pl.ds(8 + b, 1), :] += jnp.sum(dh1, axis=0, keepdims=True)

    tok = lambda w: pl.BlockSpec((tm, w), lambda i: (i, 0))
    whole = lambda a: pl.BlockSpec(a.shape, lambda i: (0, 0))
    classes = lambda a, d: a.reshape(t_all // seq * d, seq // d, a.shape[-1])
    return pl.pallas_call(
        body, name="inproj_bwd", grid=(t_all // tm,),
        in_specs=[pl.BlockSpec((WIDTH, tm), lambda i: (i // nts, i % nts)), tok(WIDTH), tok(WIDTH)]
        + [tok(WIDTH)] * 3 + [_class_spec(4, WIDTH, nts)] * 3 + [_class_spec(16, WIDTH, nts)] * 3
        + [pl.BlockSpec((16, tm), lambda i: (0, i)), tok(1), whole(wqkv), whole(wf16),
           pl.BlockSpec((1, LANES), lambda i: (0, 0)), whole(perms[2]), whole(perms[3]), tok(D_MODEL), tok(D_MODEL),
           pl.BlockSpec((1, 6, D_MODEL), lambda i: (i // nts, 0, 0))],
        out_specs=[tok(D_MODEL), tok(6 * WIDTH), pl.BlockSpec((16, D_MODEL), lambda i: (0, 0))],
        out_shape=[jax.ShapeDtypeStruct((t_all, D_MODEL), F32), jax.ShapeDtypeStruct((t_all, 6 * WIDTH), BF16),
                   jax.ShapeDtypeStruct((16, D_MODEL), F32)],
        compiler_params=_params(),
    )(dqt, dka, dva, *dil1, *[classes(a, 4) for a in dil4], *[classes(a, 16) for a in dil16], dfa16, pos, wqkv, wf16,
      freq, perms[2], perms[3], dr1, x, ada3)


FFN_TM = 1024
FFN_TN = 256
HALO = 8


FFN_CHUNK = 256


def _conv_params(cw_ref, cb_ref, n, tn):
    a = pl.ds(pl.multiple_of(n * tn, tn), tn)
    g = pl.ds(pl.multiple_of(D_FF + n * tn, tn), tn)
    return cw_ref[:, a], cw_ref[:, g], cb_ref[:, a], cb_ref[:, g]


def _conv(cat_ref, w_ref, b_ref, start, rows, halo=HALO):
    return (b_ref[...] + w_ref[0:1, :] * cat_ref[pl.ds(start + halo - 2, rows), :]
            + w_ref[1:2, :] * cat_ref[pl.ds(start + halo - 1, rows), :]
            + w_ref[2:3, :] * cat_ref[pl.ds(start + halo, rows), :])


def _ffn_up_gate(h2, w_up, conv_w, conv_b, seq):
    t_all = h2.shape[0]
    tm, tn = FFN_TM, FFN_TN
    nc = D_FF // tn
    nts = seq // tm
    pre = 16

    def body(h_ref, hp_ref, wua_ref, wug_ref, cw_ref, cb_ref, ua_ref, ug_ref, o_ref, ca_ref, cg_ref):
        first = (pl.program_id(1) % nts) == 0
        wa_ref, wg_ref, ba_ref, bg_ref = _conv_params(cw_ref, cb_ref, pl.program_id(0), tn)
        hcat = jnp.concatenate([hp_ref[...], h_ref[...]], axis=0)
        zero = jnp.zeros((pre, tn), F32)
        for w_ref, cat, u_ref in ((wua_ref, ca_ref, ua_ref), (wug_ref, cg_ref, ug_ref)):
            ub = _nn(hcat, w_ref[...]).astype(BF16)
            ue = ub.astype(F32)
            cat[0:pre, :] = jnp.where(first, zero, ue[0:pre])
            cat[pre:, :] = ue[pre:]
            u_ref[...] = ub[pre:]
        for c0 in range(0, tm, FFN_CHUNK):
            ya = _conv(ca_ref, wa_ref, ba_ref, c0, FFN_CHUNK, pre)
            yg = _conv(cg_ref, wg_ref, bg_ref, c0, FFN_CHUNK, pre)
            o_ref[c0:c0 + FFN_CHUNK, :] = (yg * jax.nn.sigmoid(yg) * ya).astype(BF16)

    wcol = lambda off: pl.BlockSpec((D_MODEL, tn), lambda n, t: (0, n + off))
    tile = pl.BlockSpec((tm, tn), lambda n, t: (t, n))
    return pl.pallas_call(
        body, name="ffn_up_gate", grid=(nc, t_all // tm),
        in_specs=[pl.BlockSpec((tm, D_MODEL), lambda n, t: (t, 0)),
                  pl.BlockSpec((pre, D_MODEL), lambda n, t: (jnp.maximum(t * (tm // pre) - 1, 0), 0)),
                  wcol(0), wcol(nc), pl.BlockSpec(conv_w.shape, lambda n, t: (0, 0)),
                  pl.BlockSpec(conv_b.shape, lambda n, t: (0, 0))],
        out_specs=[tile, tile, tile],
        out_shape=[jax.ShapeDtypeStruct((t_all, D_FF), BF16)] * 3,
        scratch_shapes=[pltpu.VMEM((tm + pre, tn), F32)] * 2, compiler_params=_params(),
    )(h2, h2, w_up, w_up, conv_w, conv_b)


def _ffn_gate_bwd(u_a, u_g, dfi, conv_w, conv_b, h2t, seq):
    t_all = u_a.shape[0]
    tm, tn = FFN_TM, FFN_TN
    nc = D_FF // tn
    nts = seq // tm

    def body(ua_ref, uap_ref, uan_ref, ug_ref, ugp_ref, ugn_ref, df_ref, dfn_ref, cw_ref, cb_ref, h_ref,
             dua_ref, dug_ref, acca_ref, accg_ref, dwa_ref, dwg_ref, ca_ref, cg_ref, ya_ref, yg_ref, dwa_sc, dwg_sc,
             out_sems):
        t = pl.program_id(0)
        n = pl.program_id(1)
        cols = pl.ds(pl.multiple_of(n * tn, tn), tn)
        first = (t % nts) == 0
        last = (t % nts) == nts - 1

        @pl.when((t == 0) & (n == 0))
        def _():
            acca_ref[...] = jnp.zeros_like(acca_ref)
            accg_ref[...] = jnp.zeros_like(accg_ref)
            dwa_sc[...] = jnp.zeros_like(dwa_sc)
            dwg_sc[...] = jnp.zeros_like(dwg_sc)
        wa_ref, wg_ref, ba_ref, bg_ref = _conv_params(cw_ref, cb_ref, n, tn)
        zero = jnp.zeros((HALO, tn), F32)
        for cat, cur, prv, nxt in ((ca_ref, ua_ref, uap_ref, uan_ref), (cg_ref, ug_ref, ugp_ref, ugn_ref)):
            cat[0:HALO, :] = jnp.where(first, zero, prv[...].astype(F32)[HALO:])
            cat[HALO:HALO + tm, :] = cur[...].astype(F32)
            cat[HALO + tm:, :] = nxt[...].astype(F32)[:HALO]
        ch = FFN_CHUNK
        sums = [[jnp.zeros((1, tn), F32) for _ in range(4)] for _ in range(2)]
        for ci, c0 in enumerate(range(0, tm, ch)):
            ya = _conv(ca_ref, wa_ref, ba_ref, c0, ch + HALO)
            yg = _conv(cg_ref, wg_ref, bg_ref, c0, ch + HALO)
            if c0 + ch < tm:
                beyond = df_ref[c0 + ch:c0 + ch + 16, :].astype(F32)[:HALO]
            else:
                beyond = jnp.where(last, 0.0, dfn_ref[...].astype(F32)[:HALO])
            dfe = jnp.concatenate([df_ref[c0:c0 + ch, :].astype(F32), beyond], axis=0)
            sg = jax.nn.sigmoid(yg)
            ya_ref[ci] = dfe * (yg * sg)
            yg_ref[ci] = dfe * ya * (sg * (1.0 + yg * (1.0 - sg)))
            for half, (dy, cat, w_ref, du_ref) in enumerate(((ya_ref, ca_ref, wa_ref, dua_ref),
                                                             (yg_ref, cg_ref, wg_ref, dug_ref))):
                d0 = dy[ci, 0:ch, :]
                du = (w_ref[2:3, :] * d0 + w_ref[1:2, :] * dy[ci, pl.ds(1, ch), :]
                      + w_ref[0:1, :] * dy[ci, pl.ds(2, ch), :])
                du_ref[c0:c0 + ch, :] = du.astype(BF16)
                for k in range(3):
                    sums[half][k] += jnp.sum(d0 * cat[pl.ds(c0 + HALO - 2 + k, ch), :], axis=0, keepdims=True)
                sums[half][3] += jnp.sum(d0, axis=0, keepdims=True)
        for half, acc in enumerate((acca_ref, accg_ref)):
            for k in range(4):
                acc[k:k + 1, cols] += sums[half][k]
        ht = h_ref[0]
        dwa_sc[:, cols] += _nn(ht, dua_ref[...])
        dwg_sc[:, cols] += _nn(ht, dug_ref[...])

        @pl.when((t == t_all // tm - 1) & (n == nc - 1))
        def _():
            copies = [pltpu.make_async_copy(dwa_sc, dwa_ref, out_sems.at[0]),
                      pltpu.make_async_copy(dwg_sc, dwg_ref, out_sems.at[1])]
            for cp in copies:
                cp.start()
            for cp in copies:
                cp.wait()

    nrow = t_all // 16
    cur = pl.BlockSpec((tm, tn), lambda t, n: (t, n))
    prev = pl.BlockSpec((16, tn), lambda t, n: (jnp.maximum(t * (tm // 16) - 1, 0), n))
    nxt = pl.BlockSpec((16, tn), lambda t, n: (jnp.minimum((t + 1) * (tm // 16), nrow - 1), n))
    acc = pl.BlockSpec((8, D_FF), lambda t, n: (0, 0))
    return pl.pallas_call(
        body, name="ffn_gate_bwd", grid=(t_all // tm, nc),
        in_specs=[cur, prev, nxt, cur, prev, nxt, cur, nxt, pl.BlockSpec(conv_w.shape, lambda t, n: (0, 0)),
                  pl.BlockSpec(conv_b.shape, lambda t, n: (0, 0)),
                  pl.BlockSpec((1, D_MODEL, tm), lambda t, n: (t, 0, 0))],
        out_specs=[cur, cur, acc, acc, ANY, ANY],
        out_shape=[jax.ShapeDtypeStruct((t_all, D_FF), BF16), jax.ShapeDtypeStruct((t_all, D_FF), BF16),
                   jax.ShapeDtypeStruct((8, D_FF), F32), jax.ShapeDtypeStruct((8, D_FF), F32),
                   jax.ShapeDtypeStruct((D_MODEL, D_FF), F32), jax.ShapeDtypeStruct((D_MODEL, D_FF), F32)],
        scratch_shapes=[pltpu.VMEM((tm + 2 * HALO, tn), F32)] * 2
        + [pltpu.VMEM((tm // FFN_CHUNK, FFN_CHUNK + HALO, tn), F32)] * 2
        + [pltpu.VMEM((D_MODEL, D_FF), F32)] * 2 + [pltpu.SemaphoreType.DMA((2,))],
        compiler_params=_params(),
    )(u_a, u_a, u_a, u_g, u_g, u_g, dfi, dfi, conv_w, conv_b, h2t)


def _ffn_down(ffn_in, w_down, xh1, ln1_g, ln1_b, ada3, ln2_g, ln2_b, target, seq):
    t_all = xh1.shape[0]
    tm = 512
    nts = seq // tm

    def body(f_ref, w_ref, xh_ref, g1_ref, b1_ref, ada_ref, g2_ref, b2_ref, tg_ref, dr2_ref, acc_ref):
        i = pl.program_id(0)

        @pl.when(i == 0)
        def _():
            acc_ref[...] = jnp.zeros_like(acc_ref)
        ffn = _nn(f_ref[...], w_ref[...])
        x1 = xh_ref[...] * g1_ref[...] + b1_ref[...]
        r2 = ALPHA * x1 + ada_ref[0, 5:6, :] * ffn
        d = r2 - jnp.mean(r2, axis=1, keepdims=True)
        rstd = lax.rsqrt(jnp.mean(d * d, axis=1, keepdims=True) + LN_EPS)
        xh2 = d * rstd
        diff = xh2 * g2_ref[...] + b2_ref[...] - tg_ref[...]
        dy = diff * (1.0 / D_MODEL)
        dr2 = _layer_norm_bwd(dy * g2_ref[...], xh2, rstd)
        dr2_ref[...] = dr2
        acc_ref[0:1, :] += jnp.sum(dy * xh2, axis=0, keepdims=True)
        acc_ref[1:2, :] += jnp.sum(dy, axis=0, keepdims=True)
        acc_ref[2:3, :] += jnp.sum(diff * diff, axis=0, keepdims=True) * (0.5 / D_MODEL)
        acc_ref[pl.ds(8 + i // nts, 1), :] += jnp.sum(dr2 * ffn, axis=0, keepdims=True)

    tok = lambda w: pl.BlockSpec((tm, w), lambda i: (i, 0))
    vec = pl.BlockSpec((1, D_MODEL), lambda i: (0, 0))
    return pl.pallas_call(
        body, name="ffn_down", grid=(t_all // tm,),
        in_specs=[tok(D_FF), pl.BlockSpec(w_down.shape, lambda i: (0, 0)), tok(D_MODEL), vec, vec,
                  pl.BlockSpec((1, 6, D_MODEL), lambda i: (i // nts, 0, 0)), vec, vec, tok(D_MODEL)],
        out_specs=[tok(D_MODEL), pl.BlockSpec((16, D_MODEL), lambda i: (0, 0))],
        out_shape=[jax.ShapeDtypeStruct((t_all, D_MODEL), F32), jax.ShapeDtypeStruct((16, D_MODEL), F32)],
        compiler_params=_params(),
    )(ffn_in, w_down, xh1, ln1_g, ln1_b, ada3, ln2_g, ln2_b, target)


def _ffn_down_bwd(dr2, ada3, w_down, seq):
    t_all = dr2.shape[0]
    tm = 512
    nts = seq // tm

    def body(d_ref, ada_ref, w_ref, dffn_ref, dfi_ref):
        dffn = (d_ref[...] * ada_ref[0, 5:6, :]).astype(BF16)
        dffn_ref[...] = dffn
        dfi_ref[...] = _nt(dffn, w_ref[...]).astype(BF16)

    tok = lambda w: pl.BlockSpec((tm, w), lambda i: (i, 0))
    return pl.pallas_call(
        body, name="ffn_down_bwd", grid=(t_all // tm,),
        in_specs=[tok(D_MODEL), pl.BlockSpec((1, 6, D_MODEL), lambda i: (i // nts, 0, 0)),
                  pl.BlockSpec(w_down.shape, lambda i: (0, 0))],
        out_specs=[tok(D_MODEL), tok(D_FF)],
        out_shape=[jax.ShapeDtypeStruct((t_all, D_MODEL), BF16), jax.ShapeDtypeStruct((t_all, D_FF), BF16)],
        compiler_params=_params(),
    )(dr2, ada3, w_down)


def _ffn_up_bwd(du_a, du_g, w_up, dr2, xh1, rs1, mix, ada3, ln1_g, ln1_b, seq):
    t_all = dr2.shape[0]
    tm = 512
    nts = seq // tm

    def body(da_ref, dg_ref, w_ref, dr2_ref, xh_ref, rs_ref, mix_ref, ada_ref, g_ref, b_ref, dr1_ref, dmix_ref,
             acc_ref):
        i = pl.program_id(0)

        @pl.when(i == 0)
        def _():
            acc_ref[...] = jnp.zeros_like(acc_ref)
        dh2 = _nt(da_ref[...], w_ref[:, :D_FF]) + _nt(dg_ref[...], w_ref[:, D_FF:])
        xh = xh_ref[...]
        x1 = xh * g_ref[...] + b_ref[...]
        dx1 = ALPHA * dr2_ref[...] + dh2 * (1.0 + ada_ref[0, 4:5, :])
        dr1 = _layer_norm_bwd(dx1 * g_ref[...], xh, rs_ref[:, 0:1])
        dr1_ref[...] = dr1
        dmix_ref[...] = (dr1 * ada_ref[0, 2:3, :]).astype(BF16)
        b = i // nts
        acc_ref[0:1, :] += jnp.sum(dx1 * xh, axis=0, keepdims=True)
        acc_ref[1:2, :] += jnp.sum(dx1, axis=0, keepdims=True)
        acc_ref[pl.ds(8 + b, 1), :] += jnp.sum(dh2 * x1, axis=0, keepdims=True)
        acc_ref[pl.ds(16 + b, 1), :] += jnp.sum(dh2, axis=0, keepdims=True)
        acc_ref[pl.ds(24 + b, 1), :] += jnp.sum(dr1 * mix_ref[...].astype(F32), axis=0, keepdims=True)

    tok = lambda w: pl.BlockSpec((tm, w), lambda i: (i, 0))
    vec = pl.BlockSpec((1, D_MODEL), lambda i: (0, 0))
    return pl.pallas_call(
        body, name="ffn_up_bwd", grid=(t_all // tm,),
        in_specs=[tok(D_FF), tok(D_FF), pl.BlockSpec(w_up.shape, lambda i: (0, 0)), tok(D_MODEL), tok(D_MODEL),
                  tok(LANES), tok(D_MODEL), pl.BlockSpec((1, 6, D_MODEL), lambda i: (i // nts, 0, 0)), vec, vec],
        out_specs=[tok(D_MODEL), tok(D_MODEL), pl.BlockSpec((32, D_MODEL), lambda i: (0, 0))],
        out_shape=[jax.ShapeDtypeStruct((t_all, D_MODEL), F32), jax.ShapeDtypeStruct((t_all, D_MODEL), BF16),
                   jax.ShapeDtypeStruct((32, D_MODEL), F32)],
        compiler_params=_params(),
    )(du_a, du_g, w_up, dr2, xh1, rs1, mix, ada3, ln1_g, ln1_b)


def _rows(a):
    return a[:, :N_HEADS].T


def _rope_freq():
    f = np.float32(ROPE_THETA) ** (-np.arange(0, ROPE_DIMS, 2, dtype=np.float32) / np.float32(ROPE_DIMS))
    return jnp.asarray(np.tile(f.astype(np.float32), LANES // (ROPE_DIMS // 2))[None, :])


def _local_step(x, positions, target, ada3, w_in, b_fgate, gn_a, gn_b, ln1_g, ln1_b, conv_b, ln2_g, ln2_b,
                late_shards):
    nbat, seq, _ = x.shape
    t_all = nbat * seq
    xf = x.reshape(t_all, D_MODEL)
    tg = target.reshape(t_all, D_MODEL)
    pos = positions.reshape(t_all, 1)
    freq = _rope_freq()

    wqkv = jnp.concatenate([w_in[:, :3 * WIDTH], w_in[:, 3 * WIDTH + N_HEADS:]], axis=1)
    wf16 = jnp.zeros((16, D_MODEL), BF16).at[:N_HEADS].set(w_in[:, 3 * WIDTH:3 * WIDTH + N_HEADS].T)
    bf = b_fgate.reshape(N_HEADS, 1)

    perms = [_perm_matrix(TOK_TM, d, tr) for tr in (False, True) for d in DILATIONS[1:]]
    h1, za, zb1, zb4, zb16, vt, fa_t = _inproj(xf, ada3, pos, wqkv, wf16, freq, perms, seq)
    zbs = [zb1, zb4.reshape(t_all, 3 * WIDTH), zb16.reshape(t_all, 3 * WIDTH)]
    f_row = _fgate_fwd(fa_t, bf, seq)
    f_col = jnp.zeros((t_all, LANES), F32).at[:, :N_HEADS].set(f_row.T * LOG2E)
    oa, lse_row_a, gathered = _fox_fwd(za, vt, f_col, seq, [late_shards[n] for n in LATE])
    w_out, w_up, conv_w, w_down = (_full_from_gathered(n, g) for n, g in zip(LATE, gathered))
    o3, l3 = zip(*[_dil_fwd(zb, seq, d) for zb, d in zip(zbs, DILATIONS)])
    ob, lse_b, lse_b4, lse_b16, merged, mix, xh1, rs1, h2, h2t = _mix_out(oa, o3, l3, gn_a, gn_b, w_out, xf, ada3, ln1_g,
                                                                      ln1_b, perms, seq)
    u_a, u_g, ffn_in = _ffn_up_gate(h2, w_up, conv_w, conv_b, seq)
    dr2, acc2 = _ffn_down(ffn_in, w_down, xh1, ln1_g, ln1_b, ada3, ln2_g, ln2_b, tg, seq)

    dffn, dfi = _ffn_down_bwd(dr2, ada3, w_down, seq)
    d_w_down = _matmul_tn(dffn, ffn_in, 512, 512, "dw_down").T
    du_a, du_g, acc_ca, acc_cg, dw_up_a, dw_up_g = _ffn_gate_bwd(u_a, u_g, dfi, conv_w, conv_b, h2t, seq)
    dr1, dmix, acc1 = _ffn_up_bwd(du_a, du_g, w_up, dr2, xh1, rs1, mix, ada3, ln1_g, ln1_b, seq)
    d_w_up = jnp.concatenate([dw_up_a, dw_up_g], axis=1)

    doa, dob, dob4, dob16, dl_a, dl_b, dl_b4, dl_b16, acc_gn = _mix_out_bwd(dmix, w_out, oa, ob, gn_a, gn_b, perms, seq)
    d_w_out = _matmul_tn(merged, dmix, 512, 512, "dw_out")
    late_grads = dict(w_out=d_w_out, w_up=d_w_up, conv_w=jnp.concatenate([acc_ca[0:3], acc_cg[0:3]], axis=1),
                      w_down=d_w_down)
    dka, dva, df_k, dqt, df_q, late_parts = _fox_bwd(za, doa, f_col, lse_row_a, _rows(dl_a), seq,
                                                     [_payload(n, _dest_major(n, late_grads[n])) for n in LATE])
    dfa_t, dbf = _fgate_bwd(_rows(df_k) + df_q, fa_t, bf, seq)
    flat = lambda a: a.reshape(t_all, a.shape[-1])
    dil = []
    for zb, d, do, lse, dl in zip(zbs, DILATIONS, (dob, flat(dob4), flat(dob16)),
                                  (lse_b, flat(lse_b4), flat(lse_b16)), (dl_b, flat(dl_b4), flat(dl_b16))):
        dil.append(_dil_bwd(zb, do, lse, dl, seq, d))
    dfa16 = jnp.zeros((16, t_all), BF16).at[:N_HEADS].set(dfa_t.astype(BF16))
    grad_x, dz, acc0 = _inproj_bwd(dqt, dka, dva, dil[0], dil[1], dil[2], dfa16, pos, wqkv, wf16, freq, perms, dr1, xf,
                                   ada3, seq)
    d_wqkv = _matmul_tn(h1, dz, 512, 512, "dw_in")
    d_wf = _matmul_rows(dfa16, h1, 512, "dw_fgate")[:N_HEADS].T
    d_w_in = jnp.concatenate([d_wqkv[:, :3 * WIDTH], d_wf, d_wqkv[:, 3 * WIDTH:]], axis=1)

    dada = jnp.concatenate([acc0[8:8 + nbat], acc0[:nbat], acc1[24:24 + nbat], acc1[16:16 + nbat], acc1[8:8 + nbat],
                            acc2[8:8 + nbat]], axis=1)

    grads = dict(
        dada=dada, b_ada=jnp.sum(dada, axis=0, keepdims=True), w_in=d_w_in, b_fgate=dbf[:, 0][None, :],
        gn_a=acc_gn[0:1, :WIDTH], gn_b=acc_gn[0:1, WIDTH:], ln1_g=acc1[0:1], ln1_b=acc1[1:2],
        conv_b=jnp.concatenate([acc_ca[3:4], acc_cg[3:4]], axis=1), ln2_g=acc2[0:1], ln2_b=acc2[1:2])
    return acc2[2:3], grad_x.reshape(x.shape), grads, dict(zip(LATE, late_parts))


LATE = ("w_out", "w_up", "conv_w", "w_down")
BIG = ("w_ada", "w_in") + LATE
COLUMN_SHARDED = ("w_ada", "w_in", "w_up", "conv_w")


def _payload(name, a):
    return a if name == "conv_w" else a.astype(BF16)
SMALL = ("b_ada", "b_fgate", "gn_a", "gn_b", "ln1_g", "ln1_b", "conv_b", "ln2_g", "ln2_b")
ADAM_ROWS = dict(w_ada=256, w_in=256, w_out=128, w_up=256, conv_w=3, w_down=176)
SMALL_ROWS = 24


def _full_from_gathered(name, g):
    if name in COLUMN_SHARDED:
        return g.transpose(1, 0, 2).reshape(g.shape[1], N_DEV * g.shape[2])
    return g.reshape(N_DEV * g.shape[1], g.shape[2])


def _dest_major(name, full):
    if name in COLUMN_SHARDED:
        r, cfull = full.shape
        return full.reshape(r, N_DEV, cfull // N_DEV).transpose(1, 0, 2)
    return full.reshape(N_DEV, full.shape[0] // N_DEV, full.shape[1])


def _pack_small(vals, extra=None):
    parts = [vals[n].reshape(-1) for n in SMALL]
    if extra is not None:
        parts.append(extra.reshape(-1))
    flat = jnp.concatenate(parts)
    return jnp.pad(flat, (0, SMALL_ROWS * D_MODEL - flat.shape[0])).reshape(SMALL_ROWS, D_MODEL)


def _unpack_small(packed, like):
    flat = packed.reshape(-1)
    out, off = {}, 0
    for n in SMALL:
        size = like[n].size
        out[n] = flat[off:off + size].reshape(like[n].shape)
        off += size
    return out, flat[off:off + D_MODEL]


def kernel(x, c, positions, w_ada, b_ada, w_in, b_fgate, gn_a, gn_b, w_out, ln1_g, ln1_b, w_up, conv_w, conv_b, w_down, ln2_g, ln2_b, loss_target, m_w_ada, m_b_ada, m_w_in, m_b_fgate, m_gn_a, m_gn_b, m_w_out, m_ln1_g, m_ln1_b, m_w_up, m_conv_w, m_conv_b, m_w_down, m_ln2_g, m_ln2_b, v_w_ada, v_b_ada, v_w_in, v_b_fgate, v_gn_a, v_gn_b, v_w_out, v_ln1_g, v_ln1_b, v_w_up, v_conv_w, v_conv_b, v_w_down, v_ln2_g, v_ln2_b):
    w = dict(w_ada=w_ada[0], b_ada=b_ada, w_in=w_in[0], b_fgate=b_fgate, gn_a=gn_a, gn_b=gn_b, w_out=w_out[0],
             ln1_g=ln1_g, ln1_b=ln1_b, w_up=w_up[0], conv_w=conv_w[0], conv_b=conv_b, w_down=w_down[0], ln2_g=ln2_g,
             ln2_b=ln2_b)
    m = dict(w_ada=m_w_ada[0], b_ada=m_b_ada, w_in=m_w_in[0], b_fgate=m_b_fgate, gn_a=m_gn_a, gn_b=m_gn_b,
             w_out=m_w_out[0], ln1_g=m_ln1_g, ln1_b=m_ln1_b, w_up=m_w_up[0], conv_w=m_conv_w[0], conv_b=m_conv_b,
             w_down=m_w_down[0], ln2_g=m_ln2_g, ln2_b=m_ln2_b)
    v = dict(w_ada=v_w_ada[0], b_ada=v_b_ada, w_in=v_w_in[0], b_fgate=v_b_fgate, gn_a=v_gn_a, gn_b=v_gn_b,
             w_out=v_w_out[0], ln1_g=v_ln1_g, ln1_b=v_ln1_b, w_up=v_w_up[0], conv_w=v_conv_w[0], conv_b=v_conv_b,
             w_down=v_w_down[0], ln2_g=v_ln2_g, ln2_b=v_ln2_b)

    nbat = x.shape[0]
    me = 4 * lax.axis_index("x") + 2 * lax.axis_index("y") + lax.axis_index("c")
    ada_cols = w["w_ada"].shape[1]

    c_all, w_in_all = _gather_two_level([c, _payload("w_in", w["w_in"])], "weight_gather")
    c_all = c_all.reshape(N_DEV * nbat, D_MODEL)
    ada_mine = _ada_fwd(c_all, w["w_ada"], lax.dynamic_slice(b_ada, (0, me * ada_cols), (1, ada_cols)))
    (ada_parts,) = _exchange([ada_mine.reshape(N_DEV, nbat, ada_cols)], [False], "ada_exchange")
    ada3 = ada_parts.transpose(1, 0, 2).reshape(nbat, 6, D_MODEL)

    loss_lanes, grad_x, g_local, parts = _local_step(
        x, positions, loss_target, ada3, _full_from_gathered("w_in", w_in_all), b_fgate, gn_a, gn_b, ln1_g, ln1_b,
        conv_b, ln2_g, ln2_b, {n: _payload(n, w[n]) for n in LATE})

    parts["w_in"], dada_all, small_all = _exchange(
        [_payload("w_in", _dest_major("w_in", g_local["w_in"])), g_local["dada"], _pack_small(g_local, loss_lanes)],
        [False, True, True], "grad_exchange")
    dada_cols = lax.dynamic_slice(dada_all.reshape(N_DEV * nbat, 6 * D_MODEL), (0, me * ada_cols),
                                  (N_DEV * nbat, ada_cols))
    parts["w_ada"] = _ada_bwd(c_all, dada_cols)[None]

    grad, delta, new_m, new_v = {}, {}, {}, {}
    for n in BIG:
        grad[n], delta[n], new_m[n], new_v[n] = (
            a[None] for a in _adamw(parts[n], w[n], m[n], v[n], ADAM_ROWS[n], "adamw_" + n))
    packed = _adamw(small_all, _pack_small(w), _pack_small(m), _pack_small(v), SMALL_ROWS, "adamw_small")
    for dst, pk in zip((grad, delta, new_m, new_v), packed):
        vals, lanes = _unpack_small(pk, w)
        dst.update(vals)
        if dst is grad:
            loss = jnp.sum(lanes)

    order = ("w_ada", "b_ada", "w_in", "b_fgate", "gn_a", "gn_b", "w_out", "ln1_g", "ln1_b", "w_up", "conv_w", "conv_b",
             "w_down", "ln2_g", "ln2_b")
    return (loss, grad_x, *[grad[n] for n in order], *[delta[n] for n in order], *[new_m[n] for n in order],
            *[new_v[n] for n in order])
```

```python
import functools

import numpy as np
import jax
import jax.numpy as jnp
from jax import lax
from jax.experimental import pallas as pl
from jax.experimental.pallas import tpu as pltpu

F32, BF16 = jnp.float32, jnp.bfloat16
MESH = pl.DeviceIdType.MESH
ANY = pl.BlockSpec(memory_space=pl.ANY)

D_MODEL = 1024
N_HEADS = 8
HEAD_DIM = 64
WIDTH = 512
D_FF = 2816
N_DEV = 8
ROPE_DIMS = 16
ROPE_THETA = 500000.0
ALPHA = 2.0 ** 0.25
LN_EPS = 1e-5
RMS_EPS = 1e-6
NEG = -1e30
Q_SCALE = 0.125
LOG2E = 1.4426950408889634
BLK = 128
LANES = 128
VMEM_LIMIT_BYTES = 56 * 1024 * 1024

ADAM_LR, ADAM_B1, ADAM_B2, ADAM_EPS, ADAM_WD, ADAM_STEP = 0.001, 0.9, 0.999, 1e-08, 0.01, 10


def _params(vmem=VMEM_LIMIT_BYTES):
    return pltpu.CompilerParams(vmem_limit_bytes=vmem)


def _nn(a, b):
    return jnp.dot(a, b, preferred_element_type=F32)


def _nt(a, b):
    return lax.dot_general(a, b, (((1,), (1,)), ((), ())), preferred_element_type=F32)


def _tn(a, b):
    return lax.dot_general(a, b, (((0,), (0,)), ((), ())), preferred_element_type=F32)


def _head_mats():
    r = lax.broadcasted_iota(jnp.int32, (LANES, WIDTH), 0)
    c = lax.broadcasted_iota(jnp.int32, (LANES, WIDTH), 1)
    e = ((c >> 6) == r).astype(BF16)
    r2 = lax.broadcasted_iota(jnp.int32, (WIDTH, LANES), 0)
    c2 = lax.broadcasted_iota(jnp.int32, (WIDTH, LANES), 1)
    et = ((r2 >> 6) == c2).astype(BF16)
    return e, et


def _split3(x):
    hi = x.astype(BF16)
    r = x - hi.astype(F32)
    mid = r.astype(BF16)
    return hi, mid, (r - mid.astype(F32)).astype(BF16)


def _hexp(w, e):
    return sum(_nn(part, e) for part in _split3(w)[:2])


def _hsum(x, et):
    return sum(_nn(part, et) for part in _split3(x)[:2])


def _perm_matrix(rows, d, transpose):
    i = np.arange(rows)
    j = (i % (rows // d)) * d + i // (rows // d)
    p = np.zeros((rows, rows), np.float32)
    p[i, j] = 1.0
    return jnp.asarray(p.T if transpose else p, BF16)


def _permute_f32(p, x):
    return sum(_nn(p, part) for part in _split3(x))


def _store_classes(ref, y, d):
    n = y.shape[0] // d
    for r in range(d):
        ref[r] = y[r * n:(r + 1) * n, :]


def _load_classes(ref, d):
    return jnp.concatenate([ref[r] for r in range(d)], axis=0)


def _rope_tabs(pos_ref, fr_ref, sign):
    ang = pos_ref[...].astype(F32) * fr_ref[...]
    lane = lax.broadcasted_iota(jnp.int32, ang.shape, 1) & (HEAD_DIM - 1)
    m1 = lane < ROPE_DIMS // 2
    m2 = (lane >= ROPE_DIMS // 2) & (lane < ROPE_DIMS)
    cos = jnp.cos(ang)
    sin = jnp.sin(ang) * sign
    return (jnp.where(m1 | m2, cos, 1.0), jnp.where(m1, -sin, 0.0), jnp.where(m2, sin, 0.0))


def _rope(z, tabs):
    c, s1, s2 = tabs
    parts = []
    for p in range(z.shape[1] // LANES):
        zp = z[:, LANES * p:LANES * (p + 1)]
        parts.append(zp * c + pltpu.roll(zp, LANES - 8, 1) * s1 + pltpu.roll(zp, 8, 1) * s2)
    return jnp.concatenate(parts, axis=1)


def _half_masks(rows):
    lane = lax.broadcasted_iota(jnp.int32, (rows, LANES), 1)
    lo = lane < HEAD_DIM
    return lo, jnp.logical_not(lo)


def _layer_norm_bwd(dxh, xh, rstd):
    m1 = jnp.mean(dxh, axis=1, keepdims=True)
    m2 = jnp.mean(dxh * xh, axis=1, keepdims=True)
    return rstd * (dxh - m1 - xh * m2)


def _coords():
    return lax.axis_index("x"), lax.axis_index("y"), lax.axis_index("c")


def _peer(x, y, c, k):
    return (1 - x if k & 4 else x, 1 - y if k & 2 else y, 1 - c if k & 1 else c)


def _comm_sems(n):
    return [pltpu.SemaphoreType.DMA((N_DEV - 1, n)), pltpu.SemaphoreType.DMA((N_DEV - 1, n)),
            pltpu.SemaphoreType.DMA((n,))]


def _comm_copies(ins, outs, to_all, sems):
    send_sems, recv_sems, local_sems = sems
    x, y, c = _coords()
    me = 4 * x + 2 * y + c
    copies = [pltpu.make_async_copy(ins[t] if to_all[t] else ins[t].at[me], outs[t].at[me], local_sems.at[t])
              for t in range(len(ins))]
    for k in range(1, N_DEV):
        px, py, pc = _peer(x, y, c, k)
        dest = 4 * px + 2 * py + pc
        for t in range(len(ins)):
            copies.append(pltpu.make_async_remote_copy(
                src_ref=ins[t] if to_all[t] else ins[t].at[dest], dst_ref=outs[t].at[me],
                send_sem=send_sems.at[k - 1, t], recv_sem=recv_sems.at[k - 1, t],
                device_id=(px, py, pc), device_id_type=MESH))
    return copies


def _comm_out_shapes(ins, to_all):
    return [jax.ShapeDtypeStruct(((N_DEV,) + a.shape) if ta else a.shape, a.dtype) for a, ta in zip(ins, to_all)]


def _exchange(ins, to_all, name):
    n = len(ins)

    def body(*refs):
        copies = _comm_copies(refs[:n], refs[n:2 * n], to_all, refs[2 * n:])
        for cp in copies:
            cp.start()
        for cp in copies:
            cp.wait()

    return pl.pallas_call(
        body, name=name, out_shape=_comm_out_shapes(ins, to_all), in_specs=[ANY] * n, out_specs=[ANY] * n,
        scratch_shapes=_comm_sems(n),
    )(*ins)


def _gather_two_level(ins, name):
    n = len(ins)

    def body(*refs):
        srcs, outs = refs[:n], refs[n:2 * n]
        send_sems, recv_sems, local_sems = refs[2 * n:]
        x, y, c = _coords()
        me = 4 * x + 2 * y + c
        sibling = (x, y, 1 - c)
        chips = [(1 - x, y), (x, 1 - y), (1 - x, 1 - y)]
        slot = lambda px, py, pc: 4 * px + 2 * py + pc

        def copy(k, t, block, to, own=False):
            return pltpu.make_async_remote_copy(
                src_ref=srcs[t] if own else outs[t].at[block], dst_ref=outs[t].at[block],
                send_sem=send_sems.at[k, t], recv_sem=recv_sems.at[k, t], device_id=to, device_id_type=MESH)

        local = [pltpu.make_async_copy(srcs[t], outs[t].at[me], local_sems.at[t]) for t in range(n)]
        first = [copy(0, t, me, sibling, own=True) for t in range(n)]
        first += [copy(1 + j, t, me, (*chip, c), own=True) for j, chip in enumerate(chips) for t in range(n)]
        for cp in local + first:
            cp.start()
        passed = []
        for j, chip in enumerate(chips):
            for t in range(n):
                copy(1 + j, t, slot(*chip, c), (x, y, c)).wait_recv()
                cp = copy(4 + j, t, slot(*chip, c), sibling)
                cp.start()
                passed.append(cp)
        for t in range(n):
            copy(0, t, slot(x, y, 1 - c), (x, y, c)).wait_recv()
            for j, chip in enumerate(chips):
                copy(4 + j, t, slot(*chip, 1 - c), (x, y, c)).wait_recv()
        for cp in first + passed:
            cp.wait_send()
        for cp in local:
            cp.wait()

    return pl.pallas_call(
        body, name=name, out_shape=_comm_out_shapes(ins, [True] * n), in_specs=[ANY] * n, out_specs=[ANY] * n,
        scratch_shapes=_comm_sems(n),
    )(*ins)


def _adamw(parts, w, m, v, rows, name):
    n_parts, r_all, cols = parts.shape
    c1 = 1.0 - ADAM_B1 ** ADAM_STEP
    c2 = 1.0 - ADAM_B2 ** ADAM_STEP

    def body(p_ref, w_ref, m_ref, v_ref, g_ref, d_ref, mo_ref, vo_ref):
        g = p_ref[0].astype(F32)
        for s in range(1, n_parts):
            g = g + p_ref[s].astype(F32)
        mn = ADAM_B1 * m_ref[...] + (1.0 - ADAM_B1) * g
        vn = ADAM_B2 * v_ref[...] + (1.0 - ADAM_B2) * (g * g)
        m_hat = mn / c1
        v_hat = vn / c2
        g_ref[...] = g
        d_ref[...] = -ADAM_LR * (m_hat / (jnp.sqrt(v_hat) + ADAM_EPS) + ADAM_WD * w_ref[...])
        mo_ref[...] = mn
        vo_ref[...] = vn

    spec = pl.BlockSpec((rows, cols), lambda i: (i, 0))
    return pl.pallas_call(
        body, name=name, grid=(r_all // rows,),
        in_specs=[pl.BlockSpec((n_parts, rows, cols), lambda i: (0, i, 0)), spec, spec, spec],
        out_specs=[spec] * 4, out_shape=[jax.ShapeDtypeStruct((r_all, cols), F32)] * 4,
        compiler_params=_params(),
    )(parts, w, m, v)


def _matmul_tn(a, b, chunk, tk, name):
    t_all, k1 = a.shape
    n = b.shape[1]

    def body(a_ref, b_ref, o_ref):
        @pl.when(pl.program_id(0) == 0)
        def _():
            o_ref[...] = jnp.zeros_like(o_ref)
        at = a_ref[...].astype(F32).T.astype(BF16)
        for j in range(0, n, chunk):
            cs = slice(j, min(j + chunk, n))
            o_ref[:, cs] += _nn(at, b_ref[:, cs])

    return pl.pallas_call(
        body, name=name, grid=(t_all // tk,),
        in_specs=[pl.BlockSpec((tk, k1), lambda t: (t, 0)), pl.BlockSpec((tk, n), lambda t: (t, 0))],
        out_specs=pl.BlockSpec((k1, n), lambda t: (0, 0)),
        out_shape=jax.ShapeDtypeStruct((k1, n), F32), compiler_params=_params(),
    )(a, b)


def _matmul_rows(a, b, tk, name):
    r, t_all = a.shape
    n = b.shape[1]

    def body(a_ref, b_ref, o_ref):
        @pl.when(pl.program_id(0) == 0)
        def _():
            o_ref[...] = jnp.zeros_like(o_ref)
        o_ref[...] += _nn(a_ref[...], b_ref[...])

    return pl.pallas_call(
        body, name=name, grid=(t_all // tk,),
        in_specs=[pl.BlockSpec((r, tk), lambda t: (0, t)), pl.BlockSpec((tk, n), lambda t: (t, 0))],
        out_specs=pl.BlockSpec((r, n), lambda t: (0, 0)),
        out_shape=jax.ShapeDtypeStruct((r, n), F32), compiler_params=_params(),
    )(a, b)


def _ada_fwd(c_all, w_ada, b_ada):
    whole = lambda a: pl.BlockSpec(a.shape, lambda j: (0, 0))

    def body(c_ref, w_ref, b_ref, o_ref):
        cv = c_ref[...]
        s = (cv * jax.nn.sigmoid(cv)).astype(BF16)
        o_ref[...] = _nn(s, w_ref[...].astype(BF16)) + b_ref[...]

    out = jax.ShapeDtypeStruct((c_all.shape[0], w_ada.shape[1]), F32)
    return pl.pallas_call(
        body, name="ada_fwd", grid=(1,), in_specs=[whole(c_all), whole(w_ada), whole(b_ada)], out_specs=whole(out),
        out_shape=out, compiler_params=_params(),
    )(c_all, w_ada, b_ada)


def _ada_bwd(c_all, dada):
    whole = lambda a: pl.BlockSpec(a.shape, lambda j: (0, 0))

    def body(c_ref, d_ref, o_ref):
        cv = c_ref[...]
        s = (cv * jax.nn.sigmoid(cv)).astype(BF16)
        o_ref[...] = _tn(s, d_ref[...].astype(BF16))

    out = jax.ShapeDtypeStruct((D_MODEL, dada.shape[1]), F32)
    return pl.pallas_call(
        body, name="ada_bwd", grid=(1,), in_specs=[whole(c_all), whole(dada)], out_specs=whole(out), out_shape=out,
        compiler_params=_params(),
    )(c_all, dada)


TOK_TM = 256
DILATIONS = (1, 4, 16)


def _class_spec(d, width, nts):
    return pl.BlockSpec((d, TOK_TM // d, width), lambda i: (i // nts, i % nts, 0))


def _class_shape(t_all, seq, d, width, dtype):
    return jax.ShapeDtypeStruct((t_all // seq * d, seq // d, width), dtype)


def _inproj(x, ada3, pos, wqkv, wf16, freq, perms, seq):
    t_all = x.shape[0]
    tm = TOK_TM
    nts = seq // tm

    def body(x_ref, ada_ref, pos_ref, w_ref, wf_ref, fr_ref, p4_ref, p16_ref, h1_ref, za_ref, zb_ref, zb4_ref,
             zb16_ref, vt_ref, fa_ref):
        h1 = (x_ref[...] * (1.0 + ada_ref[0, 1:2, :]) + ada_ref[0, 0:1, :]).astype(BF16)
        h1_ref[...] = h1
        tabs = _rope_tabs(pos_ref, fr_ref, 1.0)
        for n in range(6):
            z = _nn(h1, w_ref[:, n * WIDTH:(n + 1) * WIDTH])
            if n in (3, 4):
                z = _rope(z, tabs)
            if n in (0, 3):
                z = z * (Q_SCALE * LOG2E)
            if n == 2:
                vt_ref[...] = z.T.astype(BF16)
            dst = za_ref if n < 3 else zb_ref
            dst[:, (n % 3) * WIDTH:(n % 3 + 1) * WIDTH] = z.astype(BF16)
        fa_ref[...] = _nt(wf_ref[...], h1)[:N_HEADS]
        zb = zb_ref[...]
        _store_classes(zb4_ref, _nn(p4_ref[...], zb).astype(BF16), 4)
        _store_classes(zb16_ref, _nn(p16_ref[...], zb).astype(BF16), 16)

    tok = lambda w: pl.BlockSpec((tm, w), lambda i: (i, 0))
    whole = lambda a: pl.BlockSpec(a.shape, lambda i: (0, 0))
    return pl.pallas_call(
        body, name="inproj", grid=(t_all // tm,),
        in_specs=[tok(D_MODEL), pl.BlockSpec((1, 6, D_MODEL), lambda i: (i // nts, 0, 0)), tok(1), whole(wqkv),
                  whole(wf16), pl.BlockSpec((1, LANES), lambda i: (0, 0)), whole(perms[0]), whole(perms[1])],
        out_specs=[tok(D_MODEL), tok(3 * WIDTH), tok(3 * WIDTH), _class_spec(4, 3 * WIDTH, nts),
                   _class_spec(16, 3 * WIDTH, nts), pl.BlockSpec((WIDTH, tm), lambda i: (i // nts, i % nts)),
                   pl.BlockSpec((N_HEADS, tm), lambda i: (0, i))],
        out_shape=[jax.ShapeDtypeStruct((t_all, D_MODEL), BF16), jax.ShapeDtypeStruct((t_all, 3 * WIDTH), BF16),
                   jax.ShapeDtypeStruct((t_all, 3 * WIDTH), BF16), _class_shape(t_all, seq, 4, 3 * WIDTH, BF16),
                   _class_shape(t_all, seq, 16, 3 * WIDTH, BF16),
                   jax.ShapeDtypeStruct((t_all // seq * WIDTH, seq), BF16),
                   jax.ShapeDtypeStruct((N_HEADS, t_all), F32)],
        compiler_params=_params(),
    )(x, ada3, pos, wqkv, wf16, freq, perms[0], perms[1])


def _chunk_rows(a_t, seq):
    t_all = a_t.shape[1]
    return a_t.reshape(N_HEADS, t_all // seq, seq // LANES, LANES).transpose(1, 0, 2, 3).reshape(-1, LANES)


def _unchunk_rows(a, seq):
    nbat = a.shape[0] * LANES // (N_HEADS * seq)
    return a.reshape(nbat, N_HEADS, seq // LANES, LANES).transpose(1, 0, 2, 3).reshape(N_HEADS, nbat * seq)


def _chunk_carry(tot, nchunk, later):
    rows = tot.shape[0]
    r = lax.broadcasted_iota(jnp.int32, (rows, rows), 0)
    c = lax.broadcasted_iota(jnp.int32, (rows, rows), 1)
    sel = ((r // nchunk) == (c // nchunk)) & ((c > r) if later else (c < r))
    mat = sel.astype(BF16)
    return sum(_nn(mat, part) for part in _split3(jnp.broadcast_to(tot, (rows, LANES))))


def _fgate_fwd(fa_t, bf, seq):
    x = _chunk_rows(fa_t, seq)
    rows = x.shape[0]
    nchunk = seq // LANES
    bias = jnp.broadcast_to(bf.reshape(1, N_HEADS, 1), (rows // (N_HEADS * nchunk), N_HEADS, nchunk)).reshape(rows, 1)

    def body(x_ref, b_ref, f_ref):
        lane = lax.broadcasted_iota(jnp.int32, (rows, LANES), 1)
        xv = x_ref[...] + b_ref[...]
        lf = jnp.minimum(xv, 0.0) - jnp.log(1.0 + jnp.exp(-jnp.abs(xv)))
        for s in (1, 2, 4, 8, 16, 32, 64):
            lf = lf + jnp.where(lane >= s, pltpu.roll(lf, s, 1), 0.0)
        f_ref[...] = lf + _chunk_carry(lf[:, LANES - 1:LANES], nchunk, False)

    whole = lambda a: pl.BlockSpec(a.shape, lambda i: (0, 0))
    out = pl.pallas_call(
        body, name="fgate_fwd", grid=(1,), in_specs=[whole(x), whole(bias)], out_specs=whole(x),
        out_shape=jax.ShapeDtypeStruct(x.shape, F32), compiler_params=_params(),
    )(x, bias)
    return _unchunk_rows(out, seq)


def _fgate_bwd(df_t, fa_t, bf, seq):
    d_in = _chunk_rows(df_t, seq)
    x = _chunk_rows(fa_t, seq)
    rows = x.shape[0]
    nchunk = seq // LANES
    bias = jnp.broadcast_to(bf.reshape(1, N_HEADS, 1), (rows // (N_HEADS * nchunk), N_HEADS, nchunk)).reshape(rows, 1)

    def body(d_ref, x_ref, b_ref, o_ref, s_ref):
        lane = lax.broadcasted_iota(jnp.int32, (rows, LANES), 1)
        d = d_ref[...]
        for s in (1, 2, 4, 8, 16, 32, 64):
            d = d + jnp.where(lane < LANES - s, pltpu.roll(d, LANES - s, 1), 0.0)
        d = d + _chunk_carry(d[:, 0:1], nchunk, True)
        dfa = d * jax.nn.sigmoid(-(x_ref[...] + b_ref[...]))
        o_ref[...] = dfa
        g = lax.broadcasted_iota(jnp.int32, (2 * N_HEADS, rows), 0)
        r = lax.broadcasted_iota(jnp.int32, (2 * N_HEADS, rows), 1)
        group = (((r // nchunk) % N_HEADS) == g).astype(BF16)
        per_head = sum(_nn(group, part) for part in _split3(dfa))[:N_HEADS]
        s_ref[...] = jnp.broadcast_to(jnp.sum(per_head, axis=1, keepdims=True), (N_HEADS, LANES))

    whole = lambda a: pl.BlockSpec(a.shape, lambda i: (0, 0))
    dfa, sums = pl.pallas_call(
        body, name="fgate_bwd", grid=(1,), in_specs=[whole(d_in), whole(x), whole(bias)],
        out_specs=[whole(x), pl.BlockSpec((N_HEADS, LANES), lambda i: (0, 0))],
        out_shape=[jax.ShapeDtypeStruct(x.shape, F32), jax.ShapeDtypeStruct((N_HEADS, LANES), F32)],
        compiler_params=_params(),
    )(d_in, x, bias)
    return _unchunk_rows(dfa, seq), sums


FOX_T = 256


def _fox_prep(dst, src_ref, lo, hi):
    for p in range(4):
        v = src_ref[:, LANES * p:LANES * (p + 1)]
        dst[2 * p] = jnp.where(lo, v, jnp.zeros_like(v))
        dst[2 * p + 1] = jnp.where(hi, v, jnp.zeros_like(v))


def _fox_fwd(za, vt, f_col, seq, shards):
    t_all = za.shape[0]
    tq = FOX_T
    nq = seq // tq
    nbat = t_all // seq
    n = len(shards)
    to_all = [True] * n

    def body(*refs):
        q_ref, k_ref, vt_ref, fc_ref = refs[:4]
        o_ref, lse_ref = refs[4 + n:6 + n]
        qm_sc, m_sc, l_sc, acc_sc, a_sc, st_sc, pe_sc = refs[6 + 2 * n:13 + 2 * n]
        comm = (refs[4:4 + n], refs[6 + n:6 + 2 * n], to_all, refs[13 + 2 * n:])
        i = pl.program_id(1)

        @pl.when((pl.program_id(0) == 0) & (i == 0))
        def _():
            for cp in _comm_copies(*comm):
                cp.start()
        lo, hi = _half_masks(tq)
        r = lax.broadcasted_iota(jnp.int32, (tq, tq), 0)
        c = lax.broadcasted_iota(jnp.int32, (tq, tq), 1)
        tri = c >= r
        _fox_prep(qm_sc, q_ref, lo, hi)
        m_sc[...] = jnp.full(m_sc.shape, NEG, F32)
        l_sc[...] = jnp.zeros_like(l_sc)
        acc_sc[...] = jnp.zeros_like(acc_sc)

        def block(j, masked):
            sl = pl.ds(pl.multiple_of(j * tq, tq), tq)
            for p in range(4):
                kj = k_ref[sl, LANES * p:LANES * (p + 1)]
                for h in (2 * p, 2 * p + 1):
                    st = _nt(kj, qm_sc[h]) - fc_ref[sl, h:h + 1]
                    st_sc[h] = jnp.where(tri, st, NEG) if masked else st
            for h in range(N_HEADS):
                st = st_sc[h]
                m = m_sc[h:h + 1, :]
                mn = jnp.maximum(m, jnp.max(st, axis=0, keepdims=True))
                a = jnp.exp2(m - mn)
                pe = jnp.exp2(st - mn)
                m_sc[h:h + 1, :] = mn
                a_sc[h:h + 1, :] = a
                l_sc[h:h + 1, :] = a * l_sc[h:h + 1, :] + jnp.sum(pe, axis=0, keepdims=True)
                pe_sc[h] = pe.astype(BF16)
            for h in range(N_HEADS):
                acc_sc[h] = a_sc[h:h + 1, :] * acc_sc[h] + _nn(vt_ref[HEAD_DIM * h:HEAD_DIM * (h + 1), sl], pe_sc[h])

        def step(j, carry):
            block(j, False)
            return carry

        lax.fori_loop(0, i, step, 0)
        block(i, True)
        lse_ref[...] = m_sc[...] + jnp.log(l_sc[...]) * LOG2E
        for p in range(4):
            ot = jnp.concatenate([acc_sc[h] / l_sc[h:h + 1, :] for h in (2 * p, 2 * p + 1)], axis=0)
            o_ref[:, LANES * p:LANES * (p + 1)] = ot.T

        @pl.when((pl.program_id(0) == nbat - 1) & (i == nq - 1))
        def _():
            for cp in _comm_copies(*comm):
                cp.wait()

    res = pl.pallas_call(
        body, name="fox_fwd", grid=(nbat, nq),
        in_specs=[pl.BlockSpec((tq, WIDTH), lambda b, i: (b * nq + i, 0)),
                  pl.BlockSpec((seq, WIDTH), lambda b, i: (b, 1)), pl.BlockSpec((WIDTH, seq), lambda b, i: (b, 0)),
                  pl.BlockSpec((seq, LANES), lambda b, i: (b, 0))] + [ANY] * n,
        out_specs=[pl.BlockSpec((tq, WIDTH), lambda b, i: (b * nq + i, 0)),
                   pl.BlockSpec((N_HEADS, tq), lambda b, i: (0, b * nq + i))] + [ANY] * n,
        out_shape=[jax.ShapeDtypeStruct((t_all, WIDTH), F32), jax.ShapeDtypeStruct((N_HEADS, t_all), F32)]
        + _comm_out_shapes(shards, to_all),
        scratch_shapes=[pltpu.VMEM((N_HEADS, tq, LANES), BF16), pltpu.VMEM((N_HEADS, tq), F32),
                        pltpu.VMEM((N_HEADS, tq), F32), pltpu.VMEM((N_HEADS, HEAD_DIM, tq), F32),
                        pltpu.VMEM((N_HEADS, tq), F32), pltpu.VMEM((N_HEADS, tq, tq), F32),
                        pltpu.VMEM((N_HEADS, tq, tq), BF16)] + _comm_sems(n),
        compiler_params=_params(),
    )(za, za, vt, f_col, *shards)
    return res[0], res[1], res[2:]


def _fox_bwd(za, do, f_col, lse_row, dl_row, seq, grads):
    t_all = za.shape[0]
    tk = FOX_T
    nk = seq // tk
    nbat = t_all // seq
    n = len(grads)
    to_all = [False] * n

    def body(*refs):
        k_ref, v_ref, q_ref, do_ref, fc_ref, lr_ref, dr_ref = refs[:7]
        dk_ref, dv_ref, df_ref, dqt_ref, dfq_ref = refs[7 + n:12 + n]
        km_sc, vm_sc, fk_sc, dk_sc, dv_sc, cs_sc, kt_sc, st_sc, dp_sc, pt_sc, ds_sc = refs[12 + 2 * n:23 + 2 * n]
        comm = (refs[7:7 + n], refs[12 + n:12 + 2 * n], to_all, refs[23 + 2 * n:])
        j = pl.program_id(1)

        @pl.when(j == 0)
        def _():
            dqt_ref[...] = jnp.zeros_like(dqt_ref)
            dfq_ref[...] = jnp.zeros_like(dfq_ref)

        @pl.when((pl.program_id(0) == 0) & (j == 0))
        def _():
            for cp in _comm_copies(*comm):
                cp.start()
        lo, hi = _half_masks(tk)
        r = lax.broadcasted_iota(jnp.int32, (tk, tk), 0)
        c = lax.broadcasted_iota(jnp.int32, (tk, tk), 1)
        tri = c >= r
        _fox_prep(km_sc, k_ref, lo, hi)
        _fox_prep(vm_sc, v_ref, lo, hi)
        for h in range(N_HEADS):
            fk_sc[h] = jnp.broadcast_to(fc_ref[:, h:h + 1], (tk, tk))
        for p in range(4):
            kt_sc[p] = k_ref[:, LANES * p:LANES * (p + 1)].astype(F32).T.astype(BF16)
        dk_sc[...] = jnp.zeros_like(dk_sc)
        dv_sc[...] = jnp.zeros_like(dv_sc)
        cs_sc[...] = jnp.zeros_like(cs_sc)

        def block(i, masked):
            sl = pl.ds(pl.multiple_of(i * tk, tk), tk)
            for p in range(4):
                cs = slice(LANES * p, LANES * (p + 1))
                qi = q_ref[sl, cs]
                doi = do_ref[sl, cs]
                for h in (2 * p, 2 * p + 1):
                    st = _nt(km_sc[h], qi) - fk_sc[h] - lr_ref[h:h + 1, sl]
                    st_sc[h] = jnp.where(tri, st, NEG) if masked else st
                    dp_sc[h] = _nt(vm_sc[h], doi) - dr_ref[h:h + 1, sl]
            for h in range(N_HEADS):
                pt = jnp.exp2(st_sc[h])
                dst = pt * dp_sc[h]
                pt_sc[h] = pt.astype(BF16)
                ds_sc[h] = dst.astype(BF16)
                cs_sc[h] += dst[:, :LANES] + dst[:, LANES:]
                dfq_ref[h:h + 1, sl] += jnp.sum(dst, axis=0, keepdims=True)
            for p in range(4):
                cs = slice(LANES * p, LANES * (p + 1))
                qi = q_ref[sl, cs]
                doi = do_ref[sl, cs]
                for h in (2 * p, 2 * p + 1):
                    dv_sc[h] += _nn(pt_sc[h], doi)
                    dk_sc[h] += _nn(ds_sc[h], qi)
                    kt = kt_sc[p, HEAD_DIM * (h % 2):HEAD_DIM * (h % 2 + 1), :]
                    dqt_ref[HEAD_DIM * h:HEAD_DIM * (h + 1), sl] += _nn(kt, ds_sc[h])

        def step(i, carry):
            block(i, False)
            return carry

        block(j, True)
        lax.fori_loop(j + 1, nk, step, 0)
        df_ref[...] = jnp.zeros_like(df_ref)
        for p in range(4):
            cs = slice(LANES * p, LANES * (p + 1))
            dk_ref[:, cs] = (jnp.where(lo, dk_sc[2 * p], dk_sc[2 * p + 1]) * (1.0 / LOG2E)).astype(BF16)
            dv_ref[:, cs] = jnp.where(lo, dv_sc[2 * p], dv_sc[2 * p + 1]).astype(BF16)
            for h in (2 * p, 2 * p + 1):
                df_ref[:, h:h + 1] = -jnp.sum(cs_sc[h], axis=1, keepdims=True)

        @pl.when(j == nk - 1)
        def _():
            dqt_ref[...] = dqt_ref[...] * Q_SCALE

        @pl.when((pl.program_id(0) == nbat - 1) & (j == nk - 1))
        def _():
            for cp in _comm_copies(*comm):
                cp.wait()

    tile = lambda w, col: pl.BlockSpec((tk, w), lambda b, j: (b * nk + j, col))
    full = lambda col: pl.BlockSpec((seq, WIDTH), lambda b, j: (b, col))
    row = pl.BlockSpec((N_HEADS, seq), lambda b, j: (0, b))
    acc = pltpu.VMEM((N_HEADS, tk, LANES), F32)
    res = pl.pallas_call(
        body, name="fox_bwd", grid=(nbat, nk),
        in_specs=[tile(WIDTH, 1), tile(WIDTH, 2), full(0), full(0), tile(LANES, 0), row, row] + [ANY] * n,
        out_specs=[tile(WIDTH, 0), tile(WIDTH, 0), tile(LANES, 0), pl.BlockSpec((WIDTH, seq), lambda b, j: (b, 0)),
                   row] + [ANY] * n,
        out_shape=[jax.ShapeDtypeStruct((t_all, WIDTH), BF16), jax.ShapeDtypeStruct((t_all, WIDTH), BF16),
                   jax.ShapeDtypeStruct((t_all, LANES), F32), jax.ShapeDtypeStruct((nbat * WIDTH, seq), F32),
                   jax.ShapeDtypeStruct((N_HEADS, t_all), F32)] + _comm_out_shapes(grads, to_all),
        scratch_shapes=[pltpu.VMEM((N_HEADS, tk, LANES), BF16), pltpu.VMEM((N_HEADS, tk, LANES), BF16),
                        pltpu.VMEM((N_HEADS, tk, tk), F32), acc, acc, acc, pltpu.VMEM((4, LANES, tk), BF16),
                        pltpu.VMEM((N_HEADS, tk, tk), F32), pltpu.VMEM((N_HEADS, tk, tk), F32),
                        pltpu.VMEM((N_HEADS, tk, tk), BF16), pltpu.VMEM((N_HEADS, tk, tk), BF16)]
        + _comm_sems(n),
        compiler_params=_params(),
    )(za, za, za, do, f_col, lse_row, dl_row, *grads)
    return res[0], res[1], res[2], res[3], res[4], res[5:]


DIL_SUB = 4


def _dil_mask(has_prev):
    qi = lax.broadcasted_iota(jnp.int32, (BLK, 2 * BLK), 0)
    kj = lax.broadcasted_iota(jnp.int32, (BLK, 2 * BLK), 1)
    dist = qi + BLK - kj
    band = (dist >= 0) & (dist <= BLK)
    return band if has_prev is True else band & ((kj >= BLK) | has_prev)


def _dil_geometry(t_all, seq, d, max_sub=DIL_SUB):
    length = seq // d
    nbs = length // BLK
    sub = min(max_sub, nbs)
    spb = nbs // sub
    tile = lambda width, col: pl.BlockSpec((BLK * sub, width), lambda s: (s, col))
    whole = lambda width, col: pl.BlockSpec((length, width), lambda s: (s // spb, col))
    return nbs, sub, spb, t_all // (BLK * sub), tile, whole


def _blk(i):
    return pl.ds(pl.multiple_of(i * BLK, BLK), BLK)


def _dil_fwd(zb, seq, d):
    t_all = zb.shape[0]
    nbs, sub, spb, steps, tile, whole = _dil_geometry(t_all, seq, d)

    def body(q_ref, k_ref, v_ref, o_ref, lse_ref, s_sc, p_sc):
        first = (pl.program_id(0) % spb) * sub
        lo, hi = _half_masks(BLK)
        lse_ref[...] = jnp.zeros_like(lse_ref)
        for j in range(sub):
            blk = first + j
            mask = _dil_mask(blk != 0 if j == 0 else True)
            for p in range(4):
                cs = slice(LANES * p, LANES * (p + 1))
                qp = q_ref[BLK * j:BLK * (j + 1), cs]
                kcat = jnp.concatenate([k_ref[_blk(jnp.maximum(blk - 1, 0)), cs], k_ref[_blk(blk), cs]], axis=0)
                for e in (0, 1):
                    qe = jnp.where(lo if e == 0 else hi, qp, jnp.zeros_like(qp))
                    s_sc[N_HEADS * j + 2 * p + e] = jnp.where(mask, _nt(qe, kcat), NEG)
        inv = []
        for i in range(N_HEADS * sub):
            s = s_sc[i]
            m = jnp.max(s, axis=1, keepdims=True)
            pe = jnp.exp2(s - m)
            l = jnp.sum(pe, axis=1, keepdims=True)
            p_sc[i] = pe.astype(BF16)
            inv.append(1.0 / l)
            j, h = divmod(i, N_HEADS)
            lse_ref[BLK * j:BLK * (j + 1), h:h + 1] = m + jnp.log(l) * LOG2E
        for j in range(sub):
            blk = first + j
            for p in range(4):
                cs = slice(LANES * p, LANES * (p + 1))
                vcat = jnp.concatenate([v_ref[_blk(jnp.maximum(blk - 1, 0)), cs], v_ref[_blk(blk), cs]], axis=0)
                res = [_nn(p_sc[N_HEADS * j + h], vcat) * inv[N_HEADS * j + h] for h in (2 * p, 2 * p + 1)]
                o_ref[BLK * j:BLK * (j + 1), cs] = jnp.where(lo, res[0], res[1])

    return pl.pallas_call(
        body, name=f"dil_fwd_{d}", grid=(steps,), in_specs=[tile(WIDTH, 0), whole(WIDTH, 1), whole(WIDTH, 2)],
        out_specs=[tile(WIDTH, 0), tile(LANES, 0)],
        out_shape=[jax.ShapeDtypeStruct((t_all, WIDTH), F32), jax.ShapeDtypeStruct((t_all, LANES), F32)],
        scratch_shapes=[pltpu.VMEM((N_HEADS * sub, BLK, 2 * BLK), F32),
                        pltpu.VMEM((N_HEADS * sub, BLK, 2 * BLK), BF16)],
        compiler_params=_params(),
    )(zb, zb, zb)


def _dil_bwd(zb, do, lse, dl, seq, d):
    t_all = zb.shape[0]
    length = seq // d
    nbs, sub, spb, steps, tile, whole = _dil_geometry(t_all, seq, d, 2 if length >= 4096 else DIL_SUB)

    def body(k_ref, v_ref, q_ref, do_ref, lse_ref, dl_ref, dq_ref, dk_ref, dv_ref, s_sc, dp_sc, pt_sc, ds_sc, kt_sc,
             dqt_sc):
        step = pl.program_id(0) % spb
        first = step * sub

        @pl.when(step == 0)
        def _():
            dqt_sc[...] = jnp.zeros_like(dqt_sc)
        r = lax.broadcasted_iota(jnp.int32, (BLK, 2 * BLK), 0)
        c = lax.broadcasted_iota(jnp.int32, (BLK, 2 * BLK), 1)
        same = (c < BLK) & (c >= r)
        later = (c >= BLK) & (c - BLK <= r)
        lo, hi = _half_masks(BLK)
        for j in range(sub):
            blk = first + j
            rows = slice(BLK * j, BLK * (j + 1))
            nxt = _blk(jnp.minimum(blk + 1, nbs - 1))
            mask = same | (later & (blk + 1 != nbs)) if j == sub - 1 else same | later
            lrows = jnp.concatenate([lse_ref[_blk(blk), :].T, lse_ref[nxt, :].T], axis=1)
            erows = jnp.concatenate([dl_ref[_blk(blk), :].T, dl_ref[nxt, :].T], axis=1)
            for p in range(4):
                cs = slice(LANES * p, LANES * (p + 1))
                kp = k_ref[rows, cs]
                vp = v_ref[rows, cs]
                kt_sc[4 * j + p] = kp.astype(F32).T.astype(BF16)
                qcat = jnp.concatenate([q_ref[_blk(blk), cs], q_ref[nxt, cs]], axis=0)
                dcat = jnp.concatenate([do_ref[_blk(blk), cs], do_ref[nxt, cs]], axis=0)
                for e in (0, 1):
                    h = 2 * p + e
                    sel = lo if e == 0 else hi
                    ke = jnp.where(sel, kp, jnp.zeros_like(kp))
                    ve = jnp.where(sel, vp, jnp.zeros_like(vp))
                    s_sc[N_HEADS * j + h] = jnp.where(mask, _nt(ke, qcat) - lrows[h:h + 1, :], NEG)
                    dp_sc[N_HEADS * j + h] = _nt(ve, dcat) - erows[h:h + 1, :]
        for i in range(N_HEADS * sub):
            pt = jnp.exp2(s_sc[i])
            pt_sc[i] = pt.astype(BF16)
            ds_sc[i] = (pt * dp_sc[i]).astype(BF16)
        for j in range(sub):
            blk = first + j
            rows = slice(BLK * j, BLK * (j + 1))
            nxt = _blk(jnp.minimum(blk + 1, nbs - 1))
            cols = pl.ds(pl.multiple_of(blk * BLK, BLK), 2 * BLK)
            for p in range(4):
                cs = slice(LANES * p, LANES * (p + 1))
                qcat = jnp.concatenate([q_ref[_blk(blk), cs], q_ref[nxt, cs]], axis=0)
                dcat = jnp.concatenate([do_ref[_blk(blk), cs], do_ref[nxt, cs]], axis=0)
                i = N_HEADS * j + 2 * p
                dk_ref[rows, cs] = (jnp.where(lo, _nn(ds_sc[i], qcat), _nn(ds_sc[i + 1], qcat))
                                    * (1.0 / LOG2E)).astype(BF16)
                dv_ref[rows, cs] = jnp.where(lo, _nn(pt_sc[i], dcat), _nn(pt_sc[i + 1], dcat)).astype(BF16)
                for e in (0, 1):
                    kt = kt_sc[4 * j + p, HEAD_DIM * e:HEAD_DIM * (e + 1), :]
                    dqt_sc[HEAD_DIM * (2 * p + e):HEAD_DIM * (2 * p + e + 1), cols] += _nn(kt, ds_sc[i + e])

        @pl.when(step == spb - 1)
        def _():
            for p in range(4):
                cs = slice(LANES * p, LANES * (p + 1))
                dq_ref[:, cs] = (dqt_sc[cs, 0:length].T * Q_SCALE).astype(BF16)

    wide = pltpu.VMEM((N_HEADS * sub, BLK, 2 * BLK), F32)
    half = pltpu.VMEM((N_HEADS * sub, BLK, 2 * BLK), BF16)
    return pl.pallas_call(
        body, name=f"dil_bwd_{d}", grid=(steps,),
        in_specs=[tile(WIDTH, 1), tile(WIDTH, 2), whole(WIDTH, 0), whole(WIDTH, 0), whole(LANES, 0), whole(LANES, 0)],
        out_specs=[whole(WIDTH, 0), tile(WIDTH, 0), tile(WIDTH, 0)],
        out_shape=[jax.ShapeDtypeStruct((t_all, WIDTH), BF16)] * 3,
        scratch_shapes=[wide, wide, half, half, pltpu.VMEM((4 * sub, LANES, BLK), BF16),
                        pltpu.VMEM((WIDTH, length + BLK), F32)],
        compiler_params=_params(),
    )(zb, zb, zb, do, lse, dl)


def _mix_out(oa, o3, l3, gn_a, gn_b, w_out, x, ada3, ln_g, ln_b, perms, seq):
    t_all = x.shape[0]
    tm = TOK_TM
    nts = seq // tm

    def body(oa_ref, o1_ref, o2_ref, o3_ref, l1_ref, l2_ref, l3_ref, ga_ref, gb_ref, w_ref, x_ref, ada_ref, g_ref,
             b_ref, p4_ref, p16_ref, pt4_ref, pt16_ref, ob_ref, lse_ref, lse4_ref, lse16_ref, mg_ref, mix_ref, xh_ref,
             rs_ref, h2_ref, h2t_ref):
        e, et = _head_mats()
        la = l1_ref[...]
        lb = _permute_f32(pt4_ref[...], _load_classes(l2_ref, 4))
        lc = _permute_f32(pt16_ref[...], _load_classes(l3_ref, 16))
        mx = jnp.maximum(jnp.maximum(la, lb), lc)
        ea, eb, ec = jnp.exp2(la - mx), jnp.exp2(lb - mx), jnp.exp2(lc - mx)
        tot = ea + eb + ec
        lse = mx + jnp.log(tot) * LOG2E
        lse_ref[...] = lse
        _store_classes(lse4_ref, _permute_f32(p4_ref[...], lse), 4)
        _store_classes(lse16_ref, _permute_f32(p16_ref[...], lse), 16)
        ob = (o1_ref[...] * _hexp(ea / tot, e)
              + _permute_f32(pt4_ref[...], _load_classes(o2_ref, 4)) * _hexp(eb / tot, e)
              + _permute_f32(pt16_ref[...], _load_classes(o3_ref, 16)) * _hexp(ec / tot, e))
        ob_ref[...] = ob

        def rms(o, gain):
            rr = lax.rsqrt(_hsum(o * o, et) * (1.0 / HEAD_DIM) + RMS_EPS)
            return o * _hexp(rr, e) * gain

        merged = jnp.concatenate([rms(oa_ref[...], ga_ref[...]), rms(ob, gb_ref[...])], axis=1).astype(BF16)
        mg_ref[...] = merged
        mix = _nn(merged, w_ref[...])
        mix_ref[...] = mix.astype(BF16)
        r1 = ALPHA * x_ref[...] + ada_ref[0, 2:3, :] * mix
        d = r1 - jnp.mean(r1, axis=1, keepdims=True)
        rstd = lax.rsqrt(jnp.mean(d * d, axis=1, keepdims=True) + LN_EPS)
        xh = d * rstd
        xh_ref[...] = xh
        rs_ref[...] = jnp.broadcast_to(rstd, (tm, LANES))
        x1 = xh * g_ref[...] + b_ref[...]
        h2 = x1 * (1.0 + ada_ref[0, 4:5, :]) + ada_ref[0, 3:4, :]
        h2_ref[...] = h2.astype(BF16)
        h2t_ref[0] = h2.T.astype(BF16)

    tok = lambda w: pl.BlockSpec((tm, w), lambda i: (i, 0))
    vec = lambda w: pl.BlockSpec((1, w), lambda i: (0, 0))
    whole = lambda a: pl.BlockSpec(a.shape, lambda i: (0, 0))
    classes = lambda a, d: a.reshape(t_all // seq * d, seq // d, a.shape[-1])
    return pl.pallas_call(
        body, name="mix_out", grid=(t_all // tm,),
        in_specs=[tok(WIDTH), tok(WIDTH), _class_spec(4, WIDTH, nts), _class_spec(16, WIDTH, nts), tok(LANES),
                  _class_spec(4, LANES, nts), _class_spec(16, LANES, nts), vec(WIDTH), vec(WIDTH), whole(w_out),
                  tok(D_MODEL), pl.BlockSpec((1, 6, D_MODEL), lambda i: (i // nts, 0, 0)), vec(D_MODEL), vec(D_MODEL)]
        + [whole(p) for p in perms],
        out_specs=[tok(WIDTH), tok(LANES), _class_spec(4, LANES, nts), _class_spec(16, LANES, nts), tok(D_MODEL),
                   tok(D_MODEL), tok(D_MODEL), tok(LANES), tok(D_MODEL), pl.BlockSpec((1, D_MODEL, tm), lambda i: (i // (FFN_TM // tm), 0, i % (FFN_TM // tm)))],
        out_shape=[jax.ShapeDtypeStruct((t_all, WIDTH), F32), jax.ShapeDtypeStruct((t_all, LANES), F32),
                   _class_shape(t_all, seq, 4, LANES, F32), _class_shape(t_all, seq, 16, LANES, F32),
                   jax.ShapeDtypeStruct((t_all, D_MODEL), BF16), jax.ShapeDtypeStruct((t_all, D_MODEL), BF16),
                   jax.ShapeDtypeStruct((t_all, D_MODEL), F32), jax.ShapeDtypeStruct((t_all, LANES), F32),
                   jax.ShapeDtypeStruct((t_all, D_MODEL), BF16),
                   jax.ShapeDtypeStruct((t_all // FFN_TM, D_MODEL, FFN_TM), BF16)],
        compiler_params=_params(),
    )(oa, o3[0], classes(o3[1], 4), classes(o3[2], 16), l3[0], classes(l3[1], 4), classes(l3[2], 16), gn_a, gn_b,
      w_out, x, ada3, ln_g, ln_b, *perms)


def _mix_out_bwd(dmix, w_out, oa, ob, gn_a, gn_b, perms, seq):
    t_all = dmix.shape[0]
    tm = TOK_TM
    nts = seq // tm

    def body(dm_ref, w_ref, oa_ref, ob_ref, ga_ref, gb_ref, p4_ref, p16_ref, doa_ref, dob_ref, dob4_ref, dob16_ref,
             dla_ref, dlb_ref, dlb4_ref, dlb16_ref, acc_ref):
        @pl.when(pl.program_id(0) == 0)
        def _():
            acc_ref[...] = jnp.zeros_like(acc_ref)
        e, et = _head_mats()
        dmg = _nt(dm_ref[...], w_ref[...])

        def group(o, dn, gain):
            rr = lax.rsqrt(_hsum(o * o, et) * (1.0 / HEAD_DIM) + RMS_EPS)
            re = _hexp(rr, e)
            dgain = jnp.sum(dn * o * re, axis=0, keepdims=True)
            dxn = dn * gain
            tt = _hsum(dxn * o, et) * (rr * rr * rr) * (1.0 / HEAD_DIM)
            do = re * dxn - o * _hexp(tt, e)
            return do, _hsum(do * o, et), dgain

        doa, dla, dga = group(oa_ref[...], dmg[:, :WIDTH], ga_ref[...])
        dob, dlb, dgb = group(ob_ref[...], dmg[:, WIDTH:], gb_ref[...])
        dob = dob.astype(BF16)
        doa_ref[...] = doa.astype(BF16)
        dob_ref[...] = dob
        _store_classes(dob4_ref, _nn(p4_ref[...], dob).astype(BF16), 4)
        _store_classes(dob16_ref, _nn(p16_ref[...], dob).astype(BF16), 16)
        dla_ref[...] = dla
        dlb_ref[...] = dlb
        _store_classes(dlb4_ref, _permute_f32(p4_ref[...], dlb), 4)
        _store_classes(dlb16_ref, _permute_f32(p16_ref[...], dlb), 16)
        acc_ref[0:1, :] += jnp.concatenate([dga, dgb], axis=1)

    tok = lambda w: pl.BlockSpec((tm, w), lambda i: (i, 0))
    vec = lambda w: pl.BlockSpec((1, w), lambda i: (0, 0))
    return pl.pallas_call(
        body, name="mix_out_bwd", grid=(t_all // tm,),
        in_specs=[tok(D_MODEL), pl.BlockSpec(w_out.shape, lambda i: (0, 0)), tok(WIDTH), tok(WIDTH), vec(WIDTH),
                  vec(WIDTH), pl.BlockSpec(perms[0].shape, lambda i: (0, 0)),
                  pl.BlockSpec(perms[1].shape, lambda i: (0, 0))],
        out_specs=[tok(WIDTH), tok(WIDTH), _class_spec(4, WIDTH, nts), _class_spec(16, WIDTH, nts), tok(LANES),
                   tok(LANES), _class_spec(4, LANES, nts), _class_spec(16, LANES, nts),
                   pl.BlockSpec((8, D_MODEL), lambda i: (0, 0))],
        out_shape=[jax.ShapeDtypeStruct((t_all, WIDTH), BF16), jax.ShapeDtypeStruct((t_all, WIDTH), BF16),
                   _class_shape(t_all, seq, 4, WIDTH, BF16), _class_shape(t_all, seq, 16, WIDTH, BF16),
                   jax.ShapeDtypeStruct((t_all, LANES), F32), jax.ShapeDtypeStruct((t_all, LANES), F32),
                   _class_shape(t_all, seq, 4, LANES, F32), _class_shape(t_all, seq, 16, LANES, F32),
                   jax.ShapeDtypeStruct((8, D_MODEL), F32)],
        compiler_params=_params(),
    )(dmix, w_out, oa, ob, gn_a, gn_b, perms[0], perms[1])


def _inproj_bwd(dqt, dka, dva, dil1, dil4, dil16, dfa16, pos, wqkv, wf16, freq, perms, dr1, x, ada3, seq):
    t_all = x.shape[0]
    tm = TOK_TM
    nts = seq // tm

    def body(dqt_ref, dka_ref, dva_ref, q1_ref, k1_ref, v1_ref, q4_ref, k4_ref, v4_ref, q16_ref, k16_ref, v16_ref,
             dfa_ref, pos_ref, w_ref, wf_ref, fr_ref, pt4_ref, pt16_ref, dr1_ref, x_ref, ada_ref, gx_ref, dz_ref,
             acc_ref):
        i = pl.program_id(0)

        @pl.when(i == 0)
        def _():
            acc_ref[...] = jnp.zeros_like(acc_ref)
        tabs = _rope_tabs(pos_ref, fr_ref, -1.0)
        dz_ref[:, :WIDTH] = dqt_ref[...].T.astype(BF16)
        dz_ref[:, WIDTH:2 * WIDTH] = dka_ref[...]
        dz_ref[:, 2 * WIDTH:3 * WIDTH] = dva_ref[...]
        for t, (n1, n4, n16) in enumerate(((q1_ref, q4_ref, q16_ref), (k1_ref, k4_ref, k16_ref),
                                           (v1_ref, v4_ref, v16_ref))):
            tot = (n1[...].astype(F32) + _nn(pt4_ref[...], _load_classes(n4, 4))
                   + _nn(pt16_ref[...], _load_classes(n16, 16)))
            if t < 2:
                tot = _rope(tot, tabs)
            dz_ref[:, (3 + t) * WIDTH:(4 + t) * WIDTH] = tot.astype(BF16)
        dh1 = _tn(dfa_ref[...], wf_ref[...])
        for n in range(6):
            cs = slice(n * WIDTH, (n + 1) * WIDTH)
            dh1 = dh1 + _nt(dz_ref[:, cs], w_ref[:, cs])
        xv = x_ref[...]
        gx_ref[...] = ALPHA * dr1_ref[...] + dh1 * (1.0 + ada_ref[0, 1:2, :])
        b = i // nts
        acc_ref[pl.ds(b, 1), :] += jnp.sum(dh1 * xv, axis=0, keepdims=True)
        acc_ref[pl.ds(8 + b, 1), :] += jnp.sum(dh1, axis=0, keepdims=True)

    tok = lambda w: pl.BlockSpec((tm, w), lambda i: (i, 0))
    whole = lambda a: pl.BlockSpec(a.shape, lambda i: (0, 0))
    classes = lambda a, d: a.reshape(t_all // seq * d, seq // d, a.shape[-1])
    return pl.pallas_call(
        body, name="inproj_bwd", grid=(t_all // tm,),
        in_specs=[pl.BlockSpec((WIDTH, tm), lambda i: (i // nts, i % nts)), tok(WIDTH), tok(WIDTH)]
        + [tok(WIDTH)] * 3 + [_class_spec(4, WIDTH, nts)] * 3 + [_class_spec(16, WIDTH, nts)] * 3
        + [pl.BlockSpec((16, tm), lambda i: (0, i)), tok(1), whole(wqkv), whole(wf16),
           pl.BlockSpec((1, LANES), lambda i: (0, 0)), whole(perms[2]), whole(perms[3]), tok(D_MODEL), tok(D_MODEL),
           pl.BlockSpec((1, 6, D_MODEL), lambda i: (i // nts, 0, 0))],
        out_specs=[tok(D_MODEL), tok(6 * WIDTH), pl.BlockSpec((16, D_MODEL), lambda i: (0, 0))],
        out_shape=[jax.ShapeDtypeStruct((t_all, D_MODEL), F32), jax.ShapeDtypeStruct((t_all, 6 * WIDTH), BF16),
                   jax.ShapeDtypeStruct((16, D_MODEL), F32)],
        compiler_params=_params(),
    )(dqt, dka, dva, *dil1, *[classes(a, 4) for a in dil4], *[classes(a, 16) for a in dil16], dfa16, pos, wqkv, wf16,
      freq, perms[2], perms[3], dr1, x, ada3)


FFN_TM = 1024
FFN_TN = 256
HALO = 8


FFN_CHUNK = 256


def _conv_params(cw_ref, cb_ref, n, tn):
    a = pl.ds(pl.multiple_of(n * tn, tn), tn)
    g = pl.ds(pl.multiple_of(D_FF + n * tn, tn), tn)
    return cw_ref[:, a], cw_ref[:, g], cb_ref[:, a], cb_ref[:, g]


def _conv(cat_ref, w_ref, b_ref, start, rows, halo=HALO):
    return (b_ref[...] + w_ref[0:1, :] * cat_ref[pl.ds(start + halo - 2, rows), :]
            + w_ref[1:2, :] * cat_ref[pl.ds(start + halo - 1, rows), :]
            + w_ref[2:3, :] * cat_ref[pl.ds(start + halo, rows), :])


def _ffn_up_gate(h2, w_up, conv_w, conv_b, seq):
    t_all = h2.shape[0]
    tm, tn = FFN_TM, FFN_TN
    nc = D_FF // tn
    nts = seq // tm
    pre = 16

    def body(h_ref, hp_ref, wua_ref, wug_ref, cw_ref, cb_ref, ua_ref, ug_ref, o_ref, ca_ref, cg_ref):
        first = (pl.program_id(1) % nts) == 0
        wa_ref, wg_ref, ba_ref, bg_ref = _conv_params(cw_ref, cb_ref, pl.program_id(0), tn)
        hcat = jnp.concatenate([hp_ref[...], h_ref[...]], axis=0)
        zero = jnp.zeros((pre, tn), F32)
        for w_ref, cat, u_ref in ((wua_ref, ca_ref, ua_ref), (wug_ref, cg_ref, ug_ref)):
            ub = _nn(hcat, w_ref[...]).astype(BF16)
            ue = ub.astype(F32)
            cat[0:pre, :] = jnp.where(first, zero, ue[0:pre])
            cat[pre:, :] = ue[pre:]
            u_ref[...] = ub[pre:]
        for c0 in range(0, tm, FFN_CHUNK):
            ya = _conv(ca_ref, wa_ref, ba_ref, c0, FFN_CHUNK, pre)
            yg = _conv(cg_ref, wg_ref, bg_ref, c0, FFN_CHUNK, pre)
            o_ref[c0:c0 + FFN_CHUNK, :] = (yg * jax.nn.sigmoid(yg) * ya).astype(BF16)

    wcol = lambda off: pl.BlockSpec((D_MODEL, tn), lambda n, t: (0, n + off))
    tile = pl.BlockSpec((tm, tn), lambda n, t: (t, n))
    return pl.pallas_call(
        body, name="ffn_up_gate", grid=(nc, t_all // tm),
        in_specs=[pl.BlockSpec((tm, D_MODEL), lambda n, t: (t, 0)),
                  pl.BlockSpec((pre, D_MODEL), lambda n, t: (jnp.maximum(t * (tm // pre) - 1, 0), 0)),
                  wcol(0), wcol(nc), pl.BlockSpec(conv_w.shape, lambda n, t: (0, 0)),
                  pl.BlockSpec(conv_b.shape, lambda n, t: (0, 0))],
        out_specs=[tile, tile, tile],
        out_shape=[jax.ShapeDtypeStruct((t_all, D_FF), BF16)] * 3,
        scratch_shapes=[pltpu.VMEM((tm + pre, tn), F32)] * 2, compiler_params=_params(),
    )(h2, h2, w_up, w_up, conv_w, conv_b)


def _ffn_gate_bwd(u_a, u_g, dfi, conv_w, conv_b, h2t, seq):
    t_all = u_a.shape[0]
    tm, tn = FFN_TM, FFN_TN
    nc = D_FF // tn
    nts = seq // tm

    def body(ua_ref, uap_ref, uan_ref, ug_ref, ugp_ref, ugn_ref, df_ref, dfn_ref, cw_ref, cb_ref, h_ref,
             dua_ref, dug_ref, acca_ref, accg_ref, dwa_ref, dwg_ref, ca_ref, cg_ref, ya_ref, yg_ref, dwa_sc, dwg_sc,
             out_sems):
        t = pl.program_id(0)
        n = pl.program_id(1)
        cols = pl.ds(pl.multiple_of(n * tn, tn), tn)
        first = (t % nts) == 0
        last = (t % nts) == nts - 1

        @pl.when((t == 0) & (n == 0))
        def _():
            acca_ref[...] = jnp.zeros_like(acca_ref)
            accg_ref[...] = jnp.zeros_like(accg_ref)
            dwa_sc[...] = jnp.zeros_like(dwa_sc)
            dwg_sc[...] = jnp.zeros_like(dwg_sc)
        wa_ref, wg_ref, ba_ref, bg_ref = _conv_params(cw_ref, cb_ref, n, tn)
        zero = jnp.zeros((HALO, tn), F32)
        for cat, cur, prv, nxt in ((ca_ref, ua_ref, uap_ref, uan_ref), (cg_ref, ug_ref, ugp_ref, ugn_ref)):
            cat[0:HALO, :] = jnp.where(first, zero, prv[...].astype(F32)[HALO:])
            cat[HALO:HALO + tm, :] = cur[...].astype(F32)
            cat[HALO + tm:, :] = nxt[...].astype(F32)[:HALO]
        ch = FFN_CHUNK
        sums = [[jnp.zeros((1, tn), F32) for _ in range(4)] for _ in range(2)]
        for ci, c0 in enumerate(range(0, tm, ch)):
            ya = _conv(ca_ref, wa_ref, ba_ref, c0, ch + HALO)
            yg = _conv(cg_ref, wg_ref, bg_ref, c0, ch + HALO)
            if c0 + ch < tm:
                beyond = df_ref[c0 + ch:c0 + ch + 16, :].astype(F32)[:HALO]
            else:
                beyond = jnp.where(last, 0.0, dfn_ref[...].astype(F32)[:HALO])
            dfe = jnp.concatenate([df_ref[c0:c0 + ch, :].astype(F32), beyond], axis=0)
            sg = jax.nn.sigmoid(yg)
            ya_ref[ci] = dfe * (yg * sg)
            yg_ref[ci] = dfe * ya * (sg * (1.0 + yg * (1.0 - sg)))
            for half, (dy, cat, w_ref, du_ref) in enumerate(((ya_ref, ca_ref, wa_ref, dua_ref),
                                                             (yg_ref, cg_ref, wg_ref, dug_ref))):
                d0 = dy[ci, 0:ch, :]
                du = (w_ref[2:3, :] * d0 + w_ref[1:2, :] * dy[ci, pl.ds(1, ch), :]
                      + w_ref[0:1, :] * dy[ci, pl.ds(2, ch), :])
                du_ref[c0:c0 + ch, :] = du.astype(BF16)
                for k in range(3):
                    sums[half][k] += jnp.sum(d0 * cat[pl.ds(c0 + HALO - 2 + k, ch), :], axis=0, keepdims=True)
                sums[half][3] += jnp.sum(d0, axis=0, keepdims=True)
        for half, acc in enumerate((acca_ref, accg_ref)):
            for k in range(4):
                acc[k:k + 1, cols] += sums[half][k]
        ht = h_ref[0]
        dwa_sc[:, cols] += _nn(ht, dua_ref[...])
        dwg_sc[:, cols] += _nn(ht, dug_ref[...])

        @pl.when((t == t_all // tm - 1) & (n == nc - 1))
        def _():
            copies = [pltpu.make_async_copy(dwa_sc, dwa_ref, out_sems.at[0]),
                      pltpu.make_async_copy(dwg_sc, dwg_ref, out_sems.at[1])]
            for cp in copies:
                cp.start()
            for cp in copies:
                cp.wait()

    nrow = t_all // 16
    cur = pl.BlockSpec((tm, tn), lambda t, n: (t, n))
    prev = pl.BlockSpec((16, tn), lambda t, n: (jnp.maximum(t * (tm // 16) - 1, 0), n))
    nxt = pl.BlockSpec((16, tn), lambda t, n: (jnp.minimum((t + 1) * (tm // 16), nrow - 1), n))
    acc = pl.BlockSpec((8, D_FF), lambda t, n: (0, 0))
    return pl.pallas_call(
        body, name="ffn_gate_bwd", grid=(t_all // tm, nc),
        in_specs=[cur, prev, nxt, cur, prev, nxt, cur, nxt, pl.BlockSpec(conv_w.shape, lambda t, n: (0, 0)),
                  pl.BlockSpec(conv_b.shape, lambda t, n: (0, 0)),
                  pl.BlockSpec((1, D_MODEL, tm), lambda t, n: (t, 0, 0))],
        out_specs=[cur, cur, acc, acc, ANY, ANY],
        out_shape=[jax.ShapeDtypeStruct((t_all, D_FF), BF16), jax.ShapeDtypeStruct((t_all, D_FF), BF16),
                   jax.ShapeDtypeStruct((8, D_FF), F32), jax.ShapeDtypeStruct((8, D_FF), F32),
                   jax.ShapeDtypeStruct((D_MODEL, D_FF), F32), jax.ShapeDtypeStruct((D_MODEL, D_FF), F32)],
        scratch_shapes=[pltpu.VMEM((tm + 2 * HALO, tn), F32)] * 2
        + [pltpu.VMEM((tm // FFN_CHUNK, FFN_CHUNK + HALO, tn), F32)] * 2
        + [pltpu.VMEM((D_MODEL, D_FF), F32)] * 2 + [pltpu.SemaphoreType.DMA((2,))],
        compiler_params=_params(),
    )(u_a, u_a, u_a, u_g, u_g, u_g, dfi, dfi, conv_w, conv_b, h2t)


def _ffn_down(ffn_in, w_down, xh1, ln1_g, ln1_b, ada3, ln2_g, ln2_b, target, seq):
    t_all = xh1.shape[0]
    tm = 512
    nts = seq // tm

    def body(f_ref, w_ref, xh_ref, g1_ref, b1_ref, ada_ref, g2_ref, b2_ref, tg_ref, dr2_ref, acc_ref):
        i = pl.program_id(0)

        @pl.when(i == 0)
        def _():
            acc_ref[...] = jnp.zeros_like(acc_ref)
        ffn = _nn(f_ref[...], w_ref[...])
        x1 = xh_ref[...] * g1_ref[...] + b1_ref[...]
        r2 = ALPHA * x1 + ada_ref[0, 5:6, :] * ffn
        d = r2 - jnp.mean(r2, axis=1, keepdims=True)
        rstd = lax.rsqrt(jnp.mean(d * d, axis=1, keepdims=True) + LN_EPS)
        xh2 = d * rstd
        diff = xh2 * g2_ref[...] + b2_ref[...] - tg_ref[...]
        dy = diff * (1.0 / D_MODEL)
        dr2 = _layer_norm_bwd(dy * g2_ref[...], xh2, rstd)
        dr2_ref[...] = dr2
        acc_ref[0:1, :] += jnp.sum(dy * xh2, axis=0, keepdims=True)
        acc_ref[1:2, :] += jnp.sum(dy, axis=0, keepdims=True)
        acc_ref[2:3, :] += jnp.sum(diff * diff, axis=0, keepdims=True) * (0.5 / D_MODEL)
        acc_ref[pl.ds(8 + i // nts, 1), :] += jnp.sum(dr2 * ffn, axis=0, keepdims=True)

    tok = lambda w: pl.BlockSpec((tm, w), lambda i: (i, 0))
    vec = pl.BlockSpec((1, D_MODEL), lambda i: (0, 0))
    return pl.pallas_call(
        body, name="ffn_down", grid=(t_all // tm,),
        in_specs=[tok(D_FF), pl.BlockSpec(w_down.shape, lambda i: (0, 0)), tok(D_MODEL), vec, vec,
                  pl.BlockSpec((1, 6, D_MODEL), lambda i: (i // nts, 0, 0)), vec, vec, tok(D_MODEL)],
        out_specs=[tok(D_MODEL), pl.BlockSpec((16, D_MODEL), lambda i: (0, 0))],
        out_shape=[jax.ShapeDtypeStruct((t_all, D_MODEL), F32), jax.ShapeDtypeStruct((16, D_MODEL), F32)],
        compiler_params=_params(),
    )(ffn_in, w_down, xh1, ln1_g, ln1_b, ada3, ln2_g, ln2_b, target)


def _ffn_down_bwd(dr2, ada3, w_down, seq):
    t_all = dr2.shape[0]
    tm = 512
    nts = seq // tm

    def body(d_ref, ada_ref, w_ref, dffn_ref, dfi_ref):
        dffn = (d_ref[...] * ada_ref[0, 5:6, :]).astype(BF16)
        dffn_ref[...] = dffn
        dfi_ref[...] = _nt(dffn, w_ref[...]).astype(BF16)

    tok = lambda w: pl.BlockSpec((tm, w), lambda i: (i, 0))
    return pl.pallas_call(
        body, name="ffn_down_bwd", grid=(t_all // tm,),
        in_specs=[tok(D_MODEL), pl.BlockSpec((1, 6, D_MODEL), lambda i: (i // nts, 0, 0)),
                  pl.BlockSpec(w_down.shape, lambda i: (0, 0))],
        out_specs=[tok(D_MODEL), tok(D_FF)],
        out_shape=[jax.ShapeDtypeStruct((t_all, D_MODEL), BF16), jax.ShapeDtypeStruct((t_all, D_FF), BF16)],
        compiler_params=_params(),
    )(dr2, ada3, w_down)


def _ffn_up_bwd(du_a, du_g, w_up, dr2, xh1, rs1, mix, ada3, ln1_g, ln1_b, seq):
    t_all = dr2.shape[0]
    tm = 512
    nts = seq // tm

    def body(da_ref, dg_ref, w_ref, dr2_ref, xh_ref, rs_ref, mix_ref, ada_ref, g_ref, b_ref, dr1_ref, dmix_ref,
             acc_ref):
        i = pl.program_id(0)

        @pl.when(i == 0)
        def _():
            acc_ref[...] = jnp.zeros_like(acc_ref)
        dh2 = _nt(da_ref[...], w_ref[:, :D_FF]) + _nt(dg_ref[...], w_ref[:, D_FF:])
        xh = xh_ref[...]
        x1 = xh * g_ref[...] + b_ref[...]
        dx1 = ALPHA * dr2_ref[...] + dh2 * (1.0 + ada_ref[0, 4:5, :])
        dr1 = _layer_norm_bwd(dx1 * g_ref[...], xh, rs_ref[:, 0:1])
        dr1_ref[...] = dr1
        dmix_ref[...] = (dr1 * ada_ref[0, 2:3, :]).astype(BF16)
        b = i // nts
        acc_ref[0:1, :] += jnp.sum(dx1 * xh, axis=0, keepdims=True)
        acc_ref[1:2, :] += jnp.sum(dx1, axis=0, keepdims=True)
        acc_ref[pl.ds(8 + b, 1), :] += jnp.sum(dh2 * x1, axis=0, keepdims=True)
        acc_ref[pl.ds(16 + b, 1), :] += jnp.sum(dh2, axis=0, keepdims=True)
        acc_ref[pl.ds(24 + b, 1), :] += jnp.sum(dr1 * mix_ref[...].astype(F32), axis=0, keepdims=True)

    tok = lambda w: pl.BlockSpec((tm, w), lambda i: (i, 0))
    vec = pl.BlockSpec((1, D_MODEL), lambda i: (0, 0))
    return pl.pallas_call(
        body, name="ffn_up_bwd", grid=(t_all // tm,),
        in_specs=[tok(D_FF), tok(D_FF), pl.BlockSpec(w_up.shape, lambda i: (0, 0)), tok(D_MODEL), tok(D_MODEL),
                  tok(LANES), tok(D_MODEL), pl.BlockSpec((1, 6, D_MODEL), lambda i: (i // nts, 0, 0)), vec, vec],
        out_specs=[tok(D_MODEL), tok(D_MODEL), pl.BlockSpec((32, D_MODEL), lambda i: (0, 0))],
        out_shape=[jax.ShapeDtypeStruct((t_all, D_MODEL), F32), jax.ShapeDtypeStruct((t_all, D_MODEL), BF16),
                   jax.ShapeDtypeStruct((32, D_MODEL), F32)],
        compiler_params=_params(),
    )(du_a, du_g, w_up, dr2, xh1, rs1, mix, ada3, ln1_g, ln1_b)


def _rows(a):
    return a[:, :N_HEADS].T


def _rope_freq():
    f = np.float32(ROPE_THETA) ** (-np.arange(0, ROPE_DIMS, 2, dtype=np.float32) / np.float32(ROPE_DIMS))
    return jnp.asarray(np.tile(f.astype(np.float32), LANES // (ROPE_DIMS // 2))[None, :])


def _local_step(x, positions, target, ada3, w_in, b_fgate, gn_a, gn_b, ln1_g, ln1_b, conv_b, ln2_g, ln2_b,
                late_shards):
    nbat, seq, _ = x.shape
    t_all = nbat * seq
    xf = x.reshape(t_all, D_MODEL)
    tg = target.reshape(t_all, D_MODEL)
    pos = positions.reshape(t_all, 1)
    freq = _rope_freq()

    wqkv = jnp.concatenate([w_in[:, :3 * WIDTH], w_in[:, 3 * WIDTH + N_HEADS:]], axis=1)
    wf16 = jnp.zeros((16, D_MODEL), BF16).at[:N_HEADS].set(w_in[:, 3 * WIDTH:3 * WIDTH + N_HEADS].T)
    bf = b_fgate.reshape(N_HEADS, 1)

    perms = [_perm_matrix(TOK_TM, d, tr) for tr in (False, True) for d in DILATIONS[1:]]
    h1, za, zb1, zb4, zb16, vt, fa_t = _inproj(xf, ada3, pos, wqkv, wf16, freq, perms, seq)
    zbs = [zb1, zb4.reshape(t_all, 3 * WIDTH), zb16.reshape(t_all, 3 * WIDTH)]
    f_row = _fgate_fwd(fa_t, bf, seq)
    f_col = jnp.zeros((t_all, LANES), F32).at[:, :N_HEADS].set(f_row.T * LOG2E)
    oa, lse_row_a, gathered = _fox_fwd(za, vt, f_col, seq, [late_shards[n] for n in LATE])
    w_out, w_up, conv_w, w_down = (_full_from_gathered(n, g) for n, g in zip(LATE, gathered))
    o3, l3 = zip(*[_dil_fwd(zb, seq, d) for zb, d in zip(zbs, DILATIONS)])
    ob, lse_b, lse_b4, lse_b16, merged, mix, xh1, rs1, h2, h2t = _mix_out(oa, o3, l3, gn_a, gn_b, w_out, xf, ada3, ln1_g,
                                                                      ln1_b, perms, seq)
    u_a, u_g, ffn_in = _ffn_up_gate(h2, w_up, conv_w, conv_b, seq)
    dr2, acc2 = _ffn_down(ffn_in, w_down, xh1, ln1_g, ln1_b, ada3, ln2_g, ln2_b, tg, seq)

    dffn, dfi = _ffn_down_bwd(dr2, ada3, w_down, seq)
    d_w_down = _matmul_tn(dffn, ffn_in, 512, 512, "dw_down").T
    du_a, du_g, acc_ca, acc_cg, dw_up_a, dw_up_g = _ffn_gate_bwd(u_a, u_g, dfi, conv_w, conv_b, h2t, seq)
    dr1, dmix, acc1 = _ffn_up_bwd(du_a, du_g, w_up, dr2, xh1, rs1, mix, ada3, ln1_g, ln1_b, seq)
    d_w_up = jnp.concatenate([dw_up_a, dw_up_g], axis=1)

    doa, dob, dob4, dob16, dl_a, dl_b, dl_b4, dl_b16, acc_gn = _mix_out_bwd(dmix, w_out, oa, ob, gn_a, gn_b, perms, seq)
    d_w_out = _matmul_tn(merged, dmix, 512, 512, "dw_out")
    late_grads = dict(w_out=d_w_out, w_up=d_w_up, conv_w=jnp.concatenate([acc_ca[0:3], acc_cg[0:3]], axis=1),
                      w_down=d_w_down)
    dka, dva, df_k, dqt, df_q, late_parts = _fox_bwd(za, doa, f_col, lse_row_a, _rows(dl_a), seq,
                                                     [_payload(n, _dest_major(n, late_grads[n])) for n in LATE])
    dfa_t, dbf = _fgate_bwd(_rows(df_k) + df_q, fa_t, bf, seq)
    flat = lambda a: a.reshape(t_all, a.shape[-1])
    dil = []
    for zb, d, do, lse, dl in zip(zbs, DILATIONS, (dob, flat(dob4), flat(dob16)),
                                  (lse_b, flat(lse_b4), flat(lse_b16)), (dl_b, flat(dl_b4), flat(dl_b16))):
        dil.append(_dil_bwd(zb, do, lse, dl, seq, d))
    dfa16 = jnp.zeros((16, t_all), BF16).at[:N_HEADS].set(dfa_t.astype(BF16))
    grad_x, dz, acc0 = _inproj_bwd(dqt, dka, dva, dil[0], dil[1], dil[2], dfa16, pos, wqkv, wf16, freq, perms, dr1, xf,
                                   ada3, seq)
    d_wqkv = _matmul_tn(h1, dz, 512, 512, "dw_in")
    d_wf = _matmul_rows(dfa16, h1, 512, "dw_fgate")[:N_HEADS].T
    d_w_in = jnp.concatenate([d_wqkv[:, :3 * WIDTH], d_wf, d_wqkv[:, 3 * WIDTH:]], axis=1)

    dada = jnp.concatenate([acc0[8:8 + nbat], acc0[:nbat], acc1[24:24 + nbat], acc1[16:16 + nbat], acc1[8:8 + nbat],
                            acc2[8:8 + nbat]], axis=1)

    grads = dict(
        dada=dada, b_ada=jnp.sum(dada, axis=0, keepdims=True), w_in=d_w_in, b_fgate=dbf[:, 0][None, :],
        gn_a=acc_gn[0:1, :WIDTH], gn_b=acc_gn[0:1, WIDTH:], ln1_g=acc1[0:1], ln1_b=acc1[1:2],
        conv_b=jnp.concatenate([acc_ca[3:4], acc_cg[3:4]], axis=1), ln2_g=acc2[0:1], ln2_b=acc2[1:2])
    return acc2[2:3], grad_x.reshape(x.shape), grads, dict(zip(LATE, late_parts))


LATE = ("w_out", "w_up", "conv_w", "w_down")
BIG = ("w_ada", "w_in") + LATE
COLUMN_SHARDED = ("w_ada", "w_in", "w_up", "conv_w")


def _payload(name, a):
    return a if name == "conv_w" else a.astype(BF16)
SMALL = ("b_ada", "b_fgate", "gn_a", "gn_b", "ln1_g", "ln1_b", "conv_b", "ln2_g", "ln2_b")
ADAM_ROWS = dict(w_ada=256, w_in=256, w_out=128, w_up=256, conv_w=3, w_down=176)
SMALL_ROWS = 24


def _full_from_gathered(name, g):
    if name in COLUMN_SHARDED:
        return g.transpose(1, 0, 2).reshape(g.shape[1], N_DEV * g.shape[2])
    return g.reshape(N_DEV * g.shape[1], g.shape[2])


def _dest_major(name, full):
    if name in COLUMN_SHARDED:
        r, cfull = full.shape
        return full.reshape(r, N_DEV, cfull // N_DEV).transpose(1, 0, 2)
    return full.reshape(N_DEV, full.shape[0] // N_DEV, full.shape[1])


def _pack_small(vals, extra=None):
    parts = [vals[n].reshape(-1) for n in SMALL]
    if extra is not None:
        parts.append(extra.reshape(-1))
    flat = jnp.concatenate(parts)
    return jnp.pad(flat, (0, SMALL_ROWS * D_MODEL - flat.shape[0])).reshape(SMALL_ROWS, D_MODEL)


def _unpack_small(packed, like):
    flat = packed.reshape(-1)
    out, off = {}, 0
    for n in SMALL:
        size = like[n].size
        out[n] = flat[off:off + size].reshape(like[n].shape)
        off += size
    return out, flat[off:off + D_MODEL]


def kernel(x, c, positions, w_ada, b_ada, w_in, b_fgate, gn_a, gn_b, w_out, ln1_g, ln1_b, w_up, conv_w, conv_b, w_down, ln2_g, ln2_b, loss_target, m_w_ada, m_b_ada, m_w_in, m_b_fgate, m_gn_a, m_gn_b, m_w_out, m_ln1_g, m_ln1_b, m_w_up, m_conv_w, m_conv_b, m_w_down, m_ln2_g, m_ln2_b, v_w_ada, v_b_ada, v_w_in, v_b_fgate, v_gn_a, v_gn_b, v_w_out, v_ln1_g, v_ln1_b, v_w_up, v_conv_w, v_conv_b, v_w_down, v_ln2_g, v_ln2_b):
    w = dict(w_ada=w_ada[0], b_ada=b_ada, w_in=w_in[0], b_fgate=b_fgate, gn_a=gn_a, gn_b=gn_b, w_out=w_out[0],
             ln1_g=ln1_g, ln1_b=ln1_b, w_up=w_up[0], conv_w=conv_w[0], conv_b=conv_b, w_down=w_down[0], ln2_g=ln2_g,
             ln2_b=ln2_b)
    m = dict(w_ada=m_w_ada[0], b_ada=m_b_ada, w_in=m_w_in[0], b_fgate=m_b_fgate, gn_a=m_gn_a, gn_b=m_gn_b,
             w_out=m_w_out[0], ln1_g=m_ln1_g, ln1_b=m_ln1_b, w_up=m_w_up[0], conv_w=m_conv_w[0], conv_b=m_conv_b,
             w_down=m_w_down[0], ln2_g=m_ln2_g, ln2_b=m_ln2_b)
    v = dict(w_ada=v_w_ada[0], b_ada=v_b_ada, w_in=v_w_in[0], b_fgate=v_b_fgate, gn_a=v_gn_a, gn_b=v_gn_b,
             w_out=v_w_out[0], ln1_g=v_ln1_g, ln1_b=v_ln1_b, w_up=v_w_up[0], conv_w=v_conv_w[0], conv_b=v_conv_b,
             w_down=v_w_down[0], ln2_g=v_ln2_g, ln2_b=v_ln2_b)

    nbat = x.shape[0]
    me = 4 * lax.axis_index("x") + 2 * lax.axis_index("y") + lax.axis_index("c")
    ada_cols = w["w_ada"].shape[1]

    c_all, w_in_all = _gather_two_level([c, _payload("w_in", w["w_in"])], "weight_gather")
    c_all = c_all.reshape(N_DEV * nbat, D_MODEL)
    ada_mine = _ada_fwd(c_all, w["w_ada"], lax.dynamic_slice(b_ada, (0, me * ada_cols), (1, ada_cols)))
    (ada_parts,) = _exchange([ada_mine.reshape(N_DEV, nbat, ada_cols)], [False], "ada_exchange")
    ada3 = ada_parts.transpose(1, 0, 2).reshape(nbat, 6, D_MODEL)

    loss_lanes, grad_x, g_local, parts = _local_step(
        x, positions, loss_target, ada3, _full_from_gathered("w_in", w_in_all), b_fgate, gn_a, gn_b, ln1_g, ln1_b,
        conv_b, ln2_g, ln2_b, {n: _payload(n, w[n]) for n in LATE})

    parts["w_in"], dada_all, small_all = _exchange(
        [_payload("w_in", _dest_major("w_in", g_local["w_in"])), g_local["dada"], _pack_small(g_local, loss_lanes)],
        [False, True, True], "grad_exchange")
    dada_cols = lax.dynamic_slice(dada_all.reshape(N_DEV * nbat, 6 * D_MODEL), (0, me * ada_cols),
                                  (N_DEV * nbat, ada_cols))
    parts["w_ada"] = _ada_bwd(c_all, dada_cols)[None]

    grad, delta, new_m, new_v = {}, {}, {}, {}
    for n in BIG:
        grad[n], delta[n], new_m[n], new_v[n] = (
            a[None] for a in _adamw(parts[n], w[n], m[n], v[n], ADAM_ROWS[n], "adamw_" + n))
    packed = _adamw(small_all, _pack_small(w), _pack_small(m), _pack_small(v), SMALL_ROWS, "adamw_small")
    for dst, pk in zip((grad, delta, new_m, new_v), packed):
        vals, lanes = _unpack_small(pk, w)
        dst.update(vals)
        if dst is grad:
            loss = jnp.sum(lanes)

    order = ("w_ada", "b_ada", "w_in", "b_fgate", "gn_a", "gn_b", "w_out", "ln1_g", "ln1_b", "w_up", "conv_w", "conv_b",
             "w_down", "ln2_g", "ln2_b")
    return (loss, grad_x, *[grad[n] for n in order], *[delta[n] for n in order], *[new_m[n] for n in order],
            *[new_v[n] for n in order])
```

```python
import functools

import numpy as np
import jax
import jax.numpy as jnp
from jax import lax
from jax.experimental import pallas as pl
from jax.experimental.pallas import tpu as pltpu

F32, BF16 = jnp.float32, jnp.bfloat16
MESH = pl.DeviceIdType.MESH
ANY = pl.BlockSpec(memory_space=pl.ANY)

D_MODEL = 1024
N_HEADS = 8
HEAD_DIM = 64
WIDTH = 512
D_FF = 2816
N_DEV = 8
ROPE_DIMS = 16
ROPE_THETA = 500000.0
ALPHA = 2.0 ** 0.25
LN_EPS = 1e-5
RMS_EPS = 1e-6
NEG = -1e30
Q_SCALE = 0.125
LOG2E = 1.4426950408889634
BLK = 128
LANES = 128
VMEM_LIMIT_BYTES = 56 * 1024 * 1024

ADAM_LR, ADAM_B1, ADAM_B2, ADAM_EPS, ADAM_WD, ADAM_STEP = 0.001, 0.9, 0.999, 1e-08, 0.01, 10


def _params(vmem=VMEM_LIMIT_BYTES):
    return pltpu.CompilerParams(vmem_limit_bytes=vmem)


def _nn(a, b):
    return jnp.dot(a, b, preferred_element_type=F32)


def _nt(a, b):
    return lax.dot_general(a, b, (((1,), (1,)), ((), ())), preferred_element_type=F32)


def _tn(a, b):
    return lax.dot_general(a, b, (((0,), (0,)), ((), ())), preferred_element_type=F32)


def _head_mats():
    r = lax.broadcasted_iota(jnp.int32, (LANES, WIDTH), 0)
    c = lax.broadcasted_iota(jnp.int32, (LANES, WIDTH), 1)
    e = ((c >> 6) == r).astype(BF16)
    r2 = lax.broadcasted_iota(jnp.int32, (WIDTH, LANES), 0)
    c2 = lax.broadcasted_iota(jnp.int32, (WIDTH, LANES), 1)
    et = ((r2 >> 6) == c2).astype(BF16)
    return e, et


def _split3(x):
    hi = x.astype(BF16)
    r = x - hi.astype(F32)
    mid = r.astype(BF16)
    return hi, mid, (r - mid.astype(F32)).astype(BF16)


def _hexp(w, e):
    return sum(_nn(part, e) for part in _split3(w)[:2])


def _hsum(x, et):
    return sum(_nn(part, et) for part in _split3(x)[:2])


def _perm_matrix(rows, d, transpose):
    i = np.arange(rows)
    j = (i % (rows // d)) * d + i // (rows // d)
    p = np.zeros((rows, rows), np.float32)
    p[i, j] = 1.0
    return jnp.asarray(p.T if transpose else p, BF16)


def _permute_f32(p, x):
    return sum(_nn(p, part) for part in _split3(x))


def _store_classes(ref, y, d):
    n = y.shape[0] // d
    for r in range(d):
        ref[r] = y[r * n:(r + 1) * n, :]


def _load_classes(ref, d):
    return jnp.concatenate([ref[r] for r in range(d)], axis=0)


def _rope_tabs(pos_ref, fr_ref, sign):
    ang = pos_ref[...].astype(F32) * fr_ref[...]
    lane = lax.broadcasted_iota(jnp.int32, ang.shape, 1) & (HEAD_DIM - 1)
    m1 = lane < ROPE_DIMS // 2
    m2 = (lane >= ROPE_DIMS // 2) & (lane < ROPE_DIMS)
    cos = jnp.cos(ang)
    sin = jnp.sin(ang) * sign
    return (jnp.where(m1 | m2, cos, 1.0), jnp.where(m1, -sin, 0.0), jnp.where(m2, sin, 0.0))


def _rope(z, tabs):
    c, s1, s2 = tabs
    parts = []
    for p in range(z.shape[1] // LANES):
        zp = z[:, LANES * p:LANES * (p + 1)]
        parts.append(zp * c + pltpu.roll(zp, LANES - 8, 1) * s1 + pltpu.roll(zp, 8, 1) * s2)
    return jnp.concatenate(parts, axis=1)


def _half_masks(rows):
    lane = lax.broadcasted_iota(jnp.int32, (rows, LANES), 1)
    lo = lane < HEAD_DIM
    return lo, jnp.logical_not(lo)


def _layer_norm_bwd(dxh, xh, rstd):
    m1 = jnp.mean(dxh, axis=1, keepdims=True)
    m2 = jnp.mean(dxh * xh, axis=1, keepdims=True)
    return rstd * (dxh - m1 - xh * m2)


def _coords():
    return lax.axis_index("x"), lax.axis_index("y"), lax.axis_index("c")


def _peer(x, y, c, k):
    return (1 - x if k & 4 else x, 1 - y if k & 2 else y, 1 - c if k & 1 else c)


def _comm_sems(n):
    return [pltpu.SemaphoreType.DMA((N_DEV - 1, n)), pltpu.SemaphoreType.DMA((N_DEV - 1, n)),
            pltpu.SemaphoreType.DMA((n,))]


def _comm_copies(ins, outs, to_all, sems):
    send_sems, recv_sems, local_sems = sems
    x, y, c = _coords()
    me = 4 * x + 2 * y + c
    copies = [pltpu.make_async_copy(ins[t] if to_all[t] else ins[t].at[me], outs[t].at[me], local_sems.at[t])
              for t in range(len(ins))]
    for k in range(1, N_DEV):
        px, py, pc = _peer(x, y, c, k)
        dest = 4 * px + 2 * py + pc
        for t in range(len(ins)):
            copies.append(pltpu.make_async_remote_copy(
                src_ref=ins[t] if to_all[t] else ins[t].at[dest], dst_ref=outs[t].at[me],
                send_sem=send_sems.at[k - 1, t], recv_sem=recv_sems.at[k - 1, t],
                device_id=(px, py, pc), device_id_type=MESH))
    return copies


def _comm_out_shapes(ins, to_all):
    return [jax.ShapeDtypeStruct(((N_DEV,) + a.shape) if ta else a.shape, a.dtype) for a, ta in zip(ins, to_all)]


def _exchange(ins, to_all, name):
    n = len(ins)

    def body(*refs):
        copies = _comm_copies(refs[:n], refs[n:2 * n], to_all, refs[2 * n:])
        for cp in copies:
            cp.start()
        for cp in copies:
            cp.wait()

    return pl.pallas_call(
        body, name=name, out_shape=_comm_out_shapes(ins, to_all), in_specs=[ANY] * n, out_specs=[ANY] * n,
        scratch_shapes=_comm_sems(n),
    )(*ins)


def _gather_two_level(ins, name):
    n = len(ins)

    def body(*refs):
        srcs, outs = refs[:n], refs[n:2 * n]
        send_sems, recv_sems, local_sems = refs[2 * n:]
        x, y, c = _coords()
        me = 4 * x + 2 * y + c
        sibling = (x, y, 1 - c)
        chips = [(1 - x, y), (x, 1 - y), (1 - x, 1 - y)]
        slot = lambda px, py, pc: 4 * px + 2 * py + pc

        def copy(k, t, block, to, own=False):
            return pltpu.make_async_remote_copy(
                src_ref=srcs[t] if own else outs[t].at[block], dst_ref=outs[t].at[block],
                send_sem=send_sems.at[k, t], recv_sem=recv_sems.at[k, t], device_id=to, device_id_type=MESH)

        local = [pltpu.make_async_copy(srcs[t], outs[t].at[me], local_sems.at[t]) for t in range(n)]
        first = [copy(0, t, me, sibling, own=True) for t in range(n)]
        first += [copy(1 + j, t, me, (*chip, c), own=True) for j, chip in enumerate(chips) for t in range(n)]
        for cp in local + first:
            cp.start()
        passed = []
        for j, chip in enumerate(chips):
            for t in range(n):
                copy(1 + j, t, slot(*chip, c), (x, y, c)).wait_recv()
                cp = copy(4 + j, t, slot(*chip, c), sibling)
                cp.start()
                passed.append(cp)
        for t in range(n):
            copy(0, t, slot(x, y, 1 - c), (x, y, c)).wait_recv()
            for j, chip in enumerate(chips):
                copy(4 + j, t, slot(*chip, 1 - c), (x, y, c)).wait_recv()
        for cp in first + passed:
            cp.wait_send()
        for cp in local:
            cp.wait()

    return pl.pallas_call(
        body, name=name, out_shape=_comm_out_shapes(ins, [True] * n), in_specs=[ANY] * n, out_specs=[ANY] * n,
        scratch_shapes=_comm_sems(n),
    )(*ins)


def _adamw(parts, w, m, v, rows, name):
    n_parts, r_all, cols = parts.shape
    c1 = 1.0 - ADAM_B1 ** ADAM_STEP
    c2 = 1.0 - ADAM_B2 ** ADAM_STEP

    def body(p_ref, w_ref, m_ref, v_ref, g_ref, d_ref, mo_ref, vo_ref):
        g = p_ref[0].astype(F32)
        for s in range(1, n_parts):
            g = g + p_ref[s].astype(F32)
        mn = ADAM_B1 * m_ref[...] + (1.0 - ADAM_B1) * g
        vn = ADAM_B2 * v_ref[...] + (1.0 - ADAM_B2) * (g * g)
        m_hat = mn / c1
        v_hat = vn / c2
        g_ref[...] = g
        d_ref[...] = -ADAM_LR * (m_hat / (jnp.sqrt(v_hat) + ADAM_EPS) + ADAM_WD * w_ref[...])
        mo_ref[...] = mn
        vo_ref[...] = vn

    spec = pl.BlockSpec((rows, cols), lambda i: (i, 0))
    return pl.pallas_call(
        body, name=name, grid=(r_all // rows,),
        in_specs=[pl.BlockSpec((n_parts, rows, cols), lambda i: (0, i, 0)), spec, spec, spec],
        out_specs=[spec] * 4, out_shape=[jax.ShapeDtypeStruct((r_all, cols), F32)] * 4,
        compiler_params=_params(),
    )(parts, w, m, v)


def _matmul_tn(a, b, chunk, tk, name):
    t_all, k1 = a.shape
    n = b.shape[1]

    def body(a_ref, b_ref, o_ref):
        @pl.when(pl.program_id(0) == 0)
        def _():
            o_ref[...] = jnp.zeros_like(o_ref)
        at = a_ref[...].astype(F32).T.astype(BF16)
        for j in range(0, n, chunk):
            cs = slice(j, min(j + chunk, n))
            o_ref[:, cs] += _nn(at, b_ref[:, cs])

    return pl.pallas_call(
        body, name=name, grid=(t_all // tk,),
        in_specs=[pl.BlockSpec((tk, k1), lambda t: (t, 0)), pl.BlockSpec((tk, n), lambda t: (t, 0))],
        out_specs=pl.BlockSpec((k1, n), lambda t: (0, 0)),
        out_shape=jax.ShapeDtypeStruct((k1, n), F32), compiler_params=_params(),
    )(a, b)


def _matmul_rows(a, b, tk, name):
    r, t_all = a.shape
    n = b.shape[1]

    def body(a_ref, b_ref, o_ref):
        @pl.when(pl.program_id(0) == 0)
        def _():
            o_ref[...] = jnp.zeros_like(o_ref)
        o_ref[...] += _nn(a_ref[...], b_ref[...])

    return pl.pallas_call(
        body, name=name, grid=(t_all // tk,),
        in_specs=[pl.BlockSpec((r, tk), lambda t: (0, t)), pl.BlockSpec((tk, n), lambda t: (t, 0))],
        out_specs=pl.BlockSpec((r, n), lambda t: (0, 0)),
        out_shape=jax.ShapeDtypeStruct((r, n), F32), compiler_params=_params(),
    )(a, b)


def _ada_fwd(c_all, w_ada, b_ada):
    whole = lambda a: pl.BlockSpec(a.shape, lambda j: (0, 0))

    def body(c_ref, w_ref, b_ref, o_ref):
        cv = c_ref[...]
        s = (cv * jax.nn.sigmoid(cv)).astype(BF16)
        o_ref[...] = _nn(s, w_ref[...].astype(BF16)) + b_ref[...]

    out = jax.ShapeDtypeStruct((c_all.shape[0], w_ada.shape[1]), F32)
    return pl.pallas_call(
        body, name="ada_fwd", grid=(1,), in_specs=[whole(c_all), whole(w_ada), whole(b_ada)], out_specs=whole(out),
        out_shape=out, compiler_params=_params(),
    )(c_all, w_ada, b_ada)


def _ada_bwd(c_all, dada):
    whole = lambda a: pl.BlockSpec(a.shape, lambda j: (0, 0))

    def body(c_ref, d_ref, o_ref):
        cv = c_ref[...]
        s = (cv * jax.nn.sigmoid(cv)).astype(BF16)
        o_ref[...] = _tn(s, d_ref[...].astype(BF16))

    out = jax.ShapeDtypeStruct((D_MODEL, dada.shape[1]), F32)
    return pl.pallas_call(
        body, name="ada_bwd", grid=(1,), in_specs=[whole(c_all), whole(dada)], out_specs=whole(out), out_shape=out,
        compiler_params=_params(),
    )(c_all, dada)


TOK_TM = 256
DILATIONS = (1, 4, 16)


def _class_spec(d, width, nts):
    return pl.BlockSpec((d, TOK_TM // d, width), lambda i: (i // nts, i % nts, 0))


def _class_shape(t_all, seq, d, width, dtype):
    return jax.ShapeDtypeStruct((t_all // seq * d, seq // d, width), dtype)


def _inproj(x, ada3, pos, wqkv, wf16, freq, perms, seq):
    t_all = x.shape[0]
    tm = TOK_TM
    nts = seq // tm

    def body(x_ref, ada_ref, pos_ref, w_ref, wf_ref, fr_ref, p4_ref, p16_ref, h1_ref, za_ref, zb_ref, zb4_ref,
             zb16_ref, vt_ref, fa_ref):
        h1 = (x_ref[...] * (1.0 + ada_ref[0, 1:2, :]) + ada_ref[0, 0:1, :]).astype(BF16)
        h1_ref[...] = h1
        tabs = _rope_tabs(pos_ref, fr_ref, 1.0)
        for n in range(6):
            z = _nn(h1, w_ref[:, n * WIDTH:(n + 1) * WIDTH])
            if n in (3, 4):
                z = _rope(z, tabs)
            if n in (0, 3):
                z = z * (Q_SCALE * LOG2E)
            if n == 2:
                vt_ref[...] = z.T.astype(BF16)
            dst = za_ref if n < 3 else zb_ref
            dst[:, (n % 3) * WIDTH:(n % 3 + 1) * WIDTH] = z.astype(BF16)
        fa_ref[...] = _nt(wf_ref[...], h1)[:N_HEADS]
        zb = zb_ref[...]
        _store_classes(zb4_ref, _nn(p4_ref[...], zb).astype(BF16), 4)
        _store_classes(zb16_ref, _nn(p16_ref[...], zb).astype(BF16), 16)

    tok = lambda w: pl.BlockSpec((tm, w), lambda i: (i, 0))
    whole = lambda a: pl.BlockSpec(a.shape, lambda i: (0, 0))
    return pl.pallas_call(
        body, name="inproj", grid=(t_all // tm,),
        in_specs=[tok(D_MODEL), pl.BlockSpec((1, 6, D_MODEL), lambda i: (i // nts, 0, 0)), tok(1), whole(wqkv),
                  whole(wf16), pl.BlockSpec((1, LANES), lambda i: (0, 0)), whole(perms[0]), whole(perms[1])],
        out_specs=[tok(D_MODEL), tok(3 * WIDTH), tok(3 * WIDTH), _class_spec(4, 3 * WIDTH, nts),
                   _class_spec(16, 3 * WIDTH, nts), pl.BlockSpec((WIDTH, tm), lambda i: (i // nts, i % nts)),
                   pl.BlockSpec((N_HEADS, tm), lambda i: (0, i))],
        out_shape=[jax.ShapeDtypeStruct((t_all, D_MODEL), BF16), jax.ShapeDtypeStruct((t_all, 3 * WIDTH), BF16),
                   jax.ShapeDtypeStruct((t_all, 3 * WIDTH), BF16), _class_shape(t_all, seq, 4, 3 * WIDTH, BF16),
                   _class_shape(t_all, seq, 16, 3 * WIDTH, BF16),
                   jax.ShapeDtypeStruct((t_all // seq * WIDTH, seq), BF16),
                   jax.ShapeDtypeStruct((N_HEADS, t_all), F32)],
        compiler_params=_params(),
    )(x, ada3, pos, wqkv, wf16, freq, perms[0], perms[1])


def _chunk_rows(a_t, seq):
    t_all = a_t.shape[1]
    return a_t.reshape(N_HEADS, t_all // seq, seq // LANES, LANES).transpose(1, 0, 2, 3).reshape(-1, LANES)


def _unchunk_rows(a, seq):
    nbat = a.shape[0] * LANES // (N_HEADS * seq)
    return a.reshape(nbat, N_HEADS, seq // LANES, LANES).transpose(1, 0, 2, 3).reshape(N_HEADS, nbat * seq)


def _chunk_carry(tot, nchunk, later):
    rows = tot.shape[0]
    r = lax.broadcasted_iota(jnp.int32, (rows, rows), 0)
    c = lax.broadcasted_iota(jnp.int32, (rows, rows), 1)
    sel = ((r // nchunk) == (c // nchunk)) & ((c > r) if later else (c < r))
    mat = sel.astype(BF16)
    return sum(_nn(mat, part) for part in _split3(jnp.broadcast_to(tot, (rows, LANES))))


def _fgate_fwd(fa_t, bf, seq):
    x = _chunk_rows(fa_t, seq)
    rows = x.shape[0]
    nchunk = seq // LANES
    bias = jnp.broadcast_to(bf.reshape(1, N_HEADS, 1), (rows // (N_HEADS * nchunk), N_HEADS, nchunk)).reshape(rows, 1)

    def body(x_ref, b_ref, f_ref):
        lane = lax.broadcasted_iota(jnp.int32, (rows, LANES), 1)
        xv = x_ref[...] + b_ref[...]
        lf = jnp.minimum(xv, 0.0) - jnp.log(1.0 + jnp.exp(-jnp.abs(xv)))
        for s in (1, 2, 4, 8, 16, 32, 64):
            lf = lf + jnp.where(lane >= s, pltpu.roll(lf, s, 1), 0.0)
        f_ref[...] = lf + _chunk_carry(lf[:, LANES - 1:LANES], nchunk, False)

    whole = lambda a: pl.BlockSpec(a.shape, lambda i: (0, 0))
    out = pl.pallas_call(
        body, name="fgate_fwd", grid=(1,), in_specs=[whole(x), whole(bias)], out_specs=whole(x),
        out_shape=jax.ShapeDtypeStruct(x.shape, F32), compiler_params=_params(),
    )(x, bias)
    return _unchunk_rows(out, seq)


def _fgate_bwd(df_t, fa_t, bf, seq):
    d_in = _chunk_rows(df_t, seq)
    x = _chunk_rows(fa_t, seq)
    rows = x.shape[0]
    nchunk = seq // LANES
    bias = jnp.broadcast_to(bf.reshape(1, N_HEADS, 1), (rows // (N_HEADS * nchunk), N_HEADS, nchunk)).reshape(rows, 1)

    def body(d_ref, x_ref, b_ref, o_ref, s_ref):
        lane = lax.broadcasted_iota(jnp.int32, (rows, LANES), 1)
        d = d_ref[...]
        for s in (1, 2, 4, 8, 16, 32, 64):
            d = d + jnp.where(lane < LANES - s, pltpu.roll(d, LANES - s, 1), 0.0)
        d = d + _chunk_carry(d[:, 0:1], nchunk, True)
        dfa = d * jax.nn.sigmoid(-(x_ref[...] + b_ref[...]))
        o_ref[...] = dfa
        g = lax.broadcasted_iota(jnp.int32, (2 * N_HEADS, rows), 0)
        r = lax.broadcasted_iota(jnp.int32, (2 * N_HEADS, rows), 1)
        group = (((r // nchunk) % N_HEADS) == g).astype(BF16)
        per_head = sum(_nn(group, part) for part in _split3(dfa))[:N_HEADS]
        s_ref[...] = jnp.broadcast_to(jnp.sum(per_head, axis=1, keepdims=True), (N_HEADS, LANES))

    whole = lambda a: pl.BlockSpec(a.shape, lambda i: (0, 0))
    dfa, sums = pl.pallas_call(
        body, name="fgate_bwd", grid=(1,), in_specs=[whole(d_in), whole(x), whole(bias)],
        out_specs=[whole(x), pl.BlockSpec((N_HEADS, LANES), lambda i: (0, 0))],
        out_shape=[jax.ShapeDtypeStruct(x.shape, F32), jax.ShapeDtypeStruct((N_HEADS, LANES), F32)],
        compiler_params=_params(),
    )(d_in, x, bias)
    return _unchunk_rows(dfa, seq), sums


FOX_T = 256


def _fox_prep(dst, src_ref, lo, hi):
    for p in range(4):
        v = src_ref[:, LANES * p:LANES * (p + 1)]
        dst[2 * p] = jnp.where(lo, v, jnp.zeros_like(v))
        dst[2 * p + 1] = jnp.where(hi, v, jnp.zeros_like(v))


def _fox_fwd(za, vt, f_col, seq, shards):
    t_all = za.shape[0]
    tq = FOX_T
    nq = seq // tq
    nbat = t_all // seq
    n = len(shards)
    to_all = [True] * n

    def body(*refs):
        q_ref, k_ref, vt_ref, fc_ref = refs[:4]
        o_ref, lse_ref = refs[4 + n:6 + n]
        qm_sc, m_sc, l_sc, acc_sc, a_sc, st_sc, pe_sc = refs[6 + 2 * n:13 + 2 * n]
        comm = (refs[4:4 + n], refs[6 + n:6 + 2 * n], to_all, refs[13 + 2 * n:])
        i = pl.program_id(1)

        @pl.when((pl.program_id(0) == 0) & (i == 0))
        def _():
            for cp in _comm_copies(*comm):
                cp.start()
        lo, hi = _half_masks(tq)
        r = lax.broadcasted_iota(jnp.int32, (tq, tq), 0)
        c = lax.broadcasted_iota(jnp.int32, (tq, tq), 1)
        tri = c >= r
        _fox_prep(qm_sc, q_ref, lo, hi)
        m_sc[...] = jnp.full(m_sc.shape, NEG, F32)
        l_sc[...] = jnp.zeros_like(l_sc)
        acc_sc[...] = jnp.zeros_like(acc_sc)

        def block(j, masked):
            sl = pl.ds(pl.multiple_of(j * tq, tq), tq)
            for p in range(4):
                kj = k_ref[sl, LANES * p:LANES * (p + 1)]
                for h in (2 * p, 2 * p + 1):
                    st = _nt(kj, qm_sc[h]) - fc_ref[sl, h:h + 1]
                    st_sc[h] = jnp.where(tri, st, NEG) if masked else st
            for h in range(N_HEADS):
                st = st_sc[h]
                m = m_sc[h:h + 1, :]
                mn = jnp.maximum(m, jnp.max(st, axis=0, keepdims=True))
                a = jnp.exp2(m - mn)
                pe = jnp.exp2(st - mn)
                m_sc[h:h + 1, :] = mn
                a_sc[h:h + 1, :] = a
                l_sc[h:h + 1, :] = a * l_sc[h:h + 1, :] + jnp.sum(pe, axis=0, keepdims=True)
                pe_sc[h] = pe.astype(BF16)
            for h in range(N_HEADS):
                acc_sc[h] = a_sc[h:h + 1, :] * acc_sc[h] + _nn(vt_ref[HEAD_DIM * h:HEAD_DIM * (h + 1), sl], pe_sc[h])

        def step(j, carry):
            block(j, False)
            return carry

        lax.fori_loop(0, i, step, 0)
        block(i, True)
        lse_ref[...] = m_sc[...] + jnp.log(l_sc[...]) * LOG2E
        for p in range(4):
            ot = jnp.concatenate([acc_sc[h] / l_sc[h:h + 1, :] for h in (2 * p, 2 * p + 1)], axis=0)
            o_ref[:, LANES * p:LANES * (p + 1)] = ot.T

        @pl.when((pl.program_id(0) == nbat - 1) & (i == nq - 1))
        def _():
            for cp in _comm_copies(*comm):
                cp.wait()

    res = pl.pallas_call(
        body, name="fox_fwd", grid=(nbat, nq),
        in_specs=[pl.BlockSpec((tq, WIDTH), lambda b, i: (b * nq + i, 0)),
                  pl.BlockSpec((seq, WIDTH), lambda b, i: (b, 1)), pl.BlockSpec((WIDTH, seq), lambda b, i: (b, 0)),
                  pl.BlockSpec((seq, LANES), lambda b, i: (b, 0))] + [ANY] * n,
        out_specs=[pl.BlockSpec((tq, WIDTH), lambda b, i: (b * nq + i, 0)),
                   pl.BlockSpec((N_HEADS, tq), lambda b, i: (0, b * nq + i))] + [ANY] * n,
        out_shape=[jax.ShapeDtypeStruct((t_all, WIDTH), F32), jax.ShapeDtypeStruct((N_HEADS, t_all), F32)]
        + _comm_out_shapes(shards, to_all),
        scratch_shapes=[pltpu.VMEM((N_HEADS, tq, LANES), BF16), pltpu.VMEM((N_HEADS, tq), F32),
                        pltpu.VMEM((N_HEADS, tq), F32), pltpu.VMEM((N_HEADS, HEAD_DIM, tq), F32),
                        pltpu.VMEM((N_HEADS, tq), F32), pltpu.VMEM((N_HEADS, tq, tq), F32),
                        pltpu.VMEM((N_HEADS, tq, tq), BF16)] + _comm_sems(n),
        compiler_params=_params(),
    )(za, za, vt, f_col, *shards)
    return res[0], res[1], res[2:]


def _fox_bwd(za, do, f_col, lse_row, dl_row, seq, grads):
    t_all = za.shape[0]
    tk = FOX_T
    nk = seq // tk
    nbat = t_all // seq
    n = len(grads)
    to_all = [False] * n

    def body(*refs):
        k_ref, v_ref, q_ref, do_ref, fc_ref, lr_ref, dr_ref = refs[:7]
        dk_ref, dv_ref, df_ref, dqt_ref, dfq_ref = refs[7 + n:12 + n]
        km_sc, vm_sc, fk_sc, dk_sc, dv_sc, cs_sc, kt_sc, st_sc, dp_sc, pt_sc, ds_sc = refs[12 + 2 * n:23 + 2 * n]
        comm = (refs[7:7 + n], refs[12 + n:12 + 2 * n], to_all, refs[23 + 2 * n:])
        j = pl.program_id(1)

        @pl.when(j == 0)
        def _():
            dqt_ref[...] = jnp.zeros_like(dqt_ref)
            dfq_ref[...] = jnp.zeros_like(dfq_ref)

        @pl.when((pl.program_id(0) == 0) & (j == 0))
        def _():
            for cp in _comm_copies(*comm):
                cp.start()
        lo, hi = _half_masks(tk)
        r = lax.broadcasted_iota(jnp.int32, (tk, tk), 0)
        c = lax.broadcasted_iota(jnp.int32, (tk, tk), 1)
        tri = c >= r
        _fox_prep(km_sc, k_ref, lo, hi)
        _fox_prep(vm_sc, v_ref, lo, hi)
        for h in range(N_HEADS):
            fk_sc[h] = jnp.broadcast_to(fc_ref[:, h:h + 1], (tk, tk))
        for p in range(4):
            kt_sc[p] = k_ref[:, LANES * p:LANES * (p + 1)].astype(F32).T.astype(BF16)
        dk_sc[...] = jnp.zeros_like(dk_sc)
        dv_sc[...] = jnp.zeros_like(dv_sc)
        cs_sc[...] = jnp.zeros_like(cs_sc)

        def block(i, masked):
            sl = pl.ds(pl.multiple_of(i * tk, tk), tk)
            for p in range(4):
                cs = slice(LANES * p, LANES * (p + 1))
                qi = q_ref[sl, cs]
                doi = do_ref[sl, cs]
                for h in (2 * p, 2 * p + 1):
                    st = _nt(km_sc[h], qi) - fk_sc[h] - lr_ref[h:h + 1, sl]
                    st_sc[h] = jnp.where(tri, st, NEG) if masked else st
                    dp_sc[h] = _nt(vm_sc[h], doi) - dr_ref[h:h + 1, sl]
            for h in range(N_HEADS):
                pt = jnp.exp2(st_sc[h])
                dst = pt * dp_sc[h]
                pt_sc[h] = pt.astype(BF16)
                ds_sc[h] = dst.astype(BF16)
                cs_sc[h] += dst[:, :LANES] + dst[:, LANES:]
                dfq_ref[h:h + 1, sl] += jnp.sum(dst, axis=0, keepdims=True)
            for p in range(4):
                cs = slice(LANES * p, LANES * (p + 1))
                qi = q_ref[sl, cs]
                doi = do_ref[sl, cs]
                for h in (2 * p, 2 * p + 1):
                    dv_sc[h] += _nn(pt_sc[h], doi)
                    dk_sc[h] += _nn(ds_sc[h], qi)
                    kt = kt_sc[p, HEAD_DIM * (h % 2):HEAD_DIM * (h % 2 + 1), :]
                    dqt_ref[HEAD_DIM * h:HEAD_DIM * (h + 1), sl] += _nn(kt, ds_sc[h])

        def step(i, carry):
            block(i, False)
            return carry

        block(j, True)
        lax.fori_loop(j + 1, nk, step, 0)
        df_ref[...] = jnp.zeros_like(df_ref)
        for p in range(4):
            cs = slice(LANES * p, LANES * (p + 1))
            dk_ref[:, cs] = (jnp.where(lo, dk_sc[2 * p], dk_sc[2 * p + 1]) * (1.0 / LOG2E)).astype(BF16)
            dv_ref[:, cs] = jnp.where(lo, dv_sc[2 * p], dv_sc[2 * p + 1]).astype(BF16)
            for h in (2 * p, 2 * p + 1):
                df_ref[:, h:h + 1] = -jnp.sum(cs_sc[h], axis=1, keepdims=True)

        @pl.when(j == nk - 1)
        def _():
            dqt_ref[...] = dqt_ref[...] * Q_SCALE

        @pl.when((pl.program_id(0) == nbat - 1) & (j == nk - 1))
        def _():
            for cp in _comm_copies(*comm):
                cp.wait()

    tile = lambda w, col: pl.BlockSpec((tk, w), lambda b, j: (b * nk + j, col))
    full = lambda col: pl.BlockSpec((seq, WIDTH), lambda b, j: (b, col))
    row = pl.BlockSpec((N_HEADS, seq), lambda b, j: (0, b))
    acc = pltpu.VMEM((N_HEADS, tk, LANES), F32)
    res = pl.pallas_call(
        body, name="fox_bwd", grid=(nbat, nk),
        in_specs=[tile(WIDTH, 1), tile(WIDTH, 2), full(0), full(0), tile(LANES, 0), row, row] + [ANY] * n,
        out_specs=[tile(WIDTH, 0), tile(WIDTH, 0), tile(LANES, 0), pl.BlockSpec((WIDTH, seq), lambda b, j: (b, 0)),
                   row] + [ANY] * n,
        out_shape=[jax.ShapeDtypeStruct((t_all, WIDTH), BF16), jax.ShapeDtypeStruct((t_all, WIDTH), BF16),
                   jax.ShapeDtypeStruct((t_all, LANES), F32), jax.ShapeDtypeStruct((nbat * WIDTH, seq), F32),
                   jax.ShapeDtypeStruct((N_HEADS, t_all), F32)] + _comm_out_shapes(grads, to_all),
        scratch_shapes=[pltpu.VMEM((N_HEADS, tk, LANES), BF16), pltpu.VMEM((N_HEADS, tk, LANES), BF16),
                        pltpu.VMEM((N_HEADS, tk, tk), F32), acc, acc, acc, pltpu.VMEM((4, LANES, tk), BF16),
                        pltpu.VMEM((N_HEADS, tk, tk), F32), pltpu.VMEM((N_HEADS, tk, tk), F32),
                        pltpu.VMEM((N_HEADS, tk, tk), BF16), pltpu.VMEM((N_HEADS, tk, tk), BF16)]
        + _comm_sems(n),
        compiler_params=_params(),
    )(za, za, za, do, f_col, lse_row, dl_row, *grads)
    return res[0], res[1], res[2], res[3], res[4], res[5:]


DIL_SUB = 4


def _dil_mask(has_prev):
    qi = lax.broadcasted_iota(jnp.int32, (BLK, 2 * BLK), 0)
    kj = lax.broadcasted_iota(jnp.int32, (BLK, 2 * BLK), 1)
    dist = qi + BLK - kj
    band = (dist >= 0) & (dist <= BLK)
    return band if has_prev is True else band & ((kj >= BLK) | has_prev)


def _dil_geometry(t_all, seq, d, max_sub=DIL_SUB):
    length = seq // d
    nbs = length // BLK
    sub = min(max_sub, nbs)
    spb = nbs // sub
    tile = lambda width, col: pl.BlockSpec((BLK * sub, width), lambda s: (s, col))
    whole = lambda width, col: pl.BlockSpec((length, width), lambda s: (s // spb, col))
    return nbs, sub, spb, t_all // (BLK * sub), tile, whole


def _blk(i):
    return pl.ds(pl.multiple_of(i * BLK, BLK), BLK)


def _dil_fwd(zb, seq, d):
    t_all = zb.shape[0]
    nbs, sub, spb, steps, tile, whole = _dil_geometry(t_all, seq, d)

    def body(q_ref, k_ref, v_ref, o_ref, lse_ref, s_sc, p_sc):
        first = (pl.program_id(0) % spb) * sub
        lo, hi = _half_masks(BLK)
        lse_ref[...] = jnp.zeros_like(lse_ref)
        for j in range(sub):
            blk = first + j
            mask = _dil_mask(blk != 0 if j == 0 else True)
            for p in range(4):
                cs = slice(LANES * p, LANES * (p + 1))
                qp = q_ref[BLK * j:BLK * (j + 1), cs]
                kcat = jnp.concatenate([k_ref[_blk(jnp.maximum(blk - 1, 0)), cs], k_ref[_blk(blk), cs]], axis=0)
                for e in (0, 1):
                    qe = jnp.where(lo if e == 0 else hi, qp, jnp.zeros_like(qp))
                    s_sc[N_HEADS * j + 2 * p + e] = jnp.where(mask, _nt(qe, kcat), NEG)
        inv = []
        for i in range(N_HEADS * sub):
            s = s_sc[i]
            m = jnp.max(s, axis=1, keepdims=True)
            pe = jnp.exp2(s - m)
            l = jnp.sum(pe, axis=1, keepdims=True)
            p_sc[i] = pe.astype(BF16)
            inv.append(1.0 / l)
            j, h = divmod(i, N_HEADS)
            lse_ref[BLK * j:BLK * (j + 1), h:h + 1] = m + jnp.log(l) * LOG2E
        for j in range(sub):
            blk = first + j
            for p in range(4):
                cs = slice(LANES * p, LANES * (p + 1))
                vcat = jnp.concatenate([v_ref[_blk(jnp.maximum(blk - 1, 0)), cs], v_ref[_blk(blk), cs]], axis=0)
                res = [_nn(p_sc[N_HEADS * j + h], vcat) * inv[N_HEADS * j + h] for h in (2 * p, 2 * p + 1)]
                o_ref[BLK * j:BLK * (j + 1), cs] = jnp.where(lo, res[0], res[1]).astype(BF16)

    return pl.pallas_call(
        body, name=f"dil_fwd_{d}", grid=(steps,), in_specs=[tile(WIDTH, 0), whole(WIDTH, 1), whole(WIDTH, 2)],
        out_specs=[tile(WIDTH, 0), tile(LANES, 0)],
        out_shape=[jax.ShapeDtypeStruct((t_all, WIDTH), BF16), jax.ShapeDtypeStruct((t_all, LANES), F32)],
        scratch_shapes=[pltpu.VMEM((N_HEADS * sub, BLK, 2 * BLK), F32),
                        pltpu.VMEM((N_HEADS * sub, BLK, 2 * BLK), BF16)],
        compiler_params=_params(),
    )(zb, zb, zb)


def _dil_bwd(zb, do, lse, dl, seq, d):
    t_all = zb.shape[0]
    length = seq // d
    nbs, sub, spb, steps, tile, whole = _dil_geometry(t_all, seq, d, 2 if length >= 4096 else DIL_SUB)

    def body(k_ref, v_ref, q_ref, do_ref, lse_ref, dl_ref, dq_ref, dk_ref, dv_ref, s_sc, dp_sc, pt_sc, ds_sc, kt_sc,
             dqt_sc):
        step = pl.program_id(0) % spb
        first = step * sub

        @pl.when(step == 0)
        def _():
            dqt_sc[...] = jnp.zeros_like(dqt_sc)
        r = lax.broadcasted_iota(jnp.int32, (BLK, 2 * BLK), 0)
        c = lax.broadcasted_iota(jnp.int32, (BLK, 2 * BLK), 1)
        same = (c < BLK) & (c >= r)
        later = (c >= BLK) & (c - BLK <= r)
        lo, hi = _half_masks(BLK)
        for j in range(sub):
            blk = first + j
            rows = slice(BLK * j, BLK * (j + 1))
            nxt = _blk(jnp.minimum(blk + 1, nbs - 1))
            mask = same | (later & (blk + 1 != nbs)) if j == sub - 1 else same | later
            lrows = jnp.concatenate([lse_ref[_blk(blk), :].T, lse_ref[nxt, :].T], axis=1)
            erows = jnp.concatenate([dl_ref[_blk(blk), :].T, dl_ref[nxt, :].T], axis=1)
            for p in range(4):
                cs = slice(LANES * p, LANES * (p + 1))
                kp = k_ref[rows, cs]
                vp = v_ref[rows, cs]
                kt_sc[4 * j + p] = kp.astype(F32).T.astype(BF16)
                qcat = jnp.concatenate([q_ref[_blk(blk), cs], q_ref[nxt, cs]], axis=0)
                dcat = jnp.concatenate([do_ref[_blk(blk), cs], do_ref[nxt, cs]], axis=0)
                for e in (0, 1):
                    h = 2 * p + e
                    sel = lo if e == 0 else hi
                    ke = jnp.where(sel, kp, jnp.zeros_like(kp))
                    ve = jnp.where(sel, vp, jnp.zeros_like(vp))
                    s_sc[N_HEADS * j + h] = jnp.where(mask, _nt(ke, qcat) - lrows[h:h + 1, :], NEG)
                    dp_sc[N_HEADS * j + h] = _nt(ve, dcat) - erows[h:h + 1, :]
        for i in range(N_HEADS * sub):
            pt = jnp.exp2(s_sc[i])
            pt_sc[i] = pt.astype(BF16)
            ds_sc[i] = (pt * dp_sc[i]).astype(BF16)
        for j in range(sub):
            blk = first + j
            rows = slice(BLK * j, BLK * (j + 1))
            nxt = _blk(jnp.minimum(blk + 1, nbs - 1))
            cols = pl.ds(pl.multiple_of(blk * BLK, BLK), 2 * BLK)
            for p in range(4):
                cs = slice(LANES * p, LANES * (p + 1))
                qcat = jnp.concatenate([q_ref[_blk(blk), cs], q_ref[nxt, cs]], axis=0)
                dcat = jnp.concatenate([do_ref[_blk(blk), cs], do_ref[nxt, cs]], axis=0)
                i = N_HEADS * j + 2 * p
                dk_ref[rows, cs] = (jnp.where(lo, _nn(ds_sc[i], qcat), _nn(ds_sc[i + 1], qcat))
                                    * (1.0 / LOG2E)).astype(BF16)
                dv_ref[rows, cs] = jnp.where(lo, _nn(pt_sc[i], dcat), _nn(pt_sc[i + 1], dcat)).astype(BF16)
                for e in (0, 1):
                    kt = kt_sc[4 * j + p, HEAD_DIM * e:HEAD_DIM * (e + 1), :]
                    dqt_sc[HEAD_DIM * (2 * p + e):HEAD_DIM * (2 * p + e + 1), cols] += _nn(kt, ds_sc[i + e])

        @pl.when(step == spb - 1)
        def _():
            for p in range(4):
                cs = slice(LANES * p, LANES * (p + 1))
                dq_ref[:, cs] = (dqt_sc[cs, 0:length].T * Q_SCALE).astype(BF16)

    wide = pltpu.VMEM((N_HEADS * sub, BLK, 2 * BLK), F32)
    half = pltpu.VMEM((N_HEADS * sub, BLK, 2 * BLK), BF16)
    return pl.pallas_call(
        body, name=f"dil_bwd_{d}", grid=(steps,),
        in_specs=[tile(WIDTH, 1), tile(WIDTH, 2), whole(WIDTH, 0), whole(WIDTH, 0), whole(LANES, 0), whole(LANES, 0)],
        out_specs=[whole(WIDTH, 0), tile(WIDTH, 0), tile(WIDTH, 0)],
        out_shape=[jax.ShapeDtypeStruct((t_all, WIDTH), BF16)] * 3,
        scratch_shapes=[wide, wide, half, half, pltpu.VMEM((4 * sub, LANES, BLK), BF16),
                        pltpu.VMEM((WIDTH, length + BLK), F32)],
        compiler_params=_params(),
    )(zb, zb, zb, do, lse, dl)


def _mix_out(oa, o3, l3, gn_a, gn_b, w_out, x, ada3, ln_g, ln_b, perms, seq):
    t_all = x.shape[0]
    tm = TOK_TM
    nts = seq // tm

    def body(oa_ref, o1_ref, o2_ref, o3_ref, l1_ref, l2_ref, l3_ref, ga_ref, gb_ref, w_ref, x_ref, ada_ref, g_ref,
             b_ref, p4_ref, p16_ref, pt4_ref, pt16_ref, ob_ref, lse_ref, lse4_ref, lse16_ref, mg_ref, mix_ref, xh_ref,
             rs_ref, h2_ref, h2t_ref):
        e, et = _head_mats()
        la = l1_ref[...]
        lb = _permute_f32(pt4_ref[...], _load_classes(l2_ref, 4))
        lc = _permute_f32(pt16_ref[...], _load_classes(l3_ref, 16))
        mx = jnp.maximum(jnp.maximum(la, lb), lc)
        ea, eb, ec = jnp.exp2(la - mx), jnp.exp2(lb - mx), jnp.exp2(lc - mx)
        tot = ea + eb + ec
        lse = mx + jnp.log(tot) * LOG2E
        lse_ref[...] = lse
        _store_classes(lse4_ref, _permute_f32(p4_ref[...], lse), 4)
        _store_classes(lse16_ref, _permute_f32(p16_ref[...], lse), 16)
        ob = (o1_ref[...].astype(F32) * _hexp(ea / tot, e)
              + _nn(pt4_ref[...], _load_classes(o2_ref, 4)) * _hexp(eb / tot, e)
              + _nn(pt16_ref[...], _load_classes(o3_ref, 16)) * _hexp(ec / tot, e))
        ob_ref[...] = ob

        def rms(o, gain):
            rr = lax.rsqrt(_hsum(o * o, et) * (1.0 / HEAD_DIM) + RMS_EPS)
            return o * _hexp(rr, e) * gain

        merged = jnp.concatenate([rms(oa_ref[...], ga_ref[...]), rms(ob, gb_ref[...])], axis=1).astype(BF16)
        mg_ref[...] = merged
        mix = _nn(merged, w_ref[...])
        mix_ref[...] = mix.astype(BF16)
        r1 = ALPHA * x_ref[...] + ada_ref[0, 2:3, :] * mix
        d = r1 - jnp.mean(r1, axis=1, keepdims=True)
        rstd = lax.rsqrt(jnp.mean(d * d, axis=1, keepdims=True) + LN_EPS)
        xh = d * rstd
        xh_ref[...] = xh
        rs_ref[...] = jnp.broadcast_to(rstd, (tm, LANES))
        x1 = xh * g_ref[...] + b_ref[...]
        h2 = x1 * (1.0 + ada_ref[0, 4:5, :]) + ada_ref[0, 3:4, :]
        h2_ref[...] = h2.astype(BF16)
        h2t_ref[0] = h2.T.astype(BF16)

    tok = lambda w: pl.BlockSpec((tm, w), lambda i: (i, 0))
    vec = lambda w: pl.BlockSpec((1, w), lambda i: (0, 0))
    whole = lambda a: pl.BlockSpec(a.shape, lambda i: (0, 0))
    classes = lambda a, d: a.reshape(t_all // seq * d, seq // d, a.shape[-1])
    return pl.pallas_call(
        body, name="mix_out", grid=(t_all // tm,),
        in_specs=[tok(WIDTH), tok(WIDTH), _class_spec(4, WIDTH, nts), _class_spec(16, WIDTH, nts), tok(LANES),
                  _class_spec(4, LANES, nts), _class_spec(16, LANES, nts), vec(WIDTH), vec(WIDTH), whole(w_out),
                  tok(D_MODEL), pl.BlockSpec((1, 6, D_MODEL), lambda i: (i // nts, 0, 0)), vec(D_MODEL), vec(D_MODEL)]
        + [whole(p) for p in perms],
        out_specs=[tok(WIDTH), tok(LANES), _class_spec(4, LANES, nts), _class_spec(16, LANES, nts), tok(D_MODEL),
                   tok(D_MODEL), tok(D_MODEL), tok(LANES), tok(D_MODEL), pl.BlockSpec((1, D_MODEL, tm), lambda i: (i // (FFN_TM // tm), 0, i % (FFN_TM // tm)))],
        out_shape=[jax.ShapeDtypeStruct((t_all, WIDTH), F32), jax.ShapeDtypeStruct((t_all, LANES), F32),
                   _class_shape(t_all, seq, 4, LANES, F32), _class_shape(t_all, seq, 16, LANES, F32),
                   jax.ShapeDtypeStruct((t_all, D_MODEL), BF16), jax.ShapeDtypeStruct((t_all, D_MODEL), BF16),
                   jax.ShapeDtypeStruct((t_all, D_MODEL), F32), jax.ShapeDtypeStruct((t_all, LANES), F32),
                   jax.ShapeDtypeStruct((t_all, D_MODEL), BF16),
                   jax.ShapeDtypeStruct((t_all // FFN_TM, D_MODEL, FFN_TM), BF16)],
        compiler_params=_params(),
    )(oa, o3[0], classes(o3[1], 4), classes(o3[2], 16), l3[0], classes(l3[1], 4), classes(l3[2], 16), gn_a, gn_b,
      w_out, x, ada3, ln_g, ln_b, *perms)


def _mix_out_bwd(dmix, w_out, oa, ob, gn_a, gn_b, perms, seq):
    t_all = dmix.shape[0]
    tm = TOK_TM
    nts = seq // tm

    def body(dm_ref, w_ref, oa_ref, ob_ref, ga_ref, gb_ref, p4_ref, p16_ref, doa_ref, dob_ref, dob4_ref, dob16_ref,
             dla_ref, dlb_ref, dlb4_ref, dlb16_ref, acc_ref):
        @pl.when(pl.program_id(0) == 0)
        def _():
            acc_ref[...] = jnp.zeros_like(acc_ref)
        e, et = _head_mats()
        dmg = _nt(dm_ref[...], w_ref[...])

        def group(o, dn, gain):
            rr = lax.rsqrt(_hsum(o * o, et) * (1.0 / HEAD_DIM) + RMS_EPS)
            re = _hexp(rr, e)
            dgain = jnp.sum(dn * o * re, axis=0, keepdims=True)
            dxn = dn * gain
            tt = _hsum(dxn * o, et) * (rr * rr * rr) * (1.0 / HEAD_DIM)
            do = re * dxn - o * _hexp(tt, e)
            return do, _hsum(do * o, et), dgain

        doa, dla, dga = group(oa_ref[...], dmg[:, :WIDTH], ga_ref[...])
        dob, dlb, dgb = group(ob_ref[...], dmg[:, WIDTH:], gb_ref[...])
        dob = dob.astype(BF16)
        doa_ref[...] = doa.astype(BF16)
        dob_ref[...] = dob
        _store_classes(dob4_ref, _nn(p4_ref[...], dob).astype(BF16), 4)
        _store_classes(dob16_ref, _nn(p16_ref[...], dob).astype(BF16), 16)
        dla_ref[...] = dla
        dlb_ref[...] = dlb
        _store_classes(dlb4_ref, _permute_f32(p4_ref[...], dlb), 4)
        _store_classes(dlb16_ref, _permute_f32(p16_ref[...], dlb), 16)
        acc_ref[0:1, :] += jnp.concatenate([dga, dgb], axis=1)

    tok = lambda w: pl.BlockSpec((tm, w), lambda i: (i, 0))
    vec = lambda w: pl.BlockSpec((1, w), lambda i: (0, 0))
    return pl.pallas_call(
        body, name="mix_out_bwd", grid=(t_all // tm,),
        in_specs=[tok(D_MODEL), pl.BlockSpec(w_out.shape, lambda i: (0, 0)), tok(WIDTH), tok(WIDTH), vec(WIDTH),
                  vec(WIDTH), pl.BlockSpec(perms[0].shape, lambda i: (0, 0)),
                  pl.BlockSpec(perms[1].shape, lambda i: (0, 0))],
        out_specs=[tok(WIDTH), tok(WIDTH), _class_spec(4, WIDTH, nts), _class_spec(16, WIDTH, nts), tok(LANES),
                   tok(LANES), _class_spec(4, LANES, nts), _class_spec(16, LANES, nts),
                   pl.BlockSpec((8, D_MODEL), lambda i: (0, 0))],
        out_shape=[jax.ShapeDtypeStruct((t_all, WIDTH), BF16), jax.ShapeDtypeStruct((t_all, WIDTH), BF16),
                   _class_shape(t_all, seq, 4, WIDTH, BF16), _class_shape(t_all, seq, 16, WIDTH, BF16),
                   jax.ShapeDtypeStruct((t_all, LANES), F32), jax.ShapeDtypeStruct((t_all, LANES), F32),
                   _class_shape(t_all, seq, 4, LANES, F32), _class_shape(t_all, seq, 16, LANES, F32),
                   jax.ShapeDtypeStruct((8, D_MODEL), F32)],
        compiler_params=_params(),
    )(dmix, w_out, oa, ob, gn_a, gn_b, perms[0], perms[1])


def _inproj_bwd(dqt, dka, dva, dil1, dil4, dil16, dfa16, pos, wqkv, wf16, freq, perms, dr1, x, ada3, seq):
    t_all = x.shape[0]
    tm = TOK_TM
    nts = seq // tm

    def body(dqt_ref, dka_ref, dva_ref, q1_ref, k1_ref, v1_ref, q4_ref, k4_ref, v4_ref, q16_ref, k16_ref, v16_ref,
             dfa_ref, pos_ref, w_ref, wf_ref, fr_ref, pt4_ref, pt16_ref, dr1_ref, x_ref, ada_ref, gx_ref, dz_ref,
             acc_ref):
        i = pl.program_id(0)

        @pl.when(i == 0)
        def _():
            acc_ref[...] = jnp.zeros_like(acc_ref)
        tabs = _rope_tabs(pos_ref, fr_ref, -1.0)
        dz_ref[:, :WIDTH] = dqt_ref[...].T.astype(BF16)
        dz_ref[:, WIDTH:2 * WIDTH] = dka_ref[...]
        dz_ref[:, 2 * WIDTH:3 * WIDTH] = dva_ref[...]
        for t, (n1, n4, n16) in enumerate(((q1_ref, q4_ref, q16_ref), (k1_ref, k4_ref, k16_ref),
                                           (v1_ref, v4_ref, v16_ref))):
            tot = (n1[...].astype(F32) + _nn(pt4_ref[...], _load_classes(n4, 4))
                   + _nn(pt16_ref[...], _load_classes(n16, 16)))
            if t < 2:
                tot = _rope(tot, tabs)
            dz_ref[:, (3 + t) * WIDTH:(4 + t) * WIDTH] = tot.astype(BF16)
        dh1 = _tn(dfa_ref[...], wf_ref[...])
        for n in range(6):
            cs = slice(n * WIDTH, (n + 1) * WIDTH)
            dh1 = dh1 + _nt(dz_ref[:, cs], w_ref[:, cs])
        xv = x_ref[...]
        gx_ref[...] = ALPHA * dr1_ref[...] + dh1 * (1.0 + ada_ref[0, 1:2, :])
        b = i // nts
        acc_ref[pl.ds(b, 1), :] += jnp.sum(dh1 * xv, axis=0, keepdims=True)
        acc_ref[pl.ds(8 + b, 1), :] += jnp.sum(dh1, axis=0, keepdims=True)

    tok = lambda w: pl.BlockSpec((tm, w), lambda i: (i, 0))
    whole = lambda a: pl.BlockSpec(a.shape, lambda i: (0, 0))
    classes = lambda a, d: a.reshape(t_all // seq * d, seq // d, a.shape[-1])
    return pl.pallas_call(
        body, name="inproj_bwd", grid=(t_all // tm,),
        in_specs=[pl.BlockSpec((WIDTH, tm), lambda i: (i // nts, i % nts)), tok(WIDTH), tok(WIDTH)]
        + [tok(WIDTH)] * 3 + [_class_spec(4, WIDTH, nts)] * 3 + [_class_spec(16, WIDTH, nts)] * 3
        + [pl.BlockSpec((16, tm), lambda i: (0, i)), tok(1), whole(wqkv), whole(wf16),
           pl.BlockSpec((1, LANES), lambda i: (0, 0)), whole(perms[2]), whole(perms[3]), tok(D_MODEL), tok(D_MODEL),
           pl.BlockSpec((1, 6, D_MODEL), lambda i: (i // nts, 0, 0))],
        out_specs=[tok(D_MODEL), tok(6 * WIDTH), pl.BlockSpec((16, D_MODEL), lambda i: (0, 0))],
        out_shape=[jax.ShapeDtypeStruct((t_all, D_MODEL), F32), jax.ShapeDtypeStruct((t_all, 6 * WIDTH), BF16),
                   jax.ShapeDtypeStruct((16, D_MODEL), F32)],
        compiler_params=_params(),
    )(dqt, dka, dva, *dil1, *[classes(a, 4) for a in dil4], *[classes(a, 16) for a in dil16], dfa16, pos, wqkv, wf16,
      freq, perms[2], perms[3], dr1, x, ada3)


FFN_TM = 1024
FFN_TN = 256
HALO = 8


FFN_CHUNK = 256


def _conv_params(cw_ref, cb_ref, n, tn):
    a = pl.ds(pl.multiple_of(n * tn, tn), tn)
    g = pl.ds(pl.multiple_of(D_FF + n * tn, tn), tn)
    return cw_ref[:, a], cw_ref[:, g], cb_ref[:, a], cb_ref[:, g]


def _conv(cat_ref, w_ref, b_ref, start, rows, halo=HALO):
    return (b_ref[...] + w_ref[0:1, :] * cat_ref[pl.ds(start + halo - 2, rows), :]
            + w_ref[1:2, :] * cat_ref[pl.ds(start + halo - 1, rows), :]
            + w_ref[2:3, :] * cat_ref[pl.ds(start + halo, rows), :])


def _ffn_up_gate(h2, w_up, conv_w, conv_b, seq):
    t_all = h2.shape[0]
    tm, tn = FFN_TM, FFN_TN
    nc = D_FF // tn
    nts = seq // tm
    pre = 16

    def body(h_ref, hp_ref, wua_ref, wug_ref, cw_ref, cb_ref, ua_ref, ug_ref, o_ref, ca_ref, cg_ref):
        first = (pl.program_id(1) % nts) == 0
        wa_ref, wg_ref, ba_ref, bg_ref = _conv_params(cw_ref, cb_ref, pl.program_id(0), tn)
        hcat = jnp.concatenate([hp_ref[...], h_ref[...]], axis=0)
        zero = jnp.zeros((pre, tn), F32)
        for w_ref, cat, u_ref in ((wua_ref, ca_ref, ua_ref), (wug_ref, cg_ref, ug_ref)):
            ub = _nn(hcat, w_ref[...]).astype(BF16)
            ue = ub.astype(F32)
            cat[0:pre, :] = jnp.where(first, zero, ue[0:pre])
            cat[pre:, :] = ue[pre:]
            u_ref[...] = ub[pre:]
        for c0 in range(0, tm, FFN_CHUNK):
            ya = _conv(ca_ref, wa_ref, ba_ref, c0, FFN_CHUNK, pre)
            yg = _conv(cg_ref, wg_ref, bg_ref, c0, FFN_CHUNK, pre)
            o_ref[c0:c0 + FFN_CHUNK, :] = (yg * jax.nn.sigmoid(yg) * ya).astype(BF16)

    wcol = lambda off: pl.BlockSpec((D_MODEL, tn), lambda n, t: (0, n + off))
    tile = pl.BlockSpec((tm, tn), lambda n, t: (t, n))
    return pl.pallas_call(
        body, name="ffn_up_gate", grid=(nc, t_all // tm),
        in_specs=[pl.BlockSpec((tm, D_MODEL), lambda n, t: (t, 0)),
                  pl.BlockSpec((pre, D_MODEL), lambda n, t: (jnp.maximum(t * (tm // pre) - 1, 0), 0)),
                  wcol(0), wcol(nc), pl.BlockSpec(conv_w.shape, lambda n, t: (0, 0)),
                  pl.BlockSpec(conv_b.shape, lambda n, t: (0, 0))],
        out_specs=[tile, tile, tile],
        out_shape=[jax.ShapeDtypeStruct((t_all, D_FF), BF16)] * 3,
        scratch_shapes=[pltpu.VMEM((tm + pre, tn), F32)] * 2, compiler_params=_params(),
    )(h2, h2, w_up, w_up, conv_w, conv_b)


def _ffn_gate_bwd(u_a, u_g, dfi, conv_w, conv_b, h2t, seq):
    t_all = u_a.shape[0]
    tm, tn = FFN_TM, FFN_TN
    nc = D_FF // tn
    nts = seq // tm

    def body(ua_ref, uap_ref, uan_ref, ug_ref, ugp_ref, ugn_ref, df_ref, dfn_ref, cw_ref, cb_ref, h_ref,
             dua_ref, dug_ref, acca_ref, accg_ref, dwa_ref, dwg_ref, ca_ref, cg_ref, ya_ref, yg_ref, dwa_sc, dwg_sc,
             out_sems):
        t = pl.program_id(0)
        n = pl.program_id(1)
        cols = pl.ds(pl.multiple_of(n * tn, tn), tn)
        first = (t % nts) == 0
        last = (t % nts) == nts - 1

        @pl.when((t == 0) & (n == 0))
        def _():
            acca_ref[...] = jnp.zeros_like(acca_ref)
            accg_ref[...] = jnp.zeros_like(accg_ref)
            dwa_sc[...] = jnp.zeros_like(dwa_sc)
            dwg_sc[...] = jnp.zeros_like(dwg_sc)
        wa_ref, wg_ref, ba_ref, bg_ref = _conv_params(cw_ref, cb_ref, n, tn)
        zero = jnp.zeros((HALO, tn), F32)
        for cat, cur, prv, nxt in ((ca_ref, ua_ref, uap_ref, uan_ref), (cg_ref, ug_ref, ugp_ref, ugn_ref)):
            cat[0:HALO, :] = jnp.where(first, zero, prv[...].astype(F32)[HALO:])
            cat[HALO:HALO + tm, :] = cur[...].astype(F32)
            cat[HALO + tm:, :] = nxt[...].astype(F32)[:HALO]
        ch = FFN_CHUNK
        sums = [[jnp.zeros((1, tn), F32) for _ in range(4)] for _ in range(2)]
        for ci, c0 in enumerate(range(0, tm, ch)):
            ya = _conv(ca_ref, wa_ref, ba_ref, c0, ch + HALO)
            yg = _conv(cg_ref, wg_ref, bg_ref, c0, ch + HALO)
            if c0 + ch < tm:
                beyond = df_ref[c0 + ch:c0 + ch + 16, :].astype(F32)[:HALO]
            else:
                beyond = jnp.where(last, 0.0, dfn_ref[...].astype(F32)[:HALO])
            dfe = jnp.concatenate([df_ref[c0:c0 + ch, :].astype(F32), beyond], axis=0)
            sg = jax.nn.sigmoid(yg)
            ya_ref[ci] = dfe * (yg * sg)
            yg_ref[ci] = dfe * ya * (sg * (1.0 + yg * (1.0 - sg)))
            for half, (dy, cat, w_ref, du_ref) in enumerate(((ya_ref, ca_ref, wa_ref, dua_ref),
                                                             (yg_ref, cg_ref, wg_ref, dug_ref))):
                d0 = dy[ci, 0:ch, :]
                du = (w_ref[2:3, :] * d0 + w_ref[1:2, :] * dy[ci, pl.ds(1, ch), :]
                      + w_ref[0:1, :] * dy[ci, pl.ds(2, ch), :])
                du_ref[c0:c0 + ch, :] = du.astype(BF16)
                for k in range(3):
                    sums[half][k] += jnp.sum(d0 * cat[pl.ds(c0 + HALO - 2 + k, ch), :], axis=0, keepdims=True)
                sums[half][3] += jnp.sum(d0, axis=0, keepdims=True)
        for half, acc in enumerate((acca_ref, accg_ref)):
            for k in range(4):
                acc[k:k + 1, cols] += sums[half][k]
        ht = h_ref[0]
        dwa_sc[:, cols] += _nn(ht, dua_ref[...])
        dwg_sc[:, cols] += _nn(ht, dug_ref[...])

        @pl.when((t == t_all // tm - 1) & (n == nc - 1))
        def _():
            copies = [pltpu.make_async_copy(dwa_sc, dwa_ref, out_sems.at[0]),
                      pltpu.make_async_copy(dwg_sc, dwg_ref, out_sems.at[1])]
            for cp in copies:
                cp.start()
            for cp in copies:
                cp.wait()

    nrow = t_all // 16
    cur = pl.BlockSpec((tm, tn), lambda t, n: (t, n))
    prev = pl.BlockSpec((16, tn), lambda t, n: (jnp.maximum(t * (tm // 16) - 1, 0), n))
    nxt = pl.BlockSpec((16, tn), lambda t, n: (jnp.minimum((t + 1) * (tm // 16), nrow - 1), n))
    acc = pl.BlockSpec((8, D_FF), lambda t, n: (0, 0))
    return pl.pallas_call(
        body, name="ffn_gate_bwd", grid=(t_all // tm, nc),
        in_specs=[cur, prev, nxt, cur, prev, nxt, cur, nxt, pl.BlockSpec(conv_w.shape, lambda t, n: (0, 0)),
                  pl.BlockSpec(conv_b.shape, lambda t, n: (0, 0)),
                  pl.BlockSpec((1, D_MODEL, tm), lambda t, n: (t, 0, 0))],
        out_specs=[cur, cur, acc, acc, ANY, ANY],
        out_shape=[jax.ShapeDtypeStruct((t_all, D_FF), BF16), jax.ShapeDtypeStruct((t_all, D_FF), BF16),
                   jax.ShapeDtypeStruct((8, D_FF), F32), jax.ShapeDtypeStruct((8, D_FF), F32),
                   jax.ShapeDtypeStruct((D_MODEL, D_FF), F32), jax.ShapeDtypeStruct((D_MODEL, D_FF), F32)],
        scratch_shapes=[pltpu.VMEM((tm + 2 * HALO, tn), F32)] * 2
        + [pltpu.VMEM((tm // FFN_CHUNK, FFN_CHUNK + HALO, tn), F32)] * 2
        + [pltpu.VMEM((D_MODEL, D_FF), F32)] * 2 + [pltpu.SemaphoreType.DMA((2,))],
        compiler_params=_params(),
    )(u_a, u_a, u_a, u_g, u_g, u_g, dfi, dfi, conv_w, conv_b, h2t)


def _ffn_down(ffn_in, w_down, xh1, ln1_g, ln1_b, ada3, ln2_g, ln2_b, target, seq):
    t_all = xh1.shape[0]
    tm = 512
    nts = seq // tm

    def body(f_ref, w_ref, xh_ref, g1_ref, b1_ref, ada_ref, g2_ref, b2_ref, tg_ref, dr2_ref, acc_ref):
        i = pl.program_id(0)

        @pl.when(i == 0)
        def _():
            acc_ref[...] = jnp.zeros_like(acc_ref)
        ffn = _nn(f_ref[...], w_ref[...])
        x1 = xh_ref[...] * g1_ref[...] + b1_ref[...]
        r2 = ALPHA * x1 + ada_ref[0, 5:6, :] * ffn
        d = r2 - jnp.mean(r2, axis=1, keepdims=True)
        rstd = lax.rsqrt(jnp.mean(d * d, axis=1, keepdims=True) + LN_EPS)
        xh2 = d * rstd
        diff = xh2 * g2_ref[...] + b2_ref[...] - tg_ref[...]
        dy = diff * (1.0 / D_MODEL)
        dr2 = _layer_norm_bwd(dy * g2_ref[...], xh2, rstd)
        dr2_ref[...] = dr2
        acc_ref[0:1, :] += jnp.sum(dy * xh2, axis=0, keepdims=True)
        acc_ref[1:2, :] += jnp.sum(dy, axis=0, keepdims=True)
        acc_ref[2:3, :] += jnp.sum(diff * diff, axis=0, keepdims=True) * (0.5 / D_MODEL)
        acc_ref[pl.ds(8 + i // nts, 1), :] += jnp.sum(dr2 * ffn, axis=0, keepdims=True)

    tok = lambda w: pl.BlockSpec((tm, w), lambda i: (i, 0))
    vec = pl.BlockSpec((1, D_MODEL), lambda i: (0, 0))
    return pl.pallas_call(
        body, name="ffn_down", grid=(t_all // tm,),
        in_specs=[tok(D_FF), pl.BlockSpec(w_down.shape, lambda i: (0, 0)), tok(D_MODEL), vec, vec,
                  pl.BlockSpec((1, 6, D_MODEL), lambda i: (i // nts, 0, 0)), vec, vec, tok(D_MODEL)],
        out_specs=[tok(D_MODEL), pl.BlockSpec((16, D_MODEL), lambda i: (0, 0))],
        out_shape=[jax.ShapeDtypeStruct((t_all, D_MODEL), F32), jax.ShapeDtypeStruct((16, D_MODEL), F32)],
        compiler_params=_params(),
    )(ffn_in, w_down, xh1, ln1_g, ln1_b, ada3, ln2_g, ln2_b, target)


def _ffn_down_bwd(dr2, ada3, w_down, seq):
    t_all = dr2.shape[0]
    tm = 512
    nts = seq // tm

    def body(d_ref, ada_ref, w_ref, dffn_ref, dfi_ref):
        dffn = (d_ref[...] * ada_ref[0, 5:6, :]).astype(BF16)
        dffn_ref[...] = dffn
        dfi_ref[...] = _nt(dffn, w_ref[...]).astype(BF16)

    tok = lambda w: pl.BlockSpec((tm, w), lambda i: (i, 0))
    return pl.pallas_call(
        body, name="ffn_down_bwd", grid=(t_all // tm,),
        in_specs=[tok(D_MODEL), pl.BlockSpec((1, 6, D_MODEL), lambda i: (i // nts, 0, 0)),
                  pl.BlockSpec(w_down.shape, lambda i: (0, 0))],
        out_specs=[tok(D_MODEL), tok(D_FF)],
        out_shape=[jax.ShapeDtypeStruct((t_all, D_MODEL), BF16), jax.ShapeDtypeStruct((t_all, D_FF), BF16)],
        compiler_params=_params(),
    )(dr2, ada3, w_down)


def _ffn_up_bwd(du_a, du_g, w_up, dr2, xh1, rs1, mix, ada3, ln1_g, ln1_b, seq):
    t_all = dr2.shape[0]
    tm = 512
    nts = seq // tm

    def body(da_ref, dg_ref, w_ref, dr2_ref, xh_ref, rs_ref, mix_ref, ada_ref, g_ref, b_ref, dr1_ref, dmix_ref,
             acc_ref):
        i = pl.program_id(0)

        @pl.when(i == 0)
        def _():
            acc_ref[...] = jnp.zeros_like(acc_ref)
        dh2 = _nt(da_ref[...], w_ref[:, :D_FF]) + _nt(dg_ref[...], w_ref[:, D_FF:])
        xh = xh_ref[...]
        x1 = xh * g_ref[...] + b_ref[...]
        dx1 = ALPHA * dr2_ref[...] + dh2 * (1.0 + ada_ref[0, 4:5, :])
        dr1 = _layer_norm_bwd(dx1 * g_ref[...], xh, rs_ref[:, 0:1])
        dr1_ref[...] = dr1
        dmix_ref[...] = (dr1 * ada_ref[0, 2:3, :]).astype(BF16)
        b = i // nts
        acc_ref[0:1, :] += jnp.sum(dx1 * xh, axis=0, keepdims=True)
        acc_ref[1:2, :] += jnp.sum(dx1, axis=0, keepdims=True)
        acc_ref[pl.ds(8 + b, 1), :] += jnp.sum(dh2 * x1, axis=0, keepdims=True)
        acc_ref[pl.ds(16 + b, 1), :] += jnp.sum(dh2, axis=0, keepdims=True)
        acc_ref[pl.ds(24 + b, 1), :] += jnp.sum(dr1 * mix_ref[...].astype(F32), axis=0, keepdims=True)

    tok = lambda w: pl.BlockSpec((tm, w), lambda i: (i, 0))
    vec = pl.BlockSpec((1, D_MODEL), lambda i: (0, 0))
    return pl.pallas_call(
        body, name="ffn_up_bwd", grid=(t_all // tm,),
        in_specs=[tok(D_FF), tok(D_FF), pl.BlockSpec(w_up.shape, lambda i: (0, 0)), tok(D_MODEL), tok(D_MODEL),
                  tok(LANES), tok(D_MODEL), pl.BlockSpec((1, 6, D_MODEL), lambda i: (i // nts, 0, 0)), vec, vec],
        out_specs=[tok(D_MODEL), tok(D_MODEL), pl.BlockSpec((32, D_MODEL), lambda i: (0, 0))],
        out_shape=[jax.ShapeDtypeStruct((t_all, D_MODEL), F32), jax.ShapeDtypeStruct((t_all, D_MODEL), BF16),
                   jax.ShapeDtypeStruct((32, D_MODEL), F32)],
        compiler_params=_params(),
    )(du_a, du_g, w_up, dr2, xh1, rs1, mix, ada3, ln1_g, ln1_b)


def _rows(a):
    return a[:, :N_HEADS].T


def _rope_freq():
    f = np.float32(ROPE_THETA) ** (-np.arange(0, ROPE_DIMS, 2, dtype=np.float32) / np.float32(ROPE_DIMS))
    return jnp.asarray(np.tile(f.astype(np.float32), LANES // (ROPE_DIMS // 2))[None, :])


def _local_step(x, positions, target, ada3, w_in, b_fgate, gn_a, gn_b, ln1_g, ln1_b, conv_b, ln2_g, ln2_b,
                late_shards):
    nbat, seq, _ = x.shape
    t_all = nbat * seq
    xf = x.reshape(t_all, D_MODEL)
    tg = target.reshape(t_all, D_MODEL)
    pos = positions.reshape(t_all, 1)
    freq = _rope_freq()

    wqkv = jnp.concatenate([w_in[:, :3 * WIDTH], w_in[:, 3 * WIDTH + N_HEADS:]], axis=1)
    wf16 = jnp.zeros((16, D_MODEL), BF16).at[:N_HEADS].set(w_in[:, 3 * WIDTH:3 * WIDTH + N_HEADS].T)
    bf = b_fgate.reshape(N_HEADS, 1)

    perms = [_perm_matrix(TOK_TM, d, tr) for tr in (False, True) for d in DILATIONS[1:]]
    h1, za, zb1, zb4, zb16, vt, fa_t = _inproj(xf, ada3, pos, wqkv, wf16, freq, perms, seq)
    zbs = [zb1, zb4.reshape(t_all, 3 * WIDTH), zb16.reshape(t_all, 3 * WIDTH)]
    f_row = _fgate_fwd(fa_t, bf, seq)
    f_col = jnp.zeros((t_all, LANES), F32).at[:, :N_HEADS].set(f_row.T * LOG2E)
    oa, lse_row_a, gathered = _fox_fwd(za, vt, f_col, seq, [late_shards[n] for n in LATE])
    w_out, w_up, conv_w, w_down = (_full_from_gathered(n, g) for n, g in zip(LATE, gathered))
    o3, l3 = zip(*[_dil_fwd(zb, seq, d) for zb, d in zip(zbs, DILATIONS)])
    ob, lse_b, lse_b4, lse_b16, merged, mix, xh1, rs1, h2, h2t = _mix_out(oa, o3, l3, gn_a, gn_b, w_out, xf, ada3, ln1_g,
                                                                      ln1_b, perms, seq)
    u_a, u_g, ffn_in = _ffn_up_gate(h2, w_up, conv_w, conv_b, seq)
    dr2, acc2 = _ffn_down(ffn_in, w_down, xh1, ln1_g, ln1_b, ada3, ln2_g, ln2_b, tg, seq)

    dffn, dfi = _ffn_down_bwd(dr2, ada3, w_down, seq)
    d_w_down = _matmul_tn(dffn, ffn_in, 512, 512, "dw_down").T
    du_a, du_g, acc_ca, acc_cg, dw_up_a, dw_up_g = _ffn_gate_bwd(u_a, u_g, dfi, conv_w, conv_b, h2t, seq)
    dr1, dmix, acc1 = _ffn_up_bwd(du_a, du_g, w_up, dr2, xh1, rs1, mix, ada3, ln1_g, ln1_b, seq)
    d_w_up = jnp.concatenate([dw_up_a, dw_up_g], axis=1)

    doa, dob, dob4, dob16, dl_a, dl_b, dl_b4, dl_b16, acc_gn = _mix_out_bwd(dmix, w_out, oa, ob, gn_a, gn_b, perms, seq)
    d_w_out = _matmul_tn(merged, dmix, 512, 512, "dw_out")
    late_grads = dict(w_out=d_w_out, w_up=d_w_up, conv_w=jnp.concatenate([acc_ca[0:3], acc_cg[0:3]], axis=1),
                      w_down=d_w_down)
    dka, dva, df_k, dqt, df_q, late_parts = _fox_bwd(za, doa, f_col, lse_row_a, _rows(dl_a), seq,
                                                     [_payload(n, _dest_major(n, late_grads[n])) for n in LATE])
    dfa_t, dbf = _fgate_bwd(_rows(df_k) + df_q, fa_t, bf, seq)
    flat = lambda a: a.reshape(t_all, a.shape[-1])
    dil = []
    for zb, d, do, lse, dl in zip(zbs, DILATIONS, (dob, flat(dob4), flat(dob16)),
                                  (lse_b, flat(lse_b4), flat(lse_b16)), (dl_b, flat(dl_b4), flat(dl_b16))):
        dil.append(_dil_bwd(zb, do, lse, dl, seq, d))
    dfa16 = jnp.zeros((16, t_all), BF16).at[:N_HEADS].set(dfa_t.astype(BF16))
    grad_x, dz, acc0 = _inproj_bwd(dqt, dka, dva, dil[0], dil[1], dil[2], dfa16, pos, wqkv, wf16, freq, perms, dr1, xf,
                                   ada3, seq)
    d_wqkv = _matmul_tn(h1, dz, 512, 512, "dw_in")
    d_wf = _matmul_rows(dfa16, h1, 512, "dw_fgate")[:N_HEADS].T
    d_w_in = jnp.concatenate([d_wqkv[:, :3 * WIDTH], d_wf, d_wqkv[:, 3 * WIDTH:]], axis=1)

    dada = jnp.concatenate([acc0[8:8 + nbat], acc0[:nbat], acc1[24:24 + nbat], acc1[16:16 + nbat], acc1[8:8 + nbat],
                            acc2[8:8 + nbat]], axis=1)

    grads = dict(
        dada=dada, b_ada=jnp.sum(dada, axis=0, keepdims=True), w_in=d_w_in, b_fgate=dbf[:, 0][None, :],
        gn_a=acc_gn[0:1, :WIDTH], gn_b=acc_gn[0:1, WIDTH:], ln1_g=acc1[0:1], ln1_b=acc1[1:2],
        conv_b=jnp.concatenate([acc_ca[3:4], acc_cg[3:4]], axis=1), ln2_g=acc2[0:1], ln2_b=acc2[1:2])
    return acc2[2:3], grad_x.reshape(x.shape), grads, dict(zip(LATE, late_parts))


LATE = ("w_out", "w_up", "conv_w", "w_down")
BIG = ("w_ada", "w_in") + LATE
COLUMN_SHARDED = ("w_ada", "w_in", "w_up", "conv_w")


def _payload(name, a):
    return a if name == "conv_w" else a.astype(BF16)
SMALL = ("b_ada", "b_fgate", "gn_a", "gn_b", "ln1_g", "ln1_b", "conv_b", "ln2_g", "ln2_b")
ADAM_ROWS = dict(w_ada=256, w_in=256, w_out=128, w_up=256, conv_w=3, w_down=176)
SMALL_ROWS = 24


def _full_from_gathered(name, g):
    if name in COLUMN_SHARDED:
        return g.transpose(1, 0, 2).reshape(g.shape[1], N_DEV * g.shape[2])
    return g.reshape(N_DEV * g.shape[1], g.shape[2])


def _dest_major(name, full):
    if name in COLUMN_SHARDED:
        r, cfull = full.shape
        return full.reshape(r, N_DEV, cfull // N_DEV).transpose(1, 0, 2)
    return full.reshape(N_DEV, full.shape[0] // N_DEV, full.shape[1])


def _pack_small(vals, extra=None):
    parts = [vals[n].reshape(-1) for n in SMALL]
    if extra is not None:
        parts.append(extra.reshape(-1))
    flat = jnp.concatenate(parts)
    return jnp.pad(flat, (0, SMALL_ROWS * D_MODEL - flat.shape[0])).reshape(SMALL_ROWS, D_MODEL)


def _unpack_small(packed, like):
    flat = packed.reshape(-1)
    out, off = {}, 0
    for n in SMALL:
        size = like[n].size
        out[n] = flat[off:off + size].reshape(like[n].shape)
        off += size
    return out, flat[off:off + D_MODEL]


def kernel(x, c, positions, w_ada, b_ada, w_in, b_fgate, gn_a, gn_b, w_out, ln1_g, ln1_b, w_up, conv_w, conv_b, w_down, ln2_g, ln2_b, loss_target, m_w_ada, m_b_ada, m_w_in, m_b_fgate, m_gn_a, m_gn_b, m_w_out, m_ln1_g, m_ln1_b, m_w_up, m_conv_w, m_conv_b, m_w_down, m_ln2_g, m_ln2_b, v_w_ada, v_b_ada, v_w_in, v_b_fgate, v_gn_a, v_gn_b, v_w_out, v_ln1_g, v_ln1_b, v_w_up, v_conv_w, v_conv_b, v_w_down, v_ln2_g, v_ln2_b):
    w = dict(w_ada=w_ada[0], b_ada=b_ada, w_in=w_in[0], b_fgate=b_fgate, gn_a=gn_a, gn_b=gn_b, w_out=w_out[0],
             ln1_g=ln1_g, ln1_b=ln1_b, w_up=w_up[0], conv_w=conv_w[0], conv_b=conv_b, w_down=w_down[0], ln2_g=ln2_g,
             ln2_b=ln2_b)
    m = dict(w_ada=m_w_ada[0], b_ada=m_b_ada, w_in=m_w_in[0], b_fgate=m_b_fgate, gn_a=m_gn_a, gn_b=m_gn_b,
             w_out=m_w_out[0], ln1_g=m_ln1_g, ln1_b=m_ln1_b, w_up=m_w_up[0], conv_w=m_conv_w[0], conv_b=m_conv_b,
             w_down=m_w_down[0], ln2_g=m_ln2_g, ln2_b=m_ln2_b)
    v = dict(w_ada=v_w_ada[0], b_ada=v_b_ada, w_in=v_w_in[0], b_fgate=v_b_fgate, gn_a=v_gn_a, gn_b=v_gn_b,
             w_out=v_w_out[0], ln1_g=v_ln1_g, ln1_b=v_ln1_b, w_up=v_w_up[0], conv_w=v_conv_w[0], conv_b=v_conv_b,
             w_down=v_w_down[0], ln2_g=v_ln2_g, ln2_b=v_ln2_b)

    nbat = x.shape[0]
    me = 4 * lax.axis_index("x") + 2 * lax.axis_index("y") + lax.axis_index("c")
    ada_cols = w["w_ada"].shape[1]

    c_all, w_in_all = _gather_two_level([c, _payload("w_in", w["w_in"])], "weight_gather")
    c_all = c_all.reshape(N_DEV * nbat, D_MODEL)
    ada_mine = _ada_fwd(c_all, w["w_ada"], lax.dynamic_slice(b_ada, (0, me * ada_cols), (1, ada_cols)))
    (ada_parts,) = _exchange([ada_mine.reshape(N_DEV, nbat, ada_cols)], [False], "ada_exchange")
    ada3 = ada_parts.transpose(1, 0, 2).reshape(nbat, 6, D_MODEL)

    loss_lanes, grad_x, g_local, parts = _local_step(
        x, positions, loss_target, ada3, _full_from_gathered("w_in", w_in_all), b_fgate, gn_a, gn_b, ln1_g, ln1_b,
        conv_b, ln2_g, ln2_b, {n: _payload(n, w[n]) for n in LATE})

    parts["w_in"], dada_all, small_all = _exchange(
        [_payload("w_in", _dest_major("w_in", g_local["w_in"])), g_local["dada"], _pack_small(g_local, loss_lanes)],
        [False, True, True], "grad_exchange")
    dada_cols = lax.dynamic_slice(dada_all.reshape(N_DEV * nbat, 6 * D_MODEL), (0, me * ada_cols),
                                  (N_DEV * nbat, ada_cols))
    parts["w_ada"] = _ada_bwd(c_all, dada_cols)[None]

    grad, delta, new_m, new_v = {}, {}, {}, {}
    for n in BIG:
        grad[n], delta[n], new_m[n], new_v[n] = (
            a[None] for a in _adamw(parts[n], w[n], m[n], v[n], ADAM_ROWS[n], "adamw_" + n))
    packed = _adamw(small_all, _pack_small(w), _pack_small(m), _pack_small(v), SMALL_ROWS, "adamw_small")
    for dst, pk in zip((grad, delta, new_m, new_v), packed):
        vals, lanes = _unpack_small(pk, w)
        dst.update(vals)
        if dst is grad:
            loss = jnp.sum(lanes)

    order = ("w_ada", "b_ada", "w_in", "b_fgate", "gn_a", "gn_b", "w_out", "ln1_g", "ln1_b", "w_up", "conv_w", "conv_b",
             "w_down", "ln2_g", "ln2_b")
    return (loss, grad_x, *[grad[n] for n in order], *[delta[n] for n in order], *[new_m[n] for n in order],
            *[new_v[n] for n in order])
```

```python
import functools

import numpy as np
import jax
import jax.numpy as jnp
from jax import lax
from jax.experimental import pallas as pl
from jax.experimental.pallas import tpu as pltpu

F32, BF16 = jnp.float32, jnp.bfloat16
MESH = pl.DeviceIdType.MESH
ANY = pl.BlockSpec(memory_space=pl.ANY)

D_MODEL = 1024
N_HEADS = 8
HEAD_DIM = 64
WIDTH = 512
D_FF = 2816
N_DEV = 8
ROPE_DIMS = 16
ROPE_THETA = 500000.0
ALPHA = 2.0 ** 0.25
LN_EPS = 1e-5
RMS_EPS = 1e-6
NEG = -1e30
Q_SCALE = 0.125
LOG2E = 1.4426950408889634
BLK = 128
LANES = 128
VMEM_LIMIT_BYTES = 56 * 1024 * 1024

ADAM_LR, ADAM_B1, ADAM_B2, ADAM_EPS, ADAM_WD, ADAM_STEP = 0.001, 0.9, 0.999, 1e-08, 0.01, 10


def _params(vmem=VMEM_LIMIT_BYTES):
    return pltpu.CompilerParams(vmem_limit_bytes=vmem)


def _nn(a, b):
    return jnp.dot(a, b, preferred_element_type=F32)


def _nt(a, b):
    return lax.dot_general(a, b, (((1,), (1,)), ((), ())), preferred_element_type=F32)


def _tn(a, b):
    return lax.dot_general(a, b, (((0,), (0,)), ((), ())), preferred_element_type=F32)


def _head_mats():
    r = lax.broadcasted_iota(jnp.int32, (LANES, WIDTH), 0)
    c = lax.broadcasted_iota(jnp.int32, (LANES, WIDTH), 1)
    e = ((c >> 6) == r).astype(BF16)
    r2 = lax.broadcasted_iota(jnp.int32, (WIDTH, LANES), 0)
    c2 = lax.broadcasted_iota(jnp.int32, (WIDTH, LANES), 1)
    et = ((r2 >> 6) == c2).astype(BF16)
    return e, et


def _split3(x):
    hi = x.astype(BF16)
    r = x - hi.astype(F32)
    mid = r.astype(BF16)
    return hi, mid, (r - mid.astype(F32)).astype(BF16)


def _hexp(w, e):
    return sum(_nn(part, e) for part in _split3(w)[:2])


def _hsum(x, et):
    return sum(_nn(part, et) for part in _split3(x)[:2])


def _perm_matrix(rows, d, transpose):
    i = np.arange(rows)
    j = (i % (rows // d)) * d + i // (rows // d)
    p = np.zeros((rows, rows), np.float32)
    p[i, j] = 1.0
    return jnp.asarray(p.T if transpose else p, BF16)


def _permute_f32(p, x):
    return sum(_nn(p, part) for part in _split3(x))


def _store_classes(ref, y, d):
    n = y.shape[0] // d
    for r in range(d):
        ref[r] = y[r * n:(r + 1) * n, :]


def _load_classes(ref, d):
    return jnp.concatenate([ref[r] for r in range(d)], axis=0)


def _rope_tabs(pos_ref, fr_ref, sign):
    ang = pos_ref[...].astype(F32) * fr_ref[...]
    lane = lax.broadcasted_iota(jnp.int32, ang.shape, 1) & (HEAD_DIM - 1)
    m1 = lane < ROPE_DIMS // 2
    m2 = (lane >= ROPE_DIMS // 2) & (lane < ROPE_DIMS)
    cos = jnp.cos(ang)
    sin = jnp.sin(ang) * sign
    return (jnp.where(m1 | m2, cos, 1.0), jnp.where(m1, -sin, 0.0), jnp.where(m2, sin, 0.0))


def _rope(z, tabs):
    c, s1, s2 = tabs
    parts = []
    for p in range(z.shape[1] // LANES):
        zp = z[:, LANES * p:LANES * (p + 1)]
        parts.append(zp * c + pltpu.roll(zp, LANES - 8, 1) * s1 + pltpu.roll(zp, 8, 1) * s2)
    return jnp.concatenate(parts, axis=1)


def _half_masks(rows):
    lane = lax.broadcasted_iota(jnp.int32, (rows, LANES), 1)
    lo = lane < HEAD_DIM
    return lo, jnp.logical_not(lo)


def _layer_norm_bwd(dxh, xh, rstd):
    m1 = jnp.mean(dxh, axis=1, keepdims=True)
    m2 = jnp.mean(dxh * xh, axis=1, keepdims=True)
    return rstd * (dxh - m1 - xh * m2)


def _coords():
    return lax.axis_index("x"), lax.axis_index("y"), lax.axis_index("c")


def _peer(x, y, c, k):
    return (1 - x if k & 4 else x, 1 - y if k & 2 else y, 1 - c if k & 1 else c)


def _comm_sems(n):
    return [pltpu.SemaphoreType.DMA((N_DEV - 1, n)), pltpu.SemaphoreType.DMA((N_DEV - 1, n)),
            pltpu.SemaphoreType.DMA((n,))]


def _comm_copies(ins, outs, to_all, sems):
    send_sems, recv_sems, local_sems = sems
    x, y, c = _coords()
    me = 4 * x + 2 * y + c
    copies = [pltpu.make_async_copy(ins[t] if to_all[t] else ins[t].at[me], outs[t].at[me], local_sems.at[t])
              for t in range(len(ins))]
    for k in range(1, N_DEV):
        px, py, pc = _peer(x, y, c, k)
        dest = 4 * px + 2 * py + pc
        for t in range(len(ins)):
            copies.append(pltpu.make_async_remote_copy(
                src_ref=ins[t] if to_all[t] else ins[t].at[dest], dst_ref=outs[t].at[me],
                send_sem=send_sems.at[k - 1, t], recv_sem=recv_sems.at[k - 1, t],
                device_id=(px, py, pc), device_id_type=MESH))
    return copies


def _comm_out_shapes(ins, to_all):
    return [jax.ShapeDtypeStruct(((N_DEV,) + a.shape) if ta else a.shape, a.dtype) for a, ta in zip(ins, to_all)]


def _exchange(ins, to_all, name):
    n = len(ins)

    def body(*refs):
        copies = _comm_copies(refs[:n], refs[n:2 * n], to_all, refs[2 * n:])
        for cp in copies:
            cp.start()
        for cp in copies:
            cp.wait()

    return pl.pallas_call(
        body, name=name, out_shape=_comm_out_shapes(ins, to_all), in_specs=[ANY] * n, out_specs=[ANY] * n,
        scratch_shapes=_comm_sems(n),
    )(*ins)


def _gather_two_level(ins, name):
    n = len(ins)

    def body(*refs):
        srcs, outs = refs[:n], refs[n:2 * n]
        send_sems, recv_sems, local_sems = refs[2 * n:]
        x, y, c = _coords()
        me = 4 * x + 2 * y + c
        sibling = (x, y, 1 - c)
        chips = [(1 - x, y), (x, 1 - y), (1 - x, 1 - y)]
        slot = lambda px, py, pc: 4 * px + 2 * py + pc

        def copy(k, t, block, to, own=False):
            return pltpu.make_async_remote_copy(
                src_ref=srcs[t] if own else outs[t].at[block], dst_ref=outs[t].at[block],
                send_sem=send_sems.at[k, t], recv_sem=recv_sems.at[k, t], device_id=to, device_id_type=MESH)

        local = [pltpu.make_async_copy(srcs[t], outs[t].at[me], local_sems.at[t]) for t in range(n)]
        first = [copy(0, t, me, sibling, own=True) for t in range(n)]
        first += [copy(1 + j, t, me, (*chip, c), own=True) for j, chip in enumerate(chips) for t in range(n)]
        for cp in local + first:
            cp.start()
        passed = []
        for j, chip in enumerate(chips):
            for t in range(n):
                copy(1 + j, t, slot(*chip, c), (x, y, c)).wait_recv()
                cp = copy(4 + j, t, slot(*chip, c), sibling)
                cp.start()
                passed.append(cp)
        for t in range(n):
            copy(0, t, slot(x, y, 1 - c), (x, y, c)).wait_recv()
            for j, chip in enumerate(chips):
                copy(4 + j, t, slot(*chip, 1 - c), (x, y, c)).wait_recv()
        for cp in first + passed:
            cp.wait_send()
        for cp in local:
            cp.wait()

    return pl.pallas_call(
        body, name=name, out_shape=_comm_out_shapes(ins, [True] * n), in_specs=[ANY] * n, out_specs=[ANY] * n,
        scratch_shapes=_comm_sems(n),
    )(*ins)


def _adamw(parts, w, m, v, rows, name):
    n_parts, r_all, cols = parts.shape
    c1 = 1.0 - ADAM_B1 ** ADAM_STEP
    c2 = 1.0 - ADAM_B2 ** ADAM_STEP

    def body(p_ref, w_ref, m_ref, v_ref, g_ref, d_ref, mo_ref, vo_ref):
        g = p_ref[0].astype(F32)
        for s in range(1, n_parts):
            g = g + p_ref[s].astype(F32)
        mn = ADAM_B1 * m_ref[...] + (1.0 - ADAM_B1) * g
        vn = ADAM_B2 * v_ref[...] + (1.0 - ADAM_B2) * (g * g)
        m_hat = mn / c1
        v_hat = vn / c2
        g_ref[...] = g
        d_ref[...] = -ADAM_LR * (m_hat / (jnp.sqrt(v_hat) + ADAM_EPS) + ADAM_WD * w_ref[...])
        mo_ref[...] = mn
        vo_ref[...] = vn

    spec = pl.BlockSpec((rows, cols), lambda i: (i, 0))
    return pl.pallas_call(
        body, name=name, grid=(r_all // rows,),
        in_specs=[pl.BlockSpec((n_parts, rows, cols), lambda i: (0, i, 0)), spec, spec, spec],
        out_specs=[spec] * 4, out_shape=[jax.ShapeDtypeStruct((r_all, cols), F32)] * 4,
        compiler_params=_params(),
    )(parts, w, m, v)


def _matmul_tn(a, b, chunk, tk, name):
    t_all, k1 = a.shape
    n = b.shape[1]

    def body(a_ref, b_ref, o_ref):
        @pl.when(pl.program_id(0) == 0)
        def _():
            o_ref[...] = jnp.zeros_like(o_ref)
        at = a_ref[...].astype(F32).T.astype(BF16)
        for j in range(0, n, chunk):
            cs = slice(j, min(j + chunk, n))
            o_ref[:, cs] += _nn(at, b_ref[:, cs])

    return pl.pallas_call(
        body, name=name, grid=(t_all // tk,),
        in_specs=[pl.BlockSpec((tk, k1), lambda t: (t, 0)), pl.BlockSpec((tk, n), lambda t: (t, 0))],
        out_specs=pl.BlockSpec((k1, n), lambda t: (0, 0)),
        out_shape=jax.ShapeDtypeStruct((k1, n), F32), compiler_params=_params(),
    )(a, b)


def _matmul_rows(a, b, tk, name):
    r, t_all = a.shape
    n = b.shape[1]

    def body(a_ref, b_ref, o_ref):
        @pl.when(pl.program_id(0) == 0)
        def _():
            o_ref[...] = jnp.zeros_like(o_ref)
        o_ref[...] += _nn(a_ref[...], b_ref[...])

    return pl.pallas_call(
        body, name=name, grid=(t_all // tk,),
        in_specs=[pl.BlockSpec((r, tk), lambda t: (0, t)), pl.BlockSpec((tk, n), lambda t: (t, 0))],
        out_specs=pl.BlockSpec((r, n), lambda t: (0, 0)),
        out_shape=jax.ShapeDtypeStruct((r, n), F32), compiler_params=_params(),
    )(a, b)


def _ada_fwd(c_all, w_ada, b_ada):
    whole = lambda a: pl.BlockSpec(a.shape, lambda j: (0, 0))

    def body(c_ref, w_ref, b_ref, o_ref):
        cv = c_ref[...]
        s = (cv * jax.nn.sigmoid(cv)).astype(BF16)
        o_ref[...] = _nn(s, w_ref[...].astype(BF16)) + b_ref[...]

    out = jax.ShapeDtypeStruct((c_all.shape[0], w_ada.shape[1]), F32)
    return pl.pallas_call(
        body, name="ada_fwd", grid=(1,), in_specs=[whole(c_all), whole(w_ada), whole(b_ada)], out_specs=whole(out),
        out_shape=out, compiler_params=_params(),
    )(c_all, w_ada, b_ada)


def _ada_bwd(c_all, dada):
    whole = lambda a: pl.BlockSpec(a.shape, lambda j: (0, 0))

    def body(c_ref, d_ref, o_ref):
        cv = c_ref[...]
        s = (cv * jax.nn.sigmoid(cv)).astype(BF16)
        o_ref[...] = _tn(s, d_ref[...].astype(BF16))

    out = jax.ShapeDtypeStruct((D_MODEL, dada.shape[1]), F32)
    return pl.pallas_call(
        body, name="ada_bwd", grid=(1,), in_specs=[whole(c_all), whole(dada)], out_specs=whole(out), out_shape=out,
        compiler_params=_params(),
    )(c_all, dada)


TOK_TM = 256
DILATIONS = (1, 4, 16)


def _class_spec(d, width, nts):
    return pl.BlockSpec((d, TOK_TM // d, width), lambda i: (i // nts, i % nts, 0))


def _class_shape(t_all, seq, d, width, dtype):
    return jax.ShapeDtypeStruct((t_all // seq * d, seq // d, width), dtype)


def _inproj(x, ada3, pos, wqkv, wf16, freq, perms, seq):
    t_all = x.shape[0]
    tm = TOK_TM
    nts = seq // tm

    def body(x_ref, ada_ref, pos_ref, w_ref, wf_ref, fr_ref, p4_ref, p16_ref, h1_ref, za_ref, zb_ref, zb4_ref,
             zb16_ref, vt_ref, fa_ref):
        h1 = (x_ref[...] * (1.0 + ada_ref[0, 1:2, :]) + ada_ref[0, 0:1, :]).astype(BF16)
        h1_ref[...] = h1
        tabs = _rope_tabs(pos_ref, fr_ref, 1.0)
        for n in range(6):
            z = _nn(h1, w_ref[:, n * WIDTH:(n + 1) * WIDTH])
            if n in (3, 4):
                z = _rope(z, tabs)
            if n in (0, 3):
                z = z * (Q_SCALE * LOG2E)
            if n == 2:
                vt_ref[...] = z.T.astype(BF16)
            dst = za_ref if n < 3 else zb_ref
            dst[:, (n % 3) * WIDTH:(n % 3 + 1) * WIDTH] = z.astype(BF16)
        fa_ref[...] = _nt(wf_ref[...], h1)[:N_HEADS]
        zb = zb_ref[...]
        _store_classes(zb4_ref, _nn(p4_ref[...], zb).astype(BF16), 4)
        _store_classes(zb16_ref, _nn(p16_ref[...], zb).astype(BF16), 16)

    tok = lambda w: pl.BlockSpec((tm, w), lambda i: (i, 0))
    whole = lambda a: pl.BlockSpec(a.shape, lambda i: (0, 0))
    return pl.pallas_call(
        body, name="inproj", grid=(t_all // tm,),
        in_specs=[tok(D_MODEL), pl.BlockSpec((1, 6, D_MODEL), lambda i: (i // nts, 0, 0)), tok(1), whole(wqkv),
                  whole(wf16), pl.BlockSpec((1, LANES), lambda i: (0, 0)), whole(perms[0]), whole(perms[1])],
        out_specs=[tok(D_MODEL), tok(3 * WIDTH), tok(3 * WIDTH), _class_spec(4, 3 * WIDTH, nts),
                   _class_spec(16, 3 * WIDTH, nts), pl.BlockSpec((WIDTH, tm), lambda i: (i // nts, i % nts)),
                   pl.BlockSpec((N_HEADS, tm), lambda i: (0, i))],
        out_shape=[jax.ShapeDtypeStruct((t_all, D_MODEL), BF16), jax.ShapeDtypeStruct((t_all, 3 * WIDTH), BF16),
                   jax.ShapeDtypeStruct((t_all, 3 * WIDTH), BF16), _class_shape(t_all, seq, 4, 3 * WIDTH, BF16),
                   _class_shape(t_all, seq, 16, 3 * WIDTH, BF16),
                   jax.ShapeDtypeStruct((t_all // seq * WIDTH, seq), BF16),
                   jax.ShapeDtypeStruct((N_HEADS, t_all), F32)],
        compiler_params=_params(),
    )(x, ada3, pos, wqkv, wf16, freq, perms[0], perms[1])


def _chunk_rows(a_t, seq):
    t_all = a_t.shape[1]
    return a_t.reshape(N_HEADS, t_all // seq, seq // LANES, LANES).transpose(1, 0, 2, 3).reshape(-1, LANES)


def _unchunk_rows(a, seq):
    nbat = a.shape[0] * LANES // (N_HEADS * seq)
    return a.reshape(nbat, N_HEADS, seq // LANES, LANES).transpose(1, 0, 2, 3).reshape(N_HEADS, nbat * seq)


def _chunk_carry(tot, nchunk, later):
    rows = tot.shape[0]
    r = lax.broadcasted_iota(jnp.int32, (rows, rows), 0)
    c = lax.broadcasted_iota(jnp.int32, (rows, rows), 1)
    sel = ((r // nchunk) == (c // nchunk)) & ((c > r) if later else (c < r))
    mat = sel.astype(BF16)
    return sum(_nn(mat, part) for part in _split3(jnp.broadcast_to(tot, (rows, LANES))))


def _fgate_fwd(fa_t, bf, seq):
    x = _chunk_rows(fa_t, seq)
    rows = x.shape[0]
    nchunk = seq // LANES
    bias = jnp.broadcast_to(bf.reshape(1, N_HEADS, 1), (rows // (N_HEADS * nchunk), N_HEADS, nchunk)).reshape(rows, 1)

    def body(x_ref, b_ref, f_ref):
        lane = lax.broadcasted_iota(jnp.int32, (rows, LANES), 1)
        xv = x_ref[...] + b_ref[...]
        lf = jnp.minimum(xv, 0.0) - jnp.log(1.0 + jnp.exp(-jnp.abs(xv)))
        for s in (1, 2, 4, 8, 16, 32, 64):
            lf = lf + jnp.where(lane >= s, pltpu.roll(lf, s, 1), 0.0)
        f_ref[...] = lf + _chunk_carry(lf[:, LANES - 1:LANES], nchunk, False)

    whole = lambda a: pl.BlockSpec(a.shape, lambda i: (0, 0))
    out = pl.pallas_call(
        body, name="fgate_fwd", grid=(1,), in_specs=[whole(x), whole(bias)], out_specs=whole(x),
        out_shape=jax.ShapeDtypeStruct(x.shape, F32), compiler_params=_params(),
    )(x, bias)
    return _unchunk_rows(out, seq)


def _fgate_bwd(df_t, fa_t, bf, seq):
    d_in = _chunk_rows(df_t, seq)
    x = _chunk_rows(fa_t, seq)
    rows = x.shape[0]
    nchunk = seq // LANES
    bias = jnp.broadcast_to(bf.reshape(1, N_HEADS, 1), (rows // (N_HEADS * nchunk), N_HEADS, nchunk)).reshape(rows, 1)

    def body(d_ref, x_ref, b_ref, o_ref, s_ref):
        lane = lax.broadcasted_iota(jnp.int32, (rows, LANES), 1)
        d = d_ref[...]
        for s in (1, 2, 4, 8, 16, 32, 64):
            d = d + jnp.where(lane < LANES - s, pltpu.roll(d, LANES - s, 1), 0.0)
        d = d + _chunk_carry(d[:, 0:1], nchunk, True)
        dfa = d * jax.nn.sigmoid(-(x_ref[...] + b_ref[...]))
        o_ref[...] = dfa
        g = lax.broadcasted_iota(jnp.int32, (2 * N_HEADS, rows), 0)
        r = lax.broadcasted_iota(jnp.int32, (2 * N_HEADS, rows), 1)
        group = (((r // nchunk) % N_HEADS) == g).astype(BF16)
        per_head = sum(_nn(group, part) for part in _split3(dfa))[:N_HEADS]
        s_ref[...] = jnp.broadcast_to(jnp.sum(per_head, axis=1, keepdims=True), (N_HEADS, LANES))

    whole = lambda a: pl.BlockSpec(a.shape, lambda i: (0, 0))
    dfa, sums = pl.pallas_call(
        body, name="fgate_bwd", grid=(1,), in_specs=[whole(d_in), whole(x), whole(bias)],
        out_specs=[whole(x), pl.BlockSpec((N_HEADS, LANES), lambda i: (0, 0))],
        out_shape=[jax.ShapeDtypeStruct(x.shape, F32), jax.ShapeDtypeStruct((N_HEADS, LANES), F32)],
        compiler_params=_params(),
    )(d_in, x, bias)
    return _unchunk_rows(dfa, seq), sums


FOX_T = 256


def _fox_prep(dst, src_ref, lo, hi):
    for p in range(4):
        v = src_ref[:, LANES * p:LANES * (p + 1)]
        dst[2 * p] = jnp.where(lo, v, jnp.zeros_like(v))
        dst[2 * p + 1] = jnp.where(hi, v, jnp.zeros_like(v))


def _fox_fwd(za, vt, f_col, seq, shards):
    t_all = za.shape[0]
    tq = FOX_T
    nq = seq // tq
    nbat = t_all // seq
    n = len(shards)
    to_all = [True] * n

    def body(*refs):
        q_ref, k_ref, vt_ref, fc_ref = refs[:4]
        o_ref, lse_ref = refs[4 + n:6 + n]
        qm_sc, m_sc, l_sc, acc_sc, a_sc, st_sc, pe_sc = refs[6 + 2 * n:13 + 2 * n]
        comm = (refs[4:4 + n], refs[6 + n:6 + 2 * n], to_all, refs[13 + 2 * n:])
        i = pl.program_id(1)

        @pl.when((pl.program_id(0) == 0) & (i == 0))
        def _():
            for cp in _comm_copies(*comm):
                cp.start()
        lo, hi = _half_masks(tq)
        r = lax.broadcasted_iota(jnp.int32, (tq, tq), 0)
        c = lax.broadcasted_iota(jnp.int32, (tq, tq), 1)
        tri = c >= r
        _fox_prep(qm_sc, q_ref, lo, hi)
        m_sc[...] = jnp.full(m_sc.shape, NEG, F32)
        l_sc[...] = jnp.zeros_like(l_sc)
        acc_sc[...] = jnp.zeros_like(acc_sc)

        def block(j, masked):
            sl = pl.ds(pl.multiple_of(j * tq, tq), tq)
            for p in range(4):
                kj = k_ref[sl, LANES * p:LANES * (p + 1)]
                for h in (2 * p, 2 * p + 1):
                    st = _nt(kj, qm_sc[h]) - fc_ref[sl, h:h + 1]
                    st_sc[h] = jnp.where(tri, st, NEG) if masked else st
            for h in range(N_HEADS):
                st = st_sc[h]
                m = m_sc[h:h + 1, :]
                mn = jnp.maximum(m, jnp.max(st, axis=0, keepdims=True))
                a = jnp.exp2(m - mn)
                pe = jnp.exp2(st - mn)
                m_sc[h:h + 1, :] = mn
                a_sc[h:h + 1, :] = a
                l_sc[h:h + 1, :] = a * l_sc[h:h + 1, :] + jnp.sum(pe, axis=0, keepdims=True)
                pe_sc[h] = pe.astype(BF16)
            for h in range(N_HEADS):
                acc_sc[h] = a_sc[h:h + 1, :] * acc_sc[h] + _nn(vt_ref[HEAD_DIM * h:HEAD_DIM * (h + 1), sl], pe_sc[h])

        def step(j, carry):
            block(j, False)
            return carry

        lax.fori_loop(0, i, step, 0)
        block(i, True)
        lse_ref[...] = m_sc[...] + jnp.log(l_sc[...]) * LOG2E
        for p in range(4):
            ot = jnp.concatenate([acc_sc[h] / l_sc[h:h + 1, :] for h in (2 * p, 2 * p + 1)], axis=0)
            o_ref[:, LANES * p:LANES * (p + 1)] = ot.T

        @pl.when((pl.program_id(0) == nbat - 1) & (i == nq - 1))
        def _():
            for cp in _comm_copies(*comm):
                cp.wait()

    res = pl.pallas_call(
        body, name="fox_fwd", grid=(nbat, nq),
        in_specs=[pl.BlockSpec((tq, WIDTH), lambda b, i: (b * nq + i, 0)),
                  pl.BlockSpec((seq, WIDTH), lambda b, i: (b, 1)), pl.BlockSpec((WIDTH, seq), lambda b, i: (b, 0)),
                  pl.BlockSpec((seq, LANES), lambda b, i: (b, 0))] + [ANY] * n,
        out_specs=[pl.BlockSpec((tq, WIDTH), lambda b, i: (b * nq + i, 0)),
                   pl.BlockSpec((N_HEADS, tq), lambda b, i: (0, b * nq + i))] + [ANY] * n,
        out_shape=[jax.ShapeDtypeStruct((t_all, WIDTH), F32), jax.ShapeDtypeStruct((N_HEADS, t_all), F32)]
        + _comm_out_shapes(shards, to_all),
        scratch_shapes=[pltpu.VMEM((N_HEADS, tq, LANES), BF16), pltpu.VMEM((N_HEADS, tq), F32),
                        pltpu.VMEM((N_HEADS, tq), F32), pltpu.VMEM((N_HEADS, HEAD_DIM, tq), F32),
                        pltpu.VMEM((N_HEADS, tq), F32), pltpu.VMEM((N_HEADS, tq, tq), F32),
                        pltpu.VMEM((N_HEADS, tq, tq), BF16)] + _comm_sems(n),
        compiler_params=_params(),
    )(za, za, vt, f_col, *shards)
    return res[0], res[1], res[2:]


def _fox_bwd(za, do, f_col, lse_row, dl_row, seq, grads):
    t_all = za.shape[0]
    tk = FOX_T
    nk = seq // tk
    nbat = t_all // seq
    n = len(grads)
    to_all = [False] * n

    def body(*refs):
        k_ref, v_ref, q_ref, do_ref, fc_ref, lr_ref, dr_ref = refs[:7]
        dk_ref, dv_ref, df_ref, dqt_ref, dfq_ref = refs[7 + n:12 + n]
        km_sc, vm_sc, fk_sc, dk_sc, dv_sc, cs_sc, kt_sc, st_sc, dp_sc, pt_sc, ds_sc = refs[12 + 2 * n:23 + 2 * n]
        comm = (refs[7:7 + n], refs[12 + n:12 + 2 * n], to_all, refs[23 + 2 * n:])
        j = pl.program_id(1)

        @pl.when(j == 0)
        def _():
            dqt_ref[...] = jnp.zeros_like(dqt_ref)
            dfq_ref[...] = jnp.zeros_like(dfq_ref)

        @pl.when((pl.program_id(0) == 0) & (j == 0))
        def _():
            for cp in _comm_copies(*comm):
                cp.start()
        lo, hi = _half_masks(tk)
        r = lax.broadcasted_iota(jnp.int32, (tk, tk), 0)
        c = lax.broadcasted_iota(jnp.int32, (tk, tk), 1)
        tri = c >= r
        _fox_prep(km_sc, k_ref, lo, hi)
        _fox_prep(vm_sc, v_ref, lo, hi)
        for h in range(N_HEADS):
            fk_sc[h] = jnp.broadcast_to(fc_ref[:, h:h + 1], (tk, tk))
        for p in range(4):
            kt_sc[p] = k_ref[:, LANES * p:LANES * (p + 1)].astype(F32).T.astype(BF16)
        dk_sc[...] = jnp.zeros_like(dk_sc)
        dv_sc[...] = jnp.zeros_like(dv_sc)
        cs_sc[...] = jnp.zeros_like(cs_sc)

        def block(i, masked):
            sl = pl.ds(pl.multiple_of(i * tk, tk), tk)
            for p in range(4):
                cs = slice(LANES * p, LANES * (p + 1))
                qi = q_ref[sl, cs]
                doi = do_ref[sl, cs]
                for h in (2 * p, 2 * p + 1):
                    st = _nt(km_sc[h], qi) - fk_sc[h] - lr_ref[h:h + 1, sl]
                    st_sc[h] = jnp.where(tri, st, NEG) if masked else st
                    dp_sc[h] = _nt(vm_sc[h], doi) - dr_ref[h:h + 1, sl]
            for h in range(N_HEADS):
                pt = jnp.exp2(st_sc[h])
                dst = pt * dp_sc[h]
                pt_sc[h] = pt.astype(BF16)
                ds_sc[h] = dst.astype(BF16)
                cs_sc[h] += dst[:, :LANES] + dst[:, LANES:]
                dfq_ref[h:h + 1, sl] += jnp.sum(dst, axis=0, keepdims=True)
            for p in range(4):
                cs = slice(LANES * p, LANES * (p + 1))
                qi = q_ref[sl, cs]
                doi = do_ref[sl, cs]
                for h in (2 * p, 2 * p + 1):
                    dv_sc[h] += _nn(pt_sc[h], doi)
                    dk_sc[h] += _nn(ds_sc[h], qi)
                    kt = kt_sc[p, HEAD_DIM * (h % 2):HEAD_DIM * (h % 2 + 1), :]
                    dqt_ref[HEAD_DIM * h:HEAD_DIM * (h + 1), sl] += _nn(kt, ds_sc[h])

        def step(i, carry):
            block(i, False)
            return carry

        block(j, True)
        lax.fori_loop(j + 1, nk, step, 0)
        df_ref[...] = jnp.zeros_like(df_ref)
        for p in range(4):
            cs = slice(LANES * p, LANES * (p + 1))
            dk_ref[:, cs] = (jnp.where(lo, dk_sc[2 * p], dk_sc[2 * p + 1]) * (1.0 / LOG2E)).astype(BF16)
            dv_ref[:, cs] = jnp.where(lo, dv_sc[2 * p], dv_sc[2 * p + 1]).astype(BF16)
            for h in (2 * p, 2 * p + 1):
                df_ref[:, h:h + 1] = -jnp.sum(cs_sc[h], axis=1, keepdims=True)

        @pl.when(j == nk - 1)
        def _():
            dqt_ref[...] = dqt_ref[...] * Q_SCALE

        @pl.when((pl.program_id(0) == nbat - 1) & (j == nk - 1))
        def _():
            for cp in _comm_copies(*comm):
                cp.wait()

    tile = lambda w, col: pl.BlockSpec((tk, w), lambda b, j: (b * nk + j, col))
    full = lambda col: pl.BlockSpec((seq, WIDTH), lambda b, j: (b, col))
    row = pl.BlockSpec((N_HEADS, seq), lambda b, j: (0, b))
    acc = pltpu.VMEM((N_HEADS, tk, LANES), F32)
    res = pl.pallas_call(
        body, name="fox_bwd", grid=(nbat, nk),
        in_specs=[tile(WIDTH, 1), tile(WIDTH, 2), full(0), full(0), tile(LANES, 0), row, row] + [ANY] * n,
        out_specs=[tile(WIDTH, 0), tile(WIDTH, 0), tile(LANES, 0), pl.BlockSpec((WIDTH, seq), lambda b, j: (b, 0)),
                   row] + [ANY] * n,
        out_shape=[jax.ShapeDtypeStruct((t_all, WIDTH), BF16), jax.ShapeDtypeStruct((t_all, WIDTH), BF16),
                   jax.ShapeDtypeStruct((t_all, LANES), F32), jax.ShapeDtypeStruct((nbat * WIDTH, seq), F32),
                   jax.ShapeDtypeStruct((N_HEADS, t_all), F32)] + _comm_out_shapes(grads, to_all),
        scratch_shapes=[pltpu.VMEM((N_HEADS, tk, LANES), BF16), pltpu.VMEM((N_HEADS, tk, LANES), BF16),
                        pltpu.VMEM((N_HEADS, tk, tk), F32), acc, acc, acc, pltpu.VMEM((4, LANES, tk), BF16),
                        pltpu.VMEM((N_HEADS, tk, tk), F32), pltpu.VMEM((N_HEADS, tk, tk), F32),
                        pltpu.VMEM((N_HEADS, tk, tk), BF16), pltpu.VMEM((N_HEADS, tk, tk), BF16)]
        + _comm_sems(n),
        compiler_params=_params(),
    )(za, za, za, do, f_col, lse_row, dl_row, *grads)
    return res[0], res[1], res[2], res[3], res[4], res[5:]


DIL_SUB = 4


def _dil_mask(has_prev):
    qi = lax.broadcasted_iota(jnp.int32, (BLK, 2 * BLK), 0)
    kj = lax.broadcasted_iota(jnp.int32, (BLK, 2 * BLK), 1)
    dist = qi + BLK - kj
    band = (dist >= 0) & (dist <= BLK)
    return band if has_prev is True else band & ((kj >= BLK) | has_prev)


def _dil_geometry(t_all, seq, d, max_sub=DIL_SUB):
    length = seq // d
    nbs = length // BLK
    sub = min(max_sub, nbs)
    spb = nbs // sub
    tile = lambda width, col: pl.BlockSpec((BLK * sub, width), lambda s: (s, col))
    whole = lambda width, col: pl.BlockSpec((length, width), lambda s: (s // spb, col))
    return nbs, sub, spb, t_all // (BLK * sub), tile, whole


def _blk(i):
    return pl.ds(pl.multiple_of(i * BLK, BLK), BLK)


def _dil_fwd(zb, seq, d):
    t_all = zb.shape[0]
    nbs, sub, spb, steps, tile, whole = _dil_geometry(t_all, seq, d)

    def body(q_ref, k_ref, v_ref, o_ref, lse_ref, s_sc, p_sc):
        first = (pl.program_id(0) % spb) * sub
        lo, hi = _half_masks(BLK)
        lse_ref[...] = jnp.zeros_like(lse_ref)
        for j in range(sub):
            blk = first + j
            mask = _dil_mask(blk != 0 if j == 0 else True)
            for p in range(4):
                cs = slice(LANES * p, LANES * (p + 1))
                qp = q_ref[BLK * j:BLK * (j + 1), cs]
                kcat = jnp.concatenate([k_ref[_blk(jnp.maximum(blk - 1, 0)), cs], k_ref[_blk(blk), cs]], axis=0)
                for e in (0, 1):
                    qe = jnp.where(lo if e == 0 else hi, qp, jnp.zeros_like(qp))
                    s_sc[N_HEADS * j + 2 * p + e] = jnp.where(mask, _nt(qe, kcat), NEG)
        inv = []
        for i in range(N_HEADS * sub):
            s = s_sc[i]
            m = jnp.max(s, axis=1, keepdims=True)
            pe = jnp.exp2(s - m)
            l = jnp.sum(pe, axis=1, keepdims=True)
            p_sc[i] = pe.astype(BF16)
            inv.append(1.0 / l)
            j, h = divmod(i, N_HEADS)
            lse_ref[BLK * j:BLK * (j + 1), h:h + 1] = m + jnp.log(l) * LOG2E
        for j in range(sub):
            blk = first + j
            for p in range(4):
                cs = slice(LANES * p, LANES * (p + 1))
                vcat = jnp.concatenate([v_ref[_blk(jnp.maximum(blk - 1, 0)), cs], v_ref[_blk(blk), cs]], axis=0)
                res = [_nn(p_sc[N_HEADS * j + h], vcat) * inv[N_HEADS * j + h] for h in (2 * p, 2 * p + 1)]
                o_ref[BLK * j:BLK * (j + 1), cs] = jnp.where(lo, res[0], res[1]).astype(BF16)

    return pl.pallas_call(
        body, name=f"dil_fwd_{d}", grid=(steps,), in_specs=[tile(WIDTH, 0), whole(WIDTH, 1), whole(WIDTH, 2)],
        out_specs=[tile(WIDTH, 0), tile(LANES, 0)],
        out_shape=[jax.ShapeDtypeStruct((t_all, WIDTH), BF16), jax.ShapeDtypeStruct((t_all, LANES), F32)],
        scratch_shapes=[pltpu.VMEM((N_HEADS * sub, BLK, 2 * BLK), F32),
                        pltpu.VMEM((N_HEADS * sub, BLK, 2 * BLK), BF16)],
        compiler_params=_params(),
    )(zb, zb, zb)


def _dil_bwd(zb, do, lse, dl, seq, d):
    t_all = zb.shape[0]
    length = seq // d
    nbs, sub, spb, steps, tile, whole = _dil_geometry(t_all, seq, d, 2 if length >= 4096 else DIL_SUB)

    def body(k_ref, v_ref, q_ref, do_ref, lse_ref, dl_ref, dq_ref, dk_ref, dv_ref, s_sc, dp_sc, pt_sc, ds_sc, kt_sc,
             dqt_sc):
        step = pl.program_id(0) % spb
        first = step * sub

        @pl.when(step == 0)
        def _():
            dqt_sc[...] = jnp.zeros_like(dqt_sc)
        r = lax.broadcasted_iota(jnp.int32, (BLK, 2 * BLK), 0)
        c = lax.broadcasted_iota(jnp.int32, (BLK, 2 * BLK), 1)
        same = (c < BLK) & (c >= r)
        later = (c >= BLK) & (c - BLK <= r)
        lo, hi = _half_masks(BLK)
        for j in range(sub):
            blk = first + j
            rows = slice(BLK * j, BLK * (j + 1))
            nxt = _blk(jnp.minimum(blk + 1, nbs - 1))
            mask = same | (later & (blk + 1 != nbs)) if j == sub - 1 else same | later
            lrows = jnp.concatenate([lse_ref[_blk(blk), :].T, lse_ref[nxt, :].T], axis=1)
            erows = jnp.concatenate([dl_ref[_blk(blk), :].T, dl_ref[nxt, :].T], axis=1)
            for p in range(4):
                cs = slice(LANES * p, LANES * (p + 1))
                kp = k_ref[rows, cs]
                vp = v_ref[rows, cs]
                kt_sc[4 * j + p] = kp.astype(F32).T.astype(BF16)
                qcat = jnp.concatenate([q_ref[_blk(blk), cs], q_ref[nxt, cs]], axis=0)
                dcat = jnp.concatenate([do_ref[_blk(blk), cs], do_ref[nxt, cs]], axis=0)
                for e in (0, 1):
                    h = 2 * p + e
                    sel = lo if e == 0 else hi
                    ke = jnp.where(sel, kp, jnp.zeros_like(kp))
                    ve = jnp.where(sel, vp, jnp.zeros_like(vp))
                    s_sc[N_HEADS * j + h] = jnp.where(mask, _nt(ke, qcat) - lrows[h:h + 1, :], NEG)
                    dp_sc[N_HEADS * j + h] = _nt(ve, dcat) - erows[h:h + 1, :]
        for i in range(N_HEADS * sub):
            pt = jnp.exp2(s_sc[i])
            pt_sc[i] = pt.astype(BF16)
            ds_sc[i] = (pt * dp_sc[i]).astype(BF16)
        for j in range(sub):
            blk = first + j
            rows = slice(BLK * j, BLK * (j + 1))
            nxt = _blk(jnp.minimum(blk + 1, nbs - 1))
            cols = pl.ds(pl.multiple_of(blk * BLK, BLK), 2 * BLK)
            for p in range(4):
                cs = slice(LANES * p, LANES * (p + 1))
                qcat = jnp.concatenate([q_ref[_blk(blk), cs], q_ref[nxt, cs]], axis=0)
                dcat = jnp.concatenate([do_ref[_blk(blk), cs], do_ref[nxt, cs]], axis=0)
                i = N_HEADS * j + 2 * p
                dk_ref[rows, cs] = (jnp.where(lo, _nn(ds_sc[i], qcat), _nn(ds_sc[i + 1], qcat))
                                    * (1.0 / LOG2E)).astype(BF16)
                dv_ref[rows, cs] = jnp.where(lo, _nn(pt_sc[i], dcat), _nn(pt_sc[i + 1], dcat)).astype(BF16)
                for e in (0, 1):
                    kt = kt_sc[4 * j + p, HEAD_DIM * e:HEAD_DIM * (e + 1), :]
                    dqt_sc[HEAD_DIM * (2 * p + e):HEAD_DIM * (2 * p + e + 1), cols] += _nn(kt, ds_sc[i + e])

        @pl.when(step == spb - 1)
        def _():
            for p in range(4):
                cs = slice(LANES * p, LANES * (p + 1))
                dq_ref[:, cs] = (dqt_sc[cs, 0:length].T * Q_SCALE).astype(BF16)

    wide = pltpu.VMEM((N_HEADS * sub, BLK, 2 * BLK), F32)
    half = pltpu.VMEM((N_HEADS * sub, BLK, 2 * BLK), BF16)
    return pl.pallas_call(
        body, name=f"dil_bwd_{d}", grid=(steps,),
        in_specs=[tile(WIDTH, 1), tile(WIDTH, 2), whole(WIDTH, 0), whole(WIDTH, 0), whole(LANES, 0), whole(LANES, 0)],
        out_specs=[whole(WIDTH, 0), tile(WIDTH, 0), tile(WIDTH, 0)],
        out_shape=[jax.ShapeDtypeStruct((t_all, WIDTH), BF16)] * 3,
        scratch_shapes=[wide, wide, half, half, pltpu.VMEM((4 * sub, LANES, BLK), BF16),
                        pltpu.VMEM((WIDTH, length + BLK), F32)],
        compiler_params=_params(),
    )(zb, zb, zb, do, lse, dl)


def _mix_out(oa, o3, l3, gn_a, gn_b, w_out, x, ada3, ln_g, ln_b, perms, seq):
    t_all = x.shape[0]
    tm = TOK_TM
    nts = seq // tm

    def body(oa_ref, o1_ref, o2_ref, o3_ref, l1_ref, l2_ref, l3_ref, ga_ref, gb_ref, w_ref, x_ref, ada_ref, g_ref,
             b_ref, p4_ref, p16_ref, pt4_ref, pt16_ref, ob_ref, lse_ref, lse4_ref, lse16_ref, mg_ref, mix_ref, xh_ref,
             rs_ref, h2_ref, h2t_ref):
        e, et = _head_mats()
        la = l1_ref[...]
        lb = _permute_f32(pt4_ref[...], _load_classes(l2_ref, 4))
        lc = _permute_f32(pt16_ref[...], _load_classes(l3_ref, 16))
        mx = jnp.maximum(jnp.maximum(la, lb), lc)
        ea, eb, ec = jnp.exp2(la - mx), jnp.exp2(lb - mx), jnp.exp2(lc - mx)
        tot = ea + eb + ec
        lse = mx + jnp.log(tot) * LOG2E
        lse_ref[...] = lse
        _store_classes(lse4_ref, _permute_f32(p4_ref[...], lse), 4)
        _store_classes(lse16_ref, _permute_f32(p16_ref[...], lse), 16)
        ob = (o1_ref[...].astype(F32) * _hexp(ea / tot, e)
              + _nn(pt4_ref[...], _load_classes(o2_ref, 4)) * _hexp(eb / tot, e)
              + _nn(pt16_ref[...], _load_classes(o3_ref, 16)) * _hexp(ec / tot, e))
        ob_ref[...] = ob

        def rms(o, gain):
            rr = lax.rsqrt(_hsum(o * o, et) * (1.0 / HEAD_DIM) + RMS_EPS)
            return o * _hexp(rr, e) * gain

        merged = jnp.concatenate([rms(oa_ref[...], ga_ref[...]), rms(ob, gb_ref[...])], axis=1).astype(BF16)
        mg_ref[...] = merged
        mix = _nn(merged, w_ref[...])
        mix_ref[...] = mix.astype(BF16)
        r1 = ALPHA * x_ref[...] + ada_ref[0, 2:3, :] * mix
        d = r1 - jnp.mean(r1, axis=1, keepdims=True)
        rstd = lax.rsqrt(jnp.mean(d * d, axis=1, keepdims=True) + LN_EPS)
        xh = d * rstd
        xh_ref[...] = xh
        rs_ref[...] = jnp.broadcast_to(rstd, (tm, LANES))
        x1 = xh * g_ref[...] + b_ref[...]
        h2 = x1 * (1.0 + ada_ref[0, 4:5, :]) + ada_ref[0, 3:4, :]
        h2_ref[...] = h2.astype(BF16)
        h2t_ref[0] = h2.T.astype(BF16)

    tok = lambda w: pl.BlockSpec((tm, w), lambda i: (i, 0))
    vec = lambda w: pl.BlockSpec((1, w), lambda i: (0, 0))
    whole = lambda a: pl.BlockSpec(a.shape, lambda i: (0, 0))
    classes = lambda a, d: a.reshape(t_all // seq * d, seq // d, a.shape[-1])
    return pl.pallas_call(
        body, name="mix_out", grid=(t_all // tm,),
        in_specs=[tok(WIDTH), tok(WIDTH), _class_spec(4, WIDTH, nts), _class_spec(16, WIDTH, nts), tok(LANES),
                  _class_spec(4, LANES, nts), _class_spec(16, LANES, nts), vec(WIDTH), vec(WIDTH), whole(w_out),
                  tok(D_MODEL), pl.BlockSpec((1, 6, D_MODEL), lambda i: (i // nts, 0, 0)), vec(D_MODEL), vec(D_MODEL)]
        + [whole(p) for p in perms],
        out_specs=[tok(WIDTH), tok(LANES), _class_spec(4, LANES, nts), _class_spec(16, LANES, nts), tok(D_MODEL),
                   tok(D_MODEL), tok(D_MODEL), tok(LANES), tok(D_MODEL), pl.BlockSpec((1, D_MODEL, tm), lambda i: (i // (FFN_TM // tm), 0, i % (FFN_TM // tm)))],
        out_shape=[jax.ShapeDtypeStruct((t_all, WIDTH), F32), jax.ShapeDtypeStruct((t_all, LANES), F32),
                   _class_shape(t_all, seq, 4, LANES, F32), _class_shape(t_all, seq, 16, LANES, F32),
                   jax.ShapeDtypeStruct((t_all, D_MODEL), BF16), jax.ShapeDtypeStruct((t_all, D_MODEL), BF16),
                   jax.ShapeDtypeStruct((t_all, D_MODEL), F32), jax.ShapeDtypeStruct((t_all, LANES), F32),
                   jax.ShapeDtypeStruct((t_all, D_MODEL), BF16),
                   jax.ShapeDtypeStruct((t_all // FFN_TM, D_MODEL, FFN_TM), BF16)],
        compiler_params=_params(),
    )(oa, o3[0], classes(o3[1], 4), classes(o3[2], 16), l3[0], classes(l3[1], 4), classes(l3[2], 16), gn_a, gn_b,
      w_out, x, ada3, ln_g, ln_b, *perms)


def _mix_out_bwd(dmix, w_out, oa, ob, gn_a, gn_b, perms, seq):
    t_all = dmix.shape[0]
    tm = TOK_TM
    nts = seq // tm

    def body(dm_ref, w_ref, oa_ref, ob_ref, ga_ref, gb_ref, p4_ref, p16_ref, doa_ref, dob_ref, dob4_ref, dob16_ref,
             dla_ref, dlb_ref, dlb4_ref, dlb16_ref, acc_ref):
        @pl.when(pl.program_id(0) == 0)
        def _():
            acc_ref[...] = jnp.zeros_like(acc_ref)
        e, et = _head_mats()
        dmg = _nt(dm_ref[...], w_ref[...])

        def group(o, dn, gain):
            rr = lax.rsqrt(_hsum(o * o, et) * (1.0 / HEAD_DIM) + RMS_EPS)
            re = _hexp(rr, e)
            dgain = jnp.sum(dn * o * re, axis=0, keepdims=True)
            dxn = dn * gain
            tt = _hsum(dxn * o, et) * (rr * rr * rr) * (1.0 / HEAD_DIM)
            do = re * dxn - o * _hexp(tt, e)
            return do, _hsum(do * o, et), dgain

        doa, dla, dga = group(oa_ref[...], dmg[:, :WIDTH], ga_ref[...])
        dob, dlb, dgb = group(ob_ref[...], dmg[:, WIDTH:], gb_ref[...])
        dob = dob.astype(BF16)
        doa_ref[...] = doa.astype(BF16)
        dob_ref[...] = dob
        _store_classes(dob4_ref, _nn(p4_ref[...], dob).astype(BF16), 4)
        _store_classes(dob16_ref, _nn(p16_ref[...], dob).astype(BF16), 16)
        dla_ref[...] = dla
        dlb_ref[...] = dlb
        _store_classes(dlb4_ref, _permute_f32(p4_ref[...], dlb), 4)
        _store_classes(dlb16_ref, _permute_f32(p16_ref[...], dlb), 16)
        acc_ref[0:1, :] += jnp.concatenate([dga, dgb], axis=1)

    tok = lambda w: pl.BlockSpec((tm, w), lambda i: (i, 0))
    vec = lambda w: pl.BlockSpec((1, w), lambda i: (0, 0))
    return pl.pallas_call(
        body, name="mix_out_bwd", grid=(t_all // tm,),
        in_specs=[tok(D_MODEL), pl.BlockSpec(w_out.shape, lambda i: (0, 0)), tok(WIDTH), tok(WIDTH), vec(WIDTH),
                  vec(WIDTH), pl.BlockSpec(perms[0].shape, lambda i: (0, 0)),
                  pl.BlockSpec(perms[1].shape, lambda i: (0, 0))],
        out_specs=[tok(WIDTH), tok(WIDTH), _class_spec(4, WIDTH, nts), _class_spec(16, WIDTH, nts), tok(LANES),
                   tok(LANES), _class_spec(4, LANES, nts), _class_spec(16, LANES, nts),
                   pl.BlockSpec((8, D_MODEL), lambda i: (0, 0))],
        out_shape=[jax.ShapeDtypeStruct((t_all, WIDTH), BF16), jax.ShapeDtypeStruct((t_all, WIDTH), BF16),
                   _class_shape(t_all, seq, 4, WIDTH, BF16), _class_shape(t_all, seq, 16, WIDTH, BF16),
                   jax.ShapeDtypeStruct((t_all, LANES), F32), jax.ShapeDtypeStruct((t_all, LANES), F32),
                   _class_shape(t_all, seq, 4, LANES, F32), _class_shape(t_all, seq, 16, LANES, F32),
                   jax.ShapeDtypeStruct((8, D_MODEL), F32)],
        compiler_params=_params(),
    )(dmix, w_out, oa, ob, gn_a, gn_b, perms[0], perms[1])


def _inproj_bwd(dqt, dka, dva, dil1, dil4, dil16, dfa16, pos, wqkv, wf16, freq, perms, dr1, x, ada3, seq):
    t_all = x.shape[0]
    tm = TOK_TM
    nts = seq // tm

    def body(dqt_ref, dka_ref, dva_ref, q1_ref, k1_ref, v1_ref, q4_ref, k4_ref, v4_ref, q16_ref, k16_ref, v16_ref,
             dfa_ref, pos_ref, w_ref, wf_ref, fr_ref, pt4_ref, pt16_ref, dr1_ref, x_ref, ada_ref, gx_ref, dz_ref,
             acc_ref):
        i = pl.program_id(0)

        @pl.when(i == 0)
        def _():
            acc_ref[...] = jnp.zeros_like(acc_ref)
        tabs = _rope_tabs(pos_ref, fr_ref, -1.0)
        dz_ref[:, :WIDTH] = dqt_ref[...].T.astype(BF16)
        dz_ref[:, WIDTH:2 * WIDTH] = dka_ref[...]
        dz_ref[:, 2 * WIDTH:3 * WIDTH] = dva_ref[...]
        for t, (n1, n4, n16) in enumerate(((q1_ref, q4_ref, q16_ref), (k1_ref, k4_ref, k16_ref),
                                           (v1_ref, v4_ref, v16_ref))):
            tot = (n1[...].astype(F32) + _nn(pt4_ref[...], _load_classes(n4, 4))
                   + _nn(pt16_ref[...], _load_classes(n16, 16)))
            if t < 2:
                tot = _rope(tot, tabs)
            dz_ref[:, (3 + t) * WIDTH:(4 + t) * WIDTH] = tot.astype(BF16)
        dh1 = _tn(dfa_ref[...], wf_ref[...])
        for n in range(6):
            cs = slice(n * WIDTH, (n + 1) * WIDTH)
            dh1 = dh1 + _nt(dz_ref[:, cs], w_ref[:, cs])
        xv = x_ref[...]
        gx_ref[...] = ALPHA * dr1_ref[...] + dh1 * (1.0 + ada_ref[0, 1:2, :])
        b = i // nts
        acc_ref[pl.ds(b, 1), :] += jnp.sum(dh1 * xv, axis=0, keepdims=True)
        acc_ref[pl.ds(8 + b, 1), :] += jnp.sum(dh1, axis=0, keepdims=True)

    tok = lambda w: pl.BlockSpec((tm, w), lambda i: (i, 0))
    whole = lambda a: pl.BlockSpec(a.shape, lambda i: (0, 0))
    classes = lambda a, d: a.reshape(t_all // seq * d, seq // d, a.shape[-1])
    return pl.pallas_call(
        body, name="inproj_bwd", grid=(t_all // tm,),
        in_specs=[pl.BlockSpec((WIDTH, tm), lambda i: (i // nts, i % nts)), tok(WIDTH), tok(WIDTH)]
        + [tok(WIDTH)] * 3 + [_class_spec(4, WIDTH, nts)] * 3 + [_class_spec(16, WIDTH, nts)] * 3
        + [pl.BlockSpec((16, tm), lambda i: (0, i)), tok(1), whole(wqkv), whole(wf16),
           pl.BlockSpec((1, LANES), lambda i: (0, 0)), whole(perms[2]), whole(perms[3]), tok(D_MODEL), tok(D_MODEL),
           pl.BlockSpec((1, 6, D_MODEL), lambda i: (i // nts, 0, 0))],
        out_specs=[tok(D_MODEL), tok(6 * WIDTH), pl.BlockSpec((16, D_MODEL), lambda i: (0, 0))],
        out_shape=[jax.ShapeDtypeStruct((t_all, D_MODEL), F32), jax.ShapeDtypeStruct((t_all, 6 * WIDTH), BF16),
                   jax.ShapeDtypeStruct((16, D_MODEL), F32)],
        compiler_params=_params(),
    )(dqt, dka, dva, *dil1, *[classes(a, 4) for a in dil4], *[classes(a, 16) for a in dil16], dfa16, pos, wqkv, wf16,
      freq, perms[2], perms[3], dr1, x, ada3)


FFN_TM = 1024
FFN_TN = 256
HALO = 8


FFN_CHUNK = 256


def _conv_params(cw_ref, cb_ref, n, tn):
    a = pl.ds(pl.multiple_of(n * tn, tn), tn)
    g = pl.ds(pl.multiple_of(D_FF + n * tn, tn), tn)
    return cw_ref[:, a], cw_ref[:, g], cb_ref[:, a], cb_ref[:, g]


def _conv(cat_ref, w_ref, b_ref, start, rows, halo=HALO):
    return (b_ref[...] + w_ref[0:1, :] * cat_ref[pl.ds(start + halo - 2, rows), :]
            + w_ref[1:2, :] * cat_ref[pl.ds(start + halo - 1, rows), :]
            + w_ref[2:3, :] * cat_ref[pl.ds(start + halo, rows), :])


def _ffn_up_gate(h2, w_up, conv_w, conv_b, seq):
    t_all = h2.shape[0]
    tm, tn = min(2 * FFN_TM, seq), FFN_TN
    nc = D_FF // tn
    nts = seq // tm
    pre = 16

    def body(h_ref, hp_ref, wua_ref, wug_ref, cw_ref, cb_ref, ua_ref, ug_ref, o_ref, ca_ref, cg_ref):
        first = (pl.program_id(1) % nts) == 0
        wa_ref, wg_ref, ba_ref, bg_ref = _conv_params(cw_ref, cb_ref, pl.program_id(0), tn)
        hcat = jnp.concatenate([hp_ref[...], h_ref[...]], axis=0)
        zero = jnp.zeros((pre, tn), F32)
        for w_ref, cat, u_ref in ((wua_ref, ca_ref, ua_ref), (wug_ref, cg_ref, ug_ref)):
            ub = _nn(hcat, w_ref[...]).astype(BF16)
            ue = ub.astype(F32)
            cat[0:pre, :] = jnp.where(first, zero, ue[0:pre])
            cat[pre:, :] = ue[pre:]
            u_ref[...] = ub[pre:]
        for c0 in range(0, tm, FFN_CHUNK):
            ya = _conv(ca_ref, wa_ref, ba_ref, c0, FFN_CHUNK, pre)
            yg = _conv(cg_ref, wg_ref, bg_ref, c0, FFN_CHUNK, pre)
            o_ref[c0:c0 + FFN_CHUNK, :] = (yg * jax.nn.sigmoid(yg) * ya).astype(BF16)

    wcol = lambda off: pl.BlockSpec((D_MODEL, tn), lambda n, t: (0, n + off))
    tile = pl.BlockSpec((tm, tn), lambda n, t: (t, n))
    return pl.pallas_call(
        body, name="ffn_up_gate", grid=(nc, t_all // tm),
        in_specs=[pl.BlockSpec((tm, D_MODEL), lambda n, t: (t, 0)),
                  pl.BlockSpec((pre, D_MODEL), lambda n, t: (jnp.maximum(t * (tm // pre) - 1, 0), 0)),
                  wcol(0), wcol(nc), pl.BlockSpec(conv_w.shape, lambda n, t: (0, 0)),
                  pl.BlockSpec(conv_b.shape, lambda n, t: (0, 0))],
        out_specs=[tile, tile, tile],
        out_shape=[jax.ShapeDtypeStruct((t_all, D_FF), BF16)] * 3,
        scratch_shapes=[pltpu.VMEM((tm + pre, tn), F32)] * 2, compiler_params=_params(),
    )(h2, h2, w_up, w_up, conv_w, conv_b)


def _ffn_gate_bwd(u_a, u_g, dfi, conv_w, conv_b, h2t, seq):
    t_all = u_a.shape[0]
    tm, tn = FFN_TM, FFN_TN
    nc = D_FF // tn
    nts = seq // tm

    def body(ua_ref, uap_ref, uan_ref, ug_ref, ugp_ref, ugn_ref, df_ref, dfn_ref, cw_ref, cb_ref, h_ref,
             dua_ref, dug_ref, acca_ref, accg_ref, dwa_ref, dwg_ref, ca_ref, cg_ref, ya_ref, yg_ref, dwa_sc, dwg_sc,
             out_sems):
        t = pl.program_id(0)
        n = pl.program_id(1)
        cols = pl.ds(pl.multiple_of(n * tn, tn), tn)
        first = (t % nts) == 0
        last = (t % nts) == nts - 1

        @pl.when((t == 0) & (n == 0))
        def _():
            acca_ref[...] = jnp.zeros_like(acca_ref)
            accg_ref[...] = jnp.zeros_like(accg_ref)
            dwa_sc[...] = jnp.zeros_like(dwa_sc)
            dwg_sc[...] = jnp.zeros_like(dwg_sc)
        wa_ref, wg_ref, ba_ref, bg_ref = _conv_params(cw_ref, cb_ref, n, tn)
        zero = jnp.zeros((HALO, tn), F32)
        for cat, cur, prv, nxt in ((ca_ref, ua_ref, uap_ref, uan_ref), (cg_ref, ug_ref, ugp_ref, ugn_ref)):
            cat[0:HALO, :] = jnp.where(first, zero, prv[...].astype(F32)[HALO:])
            cat[HALO:HALO + tm, :] = cur[...].astype(F32)
            cat[HALO + tm:, :] = nxt[...].astype(F32)[:HALO]
        ch = FFN_CHUNK
        sums = [[jnp.zeros((1, tn), F32) for _ in range(4)] for _ in range(2)]
        for ci, c0 in enumerate(range(0, tm, ch)):
            ya = _conv(ca_ref, wa_ref, ba_ref, c0, ch + HALO)
            yg = _conv(cg_ref, wg_ref, bg_ref, c0, ch + HALO)
            if c0 + ch < tm:
                beyond = df_ref[c0 + ch:c0 + ch + 16, :].astype(F32)[:HALO]
            else:
                beyond = jnp.where(last, 0.0, dfn_ref[...].astype(F32)[:HALO])
            dfe = jnp.concatenate([df_ref[c0:c0 + ch, :].astype(F32), beyond], axis=0)
            sg = jax.nn.sigmoid(yg)
            ya_ref[ci] = dfe * (yg * sg)
            yg_ref[ci] = dfe * ya * (sg * (1.0 + yg * (1.0 - sg)))
            for half, (dy, cat, w_ref, du_ref) in enumerate(((ya_ref, ca_ref, wa_ref, dua_ref),
                                                             (yg_ref, cg_ref, wg_ref, dug_ref))):
                d0 = dy[ci, 0:ch, :]
                du = (w_ref[2:3, :] * d0 + w_ref[1:2, :] * dy[ci, pl.ds(1, ch), :]
                      + w_ref[0:1, :] * dy[ci, pl.ds(2, ch), :])
                du_ref[c0:c0 + ch, :] = du.astype(BF16)
                for k in range(3):
                    sums[half][k] += jnp.sum(d0 * cat[pl.ds(c0 + HALO - 2 + k, ch), :], axis=0, keepdims=True)
                sums[half][3] += jnp.sum(d0, axis=0, keepdims=True)
        for half, acc in enumerate((acca_ref, accg_ref)):
            for k in range(4):
                acc[k:k + 1, cols] += sums[half][k]
        ht = h_ref[0]
        dwa_sc[:, cols] += _nn(ht, dua_ref[...])
        dwg_sc[:, cols] += _nn(ht, dug_ref[...])

        @pl.when((t == t_all // tm - 1) & (n == nc - 1))
        def _():
            copies = [pltpu.make_async_copy(dwa_sc, dwa_ref, out_sems.at[0]),
                      pltpu.make_async_copy(dwg_sc, dwg_ref, out_sems.at[1])]
            for cp in copies:
                cp.start()
            for cp in copies:
                cp.wait()

    nrow = t_all // 16
    cur = pl.BlockSpec((tm, tn), lambda t, n: (t, n))
    prev = pl.BlockSpec((16, tn), lambda t, n: (jnp.maximum(t * (tm // 16) - 1, 0), n))
    nxt = pl.BlockSpec((16, tn), lambda t, n: (jnp.minimum((t + 1) * (tm // 16), nrow - 1), n))
    acc = pl.BlockSpec((8, D_FF), lambda t, n: (0, 0))
    return pl.pallas_call(
        body, name="ffn_gate_bwd", grid=(t_all // tm, nc),
        in_specs=[cur, prev, nxt, cur, prev, nxt, cur, nxt, pl.BlockSpec(conv_w.shape, lambda t, n: (0, 0)),
                  pl.BlockSpec(conv_b.shape, lambda t, n: (0, 0)),
                  pl.BlockSpec((1, D_MODEL, tm), lambda t, n: (t, 0, 0))],
        out_specs=[cur, cur, acc, acc, ANY, ANY],
        out_shape=[jax.ShapeDtypeStruct((t_all, D_FF), BF16), jax.ShapeDtypeStruct((t_all, D_FF), BF16),
                   jax.ShapeDtypeStruct((8, D_FF), F32), jax.ShapeDtypeStruct((8, D_FF), F32),
                   jax.ShapeDtypeStruct((D_MODEL, D_FF), F32), jax.ShapeDtypeStruct((D_MODEL, D_FF), F32)],
        scratch_shapes=[pltpu.VMEM((tm + 2 * HALO, tn), F32)] * 2
        + [pltpu.VMEM((tm // FFN_CHUNK, FFN_CHUNK + HALO, tn), F32)] * 2
        + [pltpu.VMEM((D_MODEL, D_FF), F32)] * 2 + [pltpu.SemaphoreType.DMA((2,))],
        compiler_params=_params(),
    )(u_a, u_a, u_a, u_g, u_g, u_g, dfi, dfi, conv_w, conv_b, h2t)


def _ffn_down(ffn_in, w_down, xh1, ln1_g, ln1_b, ada3, ln2_g, ln2_b, target, seq):
    t_all = xh1.shape[0]
    tm = 512
    nts = seq // tm

    def body(f_ref, w_ref, xh_ref, g1_ref, b1_ref, ada_ref, g2_ref, b2_ref, tg_ref, dr2_ref, acc_ref):
        i = pl.program_id(0)

        @pl.when(i == 0)
        def _():
            acc_ref[...] = jnp.zeros_like(acc_ref)
        ffn = _nn(f_ref[...], w_ref[...])
        x1 = xh_ref[...] * g1_ref[...] + b1_ref[...]
        r2 = ALPHA * x1 + ada_ref[0, 5:6, :] * ffn
        d = r2 - jnp.mean(r2, axis=1, keepdims=True)
        rstd = lax.rsqrt(jnp.mean(d * d, axis=1, keepdims=True) + LN_EPS)
        xh2 = d * rstd
        diff = xh2 * g2_ref[...] + b2_ref[...] - tg_ref[...]
        dy = diff * (1.0 / D_MODEL)
        dr2 = _layer_norm_bwd(dy * g2_ref[...], xh2, rstd)
        dr2_ref[...] = dr2
        acc_ref[0:1, :] += jnp.sum(dy * xh2, axis=0, keepdims=True)
        acc_ref[1:2, :] += jnp.sum(dy, axis=0, keepdims=True)
        acc_ref[2:3, :] += jnp.sum(diff * diff, axis=0, keepdims=True) * (0.5 / D_MODEL)
        acc_ref[pl.ds(8 + i // nts, 1), :] += jnp.sum(dr2 * ffn, axis=0, keepdims=True)

    tok = lambda w: pl.BlockSpec((tm, w), lambda i: (i, 0))
    vec = pl.BlockSpec((1, D_MODEL), lambda i: (0, 0))
    return pl.pallas_call(
        body, name="ffn_down", grid=(t_all // tm,),
        in_specs=[tok(D_FF), pl.BlockSpec(w_down.shape, lambda i: (0, 0)), tok(D_MODEL), vec, vec,
                  pl.BlockSpec((1, 6, D_MODEL), lambda i: (i // nts, 0, 0)), vec, vec, tok(D_MODEL)],
        out_specs=[tok(D_MODEL), pl.BlockSpec((16, D_MODEL), lambda i: (0, 0))],
        out_shape=[jax.ShapeDtypeStruct((t_all, D_MODEL), F32), jax.ShapeDtypeStruct((16, D_MODEL), F32)],
        compiler_params=_params(),
    )(ffn_in, w_down, xh1, ln1_g, ln1_b, ada3, ln2_g, ln2_b, target)


def _ffn_down_bwd(dr2, ada3, w_down, seq):
    t_all = dr2.shape[0]
    tm = 512
    nts = seq // tm

    def body(d_ref, ada_ref, w_ref, dffn_ref, dfi_ref):
        dffn = (d_ref[...] * ada_ref[0, 5:6, :]).astype(BF16)
        dffn_ref[...] = dffn
        dfi_ref[...] = _nt(dffn, w_ref[...]).astype(BF16)

    tok = lambda w: pl.BlockSpec((tm, w), lambda i: (i, 0))
    return pl.pallas_call(
        body, name="ffn_down_bwd", grid=(t_all // tm,),
        in_specs=[tok(D_MODEL), pl.BlockSpec((1, 6, D_MODEL), lambda i: (i // nts, 0, 0)),
                  pl.BlockSpec(w_down.shape, lambda i: (0, 0))],
        out_specs=[tok(D_MODEL), tok(D_FF)],
        out_shape=[jax.ShapeDtypeStruct((t_all, D_MODEL), BF16), jax.ShapeDtypeStruct((t_all, D_FF), BF16)],
        compiler_params=_params(),
    )(dr2, ada3, w_down)


def _ffn_up_bwd(du_a, du_g, w_up, dr2, xh1, rs1, mix, ada3, ln1_g, ln1_b, seq):
    t_all = dr2.shape[0]
    tm = 512
    nts = seq // tm

    def body(da_ref, dg_ref, w_ref, dr2_ref, xh_ref, rs_ref, mix_ref, ada_ref, g_ref, b_ref, dr1_ref, dmix_ref,
             acc_ref):
        i = pl.program_id(0)

        @pl.when(i == 0)
        def _():
            acc_ref[...] = jnp.zeros_like(acc_ref)
        dh2 = _nt(da_ref[...], w_ref[:, :D_FF]) + _nt(dg_ref[...], w_ref[:, D_FF:])
        xh = xh_ref[...]
        x1 = xh * g_ref[...] + b_ref[...]
        dx1 = ALPHA * dr2_ref[...] + dh2 * (1.0 + ada_ref[0, 4:5, :])
        dr1 = _layer_norm_bwd(dx1 * g_ref[...], xh, rs_ref[:, 0:1])
        dr1_ref[...] = dr1
        dmix_ref[...] = (dr1 * ada_ref[0, 2:3, :]).astype(BF16)
        b = i // nts
        acc_ref[0:1, :] += jnp.sum(dx1 * xh, axis=0, keepdims=True)
        acc_ref[1:2, :] += jnp.sum(dx1, axis=0, keepdims=True)
        acc_ref[pl.ds(8 + b, 1), :] += jnp.sum(dh2 * x1, axis=0, keepdims=True)
        acc_ref[pl.ds(16 + b, 1), :] += jnp.sum(dh2, axis=0, keepdims=True)
        acc_ref[pl.ds(24 + b, 1), :] += jnp.sum(dr1 * mix_ref[...].astype(F32), axis=0, keepdims=True)

    tok = lambda w: pl.BlockSpec((tm, w), lambda i: (i, 0))
    vec = pl.BlockSpec((1, D_MODEL), lambda i: (0, 0))
    return pl.pallas_call(
        body, name="ffn_up_bwd", grid=(t_all // tm,),
        in_specs=[tok(D_FF), tok(D_FF), pl.BlockSpec(w_up.shape, lambda i: (0, 0)), tok(D_MODEL), tok(D_MODEL),
                  tok(LANES), tok(D_MODEL), pl.BlockSpec((1, 6, D_MODEL), lambda i: (i // nts, 0, 0)), vec, vec],
        out_specs=[tok(D_MODEL), tok(D_MODEL), pl.BlockSpec((32, D_MODEL), lambda i: (0, 0))],
        out_shape=[jax.ShapeDtypeStruct((t_all, D_MODEL), F32), jax.ShapeDtypeStruct((t_all, D_MODEL), BF16),
                   jax.ShapeDtypeStruct((32, D_MODEL), F32)],
        compiler_params=_params(),
    )(du_a, du_g, w_up, dr2, xh1, rs1, mix, ada3, ln1_g, ln1_b)


def _rows(a):
    return a[:, :N_HEADS].T


def _rope_freq():
    f = np.float32(ROPE_THETA) ** (-np.arange(0, ROPE_DIMS, 2, dtype=np.float32) / np.float32(ROPE_DIMS))
    return jnp.asarray(np.tile(f.astype(np.float32), LANES // (ROPE_DIMS // 2))[None, :])


def _local_step(x, positions, target, ada3, w_in, b_fgate, gn_a, gn_b, ln1_g, ln1_b, conv_b, ln2_g, ln2_b,
                late_shards):
    nbat, seq, _ = x.shape
    t_all = nbat * seq
    xf = x.reshape(t_all, D_MODEL)
    tg = target.reshape(t_all, D_MODEL)
    pos = positions.reshape(t_all, 1)
    freq = _rope_freq()

    wqkv = jnp.concatenate([w_in[:, :3 * WIDTH], w_in[:, 3 * WIDTH + N_HEADS:]], axis=1)
    wf16 = jnp.zeros((16, D_MODEL), BF16).at[:N_HEADS].set(w_in[:, 3 * WIDTH:3 * WIDTH + N_HEADS].T)
    bf = b_fgate.reshape(N_HEADS, 1)

    perms = [_perm_matrix(TOK_TM, d, tr) for tr in (False, True) for d in DILATIONS[1:]]
    h1, za, zb1, zb4, zb16, vt, fa_t = _inproj(xf, ada3, pos, wqkv, wf16, freq, perms, seq)
    zbs = [zb1, zb4.reshape(t_all, 3 * WIDTH), zb16.reshape(t_all, 3 * WIDTH)]
    f_row = _fgate_fwd(fa_t, bf, seq)
    f_col = jnp.zeros((t_all, LANES), F32).at[:, :N_HEADS].set(f_row.T * LOG2E)
    oa, lse_row_a, gathered = _fox_fwd(za, vt, f_col, seq, [late_shards[n] for n in LATE])
    w_out, w_up, conv_w, w_down = (_full_from_gathered(n, g) for n, g in zip(LATE, gathered))
    o3, l3 = zip(*[_dil_fwd(zb, seq, d) for zb, d in zip(zbs, DILATIONS)])
    ob, lse_b, lse_b4, lse_b16, merged, mix, xh1, rs1, h2, h2t = _mix_out(oa, o3, l3, gn_a, gn_b, w_out, xf, ada3, ln1_g,
                                                                      ln1_b, perms, seq)
    u_a, u_g, ffn_in = _ffn_up_gate(h2, w_up, conv_w, conv_b, seq)
    dr2, acc2 = _ffn_down(ffn_in, w_down, xh1, ln1_g, ln1_b, ada3, ln2_g, ln2_b, tg, seq)

    dffn, dfi = _ffn_down_bwd(dr2, ada3, w_down, seq)
    d_w_down = _matmul_tn(dffn, ffn_in, 512, 512, "dw_down").T
    du_a, du_g, acc_ca, acc_cg, dw_up_a, dw_up_g = _ffn_gate_bwd(u_a, u_g, dfi, conv_w, conv_b, h2t, seq)
    dr1, dmix, acc1 = _ffn_up_bwd(du_a, du_g, w_up, dr2, xh1, rs1, mix, ada3, ln1_g, ln1_b, seq)
    d_w_up = jnp.concatenate([dw_up_a, dw_up_g], axis=1)

    doa, dob, dob4, dob16, dl_a, dl_b, dl_b4, dl_b16, acc_gn = _mix_out_bwd(dmix, w_out, oa, ob, gn_a, gn_b, perms, seq)
    d_w_out = _matmul_tn(merged, dmix, 512, 512, "dw_out")
    late_grads = dict(w_out=d_w_out, w_up=d_w_up, conv_w=jnp.concatenate([acc_ca[0:3], acc_cg[0:3]], axis=1),
                      w_down=d_w_down)
    dka, dva, df_k, dqt, df_q, late_parts = _fox_bwd(za, doa, f_col, lse_row_a, _rows(dl_a), seq,
                                                     [_payload(n, _dest_major(n, late_grads[n])) for n in LATE])
    dfa_t, dbf = _fgate_bwd(_rows(df_k) + df_q, fa_t, bf, seq)
    flat = lambda a: a.reshape(t_all, a.shape[-1])
    dil = []
    for zb, d, do, lse, dl in zip(zbs, DILATIONS, (dob, flat(dob4), flat(dob16)),
                                  (lse_b, flat(lse_b4), flat(lse_b16)), (dl_b, flat(dl_b4), flat(dl_b16))):
        dil.append(_dil_bwd(zb, do, lse, dl, seq, d))
    dfa16 = jnp.zeros((16, t_all), BF16).at[:N_HEADS].set(dfa_t.astype(BF16))
    grad_x, dz, acc0 = _inproj_bwd(dqt, dka, dva, dil[0], dil[1], dil[2], dfa16, pos, wqkv, wf16, freq, perms, dr1, xf,
                                   ada3, seq)
    d_wqkv = _matmul_tn(h1, dz, 512, 512, "dw_in")
    d_wf = _matmul_rows(dfa16, h1, 512, "dw_fgate")[:N_HEADS].T
    d_w_in = jnp.concatenate([d_wqkv[:, :3 * WIDTH], d_wf, d_wqkv[:, 3 * WIDTH:]], axis=1)

    dada = jnp.concatenate([acc0[8:8 + nbat], acc0[:nbat], acc1[24:24 + nbat], acc1[16:16 + nbat], acc1[8:8 + nbat],
                            acc2[8:8 + nbat]], axis=1)

    grads = dict(
        dada=dada, b_ada=jnp.sum(dada, axis=0, keepdims=True), w_in=d_w_in, b_fgate=dbf[:, 0][None, :],
        gn_a=acc_gn[0:1, :WIDTH], gn_b=acc_gn[0:1, WIDTH:], ln1_g=acc1[0:1], ln1_b=acc1[1:2],
        conv_b=jnp.concatenate([acc_ca[3:4], acc_cg[3:4]], axis=1), ln2_g=acc2[0:1], ln2_b=acc2[1:2])
    return acc2[2:3], grad_x.reshape(x.shape), grads, dict(zip(LATE, late_parts))


LATE = ("w_out", "w_up", "conv_w", "w_down")
BIG = ("w_ada", "w_in") + LATE
COLUMN_SHARDED = ("w_ada", "w_in", "w_up", "conv_w")


def _payload(name, a):
    return a if name == "conv_w" else a.astype(BF16)
SMALL = ("b_ada", "b_fgate", "gn_a", "gn_b", "ln1_g", "ln1_b", "conv_b", "ln2_g", "ln2_b")
ADAM_ROWS = dict(w_ada=256, w_in=256, w_out=128, w_up=256, conv_w=3, w_down=176)
SMALL_ROWS = 24


def _full_from_gathered(name, g):
    if name in COLUMN_SHARDED:
        return g.transpose(1, 0, 2).reshape(g.shape[1], N_DEV * g.shape[2])
    return g.reshape(N_DEV * g.shape[1], g.shape[2])


def _dest_major(name, full):
    if name in COLUMN_SHARDED:
        r, cfull = full.shape
        return full.reshape(r, N_DEV, cfull // N_DEV).transpose(1, 0, 2)
    return full.reshape(N_DEV, full.shape[0] // N_DEV, full.shape[1])


def _pack_small(vals, extra=None):
    parts = [vals[n].reshape(-1) for n in SMALL]
    if extra is not None:
        parts.append(extra.reshape(-1))
    flat = jnp.concatenate(parts)
    return jnp.pad(flat, (0, SMALL_ROWS * D_MODEL - flat.shape[0])).reshape(SMALL_ROWS, D_MODEL)


def _unpack_small(packed, like):
    flat = packed.reshape(-1)
    out, off = {}, 0
    for n in SMALL:
        size = like[n].size
        out[n] = flat[off:off + size].reshape(like[n].shape)
        off += size
    return out, flat[off:off + D_MODEL]


def kernel(x, c, positions, w_ada, b_ada, w_in, b_fgate, gn_a, gn_b, w_out, ln1_g, ln1_b, w_up, conv_w, conv_b, w_down, ln2_g, ln2_b, loss_target, m_w_ada, m_b_ada, m_w_in, m_b_fgate, m_gn_a, m_gn_b, m_w_out, m_ln1_g, m_ln1_b, m_w_up, m_conv_w, m_conv_b, m_w_down, m_ln2_g, m_ln2_b, v_w_ada, v_b_ada, v_w_in, v_b_fgate, v_gn_a, v_gn_b, v_w_out, v_ln1_g, v_ln1_b, v_w_up, v_conv_w, v_conv_b, v_w_down, v_ln2_g, v_ln2_b):
    w = dict(w_ada=w_ada[0], b_ada=b_ada, w_in=w_in[0], b_fgate=b_fgate, gn_a=gn_a, gn_b=gn_b, w_out=w_out[0],
             ln1_g=ln1_g, ln1_b=ln1_b, w_up=w_up[0], conv_w=conv_w[0], conv_b=conv_b, w_down=w_down[0], ln2_g=ln2_g,
             ln2_b=ln2_b)
    m = dict(w_ada=m_w_ada[0], b_ada=m_b_ada, w_in=m_w_in[0], b_fgate=m_b_fgate, gn_a=m_gn_a, gn_b=m_gn_b,
             w_out=m_w_out[0], ln1_g=m_ln1_g, ln1_b=m_ln1_b, w_up=m_w_up[0], conv_w=m_conv_w[0], conv_b=m_conv_b,
             w_down=m_w_down[0], ln2_g=m_ln2_g, ln2_b=m_ln2_b)
    v = dict(w_ada=v_w_ada[0], b_ada=v_b_ada, w_in=v_w_in[0], b_fgate=v_b_fgate, gn_a=v_gn_a, gn_b=v_gn_b,
             w_out=v_w_out[0], ln1_g=v_ln1_g, ln1_b=v_ln1_b, w_up=v_w_up[0], conv_w=v_conv_w[0], conv_b=v_conv_b,
             w_down=v_w_down[0], ln2_g=v_ln2_g, ln2_b=v_ln2_b)

    nbat = x.shape[0]
    me = 4 * lax.axis_index("x") + 2 * lax.axis_index("y") + lax.axis_index("c")
    ada_cols = w["w_ada"].shape[1]

    c_all, w_in_all = _gather_two_level([c, _payload("w_in", w["w_in"])], "weight_gather")
    c_all = c_all.reshape(N_DEV * nbat, D_MODEL)
    ada_mine = _ada_fwd(c_all, w["w_ada"], lax.dynamic_slice(b_ada, (0, me * ada_cols), (1, ada_cols)))
    (ada_parts,) = _exchange([ada_mine.reshape(N_DEV, nbat, ada_cols)], [False], "ada_exchange")
    ada3 = ada_parts.transpose(1, 0, 2).reshape(nbat, 6, D_MODEL)

    loss_lanes, grad_x, g_local, parts = _local_step(
        x, positions, loss_target, ada3, _full_from_gathered("w_in", w_in_all), b_fgate, gn_a, gn_b, ln1_g, ln1_b,
        conv_b, ln2_g, ln2_b, {n: _payload(n, w[n]) for n in LATE})

    parts["w_in"], dada_all, small_all = _exchange(
        [_payload("w_in", _dest_major("w_in", g_local["w_in"])), g_local["dada"], _pack_small(g_local, loss_lanes)],
        [False, True, True], "grad_exchange")
    dada_cols = lax.dynamic_slice(dada_all.reshape(N_DEV * nbat, 6 * D_MODEL), (0, me * ada_cols),
                                  (N_DEV * nbat, ada_cols))
    parts["w_ada"] = _ada_bwd(c_all, dada_cols)[None]

    grad, delta, new_m, new_v = {}, {}, {}, {}
    for n in BIG:
        grad[n], delta[n], new_m[n], new_v[n] = (
            a[None] for a in _adamw(parts[n], w[n], m[n], v[n], ADAM_ROWS[n], "adamw_" + n))
    packed = _adamw(small_all, _pack_small(w), _pack_small(m), _pack_small(v), SMALL_ROWS, "adamw_small")
    for dst, pk in zip((grad, delta, new_m, new_v), packed):
        vals, lanes = _unpack_small(pk, w)
        dst.update(vals)
        if dst is grad:
            loss = jnp.sum(lanes)

    order = ("w_ada", "b_ada", "w_in", "b_fgate", "gn_a", "gn_b", "w_out", "ln1_g", "ln1_b", "w_up", "conv_w", "conv_b",
             "w_down", "ln2_g", "ln2_b")
    return (loss, grad_x, *[grad[n] for n in order], *[delta[n] for n in order], *[new_m[n] for n in order],
            *[new_v[n] for n in order])
```

```python
import functools

import numpy as np
import jax
import jax.numpy as jnp
from jax import lax
from jax.experimental import pallas as pl
from jax.experimental.pallas import tpu as pltpu

F32, BF16 = jnp.float32, jnp.bfloat16
MESH = pl.DeviceIdType.MESH
ANY = pl.BlockSpec(memory_space=pl.ANY)

D_MODEL = 1024
N_HEADS = 8
HEAD_DIM = 64
WIDTH = 512
D_FF = 2816
N_DEV = 8
ROPE_DIMS = 16
ROPE_THETA = 500000.0
ALPHA = 2.0 ** 0.25
LN_EPS = 1e-5
RMS_EPS = 1e-6
NEG = -1e30
Q_SCALE = 0.125
LOG2E = 1.4426950408889634
BLK = 128
LANES = 128
VMEM_LIMIT_BYTES = 56 * 1024 * 1024

ADAM_LR, ADAM_B1, ADAM_B2, ADAM_EPS, ADAM_WD, ADAM_STEP = 0.001, 0.9, 0.999, 1e-08, 0.01, 10


def _params(vmem=VMEM_LIMIT_BYTES):
    return pltpu.CompilerParams(vmem_limit_bytes=vmem)


def _nn(a, b):
    return jnp.dot(a, b, preferred_element_type=F32)


def _nt(a, b):
    return lax.dot_general(a, b, (((1,), (1,)), ((), ())), preferred_element_type=F32)


def _tn(a, b):
    return lax.dot_general(a, b, (((0,), (0,)), ((), ())), preferred_element_type=F32)


def _head_mats():
    r = lax.broadcasted_iota(jnp.int32, (LANES, WIDTH), 0)
    c = lax.broadcasted_iota(jnp.int32, (LANES, WIDTH), 1)
    e = ((c >> 6) == r).astype(BF16)
    r2 = lax.broadcasted_iota(jnp.int32, (WIDTH, LANES), 0)
    c2 = lax.broadcasted_iota(jnp.int32, (WIDTH, LANES), 1)
    et = ((r2 >> 6) == c2).astype(BF16)
    return e, et


def _split3(x):
    hi = x.astype(BF16)
    r = x - hi.astype(F32)
    mid = r.astype(BF16)
    return hi, mid, (r - mid.astype(F32)).astype(BF16)


def _hexp(w, e):
    return sum(_nn(part, e) for part in _split3(w)[:2])


def _hsum(x, et):
    return sum(_nn(part, et) for part in _split3(x)[:2])


def _perm_matrix(rows, d, transpose):
    i = np.arange(rows)
    j = (i % (rows // d)) * d + i // (rows // d)
    p = np.zeros((rows, rows), np.float32)
    p[i, j] = 1.0
    return jnp.asarray(p.T if transpose else p, BF16)


def _permute_f32(p, x):
    return sum(_nn(p, part) for part in _split3(x))


def _store_classes(ref, y, d):
    n = y.shape[0] // d
    for r in range(d):
        ref[r] = y[r * n:(r + 1) * n, :]


def _load_classes(ref, d):
    return jnp.concatenate([ref[r] for r in range(d)], axis=0)


def _rope_tabs(pos_ref, fr_ref, sign):
    ang = pos_ref[...].astype(F32) * fr_ref[...]
    lane = lax.broadcasted_iota(jnp.int32, ang.shape, 1) & (HEAD_DIM - 1)
    m1 = lane < ROPE_DIMS // 2
    m2 = (lane >= ROPE_DIMS // 2) & (lane < ROPE_DIMS)
    cos = jnp.cos(ang)
    sin = jnp.sin(ang) * sign
    return (jnp.where(m1 | m2, cos, 1.0), jnp.where(m1, -sin, 0.0), jnp.where(m2, sin, 0.0))


def _rope(z, tabs):
    c, s1, s2 = tabs
    parts = []
    for p in range(z.shape[1] // LANES):
        zp = z[:, LANES * p:LANES * (p + 1)]
        parts.append(zp * c + pltpu.roll(zp, LANES - 8, 1) * s1 + pltpu.roll(zp, 8, 1) * s2)
    return jnp.concatenate(parts, axis=1)


def _half_masks(rows):
    lane = lax.broadcasted_iota(jnp.int32, (rows, LANES), 1)
    lo = lane < HEAD_DIM
    return lo, jnp.logical_not(lo)


def _layer_norm_bwd(dxh, xh, rstd):
    m1 = jnp.mean(dxh, axis=1, keepdims=True)
    m2 = jnp.mean(dxh * xh, axis=1, keepdims=True)
    return rstd * (dxh - m1 - xh * m2)


def _coords():
    return lax.axis_index("x"), lax.axis_index("y"), lax.axis_index("c")


def _peer(x, y, c, k):
    return (1 - x if k & 4 else x, 1 - y if k & 2 else y, 1 - c if k & 1 else c)


def _comm_sems(n):
    return [pltpu.SemaphoreType.DMA((N_DEV - 1, n)), pltpu.SemaphoreType.DMA((N_DEV - 1, n)),
            pltpu.SemaphoreType.DMA((n,))]


def _comm_copies(ins, outs, to_all, sems):
    send_sems, recv_sems, local_sems = sems
    x, y, c = _coords()
    me = 4 * x + 2 * y + c
    copies = [pltpu.make_async_copy(ins[t] if to_all[t] else ins[t].at[me], outs[t].at[me], local_sems.at[t])
              for t in range(len(ins))]
    for k in range(1, N_DEV):
        px, py, pc = _peer(x, y, c, k)
        dest = 4 * px + 2 * py + pc
        for t in range(len(ins)):
            copies.append(pltpu.make_async_remote_copy(
                src_ref=ins[t] if to_all[t] else ins[t].at[dest], dst_ref=outs[t].at[me],
                send_sem=send_sems.at[k - 1, t], recv_sem=recv_sems.at[k - 1, t],
                device_id=(px, py, pc), device_id_type=MESH))
    return copies


def _comm_out_shapes(ins, to_all):
    return [jax.ShapeDtypeStruct(((N_DEV,) + a.shape) if ta else a.shape, a.dtype) for a, ta in zip(ins, to_all)]


def _exchange(ins, to_all, name):
    n = len(ins)

    def body(*refs):
        copies = _comm_copies(refs[:n], refs[n:2 * n], to_all, refs[2 * n:])
        for cp in copies:
            cp.start()
        for cp in copies:
            cp.wait()

    return pl.pallas_call(
        body, name=name, out_shape=_comm_out_shapes(ins, to_all), in_specs=[ANY] * n, out_specs=[ANY] * n,
        scratch_shapes=_comm_sems(n),
    )(*ins)


def _gather_two_level(ins, name):
    n = len(ins)

    def body(*refs):
        srcs, outs = refs[:n], refs[n:2 * n]
        send_sems, recv_sems, local_sems = refs[2 * n:]
        x, y, c = _coords()
        me = 4 * x + 2 * y + c
        sibling = (x, y, 1 - c)
        chips = [(1 - x, y), (x, 1 - y), (1 - x, 1 - y)]
        slot = lambda px, py, pc: 4 * px + 2 * py + pc

        def copy(k, t, block, to, own=False):
            return pltpu.make_async_remote_copy(
                src_ref=srcs[t] if own else outs[t].at[block], dst_ref=outs[t].at[block],
                send_sem=send_sems.at[k, t], recv_sem=recv_sems.at[k, t], device_id=to, device_id_type=MESH)

        local = [pltpu.make_async_copy(srcs[t], outs[t].at[me], local_sems.at[t]) for t in range(n)]
        first = [copy(0, t, me, sibling, own=True) for t in range(n)]
        first += [copy(1 + j, t, me, (*chip, c), own=True) for j, chip in enumerate(chips) for t in range(n)]
        for cp in local + first:
            cp.start()
        passed = []
        for j, chip in enumerate(chips):
            for t in range(n):
                copy(1 + j, t, slot(*chip, c), (x, y, c)).wait_recv()
                cp = copy(4 + j, t, slot(*chip, c), sibling)
                cp.start()
                passed.append(cp)
        for t in range(n):
            copy(0, t, slot(x, y, 1 - c), (x, y, c)).wait_recv()
            for j, chip in enumerate(chips):
                copy(4 + j, t, slot(*chip, 1 - c), (x, y, c)).wait_recv()
        for cp in first + passed:
            cp.wait_send()
        for cp in local:
            cp.wait()

    return pl.pallas_call(
        body, name=name, out_shape=_comm_out_shapes(ins, [True] * n), in_specs=[ANY] * n, out_specs=[ANY] * n,
        scratch_shapes=_comm_sems(n),
    )(*ins)


def _adamw(parts, w, m, v, rows, name):
    n_parts, r_all, cols = parts.shape
    c1 = 1.0 - ADAM_B1 ** ADAM_STEP
    c2 = 1.0 - ADAM_B2 ** ADAM_STEP

    def body(p_ref, w_ref, m_ref, v_ref, g_ref, d_ref, mo_ref, vo_ref):
        g = p_ref[0].astype(F32)
        for s in range(1, n_parts):
            g = g + p_ref[s].astype(F32)
        mn = ADAM_B1 * m_ref[...] + (1.0 - ADAM_B1) * g
        vn = ADAM_B2 * v_ref[...] + (1.0 - ADAM_B2) * (g * g)
        m_hat = mn / c1
        v_hat = vn / c2
        g_ref[...] = g
        d_ref[...] = -ADAM_LR * (m_hat / (jnp.sqrt(v_hat) + ADAM_EPS) + ADAM_WD * w_ref[...])
        mo_ref[...] = mn
        vo_ref[...] = vn

    spec = pl.BlockSpec((rows, cols), lambda i: (i, 0))
    return pl.pallas_call(
        body, name=name, grid=(r_all // rows,),
        in_specs=[pl.BlockSpec((n_parts, rows, cols), lambda i: (0, i, 0)), spec, spec, spec],
        out_specs=[spec] * 4, out_shape=[jax.ShapeDtypeStruct((r_all, cols), F32)] * 4,
        compiler_params=_params(),
    )(parts, w, m, v)


def _matmul_tn(a, b, chunk, tk, name):
    t_all, k1 = a.shape
    n = b.shape[1]

    def body(a_ref, b_ref, o_ref):
        @pl.when(pl.program_id(0) == 0)
        def _():
            o_ref[...] = jnp.zeros_like(o_ref)
        at = a_ref[...].astype(F32).T.astype(BF16)
        for j in range(0, n, chunk):
            cs = slice(j, min(j + chunk, n))
            o_ref[:, cs] += _nn(at, b_ref[:, cs])

    return pl.pallas_call(
        body, name=name, grid=(t_all // tk,),
        in_specs=[pl.BlockSpec((tk, k1), lambda t: (t, 0)), pl.BlockSpec((tk, n), lambda t: (t, 0))],
        out_specs=pl.BlockSpec((k1, n), lambda t: (0, 0)),
        out_shape=jax.ShapeDtypeStruct((k1, n), F32), compiler_params=_params(),
    )(a, b)


def _matmul_rows(a, b, tk, name):
    r, t_all = a.shape
    n = b.shape[1]

    def body(a_ref, b_ref, o_ref):
        @pl.when(pl.program_id(0) == 0)
        def _():
            o_ref[...] = jnp.zeros_like(o_ref)
        o_ref[...] += _nn(a_ref[...], b_ref[...])

    return pl.pallas_call(
        body, name=name, grid=(t_all // tk,),
        in_specs=[pl.BlockSpec((r, tk), lambda t: (0, t)), pl.BlockSpec((tk, n), lambda t: (t, 0))],
        out_specs=pl.BlockSpec((r, n), lambda t: (0, 0)),
        out_shape=jax.ShapeDtypeStruct((r, n), F32), compiler_params=_params(),
    )(a, b)


def _ada_fwd(c_all, w_ada, b_ada):
    whole = lambda a: pl.BlockSpec(a.shape, lambda j: (0, 0))

    def body(c_ref, w_ref, b_ref, o_ref):
        cv = c_ref[...]
        s = (cv * jax.nn.sigmoid(cv)).astype(BF16)
        o_ref[...] = _nn(s, w_ref[...].astype(BF16)) + b_ref[...]

    out = jax.ShapeDtypeStruct((c_all.shape[0], w_ada.shape[1]), F32)
    return pl.pallas_call(
        body, name="ada_fwd", grid=(1,), in_specs=[whole(c_all), whole(w_ada), whole(b_ada)], out_specs=whole(out),
        out_shape=out, compiler_params=_params(),
    )(c_all, w_ada, b_ada)


def _ada_bwd(c_all, dada):
    whole = lambda a: pl.BlockSpec(a.shape, lambda j: (0, 0))

    def body(c_ref, d_ref, o_ref):
        cv = c_ref[...]
        s = (cv * jax.nn.sigmoid(cv)).astype(BF16)
        o_ref[...] = _tn(s, d_ref[...].astype(BF16))

    out = jax.ShapeDtypeStruct((D_MODEL, dada.shape[1]), F32)
    return pl.pallas_call(
        body, name="ada_bwd", grid=(1,), in_specs=[whole(c_all), whole(dada)], out_specs=whole(out), out_shape=out,
        compiler_params=_params(),
    )(c_all, dada)


TOK_TM = 256
DILATIONS = (1, 4, 16)


def _class_spec(d, width, nts):
    return pl.BlockSpec((d, TOK_TM // d, width), lambda i: (i // nts, i % nts, 0))


def _class_shape(t_all, seq, d, width, dtype):
    return jax.ShapeDtypeStruct((t_all // seq * d, seq // d, width), dtype)


def _inproj(x, ada3, pos, wqkv, wf16, freq, perms, seq):
    t_all = x.shape[0]
    tm = TOK_TM
    nts = seq // tm

    def body(x_ref, ada_ref, pos_ref, w_ref, wf_ref, fr_ref, p4_ref, p16_ref, h1_ref, za_ref, zb_ref, zb4_ref,
             zb16_ref, vt_ref, fa_ref):
        h1 = (x_ref[...] * (1.0 + ada_ref[0, 1:2, :]) + ada_ref[0, 0:1, :]).astype(BF16)
        h1_ref[...] = h1
        tabs = _rope_tabs(pos_ref, fr_ref, 1.0)
        for n in range(6):
            z = _nn(h1, w_ref[:, n * WIDTH:(n + 1) * WIDTH])
            if n in (3, 4):
                z = _rope(z, tabs)
            if n in (0, 3):
                z = z * (Q_SCALE * LOG2E)
            if n == 2:
                vt_ref[...] = z.T.astype(BF16)
            dst = za_ref if n < 3 else zb_ref
            dst[:, (n % 3) * WIDTH:(n % 3 + 1) * WIDTH] = z.astype(BF16)
        fa_ref[...] = _nt(wf_ref[...], h1)[:N_HEADS]
        zb = zb_ref[...]
        _store_classes(zb4_ref, _nn(p4_ref[...], zb).astype(BF16), 4)
        _store_classes(zb16_ref, _nn(p16_ref[...], zb).astype(BF16), 16)

    tok = lambda w: pl.BlockSpec((tm, w), lambda i: (i, 0))
    whole = lambda a: pl.BlockSpec(a.shape, lambda i: (0, 0))
    return pl.pallas_call(
        body, name="inproj", grid=(t_all // tm,),
        in_specs=[tok(D_MODEL), pl.BlockSpec((1, 6, D_MODEL), lambda i: (i // nts, 0, 0)), tok(1), whole(wqkv),
                  whole(wf16), pl.BlockSpec((1, LANES), lambda i: (0, 0)), whole(perms[0]), whole(perms[1])],
        out_specs=[tok(D_MODEL), tok(3 * WIDTH), tok(3 * WIDTH), _class_spec(4, 3 * WIDTH, nts),
                   _class_spec(16, 3 * WIDTH, nts), pl.BlockSpec((WIDTH, tm), lambda i: (i // nts, i % nts)),
                   pl.BlockSpec((N_HEADS, tm), lambda i: (0, i))],
        out_shape=[jax.ShapeDtypeStruct((t_all, D_MODEL), BF16), jax.ShapeDtypeStruct((t_all, 3 * WIDTH), BF16),
                   jax.ShapeDtypeStruct((t_all, 3 * WIDTH), BF16), _class_shape(t_all, seq, 4, 3 * WIDTH, BF16),
                   _class_shape(t_all, seq, 16, 3 * WIDTH, BF16),
                   jax.ShapeDtypeStruct((t_all // seq * WIDTH, seq), BF16),
                   jax.ShapeDtypeStruct((N_HEADS, t_all), F32)],
        compiler_params=_params(),
    )(x, ada3, pos, wqkv, wf16, freq, perms[0], perms[1])


def _chunk_rows(a_t, seq):
    t_all = a_t.shape[1]
    return a_t.reshape(N_HEADS, t_all // seq, seq // LANES, LANES).transpose(1, 0, 2, 3).reshape(-1, LANES)


def _unchunk_rows(a, seq):
    nbat = a.shape[0] * LANES // (N_HEADS * seq)
    return a.reshape(nbat, N_HEADS, seq // LANES, LANES).transpose(1, 0, 2, 3).reshape(N_HEADS, nbat * seq)


def _chunk_carry(tot, nchunk, later):
    rows = tot.shape[0]
    r = lax.broadcasted_iota(jnp.int32, (rows, rows), 0)
    c = lax.broadcasted_iota(jnp.int32, (rows, rows), 1)
    sel = ((r // nchunk) == (c // nchunk)) & ((c > r) if later else (c < r))
    mat = sel.astype(BF16)
    return sum(_nn(mat, part) for part in _split3(jnp.broadcast_to(tot, (rows, LANES))))


def _fgate_fwd(fa_t, bf, seq):
    x = _chunk_rows(fa_t, seq)
    rows = x.shape[0]
    nchunk = seq // LANES
    bias = jnp.broadcast_to(bf.reshape(1, N_HEADS, 1), (rows // (N_HEADS * nchunk), N_HEADS, nchunk)).reshape(rows, 1)

    def body(x_ref, b_ref, f_ref):
        lane = lax.broadcasted_iota(jnp.int32, (rows, LANES), 1)
        xv = x_ref[...] + b_ref[...]
        lf = jnp.minimum(xv, 0.0) - jnp.log(1.0 + jnp.exp(-jnp.abs(xv)))
        for s in (1, 2, 4, 8, 16, 32, 64):
            lf = lf + jnp.where(lane >= s, pltpu.roll(lf, s, 1), 0.0)
        f_ref[...] = lf + _chunk_carry(lf[:, LANES - 1:LANES], nchunk, False)

    whole = lambda a: pl.BlockSpec(a.shape, lambda i: (0, 0))
    out = pl.pallas_call(
        body, name="fgate_fwd", grid=(1,), in_specs=[whole(x), whole(bias)], out_specs=whole(x),
        out_shape=jax.ShapeDtypeStruct(x.shape, F32), compiler_params=_params(),
    )(x, bias)
    return _unchunk_rows(out, seq)


def _fgate_bwd(df_t, fa_t, bf, seq):
    d_in = _chunk_rows(df_t, seq)
    x = _chunk_rows(fa_t, seq)
    rows = x.shape[0]
    nchunk = seq // LANES
    bias = jnp.broadcast_to(bf.reshape(1, N_HEADS, 1), (rows // (N_HEADS * nchunk), N_HEADS, nchunk)).reshape(rows, 1)

    def body(d_ref, x_ref, b_ref, o_ref, s_ref):
        lane = lax.broadcasted_iota(jnp.int32, (rows, LANES), 1)
        d = d_ref[...]
        for s in (1, 2, 4, 8, 16, 32, 64):
            d = d + jnp.where(lane < LANES - s, pltpu.roll(d, LANES - s, 1), 0.0)
        d = d + _chunk_carry(d[:, 0:1], nchunk, True)
        dfa = d * jax.nn.sigmoid(-(x_ref[...] + b_ref[...]))
        o_ref[...] = dfa
        g = lax.broadcasted_iota(jnp.int32, (2 * N_HEADS, rows), 0)
        r = lax.broadcasted_iota(jnp.int32, (2 * N_HEADS, rows), 1)
        group = (((r // nchunk) % N_HEADS) == g).astype(BF16)
        per_head = sum(_nn(group, part) for part in _split3(dfa))[:N_HEADS]
        s_ref[...] = jnp.broadcast_to(jnp.sum(per_head, axis=1, keepdims=True), (N_HEADS, LANES))

    whole = lambda a: pl.BlockSpec(a.shape, lambda i: (0, 0))
    dfa, sums = pl.pallas_call(
        body, name="fgate_bwd", grid=(1,), in_specs=[whole(d_in), whole(x), whole(bias)],
        out_specs=[whole(x), pl.BlockSpec((N_HEADS, LANES), lambda i: (0, 0))],
        out_shape=[jax.ShapeDtypeStruct(x.shape, F32), jax.ShapeDtypeStruct((N_HEADS, LANES), F32)],
        compiler_params=_params(),
    )(d_in, x, bias)
    return _unchunk_rows(dfa, seq), sums


FOX_T = 256


def _fox_prep(dst, src_ref, lo, hi):
    for p in range(4):
        v = src_ref[:, LANES * p:LANES * (p + 1)]
        dst[2 * p] = jnp.where(lo, v, jnp.zeros_like(v))
        dst[2 * p + 1] = jnp.where(hi, v, jnp.zeros_like(v))


def _fox_fwd(za, vt, f_col, seq, shards):
    t_all = za.shape[0]
    tq = FOX_T
    nq = seq // tq
    nbat = t_all // seq
    n = len(shards)
    to_all = [True] * n

    def body(*refs):
        q_ref, k_ref, vt_ref, fc_ref = refs[:4]
        o_ref, lse_ref = refs[4 + n:6 + n]
        qm_sc, m_sc, l_sc, acc_sc, a_sc, st_sc, pe_sc = refs[6 + 2 * n:13 + 2 * n]
        comm = (refs[4:4 + n], refs[6 + n:6 + 2 * n], to_all, refs[13 + 2 * n:])
        i = pl.program_id(1)

        @pl.when((pl.program_id(0) == 0) & (i == 0))
        def _():
            for cp in _comm_copies(*comm):
                cp.start()
        lo, hi = _half_masks(tq)
        r = lax.broadcasted_iota(jnp.int32, (tq, tq), 0)
        c = lax.broadcasted_iota(jnp.int32, (tq, tq), 1)
        tri = c >= r
        _fox_prep(qm_sc, q_ref, lo, hi)
        m_sc[...] = jnp.full(m_sc.shape, NEG, F32)
        l_sc[...] = jnp.zeros_like(l_sc)
        acc_sc[...] = jnp.zeros_like(acc_sc)

        def block(j, masked):
            sl = pl.ds(pl.multiple_of(j * tq, tq), tq)
            for p in range(4):
                kj = k_ref[sl, LANES * p:LANES * (p + 1)]
                for h in (2 * p, 2 * p + 1):
                    st = _nt(kj, qm_sc[h]) - fc_ref[sl, h:h + 1]
                    st_sc[h] = jnp.where(tri, st, NEG) if masked else st
            for h in range(N_HEADS):
                st = st_sc[h]
                m = m_sc[h:h + 1, :]
                mn = jnp.maximum(m, jnp.max(st, axis=0, keepdims=True))
                a = jnp.exp2(m - mn)
                pe = jnp.exp2(st - mn)
                m_sc[h:h + 1, :] = mn
                a_sc[h:h + 1, :] = a
                l_sc[h:h + 1, :] = a * l_sc[h:h + 1, :] + jnp.sum(pe, axis=0, keepdims=True)
                pe_sc[h] = pe.astype(BF16)
            for h in range(N_HEADS):
                acc_sc[h] = a_sc[h:h + 1, :] * acc_sc[h] + _nn(vt_ref[HEAD_DIM * h:HEAD_DIM * (h + 1), sl], pe_sc[h])

        def step(j, carry):
            block(j, False)
            return carry

        lax.fori_loop(0, i, step, 0)
        block(i, True)
        lse_ref[...] = m_sc[...] + jnp.log(l_sc[...]) * LOG2E
        for p in range(4):
            ot = jnp.concatenate([acc_sc[h] / l_sc[h:h + 1, :] for h in (2 * p, 2 * p + 1)], axis=0)
            o_ref[:, LANES * p:LANES * (p + 1)] = ot.T

        @pl.when((pl.program_id(0) == nbat - 1) & (i == nq - 1))
        def _():
            for cp in _comm_copies(*comm):
                cp.wait()

    res = pl.pallas_call(
        body, name="fox_fwd", grid=(nbat, nq),
        in_specs=[pl.BlockSpec((tq, WIDTH), lambda b, i: (b * nq + i, 0)),
                  pl.BlockSpec((seq, WIDTH), lambda b, i: (b, 1)), pl.BlockSpec((WIDTH, seq), lambda b, i: (b, 0)),
                  pl.BlockSpec((seq, LANES), lambda b, i: (b, 0))] + [ANY] * n,
        out_specs=[pl.BlockSpec((tq, WIDTH), lambda b, i: (b * nq + i, 0)),
                   pl.BlockSpec((N_HEADS, tq), lambda b, i: (0, b * nq + i))] + [ANY] * n,
        out_shape=[jax.ShapeDtypeStruct((t_all, WIDTH), F32), jax.ShapeDtypeStruct((N_HEADS, t_all), F32)]
        + _comm_out_shapes(shards, to_all),
        scratch_shapes=[pltpu.VMEM((N_HEADS, tq, LANES), BF16), pltpu.VMEM((N_HEADS, tq), F32),
                        pltpu.VMEM((N_HEADS, tq), F32), pltpu.VMEM((N_HEADS, HEAD_DIM, tq), F32),
                        pltpu.VMEM((N_HEADS, tq), F32), pltpu.VMEM((N_HEADS, tq, tq), F32),
                        pltpu.VMEM((N_HEADS, tq, tq), BF16)] + _comm_sems(n),
        compiler_params=_params(),
    )(za, za, vt, f_col, *shards)
    return res[0], res[1], res[2:]


def _fox_bwd(za, do, f_col, lse_row, dl_row, seq, grads):
    t_all = za.shape[0]
    tk = FOX_T
    nk = seq // tk
    nbat = t_all // seq
    n = len(grads)
    to_all = [False] * n

    def body(*refs):
        k_ref, v_ref, q_ref, do_ref, fc_ref, lr_ref, dr_ref = refs[:7]
        dk_ref, dv_ref, df_ref, dqt_ref, dfq_ref = refs[7 + n:12 + n]
        km_sc, vm_sc, fk_sc, dk_sc, dv_sc, cs_sc, kt_sc, st_sc, dp_sc, pt_sc, ds_sc = refs[12 + 2 * n:23 + 2 * n]
        comm = (refs[7:7 + n], refs[12 + n:12 + 2 * n], to_all, refs[23 + 2 * n:])
        j = pl.program_id(1)

        @pl.when(j == 0)
        def _():
            dqt_ref[...] = jnp.zeros_like(dqt_ref)
            dfq_ref[...] = jnp.zeros_like(dfq_ref)

        @pl.when((pl.program_id(0) == 0) & (j == 0))
        def _():
            for cp in _comm_copies(*comm):
                cp.start()
        lo, hi = _half_masks(tk)
        r = lax.broadcasted_iota(jnp.int32, (tk, tk), 0)
        c = lax.broadcasted_iota(jnp.int32, (tk, tk), 1)
        tri = c >= r
        _fox_prep(km_sc, k_ref, lo, hi)
        _fox_prep(vm_sc, v_ref, lo, hi)
        for h in range(N_HEADS):
            fk_sc[h] = jnp.broadcast_to(fc_ref[:, h:h + 1], (tk, tk))
        for p in range(4):
            kt_sc[p] = k_ref[:, LANES * p:LANES * (p + 1)].astype(F32).T.astype(BF16)
        dk_sc[...] = jnp.zeros_like(dk_sc)
        dv_sc[...] = jnp.zeros_like(dv_sc)
        cs_sc[...] = jnp.zeros_like(cs_sc)

        def block(i, masked):
            sl = pl.ds(pl.multiple_of(i * tk, tk), tk)
            for p in range(4):
                cs = slice(LANES * p, LANES * (p + 1))
                qi = q_ref[sl, cs]
                doi = do_ref[sl, cs]
                for h in (2 * p, 2 * p + 1):
                    st = _nt(km_sc[h], qi) - fk_sc[h] - lr_ref[h:h + 1, sl]
                    st_sc[h] = jnp.where(tri, st, NEG) if masked else st
                    dp_sc[h] = _nt(vm_sc[h], doi) - dr_ref[h:h + 1, sl]
            for h in range(N_HEADS):
                pt = jnp.exp2(st_sc[h])
                dst = pt * dp_sc[h]
                pt_sc[h] = pt.astype(BF16)
                ds_sc[h] = dst.astype(BF16)
                cs_sc[h] += dst[:, :LANES] + dst[:, LANES:]
                dfq_ref[h:h + 1, sl] += jnp.sum(dst, axis=0, keepdims=True)
            for p in range(4):
                cs = slice(LANES * p, LANES * (p + 1))
                qi = q_ref[sl, cs]
                doi = do_ref[sl, cs]
                for h in (2 * p, 2 * p + 1):
                    dv_sc[h] += _nn(pt_sc[h], doi)
                    dk_sc[h] += _nn(ds_sc[h], qi)
                    kt = kt_sc[p, HEAD_DIM * (h % 2):HEAD_DIM * (h % 2 + 1), :]
                    dqt_ref[HEAD_DIM * h:HEAD_DIM * (h + 1), sl] += _nn(kt, ds_sc[h])

        def step(i, carry):
            block(i, False)
            return carry

        block(j, True)
        lax.fori_loop(j + 1, nk, step, 0)
        df_ref[...] = jnp.zeros_like(df_ref)
        for p in range(4):
            cs = slice(LANES * p, LANES * (p + 1))
            dk_ref[:, cs] = (jnp.where(lo, dk_sc[2 * p], dk_sc[2 * p + 1]) * (1.0 / LOG2E)).astype(BF16)
            dv_ref[:, cs] = jnp.where(lo, dv_sc[2 * p], dv_sc[2 * p + 1]).astype(BF16)
            for h in (2 * p, 2 * p + 1):
                df_ref[:, h:h + 1] = -jnp.sum(cs_sc[h], axis=1, keepdims=True)

        @pl.when(j == nk - 1)
        def _():
            dqt_ref[...] = dqt_ref[...] * Q_SCALE

        @pl.when((pl.program_id(0) == nbat - 1) & (j == nk - 1))
        def _():
            for cp in _comm_copies(*comm):
                cp.wait()

    tile = lambda w, col: pl.BlockSpec((tk, w), lambda b, j: (b * nk + j, col))
    full = lambda col: pl.BlockSpec((seq, WIDTH), lambda b, j: (b, col))
    row = pl.BlockSpec((N_HEADS, seq), lambda b, j: (0, b))
    acc = pltpu.VMEM((N_HEADS, tk, LANES), F32)
    res = pl.pallas_call(
        body, name="fox_bwd", grid=(nbat, nk),
        in_specs=[tile(WIDTH, 1), tile(WIDTH, 2), full(0), full(0), tile(LANES, 0), row, row] + [ANY] * n,
        out_specs=[tile(WIDTH, 0), tile(WIDTH, 0), tile(LANES, 0), pl.BlockSpec((WIDTH, seq), lambda b, j: (b, 0)),
                   row] + [ANY] * n,
        out_shape=[jax.ShapeDtypeStruct((t_all, WIDTH), BF16), jax.ShapeDtypeStruct((t_all, WIDTH), BF16),
                   jax.ShapeDtypeStruct((t_all, LANES), F32), jax.ShapeDtypeStruct((nbat * WIDTH, seq), F32),
                   jax.ShapeDtypeStruct((N_HEADS, t_all), F32)] + _comm_out_shapes(grads, to_all),
        scratch_shapes=[pltpu.VMEM((N_HEADS, tk, LANES), BF16), pltpu.VMEM((N_HEADS, tk, LANES), BF16),
                        pltpu.VMEM((N_HEADS, tk, tk), F32), acc, acc, acc, pltpu.VMEM((4, LANES, tk), BF16),
                        pltpu.VMEM((N_HEADS, tk, tk), F32), pltpu.VMEM((N_HEADS, tk, tk), F32),
                        pltpu.VMEM((N_HEADS, tk, tk), BF16), pltpu.VMEM((N_HEADS, tk, tk), BF16)]
        + _comm_sems(n),
        compiler_params=_params(),
    )(za, za, za, do, f_col, lse_row, dl_row, *grads)
    return res[0], res[1], res[2], res[3], res[4], res[5:]


DIL_SUB = 4


def _dil_mask(has_prev):
    qi = lax.broadcasted_iota(jnp.int32, (BLK, 2 * BLK), 0)
    kj = lax.broadcasted_iota(jnp.int32, (BLK, 2 * BLK), 1)
    dist = qi + BLK - kj
    band = (dist >= 0) & (dist <= BLK)
    return band if has_prev is True else band & ((kj >= BLK) | has_prev)


def _dil_geometry(t_all, seq, d, max_sub=DIL_SUB):
    length = seq // d
    nbs = length // BLK
    sub = min(max_sub, nbs)
    spb = nbs // sub
    tile = lambda width, col: pl.BlockSpec((BLK * sub, width), lambda s: (s, col))
    whole = lambda width, col: pl.BlockSpec((length, width), lambda s: (s // spb, col))
    return nbs, sub, spb, t_all // (BLK * sub), tile, whole


def _blk(i):
    return pl.ds(pl.multiple_of(i * BLK, BLK), BLK)


def _dil_fwd(zb, seq, d):
    t_all = zb.shape[0]
    nbs, sub, spb, steps, tile, whole = _dil_geometry(t_all, seq, d)

    def body(q_ref, k_ref, v_ref, o_ref, lse_ref, s_sc, p_sc):
        first = (pl.program_id(0) % spb) * sub
        lo, hi = _half_masks(BLK)
        lse_ref[...] = jnp.zeros_like(lse_ref)
        for j in range(sub):
            blk = first + j
            mask = _dil_mask(blk != 0 if j == 0 else True)
            for p in range(4):
                cs = slice(LANES * p, LANES * (p + 1))
                qp = q_ref[BLK * j:BLK * (j + 1), cs]
                kcat = jnp.concatenate([k_ref[_blk(jnp.maximum(blk - 1, 0)), cs], k_ref[_blk(blk), cs]], axis=0)
                for e in (0, 1):
                    qe = jnp.where(lo if e == 0 else hi, qp, jnp.zeros_like(qp))
                    s_sc[N_HEADS * j + 2 * p + e] = jnp.where(mask, _nt(qe, kcat), NEG)
        inv = []
        for i in range(N_HEADS * sub):
            s = s_sc[i]
            m = jnp.max(s, axis=1, keepdims=True)
            pe = jnp.exp2(s - m)
            l = jnp.sum(pe, axis=1, keepdims=True)
            p_sc[i] = pe.astype(BF16)
            inv.append(1.0 / l)
            j, h = divmod(i, N_HEADS)
            lse_ref[BLK * j:BLK * (j + 1), h:h + 1] = m + jnp.log(l) * LOG2E
        for j in range(sub):
            blk = first + j
            for p in range(4):
                cs = slice(LANES * p, LANES * (p + 1))
                vcat = jnp.concatenate([v_ref[_blk(jnp.maximum(blk - 1, 0)), cs], v_ref[_blk(blk), cs]], axis=0)
                res = [_nn(p_sc[N_HEADS * j + h], vcat) * inv[N_HEADS * j + h] for h in (2 * p, 2 * p + 1)]
                o_ref[BLK * j:BLK * (j + 1), cs] = jnp.where(lo, res[0], res[1]).astype(BF16)

    return pl.pallas_call(
        body, name=f"dil_fwd_{d}", grid=(steps,), in_specs=[tile(WIDTH, 0), whole(WIDTH, 1), whole(WIDTH, 2)],
        out_specs=[tile(WIDTH, 0), tile(LANES, 0)],
        out_shape=[jax.ShapeDtypeStruct((t_all, WIDTH), BF16), jax.ShapeDtypeStruct((t_all, LANES), F32)],
        scratch_shapes=[pltpu.VMEM((N_HEADS * sub, BLK, 2 * BLK), F32),
                        pltpu.VMEM((N_HEADS * sub, BLK, 2 * BLK), BF16)],
        compiler_params=_params(),
    )(zb, zb, zb)


def _dil_bwd(zb, do, lse, dl, seq, d):
    t_all = zb.shape[0]
    length = seq // d
    nbs, sub, spb, steps, tile, whole = _dil_geometry(t_all, seq, d, 2 if length >= 4096 else DIL_SUB)

    def body(k_ref, v_ref, q_ref, do_ref, lse_ref, dl_ref, dq_ref, dk_ref, dv_ref, s_sc, dp_sc, pt_sc, ds_sc, kt_sc,
             dqt_sc):
        step = pl.program_id(0) % spb
        first = step * sub

        @pl.when(step == 0)
        def _():
            dqt_sc[...] = jnp.zeros_like(dqt_sc)
        r = lax.broadcasted_iota(jnp.int32, (BLK, 2 * BLK), 0)
        c = lax.broadcasted_iota(jnp.int32, (BLK, 2 * BLK), 1)
        same = (c < BLK) & (c >= r)
        later = (c >= BLK) & (c - BLK <= r)
        lo, hi = _half_masks(BLK)
        for j in range(sub):
            blk = first + j
            rows = slice(BLK * j, BLK * (j + 1))
            nxt = _blk(jnp.minimum(blk + 1, nbs - 1))
            mask = same | (later & (blk + 1 != nbs)) if j == sub - 1 else same | later
            lrows = jnp.concatenate([lse_ref[_blk(blk), :].T, lse_ref[nxt, :].T], axis=1)
            erows = jnp.concatenate([dl_ref[_blk(blk), :].T, dl_ref[nxt, :].T], axis=1)
            for p in range(4):
                cs = slice(LANES * p, LANES * (p + 1))
                kp = k_ref[rows, cs]
                vp = v_ref[rows, cs]
                kt_sc[4 * j + p] = kp.astype(F32).T.astype(BF16)
                qcat = jnp.concatenate([q_ref[_blk(blk), cs], q_ref[nxt, cs]], axis=0)
                dcat = jnp.concatenate([do_ref[_blk(blk), cs], do_ref[nxt, cs]], axis=0)
                for e in (0, 1):
                    h = 2 * p + e
                    sel = lo if e == 0 else hi
                    ke = jnp.where(sel, kp, jnp.zeros_like(kp))
                    ve = jnp.where(sel, vp, jnp.zeros_like(vp))
                    s_sc[N_HEADS * j + h] = jnp.where(mask, _nt(ke, qcat) - lrows[h:h + 1, :], NEG)
                    dp_sc[N_HEADS * j + h] = _nt(ve, dcat) - erows[h:h + 1, :]
        for i in range(N_HEADS * sub):
            pt = jnp.exp2(s_sc[i])
            pt_sc[i] = pt.astype(BF16)
            ds_sc[i] = (pt * dp_sc[i]).astype(BF16)
        for j in range(sub):
            blk = first + j
            rows = slice(BLK * j, BLK * (j + 1))
            nxt = _blk(jnp.minimum(blk + 1, nbs - 1))
            cols = pl.ds(pl.multiple_of(blk * BLK, BLK), 2 * BLK)
            for p in range(4):
                cs = slice(LANES * p, LANES * (p + 1))
                qcat = jnp.concatenate([q_ref[_blk(blk), cs], q_ref[nxt, cs]], axis=0)
                dcat = jnp.concatenate([do_ref[_blk(blk), cs], do_ref[nxt, cs]], axis=0)
                i = N_HEADS * j + 2 * p
                dk_ref[rows, cs] = (jnp.where(lo, _nn(ds_sc[i], qcat), _nn(ds_sc[i + 1], qcat))
                                    * (1.0 / LOG2E)).astype(BF16)
                dv_ref[rows, cs] = jnp.where(lo, _nn(pt_sc[i], dcat), _nn(pt_sc[i + 1], dcat)).astype(BF16)
                for e in (0, 1):
                    kt = kt_sc[4 * j + p, HEAD_DIM * e:HEAD_DIM * (e + 1), :]
                    dqt_sc[HEAD_DIM * (2 * p + e):HEAD_DIM * (2 * p + e + 1), cols] += _nn(kt, ds_sc[i + e])

        @pl.when(step == spb - 1)
        def _():
            for p in range(4):
                cs = slice(LANES * p, LANES * (p + 1))
                dq_ref[:, cs] = (dqt_sc[cs, 0:length].T * Q_SCALE).astype(BF16)

    wide = pltpu.VMEM((N_HEADS * sub, BLK, 2 * BLK), F32)
    half = pltpu.VMEM((N_HEADS * sub, BLK, 2 * BLK), BF16)
    return pl.pallas_call(
        body, name=f"dil_bwd_{d}", grid=(steps,),
        in_specs=[tile(WIDTH, 1), tile(WIDTH, 2), whole(WIDTH, 0), whole(WIDTH, 0), whole(LANES, 0), whole(LANES, 0)],
        out_specs=[whole(WIDTH, 0), tile(WIDTH, 0), tile(WIDTH, 0)],
        out_shape=[jax.ShapeDtypeStruct((t_all, WIDTH), BF16)] * 3,
        scratch_shapes=[wide, wide, half, half, pltpu.VMEM((4 * sub, LANES, BLK), BF16),
                        pltpu.VMEM((WIDTH, length + BLK), F32)],
        compiler_params=_params(),
    )(zb, zb, zb, do, lse, dl)


def _mix_out(oa, o3, l3, gn_a, gn_b, w_out, x, ada3, ln_g, ln_b, perms, seq):
    t_all = x.shape[0]
    tm = TOK_TM
    nts = seq // tm

    def body(oa_ref, o1_ref, o2_ref, o3_ref, l1_ref, l2_ref, l3_ref, ga_ref, gb_ref, w_ref, x_ref, ada_ref, g_ref,
             b_ref, p4_ref, p16_ref, pt4_ref, pt16_ref, ob_ref, lse_ref, lse4_ref, lse16_ref, mg_ref, mix_ref, xh_ref,
             rs_ref, h2_ref, h2t_ref):
        e, et = _head_mats()
        la = l1_ref[...]
        lb = _permute_f32(pt4_ref[...], _load_classes(l2_ref, 4))
        lc = _permute_f32(pt16_ref[...], _load_classes(l3_ref, 16))
        mx = jnp.maximum(jnp.maximum(la, lb), lc)
        ea, eb, ec = jnp.exp2(la - mx), jnp.exp2(lb - mx), jnp.exp2(lc - mx)
        tot = ea + eb + ec
        lse = mx + jnp.log(tot) * LOG2E
        lse_ref[...] = lse
        _store_classes(lse4_ref, _permute_f32(p4_ref[...], lse), 4)
        _store_classes(lse16_ref, _permute_f32(p16_ref[...], lse), 16)
        ob = (o1_ref[...].astype(F32) * _hexp(ea / tot, e)
              + _nn(pt4_ref[...], _load_classes(o2_ref, 4)) * _hexp(eb / tot, e)
              + _nn(pt16_ref[...], _load_classes(o3_ref, 16)) * _hexp(ec / tot, e))
        ob_ref[...] = ob

        def rms(o, gain):
            rr = lax.rsqrt(_hsum(o * o, et) * (1.0 / HEAD_DIM) + RMS_EPS)
            return o * _hexp(rr, e) * gain

        merged = jnp.concatenate([rms(oa_ref[...], ga_ref[...]), rms(ob, gb_ref[...])], axis=1).astype(BF16)
        mg_ref[...] = merged
        mix = _nn(merged, w_ref[...])
        mix_ref[...] = mix.astype(BF16)
        r1 = ALPHA * x_ref[...] + ada_ref[0, 2:3, :] * mix
        d = r1 - jnp.mean(r1, axis=1, keepdims=True)
        rstd = lax.rsqrt(jnp.mean(d * d, axis=1, keepdims=True) + LN_EPS)
        xh = d * rstd
        xh_ref[...] = xh
        rs_ref[...] = jnp.broadcast_to(rstd, (tm, LANES))
        x1 = xh * g_ref[...] + b_ref[...]
        h2 = x1 * (1.0 + ada_ref[0, 4:5, :]) + ada_ref[0, 3:4, :]
        h2_ref[...] = h2.astype(BF16)
        h2t_ref[0] = h2.T.astype(BF16)

    tok = lambda w: pl.BlockSpec((tm, w), lambda i: (i, 0))
    vec = lambda w: pl.BlockSpec((1, w), lambda i: (0, 0))
    whole = lambda a: pl.BlockSpec(a.shape, lambda i: (0, 0))
    classes = lambda a, d: a.reshape(t_all // seq * d, seq // d, a.shape[-1])
    return pl.pallas_call(
        body, name="mix_out", grid=(t_all // tm,),
        in_specs=[tok(WIDTH), tok(WIDTH), _class_spec(4, WIDTH, nts), _class_spec(16, WIDTH, nts), tok(LANES),
                  _class_spec(4, LANES, nts), _class_spec(16, LANES, nts), vec(WIDTH), vec(WIDTH), whole(w_out),
                  tok(D_MODEL), pl.BlockSpec((1, 6, D_MODEL), lambda i: (i // nts, 0, 0)), vec(D_MODEL), vec(D_MODEL)]
        + [whole(p) for p in perms],
        out_specs=[tok(WIDTH), tok(LANES), _class_spec(4, LANES, nts), _class_spec(16, LANES, nts), tok(D_MODEL),
                   tok(D_MODEL), tok(D_MODEL), tok(LANES), tok(D_MODEL), pl.BlockSpec((1, D_MODEL, tm), lambda i: (i // (FFN_TM // tm), 0, i % (FFN_TM // tm)))],
        out_shape=[jax.ShapeDtypeStruct((t_all, WIDTH), F32), jax.ShapeDtypeStruct((t_all, LANES), F32),
                   _class_shape(t_all, seq, 4, LANES, F32), _class_shape(t_all, seq, 16, LANES, F32),
                   jax.ShapeDtypeStruct((t_all, D_MODEL), BF16), jax.ShapeDtypeStruct((t_all, D_MODEL), BF16),
                   jax.ShapeDtypeStruct((t_all, D_MODEL), F32), jax.ShapeDtypeStruct((t_all, LANES), F32),
                   jax.ShapeDtypeStruct((t_all, D_MODEL), BF16),
                   jax.ShapeDtypeStruct((t_all // FFN_TM, D_MODEL, FFN_TM), BF16)],
        compiler_params=_params(),
    )(oa, o3[0], classes(o3[1], 4), classes(o3[2], 16), l3[0], classes(l3[1], 4), classes(l3[2], 16), gn_a, gn_b,
      w_out, x, ada3, ln_g, ln_b, *perms)


def _mix_out_bwd(dmix, w_out, oa, ob, gn_a, gn_b, perms, seq):
    t_all = dmix.shape[0]
    tm = TOK_TM
    nts = seq // tm

    def body(dm_ref, w_ref, oa_ref, ob_ref, ga_ref, gb_ref, p4_ref, p16_ref, doa_ref, dob_ref, dob4_ref, dob16_ref,
             dla_ref, dlb_ref, dlb4_ref, dlb16_ref, acc_ref):
        @pl.when(pl.program_id(0) == 0)
        def _():
            acc_ref[...] = jnp.zeros_like(acc_ref)
        e, et = _head_mats()
        dmg = _nt(dm_ref[...], w_ref[...])

        def group(o, dn, gain):
            rr = lax.rsqrt(_hsum(o * o, et) * (1.0 / HEAD_DIM) + RMS_EPS)
            re = _hexp(rr, e)
            dgain = jnp.sum(dn * o * re, axis=0, keepdims=True)
            dxn = dn * gain
            tt = _hsum(dxn * o, et) * (rr * rr * rr) * (1.0 / HEAD_DIM)
            do = re * dxn - o * _hexp(tt, e)
            return do, _hsum(do * o, et), dgain

        doa, dla, dga = group(oa_ref[...], dmg[:, :WIDTH], ga_ref[...])
        dob, dlb, dgb = group(ob_ref[...], dmg[:, WIDTH:], gb_ref[...])
        dob = dob.astype(BF16)
        doa_ref[...] = doa.astype(BF16)
        dob_ref[...] = dob
        _store_classes(dob4_ref, _nn(p4_ref[...], dob).astype(BF16), 4)
        _store_classes(dob16_ref, _nn(p16_ref[...], dob).astype(BF16), 16)
        dla_ref[...] = dla
        dlb_ref[...] = dlb
        _store_classes(dlb4_ref, _permute_f32(p4_ref[...], dlb), 4)
        _store_classes(dlb16_ref, _permute_f32(p16_ref[...], dlb), 16)
        acc_ref[0:1, :] += jnp.concatenate([dga, dgb], axis=1)

    tok = lambda w: pl.BlockSpec((tm, w), lambda i: (i, 0))
    vec = lambda w: pl.BlockSpec((1, w), lambda i: (0, 0))
    return pl.pallas_call(
        body, name="mix_out_bwd", grid=(t_all // tm,),
        in_specs=[tok(D_MODEL), pl.BlockSpec(w_out.shape, lambda i: (0, 0)), tok(WIDTH), tok(WIDTH), vec(WIDTH),
                  vec(WIDTH), pl.BlockSpec(perms[0].shape, lambda i: (0, 0)),
                  pl.BlockSpec(perms[1].shape, lambda i: (0, 0))],
        out_specs=[tok(WIDTH), tok(WIDTH), _class_spec(4, WIDTH, nts), _class_spec(16, WIDTH, nts), tok(LANES),
                   tok(LANES), _class_spec(4, LANES, nts), _class_spec(16, LANES, nts),
                   pl.BlockSpec((8, D_MODEL), lambda i: (0, 0))],
        out_shape=[jax.ShapeDtypeStruct((t_all, WIDTH), BF16), jax.ShapeDtypeStruct((t_all, WIDTH), BF16),
                   _class_shape(t_all, seq, 4, WIDTH, BF16), _class_shape(t_all, seq, 16, WIDTH, BF16),
                   jax.ShapeDtypeStruct((t_all, LANES), F32), jax.ShapeDtypeStruct((t_all, LANES), F32),
                   _class_shape(t_all, seq, 4, LANES, F32), _class_shape(t_all, seq, 16, LANES, F32),
                   jax.ShapeDtypeStruct((8, D_MODEL), F32)],
        compiler_params=_params(),
    )(dmix, w_out, oa, ob, gn_a, gn_b, perms[0], perms[1])


def _inproj_bwd(dqt, dka, dva, dil1, dil4, dil16, dfa16, pos, wqkv, wf16, freq, perms, dr1, x, ada3, seq):
    t_all = x.shape[0]
    tm = TOK_TM
    nts = seq // tm

    def body(dqt_ref, dka_ref, dva_ref, q1_ref, k1_ref, v1_ref, q4_ref, k4_ref, v4_ref, q16_ref, k16_ref, v16_ref,
             dfa_ref, pos_ref, w_ref, wf_ref, fr_ref, pt4_ref, pt16_ref, dr1_ref, x_ref, ada_ref, gx_ref, dz_ref,
             acc_ref):
        i = pl.program_id(0)

        @pl.when(i == 0)
        def _():
            acc_ref[...] = jnp.zeros_like(acc_ref)
        tabs = _rope_tabs(pos_ref, fr_ref, -1.0)
        dz_ref[:, :WIDTH] = dqt_ref[...].T.astype(BF16)
        dz_ref[:, WIDTH:2 * WIDTH] = dka_ref[...]
        dz_ref[:, 2 * WIDTH:3 * WIDTH] = dva_ref[...]
        for t, (n1, n4, n16) in enumerate(((q1_ref, q4_ref, q16_ref), (k1_ref, k4_ref, k16_ref),
                                           (v1_ref, v4_ref, v16_ref))):
            tot = (n1[...].astype(F32) + _nn(pt4_ref[...], _load_classes(n4, 4))
                   + _nn(pt16_ref[...], _load_classes(n16, 16)))
            if t < 2:
                tot = _rope(tot, tabs)
            dz_ref[:, (3 + t) * WIDTH:(4 + t) * WIDTH] = tot.astype(BF16)
        dh1 = _tn(dfa_ref[...], wf_ref[...])
        for n in range(6):
            cs = slice(n * WIDTH, (n + 1) * WIDTH)
            dh1 = dh1 + _nt(dz_ref[:, cs], w_ref[:, cs])
        xv = x_ref[...]
        gx_ref[...] = ALPHA * dr1_ref[...] + dh1 * (1.0 + ada_ref[0, 1:2, :])
        b = i // nts
        acc_ref[pl.ds(b, 1), :] += jnp.sum(dh1 * xv, axis=0, keepdims=True)
        acc_ref[pl.ds(8 + b, 1), :] += jnp.sum(dh1, axis=0, keepdims=True)

    tok = lambda w: pl.BlockSpec((tm, w), lambda i: (i, 0))
    whole = lambda a: pl.BlockSpec(a.shape, lambda i: (0, 0))
    classes = lambda a, d: a.reshape(t_all // seq * d, seq // d, a.shape[-1])
    return pl.pallas_call(
        body, name="inproj_bwd", grid=(t_all // tm,),
        in_specs=[pl.BlockSpec((WIDTH, tm), lambda i: (i // nts, i % nts)), tok(WIDTH), tok(WIDTH)]
        + [tok(WIDTH)] * 3 + [_class_spec(4, WIDTH, nts)] * 3 + [_class_spec(16, WIDTH, nts)] * 3
        + [pl.BlockSpec((16, tm), lambda i: (0, i)), tok(1), whole(wqkv), whole(wf16),
           pl.BlockSpec((1, LANES), lambda i: (0, 0)), whole(perms[2]), whole(perms[3]), tok(D_MODEL), tok(D_MODEL),
           pl.BlockSpec((1, 6, D_MODEL), lambda i: (i // nts, 0, 0))],
        out_specs=[tok(D_MODEL), tok(6 * WIDTH), pl.BlockSpec((16, D_MODEL), lambda i: (0, 0))],
        out_shape=[jax.ShapeDtypeStruct((t_all, D_MODEL), F32), jax.ShapeDtypeStruct((t_all, 6 * WIDTH), BF16),
                   jax.ShapeDtypeStruct((16, D_MODEL), F32)],
        compiler_params=_params(),
    )(dqt, dka, dva, *dil1, *[classes(a, 4) for a in dil4], *[classes(a, 16) for a in dil16], dfa16, pos, wqkv, wf16,
      freq, perms[2], perms[3], dr1, x, ada3)


FFN_TM = 1024
FFN_TN = 256
HALO = 8


FFN_CHUNK = 256


def _conv_params(cw_ref, cb_ref, n, tn):
    a = pl.ds(pl.multiple_of(n * tn, tn), tn)
    g = pl.ds(pl.multiple_of(D_FF + n * tn, tn), tn)
    return cw_ref[:, a], cw_ref[:, g], cb_ref[:, a], cb_ref[:, g]


def _conv(cat_ref, w_ref, b_ref, start, rows, halo=HALO):
    return (b_ref[...] + w_ref[0:1, :] * cat_ref[pl.ds(start + halo - 2, rows), :]
            + w_ref[1:2, :] * cat_ref[pl.ds(start + halo - 1, rows), :]
            + w_ref[2:3, :] * cat_ref[pl.ds(start + halo, rows), :])


def _ffn_up_gate(h2, w_up, conv_w, conv_b, seq):
    t_all = h2.shape[0]
    tm, tn = min(2 * FFN_TM, seq), FFN_TN
    nc = D_FF // tn
    nts = seq // tm
    pre = 16

    def body(h_ref, hp_ref, wua_ref, wug_ref, cw_ref, cb_ref, ua_ref, ug_ref, o_ref, ca_ref, cg_ref):
        first = (pl.program_id(1) % nts) == 0
        wa_ref, wg_ref, ba_ref, bg_ref = _conv_params(cw_ref, cb_ref, pl.program_id(0), tn)
        hcat = jnp.concatenate([hp_ref[...], h_ref[...]], axis=0)
        zero = jnp.zeros((pre, tn), F32)
        for w_ref, cat, u_ref in ((wua_ref, ca_ref, ua_ref), (wug_ref, cg_ref, ug_ref)):
            ub = _nn(hcat, w_ref[...]).astype(BF16)
            ue = ub.astype(F32)
            cat[0:pre, :] = jnp.where(first, zero, ue[0:pre])
            cat[pre:, :] = ue[pre:]
            u_ref[...] = ub[pre:]
        for c0 in range(0, tm, FFN_CHUNK):
            ya = _conv(ca_ref, wa_ref, ba_ref, c0, FFN_CHUNK, pre)
            yg = _conv(cg_ref, wg_ref, bg_ref, c0, FFN_CHUNK, pre)
            o_ref[c0:c0 + FFN_CHUNK, :] = (yg * jax.nn.sigmoid(yg) * ya).astype(BF16)

    wcol = lambda off: pl.BlockSpec((D_MODEL, tn), lambda n, t: (0, n + off))
    tile = pl.BlockSpec((tm, tn), lambda n, t: (t, n))
    return pl.pallas_call(
        body, name="ffn_up_gate", grid=(nc, t_all // tm),
        in_specs=[pl.BlockSpec((tm, D_MODEL), lambda n, t: (t, 0)),
                  pl.BlockSpec((pre, D_MODEL), lambda n, t: (jnp.maximum(t * (tm // pre) - 1, 0), 0)),
                  wcol(0), wcol(nc), pl.BlockSpec(conv_w.shape, lambda n, t: (0, 0)),
                  pl.BlockSpec(conv_b.shape, lambda n, t: (0, 0))],
        out_specs=[tile, tile, tile],
        out_shape=[jax.ShapeDtypeStruct((t_all, D_FF), BF16)] * 3,
        scratch_shapes=[pltpu.VMEM((tm + pre, tn), F32)] * 2, compiler_params=_params(),
    )(h2, h2, w_up, w_up, conv_w, conv_b)


def _ffn_gate_bwd(u_a, u_g, dfi, conv_w, conv_b, h2t, seq):
    t_all = u_a.shape[0]
    tm, tn = FFN_TM, FFN_TN
    nc = D_FF // tn
    nts = seq // tm

    def body(ua_ref, uap_ref, uan_ref, ug_ref, ugp_ref, ugn_ref, df_ref, dfn_ref, cw_ref, cb_ref, h_ref,
             dua_ref, dug_ref, acca_ref, accg_ref, dwa_ref, dwg_ref, ca_ref, cg_ref, ya_ref, yg_ref, dwa_sc, dwg_sc,
             out_sems):
        t = pl.program_id(0)
        n = pl.program_id(1)
        cols = pl.ds(pl.multiple_of(n * tn, tn), tn)
        first = (t % nts) == 0
        last = (t % nts) == nts - 1

        @pl.when((t == 0) & (n == 0))
        def _():
            acca_ref[...] = jnp.zeros_like(acca_ref)
            accg_ref[...] = jnp.zeros_like(accg_ref)
            dwa_sc[...] = jnp.zeros_like(dwa_sc)
            dwg_sc[...] = jnp.zeros_like(dwg_sc)
        wa_ref, wg_ref, ba_ref, bg_ref = _conv_params(cw_ref, cb_ref, n, tn)
        zero = jnp.zeros((HALO, tn), F32)
        for cat, cur, prv, nxt in ((ca_ref, ua_ref, uap_ref, uan_ref), (cg_ref, ug_ref, ugp_ref, ugn_ref)):
            cat[0:HALO, :] = jnp.where(first, zero, prv[...].astype(F32)[HALO:])
            cat[HALO:HALO + tm, :] = cur[...].astype(F32)
            cat[HALO + tm:, :] = nxt[...].astype(F32)[:HALO]
        ch = FFN_CHUNK
        sums = [[jnp.zeros((1, tn), F32) for _ in range(4)] for _ in range(2)]
        for ci, c0 in enumerate(range(0, tm, ch)):
            ya = _conv(ca_ref, wa_ref, ba_ref, c0, ch + HALO)
            yg = _conv(cg_ref, wg_ref, bg_ref, c0, ch + HALO)
            if c0 + ch < tm:
                beyond = df_ref[c0 + ch:c0 + ch + 16, :].astype(F32)[:HALO]
            else:
                beyond = jnp.where(last, 0.0, dfn_ref[...].astype(F32)[:HALO])
            dfe = jnp.concatenate([df_ref[c0:c0 + ch, :].astype(F32), beyond], axis=0)
            sg = jax.nn.sigmoid(yg)
            ya_ref[ci] = dfe * (yg * sg)
            yg_ref[ci] = dfe * ya * (sg * (1.0 + yg * (1.0 - sg)))
            for half, (dy, cat, w_ref, du_ref) in enumerate(((ya_ref, ca_ref, wa_ref, dua_ref),
                                                             (yg_ref, cg_ref, wg_ref, dug_ref))):
                d0 = dy[ci, 0:ch, :]
                du = (w_ref[2:3, :] * d0 + w_ref[1:2, :] * dy[ci, pl.ds(1, ch), :]
                      + w_ref[0:1, :] * dy[ci, pl.ds(2, ch), :])
                du_ref[c0:c0 + ch, :] = du.astype(BF16)
                for k in range(3):
                    sums[half][k] += jnp.sum(d0 * cat[pl.ds(c0 + HALO - 2 + k, ch), :], axis=0, keepdims=True)
                sums[half][3] += jnp.sum(d0, axis=0, keepdims=True)
        for half, acc in enumerate((acca_ref, accg_ref)):
            for k in range(4):
                acc[k:k + 1, cols] += sums[half][k]
        ht = h_ref[0]
        dwa_sc[:, cols] += _nn(ht, dua_ref[...])
        dwg_sc[:, cols] += _nn(ht, dug_ref[...])

        @pl.when((t == t_all // tm - 1) & (n == nc - 1))
        def _():
            copies = [pltpu.make_async_copy(dwa_sc, dwa_ref, out_sems.at[0]),
                      pltpu.make_async_copy(dwg_sc, dwg_ref, out_sems.at[1])]
            for cp in copies:
                cp.start()
            for cp in copies:
                cp.wait()

    nrow = t_all // 16
    cur = pl.BlockSpec((tm, tn), lambda t, n: (t, n))
    prev = pl.BlockSpec((16, tn), lambda t, n: (jnp.maximum(t * (tm // 16) - 1, 0), n))
    nxt = pl.BlockSpec((16, tn), lambda t, n: (jnp.minimum((t + 1) * (tm // 16), nrow - 1), n))
    acc = pl.BlockSpec((8, D_FF), lambda t, n: (0, 0))
    return pl.pallas_call(
        body, name="ffn_gate_bwd", grid=(t_all // tm, nc),
        in_specs=[cur, prev, nxt, cur, prev, nxt, cur, nxt, pl.BlockSpec(conv_w.shape, lambda t, n: (0, 0)),
                  pl.BlockSpec(conv_b.shape, lambda t, n: (0, 0)),
                  pl.BlockSpec((1, D_MODEL, tm), lambda t, n: (t, 0, 0))],
        out_specs=[cur, cur, acc, acc, ANY, ANY],
        out_shape=[jax.ShapeDtypeStruct((t_all, D_FF), BF16), jax.ShapeDtypeStruct((t_all, D_FF), BF16),
                   jax.ShapeDtypeStruct((8, D_FF), F32), jax.ShapeDtypeStruct((8, D_FF), F32),
                   jax.ShapeDtypeStruct((D_MODEL, D_FF), F32), jax.ShapeDtypeStruct((D_MODEL, D_FF), F32)],
        scratch_shapes=[pltpu.VMEM((tm + 2 * HALO, tn), F32)] * 2
        + [pltpu.VMEM((tm // FFN_CHUNK, FFN_CHUNK + HALO, tn), F32)] * 2
        + [pltpu.VMEM((D_MODEL, D_FF), F32)] * 2 + [pltpu.SemaphoreType.DMA((2,))],
        compiler_params=_params(),
    )(u_a, u_a, u_a, u_g, u_g, u_g, dfi, dfi, conv_w, conv_b, h2t)


def _ffn_down(ffn_in, w_down, xh1, ln1_g, ln1_b, ada3, ln2_g, ln2_b, target, seq):
    t_all = xh1.shape[0]
    tm = 512
    nts = seq // tm

    def body(f_ref, w_ref, xh_ref, g1_ref, b1_ref, ada_ref, g2_ref, b2_ref, tg_ref, dr2_ref, acc_ref):
        i = pl.program_id(0)

        @pl.when(i == 0)
        def _():
            acc_ref[...] = jnp.zeros_like(acc_ref)
        ffn = _nn(f_ref[...], w_ref[...])
        x1 = xh_ref[...] * g1_ref[...] + b1_ref[...]
        r2 = ALPHA * x1 + ada_ref[0, 5:6, :] * ffn
        d = r2 - jnp.mean(r2, axis=1, keepdims=True)
        rstd = lax.rsqrt(jnp.mean(d * d, axis=1, keepdims=True) + LN_EPS)
        xh2 = d * rstd
        diff = xh2 * g2_ref[...] + b2_ref[...] - tg_ref[...]
        dy = diff * (1.0 / D_MODEL)
        dr2 = _layer_norm_bwd(dy * g2_ref[...], xh2, rstd)
        dr2_ref[...] = dr2
        acc_ref[0:1, :] += jnp.sum(dy * xh2, axis=0, keepdims=True)
        acc_ref[1:2, :] += jnp.sum(dy, axis=0, keepdims=True)
        acc_ref[2:3, :] += jnp.sum(diff * diff, axis=0, keepdims=True) * (0.5 / D_MODEL)
        acc_ref[pl.ds(8 + i // nts, 1), :] += jnp.sum(dr2 * ffn, axis=0, keepdims=True)

    tok = lambda w: pl.BlockSpec((tm, w), lambda i: (i, 0))
    vec = pl.BlockSpec((1, D_MODEL), lambda i: (0, 0))
    return pl.pallas_call(
        body, name="ffn_down", grid=(t_all // tm,),
        in_specs=[tok(D_FF), pl.BlockSpec(w_down.shape, lambda i: (0, 0)), tok(D_MODEL), vec, vec,
                  pl.BlockSpec((1, 6, D_MODEL), lambda i: (i // nts, 0, 0)), vec, vec, tok(D_MODEL)],
        out_specs=[tok(D_MODEL), pl.BlockSpec((16, D_MODEL), lambda i: (0, 0))],
        out_shape=[jax.ShapeDtypeStruct((t_all, D_MODEL), F32), jax.ShapeDtypeStruct((16, D_MODEL), F32)],
        compiler_params=_params(),
    )(ffn_in, w_down, xh1, ln1_g, ln1_b, ada3, ln2_g, ln2_b, target)


def _ffn_down_bwd(dr2, ada3, w_down, seq):
    t_all = dr2.shape[0]
    tm = 512
    nts = seq // tm

    def body(d_ref, ada_ref, w_ref, dffn_ref, dfi_ref):
        dffn = (d_ref[...] * ada_ref[0, 5:6, :]).astype(BF16)
        dffn_ref[...] = dffn
        dfi_ref[...] = _nt(dffn, w_ref[...]).astype(BF16)

    tok = lambda w: pl.BlockSpec((tm, w), lambda i: (i, 0))
    return pl.pallas_call(
        body, name="ffn_down_bwd", grid=(t_all // tm,),
        in_specs=[tok(D_MODEL), pl.BlockSpec((1, 6, D_MODEL), lambda i: (i // nts, 0, 0)),
                  pl.BlockSpec(w_down.shape, lambda i: (0, 0))],
        out_specs=[tok(D_MODEL), tok(D_FF)],
        out_shape=[jax.ShapeDtypeStruct((t_all, D_MODEL), BF16), jax.ShapeDtypeStruct((t_all, D_FF), BF16)],
        compiler_params=_params(),
    )(dr2, ada3, w_down)


def _ffn_up_bwd(du_a, du_g, w_up, dr2, xh1, rs1, mix, ada3, ln1_g, ln1_b, seq):
    t_all = dr2.shape[0]
    tm = 512
    nts = seq // tm

    def body(da_ref, dg_ref, w_ref, dr2_ref, xh_ref, rs_ref, mix_ref, ada_ref, g_ref, b_ref, dr1_ref, dmix_ref,
             acc_ref):
        i = pl.program_id(0)

        @pl.when(i == 0)
        def _():
            acc_ref[...] = jnp.zeros_like(acc_ref)
        dh2 = _nt(da_ref[...], w_ref[:, :D_FF]) + _nt(dg_ref[...], w_ref[:, D_FF:])
        xh = xh_ref[...]
        x1 = xh * g_ref[...] + b_ref[...]
        dx1 = ALPHA * dr2_ref[...] + dh2 * (1.0 + ada_ref[0, 4:5, :])
        dr1 = _layer_norm_bwd(dx1 * g_ref[...], xh, rs_ref[:, 0:1])
        dr1_ref[...] = dr1
        dmix_ref[...] = (dr1 * ada_ref[0, 2:3, :]).astype(BF16)
        b = i // nts
        acc_ref[0:1, :] += jnp.sum(dx1 * xh, axis=0, keepdims=True)
        acc_ref[1:2, :] += jnp.sum(dx1, axis=0, keepdims=True)
        acc_ref[pl.ds(8 + b, 1), :] += jnp.sum(dh2 * x1, axis=0, keepdims=True)
        acc_ref[pl.ds(16 + b, 1), :] += jnp.sum(dh2, axis=0, keepdims=True)
        acc_ref[pl.ds(24 + b, 1), :] += jnp.sum(dr1 * mix_ref[...].astype(F32), axis=0, keepdims=True)

    tok = lambda w: pl.BlockSpec((tm, w), lambda i: (i, 0))
    vec = pl.BlockSpec((1, D_MODEL), lambda i: (0, 0))
    return pl.pallas_call(
        body, name="ffn_up_bwd", grid=(t_all // tm,),
        in_specs=[tok(D_FF), tok(D_FF), pl.BlockSpec(w_up.shape, lambda i: (0, 0)), tok(D_MODEL), tok(D_MODEL),
                  tok(LANES), tok(D_MODEL), pl.BlockSpec((1, 6, D_MODEL), lambda i: (i // nts, 0, 0)), vec, vec],
        out_specs=[tok(D_MODEL), tok(D_MODEL), pl.BlockSpec((32, D_MODEL), lambda i: (0, 0))],
        out_shape=[jax.ShapeDtypeStruct((t_all, D_MODEL), F32), jax.ShapeDtypeStruct((t_all, D_MODEL), BF16),
                   jax.ShapeDtypeStruct((32, D_MODEL), F32)],
        compiler_params=_params(),
    )(du_a, du_g, w_up, dr2, xh1, rs1, mix, ada3, ln1_g, ln1_b)


def _rows(a):
    return a[:, :N_HEADS].T


def _rope_freq():
    f = np.float32(ROPE_THETA) ** (-np.arange(0, ROPE_DIMS, 2, dtype=np.float32) / np.float32(ROPE_DIMS))
    return jnp.asarray(np.tile(f.astype(np.float32), LANES // (ROPE_DIMS // 2))[None, :])


def _local_step(x, positions, target, ada3, w_in, b_fgate, gn_a, gn_b, ln1_g, ln1_b, conv_b, ln2_g, ln2_b,
                late_shards):
    nbat, seq, _ = x.shape
    t_all = nbat * seq
    xf = x.reshape(t_all, D_MODEL)
    tg = target.reshape(t_all, D_MODEL)
    pos = positions.reshape(t_all, 1)
    freq = _rope_freq()

    wqkv = jnp.concatenate([w_in[:, :3 * WIDTH], w_in[:, 3 * WIDTH + N_HEADS:]], axis=1)
    wf16 = jnp.zeros((16, D_MODEL), BF16).at[:N_HEADS].set(w_in[:, 3 * WIDTH:3 * WIDTH + N_HEADS].T)
    bf = b_fgate.reshape(N_HEADS, 1)

    perms = [_perm_matrix(TOK_TM, d, tr) for tr in (False, True) for d in DILATIONS[1:]]
    h1, za, zb1, zb4, zb16, vt, fa_t = _inproj(xf, ada3, pos, wqkv, wf16, freq, perms, seq)
    zbs = [zb1, zb4.reshape(t_all, 3 * WIDTH), zb16.reshape(t_all, 3 * WIDTH)]
    f_row = _fgate_fwd(fa_t, bf, seq)
    f_col = jnp.zeros((t_all, LANES), F32).at[:, :N_HEADS].set(f_row.T * LOG2E)
    oa, lse_row_a, gathered = _fox_fwd(za, vt, f_col, seq, [late_shards[n] for n in LATE])
    w_out, w_up, conv_w, w_down = (_full_from_gathered(n, g) for n, g in zip(LATE, gathered))
    o3, l3 = zip(*[_dil_fwd(zb, seq, d) for zb, d in zip(zbs, DILATIONS)])
    ob, lse_b, lse_b4, lse_b16, merged, mix, xh1, rs1, h2, h2t = _mix_out(oa, o3, l3, gn_a, gn_b, w_out, xf, ada3, ln1_g,
                                                                      ln1_b, perms, seq)
    u_a, u_g, ffn_in = _ffn_up_gate(h2, w_up, conv_w, conv_b, seq)
    dr2, acc2 = _ffn_down(ffn_in, w_down, xh1, ln1_g, ln1_b, ada3, ln2_g, ln2_b, tg, seq)

    dffn, dfi = _ffn_down_bwd(dr2, ada3, w_down, seq)
    d_w_down = _matmul_tn(dffn, ffn_in, 512, 512, "dw_down").T
    du_a, du_g, acc_ca, acc_cg, dw_up_a, dw_up_g = _ffn_gate_bwd(u_a, u_g, dfi, conv_w, conv_b, h2t, seq)
    dr1, dmix, acc1 = _ffn_up_bwd(du_a, du_g, w_up, dr2, xh1, rs1, mix, ada3, ln1_g, ln1_b, seq)

    doa, dob, dob4, dob16, dl_a, dl_b, dl_b4, dl_b16, acc_gn = _mix_out_bwd(dmix, w_out, oa, ob, gn_a, gn_b, perms, seq)
    d_w_out = _matmul_tn(merged, dmix, 512, 512, "dw_out")
    late_grads = dict(w_out=d_w_out, w_up=(dw_up_a, dw_up_g), conv_w=jnp.concatenate([acc_ca[0:3], acc_cg[0:3]], axis=1),
                      w_down=d_w_down)
    dka, dva, df_k, dqt, df_q, late_parts = _fox_bwd(za, doa, f_col, lse_row_a, _rows(dl_a), seq,
                                                     [_payload(n, _dest_major(n, late_grads[n])) for n in LATE])
    dfa_t, dbf = _fgate_bwd(_rows(df_k) + df_q, fa_t, bf, seq)
    flat = lambda a: a.reshape(t_all, a.shape[-1])
    dil = []
    for zb, d, do, lse, dl in zip(zbs, DILATIONS, (dob, flat(dob4), flat(dob16)),
                                  (lse_b, flat(lse_b4), flat(lse_b16)), (dl_b, flat(dl_b4), flat(dl_b16))):
        dil.append(_dil_bwd(zb, do, lse, dl, seq, d))
    dfa16 = jnp.zeros((16, t_all), BF16).at[:N_HEADS].set(dfa_t.astype(BF16))
    grad_x, dz, acc0 = _inproj_bwd(dqt, dka, dva, dil[0], dil[1], dil[2], dfa16, pos, wqkv, wf16, freq, perms, dr1, xf,
                                   ada3, seq)
    d_wqkv = _matmul_tn(h1, dz, 512, 512, "dw_in")
    d_wf = _matmul_rows(dfa16, h1, 512, "dw_fgate")[:N_HEADS].T
    d_w_in = jnp.concatenate([d_wqkv[:, :3 * WIDTH], d_wf, d_wqkv[:, 3 * WIDTH:]], axis=1)

    dada = jnp.concatenate([acc0[8:8 + nbat], acc0[:nbat], acc1[24:24 + nbat], acc1[16:16 + nbat], acc1[8:8 + nbat],
                            acc2[8:8 + nbat]], axis=1)

    grads = dict(
        dada=dada, b_ada=jnp.sum(dada, axis=0, keepdims=True), w_in=d_w_in, b_fgate=dbf[:, 0][None, :],
        gn_a=acc_gn[0:1, :WIDTH], gn_b=acc_gn[0:1, WIDTH:], ln1_g=acc1[0:1], ln1_b=acc1[1:2],
        conv_b=jnp.concatenate([acc_ca[3:4], acc_cg[3:4]], axis=1), ln2_g=acc2[0:1], ln2_b=acc2[1:2])
    return acc2[2:3], grad_x.reshape(x.shape), grads, dict(zip(LATE, late_parts))


LATE = ("w_out", "w_up", "conv_w", "w_down")
BIG = ("w_ada", "w_in") + LATE
COLUMN_SHARDED = ("w_ada", "w_in", "w_up", "conv_w")


def _payload(name, a):
    return a if name == "conv_w" else a.astype(BF16)
SMALL = ("b_ada", "b_fgate", "gn_a", "gn_b", "ln1_g", "ln1_b", "conv_b", "ln2_g", "ln2_b")
ADAM_ROWS = dict(w_ada=256, w_in=256, w_out=128, w_up=256, conv_w=3, w_down=176)
SMALL_ROWS = 24


def _full_from_gathered(name, g):
    if name in COLUMN_SHARDED:
        return g.transpose(1, 0, 2).reshape(g.shape[1], N_DEV * g.shape[2])
    return g.reshape(N_DEV * g.shape[1], g.shape[2])


def _dest_major(name, full):
    if name in COLUMN_SHARDED:
        parts = full if isinstance(full, tuple) else (full,)
        per = N_DEV // len(parts)
        return jnp.concatenate([p.reshape(p.shape[0], per, p.shape[1] // per).transpose(1, 0, 2) for p in parts], axis=0)
    return full.reshape(N_DEV, full.shape[0] // N_DEV, full.shape[1])


def _pack_small(vals, extra=None):
    parts = [vals[n].reshape(-1) for n in SMALL]
    if extra is not None:
        parts.append(extra.reshape(-1))
    flat = jnp.concatenate(parts)
    return jnp.pad(flat, (0, SMALL_ROWS * D_MODEL - flat.shape[0])).reshape(SMALL_ROWS, D_MODEL)


def _unpack_small(packed, like):
    flat = packed.reshape(-1)
    out, off = {}, 0
    for n in SMALL:
        size = like[n].size
        out[n] = flat[off:off + size].reshape(like[n].shape)
        off += size
    return out, flat[off:off + D_MODEL]


def kernel(x, c, positions, w_ada, b_ada, w_in, b_fgate, gn_a, gn_b, w_out, ln1_g, ln1_b, w_up, conv_w, conv_b, w_down, ln2_g, ln2_b, loss_target, m_w_ada, m_b_ada, m_w_in, m_b_fgate, m_gn_a, m_gn_b, m_w_out, m_ln1_g, m_ln1_b, m_w_up, m_conv_w, m_conv_b, m_w_down, m_ln2_g, m_ln2_b, v_w_ada, v_b_ada, v_w_in, v_b_fgate, v_gn_a, v_gn_b, v_w_out, v_ln1_g, v_ln1_b, v_w_up, v_conv_w, v_conv_b, v_w_down, v_ln2_g, v_ln2_b):
    w = dict(w_ada=w_ada[0], b_ada=b_ada, w_in=w_in[0], b_fgate=b_fgate, gn_a=gn_a, gn_b=gn_b, w_out=w_out[0],
             ln1_g=ln1_g, ln1_b=ln1_b, w_up=w_up[0], conv_w=conv_w[0], conv_b=conv_b, w_down=w_down[0], ln2_g=ln2_g,
             ln2_b=ln2_b)
    m = dict(w_ada=m_w_ada[0], b_ada=m_b_ada, w_in=m_w_in[0], b_fgate=m_b_fgate, gn_a=m_gn_a, gn_b=m_gn_b,
             w_out=m_w_out[0], ln1_g=m_ln1_g, ln1_b=m_ln1_b, w_up=m_w_up[0], conv_w=m_conv_w[0], conv_b=m_conv_b,
             w_down=m_w_down[0], ln2_g=m_ln2_g, ln2_b=m_ln2_b)
    v = dict(w_ada=v_w_ada[0], b_ada=v_b_ada, w_in=v_w_in[0], b_fgate=v_b_fgate, gn_a=v_gn_a, gn_b=v_gn_b,
             w_out=v_w_out[0], ln1_g=v_ln1_g, ln1_b=v_ln1_b, w_up=v_w_up[0], conv_w=v_conv_w[0], conv_b=v_conv_b,
             w_down=v_w_down[0], ln2_g=v_ln2_g, ln2_b=v_ln2_b)

    nbat = x.shape[0]
    me = 4 * lax.axis_index("x") + 2 * lax.axis_index("y") + lax.axis_index("c")
    ada_cols = w["w_ada"].shape[1]

    c_all, w_in_all = _gather_two_level([c, _payload("w_in", w["w_in"])], "weight_gather")
    c_all = c_all.reshape(N_DEV * nbat, D_MODEL)
    ada_mine = _ada_fwd(c_all, w["w_ada"], lax.dynamic_slice(b_ada, (0, me * ada_cols), (1, ada_cols)))
    (ada_parts,) = _exchange([ada_mine.reshape(N_DEV, nbat, ada_cols)], [False], "ada_exchange")
    ada3 = ada_parts.transpose(1, 0, 2).reshape(nbat, 6, D_MODEL)

    loss_lanes, grad_x, g_local, parts = _local_step(
        x, positions, loss_target, ada3, _full_from_gathered("w_in", w_in_all), b_fgate, gn_a, gn_b, ln1_g, ln1_b,
        conv_b, ln2_g, ln2_b, {n: _payload(n, w[n]) for n in LATE})

    parts["w_in"], dada_all, small_all = _exchange(
        [_payload("w_in", _dest_major("w_in", g_local["w_in"])), g_local["dada"], _pack_small(g_local, loss_lanes)],
        [False, True, True], "grad_exchange")
    dada_cols = lax.dynamic_slice(dada_all.reshape(N_DEV * nbat, 6 * D_MODEL), (0, me * ada_cols),
                                  (N_DEV * nbat, ada_cols))
    parts["w_ada"] = _ada_bwd(c_all, dada_cols)[None]

    grad, delta, new_m, new_v = {}, {}, {}, {}
    for n in BIG:
        grad[n], delta[n], new_m[n], new_v[n] = (
            a[None] for a in _adamw(parts[n], w[n], m[n], v[n], ADAM_ROWS[n], "adamw_" + n))
    packed = _adamw(small_all, _pack_small(w), _pack_small(m), _pack_small(v), SMALL_ROWS, "adamw_small")
    for dst, pk in zip((grad, delta, new_m, new_v), packed):
        vals, lanes = _unpack_small(pk, w)
        dst.update(vals)
        if dst is grad:
            loss = jnp.sum(lanes)

    order = ("w_ada", "b_ada", "w_in", "b_fgate", "gn_a", "gn_b", "w_out", "ln1_g", "ln1_b", "w_up", "conv_w", "conv_b",
             "w_down", "ln2_g", "ln2_b")
    return (loss, grad_x, *[grad[n] for n in order], *[delta[n] for n in order], *[new_m[n] for n in order],
            *[new_v[n] for n in order])
```

```python
import functools

import numpy as np
import jax
import jax.numpy as jnp
from jax import lax
from jax.experimental import pallas as pl
from jax.experimental.pallas import tpu as pltpu

F32, BF16 = jnp.float32, jnp.bfloat16
MESH = pl.DeviceIdType.MESH
ANY = pl.BlockSpec(memory_space=pl.ANY)

D_MODEL = 1024
N_HEADS = 8
HEAD_DIM = 64
WIDTH = 512
D_FF = 2816
N_DEV = 8
ROPE_DIMS = 16
ROPE_THETA = 500000.0
ALPHA = 2.0 ** 0.25
LN_EPS = 1e-5
RMS_EPS = 1e-6
NEG = -1e30
Q_SCALE = 0.125
LOG2E = 1.4426950408889634
BLK = 128
LANES = 128
VMEM_LIMIT_BYTES = 56 * 1024 * 1024

ADAM_LR, ADAM_B1, ADAM_B2, ADAM_EPS, ADAM_WD, ADAM_STEP = 0.001, 0.9, 0.999, 1e-08, 0.01, 10


def _params(vmem=VMEM_LIMIT_BYTES):
    return pltpu.CompilerParams(vmem_limit_bytes=vmem)


def _nn(a, b):
    return jnp.dot(a, b, preferred_element_type=F32)


def _nt(a, b):
    return lax.dot_general(a, b, (((1,), (1,)), ((), ())), preferred_element_type=F32)


def _tn(a, b):
    return lax.dot_general(a, b, (((0,), (0,)), ((), ())), preferred_element_type=F32)


def _head_mats():
    r = lax.broadcasted_iota(jnp.int32, (LANES, WIDTH), 0)
    c = lax.broadcasted_iota(jnp.int32, (LANES, WIDTH), 1)
    e = ((c >> 6) == r).astype(BF16)
    r2 = lax.broadcasted_iota(jnp.int32, (WIDTH, LANES), 0)
    c2 = lax.broadcasted_iota(jnp.int32, (WIDTH, LANES), 1)
    et = ((r2 >> 6) == c2).astype(BF16)
    return e, et


def _split3(x):
    hi = x.astype(BF16)
    r = x - hi.astype(F32)
    mid = r.astype(BF16)
    return hi, mid, (r - mid.astype(F32)).astype(BF16)


def _hexp(w, e):
    return sum(_nn(part, e) for part in _split3(w)[:2])


def _hsum(x, et):
    return sum(_nn(part, et) for part in _split3(x)[:2])


def _perm_matrix(rows, d, transpose):
    i = np.arange(rows)
    j = (i % (rows // d)) * d + i // (rows // d)
    p = np.zeros((rows, rows), np.float32)
    p[i, j] = 1.0
    return jnp.asarray(p.T if transpose else p, BF16)


def _permute_f32(p, x):
    return sum(_nn(p, part) for part in _split3(x))


def _store_classes(ref, y, d):
    n = y.shape[0] // d
    for r in range(d):
        ref[r] = y[r * n:(r + 1) * n, :]


def _load_classes(ref, d):
    return jnp.concatenate([ref[r] for r in range(d)], axis=0)


def _rope_tabs(pos_ref, fr_ref, sign):
    ang = pos_ref[...].astype(F32) * fr_ref[...]
    lane = lax.broadcasted_iota(jnp.int32, ang.shape, 1) & (HEAD_DIM - 1)
    m1 = lane < ROPE_DIMS // 2
    m2 = (lane >= ROPE_DIMS // 2) & (lane < ROPE_DIMS)
    cos = jnp.cos(ang)
    sin = jnp.sin(ang) * sign
    return (jnp.where(m1 | m2, cos, 1.0), jnp.where(m1, -sin, 0.0), jnp.where(m2, sin, 0.0))


def _rope(z, tabs):
    c, s1, s2 = tabs
    parts = []
    for p in range(z.shape[1] // LANES):
        zp = z[:, LANES * p:LANES * (p + 1)]
        parts.append(zp * c + pltpu.roll(zp, LANES - 8, 1) * s1 + pltpu.roll(zp, 8, 1) * s2)
    return jnp.concatenate(parts, axis=1)


def _half_masks(rows):
    lane = lax.broadcasted_iota(jnp.int32, (rows, LANES), 1)
    lo = lane < HEAD_DIM
    return lo, jnp.logical_not(lo)


def _layer_norm_bwd(dxh, xh, rstd):
    m1 = jnp.mean(dxh, axis=1, keepdims=True)
    m2 = jnp.mean(dxh * xh, axis=1, keepdims=True)
    return rstd * (dxh - m1 - xh * m2)


def _coords():
    return lax.axis_index("x"), lax.axis_index("y"), lax.axis_index("c")


def _peer(x, y, c, k):
    return (1 - x if k & 4 else x, 1 - y if k & 2 else y, 1 - c if k & 1 else c)


def _comm_sems(n):
    return [pltpu.SemaphoreType.DMA((N_DEV - 1, n)), pltpu.SemaphoreType.DMA((N_DEV - 1, n)),
            pltpu.SemaphoreType.DMA((n,))]


def _comm_copies(ins, outs, to_all, sems):
    send_sems, recv_sems, local_sems = sems
    x, y, c = _coords()
    me = 4 * x + 2 * y + c
    copies = [pltpu.make_async_copy(ins[t] if to_all[t] else ins[t].at[me], outs[t].at[me], local_sems.at[t])
              for t in range(len(ins))]
    for k in range(1, N_DEV):
        px, py, pc = _peer(x, y, c, k)
        dest = 4 * px + 2 * py + pc
        for t in range(len(ins)):
            copies.append(pltpu.make_async_remote_copy(
                src_ref=ins[t] if to_all[t] else ins[t].at[dest], dst_ref=outs[t].at[me],
                send_sem=send_sems.at[k - 1, t], recv_sem=recv_sems.at[k - 1, t],
                device_id=(px, py, pc), device_id_type=MESH))
    return copies


def _comm_out_shapes(ins, to_all):
    return [jax.ShapeDtypeStruct(((N_DEV,) + a.shape) if ta else a.shape, a.dtype) for a, ta in zip(ins, to_all)]


def _exchange(ins, to_all, name):
    n = len(ins)

    def body(*refs):
        copies = _comm_copies(refs[:n], refs[n:2 * n], to_all, refs[2 * n:])
        for cp in copies:
            cp.start()
        for cp in copies:
            cp.wait()

    return pl.pallas_call(
        body, name=name, out_shape=_comm_out_shapes(ins, to_all), in_specs=[ANY] * n, out_specs=[ANY] * n,
        scratch_shapes=_comm_sems(n),
    )(*ins)


def _gather_two_level(ins, name):
    n = len(ins)

    def body(*refs):
        srcs, outs = refs[:n], refs[n:2 * n]
        send_sems, recv_sems, local_sems = refs[2 * n:]
        x, y, c = _coords()
        me = 4 * x + 2 * y + c
        sibling = (x, y, 1 - c)
        chips = [(1 - x, y), (x, 1 - y), (1 - x, 1 - y)]
        slot = lambda px, py, pc: 4 * px + 2 * py + pc

        def copy(k, t, block, to, own=False):
            return pltpu.make_async_remote_copy(
                src_ref=srcs[t] if own else outs[t].at[block], dst_ref=outs[t].at[block],
                send_sem=send_sems.at[k, t], recv_sem=recv_sems.at[k, t], device_id=to, device_id_type=MESH)

        local = [pltpu.make_async_copy(srcs[t], outs[t].at[me], local_sems.at[t]) for t in range(n)]
        first = [copy(0, t, me, sibling, own=True) for t in range(n)]
        first += [copy(1 + j, t, me, (*chip, c), own=True) for j, chip in enumerate(chips) for t in range(n)]
        for cp in local + first:
            cp.start()
        passed = []
        for j, chip in enumerate(chips):
            for t in range(n):
                copy(1 + j, t, slot(*chip, c), (x, y, c)).wait_recv()
                cp = copy(4 + j, t, slot(*chip, c), sibling)
                cp.start()
                passed.append(cp)
        for t in range(n):
            copy(0, t, slot(x, y, 1 - c), (x, y, c)).wait_recv()
            for j, chip in enumerate(chips):
                copy(4 + j, t, slot(*chip, 1 - c), (x, y, c)).wait_recv()
        for cp in first + passed:
            cp.wait_send()
        for cp in local:
            cp.wait()

    return pl.pallas_call(
        body, name=name, out_shape=_comm_out_shapes(ins, [True] * n), in_specs=[ANY] * n, out_specs=[ANY] * n,
        scratch_shapes=_comm_sems(n),
    )(*ins)


def _adamw(parts, w, m, v, rows, name):
    n_parts, r_all, cols = parts.shape
    c1 = 1.0 - ADAM_B1 ** ADAM_STEP
    c2 = 1.0 - ADAM_B2 ** ADAM_STEP

    def body(p_ref, w_ref, m_ref, v_ref, g_ref, d_ref, mo_ref, vo_ref):
        g = p_ref[0].astype(F32)
        for s in range(1, n_parts):
            g = g + p_ref[s].astype(F32)
        mn = ADAM_B1 * m_ref[...] + (1.0 - ADAM_B1) * g
        vn = ADAM_B2 * v_ref[...] + (1.0 - ADAM_B2) * (g * g)
        m_hat = mn / c1
        v_hat = vn / c2
        g_ref[...] = g
        d_ref[...] = -ADAM_LR * (m_hat / (jnp.sqrt(v_hat) + ADAM_EPS) + ADAM_WD * w_ref[...])
        mo_ref[...] = mn
        vo_ref[...] = vn

    spec = pl.BlockSpec((rows, cols), lambda i: (i, 0))
    return pl.pallas_call(
        body, name=name, grid=(r_all // rows,),
        in_specs=[pl.BlockSpec((n_parts, rows, cols), lambda i: (0, i, 0)), spec, spec, spec],
        out_specs=[spec] * 4, out_shape=[jax.ShapeDtypeStruct((r_all, cols), F32)] * 4,
        compiler_params=_params(),
    )(parts, w, m, v)


def _matmul_tn(a, b, chunk, tk, name):
    t_all, k1 = a.shape
    n = b.shape[1]

    def body(a_ref, b_ref, o_ref):
        @pl.when(pl.program_id(0) == 0)
        def _():
            o_ref[...] = jnp.zeros_like(o_ref)
        at = a_ref[...].astype(F32).T.astype(BF16)
        for j in range(0, n, chunk):
            cs = slice(j, min(j + chunk, n))
            o_ref[:, cs] += _nn(at, b_ref[:, cs])

    return pl.pallas_call(
        body, name=name, grid=(t_all // tk,),
        in_specs=[pl.BlockSpec((tk, k1), lambda t: (t, 0)), pl.BlockSpec((tk, n), lambda t: (t, 0))],
        out_specs=pl.BlockSpec((k1, n), lambda t: (0, 0)),
        out_shape=jax.ShapeDtypeStruct((k1, n), F32), compiler_params=_params(),
    )(a, b)


def _matmul_rows(a, b, tk, name):
    r, t_all = a.shape
    n = b.shape[1]

    def body(a_ref, b_ref, o_ref):
        @pl.when(pl.program_id(0) == 0)
        def _():
            o_ref[...] = jnp.zeros_like(o_ref)
        o_ref[...] += _nn(a_ref[...], b_ref[...])

    return pl.pallas_call(
        body, name=name, grid=(t_all // tk,),
        in_specs=[pl.BlockSpec((r, tk), lambda t: (0, t)), pl.BlockSpec((tk, n), lambda t: (t, 0))],
        out_specs=pl.BlockSpec((r, n), lambda t: (0, 0)),
        out_shape=jax.ShapeDtypeStruct((r, n), F32), compiler_params=_params(),
    )(a, b)


def _ada_fwd(c_all, w_ada, b_ada):
    whole = lambda a: pl.BlockSpec(a.shape, lambda j: (0, 0))

    def body(c_ref, w_ref, b_ref, o_ref):
        cv = c_ref[...]
        s = (cv * jax.nn.sigmoid(cv)).astype(BF16)
        o_ref[...] = _nn(s, w_ref[...].astype(BF16)) + b_ref[...]

    out = jax.ShapeDtypeStruct((c_all.shape[0], w_ada.shape[1]), F32)
    return pl.pallas_call(
        body, name="ada_fwd", grid=(1,), in_specs=[whole(c_all), whole(w_ada), whole(b_ada)], out_specs=whole(out),
        out_shape=out, compiler_params=_params(),
    )(c_all, w_ada, b_ada)


def _ada_bwd(c_all, dada):
    whole = lambda a: pl.BlockSpec(a.shape, lambda j: (0, 0))

    def body(c_ref, d_ref, o_ref):
        cv = c_ref[...]
        s = (cv * jax.nn.sigmoid(cv)).astype(BF16)
        o_ref[...] = _tn(s, d_ref[...].astype(BF16))

    out = jax.ShapeDtypeStruct((D_MODEL, dada.shape[1]), F32)
    return pl.pallas_call(
        body, name="ada_bwd", grid=(1,), in_specs=[whole(c_all), whole(dada)], out_specs=whole(out), out_shape=out,
        compiler_params=_params(),
    )(c_all, dada)


TOK_TM = 256
DILATIONS = (1, 4, 16)


def _class_spec(d, width, nts):
    return pl.BlockSpec((d, TOK_TM // d, width), lambda i: (i // nts, i % nts, 0))


def _class_shape(t_all, seq, d, width, dtype):
    return jax.ShapeDtypeStruct((t_all // seq * d, seq // d, width), dtype)


def _inproj(x, ada3, pos, wqkv, wf16, freq, perms, seq):
    t_all = x.shape[0]
    tm = TOK_TM
    nts = seq // tm

    def body(x_ref, ada_ref, pos_ref, w_ref, wf_ref, fr_ref, p4_ref, p16_ref, h1_ref, za_ref, zb_ref, zb4_ref,
             zb16_ref, vt_ref, fa_ref):
        h1 = (x_ref[...] * (1.0 + ada_ref[0, 1:2, :]) + ada_ref[0, 0:1, :]).astype(BF16)
        h1_ref[...] = h1
        tabs = _rope_tabs(pos_ref, fr_ref, 1.0)
        for n in range(6):
            z = _nn(h1, w_ref[:, n * WIDTH:(n + 1) * WIDTH])
            if n in (3, 4):
                z = _rope(z, tabs)
            if n in (0, 3):
                z = z * (Q_SCALE * LOG2E)
            if n == 2:
                vt_ref[...] = z.T.astype(BF16)
            dst = za_ref if n < 3 else zb_ref
            dst[:, (n % 3) * WIDTH:(n % 3 + 1) * WIDTH] = z.astype(BF16)
        fa_ref[...] = _nt(wf_ref[...], h1)[:N_HEADS]
        zb = zb_ref[...]
        _store_classes(zb4_ref, _nn(p4_ref[...], zb).astype(BF16), 4)
        _store_classes(zb16_ref, _nn(p16_ref[...], zb).astype(BF16), 16)

    tok = lambda w: pl.BlockSpec((tm, w), lambda i: (i, 0))
    whole = lambda a: pl.BlockSpec(a.shape, lambda i: (0, 0))
    return pl.pallas_call(
        body, name="inproj", grid=(t_all // tm,),
        in_specs=[tok(D_MODEL), pl.BlockSpec((1, 6, D_MODEL), lambda i: (i // nts, 0, 0)), tok(1), whole(wqkv),
                  whole(wf16), pl.BlockSpec((1, LANES), lambda i: (0, 0)), whole(perms[0]), whole(perms[1])],
        out_specs=[tok(D_MODEL), tok(3 * WIDTH), tok(3 * WIDTH), _class_spec(4, 3 * WIDTH, nts),
                   _class_spec(16, 3 * WIDTH, nts), pl.BlockSpec((WIDTH, tm), lambda i: (i // nts, i % nts)),
                   pl.BlockSpec((N_HEADS, tm), lambda i: (0, i))],
        out_shape=[jax.ShapeDtypeStruct((t_all, D_MODEL), BF16), jax.ShapeDtypeStruct((t_all, 3 * WIDTH), BF16),
                   jax.ShapeDtypeStruct((t_all, 3 * WIDTH), BF16), _class_shape(t_all, seq, 4, 3 * WIDTH, BF16),
                   _class_shape(t_all, seq, 16, 3 * WIDTH, BF16),
                   jax.ShapeDtypeStruct((t_all // seq * WIDTH, seq), BF16),
                   jax.ShapeDtypeStruct((N_HEADS, t_all), F32)],
        compiler_params=_params(),
    )(x, ada3, pos, wqkv, wf16, freq, perms[0], perms[1])


def _chunk_rows(a_t, seq):
    t_all = a_t.shape[1]
    return a_t.reshape(N_HEADS, t_all // seq, seq // LANES, LANES).transpose(1, 0, 2, 3).reshape(-1, LANES)


def _unchunk_rows(a, seq):
    nbat = a.shape[0] * LANES // (N_HEADS * seq)
    return a.reshape(nbat, N_HEADS, seq // LANES, LANES).transpose(1, 0, 2, 3).reshape(N_HEADS, nbat * seq)


def _chunk_carry(tot, nchunk, later):
    rows = tot.shape[0]
    r = lax.broadcasted_iota(jnp.int32, (rows, rows), 0)
    c = lax.broadcasted_iota(jnp.int32, (rows, rows), 1)
    sel = ((r // nchunk) == (c // nchunk)) & ((c > r) if later else (c < r))
    mat = sel.astype(BF16)
    return sum(_nn(mat, part) for part in _split3(jnp.broadcast_to(tot, (rows, LANES))))


def _fgate_fwd(fa_t, bf, seq):
    x = _chunk_rows(fa_t, seq)
    rows = x.shape[0]
    nchunk = seq // LANES
    bias = jnp.broadcast_to(bf.reshape(1, N_HEADS, 1), (rows // (N_HEADS * nchunk), N_HEADS, nchunk)).reshape(rows, 1)

    def body(x_ref, b_ref, f_ref):
        lane = lax.broadcasted_iota(jnp.int32, (rows, LANES), 1)
        xv = x_ref[...] + b_ref[...]
        lf = jnp.minimum(xv, 0.0) - jnp.log(1.0 + jnp.exp(-jnp.abs(xv)))
        for s in (1, 2, 4, 8, 16, 32, 64):
            lf = lf + jnp.where(lane >= s, pltpu.roll(lf, s, 1), 0.0)
        f_ref[...] = lf + _chunk_carry(lf[:, LANES - 1:LANES], nchunk, False)

    whole = lambda a: pl.BlockSpec(a.shape, lambda i: (0, 0))
    out = pl.pallas_call(
        body, name="fgate_fwd", grid=(1,), in_specs=[whole(x), whole(bias)], out_specs=whole(x),
        out_shape=jax.ShapeDtypeStruct(x.shape, F32), compiler_params=_params(),
    )(x, bias)
    return _unchunk_rows(out, seq)


def _fgate_bwd(df_t, fa_t, bf, seq):
    d_in = _chunk_rows(df_t, seq)
    x = _chunk_rows(fa_t, seq)
    rows = x.shape[0]
    nchunk = seq // LANES
    bias = jnp.broadcast_to(bf.reshape(1, N_HEADS, 1), (rows // (N_HEADS * nchunk), N_HEADS, nchunk)).reshape(rows, 1)

    def body(d_ref, x_ref, b_ref, o_ref, s_ref):
        lane = lax.broadcasted_iota(jnp.int32, (rows, LANES), 1)
        d = d_ref[...]
        for s in (1, 2, 4, 8, 16, 32, 64):
            d = d + jnp.where(lane < LANES - s, pltpu.roll(d, LANES - s, 1), 0.0)
        d = d + _chunk_carry(d[:, 0:1], nchunk, True)
        dfa = d * jax.nn.sigmoid(-(x_ref[...] + b_ref[...]))
        o_ref[...] = dfa
        g = lax.broadcasted_iota(jnp.int32, (2 * N_HEADS, rows), 0)
        r = lax.broadcasted_iota(jnp.int32, (2 * N_HEADS, rows), 1)
        group = (((r // nchunk) % N_HEADS) == g).astype(BF16)
        per_head = sum(_nn(group, part) for part in _split3(dfa))[:N_HEADS]
        s_ref[...] = jnp.broadcast_to(jnp.sum(per_head, axis=1, keepdims=True), (N_HEADS, LANES))

    whole = lambda a: pl.BlockSpec(a.shape, lambda i: (0, 0))
    dfa, sums = pl.pallas_call(
        body, name="fgate_bwd", grid=(1,), in_specs=[whole(d_in), whole(x), whole(bias)],
        out_specs=[whole(x), pl.BlockSpec((N_HEADS, LANES), lambda i: (0, 0))],
        out_shape=[jax.ShapeDtypeStruct(x.shape, F32), jax.ShapeDtypeStruct((N_HEADS, LANES), F32)],
        compiler_params=_params(),
    )(d_in, x, bias)
    return _unchunk_rows(dfa, seq), sums


FOX_T = 256


def _fox_prep(dst, src_ref, lo, hi):
    for p in range(4):
        v = src_ref[:, LANES * p:LANES * (p + 1)]
        dst[2 * p] = jnp.where(lo, v, jnp.zeros_like(v))
        dst[2 * p + 1] = jnp.where(hi, v, jnp.zeros_like(v))


def _fox_fwd(za, vt, f_col, seq, shards):
    t_all = za.shape[0]
    tq = FOX_T
    nq = seq // tq
    nbat = t_all // seq
    n = len(shards)
    to_all = [True] * n

    def body(*refs):
        q_ref, k_ref, vt_ref, fc_ref = refs[:4]
        o_ref, lse_ref = refs[4 + n:6 + n]
        qm_sc, m_sc, l_sc, acc_sc, a_sc, st_sc, pe_sc = refs[6 + 2 * n:13 + 2 * n]
        comm = (refs[4:4 + n], refs[6 + n:6 + 2 * n], to_all, refs[13 + 2 * n:])
        i = pl.program_id(1)

        @pl.when((pl.program_id(0) == 0) & (i == 0))
        def _():
            for cp in _comm_copies(*comm):
                cp.start()
        lo, hi = _half_masks(tq)
        r = lax.broadcasted_iota(jnp.int32, (tq, tq), 0)
        c = lax.broadcasted_iota(jnp.int32, (tq, tq), 1)
        tri = c >= r
        _fox_prep(qm_sc, q_ref, lo, hi)
        m_sc[...] = jnp.full(m_sc.shape, NEG, F32)
        l_sc[...] = jnp.zeros_like(l_sc)
        acc_sc[...] = jnp.zeros_like(acc_sc)

        def block(j, masked):
            sl = pl.ds(pl.multiple_of(j * tq, tq), tq)
            for p in range(4):
                kj = k_ref[sl, LANES * p:LANES * (p + 1)]
                for h in (2 * p, 2 * p + 1):
                    st = _nt(kj, qm_sc[h]) - fc_ref[sl, h:h + 1]
                    st_sc[h] = jnp.where(tri, st, NEG) if masked else st
            for h in range(N_HEADS):
                st = st_sc[h]
                m = m_sc[h:h + 1, :]
                mn = jnp.maximum(m, jnp.max(st, axis=0, keepdims=True))
                a = jnp.exp2(m - mn)
                pe = jnp.exp2(st - mn)
                m_sc[h:h + 1, :] = mn
                a_sc[h:h + 1, :] = a
                l_sc[h:h + 1, :] = a * l_sc[h:h + 1, :] + jnp.sum(pe, axis=0, keepdims=True)
                pe_sc[h] = pe.astype(BF16)
            for h in range(N_HEADS):
                acc_sc[h] = a_sc[h:h + 1, :] * acc_sc[h] + _nn(vt_ref[HEAD_DIM * h:HEAD_DIM * (h + 1), sl], pe_sc[h])

        def step(j, carry):
            block(j, False)
            return carry

        lax.fori_loop(0, i, step, 0)
        block(i, True)
        lse_ref[...] = m_sc[...] + jnp.log(l_sc[...]) * LOG2E
        for p in range(4):
            ot = jnp.concatenate([acc_sc[h] / l_sc[h:h + 1, :] for h in (2 * p, 2 * p + 1)], axis=0)
            o_ref[:, LANES * p:LANES * (p + 1)] = ot.T

        @pl.when((pl.program_id(0) == nbat - 1) & (i == nq - 1))
        def _():
            for cp in _comm_copies(*comm):
                cp.wait()

    res = pl.pallas_call(
        body, name="fox_fwd", grid=(nbat, nq),
        in_specs=[pl.BlockSpec((tq, WIDTH), lambda b, i: (b * nq + i, 0)),
                  pl.BlockSpec((seq, WIDTH), lambda b, i: (b, 1)), pl.BlockSpec((WIDTH, seq), lambda b, i: (b, 0)),
                  pl.BlockSpec((seq, LANES), lambda b, i: (b, 0))] + [ANY] * n,
        out_specs=[pl.BlockSpec((tq, WIDTH), lambda b, i: (b * nq + i, 0)),
                   pl.BlockSpec((N_HEADS, tq), lambda b, i: (0, b * nq + i))] + [ANY] * n,
        out_shape=[jax.ShapeDtypeStruct((t_all, WIDTH), F32), jax.ShapeDtypeStruct((N_HEADS, t_all), F32)]
        + _comm_out_shapes(shards, to_all),
        scratch_shapes=[pltpu.VMEM((N_HEADS, tq, LANES), BF16), pltpu.VMEM((N_HEADS, tq), F32),
                        pltpu.VMEM((N_HEADS, tq), F32), pltpu.VMEM((N_HEADS, HEAD_DIM, tq), F32),
                        pltpu.VMEM((N_HEADS, tq), F32), pltpu.VMEM((N_HEADS, tq, tq), F32),
                        pltpu.VMEM((N_HEADS, tq, tq), BF16)] + _comm_sems(n),
        compiler_params=_params(),
    )(za, za, vt, f_col, *shards)
    return res[0], res[1], res[2:]


def _fox_bwd(za, do, f_col, lse_row, dl_row, seq, grads):
    t_all = za.shape[0]
    tk = FOX_T
    nk = seq // tk
    nbat = t_all // seq
    n = len(grads)
    to_all = [False] * n

    def body(*refs):
        k_ref, v_ref, q_ref, do_ref, fc_ref, lr_ref, dr_ref = refs[:7]
        dk_ref, dv_ref, df_ref, dqt_ref, dfq_ref = refs[7 + n:12 + n]
        km_sc, vm_sc, fk_sc, dk_sc, dv_sc, cs_sc, kt_sc, st_sc, dp_sc, pt_sc, ds_sc = refs[12 + 2 * n:23 + 2 * n]
        comm = (refs[7:7 + n], refs[12 + n:12 + 2 * n], to_all, refs[23 + 2 * n:])
        j = pl.program_id(1)

        @pl.when(j == 0)
        def _():
            dqt_ref[...] = jnp.zeros_like(dqt_ref)
            dfq_ref[...] = jnp.zeros_like(dfq_ref)

        @pl.when((pl.program_id(0) == 0) & (j == 0))
        def _():
            for cp in _comm_copies(*comm):
                cp.start()
        lo, hi = _half_masks(tk)
        r = lax.broadcasted_iota(jnp.int32, (tk, tk), 0)
        c = lax.broadcasted_iota(jnp.int32, (tk, tk), 1)
        tri = c >= r
        _fox_prep(km_sc, k_ref, lo, hi)
        _fox_prep(vm_sc, v_ref, lo, hi)
        for h in range(N_HEADS):
            fk_sc[h] = jnp.broadcast_to(fc_ref[:, h:h + 1], (tk, tk))
        for p in range(4):
            kt_sc[p] = k_ref[:, LANES * p:LANES * (p + 1)].astype(F32).T.astype(BF16)
        dk_sc[...] = jnp.zeros_like(dk_sc)
        dv_sc[...] = jnp.zeros_like(dv_sc)
        cs_sc[...] = jnp.zeros_like(cs_sc)

        def block(i, masked):
            sl = pl.ds(pl.multiple_of(i * tk, tk), tk)
            for p in range(4):
                cs = slice(LANES * p, LANES * (p + 1))
                qi = q_ref[sl, cs]
                doi = do_ref[sl, cs]
                for h in (2 * p, 2 * p + 1):
                    st = _nt(km_sc[h], qi) - fk_sc[h] - lr_ref[h:h + 1, sl]
                    st_sc[h] = jnp.where(tri, st, NEG) if masked else st
                    dp_sc[h] = _nt(vm_sc[h], doi) - dr_ref[h:h + 1, sl]
            for h in range(N_HEADS):
                pt = jnp.exp2(st_sc[h])
                dst = pt * dp_sc[h]
                pt_sc[h] = pt.astype(BF16)
                ds_sc[h] = dst.astype(BF16)
                cs_sc[h] += dst[:, :LANES] + dst[:, LANES:]
                dfq_ref[h:h + 1, sl] += jnp.sum(dst, axis=0, keepdims=True)
            for p in range(4):
                cs = slice(LANES * p, LANES * (p + 1))
                qi = q_ref[sl, cs]
                doi = do_ref[sl, cs]
                for h in (2 * p, 2 * p + 1):
                    dv_sc[h] += _nn(pt_sc[h], doi)
                    dk_sc[h] += _nn(ds_sc[h], qi)
                    kt = kt_sc[p, HEAD_DIM * (h % 2):HEAD_DIM * (h % 2 + 1), :]
                    dqt_ref[HEAD_DIM * h:HEAD_DIM * (h + 1), sl] += _nn(kt, ds_sc[h])

        def step(i, carry):
            block(i, False)
            return carry

        block(j, True)
        lax.fori_loop(j + 1, nk, step, 0)
        df_ref[...] = jnp.zeros_like(df_ref)
        for p in range(4):
            cs = slice(LANES * p, LANES * (p + 1))
            dk_ref[:, cs] = (jnp.where(lo, dk_sc[2 * p], dk_sc[2 * p + 1]) * (1.0 / LOG2E)).astype(BF16)
            dv_ref[:, cs] = jnp.where(lo, dv_sc[2 * p], dv_sc[2 * p + 1]).astype(BF16)
            for h in (2 * p, 2 * p + 1):
                df_ref[:, h:h + 1] = -jnp.sum(cs_sc[h], axis=1, keepdims=True)

        @pl.when(j == nk - 1)
        def _():
            dqt_ref[...] = dqt_ref[...] * Q_SCALE

        @pl.when((pl.program_id(0) == nbat - 1) & (j == nk - 1))
        def _():
            for cp in _comm_copies(*comm):
                cp.wait()

    tile = lambda w, col: pl.BlockSpec((tk, w), lambda b, j: (b * nk + j, col))
    full = lambda col: pl.BlockSpec((seq, WIDTH), lambda b, j: (b, col))
    row = pl.BlockSpec((N_HEADS, seq), lambda b, j: (0, b))
    acc = pltpu.VMEM((N_HEADS, tk, LANES), F32)
    res = pl.pallas_call(
        body, name="fox_bwd", grid=(nbat, nk),
        in_specs=[tile(WIDTH, 1), tile(WIDTH, 2), full(0), full(0), tile(LANES, 0), row, row] + [ANY] * n,
        out_specs=[tile(WIDTH, 0), tile(WIDTH, 0), tile(LANES, 0), pl.BlockSpec((WIDTH, seq), lambda b, j: (b, 0)),
                   row] + [ANY] * n,
        out_shape=[jax.ShapeDtypeStruct((t_all, WIDTH), BF16), jax.ShapeDtypeStruct((t_all, WIDTH), BF16),
                   jax.ShapeDtypeStruct((t_all, LANES), F32), jax.ShapeDtypeStruct((nbat * WIDTH, seq), F32),
                   jax.ShapeDtypeStruct((N_HEADS, t_all), F32)] + _comm_out_shapes(grads, to_all),
        scratch_shapes=[pltpu.VMEM((N_HEADS, tk, LANES), BF16), pltpu.VMEM((N_HEADS, tk, LANES), BF16),
                        pltpu.VMEM((N_HEADS, tk, tk), F32), acc, acc, acc, pltpu.VMEM((4, LANES, tk), BF16),
                        pltpu.VMEM((N_HEADS, tk, tk), F32), pltpu.VMEM((N_HEADS, tk, tk), F32),
                        pltpu.VMEM((N_HEADS, tk, tk), BF16), pltpu.VMEM((N_HEADS, tk, tk), BF16)]
        + _comm_sems(n),
        compiler_params=_params(),
    )(za, za, za, do, f_col, lse_row, dl_row, *grads)
    return res[0], res[1], res[2], res[3], res[4], res[5:]


DIL_SUB = 4


def _dil_mask(has_prev):
    qi = lax.broadcasted_iota(jnp.int32, (BLK, 2 * BLK), 0)
    kj = lax.broadcasted_iota(jnp.int32, (BLK, 2 * BLK), 1)
    dist = qi + BLK - kj
    band = (dist >= 0) & (dist <= BLK)
    return band if has_prev is True else band & ((kj >= BLK) | has_prev)


def _dil_geometry(t_all, seq, d, max_sub=DIL_SUB):
    length = seq // d
    nbs = length // BLK
    sub = min(max_sub, nbs)
    spb = nbs // sub
    tile = lambda width, col: pl.BlockSpec((BLK * sub, width), lambda s: (s, col))
    whole = lambda width, col: pl.BlockSpec((length, width), lambda s: (s // spb, col))
    return nbs, sub, spb, t_all // (BLK * sub), tile, whole


def _blk(i):
    return pl.ds(pl.multiple_of(i * BLK, BLK), BLK)


def _dil_fwd(zb, seq, d):
    t_all = zb.shape[0]
    nbs, sub, spb, steps, tile, whole = _dil_geometry(t_all, seq, d, 2 * DIL_SUB)

    def body(q_ref, k_ref, v_ref, o_ref, lse_ref, s_sc, p_sc):
        first = (pl.program_id(0) % spb) * sub
        lo, hi = _half_masks(BLK)
        lse_ref[...] = jnp.zeros_like(lse_ref)
        for j in range(sub):
            blk = first + j
            mask = _dil_mask(blk != 0 if j == 0 else True)
            for p in range(4):
                cs = slice(LANES * p, LANES * (p + 1))
                qp = q_ref[BLK * j:BLK * (j + 1), cs]
                kcat = jnp.concatenate([k_ref[_blk(jnp.maximum(blk - 1, 0)), cs], k_ref[_blk(blk), cs]], axis=0)
                for e in (0, 1):
                    qe = jnp.where(lo if e == 0 else hi, qp, jnp.zeros_like(qp))
                    s_sc[N_HEADS * j + 2 * p + e] = jnp.where(mask, _nt(qe, kcat), NEG)
        inv = []
        for i in range(N_HEADS * sub):
            s = s_sc[i]
            m = jnp.max(s, axis=1, keepdims=True)
            pe = jnp.exp2(s - m)
            l = jnp.sum(pe, axis=1, keepdims=True)
            p_sc[i] = pe.astype(BF16)
            inv.append(1.0 / l)
            j, h = divmod(i, N_HEADS)
            lse_ref[BLK * j:BLK * (j + 1), h:h + 1] = m + jnp.log(l) * LOG2E
        for j in range(sub):
            blk = first + j
            for p in range(4):
                cs = slice(LANES * p, LANES * (p + 1))
                vcat = jnp.concatenate([v_ref[_blk(jnp.maximum(blk - 1, 0)), cs], v_ref[_blk(blk), cs]], axis=0)
                res = [_nn(p_sc[N_HEADS * j + h], vcat) * inv[N_HEADS * j + h] for h in (2 * p, 2 * p + 1)]
                o_ref[BLK * j:BLK * (j + 1), cs] = jnp.where(lo, res[0], res[1]).astype(BF16)

    return pl.pallas_call(
        body, name=f"dil_fwd_{d}", grid=(steps,), in_specs=[tile(WIDTH, 0), whole(WIDTH, 1), whole(WIDTH, 2)],
        out_specs=[tile(WIDTH, 0), tile(LANES, 0)],
        out_shape=[jax.ShapeDtypeStruct((t_all, WIDTH), BF16), jax.ShapeDtypeStruct((t_all, LANES), F32)],
        scratch_shapes=[pltpu.VMEM((N_HEADS * sub, BLK, 2 * BLK), F32),
                        pltpu.VMEM((N_HEADS * sub, BLK, 2 * BLK), BF16)],
        compiler_params=_params(),
    )(zb, zb, zb)


def _dil_bwd(zb, do, lse, dl, seq, d):
    t_all = zb.shape[0]
    length = seq // d
    nbs, sub, spb, steps, tile, whole = _dil_geometry(t_all, seq, d, 2 if length >= 4096 else DIL_SUB)

    def body(k_ref, v_ref, q_ref, do_ref, lse_ref, dl_ref, dq_ref, dk_ref, dv_ref, s_sc, dp_sc, pt_sc, ds_sc, kt_sc,
             dqt_sc):
        step = pl.program_id(0) % spb
        first = step * sub

        @pl.when(step == 0)
        def _():
            dqt_sc[...] = jnp.zeros_like(dqt_sc)
        r = lax.broadcasted_iota(jnp.int32, (BLK, 2 * BLK), 0)
        c = lax.broadcasted_iota(jnp.int32, (BLK, 2 * BLK), 1)
        same = (c < BLK) & (c >= r)
        later = (c >= BLK) & (c - BLK <= r)
        lo, hi = _half_masks(BLK)
        for j in range(sub):
            blk = first + j
            rows = slice(BLK * j, BLK * (j + 1))
            nxt = _blk(jnp.minimum(blk + 1, nbs - 1))
            mask = same | (later & (blk + 1 != nbs)) if j == sub - 1 else same | later
            lrows = jnp.concatenate([lse_ref[_blk(blk), :].T, lse_ref[nxt, :].T], axis=1)
            erows = jnp.concatenate([dl_ref[_blk(blk), :].T, dl_ref[nxt, :].T], axis=1)
            for p in range(4):
                cs = slice(LANES * p, LANES * (p + 1))
                kp = k_ref[rows, cs]
                vp = v_ref[rows, cs]
                kt_sc[4 * j + p] = kp.astype(F32).T.astype(BF16)
                qcat = jnp.concatenate([q_ref[_blk(blk), cs], q_ref[nxt, cs]], axis=0)
                dcat = jnp.concatenate([do_ref[_blk(blk), cs], do_ref[nxt, cs]], axis=0)
                for e in (0, 1):
                    h = 2 * p + e
                    sel = lo if e == 0 else hi
                    ke = jnp.where(sel, kp, jnp.zeros_like(kp))
                    ve = jnp.where(sel, vp, jnp.zeros_like(vp))
                    s_sc[N_HEADS * j + h] = jnp.where(mask, _nt(ke, qcat) - lrows[h:h + 1, :], NEG)
                    dp_sc[N_HEADS * j + h] = _nt(ve, dcat) - erows[h:h + 1, :]
        for i in range(N_HEADS * sub):
            pt = jnp.exp2(s_sc[i])
            pt_sc[i] = pt.astype(BF16)
            ds_sc[i] = (pt * dp_sc[i]).astype(BF16)
        for j in range(sub):
            blk = first + j
            rows = slice(BLK * j, BLK * (j + 1))
            nxt = _blk(jnp.minimum(blk + 1, nbs - 1))
            cols = pl.ds(pl.multiple_of(blk * BLK, BLK), 2 * BLK)
            for p in range(4):
                cs = slice(LANES * p, LANES * (p + 1))
                qcat = jnp.concatenate([q_ref[_blk(blk), cs], q_ref[nxt, cs]], axis=0)
                dcat = jnp.concatenate([do_ref[_blk(blk), cs], do_ref[nxt, cs]], axis=0)
                i = N_HEADS * j + 2 * p
                dk_ref[rows, cs] = (jnp.where(lo, _nn(ds_sc[i], qcat), _nn(ds_sc[i + 1], qcat))
                                    * (1.0 / LOG2E)).astype(BF16)
                dv_ref[rows, cs] = jnp.where(lo, _nn(pt_sc[i], dcat), _nn(pt_sc[i + 1], dcat)).astype(BF16)
                for e in (0, 1):
                    kt = kt_sc[4 * j + p, HEAD_DIM * e:HEAD_DIM * (e + 1), :]
                    dqt_sc[HEAD_DIM * (2 * p + e):HEAD_DIM * (2 * p + e + 1), cols] += _nn(kt, ds_sc[i + e])

        @pl.when(step == spb - 1)
        def _():
            for p in range(4):
                cs = slice(LANES * p, LANES * (p + 1))
                dq_ref[:, cs] = (dqt_sc[cs, 0:length].T * Q_SCALE).astype(BF16)

    wide = pltpu.VMEM((N_HEADS * sub, BLK, 2 * BLK), F32)
    half = pltpu.VMEM((N_HEADS * sub, BLK, 2 * BLK), BF16)
    return pl.pallas_call(
        body, name=f"dil_bwd_{d}", grid=(steps,),
        in_specs=[tile(WIDTH, 1), tile(WIDTH, 2), whole(WIDTH, 0), whole(WIDTH, 0), whole(LANES, 0), whole(LANES, 0)],
        out_specs=[whole(WIDTH, 0), tile(WIDTH, 0), tile(WIDTH, 0)],
        out_shape=[jax.ShapeDtypeStruct((t_all, WIDTH), BF16)] * 3,
        scratch_shapes=[wide, wide, half, half, pltpu.VMEM((4 * sub, LANES, BLK), BF16),
                        pltpu.VMEM((WIDTH, length + BLK), F32)],
        compiler_params=_params(),
    )(zb, zb, zb, do, lse, dl)


def _mix_out(oa, o3, l3, gn_a, gn_b, w_out, x, ada3, ln_g, ln_b, perms, seq):
    t_all = x.shape[0]
    tm = TOK_TM
    nts = seq // tm

    def body(oa_ref, o1_ref, o2_ref, o3_ref, l1_ref, l2_ref, l3_ref, ga_ref, gb_ref, w_ref, x_ref, ada_ref, g_ref,
             b_ref, p4_ref, p16_ref, pt4_ref, pt16_ref, ob_ref, lse_ref, lse4_ref, lse16_ref, mg_ref, mix_ref, xh_ref,
             rs_ref, h2_ref, h2t_ref):
        e, et = _head_mats()
        la = l1_ref[...]
        lb = _permute_f32(pt4_ref[...], _load_classes(l2_ref, 4))
        lc = _permute_f32(pt16_ref[...], _load_classes(l3_ref, 16))
        mx = jnp.maximum(jnp.maximum(la, lb), lc)
        ea, eb, ec = jnp.exp2(la - mx), jnp.exp2(lb - mx), jnp.exp2(lc - mx)
        tot = ea + eb + ec
        lse = mx + jnp.log(tot) * LOG2E
        lse_ref[...] = lse
        _store_classes(lse4_ref, _permute_f32(p4_ref[...], lse), 4)
        _store_classes(lse16_ref, _permute_f32(p16_ref[...], lse), 16)
        ob = (o1_ref[...].astype(F32) * _hexp(ea / tot, e)
              + _nn(pt4_ref[...], _load_classes(o2_ref, 4)) * _hexp(eb / tot, e)
              + _nn(pt16_ref[...], _load_classes(o3_ref, 16)) * _hexp(ec / tot, e))
        ob_ref[...] = ob

        def rms(o, gain):
            rr = lax.rsqrt(_hsum(o * o, et) * (1.0 / HEAD_DIM) + RMS_EPS)
            return o * _hexp(rr, e) * gain

        merged = jnp.concatenate([rms(oa_ref[...], ga_ref[...]), rms(ob, gb_ref[...])], axis=1).astype(BF16)
        mg_ref[...] = merged
        mix = _nn(merged, w_ref[...])
        mix_ref[...] = mix.astype(BF16)
        r1 = ALPHA * x_ref[...] + ada_ref[0, 2:3, :] * mix
        d = r1 - jnp.mean(r1, axis=1, keepdims=True)
        rstd = lax.rsqrt(jnp.mean(d * d, axis=1, keepdims=True) + LN_EPS)
        xh = d * rstd
        xh_ref[...] = xh
        rs_ref[...] = jnp.broadcast_to(rstd, (tm, LANES))
        x1 = xh * g_ref[...] + b_ref[...]
        h2 = x1 * (1.0 + ada_ref[0, 4:5, :]) + ada_ref[0, 3:4, :]
        h2_ref[...] = h2.astype(BF16)
        h2t_ref[0] = h2.T.astype(BF16)

    tok = lambda w: pl.BlockSpec((tm, w), lambda i: (i, 0))
    vec = lambda w: pl.BlockSpec((1, w), lambda i: (0, 0))
    whole = lambda a: pl.BlockSpec(a.shape, lambda i: (0, 0))
    classes = lambda a, d: a.reshape(t_all // seq * d, seq // d, a.shape[-1])
    return pl.pallas_call(
        body, name="mix_out", grid=(t_all // tm,),
        in_specs=[tok(WIDTH), tok(WIDTH), _class_spec(4, WIDTH, nts), _class_spec(16, WIDTH, nts), tok(LANES),
                  _class_spec(4, LANES, nts), _class_spec(16, LANES, nts), vec(WIDTH), vec(WIDTH), whole(w_out),
                  tok(D_MODEL), pl.BlockSpec((1, 6, D_MODEL), lambda i: (i // nts, 0, 0)), vec(D_MODEL), vec(D_MODEL)]
        + [whole(p) for p in perms],
        out_specs=[tok(WIDTH), tok(LANES), _class_spec(4, LANES, nts), _class_spec(16, LANES, nts), tok(D_MODEL),
                   tok(D_MODEL), tok(D_MODEL), tok(LANES), tok(D_MODEL), pl.BlockSpec((1, D_MODEL, tm), lambda i: (i // (FFN_TM // tm), 0, i % (FFN_TM // tm)))],
        out_shape=[jax.ShapeDtypeStruct((t_all, WIDTH), F32), jax.ShapeDtypeStruct((t_all, LANES), F32),
                   _class_shape(t_all, seq, 4, LANES, F32), _class_shape(t_all, seq, 16, LANES, F32),
                   jax.ShapeDtypeStruct((t_all, D_MODEL), BF16), jax.ShapeDtypeStruct((t_all, D_MODEL), BF16),
                   jax.ShapeDtypeStruct((t_all, D_MODEL), F32), jax.ShapeDtypeStruct((t_all, LANES), F32),
                   jax.ShapeDtypeStruct((t_all, D_MODEL), BF16),
                   jax.ShapeDtypeStruct((t_all // FFN_TM, D_MODEL, FFN_TM), BF16)],
        compiler_params=_params(),
    )(oa, o3[0], classes(o3[1], 4), classes(o3[2], 16), l3[0], classes(l3[1], 4), classes(l3[2], 16), gn_a, gn_b,
      w_out, x, ada3, ln_g, ln_b, *perms)


def _mix_out_bwd(dmix, w_out, oa, ob, gn_a, gn_b, perms, seq):
    t_all = dmix.shape[0]
    tm = TOK_TM
    nts = seq // tm

    def body(dm_ref, w_ref, oa_ref, ob_ref, ga_ref, gb_ref, p4_ref, p16_ref, doa_ref, dob_ref, dob4_ref, dob16_ref,
             dla_ref, dlb_ref, dlb4_ref, dlb16_ref, acc_ref):
        @pl.when(pl.program_id(0) == 0)
        def _():
            acc_ref[...] = jnp.zeros_like(acc_ref)
        e, et = _head_mats()
        dmg = _nt(dm_ref[...], w_ref[...])

        def group(o, dn, gain):
            rr = lax.rsqrt(_hsum(o * o, et) * (1.0 / HEAD_DIM) + RMS_EPS)
            re = _hexp(rr, e)
            dgain = jnp.sum(dn * o * re, axis=0, keepdims=True)
            dxn = dn * gain
            tt = _hsum(dxn * o, et) * (rr * rr * rr) * (1.0 / HEAD_DIM)
            do = re * dxn - o * _hexp(tt, e)
            return do, _hsum(do * o, et), dgain

        doa, dla, dga = group(oa_ref[...], dmg[:, :WIDTH], ga_ref[...])
        dob, dlb, dgb = group(ob_ref[...], dmg[:, WIDTH:], gb_ref[...])
        dob = dob.astype(BF16)
        doa_ref[...] = doa.astype(BF16)
        dob_ref[...] = dob
        _store_classes(dob4_ref, _nn(p4_ref[...], dob).astype(BF16), 4)
        _store_classes(dob16_ref, _nn(p16_ref[...], dob).astype(BF16), 16)
        dla_ref[...] = dla
        dlb_ref[...] = dlb
        _store_classes(dlb4_ref, _permute_f32(p4_ref[...], dlb), 4)
        _store_classes(dlb16_ref, _permute_f32(p16_ref[...], dlb), 16)
        acc_ref[0:1, :] += jnp.concatenate([dga, dgb], axis=1)

    tok = lambda w: pl.BlockSpec((tm, w), lambda i: (i, 0))
    vec = lambda w: pl.BlockSpec((1, w), lambda i: (0, 0))
    return pl.pallas_call(
        body, name="mix_out_bwd", grid=(t_all // tm,),
        in_specs=[tok(D_MODEL), pl.BlockSpec(w_out.shape, lambda i: (0, 0)), tok(WIDTH), tok(WIDTH), vec(WIDTH),
                  vec(WIDTH), pl.BlockSpec(perms[0].shape, lambda i: (0, 0)),
                  pl.BlockSpec(perms[1].shape, lambda i: (0, 0))],
        out_specs=[tok(WIDTH), tok(WIDTH), _class_spec(4, WIDTH, nts), _class_spec(16, WIDTH, nts), tok(LANES),
                   tok(LANES), _class_spec(4, LANES, nts), _class_spec(16, LANES, nts),
                   pl.BlockSpec((8, D_MODEL), lambda i: (0, 0))],
        out_shape=[jax.ShapeDtypeStruct((t_all, WIDTH), BF16), jax.ShapeDtypeStruct((t_all, WIDTH), BF16),
                   _class_shape(t_all, seq, 4, WIDTH, BF16), _class_shape(t_all, seq, 16, WIDTH, BF16),
                   jax.ShapeDtypeStruct((t_all, LANES), F32), jax.ShapeDtypeStruct((t_all, LANES), F32),
                   _class_shape(t_all, seq, 4, LANES, F32), _class_shape(t_all, seq, 16, LANES, F32),
                   jax.ShapeDtypeStruct((8, D_MODEL), F32)],
        compiler_params=_params(),
    )(dmix, w_out, oa, ob, gn_a, gn_b, perms[0], perms[1])


def _inproj_bwd(dqt, dka, dva, dil1, dil4, dil16, dfa16, pos, wqkv, wf16, freq, perms, dr1, x, ada3, seq):
    t_all = x.shape[0]
    tm = TOK_TM
    nts = seq // tm

    def body(dqt_ref, dka_ref, dva_ref, q1_ref, k1_ref, v1_ref, q4_ref, k4_ref, v4_ref, q16_ref, k16_ref, v16_ref,
             dfa_ref, pos_ref, w_ref, wf_ref, fr_ref, pt4_ref, pt16_ref, dr1_ref, x_ref, ada_ref, gx_ref, dz_ref,
             acc_ref):
        i = pl.program_id(0)

        @pl.when(i == 0)
        def _():
            acc_ref[...] = jnp.zeros_like(acc_ref)
        tabs = _rope_tabs(pos_ref, fr_ref, -1.0)
        dz_ref[:, :WIDTH] = dqt_ref[...].T.astype(BF16)
        dz_ref[:, WIDTH:2 * WIDTH] = dka_ref[...]
        dz_ref[:, 2 * WIDTH:3 * WIDTH] = dva_ref[...]
        for t, (n1, n4, n16) in enumerate(((q1_ref, q4_ref, q16_ref), (k1_ref, k4_ref, k16_ref),
                                           (v1_ref, v4_ref, v16_ref))):
            tot = (n1[...].astype(F32) + _nn(pt4_ref[...], _load_classes(n4, 4))
                   + _nn(pt16_ref[...], _load_classes(n16, 16)))
            if t < 2:
                tot = _rope(tot, tabs)
            dz_ref[:, (3 + t) * WIDTH:(4 + t) * WIDTH] = tot.astype(BF16)
        dh1 = _tn(dfa_ref[...], wf_ref[...])
        for n in range(6):
            cs = slice(n * WIDTH, (n + 1) * WIDTH)
            dh1 = dh1 + _nt(dz_ref[:, cs], w_ref[:, cs])
        xv = x_ref[...]
        gx_ref[...] = ALPHA * dr1_ref[...] + dh1 * (1.0 + ada_ref[0, 1:2, :])
        b = i // nts
        acc_ref[pl.ds(b, 1), :] += jnp.sum(dh1 * xv, axis=0, keepdims=True)
        acc_ref[pl.ds(8 + b, 1), :] += jnp.sum(dh1, axis=0, keepdims=True)

    tok = lambda w: pl.BlockSpec((tm, w), lambda i: (i, 0))
    whole = lambda a: pl.BlockSpec(a.shape, lambda i: (0, 0))
    classes = lambda a, d: a.reshape(t_all // seq * d, seq // d, a.shape[-1])
    return pl.pallas_call(
        body, name="inproj_bwd", grid=(t_all // tm,),
        in_specs=[pl.BlockSpec((WIDTH, tm), lambda i: (i // nts, i % nts)), tok(WIDTH), tok(WIDTH)]
        + [tok(WIDTH)] * 3 + [_class_spec(4, WIDTH, nts)] * 3 + [_class_spec(16, WIDTH, nts)] * 3
        + [pl.BlockSpec((16, tm), lambda i: (0, i)), tok(1), whole(wqkv), whole(wf16),
           pl.BlockSpec((1, LANES), lambda i: (0, 0)), whole(perms[2]), whole(perms[3]), tok(D_MODEL), tok(D_MODEL),
           pl.BlockSpec((1, 6, D_MODEL), lambda i: (i // nts, 0, 0))],
        out_specs=[tok(D_MODEL), tok(6 * WIDTH), pl.BlockSpec((16, D_MODEL), lambda i: (0, 0))],
        out_shape=[jax.ShapeDtypeStruct((t_all, D_MODEL), F32), jax.ShapeDtypeStruct((t_all, 6 * WIDTH), BF16),
                   jax.ShapeDtypeStruct((16, D_MODEL), F32)],
        compiler_params=_params(),
    )(dqt, dka, dva, *dil1, *[classes(a, 4) for a in dil4], *[classes(a, 16) for a in dil16], dfa16, pos, wqkv, wf16,
      freq, perms[2], perms[3], dr1, x, ada3)


FFN_TM = 1024
FFN_TN = 256
HALO = 8


FFN_CHUNK = 256


def _conv_params(cw_ref, cb_ref, n, tn):
    a = pl.ds(pl.multiple_of(n * tn, tn), tn)
    g = pl.ds(pl.multiple_of(D_FF + n * tn, tn), tn)
    return cw_ref[:, a], cw_ref[:, g], cb_ref[:, a], cb_ref[:, g]


def _conv(cat_ref, w_ref, b_ref, start, rows, halo=HALO):
    return (b_ref[...] + w_ref[0:1, :] * cat_ref[pl.ds(start + halo - 2, rows), :]
            + w_ref[1:2, :] * cat_ref[pl.ds(start + halo - 1, rows), :]
            + w_ref[2:3, :] * cat_ref[pl.ds(start + halo, rows), :])


def _ffn_up_gate(h2, w_up, conv_w, conv_b, seq):
    t_all = h2.shape[0]
    tm, tn = min(2 * FFN_TM, seq), FFN_TN
    nc = D_FF // tn
    nts = seq // tm
    pre = 16

    def body(h_ref, hp_ref, wua_ref, wug_ref, cw_ref, cb_ref, ua_ref, ug_ref, o_ref, ca_ref, cg_ref):
        first = (pl.program_id(1) % nts) == 0
        wa_ref, wg_ref, ba_ref, bg_ref = _conv_params(cw_ref, cb_ref, pl.program_id(0), tn)
        hcat = jnp.concatenate([hp_ref[...], h_ref[...]], axis=0)
        zero = jnp.zeros((pre, tn), F32)
        for w_ref, cat, u_ref in ((wua_ref, ca_ref, ua_ref), (wug_ref, cg_ref, ug_ref)):
            ub = _nn(hcat, w_ref[...]).astype(BF16)
            ue = ub.astype(F32)
            cat[0:pre, :] = jnp.where(first, zero, ue[0:pre])
            cat[pre:, :] = ue[pre:]
            u_ref[...] = ub[pre:]
        for c0 in range(0, tm, FFN_CHUNK):
            ya = _conv(ca_ref, wa_ref, ba_ref, c0, FFN_CHUNK, pre)
            yg = _conv(cg_ref, wg_ref, bg_ref, c0, FFN_CHUNK, pre)
            o_ref[c0:c0 + FFN_CHUNK, :] = (yg * jax.nn.sigmoid(yg) * ya).astype(BF16)

    wcol = lambda off: pl.BlockSpec((D_MODEL, tn), lambda n, t: (0, n + off))
    tile = pl.BlockSpec((tm, tn), lambda n, t: (t, n))
    return pl.pallas_call(
        body, name="ffn_up_gate", grid=(nc, t_all // tm),
        in_specs=[pl.BlockSpec((tm, D_MODEL), lambda n, t: (t, 0)),
                  pl.BlockSpec((pre, D_MODEL), lambda n, t: (jnp.maximum(t * (tm // pre) - 1, 0), 0)),
                  wcol(0), wcol(nc), pl.BlockSpec(conv_w.shape, lambda n, t: (0, 0)),
                  pl.BlockSpec(conv_b.shape, lambda n, t: (0, 0))],
        out_specs=[tile, tile, tile],
        out_shape=[jax.ShapeDtypeStruct((t_all, D_FF), BF16)] * 3,
        scratch_shapes=[pltpu.VMEM((tm + pre, tn), F32)] * 2, compiler_params=_params(),
    )(h2, h2, w_up, w_up, conv_w, conv_b)


def _ffn_gate_bwd(u_a, u_g, dfi, conv_w, conv_b, h2t, seq):
    t_all = u_a.shape[0]
    tm, tn = FFN_TM, FFN_TN
    nc = D_FF // tn
    nts = seq // tm

    def body(ua_ref, uap_ref, uan_ref, ug_ref, ugp_ref, ugn_ref, df_ref, dfn_ref, cw_ref, cb_ref, h_ref,
             dua_ref, dug_ref, acca_ref, accg_ref, dwa_ref, dwg_ref, ca_ref, cg_ref, ya_ref, yg_ref, dwa_sc, dwg_sc,
             out_sems):
        t = pl.program_id(0)
        n = pl.program_id(1)
        cols = pl.ds(pl.multiple_of(n * tn, tn), tn)
        first = (t % nts) == 0
        last = (t % nts) == nts - 1

        @pl.when((t == 0) & (n == 0))
        def _():
            acca_ref[...] = jnp.zeros_like(acca_ref)
            accg_ref[...] = jnp.zeros_like(accg_ref)
            dwa_sc[...] = jnp.zeros_like(dwa_sc)
            dwg_sc[...] = jnp.zeros_like(dwg_sc)
        wa_ref, wg_ref, ba_ref, bg_ref = _conv_params(cw_ref, cb_ref, n, tn)
        zero = jnp.zeros((HALO, tn), F32)
        for cat, cur, prv, nxt in ((ca_ref, ua_ref, uap_ref, uan_ref), (cg_ref, ug_ref, ugp_ref, ugn_ref)):
            cat[0:HALO, :] = jnp.where(first, zero, prv[...].astype(F32)[HALO:])
            cat[HALO:HALO + tm, :] = cur[...].astype(F32)
            cat[HALO + tm:, :] = nxt[...].astype(F32)[:HALO]
        ch = FFN_CHUNK
        sums = [[jnp.zeros((1, tn), F32) for _ in range(4)] for _ in range(2)]
        for ci, c0 in enumerate(range(0, tm, ch)):
            ya = _conv(ca_ref, wa_ref, ba_ref, c0, ch + HALO)
            yg = _conv(cg_ref, wg_ref, bg_ref, c0, ch + HALO)
            if c0 + ch < tm:
                beyond = df_ref[c0 + ch:c0 + ch + 16, :].astype(F32)[:HALO]
            else:
                beyond = jnp.where(last, 0.0, dfn_ref[...].astype(F32)[:HALO])
            dfe = jnp.concatenate([df_ref[c0:c0 + ch, :].astype(F32), beyond], axis=0)
            sg = jax.nn.sigmoid(yg)
            ya_ref[ci] = dfe * (yg * sg)
            yg_ref[ci] = dfe * ya * (sg * (1.0 + yg * (1.0 - sg)))
            for half, (dy, cat, w_ref, du_ref) in enumerate(((ya_ref, ca_ref, wa_ref, dua_ref),
                                                             (yg_ref, cg_ref, wg_ref, dug_ref))):
                d0 = dy[ci, 0:ch, :]
                du = (w_ref[2:3, :] * d0 + w_ref[1:2, :] * dy[ci, pl.ds(1, ch), :]
                      + w_ref[0:1, :] * dy[ci, pl.ds(2, ch), :])
                du_ref[c0:c0 + ch, :] = du.astype(BF16)
                for k in range(3):
                    sums[half][k] += jnp.sum(d0 * cat[pl.ds(c0 + HALO - 2 + k, ch), :], axis=0, keepdims=True)
                sums[half][3] += jnp.sum(d0, axis=0, keepdims=True)
        for half, acc in enumerate((acca_ref, accg_ref)):
            for k in range(4):
                acc[k:k + 1, cols] += sums[half][k]
        ht = h_ref[0]
        dwa_sc[:, cols] += _nn(ht, dua_ref[...])
        dwg_sc[:, cols] += _nn(ht, dug_ref[...])

        @pl.when((t == t_all // tm - 1) & (n == nc - 1))
        def _():
            copies = [pltpu.make_async_copy(dwa_sc, dwa_ref, out_sems.at[0]),
                      pltpu.make_async_copy(dwg_sc, dwg_ref, out_sems.at[1])]
            for cp in copies:
                cp.start()
            for cp in copies:
                cp.wait()

    nrow = t_all // 16
    cur = pl.BlockSpec((tm, tn), lambda t, n: (t, n))
    prev = pl.BlockSpec((16, tn), lambda t, n: (jnp.maximum(t * (tm // 16) - 1, 0), n))
    nxt = pl.BlockSpec((16, tn), lambda t, n: (jnp.minimum((t + 1) * (tm // 16), nrow - 1), n))
    acc = pl.BlockSpec((8, D_FF), lambda t, n: (0, 0))
    return pl.pallas_call(
        body, name="ffn_gate_bwd", grid=(t_all // tm, nc),
        in_specs=[cur, prev, nxt, cur, prev, nxt, cur, nxt, pl.BlockSpec(conv_w.shape, lambda t, n: (0, 0)),
                  pl.BlockSpec(conv_b.shape, lambda t, n: (0, 0)),
                  pl.BlockSpec((1, D_MODEL, tm), lambda t, n: (t, 0, 0))],
        out_specs=[cur, cur, acc, acc, ANY, ANY],
        out_shape=[jax.ShapeDtypeStruct((t_all, D_FF), BF16), jax.ShapeDtypeStruct((t_all, D_FF), BF16),
                   jax.ShapeDtypeStruct((8, D_FF), F32), jax.ShapeDtypeStruct((8, D_FF), F32),
                   jax.ShapeDtypeStruct((D_MODEL, D_FF), F32), jax.ShapeDtypeStruct((D_MODEL, D_FF), F32)],
        scratch_shapes=[pltpu.VMEM((tm + 2 * HALO, tn), F32)] * 2
        + [pltpu.VMEM((tm // FFN_CHUNK, FFN_CHUNK + HALO, tn), F32)] * 2
        + [pltpu.VMEM((D_MODEL, D_FF), F32)] * 2 + [pltpu.SemaphoreType.DMA((2,))],
        compiler_params=_params(),
    )(u_a, u_a, u_a, u_g, u_g, u_g, dfi, dfi, conv_w, conv_b, h2t)


def _ffn_down(ffn_in, w_down, xh1, ln1_g, ln1_b, ada3, ln2_g, ln2_b, target, seq):
    t_all = xh1.shape[0]
    tm = 512
    nts = seq // tm

    def body(f_ref, w_ref, xh_ref, g1_ref, b1_ref, ada_ref, g2_ref, b2_ref, tg_ref, dr2_ref, acc_ref):
        i = pl.program_id(0)

        @pl.when(i == 0)
        def _():
            acc_ref[...] = jnp.zeros_like(acc_ref)
        ffn = _nn(f_ref[...], w_ref[...])
        x1 = xh_ref[...] * g1_ref[...] + b1_ref[...]
        r2 = ALPHA * x1 + ada_ref[0, 5:6, :] * ffn
        d = r2 - jnp.mean(r2, axis=1, keepdims=True)
        rstd = lax.rsqrt(jnp.mean(d * d, axis=1, keepdims=True) + LN_EPS)
        xh2 = d * rstd
        diff = xh2 * g2_ref[...] + b2_ref[...] - tg_ref[...]
        dy = diff * (1.0 / D_MODEL)
        dr2 = _layer_norm_bwd(dy * g2_ref[...], xh2, rstd)
        dr2_ref[...] = dr2
        acc_ref[0:1, :] += jnp.sum(dy * xh2, axis=0, keepdims=True)
        acc_ref[1:2, :] += jnp.sum(dy, axis=0, keepdims=True)
        acc_ref[2:3, :] += jnp.sum(diff * diff, axis=0, keepdims=True) * (0.5 / D_MODEL)
        acc_ref[pl.ds(8 + i // nts, 1), :] += jnp.sum(dr2 * ffn, axis=0, keepdims=True)

    tok = lambda w: pl.BlockSpec((tm, w), lambda i: (i, 0))
    vec = pl.BlockSpec((1, D_MODEL), lambda i: (0, 0))
    return pl.pallas_call(
        body, name="ffn_down", grid=(t_all // tm,),
        in_specs=[tok(D_FF), pl.BlockSpec(w_down.shape, lambda i: (0, 0)), tok(D_MODEL), vec, vec,
                  pl.BlockSpec((1, 6, D_MODEL), lambda i: (i // nts, 0, 0)), vec, vec, tok(D_MODEL)],
        out_specs=[tok(D_MODEL), pl.BlockSpec((16, D_MODEL), lambda i: (0, 0))],
        out_shape=[jax.ShapeDtypeStruct((t_all, D_MODEL), F32), jax.ShapeDtypeStruct((16, D_MODEL), F32)],
        compiler_params=_params(),
    )(ffn_in, w_down, xh1, ln1_g, ln1_b, ada3, ln2_g, ln2_b, target)


def _ffn_down_bwd(dr2, ada3, w_down, seq):
    t_all = dr2.shape[0]
    tm = 512
    nts = seq // tm

    def body(d_ref, ada_ref, w_ref, dffn_ref, dfi_ref):
        dffn = (d_ref[...] * ada_ref[0, 5:6, :]).astype(BF16)
        dffn_ref[...] = dffn
        dfi_ref[...] = _nt(dffn, w_ref[...]).astype(BF16)

    tok = lambda w: pl.BlockSpec((tm, w), lambda i: (i, 0))
    return pl.pallas_call(
        body, name="ffn_down_bwd", grid=(t_all // tm,),
        in_specs=[tok(D_MODEL), pl.BlockSpec((1, 6, D_MODEL), lambda i: (i // nts, 0, 0)),
                  pl.BlockSpec(w_down.shape, lambda i: (0, 0))],
        out_specs=[tok(D_MODEL), tok(D_FF)],
        out_shape=[jax.ShapeDtypeStruct((t_all, D_MODEL), BF16), jax.ShapeDtypeStruct((t_all, D_FF), BF16)],
        compiler_params=_params(),
    )(dr2, ada3, w_down)


def _ffn_up_bwd(du_a, du_g, w_up, dr2, xh1, rs1, mix, ada3, ln1_g, ln1_b, seq):
    t_all = dr2.shape[0]
    tm = 512
    nts = seq // tm

    def body(da_ref, dg_ref, w_ref, dr2_ref, xh_ref, rs_ref, mix_ref, ada_ref, g_ref, b_ref, dr1_ref, dmix_ref,
             acc_ref):
        i = pl.program_id(0)

        @pl.when(i == 0)
        def _():
            acc_ref[...] = jnp.zeros_like(acc_ref)
        dh2 = _nt(da_ref[...], w_ref[:, :D_FF]) + _nt(dg_ref[...], w_ref[:, D_FF:])
        xh = xh_ref[...]
        x1 = xh * g_ref[...] + b_ref[...]
        dx1 = ALPHA * dr2_ref[...] + dh2 * (1.0 + ada_ref[0, 4:5, :])
        dr1 = _layer_norm_bwd(dx1 * g_ref[...], xh, rs_ref[:, 0:1])
        dr1_ref[...] = dr1
        dmix_ref[...] = (dr1 * ada_ref[0, 2:3, :]).astype(BF16)
        b = i // nts
        acc_ref[0:1, :] += jnp.sum(dx1 * xh, axis=0, keepdims=True)
        acc_ref[1:2, :] += jnp.sum(dx1, axis=0, keepdims=True)
        acc_ref[pl.ds(8 + b, 1), :] += jnp.sum(dh2 * x1, axis=0, keepdims=True)
        acc_ref[pl.ds(16 + b, 1), :] += jnp.sum(dh2, axis=0, keepdims=True)
        acc_ref[pl.ds(24 + b, 1), :] += jnp.sum(dr1 * mix_ref[...].astype(F32), axis=0, keepdims=True)

    tok = lambda w: pl.BlockSpec((tm, w), lambda i: (i, 0))
    vec = pl.BlockSpec((1, D_MODEL), lambda i: (0, 0))
    return pl.pallas_call(
        body, name="ffn_up_bwd", grid=(t_all // tm,),
        in_specs=[tok(D_FF), tok(D_FF), pl.BlockSpec(w_up.shape, lambda i: (0, 0)), tok(D_MODEL), tok(D_MODEL),
                  tok(LANES), tok(D_MODEL), pl.BlockSpec((1, 6, D_MODEL), lambda i: (i // nts, 0, 0)), vec, vec],
        out_specs=[tok(D_MODEL), tok(D_MODEL), pl.BlockSpec((32, D_MODEL), lambda i: (0, 0))],
        out_shape=[jax.ShapeDtypeStruct((t_all, D_MODEL), F32), jax.ShapeDtypeStruct((t_all, D_MODEL), BF16),
                   jax.ShapeDtypeStruct((32, D_MODEL), F32)],
        compiler_params=_params(),
    )(du_a, du_g, w_up, dr2, xh1, rs1, mix, ada3, ln1_g, ln1_b)


def _rows(a):
    return a[:, :N_HEADS].T


def _rope_freq():
    f = np.float32(ROPE_THETA) ** (-np.arange(0, ROPE_DIMS, 2, dtype=np.float32) / np.float32(ROPE_DIMS))
    return jnp.asarray(np.tile(f.astype(np.float32), LANES // (ROPE_DIMS // 2))[None, :])


def _local_step(x, positions, target, ada3, w_in, b_fgate, gn_a, gn_b, ln1_g, ln1_b, conv_b, ln2_g, ln2_b,
                late_shards):
    nbat, seq, _ = x.shape
    t_all = nbat * seq
    xf = x.reshape(t_all, D_MODEL)
    tg = target.reshape(t_all, D_MODEL)
    pos = positions.reshape(t_all, 1)
    freq = _rope_freq()

    wqkv = jnp.concatenate([w_in[:, :3 * WIDTH], w_in[:, 3 * WIDTH + N_HEADS:]], axis=1)
    wf16 = jnp.zeros((16, D_MODEL), BF16).at[:N_HEADS].set(w_in[:, 3 * WIDTH:3 * WIDTH + N_HEADS].T)
    bf = b_fgate.reshape(N_HEADS, 1)

    perms = [_perm_matrix(TOK_TM, d, tr) for tr in (False, True) for d in DILATIONS[1:]]
    h1, za, zb1, zb4, zb16, vt, fa_t = _inproj(xf, ada3, pos, wqkv, wf16, freq, perms, seq)
    zbs = [zb1, zb4.reshape(t_all, 3 * WIDTH), zb16.reshape(t_all, 3 * WIDTH)]
    f_row = _fgate_fwd(fa_t, bf, seq)
    f_col = jnp.zeros((t_all, LANES), F32).at[:, :N_HEADS].set(f_row.T * LOG2E)
    oa, lse_row_a, gathered = _fox_fwd(za, vt, f_col, seq, [late_shards[n] for n in LATE])
    w_out, w_up, conv_w, w_down = (_full_from_gathered(n, g) for n, g in zip(LATE, gathered))
    o3, l3 = zip(*[_dil_fwd(zb, seq, d) for zb, d in zip(zbs, DILATIONS)])
    ob, lse_b, lse_b4, lse_b16, merged, mix, xh1, rs1, h2, h2t = _mix_out(oa, o3, l3, gn_a, gn_b, w_out, xf, ada3, ln1_g,
                                                                      ln1_b, perms, seq)
    u_a, u_g, ffn_in = _ffn_up_gate(h2, w_up, conv_w, conv_b, seq)
    dr2, acc2 = _ffn_down(ffn_in, w_down, xh1, ln1_g, ln1_b, ada3, ln2_g, ln2_b, tg, seq)

    dffn, dfi = _ffn_down_bwd(dr2, ada3, w_down, seq)
    d_w_down = _matmul_tn(dffn, ffn_in, 512, 512, "dw_down").T
    du_a, du_g, acc_ca, acc_cg, dw_up_a, dw_up_g = _ffn_gate_bwd(u_a, u_g, dfi, conv_w, conv_b, h2t, seq)
    dr1, dmix, acc1 = _ffn_up_bwd(du_a, du_g, w_up, dr2, xh1, rs1, mix, ada3, ln1_g, ln1_b, seq)

    doa, dob, dob4, dob16, dl_a, dl_b, dl_b4, dl_b16, acc_gn = _mix_out_bwd(dmix, w_out, oa, ob, gn_a, gn_b, perms, seq)
    d_w_out = _matmul_tn(merged, dmix, 512, 512, "dw_out")
    late_grads = dict(w_out=d_w_out, w_up=(dw_up_a, dw_up_g), conv_w=jnp.concatenate([acc_ca[0:3], acc_cg[0:3]], axis=1),
                      w_down=d_w_down)
    dka, dva, df_k, dqt, df_q, late_parts = _fox_bwd(za, doa, f_col, lse_row_a, _rows(dl_a), seq,
                                                     [_payload(n, _dest_major(n, late_grads[n])) for n in LATE])
    dfa_t, dbf = _fgate_bwd(_rows(df_k) + df_q, fa_t, bf, seq)
    flat = lambda a: a.reshape(t_all, a.shape[-1])
    dil = []
    for zb, d, do, lse, dl in zip(zbs, DILATIONS, (dob, flat(dob4), flat(dob16)),
                                  (lse_b, flat(lse_b4), flat(lse_b16)), (dl_b, flat(dl_b4), flat(dl_b16))):
        dil.append(_dil_bwd(zb, do, lse, dl, seq, d))
    dfa16 = jnp.zeros((16, t_all), BF16).at[:N_HEADS].set(dfa_t.astype(BF16))
    grad_x, dz, acc0 = _inproj_bwd(dqt, dka, dva, dil[0], dil[1], dil[2], dfa16, pos, wqkv, wf16, freq, perms, dr1, xf,
                                   ada3, seq)
    d_wqkv = _matmul_tn(h1, dz, 512, 512, "dw_in")
    d_wf = _matmul_rows(dfa16, h1, 512, "dw_fgate")[:N_HEADS].T
    d_w_in = jnp.concatenate([d_wqkv[:, :3 * WIDTH], d_wf, d_wqkv[:, 3 * WIDTH:]], axis=1)

    dada = jnp.concatenate([acc0[8:8 + nbat], acc0[:nbat], acc1[24:24 + nbat], acc1[16:16 + nbat], acc1[8:8 + nbat],
                            acc2[8:8 + nbat]], axis=1)

    grads = dict(
        dada=dada, b_ada=jnp.sum(dada, axis=0, keepdims=True), w_in=d_w_in, b_fgate=dbf[:, 0][None, :],
        gn_a=acc_gn[0:1, :WIDTH], gn_b=acc_gn[0:1, WIDTH:], ln1_g=acc1[0:1], ln1_b=acc1[1:2],
        conv_b=jnp.concatenate([acc_ca[3:4], acc_cg[3:4]], axis=1), ln2_g=acc2[0:1], ln2_b=acc2[1:2])
    return acc2[2:3], grad_x.reshape(x.shape), grads, dict(zip(LATE, late_parts))


LATE = ("w_out", "w_up", "conv_w", "w_down")
BIG = ("w_ada", "w_in") + LATE
COLUMN_SHARDED = ("w_ada", "w_in", "w_up", "conv_w")


def _payload(name, a):
    return a if name == "conv_w" else a.astype(BF16)
SMALL = ("b_ada", "b_fgate", "gn_a", "gn_b", "ln1_g", "ln1_b", "conv_b", "ln2_g", "ln2_b")
ADAM_ROWS = dict(w_ada=256, w_in=256, w_out=128, w_up=256, conv_w=3, w_down=176)
SMALL_ROWS = 24


def _full_from_gathered(name, g):
    if name in COLUMN_SHARDED:
        return g.transpose(1, 0, 2).reshape(g.shape[1], N_DEV * g.shape[2])
    return g.reshape(N_DEV * g.shape[1], g.shape[2])


def _dest_major(name, full):
    if name in COLUMN_SHARDED:
        parts = full if isinstance(full, tuple) else (full,)
        per = N_DEV // len(parts)
        return jnp.concatenate([p.reshape(p.shape[0], per, p.shape[1] // per).transpose(1, 0, 2) for p in parts], axis=0)
    return full.reshape(N_DEV, full.shape[0] // N_DEV, full.shape[1])


def _pack_small(vals, extra=None):
    parts = [vals[n].reshape(-1) for n in SMALL]
    if extra is not None:
        parts.append(extra.reshape(-1))
    flat = jnp.concatenate(parts)
    return jnp.pad(flat, (0, SMALL_ROWS * D_MODEL - flat.shape[0])).reshape(SMALL_ROWS, D_MODEL)


def _unpack_small(packed, like):
    flat = packed.reshape(-1)
    out, off = {}, 0
    for n in SMALL:
        size = like[n].size
        out[n] = flat[off:off + size].reshape(like[n].shape)
        off += size
    return out, flat[off:off + D_MODEL]


def kernel(x, c, positions, w_ada, b_ada, w_in, b_fgate, gn_a, gn_b, w_out, ln1_g, ln1_b, w_up, conv_w, conv_b, w_down, ln2_g, ln2_b, loss_target, m_w_ada, m_b_ada, m_w_in, m_b_fgate, m_gn_a, m_gn_b, m_w_out, m_ln1_g, m_ln1_b, m_w_up, m_conv_w, m_conv_b, m_w_down, m_ln2_g, m_ln2_b, v_w_ada, v_b_ada, v_w_in, v_b_fgate, v_gn_a, v_gn_b, v_w_out, v_ln1_g, v_ln1_b, v_w_up, v_conv_w, v_conv_b, v_w_down, v_ln2_g, v_ln2_b):
    w = dict(w_ada=w_ada[0], b_ada=b_ada, w_in=w_in[0], b_fgate=b_fgate, gn_a=gn_a, gn_b=gn_b, w_out=w_out[0],
             ln1_g=ln1_g, ln1_b=ln1_b, w_up=w_up[0], conv_w=conv_w[0], conv_b=conv_b, w_down=w_down[0], ln2_g=ln2_g,
             ln2_b=ln2_b)
    m = dict(w_ada=m_w_ada[0], b_ada=m_b_ada, w_in=m_w_in[0], b_fgate=m_b_fgate, gn_a=m_gn_a, gn_b=m_gn_b,
             w_out=m_w_out[0], ln1_g=m_ln1_g, ln1_b=m_ln1_b, w_up=m_w_up[0], conv_w=m_conv_w[0], conv_b=m_conv_b,
             w_down=m_w_down[0], ln2_g=m_ln2_g, ln2_b=m_ln2_b)
    v = dict(w_ada=v_w_ada[0], b_ada=v_b_ada, w_in=v_w_in[0], b_fgate=v_b_fgate, gn_a=v_gn_a, gn_b=v_gn_b,
             w_out=v_w_out[0], ln1_g=v_ln1_g, ln1_b=v_ln1_b, w_up=v_w_up[0], conv_w=v_conv_w[0], conv_b=v_conv_b,
             w_down=v_w_down[0], ln2_g=v_ln2_g, ln2_b=v_ln2_b)

    nbat = x.shape[0]
    me = 4 * lax.axis_index("x") + 2 * lax.axis_index("y") + lax.axis_index("c")
    ada_cols = w["w_ada"].shape[1]

    c_all, w_in_all = _gather_two_level([c, _payload("w_in", w["w_in"])], "weight_gather")
    c_all = c_all.reshape(N_DEV * nbat, D_MODEL)
    ada_mine = _ada_fwd(c_all, w["w_ada"], lax.dynamic_slice(b_ada, (0, me * ada_cols), (1, ada_cols)))
    (ada_parts,) = _exchange([ada_mine.reshape(N_DEV, nbat, ada_cols)], [False], "ada_exchange")
    ada3 = ada_parts.transpose(1, 0, 2).reshape(nbat, 6, D_MODEL)

    loss_lanes, grad_x, g_local, parts = _local_step(
        x, positions, loss_target, ada3, _full_from_gathered("w_in", w_in_all), b_fgate, gn_a, gn_b, ln1_g, ln1_b,
        conv_b, ln2_g, ln2_b, {n: _payload(n, w[n]) for n in LATE})

    parts["w_in"], dada_all, small_all = _exchange(
        [_payload("w_in", _dest_major("w_in", g_local["w_in"])), g_local["dada"], _pack_small(g_local, loss_lanes)],
        [False, True, True], "grad_exchange")
    dada_cols = lax.dynamic_slice(dada_all.reshape(N_DEV * nbat, 6 * D_MODEL), (0, me * ada_cols),
                                  (N_DEV * nbat, ada_cols))
    parts["w_ada"] = _ada_bwd(c_all, dada_cols)[None]

    grad, delta, new_m, new_v = {}, {}, {}, {}
    for n in BIG:
        grad[n], delta[n], new_m[n], new_v[n] = (
            a[None] for a in _adamw(parts[n], w[n], m[n], v[n], ADAM_ROWS[n], "adamw_" + n))
    packed = _adamw(small_all, _pack_small(w), _pack_small(m), _pack_small(v), SMALL_ROWS, "adamw_small")
    for dst, pk in zip((grad, delta, new_m, new_v), packed):
        vals, lanes = _unpack_small(pk, w)
        dst.update(vals)
        if dst is grad:
            loss = jnp.sum(lanes)

    order = ("w_ada", "b_ada", "w_in", "b_fgate", "gn_a", "gn_b", "w_out", "ln1_g", "ln1_b", "w_up", "conv_w", "conv_b",
             "w_down", "ln2_g", "ln2_b")
    return (loss, grad_x, *[grad[n] for n in order], *[delta[n] for n in order], *[new_m[n] for n in order],
            *[new_v[n] for n in order])
```

```python
import functools

import numpy as np
import jax
import jax.numpy as jnp
from jax import lax
from jax.experimental import pallas as pl
from jax.experimental.pallas import tpu as pltpu

F32, BF16 = jnp.float32, jnp.bfloat16
MESH = pl.DeviceIdType.MESH
ANY = pl.BlockSpec(memory_space=pl.ANY)

D_MODEL = 1024
N_HEADS = 8
HEAD_DIM = 64
WIDTH = 512
D_FF = 2816
N_DEV = 8
ROPE_DIMS = 16
ROPE_THETA = 500000.0
ALPHA = 2.0 ** 0.25
LN_EPS = 1e-5
RMS_EPS = 1e-6
NEG = -1e30
Q_SCALE = 0.125
LOG2E = 1.4426950408889634
BLK = 128
LANES = 128
VMEM_LIMIT_BYTES = 56 * 1024 * 1024

ADAM_LR, ADAM_B1, ADAM_B2, ADAM_EPS, ADAM_WD, ADAM_STEP = 0.001, 0.9, 0.999, 1e-08, 0.01, 10


def _params(vmem=VMEM_LIMIT_BYTES):
    return pltpu.CompilerParams(vmem_limit_bytes=vmem)


def _nn(a, b):
    return jnp.dot(a, b, preferred_element_type=F32)


def _nt(a, b):
    return lax.dot_general(a, b, (((1,), (1,)), ((), ())), preferred_element_type=F32)


def _tn(a, b):
    return lax.dot_general(a, b, (((0,), (0,)), ((), ())), preferred_element_type=F32)


def _head_mats():
    r = lax.broadcasted_iota(jnp.int32, (LANES, WIDTH), 0)
    c = lax.broadcasted_iota(jnp.int32, (LANES, WIDTH), 1)
    e = ((c >> 6) == r).astype(BF16)
    r2 = lax.broadcasted_iota(jnp.int32, (WIDTH, LANES), 0)
    c2 = lax.broadcasted_iota(jnp.int32, (WIDTH, LANES), 1)
    et = ((r2 >> 6) == c2).astype(BF16)
    return e, et


def _split3(x):
    hi = x.astype(BF16)
    r = x - hi.astype(F32)
    mid = r.astype(BF16)
    return hi, mid, (r - mid.astype(F32)).astype(BF16)


def _hexp(w, e):
    return sum(_nn(part, e) for part in _split3(w)[:2])


def _hsum(x, et):
    return sum(_nn(part, et) for part in _split3(x)[:2])


def _perm_matrix(rows, d, transpose):
    i = np.arange(rows)
    j = (i % (rows // d)) * d + i // (rows // d)
    p = np.zeros((rows, rows), np.float32)
    p[i, j] = 1.0
    return jnp.asarray(p.T if transpose else p, BF16)


def _permute_f32(p, x):
    return sum(_nn(p, part) for part in _split3(x))


def _store_classes(ref, y, d):
    n = y.shape[0] // d
    for r in range(d):
        ref[r] = y[r * n:(r + 1) * n, :]


def _load_classes(ref, d):
    return jnp.concatenate([ref[r] for r in range(d)], axis=0)


def _rope_tabs(pos_ref, fr_ref, sign):
    ang = pos_ref[...].astype(F32) * fr_ref[...]
    lane = lax.broadcasted_iota(jnp.int32, ang.shape, 1) & (HEAD_DIM - 1)
    m1 = lane < ROPE_DIMS // 2
    m2 = (lane >= ROPE_DIMS // 2) & (lane < ROPE_DIMS)
    cos = jnp.cos(ang)
    sin = jnp.sin(ang) * sign
    return (jnp.where(m1 | m2, cos, 1.0), jnp.where(m1, -sin, 0.0), jnp.where(m2, sin, 0.0))


def _rope(z, tabs):
    c, s1, s2 = tabs
    parts = []
    for p in range(z.shape[1] // LANES):
        zp = z[:, LANES * p:LANES * (p + 1)]
        parts.append(zp * c + pltpu.roll(zp, LANES - 8, 1) * s1 + pltpu.roll(zp, 8, 1) * s2)
    return jnp.concatenate(parts, axis=1)


def _half_masks(rows):
    lane = lax.broadcasted_iota(jnp.int32, (rows, LANES), 1)
    lo = lane < HEAD_DIM
    return lo, jnp.logical_not(lo)


def _layer_norm_bwd(dxh, xh, rstd):
    m1 = jnp.mean(dxh, axis=1, keepdims=True)
    m2 = jnp.mean(dxh * xh, axis=1, keepdims=True)
    return rstd * (dxh - m1 - xh * m2)


def _coords():
    return lax.axis_index("x"), lax.axis_index("y"), lax.axis_index("c")


def _peer(x, y, c, k):
    return (1 - x if k & 4 else x, 1 - y if k & 2 else y, 1 - c if k & 1 else c)


def _comm_sems(n):
    return [pltpu.SemaphoreType.DMA((N_DEV - 1, n)), pltpu.SemaphoreType.DMA((N_DEV - 1, n)),
            pltpu.SemaphoreType.DMA((n,))]


def _comm_copies(ins, outs, to_all, sems):
    send_sems, recv_sems, local_sems = sems
    x, y, c = _coords()
    me = 4 * x + 2 * y + c
    copies = [pltpu.make_async_copy(ins[t] if to_all[t] else ins[t].at[me], outs[t].at[me], local_sems.at[t])
              for t in range(len(ins))]
    for k in range(1, N_DEV):
        px, py, pc = _peer(x, y, c, k)
        dest = 4 * px + 2 * py + pc
        for t in range(len(ins)):
            copies.append(pltpu.make_async_remote_copy(
                src_ref=ins[t] if to_all[t] else ins[t].at[dest], dst_ref=outs[t].at[me],
                send_sem=send_sems.at[k - 1, t], recv_sem=recv_sems.at[k - 1, t],
                device_id=(px, py, pc), device_id_type=MESH))
    return copies


def _comm_out_shapes(ins, to_all):
    return [jax.ShapeDtypeStruct(((N_DEV,) + a.shape) if ta else a.shape, a.dtype) for a, ta in zip(ins, to_all)]


def _exchange(ins, to_all, name):
    n = len(ins)

    def body(*refs):
        copies = _comm_copies(refs[:n], refs[n:2 * n], to_all, refs[2 * n:])
        for cp in copies:
            cp.start()
        for cp in copies:
            cp.wait()

    return pl.pallas_call(
        body, name=name, out_shape=_comm_out_shapes(ins, to_all), in_specs=[ANY] * n, out_specs=[ANY] * n,
        scratch_shapes=_comm_sems(n),
    )(*ins)


def _gather_two_level(ins, name):
    n = len(ins)

    def body(*refs):
        srcs, outs = refs[:n], refs[n:2 * n]
        send_sems, recv_sems, local_sems = refs[2 * n:]
        x, y, c = _coords()
        me = 4 * x + 2 * y + c
        sibling = (x, y, 1 - c)
        chips = [(1 - x, y), (x, 1 - y), (1 - x, 1 - y)]
        slot = lambda px, py, pc: 4 * px + 2 * py + pc

        def copy(k, t, block, to, own=False):
            return pltpu.make_async_remote_copy(
                src_ref=srcs[t] if own else outs[t].at[block], dst_ref=outs[t].at[block],
                send_sem=send_sems.at[k, t], recv_sem=recv_sems.at[k, t], device_id=to, device_id_type=MESH)

        local = [pltpu.make_async_copy(srcs[t], outs[t].at[me], local_sems.at[t]) for t in range(n)]
        first = [copy(0, t, me, sibling, own=True) for t in range(n)]
        first += [copy(1 + j, t, me, (*chip, c), own=True) for j, chip in enumerate(chips) for t in range(n)]
        for cp in local + first:
            cp.start()
        passed = []
        for j, chip in enumerate(chips):
            for t in range(n):
                copy(1 + j, t, slot(*chip, c), (x, y, c)).wait_recv()
                cp = copy(4 + j, t, slot(*chip, c), sibling)
                cp.start()
                passed.append(cp)
        for t in range(n):
            copy(0, t, slot(x, y, 1 - c), (x, y, c)).wait_recv()
            for j, chip in enumerate(chips):
                copy(4 + j, t, slot(*chip, 1 - c), (x, y, c)).wait_recv()
        for cp in first + passed:
            cp.wait_send()
        for cp in local:
            cp.wait()

    return pl.pallas_call(
        body, name=name, out_shape=_comm_out_shapes(ins, [True] * n), in_specs=[ANY] * n, out_specs=[ANY] * n,
        scratch_shapes=_comm_sems(n),
    )(*ins)


def _adamw(parts, w, m, v, rows, name):
    n_parts, r_all, cols = parts.shape
    c1 = 1.0 - ADAM_B1 ** ADAM_STEP
    c2 = 1.0 - ADAM_B2 ** ADAM_STEP

    def body(p_ref, w_ref, m_ref, v_ref, g_ref, d_ref, mo_ref, vo_ref):
        g = p_ref[0].astype(F32)
        for s in range(1, n_parts):
            g = g + p_ref[s].astype(F32)
        mn = ADAM_B1 * m_ref[...] + (1.0 - ADAM_B1) * g
        vn = ADAM_B2 * v_ref[...] + (1.0 - ADAM_B2) * (g * g)
        m_hat = mn / c1
        v_hat = vn / c2
        g_ref[...] = g
        d_ref[...] = -ADAM_LR * (m_hat / (jnp.sqrt(v_hat) + ADAM_EPS) + ADAM_WD * w_ref[...])
        mo_ref[...] = mn
        vo_ref[...] = vn

    spec = pl.BlockSpec((rows, cols), lambda i: (i, 0))
    return pl.pallas_call(
        body, name=name, grid=(r_all // rows,),
        in_specs=[pl.BlockSpec((n_parts, rows, cols), lambda i: (0, i, 0)), spec, spec, spec],
        out_specs=[spec] * 4, out_shape=[jax.ShapeDtypeStruct((r_all, cols), F32)] * 4,
        compiler_params=_params(),
    )(parts, w, m, v)


def _matmul_tn(a, b, chunk, tk, name):
    t_all, k1 = a.shape
    n = b.shape[1]

    def body(a_ref, b_ref, o_ref):
        @pl.when(pl.program_id(0) == 0)
        def _():
            o_ref[...] = jnp.zeros_like(o_ref)
        at = a_ref[...].astype(F32).T.astype(BF16)
        for j in range(0, n, chunk):
            cs = slice(j, min(j + chunk, n))
            o_ref[:, cs] += _nn(at, b_ref[:, cs])

    return pl.pallas_call(
        body, name=name, grid=(t_all // tk,),
        in_specs=[pl.BlockSpec((tk, k1), lambda t: (t, 0)), pl.BlockSpec((tk, n), lambda t: (t, 0))],
        out_specs=pl.BlockSpec((k1, n), lambda t: (0, 0)),
        out_shape=jax.ShapeDtypeStruct((k1, n), F32), compiler_params=_params(),
    )(a, b)


def _matmul_rows(a, b, tk, name):
    r, t_all = a.shape
    n = b.shape[1]

    def body(a_ref, b_ref, o_ref):
        @pl.when(pl.program_id(0) == 0)
        def _():
            o_ref[...] = jnp.zeros_like(o_ref)
        o_ref[...] += _nn(a_ref[...], b_ref[...])

    return pl.pallas_call(
        body, name=name, grid=(t_all // tk,),
        in_specs=[pl.BlockSpec((r, tk), lambda t: (0, t)), pl.BlockSpec((tk, n), lambda t: (t, 0))],
        out_specs=pl.BlockSpec((r, n), lambda t: (0, 0)),
        out_shape=jax.ShapeDtypeStruct((r, n), F32), compiler_params=_params(),
    )(a, b)


def _ada_fwd(c_all, w_ada, b_ada):
    whole = lambda a: pl.BlockSpec(a.shape, lambda j: (0, 0))

    def body(c_ref, w_ref, b_ref, o_ref):
        cv = c_ref[...]
        s = (cv * jax.nn.sigmoid(cv)).astype(BF16)
        o_ref[...] = _nn(s, w_ref[...].astype(BF16)) + b_ref[...]

    out = jax.ShapeDtypeStruct((c_all.shape[0], w_ada.shape[1]), F32)
    return pl.pallas_call(
        body, name="ada_fwd", grid=(1,), in_specs=[whole(c_all), whole(w_ada), whole(b_ada)], out_specs=whole(out),
        out_shape=out, compiler_params=_params(),
    )(c_all, w_ada, b_ada)


def _ada_bwd(c_all, dada):
    whole = lambda a: pl.BlockSpec(a.shape, lambda j: (0, 0))

    def body(c_ref, d_ref, o_ref):
        cv = c_ref[...]
        s = (cv * jax.nn.sigmoid(cv)).astype(BF16)
        o_ref[...] = _tn(s, d_ref[...].astype(BF16))

    out = jax.ShapeDtypeStruct((D_MODEL, dada.shape[1]), F32)
    return pl.pallas_call(
        body, name="ada_bwd", grid=(1,), in_specs=[whole(c_all), whole(dada)], out_specs=whole(out), out_shape=out,
        compiler_params=_params(),
    )(c_all, dada)


TOK_TM = 256
DILATIONS = (1, 4, 16)


def _class_spec(d, width, nts):
    return pl.BlockSpec((d, TOK_TM // d, width), lambda i: (i // nts, i % nts, 0))


def _class_shape(t_all, seq, d, width, dtype):
    return jax.ShapeDtypeStruct((t_all // seq * d, seq // d, width), dtype)


def _inproj(x, ada3, pos, wqkv, wf16, freq, perms, seq):
    t_all = x.shape[0]
    tm = TOK_TM
    nts = seq // tm

    def body(x_ref, ada_ref, pos_ref, w_ref, wf_ref, fr_ref, p4_ref, p16_ref, h1_ref, za_ref, zb_ref, zb4_ref,
             zb16_ref, vt_ref, fa_ref):
        h1 = (x_ref[...] * (1.0 + ada_ref[0, 1:2, :]) + ada_ref[0, 0:1, :]).astype(BF16)
        h1_ref[...] = h1
        tabs = _rope_tabs(pos_ref, fr_ref, 1.0)
        for n in range(6):
            z = _nn(h1, w_ref[:, n * WIDTH:(n + 1) * WIDTH])
            if n in (3, 4):
                z = _rope(z, tabs)
            if n in (0, 3):
                z = z * (Q_SCALE * LOG2E)
            if n == 2:
                vt_ref[...] = z.T.astype(BF16)
            dst = za_ref if n < 3 else zb_ref
            dst[:, (n % 3) * WIDTH:(n % 3 + 1) * WIDTH] = z.astype(BF16)
        fa_ref[...] = _nt(wf_ref[...], h1)[:N_HEADS]
        zb = zb_ref[...]
        _store_classes(zb4_ref, _nn(p4_ref[...], zb).astype(BF16), 4)
        _store_classes(zb16_ref, _nn(p16_ref[...], zb).astype(BF16), 16)

    tok = lambda w: pl.BlockSpec((tm, w), lambda i: (i, 0))
    whole = lambda a: pl.BlockSpec(a.shape, lambda i: (0, 0))
    return pl.pallas_call(
        body, name="inproj", grid=(t_all // tm,),
        in_specs=[tok(D_MODEL), pl.BlockSpec((1, 6, D_MODEL), lambda i: (i // nts, 0, 0)), tok(1), whole(wqkv),
                  whole(wf16), pl.BlockSpec((1, LANES), lambda i: (0, 0)), whole(perms[0]), whole(perms[1])],
        out_specs=[tok(D_MODEL), tok(3 * WIDTH), tok(3 * WIDTH), _class_spec(4, 3 * WIDTH, nts),
                   _class_spec(16, 3 * WIDTH, nts), pl.BlockSpec((WIDTH, tm), lambda i: (i // nts, i % nts)),
                   pl.BlockSpec((N_HEADS, tm), lambda i: (0, i))],
        out_shape=[jax.ShapeDtypeStruct((t_all, D_MODEL), BF16), jax.ShapeDtypeStruct((t_all, 3 * WIDTH), BF16),
                   jax.ShapeDtypeStruct((t_all, 3 * WIDTH), BF16), _class_shape(t_all, seq, 4, 3 * WIDTH, BF16),
                   _class_shape(t_all, seq, 16, 3 * WIDTH, BF16),
                   jax.ShapeDtypeStruct((t_all // seq * WIDTH, seq), BF16),
                   jax.ShapeDtypeStruct((N_HEADS, t_all), F32)],
        compiler_params=_params(),
    )(x, ada3, pos, wqkv, wf16, freq, perms[0], perms[1])


def _chunk_rows(a_t, seq):
    t_all = a_t.shape[1]
    return a_t.reshape(N_HEADS, t_all // seq, seq // LANES, LANES).transpose(1, 0, 2, 3).reshape(-1, LANES)


def _unchunk_rows(a, seq):
    nbat = a.shape[0] * LANES // (N_HEADS * seq)
    return a.reshape(nbat, N_HEADS, seq // LANES, LANES).transpose(1, 0, 2, 3).reshape(N_HEADS, nbat * seq)


def _chunk_carry(tot, nchunk, later):
    rows = tot.shape[0]
    r = lax.broadcasted_iota(jnp.int32, (rows, rows), 0)
    c = lax.broadcasted_iota(jnp.int32, (rows, rows), 1)
    sel = ((r // nchunk) == (c // nchunk)) & ((c > r) if later else (c < r))
    mat = sel.astype(BF16)
    return sum(_nn(mat, part) for part in _split3(jnp.broadcast_to(tot, (rows, LANES))))


def _fgate_fwd(fa_t, bf, seq):
    x = _chunk_rows(fa_t, seq)
    rows = x.shape[0]
    nchunk = seq // LANES
    bias = jnp.broadcast_to(bf.reshape(1, N_HEADS, 1), (rows // (N_HEADS * nchunk), N_HEADS, nchunk)).reshape(rows, 1)

    def body(x_ref, b_ref, f_ref):
        lane = lax.broadcasted_iota(jnp.int32, (rows, LANES), 1)
        xv = x_ref[...] + b_ref[...]
        lf = jnp.minimum(xv, 0.0) - jnp.log(1.0 + jnp.exp(-jnp.abs(xv)))
        for s in (1, 2, 4, 8, 16, 32, 64):
            lf = lf + jnp.where(lane >= s, pltpu.roll(lf, s, 1), 0.0)
        f_ref[...] = lf + _chunk_carry(lf[:, LANES - 1:LANES], nchunk, False)

    whole = lambda a: pl.BlockSpec(a.shape, lambda i: (0, 0))
    out = pl.pallas_call(
        body, name="fgate_fwd", grid=(1,), in_specs=[whole(x), whole(bias)], out_specs=whole(x),
        out_shape=jax.ShapeDtypeStruct(x.shape, F32), compiler_params=_params(),
    )(x, bias)
    return _unchunk_rows(out, seq)


def _fgate_bwd(df_t, fa_t, bf, seq):
    d_in = _chunk_rows(df_t, seq)
    x = _chunk_rows(fa_t, seq)
    rows = x.shape[0]
    nchunk = seq // LANES
    bias = jnp.broadcast_to(bf.reshape(1, N_HEADS, 1), (rows // (N_HEADS * nchunk), N_HEADS, nchunk)).reshape(rows, 1)

    def body(d_ref, x_ref, b_ref, o_ref, s_ref):
        lane = lax.broadcasted_iota(jnp.int32, (rows, LANES), 1)
        d = d_ref[...]
        for s in (1, 2, 4, 8, 16, 32, 64):
            d = d + jnp.where(lane < LANES - s, pltpu.roll(d, LANES - s, 1), 0.0)
        d = d + _chunk_carry(d[:, 0:1], nchunk, True)
        dfa = d * jax.nn.sigmoid(-(x_ref[...] + b_ref[...]))
        o_ref[...] = dfa
        g = lax.broadcasted_iota(jnp.int32, (2 * N_HEADS, rows), 0)
        r = lax.broadcasted_iota(jnp.int32, (2 * N_HEADS, rows), 1)
        group = (((r // nchunk) % N_HEADS) == g).astype(BF16)
        per_head = sum(_nn(group, part) for part in _split3(dfa))[:N_HEADS]
        s_ref[...] = jnp.broadcast_to(jnp.sum(per_head, axis=1, keepdims=True), (N_HEADS, LANES))

    whole = lambda a: pl.BlockSpec(a.shape, lambda i: (0, 0))
    dfa, sums = pl.pallas_call(
        body, name="fgate_bwd", grid=(1,), in_specs=[whole(d_in), whole(x), whole(bias)],
        out_specs=[whole(x), pl.BlockSpec((N_HEADS, LANES), lambda i: (0, 0))],
        out_shape=[jax.ShapeDtypeStruct(x.shape, F32), jax.ShapeDtypeStruct((N_HEADS, LANES), F32)],
        compiler_params=_params(),
    )(d_in, x, bias)
    return _unchunk_rows(dfa, seq), sums


FOX_T = 256


def _fox_prep(dst, src_ref, lo, hi):
    for p in range(4):
        v = src_ref[:, LANES * p:LANES * (p + 1)]
        dst[2 * p] = jnp.where(lo, v, jnp.zeros_like(v))
        dst[2 * p + 1] = jnp.where(hi, v, jnp.zeros_like(v))


def _fox_fwd(za, vt, f_col, seq, shards):
    t_all = za.shape[0]
    tq = FOX_T
    nq = seq // tq
    nbat = t_all // seq
    n = len(shards)
    to_all = [True] * n

    def body(*refs):
        q_ref, k_ref, vt_ref, fc_ref = refs[:4]
        o_ref, lse_ref = refs[4 + n:6 + n]
        qm_sc, m_sc, l_sc, acc_sc, a_sc, st_sc, pe_sc = refs[6 + 2 * n:13 + 2 * n]
        comm = (refs[4:4 + n], refs[6 + n:6 + 2 * n], to_all, refs[13 + 2 * n:])
        i = pl.program_id(1)

        @pl.when((pl.program_id(0) == 0) & (i == 0))
        def _():
            for cp in _comm_copies(*comm):
                cp.start()
        lo, hi = _half_masks(tq)
        r = lax.broadcasted_iota(jnp.int32, (tq, tq), 0)
        c = lax.broadcasted_iota(jnp.int32, (tq, tq), 1)
        tri = c >= r
        _fox_prep(qm_sc, q_ref, lo, hi)
        m_sc[...] = jnp.full(m_sc.shape, NEG, F32)
        l_sc[...] = jnp.zeros_like(l_sc)
        acc_sc[...] = jnp.zeros_like(acc_sc)

        def block(j, masked):
            sl = pl.ds(pl.multiple_of(j * tq, tq), tq)
            for p in range(4):
                kj = k_ref[sl, LANES * p:LANES * (p + 1)]
                for h in (2 * p, 2 * p + 1):
                    st = _nt(kj, qm_sc[h]) - fc_ref[sl, h:h + 1]
                    st_sc[h] = jnp.where(tri, st, NEG) if masked else st
            for h in range(N_HEADS):
                st = st_sc[h]
                m = m_sc[h:h + 1, :]
                mn = jnp.maximum(m, jnp.max(st, axis=0, keepdims=True))
                a = jnp.exp2(m - mn)
                pe = jnp.exp2(st - mn)
                m_sc[h:h + 1, :] = mn
                a_sc[h:h + 1, :] = a
                l_sc[h:h + 1, :] = a * l_sc[h:h + 1, :] + jnp.sum(pe, axis=0, keepdims=True)
                pe_sc[h] = pe.astype(BF16)
            for h in range(N_HEADS):
                acc_sc[h] = a_sc[h:h + 1, :] * acc_sc[h] + _nn(vt_ref[HEAD_DIM * h:HEAD_DIM * (h + 1), sl], pe_sc[h])

        def step(j, carry):
            block(j, False)
            return carry

        lax.fori_loop(0, i, step, 0)
        block(i, True)
        lse_ref[...] = m_sc[...] + jnp.log(l_sc[...]) * LOG2E
        for p in range(4):
            ot = jnp.concatenate([acc_sc[h] / l_sc[h:h + 1, :] for h in (2 * p, 2 * p + 1)], axis=0)
            o_ref[:, LANES * p:LANES * (p + 1)] = ot.T

        @pl.when((pl.program_id(0) == nbat - 1) & (i == nq - 1))
        def _():
            for cp in _comm_copies(*comm):
                cp.wait()

    res = pl.pallas_call(
        body, name="fox_fwd", grid=(nbat, nq),
        in_specs=[pl.BlockSpec((tq, WIDTH), lambda b, i: (b * nq + i, 0)),
                  pl.BlockSpec((seq, WIDTH), lambda b, i: (b, 1)), pl.BlockSpec((WIDTH, seq), lambda b, i: (b, 0)),
                  pl.BlockSpec((seq, LANES), lambda b, i: (b, 0))] + [ANY] * n,
        out_specs=[pl.BlockSpec((tq, WIDTH), lambda b, i: (b * nq + i, 0)),
                   pl.BlockSpec((N_HEADS, tq), lambda b, i: (0, b * nq + i))] + [ANY] * n,
        out_shape=[jax.ShapeDtypeStruct((t_all, WIDTH), F32), jax.ShapeDtypeStruct((N_HEADS, t_all), F32)]
        + _comm_out_shapes(shards, to_all),
        scratch_shapes=[pltpu.VMEM((N_HEADS, tq, LANES), BF16), pltpu.VMEM((N_HEADS, tq), F32),
                        pltpu.VMEM((N_HEADS, tq), F32), pltpu.VMEM((N_HEADS, HEAD_DIM, tq), F32),
                        pltpu.VMEM((N_HEADS, tq), F32), pltpu.VMEM((N_HEADS, tq, tq), F32),
                        pltpu.VMEM((N_HEADS, tq, tq), BF16)] + _comm_sems(n),
        compiler_params=_params(),
    )(za, za, vt, f_col, *shards)
    return res[0], res[1], res[2:]


def _fox_bwd(za, do, f_col, lse_row, dl_row, seq, grads):
    t_all = za.shape[0]
    tk = FOX_T
    nk = seq // tk
    nbat = t_all // seq
    n = len(grads)
    to_all = [False] * n

    def body(*refs):
        k_ref, v_ref, q_ref, do_ref, fc_ref, lr_ref, dr_ref = refs[:7]
        dk_ref, dv_ref, df_ref, dqt_ref, dfq_ref = refs[7 + n:12 + n]
        km_sc, vm_sc, fk_sc, dk_sc, dv_sc, cs_sc, kt_sc, st_sc, dp_sc, pt_sc, ds_sc = refs[12 + 2 * n:23 + 2 * n]
        comm = (refs[7:7 + n], refs[12 + n:12 + 2 * n], to_all, refs[23 + 2 * n:])
        j = pl.program_id(1)

        @pl.when(j == 0)
        def _():
            dqt_ref[...] = jnp.zeros_like(dqt_ref)
            dfq_ref[...] = jnp.zeros_like(dfq_ref)

        @pl.when((pl.program_id(0) == 0) & (j == 0))
        def _():
            for cp in _comm_copies(*comm):
                cp.start()
        lo, hi = _half_masks(tk)
        r = lax.broadcasted_iota(jnp.int32, (tk, tk), 0)
        c = lax.broadcasted_iota(jnp.int32, (tk, tk), 1)
        tri = c >= r
        _fox_prep(km_sc, k_ref, lo, hi)
        _fox_prep(vm_sc, v_ref, lo, hi)
        for h in range(N_HEADS):
            fk_sc[h] = jnp.broadcast_to(fc_ref[:, h:h + 1], (tk, tk))
        for p in range(4):
            kt_sc[p] = k_ref[:, LANES * p:LANES * (p + 1)].astype(F32).T.astype(BF16)
        dk_sc[...] = jnp.zeros_like(dk_sc)
        dv_sc[...] = jnp.zeros_like(dv_sc)
        cs_sc[...] = jnp.zeros_like(cs_sc)

        def block(i, masked):
            sl = pl.ds(pl.multiple_of(i * tk, tk), tk)
            for p in range(4):
                cs = slice(LANES * p, LANES * (p + 1))
                qi = q_ref[sl, cs]
                doi = do_ref[sl, cs]
                for h in (2 * p, 2 * p + 1):
                    st = _nt(km_sc[h], qi) - fk_sc[h] - lr_ref[h:h + 1, sl]
                    st_sc[h] = jnp.where(tri, st, NEG) if masked else st
                    dp_sc[h] = _nt(vm_sc[h], doi) - dr_ref[h:h + 1, sl]
            for h in range(N_HEADS):
                pt = jnp.exp2(st_sc[h])
                dst = pt * dp_sc[h]
                pt_sc[h] = pt.astype(BF16)
                ds_sc[h] = dst.astype(BF16)
                cs_sc[h] += dst[:, :LANES] + dst[:, LANES:]
                dfq_ref[h:h + 1, sl] += jnp.sum(dst, axis=0, keepdims=True)
            for p in range(4):
                cs = slice(LANES * p, LANES * (p + 1))
                qi = q_ref[sl, cs]
                doi = do_ref[sl, cs]
                for h in (2 * p, 2 * p + 1):
                    dv_sc[h] += _nn(pt_sc[h], doi)
                    dk_sc[h] += _nn(ds_sc[h], qi)
                    kt = kt_sc[p, HEAD_DIM * (h % 2):HEAD_DIM * (h % 2 + 1), :]
                    dqt_ref[HEAD_DIM * h:HEAD_DIM * (h + 1), sl] += _nn(kt, ds_sc[h])

        def step(i, carry):
            block(i, False)
            return carry

        block(j, True)
        lax.fori_loop(j + 1, nk, step, 0)
        df_ref[...] = jnp.zeros_like(df_ref)
        for p in range(4):
            cs = slice(LANES * p, LANES * (p + 1))
            dk_ref[:, cs] = (jnp.where(lo, dk_sc[2 * p], dk_sc[2 * p + 1]) * (1.0 / LOG2E)).astype(BF16)
            dv_ref[:, cs] = jnp.where(lo, dv_sc[2 * p], dv_sc[2 * p + 1]).astype(BF16)
            for h in (2 * p, 2 * p + 1):
                df_ref[:, h:h + 1] = -jnp.sum(cs_sc[h], axis=1, keepdims=True)

        @pl.when(j == nk - 1)
        def _():
            dqt_ref[...] = dqt_ref[...] * Q_SCALE

        @pl.when((pl.program_id(0) == nbat - 1) & (j == nk - 1))
        def _():
            for cp in _comm_copies(*comm):
                cp.wait()

    tile = lambda w, col: pl.BlockSpec((tk, w), lambda b, j: (b * nk + j, col))
    full = lambda col: pl.BlockSpec((seq, WIDTH), lambda b, j: (b, col))
    row = pl.BlockSpec((N_HEADS, seq), lambda b, j: (0, b))
    acc = pltpu.VMEM((N_HEADS, tk, LANES), F32)
    res = pl.pallas_call(
        body, name="fox_bwd", grid=(nbat, nk),
        in_specs=[tile(WIDTH, 1), tile(WIDTH, 2), full(0), full(0), tile(LANES, 0), row, row] + [ANY] * n,
        out_specs=[tile(WIDTH, 0), tile(WIDTH, 0), tile(LANES, 0), pl.BlockSpec((WIDTH, seq), lambda b, j: (b, 0)),
                   row] + [ANY] * n,
        out_shape=[jax.ShapeDtypeStruct((t_all, WIDTH), BF16), jax.ShapeDtypeStruct((t_all, WIDTH), BF16),
                   jax.ShapeDtypeStruct((t_all, LANES), F32), jax.ShapeDtypeStruct((nbat * WIDTH, seq), F32),
                   jax.ShapeDtypeStruct((N_HEADS, t_all), F32)] + _comm_out_shapes(grads, to_all),
        scratch_shapes=[pltpu.VMEM((N_HEADS, tk, LANES), BF16), pltpu.VMEM((N_HEADS, tk, LANES), BF16),
                        pltpu.VMEM((N_HEADS, tk, tk), F32), acc, acc, acc, pltpu.VMEM((4, LANES, tk), BF16),
                        pltpu.VMEM((N_HEADS, tk, tk), F32), pltpu.VMEM((N_HEADS, tk, tk), F32),
                        pltpu.VMEM((N_HEADS, tk, tk), BF16), pltpu.VMEM((N_HEADS, tk, tk), BF16)]
        + _comm_sems(n),
        compiler_params=_params(),
    )(za, za, za, do, f_col, lse_row, dl_row, *grads)
    return res[0], res[1], res[2], res[3], res[4], res[5:]


DIL_SUB = 4


def _dil_mask(has_prev):
    qi = lax.broadcasted_iota(jnp.int32, (BLK, 2 * BLK), 0)
    kj = lax.broadcasted_iota(jnp.int32, (BLK, 2 * BLK), 1)
    dist = qi + BLK - kj
    band = (dist >= 0) & (dist <= BLK)
    return band if has_prev is True else band & ((kj >= BLK) | has_prev)


def _dil_geometry(t_all, seq, d, max_sub=DIL_SUB):
    length = seq // d
    nbs = length // BLK
    sub = min(max_sub, nbs)
    spb = nbs // sub
    tile = lambda width, col: pl.BlockSpec((BLK * sub, width), lambda s: (s, col))
    whole = lambda width, col: pl.BlockSpec((length, width), lambda s: (s // spb, col))
    return nbs, sub, spb, t_all // (BLK * sub), tile, whole


def _blk(i):
    return pl.ds(pl.multiple_of(i * BLK, BLK), BLK)


def _dil_fwd(zb, seq, d):
    t_all = zb.shape[0]
    nbs, sub, spb, steps, tile, whole = _dil_geometry(t_all, seq, d)

    def body(q_ref, k_ref, v_ref, o_ref, lse_ref, s_sc, p_sc):
        first = (pl.program_id(0) % spb) * sub
        lo, hi = _half_masks(BLK)
        lse_ref[...] = jnp.zeros_like(lse_ref)
        for j in range(sub):
            blk = first + j
            mask = _dil_mask(blk != 0 if j == 0 else True)
            for p in range(4):
                cs = slice(LANES * p, LANES * (p + 1))
                qp = q_ref[BLK * j:BLK * (j + 1), cs]
                kcat = jnp.concatenate([k_ref[_blk(jnp.maximum(blk - 1, 0)), cs], k_ref[_blk(blk), cs]], axis=0)
                for e in (0, 1):
                    qe = jnp.where(lo if e == 0 else hi, qp, jnp.zeros_like(qp))
                    s_sc[N_HEADS * j + 2 * p + e] = jnp.where(mask, _nt(qe, kcat), NEG)
        inv = []
        for i in range(N_HEADS * sub):
            s = s_sc[i]
            m = jnp.max(s, axis=1, keepdims=True)
            pe = jnp.exp2(s - m)
            l = jnp.sum(pe, axis=1, keepdims=True)
            p_sc[i] = pe.astype(BF16)
            inv.append(1.0 / l)
            j, h = divmod(i, N_HEADS)
            lse_ref[BLK * j:BLK * (j + 1), h:h + 1] = m + jnp.log(l) * LOG2E
        for j in range(sub):
            blk = first + j
            for p in range(4):
                cs = slice(LANES * p, LANES * (p + 1))
                vcat = jnp.concatenate([v_ref[_blk(jnp.maximum(blk - 1, 0)), cs], v_ref[_blk(blk), cs]], axis=0)
                res = [_nn(p_sc[N_HEADS * j + h], vcat) * inv[N_HEADS * j + h] for h in (2 * p, 2 * p + 1)]
                o_ref[BLK * j:BLK * (j + 1), cs] = jnp.where(lo, res[0], res[1]).astype(BF16)

    return pl.pallas_call(
        body, name=f"dil_fwd_{d}", grid=(steps,), in_specs=[tile(WIDTH, 0), whole(WIDTH, 1), whole(WIDTH, 2)],
        out_specs=[tile(WIDTH, 0), tile(LANES, 0)],
        out_shape=[jax.ShapeDtypeStruct((t_all, WIDTH), BF16), jax.ShapeDtypeStruct((t_all, LANES), F32)],
        scratch_shapes=[pltpu.VMEM((N_HEADS * sub, BLK, 2 * BLK), F32),
                        pltpu.VMEM((N_HEADS * sub, BLK, 2 * BLK), BF16)],
        compiler_params=_params(),
    )(zb, zb, zb)


def _dil_bwd(zb, do, lse, dl, seq, d):
    t_all = zb.shape[0]
    length = seq // d
    nbs, sub, spb, steps, tile, whole = _dil_geometry(t_all, seq, d, 2 if length >= 4096 else DIL_SUB)

    def body(k_ref, v_ref, q_ref, do_ref, lse_ref, dl_ref, dq_ref, dk_ref, dv_ref, s_sc, dp_sc, pt_sc, ds_sc, kt_sc,
             dqt_sc):
        step = pl.program_id(0) % spb
        first = step * sub

        @pl.when(step == 0)
        def _():
            dqt_sc[...] = jnp.zeros_like(dqt_sc)
        r = lax.broadcasted_iota(jnp.int32, (BLK, 2 * BLK), 0)
        c = lax.broadcasted_iota(jnp.int32, (BLK, 2 * BLK), 1)
        same = (c < BLK) & (c >= r)
        later = (c >= BLK) & (c - BLK <= r)
        lo, hi = _half_masks(BLK)
        for j in range(sub):
            blk = first + j
            rows = slice(BLK * j, BLK * (j + 1))
            nxt = _blk(jnp.minimum(blk + 1, nbs - 1))
            mask = same | (later & (blk + 1 != nbs)) if j == sub - 1 else same | later
            lrows = jnp.concatenate([lse_ref[_blk(blk), :].T, lse_ref[nxt, :].T], axis=1)
            erows = jnp.concatenate([dl_ref[_blk(blk), :].T, dl_ref[nxt, :].T], axis=1)
            for p in range(4):
                cs = slice(LANES * p, LANES * (p + 1))
                kp = k_ref[rows, cs]
                vp = v_ref[rows, cs]
                kt_sc[4 * j + p] = kp.astype(F32).T.astype(BF16)
                qcat = jnp.concatenate([q_ref[_blk(blk), cs], q_ref[nxt, cs]], axis=0)
                dcat = jnp.concatenate([do_ref[_blk(blk), cs], do_ref[nxt, cs]], axis=0)
                for e in (0, 1):
                    h = 2 * p + e
                    sel = lo if e == 0 else hi
                    ke = jnp.where(sel, kp, jnp.zeros_like(kp))
                    ve = jnp.where(sel, vp, jnp.zeros_like(vp))
                    s_sc[N_HEADS * j + h] = jnp.where(mask, _nt(ke, qcat) - lrows[h:h + 1, :], NEG)
                    dp_sc[N_HEADS * j + h] = _nt(ve, dcat) - erows[h:h + 1, :]
        for i in range(N_HEADS * sub):
            pt = jnp.exp2(s_sc[i])
            pt_sc[i] = pt.astype(BF16)
            ds_sc[i] = (pt * dp_sc[i]).astype(BF16)
        for j in range(sub):
            blk = first + j
            rows = slice(BLK * j, BLK * (j + 1))
            nxt = _blk(jnp.minimum(blk + 1, nbs - 1))
            cols = pl.ds(pl.multiple_of(blk * BLK, BLK), 2 * BLK)
            for p in range(4):
                cs = slice(LANES * p, LANES * (p + 1))
                qcat = jnp.concatenate([q_ref[_blk(blk), cs], q_ref[nxt, cs]], axis=0)
                dcat = jnp.concatenate([do_ref[_blk(blk), cs], do_ref[nxt, cs]], axis=0)
                i = N_HEADS * j + 2 * p
                dk_ref[rows, cs] = (jnp.where(lo, _nn(ds_sc[i], qcat), _nn(ds_sc[i + 1], qcat))
                                    * (1.0 / LOG2E)).astype(BF16)
                dv_ref[rows, cs] = jnp.where(lo, _nn(pt_sc[i], dcat), _nn(pt_sc[i + 1], dcat)).astype(BF16)
                for e in (0, 1):
                    kt = kt_sc[4 * j + p, HEAD_DIM * e:HEAD_DIM * (e + 1), :]
                    dqt_sc[HEAD_DIM * (2 * p + e):HEAD_DIM * (2 * p + e + 1), cols] += _nn(kt, ds_sc[i + e])

        @pl.when(step == spb - 1)
        def _():
            for p in range(4):
                cs = slice(LANES * p, LANES * (p + 1))
                dq_ref[:, cs] = (dqt_sc[cs, 0:length].T * Q_SCALE).astype(BF16)

    wide = pltpu.VMEM((N_HEADS * sub, BLK, 2 * BLK), F32)
    half = pltpu.VMEM((N_HEADS * sub, BLK, 2 * BLK), BF16)
    return pl.pallas_call(
        body, name=f"dil_bwd_{d}", grid=(steps,),
        in_specs=[tile(WIDTH, 1), tile(WIDTH, 2), whole(WIDTH, 0), whole(WIDTH, 0), whole(LANES, 0), whole(LANES, 0)],
        out_specs=[whole(WIDTH, 0), tile(WIDTH, 0), tile(WIDTH, 0)],
        out_shape=[jax.ShapeDtypeStruct((t_all, WIDTH), BF16)] * 3,
        scratch_shapes=[wide, wide, half, half, pltpu.VMEM((4 * sub, LANES, BLK), BF16),
                        pltpu.VMEM((WIDTH, length + BLK), F32)],
        compiler_params=_params(),
    )(zb, zb, zb, do, lse, dl)


def _mix_out(oa, o3, l3, gn_a, gn_b, w_out, x, ada3, ln_g, ln_b, perms, seq):
    t_all = x.shape[0]
    tm = TOK_TM
    nts = seq // tm

    def body(oa_ref, o1_ref, o2_ref, o3_ref, l1_ref, l2_ref, l3_ref, ga_ref, gb_ref, w_ref, x_ref, ada_ref, g_ref,
             b_ref, p4_ref, p16_ref, pt4_ref, pt16_ref, ob_ref, lse_ref, lse4_ref, lse16_ref, mg_ref, mix_ref, xh_ref,
             rs_ref, h2_ref, h2t_ref):
        e, et = _head_mats()
        la = l1_ref[...]
        lb = _permute_f32(pt4_ref[...], _load_classes(l2_ref, 4))
        lc = _permute_f32(pt16_ref[...], _load_classes(l3_ref, 16))
        mx = jnp.maximum(jnp.maximum(la, lb), lc)
        ea, eb, ec = jnp.exp2(la - mx), jnp.exp2(lb - mx), jnp.exp2(lc - mx)
        tot = ea + eb + ec
        lse = mx + jnp.log(tot) * LOG2E
        lse_ref[...] = lse
        _store_classes(lse4_ref, _permute_f32(p4_ref[...], lse), 4)
        _store_classes(lse16_ref, _permute_f32(p16_ref[...], lse), 16)
        ob = (o1_ref[...].astype(F32) * _hexp(ea / tot, e)
              + _nn(pt4_ref[...], _load_classes(o2_ref, 4)) * _hexp(eb / tot, e)
              + _nn(pt16_ref[...], _load_classes(o3_ref, 16)) * _hexp(ec / tot, e))
        ob_ref[...] = ob

        def rms(o, gain):
            rr = lax.rsqrt(_hsum(o * o, et) * (1.0 / HEAD_DIM) + RMS_EPS)
            return o * _hexp(rr, e) * gain

        merged = jnp.concatenate([rms(oa_ref[...], ga_ref[...]), rms(ob, gb_ref[...])], axis=1).astype(BF16)
        mg_ref[...] = merged
        mix = _nn(merged, w_ref[...])
        mix_ref[...] = mix.astype(BF16)
        r1 = ALPHA * x_ref[...] + ada_ref[0, 2:3, :] * mix
        d = r1 - jnp.mean(r1, axis=1, keepdims=True)
        rstd = lax.rsqrt(jnp.mean(d * d, axis=1, keepdims=True) + LN_EPS)
        xh = d * rstd
        xh_ref[...] = xh
        rs_ref[...] = jnp.broadcast_to(rstd, (tm, LANES))
        x1 = xh * g_ref[...] + b_ref[...]
        h2 = x1 * (1.0 + ada_ref[0, 4:5, :]) + ada_ref[0, 3:4, :]
        h2_ref[...] = h2.astype(BF16)
        h2t_ref[0] = h2.T.astype(BF16)

    tok = lambda w: pl.BlockSpec((tm, w), lambda i: (i, 0))
    vec = lambda w: pl.BlockSpec((1, w), lambda i: (0, 0))
    whole = lambda a: pl.BlockSpec(a.shape, lambda i: (0, 0))
    classes = lambda a, d: a.reshape(t_all // seq * d, seq // d, a.shape[-1])
    return pl.pallas_call(
        body, name="mix_out", grid=(t_all // tm,),
        in_specs=[tok(WIDTH), tok(WIDTH), _class_spec(4, WIDTH, nts), _class_spec(16, WIDTH, nts), tok(LANES),
                  _class_spec(4, LANES, nts), _class_spec(16, LANES, nts), vec(WIDTH), vec(WIDTH), whole(w_out),
                  tok(D_MODEL), pl.BlockSpec((1, 6, D_MODEL), lambda i: (i // nts, 0, 0)), vec(D_MODEL), vec(D_MODEL)]
        + [whole(p) for p in perms],
        out_specs=[tok(WIDTH), tok(LANES), _class_spec(4, LANES, nts), _class_spec(16, LANES, nts), tok(D_MODEL),
                   tok(D_MODEL), tok(D_MODEL), tok(LANES), tok(D_MODEL), pl.BlockSpec((1, D_MODEL, tm), lambda i: (i // (FFN_TM // tm), 0, i % (FFN_TM // tm)))],
        out_shape=[jax.ShapeDtypeStruct((t_all, WIDTH), F32), jax.ShapeDtypeStruct((t_all, LANES), F32),
                   _class_shape(t_all, seq, 4, LANES, F32), _class_shape(t_all, seq, 16, LANES, F32),
                   jax.ShapeDtypeStruct((t_all, D_MODEL), BF16), jax.ShapeDtypeStruct((t_all, D_MODEL), BF16),
                   jax.ShapeDtypeStruct((t_all, D_MODEL), F32), jax.ShapeDtypeStruct((t_all, LANES), F32),
                   jax.ShapeDtypeStruct((t_all, D_MODEL), BF16),
                   jax.ShapeDtypeStruct((t_all // FFN_TM, D_MODEL, FFN_TM), BF16)],
        compiler_params=_params(),
    )(oa, o3[0], classes(o3[1], 4), classes(o3[2], 16), l3[0], classes(l3[1], 4), classes(l3[2], 16), gn_a, gn_b,
      w_out, x, ada3, ln_g, ln_b, *perms)


def _mix_out_bwd(dmix, w_out, oa, ob, gn_a, gn_b, perms, seq):
    t_all = dmix.shape[0]
    tm = TOK_TM
    nts = seq // tm

    def body(dm_ref, w_ref, oa_ref, ob_ref, ga_ref, gb_ref, p4_ref, p16_ref, doa_ref, dob_ref, dob4_ref, dob16_ref,
             dla_ref, dlb_ref, dlb4_ref, dlb16_ref, acc_ref):
        @pl.when(pl.program_id(0) == 0)
        def _():
            acc_ref[...] = jnp.zeros_like(acc_ref)
        e, et = _head_mats()
        dmg = _nt(dm_ref[...], w_ref[...])

        def group(o, dn, gain):
            rr = lax.rsqrt(_hsum(o * o, et) * (1.0 / HEAD_DIM) + RMS_EPS)
            re = _hexp(rr, e)
            dgain = jnp.sum(dn * o * re, axis=0, keepdims=True)
            dxn = dn * gain
            tt = _hsum(dxn * o, et) * (rr * rr * rr) * (1.0 / HEAD_DIM)
            do = re * dxn - o * _hexp(tt, e)
            return do, _hsum(do * o, et), dgain

        doa, dla, dga = group(oa_ref[...], dmg[:, :WIDTH], ga_ref[...])
        dob, dlb, dgb = group(ob_ref[...], dmg[:, WIDTH:], gb_ref[...])
        dob = dob.astype(BF16)
        doa_ref[...] = doa.astype(BF16)
        dob_ref[...] = dob
        _store_classes(dob4_ref, _nn(p4_ref[...], dob).astype(BF16), 4)
        _store_classes(dob16_ref, _nn(p16_ref[...], dob).astype(BF16), 16)
        dla_ref[...] = dla
        dlb_ref[...] = dlb
        _store_classes(dlb4_ref, _permute_f32(p4_ref[...], dlb), 4)
        _store_classes(dlb16_ref, _permute_f32(p16_ref[...], dlb), 16)
        acc_ref[0:1, :] += jnp.concatenate([dga, dgb], axis=1)

    tok = lambda w: pl.BlockSpec((tm, w), lambda i: (i, 0))
    vec = lambda w: pl.BlockSpec((1, w), lambda i: (0, 0))
    return pl.pallas_call(
        body, name="mix_out_bwd", grid=(t_all // tm,),
        in_specs=[tok(D_MODEL), pl.BlockSpec(w_out.shape, lambda i: (0, 0)), tok(WIDTH), tok(WIDTH), vec(WIDTH),
                  vec(WIDTH), pl.BlockSpec(perms[0].shape, lambda i: (0, 0)),
                  pl.BlockSpec(perms[1].shape, lambda i: (0, 0))],
        out_specs=[tok(WIDTH), tok(WIDTH), _class_spec(4, WIDTH, nts), _class_spec(16, WIDTH, nts), tok(LANES),
                   tok(LANES), _class_spec(4, LANES, nts), _class_spec(16, LANES, nts),
                   pl.BlockSpec((8, D_MODEL), lambda i: (0, 0))],
        out_shape=[jax.ShapeDtypeStruct((t_all, WIDTH), BF16), jax.ShapeDtypeStruct((t_all, WIDTH), BF16),
                   _class_shape(t_all, seq, 4, WIDTH, BF16), _class_shape(t_all, seq, 16, WIDTH, BF16),
                   jax.ShapeDtypeStruct((t_all, LANES), F32), jax.ShapeDtypeStruct((t_all, LANES), F32),
                   _class_shape(t_all, seq, 4, LANES, F32), _class_shape(t_all, seq, 16, LANES, F32),
                   jax.ShapeDtypeStruct((8, D_MODEL), F32)],
        compiler_params=_params(),
    )(dmix, w_out, oa, ob, gn_a, gn_b, perms[0], perms[1])


def _inproj_bwd(dqt, dka, dva, dil1, dil4, dil16, dfa16, pos, wqkv, wf16, freq, perms, dr1, x, ada3, seq):
    t_all = x.shape[0]
    tm = TOK_TM
    nts = seq // tm

    def body(dqt_ref, dka_ref, dva_ref, q1_ref, k1_ref, v1_ref, q4_ref, k4_ref, v4_ref, q16_ref, k16_ref, v16_ref,
             dfa_ref, pos_ref, w_ref, wf_ref, fr_ref, pt4_ref, pt16_ref, dr1_ref, x_ref, ada_ref, gx_ref, dz_ref,
             acc_ref):
        i = pl.program_id(0)

        @pl.when(i == 0)
        def _():
            acc_ref[...] = jnp.zeros_like(acc_ref)
        tabs = _rope_tabs(pos_ref, fr_ref, -1.0)
        dz_ref[:, :WIDTH] = dqt_ref[...].T.astype(BF16)
        dz_ref[:, WIDTH:2 * WIDTH] = dka_ref[...]
        dz_ref[:, 2 * WIDTH:3 * WIDTH] = dva_ref[...]
        for t, (n1, n4, n16) in enumerate(((q1_ref, q4_ref, q16_ref), (k1_ref, k4_ref, k16_ref),
                                           (v1_ref, v4_ref, v16_ref))):
            tot = (n1[...].astype(F32) + _nn(pt4_ref[...], _load_classes(n4, 4))
                   + _nn(pt16_ref[...], _load_classes(n16, 16)))
            if t < 2:
                tot = _rope(tot, tabs)
            dz_ref[:, (3 + t) * WIDTH:(4 + t) * WIDTH] = tot.astype(BF16)
        dh1 = _tn(dfa_ref[...], wf_ref[...])
        for n in range(6):
            cs = slice(n * WIDTH, (n + 1) * WIDTH)
            dh1 = dh1 + _nt(dz_ref[:, cs], w_ref[:, cs])
        xv = x_ref[...]
        gx_ref[...] = ALPHA * dr1_ref[...] + dh1 * (1.0 + ada_ref[0, 1:2, :])
        b = i // nts
        acc_ref[pl.ds(b, 1), :] += jnp.sum(dh1 * xv, axis=0, keepdims=True)
        acc_ref[pl.ds(8 + b, 1), :] += jnp.sum(dh1, axis=0, keepdims=True)

    tok = lambda w: pl.BlockSpec((tm, w), lambda i: (i, 0))
    whole = lambda a: pl.BlockSpec(a.shape, lambda i: (0, 0))
    classes = lambda a, d: a.reshape(t_all // seq * d, seq // d, a.shape[-1])
    return pl.pallas_call(
        body, name="inproj_bwd", grid=(t_all // tm,),
        in_specs=[pl.BlockSpec((WIDTH, tm), lambda i: (i // nts, i % nts)), tok(WIDTH), tok(WIDTH)]
        + [tok(WIDTH)] * 3 + [_class_spec(4, WIDTH, nts)] * 3 + [_class_spec(16, WIDTH, nts)] * 3
        + [pl.BlockSpec((16, tm), lambda i: (0, i)), tok(1), whole(wqkv), whole(wf16),
           pl.BlockSpec((1, LANES), lambda i: (0, 0)), whole(perms[2]), whole(perms[3]), tok(D_MODEL), tok(D_MODEL),
           pl.BlockSpec((1, 6, D_MODEL), lambda i: (i // nts, 0, 0))],
        out_specs=[tok(D_MODEL), tok(6 * WIDTH), pl.BlockSpec((16, D_MODEL), lambda i: (0, 0))],
        out_shape=[jax.ShapeDtypeStruct((t_all, D_MODEL), F32), jax.ShapeDtypeStruct((t_all, 6 * WIDTH), BF16),
                   jax.ShapeDtypeStruct((16, D_MODEL), F32)],
        compiler_params=_params(),
    )(dqt, dka, dva, *dil1, *[classes(a, 4) for a in dil4], *[classes(a, 16) for a in dil16], dfa16, pos, wqkv, wf16,
      freq, perms[2], perms[3], dr1, x, ada3)


FFN_TM = 1024
FFN_TN = 256
HALO = 8


FFN_CHUNK = 256


def _conv_params(cw_ref, cb_ref, n, tn):
    a = pl.ds(pl.multiple_of(n * tn, tn), tn)
    g = pl.ds(pl.multiple_of(D_FF + n * tn, tn), tn)
    return cw_ref[:, a], cw_ref[:, g], cb_ref[:, a], cb_ref[:, g]


def _conv(cat_ref, w_ref, b_ref, start, rows, halo=HALO):
    return (b_ref[...] + w_ref[0:1, :] * cat_ref[pl.ds(start + halo - 2, rows), :]
            + w_ref[1:2, :] * cat_ref[pl.ds(start + halo - 1, rows), :]
            + w_ref[2:3, :] * cat_ref[pl.ds(start + halo, rows), :])


def _ffn_up_gate(h2, w_up, conv_w, conv_b, seq):
    t_all = h2.shape[0]
    tm, tn = min(2 * FFN_TM, seq), FFN_TN
    nc = D_FF // tn
    nts = seq // tm
    pre = 16

    def body(h_ref, hp_ref, wua_ref, wug_ref, cw_ref, cb_ref, ua_ref, ug_ref, o_ref, ca_ref, cg_ref):
        first = (pl.program_id(1) % nts) == 0
        wa_ref, wg_ref, ba_ref, bg_ref = _conv_params(cw_ref, cb_ref, pl.program_id(0), tn)
        hcat = jnp.concatenate([hp_ref[...], h_ref[...]], axis=0)
        zero = jnp.zeros((pre, tn), F32)
        for w_ref, cat, u_ref in ((wua_ref, ca_ref, ua_ref), (wug_ref, cg_ref, ug_ref)):
            ub = _nn(hcat, w_ref[...]).astype(BF16)
            ue = ub.astype(F32)
            cat[0:pre, :] = jnp.where(first, zero, ue[0:pre])
            cat[pre:, :] = ue[pre:]
            u_ref[...] = ub[pre:]
        for c0 in range(0, tm, FFN_CHUNK):
            ya = _conv(ca_ref, wa_ref, ba_ref, c0, FFN_CHUNK, pre)
            yg = _conv(cg_ref, wg_ref, bg_ref, c0, FFN_CHUNK, pre)
            o_ref[c0:c0 + FFN_CHUNK, :] = (yg * jax.nn.sigmoid(yg) * ya).astype(BF16)

    wcol = lambda off: pl.BlockSpec((D_MODEL, tn), lambda n, t: (0, n + off))
    tile = pl.BlockSpec((tm, tn), lambda n, t: (t, n))
    return pl.pallas_call(
        body, name="ffn_up_gate", grid=(nc, t_all // tm),
        in_specs=[pl.BlockSpec((tm, D_MODEL), lambda n, t: (t, 0)),
                  pl.BlockSpec((pre, D_MODEL), lambda n, t: (jnp.maximum(t * (tm // pre) - 1, 0), 0)),
                  wcol(0), wcol(nc), pl.BlockSpec(conv_w.shape, lambda n, t: (0, 0)),
                  pl.BlockSpec(conv_b.shape, lambda n, t: (0, 0))],
        out_specs=[tile, tile, tile],
        out_shape=[jax.ShapeDtypeStruct((t_all, D_FF), BF16)] * 3,
        scratch_shapes=[pltpu.VMEM((tm + pre, tn), F32)] * 2, compiler_params=_params(),
    )(h2, h2, w_up, w_up, conv_w, conv_b)


def _ffn_gate_bwd(u_a, u_g, dfi, conv_w, conv_b, h2t, seq):
    t_all = u_a.shape[0]
    tm, tn = FFN_TM, FFN_TN
    nc = D_FF // tn
    nts = seq // tm

    def body(ua_ref, uap_ref, uan_ref, ug_ref, ugp_ref, ugn_ref, df_ref, dfn_ref, cw_ref, cb_ref, h_ref,
             dua_ref, dug_ref, acca_ref, accg_ref, dwa_ref, dwg_ref, ca_ref, cg_ref, ya_ref, yg_ref, dwa_sc, dwg_sc,
             out_sems):
        t = pl.program_id(0)
        n = pl.program_id(1)
        cols = pl.ds(pl.multiple_of(n * tn, tn), tn)
        first = (t % nts) == 0
        last = (t % nts) == nts - 1

        @pl.when((t == 0) & (n == 0))
        def _():
            acca_ref[...] = jnp.zeros_like(acca_ref)
            accg_ref[...] = jnp.zeros_like(accg_ref)
            dwa_sc[...] = jnp.zeros_like(dwa_sc)
            dwg_sc[...] = jnp.zeros_like(dwg_sc)
        wa_ref, wg_ref, ba_ref, bg_ref = _conv_params(cw_ref, cb_ref, n, tn)
        zero = jnp.zeros((HALO, tn), F32)
        for cat, cur, prv, nxt in ((ca_ref, ua_ref, uap_ref, uan_ref), (cg_ref, ug_ref, ugp_ref, ugn_ref)):
            cat[0:HALO, :] = jnp.where(first, zero, prv[...].astype(F32)[HALO:])
            cat[HALO:HALO + tm, :] = cur[...].astype(F32)
            cat[HALO + tm:, :] = nxt[...].astype(F32)[:HALO]
        ch = FFN_CHUNK
        sums = [[jnp.zeros((1, tn), F32) for _ in range(4)] for _ in range(2)]
        for ci, c0 in enumerate(range(0, tm, ch)):
            ya = _conv(ca_ref, wa_ref, ba_ref, c0, ch + HALO)
            yg = _conv(cg_ref, wg_ref, bg_ref, c0, ch + HALO)
            if c0 + ch < tm:
                beyond = df_ref[c0 + ch:c0 + ch + 16, :].astype(F32)[:HALO]
            else:
                beyond = jnp.where(last, 0.0, dfn_ref[...].astype(F32)[:HALO])
            dfe = jnp.concatenate([df_ref[c0:c0 + ch, :].astype(F32), beyond], axis=0)
            sg = jax.nn.sigmoid(yg)
            ya_ref[ci] = dfe * (yg * sg)
            yg_ref[ci] = dfe * ya * (sg * (1.0 + yg * (1.0 - sg)))
            for half, (dy, cat, w_ref, du_ref) in enumerate(((ya_ref, ca_ref, wa_ref, dua_ref),
                                                             (yg_ref, cg_ref, wg_ref, dug_ref))):
                d0 = dy[ci, 0:ch, :]
                du = (w_ref[2:3, :] * d0 + w_ref[1:2, :] * dy[ci, pl.ds(1, ch), :]
                      + w_ref[0:1, :] * dy[ci, pl.ds(2, ch), :])
                du_ref[c0:c0 + ch, :] = du.astype(BF16)
                for k in range(3):
                    sums[half][k] += jnp.sum(d0 * cat[pl.ds(c0 + HALO - 2 + k, ch), :], axis=0, keepdims=True)
                sums[half][3] += jnp.sum(d0, axis=0, keepdims=True)
        for half, acc in enumerate((acca_ref, accg_ref)):
            for k in range(4):
                acc[k:k + 1, cols] += sums[half][k]
        ht = h_ref[0]
        dwa_sc[:, cols] += _nn(ht, dua_ref[...])
        dwg_sc[:, cols] += _nn(ht, dug_ref[...])

        @pl.when((t == t_all // tm - 1) & (n == nc - 1))
        def _():
            copies = [pltpu.make_async_copy(dwa_sc, dwa_ref, out_sems.at[0]),
                      pltpu.make_async_copy(dwg_sc, dwg_ref, out_sems.at[1])]
            for cp in copies:
                cp.start()
            for cp in copies:
                cp.wait()

    nrow = t_all // 16
    cur = pl.BlockSpec((tm, tn), lambda t, n: (t, n))
    prev = pl.BlockSpec((16, tn), lambda t, n: (jnp.maximum(t * (tm // 16) - 1, 0), n))
    nxt = pl.BlockSpec((16, tn), lambda t, n: (jnp.minimum((t + 1) * (tm // 16), nrow - 1), n))
    acc = pl.BlockSpec((8, D_FF), lambda t, n: (0, 0))
    return pl.pallas_call(
        body, name="ffn_gate_bwd", grid=(t_all // tm, nc),
        in_specs=[cur, prev, nxt, cur, prev, nxt, cur, nxt, pl.BlockSpec(conv_w.shape, lambda t, n: (0, 0)),
                  pl.BlockSpec(conv_b.shape, lambda t, n: (0, 0)),
                  pl.BlockSpec((1, D_MODEL, tm), lambda t, n: (t, 0, 0))],
        out_specs=[cur, cur, acc, acc, ANY, ANY],
        out_shape=[jax.ShapeDtypeStruct((t_all, D_FF), BF16), jax.ShapeDtypeStruct((t_all, D_FF), BF16),
                   jax.ShapeDtypeStruct((8, D_FF), F32), jax.ShapeDtypeStruct((8, D_FF), F32),
                   jax.ShapeDtypeStruct((D_MODEL, D_FF), F32), jax.ShapeDtypeStruct((D_MODEL, D_FF), F32)],
        scratch_shapes=[pltpu.VMEM((tm + 2 * HALO, tn), F32)] * 2
        + [pltpu.VMEM((tm // FFN_CHUNK, FFN_CHUNK + HALO, tn), F32)] * 2
        + [pltpu.VMEM((D_MODEL, D_FF), F32)] * 2 + [pltpu.SemaphoreType.DMA((2,))],
        compiler_params=_params(),
    )(u_a, u_a, u_a, u_g, u_g, u_g, dfi, dfi, conv_w, conv_b, h2t)


def _ffn_down(ffn_in, w_down, xh1, ln1_g, ln1_b, ada3, ln2_g, ln2_b, target, seq):
    t_all = xh1.shape[0]
    tm = 512
    nts = seq // tm

    def body(f_ref, w_ref, xh_ref, g1_ref, b1_ref, ada_ref, g2_ref, b2_ref, tg_ref, dr2_ref, acc_ref):
        i = pl.program_id(0)

        @pl.when(i == 0)
        def _():
            acc_ref[...] = jnp.zeros_like(acc_ref)
        ffn = _nn(f_ref[...], w_ref[...])
        x1 = xh_ref[...] * g1_ref[...] + b1_ref[...]
        r2 = ALPHA * x1 + ada_ref[0, 5:6, :] * ffn
        d = r2 - jnp.mean(r2, axis=1, keepdims=True)
        rstd = lax.rsqrt(jnp.mean(d * d, axis=1, keepdims=True) + LN_EPS)
        xh2 = d * rstd
        diff = xh2 * g2_ref[...] + b2_ref[...] - tg_ref[...]
        dy = diff * (1.0 / D_MODEL)
        dr2 = _layer_norm_bwd(dy * g2_ref[...], xh2, rstd)
        dr2_ref[...] = dr2
        acc_ref[0:1, :] += jnp.sum(dy * xh2, axis=0, keepdims=True)
        acc_ref[1:2, :] += jnp.sum(dy, axis=0, keepdims=True)
        acc_ref[2:3, :] += jnp.sum(diff * diff, axis=0, keepdims=True) * (0.5 / D_MODEL)
        acc_ref[pl.ds(8 + i // nts, 1), :] += jnp.sum(dr2 * ffn, axis=0, keepdims=True)

    tok = lambda w: pl.BlockSpec((tm, w), lambda i: (i, 0))
    vec = pl.BlockSpec((1, D_MODEL), lambda i: (0, 0))
    return pl.pallas_call(
        body, name="ffn_down", grid=(t_all // tm,),
        in_specs=[tok(D_FF), pl.BlockSpec(w_down.shape, lambda i: (0, 0)), tok(D_MODEL), vec, vec,
                  pl.BlockSpec((1, 6, D_MODEL), lambda i: (i // nts, 0, 0)), vec, vec, tok(D_MODEL)],
        out_specs=[tok(D_MODEL), pl.BlockSpec((16, D_MODEL), lambda i: (0, 0))],
        out_shape=[jax.ShapeDtypeStruct((t_all, D_MODEL), F32), jax.ShapeDtypeStruct((16, D_MODEL), F32)],
        compiler_params=_params(),
    )(ffn_in, w_down, xh1, ln1_g, ln1_b, ada3, ln2_g, ln2_b, target)


def _ffn_down_bwd(dr2, ada3, w_down, seq):
    t_all = dr2.shape[0]
    tm = 512
    nts = seq // tm

    def body(d_ref, ada_ref, w_ref, dffn_ref, dfi_ref):
        dffn = (d_ref[...] * ada_ref[0, 5:6, :]).astype(BF16)
        dffn_ref[...] = dffn
        dfi_ref[...] = _nt(dffn, w_ref[...]).astype(BF16)

    tok = lambda w: pl.BlockSpec((tm, w), lambda i: (i, 0))
    return pl.pallas_call(
        body, name="ffn_down_bwd", grid=(t_all // tm,),
        in_specs=[tok(D_MODEL), pl.BlockSpec((1, 6, D_MODEL), lambda i: (i // nts, 0, 0)),
                  pl.BlockSpec(w_down.shape, lambda i: (0, 0))],
        out_specs=[tok(D_MODEL), tok(D_FF)],
        out_shape=[jax.ShapeDtypeStruct((t_all, D_MODEL), BF16), jax.ShapeDtypeStruct((t_all, D_FF), BF16)],
        compiler_params=_params(),
    )(dr2, ada3, w_down)


def _ffn_up_bwd(du_a, du_g, w_up, dr2, xh1, rs1, mix, ada3, ln1_g, ln1_b, seq):
    t_all = dr2.shape[0]
    tm = 512
    nts = seq // tm

    def body(da_ref, dg_ref, w_ref, dr2_ref, xh_ref, rs_ref, mix_ref, ada_ref, g_ref, b_ref, dr1_ref, dmix_ref,
             acc_ref):
        i = pl.program_id(0)

        @pl.when(i == 0)
        def _():
            acc_ref[...] = jnp.zeros_like(acc_ref)
        dh2 = _nt(da_ref[...], w_ref[:, :D_FF]) + _nt(dg_ref[...], w_ref[:, D_FF:])
        xh = xh_ref[...]
        x1 = xh * g_ref[...] + b_ref[...]
        dx1 = ALPHA * dr2_ref[...] + dh2 * (1.0 + ada_ref[0, 4:5, :])
        dr1 = _layer_norm_bwd(dx1 * g_ref[...], xh, rs_ref[:, 0:1])
        dr1_ref[...] = dr1
        dmix_ref[...] = (dr1 * ada_ref[0, 2:3, :]).astype(BF16)
        b = i // nts
        acc_ref[0:1, :] += jnp.sum(dx1 * xh, axis=0, keepdims=True)
        acc_ref[1:2, :] += jnp.sum(dx1, axis=0, keepdims=True)
        acc_ref[pl.ds(8 + b, 1), :] += jnp.sum(dh2 * x1, axis=0, keepdims=True)
        acc_ref[pl.ds(16 + b, 1), :] += jnp.sum(dh2, axis=0, keepdims=True)
        acc_ref[pl.ds(24 + b, 1), :] += jnp.sum(dr1 * mix_ref[...].astype(F32), axis=0, keepdims=True)

    tok = lambda w: pl.BlockSpec((tm, w), lambda i: (i, 0))
    vec = pl.BlockSpec((1, D_MODEL), lambda i: (0, 0))
    return pl.pallas_call(
        body, name="ffn_up_bwd", grid=(t_all // tm,),
        in_specs=[tok(D_FF), tok(D_FF), pl.BlockSpec(w_up.shape, lambda i: (0, 0)), tok(D_MODEL), tok(D_MODEL),
                  tok(LANES), tok(D_MODEL), pl.BlockSpec((1, 6, D_MODEL), lambda i: (i // nts, 0, 0)), vec, vec],
        out_specs=[tok(D_MODEL), tok(D_MODEL), pl.BlockSpec((32, D_MODEL), lambda i: (0, 0))],
        out_shape=[jax.ShapeDtypeStruct((t_all, D_MODEL), F32), jax.ShapeDtypeStruct((t_all, D_MODEL), BF16),
                   jax.ShapeDtypeStruct((32, D_MODEL), F32)],
        compiler_params=_params(),
    )(du_a, du_g, w_up, dr2, xh1, rs1, mix, ada3, ln1_g, ln1_b)


def _rows(a):
    return a[:, :N_HEADS].T


def _rope_freq():
    f = np.float32(ROPE_THETA) ** (-np.arange(0, ROPE_DIMS, 2, dtype=np.float32) / np.float32(ROPE_DIMS))
    return jnp.asarray(np.tile(f.astype(np.float32), LANES // (ROPE_DIMS // 2))[None, :])


def _local_step(x, positions, target, ada3, w_in, b_fgate, gn_a, gn_b, ln1_g, ln1_b, conv_b, ln2_g, ln2_b,
                late_shards):
    nbat, seq, _ = x.shape
    t_all = nbat * seq
    xf = x.reshape(t_all, D_MODEL)
    tg = target.reshape(t_all, D_MODEL)
    pos = positions.reshape(t_all, 1)
    freq = _rope_freq()

    wqkv = jnp.concatenate([w_in[:, :3 * WIDTH], w_in[:, 3 * WIDTH + N_HEADS:]], axis=1)
    wf16 = jnp.zeros((16, D_MODEL), BF16).at[:N_HEADS].set(w_in[:, 3 * WIDTH:3 * WIDTH + N_HEADS].T)
    bf = b_fgate.reshape(N_HEADS, 1)

    perms = [_perm_matrix(TOK_TM, d, tr) for tr in (False, True) for d in DILATIONS[1:]]
    h1, za, zb1, zb4, zb16, vt, fa_t = _inproj(xf, ada3, pos, wqkv, wf16, freq, perms, seq)
    zbs = [zb1, zb4.reshape(t_all, 3 * WIDTH), zb16.reshape(t_all, 3 * WIDTH)]
    f_row = _fgate_fwd(fa_t, bf, seq)
    f_col = jnp.zeros((t_all, LANES), F32).at[:, :N_HEADS].set(f_row.T * LOG2E)
    oa, lse_row_a, gathered = _fox_fwd(za, vt, f_col, seq, [late_shards[n] for n in LATE])
    w_out, w_up, conv_w, w_down = (_full_from_gathered(n, g) for n, g in zip(LATE, gathered))
    o3, l3 = zip(*[_dil_fwd(zb, seq, d) for zb, d in zip(zbs, DILATIONS)])
    ob, lse_b, lse_b4, lse_b16, merged, mix, xh1, rs1, h2, h2t = _mix_out(oa, o3, l3, gn_a, gn_b, w_out, xf, ada3, ln1_g,
                                                                      ln1_b, perms, seq)
    u_a, u_g, ffn_in = _ffn_up_gate(h2, w_up, conv_w, conv_b, seq)
    dr2, acc2 = _ffn_down(ffn_in, w_down, xh1, ln1_g, ln1_b, ada3, ln2_g, ln2_b, tg, seq)

    dffn, dfi = _ffn_down_bwd(dr2, ada3, w_down, seq)
    d_w_down = _matmul_tn(dffn, ffn_in, 512, 1024, "dw_down").T
    du_a, du_g, acc_ca, acc_cg, dw_up_a, dw_up_g = _ffn_gate_bwd(u_a, u_g, dfi, conv_w, conv_b, h2t, seq)
    dr1, dmix, acc1 = _ffn_up_bwd(du_a, du_g, w_up, dr2, xh1, rs1, mix, ada3, ln1_g, ln1_b, seq)

    doa, dob, dob4, dob16, dl_a, dl_b, dl_b4, dl_b16, acc_gn = _mix_out_bwd(dmix, w_out, oa, ob, gn_a, gn_b, perms, seq)
    d_w_out = _matmul_tn(merged, dmix, 512, 1024, "dw_out")
    late_grads = dict(w_out=d_w_out, w_up=(dw_up_a, dw_up_g), conv_w=jnp.concatenate([acc_ca[0:3], acc_cg[0:3]], axis=1),
                      w_down=d_w_down)
    dka, dva, df_k, dqt, df_q, late_parts = _fox_bwd(za, doa, f_col, lse_row_a, _rows(dl_a), seq,
                                                     [_payload(n, _dest_major(n, late_grads[n])) for n in LATE])
    dfa_t, dbf = _fgate_bwd(_rows(df_k) + df_q, fa_t, bf, seq)
    flat = lambda a: a.reshape(t_all, a.shape[-1])
    dil = []
    for zb, d, do, lse, dl in zip(zbs, DILATIONS, (dob, flat(dob4), flat(dob16)),
                                  (lse_b, flat(lse_b4), flat(lse_b16)), (dl_b, flat(dl_b4), flat(dl_b16))):
        dil.append(_dil_bwd(zb, do, lse, dl, seq, d))
    dfa16 = jnp.zeros((16, t_all), BF16).at[:N_HEADS].set(dfa_t.astype(BF16))
    grad_x, dz, acc0 = _inproj_bwd(dqt, dka, dva, dil[0], dil[1], dil[2], dfa16, pos, wqkv, wf16, freq, perms, dr1, xf,
                                   ada3, seq)
    d_wqkv = _matmul_tn(h1, dz, 512, 1024, "dw_in")
    d_wf = _matmul_rows(dfa16, h1, 512, "dw_fgate")[:N_HEADS].T
    d_w_in = jnp.concatenate([d_wqkv[:, :3 * WIDTH], d_wf, d_wqkv[:, 3 * WIDTH:]], axis=1)

    dada = jnp.concatenate([acc0[8:8 + nbat], acc0[:nbat], acc1[24:24 + nbat], acc1[16:16 + nbat], acc1[8:8 + nbat],
                            acc2[8:8 + nbat]], axis=1)

    grads = dict(
        dada=dada, b_ada=jnp.sum(dada, axis=0, keepdims=True), w_in=d_w_in, b_fgate=dbf[:, 0][None, :],
        gn_a=acc_gn[0:1, :WIDTH], gn_b=acc_gn[0:1, WIDTH:], ln1_g=acc1[0:1], ln1_b=acc1[1:2],
        conv_b=jnp.concatenate([acc_ca[3:4], acc_cg[3:4]], axis=1), ln2_g=acc2[0:1], ln2_b=acc2[1:2])
    return acc2[2:3], grad_x.reshape(x.shape), grads, dict(zip(LATE, late_parts))


LATE = ("w_out", "w_up", "conv_w", "w_down")
BIG = ("w_ada", "w_in") + LATE
COLUMN_SHARDED = ("w_ada", "w_in", "w_up", "conv_w")


def _payload(name, a):
    return a if name == "conv_w" else a.astype(BF16)
SMALL = ("b_ada", "b_fgate", "gn_a", "gn_b", "ln1_g", "ln1_b", "conv_b", "ln2_g", "ln2_b")
ADAM_ROWS = dict(w_ada=256, w_in=256, w_out=128, w_up=256, conv_w=3, w_down=176)
SMALL_ROWS = 24


def _full_from_gathered(name, g):
    if name in COLUMN_SHARDED:
        return g.transpose(1, 0, 2).reshape(g.shape[1], N_DEV * g.shape[2])
    return g.reshape(N_DEV * g.shape[1], g.shape[2])


def _dest_major(name, full):
    if name in COLUMN_SHARDED:
        parts = full if isinstance(full, tuple) else (full,)
        per = N_DEV // len(parts)
        return jnp.concatenate([p.reshape(p.shape[0], per, p.shape[1] // per).transpose(1, 0, 2) for p in parts], axis=0)
    return full.reshape(N_DEV, full.shape[0] // N_DEV, full.shape[1])


def _pack_small(vals, extra=None):
    parts = [vals[n].reshape(-1) for n in SMALL]
    if extra is not None:
        parts.append(extra.reshape(-1))
    flat = jnp.concatenate(parts)
    return jnp.pad(flat, (0, SMALL_ROWS * D_MODEL - flat.shape[0])).reshape(SMALL_ROWS, D_MODEL)


def _unpack_small(packed, like):
    flat = packed.reshape(-1)
    out, off = {}, 0
    for n in SMALL:
        size = like[n].size
        out[n] = flat[off:off + size].reshape(like[n].shape)
        off += size
    return out, flat[off:off + D_MODEL]


def kernel(x, c, positions, w_ada, b_ada, w_in, b_fgate, gn_a, gn_b, w_out, ln1_g, ln1_b, w_up, conv_w, conv_b, w_down, ln2_g, ln2_b, loss_target, m_w_ada, m_b_ada, m_w_in, m_b_fgate, m_gn_a, m_gn_b, m_w_out, m_ln1_g, m_ln1_b, m_w_up, m_conv_w, m_conv_b, m_w_down, m_ln2_g, m_ln2_b, v_w_ada, v_b_ada, v_w_in, v_b_fgate, v_gn_a, v_gn_b, v_w_out, v_ln1_g, v_ln1_b, v_w_up, v_conv_w, v_conv_b, v_w_down, v_ln2_g, v_ln2_b):
    w = dict(w_ada=w_ada[0], b_ada=b_ada, w_in=w_in[0], b_fgate=b_fgate, gn_a=gn_a, gn_b=gn_b, w_out=w_out[0],
             ln1_g=ln1_g, ln1_b=ln1_b, w_up=w_up[0], conv_w=conv_w[0], conv_b=conv_b, w_down=w_down[0], ln2_g=ln2_g,
             ln2_b=ln2_b)
    m = dict(w_ada=m_w_ada[0], b_ada=m_b_ada, w_in=m_w_in[0], b_fgate=m_b_fgate, gn_a=m_gn_a, gn_b=m_gn_b,
             w_out=m_w_out[0], ln1_g=m_ln1_g, ln1_b=m_ln1_b, w_up=m_w_up[0], conv_w=m_conv_w[0], conv_b=m_conv_b,
             w_down=m_w_down[0], ln2_g=m_ln2_g, ln2_b=m_ln2_b)
    v = dict(w_ada=v_w_ada[0], b_ada=v_b_ada, w_in=v_w_in[0], b_fgate=v_b_fgate, gn_a=v_gn_a, gn_b=v_gn_b,
             w_out=v_w_out[0], ln1_g=v_ln1_g, ln1_b=v_ln1_b, w_up=v_w_up[0], conv_w=v_conv_w[0], conv_b=v_conv_b,
             w_down=v_w_down[0], ln2_g=v_ln2_g, ln2_b=v_ln2_b)

    nbat = x.shape[0]
    me = 4 * lax.axis_index("x") + 2 * lax.axis_index("y") + lax.axis_index("c")
    ada_cols = w["w_ada"].shape[1]

    c_all, w_in_all = _gather_two_level([c, _payload("w_in", w["w_in"])], "weight_gather")
    c_all = c_all.reshape(N_DEV * nbat, D_MODEL)
    ada_mine = _ada_fwd(c_all, w["w_ada"], lax.dynamic_slice(b_ada, (0, me * ada_cols), (1, ada_cols)))
    (ada_parts,) = _exchange([ada_mine.reshape(N_DEV, nbat, ada_cols)], [False], "ada_exchange")
    ada3 = ada_parts.transpose(1, 0, 2).reshape(nbat, 6, D_MODEL)

    loss_lanes, grad_x, g_local, parts = _local_step(
        x, positions, loss_target, ada3, _full_from_gathered("w_in", w_in_all), b_fgate, gn_a, gn_b, ln1_g, ln1_b,
        conv_b, ln2_g, ln2_b, {n: _payload(n, w[n]) for n in LATE})

    parts["w_in"], dada_all, small_all = _exchange(
        [_payload("w_in", _dest_major("w_in", g_local["w_in"])), g_local["dada"], _pack_small(g_local, loss_lanes)],
        [False, True, True], "grad_exchange")
    dada_cols = lax.dynamic_slice(dada_all.reshape(N_DEV * nbat, 6 * D_MODEL), (0, me * ada_cols),
                                  (N_DEV * nbat, ada_cols))
    parts["w_ada"] = _ada_bwd(c_all, dada_cols)[None]

    grad, delta, new_m, new_v = {}, {}, {}, {}
    for n in BIG:
        grad[n], delta[n], new_m[n], new_v[n] = (
            a[None] for a in _adamw(parts[n], w[n], m[n], v[n], ADAM_ROWS[n], "adamw_" + n))
    packed = _adamw(small_all, _pack_small(w), _pack_small(m), _pack_small(v), SMALL_ROWS, "adamw_small")
    for dst, pk in zip((grad, delta, new_m, new_v), packed):
        vals, lanes = _unpack_small(pk, w)
        dst.update(vals)
        if dst is grad:
            loss = jnp.sum(lanes)

    order = ("w_ada", "b_ada", "w_in", "b_fgate", "gn_a", "gn_b", "w_out", "ln1_g", "ln1_b", "w_up", "conv_w", "conv_b",
             "w_down", "ln2_g", "ln2_b")
    return (loss, grad_x, *[grad[n] for n in order], *[delta[n] for n in order], *[new_m[n] for n in order],
            *[new_v[n] for n in order])
```

```python
import functools

import numpy as np
import jax
import jax.numpy as jnp
from jax import lax
from jax.experimental import pallas as pl
from jax.experimental.pallas import tpu as pltpu

F32, BF16 = jnp.float32, jnp.bfloat16
MESH = pl.DeviceIdType.MESH
ANY = pl.BlockSpec(memory_space=pl.ANY)

D_MODEL = 1024
N_HEADS = 8
HEAD_DIM = 64
WIDTH = 512
D_FF = 2816
N_DEV = 8
ROPE_DIMS = 16
ROPE_THETA = 500000.0
ALPHA = 2.0 ** 0.25
LN_EPS = 1e-5
RMS_EPS = 1e-6
NEG = -1e30
Q_SCALE = 0.125
LOG2E = 1.4426950408889634
BLK = 128
LANES = 128
VMEM_LIMIT_BYTES = 56 * 1024 * 1024

ADAM_LR, ADAM_B1, ADAM_B2, ADAM_EPS, ADAM_WD, ADAM_STEP = 0.001, 0.9, 0.999, 1e-08, 0.01, 10


def _params(vmem=VMEM_LIMIT_BYTES):
    return pltpu.CompilerParams(vmem_limit_bytes=vmem)


def _nn(a, b):
    return jnp.dot(a, b, preferred_element_type=F32)


def _nt(a, b):
    return lax.dot_general(a, b, (((1,), (1,)), ((), ())), preferred_element_type=F32)


def _tn(a, b):
    return lax.dot_general(a, b, (((0,), (0,)), ((), ())), preferred_element_type=F32)


def _head_mats():
    r = lax.broadcasted_iota(jnp.int32, (LANES, WIDTH), 0)
    c = lax.broadcasted_iota(jnp.int32, (LANES, WIDTH), 1)
    e = ((c >> 6) == r).astype(BF16)
    r2 = lax.broadcasted_iota(jnp.int32, (WIDTH, LANES), 0)
    c2 = lax.broadcasted_iota(jnp.int32, (WIDTH, LANES), 1)
    et = ((r2 >> 6) == c2).astype(BF16)
    return e, et


def _split3(x):
    hi = x.astype(BF16)
    r = x - hi.astype(F32)
    mid = r.astype(BF16)
    return hi, mid, (r - mid.astype(F32)).astype(BF16)


def _hexp(w, e):
    return sum(_nn(part, e) for part in _split3(w)[:2])


def _hsum(x, et):
    return sum(_nn(part, et) for part in _split3(x)[:2])


def _perm_matrix(rows, d, transpose):
    i = np.arange(rows)
    j = (i % (rows // d)) * d + i // (rows // d)
    p = np.zeros((rows, rows), np.float32)
    p[i, j] = 1.0
    return jnp.asarray(p.T if transpose else p, BF16)


def _permute_f32(p, x):
    return sum(_nn(p, part) for part in _split3(x))


def _store_classes(ref, y, d):
    n = y.shape[0] // d
    for r in range(d):
        ref[r] = y[r * n:(r + 1) * n, :]


def _load_classes(ref, d):
    return jnp.concatenate([ref[r] for r in range(d)], axis=0)


def _rope_tabs(pos_ref, fr_ref, sign):
    ang = pos_ref[...].astype(F32) * fr_ref[...]
    lane = lax.broadcasted_iota(jnp.int32, ang.shape, 1) & (HEAD_DIM - 1)
    m1 = lane < ROPE_DIMS // 2
    m2 = (lane >= ROPE_DIMS // 2) & (lane < ROPE_DIMS)
    cos = jnp.cos(ang)
    sin = jnp.sin(ang) * sign
    return (jnp.where(m1 | m2, cos, 1.0), jnp.where(m1, -sin, 0.0), jnp.where(m2, sin, 0.0))


def _rope(z, tabs):
    c, s1, s2 = tabs
    parts = []
    for p in range(z.shape[1] // LANES):
        zp = z[:, LANES * p:LANES * (p + 1)]
        parts.append(zp * c + pltpu.roll(zp, LANES - 8, 1) * s1 + pltpu.roll(zp, 8, 1) * s2)
    return jnp.concatenate(parts, axis=1)


def _half_masks(rows):
    lane = lax.broadcasted_iota(jnp.int32, (rows, LANES), 1)
    lo = lane < HEAD_DIM
    return lo, jnp.logical_not(lo)


def _layer_norm_bwd(dxh, xh, rstd):
    m1 = jnp.mean(dxh, axis=1, keepdims=True)
    m2 = jnp.mean(dxh * xh, axis=1, keepdims=True)
    return rstd * (dxh - m1 - xh * m2)


def _coords():
    return lax.axis_index("x"), lax.axis_index("y"), lax.axis_index("c")


def _peer(x, y, c, k):
    return (1 - x if k & 4 else x, 1 - y if k & 2 else y, 1 - c if k & 1 else c)


def _comm_sems(n):
    return [pltpu.SemaphoreType.DMA((N_DEV - 1, n)), pltpu.SemaphoreType.DMA((N_DEV - 1, n)),
            pltpu.SemaphoreType.DMA((n,))]


def _comm_copies(ins, outs, to_all, sems):
    send_sems, recv_sems, local_sems = sems
    x, y, c = _coords()
    me = 4 * x + 2 * y + c
    copies = [pltpu.make_async_copy(ins[t] if to_all[t] else ins[t].at[me], outs[t].at[me], local_sems.at[t])
              for t in range(len(ins))]
    for k in range(1, N_DEV):
        px, py, pc = _peer(x, y, c, k)
        dest = 4 * px + 2 * py + pc
        for t in range(len(ins)):
            copies.append(pltpu.make_async_remote_copy(
                src_ref=ins[t] if to_all[t] else ins[t].at[dest], dst_ref=outs[t].at[me],
                send_sem=send_sems.at[k - 1, t], recv_sem=recv_sems.at[k - 1, t],
                device_id=(px, py, pc), device_id_type=MESH))
    return copies


def _comm_out_shapes(ins, to_all):
    return [jax.ShapeDtypeStruct(((N_DEV,) + a.shape) if ta else a.shape, a.dtype) for a, ta in zip(ins, to_all)]


def _exchange(ins, to_all, name):
    n = len(ins)

    def body(*refs):
        copies = _comm_copies(refs[:n], refs[n:2 * n], to_all, refs[2 * n:])
        for cp in copies:
            cp.start()
        for cp in copies:
            cp.wait()

    return pl.pallas_call(
        body, name=name, out_shape=_comm_out_shapes(ins, to_all), in_specs=[ANY] * n, out_specs=[ANY] * n,
        scratch_shapes=_comm_sems(n),
    )(*ins)


def _gather_two_level(ins, name):
    n = len(ins)

    def body(*refs):
        srcs, outs = refs[:n], refs[n:2 * n]
        send_sems, recv_sems, local_sems = refs[2 * n:]
        x, y, c = _coords()
        me = 4 * x + 2 * y + c
        sibling = (x, y, 1 - c)
        chips = [(1 - x, y), (x, 1 - y), (1 - x, 1 - y)]
        slot = lambda px, py, pc: 4 * px + 2 * py + pc

        def copy(k, t, block, to, own=False):
            return pltpu.make_async_remote_copy(
                src_ref=srcs[t] if own else outs[t].at[block], dst_ref=outs[t].at[block],
                send_sem=send_sems.at[k, t], recv_sem=recv_sems.at[k, t], device_id=to, device_id_type=MESH)

        local = [pltpu.make_async_copy(srcs[t], outs[t].at[me], local_sems.at[t]) for t in range(n)]
        first = [copy(0, t, me, sibling, own=True) for t in range(n)]
        first += [copy(1 + j, t, me, (*chip, c), own=True) for j, chip in enumerate(chips) for t in range(n)]
        for cp in local + first:
            cp.start()
        passed = []
        for j, chip in enumerate(chips):
            for t in range(n):
                copy(1 + j, t, slot(*chip, c), (x, y, c)).wait_recv()
                cp = copy(4 + j, t, slot(*chip, c), sibling)
                cp.start()
                passed.append(cp)
        for t in range(n):
            copy(0, t, slot(x, y, 1 - c), (x, y, c)).wait_recv()
            for j, chip in enumerate(chips):
                copy(4 + j, t, slot(*chip, 1 - c), (x, y, c)).wait_recv()
        for cp in first + passed:
            cp.wait_send()
        for cp in local:
            cp.wait()

    return pl.pallas_call(
        body, name=name, out_shape=_comm_out_shapes(ins, [True] * n), in_specs=[ANY] * n, out_specs=[ANY] * n,
        scratch_shapes=_comm_sems(n),
    )(*ins)


def _adamw(parts, w, m, v, rows, name):
    n_parts, r_all, cols = parts.shape
    c1 = 1.0 - ADAM_B1 ** ADAM_STEP
    c2 = 1.0 - ADAM_B2 ** ADAM_STEP

    def body(p_ref, w_ref, m_ref, v_ref, g_ref, d_ref, mo_ref, vo_ref):
        g = p_ref[0].astype(F32)
        for s in range(1, n_parts):
            g = g + p_ref[s].astype(F32)
        mn = ADAM_B1 * m_ref[...] + (1.0 - ADAM_B1) * g
        vn = ADAM_B2 * v_ref[...] + (1.0 - ADAM_B2) * (g * g)
        m_hat = mn / c1
        v_hat = vn / c2
        g_ref[...] = g
        d_ref[...] = -ADAM_LR * (m_hat / (jnp.sqrt(v_hat) + ADAM_EPS) + ADAM_WD * w_ref[...])
        mo_ref[...] = mn
        vo_ref[...] = vn

    spec = pl.BlockSpec((rows, cols), lambda i: (i, 0))
    return pl.pallas_call(
        body, name=name, grid=(r_all // rows,),
        in_specs=[pl.BlockSpec((n_parts, rows, cols), lambda i: (0, i, 0)), spec, spec, spec],
        out_specs=[spec] * 4, out_shape=[jax.ShapeDtypeStruct((r_all, cols), F32)] * 4,
        compiler_params=_params(),
    )(parts, w, m, v)


def _matmul_tn(a, b, chunk, tk, name):
    t_all, k1 = a.shape
    n = b.shape[1]

    def body(a_ref, b_ref, o_ref):
        @pl.when(pl.program_id(0) == 0)
        def _():
            o_ref[...] = jnp.zeros_like(o_ref)
        at = a_ref[...].astype(F32).T.astype(BF16)
        for j in range(0, n, chunk):
            cs = slice(j, min(j + chunk, n))
            o_ref[:, cs] += _nn(at, b_ref[:, cs])

    return pl.pallas_call(
        body, name=name, grid=(t_all // tk,),
        in_specs=[pl.BlockSpec((tk, k1), lambda t: (t, 0)), pl.BlockSpec((tk, n), lambda t: (t, 0))],
        out_specs=pl.BlockSpec((k1, n), lambda t: (0, 0)),
        out_shape=jax.ShapeDtypeStruct((k1, n), F32), compiler_params=_params(),
    )(a, b)


def _matmul_rows(a, b, tk, name):
    r, t_all = a.shape
    n = b.shape[1]

    def body(a_ref, b_ref, o_ref):
        @pl.when(pl.program_id(0) == 0)
        def _():
            o_ref[...] = jnp.zeros_like(o_ref)
        o_ref[...] += _nn(a_ref[...], b_ref[...])

    return pl.pallas_call(
        body, name=name, grid=(t_all // tk,),
        in_specs=[pl.BlockSpec((r, tk), lambda t: (0, t)), pl.BlockSpec((tk, n), lambda t: (t, 0))],
        out_specs=pl.BlockSpec((r, n), lambda t: (0, 0)),
        out_shape=jax.ShapeDtypeStruct((r, n), F32), compiler_params=_params(),
    )(a, b)


def _ada_fwd(c_all, w_ada, b_ada):
    whole = lambda a: pl.BlockSpec(a.shape, lambda j: (0, 0))

    def body(c_ref, w_ref, b_ref, o_ref):
        cv = c_ref[...]
        s = (cv * jax.nn.sigmoid(cv)).astype(BF16)
        o_ref[...] = _nn(s, w_ref[...].astype(BF16)) + b_ref[...]

    out = jax.ShapeDtypeStruct((c_all.shape[0], w_ada.shape[1]), F32)
    return pl.pallas_call(
        body, name="ada_fwd", grid=(1,), in_specs=[whole(c_all), whole(w_ada), whole(b_ada)], out_specs=whole(out),
        out_shape=out, compiler_params=_params(),
    )(c_all, w_ada, b_ada)


def _ada_bwd(c_all, dada):
    whole = lambda a: pl.BlockSpec(a.shape, lambda j: (0, 0))

    def body(c_ref, d_ref, o_ref):
        cv = c_ref[...]
        s = (cv * jax.nn.sigmoid(cv)).astype(BF16)
        o_ref[...] = _tn(s, d_ref[...].astype(BF16))

    out = jax.ShapeDtypeStruct((D_MODEL, dada.shape[1]), F32)
    return pl.pallas_call(
        body, name="ada_bwd", grid=(1,), in_specs=[whole(c_all), whole(dada)], out_specs=whole(out), out_shape=out,
        compiler_params=_params(),
    )(c_all, dada)


TOK_TM = 256
DILATIONS = (1, 4, 16)


def _class_spec(d, width, nts):
    return pl.BlockSpec((d, TOK_TM // d, width), lambda i: (i // nts, i % nts, 0))


def _class_shape(t_all, seq, d, width, dtype):
    return jax.ShapeDtypeStruct((t_all // seq * d, seq // d, width), dtype)


def _inproj(x, ada3, pos, wqkv, wf16, freq, perms, seq):
    t_all = x.shape[0]
    tm = TOK_TM
    nts = seq // tm

    def body(x_ref, ada_ref, pos_ref, w_ref, wf_ref, fr_ref, p4_ref, p16_ref, h1_ref, za_ref, zb_ref, zb4_ref,
             zb16_ref, vt_ref, fa_ref):
        h1 = (x_ref[...] * (1.0 + ada_ref[0, 1:2, :]) + ada_ref[0, 0:1, :]).astype(BF16)
        h1_ref[...] = h1
        tabs = _rope_tabs(pos_ref, fr_ref, 1.0)
        for n in range(6):
            z = _nn(h1, w_ref[:, n * WIDTH:(n + 1) * WIDTH])
            if n in (3, 4):
                z = _rope(z, tabs)
            if n in (0, 3):
                z = z * (Q_SCALE * LOG2E)
            if n == 2:
                vt_ref[...] = z.T.astype(BF16)
            dst = za_ref if n < 3 else zb_ref
            dst[:, (n % 3) * WIDTH:(n % 3 + 1) * WIDTH] = z.astype(BF16)
        fa_ref[...] = _nt(wf_ref[...], h1)[:N_HEADS]
        zb = zb_ref[...]
        _store_classes(zb4_ref, _nn(p4_ref[...], zb).astype(BF16), 4)
        _store_classes(zb16_ref, _nn(p16_ref[...], zb).astype(BF16), 16)

    tok = lambda w: pl.BlockSpec((tm, w), lambda i: (i, 0))
    whole = lambda a: pl.BlockSpec(a.shape, lambda i: (0, 0))
    return pl.pallas_call(
        body, name="inproj", grid=(t_all // tm,),
        in_specs=[tok(D_MODEL), pl.BlockSpec((1, 6, D_MODEL), lambda i: (i // nts, 0, 0)), tok(1), whole(wqkv),
                  whole(wf16), pl.BlockSpec((1, LANES), lambda i: (0, 0)), whole(perms[0]), whole(perms[1])],
        out_specs=[tok(D_MODEL), tok(3 * WIDTH), tok(3 * WIDTH), _class_spec(4, 3 * WIDTH, nts),
                   _class_spec(16, 3 * WIDTH, nts), pl.BlockSpec((WIDTH, tm), lambda i: (i // nts, i % nts)),
                   pl.BlockSpec((N_HEADS, tm), lambda i: (0, i))],
        out_shape=[jax.ShapeDtypeStruct((t_all, D_MODEL), BF16), jax.ShapeDtypeStruct((t_all, 3 * WIDTH), BF16),
                   jax.ShapeDtypeStruct((t_all, 3 * WIDTH), BF16), _class_shape(t_all, seq, 4, 3 * WIDTH, BF16),
                   _class_shape(t_all, seq, 16, 3 * WIDTH, BF16),
                   jax.ShapeDtypeStruct((t_all // seq * WIDTH, seq), BF16),
                   jax.ShapeDtypeStruct((N_HEADS, t_all), F32)],
        compiler_params=_params(),
    )(x, ada3, pos, wqkv, wf16, freq, perms[0], perms[1])


def _chunk_rows(a_t, seq):
    t_all = a_t.shape[1]
    return a_t.reshape(N_HEADS, t_all // seq, seq // LANES, LANES).transpose(1, 0, 2, 3).reshape(-1, LANES)


def _unchunk_rows(a, seq):
    nbat = a.shape[0] * LANES // (N_HEADS * seq)
    return a.reshape(nbat, N_HEADS, seq // LANES, LANES).transpose(1, 0, 2, 3).reshape(N_HEADS, nbat * seq)


def _chunk_carry(tot, nchunk, later):
    rows = tot.shape[0]
    r = lax.broadcasted_iota(jnp.int32, (rows, rows), 0)
    c = lax.broadcasted_iota(jnp.int32, (rows, rows), 1)
    sel = ((r // nchunk) == (c // nchunk)) & ((c > r) if later else (c < r))
    mat = sel.astype(BF16)
    return sum(_nn(mat, part) for part in _split3(jnp.broadcast_to(tot, (rows, LANES))))


def _fgate_fwd(fa_t, bf, seq):
    x = _chunk_rows(fa_t, seq)
    rows = x.shape[0]
    nchunk = seq // LANES
    bias = jnp.broadcast_to(bf.reshape(1, N_HEADS, 1), (rows // (N_HEADS * nchunk), N_HEADS, nchunk)).reshape(rows, 1)

    def body(x_ref, b_ref, f_ref):
        lane = lax.broadcasted_iota(jnp.int32, (rows, LANES), 1)
        xv = x_ref[...] + b_ref[...]
        lf = jnp.minimum(xv, 0.0) - jnp.log(1.0 + jnp.exp(-jnp.abs(xv)))
        for s in (1, 2, 4, 8, 16, 32, 64):
            lf = lf + jnp.where(lane >= s, pltpu.roll(lf, s, 1), 0.0)
        f_ref[...] = lf + _chunk_carry(lf[:, LANES - 1:LANES], nchunk, False)

    whole = lambda a: pl.BlockSpec(a.shape, lambda i: (0, 0))
    out = pl.pallas_call(
        body, name="fgate_fwd", grid=(1,), in_specs=[whole(x), whole(bias)], out_specs=whole(x),
        out_shape=jax.ShapeDtypeStruct(x.shape, F32), compiler_params=_params(),
    )(x, bias)
    return _unchunk_rows(out, seq)


def _fgate_bwd(df_t, fa_t, bf, seq):
    d_in = _chunk_rows(df_t, seq)
    x = _chunk_rows(fa_t, seq)
    rows = x.shape[0]
    nchunk = seq // LANES
    bias = jnp.broadcast_to(bf.reshape(1, N_HEADS, 1), (rows // (N_HEADS * nchunk), N_HEADS, nchunk)).reshape(rows, 1)

    def body(d_ref, x_ref, b_ref, o_ref, s_ref):
        lane = lax.broadcasted_iota(jnp.int32, (rows, LANES), 1)
        d = d_ref[...]
        for s in (1, 2, 4, 8, 16, 32, 64):
            d = d + jnp.where(lane < LANES - s, pltpu.roll(d, LANES - s, 1), 0.0)
        d = d + _chunk_carry(d[:, 0:1], nchunk, True)
        dfa = d * jax.nn.sigmoid(-(x_ref[...] + b_ref[...]))
        o_ref[...] = dfa
        g = lax.broadcasted_iota(jnp.int32, (2 * N_HEADS, rows), 0)
        r = lax.broadcasted_iota(jnp.int32, (2 * N_HEADS, rows), 1)
        group = (((r // nchunk) % N_HEADS) == g).astype(BF16)
        per_head = sum(_nn(group, part) for part in _split3(dfa))[:N_HEADS]
        s_ref[...] = jnp.broadcast_to(jnp.sum(per_head, axis=1, keepdims=True), (N_HEADS, LANES))

    whole = lambda a: pl.BlockSpec(a.shape, lambda i: (0, 0))
    dfa, sums = pl.pallas_call(
        body, name="fgate_bwd", grid=(1,), in_specs=[whole(d_in), whole(x), whole(bias)],
        out_specs=[whole(x), pl.BlockSpec((N_HEADS, LANES), lambda i: (0, 0))],
        out_shape=[jax.ShapeDtypeStruct(x.shape, F32), jax.ShapeDtypeStruct((N_HEADS, LANES), F32)],
        compiler_params=_params(),
    )(d_in, x, bias)
    return _unchunk_rows(dfa, seq), sums


FOX_T = 256


def _fox_prep(dst, src_ref, lo, hi):
    for p in range(4):
        v = src_ref[:, LANES * p:LANES * (p + 1)]
        dst[2 * p] = jnp.where(lo, v, jnp.zeros_like(v))
        dst[2 * p + 1] = jnp.where(hi, v, jnp.zeros_like(v))


def _fox_fwd(za, vt, f_col, seq, shards):
    t_all = za.shape[0]
    tq = FOX_T
    nq = seq // tq
    nbat = t_all // seq
    n = len(shards)
    to_all = [True] * n

    def body(*refs):
        q_ref, k_ref, vt_ref, fc_ref = refs[:4]
        o_ref, lse_ref = refs[4 + n:6 + n]
        qm_sc, m_sc, l_sc, acc_sc, a_sc, st_sc, pe_sc = refs[6 + 2 * n:13 + 2 * n]
        comm = (refs[4:4 + n], refs[6 + n:6 + 2 * n], to_all, refs[13 + 2 * n:])
        i = pl.program_id(1)

        @pl.when((pl.program_id(0) == 0) & (i == 0))
        def _():
            for cp in _comm_copies(*comm):
                cp.start()
        lo, hi = _half_masks(tq)
        r = lax.broadcasted_iota(jnp.int32, (tq, tq), 0)
        c = lax.broadcasted_iota(jnp.int32, (tq, tq), 1)
        tri = c >= r
        _fox_prep(qm_sc, q_ref, lo, hi)
        m_sc[...] = jnp.full(m_sc.shape, NEG, F32)
        l_sc[...] = jnp.zeros_like(l_sc)
        acc_sc[...] = jnp.zeros_like(acc_sc)

        def block(j, masked):
            sl = pl.ds(pl.multiple_of(j * tq, tq), tq)
            for p in range(4):
                kj = k_ref[sl, LANES * p:LANES * (p + 1)]
                for h in (2 * p, 2 * p + 1):
                    st = _nt(kj, qm_sc[h]) - fc_ref[sl, h:h + 1]
                    st_sc[h] = jnp.where(tri, st, NEG) if masked else st
            for h in range(N_HEADS):
                st = st_sc[h]
                m = m_sc[h:h + 1, :]
                mn = jnp.maximum(m, jnp.max(st, axis=0, keepdims=True))
                a = jnp.exp2(m - mn)
                pe = jnp.exp2(st - mn)
                m_sc[h:h + 1, :] = mn
                a_sc[h:h + 1, :] = a
                l_sc[h:h + 1, :] = a * l_sc[h:h + 1, :] + jnp.sum(pe, axis=0, keepdims=True)
                pe_sc[h] = pe.astype(BF16)
            for h in range(N_HEADS):
                acc_sc[h] = a_sc[h:h + 1, :] * acc_sc[h] + _nn(vt_ref[HEAD_DIM * h:HEAD_DIM * (h + 1), sl], pe_sc[h])

        def step(j, carry):
            block(j, False)
            return carry

        lax.fori_loop(0, i, step, 0)
        block(i, True)
        lse_ref[...] = m_sc[...] + jnp.log(l_sc[...]) * LOG2E
        for p in range(4):
            ot = jnp.concatenate([acc_sc[h] / l_sc[h:h + 1, :] for h in (2 * p, 2 * p + 1)], axis=0)
            o_ref[:, LANES * p:LANES * (p + 1)] = ot.T

        @pl.when((pl.program_id(0) == nbat - 1) & (i == nq - 1))
        def _():
            for cp in _comm_copies(*comm):
                cp.wait()

    res = pl.pallas_call(
        body, name="fox_fwd", grid=(nbat, nq),
        in_specs=[pl.BlockSpec((tq, WIDTH), lambda b, i: (b * nq + i, 0)),
                  pl.BlockSpec((seq, WIDTH), lambda b, i: (b, 1)), pl.BlockSpec((WIDTH, seq), lambda b, i: (b, 0)),
                  pl.BlockSpec((seq, LANES), lambda b, i: (b, 0))] + [ANY] * n,
        out_specs=[pl.BlockSpec((tq, WIDTH), lambda b, i: (b * nq + i, 0)),
                   pl.BlockSpec((N_HEADS, tq), lambda b, i: (0, b * nq + i))] + [ANY] * n,
        out_shape=[jax.ShapeDtypeStruct((t_all, WIDTH), F32), jax.ShapeDtypeStruct((N_HEADS, t_all), F32)]
        + _comm_out_shapes(shards, to_all),
        scratch_shapes=[pltpu.VMEM((N_HEADS, tq, LANES), BF16), pltpu.VMEM((N_HEADS, tq), F32),
                        pltpu.VMEM((N_HEADS, tq), F32), pltpu.VMEM((N_HEADS, HEAD_DIM, tq), F32),
                        pltpu.VMEM((N_HEADS, tq), F32), pltpu.VMEM((N_HEADS, tq, tq), F32),
                        pltpu.VMEM((N_HEADS, tq, tq), BF16)] + _comm_sems(n),
        compiler_params=_params(),
    )(za, za, vt, f_col, *shards)
    return res[0], res[1], res[2:]


def _fox_bwd(za, do, f_col, lse_row, dl_row, seq, grads):
    t_all = za.shape[0]
    tk = FOX_T
    nk = seq // tk
    nbat = t_all // seq
    n = len(grads)
    to_all = [False] * n

    def body(*refs):
        k_ref, v_ref, q_ref, do_ref, fc_ref, lr_ref, dr_ref = refs[:7]
        dk_ref, dv_ref, df_ref, dqt_ref, dfq_ref = refs[7 + n:12 + n]
        km_sc, vm_sc, fk_sc, dk_sc, dv_sc, cs_sc, kt_sc, st_sc, dp_sc, pt_sc, ds_sc = refs[12 + 2 * n:23 + 2 * n]
        comm = (refs[7:7 + n], refs[12 + n:12 + 2 * n], to_all, refs[23 + 2 * n:])
        j = pl.program_id(1)

        @pl.when(j == 0)
        def _():
            dqt_ref[...] = jnp.zeros_like(dqt_ref)
            dfq_ref[...] = jnp.zeros_like(dfq_ref)

        @pl.when((pl.program_id(0) == 0) & (j == 0))
        def _():
            for cp in _comm_copies(*comm):
                cp.start()
        lo, hi = _half_masks(tk)
        r = lax.broadcasted_iota(jnp.int32, (tk, tk), 0)
        c = lax.broadcasted_iota(jnp.int32, (tk, tk), 1)
        tri = c >= r
        _fox_prep(km_sc, k_ref, lo, hi)
        _fox_prep(vm_sc, v_ref, lo, hi)
        for h in range(N_HEADS):
            fk_sc[h] = jnp.broadcast_to(fc_ref[:, h:h + 1], (tk, tk))
        for p in range(4):
            kt_sc[p] = k_ref[:, LANES * p:LANES * (p + 1)].astype(F32).T.astype(BF16)
        dk_sc[...] = jnp.zeros_like(dk_sc)
        dv_sc[...] = jnp.zeros_like(dv_sc)
        cs_sc[...] = jnp.zeros_like(cs_sc)

        def block(i, masked):
            sl = pl.ds(pl.multiple_of(i * tk, tk), tk)
            for p in range(4):
                cs = slice(LANES * p, LANES * (p + 1))
                qi = q_ref[sl, cs]
                doi = do_ref[sl, cs]
                for h in (2 * p, 2 * p + 1):
                    st = _nt(km_sc[h], qi) - fk_sc[h] - lr_ref[h:h + 1, sl]
                    st_sc[h] = jnp.where(tri, st, NEG) if masked else st
                    dp_sc[h] = _nt(vm_sc[h], doi) - dr_ref[h:h + 1, sl]
            for h in range(N_HEADS):
                pt = jnp.exp2(st_sc[h])
                dst = pt * dp_sc[h]
                pt_sc[h] = pt.astype(BF16)
                ds_sc[h] = dst.astype(BF16)
                cs_sc[h] += dst[:, :LANES] + dst[:, LANES:]
                dfq_ref[h:h + 1, sl] += jnp.sum(dst, axis=0, keepdims=True)
            for p in range(4):
                cs = slice(LANES * p, LANES * (p + 1))
                qi = q_ref[sl, cs]
                doi = do_ref[sl, cs]
                for h in (2 * p, 2 * p + 1):
                    dv_sc[h] += _nn(pt_sc[h], doi)
                    dk_sc[h] += _nn(ds_sc[h], qi)
                    kt = kt_sc[p, HEAD_DIM * (h % 2):HEAD_DIM * (h % 2 + 1), :]
                    dqt_ref[HEAD_DIM * h:HEAD_DIM * (h + 1), sl] += _nn(kt, ds_sc[h])

        def step(i, carry):
            block(i, False)
            return carry

        block(j, True)
        lax.fori_loop(j + 1, nk, step, 0)
        df_ref[...] = jnp.zeros_like(df_ref)
        for p in range(4):
            cs = slice(LANES * p, LANES * (p + 1))
            dk_ref[:, cs] = (jnp.where(lo, dk_sc[2 * p], dk_sc[2 * p + 1]) * (1.0 / LOG2E)).astype(BF16)
            dv_ref[:, cs] = jnp.where(lo, dv_sc[2 * p], dv_sc[2 * p + 1]).astype(BF16)
            for h in (2 * p, 2 * p + 1):
                df_ref[:, h:h + 1] = -jnp.sum(cs_sc[h], axis=1, keepdims=True)

        @pl.when(j == nk - 1)
        def _():
            dqt_ref[...] = dqt_ref[...] * Q_SCALE

        @pl.when((pl.program_id(0) == nbat - 1) & (j == nk - 1))
        def _():
            for cp in _comm_copies(*comm):
                cp.wait()

    tile = lambda w, col: pl.BlockSpec((tk, w), lambda b, j: (b * nk + j, col))
    full = lambda col: pl.BlockSpec((seq, WIDTH), lambda b, j: (b, col))
    row = pl.BlockSpec((N_HEADS, seq), lambda b, j: (0, b))
    acc = pltpu.VMEM((N_HEADS, tk, LANES), F32)
    res = pl.pallas_call(
        body, name="fox_bwd", grid=(nbat, nk),
        in_specs=[tile(WIDTH, 1), tile(WIDTH, 2), full(0), full(0), tile(LANES, 0), row, row] + [ANY] * n,
        out_specs=[tile(WIDTH, 0), tile(WIDTH, 0), tile(LANES, 0), pl.BlockSpec((WIDTH, seq), lambda b, j: (b, 0)),
                   row] + [ANY] * n,
        out_shape=[jax.ShapeDtypeStruct((t_all, WIDTH), BF16), jax.ShapeDtypeStruct((t_all, WIDTH), BF16),
                   jax.ShapeDtypeStruct((t_all, LANES), F32), jax.ShapeDtypeStruct((nbat * WIDTH, seq), F32),
                   jax.ShapeDtypeStruct((N_HEADS, t_all), F32)] + _comm_out_shapes(grads, to_all),
        scratch_shapes=[pltpu.VMEM((N_HEADS, tk, LANES), BF16), pltpu.VMEM((N_HEADS, tk, LANES), BF16),
                        pltpu.VMEM((N_HEADS, tk, tk), F32), acc, acc, acc, pltpu.VMEM((4, LANES, tk), BF16),
                        pltpu.VMEM((N_HEADS, tk, tk), F32), pltpu.VMEM((N_HEADS, tk, tk), F32),
                        pltpu.VMEM((N_HEADS, tk, tk), BF16), pltpu.VMEM((N_HEADS, tk, tk), BF16)]
        + _comm_sems(n),
        compiler_params=_params(),
    )(za, za, za, do, f_col, lse_row, dl_row, *grads)
    return res[0], res[1], res[2], res[3], res[4], res[5:]


DIL_SUB = 4


def _dil_mask(has_prev):
    qi = lax.broadcasted_iota(jnp.int32, (BLK, 2 * BLK), 0)
    kj = lax.broadcasted_iota(jnp.int32, (BLK, 2 * BLK), 1)
    dist = qi + BLK - kj
    band = (dist >= 0) & (dist <= BLK)
    return band if has_prev is True else band & ((kj >= BLK) | has_prev)


def _dil_geometry(t_all, seq, d, max_sub=DIL_SUB):
    length = seq // d
    nbs = length // BLK
    sub = min(max_sub, nbs)
    spb = nbs // sub
    tile = lambda width, col: pl.BlockSpec((BLK * sub, width), lambda s: (s, col))
    whole = lambda width, col: pl.BlockSpec((length, width), lambda s: (s // spb, col))
    return nbs, sub, spb, t_all // (BLK * sub), tile, whole


def _blk(i):
    return pl.ds(pl.multiple_of(i * BLK, BLK), BLK)


def _dil_fwd(zb, seq, d):
    t_all = zb.shape[0]
    nbs, sub, spb, steps, tile, whole = _dil_geometry(t_all, seq, d)

    def body(q_ref, k_ref, v_ref, o_ref, lse_ref, s_sc, p_sc):
        first = (pl.program_id(0) % spb) * sub
        lo, hi = _half_masks(BLK)
        lse_ref[...] = jnp.zeros_like(lse_ref)
        for j in range(sub):
            blk = first + j
            mask = _dil_mask(blk != 0 if j == 0 else True)
            for p in range(4):
                cs = slice(LANES * p, LANES * (p + 1))
                qp = q_ref[BLK * j:BLK * (j + 1), cs]
                kcat = jnp.concatenate([k_ref[_blk(jnp.maximum(blk - 1, 0)), cs], k_ref[_blk(blk), cs]], axis=0)
                for e in (0, 1):
                    qe = jnp.where(lo if e == 0 else hi, qp, jnp.zeros_like(qp))
                    s_sc[N_HEADS * j + 2 * p + e] = jnp.where(mask, _nt(qe, kcat), NEG)
        inv = []
        for i in range(N_HEADS * sub):
            s = s_sc[i]
            m = jnp.max(s, axis=1, keepdims=True)
            pe = jnp.exp2(s - m)
            l = jnp.sum(pe, axis=1, keepdims=True)
            p_sc[i] = pe.astype(BF16)
            inv.append(1.0 / l)
            j, h = divmod(i, N_HEADS)
            lse_ref[BLK * j:BLK * (j + 1), h:h + 1] = m + jnp.log(l) * LOG2E
        for j in range(sub):
            blk = first + j
            for p in range(4):
                cs = slice(LANES * p, LANES * (p + 1))
                vcat = jnp.concatenate([v_ref[_blk(jnp.maximum(blk - 1, 0)), cs], v_ref[_blk(blk), cs]], axis=0)
                res = [_nn(p_sc[N_HEADS * j + h], vcat) * inv[N_HEADS * j + h] for h in (2 * p, 2 * p + 1)]
                o_ref[BLK * j:BLK * (j + 1), cs] = jnp.where(lo, res[0], res[1]).astype(BF16)

    return pl.pallas_call(
        body, name=f"dil_fwd_{d}", grid=(steps,), in_specs=[tile(WIDTH, 0), whole(WIDTH, 1), whole(WIDTH, 2)],
        out_specs=[tile(WIDTH, 0), tile(LANES, 0)],
        out_shape=[jax.ShapeDtypeStruct((t_all, WIDTH), BF16), jax.ShapeDtypeStruct((t_all, LANES), F32)],
        scratch_shapes=[pltpu.VMEM((N_HEADS * sub, BLK, 2 * BLK), F32),
                        pltpu.VMEM((N_HEADS * sub, BLK, 2 * BLK), BF16)],
        compiler_params=_params(),
    )(zb, zb, zb)


def _dil_bwd(zb, do, lse, dl, seq, d):
    t_all = zb.shape[0]
    length = seq // d
    nbs, sub, spb, steps, tile, whole = _dil_geometry(t_all, seq, d, 2 if length >= 4096 else DIL_SUB)

    def body(k_ref, v_ref, q_ref, do_ref, lse_ref, dl_ref, dq_ref, dk_ref, dv_ref, s_sc, dp_sc, pt_sc, ds_sc, kt_sc,
             dqt_sc):
        step = pl.program_id(0) % spb
        first = step * sub

        @pl.when(step == 0)
        def _():
            dqt_sc[...] = jnp.zeros_like(dqt_sc)
        r = lax.broadcasted_iota(jnp.int32, (BLK, 2 * BLK), 0)
        c = lax.broadcasted_iota(jnp.int32, (BLK, 2 * BLK), 1)
        same = (c < BLK) & (c >= r)
        later = (c >= BLK) & (c - BLK <= r)
        lo, hi = _half_masks(BLK)
        for j in range(sub):
            blk = first + j
            rows = slice(BLK * j, BLK * (j + 1))
            nxt = _blk(jnp.minimum(blk + 1, nbs - 1))
            mask = same | (later & (blk + 1 != nbs)) if j == sub - 1 else same | later
            lrows = jnp.concatenate([lse_ref[_blk(blk), :].T, lse_ref[nxt, :].T], axis=1)
            erows = jnp.concatenate([dl_ref[_blk(blk), :].T, dl_ref[nxt, :].T], axis=1)
            for p in range(4):
                cs = slice(LANES * p, LANES * (p + 1))
                kp = k_ref[rows, cs]
                vp = v_ref[rows, cs]
                kt_sc[4 * j + p] = kp.astype(F32).T.astype(BF16)
                qcat = jnp.concatenate([q_ref[_blk(blk), cs], q_ref[nxt, cs]], axis=0)
                dcat = jnp.concatenate([do_ref[_blk(blk), cs], do_ref[nxt, cs]], axis=0)
                for e in (0, 1):
                    h = 2 * p + e
                    sel = lo if e == 0 else hi
                    ke = jnp.where(sel, kp, jnp.zeros_like(kp))
                    ve = jnp.where(sel, vp, jnp.zeros_like(vp))
                    s_sc[N_HEADS * j + h] = jnp.where(mask, _nt(ke, qcat) - lrows[h:h + 1, :], NEG)
                    dp_sc[N_HEADS * j + h] = _nt(ve, dcat) - erows[h:h + 1, :]
        for i in range(N_HEADS * sub):
            pt = jnp.exp2(s_sc[i])
            pt_sc[i] = pt.astype(BF16)
            ds_sc[i] = (pt * dp_sc[i]).astype(BF16)
        for j in range(sub):
            blk = first + j
            rows = slice(BLK * j, BLK * (j + 1))
            nxt = _blk(jnp.minimum(blk + 1, nbs - 1))
            cols = pl.ds(pl.multiple_of(blk * BLK, BLK), 2 * BLK)
            for p in range(4):
                cs = slice(LANES * p, LANES * (p + 1))
                qcat = jnp.concatenate([q_ref[_blk(blk), cs], q_ref[nxt, cs]], axis=0)
                dcat = jnp.concatenate([do_ref[_blk(blk), cs], do_ref[nxt, cs]], axis=0)
                i = N_HEADS * j + 2 * p
                dk_ref[rows, cs] = (jnp.where(lo, _nn(ds_sc[i], qcat), _nn(ds_sc[i + 1], qcat))
                                    * (1.0 / LOG2E)).astype(BF16)
                dv_ref[rows, cs] = jnp.where(lo, _nn(pt_sc[i], dcat), _nn(pt_sc[i + 1], dcat)).astype(BF16)
                for e in (0, 1):
                    kt = kt_sc[4 * j + p, HEAD_DIM * e:HEAD_DIM * (e + 1), :]
                    dqt_sc[HEAD_DIM * (2 * p + e):HEAD_DIM * (2 * p + e + 1), cols] += _nn(kt, ds_sc[i + e])

        @pl.when(step == spb - 1)
        def _():
            for p in range(4):
                cs = slice(LANES * p, LANES * (p + 1))
                dq_ref[:, cs] = (dqt_sc[cs, 0:length].T * Q_SCALE).astype(BF16)

    wide = pltpu.VMEM((N_HEADS * sub, BLK, 2 * BLK), F32)
    half = pltpu.VMEM((N_HEADS * sub, BLK, 2 * BLK), BF16)
    return pl.pallas_call(
        body, name=f"dil_bwd_{d}", grid=(steps,),
        in_specs=[tile(WIDTH, 1), tile(WIDTH, 2), whole(WIDTH, 0), whole(WIDTH, 0), whole(LANES, 0), whole(LANES, 0)],
        out_specs=[whole(WIDTH, 0), tile(WIDTH, 0), tile(WIDTH, 0)],
        out_shape=[jax.ShapeDtypeStruct((t_all, WIDTH), BF16)] * 3,
        scratch_shapes=[wide, wide, half, half, pltpu.VMEM((4 * sub, LANES, BLK), BF16),
                        pltpu.VMEM((WIDTH, length + BLK), F32)],
        compiler_params=_params(),
    )(zb, zb, zb, do, lse, dl)


def _mix_out(oa, o3, l3, gn_a, gn_b, w_out, x, ada3, ln_g, ln_b, perms, seq):
    t_all = x.shape[0]
    tm = TOK_TM
    nts = seq // tm

    def body(oa_ref, o1_ref, o2_ref, o3_ref, l1_ref, l2_ref, l3_ref, ga_ref, gb_ref, w_ref, x_ref, ada_ref, g_ref,
             b_ref, p4_ref, p16_ref, pt4_ref, pt16_ref, ob_ref, lse_ref, lse4_ref, lse16_ref, mg_ref, mix_ref, xh_ref,
             rs_ref, h2_ref, h2t_ref):
        e, et = _head_mats()
        la = l1_ref[...]
        lb = _permute_f32(pt4_ref[...], _load_classes(l2_ref, 4))
        lc = _permute_f32(pt16_ref[...], _load_classes(l3_ref, 16))
        mx = jnp.maximum(jnp.maximum(la, lb), lc)
        ea, eb, ec = jnp.exp2(la - mx), jnp.exp2(lb - mx), jnp.exp2(lc - mx)
        tot = ea + eb + ec
        lse = mx + jnp.log(tot) * LOG2E
        lse_ref[...] = lse
        _store_classes(lse4_ref, _permute_f32(p4_ref[...], lse), 4)
        _store_classes(lse16_ref, _permute_f32(p16_ref[...], lse), 16)
        ob = (o1_ref[...].astype(F32) * _hexp(ea / tot, e)
              + _nn(pt4_ref[...], _load_classes(o2_ref, 4)) * _hexp(eb / tot, e)
              + _nn(pt16_ref[...], _load_classes(o3_ref, 16)) * _hexp(ec / tot, e))
        ob_ref[...] = ob

        def rms(o, gain):
            rr = lax.rsqrt(_hsum(o * o, et) * (1.0 / HEAD_DIM) + RMS_EPS)
            return o * _hexp(rr, e) * gain

        merged = jnp.concatenate([rms(oa_ref[...], ga_ref[...]), rms(ob, gb_ref[...])], axis=1).astype(BF16)
        mg_ref[...] = merged
        mix = _nn(merged, w_ref[...])
        mix_ref[...] = mix.astype(BF16)
        r1 = ALPHA * x_ref[...] + ada_ref[0, 2:3, :] * mix
        d = r1 - jnp.mean(r1, axis=1, keepdims=True)
        rstd = lax.rsqrt(jnp.mean(d * d, axis=1, keepdims=True) + LN_EPS)
        xh = d * rstd
        xh_ref[...] = xh
        rs_ref[...] = jnp.broadcast_to(rstd, (tm, LANES))
        x1 = xh * g_ref[...] + b_ref[...]
        h2 = x1 * (1.0 + ada_ref[0, 4:5, :]) + ada_ref[0, 3:4, :]
        h2_ref[...] = h2.astype(BF16)
        h2t_ref[0] = h2.T.astype(BF16)

    tok = lambda w: pl.BlockSpec((tm, w), lambda i: (i, 0))
    vec = lambda w: pl.BlockSpec((1, w), lambda i: (0, 0))
    whole = lambda a: pl.BlockSpec(a.shape, lambda i: (0, 0))
    classes = lambda a, d: a.reshape(t_all // seq * d, seq // d, a.shape[-1])
    return pl.pallas_call(
        body, name="mix_out", grid=(t_all // tm,),
        in_specs=[tok(WIDTH), tok(WIDTH), _class_spec(4, WIDTH, nts), _class_spec(16, WIDTH, nts), tok(LANES),
                  _class_spec(4, LANES, nts), _class_spec(16, LANES, nts), vec(WIDTH), vec(WIDTH), whole(w_out),
                  tok(D_MODEL), pl.BlockSpec((1, 6, D_MODEL), lambda i: (i // nts, 0, 0)), vec(D_MODEL), vec(D_MODEL)]
        + [whole(p) for p in perms],
        out_specs=[tok(WIDTH), tok(LANES), _class_spec(4, LANES, nts), _class_spec(16, LANES, nts), tok(D_MODEL),
                   tok(D_MODEL), tok(D_MODEL), tok(LANES), tok(D_MODEL), pl.BlockSpec((1, D_MODEL, tm), lambda i: (i // (FFN_TM // tm), 0, i % (FFN_TM // tm)))],
        out_shape=[jax.ShapeDtypeStruct((t_all, WIDTH), F32), jax.ShapeDtypeStruct((t_all, LANES), F32),
                   _class_shape(t_all, seq, 4, LANES, F32), _class_shape(t_all, seq, 16, LANES, F32),
                   jax.ShapeDtypeStruct((t_all, D_MODEL), BF16), jax.ShapeDtypeStruct((t_all, D_MODEL), BF16),
                   jax.ShapeDtypeStruct((t_all, D_MODEL), F32), jax.ShapeDtypeStruct((t_all, LANES), F32),
                   jax.ShapeDtypeStruct((t_all, D_MODEL), BF16),
                   jax.ShapeDtypeStruct((t_all // FFN_TM, D_MODEL, FFN_TM), BF16)],
        compiler_params=_params(),
    )(oa, o3[0], classes(o3[1], 4), classes(o3[2], 16), l3[0], classes(l3[1], 4), classes(l3[2], 16), gn_a, gn_b,
      w_out, x, ada3, ln_g, ln_b, *perms)


def _mix_out_bwd(dmix, w_out, oa, ob, gn_a, gn_b, perms, seq):
    t_all = dmix.shape[0]
    tm = TOK_TM
    nts = seq // tm

    def body(dm_ref, w_ref, oa_ref, ob_ref, ga_ref, gb_ref, p4_ref, p16_ref, doa_ref, dob_ref, dob4_ref, dob16_ref,
             dla_ref, dlb_ref, dlb4_ref, dlb16_ref, acc_ref):
        @pl.when(pl.program_id(0) == 0)
        def _():
            acc_ref[...] = jnp.zeros_like(acc_ref)
        e, et = _head_mats()
        dmg = _nt(dm_ref[...], w_ref[...])

        def group(o, dn, gain):
            rr = lax.rsqrt(_hsum(o * o, et) * (1.0 / HEAD_DIM) + RMS_EPS)
            re = _hexp(rr, e)
            dgain = jnp.sum(dn * o * re, axis=0, keepdims=True)
            dxn = dn * gain
            tt = _hsum(dxn * o, et) * (rr * rr * rr) * (1.0 / HEAD_DIM)
            do = re * dxn - o * _hexp(tt, e)
            return do, _hsum(do * o, et), dgain

        doa, dla, dga = group(oa_ref[...], dmg[:, :WIDTH], ga_ref[...])
        dob, dlb, dgb = group(ob_ref[...], dmg[:, WIDTH:], gb_ref[...])
        dob = dob.astype(BF16)
        doa_ref[...] = doa.astype(BF16)
        dob_ref[...] = dob
        _store_classes(dob4_ref, _nn(p4_ref[...], dob).astype(BF16), 4)
        _store_classes(dob16_ref, _nn(p16_ref[...], dob).astype(BF16), 16)
        dla_ref[...] = dla
        dlb_ref[...] = dlb
        _store_classes(dlb4_ref, _permute_f32(p4_ref[...], dlb), 4)
        _store_classes(dlb16_ref, _permute_f32(p16_ref[...], dlb), 16)
        acc_ref[0:1, :] += jnp.concatenate([dga, dgb], axis=1)

    tok = lambda w: pl.BlockSpec((tm, w), lambda i: (i, 0))
    vec = lambda w: pl.BlockSpec((1, w), lambda i: (0, 0))
    return pl.pallas_call(
        body, name="mix_out_bwd", grid=(t_all // tm,),
        in_specs=[tok(D_MODEL), pl.BlockSpec(w_out.shape, lambda i: (0, 0)), tok(WIDTH), tok(WIDTH), vec(WIDTH),
                  vec(WIDTH), pl.BlockSpec(perms[0].shape, lambda i: (0, 0)),
                  pl.BlockSpec(perms[1].shape, lambda i: (0, 0))],
        out_specs=[tok(WIDTH), tok(WIDTH), _class_spec(4, WIDTH, nts), _class_spec(16, WIDTH, nts), tok(LANES),
                   tok(LANES), _class_spec(4, LANES, nts), _class_spec(16, LANES, nts),
                   pl.BlockSpec((8, D_MODEL), lambda i: (0, 0))],
        out_shape=[jax.ShapeDtypeStruct((t_all, WIDTH), BF16), jax.ShapeDtypeStruct((t_all, WIDTH), BF16),
                   _class_shape(t_all, seq, 4, WIDTH, BF16), _class_shape(t_all, seq, 16, WIDTH, BF16),
                   jax.ShapeDtypeStruct((t_all, LANES), F32), jax.ShapeDtypeStruct((t_all, LANES), F32),
                   _class_shape(t_all, seq, 4, LANES, F32), _class_shape(t_all, seq, 16, LANES, F32),
                   jax.ShapeDtypeStruct((8, D_MODEL), F32)],
        compiler_params=_params(),
    )(dmix, w_out, oa, ob, gn_a, gn_b, perms[0], perms[1])


def _inproj_bwd(dqt, dka, dva, dil1, dil4, dil16, dfa16, pos, wqkv, wf16, freq, perms, dr1, x, ada3, seq):
    t_all = x.shape[0]
    tm = TOK_TM
    nts = seq // tm

    def body(dqt_ref, dka_ref, dva_ref, q1_ref, k1_ref, v1_ref, q4_ref, k4_ref, v4_ref, q16_ref, k16_ref, v16_ref,
             dfa_ref, pos_ref, w_ref, wf_ref, fr_ref, pt4_ref, pt16_ref, dr1_ref, x_ref, ada_ref, gx_ref, dz_ref,
             acc_ref):
        i = pl.program_id(0)

        @pl.when(i == 0)
        def _():
            acc_ref[...] = jnp.zeros_like(acc_ref)
        tabs = _rope_tabs(pos_ref, fr_ref, -1.0)
        dz_ref[:, :WIDTH] = dqt_ref[...].T.astype(BF16)
        dz_ref[:, WIDTH:2 * WIDTH] = dka_ref[...]
        dz_ref[:, 2 * WIDTH:3 * WIDTH] = dva_ref[...]
        for t, (n1, n4, n16) in enumerate(((q1_ref, q4_ref, q16_ref), (k1_ref, k4_ref, k16_ref),
                                           (v1_ref, v4_ref, v16_ref))):
            tot = (n1[...].astype(F32) + _nn(pt4_ref[...], _load_classes(n4, 4))
                   + _nn(pt16_ref[...], _load_classes(n16, 16)))
            if t < 2:
                tot = _rope(tot, tabs)
            dz_ref[:, (3 + t) * WIDTH:(4 + t) * WIDTH] = tot.astype(BF16)
        dh1 = _tn(dfa_ref[...], wf_ref[...])
        for n in range(6):
            cs = slice(n * WIDTH, (n + 1) * WIDTH)
            dh1 = dh1 + _nt(dz_ref[:, cs], w_ref[:, cs])
        xv = x_ref[...]
        gx_ref[...] = ALPHA * dr1_ref[...] + dh1 * (1.0 + ada_ref[0, 1:2, :])
        b = i // nts
        acc_ref[pl.ds(b, 1), :] += jnp.sum(dh1 * xv, axis=0, keepdims=True)
        acc_ref[pl.ds(8 + b, 1), :] += jnp.sum(dh1, axis=0, keepdims=True)

    tok = lambda w: pl.BlockSpec((tm, w), lambda i: (i, 0))
    whole = lambda a: pl.BlockSpec(a.shape, lambda i: (0, 0))
    classes = lambda a, d: a.reshape(t_all // seq * d, seq // d, a.shape[-1])
    return pl.pallas_call(
        body, name="inproj_bwd", grid=(t_all // tm,),
        in_specs=[pl.BlockSpec((WIDTH, tm), lambda i: (i // nts, i % nts)), tok(WIDTH), tok(WIDTH)]
        + [tok(WIDTH)] * 3 + [_class_spec(4, WIDTH, nts)] * 3 + [_class_spec(16, WIDTH, nts)] * 3
        + [pl.BlockSpec((16, tm), lambda i: (0, i)), tok(1), whole(wqkv), whole(wf16),
           pl.BlockSpec((1, LANES), lambda i: (0, 0)), whole(perms[2]), whole(perms[3]), tok(D_MODEL), tok(D_MODEL),
           pl.BlockSpec((1, 6, D_MODEL), lambda i: (i // nts, 0, 0))],
        out_specs=[tok(D_MODEL), tok(6 * WIDTH), pl.BlockSpec((16, D_MODEL), lambda i: (0, 0))],
        out_shape=[jax.ShapeDtypeStruct((t_all, D_MODEL), F32), jax.ShapeDtypeStruct((t_all, 6 * WIDTH), BF16),
                   jax.ShapeDtypeStruct((16, D_MODEL), F32)],
        compiler_params=_params(),
    )(dqt, dka, dva, *dil1, *[classes(a, 4) for a in dil4], *[classes(a, 16) for a in dil16], dfa16, pos, wqkv, wf16,
      freq, perms[2], perms[3], dr1, x, ada3)


FFN_TM = 1024
FFN_TN = 256
HALO = 8


FFN_CHUNK = 256


def _conv_params(cw_ref, cb_ref, n, tn):
    a = pl.ds(pl.multiple_of(n * tn, tn), tn)
    g = pl.ds(pl.multiple_of(D_FF + n * tn, tn), tn)
    return cw_ref[:, a], cw_ref[:, g], cb_ref[:, a], cb_ref[:, g]


def _conv(cat_ref, w_ref, b_ref, start, rows, halo=HALO):
    return (b_ref[...] + w_ref[0:1, :] * cat_ref[pl.ds(start + halo - 2, rows), :]
            + w_ref[1:2, :] * cat_ref[pl.ds(start + halo - 1, rows), :]
            + w_ref[2:3, :] * cat_ref[pl.ds(start + halo, rows), :])


def _ffn_up_gate(h2, w_up, conv_w, conv_b, seq):
    t_all = h2.shape[0]
    tm, tn = min(2 * FFN_TM, seq), FFN_TN
    nc = D_FF // tn
    nts = seq // tm
    pre = 16

    def body(h_ref, hp_ref, wua_ref, wug_ref, cw_ref, cb_ref, ua_ref, ug_ref, o_ref, ca_ref, cg_ref):
        first = (pl.program_id(1) % nts) == 0
        wa_ref, wg_ref, ba_ref, bg_ref = _conv_params(cw_ref, cb_ref, pl.program_id(0), tn)
        hcat = jnp.concatenate([hp_ref[...], h_ref[...]], axis=0)
        zero = jnp.zeros((pre, tn), F32)
        for w_ref, cat, u_ref in ((wua_ref, ca_ref, ua_ref), (wug_ref, cg_ref, ug_ref)):
            ub = _nn(hcat, w_ref[...]).astype(BF16)
            ue = ub.astype(F32)
            cat[0:pre, :] = jnp.where(first, zero, ue[0:pre])
            cat[pre:, :] = ue[pre:]
            u_ref[...] = ub[pre:]
        for c0 in range(0, tm, FFN_CHUNK):
            ya = _conv(ca_ref, wa_ref, ba_ref, c0, FFN_CHUNK, pre)
            yg = _conv(cg_ref, wg_ref, bg_ref, c0, FFN_CHUNK, pre)
            o_ref[c0:c0 + FFN_CHUNK, :] = (yg * jax.nn.sigmoid(yg) * ya).astype(BF16)

    wcol = lambda off: pl.BlockSpec((D_MODEL, tn), lambda n, t: (0, n + off))
    tile = pl.BlockSpec((tm, tn), lambda n, t: (t, n))
    return pl.pallas_call(
        body, name="ffn_up_gate", grid=(nc, t_all // tm),
        in_specs=[pl.BlockSpec((tm, D_MODEL), lambda n, t: (t, 0)),
                  pl.BlockSpec((pre, D_MODEL), lambda n, t: (jnp.maximum(t * (tm // pre) - 1, 0), 0)),
                  wcol(0), wcol(nc), pl.BlockSpec(conv_w.shape, lambda n, t: (0, 0)),
                  pl.BlockSpec(conv_b.shape, lambda n, t: (0, 0))],
        out_specs=[tile, tile, tile],
        out_shape=[jax.ShapeDtypeStruct((t_all, D_FF), BF16)] * 3,
        scratch_shapes=[pltpu.VMEM((tm + pre, tn), F32)] * 2, compiler_params=_params(),
    )(h2, h2, w_up, w_up, conv_w, conv_b)


def _ffn_gate_bwd(u_a, u_g, dfi, conv_w, conv_b, h2t, seq):
    t_all = u_a.shape[0]
    tm, tn = FFN_TM, FFN_TN
    nc = D_FF // tn
    nts = seq // tm

    def body(ua_ref, uap_ref, uan_ref, ug_ref, ugp_ref, ugn_ref, df_ref, dfn_ref, cw_ref, cb_ref, h_ref,
             dua_ref, dug_ref, acca_ref, accg_ref, dwa_ref, dwg_ref, ca_ref, cg_ref, ya_ref, yg_ref, dwa_sc, dwg_sc,
             out_sems):
        t = pl.program_id(0)
        n = pl.program_id(1)
        cols = pl.ds(pl.multiple_of(n * tn, tn), tn)
        first = (t % nts) == 0
        last = (t % nts) == nts - 1

        @pl.when((t == 0) & (n == 0))
        def _():
            acca_ref[...] = jnp.zeros_like(acca_ref)
            accg_ref[...] = jnp.zeros_like(accg_ref)
            dwa_sc[...] = jnp.zeros_like(dwa_sc)
            dwg_sc[...] = jnp.zeros_like(dwg_sc)
        wa_ref, wg_ref, ba_ref, bg_ref = _conv_params(cw_ref, cb_ref, n, tn)
        zero = jnp.zeros((HALO, tn), F32)
        for cat, cur, prv, nxt in ((ca_ref, ua_ref, uap_ref, uan_ref), (cg_ref, ug_ref, ugp_ref, ugn_ref)):
            cat[0:HALO, :] = jnp.where(first, zero, prv[...].astype(F32)[HALO:])
            cat[HALO:HALO + tm, :] = cur[...].astype(F32)
            cat[HALO + tm:, :] = nxt[...].astype(F32)[:HALO]
        ch = FFN_CHUNK
        sums = [[jnp.zeros((1, tn), F32) for _ in range(4)] for _ in range(2)]
        for ci, c0 in enumerate(range(0, tm, ch)):
            ya = _conv(ca_ref, wa_ref, ba_ref, c0, ch + HALO)
            yg = _conv(cg_ref, wg_ref, bg_ref, c0, ch + HALO)
            if c0 + ch < tm:
                beyond = df_ref[c0 + ch:c0 + ch + 16, :].astype(F32)[:HALO]
            else:
                beyond = jnp.where(last, 0.0, dfn_ref[...].astype(F32)[:HALO])
            dfe = jnp.concatenate([df_ref[c0:c0 + ch, :].astype(F32), beyond], axis=0)
            sg = jax.nn.sigmoid(yg)
            ya_ref[ci] = dfe * (yg * sg)
            yg_ref[ci] = dfe * ya * (sg * (1.0 + yg * (1.0 - sg)))
            for half, (dy, cat, w_ref, du_ref) in enumerate(((ya_ref, ca_ref, wa_ref, dua_ref),
                                                             (yg_ref, cg_ref, wg_ref, dug_ref))):
                d0 = dy[ci, 0:ch, :]
                du = (w_ref[2:3, :] * d0 + w_ref[1:2, :] * dy[ci, pl.ds(1, ch), :]
                      + w_ref[0:1, :] * dy[ci, pl.ds(2, ch), :])
                du_ref[c0:c0 + ch, :] = du.astype(BF16)
                for k in range(3):
                    sums[half][k] += jnp.sum(d0 * cat[pl.ds(c0 + HALO - 2 + k, ch), :], axis=0, keepdims=True)
                sums[half][3] += jnp.sum(d0, axis=0, keepdims=True)
        for half, acc in enumerate((acca_ref, accg_ref)):
            for k in range(4):
                acc[k:k + 1, cols] += sums[half][k]
        ht = h_ref[0]
        dwa_sc[:, cols] += _nn(ht, dua_ref[...])
        dwg_sc[:, cols] += _nn(ht, dug_ref[...])

        @pl.when((t == t_all // tm - 1) & (n == nc - 1))
        def _():
            copies = [pltpu.make_async_copy(dwa_sc, dwa_ref, out_sems.at[0]),
                      pltpu.make_async_copy(dwg_sc, dwg_ref, out_sems.at[1])]
            for cp in copies:
                cp.start()
            for cp in copies:
                cp.wait()

    nrow = t_all // 16
    cur = pl.BlockSpec((tm, tn), lambda t, n: (t, n))
    prev = pl.BlockSpec((16, tn), lambda t, n: (jnp.maximum(t * (tm // 16) - 1, 0), n))
    nxt = pl.BlockSpec((16, tn), lambda t, n: (jnp.minimum((t + 1) * (tm // 16), nrow - 1), n))
    acc = pl.BlockSpec((8, D_FF), lambda t, n: (0, 0))
    return pl.pallas_call(
        body, name="ffn_gate_bwd", grid=(t_all // tm, nc),
        in_specs=[cur, prev, nxt, cur, prev, nxt, cur, nxt, pl.BlockSpec(conv_w.shape, lambda t, n: (0, 0)),
                  pl.BlockSpec(conv_b.shape, lambda t, n: (0, 0)),
                  pl.BlockSpec((1, D_MODEL, tm), lambda t, n: (t, 0, 0))],
        out_specs=[cur, cur, acc, acc, ANY, ANY],
        out_shape=[jax.ShapeDtypeStruct((t_all, D_FF), BF16), jax.ShapeDtypeStruct((t_all, D_FF), BF16),
                   jax.ShapeDtypeStruct((8, D_FF), F32), jax.ShapeDtypeStruct((8, D_FF), F32),
                   jax.ShapeDtypeStruct((D_MODEL, D_FF), F32), jax.ShapeDtypeStruct((D_MODEL, D_FF), F32)],
        scratch_shapes=[pltpu.VMEM((tm + 2 * HALO, tn), F32)] * 2
        + [pltpu.VMEM((tm // FFN_CHUNK, FFN_CHUNK + HALO, tn), F32)] * 2
        + [pltpu.VMEM((D_MODEL, D_FF), F32)] * 2 + [pltpu.SemaphoreType.DMA((2,))],
        compiler_params=_params(),
    )(u_a, u_a, u_a, u_g, u_g, u_g, dfi, dfi, conv_w, conv_b, h2t)


def _ffn_down(ffn_in, w_down, xh1, ln1_g, ln1_b, ada3, ln2_g, ln2_b, target, seq):
    t_all = xh1.shape[0]
    tm = 512
    nts = seq // tm

    def body(f_ref, w_ref, xh_ref, g1_ref, b1_ref, ada_ref, g2_ref, b2_ref, tg_ref, dr2_ref, acc_ref, dffn_ref, dfi_ref):
        i = pl.program_id(0)

        @pl.when(i == 0)
        def _():
            acc_ref[...] = jnp.zeros_like(acc_ref)
        ffn = _nn(f_ref[...], w_ref[...])
        x1 = xh_ref[...] * g1_ref[...] + b1_ref[...]
        r2 = ALPHA * x1 + ada_ref[0, 5:6, :] * ffn
        d = r2 - jnp.mean(r2, axis=1, keepdims=True)
        rstd = lax.rsqrt(jnp.mean(d * d, axis=1, keepdims=True) + LN_EPS)
        xh2 = d * rstd
        diff = xh2 * g2_ref[...] + b2_ref[...] - tg_ref[...]
        dy = diff * (1.0 / D_MODEL)
        dr2 = _layer_norm_bwd(dy * g2_ref[...], xh2, rstd)
        dr2_ref[...] = dr2
        dffn = (dr2 * ada_ref[0, 5:6, :]).astype(BF16)
        dffn_ref[...] = dffn
        dfi_ref[...] = _nt(dffn, w_ref[...]).astype(BF16)
        acc_ref[0:1, :] += jnp.sum(dy * xh2, axis=0, keepdims=True)
        acc_ref[1:2, :] += jnp.sum(dy, axis=0, keepdims=True)
        acc_ref[2:3, :] += jnp.sum(diff * diff, axis=0, keepdims=True) * (0.5 / D_MODEL)
        acc_ref[pl.ds(8 + i // nts, 1), :] += jnp.sum(dr2 * ffn, axis=0, keepdims=True)

    tok = lambda w: pl.BlockSpec((tm, w), lambda i: (i, 0))
    vec = pl.BlockSpec((1, D_MODEL), lambda i: (0, 0))
    return pl.pallas_call(
        body, name="ffn_down", grid=(t_all // tm,),
        in_specs=[tok(D_FF), pl.BlockSpec(w_down.shape, lambda i: (0, 0)), tok(D_MODEL), vec, vec,
                  pl.BlockSpec((1, 6, D_MODEL), lambda i: (i // nts, 0, 0)), vec, vec, tok(D_MODEL)],
        out_specs=[tok(D_MODEL), pl.BlockSpec((16, D_MODEL), lambda i: (0, 0)), tok(D_MODEL), tok(D_FF)],
        out_shape=[jax.ShapeDtypeStruct((t_all, D_MODEL), F32), jax.ShapeDtypeStruct((16, D_MODEL), F32),
                   jax.ShapeDtypeStruct((t_all, D_MODEL), BF16), jax.ShapeDtypeStruct((t_all, D_FF), BF16)],
        compiler_params=_params(),
    )(ffn_in, w_down, xh1, ln1_g, ln1_b, ada3, ln2_g, ln2_b, target)


def _ffn_up_bwd(du_a, du_g, w_up, dr2, xh1, rs1, mix, ada3, ln1_g, ln1_b, seq):
    t_all = dr2.shape[0]
    tm = 512
    nts = seq // tm

    def body(da_ref, dg_ref, w_ref, dr2_ref, xh_ref, rs_ref, mix_ref, ada_ref, g_ref, b_ref, dr1_ref, dmix_ref,
             acc_ref):
        i = pl.program_id(0)

        @pl.when(i == 0)
        def _():
            acc_ref[...] = jnp.zeros_like(acc_ref)
        dh2 = _nt(da_ref[...], w_ref[:, :D_FF]) + _nt(dg_ref[...], w_ref[:, D_FF:])
        xh = xh_ref[...]
        x1 = xh * g_ref[...] + b_ref[...]
        dx1 = ALPHA * dr2_ref[...] + dh2 * (1.0 + ada_ref[0, 4:5, :])
        dr1 = _layer_norm_bwd(dx1 * g_ref[...], xh, rs_ref[:, 0:1])
        dr1_ref[...] = dr1
        dmix_ref[...] = (dr1 * ada_ref[0, 2:3, :]).astype(BF16)
        b = i // nts
        acc_ref[0:1, :] += jnp.sum(dx1 * xh, axis=0, keepdims=True)
        acc_ref[1:2, :] += jnp.sum(dx1, axis=0, keepdims=True)
        acc_ref[pl.ds(8 + b, 1), :] += jnp.sum(dh2 * x1, axis=0, keepdims=True)
        acc_ref[pl.ds(16 + b, 1), :] += jnp.sum(dh2, axis=0, keepdims=True)
        acc_ref[pl.ds(24 + b, 1), :] += jnp.sum(dr1 * mix_ref[...].astype(F32), axis=0, keepdims=True)

    tok = lambda w: pl.BlockSpec((tm, w), lambda i: (i, 0))
    vec = pl.BlockSpec((1, D_MODEL), lambda i: (0, 0))
    return pl.pallas_call(
        body, name="ffn_up_bwd", grid=(t_all // tm,),
        in_specs=[tok(D_FF), tok(D_FF), pl.BlockSpec(w_up.shape, lambda i: (0, 0)), tok(D_MODEL), tok(D_MODEL),
                  tok(LANES), tok(D_MODEL), pl.BlockSpec((1, 6, D_MODEL), lambda i: (i // nts, 0, 0)), vec, vec],
        out_specs=[tok(D_MODEL), tok(D_MODEL), pl.BlockSpec((32, D_MODEL), lambda i: (0, 0))],
        out_shape=[jax.ShapeDtypeStruct((t_all, D_MODEL), F32), jax.ShapeDtypeStruct((t_all, D_MODEL), BF16),
                   jax.ShapeDtypeStruct((32, D_MODEL), F32)],
        compiler_params=_params(),
    )(du_a, du_g, w_up, dr2, xh1, rs1, mix, ada3, ln1_g, ln1_b)


def _rows(a):
    return a[:, :N_HEADS].T


def _rope_freq():
    f = np.float32(ROPE_THETA) ** (-np.arange(0, ROPE_DIMS, 2, dtype=np.float32) / np.float32(ROPE_DIMS))
    return jnp.asarray(np.tile(f.astype(np.float32), LANES // (ROPE_DIMS // 2))[None, :])


def _local_step(x, positions, target, ada3, w_in, b_fgate, gn_a, gn_b, ln1_g, ln1_b, conv_b, ln2_g, ln2_b,
                late_shards):
    nbat, seq, _ = x.shape
    t_all = nbat * seq
    xf = x.reshape(t_all, D_MODEL)
    tg = target.reshape(t_all, D_MODEL)
    pos = positions.reshape(t_all, 1)
    freq = _rope_freq()

    wqkv = jnp.concatenate([w_in[:, :3 * WIDTH], w_in[:, 3 * WIDTH + N_HEADS:]], axis=1)
    wf16 = jnp.zeros((16, D_MODEL), BF16).at[:N_HEADS].set(w_in[:, 3 * WIDTH:3 * WIDTH + N_HEADS].T)
    bf = b_fgate.reshape(N_HEADS, 1)

    perms = [_perm_matrix(TOK_TM, d, tr) for tr in (False, True) for d in DILATIONS[1:]]
    h1, za, zb1, zb4, zb16, vt, fa_t = _inproj(xf, ada3, pos, wqkv, wf16, freq, perms, seq)
    zbs = [zb1, zb4.reshape(t_all, 3 * WIDTH), zb16.reshape(t_all, 3 * WIDTH)]
    f_row = _fgate_fwd(fa_t, bf, seq)
    f_col = jnp.zeros((t_all, LANES), F32).at[:, :N_HEADS].set(f_row.T * LOG2E)
    oa, lse_row_a, gathered = _fox_fwd(za, vt, f_col, seq, [late_shards[n] for n in LATE])
    w_out, w_up, conv_w, w_down = (_full_from_gathered(n, g) for n, g in zip(LATE, gathered))
    o3, l3 = zip(*[_dil_fwd(zb, seq, d) for zb, d in zip(zbs, DILATIONS)])
    ob, lse_b, lse_b4, lse_b16, merged, mix, xh1, rs1, h2, h2t = _mix_out(oa, o3, l3, gn_a, gn_b, w_out, xf, ada3, ln1_g,
                                                                      ln1_b, perms, seq)
    u_a, u_g, ffn_in = _ffn_up_gate(h2, w_up, conv_w, conv_b, seq)
    dr2, acc2, dffn, dfi = _ffn_down(ffn_in, w_down, xh1, ln1_g, ln1_b, ada3, ln2_g, ln2_b, tg, seq)

    d_w_down = _matmul_tn(dffn, ffn_in, 512, 1024, "dw_down").T
    du_a, du_g, acc_ca, acc_cg, dw_up_a, dw_up_g = _ffn_gate_bwd(u_a, u_g, dfi, conv_w, conv_b, h2t, seq)
    dr1, dmix, acc1 = _ffn_up_bwd(du_a, du_g, w_up, dr2, xh1, rs1, mix, ada3, ln1_g, ln1_b, seq)

    doa, dob, dob4, dob16, dl_a, dl_b, dl_b4, dl_b16, acc_gn = _mix_out_bwd(dmix, w_out, oa, ob, gn_a, gn_b, perms, seq)
    d_w_out = _matmul_tn(merged, dmix, 512, 1024, "dw_out")
    late_grads = dict(w_out=d_w_out, w_up=(dw_up_a, dw_up_g), conv_w=jnp.concatenate([acc_ca[0:3], acc_cg[0:3]], axis=1),
                      w_down=d_w_down)
    dka, dva, df_k, dqt, df_q, late_parts = _fox_bwd(za, doa, f_col, lse_row_a, _rows(dl_a), seq,
                                                     [_payload(n, _dest_major(n, late_grads[n])) for n in LATE])
    dfa_t, dbf = _fgate_bwd(_rows(df_k) + df_q, fa_t, bf, seq)
    flat = lambda a: a.reshape(t_all, a.shape[-1])
    dil = []
    for zb, d, do, lse, dl in zip(zbs, DILATIONS, (dob, flat(dob4), flat(dob16)),
                                  (lse_b, flat(lse_b4), flat(lse_b16)), (dl_b, flat(dl_b4), flat(dl_b16))):
        dil.append(_dil_bwd(zb, do, lse, dl, seq, d))
    dfa16 = jnp.zeros((16, t_all), BF16).at[:N_HEADS].set(dfa_t.astype(BF16))
    grad_x, dz, acc0 = _inproj_bwd(dqt, dka, dva, dil[0], dil[1], dil[2], dfa16, pos, wqkv, wf16, freq, perms, dr1, xf,
                                   ada3, seq)
    d_wqkv = _matmul_tn(h1, dz, 512, 1024, "dw_in")
    d_wf = _matmul_rows(dfa16, h1, 512, "dw_fgate")[:N_HEADS].T
    d_w_in = jnp.concatenate([d_wqkv[:, :3 * WIDTH], d_wf, d_wqkv[:, 3 * WIDTH:]], axis=1)

    dada = jnp.concatenate([acc0[8:8 + nbat], acc0[:nbat], acc1[24:24 + nbat], acc1[16:16 + nbat], acc1[8:8 + nbat],
                            acc2[8:8 + nbat]], axis=1)

    grads = dict(
        dada=dada, b_ada=jnp.sum(dada, axis=0, keepdims=True), w_in=d_w_in, b_fgate=dbf[:, 0][None, :],
        gn_a=acc_gn[0:1, :WIDTH], gn_b=acc_gn[0:1, WIDTH:], ln1_g=acc1[0:1], ln1_b=acc1[1:2],
        conv_b=jnp.concatenate([acc_ca[3:4], acc_cg[3:4]], axis=1), ln2_g=acc2[0:1], ln2_b=acc2[1:2])
    return acc2[2:3], grad_x.reshape(x.shape), grads, dict(zip(LATE, late_parts))


LATE = ("w_out", "w_up", "conv_w", "w_down")
BIG = ("w_ada", "w_in") + LATE
COLUMN_SHARDED = ("w_ada", "w_in", "w_up", "conv_w")


def _payload(name, a):
    return a if name == "conv_w" else a.astype(BF16)
SMALL = ("b_ada", "b_fgate", "gn_a", "gn_b", "ln1_g", "ln1_b", "conv_b", "ln2_g", "ln2_b")
ADAM_ROWS = dict(w_ada=256, w_in=256, w_out=128, w_up=256, conv_w=3, w_down=176)
SMALL_ROWS = 24


def _full_from_gathered(name, g):
    if name in COLUMN_SHARDED:
        return g.transpose(1, 0, 2).reshape(g.shape[1], N_DEV * g.shape[2])
    return g.reshape(N_DEV * g.shape[1], g.shape[2])


def _dest_major(name, full):
    if name in COLUMN_SHARDED:
        parts = full if isinstance(full, tuple) else (full,)
        per = N_DEV // len(parts)
        return jnp.concatenate([p.reshape(p.shape[0], per, p.shape[1] // per).transpose(1, 0, 2) for p in parts], axis=0)
    return full.reshape(N_DEV, full.shape[0] // N_DEV, full.shape[1])


def _pack_small(vals, extra=None):
    parts = [vals[n].reshape(-1) for n in SMALL]
    if extra is not None:
        parts.append(extra.reshape(-1))
    flat = jnp.concatenate(parts)
    return jnp.pad(flat, (0, SMALL_ROWS * D_MODEL - flat.shape[0])).reshape(SMALL_ROWS, D_MODEL)


def _unpack_small(packed, like):
    flat = packed.reshape(-1)
    out, off = {}, 0
    for n in SMALL:
        size = like[n].size
        out[n] = flat[off:off + size].reshape(like[n].shape)
        off += size
    return out, flat[off:off + D_MODEL]


def kernel(x, c, positions, w_ada, b_ada, w_in, b_fgate, gn_a, gn_b, w_out, ln1_g, ln1_b, w_up, conv_w, conv_b, w_down, ln2_g, ln2_b, loss_target, m_w_ada, m_b_ada, m_w_in, m_b_fgate, m_gn_a, m_gn_b, m_w_out, m_ln1_g, m_ln1_b, m_w_up, m_conv_w, m_conv_b, m_w_down, m_ln2_g, m_ln2_b, v_w_ada, v_b_ada, v_w_in, v_b_fgate, v_gn_a, v_gn_b, v_w_out, v_ln1_g, v_ln1_b, v_w_up, v_conv_w, v_conv_b, v_w_down, v_ln2_g, v_ln2_b):
    w = dict(w_ada=w_ada[0], b_ada=b_ada, w_in=w_in[0], b_fgate=b_fgate, gn_a=gn_a, gn_b=gn_b, w_out=w_out[0],
             ln1_g=ln1_g, ln1_b=ln1_b, w_up=w_up[0], conv_w=conv_w[0], conv_b=conv_b, w_down=w_down[0], ln2_g=ln2_g,
             ln2_b=ln2_b)
    m = dict(w_ada=m_w_ada[0], b_ada=m_b_ada, w_in=m_w_in[0], b_fgate=m_b_fgate, gn_a=m_gn_a, gn_b=m_gn_b,
             w_out=m_w_out[0], ln1_g=m_ln1_g, ln1_b=m_ln1_b, w_up=m_w_up[0], conv_w=m_conv_w[0], conv_b=m_conv_b,
             w_down=m_w_down[0], ln2_g=m_ln2_g, ln2_b=m_ln2_b)
    v = dict(w_ada=v_w_ada[0], b_ada=v_b_ada, w_in=v_w_in[0], b_fgate=v_b_fgate, gn_a=v_gn_a, gn_b=v_gn_b,
             w_out=v_w_out[0], ln1_g=v_ln1_g, ln1_b=v_ln1_b, w_up=v_w_up[0], conv_w=v_conv_w[0], conv_b=v_conv_b,
             w_down=v_w_down[0], ln2_g=v_ln2_g, ln2_b=v_ln2_b)

    nbat = x.shape[0]
    me = 4 * lax.axis_index("x") + 2 * lax.axis_index("y") + lax.axis_index("c")
    ada_cols = w["w_ada"].shape[1]

    c_all, w_in_all = _gather_two_level([c, _payload("w_in", w["w_in"])], "weight_gather")
    c_all = c_all.reshape(N_DEV * nbat, D_MODEL)
    ada_mine = _ada_fwd(c_all, w["w_ada"], lax.dynamic_slice(b_ada, (0, me * ada_cols), (1, ada_cols)))
    (ada_parts,) = _exchange([ada_mine.reshape(N_DEV, nbat, ada_cols)], [False], "ada_exchange")
    ada3 = ada_parts.transpose(1, 0, 2).reshape(nbat, 6, D_MODEL)

    loss_lanes, grad_x, g_local, parts = _local_step(
        x, positions, loss_target, ada3, _full_from_gathered("w_in", w_in_all), b_fgate, gn_a, gn_b, ln1_g, ln1_b,
        conv_b, ln2_g, ln2_b, {n: _payload(n, w[n]) for n in LATE})

    parts["w_in"], dada_all, small_all = _exchange(
        [_payload("w_in", _dest_major("w_in", g_local["w_in"])), g_local["dada"], _pack_small(g_local, loss_lanes)],
        [False, True, True], "grad_exchange")
    dada_cols = lax.dynamic_slice(dada_all.reshape(N_DEV * nbat, 6 * D_MODEL), (0, me * ada_cols),
                                  (N_DEV * nbat, ada_cols))
    parts["w_ada"] = _ada_bwd(c_all, dada_cols)[None]

    grad, delta, new_m, new_v = {}, {}, {}, {}
    for n in BIG:
        grad[n], delta[n], new_m[n], new_v[n] = (
            a[None] for a in _adamw(parts[n], w[n], m[n], v[n], ADAM_ROWS[n], "adamw_" + n))
    packed = _adamw(small_all, _pack_small(w), _pack_small(m), _pack_small(v), SMALL_ROWS, "adamw_small")
    for dst, pk in zip((grad, delta, new_m, new_v), packed):
        vals, lanes = _unpack_small(pk, w)
        dst.update(vals)
        if dst is grad:
            loss = jnp.sum(lanes)

    order = ("w_ada", "b_ada", "w_in", "b_fgate", "gn_a", "gn_b", "w_out", "ln1_g", "ln1_b", "w_up", "conv_w", "conv_b",
             "w_down", "ln2_g", "ln2_b")
    return (loss, grad_x, *[grad[n] for n in order], *[delta[n] for n in order], *[new_m[n] for n in order],
            *[new_v[n] for n in order])
```
